```python
import jax, jax.numpy as jnp
from jax import lax
import numpy as np

D_MODEL = 1024
BATCH = 16
SEQ = 2048
DEPTH = 1

EPS = 1e-6
A_GROUPS = 8
A_GROUP_DIM = D_MODEL // A_GROUPS
A_WIDTH = A_GROUPS * A_GROUP_DIM
CHUNK = 128
MLA_HEADS = 8
QK_NOPE_DIM = 128
QK_ROPE_DIM = 64
QK_HEAD_DIM = QK_NOPE_DIM + QK_ROPE_DIM
V_HEAD_DIM = D_MODEL // MLA_HEADS
Q_LORA_RANK = 256
KV_LORA_RANK = 128
ROPE_THETA = 10000.0
Q_BLOCK = 128
D_FF = 2816
CONV_WIDTH = 3

IN_DIM = 2 * A_WIDTH + Q_LORA_RANK + KV_LORA_RANK + QK_ROPE_DIM + 2 * D_MODEL
SPLIT_U = A_WIDTH
SPLIT_V = SPLIT_U + A_WIDTH
SPLIT_CQ = SPLIT_V + Q_LORA_RANK
SPLIT_CKV = SPLIT_CQ + KV_LORA_RANK
SPLIT_KR = SPLIT_CKV + QK_ROPE_DIM
SPLIT_GA = SPLIT_KR + D_MODEL

kernel_name = "hybrid_gmlp_mla_convffn"


def rms_norm(x, g):
    xf = x.astype(jnp.float32)
    xf = xf * lax.rsqrt(jnp.mean(jnp.square(xf), axis=-1, keepdims=True) + EPS)
    return xf.astype(x.dtype) * g


def layer_norm(x, g, b):
    xf = x.astype(jnp.float32)
    mu = jnp.mean(xf, axis=-1, keepdims=True)
    var = jnp.mean(jnp.square(xf - mu), axis=-1, keepdims=True)
    return ((xf - mu) * lax.rsqrt(var + EPS)).astype(x.dtype) * g + b


def rope_cos_sin(positions):
    inv_freq = 1.0 / (ROPE_THETA ** (jnp.arange(0, QK_ROPE_DIM, 2, dtype=jnp.float32) / QK_ROPE_DIM))
    ang = positions.astype(jnp.float32)[..., None] * inv_freq
    return jnp.cos(ang), jnp.sin(ang)


def apply_rope(x, cos, sin):
    x1, x2 = jnp.split(x.astype(jnp.float32), 2, axis=-1)
    return jnp.concatenate([x1 * cos - x2 * sin, x1 * sin + x2 * cos], axis=-1).astype(x.dtype)


def chunked_spatial_gating(u, v, v_g, v_b, w_s, b_s):
    B, S, _ = v.shape
    n_chunks = S // CHUNK
    v = layer_norm(v, v_g, v_b)
    vc = v.reshape(B, n_chunks, CHUNK, A_GROUPS, A_GROUP_DIM)
    causal = jnp.tril(jnp.ones((CHUNK, CHUNK), dtype=bool))
    w = jnp.where(causal[None], w_s, 0.0).astype(vc.dtype)
    mixed = jnp.einsum('gts,bnsgc->bntgc', w, vc) + b_s.T[None, None, :, :, None]
    return u * mixed.reshape(B, S, A_WIDTH)


def latent_attention(c_q, c_kv, k_rope, cos, sin, q_norm_g, w_uq, kv_norm_g, w_ukv):
    B, S, _ = c_q.shape
    q = (rms_norm(c_q, q_norm_g) @ w_uq).reshape(B, S, MLA_HEADS, QK_HEAD_DIM)
    q_nope, q_rope = jnp.split(q, [QK_NOPE_DIM], axis=-1)
    q_rope = apply_rope(q_rope, cos[:, :, None, :], sin[:, :, None, :])
    kv = (rms_norm(c_kv, kv_norm_g) @ w_ukv).reshape(B, S, MLA_HEADS, QK_NOPE_DIM + V_HEAD_DIM)
    k_nope, v = jnp.split(kv, [QK_NOPE_DIM], axis=-1)
    k_rope = apply_rope(k_rope, cos, sin)
    scale = QK_HEAD_DIM ** -0.5
    n_blocks = S // Q_BLOCK
    qn_blocks = q_nope.reshape(B, n_blocks, Q_BLOCK, MLA_HEADS, QK_NOPE_DIM).transpose(1, 0, 2, 3, 4)
    qr_blocks = q_rope.reshape(B, n_blocks, Q_BLOCK, MLA_HEADS, QK_ROPE_DIM).transpose(1, 0, 2, 3, 4)
    key_pos = jnp.arange(S)

    def one_block(args):
        qn, qr, i = args
        s = jnp.einsum('bqhd,bkhd->bhqk', qn, k_nope) + jnp.einsum('bqhr,bkr->bhqk', qr, k_rope)
        s = s.astype(jnp.float32) * scale
        q_pos = i * Q_BLOCK + jnp.arange(Q_BLOCK)
        s = jnp.where(key_pos[None, :] <= q_pos[:, None], s, -jnp.inf)
        p = jax.nn.softmax(s, axis=-1).astype(v.dtype)
        return jnp.einsum('bhqk,bkhd->bqhd', p, v)

    out = lax.map(one_block, (qn_blocks, qr_blocks, jnp.arange(n_blocks)))
    return out.transpose(1, 0, 2, 3, 4).reshape(B, S, MLA_HEADS * V_HEAD_DIM)


def causal_depthwise_conv(x, w, b):
    S = x.shape[1]
    xp = jnp.pad(x, ((0, 0), (CONV_WIDTH - 1, 0), (0, 0)))
    return b + sum(w[k] * xp[:, k:k + S] for k in range(CONV_WIDTH))


def conv_gated_ffn(h, w_up, conv_w, conv_b, w_down):
    up = causal_depthwise_conv(h @ w_up, conv_w, conv_b)
    gate, val = jnp.split(up, 2, axis=-1)
    return (jax.nn.silu(gate) * val) @ w_down


def _fwd_setup_inputs(seed: int = 0) -> dict:
    key = jax.random.key(seed)
    ks = jax.random.split(key, 20)
    f32 = jnp.float32

    def nrm(k, shape, fan_in):
        return jax.random.normal(k, shape, f32) * (fan_in ** -0.5)

    def gain(k, shape):
        return 1.0 + 0.02 * jax.random.normal(k, shape, f32)

    x = jax.random.normal(ks[0], (BATCH, SEQ, D_MODEL), f32)
    offset = jax.random.randint(ks[1], (BATCH, 1), 0, 1024, dtype=jnp.int32)
    positions = jnp.arange(SEQ, dtype=jnp.int32)[None, :] + offset
    return {
        "x": x,
        "positions": positions,
        "mix_norm": gain(ks[2], (DEPTH, D_MODEL)),
        "w_in": nrm(ks[3], (DEPTH, D_MODEL, IN_DIM), D_MODEL),
        "a_v_norm_g": gain(ks[4], (DEPTH, A_WIDTH)),
        "a_v_norm_b": 0.02 * jax.random.normal(ks[5], (DEPTH, A_WIDTH), f32),
        "a_spatial_w": nrm(ks[6], (DEPTH, A_GROUPS, CHUNK, CHUNK), CHUNK),
        "a_spatial_b": gain(ks[7], (DEPTH, A_GROUPS, CHUNK)),
        "q_a_norm": gain(ks[8], (DEPTH, Q_LORA_RANK)),
        "w_uq": nrm(ks[9], (DEPTH, Q_LORA_RANK, MLA_HEADS * QK_HEAD_DIM), Q_LORA_RANK),
        "kv_a_norm": gain(ks[10], (DEPTH, KV_LORA_RANK)),
        "w_ukv": nrm(ks[11], (DEPTH, KV_LORA_RANK, MLA_HEADS * (QK_NOPE_DIM + V_HEAD_DIM)), KV_LORA_RANK),
        "w_out": nrm(ks[12], (DEPTH, D_MODEL, D_MODEL), D_MODEL),
        "ffn_norm": gain(ks[13], (DEPTH, D_MODEL)),
        "w_up": nrm(ks[14], (DEPTH, D_MODEL, 2 * D_FF), D_MODEL),
        "conv_w": nrm(ks[15], (DEPTH, CONV_WIDTH, 2 * D_FF), CONV_WIDTH),
        "conv_b": 0.02 * jax.random.normal(ks[16], (DEPTH, 2 * D_FF), f32),
        "w_down": nrm(ks[17], (DEPTH, D_FF, D_MODEL), D_FF),
        "final_norm": gain(ks[18], (D_MODEL,)),
    }


def _fwd_reference(x, positions, mix_norm, w_in, a_v_norm_g, a_v_norm_b, a_spatial_w, a_spatial_b,
              q_a_norm, w_uq, kv_a_norm, w_ukv, w_out, ffn_norm, w_up, conv_w, conv_b, w_down,
              final_norm):
    cos, sin = rope_cos_sin(positions)
    for l in range(DEPTH):
        h = rms_norm(x, mix_norm[l])
        z = h @ w_in[l]
        u_a, v_a, c_q, c_kv, k_rope, g_a, g_b = jnp.split(
            z, [SPLIT_U, SPLIT_V, SPLIT_CQ, SPLIT_CKV, SPLIT_KR, SPLIT_GA], axis=-1)
        y_a = chunked_spatial_gating(jax.nn.gelu(u_a), jax.nn.gelu(v_a), a_v_norm_g[l], a_v_norm_b[l],
                                     a_spatial_w[l], a_spatial_b[l])
        y_b = latent_attention(c_q, c_kv, k_rope, cos, sin, q_a_norm[l], w_uq[l],
                               kv_a_norm[l], w_ukv[l])
        merged = jax.nn.sigmoid(g_a) * y_a + jax.nn.sigmoid(g_b) * y_b
        x = x + merged @ w_out[l]
        x = x + conv_gated_ffn(rms_norm(x, ffn_norm[l]), w_up[l], conv_w[l], conv_b[l], w_down[l])
    return rms_norm(x, final_norm)


import jax as _jax
import jax.numpy as _jnp

TWIN_FORMAT = 'train_step'
FWD_PARAMS = ['x', 'positions', 'mix_norm', 'w_in', 'a_v_norm_g', 'a_v_norm_b', 'a_spatial_w', 'a_spatial_b', 'q_a_norm', 'w_uq', 'kv_a_norm', 'w_ukv', 'w_out', 'ffn_norm', 'w_up', 'conv_w', 'conv_b', 'w_down', 'final_norm']
TWIN_WEIGHTS = ['mix_norm', 'w_in', 'a_v_norm_g', 'a_v_norm_b', 'a_spatial_w', 'a_spatial_b', 'q_a_norm', 'w_uq', 'kv_a_norm', 'w_ukv', 'w_out', 'ffn_norm', 'w_up', 'conv_w', 'conv_b', 'w_down', 'final_norm']
TWIN_DIFF_INPUT = 'x'
TWIN_INPUTS = ['x', 'positions', 'mix_norm', 'w_in', 'a_v_norm_g', 'a_v_norm_b', 'a_spatial_w', 'a_spatial_b', 'q_a_norm', 'w_uq', 'kv_a_norm', 'w_ukv', 'w_out', 'ffn_norm', 'w_up', 'conv_w', 'conv_b', 'w_down', 'final_norm', 'loss_target', 'm_mix_norm', 'm_w_in', 'm_a_v_norm_g', 'm_a_v_norm_b', 'm_a_spatial_w', 'm_a_spatial_b', 'm_q_a_norm', 'm_w_uq', 'm_kv_a_norm', 'm_w_ukv', 'm_w_out', 'm_ffn_norm', 'm_w_up', 'm_conv_w', 'm_conv_b', 'm_w_down', 'm_final_norm', 'v_mix_norm', 'v_w_in', 'v_a_v_norm_g', 'v_a_v_norm_b', 'v_a_spatial_w', 'v_a_spatial_b', 'v_q_a_norm', 'v_w_uq', 'v_kv_a_norm', 'v_w_ukv', 'v_w_out', 'v_ffn_norm', 'v_w_up', 'v_conv_w', 'v_conv_b', 'v_w_down', 'v_final_norm']
TWIN_OUTPUTS = ['loss', 'grad_x', 'grad_mix_norm', 'grad_w_in', 'grad_a_v_norm_g', 'grad_a_v_norm_b', 'grad_a_spatial_w', 'grad_a_spatial_b', 'grad_q_a_norm', 'grad_w_uq', 'grad_kv_a_norm', 'grad_w_ukv', 'grad_w_out', 'grad_ffn_norm', 'grad_w_up', 'grad_conv_w', 'grad_conv_b', 'grad_w_down', 'grad_final_norm', 'delta_mix_norm', 'delta_w_in', 'delta_a_v_norm_g', 'delta_a_v_norm_b', 'delta_a_spatial_w', 'delta_a_spatial_b', 'delta_q_a_norm', 'delta_w_uq', 'delta_kv_a_norm', 'delta_w_ukv', 'delta_w_out', 'delta_ffn_norm', 'delta_w_up', 'delta_conv_w', 'delta_conv_b', 'delta_w_down', 'delta_final_norm', 'new_m_mix_norm', 'new_m_w_in', 'new_m_a_v_norm_g', 'new_m_a_v_norm_b', 'new_m_a_spatial_w', 'new_m_a_spatial_b', 'new_m_q_a_norm', 'new_m_w_uq', 'new_m_kv_a_norm', 'new_m_w_ukv', 'new_m_w_out', 'new_m_ffn_norm', 'new_m_w_up', 'new_m_conv_w', 'new_m_conv_b', 'new_m_w_down', 'new_m_final_norm', 'new_v_mix_norm', 'new_v_w_in', 'new_v_a_v_norm_g', 'new_v_a_v_norm_b', 'new_v_a_spatial_w', 'new_v_a_spatial_b', 'new_v_q_a_norm', 'new_v_w_uq', 'new_v_kv_a_norm', 'new_v_w_ukv', 'new_v_w_out', 'new_v_ffn_norm', 'new_v_w_up', 'new_v_conv_w', 'new_v_conv_b', 'new_v_w_down', 'new_v_final_norm']
TWIN_LEAF_KINDS = {'loss': 'loss', 'grad_x': 'grad_x', 'grad_mix_norm': 'grad_w', 'grad_w_in': 'grad_w', 'grad_a_v_norm_g': 'grad_w', 'grad_a_v_norm_b': 'grad_w', 'grad_a_spatial_w': 'grad_w', 'grad_a_spatial_b': 'grad_w', 'grad_q_a_norm': 'grad_w', 'grad_w_uq': 'grad_w', 'grad_kv_a_norm': 'grad_w', 'grad_w_ukv': 'grad_w', 'grad_w_out': 'grad_w', 'grad_ffn_norm': 'grad_w', 'grad_w_up': 'grad_w', 'grad_conv_w': 'grad_w', 'grad_conv_b': 'grad_w', 'grad_w_down': 'grad_w', 'grad_final_norm': 'grad_w', 'delta_mix_norm': 'delta_w', 'delta_w_in': 'delta_w', 'delta_a_v_norm_g': 'delta_w', 'delta_a_v_norm_b': 'delta_w', 'delta_a_spatial_w': 'delta_w', 'delta_a_spatial_b': 'delta_w', 'delta_q_a_norm': 'delta_w', 'delta_w_uq': 'delta_w', 'delta_kv_a_norm': 'delta_w', 'delta_w_ukv': 'delta_w', 'delta_w_out': 'delta_w', 'delta_ffn_norm': 'delta_w', 'delta_w_up': 'delta_w', 'delta_conv_w': 'delta_w', 'delta_conv_b': 'delta_w', 'delta_w_down': 'delta_w', 'delta_final_norm': 'delta_w', 'new_m_mix_norm': 'new_m', 'new_m_w_in': 'new_m', 'new_m_a_v_norm_g': 'new_m', 'new_m_a_v_norm_b': 'new_m', 'new_m_a_spatial_w': 'new_m', 'new_m_a_spatial_b': 'new_m', 'new_m_q_a_norm': 'new_m', 'new_m_w_uq': 'new_m', 'new_m_kv_a_norm': 'new_m', 'new_m_w_ukv': 'new_m', 'new_m_w_out': 'new_m', 'new_m_ffn_norm': 'new_m', 'new_m_w_up': 'new_m', 'new_m_conv_w': 'new_m', 'new_m_conv_b': 'new_m', 'new_m_w_down': 'new_m', 'new_m_final_norm': 'new_m', 'new_v_mix_norm': 'new_v', 'new_v_w_in': 'new_v', 'new_v_a_v_norm_g': 'new_v', 'new_v_a_v_norm_b': 'new_v', 'new_v_a_spatial_w': 'new_v', 'new_v_a_spatial_b': 'new_v', 'new_v_q_a_norm': 'new_v', 'new_v_w_uq': 'new_v', 'new_v_kv_a_norm': 'new_v', 'new_v_w_ukv': 'new_v', 'new_v_w_out': 'new_v', 'new_v_ffn_norm': 'new_v', 'new_v_w_up': 'new_v', 'new_v_conv_w': 'new_v', 'new_v_conv_b': 'new_v', 'new_v_w_down': 'new_v', 'new_v_final_norm': 'new_v'}


def _forward(args):
    return _fwd_reference(*[args[k] for k in FWD_PARAMS])


def _output_shape():
    out = _jax.eval_shape(lambda: _forward(_fwd_setup_inputs(0)))
    return out.shape, out.dtype

N_MICROBATCH = 1
ADAM_LR = 0.001
ADAM_B1 = 0.9
ADAM_B2 = 0.999
ADAM_EPS = 1e-08
ADAM_WD = 0.01
ADAM_STEP = 10
PER_EXAMPLE_BATCH_AXIS = {'x': 0, 'positions': 0, 'loss_target': 0}
SHARED_INPUTS = []
_WEIGHT_DTYPES = {'mix_norm': _jnp.float32, 'w_in': _jnp.float32, 'a_v_norm_g': _jnp.float32, 'a_v_norm_b': _jnp.float32, 'a_spatial_w': _jnp.float32, 'a_spatial_b': _jnp.float32, 'q_a_norm': _jnp.float32, 'w_uq': _jnp.float32, 'kv_a_norm': _jnp.float32, 'w_ukv': _jnp.float32, 'w_out': _jnp.float32, 'ffn_norm': _jnp.float32, 'w_up': _jnp.float32, 'conv_w': _jnp.float32, 'conv_b': _jnp.float32, 'w_down': _jnp.float32, 'final_norm': _jnp.float32}
MOMENT_SCALE = {'mix_norm': 1.049558e-01, 'w_in': 5.068803e-02, 'a_v_norm_g': 4.452198e-02, 'a_v_norm_b': 4.806323e-02, 'a_spatial_w': 4.598508e-02, 'a_spatial_b': 7.044837e-02, 'q_a_norm': 3.525842e-02, 'w_uq': 1.523238e-02, 'kv_a_norm': 7.513694e-02, 'w_ukv': 1.914653e-02, 'w_out': 8.113464e-02, 'ffn_norm': 1.304285e-01, 'w_up': 5.367246e-02, 'conv_w': 5.304594e-02, 'conv_b': 5.284949e-02, 'w_down': 8.790921e-02, 'final_norm': 3.196775e+01}


def _to_microbatches(a, axis):
    t = _jnp.moveaxis(a, axis, 0)
    t = t.reshape((N_MICROBATCH, t.shape[0] // N_MICROBATCH) + t.shape[1:])
    return _jnp.moveaxis(t, 1, axis + 1)


def setup_inputs(seed: int = 0) -> dict:
    inp = _fwd_setup_inputs(seed)
    key = _jax.random.fold_in(_jax.random.key(seed), 7919)
    shape, _ = _output_shape()
    out = dict(inp)
    out["loss_target"] = _jax.random.normal(_jax.random.fold_in(key, 0), shape, _jnp.float32)
    for i, name in enumerate(TWIN_WEIGHTS):
        w = inp[name].astype(_jnp.float32)
        if MOMENT_SCALE is None:
            s = _jnp.sqrt(_jnp.mean(_jnp.square(w)) + 1e-30)
        else:
            s = MOMENT_SCALE[name]
        km, kv = _jax.random.split(_jax.random.fold_in(key, i + 1))
        out[name] = w
        out["m_" + name] = s * _jax.random.normal(km, w.shape, _jnp.float32)
        out["v_" + name] = (s * s) * _jax.random.uniform(kv, w.shape, _jnp.float32, 0.5, 1.5)
    if N_MICROBATCH > 1:
        for name, axis in PER_EXAMPLE_BATCH_AXIS.items():
            out[name] = _to_microbatches(out[name], axis)
    return {'x': out['x'], 'positions': out['positions'], 'mix_norm': out['mix_norm'], 'w_in': out['w_in'], 'a_v_norm_g': out['a_v_norm_g'], 'a_v_norm_b': out['a_v_norm_b'], 'a_spatial_w': out['a_spatial_w'], 'a_spatial_b': out['a_spatial_b'], 'q_a_norm': out['q_a_norm'], 'w_uq': out['w_uq'], 'kv_a_norm': out['kv_a_norm'], 'w_ukv': out['w_ukv'], 'w_out': out['w_out'], 'ffn_norm': out['ffn_norm'], 'w_up': out['w_up'], 'conv_w': out['conv_w'], 'conv_b': out['conv_b'], 'w_down': out['w_down'], 'final_norm': out['final_norm'], 'loss_target': out['loss_target'], 'm_mix_norm': out['m_mix_norm'], 'm_w_in': out['m_w_in'], 'm_a_v_norm_g': out['m_a_v_norm_g'], 'm_a_v_norm_b': out['m_a_v_norm_b'], 'm_a_spatial_w': out['m_a_spatial_w'], 'm_a_spatial_b': out['m_a_spatial_b'], 'm_q_a_norm': out['m_q_a_norm'], 'm_w_uq': out['m_w_uq'], 'm_kv_a_norm': out['m_kv_a_norm'], 'm_w_ukv': out['m_w_ukv'], 'm_w_out': out['m_w_out'], 'm_ffn_norm': out['m_ffn_norm'], 'm_w_up': out['m_w_up'], 'm_conv_w': out['m_conv_w'], 'm_conv_b': out['m_conv_b'], 'm_w_down': out['m_w_down'], 'm_final_norm': out['m_final_norm'], 'v_mix_norm': out['v_mix_norm'], 'v_w_in': out['v_w_in'], 'v_a_v_norm_g': out['v_a_v_norm_g'], 'v_a_v_norm_b': out['v_a_v_norm_b'], 'v_a_spatial_w': out['v_a_spatial_w'], 'v_a_spatial_b': out['v_a_spatial_b'], 'v_q_a_norm': out['v_q_a_norm'], 'v_w_uq': out['v_w_uq'], 'v_kv_a_norm': out['v_kv_a_norm'], 'v_w_ukv': out['v_w_ukv'], 'v_w_out': out['v_w_out'], 'v_ffn_norm': out['v_ffn_norm'], 'v_w_up': out['v_w_up'], 'v_conv_w': out['v_conv_w'], 'v_conv_b': out['v_conv_b'], 'v_w_down': out['v_w_down'], 'v_final_norm': out['v_final_norm']}


def _loss(weights, diff, rest, loss_target):
    with _jax.named_scope("forward"):
        args = {**rest, TWIN_DIFF_INPUT: diff, **{k: w.astype(_WEIGHT_DTYPES[k]) for k, w in weights.items()}}
        y = _forward(args)
    with _jax.named_scope("loss_head"):
        err = _jnp.square(y.astype(_jnp.float32) - loss_target)
        return 0.5 * _jnp.sum(_jnp.mean(err, axis=-1)) if err.ndim else 0.5 * err


def _adamw(w, g, m, v):
    m = ADAM_B1 * m + (1.0 - ADAM_B1) * g
    v = ADAM_B2 * v + (1.0 - ADAM_B2) * _jnp.square(g)
    m_hat = m / (1.0 - ADAM_B1 ** ADAM_STEP)
    v_hat = v / (1.0 - ADAM_B2 ** ADAM_STEP)
    delta = -ADAM_LR * (m_hat / (_jnp.sqrt(v_hat) + ADAM_EPS) + ADAM_WD * w)
    return delta, m, v


def reference(x, positions, mix_norm, w_in, a_v_norm_g, a_v_norm_b, a_spatial_w, a_spatial_b, q_a_norm, w_uq, kv_a_norm, w_ukv, w_out, ffn_norm, w_up, conv_w, conv_b, w_down, final_norm, loss_target, m_mix_norm, m_w_in, m_a_v_norm_g, m_a_v_norm_b, m_a_spatial_w, m_a_spatial_b, m_q_a_norm, m_w_uq, m_kv_a_norm, m_w_ukv, m_w_out, m_ffn_norm, m_w_up, m_conv_w, m_conv_b, m_w_down, m_final_norm, v_mix_norm, v_w_in, v_a_v_norm_g, v_a_v_norm_b, v_a_spatial_w, v_a_spatial_b, v_q_a_norm, v_w_uq, v_kv_a_norm, v_w_ukv, v_w_out, v_ffn_norm, v_w_up, v_conv_w, v_conv_b, v_w_down, v_final_norm):
    given = dict(x=x, positions=positions, mix_norm=mix_norm, w_in=w_in, a_v_norm_g=a_v_norm_g, a_v_norm_b=a_v_norm_b, a_spatial_w=a_spatial_w, a_spatial_b=a_spatial_b, q_a_norm=q_a_norm, w_uq=w_uq, kv_a_norm=kv_a_norm, w_ukv=w_ukv, w_out=w_out, ffn_norm=ffn_norm, w_up=w_up, conv_w=conv_w, conv_b=conv_b, w_down=w_down, final_norm=final_norm, loss_target=loss_target, m_mix_norm=m_mix_norm, m_w_in=m_w_in, m_a_v_norm_g=m_a_v_norm_g, m_a_v_norm_b=m_a_v_norm_b, m_a_spatial_w=m_a_spatial_w, m_a_spatial_b=m_a_spatial_b, m_q_a_norm=m_q_a_norm, m_w_uq=m_w_uq, m_kv_a_norm=m_kv_a_norm, m_w_ukv=m_w_ukv, m_w_out=m_w_out, m_ffn_norm=m_ffn_norm, m_w_up=m_w_up, m_conv_w=m_conv_w, m_conv_b=m_conv_b, m_w_down=m_w_down, m_final_norm=m_final_norm, v_mix_norm=v_mix_norm, v_w_in=v_w_in, v_a_v_norm_g=v_a_v_norm_g, v_a_v_norm_b=v_a_v_norm_b, v_a_spatial_w=v_a_spatial_w, v_a_spatial_b=v_a_spatial_b, v_q_a_norm=v_q_a_norm, v_w_uq=v_w_uq, v_kv_a_norm=v_kv_a_norm, v_w_ukv=v_w_ukv, v_w_out=v_w_out, v_ffn_norm=v_ffn_norm, v_w_up=v_w_up, v_conv_w=v_conv_w, v_conv_b=v_conv_b, v_w_down=v_w_down, v_final_norm=v_final_norm)
    weights = {n: given[n] for n in TWIN_WEIGHTS}
    shared = {n: given[n] for n in SHARED_INPUTS}
    per_example = {n: given[n] for n in ['x', 'positions']}
    grad_fn = _jax.value_and_grad(_loss, argnums=(0, 1))

    def one_microbatch(ex, loss_target):
        ex = dict(ex)
        diff = ex.pop(TWIN_DIFF_INPUT)
        return grad_fn(weights, diff, {**shared, **ex}, loss_target)

    if N_MICROBATCH == 1:
        loss, (grad_w, grad_x) = one_microbatch(per_example, given["loss_target"])
    else:
        def body(carry, xs):
            loss_sum, grad_sum = carry
            l_k, (gw_k, gx_k) = one_microbatch(xs[0], xs[1])
            with _jax.named_scope("update"):
                return (loss_sum + l_k, _jax.tree.map(_jnp.add, grad_sum, gw_k)), gx_k

        init = (_jnp.zeros((), _jnp.float32), _jax.tree.map(_jnp.zeros_like, weights))
        (loss, grad_w), grad_x = _jax.lax.scan(body, init, (per_example, given["loss_target"]))
    with _jax.named_scope("update"):
        delta_w, new_m, new_v = {}, {}, {}
        for n in TWIN_WEIGHTS:
            delta_w[n], new_m[n], new_v[n] = _adamw(weights[n], grad_w[n], given["m_" + n], given["v_" + n])
    return (loss, grad_x, *[grad_w[n] for n in TWIN_WEIGHTS], *[delta_w[n] for n in TWIN_WEIGHTS],
            *[new_m[n] for n in TWIN_WEIGHTS], *[new_v[n] for n in TWIN_WEIGHTS])
```

```python
import functools
import math

import jax
import jax.numpy as jnp
from jax import lax
from jax.experimental import pallas as pl
from jax.experimental.pallas import tpu as pltpu

F32 = jnp.float32
MXU_DTYPE = jnp.bfloat16
MESH = pl.DeviceIdType.MESH

D_MODEL = 1024
EPS = 1e-6
A_GROUPS = 8
CHUNK = 128
HEADS = 8
NOPE = 128
ROPE = 64
QK_DIM = NOPE + ROPE
HEAD_PAD = 256
Q_RANK = 256
KV_RANK = 128
ROPE_THETA = 10000.0
D_FF = 2816
FF_TILE = 256
N_FF_TILES = D_FF // FF_TILE
LAT = 512
IN_PAD = 4 * D_MODEL + LAT
N_CHIPS = 4
ADAM_LR, ADAM_B1, ADAM_B2, ADAM_EPS, ADAM_WD, ADAM_STEP = 0.001, 0.9, 0.999, 1e-08, 0.01, 10

VMEM_CAP_V7X = 64 * 1024 * 1024
NEG = -1e30


def _params(sem, nbytes):
    limit = int(min(VMEM_CAP_V7X - (8 << 20), max(32 << 20, 3 * nbytes)))
    return pltpu.CompilerParams(dimension_semantics=sem, vmem_limit_bytes=limit)


def _nbytes(shape, dtype):
    return math.prod(shape) * jnp.dtype(dtype).itemsize


_DIMS = {"nn": (((1,), (0,)), ((), ())), "nt": (((1,), (1,)), ((), ())), "tn": (((0,), (0,)), ((), ()))}


def _mm(a, b, mode, name, *, tm, tn, tk, out_dtype=F32, add=None):
    if mode == "nn":
        (M, K), (_, N) = a.shape, b.shape
        a_spec = pl.BlockSpec((tm, tk), lambda i, j, k: (i, k))
        b_spec = pl.BlockSpec((tk, tn), lambda i, j, k: (k, j))
        a_blk, b_blk = (tm, tk), (tk, tn)
    elif mode == "nt":
        (M, K), (N, _) = a.shape, b.shape
        a_spec = pl.BlockSpec((tm, tk), lambda i, j, k: (i, k))
        b_spec = pl.BlockSpec((tn, tk), lambda i, j, k: (j, k))
        a_blk, b_blk = (tm, tk), (tn, tk)
    else:
        (K, M), (_, N) = a.shape, b.shape
        a_spec = pl.BlockSpec((tk, tm), lambda i, j, k: (k, i))
        b_spec = pl.BlockSpec((tk, tn), lambda i, j, k: (k, j))
        a_blk, b_blk = (tk, tm), (tk, tn)
    assert M % tm == 0 and N % tn == 0 and K % tk == 0, (name, M, N, K, tm, tn, tk)
    nk = K // tk
    dims = _DIMS[mode]
    has_add = add is not None

    def body(*refs):
        if has_add:
            a_ref, b_ref, add_ref, o_ref, acc = refs
        else:
            a_ref, b_ref, o_ref, acc = refs
        k = pl.program_id(2)

        @pl.when(k == 0)
        def _():
            acc[...] = jnp.zeros_like(acc)

        acc[...] += lax.dot_general(a_ref[...].astype(MXU_DTYPE), b_ref[...].astype(MXU_DTYPE), dims,
                                    preferred_element_type=F32)

        @pl.when(k == nk - 1)
        def _():
            r = acc[...]
            if has_add:
                r = r + add_ref[...]
            o_ref[...] = r.astype(out_dtype)

    in_specs = [a_spec, b_spec]
    args = [a, b]
    nbytes = _nbytes(a_blk, a.dtype) + _nbytes(b_blk, b.dtype) + 3 * _nbytes((tm, tn), F32)
    if has_add:
        in_specs.append(pl.BlockSpec((tm, tn), lambda i, j, k: (i, j)))
        args.append(add)
        nbytes += _nbytes((tm, tn), F32)
    return pl.pallas_call(
        body, name=name, out_shape=jax.ShapeDtypeStruct((M, N), out_dtype),
        grid=(M // tm, N // tn, nk), in_specs=in_specs,
        out_specs=pl.BlockSpec((tm, tn), lambda i, j, k: (i, j)),
        scratch_shapes=[pltpu.VMEM((tm, tn), F32)],
        compiler_params=_params(("parallel", "parallel", "arbitrary"), nbytes),
    )(*args)


_GELU_C = math.sqrt(2.0 / math.pi)
_GELU_A = 0.044715


def _sigmoid(x):
    return 1.0 / (1.0 + jnp.exp(-x))


def _gelu(x):
    t = jnp.tanh(_GELU_C * (x + _GELU_A * (x * x * x)))
    return x * (0.5 * (1.0 + t))


def _gelu_and_grad(x):
    x2 = x * x
    t = jnp.tanh(_GELU_C * (x + _GELU_A * (x2 * x)))
    cdf = 0.5 * (1.0 + t)
    grad = cdf + 0.5 * x * (1.0 - t * t) * (_GELU_C * (1.0 + 3.0 * _GELU_A * x2))
    return x * cdf, grad


def _rope_mix(g, cos_a, sin_a):
    return g * cos_a + pltpu.roll(g, 64, 1) * sin_a


def _rope_mix_bwd(d, cos_a, sin_a):
    return d * cos_a + pltpu.roll(d * sin_a, 64, 1)


def _rms_fwd(x, g, name, tr=512):
    T, D = x.shape

    def body(x_ref, g_ref, h_ref):
        xv = x_ref[...]
        r = lax.rsqrt(jnp.mean(xv * xv, axis=-1, keepdims=True) + EPS)
        h_ref[...] = ((xv * r) * g_ref[...]).astype(h_ref.dtype)

    return pl.pallas_call(
        body, name=name, out_shape=jax.ShapeDtypeStruct((T, D), MXU_DTYPE), grid=(T // tr,),
        in_specs=[pl.BlockSpec((tr, D), lambda i: (i, 0)), pl.BlockSpec((1, D), lambda i: (0, 0))],
        out_specs=pl.BlockSpec((tr, D), lambda i: (i, 0)),
        compiler_params=_params(("parallel",), 3 * _nbytes((tr, D), F32)),
    )(x, g)


def _rms_bwd(x, g, dh, dres, name, tr=512):
    T, D = x.shape

    def body(x_ref, g_ref, dh_ref, dres_ref, dx_ref, gg_ref):
        @pl.when(pl.program_id(0) == 0)
        def _():
            gg_ref[...] = jnp.zeros_like(gg_ref)

        xv = x_ref[...]
        r = lax.rsqrt(jnp.mean(xv * xv, axis=-1, keepdims=True) + EPS)
        xn = xv * r
        dhv = dh_ref[...]
        dxn = dhv * g_ref[...]
        dx_ref[...] = dres_ref[...] + r * (dxn - xn * jnp.mean(dxn * xn, axis=-1, keepdims=True))
        gg_ref[...] += jnp.sum(dhv * xn, axis=0, keepdims=True)

    row = pl.BlockSpec((tr, D), lambda i: (i, 0))
    vec = pl.BlockSpec((1, D), lambda i: (0, 0))
    return pl.pallas_call(
        body, name=name,
        out_shape=(jax.ShapeDtypeStruct((T, D), F32), jax.ShapeDtypeStruct((1, D), F32)),
        grid=(T // tr,), in_specs=[row, vec, row, row], out_specs=(row, vec),
        compiler_params=_params(("arbitrary",), 6 * _nbytes((tr, D), F32)),
    )(x, g, dh, dres)


def _lat_fwd(z, gq, gkv, wq, wkv, cos_a, sin_a, tr=256):
    T = z.shape[0]
    lat_blk = (4 * D_MODEL) // LAT

    def body(z_ref, gq_ref, gkv_ref, wq_ref, wkv_ref, cos_ref, sin_ref, q_ref, k_ref, v_ref, cqn_ref, ckvn_ref):
        zl = z_ref[...]
        cos_v, sin_v = cos_ref[...], sin_ref[...]
        cq = zl[:, :Q_RANK]
        ckv = zl[:, Q_RANK:Q_RANK + KV_RANK]
        krb = zl[:, Q_RANK + KV_RANK:]
        cqn = ((cq * lax.rsqrt(jnp.mean(cq * cq, axis=-1, keepdims=True) + EPS)) * gq_ref[...]).astype(MXU_DTYPE)
        ckvn = ((ckv * lax.rsqrt(jnp.mean(ckv * ckv, axis=-1, keepdims=True) + EPS)) * gkv_ref[...]).astype(MXU_DTYPE)
        cqn_ref[...] = cqn
        ckvn_ref[...] = ckvn
        krr = _rope_mix(krb, cos_v, sin_v).astype(MXU_DTYPE)
        q = jnp.dot(cqn, wq_ref[...], preferred_element_type=F32)
        kv = jnp.dot(ckvn, wkv_ref[...], preferred_element_type=F32)
        for h in range(HEADS):
            o = h * HEAD_PAD
            q_ref[:, o:o + NOPE] = q[:, o:o + NOPE].astype(MXU_DTYPE)
            q_ref[:, o + NOPE:o + HEAD_PAD] = _rope_mix(q[:, o + NOPE:o + HEAD_PAD], cos_v, sin_v).astype(MXU_DTYPE)
            k_ref[:, o:o + NOPE] = kv[:, h * NOPE:(h + 1) * NOPE].astype(MXU_DTYPE)
            k_ref[:, o + NOPE:o + HEAD_PAD] = krr
        v_ref[...] = kv[:, HEADS * NOPE:].astype(MXU_DTYPE)

    def row(w):
        return pl.BlockSpec((tr, w), lambda i: (i, 0))

    def full(a):
        return pl.BlockSpec(a.shape, lambda i: (0, 0))

    return pl.pallas_call(
        body, name="lat_fwd",
        out_shape=(jax.ShapeDtypeStruct((T, HEADS * HEAD_PAD), MXU_DTYPE), jax.ShapeDtypeStruct((T, HEADS * HEAD_PAD), MXU_DTYPE),
                   jax.ShapeDtypeStruct((T, HEADS * NOPE), MXU_DTYPE), jax.ShapeDtypeStruct((T, Q_RANK), MXU_DTYPE),
                   jax.ShapeDtypeStruct((T, KV_RANK), MXU_DTYPE)),
        grid=(T // tr,),
        in_specs=[pl.BlockSpec((tr, LAT), lambda i: (i, lat_blk)), full(gq), full(gkv), full(wq), full(wkv), row(128), row(128)],
        out_specs=(row(HEADS * HEAD_PAD), row(HEADS * HEAD_PAD), row(HEADS * NOPE), row(Q_RANK), row(KV_RANK)),
        compiler_params=_params(("parallel",), 8 * _nbytes((tr, HEADS * HEAD_PAD), F32)),
    )(z, gq, gkv, wq, wkv, cos_a, sin_a)


ATT_BLOCK = 256
_SCALE = QK_DIM ** -0.5


def _causal_mask(n):
    return lax.broadcasted_iota(jnp.int32, (n, n), 1) <= lax.broadcasted_iota(jnp.int32, (n, n), 0)


def _attn_fwd(q, k, v, B, S):
    tq = ATT_BLOCK
    nq = S // tq
    T = B * S

    def body(q_ref, k_ref, v_ref, o_ref, lse_ref):
        qi = pl.program_id(2)
        qv = q_ref[...]

        def step(j, carry, masked):
            m, l, acc = carry
            rows = pl.ds(pl.multiple_of(j * tq, tq), tq)
            s = lax.dot_general(qv, k_ref[rows, :], _DIMS["nt"], preferred_element_type=F32) * _SCALE
            if masked:
                s = jnp.where(_causal_mask(tq), s, NEG)
            m_new = jnp.maximum(m, jnp.max(s, axis=-1, keepdims=True))
            alpha = jnp.exp(m - m_new)
            p = jnp.exp(s - m_new)
            l = alpha * l + jnp.sum(p, axis=-1, keepdims=True)
            acc = alpha * acc + jnp.dot(p.astype(MXU_DTYPE), v_ref[rows, :], preferred_element_type=F32)
            return m_new, l, acc

        init = (jnp.full((tq, 1), NEG, F32), jnp.zeros((tq, 1), F32), jnp.zeros((tq, NOPE), F32))
        carry = lax.fori_loop(0, qi, lambda j, c: step(j, c, False), init)
        m, l, acc = step(qi, carry, True)
        o_ref[...] = acc / l
        lse_ref[0] = m + jnp.log(l)

    return pl.pallas_call(
        body, name="attn_fwd",
        out_shape=(jax.ShapeDtypeStruct((T, HEADS * NOPE), F32), jax.ShapeDtypeStruct((HEADS, T, 1), F32)),
        grid=(B, HEADS, nq),
        in_specs=[pl.BlockSpec((tq, HEAD_PAD), lambda b, h, i: (b * nq + i, h)),
                  pl.BlockSpec((S, HEAD_PAD), lambda b, h, i: (b, h)),
                  pl.BlockSpec((S, NOPE), lambda b, h, i: (b, h))],
        out_specs=(pl.BlockSpec((tq, NOPE), lambda b, h, i: (b * nq + i, h)),
                   pl.BlockSpec((1, tq, 1), lambda b, h, i: (h, b * nq + i, 0))),
        compiler_params=_params(("parallel", "parallel", "arbitrary"), 4 * _nbytes((S, HEAD_PAD), MXU_DTYPE)),
    )(q, k, v)


def _attn_bwd(q, k, v, do, lse, delta, B, S):
    tq = ATT_BLOCK
    nq = S // tq
    T = B * S

    def body(q_ref, k_ref, v_ref, do_ref, lse_ref, dl_ref, dq_ref, dk_ref, dv_ref):
        kj = pl.program_id(2)

        @pl.when(kj == 0)
        def _():
            dq_ref[...] = jnp.zeros_like(dq_ref)

        kv_, vv = k_ref[...], v_ref[...]

        def step(i, carry, masked):
            dk, dv = carry
            rows = pl.ds(pl.multiple_of(i * tq, tq), tq)
            qv, dov = q_ref[rows, :], do_ref[rows, :]
            s = lax.dot_general(qv, kv_, _DIMS["nt"], preferred_element_type=F32) * _SCALE
            p = jnp.exp(s - lse_ref[0, rows, :])
            if masked:
                p = jnp.where(_causal_mask(tq), p, 0.0)
            dv = dv + lax.dot_general(p.astype(MXU_DTYPE), dov, _DIMS["tn"], preferred_element_type=F32)
            dp = lax.dot_general(dov, vv, _DIMS["nt"], preferred_element_type=F32)
            ds = (p * (dp - dl_ref[0, rows, :]) * _SCALE).astype(MXU_DTYPE)
            dk = dk + lax.dot_general(ds, qv, _DIMS["tn"], preferred_element_type=F32)
            dq_ref[rows, :] += jnp.dot(ds, kv_, preferred_element_type=F32)
            return dk, dv

        carry = step(kj, (jnp.zeros((tq, HEAD_PAD), F32), jnp.zeros((tq, NOPE), F32)), True)
        dk, dv = lax.fori_loop(kj + 1, nq, lambda i, c: step(i, c, False), carry)
        dk_ref[...] = dk
        dv_ref[...] = dv.astype(dv_ref.dtype)

    seq = lambda w: pl.BlockSpec((S, w), lambda b, h, j: (b, h))
    col = pl.BlockSpec((1, S, 1), lambda b, h, j: (h, b, 0))
    return pl.pallas_call(
        body, name="attn_bwd",
        out_shape=(jax.ShapeDtypeStruct((T, HEADS * HEAD_PAD), F32), jax.ShapeDtypeStruct((T, HEADS * HEAD_PAD), F32),
                   jax.ShapeDtypeStruct((T, HEADS * NOPE), MXU_DTYPE)),
        grid=(B, HEADS, nq),
        in_specs=[seq(HEAD_PAD), pl.BlockSpec((tq, HEAD_PAD), lambda b, h, j: (b * nq + j, h)),
                  pl.BlockSpec((tq, NOPE), lambda b, h, j: (b * nq + j, h)), seq(NOPE), col, col],
        out_specs=(seq(HEAD_PAD), pl.BlockSpec((tq, HEAD_PAD), lambda b, h, j: (b * nq + j, h)),
                   pl.BlockSpec((tq, NOPE), lambda b, h, j: (b * nq + j, h))),
        compiler_params=_params(("parallel", "parallel", "arbitrary"), 8 * _nbytes((S, HEAD_PAD), F32)),
    )(q, k, v, do, lse, delta)


MIX_ROWS = 256


def _tril_weights(ws_ref, g):
    return jnp.where(_causal_mask(CHUNK), ws_ref[g], 0.0).astype(MXU_DTYPE)


def _layer_norm_stats(va):
    mu = jnp.mean(va, axis=-1, keepdims=True)
    xc = va - mu
    rs = lax.rsqrt(jnp.mean(xc * xc, axis=-1, keepdims=True) + EPS)
    return xc * rs


def _mix_specs(tr):
    zcol = lambda c: pl.BlockSpec((tr, D_MODEL), lambda i, c=c: (i, c))
    row = pl.BlockSpec((tr, D_MODEL), lambda i: (i, 0))
    vec = pl.BlockSpec((1, D_MODEL), lambda i: (0, 0))
    ws = pl.BlockSpec((A_GROUPS, CHUNK, CHUNK), lambda i: (0, 0, 0))
    bs = pl.BlockSpec((CHUNK, 128), lambda i: (0, 0))
    return zcol, row, vec, ws, bs


def _mix_fwd(z, yb, ln_g, ln_b, ws, bs_t):
    T = z.shape[0]
    tr = MIX_ROWS
    zcol, row, vec, ws_spec, bs_spec = _mix_specs(tr)

    def body(zu_ref, zv_ref, zga_ref, zgb_ref, yb_ref, g_ref, b_ref, ws_ref, bs_ref, out_ref, vn_s):
        vhat = _layer_norm_stats(_gelu(zv_ref[...]))
        vn_s[...] = (vhat * g_ref[...] + b_ref[...]).astype(MXU_DTYPE)
        for g in range(A_GROUPS):
            w = _tril_weights(ws_ref, g)
            bias = bs_ref[:, g:g + 1]
            cols = slice(g * CHUNK, (g + 1) * CHUNK)
            for c in range(tr // CHUNK):
                rows = slice(c * CHUNK, (c + 1) * CHUNK)
                mixed = jnp.dot(w, vn_s[rows, cols], preferred_element_type=F32) + bias
                ya = _gelu(zu_ref[rows, cols]) * mixed
                merged = _sigmoid(zga_ref[rows, cols]) * ya + _sigmoid(zgb_ref[rows, cols]) * yb_ref[rows, cols]
                out_ref[rows, cols] = merged.astype(MXU_DTYPE)

    return pl.pallas_call(
        body, name="mix_fwd", out_shape=jax.ShapeDtypeStruct((T, D_MODEL), MXU_DTYPE), grid=(T // tr,),
        in_specs=[zcol(0), zcol(1), zcol(2), zcol(3), row, vec, vec, ws_spec, bs_spec], out_specs=row,
        scratch_shapes=[pltpu.VMEM((tr, D_MODEL), MXU_DTYPE)],
        compiler_params=_params(("parallel",), 8 * _nbytes((tr, D_MODEL), F32)),
    )(z, z, z, z, yb, ln_g, ln_b, ws, bs_t)


def _mix_bwd(z, yb, dm, ln_g, ln_b, ws, bs_t):
    T = z.shape[0]
    tr = MIX_ROWS
    zcol, row, vec, ws_spec, bs_spec = _mix_specs(tr)

    def body(zu_ref, zv_ref, zga_ref, zgb_ref, yb_ref, dm_ref, g_ref, b_ref, ws_ref, bs_ref,
             dz_ref, dyb_ref, dl_ref, gws_ref, gbs_ref, glg_ref, glb_ref, vn_s, dvn_s):
        @pl.when(pl.program_id(0) == 0)
        def _():
            gws_ref[...] = jnp.zeros_like(gws_ref)
            gbs_ref[...] = jnp.zeros_like(gbs_ref)
            glg_ref[...] = jnp.zeros_like(glg_ref)
            glb_ref[...] = jnp.zeros_like(glb_ref)

        lane = lax.broadcasted_iota(jnp.int32, (CHUNK, 128), 1)
        va, dgelu_v = _gelu_and_grad(zv_ref[...])
        mu = jnp.mean(va, axis=-1, keepdims=True)
        xc = va - mu
        rs = lax.rsqrt(jnp.mean(xc * xc, axis=-1, keepdims=True) + EPS)
        vhat = xc * rs
        vn_s[...] = (vhat * g_ref[...] + b_ref[...]).astype(MXU_DTYPE)
        gbs_acc = jnp.zeros((CHUNK, 128), F32)
        for g in range(A_GROUPS):
            w = _tril_weights(ws_ref, g)
            bias = bs_ref[:, g:g + 1]
            cols = slice(g * CHUNK, (g + 1) * CHUNK)
            gw_acc = jnp.zeros((CHUNK, CHUNK), F32)
            for c in range(tr // CHUNK):
                rows = slice(c * CHUNK, (c + 1) * CHUNK)
                vn = vn_s[rows, cols]
                mixed = jnp.dot(w, vn, preferred_element_type=F32) + bias
                ua, dgelu_u = _gelu_and_grad(zu_ref[rows, cols])
                dmv = dm_ref[rows, cols]
                sa = _sigmoid(zga_ref[rows, cols])
                dya = dmv * sa
                dz_ref[rows, 2 * D_MODEL + g * CHUNK:2 * D_MODEL + (g + 1) * CHUNK] = (
                    dmv * (ua * mixed) * (sa * (1.0 - sa))).astype(dz_ref.dtype)
                dz_ref[rows, cols] = (dya * mixed * dgelu_u).astype(dz_ref.dtype)
                dmix = dya * ua
                gbs_acc = gbs_acc + jnp.where(lane == g, jnp.sum(dmix, axis=-1, keepdims=True), 0.0)
                dmix_b = dmix.astype(MXU_DTYPE)
                gw_acc = gw_acc + lax.dot_general(dmix_b, vn, _DIMS["nt"], preferred_element_type=F32)
                dvn_s[rows, cols] = lax.dot_general(w, dmix_b, _DIMS["tn"], preferred_element_type=F32)
            gws_ref[g] += jnp.where(_causal_mask(CHUNK), gw_acc, 0.0)
        gbs_ref[...] += gbs_acc

        dvn = dvn_s[...]
        glg_ref[...] += jnp.sum(dvn * vhat, axis=0, keepdims=True)
        glb_ref[...] += jnp.sum(dvn, axis=0, keepdims=True)
        dvh = dvn * g_ref[...]
        dva = rs * (dvh - jnp.mean(dvh, axis=-1, keepdims=True) - vhat * jnp.mean(dvh * vhat, axis=-1, keepdims=True))
        dz_ref[:, D_MODEL:2 * D_MODEL] = (dva * dgelu_v).astype(dz_ref.dtype)

        dmv = dm_ref[...]
        ybv = yb_ref[...]
        sb = _sigmoid(zgb_ref[...])
        dyb = dmv * sb
        dyb_ref[...] = dyb.astype(dyb_ref.dtype)
        dz_ref[:, 3 * D_MODEL:4 * D_MODEL] = (dmv * ybv * (sb * (1.0 - sb))).astype(dz_ref.dtype)
        dz_ref[:, 4 * D_MODEL:] = jnp.zeros((tr, LAT), dz_ref.dtype)
        prod = dyb * ybv
        lane_r = lax.broadcasted_iota(jnp.int32, (tr, 128), 1)
        dl = jnp.zeros((tr, 128), F32)
        for h in range(HEADS):
            dl = dl + jnp.where(lane_r == h, jnp.sum(prod[:, h * NOPE:(h + 1) * NOPE], axis=-1, keepdims=True), 0.0)
        dl_ref[...] = dl

    return pl.pallas_call(
        body, name="mix_bwd",
        out_shape=(jax.ShapeDtypeStruct((T, IN_PAD), MXU_DTYPE), jax.ShapeDtypeStruct((T, D_MODEL), MXU_DTYPE),
                   jax.ShapeDtypeStruct((T, 128), F32), jax.ShapeDtypeStruct((A_GROUPS, CHUNK, CHUNK), F32),
                   jax.ShapeDtypeStruct((CHUNK, 128), F32), jax.ShapeDtypeStruct((1, D_MODEL), F32),
                   jax.ShapeDtypeStruct((1, D_MODEL), F32)),
        grid=(T // tr,),
        in_specs=[zcol(0), zcol(1), zcol(2), zcol(3), row, row, vec, vec, ws_spec, bs_spec],
        out_specs=(pl.BlockSpec((tr, IN_PAD), lambda i: (i, 0)), row, pl.BlockSpec((tr, 128), lambda i: (i, 0)),
                   ws_spec, bs_spec, vec, vec),
        scratch_shapes=[pltpu.VMEM((tr, D_MODEL), MXU_DTYPE), pltpu.VMEM((tr, D_MODEL), F32)],
        compiler_params=_params(("arbitrary",), 12 * _nbytes((tr, D_MODEL), F32)),
    )(z, z, z, z, yb, dm, ln_g, ln_b, ws, bs_t)


def _lat_bwd(dz, z, dq, dk, dv, gq, gkv, wq, wkv, cos_a, sin_a, tr=256):
    T = z.shape[0]
    lat_blk = (4 * D_MODEL) // LAT

    def body(dz_in, z_ref, dq_ref, dk_ref, dv_ref, gq_ref, gkv_ref, wq_ref, wkv_ref, cos_ref, sin_ref,
             dz_ref, dqr_ref, dkv_ref, ggq_ref, ggkv_ref):
        del dz_in

        @pl.when(pl.program_id(0) == 0)
        def _():
            ggq_ref[...] = jnp.zeros_like(ggq_ref)
            ggkv_ref[...] = jnp.zeros_like(ggkv_ref)

        cos_v, sin_v = cos_ref[...], sin_ref[...]
        dkr = jnp.zeros((tr, 128), F32)
        for h in range(HEADS):
            o = h * HEAD_PAD
            dqr_ref[:, o:o + NOPE] = dq_ref[:, o:o + NOPE].astype(MXU_DTYPE)
            dqr_ref[:, o + NOPE:o + HEAD_PAD] = _rope_mix_bwd(dq_ref[:, o + NOPE:o + HEAD_PAD], cos_v, sin_v).astype(MXU_DTYPE)
            dkv_ref[:, h * NOPE:(h + 1) * NOPE] = dk_ref[:, o:o + NOPE].astype(MXU_DTYPE)
            dkr = dkr + _rope_mix_bwd(dk_ref[:, o + NOPE:o + HEAD_PAD], cos_v, sin_v)
        dkv_ref[:, HEADS * NOPE:] = dv_ref[...]
        dcqn = lax.dot_general(dqr_ref[...], wq_ref[...], _DIMS["nt"], preferred_element_type=F32)
        dckvn = lax.dot_general(dkv_ref[...], wkv_ref[...], _DIMS["nt"], preferred_element_type=F32)

        zl = z_ref[...]

        def rms_bwd(c, dn, g_ref, gg_ref):
            r = lax.rsqrt(jnp.mean(c * c, axis=-1, keepdims=True) + EPS)
            ch = c * r
            gg_ref[...] += jnp.sum(dn * ch, axis=0, keepdims=True)
            dch = dn * g_ref[...]
            return r * (dch - ch * jnp.mean(dch * ch, axis=-1, keepdims=True))

        dz_ref[:, :Q_RANK] = rms_bwd(zl[:, :Q_RANK], dcqn, gq_ref, ggq_ref).astype(dz_ref.dtype)
        dz_ref[:, Q_RANK:Q_RANK + KV_RANK] = rms_bwd(zl[:, Q_RANK:Q_RANK + KV_RANK], dckvn, gkv_ref, ggkv_ref).astype(dz_ref.dtype)
        dz_ref[:, Q_RANK + KV_RANK:] = dkr.astype(dz_ref.dtype)

    def row(w):
        return pl.BlockSpec((tr, w), lambda i: (i, 0))

    def full(a):
        return pl.BlockSpec(a.shape, lambda i: (0, 0))

    lat = pl.BlockSpec((tr, LAT), lambda i: (i, lat_blk))
    return pl.pallas_call(
        body, name="lat_bwd",
        out_shape=(jax.ShapeDtypeStruct(dz.shape, dz.dtype), jax.ShapeDtypeStruct((T, HEADS * HEAD_PAD), MXU_DTYPE),
                   jax.ShapeDtypeStruct((T, 2 * HEADS * NOPE), MXU_DTYPE), jax.ShapeDtypeStruct(gq.shape, F32),
                   jax.ShapeDtypeStruct(gkv.shape, F32)),
        grid=(T // tr,),
        in_specs=[pl.BlockSpec(memory_space=pl.ANY), lat, row(HEADS * HEAD_PAD), row(HEADS * HEAD_PAD), row(HEADS * NOPE),
                  full(gq), full(gkv), full(wq), full(wkv), row(128), row(128)],
        out_specs=(lat, row(HEADS * HEAD_PAD), row(2 * HEADS * NOPE), full(gq), full(gkv)),
        input_output_aliases={0: 0},
        compiler_params=_params(("arbitrary",), 8 * _nbytes((tr, HEADS * HEAD_PAD), F32)),
    )(dz, z, dq, dk, dv, gq, gkv, wq, wkv, cos_a, sin_a)


def _shift_down(x, k, row_idx):
    return jnp.where(row_idx >= k, pltpu.roll(x, k, 0), 0.0)


def _shift_up(x, k, row_idx, S):
    return jnp.where(row_idx < S - k, pltpu.roll(x, S - k, 0), 0.0)


def _conv(x, cw_ref, cb_ref, row_idx):
    return (cb_ref[...] + cw_ref[0:1, :] * _shift_down(x, 2, row_idx) + cw_ref[1:2, :] * _shift_down(x, 1, row_idx)
            + cw_ref[2:3, :] * x)


def _gate_fwd(up_pre, conv_w, conv_b, B, S):
    T = B * S
    W = 2 * FF_TILE

    def body(up_ref, cw_ref, cb_ref, act_ref):
        row_idx = lax.broadcasted_iota(jnp.int32, (S, W), 0)
        up = _conv(up_ref[...], cw_ref, cb_ref, row_idx)
        gate, val = up[:, :FF_TILE], up[:, FF_TILE:]
        act_ref[...] = (gate * _sigmoid(gate) * val).astype(act_ref.dtype)

    return pl.pallas_call(
        body, name="gate_fwd", out_shape=jax.ShapeDtypeStruct((T, D_FF), MXU_DTYPE), grid=(B, N_FF_TILES),
        in_specs=[pl.BlockSpec((S, W), lambda b, j: (b, j)), pl.BlockSpec((3, W), lambda b, j: (0, j)),
                  pl.BlockSpec((1, W), lambda b, j: (0, j))],
        out_specs=pl.BlockSpec((S, FF_TILE), lambda b, j: (b, j)),
        compiler_params=_params(("parallel", "parallel"), 8 * _nbytes((S, W), F32)),
    )(up_pre, conv_w, conv_b)


def _gate_bwd(up_pre, dact, conv_w, conv_b, B, S):
    T = B * S
    W = 2 * FF_TILE

    def body(up_ref, da_ref, cw_ref, cb_ref, dup_ref, gcw_ref, gcb_ref):
        @pl.when(pl.program_id(1) == 0)
        def _():
            gcw_ref[...] = jnp.zeros_like(gcw_ref)
            gcb_ref[...] = jnp.zeros_like(gcb_ref)

        row_idx = lax.broadcasted_iota(jnp.int32, (S, W), 0)
        x = up_ref[...]
        up = _conv(x, cw_ref, cb_ref, row_idx)
        gate, val = up[:, :FF_TILE], up[:, FF_TILE:]
        sg = _sigmoid(gate)
        da = da_ref[...]
        dgate = da * val * (sg * (1.0 + gate * (1.0 - sg)))
        dval = da * (gate * sg)
        dup = jnp.concatenate([dgate, dval], axis=1)
        gcb_ref[...] += jnp.sum(dup, axis=0, keepdims=True)
        gcw_ref[0:1, :] += jnp.sum(dup * _shift_down(x, 2, row_idx), axis=0, keepdims=True)
        gcw_ref[1:2, :] += jnp.sum(dup * _shift_down(x, 1, row_idx), axis=0, keepdims=True)
        gcw_ref[2:3, :] += jnp.sum(dup * x, axis=0, keepdims=True)
        dx = (cw_ref[2:3, :] * dup + cw_ref[1:2, :] * _shift_up(dup, 1, row_idx, S)
              + cw_ref[0:1, :] * _shift_up(dup, 2, row_idx, S))
        dup_ref[...] = dx.astype(dup_ref.dtype)

    return pl.pallas_call(
        body, name="gate_bwd",
        out_shape=(jax.ShapeDtypeStruct((T, 2 * D_FF), MXU_DTYPE), jax.ShapeDtypeStruct((3, 2 * D_FF), F32),
                   jax.ShapeDtypeStruct((1, 2 * D_FF), F32)),
        grid=(N_FF_TILES, B),
        in_specs=[pl.BlockSpec((S, W), lambda j, b: (b, j)), pl.BlockSpec((S, FF_TILE), lambda j, b: (b, j)),
                  pl.BlockSpec((3, W), lambda j, b: (0, j)), pl.BlockSpec((1, W), lambda j, b: (0, j))],
        out_specs=(pl.BlockSpec((S, W), lambda j, b: (b, j)), pl.BlockSpec((3, W), lambda j, b: (0, j)),
                   pl.BlockSpec((1, W), lambda j, b: (0, j))),
        compiler_params=_params(("parallel", "arbitrary"), 12 * _nbytes((S, W), F32)),
    )(up_pre, dact, conv_w, conv_b)


def _final(x2, tgt, g, tr=512):
    T, D = x2.shape

    def body(x_ref, t_ref, g_ref, dx_ref, loss_ref, gg_ref):
        @pl.when(pl.program_id(0) == 0)
        def _():
            loss_ref[...] = jnp.zeros_like(loss_ref)
            gg_ref[...] = jnp.zeros_like(gg_ref)

        xv = x_ref[...]
        gv = g_ref[...]
        r = lax.rsqrt(jnp.mean(xv * xv, axis=-1, keepdims=True) + EPS)
        xn = xv * r
        err = xn * gv - t_ref[...]
        loss_ref[...] += 0.5 * jnp.sum(jnp.mean(err * err, axis=-1, keepdims=True), axis=0, keepdims=True)
        dy = err * (1.0 / D)
        gg_ref[...] += jnp.sum(dy * xn, axis=0, keepdims=True)
        dxn = dy * gv
        dx_ref[...] = r * (dxn - xn * jnp.mean(dxn * xn, axis=-1, keepdims=True))

    row = pl.BlockSpec((tr, D), lambda i: (i, 0))
    vec = pl.BlockSpec((1, D), lambda i: (0, 0))
    return pl.pallas_call(
        body, name="final_loss",
        out_shape=(jax.ShapeDtypeStruct((T, D), F32), jax.ShapeDtypeStruct((1, 128), F32), jax.ShapeDtypeStruct((1, D), F32)),
        grid=(T // tr,), in_specs=[row, row, vec],
        out_specs=(row, pl.BlockSpec((1, 128), lambda i: (0, 0)), vec),
        compiler_params=_params(("arbitrary",), 6 * _nbytes((tr, D), F32)),
    )(x2, tgt, g)


def _sum_slabs(parts, name, tr):
    rows, cols = parts[0].shape
    n = len(parts)

    def body(*refs):
        acc = refs[0][...]
        for r in refs[1:n]:
            acc = acc + r[...]
        refs[n][...] = acc

    blk = pl.BlockSpec((tr, cols), lambda i: (i, 0))
    return pl.pallas_call(
        body, name=name, out_shape=jax.ShapeDtypeStruct((rows, cols), F32), grid=(rows // tr,),
        in_specs=[blk] * n, out_specs=blk,
        compiler_params=_params(("parallel",), (n + 1) * _nbytes((tr, cols), F32)),
    )(*parts)


def _adamw(w, g, m, v, name):
    rows, cols = w.shape
    tr = rows
    for cand in (256, 128, 64, 32, 16, 8):
        if rows % cand == 0:
            tr = cand
            break
    c1 = 1.0 - ADAM_B1 ** ADAM_STEP
    c2 = 1.0 - ADAM_B2 ** ADAM_STEP

    def body(w_ref, g_ref, m_ref, v_ref, d_ref, nm_ref, nv_ref):
        gv = g_ref[...]
        nm = ADAM_B1 * m_ref[...] + (1.0 - ADAM_B1) * gv
        nv = ADAM_B2 * v_ref[...] + (1.0 - ADAM_B2) * (gv * gv)
        nm_ref[...] = nm
        nv_ref[...] = nv
        d_ref[...] = -ADAM_LR * ((nm / c1) / (jnp.sqrt(nv / c2) + ADAM_EPS) + ADAM_WD * w_ref[...])

    blk = pl.BlockSpec((tr, cols), lambda i: (i, 0))
    sds = jax.ShapeDtypeStruct((rows, cols), F32)
    return pl.pallas_call(
        body, name=name, out_shape=(sds, sds, sds), grid=(rows // tr,), in_specs=[blk] * 4, out_specs=(blk, blk, blk),
        compiler_params=_params(("parallel",), 7 * _nbytes((tr, cols), F32)),
    )(w, g, m, v)


_ANY = pl.BlockSpec(memory_space=pl.ANY)


def _place():
    x, y, c = lax.axis_index("x"), lax.axis_index("y"), lax.axis_index("c")
    chips = [(1 - x, y), (x, 1 - y), (1 - x, 1 - y)]
    return x, y, c, chips


def _gather_weights(pack, conv_w):
    rows = pack.shape[0]
    half = rows // 2

    def body(p_ref, cw_ref, out_ref, cwo_ref, send, recv, fsend, frecv, csend, crecv, loc):
        x, y, c, chips = _place()
        me = 2 * x + y
        own = pltpu.make_async_copy(p_ref, out_ref.at[me], loc.at[0])
        own_cw = pltpu.make_async_copy(cw_ref, cwo_ref.at[me], loc.at[1])
        own.start()
        own_cw.start()
        mine = pl.ds(pl.multiple_of(c * half, 16), half)
        theirs = pl.ds(pl.multiple_of((1 - c) * half, 16), half)
        first, passed = [], []
        for j, (px, py) in enumerate(chips):
            first.append(pltpu.make_async_remote_copy(
                src_ref=p_ref.at[mine, :], dst_ref=out_ref.at[me, mine, :], send_sem=send.at[j], recv_sem=recv.at[j],
                device_id=(px, py, c), device_id_type=MESH))
            first.append(pltpu.make_async_remote_copy(
                src_ref=cw_ref, dst_ref=cwo_ref.at[me], send_sem=csend.at[j], recv_sem=crecv.at[j],
                device_id=(px, py, c), device_id_type=MESH))
        for cp in first:
            cp.start()
        for j, (px, py) in enumerate(chips):
            src_chip = 2 * px + py
            landed = out_ref.at[src_chip, mine, :]
            pltpu.make_async_remote_copy(src_ref=landed, dst_ref=landed, send_sem=send.at[j], recv_sem=recv.at[j],
                                         device_id=(px, py, c), device_id_type=MESH).wait_recv()
            fw = pltpu.make_async_remote_copy(src_ref=landed, dst_ref=landed, send_sem=fsend.at[j], recv_sem=frecv.at[j],
                                              device_id=(x, y, 1 - c), device_id_type=MESH)
            fw.start()
            passed.append(fw)
        for j, (px, py) in enumerate(chips):
            src_chip = 2 * px + py
            other = out_ref.at[src_chip, theirs, :]
            pltpu.make_async_remote_copy(src_ref=other, dst_ref=other, send_sem=fsend.at[j], recv_sem=frecv.at[j],
                                         device_id=(x, y, 1 - c), device_id_type=MESH).wait_recv()
            pltpu.make_async_remote_copy(src_ref=cw_ref, dst_ref=cwo_ref.at[src_chip], send_sem=csend.at[j],
                                         recv_sem=crecv.at[j], device_id=(px, py, c), device_id_type=MESH).wait_recv()
        for cp in first + passed:
            cp.wait_send()
        own.wait()
        own_cw.wait()

    dma3 = pltpu.SemaphoreType.DMA((3,))
    return pl.pallas_call(
        body, name="gather_weights",
        out_shape=(jax.ShapeDtypeStruct((N_CHIPS,) + pack.shape, pack.dtype),
                   jax.ShapeDtypeStruct((N_CHIPS,) + conv_w.shape, conv_w.dtype)),
        in_specs=[_ANY, _ANY], out_specs=(_ANY, _ANY),
        scratch_shapes=[dma3, dma3, dma3, dma3, dma3, dma3, pltpu.SemaphoreType.DMA((2,))],
    )(pack, conv_w)


def _swap_halves(g):
    n, rows, cols = g.shape
    half = rows // 2

    def body(g_ref, got_ref, send, recv):
        x, y, c, _ = _place()
        theirs = pl.ds(pl.multiple_of((1 - c) * half, 8), half)
        cp = pltpu.make_async_remote_copy(src_ref=g_ref.at[:, theirs, :], dst_ref=got_ref, send_sem=send, recv_sem=recv,
                                          device_id=(x, y, 1 - c), device_id_type=MESH)
        cp.start()
        cp.wait()

    return pl.pallas_call(
        body, name="grad_swap_halves", out_shape=jax.ShapeDtypeStruct((n, half, cols), g.dtype),
        in_specs=[_ANY], out_specs=_ANY,
        scratch_shapes=[pltpu.SemaphoreType.DMA, pltpu.SemaphoreType.DMA],
    )(g)


def _scatter_to_chips(p):
    n, rows, cols = p.shape

    def body(p_ref, out_ref, send, recv, loc):
        x, y, c, chips = _place()
        me = 2 * x + y
        own = pltpu.make_async_copy(p_ref.at[me], out_ref.at[me], loc)
        own.start()
        cps = []
        for j, (px, py) in enumerate(chips):
            cps.append(pltpu.make_async_remote_copy(
                src_ref=p_ref.at[2 * px + py], dst_ref=out_ref.at[me], send_sem=send.at[j], recv_sem=recv.at[j],
                device_id=(px, py, c), device_id_type=MESH))
        for cp in cps:
            cp.start()
        for j, (px, py) in enumerate(chips):
            slot = out_ref.at[2 * px + py]
            pltpu.make_async_remote_copy(src_ref=slot, dst_ref=slot, send_sem=send.at[j], recv_sem=recv.at[j],
                                         device_id=(px, py, c), device_id_type=MESH).wait_recv()
        for cp in cps:
            cp.wait_send()
        own.wait()

    dma3 = pltpu.SemaphoreType.DMA((3,))
    return pl.pallas_call(
        body, name="grad_scatter_chips", out_shape=jax.ShapeDtypeStruct(p.shape, p.dtype),
        in_specs=[_ANY], out_specs=_ANY, scratch_shapes=[dma3, dma3, pltpu.SemaphoreType.DMA],
    )(p)


def _join_halves(s):
    half, cols = s.shape

    def body(s_ref, out_ref, send, recv, loc):
        x, y, c, _ = _place()
        mine = pl.ds(pl.multiple_of(c * half, 8), half)
        theirs = pl.ds(pl.multiple_of((1 - c) * half, 8), half)
        own = pltpu.make_async_copy(s_ref, out_ref.at[mine, :], loc)
        own.start()
        cp = pltpu.make_async_remote_copy(src_ref=s_ref, dst_ref=out_ref.at[mine, :], send_sem=send, recv_sem=recv,
                                          device_id=(x, y, 1 - c), device_id_type=MESH)
        cp.start()
        got = out_ref.at[theirs, :]
        pltpu.make_async_remote_copy(src_ref=got, dst_ref=got, send_sem=send, recv_sem=recv,
                                     device_id=(x, y, 1 - c), device_id_type=MESH).wait_recv()
        cp.wait_send()
        own.wait()

    return pl.pallas_call(
        body, name="grad_join_halves", out_shape=jax.ShapeDtypeStruct((2 * half, cols), s.dtype),
        in_specs=[_ANY], out_specs=_ANY,
        scratch_shapes=[pltpu.SemaphoreType.DMA, pltpu.SemaphoreType.DMA, pltpu.SemaphoreType.DMA],
    )(s)


def _gather_all(p):
    rows, cols = p.shape

    def body(p_ref, out_ref, send, recv, loc):
        x, y, c, _ = _place()
        me = 4 * x + 2 * y + c
        own = pltpu.make_async_copy(p_ref, out_ref.at[me], loc)
        own.start()
        flips = [(fx, fy, fc) for fx in (0, 1) for fy in (0, 1) for fc in (0, 1)][1:]
        peers = [(x ^ fx, y ^ fy, c ^ fc) for fx, fy, fc in flips]
        cps = [pltpu.make_async_remote_copy(src_ref=p_ref, dst_ref=out_ref.at[me], send_sem=send.at[j], recv_sem=recv.at[j],
                                            device_id=peer, device_id_type=MESH) for j, peer in enumerate(peers)]
        for cp in cps:
            cp.start()
        for j, (px, py, pc) in enumerate(peers):
            slot = out_ref.at[4 * px + 2 * py + pc]
            pltpu.make_async_remote_copy(src_ref=slot, dst_ref=slot, send_sem=send.at[j], recv_sem=recv.at[j],
                                         device_id=(px, py, pc), device_id_type=MESH).wait_recv()
        for cp in cps:
            cp.wait_send()
        own.wait()

    dma7 = pltpu.SemaphoreType.DMA((7,))
    return pl.pallas_call(
        body, name="small_grads_gather", out_shape=jax.ShapeDtypeStruct((8, rows, cols), p.dtype),
        in_specs=[_ANY], out_specs=_ANY, scratch_shapes=[dma7, dma7, pltpu.SemaphoreType.DMA],
    )(p)


def _rot_cols(w):
    a, b = jnp.split(w, 2, axis=-1)
    return jnp.concatenate([-b, a], axis=-1)


def _rot_cols_t(g):
    a, b = jnp.split(g, 2, axis=-1)
    return jnp.concatenate([b, -a], axis=-1)


def _cols_from_chips(a):
    n, r, cs = a.shape
    return jnp.transpose(a, (1, 0, 2)).reshape(r, n * cs)


def _cols_to_chips(a):
    r, cc = a.shape
    return jnp.transpose(a.reshape(r, N_CHIPS, cc // N_CHIPS), (1, 0, 2))


def _ff_interleave(a):
    lead = a.shape[:-1]
    t = a.reshape(lead + (2, N_FF_TILES, FF_TILE))
    return jnp.swapaxes(t, -3, -2).reshape(lead + (2 * D_FF,))


def _ff_deinterleave(a):
    lead = a.shape[:-1]
    t = a.reshape(lead + (N_FF_TILES, 2, FF_TILE))
    return jnp.swapaxes(t, -3, -2).reshape(lead + (2 * D_FF,))


_SEG = (D_MODEL, 2 * D_MODEL, 2 * D_MODEL + Q_RANK, 2 * D_MODEL + Q_RANK + KV_RANK, 2 * D_MODEL + Q_RANK + KV_RANK + ROPE,
        3 * D_MODEL + Q_RANK + KV_RANK + ROPE)


def _w_in_to_pad(w):
    u, v, cq, ckv, kr, ga, gb = jnp.split(w, _SEG, axis=1)
    return jnp.concatenate([u, v, ga, gb, cq, ckv, kr, _rot_cols(kr)], axis=1)


def _w_in_from_pad(g):
    u, v, ga, gb, cq, ckv, kr, krr = jnp.split(
        g, (D_MODEL, 2 * D_MODEL, 3 * D_MODEL, 4 * D_MODEL, 4 * D_MODEL + Q_RANK, 4 * D_MODEL + Q_RANK + KV_RANK,
            4 * D_MODEL + Q_RANK + KV_RANK + ROPE), axis=1)
    return jnp.concatenate([u, v, cq, ckv, kr + _rot_cols_t(krr), ga, gb], axis=1)


def _w_uq_to_pad(w):
    t = w.reshape(Q_RANK, HEADS, QK_DIM)
    nope, rope = t[..., :NOPE], t[..., NOPE:]
    return jnp.concatenate([nope, rope, _rot_cols(rope)], axis=-1).reshape(Q_RANK, HEADS * HEAD_PAD)


def _w_uq_from_pad(g):
    t = g.reshape(Q_RANK, HEADS, HEAD_PAD)
    nope, rope, rot = t[..., :NOPE], t[..., NOPE:QK_DIM], t[..., QK_DIM:]
    return jnp.concatenate([nope, rope + _rot_cols_t(rot)], axis=-1).reshape(Q_RANK, HEADS * QK_DIM)


def _w_ukv_to_pad(w):
    t = w.reshape(KV_RANK, HEADS, 2, NOPE)
    return jnp.swapaxes(t, 1, 2).reshape(KV_RANK, 2 * HEADS * NOPE)


def _w_ukv_from_pad(g):
    t = g.reshape(KV_RANK, 2, HEADS, NOPE)
    return jnp.swapaxes(t, 1, 2).reshape(KV_RANK, 2 * HEADS * NOPE)


def _rope_tables(positions):
    inv_freq = 1.0 / (ROPE_THETA ** (jnp.arange(0, ROPE, 2, dtype=F32) / ROPE))
    ang = positions.astype(F32).reshape(-1, 1) * inv_freq
    cos, sin = jnp.cos(ang), jnp.sin(ang)
    zero = jnp.zeros((ang.shape[0], 64), F32)
    return jnp.concatenate([cos, cos, zero], axis=1), jnp.concatenate([sin, sin, zero], axis=1)


_BIG = (("w_in", D_MODEL, 1136, True), ("w_uq", Q_RANK, 384, True), ("w_ukv", KV_RANK, 512, True),
        ("w_out", 256, D_MODEL, False), ("w_up", D_MODEL, 1408, True), ("w_down", 704, D_MODEL, False))
_PACK_ROWS = sum(r * c for _, r, c, _ in _BIG) // 128


def _pack(parts):
    return jnp.concatenate([p.reshape(-1, 128) for p in parts], axis=0)


def _unpack(flat):
    out, o = [], 0
    for _, r, c, _ in _BIG:
        n = r * c // 128
        out.append(flat[..., o:o + n, :].reshape(flat.shape[:-2] + (r, c)))
        o += n
    return out


def _local_step(x, positions, tgt, wts):
    B, S, D = x.shape
    T = B * S
    xf = x.reshape(T, D)
    cos_a, sin_a = _rope_tables(positions)
    bs_t = jnp.pad(wts["a_spatial_b"].T, ((0, 0), (0, 128 - A_GROUPS)))

    h = _rms_fwd(xf, wts["mix_norm"], "norm1_fwd")
    z = _mm(h, wts["w_in"], "nn", "in_proj", tm=512, tn=1536, tk=D)
    q, k, v, cqn, ckvn = _lat_fwd(z, wts["q_a_norm"], wts["kv_a_norm"], wts["w_q"], wts["w_kv"], cos_a, sin_a)
    yb, lse = _attn_fwd(q, k, v, B, S)
    merged = _mix_fwd(z, yb, wts["a_v_norm_g"], wts["a_v_norm_b"], wts["a_spatial_w"], bs_t)
    x1 = _mm(merged, wts["w_out"], "nn", "out_proj", tm=512, tn=D, tk=D, add=xf)
    h2 = _rms_fwd(x1, wts["ffn_norm"], "norm2_fwd")
    up_pre = _mm(h2, wts["w_up"], "nn", "up_proj", tm=512, tn=1408, tk=D)
    act = _gate_fwd(up_pre, wts["conv_w"], wts["conv_b"], B, S)
    x2 = _mm(act, wts["w_down"], "nn", "down_proj", tm=512, tn=D, tk=1408, add=x1)
    dx2, loss_row, g_final = _final(x2, tgt.reshape(T, D), wts["final_norm"])

    g = {"final_norm": g_final}
    dact = _mm(dx2, wts["w_down"], "nt", "down_proj_dx", tm=512, tn=1408, tk=D)
    g["w_down"] = _mm(act, dx2, "tn", "down_proj_dw", tm=1408, tn=D, tk=512)
    dup, g["conv_w"], g["conv_b"] = _gate_bwd(up_pre, dact, wts["conv_w"], wts["conv_b"], B, S)
    dh2 = _mm(dup, wts["w_up"], "nt", "up_proj_dx", tm=512, tn=D, tk=1408)
    g["w_up"] = _mm(h2, dup, "tn", "up_proj_dw", tm=D, tn=1408, tk=512)
    dx1, g["ffn_norm"] = _rms_bwd(x1, wts["ffn_norm"], dh2, dx2, "norm2_bwd")
    dm = _mm(dx1, wts["w_out"], "nt", "out_proj_dx", tm=512, tn=D, tk=D)
    g["w_out"] = _mm(merged, dx1, "tn", "out_proj_dw", tm=D, tn=D, tk=512)
    dz, dyb, dl, g["a_spatial_w"], gbs, g["a_v_norm_g"], g["a_v_norm_b"] = _mix_bwd(
        z, yb, dm, wts["a_v_norm_g"], wts["a_v_norm_b"], wts["a_spatial_w"], bs_t)
    g["a_spatial_b"] = gbs[:, :A_GROUPS].T
    delta = dl[:, :HEADS].T.reshape(HEADS, T, 1)
    dq, dk, dv = _attn_bwd(q, k, v, dyb, lse, delta, B, S)
    dz, dq_raw, dkv, g["q_a_norm"], g["kv_a_norm"] = _lat_bwd(
        dz, z, dq, dk, dv, wts["q_a_norm"], wts["kv_a_norm"], wts["w_q"], wts["w_kv"], cos_a, sin_a)
    g["w_q"] = _mm(cqn, dq_raw, "tn", "q_proj_dw", tm=Q_RANK, tn=HEADS * HEAD_PAD, tk=512)
    g["w_kv"] = _mm(ckvn, dkv, "tn", "kv_proj_dw", tm=KV_RANK, tn=2 * HEADS * NOPE, tk=512)
    dh = _mm(dz, wts["w_in"], "nt", "in_proj_dx", tm=512, tn=D, tk=1536)
    g["w_in"] = _mm(h, dz, "tn", "in_proj_dw", tm=D, tn=1536, tk=512)
    dx, g["mix_norm"] = _rms_bwd(xf, wts["mix_norm"], dh, dx1, "norm1_bwd")
    return loss_row[0, 0], dx.reshape(B, S, D), g


_SMALL = (("mix_norm", (1, D_MODEL)), ("a_v_norm_g", (1, D_MODEL)), ("a_v_norm_b", (1, D_MODEL)),
          ("a_spatial_w", (A_GROUPS * CHUNK, CHUNK)), ("a_spatial_b", (1, A_GROUPS * CHUNK)), ("q_a_norm", (1, Q_RANK)),
          ("kv_a_norm", (1, KV_RANK)), ("ffn_norm", (1, D_MODEL)), ("conv_b", (1, 2 * D_FF)), ("final_norm", (1, D_MODEL)),
          ("conv_w", (3, 2 * D_FF)))
_SMALL_ROWS = -(-sum(math.prod(s) for _, s in _SMALL) // (128 * 8)) * 8


def kernel(x, positions, mix_norm, w_in, a_v_norm_g, a_v_norm_b, a_spatial_w, a_spatial_b, q_a_norm, w_uq, kv_a_norm, w_ukv, w_out, ffn_norm, w_up, conv_w, conv_b, w_down, final_norm, loss_target, m_mix_norm, m_w_in, m_a_v_norm_g, m_a_v_norm_b, m_a_spatial_w, m_a_spatial_b, m_q_a_norm, m_w_uq, m_kv_a_norm, m_w_ukv, m_w_out, m_ffn_norm, m_w_up, m_conv_w, m_conv_b, m_w_down, m_final_norm, v_mix_norm, v_w_in, v_a_v_norm_g, v_a_v_norm_b, v_a_spatial_w, v_a_spatial_b, v_q_a_norm, v_w_uq, v_kv_a_norm, v_w_ukv, v_w_out, v_ffn_norm, v_w_up, v_conv_w, v_conv_b, v_w_down, v_final_norm):
    weights = dict(mix_norm=mix_norm, w_in=w_in, a_v_norm_g=a_v_norm_g, a_v_norm_b=a_v_norm_b, a_spatial_w=a_spatial_w,
                   a_spatial_b=a_spatial_b, q_a_norm=q_a_norm, w_uq=w_uq, kv_a_norm=kv_a_norm, w_ukv=w_ukv, w_out=w_out,
                   ffn_norm=ffn_norm, w_up=w_up, conv_w=conv_w, conv_b=conv_b, w_down=w_down, final_norm=final_norm)
    m_in = dict(mix_norm=m_mix_norm, w_in=m_w_in, a_v_norm_g=m_a_v_norm_g, a_v_norm_b=m_a_v_norm_b,
                a_spatial_w=m_a_spatial_w, a_spatial_b=m_a_spatial_b, q_a_norm=m_q_a_norm, w_uq=m_w_uq,
                kv_a_norm=m_kv_a_norm, w_ukv=m_w_ukv, w_out=m_w_out, ffn_norm=m_ffn_norm, w_up=m_w_up, conv_w=m_conv_w,
                conv_b=m_conv_b, w_down=m_w_down, final_norm=m_final_norm)
    v_in = dict(mix_norm=v_mix_norm, w_in=v_w_in, a_v_norm_g=v_a_v_norm_g, a_v_norm_b=v_a_v_norm_b,
                a_spatial_w=v_a_spatial_w, a_spatial_b=v_a_spatial_b, q_a_norm=v_q_a_norm, w_uq=v_w_uq,
                kv_a_norm=v_kv_a_norm, w_ukv=v_w_ukv, w_out=v_w_out, ffn_norm=v_ffn_norm, w_up=v_w_up, conv_w=v_conv_w,
                conv_b=v_conv_b, w_down=v_w_down, final_norm=v_final_norm)
    names = list(weights)
    chip = 2 * lax.axis_index("x") + lax.axis_index("y")

    pack = _pack([weights[n][0].astype(MXU_DTYPE) for n, _, _, _ in _BIG])
    got, cw_all = _gather_weights(pack, conv_w[0])
    full = {}
    for (n, _, _, by_cols), a in zip(_BIG, _unpack(got)):
        full[n] = _cols_from_chips(a) if by_cols else a.reshape(-1, a.shape[-1])
    wts = dict(
        mix_norm=mix_norm, a_v_norm_g=a_v_norm_g, a_v_norm_b=a_v_norm_b, a_spatial_w=a_spatial_w[0],
        a_spatial_b=a_spatial_b[0], q_a_norm=q_a_norm, kv_a_norm=kv_a_norm, ffn_norm=ffn_norm,
        final_norm=final_norm.reshape(1, D_MODEL),
        w_in=_w_in_to_pad(full["w_in"]), w_q=_w_uq_to_pad(full["w_uq"]), w_kv=_w_ukv_to_pad(full["w_ukv"]),
        w_out=full["w_out"], w_up=_ff_interleave(full["w_up"]), w_down=full["w_down"],
        conv_w=_ff_interleave(_cols_from_chips(cw_all)), conv_b=_ff_interleave(conv_b))

    loss_part, grad_x, g = _local_step(x, positions, loss_target, wts)
    loss = lax.psum(loss_part, ("x", "y", "c"))

    g_ref_layout = [_w_in_from_pad(g["w_in"]), _w_uq_from_pad(g["w_q"]), _w_ukv_from_pad(g["w_kv"]), g["w_out"],
                    _ff_deinterleave(g["w_up"]), g["w_down"]]
    slabs = []
    for (n, r, c, by_cols), a in zip(_BIG, g_ref_layout):
        s = _cols_to_chips(a) if by_cols else a.reshape(N_CHIPS, r, c)
        slabs.append(s.reshape(N_CHIPS, r * c // 128, 128))
    gpack = jnp.concatenate(slabs, axis=1)
    half = _PACK_ROWS // 2
    core = lax.axis_index("c")
    mine = lax.dynamic_slice_in_dim(gpack, core * half, half, axis=1)
    from_sibling = _swap_halves(gpack)
    chip_sum = _sum_slabs([mine.reshape(-1, 128), from_sibling.reshape(-1, 128)], "grad_pair_sum", tr=3664)
    landed = _scatter_to_chips(chip_sum.reshape(N_CHIPS, half, 128))
    reduced_half = _sum_slabs([landed[j] for j in range(N_CHIPS)], "grad_chip_sum", tr=3664)
    g_big = dict(zip([n for n, _, _, _ in _BIG], _unpack(_join_halves(reduced_half))))

    g_small_parts = dict(g)
    g_small_parts["conv_w"] = _ff_deinterleave(g["conv_w"])
    g_small_parts["conv_b"] = _ff_deinterleave(g["conv_b"])
    flat = jnp.concatenate([g_small_parts[n].reshape(-1) for n, _ in _SMALL])
    flat = jnp.pad(flat, (0, _SMALL_ROWS * 128 - flat.shape[0])).reshape(_SMALL_ROWS, 128)
    everyone = _gather_all(flat)
    total = _sum_slabs([everyone[j] for j in range(8)], "small_grads_sum", tr=_SMALL_ROWS).reshape(-1)
    g_small, o = {}, 0
    for n, shp in _SMALL:
        g_small[n] = total[o:o + math.prod(shp)].reshape(shp)
        o += math.prod(shp)
    g_small["conv_w"] = lax.dynamic_slice_in_dim(g_small["conv_w"], chip * 1408, 1408, axis=1)

    grads, deltas, new_m, new_v = {}, {}, {}, {}
    for n in names:
        w = weights[n]
        g2 = g_big[n] if n in g_big else g_small[n]
        shape2 = g2.shape
        d, nm, nv = _adamw(w.reshape(shape2), g2, m_in[n].reshape(shape2), v_in[n].reshape(shape2), "adamw_" + n)
        grads[n], deltas[n], new_m[n], new_v[n] = (t.reshape(w.shape) for t in (g2, d, nm, nv))
    return (loss, grad_x, *[grads[n] for n in names], *[deltas[n] for n in names], *[new_m[n] for n in names],
            *[new_v[n] for n in names])
```

```python
import functools
import math

import jax
import jax.numpy as jnp
from jax import lax
from jax.experimental import pallas as pl
from jax.experimental.pallas import tpu as pltpu

F32 = jnp.float32
MXU_DTYPE = jnp.bfloat16
MESH = pl.DeviceIdType.MESH

D_MODEL = 1024
EPS = 1e-6
A_GROUPS = 8
CHUNK = 128
HEADS = 8
NOPE = 128
ROPE = 64
QK_DIM = NOPE + ROPE
HEAD_PAD = 256
Q_RANK = 256
KV_RANK = 128
ROPE_THETA = 10000.0
D_FF = 2816
FF_TILE = 256
N_FF_TILES = D_FF // FF_TILE
LAT = 512
IN_PAD = 4 * D_MODEL + LAT
N_CHIPS = 4
ADAM_LR, ADAM_B1, ADAM_B2, ADAM_EPS, ADAM_WD, ADAM_STEP = 0.001, 0.9, 0.999, 1e-08, 0.01, 10

VMEM_CAP_V7X = 64 * 1024 * 1024
NEG = -1e30


def _params(sem, nbytes):
    limit = int(min(VMEM_CAP_V7X - (8 << 20), max(32 << 20, 3 * nbytes)))
    return pltpu.CompilerParams(dimension_semantics=sem, vmem_limit_bytes=limit)


def _nbytes(shape, dtype):
    return math.prod(shape) * jnp.dtype(dtype).itemsize


_DIMS = {"nn": (((1,), (0,)), ((), ())), "nt": (((1,), (1,)), ((), ())), "tn": (((0,), (0,)), ((), ()))}


def _mm(a, b, mode, name, *, tm, tn, tk, out_dtype=F32, add=None, dims=None, a_spec=None, b_spec=None,
        o_spec=None, out_shape=None):
    if dims is None:
        if mode == "nn":
            (M, K), (_, N) = a.shape, b.shape
        elif mode == "nt":
            (M, K), (N, _) = a.shape, b.shape
        else:
            (K, M), (_, N) = a.shape, b.shape
    else:
        M, N, K = dims
    a_blk = (tk, tm) if mode == "tn" else (tm, tk)
    b_blk = (tn, tk) if mode == "nt" else (tk, tn)
    if a_spec is None:
        a_spec = pl.BlockSpec(a_blk, (lambda i, j, k: (k, i)) if mode == "tn" else (lambda i, j, k: (i, k)))
    if b_spec is None:
        b_spec = pl.BlockSpec(b_blk, (lambda i, j, k: (j, k)) if mode == "nt" else (lambda i, j, k: (k, j)))
    if o_spec is None:
        o_spec = pl.BlockSpec((tm, tn), lambda i, j, k: (i, j))
    if out_shape is None:
        out_shape = (M, N)
    assert M % tm == 0 and N % tn == 0 and K % tk == 0, (name, M, N, K, tm, tn, tk)
    nk = K // tk
    contract = _DIMS[mode]
    has_add = add is not None

    def body(*refs):
        if has_add:
            a_ref, b_ref, add_ref, o_ref, acc = refs
        else:
            a_ref, b_ref, o_ref, acc = refs
        k = pl.program_id(2)

        @pl.when(k == 0)
        def _():
            acc[...] = jnp.zeros_like(acc)

        acc[...] += lax.dot_general(a_ref[...].astype(MXU_DTYPE), b_ref[...].astype(MXU_DTYPE), contract,
                                    preferred_element_type=F32)

        @pl.when(k == nk - 1)
        def _():
            r = acc[...]
            if has_add:
                r = r + add_ref[...]
            o_ref[...] = r.astype(out_dtype)

    in_specs = [a_spec, b_spec]
    args = [a, b]
    nbytes = _nbytes(a_blk, a.dtype) + _nbytes(b_blk, b.dtype) + 3 * _nbytes((tm, tn), F32)
    if has_add:
        in_specs.append(pl.BlockSpec((tm, tn), lambda i, j, k: (i, j)))
        args.append(add)
        nbytes += _nbytes((tm, tn), F32)
    return pl.pallas_call(
        body, name=name, out_shape=jax.ShapeDtypeStruct(out_shape, out_dtype),
        grid=(M // tm, N // tn, nk), in_specs=in_specs, out_specs=o_spec,
        scratch_shapes=[pltpu.VMEM((tm, tn), F32)],
        compiler_params=_params(("parallel", "parallel", "arbitrary"), nbytes),
    )(*args)


_GELU_C = math.sqrt(2.0 / math.pi)
_GELU_A = 0.044715


def _sigmoid(x):
    return 1.0 / (1.0 + jnp.exp(-x))


def _gelu(x):
    t = jnp.tanh(_GELU_C * (x + _GELU_A * (x * x * x)))
    return x * (0.5 * (1.0 + t))


def _gelu_and_grad(x):
    x2 = x * x
    t = jnp.tanh(_GELU_C * (x + _GELU_A * (x2 * x)))
    cdf = 0.5 * (1.0 + t)
    grad = cdf + 0.5 * x * (1.0 - t * t) * (_GELU_C * (1.0 + 3.0 * _GELU_A * x2))
    return x * cdf, grad


def _rope_mix(g, cos_a, sin_a):
    return g * cos_a + pltpu.roll(g, 64, 1) * sin_a


def _rope_mix_bwd(d, cos_a, sin_a):
    return d * cos_a + pltpu.roll(d * sin_a, 64, 1)


def _rms_fwd(x, g, name, tr=512):
    T, D = x.shape

    def body(x_ref, g_ref, h_ref):
        xv = x_ref[...]
        r = lax.rsqrt(jnp.mean(xv * xv, axis=-1, keepdims=True) + EPS)
        h_ref[...] = ((xv * r) * g_ref[...]).astype(h_ref.dtype)

    return pl.pallas_call(
        body, name=name, out_shape=jax.ShapeDtypeStruct((T, D), MXU_DTYPE), grid=(T // tr,),
        in_specs=[pl.BlockSpec((tr, D), lambda i: (i, 0)), pl.BlockSpec((1, D), lambda i: (0, 0))],
        out_specs=pl.BlockSpec((tr, D), lambda i: (i, 0)),
        compiler_params=_params(("parallel",), 3 * _nbytes((tr, D), F32)),
    )(x, g)


def _rms_bwd(x, g, dh, dres, name, tr=512):
    T, D = x.shape

    def body(x_ref, g_ref, dh_ref, dres_ref, dx_ref, gg_ref):
        @pl.when(pl.program_id(0) == 0)
        def _():
            gg_ref[...] = jnp.zeros_like(gg_ref)

        xv = x_ref[...]
        r = lax.rsqrt(jnp.mean(xv * xv, axis=-1, keepdims=True) + EPS)
        xn = xv * r
        dhv = dh_ref[...]
        dxn = dhv * g_ref[...]
        dx_ref[...] = dres_ref[...] + r * (dxn - xn * jnp.mean(dxn * xn, axis=-1, keepdims=True))
        gg_ref[...] += jnp.sum(dhv * xn, axis=0, keepdims=True)

    row = pl.BlockSpec((tr, D), lambda i: (i, 0))
    vec = pl.BlockSpec((1, D), lambda i: (0, 0))
    return pl.pallas_call(
        body, name=name,
        out_shape=(jax.ShapeDtypeStruct((T, D), F32), jax.ShapeDtypeStruct((1, D), F32)),
        grid=(T // tr,), in_specs=[row, vec, row, row], out_specs=(row, vec),
        compiler_params=_params(("arbitrary",), 6 * _nbytes((tr, D), F32)),
    )(x, g, dh, dres)


def _lat_fwd(z, gq, gkv, wq, wkv, cos_a, sin_a, tr=256):
    T = z.shape[0]
    lat_blk = (4 * D_MODEL) // LAT

    def body(z_ref, gq_ref, gkv_ref, wq_ref, wkv_ref, cos_ref, sin_ref, q_ref, k_ref, v_ref, cqn_ref, ckvn_ref):
        zl = z_ref[...]
        cos_v, sin_v = cos_ref[...], sin_ref[...]
        cq = zl[:, :Q_RANK]
        ckv = zl[:, Q_RANK:Q_RANK + KV_RANK]
        krb = zl[:, Q_RANK + KV_RANK:]
        cqn = ((cq * lax.rsqrt(jnp.mean(cq * cq, axis=-1, keepdims=True) + EPS)) * gq_ref[...]).astype(MXU_DTYPE)
        ckvn = ((ckv * lax.rsqrt(jnp.mean(ckv * ckv, axis=-1, keepdims=True) + EPS)) * gkv_ref[...]).astype(MXU_DTYPE)
        cqn_ref[...] = cqn
        ckvn_ref[...] = ckvn
        krr = _rope_mix(krb, cos_v, sin_v).astype(MXU_DTYPE)
        q = jnp.dot(cqn, wq_ref[...], preferred_element_type=F32)
        kv = jnp.dot(ckvn, wkv_ref[...], preferred_element_type=F32)
        for h in range(HEADS):
            o = h * HEAD_PAD
            q_ref[:, o:o + NOPE] = q[:, o:o + NOPE].astype(MXU_DTYPE)
            q_ref[:, o + NOPE:o + HEAD_PAD] = _rope_mix(q[:, o + NOPE:o + HEAD_PAD], cos_v, sin_v).astype(MXU_DTYPE)
            k_ref[:, o:o + NOPE] = kv[:, h * NOPE:(h + 1) * NOPE].astype(MXU_DTYPE)
            k_ref[:, o + NOPE:o + HEAD_PAD] = krr
        v_ref[...] = kv[:, HEADS * NOPE:].astype(MXU_DTYPE)

    def row(w):
        return pl.BlockSpec((tr, w), lambda i: (i, 0))

    def full(a):
        return pl.BlockSpec(a.shape, lambda i: (0, 0))

    return pl.pallas_call(
        body, name="lat_fwd",
        out_shape=(jax.ShapeDtypeStruct((T, HEADS * HEAD_PAD), MXU_DTYPE), jax.ShapeDtypeStruct((T, HEADS * HEAD_PAD), MXU_DTYPE),
                   jax.ShapeDtypeStruct((T, HEADS * NOPE), MXU_DTYPE), jax.ShapeDtypeStruct((T, Q_RANK), MXU_DTYPE),
                   jax.ShapeDtypeStruct((T, KV_RANK), MXU_DTYPE)),
        grid=(T // tr,),
        in_specs=[pl.BlockSpec((tr, LAT), lambda i: (i, lat_blk)), full(gq), full(gkv), full(wq), full(wkv), row(128), row(128)],
        out_specs=(row(HEADS * HEAD_PAD), row(HEADS * HEAD_PAD), row(HEADS * NOPE), row(Q_RANK), row(KV_RANK)),
        compiler_params=_params(("parallel",), 8 * _nbytes((tr, HEADS * HEAD_PAD), F32)),
    )(z, gq, gkv, wq, wkv, cos_a, sin_a)


ATT_BLOCK = 256
_SCALE = QK_DIM ** -0.5


def _causal_mask(n):
    return lax.broadcasted_iota(jnp.int32, (n, n), 1) <= lax.broadcasted_iota(jnp.int32, (n, n), 0)


def _causal_mask_t(n):
    return lax.broadcasted_iota(jnp.int32, (n, n), 0) <= lax.broadcasted_iota(jnp.int32, (n, n), 1)


def _attn_fwd(q, k, v, B, S):
    tq = ATT_BLOCK
    nq = S // tq
    T = B * S

    def body(q_ref, k_ref, v_ref, o_ref, lse_ref):
        qi = pl.program_id(2)
        qv = q_ref[...]

        def step(j, carry, masked):
            m, l, acc = carry
            rows = pl.ds(pl.multiple_of(j * tq, tq), tq)
            st = lax.dot_general(k_ref[rows, :], qv, _DIMS["nt"], preferred_element_type=F32) * _SCALE
            if masked:
                st = jnp.where(_causal_mask_t(tq), st, NEG)
            m_new = jnp.maximum(m, jnp.max(st, axis=0, keepdims=True))
            alpha = jnp.exp(m - m_new)
            p = jnp.exp(st - m_new)
            l = alpha * l + jnp.sum(p, axis=0, keepdims=True)
            acc = alpha * acc + lax.dot_general(v_ref[rows, :], p.astype(MXU_DTYPE), _DIMS["tn"],
                                                preferred_element_type=F32)
            return m_new, l, acc

        init = (jnp.full((1, tq), NEG, F32), jnp.zeros((1, tq), F32), jnp.zeros((NOPE, tq), F32))
        carry = lax.fori_loop(0, qi, lambda j, c: step(j, c, False), init)
        m, l, acc = step(qi, carry, True)
        o_ref[...] = (acc / l).T
        lse_ref[0] = m + jnp.log(l)

    return pl.pallas_call(
        body, name="attn_fwd",
        out_shape=(jax.ShapeDtypeStruct((T, HEADS * NOPE), F32), jax.ShapeDtypeStruct((HEADS * B * nq, 1, tq), F32)),
        grid=(B, HEADS, nq),
        in_specs=[pl.BlockSpec((tq, HEAD_PAD), lambda b, h, i: (b * nq + i, h)),
                  pl.BlockSpec((S, HEAD_PAD), lambda b, h, i: (b, h)),
                  pl.BlockSpec((S, NOPE), lambda b, h, i: (b, h))],
        out_specs=(pl.BlockSpec((tq, NOPE), lambda b, h, i: (b * nq + i, h)),
                   pl.BlockSpec((1, 1, tq), lambda b, h, i: ((h * B + b) * nq + i, 0, 0))),
        compiler_params=_params(("parallel", "parallel", "arbitrary"), 4 * _nbytes((S, HEAD_PAD), MXU_DTYPE)),
    )(q, k, v)


def _attn_bwd(q, k, v, do, lse, delta, B, S):
    tq = ATT_BLOCK
    nq = S // tq
    T = B * S

    def body(q_ref, k_ref, v_ref, do_ref, lse_ref, dl_ref, dq_ref, dk_ref, dv_ref):
        kj = pl.program_id(2)

        @pl.when(kj == 0)
        def _():
            dq_ref[...] = jnp.zeros_like(dq_ref)

        kv_, vv = k_ref[...], v_ref[...]

        def step(i, carry, masked):
            dk, dv = carry
            rows = pl.ds(pl.multiple_of(i * tq, tq), tq)
            qv, dov = q_ref[rows, :], do_ref[rows, :]
            st = lax.dot_general(kv_, qv, _DIMS["nt"], preferred_element_type=F32) * _SCALE
            p = jnp.exp(st - lse_ref[i])
            if masked:
                p = jnp.where(_causal_mask_t(tq), p, 0.0)
            dv = dv + jnp.dot(p.astype(MXU_DTYPE), dov, preferred_element_type=F32)
            dpt = lax.dot_general(vv, dov, _DIMS["nt"], preferred_element_type=F32)
            ds = (p * (dpt - dl_ref[i]) * _SCALE).astype(MXU_DTYPE)
            dk = dk + jnp.dot(ds, qv, preferred_element_type=F32)
            dq_ref[rows, :] += lax.dot_general(ds, kv_, _DIMS["tn"], preferred_element_type=F32)
            return dk, dv

        carry = step(kj, (jnp.zeros((tq, HEAD_PAD), F32), jnp.zeros((tq, NOPE), F32)), True)
        dk, dv = lax.fori_loop(kj + 1, nq, lambda i, c: step(i, c, False), carry)
        dk_ref[...] = dk
        dv_ref[...] = dv.astype(dv_ref.dtype)

    seq = lambda w: pl.BlockSpec((S, w), lambda b, h, j: (b, h))
    stat = pl.BlockSpec((nq, 1, tq), lambda b, h, j: (h * B + b, 0, 0))
    return pl.pallas_call(
        body, name="attn_bwd",
        out_shape=(jax.ShapeDtypeStruct((T, HEADS * HEAD_PAD), F32), jax.ShapeDtypeStruct((T, HEADS * HEAD_PAD), F32),
                   jax.ShapeDtypeStruct((T, HEADS * NOPE), MXU_DTYPE)),
        grid=(B, HEADS, nq),
        in_specs=[seq(HEAD_PAD), pl.BlockSpec((tq, HEAD_PAD), lambda b, h, j: (b * nq + j, h)),
                  pl.BlockSpec((tq, NOPE), lambda b, h, j: (b * nq + j, h)), seq(NOPE), stat, stat],
        out_specs=(seq(HEAD_PAD), pl.BlockSpec((tq, HEAD_PAD), lambda b, h, j: (b * nq + j, h)),
                   pl.BlockSpec((tq, NOPE), lambda b, h, j: (b * nq + j, h))),
        compiler_params=_params(("parallel", "parallel", "arbitrary"), 8 * _nbytes((S, HEAD_PAD), F32)),
    )(q, k, v, do, lse, delta)


MIX_ROWS = 256


def _tril_weights(ws_ref, g):
    return jnp.where(_causal_mask(CHUNK), ws_ref[g], 0.0).astype(MXU_DTYPE)


def _layer_norm_stats(va):
    mu = jnp.mean(va, axis=-1, keepdims=True)
    xc = va - mu
    rs = lax.rsqrt(jnp.mean(xc * xc, axis=-1, keepdims=True) + EPS)
    return xc * rs


def _mix_specs(tr):
    zcol = lambda c: pl.BlockSpec((tr, D_MODEL), lambda i, c=c: (i, c))
    row = pl.BlockSpec((tr, D_MODEL), lambda i: (i, 0))
    vec = pl.BlockSpec((1, D_MODEL), lambda i: (0, 0))
    ws = pl.BlockSpec((A_GROUPS, CHUNK, CHUNK), lambda i: (0, 0, 0))
    bs = pl.BlockSpec((CHUNK, 128), lambda i: (0, 0))
    return zcol, row, vec, ws, bs


def _mix_fwd(z, yb, ln_g, ln_b, ws, bs_t):
    T = z.shape[0]
    tr = MIX_ROWS
    zcol, row, vec, ws_spec, bs_spec = _mix_specs(tr)

    def body(zu_ref, zv_ref, zga_ref, zgb_ref, yb_ref, g_ref, b_ref, ws_ref, bs_ref, out_ref, vn_s):
        vhat = _layer_norm_stats(_gelu(zv_ref[...]))
        vn_s[...] = (vhat * g_ref[...] + b_ref[...]).astype(MXU_DTYPE)
        for g in range(A_GROUPS):
            w = _tril_weights(ws_ref, g)
            bias = bs_ref[:, g:g + 1]
            cols = slice(g * CHUNK, (g + 1) * CHUNK)
            for c in range(tr // CHUNK):
                rows = slice(c * CHUNK, (c + 1) * CHUNK)
                mixed = jnp.dot(w, vn_s[rows, cols], preferred_element_type=F32) + bias
                ya = _gelu(zu_ref[rows, cols]) * mixed
                merged = _sigmoid(zga_ref[rows, cols]) * ya + _sigmoid(zgb_ref[rows, cols]) * yb_ref[rows, cols]
                out_ref[rows, cols] = merged.astype(MXU_DTYPE)

    return pl.pallas_call(
        body, name="mix_fwd", out_shape=jax.ShapeDtypeStruct((T, D_MODEL), MXU_DTYPE), grid=(T // tr,),
        in_specs=[zcol(0), zcol(1), zcol(2), zcol(3), row, vec, vec, ws_spec, bs_spec], out_specs=row,
        scratch_shapes=[pltpu.VMEM((tr, D_MODEL), MXU_DTYPE)],
        compiler_params=_params(("parallel",), 8 * _nbytes((tr, D_MODEL), F32)),
    )(z, z, z, z, yb, ln_g, ln_b, ws, bs_t)


def _mix_bwd(z, yb, dm, ln_g, ln_b, ws, bs_t):
    T = z.shape[0]
    tr = MIX_ROWS
    zcol, row, vec, ws_spec, bs_spec = _mix_specs(tr)

    def body(zu_ref, zv_ref, zga_ref, zgb_ref, yb_ref, dm_ref, g_ref, b_ref, ws_ref, bs_ref,
             dz_ref, dyb_ref, dl_ref, gws_ref, gbs_ref, glg_ref, glb_ref, vn_s, dvn_s):
        @pl.when(pl.program_id(0) == 0)
        def _():
            gws_ref[...] = jnp.zeros_like(gws_ref)
            gbs_ref[...] = jnp.zeros_like(gbs_ref)
            glg_ref[...] = jnp.zeros_like(glg_ref)
            glb_ref[...] = jnp.zeros_like(glb_ref)

        lane = lax.broadcasted_iota(jnp.int32, (CHUNK, 128), 1)
        va, dgelu_v = _gelu_and_grad(zv_ref[...])
        mu = jnp.mean(va, axis=-1, keepdims=True)
        xc = va - mu
        rs = lax.rsqrt(jnp.mean(xc * xc, axis=-1, keepdims=True) + EPS)
        vhat = xc * rs
        vn_s[...] = (vhat * g_ref[...] + b_ref[...]).astype(MXU_DTYPE)
        gbs_acc = jnp.zeros((CHUNK, 128), F32)
        for g in range(A_GROUPS):
            w = _tril_weights(ws_ref, g)
            bias = bs_ref[:, g:g + 1]
            cols = slice(g * CHUNK, (g + 1) * CHUNK)
            gw_acc = jnp.zeros((CHUNK, CHUNK), F32)
            for c in range(tr // CHUNK):
                rows = slice(c * CHUNK, (c + 1) * CHUNK)
                vn = vn_s[rows, cols]
                mixed = jnp.dot(w, vn, preferred_element_type=F32) + bias
                ua, dgelu_u = _gelu_and_grad(zu_ref[rows, cols])
                dmv = dm_ref[rows, cols]
                sa = _sigmoid(zga_ref[rows, cols])
                dya = dmv * sa
                dz_ref[rows, 2 * D_MODEL + g * CHUNK:2 * D_MODEL + (g + 1) * CHUNK] = (
                    dmv * (ua * mixed) * (sa * (1.0 - sa))).astype(dz_ref.dtype)
                dz_ref[rows, cols] = (dya * mixed * dgelu_u).astype(dz_ref.dtype)
                dmix = dya * ua
                gbs_acc = gbs_acc + jnp.where(lane == g, jnp.sum(dmix, axis=-1, keepdims=True), 0.0)
                dmix_b = dmix.astype(MXU_DTYPE)
                gw_acc = gw_acc + lax.dot_general(dmix_b, vn, _DIMS["nt"], preferred_element_type=F32)
                dvn_s[rows, cols] = lax.dot_general(w, dmix_b, _DIMS["tn"], preferred_element_type=F32)
            gws_ref[g] += jnp.where(_causal_mask(CHUNK), gw_acc, 0.0)
        gbs_ref[...] += gbs_acc

        dvn = dvn_s[...]
        glg_ref[...] += jnp.sum(dvn * vhat, axis=0, keepdims=True)
        glb_ref[...] += jnp.sum(dvn, axis=0, keepdims=True)
        dvh = dvn * g_ref[...]
        dva = rs * (dvh - jnp.mean(dvh, axis=-1, keepdims=True) - vhat * jnp.mean(dvh * vhat, axis=-1, keepdims=True))
        dz_ref[:, D_MODEL:2 * D_MODEL] = (dva * dgelu_v).astype(dz_ref.dtype)

        dmv = dm_ref[...]
        ybv = yb_ref[...]
        sb = _sigmoid(zgb_ref[...])
        dyb = dmv * sb
        dyb_ref[...] = dyb.astype(dyb_ref.dtype)
        dz_ref[:, 3 * D_MODEL:4 * D_MODEL] = (dmv * ybv * (sb * (1.0 - sb))).astype(dz_ref.dtype)
        dz_ref[:, 4 * D_MODEL:] = jnp.zeros((tr, LAT), dz_ref.dtype)
        prod = dyb * ybv
        sel = (lax.broadcasted_iota(jnp.int32, (HEADS, D_MODEL), 1) // NOPE
               == lax.broadcasted_iota(jnp.int32, (HEADS, D_MODEL), 0)).astype(jnp.bfloat16)
        hi = prod.astype(jnp.bfloat16)
        rest = prod - hi.astype(F32)
        mid = rest.astype(jnp.bfloat16)
        lo = (rest - mid.astype(F32)).astype(jnp.bfloat16)
        dl_ref[...] = (lax.dot_general(sel, hi, _DIMS["nt"], preferred_element_type=F32)
                       + lax.dot_general(sel, mid, _DIMS["nt"], preferred_element_type=F32)
                       + lax.dot_general(sel, lo, _DIMS["nt"], preferred_element_type=F32))

    return pl.pallas_call(
        body, name="mix_bwd",
        out_shape=(jax.ShapeDtypeStruct((T, IN_PAD), MXU_DTYPE), jax.ShapeDtypeStruct((T, D_MODEL), MXU_DTYPE),
                   jax.ShapeDtypeStruct((HEADS, T), F32), jax.ShapeDtypeStruct((A_GROUPS, CHUNK, CHUNK), F32),
                   jax.ShapeDtypeStruct((CHUNK, 128), F32), jax.ShapeDtypeStruct((1, D_MODEL), F32),
                   jax.ShapeDtypeStruct((1, D_MODEL), F32)),
        grid=(T // tr,),
        in_specs=[zcol(0), zcol(1), zcol(2), zcol(3), row, row, vec, vec, ws_spec, bs_spec],
        out_specs=(pl.BlockSpec((tr, IN_PAD), lambda i: (i, 0)), row, pl.BlockSpec((HEADS, tr), lambda i: (0, i)),
                   ws_spec, bs_spec, vec, vec),
        scratch_shapes=[pltpu.VMEM((tr, D_MODEL), MXU_DTYPE), pltpu.VMEM((tr, D_MODEL), F32)],
        compiler_params=_params(("arbitrary",), 12 * _nbytes((tr, D_MODEL), F32)),
    )(z, z, z, z, yb, dm, ln_g, ln_b, ws, bs_t)


def _lat_bwd(dz, z, dq, dk, dv, gq, gkv, wq, wkv, cos_a, sin_a, tr=256):
    T = z.shape[0]
    lat_blk = (4 * D_MODEL) // LAT

    def body(dz_in, z_ref, dq_ref, dk_ref, dv_ref, gq_ref, gkv_ref, wq_ref, wkv_ref, cos_ref, sin_ref,
             dz_ref, dqr_ref, dkv_ref, ggq_ref, ggkv_ref):
        del dz_in

        @pl.when(pl.program_id(0) == 0)
        def _():
            ggq_ref[...] = jnp.zeros_like(ggq_ref)
            ggkv_ref[...] = jnp.zeros_like(ggkv_ref)

        cos_v, sin_v = cos_ref[...], sin_ref[...]
        dkr = jnp.zeros((tr, 128), F32)
        for h in range(HEADS):
            o = h * HEAD_PAD
            dqr_ref[:, o:o + NOPE] = dq_ref[:, o:o + NOPE].astype(MXU_DTYPE)
            dqr_ref[:, o + NOPE:o + HEAD_PAD] = _rope_mix_bwd(dq_ref[:, o + NOPE:o + HEAD_PAD], cos_v, sin_v).astype(MXU_DTYPE)
            dkv_ref[:, h * NOPE:(h + 1) * NOPE] = dk_ref[:, o:o + NOPE].astype(MXU_DTYPE)
            dkr = dkr + _rope_mix_bwd(dk_ref[:, o + NOPE:o + HEAD_PAD], cos_v, sin_v)
        dkv_ref[:, HEADS * NOPE:] = dv_ref[...]
        dcqn = lax.dot_general(dqr_ref[...], wq_ref[...], _DIMS["nt"], preferred_element_type=F32)
        dckvn = lax.dot_general(dkv_ref[...], wkv_ref[...], _DIMS["nt"], preferred_element_type=F32)

        zl = z_ref[...]

        def rms_bwd(c, dn, g_ref, gg_ref):
            r = lax.rsqrt(jnp.mean(c * c, axis=-1, keepdims=True) + EPS)
            ch = c * r
            gg_ref[...] += jnp.sum(dn * ch, axis=0, keepdims=True)
            dch = dn * g_ref[...]
            return r * (dch - ch * jnp.mean(dch * ch, axis=-1, keepdims=True))

        dz_ref[:, :Q_RANK] = rms_bwd(zl[:, :Q_RANK], dcqn, gq_ref, ggq_ref).astype(dz_ref.dtype)
        dz_ref[:, Q_RANK:Q_RANK + KV_RANK] = rms_bwd(zl[:, Q_RANK:Q_RANK + KV_RANK], dckvn, gkv_ref, ggkv_ref).astype(dz_ref.dtype)
        dz_ref[:, Q_RANK + KV_RANK:] = dkr.astype(dz_ref.dtype)

    def row(w):
        return pl.BlockSpec((tr, w), lambda i: (i, 0))

    def full(a):
        return pl.BlockSpec(a.shape, lambda i: (0, 0))

    lat = pl.BlockSpec((tr, LAT), lambda i: (i, lat_blk))
    return pl.pallas_call(
        body, name="lat_bwd",
        out_shape=(jax.ShapeDtypeStruct(dz.shape, dz.dtype), jax.ShapeDtypeStruct((T, HEADS * HEAD_PAD), MXU_DTYPE),
                   jax.ShapeDtypeStruct((T, 2 * HEADS * NOPE), MXU_DTYPE), jax.ShapeDtypeStruct(gq.shape, F32),
                   jax.ShapeDtypeStruct(gkv.shape, F32)),
        grid=(T // tr,),
        in_specs=[pl.BlockSpec(memory_space=pl.ANY), lat, row(HEADS * HEAD_PAD), row(HEADS * HEAD_PAD), row(HEADS * NOPE),
                  full(gq), full(gkv), full(wq), full(wkv), row(128), row(128)],
        out_specs=(lat, row(HEADS * HEAD_PAD), row(2 * HEADS * NOPE), full(gq), full(gkv)),
        input_output_aliases={0: 0},
        compiler_params=_params(("arbitrary",), 8 * _nbytes((tr, HEADS * HEAD_PAD), F32)),
    )(dz, z, dq, dk, dv, gq, gkv, wq, wkv, cos_a, sin_a)


def _shift_down(x, k, row_idx):
    return jnp.where(row_idx >= k, pltpu.roll(x, k, 0), 0.0)


def _shift_up(x, k, row_idx, S):
    return jnp.where(row_idx < S - k, pltpu.roll(x, S - k, 0), 0.0)


def _conv(x, cw, cb, row_idx):
    return cb + cw[0:1, :] * _shift_down(x, 2, row_idx) + cw[1:2, :] * _shift_down(x, 1, row_idx) + cw[2:3, :] * x


def _gate_fwd(up3, conv_w, conv_b, B, S):
    T = B * S
    W = FF_TILE

    def body(up_ref, cw_ref, cb_ref, act_ref):
        row_idx = lax.broadcasted_iota(jnp.int32, (S, W), 0)
        gate = _conv(up_ref[0], cw_ref[0], cb_ref[0], row_idx)
        val = _conv(up_ref[1], cw_ref[1], cb_ref[1], row_idx)
        act_ref[...] = (gate * _sigmoid(gate) * val).astype(act_ref.dtype)

    return pl.pallas_call(
        body, name="gate_fwd", out_shape=jax.ShapeDtypeStruct((T, D_FF), MXU_DTYPE), grid=(B, N_FF_TILES),
        in_specs=[pl.BlockSpec((2, S, W), lambda b, j: (0, b, j)), pl.BlockSpec((2, 3, W), lambda b, j: (0, 0, j)),
                  pl.BlockSpec((2, 1, W), lambda b, j: (0, 0, j))],
        out_specs=pl.BlockSpec((S, W), lambda b, j: (b, j)),
        compiler_params=_params(("parallel", "parallel"), 12 * _nbytes((S, W), F32)),
    )(up3, conv_w, conv_b)


def _gate_bwd(up3, dact, conv_w, conv_b, B, S):
    T = B * S
    W = FF_TILE

    def body(up_ref, da_ref, cw_ref, cb_ref, dup_ref, gcw_ref, gcb_ref):
        @pl.when(pl.program_id(1) == 0)
        def _():
            gcw_ref[...] = jnp.zeros_like(gcw_ref)
            gcb_ref[...] = jnp.zeros_like(gcb_ref)

        row_idx = lax.broadcasted_iota(jnp.int32, (S, W), 0)
        gate = _conv(up_ref[0], cw_ref[0], cb_ref[0], row_idx)
        val = _conv(up_ref[1], cw_ref[1], cb_ref[1], row_idx)
        sg = _sigmoid(gate)
        da = da_ref[...]
        d_halves = (da * val * (sg * (1.0 + gate * (1.0 - sg))), da * (gate * sg))
        for half, dup in enumerate(d_halves):
            x = up_ref[half]
            cw = cw_ref[half]
            gcb_ref[half] += jnp.sum(dup, axis=0, keepdims=True)
            gcw_ref[half, 0:1, :] += jnp.sum(dup * _shift_down(x, 2, row_idx), axis=0, keepdims=True)
            gcw_ref[half, 1:2, :] += jnp.sum(dup * _shift_down(x, 1, row_idx), axis=0, keepdims=True)
            gcw_ref[half, 2:3, :] += jnp.sum(dup * x, axis=0, keepdims=True)
            dx = (cw[2:3, :] * dup + cw[1:2, :] * _shift_up(dup, 1, row_idx, S) + cw[0:1, :] * _shift_up(dup, 2, row_idx, S))
            dup_ref[half] = dx.astype(dup_ref.dtype)

    up_spec = pl.BlockSpec((2, S, W), lambda j, b: (0, b, j))
    cw_spec = pl.BlockSpec((2, 3, W), lambda j, b: (0, 0, j))
    cb_spec = pl.BlockSpec((2, 1, W), lambda j, b: (0, 0, j))
    return pl.pallas_call(
        body, name="gate_bwd",
        out_shape=(jax.ShapeDtypeStruct((2, T, D_FF), MXU_DTYPE), jax.ShapeDtypeStruct((2, 3, D_FF), F32),
                   jax.ShapeDtypeStruct((2, 1, D_FF), F32)),
        grid=(N_FF_TILES, B),
        in_specs=[up_spec, pl.BlockSpec((S, W), lambda j, b: (b, j)), cw_spec, cb_spec],
        out_specs=(up_spec, cw_spec, cb_spec),
        compiler_params=_params(("parallel", "arbitrary"), 16 * _nbytes((S, W), F32)),
    )(up3, dact, conv_w, conv_b)


def _final(x2, tgt, g, tr=512):
    T, D = x2.shape

    def body(x_ref, t_ref, g_ref, dx_ref, loss_ref, gg_ref):
        @pl.when(pl.program_id(0) == 0)
        def _():
            loss_ref[...] = jnp.zeros_like(loss_ref)
            gg_ref[...] = jnp.zeros_like(gg_ref)

        xv = x_ref[...]
        gv = g_ref[...]
        r = lax.rsqrt(jnp.mean(xv * xv, axis=-1, keepdims=True) + EPS)
        xn = xv * r
        err = xn * gv - t_ref[...]
        loss_ref[...] += 0.5 * jnp.sum(jnp.mean(err * err, axis=-1, keepdims=True), axis=0, keepdims=True)
        dy = err * (1.0 / D)
        gg_ref[...] += jnp.sum(dy * xn, axis=0, keepdims=True)
        dxn = dy * gv
        dx_ref[...] = r * (dxn - xn * jnp.mean(dxn * xn, axis=-1, keepdims=True))

    row = pl.BlockSpec((tr, D), lambda i: (i, 0))
    vec = pl.BlockSpec((1, D), lambda i: (0, 0))
    return pl.pallas_call(
        body, name="final_loss",
        out_shape=(jax.ShapeDtypeStruct((T, D), F32), jax.ShapeDtypeStruct((1, 128), F32), jax.ShapeDtypeStruct((1, D), F32)),
        grid=(T // tr,), in_specs=[row, row, vec],
        out_specs=(row, pl.BlockSpec((1, 128), lambda i: (0, 0)), vec),
        compiler_params=_params(("arbitrary",), 6 * _nbytes((tr, D), F32)),
    )(x2, tgt, g)


def _sum_slabs(parts, name, tr):
    rows, cols = parts[0].shape
    n = len(parts)

    def body(*refs):
        acc = refs[0][...]
        for r in refs[1:n]:
            acc = acc + r[...]
        refs[n][...] = acc

    blk = pl.BlockSpec((tr, cols), lambda i: (i, 0))
    return pl.pallas_call(
        body, name=name, out_shape=jax.ShapeDtypeStruct((rows, cols), F32), grid=(rows // tr,),
        in_specs=[blk] * n, out_specs=blk,
        compiler_params=_params(("parallel",), (n + 1) * _nbytes((tr, cols), F32)),
    )(*parts)


def _adamw(w, g, m, v, name):
    rows, cols = w.shape
    tr = rows
    for cand in (256, 128, 64, 32, 16, 8):
        if rows % cand == 0:
            tr = cand
            break
    c1 = 1.0 - ADAM_B1 ** ADAM_STEP
    c2 = 1.0 - ADAM_B2 ** ADAM_STEP

    def body(w_ref, g_ref, m_ref, v_ref, d_ref, nm_ref, nv_ref):
        gv = g_ref[...]
        nm = ADAM_B1 * m_ref[...] + (1.0 - ADAM_B1) * gv
        nv = ADAM_B2 * v_ref[...] + (1.0 - ADAM_B2) * (gv * gv)
        nm_ref[...] = nm
        nv_ref[...] = nv
        d_ref[...] = -ADAM_LR * ((nm / c1) / (jnp.sqrt(nv / c2) + ADAM_EPS) + ADAM_WD * w_ref[...])

    blk = pl.BlockSpec((tr, cols), lambda i: (i, 0))
    sds = jax.ShapeDtypeStruct((rows, cols), F32)
    return pl.pallas_call(
        body, name=name, out_shape=(sds, sds, sds), grid=(rows // tr,), in_specs=[blk] * 4, out_specs=(blk, blk, blk),
        compiler_params=_params(("parallel",), 7 * _nbytes((tr, cols), F32)),
    )(w, g, m, v)


_ANY = pl.BlockSpec(memory_space=pl.ANY)


def _place():
    x, y, c = lax.axis_index("x"), lax.axis_index("y"), lax.axis_index("c")
    chips = [(1 - x, y), (x, 1 - y), (1 - x, 1 - y)]
    return x, y, c, chips


def _half_rows(c, rows, align):
    half = rows // 2
    return pl.ds(pl.multiple_of(c * half, align), half), pl.ds(pl.multiple_of((1 - c) * half, align), half)


def _gather_weights(shards, conv_w):
    n = len(shards)

    def body(*refs):
        ins, cw_ref, outs, cwo_ref = refs[:n], refs[n], refs[n + 1:2 * n + 1], refs[2 * n + 1]
        send, recv, fsend, frecv, csend, crecv, loc = refs[2 * n + 2:]
        x, y, c, chips = _place()
        me = 2 * x + y
        local = [pltpu.make_async_copy(ins[w], outs[w].at[me], loc.at[w]) for w in range(n)]
        local.append(pltpu.make_async_copy(cw_ref, cwo_ref.at[me], loc.at[n]))
        for cp in local:
            cp.start()
        first, passed = [], []
        for w in range(n):
            mine, _ = _half_rows(c, shards[w].shape[0], 16)
            for j, (px, py) in enumerate(chips):
                first.append(pltpu.make_async_remote_copy(
                    src_ref=ins[w].at[mine, :], dst_ref=outs[w].at[me, mine, :], send_sem=send.at[3 * w + j],
                    recv_sem=recv.at[3 * w + j], device_id=(px, py, c), device_id_type=MESH))
        for j, (px, py) in enumerate(chips):
            first.append(pltpu.make_async_remote_copy(
                src_ref=cw_ref, dst_ref=cwo_ref.at[me], send_sem=csend.at[j], recv_sem=crecv.at[j],
                device_id=(px, py, c), device_id_type=MESH))
        for cp in first:
            cp.start()
        for w in range(n):
            mine, _ = _half_rows(c, shards[w].shape[0], 16)
            for j, (px, py) in enumerate(chips):
                landed = outs[w].at[2 * px + py, mine, :]
                pltpu.make_async_remote_copy(src_ref=landed, dst_ref=landed, send_sem=send.at[3 * w + j],
                                             recv_sem=recv.at[3 * w + j], device_id=(px, py, c),
                                             device_id_type=MESH).wait_recv()
                fw = pltpu.make_async_remote_copy(src_ref=landed, dst_ref=landed, send_sem=fsend.at[3 * w + j],
                                                  recv_sem=frecv.at[3 * w + j], device_id=(x, y, 1 - c),
                                                  device_id_type=MESH)
                fw.start()
                passed.append(fw)
        for w in range(n):
            _, theirs = _half_rows(c, shards[w].shape[0], 16)
            for j, (px, py) in enumerate(chips):
                other = outs[w].at[2 * px + py, theirs, :]
                pltpu.make_async_remote_copy(src_ref=other, dst_ref=other, send_sem=fsend.at[3 * w + j],
                                             recv_sem=frecv.at[3 * w + j], device_id=(x, y, 1 - c),
                                             device_id_type=MESH).wait_recv()
        for j, (px, py) in enumerate(chips):
            pltpu.make_async_remote_copy(src_ref=cw_ref, dst_ref=cwo_ref.at[2 * px + py], send_sem=csend.at[j],
                                         recv_sem=crecv.at[j], device_id=(px, py, c), device_id_type=MESH).wait_recv()
        for cp in first + passed:
            cp.wait_send()
        for cp in local:
            cp.wait()

    dma = lambda k: pltpu.SemaphoreType.DMA((k,))
    return pl.pallas_call(
        body, name="gather_weights",
        out_shape=tuple(jax.ShapeDtypeStruct((N_CHIPS,) + s.shape, s.dtype) for s in list(shards) + [conv_w]),
        in_specs=[_ANY] * (n + 1), out_specs=tuple([_ANY] * (n + 1)),
        scratch_shapes=[dma(3 * n), dma(3 * n), dma(3 * n), dma(3 * n), dma(3), dma(3), dma(n + 1)],
    )(*shards, conv_w)


def _swap_halves(gs):
    n = len(gs)

    def body(*refs):
        ins, outs, send, recv = refs[:n], refs[n:2 * n], refs[2 * n], refs[2 * n + 1]
        x, y, c, _ = _place()
        cps = []
        for w in range(n):
            _, theirs = _half_rows(c, gs[w].shape[1], 8)
            cps.append(pltpu.make_async_remote_copy(
                src_ref=ins[w].at[:, theirs, :], dst_ref=outs[w], send_sem=send.at[w], recv_sem=recv.at[w],
                device_id=(x, y, 1 - c), device_id_type=MESH))
        for cp in cps:
            cp.start()
        for cp in cps:
            cp.wait()

    return pl.pallas_call(
        body, name="grad_swap_halves",
        out_shape=tuple(jax.ShapeDtypeStruct((g.shape[0], g.shape[1] // 2, g.shape[2]), g.dtype) for g in gs),
        in_specs=[_ANY] * n, out_specs=tuple([_ANY] * n),
        scratch_shapes=[pltpu.SemaphoreType.DMA((n,)), pltpu.SemaphoreType.DMA((n,))],
    )(*gs)


GRAD_PAYLOAD = jnp.bfloat16


def _pair_sum(gs, gots):
    n = len(gs)
    core = lax.axis_index("c").astype(jnp.int32).reshape(1)

    def body(core_ref, *refs):
        del core_ref
        for w in range(n):
            refs[2 * n + w][...] = (refs[w][...] + refs[n + w][...]).astype(GRAD_PAYLOAD)

    in_specs, out_specs, out_shape, nbytes = [], [], [], 0
    for g in gs:
        q = g.shape[1] // 4
        in_specs.append(pl.BlockSpec((1, q, g.shape[2]), lambda s, r, core: (s, 2 * core[0] + r, 0)))
        nbytes += 3 * _nbytes((q, g.shape[2]), F32)
    for g in gs:
        q = g.shape[1] // 4
        in_specs.append(pl.BlockSpec((1, q, g.shape[2]), lambda s, r, core: (s, r, 0)))
        out_specs.append(pl.BlockSpec((1, q, g.shape[2]), lambda s, r, core: (s, r, 0)))
        out_shape.append(jax.ShapeDtypeStruct((g.shape[0], g.shape[1] // 2, g.shape[2]), GRAD_PAYLOAD))
    return pl.pallas_call(
        body, name="grad_pair_sum", out_shape=tuple(out_shape),
        grid_spec=pltpu.PrefetchScalarGridSpec(num_scalar_prefetch=1, grid=(N_CHIPS, 2), in_specs=in_specs,
                                               out_specs=tuple(out_specs)),
        compiler_params=_params(("parallel", "parallel"), nbytes),
    )(core, *gs, *gots)


def _scatter_to_chips(ps):
    n = len(ps)

    def body(*refs):
        ins, outs, send, recv, loc = refs[:n], refs[n:2 * n], refs[2 * n], refs[2 * n + 1], refs[2 * n + 2]
        x, y, c, chips = _place()
        me = 2 * x + y
        local = [pltpu.make_async_copy(ins[w].at[me], outs[w].at[me], loc.at[w]) for w in range(n)]
        for cp in local:
            cp.start()
        cps = []
        for w in range(n):
            for j, (px, py) in enumerate(chips):
                cps.append(pltpu.make_async_remote_copy(
                    src_ref=ins[w].at[2 * px + py], dst_ref=outs[w].at[me], send_sem=send.at[3 * w + j],
                    recv_sem=recv.at[3 * w + j], device_id=(px, py, c), device_id_type=MESH))
        for cp in cps:
            cp.start()
        for w in range(n):
            for j, (px, py) in enumerate(chips):
                slot = outs[w].at[2 * px + py]
                pltpu.make_async_remote_copy(src_ref=slot, dst_ref=slot, send_sem=send.at[3 * w + j],
                                             recv_sem=recv.at[3 * w + j], device_id=(px, py, c),
                                             device_id_type=MESH).wait_recv()
        for cp in cps:
            cp.wait_send()
        for cp in local:
            cp.wait()

    dma = lambda k: pltpu.SemaphoreType.DMA((k,))
    return pl.pallas_call(
        body, name="grad_scatter_chips", out_shape=tuple(jax.ShapeDtypeStruct(p.shape, p.dtype) for p in ps),
        in_specs=[_ANY] * n, out_specs=tuple([_ANY] * n), scratch_shapes=[dma(3 * n), dma(3 * n), dma(n)],
    )(*ps)


def _chip_sum(landed):
    n = len(landed)

    def body(*refs):
        for w in range(n):
            src = refs[w]
            refs[n + w][...] = ((src[0].astype(F32) + src[1].astype(F32)) + src[2].astype(F32)) + src[3].astype(F32)

    in_specs, out_specs, out_shape, nbytes = [], [], [], 0
    for a in landed:
        q = a.shape[1] // 2
        in_specs.append(pl.BlockSpec((N_CHIPS, q, a.shape[2]), lambda r: (0, r, 0)))
        out_specs.append(pl.BlockSpec((q, a.shape[2]), lambda r: (r, 0)))
        out_shape.append(jax.ShapeDtypeStruct(a.shape[1:], F32))
        nbytes += 4 * _nbytes((q, a.shape[2]), F32)
    return pl.pallas_call(
        body, name="grad_chip_sum", out_shape=tuple(out_shape), grid=(2,), in_specs=in_specs,
        out_specs=tuple(out_specs), compiler_params=_params(("parallel",), nbytes),
    )(*landed)


def _join_halves(ss):
    n = len(ss)

    def body(*refs):
        ins, outs, send, recv, loc = refs[:n], refs[n:2 * n], refs[2 * n], refs[2 * n + 1], refs[2 * n + 2]
        x, y, c, _ = _place()
        local, cps = [], []
        for w in range(n):
            mine, _ = _half_rows(c, 2 * ss[w].shape[0], 8)
            local.append(pltpu.make_async_copy(ins[w], outs[w].at[mine, :], loc.at[w]))
            cps.append(pltpu.make_async_remote_copy(
                src_ref=ins[w], dst_ref=outs[w].at[mine, :], send_sem=send.at[w], recv_sem=recv.at[w],
                device_id=(x, y, 1 - c), device_id_type=MESH))
        for cp in local + cps:
            cp.start()
        for w in range(n):
            _, theirs = _half_rows(c, 2 * ss[w].shape[0], 8)
            got = outs[w].at[theirs, :]
            pltpu.make_async_remote_copy(src_ref=got, dst_ref=got, send_sem=send.at[w], recv_sem=recv.at[w],
                                         device_id=(x, y, 1 - c), device_id_type=MESH).wait_recv()
        for cp in cps:
            cp.wait_send()
        for cp in local:
            cp.wait()

    dma = lambda k: pltpu.SemaphoreType.DMA((k,))
    return pl.pallas_call(
        body, name="grad_join_halves",
        out_shape=tuple(jax.ShapeDtypeStruct((2 * s.shape[0], s.shape[1]), s.dtype) for s in ss),
        in_specs=[_ANY] * n, out_specs=tuple([_ANY] * n), scratch_shapes=[dma(n), dma(n), dma(n)],
    )(*ss)


def _gather_all(p):
    rows, cols = p.shape

    def body(p_ref, out_ref, send, recv, loc):
        x, y, c, _ = _place()
        me = 4 * x + 2 * y + c
        own = pltpu.make_async_copy(p_ref, out_ref.at[me], loc)
        own.start()
        flips = [(fx, fy, fc) for fx in (0, 1) for fy in (0, 1) for fc in (0, 1)][1:]
        peers = [(x ^ fx, y ^ fy, c ^ fc) for fx, fy, fc in flips]
        cps = [pltpu.make_async_remote_copy(src_ref=p_ref, dst_ref=out_ref.at[me], send_sem=send.at[j], recv_sem=recv.at[j],
                                            device_id=peer, device_id_type=MESH) for j, peer in enumerate(peers)]
        for cp in cps:
            cp.start()
        for j, (px, py, pc) in enumerate(peers):
            slot = out_ref.at[4 * px + 2 * py + pc]
            pltpu.make_async_remote_copy(src_ref=slot, dst_ref=slot, send_sem=send.at[j], recv_sem=recv.at[j],
                                         device_id=(px, py, pc), device_id_type=MESH).wait_recv()
        for cp in cps:
            cp.wait_send()
        own.wait()

    dma7 = pltpu.SemaphoreType.DMA((7,))
    return pl.pallas_call(
        body, name="small_grads_gather", out_shape=jax.ShapeDtypeStruct((8, rows, cols), p.dtype),
        in_specs=[_ANY], out_specs=_ANY, scratch_shapes=[dma7, dma7, pltpu.SemaphoreType.DMA],
    )(p)


def _rot_cols(w):
    a, b = jnp.split(w, 2, axis=-1)
    return jnp.concatenate([-b, a], axis=-1)


def _rot_cols_t(g):
    a, b = jnp.split(g, 2, axis=-1)
    return jnp.concatenate([b, -a], axis=-1)


def _cols_from_chips(a):
    n, r, cs = a.shape
    return jnp.transpose(a, (1, 0, 2)).reshape(r, n * cs)


def _cols_to_chips(a):
    r, cc = a.shape
    return jnp.transpose(a.reshape(r, N_CHIPS, cc // N_CHIPS), (1, 0, 2))


def _conv_w_split(cw):
    return jnp.swapaxes(cw.reshape(3, 2, D_FF), 0, 1)


def _conv_w_join(g):
    return jnp.swapaxes(g, 0, 1).reshape(3, 2 * D_FF)


_SEG =(D_MODEL, 2 * D_MODEL, 2 * D_MODEL + Q_RANK, 2 * D_MODEL + Q_RANK + KV_RANK, 2 * D_MODEL + Q_RANK + KV_RANK + ROPE,
        3 * D_MODEL + Q_RANK + KV_RANK + ROPE)


def _w_in_to_pad(w):
    u, v, cq, ckv, kr, ga, gb = jnp.split(w, _SEG, axis=1)
    return jnp.concatenate([u, v, ga, gb, cq, ckv, kr, _rot_cols(kr)], axis=1)


def _w_in_from_pad(g):
    u, v, ga, gb, cq, ckv, kr, krr = jnp.split(
        g, (D_MODEL, 2 * D_MODEL, 3 * D_MODEL, 4 * D_MODEL, 4 * D_MODEL + Q_RANK, 4 * D_MODEL + Q_RANK + KV_RANK,
            4 * D_MODEL + Q_RANK + KV_RANK + ROPE), axis=1)
    return jnp.concatenate([u, v, cq, ckv, kr + _rot_cols_t(krr), ga, gb], axis=1)


def _w_uq_to_pad(w):
    t = w.reshape(Q_RANK, HEADS, QK_DIM)
    nope, rope = t[..., :NOPE], t[..., NOPE:]
    return jnp.concatenate([nope, rope, _rot_cols(rope)], axis=-1).reshape(Q_RANK, HEADS * HEAD_PAD)


def _w_uq_from_pad(g):
    t = g.reshape(Q_RANK, HEADS, HEAD_PAD)
    nope, rope, rot = t[..., :NOPE], t[..., NOPE:QK_DIM], t[..., QK_DIM:]
    return jnp.concatenate([nope, rope + _rot_cols_t(rot)], axis=-1).reshape(Q_RANK, HEADS * QK_DIM)


def _w_ukv_to_pad(w):
    t = w.reshape(KV_RANK, HEADS, 2, NOPE)
    return jnp.swapaxes(t, 1, 2).reshape(KV_RANK, 2 * HEADS * NOPE)


def _w_ukv_from_pad(g):
    t = g.reshape(KV_RANK, 2, HEADS, NOPE)
    return jnp.swapaxes(t, 1, 2).reshape(KV_RANK, 2 * HEADS * NOPE)


def _rope_tables(positions):
    inv_freq = 1.0 / (ROPE_THETA ** (jnp.arange(0, ROPE, 2, dtype=F32) / ROPE))
    ang = positions.astype(F32).reshape(-1, 1) * inv_freq
    cos, sin = jnp.cos(ang), jnp.sin(ang)
    zero = jnp.zeros((ang.shape[0], 64), F32)
    return jnp.concatenate([cos, cos, zero], axis=1), jnp.concatenate([sin, sin, zero], axis=1)


_BIG = ("w_in", "w_uq", "w_ukv", "w_out", "w_up", "w_down")
UP_SHARD = 2 * D_FF // N_CHIPS


def _local_step(x, positions, tgt, wts):
    B, S, D = x.shape
    T = B * S
    xf = x.reshape(T, D)
    cos_a, sin_a = _rope_tables(positions)
    bs_t = jnp.pad(wts["a_spatial_b"].T, ((0, 0), (0, 128 - A_GROUPS)))

    h = _rms_fwd(xf, wts["mix_norm"], "norm1_fwd")
    z = _mm(h, wts["w_in"], "nn", "in_proj", tm=512, tn=1536, tk=D)
    q, k, v, cqn, ckvn = _lat_fwd(z, wts["q_a_norm"], wts["kv_a_norm"], wts["w_q"], wts["w_kv"], cos_a, sin_a)
    yb, lse = _attn_fwd(q, k, v, B, S)
    merged = _mix_fwd(z, yb, wts["a_v_norm_g"], wts["a_v_norm_b"], wts["a_spatial_w"], bs_t)
    x1 = _mm(merged, wts["w_out"], "nn", "out_proj", tm=512, tn=D, tk=D, add=xf)
    h2 = _rms_fwd(x1, wts["ffn_norm"], "norm2_fwd")
    up_pre = _mm(h2, wts["w_up"], "nn", "up_proj", tm=512, tn=UP_SHARD, tk=D, dims=(T, 2 * D_FF, D),
                 b_spec=pl.BlockSpec((None, D, UP_SHARD), lambda i, j, k: (j, 0, 0)),
                 o_spec=pl.BlockSpec((None, 512, UP_SHARD), lambda i, j, k: (j // 2, i, j % 2)), out_shape=(2, T, D_FF))
    act = _gate_fwd(up_pre, wts["conv_w"], wts["conv_b"], B, S)
    x2 = _mm(act, wts["w_down"], "nn", "down_proj", tm=512, tn=D, tk=1408, add=x1)
    dx2, loss_row, g_final = _final(x2, tgt.reshape(T, D), wts["final_norm"])

    g = {"final_norm": g_final}
    dact = _mm(dx2, wts["w_down"], "nt", "down_proj_dx", tm=512, tn=1408, tk=D)
    g["w_down"] = _mm(act, dx2, "tn", "down_proj_dw", tm=1408, tn=D, tk=512)
    dup, g["conv_w"], g["conv_b"] = _gate_bwd(up_pre, dact, wts["conv_w"], wts["conv_b"], B, S)
    dh2 = _mm(dup, wts["w_up"], "nt", "up_proj_dx", tm=512, tn=D, tk=UP_SHARD, dims=(T, D, 2 * D_FF),
              a_spec=pl.BlockSpec((None, 512, UP_SHARD), lambda i, j, k: (k // 2, i, k % 2)),
              b_spec=pl.BlockSpec((None, D, UP_SHARD), lambda i, j, k: (k, 0, 0)))
    g["w_up"] = _mm(h2, dup, "tn", "up_proj_dw", tm=D, tn=UP_SHARD, tk=512, dims=(D, 2 * D_FF, T),
                    b_spec=pl.BlockSpec((None, 512, UP_SHARD), lambda i, j, k: (j // 2, k, j % 2)),
                    o_spec=pl.BlockSpec((None, D, UP_SHARD), lambda i, j, k: (j, 0, 0)), out_shape=(N_CHIPS, D, UP_SHARD))
    dx1, g["ffn_norm"] = _rms_bwd(x1, wts["ffn_norm"], dh2, dx2, "norm2_bwd")
    dm = _mm(dx1, wts["w_out"], "nt", "out_proj_dx", tm=512, tn=D, tk=D)
    g["w_out"] = _mm(merged, dx1, "tn", "out_proj_dw", tm=D, tn=D, tk=512)
    dz, dyb, dl, g["a_spatial_w"], gbs, g["a_v_norm_g"], g["a_v_norm_b"] = _mix_bwd(
        z, yb, dm, wts["a_v_norm_g"], wts["a_v_norm_b"], wts["a_spatial_w"], bs_t)
    g["a_spatial_b"] = gbs[:, :A_GROUPS].T
    delta = dl.reshape(HEADS * T // ATT_BLOCK, 1, ATT_BLOCK)
    dq, dk, dv = _attn_bwd(q, k, v, dyb, lse, delta, B, S)
    dz, dq_raw, dkv, g["q_a_norm"], g["kv_a_norm"] = _lat_bwd(
        dz, z, dq, dk, dv, wts["q_a_norm"], wts["kv_a_norm"], wts["w_q"], wts["w_kv"], cos_a, sin_a)
    g["w_q"] = _mm(cqn, dq_raw, "tn", "q_proj_dw", tm=Q_RANK, tn=HEADS * HEAD_PAD, tk=512)
    g["w_kv"] = _mm(ckvn, dkv, "tn", "kv_proj_dw", tm=KV_RANK, tn=2 * HEADS * NOPE, tk=512)
    dh = _mm(dz, wts["w_in"], "nt", "in_proj_dx", tm=512, tn=D, tk=1536)
    g["w_in"] = _mm(h, dz, "tn", "in_proj_dw", tm=D, tn=1536, tk=512)
    dx, g["mix_norm"] = _rms_bwd(xf, wts["mix_norm"], dh, dx1, "norm1_bwd")
    return loss_row[0, 0], dx.reshape(B, S, D), g


_SMALL = (("mix_norm", (1, D_MODEL)), ("a_v_norm_g", (1, D_MODEL)), ("a_v_norm_b", (1, D_MODEL)),
          ("a_spatial_w", (A_GROUPS * CHUNK, CHUNK)), ("a_spatial_b", (1, A_GROUPS * CHUNK)), ("q_a_norm", (1, Q_RANK)),
          ("kv_a_norm", (1, KV_RANK)), ("ffn_norm", (1, D_MODEL)), ("conv_b", (1, 2 * D_FF)), ("final_norm", (1, D_MODEL)),
          ("conv_w", (3, 2 * D_FF)))
_SMALL_ROWS = -(-sum(math.prod(s) for _, s in _SMALL) // (128 * 8)) * 8


def kernel(x, positions, mix_norm, w_in, a_v_norm_g, a_v_norm_b, a_spatial_w, a_spatial_b, q_a_norm, w_uq, kv_a_norm, w_ukv, w_out, ffn_norm, w_up, conv_w, conv_b, w_down, final_norm, loss_target, m_mix_norm, m_w_in, m_a_v_norm_g, m_a_v_norm_b, m_a_spatial_w, m_a_spatial_b, m_q_a_norm, m_w_uq, m_kv_a_norm, m_w_ukv, m_w_out, m_ffn_norm, m_w_up, m_conv_w, m_conv_b, m_w_down, m_final_norm, v_mix_norm, v_w_in, v_a_v_norm_g, v_a_v_norm_b, v_a_spatial_w, v_a_spatial_b, v_q_a_norm, v_w_uq, v_kv_a_norm, v_w_ukv, v_w_out, v_ffn_norm, v_w_up, v_conv_w, v_conv_b, v_w_down, v_final_norm):
    weights = dict(mix_norm=mix_norm, w_in=w_in, a_v_norm_g=a_v_norm_g, a_v_norm_b=a_v_norm_b, a_spatial_w=a_spatial_w,
                   a_spatial_b=a_spatial_b, q_a_norm=q_a_norm, w_uq=w_uq, kv_a_norm=kv_a_norm, w_ukv=w_ukv, w_out=w_out,
                   ffn_norm=ffn_norm, w_up=w_up, conv_w=conv_w, conv_b=conv_b, w_down=w_down, final_norm=final_norm)
    m_in = dict(mix_norm=m_mix_norm, w_in=m_w_in, a_v_norm_g=m_a_v_norm_g, a_v_norm_b=m_a_v_norm_b,
                a_spatial_w=m_a_spatial_w, a_spatial_b=m_a_spatial_b, q_a_norm=m_q_a_norm, w_uq=m_w_uq,
                kv_a_norm=m_kv_a_norm, w_ukv=m_w_ukv, w_out=m_w_out, ffn_norm=m_ffn_norm, w_up=m_w_up, conv_w=m_conv_w,
                conv_b=m_conv_b, w_down=m_w_down, final_norm=m_final_norm)
    v_in = dict(mix_norm=v_mix_norm, w_in=v_w_in, a_v_norm_g=v_a_v_norm_g, a_v_norm_b=v_a_v_norm_b,
                a_spatial_w=v_a_spatial_w, a_spatial_b=v_a_spatial_b, q_a_norm=v_q_a_norm, w_uq=v_w_uq,
                kv_a_norm=v_kv_a_norm, w_ukv=v_w_ukv, w_out=v_w_out, ffn_norm=v_ffn_norm, w_up=v_w_up, conv_w=v_conv_w,
                conv_b=v_conv_b, w_down=v_w_down, final_norm=v_final_norm)
    names = list(weights)
    chip = 2 * lax.axis_index("x") + lax.axis_index("y")

    w_in_sh, w_uq_sh, w_ukv_sh, w_out_sh, w_up_sh, w_down_sh, cw_all = _gather_weights(
        [weights[n][0].astype(MXU_DTYPE) for n in _BIG], conv_w[0])
    wts = dict(
        mix_norm=mix_norm, a_v_norm_g=a_v_norm_g, a_v_norm_b=a_v_norm_b, a_spatial_w=a_spatial_w[0],
        a_spatial_b=a_spatial_b[0], q_a_norm=q_a_norm, kv_a_norm=kv_a_norm, ffn_norm=ffn_norm,
        final_norm=final_norm.reshape(1, D_MODEL),
        w_in=_w_in_to_pad(_cols_from_chips(w_in_sh)), w_q=_w_uq_to_pad(_cols_from_chips(w_uq_sh)),
        w_kv=_w_ukv_to_pad(_cols_from_chips(w_ukv_sh)), w_out=w_out_sh.reshape(D_MODEL, D_MODEL), w_up=w_up_sh,
        w_down=w_down_sh.reshape(D_FF, D_MODEL), conv_w=_conv_w_split(_cols_from_chips(cw_all)),
        conv_b=conv_b.reshape(2, 1, D_FF))

    loss_part, grad_x, g = _local_step(x, positions, loss_target, wts)
    loss = lax.psum(loss_part, ("x", "y", "c"))

    slabs = [_cols_to_chips(_w_in_from_pad(g["w_in"])), _cols_to_chips(_w_uq_from_pad(g["w_q"])),
             _cols_to_chips(_w_ukv_from_pad(g["w_kv"])), g["w_out"].reshape(N_CHIPS, D_MODEL // N_CHIPS, D_MODEL),
             g["w_up"], g["w_down"].reshape(N_CHIPS, D_FF // N_CHIPS, D_MODEL)]
    landed = _scatter_to_chips(_pair_sum(slabs, _swap_halves(slabs)))
    g_big = dict(zip(_BIG, _join_halves(_chip_sum(landed))))

    g_small_parts = dict(g)
    g_small_parts["conv_w"] = _conv_w_join(g["conv_w"])
    g_small_parts["conv_b"] = g["conv_b"].reshape(1, 2 * D_FF)
    flat = jnp.concatenate([g_small_parts[n].reshape(-1) for n, _ in _SMALL])
    flat = jnp.pad(flat, (0, _SMALL_ROWS * 128 - flat.shape[0])).reshape(_SMALL_ROWS, 128)
    everyone = _gather_all(flat)
    total = _sum_slabs([everyone[j] for j in range(8)], "small_grads_sum", tr=_SMALL_ROWS).reshape(-1)
    g_small, o = {}, 0
    for n, shp in _SMALL:
        g_small[n] = total[o:o + math.prod(shp)].reshape(shp)
        o += math.prod(shp)
    g_small["conv_w"] = lax.dynamic_slice_in_dim(g_small["conv_w"], chip * 1408, 1408, axis=1)

    grads, deltas, new_m, new_v = {}, {}, {}, {}
    for n in names:
        w = weights[n]
        g2 = g_big[n] if n in g_big else g_small[n]
        shape2 = g2.shape
        d, nm, nv = _adamw(w.reshape(shape2), g2, m_in[n].reshape(shape2), v_in[n].reshape(shape2), "adamw_" + n)
        grads[n], deltas[n], new_m[n], new_v[n] = (t.reshape(w.shape) for t in (g2, d, nm, nv))
    return (loss, grad_x, *[grads[n] for n in names], *[deltas[n] for n in names], *[new_m[n] for n in names],
            *[new_v[n] for n in names])
```

```python
import functools
import math

import jax
import jax.numpy as jnp
from jax import lax
from jax.experimental import pallas as pl
from jax.experimental.pallas import tpu as pltpu

F32 = jnp.float32
MXU_DTYPE = jnp.bfloat16
MESH = pl.DeviceIdType.MESH

D_MODEL = 1024
EPS = 1e-6
A_GROUPS = 8
CHUNK = 128
HEADS = 8
NOPE = 128
ROPE = 64
QK_DIM = NOPE + ROPE
HEAD_PAD = 256
Q_RANK = 256
KV_RANK = 128
ROPE_THETA = 10000.0
D_FF = 2816
FF_TILE = 256
N_FF_TILES = D_FF // FF_TILE
LAT = 512
IN_PAD = 4 * D_MODEL + LAT
N_CHIPS = 4
ADAM_LR, ADAM_B1, ADAM_B2, ADAM_EPS, ADAM_WD, ADAM_STEP = 0.001, 0.9, 0.999, 1e-08, 0.01, 10

VMEM_CAP_V7X = 64 * 1024 * 1024
NEG = -1e30


def _params(sem, nbytes):
    limit = int(min(VMEM_CAP_V7X - (8 << 20), max(32 << 20, 3 * nbytes)))
    return pltpu.CompilerParams(dimension_semantics=sem, vmem_limit_bytes=limit)


def _nbytes(shape, dtype):
    return math.prod(shape) * jnp.dtype(dtype).itemsize


_DIMS = {"nn": (((1,), (0,)), ((), ())), "nt": (((1,), (1,)), ((), ())), "tn": (((0,), (0,)), ((), ()))}


def _mm(a, b, mode, name, *, tm, tn, tk, out_dtype=F32, add=None, dims=None, a_spec=None, b_spec=None,
        o_spec=None, out_shape=None):
    if dims is None:
        if mode == "nn":
            (M, K), (_, N) = a.shape, b.shape
        elif mode == "nt":
            (M, K), (N, _) = a.shape, b.shape
        else:
            (K, M), (_, N) = a.shape, b.shape
    else:
        M, N, K = dims
    a_blk = (tk, tm) if mode == "tn" else (tm, tk)
    b_blk = (tn, tk) if mode == "nt" else (tk, tn)
    if a_spec is None:
        a_spec = pl.BlockSpec(a_blk, (lambda i, j, k: (k, i)) if mode == "tn" else (lambda i, j, k: (i, k)))
    if b_spec is None:
        b_spec = pl.BlockSpec(b_blk, (lambda i, j, k: (j, k)) if mode == "nt" else (lambda i, j, k: (k, j)))
    if o_spec is None:
        o_spec = pl.BlockSpec((tm, tn), lambda i, j, k: (i, j))
    if out_shape is None:
        out_shape = (M, N)
    assert M % tm == 0 and N % tn == 0 and K % tk == 0, (name, M, N, K, tm, tn, tk)
    nk = K // tk
    contract = _DIMS[mode]
    has_add = add is not None

    def body(*refs):
        if has_add:
            a_ref, b_ref, add_ref, o_ref, acc = refs
        else:
            a_ref, b_ref, o_ref, acc = refs
        k = pl.program_id(2)

        @pl.when(k == 0)
        def _():
            acc[...] = jnp.zeros_like(acc)

        acc[...] += lax.dot_general(a_ref[...].astype(MXU_DTYPE), b_ref[...].astype(MXU_DTYPE), contract,
                                    preferred_element_type=F32)

        @pl.when(k == nk - 1)
        def _():
            r = acc[...]
            if has_add:
                r = r + add_ref[...]
            o_ref[...] = r.astype(out_dtype)

    in_specs = [a_spec, b_spec]
    args = [a, b]
    nbytes = _nbytes(a_blk, a.dtype) + _nbytes(b_blk, b.dtype) + 3 * _nbytes((tm, tn), F32)
    if has_add:
        in_specs.append(pl.BlockSpec((tm, tn), lambda i, j, k: (i, j)))
        args.append(add)
        nbytes += _nbytes((tm, tn), F32)
    return pl.pallas_call(
        body, name=name, out_shape=jax.ShapeDtypeStruct(out_shape, out_dtype),
        grid=(M // tm, N // tn, nk), in_specs=in_specs, out_specs=o_spec,
        scratch_shapes=[pltpu.VMEM((tm, tn), F32)],
        compiler_params=_params(("parallel", "parallel", "arbitrary"), nbytes),
    )(*args)


_GELU_C = math.sqrt(2.0 / math.pi)
_GELU_A = 0.044715


def _sigmoid(x):
    return 1.0 / (1.0 + jnp.exp(-x))


def _gelu(x):
    t = jnp.tanh(_GELU_C * (x + _GELU_A * (x * x * x)))
    return x * (0.5 * (1.0 + t))


def _gelu_and_grad(x):
    x2 = x * x
    t = jnp.tanh(_GELU_C * (x + _GELU_A * (x2 * x)))
    cdf = 0.5 * (1.0 + t)
    grad = cdf + 0.5 * x * (1.0 - t * t) * (_GELU_C * (1.0 + 3.0 * _GELU_A * x2))
    return x * cdf, grad


def _rope_mix(g, cos_a, sin_a):
    return g * cos_a + pltpu.roll(g, 64, 1) * sin_a


def _rope_mix_bwd(d, cos_a, sin_a):
    return d * cos_a + pltpu.roll(d * sin_a, 64, 1)


def _rms_fwd(x, g, name, tr=512):
    T, D = x.shape

    def body(x_ref, g_ref, h_ref):
        xv = x_ref[...]
        r = lax.rsqrt(jnp.mean(xv * xv, axis=-1, keepdims=True) + EPS)
        h_ref[...] = ((xv * r) * g_ref[...]).astype(h_ref.dtype)

    return pl.pallas_call(
        body, name=name, out_shape=jax.ShapeDtypeStruct((T, D), MXU_DTYPE), grid=(T // tr,),
        in_specs=[pl.BlockSpec((tr, D), lambda i: (i, 0)), pl.BlockSpec((1, D), lambda i: (0, 0))],
        out_specs=pl.BlockSpec((tr, D), lambda i: (i, 0)),
        compiler_params=_params(("parallel",), 3 * _nbytes((tr, D), F32)),
    )(x, g)


def _rms_bwd(x, g, dh, dres, name, tr=512):
    T, D = x.shape

    def body(x_ref, g_ref, dh_ref, dres_ref, dx_ref, gg_ref):
        @pl.when(pl.program_id(0) == 0)
        def _():
            gg_ref[...] = jnp.zeros_like(gg_ref)

        xv = x_ref[...]
        r = lax.rsqrt(jnp.mean(xv * xv, axis=-1, keepdims=True) + EPS)
        xn = xv * r
        dhv = dh_ref[...]
        dxn = dhv * g_ref[...]
        dx_ref[...] = dres_ref[...] + r * (dxn - xn * jnp.mean(dxn * xn, axis=-1, keepdims=True))
        gg_ref[...] += jnp.sum(dhv * xn, axis=0, keepdims=True)

    row = pl.BlockSpec((tr, D), lambda i: (i, 0))
    vec = pl.BlockSpec((1, D), lambda i: (0, 0))
    return pl.pallas_call(
        body, name=name,
        out_shape=(jax.ShapeDtypeStruct((T, D), F32), jax.ShapeDtypeStruct((1, D), F32)),
        grid=(T // tr,), in_specs=[row, vec, row, row], out_specs=(row, vec),
        compiler_params=_params(("arbitrary",), 6 * _nbytes((tr, D), F32)),
    )(x, g, dh, dres)


def _lat_fwd(z, gq, gkv, wq, wkv, cos_a, sin_a, tr=256):
    T = z.shape[0]
    lat_blk = (4 * D_MODEL) // LAT

    def body(z_ref, gq_ref, gkv_ref, wq_ref, wkv_ref, cos_ref, sin_ref, q_ref, k_ref, v_ref, cqn_ref, ckvn_ref):
        zl = z_ref[...]
        cos_v, sin_v = cos_ref[...], sin_ref[...]
        cq = zl[:, :Q_RANK]
        ckv = zl[:, Q_RANK:Q_RANK + KV_RANK]
        krb = zl[:, Q_RANK + KV_RANK:]
        cqn = ((cq * lax.rsqrt(jnp.mean(cq * cq, axis=-1, keepdims=True) + EPS)) * gq_ref[...]).astype(MXU_DTYPE)
        ckvn = ((ckv * lax.rsqrt(jnp.mean(ckv * ckv, axis=-1, keepdims=True) + EPS)) * gkv_ref[...]).astype(MXU_DTYPE)
        cqn_ref[...] = cqn
        ckvn_ref[...] = ckvn
        krr = _rope_mix(krb, cos_v, sin_v).astype(MXU_DTYPE)
        q = jnp.dot(cqn, wq_ref[...], preferred_element_type=F32)
        kv = jnp.dot(ckvn, wkv_ref[...], preferred_element_type=F32)
        for h in range(HEADS):
            o = h * HEAD_PAD
            q_ref[:, o:o + NOPE] = q[:, o:o + NOPE].astype(MXU_DTYPE)
            q_ref[:, o + NOPE:o + HEAD_PAD] = _rope_mix(q[:, o + NOPE:o + HEAD_PAD], cos_v, sin_v).astype(MXU_DTYPE)
            k_ref[:, o:o + NOPE] = kv[:, h * NOPE:(h + 1) * NOPE].astype(MXU_DTYPE)
            k_ref[:, o + NOPE:o + HEAD_PAD] = krr
        v_ref[...] = kv[:, HEADS * NOPE:].astype(MXU_DTYPE)

    def row(w):
        return pl.BlockSpec((tr, w), lambda i: (i, 0))

    def full(a):
        return pl.BlockSpec(a.shape, lambda i: (0, 0))

    return pl.pallas_call(
        body, name="lat_fwd",
        out_shape=(jax.ShapeDtypeStruct((T, HEADS * HEAD_PAD), MXU_DTYPE), jax.ShapeDtypeStruct((T, HEADS * HEAD_PAD), MXU_DTYPE),
                   jax.ShapeDtypeStruct((T, HEADS * NOPE), MXU_DTYPE), jax.ShapeDtypeStruct((T, Q_RANK), MXU_DTYPE),
                   jax.ShapeDtypeStruct((T, KV_RANK), MXU_DTYPE)),
        grid=(T // tr,),
        in_specs=[pl.BlockSpec((tr, LAT), lambda i: (i, lat_blk)), full(gq), full(gkv), full(wq), full(wkv), row(128), row(128)],
        out_specs=(row(HEADS * HEAD_PAD), row(HEADS * HEAD_PAD), row(HEADS * NOPE), row(Q_RANK), row(KV_RANK)),
        compiler_params=_params(("parallel",), 8 * _nbytes((tr, HEADS * HEAD_PAD), F32)),
    )(z, gq, gkv, wq, wkv, cos_a, sin_a)


ATT_BLOCK = 256
_SCALE = QK_DIM ** -0.5


def _causal_mask(n):
    return lax.broadcasted_iota(jnp.int32, (n, n), 1) <= lax.broadcasted_iota(jnp.int32, (n, n), 0)


def _causal_mask_t(n):
    return lax.broadcasted_iota(jnp.int32, (n, n), 0) <= lax.broadcasted_iota(jnp.int32, (n, n), 1)


ATT_HEADS = 2


def _attn_fwd(q, k, v, B, S):
    tq = ATT_BLOCK
    nq = S // tq
    T = B * S
    hp, groups = ATT_HEADS, HEADS // ATT_HEADS

    def body(q_ref, k_ref, v_ref, o_ref, *lse_refs):
        qi = pl.program_id(2)
        qs = [q_ref[:, t * HEAD_PAD:(t + 1) * HEAD_PAD] for t in range(hp)]

        def step(j, carry, masked):
            rows = pl.ds(pl.multiple_of(j * tq, tq), tq)
            out = []
            for t in range(hp):
                m, l, acc = carry[t]
                st = lax.dot_general(k_ref[rows, t * HEAD_PAD:(t + 1) * HEAD_PAD], qs[t], _DIMS["nt"],
                                     preferred_element_type=F32) * _SCALE
                if masked:
                    st = jnp.where(_causal_mask_t(tq), st, NEG)
                m_new = jnp.maximum(m, jnp.max(st, axis=0, keepdims=True))
                alpha = jnp.exp(m - m_new)
                p = jnp.exp(st - m_new)
                l = alpha * l + jnp.sum(p, axis=0, keepdims=True)
                acc = alpha * acc + lax.dot_general(v_ref[rows, t * NOPE:(t + 1) * NOPE], p.astype(MXU_DTYPE),
                                                    _DIMS["tn"], preferred_element_type=F32)
                out.append((m_new, l, acc))
            return tuple(out)

        init = tuple((jnp.full((1, tq), NEG, F32), jnp.zeros((1, tq), F32), jnp.zeros((NOPE, tq), F32))
                     for _ in range(hp))
        carry = lax.fori_loop(0, qi, lambda j, c: step(j, c, False), init)
        carry = step(qi, carry, True)
        for t in range(hp):
            m, l, acc = carry[t]
            o_ref[:, t * NOPE:(t + 1) * NOPE] = (acc / l).T
            lse_refs[t][0] = m + jnp.log(l)

    lse_sds = jax.ShapeDtypeStruct((groups * B * nq, 1, tq), F32)
    lse_spec = pl.BlockSpec((1, 1, tq), lambda b, h, i: ((h * B + b) * nq + i, 0, 0))
    return pl.pallas_call(
        body, name="attn_fwd",
        out_shape=(jax.ShapeDtypeStruct((T, HEADS * NOPE), F32),) + (lse_sds,) * hp,
        grid=(B, groups, nq),
        in_specs=[pl.BlockSpec((tq, hp * HEAD_PAD), lambda b, h, i: (b * nq + i, h)),
                  pl.BlockSpec((S, hp * HEAD_PAD), lambda b, h, i: (b, h)),
                  pl.BlockSpec((S, hp * NOPE), lambda b, h, i: (b, h))],
        out_specs=(pl.BlockSpec((tq, hp * NOPE), lambda b, h, i: (b * nq + i, h)),) + (lse_spec,) * hp,
        compiler_params=_params(("parallel", "parallel", "arbitrary"), 4 * hp * _nbytes((S, HEAD_PAD), MXU_DTYPE)),
    )(q, k, v)


def _attn_bwd(q, k, v, do, lses, delta, B, S):
    tq = ATT_BLOCK
    nq = S // tq
    T = B * S
    hp, groups = ATT_HEADS, HEADS // ATT_HEADS

    def body(q_ref, k_ref, v_ref, do_ref, *refs):
        lse_refs, dl_refs = refs[:hp], refs[hp:2 * hp]
        dq_ref, dk_ref, dv_ref = refs[2 * hp:]
        kj = pl.program_id(2)

        @pl.when(kj == 0)
        def _():
            dq_ref[...] = jnp.zeros_like(dq_ref)

        def step(i, carry, masked):
            rows = pl.ds(pl.multiple_of(i * tq, tq), tq)
            out = []
            for t in range(hp):
                dk, dv = carry[t]
                qk_cols = slice(t * HEAD_PAD, (t + 1) * HEAD_PAD)
                v_cols = slice(t * NOPE, (t + 1) * NOPE)
                kv_, vv = k_ref[:, qk_cols], v_ref[:, v_cols]
                qv, dov = q_ref[rows, qk_cols], do_ref[rows, v_cols]
                st = lax.dot_general(kv_, qv, _DIMS["nt"], preferred_element_type=F32) * _SCALE
                p = jnp.exp(st - lse_refs[t][i])
                if masked:
                    p = jnp.where(_causal_mask_t(tq), p, 0.0)
                dv = dv + jnp.dot(p.astype(MXU_DTYPE), dov, preferred_element_type=F32)
                dpt = lax.dot_general(vv, dov, _DIMS["nt"], preferred_element_type=F32)
                ds = (p * (dpt - dl_refs[t][i]) * _SCALE).astype(MXU_DTYPE)
                dk = dk + jnp.dot(ds, qv, preferred_element_type=F32)
                dq_ref[rows, qk_cols] += lax.dot_general(ds, kv_, _DIMS["tn"], preferred_element_type=F32)
                out.append((dk, dv))
            return tuple(out)

        init = tuple((jnp.zeros((tq, HEAD_PAD), F32), jnp.zeros((tq, NOPE), F32)) for _ in range(hp))
        carry = step(kj, init, True)
        carry = lax.fori_loop(kj + 1, nq, lambda i, c: step(i, c, False), carry)
        for t in range(hp):
            dk_ref[:, t * HEAD_PAD:(t + 1) * HEAD_PAD] = carry[t][0]
            dv_ref[:, t * NOPE:(t + 1) * NOPE] = carry[t][1].astype(dv_ref.dtype)

    seq = lambda w: pl.BlockSpec((S, w), lambda b, h, j: (b, h))
    blk = lambda w: pl.BlockSpec((tq, w), lambda b, h, j: (b * nq + j, h))
    lse_spec = pl.BlockSpec((nq, 1, tq), lambda b, h, j: (h * B + b, 0, 0))
    dl_specs = [pl.BlockSpec((nq, 1, tq), lambda b, h, j, t=t: ((h * hp + t) * B + b, 0, 0)) for t in range(hp)]
    return pl.pallas_call(
        body, name="attn_bwd",
        out_shape=(jax.ShapeDtypeStruct((T, HEADS * HEAD_PAD), F32), jax.ShapeDtypeStruct((T, HEADS * HEAD_PAD), F32),
                   jax.ShapeDtypeStruct((T, HEADS * NOPE), MXU_DTYPE)),
        grid=(B, groups, nq),
        in_specs=[seq(hp * HEAD_PAD), blk(hp * HEAD_PAD), blk(hp * NOPE), seq(hp * NOPE)] + [lse_spec] * hp + dl_specs,
        out_specs=(seq(hp * HEAD_PAD), blk(hp * HEAD_PAD), blk(hp * NOPE)),
        compiler_params=_params(("parallel", "parallel", "arbitrary"), 8 * hp * _nbytes((S, HEAD_PAD), F32)),
    )(q, k, v, do, *lses, *([delta] * hp))


MIX_ROWS = 256


def _tril_weights(ws_ref, g):
    return jnp.where(_causal_mask(CHUNK), ws_ref[g], 0.0).astype(MXU_DTYPE)


def _layer_norm_stats(va):
    mu = jnp.mean(va, axis=-1, keepdims=True)
    xc = va - mu
    rs = lax.rsqrt(jnp.mean(xc * xc, axis=-1, keepdims=True) + EPS)
    return xc * rs


def _mix_specs(tr):
    zcol = lambda c: pl.BlockSpec((tr, D_MODEL), lambda i, c=c: (i, c))
    row = pl.BlockSpec((tr, D_MODEL), lambda i: (i, 0))
    vec = pl.BlockSpec((1, D_MODEL), lambda i: (0, 0))
    ws = pl.BlockSpec((A_GROUPS, CHUNK, CHUNK), lambda i: (0, 0, 0))
    bs = pl.BlockSpec((CHUNK, 128), lambda i: (0, 0))
    return zcol, row, vec, ws, bs


def _mix_fwd(z, yb, ln_g, ln_b, ws, bs_t):
    T = z.shape[0]
    tr = MIX_ROWS
    zcol, row, vec, ws_spec, bs_spec = _mix_specs(tr)

    def body(zu_ref, zv_ref, zga_ref, zgb_ref, yb_ref, g_ref, b_ref, ws_ref, bs_ref, out_ref, vn_s):
        vhat = _layer_norm_stats(_gelu(zv_ref[...]))
        vn_s[...] = (vhat * g_ref[...] + b_ref[...]).astype(MXU_DTYPE)
        for g in range(A_GROUPS):
            w = _tril_weights(ws_ref, g)
            bias = bs_ref[:, g:g + 1]
            cols = slice(g * CHUNK, (g + 1) * CHUNK)
            for c in range(tr // CHUNK):
                rows = slice(c * CHUNK, (c + 1) * CHUNK)
                mixed = jnp.dot(w, vn_s[rows, cols], preferred_element_type=F32) + bias
                ya = _gelu(zu_ref[rows, cols]) * mixed
                merged = _sigmoid(zga_ref[rows, cols]) * ya + _sigmoid(zgb_ref[rows, cols]) * yb_ref[rows, cols]
                out_ref[rows, cols] = merged.astype(MXU_DTYPE)

    return pl.pallas_call(
        body, name="mix_fwd", out_shape=jax.ShapeDtypeStruct((T, D_MODEL), MXU_DTYPE), grid=(T // tr,),
        in_specs=[zcol(0), zcol(1), zcol(2), zcol(3), row, vec, vec, ws_spec, bs_spec], out_specs=row,
        scratch_shapes=[pltpu.VMEM((tr, D_MODEL), MXU_DTYPE)],
        compiler_params=_params(("parallel",), 8 * _nbytes((tr, D_MODEL), F32)),
    )(z, z, z, z, yb, ln_g, ln_b, ws, bs_t)


def _mix_bwd(z, yb, dm, ln_g, ln_b, ws, bs_t):
    T = z.shape[0]
    tr = MIX_ROWS
    zcol, row, vec, ws_spec, bs_spec = _mix_specs(tr)

    def body(zu_ref, zv_ref, zga_ref, zgb_ref, yb_ref, dm_ref, g_ref, b_ref, ws_ref, bs_ref,
             dz_ref, dyb_ref, dl_ref, gws_ref, gbs_ref, glg_ref, glb_ref, vn_s, dvn_s):
        @pl.when(pl.program_id(0) == 0)
        def _():
            gws_ref[...] = jnp.zeros_like(gws_ref)
            gbs_ref[...] = jnp.zeros_like(gbs_ref)
            glg_ref[...] = jnp.zeros_like(glg_ref)
            glb_ref[...] = jnp.zeros_like(glb_ref)

        lane = lax.broadcasted_iota(jnp.int32, (CHUNK, 128), 1)
        va, dgelu_v = _gelu_and_grad(zv_ref[...])
        mu = jnp.mean(va, axis=-1, keepdims=True)
        xc = va - mu
        rs = lax.rsqrt(jnp.mean(xc * xc, axis=-1, keepdims=True) + EPS)
        vhat = xc * rs
        vn_s[...] = (vhat * g_ref[...] + b_ref[...]).astype(MXU_DTYPE)
        gbs_acc = jnp.zeros((CHUNK, 128), F32)
        for g in range(A_GROUPS):
            w = _tril_weights(ws_ref, g)
            bias = bs_ref[:, g:g + 1]
            cols = slice(g * CHUNK, (g + 1) * CHUNK)
            gw_acc = jnp.zeros((CHUNK, CHUNK), F32)
            for c in range(tr // CHUNK):
                rows = slice(c * CHUNK, (c + 1) * CHUNK)
                vn = vn_s[rows, cols]
                mixed = jnp.dot(w, vn, preferred_element_type=F32) + bias
                ua, dgelu_u = _gelu_and_grad(zu_ref[rows, cols])
                dmv = dm_ref[rows, cols]
                sa = _sigmoid(zga_ref[rows, cols])
                dya = dmv * sa
                dz_ref[rows, 2 * D_MODEL + g * CHUNK:2 * D_MODEL + (g + 1) * CHUNK] = (
                    dmv * (ua * mixed) * (sa * (1.0 - sa))).astype(dz_ref.dtype)
                dz_ref[rows, cols] = (dya * mixed * dgelu_u).astype(dz_ref.dtype)
                dmix = dya * ua
                gbs_acc = gbs_acc + jnp.where(lane == g, jnp.sum(dmix, axis=-1, keepdims=True), 0.0)
                dmix_b = dmix.astype(MXU_DTYPE)
                gw_acc = gw_acc + lax.dot_general(dmix_b, vn, _DIMS["nt"], preferred_element_type=F32)
                dvn_s[rows, cols] = lax.dot_general(w, dmix_b, _DIMS["tn"], preferred_element_type=F32)
            gws_ref[g] += jnp.where(_causal_mask(CHUNK), gw_acc, 0.0)
        gbs_ref[...] += gbs_acc

        dvn = dvn_s[...]
        glg_ref[...] += jnp.sum(dvn * vhat, axis=0, keepdims=True)
        glb_ref[...] += jnp.sum(dvn, axis=0, keepdims=True)
        dvh = dvn * g_ref[...]
        dva = rs * (dvh - jnp.mean(dvh, axis=-1, keepdims=True) - vhat * jnp.mean(dvh * vhat, axis=-1, keepdims=True))
        dz_ref[:, D_MODEL:2 * D_MODEL] = (dva * dgelu_v).astype(dz_ref.dtype)

        dmv = dm_ref[...]
        ybv = yb_ref[...]
        sb = _sigmoid(zgb_ref[...])
        dyb = dmv * sb
        dyb_ref[...] = dyb.astype(dyb_ref.dtype)
        dz_ref[:, 3 * D_MODEL:4 * D_MODEL] = (dmv * ybv * (sb * (1.0 - sb))).astype(dz_ref.dtype)
        dz_ref[:, 4 * D_MODEL:] = jnp.zeros((tr, LAT), dz_ref.dtype)
        prod = dyb * ybv
        sel = (lax.broadcasted_iota(jnp.int32, (HEADS, D_MODEL), 1) // NOPE
               == lax.broadcasted_iota(jnp.int32, (HEADS, D_MODEL), 0)).astype(jnp.bfloat16)
        hi = prod.astype(jnp.bfloat16)
        rest = prod - hi.astype(F32)
        mid = rest.astype(jnp.bfloat16)
        lo = (rest - mid.astype(F32)).astype(jnp.bfloat16)
        dl_ref[...] = (lax.dot_general(sel, hi, _DIMS["nt"], preferred_element_type=F32)
                       + lax.dot_general(sel, mid, _DIMS["nt"], preferred_element_type=F32)
                       + lax.dot_general(sel, lo, _DIMS["nt"], preferred_element_type=F32))

    return pl.pallas_call(
        body, name="mix_bwd",
        out_shape=(jax.ShapeDtypeStruct((T, IN_PAD), MXU_DTYPE), jax.ShapeDtypeStruct((T, D_MODEL), MXU_DTYPE),
                   jax.ShapeDtypeStruct((HEADS, T), F32), jax.ShapeDtypeStruct((A_GROUPS, CHUNK, CHUNK), F32),
                   jax.ShapeDtypeStruct((CHUNK, 128), F32), jax.ShapeDtypeStruct((1, D_MODEL), F32),
                   jax.ShapeDtypeStruct((1, D_MODEL), F32)),
        grid=(T // tr,),
        in_specs=[zcol(0), zcol(1), zcol(2), zcol(3), row, row, vec, vec, ws_spec, bs_spec],
        out_specs=(pl.BlockSpec((tr, IN_PAD), lambda i: (i, 0)), row, pl.BlockSpec((HEADS, tr), lambda i: (0, i)),
                   ws_spec, bs_spec, vec, vec),
        scratch_shapes=[pltpu.VMEM((tr, D_MODEL), MXU_DTYPE), pltpu.VMEM((tr, D_MODEL), F32)],
        compiler_params=_params(("arbitrary",), 12 * _nbytes((tr, D_MODEL), F32)),
    )(z, z, z, z, yb, dm, ln_g, ln_b, ws, bs_t)


def _lat_bwd(dz, z, dq, dk, dv, gq, gkv, wq, wkv, cos_a, sin_a, tr=256):
    T = z.shape[0]
    lat_blk = (4 * D_MODEL) // LAT

    def body(dz_in, z_ref, dq_ref, dk_ref, dv_ref, gq_ref, gkv_ref, wq_ref, wkv_ref, cos_ref, sin_ref,
             dz_ref, dqr_ref, dkv_ref, ggq_ref, ggkv_ref):
        del dz_in

        @pl.when(pl.program_id(0) == 0)
        def _():
            ggq_ref[...] = jnp.zeros_like(ggq_ref)
            ggkv_ref[...] = jnp.zeros_like(ggkv_ref)

        cos_v, sin_v = cos_ref[...], sin_ref[...]
        dkr = jnp.zeros((tr, 128), F32)
        for h in range(HEADS):
            o = h * HEAD_PAD
            dqr_ref[:, o:o + NOPE] = dq_ref[:, o:o + NOPE].astype(MXU_DTYPE)
            dqr_ref[:, o + NOPE:o + HEAD_PAD] = _rope_mix_bwd(dq_ref[:, o + NOPE:o + HEAD_PAD], cos_v, sin_v).astype(MXU_DTYPE)
            dkv_ref[:, h * NOPE:(h + 1) * NOPE] = dk_ref[:, o:o + NOPE].astype(MXU_DTYPE)
            dkr = dkr + _rope_mix_bwd(dk_ref[:, o + NOPE:o + HEAD_PAD], cos_v, sin_v)
        dkv_ref[:, HEADS * NOPE:] = dv_ref[...]
        dcqn = lax.dot_general(dqr_ref[...], wq_ref[...], _DIMS["nt"], preferred_element_type=F32)
        dckvn = lax.dot_general(dkv_ref[...], wkv_ref[...], _DIMS["nt"], preferred_element_type=F32)

        zl = z_ref[...]

        def rms_bwd(c, dn, g_ref, gg_ref):
            r = lax.rsqrt(jnp.mean(c * c, axis=-1, keepdims=True) + EPS)
            ch = c * r
            gg_ref[...] += jnp.sum(dn * ch, axis=0, keepdims=True)
            dch = dn * g_ref[...]
            return r * (dch - ch * jnp.mean(dch * ch, axis=-1, keepdims=True))

        dz_ref[:, :Q_RANK] = rms_bwd(zl[:, :Q_RANK], dcqn, gq_ref, ggq_ref).astype(dz_ref.dtype)
        dz_ref[:, Q_RANK:Q_RANK + KV_RANK] = rms_bwd(zl[:, Q_RANK:Q_RANK + KV_RANK], dckvn, gkv_ref, ggkv_ref).astype(dz_ref.dtype)
        dz_ref[:, Q_RANK + KV_RANK:] = dkr.astype(dz_ref.dtype)

    def row(w):
        return pl.BlockSpec((tr, w), lambda i: (i, 0))

    def full(a):
        return pl.BlockSpec(a.shape, lambda i: (0, 0))

    lat = pl.BlockSpec((tr, LAT), lambda i: (i, lat_blk))
    return pl.pallas_call(
        body, name="lat_bwd",
        out_shape=(jax.ShapeDtypeStruct(dz.shape, dz.dtype), jax.ShapeDtypeStruct((T, HEADS * HEAD_PAD), MXU_DTYPE),
                   jax.ShapeDtypeStruct((T, 2 * HEADS * NOPE), MXU_DTYPE), jax.ShapeDtypeStruct(gq.shape, F32),
                   jax.ShapeDtypeStruct(gkv.shape, F32)),
        grid=(T // tr,),
        in_specs=[pl.BlockSpec(memory_space=pl.ANY), lat, row(HEADS * HEAD_PAD), row(HEADS * HEAD_PAD), row(HEADS * NOPE),
                  full(gq), full(gkv), full(wq), full(wkv), row(128), row(128)],
        out_specs=(lat, row(HEADS * HEAD_PAD), row(2 * HEADS * NOPE), full(gq), full(gkv)),
        input_output_aliases={0: 0},
        compiler_params=_params(("arbitrary",), 8 * _nbytes((tr, HEADS * HEAD_PAD), F32)),
    )(dz, z, dq, dk, dv, gq, gkv, wq, wkv, cos_a, sin_a)


def _shift_down(x, k, row_idx):
    return jnp.where(row_idx >= k, pltpu.roll(x, k, 0), 0.0)


def _shift_up(x, k, row_idx, S):
    return jnp.where(row_idx < S - k, pltpu.roll(x, S - k, 0), 0.0)


def _conv(x, cw, cb, row_idx):
    return cb + cw[0:1, :] * _shift_down(x, 2, row_idx) + cw[1:2, :] * _shift_down(x, 1, row_idx) + cw[2:3, :] * x


def _gate_fwd(up3, conv_w, conv_b, B, S):
    T = B * S
    W = FF_TILE

    def body(up_ref, cw_ref, cb_ref, act_ref):
        row_idx = lax.broadcasted_iota(jnp.int32, (S, W), 0)
        gate = _conv(up_ref[0], cw_ref[0], cb_ref[0], row_idx)
        val = _conv(up_ref[1], cw_ref[1], cb_ref[1], row_idx)
        act_ref[...] = (gate * _sigmoid(gate) * val).astype(act_ref.dtype)

    return pl.pallas_call(
        body, name="gate_fwd", out_shape=jax.ShapeDtypeStruct((T, D_FF), MXU_DTYPE), grid=(B, N_FF_TILES),
        in_specs=[pl.BlockSpec((2, S, W), lambda b, j: (0, b, j)), pl.BlockSpec((2, 3, W), lambda b, j: (0, 0, j)),
                  pl.BlockSpec((2, 1, W), lambda b, j: (0, 0, j))],
        out_specs=pl.BlockSpec((S, W), lambda b, j: (b, j)),
        compiler_params=_params(("parallel", "parallel"), 12 * _nbytes((S, W), F32)),
    )(up3, conv_w, conv_b)


def _gate_bwd(up3, dact, conv_w, conv_b, B, S):
    T = B * S
    W = FF_TILE

    def body(up_ref, da_ref, cw_ref, cb_ref, dup_ref, gcw_ref, gcb_ref):
        @pl.when(pl.program_id(1) == 0)
        def _():
            gcw_ref[...] = jnp.zeros_like(gcw_ref)
            gcb_ref[...] = jnp.zeros_like(gcb_ref)

        row_idx = lax.broadcasted_iota(jnp.int32, (S, W), 0)
        gate = _conv(up_ref[0], cw_ref[0], cb_ref[0], row_idx)
        val = _conv(up_ref[1], cw_ref[1], cb_ref[1], row_idx)
        sg = _sigmoid(gate)
        da = da_ref[...]
        d_halves = (da * val * (sg * (1.0 + gate * (1.0 - sg))), da * (gate * sg))
        for half, dup in enumerate(d_halves):
            x = up_ref[half]
            cw = cw_ref[half]
            gcb_ref[half] += jnp.sum(dup, axis=0, keepdims=True)
            gcw_ref[half, 0:1, :] += jnp.sum(dup * _shift_down(x, 2, row_idx), axis=0, keepdims=True)
            gcw_ref[half, 1:2, :] += jnp.sum(dup * _shift_down(x, 1, row_idx), axis=0, keepdims=True)
            gcw_ref[half, 2:3, :] += jnp.sum(dup * x, axis=0, keepdims=True)
            dx = (cw[2:3, :] * dup + cw[1:2, :] * _shift_up(dup, 1, row_idx, S) + cw[0:1, :] * _shift_up(dup, 2, row_idx, S))
            dup_ref[half] = dx.astype(dup_ref.dtype)

    up_spec = pl.BlockSpec((2, S, W), lambda j, b: (0, b, j))
    cw_spec = pl.BlockSpec((2, 3, W), lambda j, b: (0, 0, j))
    cb_spec = pl.BlockSpec((2, 1, W), lambda j, b: (0, 0, j))
    return pl.pallas_call(
        body, name="gate_bwd",
        out_shape=(jax.ShapeDtypeStruct((2, T, D_FF), MXU_DTYPE), jax.ShapeDtypeStruct((2, 3, D_FF), F32),
                   jax.ShapeDtypeStruct((2, 1, D_FF), F32)),
        grid=(N_FF_TILES, B),
        in_specs=[up_spec, pl.BlockSpec((S, W), lambda j, b: (b, j)), cw_spec, cb_spec],
        out_specs=(up_spec, cw_spec, cb_spec),
        compiler_params=_params(("parallel", "arbitrary"), 16 * _nbytes((S, W), F32)),
    )(up3, dact, conv_w, conv_b)


def _final(x2, tgt, g, tr=512):
    T, D = x2.shape

    def body(x_ref, t_ref, g_ref, dx_ref, loss_ref, gg_ref):
        @pl.when(pl.program_id(0) == 0)
        def _():
            loss_ref[...] = jnp.zeros_like(loss_ref)
            gg_ref[...] = jnp.zeros_like(gg_ref)

        xv = x_ref[...]
        gv = g_ref[...]
        r = lax.rsqrt(jnp.mean(xv * xv, axis=-1, keepdims=True) + EPS)
        xn = xv * r
        err = xn * gv - t_ref[...]
        loss_ref[...] += 0.5 * jnp.sum(jnp.mean(err * err, axis=-1, keepdims=True), axis=0, keepdims=True)
        dy = err * (1.0 / D)
        gg_ref[...] += jnp.sum(dy * xn, axis=0, keepdims=True)
        dxn = dy * gv
        dx_ref[...] = r * (dxn - xn * jnp.mean(dxn * xn, axis=-1, keepdims=True))

    row = pl.BlockSpec((tr, D), lambda i: (i, 0))
    vec = pl.BlockSpec((1, D), lambda i: (0, 0))
    return pl.pallas_call(
        body, name="final_loss",
        out_shape=(jax.ShapeDtypeStruct((T, D), F32), jax.ShapeDtypeStruct((1, 128), F32), jax.ShapeDtypeStruct((1, D), F32)),
        grid=(T // tr,), in_specs=[row, row, vec],
        out_specs=(row, pl.BlockSpec((1, 128), lambda i: (0, 0)), vec),
        compiler_params=_params(("arbitrary",), 6 * _nbytes((tr, D), F32)),
    )(x2, tgt, g)


def _sum_slabs(parts, name, tr):
    rows, cols = parts[0].shape
    n = len(parts)

    def body(*refs):
        acc = refs[0][...]
        for r in refs[1:n]:
            acc = acc + r[...]
        refs[n][...] = acc

    blk = pl.BlockSpec((tr, cols), lambda i: (i, 0))
    return pl.pallas_call(
        body, name=name, out_shape=jax.ShapeDtypeStruct((rows, cols), F32), grid=(rows // tr,),
        in_specs=[blk] * n, out_specs=blk,
        compiler_params=_params(("parallel",), (n + 1) * _nbytes((tr, cols), F32)),
    )(*parts)


def _adamw(w, g, m, v, name):
    rows, cols = w.shape
    tr = rows
    for cand in (256, 128, 64, 32, 16, 8):
        if rows % cand == 0:
            tr = cand
            break
    c1 = 1.0 - ADAM_B1 ** ADAM_STEP
    c2 = 1.0 - ADAM_B2 ** ADAM_STEP

    def body(w_ref, g_ref, m_ref, v_ref, d_ref, nm_ref, nv_ref):
        gv = g_ref[...]
        nm = ADAM_B1 * m_ref[...] + (1.0 - ADAM_B1) * gv
        nv = ADAM_B2 * v_ref[...] + (1.0 - ADAM_B2) * (gv * gv)
        nm_ref[...] = nm
        nv_ref[...] = nv
        d_ref[...] = -ADAM_LR * ((nm / c1) / (jnp.sqrt(nv / c2) + ADAM_EPS) + ADAM_WD * w_ref[...])

    blk = pl.BlockSpec((tr, cols), lambda i: (i, 0))
    sds = jax.ShapeDtypeStruct((rows, cols), F32)
    return pl.pallas_call(
        body, name=name, out_shape=(sds, sds, sds), grid=(rows // tr,), in_specs=[blk] * 4, out_specs=(blk, blk, blk),
        compiler_params=_params(("parallel",), 7 * _nbytes((tr, cols), F32)),
    )(w, g, m, v)


_ANY = pl.BlockSpec(memory_space=pl.ANY)


def _place():
    x, y, c = lax.axis_index("x"), lax.axis_index("y"), lax.axis_index("c")
    chips = [(1 - x, y), (x, 1 - y), (1 - x, 1 - y)]
    return x, y, c, chips


def _gather_weights(shards, conv_w):
    n = len(shards)

    def body(*refs):
        ins, cw_ref, outs, cwo_ref = refs[:n], refs[n], refs[n + 1:2 * n + 1], refs[2 * n + 1]
        send, recv, fsend, frecv, csend, crecv, loc = refs[2 * n + 2:]
        x, y, c, chips = _place()
        me = 2 * x + y
        local = [pltpu.make_async_copy(ins[w], outs[w].at[me], loc.at[w]) for w in range(n)]
        local.append(pltpu.make_async_copy(cw_ref, cwo_ref.at[me], loc.at[n]))
        for cp in local:
            cp.start()
        first, passed = [], []
        for w in range(n):
            for j, (px, py) in enumerate(chips):
                first.append(pltpu.make_async_remote_copy(
                    src_ref=ins[w].at[c], dst_ref=outs[w].at[me, c], send_sem=send.at[3 * w + j],
                    recv_sem=recv.at[3 * w + j], device_id=(px, py, c), device_id_type=MESH))
        for j, (px, py) in enumerate(chips):
            first.append(pltpu.make_async_remote_copy(
                src_ref=cw_ref, dst_ref=cwo_ref.at[me], send_sem=csend.at[j], recv_sem=crecv.at[j],
                device_id=(px, py, c), device_id_type=MESH))
        for cp in first:
            cp.start()
        for w in range(n):
            for j, (px, py) in enumerate(chips):
                landed = outs[w].at[2 * px + py, c]
                pltpu.make_async_remote_copy(src_ref=landed, dst_ref=landed, send_sem=send.at[3 * w + j],
                                             recv_sem=recv.at[3 * w + j], device_id=(px, py, c),
                                             device_id_type=MESH).wait_recv()
                fw = pltpu.make_async_remote_copy(src_ref=landed, dst_ref=landed, send_sem=fsend.at[3 * w + j],
                                                  recv_sem=frecv.at[3 * w + j], device_id=(x, y, 1 - c),
                                                  device_id_type=MESH)
                fw.start()
                passed.append(fw)
        for w in range(n):
            for j, (px, py) in enumerate(chips):
                other = outs[w].at[2 * px + py, 1 - c]
                pltpu.make_async_remote_copy(src_ref=other, dst_ref=other, send_sem=fsend.at[3 * w + j],
                                             recv_sem=frecv.at[3 * w + j], device_id=(x, y, 1 - c),
                                             device_id_type=MESH).wait_recv()
        for j, (px, py) in enumerate(chips):
            pltpu.make_async_remote_copy(src_ref=cw_ref, dst_ref=cwo_ref.at[2 * px + py], send_sem=csend.at[j],
                                         recv_sem=crecv.at[j], device_id=(px, py, c), device_id_type=MESH).wait_recv()
        for cp in first + passed:
            cp.wait_send()
        for cp in local:
            cp.wait()

    dma = lambda k: pltpu.SemaphoreType.DMA((k,))
    return pl.pallas_call(
        body, name="gather_weights",
        out_shape=tuple(jax.ShapeDtypeStruct((N_CHIPS,) + s.shape, s.dtype) for s in list(shards) + [conv_w]),
        in_specs=[_ANY] * (n + 1), out_specs=tuple([_ANY] * (n + 1)),
        scratch_shapes=[dma(3 * n), dma(3 * n), dma(3 * n), dma(3 * n), dma(3), dma(3), dma(n + 1)],
    )(*shards, conv_w)


def _swap_halves(gs):
    n = len(gs)

    def body(*refs):
        ins, outs, send, recv = refs[:n], refs[n:2 * n], refs[2 * n], refs[2 * n + 1]
        x, y, c, _ = _place()
        cps = []
        for w in range(n):
            cps.append(pltpu.make_async_remote_copy(
                src_ref=ins[w].at[:, 1 - c], dst_ref=outs[w], send_sem=send.at[w], recv_sem=recv.at[w],
                device_id=(x, y, 1 - c), device_id_type=MESH))
        for cp in cps:
            cp.start()
        for cp in cps:
            cp.wait()

    return pl.pallas_call(
        body, name="grad_swap_halves",
        out_shape=tuple(jax.ShapeDtypeStruct((g.shape[0],) + g.shape[2:], g.dtype) for g in gs),
        in_specs=[_ANY] * n, out_specs=tuple([_ANY] * n),
        scratch_shapes=[pltpu.SemaphoreType.DMA((n,)), pltpu.SemaphoreType.DMA((n,))],
    )(*gs)


GRAD_PAYLOAD = jnp.bfloat16


def _pair_sum(gs, gots):
    n = len(gs)
    core = lax.axis_index("c").astype(jnp.int32).reshape(1)

    def body(core_ref, *refs):
        del core_ref
        for w in range(n):
            refs[2 * n + w][...] = (refs[w][...] + refs[n + w][...]).astype(GRAD_PAYLOAD)

    in_specs, out_specs, out_shape, nbytes = [], [], [], 0
    for g in gs:
        q = g.shape[1] // 4
        in_specs.append(pl.BlockSpec((1, q, g.shape[2]), lambda s, r, core: (s, 2 * core[0] + r, 0)))
        nbytes += 3 * _nbytes((q, g.shape[2]), F32)
    for g in gs:
        q = g.shape[1] // 4
        in_specs.append(pl.BlockSpec((1, q, g.shape[2]), lambda s, r, core: (s, r, 0)))
        out_specs.append(pl.BlockSpec((1, q, g.shape[2]), lambda s, r, core: (s, r, 0)))
        out_shape.append(jax.ShapeDtypeStruct((g.shape[0], g.shape[1] // 2, g.shape[2]), GRAD_PAYLOAD))
    return pl.pallas_call(
        body, name="grad_pair_sum", out_shape=tuple(out_shape),
        grid_spec=pltpu.PrefetchScalarGridSpec(num_scalar_prefetch=1, grid=(N_CHIPS, 2), in_specs=in_specs,
                                               out_specs=tuple(out_specs)),
        compiler_params=_params(("parallel", "parallel"), nbytes),
    )(core, *gs, *gots)


def _scatter_to_chips(ps):
    n = len(ps)

    def body(*refs):
        ins, outs, send, recv, loc = refs[:n], refs[n:2 * n], refs[2 * n], refs[2 * n + 1], refs[2 * n + 2]
        x, y, c, chips = _place()
        me = 2 * x + y
        local = [pltpu.make_async_copy(ins[w].at[me], outs[w].at[me], loc.at[w]) for w in range(n)]
        for cp in local:
            cp.start()
        cps = []
        for w in range(n):
            for j, (px, py) in enumerate(chips):
                cps.append(pltpu.make_async_remote_copy(
                    src_ref=ins[w].at[2 * px + py], dst_ref=outs[w].at[me], send_sem=send.at[3 * w + j],
                    recv_sem=recv.at[3 * w + j], device_id=(px, py, c), device_id_type=MESH))
        for cp in cps:
            cp.start()
        for w in range(n):
            for j, (px, py) in enumerate(chips):
                slot = outs[w].at[2 * px + py]
                pltpu.make_async_remote_copy(src_ref=slot, dst_ref=slot, send_sem=send.at[3 * w + j],
                                             recv_sem=recv.at[3 * w + j], device_id=(px, py, c),
                                             device_id_type=MESH).wait_recv()
        for cp in cps:
            cp.wait_send()
        for cp in local:
            cp.wait()

    dma = lambda k: pltpu.SemaphoreType.DMA((k,))
    return pl.pallas_call(
        body, name="grad_scatter_chips", out_shape=tuple(jax.ShapeDtypeStruct(p.shape, p.dtype) for p in ps),
        in_specs=[_ANY] * n, out_specs=tuple([_ANY] * n), scratch_shapes=[dma(3 * n), dma(3 * n), dma(n)],
    )(*ps)


def _chip_sum(landed):
    n = len(landed)

    def body(*refs):
        for w in range(n):
            src = refs[w]
            refs[n + w][...] = ((src[0].astype(F32) + src[1].astype(F32)) + src[2].astype(F32)) + src[3].astype(F32)

    in_specs, out_specs, out_shape, nbytes = [], [], [], 0
    for a in landed:
        q = a.shape[1] // 2
        in_specs.append(pl.BlockSpec((N_CHIPS, q, a.shape[2]), lambda r: (0, r, 0)))
        out_specs.append(pl.BlockSpec((q, a.shape[2]), lambda r: (r, 0)))
        out_shape.append(jax.ShapeDtypeStruct(a.shape[1:], F32))
        nbytes += 4 * _nbytes((q, a.shape[2]), F32)
    return pl.pallas_call(
        body, name="grad_chip_sum", out_shape=tuple(out_shape), grid=(2,), in_specs=in_specs,
        out_specs=tuple(out_specs), compiler_params=_params(("parallel",), nbytes),
    )(*landed)


def _join_halves(ss):
    n = len(ss)

    def body(*refs):
        ins, outs, send, recv, loc = refs[:n], refs[n:2 * n], refs[2 * n], refs[2 * n + 1], refs[2 * n + 2]
        x, y, c, _ = _place()
        local, cps = [], []
        for w in range(n):
            local.append(pltpu.make_async_copy(ins[w], outs[w].at[c], loc.at[w]))
            cps.append(pltpu.make_async_remote_copy(
                src_ref=ins[w], dst_ref=outs[w].at[c], send_sem=send.at[w], recv_sem=recv.at[w],
                device_id=(x, y, 1 - c), device_id_type=MESH))
        for cp in local + cps:
            cp.start()
        for w in range(n):
            got = outs[w].at[1 - c]
            pltpu.make_async_remote_copy(src_ref=got, dst_ref=got, send_sem=send.at[w], recv_sem=recv.at[w],
                                         device_id=(x, y, 1 - c), device_id_type=MESH).wait_recv()
        for cp in cps:
            cp.wait_send()
        for cp in local:
            cp.wait()

    dma = lambda k: pltpu.SemaphoreType.DMA((k,))
    return pl.pallas_call(
        body, name="grad_join_halves",
        out_shape=tuple(jax.ShapeDtypeStruct((2,) + s.shape, s.dtype) for s in ss),
        in_specs=[_ANY] * n, out_specs=tuple([_ANY] * n), scratch_shapes=[dma(n), dma(n), dma(n)],
    )(*ss)


def _gather_all(p):
    rows, cols = p.shape

    def body(p_ref, out_ref, send, recv, loc):
        x, y, c, _ = _place()
        me = 4 * x + 2 * y + c
        own = pltpu.make_async_copy(p_ref, out_ref.at[me], loc)
        own.start()
        flips = [(fx, fy, fc) for fx in (0, 1) for fy in (0, 1) for fc in (0, 1)][1:]
        peers = [(x ^ fx, y ^ fy, c ^ fc) for fx, fy, fc in flips]
        cps = [pltpu.make_async_remote_copy(src_ref=p_ref, dst_ref=out_ref.at[me], send_sem=send.at[j], recv_sem=recv.at[j],
                                            device_id=peer, device_id_type=MESH) for j, peer in enumerate(peers)]
        for cp in cps:
            cp.start()
        for j, (px, py, pc) in enumerate(peers):
            slot = out_ref.at[4 * px + 2 * py + pc]
            pltpu.make_async_remote_copy(src_ref=slot, dst_ref=slot, send_sem=send.at[j], recv_sem=recv.at[j],
                                         device_id=(px, py, pc), device_id_type=MESH).wait_recv()
        for cp in cps:
            cp.wait_send()
        own.wait()

    dma7 = pltpu.SemaphoreType.DMA((7,))
    return pl.pallas_call(
        body, name="small_grads_gather", out_shape=jax.ShapeDtypeStruct((8, rows, cols), p.dtype),
        in_specs=[_ANY], out_specs=_ANY, scratch_shapes=[dma7, dma7, pltpu.SemaphoreType.DMA],
    )(p)


def _rot_cols(w):
    a, b = jnp.split(w, 2, axis=-1)
    return jnp.concatenate([-b, a], axis=-1)


def _rot_cols_t(g):
    a, b = jnp.split(g, 2, axis=-1)
    return jnp.concatenate([b, -a], axis=-1)


def _cols_from_chips(a):
    n, r, cs = a.shape
    return jnp.transpose(a, (1, 0, 2)).reshape(r, n * cs)


def _cols_to_chips(a):
    r, cc = a.shape
    return jnp.transpose(a.reshape(r, N_CHIPS, cc // N_CHIPS), (1, 0, 2))


def _conv_w_split(cw):
    return jnp.swapaxes(cw.reshape(3, 2, D_FF), 0, 1)


def _conv_w_join(g):
    return jnp.swapaxes(g, 0, 1).reshape(3, 2 * D_FF)


_SEG =(D_MODEL, 2 * D_MODEL, 2 * D_MODEL + Q_RANK, 2 * D_MODEL + Q_RANK + KV_RANK, 2 * D_MODEL + Q_RANK + KV_RANK + ROPE,
        3 * D_MODEL + Q_RANK + KV_RANK + ROPE)


def _w_in_to_pad(w):
    u, v, cq, ckv, kr, ga, gb = jnp.split(w, _SEG, axis=1)
    return jnp.concatenate([u, v, ga, gb, cq, ckv, kr, _rot_cols(kr)], axis=1)


def _w_in_from_pad(g):
    u, v, ga, gb, cq, ckv, kr, krr = jnp.split(
        g, (D_MODEL, 2 * D_MODEL, 3 * D_MODEL, 4 * D_MODEL, 4 * D_MODEL + Q_RANK, 4 * D_MODEL + Q_RANK + KV_RANK,
            4 * D_MODEL + Q_RANK + KV_RANK + ROPE), axis=1)
    return jnp.concatenate([u, v, cq, ckv, kr + _rot_cols_t(krr), ga, gb], axis=1)


def _w_uq_to_pad(w):
    t = w.reshape(Q_RANK, HEADS, QK_DIM)
    nope, rope = t[..., :NOPE], t[..., NOPE:]
    return jnp.concatenate([nope, rope, _rot_cols(rope)], axis=-1).reshape(Q_RANK, HEADS * HEAD_PAD)


def _w_uq_from_pad(g):
    t = g.reshape(Q_RANK, HEADS, HEAD_PAD)
    nope, rope, rot = t[..., :NOPE], t[..., NOPE:QK_DIM], t[..., QK_DIM:]
    return jnp.concatenate([nope, rope + _rot_cols_t(rot)], axis=-1).reshape(Q_RANK, HEADS * QK_DIM)


def _w_ukv_to_pad(w):
    t = w.reshape(KV_RANK, HEADS, 2, NOPE)
    return jnp.swapaxes(t, 1, 2).reshape(KV_RANK, 2 * HEADS * NOPE)


def _w_ukv_from_pad(g):
    t = g.reshape(KV_RANK, 2, HEADS, NOPE)
    return jnp.swapaxes(t, 1, 2).reshape(KV_RANK, 2 * HEADS * NOPE)


def _rope_tables(positions):
    inv_freq = 1.0 / (ROPE_THETA ** (jnp.arange(0, ROPE, 2, dtype=F32) / ROPE))
    ang = positions.astype(F32).reshape(-1, 1) * inv_freq
    cos, sin = jnp.cos(ang), jnp.sin(ang)
    zero = jnp.zeros((ang.shape[0], 64), F32)
    return jnp.concatenate([cos, cos, zero], axis=1), jnp.concatenate([sin, sin, zero], axis=1)


_BIG = ("w_in", "w_uq", "w_ukv", "w_out", "w_up", "w_down")
UP_SHARD = 2 * D_FF // N_CHIPS


def _local_step(x, positions, tgt, wts):
    B, S, D = x.shape
    T = B * S
    xf = x.reshape(T, D)
    cos_a, sin_a = _rope_tables(positions)
    bs_t = jnp.pad(wts["a_spatial_b"].T, ((0, 0), (0, 128 - A_GROUPS)))

    h = _rms_fwd(xf, wts["mix_norm"], "norm1_fwd")
    z = _mm(h, wts["w_in"], "nn", "in_proj", tm=512, tn=1536, tk=D)
    q, k, v, cqn, ckvn = _lat_fwd(z, wts["q_a_norm"], wts["kv_a_norm"], wts["w_q"], wts["w_kv"], cos_a, sin_a)
    yb, *lses = _attn_fwd(q, k, v, B, S)
    merged = _mix_fwd(z, yb, wts["a_v_norm_g"], wts["a_v_norm_b"], wts["a_spatial_w"], bs_t)
    x1 = _mm(merged, wts["w_out"], "nn", "out_proj", tm=512, tn=D, tk=D, add=xf)
    h2 = _rms_fwd(x1, wts["ffn_norm"], "norm2_fwd")
    up_pre = _mm(h2, wts["w_up"], "nn", "up_proj", tm=512, tn=UP_SHARD, tk=D, dims=(T, 2 * D_FF, D),
                 b_spec=pl.BlockSpec((None, D, UP_SHARD), lambda i, j, k: (j, 0, 0)),
                 o_spec=pl.BlockSpec((None, 512, UP_SHARD), lambda i, j, k: (j // 2, i, j % 2)), out_shape=(2, T, D_FF))
    act = _gate_fwd(up_pre, wts["conv_w"], wts["conv_b"], B, S)
    x2 = _mm(act, wts["w_down"], "nn", "down_proj", tm=512, tn=D, tk=1408, add=x1)
    dx2, loss_row, g_final = _final(x2, tgt.reshape(T, D), wts["final_norm"])

    g = {"final_norm": g_final}
    dact = _mm(dx2, wts["w_down"], "nt", "down_proj_dx", tm=512, tn=1408, tk=D)
    g["w_down"] = _mm(act, dx2, "tn", "down_proj_dw", tm=1408, tn=D, tk=512)
    dup, g["conv_w"], g["conv_b"] = _gate_bwd(up_pre, dact, wts["conv_w"], wts["conv_b"], B, S)
    dh2 = _mm(dup, wts["w_up"], "nt", "up_proj_dx", tm=512, tn=D, tk=UP_SHARD, dims=(T, D, 2 * D_FF),
              a_spec=pl.BlockSpec((None, 512, UP_SHARD), lambda i, j, k: (k // 2, i, k % 2)),
              b_spec=pl.BlockSpec((None, D, UP_SHARD), lambda i, j, k: (k, 0, 0)))
    g["w_up"] = _mm(h2, dup, "tn", "up_proj_dw", tm=D, tn=UP_SHARD, tk=512, dims=(D, 2 * D_FF, T),
                    b_spec=pl.BlockSpec((None, 512, UP_SHARD), lambda i, j, k: (j // 2, k, j % 2)),
                    o_spec=pl.BlockSpec((None, D, UP_SHARD), lambda i, j, k: (j, 0, 0)), out_shape=(N_CHIPS, D, UP_SHARD))
    dx1, g["ffn_norm"] = _rms_bwd(x1, wts["ffn_norm"], dh2, dx2, "norm2_bwd")
    dm = _mm(dx1, wts["w_out"], "nt", "out_proj_dx", tm=512, tn=D, tk=D)
    g["w_out"] = _mm(merged, dx1, "tn", "out_proj_dw", tm=D, tn=D, tk=512)
    dz, dyb, dl, g["a_spatial_w"], gbs, g["a_v_norm_g"], g["a_v_norm_b"] = _mix_bwd(
        z, yb, dm, wts["a_v_norm_g"], wts["a_v_norm_b"], wts["a_spatial_w"], bs_t)
    g["a_spatial_b"] = gbs[:, :A_GROUPS].T
    delta = dl.reshape(HEADS * T // ATT_BLOCK, 1, ATT_BLOCK)
    dq, dk, dv = _attn_bwd(q, k, v, dyb, lses, delta, B, S)
    dz, dq_raw, dkv, g["q_a_norm"], g["kv_a_norm"] = _lat_bwd(
        dz, z, dq, dk, dv, wts["q_a_norm"], wts["kv_a_norm"], wts["w_q"], wts["w_kv"], cos_a, sin_a)
    g["w_q"] = _mm(cqn, dq_raw, "tn", "q_proj_dw", tm=Q_RANK, tn=HEADS * HEAD_PAD, tk=512)
    g["w_kv"] = _mm(ckvn, dkv, "tn", "kv_proj_dw", tm=KV_RANK, tn=2 * HEADS * NOPE, tk=512)
    dh = _mm(dz, wts["w_in"], "nt", "in_proj_dx", tm=512, tn=D, tk=1536)
    g["w_in"] = _mm(h, dz, "tn", "in_proj_dw", tm=D, tn=1536, tk=512)
    dx, g["mix_norm"] = _rms_bwd(xf, wts["mix_norm"], dh, dx1, "norm1_bwd")
    return loss_row[0, 0], dx.reshape(B, S, D), g


_SMALL = (("mix_norm", (1, D_MODEL)), ("a_v_norm_g", (1, D_MODEL)), ("a_v_norm_b", (1, D_MODEL)),
          ("a_spatial_w", (A_GROUPS * CHUNK, CHUNK)), ("a_spatial_b", (1, A_GROUPS * CHUNK)), ("q_a_norm", (1, Q_RANK)),
          ("kv_a_norm", (1, KV_RANK)), ("ffn_norm", (1, D_MODEL)), ("conv_b", (1, 2 * D_FF)), ("final_norm", (1, D_MODEL)),
          ("conv_w", (3, 2 * D_FF)))
_SMALL_ROWS = -(-sum(math.prod(s) for _, s in _SMALL) // (128 * 8)) * 8


def kernel(x, positions, mix_norm, w_in, a_v_norm_g, a_v_norm_b, a_spatial_w, a_spatial_b, q_a_norm, w_uq, kv_a_norm, w_ukv, w_out, ffn_norm, w_up, conv_w, conv_b, w_down, final_norm, loss_target, m_mix_norm, m_w_in, m_a_v_norm_g, m_a_v_norm_b, m_a_spatial_w, m_a_spatial_b, m_q_a_norm, m_w_uq, m_kv_a_norm, m_w_ukv, m_w_out, m_ffn_norm, m_w_up, m_conv_w, m_conv_b, m_w_down, m_final_norm, v_mix_norm, v_w_in, v_a_v_norm_g, v_a_v_norm_b, v_a_spatial_w, v_a_spatial_b, v_q_a_norm, v_w_uq, v_kv_a_norm, v_w_ukv, v_w_out, v_ffn_norm, v_w_up, v_conv_w, v_conv_b, v_w_down, v_final_norm):
    weights = dict(mix_norm=mix_norm, w_in=w_in, a_v_norm_g=a_v_norm_g, a_v_norm_b=a_v_norm_b, a_spatial_w=a_spatial_w,
                   a_spatial_b=a_spatial_b, q_a_norm=q_a_norm, w_uq=w_uq, kv_a_norm=kv_a_norm, w_ukv=w_ukv, w_out=w_out,
                   ffn_norm=ffn_norm, w_up=w_up, conv_w=conv_w, conv_b=conv_b, w_down=w_down, final_norm=final_norm)
    m_in = dict(mix_norm=m_mix_norm, w_in=m_w_in, a_v_norm_g=m_a_v_norm_g, a_v_norm_b=m_a_v_norm_b,
                a_spatial_w=m_a_spatial_w, a_spatial_b=m_a_spatial_b, q_a_norm=m_q_a_norm, w_uq=m_w_uq,
                kv_a_norm=m_kv_a_norm, w_ukv=m_w_ukv, w_out=m_w_out, ffn_norm=m_ffn_norm, w_up=m_w_up, conv_w=m_conv_w,
                conv_b=m_conv_b, w_down=m_w_down, final_norm=m_final_norm)
    v_in = dict(mix_norm=v_mix_norm, w_in=v_w_in, a_v_norm_g=v_a_v_norm_g, a_v_norm_b=v_a_v_norm_b,
                a_spatial_w=v_a_spatial_w, a_spatial_b=v_a_spatial_b, q_a_norm=v_q_a_norm, w_uq=v_w_uq,
                kv_a_norm=v_kv_a_norm, w_ukv=v_w_ukv, w_out=v_w_out, ffn_norm=v_ffn_norm, w_up=v_w_up, conv_w=v_conv_w,
                conv_b=v_conv_b, w_down=v_w_down, final_norm=v_final_norm)
    names = list(weights)
    chip = 2 * lax.axis_index("x") + lax.axis_index("y")

    def halves(a):
        return a.reshape(a.shape[:-2] + (2, a.shape[-2] // 2, a.shape[-1]))

    def whole(a):
        return a.reshape(a.shape[:-3] + (2 * a.shape[-2], a.shape[-1]))

    *gathered, cw_all = _gather_weights([halves(weights[n][0].astype(MXU_DTYPE)) for n in _BIG], conv_w[0])
    w_in_sh, w_uq_sh, w_ukv_sh, w_out_sh, w_up_sh, w_down_sh = (whole(a) for a in gathered)
    wts = dict(
        mix_norm=mix_norm, a_v_norm_g=a_v_norm_g, a_v_norm_b=a_v_norm_b, a_spatial_w=a_spatial_w[0],
        a_spatial_b=a_spatial_b[0], q_a_norm=q_a_norm, kv_a_norm=kv_a_norm, ffn_norm=ffn_norm,
        final_norm=final_norm.reshape(1, D_MODEL),
        w_in=_w_in_to_pad(_cols_from_chips(w_in_sh)), w_q=_w_uq_to_pad(_cols_from_chips(w_uq_sh)),
        w_kv=_w_ukv_to_pad(_cols_from_chips(w_ukv_sh)), w_out=w_out_sh.reshape(D_MODEL, D_MODEL), w_up=w_up_sh,
        w_down=w_down_sh.reshape(D_FF, D_MODEL), conv_w=_conv_w_split(_cols_from_chips(cw_all)),
        conv_b=conv_b.reshape(2, 1, D_FF))

    loss_part, grad_x, g = _local_step(x, positions, loss_target, wts)
    loss = lax.psum(loss_part, ("x", "y", "c"))

    slabs = [_cols_to_chips(_w_in_from_pad(g["w_in"])), _cols_to_chips(_w_uq_from_pad(g["w_q"])),
             _cols_to_chips(_w_ukv_from_pad(g["w_kv"])), g["w_out"].reshape(N_CHIPS, D_MODEL // N_CHIPS, D_MODEL),
             g["w_up"], g["w_down"].reshape(N_CHIPS, D_FF // N_CHIPS, D_MODEL)]
    landed = _scatter_to_chips(_pair_sum(slabs, _swap_halves([halves(s) for s in slabs])))
    g_big = dict(zip(_BIG, (whole(a) for a in _join_halves(_chip_sum(landed)))))

    g_small_parts = dict(g)
    g_small_parts["conv_w"] = _conv_w_join(g["conv_w"])
    g_small_parts["conv_b"] = g["conv_b"].reshape(1, 2 * D_FF)
    flat = jnp.concatenate([g_small_parts[n].reshape(-1) for n, _ in _SMALL])
    flat = jnp.pad(flat, (0, _SMALL_ROWS * 128 - flat.shape[0])).reshape(_SMALL_ROWS, 128)
    everyone = _gather_all(flat)
    total = _sum_slabs([everyone[j] for j in range(8)], "small_grads_sum", tr=_SMALL_ROWS).reshape(-1)
    g_small, o = {}, 0
    for n, shp in _SMALL:
        g_small[n] = total[o:o + math.prod(shp)].reshape(shp)
        o += math.prod(shp)
    g_small["conv_w"] = lax.dynamic_slice_in_dim(g_small["conv_w"], chip * 1408, 1408, axis=1)

    grads, deltas, new_m, new_v = {}, {}, {}, {}
    for n in names:
        w = weights[n]
        g2 = g_big[n] if n in g_big else g_small[n]
        shape2 = g2.shape
        d, nm, nv = _adamw(w.reshape(shape2), g2, m_in[n].reshape(shape2), v_in[n].reshape(shape2), "adamw_" + n)
        grads[n], deltas[n], new_m[n], new_v[n] = (t.reshape(w.shape) for t in (g2, d, nm, nv))
    return (loss, grad_x, *[grads[n] for n in names], *[deltas[n] for n in names], *[new_m[n] for n in names],
            *[new_v[n] for n in names])
```

```python
import functools
import math

import jax
import jax.numpy as jnp
from jax import lax
from jax.experimental import pallas as pl
from jax.experimental.pallas import tpu as pltpu

F32 = jnp.float32
MXU_DTYPE = jnp.bfloat16
MESH = pl.DeviceIdType.MESH

D_MODEL = 1024
EPS = 1e-6
A_GROUPS = 8
CHUNK = 128
HEADS = 8
NOPE = 128
ROPE = 64
QK_DIM = NOPE + ROPE
HEAD_PAD = 256
Q_RANK = 256
KV_RANK = 128
ROPE_THETA = 10000.0
D_FF = 2816
FF_TILE = 256
N_FF_TILES = D_FF // FF_TILE
LAT = 512
IN_PAD = 4 * D_MODEL + LAT
N_CHIPS = 4
ADAM_LR, ADAM_B1, ADAM_B2, ADAM_EPS, ADAM_WD, ADAM_STEP = 0.001, 0.9, 0.999, 1e-08, 0.01, 10

VMEM_CAP_V7X = 64 * 1024 * 1024
NEG = -1e30


def _params(sem, nbytes):
    limit = int(min(VMEM_CAP_V7X - (8 << 20), max(32 << 20, 3 * nbytes)))
    return pltpu.CompilerParams(dimension_semantics=sem, vmem_limit_bytes=limit)


def _nbytes(shape, dtype):
    return math.prod(shape) * jnp.dtype(dtype).itemsize


_DIMS = {"nn": (((1,), (0,)), ((), ())), "nt": (((1,), (1,)), ((), ())), "tn": (((0,), (0,)), ((), ()))}


def _mm(a, b, mode, name, *, tm, tn, tk, out_dtype=F32, add=None, dims=None, a_spec=None, b_spec=None,
        o_spec=None, out_shape=None):
    if dims is None:
        if mode == "nn":
            (M, K), (_, N) = a.shape, b.shape
        elif mode == "nt":
            (M, K), (N, _) = a.shape, b.shape
        else:
            (K, M), (_, N) = a.shape, b.shape
    else:
        M, N, K = dims
    a_blk = (tk, tm) if mode == "tn" else (tm, tk)
    b_blk = (tn, tk) if mode == "nt" else (tk, tn)
    if a_spec is None:
        a_spec = pl.BlockSpec(a_blk, (lambda i, j, k: (k, i)) if mode == "tn" else (lambda i, j, k: (i, k)))
    if b_spec is None:
        b_spec = pl.BlockSpec(b_blk, (lambda i, j, k: (j, k)) if mode == "nt" else (lambda i, j, k: (k, j)))
    if o_spec is None:
        o_spec = pl.BlockSpec((tm, tn), lambda i, j, k: (i, j))
    if out_shape is None:
        out_shape = (M, N)
    assert M % tm == 0 and N % tn == 0 and K % tk == 0, (name, M, N, K, tm, tn, tk)
    nk = K // tk
    contract = _DIMS[mode]
    has_add = add is not None

    def body(*refs):
        if has_add:
            a_ref, b_ref, add_ref, o_ref, acc = refs
        else:
            a_ref, b_ref, o_ref, acc = refs
        k = pl.program_id(2)

        @pl.when(k == 0)
        def _():
            acc[...] = jnp.zeros_like(acc)

        acc[...] += lax.dot_general(a_ref[...].astype(MXU_DTYPE), b_ref[...].astype(MXU_DTYPE), contract,
                                    preferred_element_type=F32)

        @pl.when(k == nk - 1)
        def _():
            r = acc[...]
            if has_add:
                r = r + add_ref[...]
            o_ref[...] = r.astype(out_dtype)

    in_specs = [a_spec, b_spec]
    args = [a, b]
    nbytes = _nbytes(a_blk, a.dtype) + _nbytes(b_blk, b.dtype) + 3 * _nbytes((tm, tn), F32)
    if has_add:
        in_specs.append(pl.BlockSpec((tm, tn), lambda i, j, k: (i, j)))
        args.append(add)
        nbytes += _nbytes((tm, tn), F32)
    return pl.pallas_call(
        body, name=name, out_shape=jax.ShapeDtypeStruct(out_shape, out_dtype),
        grid=(M // tm, N // tn, nk), in_specs=in_specs, out_specs=o_spec,
        scratch_shapes=[pltpu.VMEM((tm, tn), F32)],
        compiler_params=_params(("parallel", "parallel", "arbitrary"), nbytes),
    )(*args)


_GELU_C = math.sqrt(2.0 / math.pi)
_GELU_A = 0.044715


def _sigmoid(x):
    return 1.0 / (1.0 + jnp.exp(-x))


def _gelu(x):
    t = jnp.tanh(_GELU_C * (x + _GELU_A * (x * x * x)))
    return x * (0.5 * (1.0 + t))


def _gelu_and_grad(x):
    x2 = x * x
    t = jnp.tanh(_GELU_C * (x + _GELU_A * (x2 * x)))
    cdf = 0.5 * (1.0 + t)
    grad = cdf + 0.5 * x * (1.0 - t * t) * (_GELU_C * (1.0 + 3.0 * _GELU_A * x2))
    return x * cdf, grad


def _rope_mix(g, cos_a, sin_a):
    return g * cos_a + pltpu.roll(g, 64, 1) * sin_a


def _rope_mix_bwd(d, cos_a, sin_a):
    return d * cos_a + pltpu.roll(d * sin_a, 64, 1)


def _rms_fwd(x, g, name, tr=512):
    T, D = x.shape

    def body(x_ref, g_ref, h_ref):
        xv = x_ref[...]
        r = lax.rsqrt(jnp.mean(xv * xv, axis=-1, keepdims=True) + EPS)
        h_ref[...] = ((xv * r) * g_ref[...]).astype(h_ref.dtype)

    return pl.pallas_call(
        body, name=name, out_shape=jax.ShapeDtypeStruct((T, D), MXU_DTYPE), grid=(T // tr,),
        in_specs=[pl.BlockSpec((tr, D), lambda i: (i, 0)), pl.BlockSpec((1, D), lambda i: (0, 0))],
        out_specs=pl.BlockSpec((tr, D), lambda i: (i, 0)),
        compiler_params=_params(("parallel",), 3 * _nbytes((tr, D), F32)),
    )(x, g)


def _rms_bwd(x, g, dh, dres, name, tr=512):
    T, D = x.shape

    def body(x_ref, g_ref, dh_ref, dres_ref, dx_ref, gg_ref):
        @pl.when(pl.program_id(0) == 0)
        def _():
            gg_ref[...] = jnp.zeros_like(gg_ref)

        xv = x_ref[...]
        r = lax.rsqrt(jnp.mean(xv * xv, axis=-1, keepdims=True) + EPS)
        xn = xv * r
        dhv = dh_ref[...]
        dxn = dhv * g_ref[...]
        dx_ref[...] = dres_ref[...] + r * (dxn - xn * jnp.mean(dxn * xn, axis=-1, keepdims=True))
        gg_ref[...] += jnp.sum(dhv * xn, axis=0, keepdims=True)

    row = pl.BlockSpec((tr, D), lambda i: (i, 0))
    vec = pl.BlockSpec((1, D), lambda i: (0, 0))
    return pl.pallas_call(
        body, name=name,
        out_shape=(jax.ShapeDtypeStruct((T, D), F32), jax.ShapeDtypeStruct((1, D), F32)),
        grid=(T // tr,), in_specs=[row, vec, row, row], out_specs=(row, vec),
        compiler_params=_params(("arbitrary",), 6 * _nbytes((tr, D), F32)),
    )(x, g, dh, dres)


def _lat_fwd(z, gq, gkv, wq, wkv, cos_a, sin_a, tr=256):
    T = z.shape[0]
    lat_blk = (4 * D_MODEL) // LAT

    def body(z_ref, gq_ref, gkv_ref, wq_ref, wkv_ref, cos_ref, sin_ref, q_ref, k_ref, v_ref, cqn_ref, ckvn_ref):
        zl = z_ref[...]
        cos_v, sin_v = cos_ref[...], sin_ref[...]
        cq = zl[:, :Q_RANK]
        ckv = zl[:, Q_RANK:Q_RANK + KV_RANK]
        krb = zl[:, Q_RANK + KV_RANK:]
        cqn = ((cq * lax.rsqrt(jnp.mean(cq * cq, axis=-1, keepdims=True) + EPS)) * gq_ref[...]).astype(MXU_DTYPE)
        ckvn = ((ckv * lax.rsqrt(jnp.mean(ckv * ckv, axis=-1, keepdims=True) + EPS)) * gkv_ref[...]).astype(MXU_DTYPE)
        cqn_ref[...] = cqn
        ckvn_ref[...] = ckvn
        krr = _rope_mix(krb, cos_v, sin_v).astype(MXU_DTYPE)
        q = jnp.dot(cqn, wq_ref[...], preferred_element_type=F32)
        kv = jnp.dot(ckvn, wkv_ref[...], preferred_element_type=F32)
        for h in range(HEADS):
            o = h * HEAD_PAD
            q_ref[:, o:o + NOPE] = q[:, o:o + NOPE].astype(MXU_DTYPE)
            q_ref[:, o + NOPE:o + HEAD_PAD] = _rope_mix(q[:, o + NOPE:o + HEAD_PAD], cos_v, sin_v).astype(MXU_DTYPE)
            k_ref[:, o:o + NOPE] = kv[:, h * NOPE:(h + 1) * NOPE].astype(MXU_DTYPE)
            k_ref[:, o + NOPE:o + HEAD_PAD] = krr
        v_ref[...] = kv[:, HEADS * NOPE:].astype(MXU_DTYPE)

    def row(w):
        return pl.BlockSpec((tr, w), lambda i: (i, 0))

    def full(a):
        return pl.BlockSpec(a.shape, lambda i: (0, 0))

    return pl.pallas_call(
        body, name="lat_fwd",
        out_shape=(jax.ShapeDtypeStruct((T, HEADS * HEAD_PAD), MXU_DTYPE), jax.ShapeDtypeStruct((T, HEADS * HEAD_PAD), MXU_DTYPE),
                   jax.ShapeDtypeStruct((T, HEADS * NOPE), MXU_DTYPE), jax.ShapeDtypeStruct((T, Q_RANK), MXU_DTYPE),
                   jax.ShapeDtypeStruct((T, KV_RANK), MXU_DTYPE)),
        grid=(T // tr,),
        in_specs=[pl.BlockSpec((tr, LAT), lambda i: (i, lat_blk)), full(gq), full(gkv), full(wq), full(wkv), row(128), row(128)],
        out_specs=(row(HEADS * HEAD_PAD), row(HEADS * HEAD_PAD), row(HEADS * NOPE), row(Q_RANK), row(KV_RANK)),
        compiler_params=_params(("parallel",), 8 * _nbytes((tr, HEADS * HEAD_PAD), F32)),
    )(z, gq, gkv, wq, wkv, cos_a, sin_a)


ATT_BLOCK = 256
_SCALE = QK_DIM ** -0.5


def _causal_mask(n):
    return lax.broadcasted_iota(jnp.int32, (n, n), 1) <= lax.broadcasted_iota(jnp.int32, (n, n), 0)


def _causal_mask_t(n):
    return lax.broadcasted_iota(jnp.int32, (n, n), 0) <= lax.broadcasted_iota(jnp.int32, (n, n), 1)


ATT_HEADS = 2


def _attn_fwd(q, k, v, B, S):
    tq = ATT_BLOCK
    nq = S // tq
    T = B * S
    hp, groups = ATT_HEADS, HEADS // ATT_HEADS

    def body(q_ref, k_ref, v_ref, o_ref, *lse_refs):
        qi = pl.program_id(2)
        qs = [q_ref[:, t * HEAD_PAD:(t + 1) * HEAD_PAD] for t in range(hp)]

        def step(j, carry, masked):
            rows = pl.ds(pl.multiple_of(j * tq, tq), tq)
            out = []
            for t in range(hp):
                m, l, acc = carry[t]
                st = lax.dot_general(k_ref[rows, t * HEAD_PAD:(t + 1) * HEAD_PAD], qs[t], _DIMS["nt"],
                                     preferred_element_type=F32) * _SCALE
                if masked:
                    st = jnp.where(_causal_mask_t(tq), st, NEG)
                m_new = jnp.maximum(m, jnp.max(st, axis=0, keepdims=True))
                alpha = jnp.exp(m - m_new)
                p = jnp.exp(st - m_new)
                l = alpha * l + jnp.sum(p, axis=0, keepdims=True)
                acc = alpha * acc + lax.dot_general(v_ref[rows, t * NOPE:(t + 1) * NOPE], p.astype(MXU_DTYPE),
                                                    _DIMS["tn"], preferred_element_type=F32)
                out.append((m_new, l, acc))
            return tuple(out)

        init = tuple((jnp.full((1, tq), NEG, F32), jnp.zeros((1, tq), F32), jnp.zeros((NOPE, tq), F32))
                     for _ in range(hp))
        carry = lax.fori_loop(0, qi, lambda j, c: step(j, c, False), init)
        carry = step(qi, carry, True)
        for t in range(hp):
            m, l, acc = carry[t]
            o_ref[:, t * NOPE:(t + 1) * NOPE] = (acc / l).T
            lse_refs[t][0] = m + jnp.log(l)

    lse_sds = jax.ShapeDtypeStruct((groups * B * nq, 1, tq), F32)
    lse_spec = pl.BlockSpec((1, 1, tq), lambda b, h, i: ((h * B + b) * nq + i, 0, 0))
    return pl.pallas_call(
        body, name="attn_fwd",
        out_shape=(jax.ShapeDtypeStruct((T, HEADS * NOPE), F32),) + (lse_sds,) * hp,
        grid=(B, groups, nq),
        in_specs=[pl.BlockSpec((tq, hp * HEAD_PAD), lambda b, h, i: (b * nq + i, h)),
                  pl.BlockSpec((S, hp * HEAD_PAD), lambda b, h, i: (b, h)),
                  pl.BlockSpec((S, hp * NOPE), lambda b, h, i: (b, h))],
        out_specs=(pl.BlockSpec((tq, hp * NOPE), lambda b, h, i: (b * nq + i, h)),) + (lse_spec,) * hp,
        compiler_params=_params(("parallel", "parallel", "arbitrary"), 4 * hp * _nbytes((S, HEAD_PAD), MXU_DTYPE)),
    )(q, k, v)


def _attn_bwd(q, k, v, do, lses, delta, B, S):
    tq = ATT_BLOCK
    nq = S // tq
    T = B * S
    hp, groups = ATT_HEADS, HEADS // ATT_HEADS

    def body(q_ref, k_ref, v_ref, do_ref, *refs):
        lse_refs, dl_refs = refs[:hp], refs[hp:2 * hp]
        dq_ref, dk_ref, dv_ref = refs[2 * hp:]
        kj = pl.program_id(2)

        @pl.when(kj == 0)
        def _():
            dq_ref[...] = jnp.zeros_like(dq_ref)

        def step(i, carry, masked):
            rows = pl.ds(pl.multiple_of(i * tq, tq), tq)
            out = []
            for t in range(hp):
                dk, dv = carry[t]
                qk_cols = slice(t * HEAD_PAD, (t + 1) * HEAD_PAD)
                v_cols = slice(t * NOPE, (t + 1) * NOPE)
                kv_, vv = k_ref[:, qk_cols], v_ref[:, v_cols]
                qv, dov = q_ref[rows, qk_cols], do_ref[rows, v_cols]
                st = lax.dot_general(kv_, qv, _DIMS["nt"], preferred_element_type=F32) * _SCALE
                p = jnp.exp(st - lse_refs[t][i])
                if masked:
                    p = jnp.where(_causal_mask_t(tq), p, 0.0)
                dv = dv + jnp.dot(p.astype(MXU_DTYPE), dov, preferred_element_type=F32)
                dpt = lax.dot_general(vv, dov, _DIMS["nt"], preferred_element_type=F32)
                ds = (p * (dpt - dl_refs[t][i]) * _SCALE).astype(MXU_DTYPE)
                dk = dk + jnp.dot(ds, qv, preferred_element_type=F32)
                dq_ref[rows, qk_cols] += lax.dot_general(ds, kv_, _DIMS["tn"], preferred_element_type=F32)
                out.append((dk, dv))
            return tuple(out)

        init = tuple((jnp.zeros((tq, HEAD_PAD), F32), jnp.zeros((tq, NOPE), F32)) for _ in range(hp))
        carry = step(kj, init, True)
        carry = lax.fori_loop(kj + 1, nq, lambda i, c: step(i, c, False), carry)
        for t in range(hp):
            dk_ref[:, t * HEAD_PAD:(t + 1) * HEAD_PAD] = carry[t][0]
            dv_ref[:, t * NOPE:(t + 1) * NOPE] = carry[t][1].astype(dv_ref.dtype)

    seq = lambda w: pl.BlockSpec((S, w), lambda b, h, j: (b, h))
    blk = lambda w: pl.BlockSpec((tq, w), lambda b, h, j: (b * nq + j, h))
    lse_spec = pl.BlockSpec((nq, 1, tq), lambda b, h, j: (h * B + b, 0, 0))
    dl_specs = [pl.BlockSpec((nq, 1, tq), lambda b, h, j, t=t: ((h * hp + t) * B + b, 0, 0)) for t in range(hp)]
    return pl.pallas_call(
        body, name="attn_bwd",
        out_shape=(jax.ShapeDtypeStruct((T, HEADS * HEAD_PAD), F32), jax.ShapeDtypeStruct((T, HEADS * HEAD_PAD), F32),
                   jax.ShapeDtypeStruct((T, HEADS * NOPE), MXU_DTYPE)),
        grid=(B, groups, nq),
        in_specs=[seq(hp * HEAD_PAD), blk(hp * HEAD_PAD), blk(hp * NOPE), seq(hp * NOPE)] + [lse_spec] * hp + dl_specs,
        out_specs=(seq(hp * HEAD_PAD), blk(hp * HEAD_PAD), blk(hp * NOPE)),
        compiler_params=_params(("parallel", "parallel", "arbitrary"), 8 * hp * _nbytes((S, HEAD_PAD), F32)),
    )(q, k, v, do, *lses, *([delta] * hp))


MIX_ROWS = 256


def _tril_weights(ws_ref, g):
    return jnp.where(_causal_mask(CHUNK), ws_ref[g], 0.0).astype(MXU_DTYPE)


def _layer_norm_stats(va):
    mu = jnp.mean(va, axis=-1, keepdims=True)
    xc = va - mu
    rs = lax.rsqrt(jnp.mean(xc * xc, axis=-1, keepdims=True) + EPS)
    return xc * rs


def _mix_specs(tr):
    zcol = lambda c: pl.BlockSpec((tr, D_MODEL), lambda i, c=c: (i, c))
    row = pl.BlockSpec((tr, D_MODEL), lambda i: (i, 0))
    vec = pl.BlockSpec((1, D_MODEL), lambda i: (0, 0))
    ws = pl.BlockSpec((A_GROUPS, CHUNK, CHUNK), lambda i: (0, 0, 0))
    bs = pl.BlockSpec((CHUNK, 128), lambda i: (0, 0))
    return zcol, row, vec, ws, bs


def _mix_fwd(z, yb, ln_g, ln_b, ws, bs_t):
    T = z.shape[0]
    tr = MIX_ROWS
    zcol, row, vec, ws_spec, bs_spec = _mix_specs(tr)

    def body(zu_ref, zv_ref, zga_ref, zgb_ref, yb_ref, g_ref, b_ref, ws_ref, bs_ref, out_ref, vn_s):
        vhat = _layer_norm_stats(_gelu(zv_ref[...]))
        vn_s[...] = (vhat * g_ref[...] + b_ref[...]).astype(MXU_DTYPE)
        for g in range(A_GROUPS):
            w = _tril_weights(ws_ref, g)
            bias = bs_ref[:, g:g + 1]
            cols = slice(g * CHUNK, (g + 1) * CHUNK)
            for c in range(tr // CHUNK):
                rows = slice(c * CHUNK, (c + 1) * CHUNK)
                mixed = jnp.dot(w, vn_s[rows, cols], preferred_element_type=F32) + bias
                ya = _gelu(zu_ref[rows, cols]) * mixed
                merged = _sigmoid(zga_ref[rows, cols]) * ya + _sigmoid(zgb_ref[rows, cols]) * yb_ref[rows, cols]
                out_ref[rows, cols] = merged.astype(MXU_DTYPE)

    return pl.pallas_call(
        body, name="mix_fwd", out_shape=jax.ShapeDtypeStruct((T, D_MODEL), MXU_DTYPE), grid=(T // tr,),
        in_specs=[zcol(0), zcol(1), zcol(2), zcol(3), row, vec, vec, ws_spec, bs_spec], out_specs=row,
        scratch_shapes=[pltpu.VMEM((tr, D_MODEL), MXU_DTYPE)],
        compiler_params=_params(("parallel",), 8 * _nbytes((tr, D_MODEL), F32)),
    )(z, z, z, z, yb, ln_g, ln_b, ws, bs_t)


def _mix_bwd(z, yb, dm, ln_g, ln_b, ws, bs_t):
    T = z.shape[0]
    tr = MIX_ROWS
    zcol, row, vec, ws_spec, bs_spec = _mix_specs(tr)

    def body(zu_ref, zv_ref, zga_ref, zgb_ref, yb_ref, dm_ref, g_ref, b_ref, ws_ref, bs_ref,
             dz_ref, dyb_ref, dl_ref, gws_ref, gbs_ref, glg_ref, glb_ref, vn_s, dvn_s):
        @pl.when(pl.program_id(0) == 0)
        def _():
            gws_ref[...] = jnp.zeros_like(gws_ref)
            gbs_ref[...] = jnp.zeros_like(gbs_ref)
            glg_ref[...] = jnp.zeros_like(glg_ref)
            glb_ref[...] = jnp.zeros_like(glb_ref)

        lane = lax.broadcasted_iota(jnp.int32, (CHUNK, 128), 1)
        va, dgelu_v = _gelu_and_grad(zv_ref[...])
        mu = jnp.mean(va, axis=-1, keepdims=True)
        xc = va - mu
        rs = lax.rsqrt(jnp.mean(xc * xc, axis=-1, keepdims=True) + EPS)
        vhat = xc * rs
        vn_s[...] = (vhat * g_ref[...] + b_ref[...]).astype(MXU_DTYPE)
        gbs_acc = jnp.zeros((CHUNK, 128), F32)
        for g in range(A_GROUPS):
            w = _tril_weights(ws_ref, g)
            bias = bs_ref[:, g:g + 1]
            cols = slice(g * CHUNK, (g + 1) * CHUNK)
            gw_acc = jnp.zeros((CHUNK, CHUNK), F32)
            for c in range(tr // CHUNK):
                rows = slice(c * CHUNK, (c + 1) * CHUNK)
                vn = vn_s[rows, cols]
                mixed = jnp.dot(w, vn, preferred_element_type=F32) + bias
                ua, dgelu_u = _gelu_and_grad(zu_ref[rows, cols])
                dmv = dm_ref[rows, cols]
                sa = _sigmoid(zga_ref[rows, cols])
                dya = dmv * sa
                dz_ref[rows, 2 * D_MODEL + g * CHUNK:2 * D_MODEL + (g + 1) * CHUNK] = (
                    dmv * (ua * mixed) * (sa * (1.0 - sa))).astype(dz_ref.dtype)
                dz_ref[rows, cols] = (dya * mixed * dgelu_u).astype(dz_ref.dtype)
                dmix = dya * ua
                gbs_acc = gbs_acc + jnp.where(lane == g, jnp.sum(dmix, axis=-1, keepdims=True), 0.0)
                dmix_b = dmix.astype(MXU_DTYPE)
                gw_acc = gw_acc + lax.dot_general(dmix_b, vn, _DIMS["nt"], preferred_element_type=F32)
                dvn_s[rows, cols] = lax.dot_general(w, dmix_b, _DIMS["tn"], preferred_element_type=F32)
            gws_ref[g] += jnp.where(_causal_mask(CHUNK), gw_acc, 0.0)
        gbs_ref[...] += gbs_acc

        dvn = dvn_s[...]
        glg_ref[...] += jnp.sum(dvn * vhat, axis=0, keepdims=True)
        glb_ref[...] += jnp.sum(dvn, axis=0, keepdims=True)
        dvh = dvn * g_ref[...]
        dva = rs * (dvh - jnp.mean(dvh, axis=-1, keepdims=True) - vhat * jnp.mean(dvh * vhat, axis=-1, keepdims=True))
        dz_ref[:, D_MODEL:2 * D_MODEL] = (dva * dgelu_v).astype(dz_ref.dtype)

        dmv = dm_ref[...]
        ybv = yb_ref[...]
        sb = _sigmoid(zgb_ref[...])
        dyb = dmv * sb
        dyb_ref[...] = dyb.astype(dyb_ref.dtype)
        dz_ref[:, 3 * D_MODEL:4 * D_MODEL] = (dmv * ybv * (sb * (1.0 - sb))).astype(dz_ref.dtype)
        dz_ref[:, 4 * D_MODEL:] = jnp.zeros((tr, LAT), dz_ref.dtype)
        prod = dyb * ybv
        sel = (lax.broadcasted_iota(jnp.int32, (HEADS, D_MODEL), 1) // NOPE
               == lax.broadcasted_iota(jnp.int32, (HEADS, D_MODEL), 0)).astype(jnp.bfloat16)
        hi = prod.astype(jnp.bfloat16)
        rest = prod - hi.astype(F32)
        mid = rest.astype(jnp.bfloat16)
        lo = (rest - mid.astype(F32)).astype(jnp.bfloat16)
        dl_ref[...] = (lax.dot_general(sel, hi, _DIMS["nt"], preferred_element_type=F32)
                       + lax.dot_general(sel, mid, _DIMS["nt"], preferred_element_type=F32)
                       + lax.dot_general(sel, lo, _DIMS["nt"], preferred_element_type=F32))

    return pl.pallas_call(
        body, name="mix_bwd",
        out_shape=(jax.ShapeDtypeStruct((T, IN_PAD), MXU_DTYPE), jax.ShapeDtypeStruct((T, D_MODEL), MXU_DTYPE),
                   jax.ShapeDtypeStruct((HEADS, T), F32), jax.ShapeDtypeStruct((A_GROUPS, CHUNK, CHUNK), F32),
                   jax.ShapeDtypeStruct((CHUNK, 128), F32), jax.ShapeDtypeStruct((1, D_MODEL), F32),
                   jax.ShapeDtypeStruct((1, D_MODEL), F32)),
        grid=(T // tr,),
        in_specs=[zcol(0), zcol(1), zcol(2), zcol(3), row, row, vec, vec, ws_spec, bs_spec],
        out_specs=(pl.BlockSpec((tr, IN_PAD), lambda i: (i, 0)), row, pl.BlockSpec((HEADS, tr), lambda i: (0, i)),
                   ws_spec, bs_spec, vec, vec),
        scratch_shapes=[pltpu.VMEM((tr, D_MODEL), MXU_DTYPE), pltpu.VMEM((tr, D_MODEL), F32)],
        compiler_params=_params(("arbitrary",), 12 * _nbytes((tr, D_MODEL), F32)),
    )(z, z, z, z, yb, dm, ln_g, ln_b, ws, bs_t)


def _lat_bwd(dz, z, dq, dk, dv, gq, gkv, wq, wkv, cos_a, sin_a, tr=256):
    T = z.shape[0]
    lat_blk = (4 * D_MODEL) // LAT

    def body(dz_in, z_ref, dq_ref, dk_ref, dv_ref, gq_ref, gkv_ref, wq_ref, wkv_ref, cos_ref, sin_ref,
             dz_ref, dqr_ref, dkv_ref, ggq_ref, ggkv_ref):
        del dz_in

        @pl.when(pl.program_id(0) == 0)
        def _():
            ggq_ref[...] = jnp.zeros_like(ggq_ref)
            ggkv_ref[...] = jnp.zeros_like(ggkv_ref)

        cos_v, sin_v = cos_ref[...], sin_ref[...]
        dkr = jnp.zeros((tr, 128), F32)
        for h in range(HEADS):
            o = h * HEAD_PAD
            dqr_ref[:, o:o + NOPE] = dq_ref[:, o:o + NOPE].astype(MXU_DTYPE)
            dqr_ref[:, o + NOPE:o + HEAD_PAD] = _rope_mix_bwd(dq_ref[:, o + NOPE:o + HEAD_PAD], cos_v, sin_v).astype(MXU_DTYPE)
            dkv_ref[:, h * NOPE:(h + 1) * NOPE] = dk_ref[:, o:o + NOPE].astype(MXU_DTYPE)
            dkr = dkr + _rope_mix_bwd(dk_ref[:, o + NOPE:o + HEAD_PAD], cos_v, sin_v)
        dkv_ref[:, HEADS * NOPE:] = dv_ref[...]
        dcqn = lax.dot_general(dqr_ref[...], wq_ref[...], _DIMS["nt"], preferred_element_type=F32)
        dckvn = lax.dot_general(dkv_ref[...], wkv_ref[...], _DIMS["nt"], preferred_element_type=F32)

        zl = z_ref[...]

        def rms_bwd(c, dn, g_ref, gg_ref):
            r = lax.rsqrt(jnp.mean(c * c, axis=-1, keepdims=True) + EPS)
            ch = c * r
            gg_ref[...] += jnp.sum(dn * ch, axis=0, keepdims=True)
            dch = dn * g_ref[...]
            return r * (dch - ch * jnp.mean(dch * ch, axis=-1, keepdims=True))

        dz_ref[:, :Q_RANK] = rms_bwd(zl[:, :Q_RANK], dcqn, gq_ref, ggq_ref).astype(dz_ref.dtype)
        dz_ref[:, Q_RANK:Q_RANK + KV_RANK] = rms_bwd(zl[:, Q_RANK:Q_RANK + KV_RANK], dckvn, gkv_ref, ggkv_ref).astype(dz_ref.dtype)
        dz_ref[:, Q_RANK + KV_RANK:] = dkr.astype(dz_ref.dtype)

    def row(w):
        return pl.BlockSpec((tr, w), lambda i: (i, 0))

    def full(a):
        return pl.BlockSpec(a.shape, lambda i: (0, 0))

    lat = pl.BlockSpec((tr, LAT), lambda i: (i, lat_blk))
    return pl.pallas_call(
        body, name="lat_bwd",
        out_shape=(jax.ShapeDtypeStruct(dz.shape, dz.dtype), jax.ShapeDtypeStruct((T, HEADS * HEAD_PAD), MXU_DTYPE),
                   jax.ShapeDtypeStruct((T, 2 * HEADS * NOPE), MXU_DTYPE), jax.ShapeDtypeStruct(gq.shape, F32),
                   jax.ShapeDtypeStruct(gkv.shape, F32)),
        grid=(T // tr,),
        in_specs=[pl.BlockSpec(memory_space=pl.ANY), lat, row(HEADS * HEAD_PAD), row(HEADS * HEAD_PAD), row(HEADS * NOPE),
                  full(gq), full(gkv), full(wq), full(wkv), row(128), row(128)],
        out_specs=(lat, row(HEADS * HEAD_PAD), row(2 * HEADS * NOPE), full(gq), full(gkv)),
        input_output_aliases={0: 0},
        compiler_params=_params(("arbitrary",), 8 * _nbytes((tr, HEADS * HEAD_PAD), F32)),
    )(dz, z, dq, dk, dv, gq, gkv, wq, wkv, cos_a, sin_a)


def _shift_down(x, k, row_idx):
    return jnp.where(row_idx >= k, pltpu.roll(x, k, 0), 0.0)


def _shift_up(x, k, row_idx, S):
    return jnp.where(row_idx < S - k, pltpu.roll(x, S - k, 0), 0.0)


def _conv(x, cw, cb, row_idx):
    return cb + cw[0:1, :] * _shift_down(x, 2, row_idx) + cw[1:2, :] * _shift_down(x, 1, row_idx) + cw[2:3, :] * x


def _gate_fwd(up3, conv_w, conv_b, B, S):
    T = B * S
    W = FF_TILE

    def body(up_ref, cw_ref, cb_ref, act_ref):
        row_idx = lax.broadcasted_iota(jnp.int32, (S, W), 0)
        gate = _conv(up_ref[0], cw_ref[0], cb_ref[0], row_idx)
        val = _conv(up_ref[1], cw_ref[1], cb_ref[1], row_idx)
        act_ref[...] = (gate * _sigmoid(gate) * val).astype(act_ref.dtype)

    return pl.pallas_call(
        body, name="gate_fwd", out_shape=jax.ShapeDtypeStruct((T, D_FF), MXU_DTYPE), grid=(B, N_FF_TILES),
        in_specs=[pl.BlockSpec((2, S, W), lambda b, j: (0, b, j)), pl.BlockSpec((2, 3, W), lambda b, j: (0, 0, j)),
                  pl.BlockSpec((2, 1, W), lambda b, j: (0, 0, j))],
        out_specs=pl.BlockSpec((S, W), lambda b, j: (b, j)),
        compiler_params=_params(("parallel", "parallel"), 12 * _nbytes((S, W), F32)),
    )(up3, conv_w, conv_b)


def _gate_bwd(up3, dact, conv_w, conv_b, B, S):
    T = B * S
    W = FF_TILE

    def body(up_ref, da_ref, cw_ref, cb_ref, dup_ref, gcw_ref, gcb_ref):
        @pl.when(pl.program_id(1) == 0)
        def _():
            gcw_ref[...] = jnp.zeros_like(gcw_ref)
            gcb_ref[...] = jnp.zeros_like(gcb_ref)

        row_idx = lax.broadcasted_iota(jnp.int32, (S, W), 0)
        gate = _conv(up_ref[0], cw_ref[0], cb_ref[0], row_idx)
        val = _conv(up_ref[1], cw_ref[1], cb_ref[1], row_idx)
        sg = _sigmoid(gate)
        da = da_ref[...]
        d_halves = (da * val * (sg * (1.0 + gate * (1.0 - sg))), da * (gate * sg))
        for half, dup in enumerate(d_halves):
            x = up_ref[half]
            cw = cw_ref[half]
            gcb_ref[half] += jnp.sum(dup, axis=0, keepdims=True)
            gcw_ref[half, 0:1, :] += jnp.sum(dup * _shift_down(x, 2, row_idx), axis=0, keepdims=True)
            gcw_ref[half, 1:2, :] += jnp.sum(dup * _shift_down(x, 1, row_idx), axis=0, keepdims=True)
            gcw_ref[half, 2:3, :] += jnp.sum(dup * x, axis=0, keepdims=True)
            dx = (cw[2:3, :] * dup + cw[1:2, :] * _shift_up(dup, 1, row_idx, S) + cw[0:1, :] * _shift_up(dup, 2, row_idx, S))
            dup_ref[half] = dx.astype(dup_ref.dtype)

    up_spec = pl.BlockSpec((2, S, W), lambda j, b: (0, b, j))
    cw_spec = pl.BlockSpec((2, 3, W), lambda j, b: (0, 0, j))
    cb_spec = pl.BlockSpec((2, 1, W), lambda j, b: (0, 0, j))
    return pl.pallas_call(
        body, name="gate_bwd",
        out_shape=(jax.ShapeDtypeStruct((2, T, D_FF), MXU_DTYPE), jax.ShapeDtypeStruct((2, 3, D_FF), F32),
                   jax.ShapeDtypeStruct((2, 1, D_FF), F32)),
        grid=(N_FF_TILES, B),
        in_specs=[up_spec, pl.BlockSpec((S, W), lambda j, b: (b, j)), cw_spec, cb_spec],
        out_specs=(up_spec, cw_spec, cb_spec),
        compiler_params=_params(("parallel", "arbitrary"), 16 * _nbytes((S, W), F32)),
    )(up3, dact, conv_w, conv_b)


def _final(x2, tgt, g, tr=512):
    T, D = x2.shape

    def body(x_ref, t_ref, g_ref, dx_ref, loss_ref, gg_ref):
        @pl.when(pl.program_id(0) == 0)
        def _():
            loss_ref[...] = jnp.zeros_like(loss_ref)
            gg_ref[...] = jnp.zeros_like(gg_ref)

        xv = x_ref[...]
        gv = g_ref[...]
        r = lax.rsqrt(jnp.mean(xv * xv, axis=-1, keepdims=True) + EPS)
        xn = xv * r
        err = xn * gv - t_ref[...]
        loss_ref[...] += 0.5 * jnp.sum(jnp.mean(err * err, axis=-1, keepdims=True), axis=0, keepdims=True)
        dy = err * (1.0 / D)
        gg_ref[...] += jnp.sum(dy * xn, axis=0, keepdims=True)
        dxn = dy * gv
        dx_ref[...] = r * (dxn - xn * jnp.mean(dxn * xn, axis=-1, keepdims=True))

    row = pl.BlockSpec((tr, D), lambda i: (i, 0))
    vec = pl.BlockSpec((1, D), lambda i: (0, 0))
    return pl.pallas_call(
        body, name="final_loss",
        out_shape=(jax.ShapeDtypeStruct((T, D), F32), jax.ShapeDtypeStruct((1, 128), F32), jax.ShapeDtypeStruct((1, D), F32)),
        grid=(T // tr,), in_specs=[row, row, vec],
        out_specs=(row, pl.BlockSpec((1, 128), lambda i: (0, 0)), vec),
        compiler_params=_params(("arbitrary",), 6 * _nbytes((tr, D), F32)),
    )(x2, tgt, g)


def _sum_slabs(parts, name, tr):
    rows, cols = parts[0].shape
    n = len(parts)

    def body(*refs):
        acc = refs[0][...]
        for r in refs[1:n]:
            acc = acc + r[...]
        refs[n][...] = acc

    blk = pl.BlockSpec((tr, cols), lambda i: (i, 0))
    return pl.pallas_call(
        body, name=name, out_shape=jax.ShapeDtypeStruct((rows, cols), F32), grid=(rows // tr,),
        in_specs=[blk] * n, out_specs=blk,
        compiler_params=_params(("parallel",), (n + 1) * _nbytes((tr, cols), F32)),
    )(*parts)


def _adamw(w, g, m, v, name):
    rows, cols = w.shape
    tr = rows
    for cand in (256, 128, 64, 32, 16, 8):
        if rows % cand == 0:
            tr = cand
            break
    c1 = 1.0 - ADAM_B1 ** ADAM_STEP
    c2 = 1.0 - ADAM_B2 ** ADAM_STEP

    def body(w_ref, g_ref, m_ref, v_ref, d_ref, nm_ref, nv_ref):
        gv = g_ref[...]
        nm = ADAM_B1 * m_ref[...] + (1.0 - ADAM_B1) * gv
        nv = ADAM_B2 * v_ref[...] + (1.0 - ADAM_B2) * (gv * gv)
        nm_ref[...] = nm
        nv_ref[...] = nv
        d_ref[...] = -ADAM_LR * ((nm / c1) / (jnp.sqrt(nv / c2) + ADAM_EPS) + ADAM_WD * w_ref[...])

    blk = pl.BlockSpec((tr, cols), lambda i: (i, 0))
    sds = jax.ShapeDtypeStruct((rows, cols), F32)
    return pl.pallas_call(
        body, name=name, out_shape=(sds, sds, sds), grid=(rows // tr,), in_specs=[blk] * 4, out_specs=(blk, blk, blk),
        compiler_params=_params(("parallel",), 7 * _nbytes((tr, cols), F32)),
    )(w, g, m, v)


_ANY = pl.BlockSpec(memory_space=pl.ANY)


def _place():
    x, y, c = lax.axis_index("x"), lax.axis_index("y"), lax.axis_index("c")
    chips = [(1 - x, y), (x, 1 - y), (1 - x, 1 - y)]
    return x, y, c, chips


def _gather_weights(shards, conv_w):
    n = len(shards)

    def body(*refs):
        ins, cw_ref, outs, cwo_ref = refs[:n], refs[n], refs[n + 1:2 * n + 1], refs[2 * n + 1]
        send, recv, fsend, frecv, csend, crecv, osend, orecv = refs[2 * n + 2:]
        x, y, c, chips = _place()
        me = 2 * x + y
        sources = list(ins) + [cw_ref]
        targets = list(outs) + [cwo_ref]
        first, passed = [], []
        for w in range(n + 1):
            first.append(pltpu.make_async_remote_copy(
                src_ref=sources[w], dst_ref=targets[w].at[me], send_sem=osend.at[w], recv_sem=orecv.at[w],
                device_id=(x, y, 1 - c), device_id_type=MESH))
        for w in range(n):
            for j, (px, py) in enumerate(chips):
                first.append(pltpu.make_async_remote_copy(
                    src_ref=ins[w].at[c], dst_ref=outs[w].at[me, c], send_sem=send.at[3 * w + j],
                    recv_sem=recv.at[3 * w + j], device_id=(px, py, c), device_id_type=MESH))
        for j, (px, py) in enumerate(chips):
            first.append(pltpu.make_async_remote_copy(
                src_ref=cw_ref, dst_ref=cwo_ref.at[me], send_sem=csend.at[j], recv_sem=crecv.at[j],
                device_id=(px, py, c), device_id_type=MESH))
        for cp in first:
            cp.start()
        for w in range(n):
            for j, (px, py) in enumerate(chips):
                landed = outs[w].at[2 * px + py, c]
                pltpu.make_async_remote_copy(src_ref=landed, dst_ref=landed, send_sem=send.at[3 * w + j],
                                             recv_sem=recv.at[3 * w + j], device_id=(px, py, c),
                                             device_id_type=MESH).wait_recv()
                fw = pltpu.make_async_remote_copy(src_ref=landed, dst_ref=landed, send_sem=fsend.at[3 * w + j],
                                                  recv_sem=frecv.at[3 * w + j], device_id=(x, y, 1 - c),
                                                  device_id_type=MESH)
                fw.start()
                passed.append(fw)
        for w in range(n):
            for j, (px, py) in enumerate(chips):
                other = outs[w].at[2 * px + py, 1 - c]
                pltpu.make_async_remote_copy(src_ref=other, dst_ref=other, send_sem=fsend.at[3 * w + j],
                                             recv_sem=frecv.at[3 * w + j], device_id=(x, y, 1 - c),
                                             device_id_type=MESH).wait_recv()
        for j, (px, py) in enumerate(chips):
            pltpu.make_async_remote_copy(src_ref=cw_ref, dst_ref=cwo_ref.at[2 * px + py], send_sem=csend.at[j],
                                         recv_sem=crecv.at[j], device_id=(px, py, c), device_id_type=MESH).wait_recv()
        for w in range(n + 1):
            own = targets[w].at[me]
            pltpu.make_async_remote_copy(src_ref=own, dst_ref=own, send_sem=osend.at[w], recv_sem=orecv.at[w],
                                         device_id=(x, y, 1 - c), device_id_type=MESH).wait_recv()
        for cp in first + passed:
            cp.wait_send()

    dma = lambda k: pltpu.SemaphoreType.DMA((k,))
    return pl.pallas_call(
        body, name="gather_weights",
        out_shape=tuple(jax.ShapeDtypeStruct((N_CHIPS,) + s.shape, s.dtype) for s in list(shards) + [conv_w]),
        in_specs=[_ANY] * (n + 1), out_specs=tuple([_ANY] * (n + 1)),
        scratch_shapes=[dma(3 * n), dma(3 * n), dma(3 * n), dma(3 * n), dma(3), dma(3), dma(n + 1), dma(n + 1)],
    )(*shards, conv_w)


def _swap_halves(gs):
    n = len(gs)

    def body(*refs):
        ins, outs, send, recv = refs[:n], refs[n:2 * n], refs[2 * n], refs[2 * n + 1]
        x, y, c, _ = _place()
        cps = []
        for w in range(n):
            cps.append(pltpu.make_async_remote_copy(
                src_ref=ins[w].at[:, 1 - c], dst_ref=outs[w], send_sem=send.at[w], recv_sem=recv.at[w],
                device_id=(x, y, 1 - c), device_id_type=MESH))
        for cp in cps:
            cp.start()
        for cp in cps:
            cp.wait()

    return pl.pallas_call(
        body, name="grad_swap_halves",
        out_shape=tuple(jax.ShapeDtypeStruct((g.shape[0],) + g.shape[2:], g.dtype) for g in gs),
        in_specs=[_ANY] * n, out_specs=tuple([_ANY] * n),
        scratch_shapes=[pltpu.SemaphoreType.DMA((n,)), pltpu.SemaphoreType.DMA((n,))],
    )(*gs)


GRAD_PAYLOAD = jnp.bfloat16


def _pair_sum(gs, gots):
    n = len(gs)
    core = lax.axis_index("c").astype(jnp.int32).reshape(1)

    def body(core_ref, *refs):
        del core_ref
        for w in range(n):
            refs[2 * n + w][...] = (refs[w][...] + refs[n + w][...]).astype(GRAD_PAYLOAD)

    in_specs, out_specs, out_shape, nbytes = [], [], [], 0
    for g in gs:
        q = g.shape[1] // 4
        in_specs.append(pl.BlockSpec((1, q, g.shape[2]), lambda s, r, core: (s, 2 * core[0] + r, 0)))
        nbytes += 3 * _nbytes((q, g.shape[2]), F32)
    for g in gs:
        q = g.shape[1] // 4
        in_specs.append(pl.BlockSpec((1, q, g.shape[2]), lambda s, r, core: (s, r, 0)))
        out_specs.append(pl.BlockSpec((1, q, g.shape[2]), lambda s, r, core: (s, r, 0)))
        out_shape.append(jax.ShapeDtypeStruct((g.shape[0], g.shape[1] // 2, g.shape[2]), GRAD_PAYLOAD))
    return pl.pallas_call(
        body, name="grad_pair_sum", out_shape=tuple(out_shape),
        grid_spec=pltpu.PrefetchScalarGridSpec(num_scalar_prefetch=1, grid=(N_CHIPS, 2), in_specs=in_specs,
                                               out_specs=tuple(out_specs)),
        compiler_params=_params(("parallel", "parallel"), nbytes),
    )(core, *gs, *gots)


def _scatter_to_chips(ps):
    n = len(ps)

    def body(*refs):
        ins, outs, send, recv = refs[:n], refs[n:2 * n], refs[2 * n], refs[2 * n + 1]
        x, y, c, chips = _place()
        me = 2 * x + y
        cps = []
        for w in range(n):
            for j, (px, py) in enumerate(chips):
                cps.append(pltpu.make_async_remote_copy(
                    src_ref=ins[w].at[2 * px + py], dst_ref=outs[w].at[me], send_sem=send.at[3 * w + j],
                    recv_sem=recv.at[3 * w + j], device_id=(px, py, c), device_id_type=MESH))
        for cp in cps:
            cp.start()
        for w in range(n):
            for j, (px, py) in enumerate(chips):
                slot = outs[w].at[2 * px + py]
                pltpu.make_async_remote_copy(src_ref=slot, dst_ref=slot, send_sem=send.at[3 * w + j],
                                             recv_sem=recv.at[3 * w + j], device_id=(px, py, c),
                                             device_id_type=MESH).wait_recv()
        for cp in cps:
            cp.wait_send()

    dma = lambda k: pltpu.SemaphoreType.DMA((k,))
    return pl.pallas_call(
        body, name="grad_scatter_chips", out_shape=tuple(jax.ShapeDtypeStruct(p.shape, p.dtype) for p in ps),
        in_specs=[_ANY] * n, out_specs=tuple([_ANY] * n), scratch_shapes=[dma(3 * n), dma(3 * n)],
    )(*ps)


def _chip_sum(ps, landed):
    n = len(ps)
    x, y, c = lax.axis_index("x"), lax.axis_index("y"), lax.axis_index("c")
    where = jnp.stack([2 * x + y, 2 * (1 - x) + y, 2 * x + (1 - y), 2 * (1 - x) + (1 - y), c]).astype(jnp.int32)

    def body(where_ref, *refs):
        del where_ref
        for w in range(n):
            terms = [refs[4 * w + t][...].astype(F32) for t in range(4)]
            refs[4 * n + w][...] = ((terms[0] + terms[1]) + terms[2]) + terms[3]

    in_specs, out_specs, out_shape, args, nbytes = [], [], [], [], 0
    for p, a in zip(ps, landed):
        q = a.shape[1] // 2
        blk = (1, q, a.shape[2])
        in_specs.append(pl.BlockSpec(blk, lambda r, where: (where[0], r, 0)))
        args.append(p)
        for t in (1, 2, 3):
            in_specs.append(pl.BlockSpec(blk, lambda r, where, t=t: (where[t], r, 0)))
            args.append(a)
        out_specs.append(pl.BlockSpec(blk, lambda r, where: (where[4], r, 0)))
        out_shape.append(jax.ShapeDtypeStruct((2,) + a.shape[1:], F32))
        nbytes += 4 * _nbytes(blk, F32)
    return pl.pallas_call(
        body, name="grad_chip_sum", out_shape=tuple(out_shape),
        grid_spec=pltpu.PrefetchScalarGridSpec(num_scalar_prefetch=1, grid=(2,), in_specs=in_specs,
                                               out_specs=tuple(out_specs)),
        compiler_params=_params(("parallel",), nbytes),
    )(where, *args)


def _join_halves(ss):
    n = len(ss)

    def body(*refs):
        outs, send, recv = refs[n:2 * n], refs[2 * n], refs[2 * n + 1]
        x, y, c, _ = _place()
        cps = []
        for w in range(n):
            cps.append(pltpu.make_async_remote_copy(
                src_ref=outs[w].at[c], dst_ref=outs[w].at[c], send_sem=send.at[w], recv_sem=recv.at[w],
                device_id=(x, y, 1 - c), device_id_type=MESH))
        for cp in cps:
            cp.start()
        for w in range(n):
            got = outs[w].at[1 - c]
            pltpu.make_async_remote_copy(src_ref=got, dst_ref=got, send_sem=send.at[w], recv_sem=recv.at[w],
                                         device_id=(x, y, 1 - c), device_id_type=MESH).wait_recv()
        for cp in cps:
            cp.wait_send()

    dma = lambda k: pltpu.SemaphoreType.DMA((k,))
    return pl.pallas_call(
        body, name="grad_join_halves",
        out_shape=tuple(jax.ShapeDtypeStruct(s.shape, s.dtype) for s in ss),
        in_specs=[_ANY] * n, out_specs=tuple([_ANY] * n), input_output_aliases={w: w for w in range(n)},
        scratch_shapes=[dma(n), dma(n)],
    )(*ss)


def _gather_all(p):
    rows, cols = p.shape

    def body(p_ref, out_ref, send, recv):
        x, y, c, _ = _place()
        me = 4 * x + 2 * y + c
        flips = [(fx, fy, fc) for fx in (0, 1) for fy in (0, 1) for fc in (0, 1)][1:]
        peers = [(x ^ fx, y ^ fy, c ^ fc) for fx, fy, fc in flips]
        cps = [pltpu.make_async_remote_copy(src_ref=p_ref, dst_ref=out_ref.at[me], send_sem=send.at[j], recv_sem=recv.at[j],
                                            device_id=peer, device_id_type=MESH) for j, peer in enumerate(peers)]
        for cp in cps:
            cp.start()
        for j, (px, py, pc) in enumerate(peers):
            slot = out_ref.at[4 * px + 2 * py + pc]
            pltpu.make_async_remote_copy(src_ref=slot, dst_ref=slot, send_sem=send.at[j], recv_sem=recv.at[j],
                                         device_id=(px, py, pc), device_id_type=MESH).wait_recv()
        for cp in cps:
            cp.wait_send()

    dma7 = pltpu.SemaphoreType.DMA((7,))
    return pl.pallas_call(
        body, name="small_grads_gather", out_shape=jax.ShapeDtypeStruct((8, rows, cols), p.dtype),
        in_specs=[_ANY], out_specs=_ANY, scratch_shapes=[dma7, dma7],
    )(p)


def _rot_cols(w):
    a, b = jnp.split(w, 2, axis=-1)
    return jnp.concatenate([-b, a], axis=-1)


def _rot_cols_t(g):
    a, b = jnp.split(g, 2, axis=-1)
    return jnp.concatenate([b, -a], axis=-1)


def _cols_from_chips(a):
    n, r, cs = a.shape
    return jnp.transpose(a, (1, 0, 2)).reshape(r, n * cs)


def _cols_to_chips(a):
    r, cc = a.shape
    return jnp.transpose(a.reshape(r, N_CHIPS, cc // N_CHIPS), (1, 0, 2))


def _conv_w_split(cw):
    return jnp.swapaxes(cw.reshape(3, 2, D_FF), 0, 1)


def _conv_w_join(g):
    return jnp.swapaxes(g, 0, 1).reshape(3, 2 * D_FF)


_SEG =(D_MODEL, 2 * D_MODEL, 2 * D_MODEL + Q_RANK, 2 * D_MODEL + Q_RANK + KV_RANK, 2 * D_MODEL + Q_RANK + KV_RANK + ROPE,
        3 * D_MODEL + Q_RANK + KV_RANK + ROPE)


def _w_in_to_pad(w):
    u, v, cq, ckv, kr, ga, gb = jnp.split(w, _SEG, axis=1)
    return jnp.concatenate([u, v, ga, gb, cq, ckv, kr, _rot_cols(kr)], axis=1)


def _w_in_from_pad(g):
    u, v, ga, gb, cq, ckv, kr, krr = jnp.split(
        g, (D_MODEL, 2 * D_MODEL, 3 * D_MODEL, 4 * D_MODEL, 4 * D_MODEL + Q_RANK, 4 * D_MODEL + Q_RANK + KV_RANK,
            4 * D_MODEL + Q_RANK + KV_RANK + ROPE), axis=1)
    return jnp.concatenate([u, v, cq, ckv, kr + _rot_cols_t(krr), ga, gb], axis=1)


def _w_uq_to_pad(w):
    t = w.reshape(Q_RANK, HEADS, QK_DIM)
    nope, rope = t[..., :NOPE], t[..., NOPE:]
    return jnp.concatenate([nope, rope, _rot_cols(rope)], axis=-1).reshape(Q_RANK, HEADS * HEAD_PAD)


def _w_uq_from_pad(g):
    t = g.reshape(Q_RANK, HEADS, HEAD_PAD)
    nope, rope, rot = t[..., :NOPE], t[..., NOPE:QK_DIM], t[..., QK_DIM:]
    return jnp.concatenate([nope, rope + _rot_cols_t(rot)], axis=-1).reshape(Q_RANK, HEADS * QK_DIM)


def _w_ukv_to_pad(w):
    t = w.reshape(KV_RANK, HEADS, 2, NOPE)
    return jnp.swapaxes(t, 1, 2).reshape(KV_RANK, 2 * HEADS * NOPE)


def _w_ukv_from_pad(g):
    t = g.reshape(KV_RANK, 2, HEADS, NOPE)
    return jnp.swapaxes(t, 1, 2).reshape(KV_RANK, 2 * HEADS * NOPE)


def _rope_tables(positions):
    inv_freq = 1.0 / (ROPE_THETA ** (jnp.arange(0, ROPE, 2, dtype=F32) / ROPE))
    ang = positions.astype(F32).reshape(-1, 1) * inv_freq
    cos, sin = jnp.cos(ang), jnp.sin(ang)
    zero = jnp.zeros((ang.shape[0], 64), F32)
    return jnp.concatenate([cos, cos, zero], axis=1), jnp.concatenate([sin, sin, zero], axis=1)


_BIG = ("w_in", "w_uq", "w_ukv", "w_out", "w_up", "w_down")
UP_SHARD = 2 * D_FF // N_CHIPS


def _local_step(x, positions, tgt, wts):
    B, S, D = x.shape
    T = B * S
    xf = x.reshape(T, D)
    cos_a, sin_a = _rope_tables(positions)
    bs_t = jnp.pad(wts["a_spatial_b"].T, ((0, 0), (0, 128 - A_GROUPS)))

    h = _rms_fwd(xf, wts["mix_norm"], "norm1_fwd")
    z = _mm(h, wts["w_in"], "nn", "in_proj", tm=512, tn=1536, tk=D)
    q, k, v, cqn, ckvn = _lat_fwd(z, wts["q_a_norm"], wts["kv_a_norm"], wts["w_q"], wts["w_kv"], cos_a, sin_a)
    yb, *lses = _attn_fwd(q, k, v, B, S)
    merged = _mix_fwd(z, yb, wts["a_v_norm_g"], wts["a_v_norm_b"], wts["a_spatial_w"], bs_t)
    x1 = _mm(merged, wts["w_out"], "nn", "out_proj", tm=512, tn=D, tk=D, add=xf)
    h2 = _rms_fwd(x1, wts["ffn_norm"], "norm2_fwd")
    up_pre = _mm(h2, wts["w_up"], "nn", "up_proj", tm=512, tn=UP_SHARD, tk=D, dims=(T, 2 * D_FF, D),
                 b_spec=pl.BlockSpec((None, D, UP_SHARD), lambda i, j, k: (j, 0, 0)),
                 o_spec=pl.BlockSpec((None, 512, UP_SHARD), lambda i, j, k: (j // 2, i, j % 2)), out_shape=(2, T, D_FF))
    act = _gate_fwd(up_pre, wts["conv_w"], wts["conv_b"], B, S)
    x2 = _mm(act, wts["w_down"], "nn", "down_proj", tm=512, tn=D, tk=1408, add=x1)
    dx2, loss_row, g_final = _final(x2, tgt.reshape(T, D), wts["final_norm"])

    g = {"final_norm": g_final}
    dact = _mm(dx2, wts["w_down"], "nt", "down_proj_dx", tm=512, tn=1408, tk=D)
    g["w_down"] = _mm(act, dx2, "tn", "down_proj_dw", tm=1408, tn=D, tk=512)
    dup, g["conv_w"], g["conv_b"] = _gate_bwd(up_pre, dact, wts["conv_w"], wts["conv_b"], B, S)
    dh2 = _mm(dup, wts["w_up"], "nt", "up_proj_dx", tm=512, tn=D, tk=UP_SHARD, dims=(T, D, 2 * D_FF),
              a_spec=pl.BlockSpec((None, 512, UP_SHARD), lambda i, j, k: (k // 2, i, k % 2)),
              b_spec=pl.BlockSpec((None, D, UP_SHARD), lambda i, j, k: (k, 0, 0)))
    g["w_up"] = _mm(h2, dup, "tn", "up_proj_dw", tm=D, tn=UP_SHARD, tk=512, dims=(D, 2 * D_FF, T),
                    b_spec=pl.BlockSpec((None, 512, UP_SHARD), lambda i, j, k: (j // 2, k, j % 2)),
                    o_spec=pl.BlockSpec((None, D, UP_SHARD), lambda i, j, k: (j, 0, 0)), out_shape=(N_CHIPS, D, UP_SHARD))
    dx1, g["ffn_norm"] = _rms_bwd(x1, wts["ffn_norm"], dh2, dx2, "norm2_bwd")
    dm = _mm(dx1, wts["w_out"], "nt", "out_proj_dx", tm=512, tn=D, tk=D)
    g["w_out"] = _mm(merged, dx1, "tn", "out_proj_dw", tm=D, tn=D, tk=512)
    dz, dyb, dl, g["a_spatial_w"], gbs, g["a_v_norm_g"], g["a_v_norm_b"] = _mix_bwd(
        z, yb, dm, wts["a_v_norm_g"], wts["a_v_norm_b"], wts["a_spatial_w"], bs_t)
    g["a_spatial_b"] = gbs[:, :A_GROUPS].T
    delta = dl.reshape(HEADS * T // ATT_BLOCK, 1, ATT_BLOCK)
    dq, dk, dv = _attn_bwd(q, k, v, dyb, lses, delta, B, S)
    dz, dq_raw, dkv, g["q_a_norm"], g["kv_a_norm"] = _lat_bwd(
        dz, z, dq, dk, dv, wts["q_a_norm"], wts["kv_a_norm"], wts["w_q"], wts["w_kv"], cos_a, sin_a)
    g["w_q"] = _mm(cqn, dq_raw, "tn", "q_proj_dw", tm=Q_RANK, tn=HEADS * HEAD_PAD, tk=512)
    g["w_kv"] = _mm(ckvn, dkv, "tn", "kv_proj_dw", tm=KV_RANK, tn=2 * HEADS * NOPE, tk=512)
    dh = _mm(dz, wts["w_in"], "nt", "in_proj_dx", tm=512, tn=D, tk=1536)
    g["w_in"] = _mm(h, dz, "tn", "in_proj_dw", tm=D, tn=1536, tk=512)
    dx, g["mix_norm"] = _rms_bwd(xf, wts["mix_norm"], dh, dx1, "norm1_bwd")
    return loss_row[0, 0], dx.reshape(B, S, D), g


_SMALL = (("mix_norm", (1, D_MODEL)), ("a_v_norm_g", (1, D_MODEL)), ("a_v_norm_b", (1, D_MODEL)),
          ("a_spatial_w", (A_GROUPS * CHUNK, CHUNK)), ("a_spatial_b", (1, A_GROUPS * CHUNK)), ("q_a_norm", (1, Q_RANK)),
          ("kv_a_norm", (1, KV_RANK)), ("ffn_norm", (1, D_MODEL)), ("conv_b", (1, 2 * D_FF)), ("final_norm", (1, D_MODEL)),
          ("conv_w", (3, 2 * D_FF)))
_SMALL_ROWS = -(-sum(math.prod(s) for _, s in _SMALL) // (128 * 8)) * 8


def kernel(x, positions, mix_norm, w_in, a_v_norm_g, a_v_norm_b, a_spatial_w, a_spatial_b, q_a_norm, w_uq, kv_a_norm, w_ukv, w_out, ffn_norm, w_up, conv_w, conv_b, w_down, final_norm, loss_target, m_mix_norm, m_w_in, m_a_v_norm_g, m_a_v_norm_b, m_a_spatial_w, m_a_spatial_b, m_q_a_norm, m_w_uq, m_kv_a_norm, m_w_ukv, m_w_out, m_ffn_norm, m_w_up, m_conv_w, m_conv_b, m_w_down, m_final_norm, v_mix_norm, v_w_in, v_a_v_norm_g, v_a_v_norm_b, v_a_spatial_w, v_a_spatial_b, v_q_a_norm, v_w_uq, v_kv_a_norm, v_w_ukv, v_w_out, v_ffn_norm, v_w_up, v_conv_w, v_conv_b, v_w_down, v_final_norm):
    weights = dict(mix_norm=mix_norm, w_in=w_in, a_v_norm_g=a_v_norm_g, a_v_norm_b=a_v_norm_b, a_spatial_w=a_spatial_w,
                   a_spatial_b=a_spatial_b, q_a_norm=q_a_norm, w_uq=w_uq, kv_a_norm=kv_a_norm, w_ukv=w_ukv, w_out=w_out,
                   ffn_norm=ffn_norm, w_up=w_up, conv_w=conv_w, conv_b=conv_b, w_down=w_down, final_norm=final_norm)
    m_in = dict(mix_norm=m_mix_norm, w_in=m_w_in, a_v_norm_g=m_a_v_norm_g, a_v_norm_b=m_a_v_norm_b,
                a_spatial_w=m_a_spatial_w, a_spatial_b=m_a_spatial_b, q_a_norm=m_q_a_norm, w_uq=m_w_uq,
                kv_a_norm=m_kv_a_norm, w_ukv=m_w_ukv, w_out=m_w_out, ffn_norm=m_ffn_norm, w_up=m_w_up, conv_w=m_conv_w,
                conv_b=m_conv_b, w_down=m_w_down, final_norm=m_final_norm)
    v_in = dict(mix_norm=v_mix_norm, w_in=v_w_in, a_v_norm_g=v_a_v_norm_g, a_v_norm_b=v_a_v_norm_b,
                a_spatial_w=v_a_spatial_w, a_spatial_b=v_a_spatial_b, q_a_norm=v_q_a_norm, w_uq=v_w_uq,
                kv_a_norm=v_kv_a_norm, w_ukv=v_w_ukv, w_out=v_w_out, ffn_norm=v_ffn_norm, w_up=v_w_up, conv_w=v_conv_w,
                conv_b=v_conv_b, w_down=v_w_down, final_norm=v_final_norm)
    names = list(weights)
    chip = 2 * lax.axis_index("x") + lax.axis_index("y")

    def halves(a):
        return a.reshape(a.shape[:-2] + (2, a.shape[-2] // 2, a.shape[-1]))

    def whole(a):
        return a.reshape(a.shape[:-3] + (2 * a.shape[-2], a.shape[-1]))

    *gathered, cw_all = _gather_weights([halves(weights[n][0].astype(MXU_DTYPE)) for n in _BIG], conv_w[0])
    w_in_sh, w_uq_sh, w_ukv_sh, w_out_sh, w_up_sh, w_down_sh = (whole(a) for a in gathered)
    wts = dict(
        mix_norm=mix_norm, a_v_norm_g=a_v_norm_g, a_v_norm_b=a_v_norm_b, a_spatial_w=a_spatial_w[0],
        a_spatial_b=a_spatial_b[0], q_a_norm=q_a_norm, kv_a_norm=kv_a_norm, ffn_norm=ffn_norm,
        final_norm=final_norm.reshape(1, D_MODEL),
        w_in=_w_in_to_pad(_cols_from_chips(w_in_sh)), w_q=_w_uq_to_pad(_cols_from_chips(w_uq_sh)),
        w_kv=_w_ukv_to_pad(_cols_from_chips(w_ukv_sh)), w_out=w_out_sh.reshape(D_MODEL, D_MODEL), w_up=w_up_sh,
        w_down=w_down_sh.reshape(D_FF, D_MODEL), conv_w=_conv_w_split(_cols_from_chips(cw_all)),
        conv_b=conv_b.reshape(2, 1, D_FF))

    loss_part, grad_x, g = _local_step(x, positions, loss_target, wts)
    loss = lax.psum(loss_part, ("x", "y", "c"))

    slabs = [_cols_to_chips(_w_in_from_pad(g["w_in"])), _cols_to_chips(_w_uq_from_pad(g["w_q"])),
             _cols_to_chips(_w_ukv_from_pad(g["w_kv"])), g["w_out"].reshape(N_CHIPS, D_MODEL // N_CHIPS, D_MODEL),
             g["w_up"], g["w_down"].reshape(N_CHIPS, D_FF // N_CHIPS, D_MODEL)]
    pair_sums = _pair_sum(slabs, _swap_halves([halves(s) for s in slabs]))
    landed = _scatter_to_chips(pair_sums)
    g_big = dict(zip(_BIG, (whole(a) for a in _join_halves(_chip_sum(pair_sums, landed)))))

    g_small_parts = dict(g)
    g_small_parts["conv_w"] = _conv_w_join(g["conv_w"])
    g_small_parts["conv_b"] = g["conv_b"].reshape(1, 2 * D_FF)
    flat = jnp.concatenate([g_small_parts[n].reshape(-1) for n, _ in _SMALL])
    flat = jnp.pad(flat, (0, _SMALL_ROWS * 128 - flat.shape[0])).reshape(_SMALL_ROWS, 128)
    device = 2 * chip + lax.axis_index("c")
    everyone = lax.dynamic_update_slice(_gather_all(flat), flat[None], (device, 0, 0))
    total = _sum_slabs([everyone[j] for j in range(8)], "small_grads_sum", tr=_SMALL_ROWS).reshape(-1)
    g_small, o = {}, 0
    for n, shp in _SMALL:
        g_small[n] = total[o:o + math.prod(shp)].reshape(shp)
        o += math.prod(shp)
    g_small["conv_w"] = lax.dynamic_slice_in_dim(g_small["conv_w"], chip * 1408, 1408, axis=1)

    grads, deltas, new_m, new_v = {}, {}, {}, {}
    for n in names:
        w = weights[n]
        g2 = g_big[n] if n in g_big else g_small[n]
        shape2 = g2.shape
        d, nm, nv = _adamw(w.reshape(shape2), g2, m_in[n].reshape(shape2), v_in[n].reshape(shape2), "adamw_" + n)
        grads[n], deltas[n], new_m[n], new_v[n] = (t.reshape(w.shape) for t in (g2, d, nm, nv))
    return (loss, grad_x, *[grads[n] for n in names], *[deltas[n] for n in names], *[new_m[n] for n in names],
            *[new_v[n] for n in names])
```

```python
import functools
import math

import jax
import jax.numpy as jnp
from jax import lax
from jax.experimental import pallas as pl
from jax.experimental.pallas import tpu as pltpu

F32 = jnp.float32
MXU_DTYPE = jnp.bfloat16
MESH = pl.DeviceIdType.MESH

D_MODEL = 1024
EPS = 1e-6
A_GROUPS = 8
CHUNK = 128
HEADS = 8
NOPE = 128
ROPE = 64
QK_DIM = NOPE + ROPE
HEAD_PAD = 256
Q_RANK = 256
KV_RANK = 128
ROPE_THETA = 10000.0
D_FF = 2816
FF_TILE = 256
N_FF_TILES = D_FF // FF_TILE
LAT = 512
IN_PAD = 4 * D_MODEL + LAT
N_CHIPS = 4
ADAM_LR, ADAM_B1, ADAM_B2, ADAM_EPS, ADAM_WD, ADAM_STEP = 0.001, 0.9, 0.999, 1e-08, 0.01, 10

VMEM_CAP_V7X = 64 * 1024 * 1024
NEG = -1e30


def _params(sem, nbytes):
    limit = int(min(VMEM_CAP_V7X - (8 << 20), max(32 << 20, 3 * nbytes)))
    return pltpu.CompilerParams(dimension_semantics=sem, vmem_limit_bytes=limit)


def _nbytes(shape, dtype):
    return math.prod(shape) * jnp.dtype(dtype).itemsize


_DIMS = {"nn": (((1,), (0,)), ((), ())), "nt": (((1,), (1,)), ((), ())), "tn": (((0,), (0,)), ((), ()))}


def _mm(a, b, mode, name, *, tm, tn, tk, out_dtype=F32, add=None, dims=None, a_spec=None, b_spec=None,
        o_spec=None, out_shape=None):
    if dims is None:
        if mode == "nn":
            (M, K), (_, N) = a.shape, b.shape
        elif mode == "nt":
            (M, K), (N, _) = a.shape, b.shape
        else:
            (K, M), (_, N) = a.shape, b.shape
    else:
        M, N, K = dims
    a_blk = (tk, tm) if mode == "tn" else (tm, tk)
    b_blk = (tn, tk) if mode == "nt" else (tk, tn)
    if a_spec is None:
        a_spec = pl.BlockSpec(a_blk, (lambda i, j, k: (k, i)) if mode == "tn" else (lambda i, j, k: (i, k)))
    if b_spec is None:
        b_spec = pl.BlockSpec(b_blk, (lambda i, j, k: (j, k)) if mode == "nt" else (lambda i, j, k: (k, j)))
    if o_spec is None:
        o_spec = pl.BlockSpec((tm, tn), lambda i, j, k: (i, j))
    if out_shape is None:
        out_shape = (M, N)
    assert M % tm == 0 and N % tn == 0 and K % tk == 0, (name, M, N, K, tm, tn, tk)
    nk = K // tk
    contract = _DIMS[mode]
    has_add = add is not None

    def body(*refs):
        if has_add:
            a_ref, b_ref, add_ref, o_ref, acc = refs
        else:
            a_ref, b_ref, o_ref, acc = refs
        k = pl.program_id(2)

        @pl.when(k == 0)
        def _():
            acc[...] = jnp.zeros_like(acc)

        acc[...] += lax.dot_general(a_ref[...].astype(MXU_DTYPE), b_ref[...].astype(MXU_DTYPE), contract,
                                    preferred_element_type=F32)

        @pl.when(k == nk - 1)
        def _():
            r = acc[...]
            if has_add:
                r = r + add_ref[...]
            o_ref[...] = r.astype(out_dtype)

    in_specs = [a_spec, b_spec]
    args = [a, b]
    nbytes = _nbytes(a_blk, a.dtype) + _nbytes(b_blk, b.dtype) + 3 * _nbytes((tm, tn), F32)
    if has_add:
        in_specs.append(pl.BlockSpec((tm, tn), lambda i, j, k: (i, j)))
        args.append(add)
        nbytes += _nbytes((tm, tn), F32)
    return pl.pallas_call(
        body, name=name, out_shape=jax.ShapeDtypeStruct(out_shape, out_dtype),
        grid=(M // tm, N // tn, nk), in_specs=in_specs, out_specs=o_spec,
        scratch_shapes=[pltpu.VMEM((tm, tn), F32)],
        compiler_params=_params(("parallel", "parallel", "arbitrary"), nbytes),
    )(*args)


_GELU_C = math.sqrt(2.0 / math.pi)
_GELU_A = 0.044715


def _sigmoid(x):
    return 1.0 / (1.0 + jnp.exp(-x))


def _gelu(x):
    t = jnp.tanh(_GELU_C * (x + _GELU_A * (x * x * x)))
    return x * (0.5 * (1.0 + t))


def _gelu_and_grad(x):
    x2 = x * x
    t = jnp.tanh(_GELU_C * (x + _GELU_A * (x2 * x)))
    cdf = 0.5 * (1.0 + t)
    grad = cdf + 0.5 * x * (1.0 - t * t) * (_GELU_C * (1.0 + 3.0 * _GELU_A * x2))
    return x * cdf, grad


def _rope_mix(g, cos_a, sin_a):
    return g * cos_a + pltpu.roll(g, 64, 1) * sin_a


def _rope_mix_bwd(d, cos_a, sin_a):
    return d * cos_a + pltpu.roll(d * sin_a, 64, 1)


def _rms_fwd(x, g, name, tr=512):
    T, D = x.shape

    def body(x_ref, g_ref, h_ref):
        xv = x_ref[...]
        r = lax.rsqrt(jnp.mean(xv * xv, axis=-1, keepdims=True) + EPS)
        h_ref[...] = ((xv * r) * g_ref[...]).astype(h_ref.dtype)

    return pl.pallas_call(
        body, name=name, out_shape=jax.ShapeDtypeStruct((T, D), MXU_DTYPE), grid=(T // tr,),
        in_specs=[pl.BlockSpec((tr, D), lambda i: (i, 0)), pl.BlockSpec((1, D), lambda i: (0, 0))],
        out_specs=pl.BlockSpec((tr, D), lambda i: (i, 0)),
        compiler_params=_params(("parallel",), 3 * _nbytes((tr, D), F32)),
    )(x, g)


def _rms_bwd(x, g, dh, dres, name, tr=512):
    T, D = x.shape

    def body(x_ref, g_ref, dh_ref, dres_ref, dx_ref, gg_ref):
        @pl.when(pl.program_id(0) == 0)
        def _():
            gg_ref[...] = jnp.zeros_like(gg_ref)

        xv = x_ref[...]
        r = lax.rsqrt(jnp.mean(xv * xv, axis=-1, keepdims=True) + EPS)
        xn = xv * r
        dhv = dh_ref[...]
        dxn = dhv * g_ref[...]
        dx_ref[...] = dres_ref[...] + r * (dxn - xn * jnp.mean(dxn * xn, axis=-1, keepdims=True))
        gg_ref[...] += jnp.sum(dhv * xn, axis=0, keepdims=True)

    row = pl.BlockSpec((tr, D), lambda i: (i, 0))
    vec = pl.BlockSpec((1, D), lambda i: (0, 0))
    return pl.pallas_call(
        body, name=name,
        out_shape=(jax.ShapeDtypeStruct((T, D), F32), jax.ShapeDtypeStruct((1, D), F32)),
        grid=(T // tr,), in_specs=[row, vec, row, row], out_specs=(row, vec),
        compiler_params=_params(("arbitrary",), 6 * _nbytes((tr, D), F32)),
    )(x, g, dh, dres)


def _lat_fwd(z, gq, gkv, wq, wkv, cos_a, sin_a, tr=256):
    T = z.shape[0]
    lat_blk = (4 * D_MODEL) // LAT

    def body(z_ref, gq_ref, gkv_ref, wq_ref, wkv_ref, cos_ref, sin_ref, q_ref, k_ref, v_ref, cqn_ref, ckvn_ref):
        zl = z_ref[...]
        cos_v, sin_v = cos_ref[...], sin_ref[...]
        cq = zl[:, :Q_RANK]
        ckv = zl[:, Q_RANK:Q_RANK + KV_RANK]
        krb = zl[:, Q_RANK + KV_RANK:]
        cqn = ((cq * lax.rsqrt(jnp.mean(cq * cq, axis=-1, keepdims=True) + EPS)) * gq_ref[...]).astype(MXU_DTYPE)
        ckvn = ((ckv * lax.rsqrt(jnp.mean(ckv * ckv, axis=-1, keepdims=True) + EPS)) * gkv_ref[...]).astype(MXU_DTYPE)
        cqn_ref[...] = cqn
        ckvn_ref[...] = ckvn
        krr = _rope_mix(krb, cos_v, sin_v).astype(MXU_DTYPE)
        q = jnp.dot(cqn, wq_ref[...], preferred_element_type=F32)
        kv = jnp.dot(ckvn, wkv_ref[...], preferred_element_type=F32)
        for h in range(HEADS):
            o = h * HEAD_PAD
            q_ref[:, o:o + NOPE] = q[:, o:o + NOPE].astype(MXU_DTYPE)
            q_ref[:, o + NOPE:o + HEAD_PAD] = _rope_mix(q[:, o + NOPE:o + HEAD_PAD], cos_v, sin_v).astype(MXU_DTYPE)
            k_ref[:, o:o + NOPE] = kv[:, h * NOPE:(h + 1) * NOPE].astype(MXU_DTYPE)
            k_ref[:, o + NOPE:o + HEAD_PAD] = krr
        v_ref[...] = kv[:, HEADS * NOPE:].astype(MXU_DTYPE)

    def row(w):
        return pl.BlockSpec((tr, w), lambda i: (i, 0))

    def full(a):
        return pl.BlockSpec(a.shape, lambda i: (0, 0))

    return pl.pallas_call(
        body, name="lat_fwd",
        out_shape=(jax.ShapeDtypeStruct((T, HEADS * HEAD_PAD), MXU_DTYPE), jax.ShapeDtypeStruct((T, HEADS * HEAD_PAD), MXU_DTYPE),
                   jax.ShapeDtypeStruct((T, HEADS * NOPE), MXU_DTYPE), jax.ShapeDtypeStruct((T, Q_RANK), MXU_DTYPE),
                   jax.ShapeDtypeStruct((T, KV_RANK), MXU_DTYPE)),
        grid=(T // tr,),
        in_specs=[pl.BlockSpec((tr, LAT), lambda i: (i, lat_blk)), full(gq), full(gkv), full(wq), full(wkv), row(128), row(128)],
        out_specs=(row(HEADS * HEAD_PAD), row(HEADS * HEAD_PAD), row(HEADS * NOPE), row(Q_RANK), row(KV_RANK)),
        compiler_params=_params(("parallel",), 8 * _nbytes((tr, HEADS * HEAD_PAD), F32)),
    )(z, gq, gkv, wq, wkv, cos_a, sin_a)


ATT_BLOCK = 256
_SCALE = QK_DIM ** -0.5


def _causal_mask(n):
    return lax.broadcasted_iota(jnp.int32, (n, n), 1) <= lax.broadcasted_iota(jnp.int32, (n, n), 0)


def _causal_mask_t(n):
    return lax.broadcasted_iota(jnp.int32, (n, n), 0) <= lax.broadcasted_iota(jnp.int32, (n, n), 1)


ATT_HEADS = 2


def _attn_fwd(q, k, v, B, S):
    tq = ATT_BLOCK
    nq = S // tq
    T = B * S
    hp, groups = ATT_HEADS, HEADS // ATT_HEADS

    def body(q_ref, k_ref, v_ref, o_ref, *lse_refs):
        qi = pl.program_id(2)
        qs = [q_ref[:, t * HEAD_PAD:(t + 1) * HEAD_PAD] for t in range(hp)]

        def scores(j, t):
            rows = pl.ds(pl.multiple_of(j * tq, tq), tq)
            return lax.dot_general(k_ref[rows, t * HEAD_PAD:(t + 1) * HEAD_PAD], qs[t], _DIMS["nt"],
                                   preferred_element_type=F32)

        def step(j, carry, last):
            rows = pl.ds(pl.multiple_of(j * tq, tq), tq)
            out = []
            for t in range(hp):
                m, l, acc, st = carry[t]
                st_next = st if last else scores(j + 1, t)
                st = st * _SCALE
                if last:
                    st = jnp.where(_causal_mask_t(tq), st, NEG)
                m_new = jnp.maximum(m, jnp.max(st, axis=0, keepdims=True))
                alpha = jnp.exp(m - m_new)
                p = jnp.exp(st - m_new)
                l = alpha * l + jnp.sum(p, axis=0, keepdims=True)
                acc = alpha * acc + lax.dot_general(v_ref[rows, t * NOPE:(t + 1) * NOPE], p.astype(MXU_DTYPE),
                                                    _DIMS["tn"], preferred_element_type=F32)
                out.append((m_new, l, acc, st_next))
            return tuple(out)

        init = tuple((jnp.full((1, tq), NEG, F32), jnp.zeros((1, tq), F32), jnp.zeros((NOPE, tq), F32), scores(0, t))
                     for t in range(hp))
        carry = lax.fori_loop(0, qi, lambda j, c: step(j, c, False), init)
        carry = step(qi, carry, True)
        for t in range(hp):
            m, l, acc, _ = carry[t]
            o_ref[:, t * NOPE:(t + 1) * NOPE] = (acc / l).T
            lse_refs[t][0] = m + jnp.log(l)

    lse_sds = jax.ShapeDtypeStruct((groups * B * nq, 1, tq), F32)
    lse_spec = pl.BlockSpec((1, 1, tq), lambda b, h, i: ((h * B + b) * nq + i, 0, 0))
    return pl.pallas_call(
        body, name="attn_fwd",
        out_shape=(jax.ShapeDtypeStruct((T, HEADS * NOPE), F32),) + (lse_sds,) * hp,
        grid=(B, groups, nq),
        in_specs=[pl.BlockSpec((tq, hp * HEAD_PAD), lambda b, h, i: (b * nq + i, h)),
                  pl.BlockSpec((S, hp * HEAD_PAD), lambda b, h, i: (b, h)),
                  pl.BlockSpec((S, hp * NOPE), lambda b, h, i: (b, h))],
        out_specs=(pl.BlockSpec((tq, hp * NOPE), lambda b, h, i: (b * nq + i, h)),) + (lse_spec,) * hp,
        compiler_params=_params(("parallel", "parallel", "arbitrary"), 4 * hp * _nbytes((S, HEAD_PAD), MXU_DTYPE)),
    )(q, k, v)


def _attn_bwd(q, k, v, do, lses, delta, B, S):
    tq = ATT_BLOCK
    nq = S // tq
    T = B * S
    hp, groups = ATT_HEADS, HEADS // ATT_HEADS

    def body(q_ref, k_ref, v_ref, do_ref, *refs):
        lse_refs, dl_refs = refs[:hp], refs[hp:2 * hp]
        dq_ref, dk_ref, dv_ref = refs[2 * hp:]
        kj = pl.program_id(2)

        @pl.when(kj == 0)
        def _():
            dq_ref[...] = jnp.zeros_like(dq_ref)

        def products(i, t):
            rows = pl.ds(pl.multiple_of(i * tq, tq), tq)
            st = lax.dot_general(k_ref[:, t * HEAD_PAD:(t + 1) * HEAD_PAD], q_ref[rows, t * HEAD_PAD:(t + 1) * HEAD_PAD],
                                 _DIMS["nt"], preferred_element_type=F32)
            dpt = lax.dot_general(v_ref[:, t * NOPE:(t + 1) * NOPE], do_ref[rows, t * NOPE:(t + 1) * NOPE],
                                  _DIMS["nt"], preferred_element_type=F32)
            return st, dpt

        def step(i, carry, masked):
            rows = pl.ds(pl.multiple_of(i * tq, tq), tq)
            nxt = jnp.minimum(i + 1, nq - 1)
            out = []
            for t in range(hp):
                dk, dv, st, dpt = carry[t]
                st_next, dpt_next = products(nxt, t)
                qk_cols = slice(t * HEAD_PAD, (t + 1) * HEAD_PAD)
                v_cols = slice(t * NOPE, (t + 1) * NOPE)
                p = jnp.exp(st * _SCALE - lse_refs[t][i])
                if masked:
                    p = jnp.where(_causal_mask_t(tq), p, 0.0)
                dv = dv + jnp.dot(p.astype(MXU_DTYPE), do_ref[rows, v_cols], preferred_element_type=F32)
                ds = (p * (dpt - dl_refs[t][i]) * _SCALE).astype(MXU_DTYPE)
                dk = dk + jnp.dot(ds, q_ref[rows, qk_cols], preferred_element_type=F32)
                dq_ref[rows, qk_cols] += lax.dot_general(ds, k_ref[:, qk_cols], _DIMS["tn"], preferred_element_type=F32)
                out.append((dk, dv, st_next, dpt_next))
            return tuple(out)

        init = tuple((jnp.zeros((tq, HEAD_PAD), F32), jnp.zeros((tq, NOPE), F32)) + products(kj, t) for t in range(hp))
        carry = step(kj, init, True)
        carry = lax.fori_loop(kj + 1, nq, lambda i, c: step(i, c, False), carry)
        for t in range(hp):
            dk_ref[:, t * HEAD_PAD:(t + 1) * HEAD_PAD] = carry[t][0]
            dv_ref[:, t * NOPE:(t + 1) * NOPE] = carry[t][1].astype(dv_ref.dtype)

    seq = lambda w: pl.BlockSpec((S, w), lambda b, h, j: (b, h))
    blk = lambda w: pl.BlockSpec((tq, w), lambda b, h, j: (b * nq + j, h))
    lse_spec = pl.BlockSpec((nq, 1, tq), lambda b, h, j: (h * B + b, 0, 0))
    dl_specs = [pl.BlockSpec((nq, 1, tq), lambda b, h, j, t=t: ((h * hp + t) * B + b, 0, 0)) for t in range(hp)]
    return pl.pallas_call(
        body, name="attn_bwd",
        out_shape=(jax.ShapeDtypeStruct((T, HEADS * HEAD_PAD), F32), jax.ShapeDtypeStruct((T, HEADS * HEAD_PAD), F32),
                   jax.ShapeDtypeStruct((T, HEADS * NOPE), MXU_DTYPE)),
        grid=(B, groups, nq),
        in_specs=[seq(hp * HEAD_PAD), blk(hp * HEAD_PAD), blk(hp * NOPE), seq(hp * NOPE)] + [lse_spec] * hp + dl_specs,
        out_specs=(seq(hp * HEAD_PAD), blk(hp * HEAD_PAD), blk(hp * NOPE)),
        compiler_params=_params(("parallel", "parallel", "arbitrary"), 8 * hp * _nbytes((S, HEAD_PAD), F32)),
    )(q, k, v, do, *lses, *([delta] * hp))


MIX_ROWS = 256


def _tril_weights(ws_ref, g):
    return jnp.where(_causal_mask(CHUNK), ws_ref[g], 0.0).astype(MXU_DTYPE)


def _layer_norm_stats(va):
    mu = jnp.mean(va, axis=-1, keepdims=True)
    xc = va - mu
    rs = lax.rsqrt(jnp.mean(xc * xc, axis=-1, keepdims=True) + EPS)
    return xc * rs


def _mix_specs(tr):
    zcol = lambda c: pl.BlockSpec((tr, D_MODEL), lambda i, c=c: (i, c))
    row = pl.BlockSpec((tr, D_MODEL), lambda i: (i, 0))
    vec = pl.BlockSpec((1, D_MODEL), lambda i: (0, 0))
    ws = pl.BlockSpec((A_GROUPS, CHUNK, CHUNK), lambda i: (0, 0, 0))
    bs = pl.BlockSpec((CHUNK, 128), lambda i: (0, 0))
    return zcol, row, vec, ws, bs


def _mix_fwd(z, yb, ln_g, ln_b, ws, bs_t):
    T = z.shape[0]
    tr = MIX_ROWS
    zcol, row, vec, ws_spec, bs_spec = _mix_specs(tr)

    def body(zu_ref, zv_ref, zga_ref, zgb_ref, yb_ref, g_ref, b_ref, ws_ref, bs_ref, out_ref, vn_s):
        vhat = _layer_norm_stats(_gelu(zv_ref[...]))
        vn_s[...] = (vhat * g_ref[...] + b_ref[...]).astype(MXU_DTYPE)
        for g in range(A_GROUPS):
            w = _tril_weights(ws_ref, g)
            bias = bs_ref[:, g:g + 1]
            cols = slice(g * CHUNK, (g + 1) * CHUNK)
            for c in range(tr // CHUNK):
                rows = slice(c * CHUNK, (c + 1) * CHUNK)
                mixed = jnp.dot(w, vn_s[rows, cols], preferred_element_type=F32) + bias
                ya = _gelu(zu_ref[rows, cols]) * mixed
                merged = _sigmoid(zga_ref[rows, cols]) * ya + _sigmoid(zgb_ref[rows, cols]) * yb_ref[rows, cols]
                out_ref[rows, cols] = merged.astype(MXU_DTYPE)

    return pl.pallas_call(
        body, name="mix_fwd", out_shape=jax.ShapeDtypeStruct((T, D_MODEL), MXU_DTYPE), grid=(T // tr,),
        in_specs=[zcol(0), zcol(1), zcol(2), zcol(3), row, vec, vec, ws_spec, bs_spec], out_specs=row,
        scratch_shapes=[pltpu.VMEM((tr, D_MODEL), MXU_DTYPE)],
        compiler_params=_params(("parallel",), 8 * _nbytes((tr, D_MODEL), F32)),
    )(z, z, z, z, yb, ln_g, ln_b, ws, bs_t)


def _mix_bwd(z, yb, dm, ln_g, ln_b, ws, bs_t):
    T = z.shape[0]
    tr = MIX_ROWS
    zcol, row, vec, ws_spec, bs_spec = _mix_specs(tr)

    def body(zu_ref, zv_ref, zga_ref, zgb_ref, yb_ref, dm_ref, g_ref, b_ref, ws_ref, bs_ref,
             dz_ref, dyb_ref, dl_ref, gws_ref, gbs_ref, glg_ref, glb_ref, vn_s, dvn_s):
        @pl.when(pl.program_id(0) == 0)
        def _():
            gws_ref[...] = jnp.zeros_like(gws_ref)
            gbs_ref[...] = jnp.zeros_like(gbs_ref)
            glg_ref[...] = jnp.zeros_like(glg_ref)
            glb_ref[...] = jnp.zeros_like(glb_ref)

        lane = lax.broadcasted_iota(jnp.int32, (CHUNK, 128), 1)
        va, dgelu_v = _gelu_and_grad(zv_ref[...])
        mu = jnp.mean(va, axis=-1, keepdims=True)
        xc = va - mu
        rs = lax.rsqrt(jnp.mean(xc * xc, axis=-1, keepdims=True) + EPS)
        vhat = xc * rs
        vn_s[...] = (vhat * g_ref[...] + b_ref[...]).astype(MXU_DTYPE)
        gbs_acc = jnp.zeros((CHUNK, 128), F32)
        for g in range(A_GROUPS):
            w = _tril_weights(ws_ref, g)
            bias = bs_ref[:, g:g + 1]
            cols = slice(g * CHUNK, (g + 1) * CHUNK)
            gw_acc = jnp.zeros((CHUNK, CHUNK), F32)
            for c in range(tr // CHUNK):
                rows = slice(c * CHUNK, (c + 1) * CHUNK)
                vn = vn_s[rows, cols]
                mixed = jnp.dot(w, vn, preferred_element_type=F32) + bias
                ua, dgelu_u = _gelu_and_grad(zu_ref[rows, cols])
                dmv = dm_ref[rows, cols]
                sa = _sigmoid(zga_ref[rows, cols])
                dya = dmv * sa
                dz_ref[rows, 2 * D_MODEL + g * CHUNK:2 * D_MODEL + (g + 1) * CHUNK] = (
                    dmv * (ua * mixed) * (sa * (1.0 - sa))).astype(dz_ref.dtype)
                dz_ref[rows, cols] = (dya * mixed * dgelu_u).astype(dz_ref.dtype)
                dmix = dya * ua
                gbs_acc = gbs_acc + jnp.where(lane == g, jnp.sum(dmix, axis=-1, keepdims=True), 0.0)
                dmix_b = dmix.astype(MXU_DTYPE)
                gw_acc = gw_acc + lax.dot_general(dmix_b, vn, _DIMS["nt"], preferred_element_type=F32)
                dvn_s[rows, cols] = lax.dot_general(w, dmix_b, _DIMS["tn"], preferred_element_type=F32)
            gws_ref[g] += jnp.where(_causal_mask(CHUNK), gw_acc, 0.0)
        gbs_ref[...] += gbs_acc

        dvn = dvn_s[...]
        glg_ref[...] += jnp.sum(dvn * vhat, axis=0, keepdims=True)
        glb_ref[...] += jnp.sum(dvn, axis=0, keepdims=True)
        dvh = dvn * g_ref[...]
        dva = rs * (dvh - jnp.mean(dvh, axis=-1, keepdims=True) - vhat * jnp.mean(dvh * vhat, axis=-1, keepdims=True))
        dz_ref[:, D_MODEL:2 * D_MODEL] = (dva * dgelu_v).astype(dz_ref.dtype)

        dmv = dm_ref[...]
        ybv = yb_ref[...]
        sb = _sigmoid(zgb_ref[...])
        dyb = dmv * sb
        dyb_ref[...] = dyb.astype(dyb_ref.dtype)
        dz_ref[:, 3 * D_MODEL:4 * D_MODEL] = (dmv * ybv * (sb * (1.0 - sb))).astype(dz_ref.dtype)
        dz_ref[:, 4 * D_MODEL:] = jnp.zeros((tr, LAT), dz_ref.dtype)
        prod = dyb * ybv
        sel = (lax.broadcasted_iota(jnp.int32, (HEADS, D_MODEL), 1) // NOPE
               == lax.broadcasted_iota(jnp.int32, (HEADS, D_MODEL), 0)).astype(jnp.bfloat16)
        hi = prod.astype(jnp.bfloat16)
        rest = prod - hi.astype(F32)
        mid = rest.astype(jnp.bfloat16)
        lo = (rest - mid.astype(F32)).astype(jnp.bfloat16)
        dl_ref[...] = (lax.dot_general(sel, hi, _DIMS["nt"], preferred_element_type=F32)
                       + lax.dot_general(sel, mid, _DIMS["nt"], preferred_element_type=F32)
                       + lax.dot_general(sel, lo, _DIMS["nt"], preferred_element_type=F32))

    return pl.pallas_call(
        body, name="mix_bwd",
        out_shape=(jax.ShapeDtypeStruct((T, IN_PAD), MXU_DTYPE), jax.ShapeDtypeStruct((T, D_MODEL), MXU_DTYPE),
                   jax.ShapeDtypeStruct((HEADS, T), F32), jax.ShapeDtypeStruct((A_GROUPS, CHUNK, CHUNK), F32),
                   jax.ShapeDtypeStruct((CHUNK, 128), F32), jax.ShapeDtypeStruct((1, D_MODEL), F32),
                   jax.ShapeDtypeStruct((1, D_MODEL), F32)),
        grid=(T // tr,),
        in_specs=[zcol(0), zcol(1), zcol(2), zcol(3), row, row, vec, vec, ws_spec, bs_spec],
        out_specs=(pl.BlockSpec((tr, IN_PAD), lambda i: (i, 0)), row, pl.BlockSpec((HEADS, tr), lambda i: (0, i)),
                   ws_spec, bs_spec, vec, vec),
        scratch_shapes=[pltpu.VMEM((tr, D_MODEL), MXU_DTYPE), pltpu.VMEM((tr, D_MODEL), F32)],
        compiler_params=_params(("arbitrary",), 12 * _nbytes((tr, D_MODEL), F32)),
    )(z, z, z, z, yb, dm, ln_g, ln_b, ws, bs_t)


def _lat_bwd(dz, z, dq, dk, dv, gq, gkv, wq, wkv, cos_a, sin_a, tr=256):
    T = z.shape[0]
    lat_blk = (4 * D_MODEL) // LAT

    def body(dz_in, z_ref, dq_ref, dk_ref, dv_ref, gq_ref, gkv_ref, wq_ref, wkv_ref, cos_ref, sin_ref,
             dz_ref, dqr_ref, dkv_ref, ggq_ref, ggkv_ref):
        del dz_in

        @pl.when(pl.program_id(0) == 0)
        def _():
            ggq_ref[...] = jnp.zeros_like(ggq_ref)
            ggkv_ref[...] = jnp.zeros_like(ggkv_ref)

        cos_v, sin_v = cos_ref[...], sin_ref[...]
        dkr = jnp.zeros((tr, 128), F32)
        for h in range(HEADS):
            o = h * HEAD_PAD
            dqr_ref[:, o:o + NOPE] = dq_ref[:, o:o + NOPE].astype(MXU_DTYPE)
            dqr_ref[:, o + NOPE:o + HEAD_PAD] = _rope_mix_bwd(dq_ref[:, o + NOPE:o + HEAD_PAD], cos_v, sin_v).astype(MXU_DTYPE)
            dkv_ref[:, h * NOPE:(h + 1) * NOPE] = dk_ref[:, o:o + NOPE].astype(MXU_DTYPE)
            dkr = dkr + _rope_mix_bwd(dk_ref[:, o + NOPE:o + HEAD_PAD], cos_v, sin_v)
        dkv_ref[:, HEADS * NOPE:] = dv_ref[...]
        dcqn = lax.dot_general(dqr_ref[...], wq_ref[...], _DIMS["nt"], preferred_element_type=F32)
        dckvn = lax.dot_general(dkv_ref[...], wkv_ref[...], _DIMS["nt"], preferred_element_type=F32)

        zl = z_ref[...]

        def rms_bwd(c, dn, g_ref, gg_ref):
            r = lax.rsqrt(jnp.mean(c * c, axis=-1, keepdims=True) + EPS)
            ch = c * r
            gg_ref[...] += jnp.sum(dn * ch, axis=0, keepdims=True)
            dch = dn * g_ref[...]
            return r * (dch - ch * jnp.mean(dch * ch, axis=-1, keepdims=True))

        dz_ref[:, :Q_RANK] = rms_bwd(zl[:, :Q_RANK], dcqn, gq_ref, ggq_ref).astype(dz_ref.dtype)
        dz_ref[:, Q_RANK:Q_RANK + KV_RANK] = rms_bwd(zl[:, Q_RANK:Q_RANK + KV_RANK], dckvn, gkv_ref, ggkv_ref).astype(dz_ref.dtype)
        dz_ref[:, Q_RANK + KV_RANK:] = dkr.astype(dz_ref.dtype)

    def row(w):
        return pl.BlockSpec((tr, w), lambda i: (i, 0))

    def full(a):
        return pl.BlockSpec(a.shape, lambda i: (0, 0))

    lat = pl.BlockSpec((tr, LAT), lambda i: (i, lat_blk))
    return pl.pallas_call(
        body, name="lat_bwd",
        out_shape=(jax.ShapeDtypeStruct(dz.shape, dz.dtype), jax.ShapeDtypeStruct((T, HEADS * HEAD_PAD), MXU_DTYPE),
                   jax.ShapeDtypeStruct((T, 2 * HEADS * NOPE), MXU_DTYPE), jax.ShapeDtypeStruct(gq.shape, F32),
                   jax.ShapeDtypeStruct(gkv.shape, F32)),
        grid=(T // tr,),
        in_specs=[pl.BlockSpec(memory_space=pl.ANY), lat, row(HEADS * HEAD_PAD), row(HEADS * HEAD_PAD), row(HEADS * NOPE),
                  full(gq), full(gkv), full(wq), full(wkv), row(128), row(128)],
        out_specs=(lat, row(HEADS * HEAD_PAD), row(2 * HEADS * NOPE), full(gq), full(gkv)),
        input_output_aliases={0: 0},
        compiler_params=_params(("arbitrary",), 8 * _nbytes((tr, HEADS * HEAD_PAD), F32)),
    )(dz, z, dq, dk, dv, gq, gkv, wq, wkv, cos_a, sin_a)


def _shift_down(x, k, row_idx):
    return jnp.where(row_idx >= k, pltpu.roll(x, k, 0), 0.0)


def _shift_up(x, k, row_idx, S):
    return jnp.where(row_idx < S - k, pltpu.roll(x, S - k, 0), 0.0)


def _conv(x, cw, cb, row_idx):
    return cb + cw[0:1, :] * _shift_down(x, 2, row_idx) + cw[1:2, :] * _shift_down(x, 1, row_idx) + cw[2:3, :] * x


def _gate_fwd(up3, conv_w, conv_b, B, S):
    T = B * S
    W = FF_TILE

    def body(up_ref, cw_ref, cb_ref, act_ref):
        row_idx = lax.broadcasted_iota(jnp.int32, (S, W), 0)
        gate = _conv(up_ref[0], cw_ref[0], cb_ref[0], row_idx)
        val = _conv(up_ref[1], cw_ref[1], cb_ref[1], row_idx)
        act_ref[...] = (gate * _sigmoid(gate) * val).astype(act_ref.dtype)

    return pl.pallas_call(
        body, name="gate_fwd", out_shape=jax.ShapeDtypeStruct((T, D_FF), MXU_DTYPE), grid=(B, N_FF_TILES),
        in_specs=[pl.BlockSpec((2, S, W), lambda b, j: (0, b, j)), pl.BlockSpec((2, 3, W), lambda b, j: (0, 0, j)),
                  pl.BlockSpec((2, 1, W), lambda b, j: (0, 0, j))],
        out_specs=pl.BlockSpec((S, W), lambda b, j: (b, j)),
        compiler_params=_params(("parallel", "parallel"), 12 * _nbytes((S, W), F32)),
    )(up3, conv_w, conv_b)


def _gate_bwd(up3, dact, conv_w, conv_b, B, S):
    T = B * S
    W = FF_TILE

    def body(up_ref, da_ref, cw_ref, cb_ref, dup_ref, gcw_ref, gcb_ref):
        @pl.when(pl.program_id(1) == 0)
        def _():
            gcw_ref[...] = jnp.zeros_like(gcw_ref)
            gcb_ref[...] = jnp.zeros_like(gcb_ref)

        row_idx = lax.broadcasted_iota(jnp.int32, (S, W), 0)
        gate = _conv(up_ref[0], cw_ref[0], cb_ref[0], row_idx)
        val = _conv(up_ref[1], cw_ref[1], cb_ref[1], row_idx)
        sg = _sigmoid(gate)
        da = da_ref[...]
        d_halves = (da * val * (sg * (1.0 + gate * (1.0 - sg))), da * (gate * sg))
        for half, dup in enumerate(d_halves):
            x = up_ref[half]
            cw = cw_ref[half]
            gcb_ref[half] += jnp.sum(dup, axis=0, keepdims=True)
            gcw_ref[half, 0:1, :] += jnp.sum(dup * _shift_down(x, 2, row_idx), axis=0, keepdims=True)
            gcw_ref[half, 1:2, :] += jnp.sum(dup * _shift_down(x, 1, row_idx), axis=0, keepdims=True)
            gcw_ref[half, 2:3, :] += jnp.sum(dup * x, axis=0, keepdims=True)
            dx = (cw[2:3, :] * dup + cw[1:2, :] * _shift_up(dup, 1, row_idx, S) + cw[0:1, :] * _shift_up(dup, 2, row_idx, S))
            dup_ref[half] = dx.astype(dup_ref.dtype)

    up_spec = pl.BlockSpec((2, S, W), lambda j, b: (0, b, j))
    cw_spec = pl.BlockSpec((2, 3, W), lambda j, b: (0, 0, j))
    cb_spec = pl.BlockSpec((2, 1, W), lambda j, b: (0, 0, j))
    return pl.pallas_call(
        body, name="gate_bwd",
        out_shape=(jax.ShapeDtypeStruct((2, T, D_FF), MXU_DTYPE), jax.ShapeDtypeStruct((2, 3, D_FF), F32),
                   jax.ShapeDtypeStruct((2, 1, D_FF), F32)),
        grid=(N_FF_TILES, B),
        in_specs=[up_spec, pl.BlockSpec((S, W), lambda j, b: (b, j)), cw_spec, cb_spec],
        out_specs=(up_spec, cw_spec, cb_spec),
        compiler_params=_params(("parallel", "arbitrary"), 16 * _nbytes((S, W), F32)),
    )(up3, dact, conv_w, conv_b)


def _final(x2, tgt, g, tr=512):
    T, D = x2.shape

    def body(x_ref, t_ref, g_ref, dx_ref, loss_ref, gg_ref):
        @pl.when(pl.program_id(0) == 0)
        def _():
            loss_ref[...] = jnp.zeros_like(loss_ref)
            gg_ref[...] = jnp.zeros_like(gg_ref)

        xv = x_ref[...]
        gv = g_ref[...]
        r = lax.rsqrt(jnp.mean(xv * xv, axis=-1, keepdims=True) + EPS)
        xn = xv * r
        err = xn * gv - t_ref[...]
        loss_ref[...] += 0.5 * jnp.sum(jnp.mean(err * err, axis=-1, keepdims=True), axis=0, keepdims=True)
        dy = err * (1.0 / D)
        gg_ref[...] += jnp.sum(dy * xn, axis=0, keepdims=True)
        dxn = dy * gv
        dx_ref[...] = r * (dxn - xn * jnp.mean(dxn * xn, axis=-1, keepdims=True))

    row = pl.BlockSpec((tr, D), lambda i: (i, 0))
    vec = pl.BlockSpec((1, D), lambda i: (0, 0))
    return pl.pallas_call(
        body, name="final_loss",
        out_shape=(jax.ShapeDtypeStruct((T, D), F32), jax.ShapeDtypeStruct((1, 128), F32), jax.ShapeDtypeStruct((1, D), F32)),
        grid=(T // tr,), in_specs=[row, row, vec],
        out_specs=(row, pl.BlockSpec((1, 128), lambda i: (0, 0)), vec),
        compiler_params=_params(("arbitrary",), 6 * _nbytes((tr, D), F32)),
    )(x2, tgt, g)


def _sum_slabs(parts, name, tr):
    rows, cols = parts[0].shape
    n = len(parts)

    def body(*refs):
        acc = refs[0][...]
        for r in refs[1:n]:
            acc = acc + r[...]
        refs[n][...] = acc

    blk = pl.BlockSpec((tr, cols), lambda i: (i, 0))
    return pl.pallas_call(
        body, name=name, out_shape=jax.ShapeDtypeStruct((rows, cols), F32), grid=(rows // tr,),
        in_specs=[blk] * n, out_specs=blk,
        compiler_params=_params(("parallel",), (n + 1) * _nbytes((tr, cols), F32)),
    )(*parts)


def _adamw(w, g, m, v, name):
    rows, cols = w.shape
    tr = rows
    for cand in (256, 128, 64, 32, 16, 8):
        if rows % cand == 0:
            tr = cand
            break
    c1 = 1.0 - ADAM_B1 ** ADAM_STEP
    c2 = 1.0 - ADAM_B2 ** ADAM_STEP

    def body(w_ref, g_ref, m_ref, v_ref, d_ref, nm_ref, nv_ref):
        gv = g_ref[...]
        nm = ADAM_B1 * m_ref[...] + (1.0 - ADAM_B1) * gv
        nv = ADAM_B2 * v_ref[...] + (1.0 - ADAM_B2) * (gv * gv)
        nm_ref[...] = nm
        nv_ref[...] = nv
        d_ref[...] = -ADAM_LR * ((nm / c1) / (jnp.sqrt(nv / c2) + ADAM_EPS) + ADAM_WD * w_ref[...])

    blk = pl.BlockSpec((tr, cols), lambda i: (i, 0))
    sds = jax.ShapeDtypeStruct((rows, cols), F32)
    return pl.pallas_call(
        body, name=name, out_shape=(sds, sds, sds), grid=(rows // tr,), in_specs=[blk] * 4, out_specs=(blk, blk, blk),
        compiler_params=_params(("parallel",), 7 * _nbytes((tr, cols), F32)),
    )(w, g, m, v)


_ANY = pl.BlockSpec(memory_space=pl.ANY)


def _place():
    x, y, c = lax.axis_index("x"), lax.axis_index("y"), lax.axis_index("c")
    chips = [(1 - x, y), (x, 1 - y), (1 - x, 1 - y)]
    return x, y, c, chips


def _gather_weights(shards, conv_w):
    n = len(shards)

    def body(*refs):
        ins, cw_ref, outs, cwo_ref = refs[:n], refs[n], refs[n + 1:2 * n + 1], refs[2 * n + 1]
        send, recv, fsend, frecv, csend, crecv, osend, orecv = refs[2 * n + 2:]
        x, y, c, chips = _place()
        me = 2 * x + y
        sources = list(ins) + [cw_ref]
        targets = list(outs) + [cwo_ref]
        first, passed = [], []
        for w in range(n + 1):
            first.append(pltpu.make_async_remote_copy(
                src_ref=sources[w], dst_ref=targets[w].at[me], send_sem=osend.at[w], recv_sem=orecv.at[w],
                device_id=(x, y, 1 - c), device_id_type=MESH))
        for w in range(n):
            for j, (px, py) in enumerate(chips):
                first.append(pltpu.make_async_remote_copy(
                    src_ref=ins[w].at[c], dst_ref=outs[w].at[me, c], send_sem=send.at[3 * w + j],
                    recv_sem=recv.at[3 * w + j], device_id=(px, py, c), device_id_type=MESH))
        for j, (px, py) in enumerate(chips):
            first.append(pltpu.make_async_remote_copy(
                src_ref=cw_ref, dst_ref=cwo_ref.at[me], send_sem=csend.at[j], recv_sem=crecv.at[j],
                device_id=(px, py, c), device_id_type=MESH))
        for cp in first:
            cp.start()
        for w in range(n):
            for j, (px, py) in enumerate(chips):
                landed = outs[w].at[2 * px + py, c]
                pltpu.make_async_remote_copy(src_ref=landed, dst_ref=landed, send_sem=send.at[3 * w + j],
                                             recv_sem=recv.at[3 * w + j], device_id=(px, py, c),
                                             device_id_type=MESH).wait_recv()
                fw = pltpu.make_async_remote_copy(src_ref=landed, dst_ref=landed, send_sem=fsend.at[3 * w + j],
                                                  recv_sem=frecv.at[3 * w + j], device_id=(x, y, 1 - c),
                                                  device_id_type=MESH)
                fw.start()
                passed.append(fw)
        for w in range(n):
            for j, (px, py) in enumerate(chips):
                other = outs[w].at[2 * px + py, 1 - c]
                pltpu.make_async_remote_copy(src_ref=other, dst_ref=other, send_sem=fsend.at[3 * w + j],
                                             recv_sem=frecv.at[3 * w + j], device_id=(x, y, 1 - c),
                                             device_id_type=MESH).wait_recv()
        for j, (px, py) in enumerate(chips):
            pltpu.make_async_remote_copy(src_ref=cw_ref, dst_ref=cwo_ref.at[2 * px + py], send_sem=csend.at[j],
                                         recv_sem=crecv.at[j], device_id=(px, py, c), device_id_type=MESH).wait_recv()
        for w in range(n + 1):
            own = targets[w].at[me]
            pltpu.make_async_remote_copy(src_ref=own, dst_ref=own, send_sem=osend.at[w], recv_sem=orecv.at[w],
                                         device_id=(x, y, 1 - c), device_id_type=MESH).wait_recv()
        for cp in first + passed:
            cp.wait_send()

    dma = lambda k: pltpu.SemaphoreType.DMA((k,))
    return pl.pallas_call(
        body, name="gather_weights",
        out_shape=tuple(jax.ShapeDtypeStruct((N_CHIPS,) + s.shape, s.dtype) for s in list(shards) + [conv_w]),
        in_specs=[_ANY] * (n + 1), out_specs=tuple([_ANY] * (n + 1)),
        scratch_shapes=[dma(3 * n), dma(3 * n), dma(3 * n), dma(3 * n), dma(3), dma(3), dma(n + 1), dma(n + 1)],
    )(*shards, conv_w)


def _swap_halves(gs):
    n = len(gs)

    def body(*refs):
        ins, outs, send, recv = refs[:n], refs[n:2 * n], refs[2 * n], refs[2 * n + 1]
        x, y, c, _ = _place()
        cps = []
        for w in range(n):
            cps.append(pltpu.make_async_remote_copy(
                src_ref=ins[w].at[:, 1 - c], dst_ref=outs[w], send_sem=send.at[w], recv_sem=recv.at[w],
                device_id=(x, y, 1 - c), device_id_type=MESH))
        for cp in cps:
            cp.start()
        for cp in cps:
            cp.wait()

    return pl.pallas_call(
        body, name="grad_swap_halves",
        out_shape=tuple(jax.ShapeDtypeStruct((g.shape[0],) + g.shape[2:], g.dtype) for g in gs),
        in_specs=[_ANY] * n, out_specs=tuple([_ANY] * n),
        scratch_shapes=[pltpu.SemaphoreType.DMA((n,)), pltpu.SemaphoreType.DMA((n,))],
    )(*gs)


GRAD_PAYLOAD = jnp.bfloat16


def _pair_sum(gs, gots):
    n = len(gs)
    core = lax.axis_index("c").astype(jnp.int32).reshape(1)

    def body(core_ref, *refs):
        del core_ref
        for w in range(n):
            refs[2 * n + w][...] = (refs[w][...] + refs[n + w][...]).astype(GRAD_PAYLOAD)

    in_specs, out_specs, out_shape, nbytes = [], [], [], 0
    for g in gs:
        q = g.shape[1] // 4
        in_specs.append(pl.BlockSpec((1, q, g.shape[2]), lambda s, r, core: (s, 2 * core[0] + r, 0)))
        nbytes += 3 * _nbytes((q, g.shape[2]), F32)
    for g in gs:
        q = g.shape[1] // 4
        in_specs.append(pl.BlockSpec((1, q, g.shape[2]), lambda s, r, core: (s, r, 0)))
        out_specs.append(pl.BlockSpec((1, q, g.shape[2]), lambda s, r, core: (s, r, 0)))
        out_shape.append(jax.ShapeDtypeStruct((g.shape[0], g.shape[1] // 2, g.shape[2]), GRAD_PAYLOAD))
    return pl.pallas_call(
        body, name="grad_pair_sum", out_shape=tuple(out_shape),
        grid_spec=pltpu.PrefetchScalarGridSpec(num_scalar_prefetch=1, grid=(N_CHIPS, 2), in_specs=in_specs,
                                               out_specs=tuple(out_specs)),
        compiler_params=_params(("parallel", "parallel"), nbytes),
    )(core, *gs, *gots)


def _scatter_to_chips(ps):
    n = len(ps)

    def body(*refs):
        ins, outs, send, recv = refs[:n], refs[n:2 * n], refs[2 * n], refs[2 * n + 1]
        x, y, c, chips = _place()
        me = 2 * x + y
        cps = []
        for w in range(n):
            for j, (px, py) in enumerate(chips):
                cps.append(pltpu.make_async_remote_copy(
                    src_ref=ins[w].at[2 * px + py], dst_ref=outs[w].at[me], send_sem=send.at[3 * w + j],
                    recv_sem=recv.at[3 * w + j], device_id=(px, py, c), device_id_type=MESH))
        for cp in cps:
            cp.start()
        for w in range(n):
            for j, (px, py) in enumerate(chips):
                slot = outs[w].at[2 * px + py]
                pltpu.make_async_remote_copy(src_ref=slot, dst_ref=slot, send_sem=send.at[3 * w + j],
                                             recv_sem=recv.at[3 * w + j], device_id=(px, py, c),
                                             device_id_type=MESH).wait_recv()
        for cp in cps:
            cp.wait_send()

    dma = lambda k: pltpu.SemaphoreType.DMA((k,))
    return pl.pallas_call(
        body, name="grad_scatter_chips", out_shape=tuple(jax.ShapeDtypeStruct(p.shape, p.dtype) for p in ps),
        in_specs=[_ANY] * n, out_specs=tuple([_ANY] * n), scratch_shapes=[dma(3 * n), dma(3 * n)],
    )(*ps)


def _chip_sum(ps, landed):
    n = len(ps)
    x, y, c = lax.axis_index("x"), lax.axis_index("y"), lax.axis_index("c")
    where = jnp.stack([2 * x + y, 2 * (1 - x) + y, 2 * x + (1 - y), 2 * (1 - x) + (1 - y), c]).astype(jnp.int32)

    def body(where_ref, *refs):
        del where_ref
        for w in range(n):
            terms = [refs[4 * w + t][...].astype(F32) for t in range(4)]
            refs[4 * n + w][...] = ((terms[0] + terms[1]) + terms[2]) + terms[3]

    in_specs, out_specs, out_shape, args, nbytes = [], [], [], [], 0
    for p, a in zip(ps, landed):
        q = a.shape[1] // 2
        blk = (1, q, a.shape[2])
        in_specs.append(pl.BlockSpec(blk, lambda r, where: (where[0], r, 0)))
        args.append(p)
        for t in (1, 2, 3):
            in_specs.append(pl.BlockSpec(blk, lambda r, where, t=t: (where[t], r, 0)))
            args.append(a)
        out_specs.append(pl.BlockSpec(blk, lambda r, where: (where[4], r, 0)))
        out_shape.append(jax.ShapeDtypeStruct((2,) + a.shape[1:], F32))
        nbytes += 4 * _nbytes(blk, F32)
    return pl.pallas_call(
        body, name="grad_chip_sum", out_shape=tuple(out_shape),
        grid_spec=pltpu.PrefetchScalarGridSpec(num_scalar_prefetch=1, grid=(2,), in_specs=in_specs,
                                               out_specs=tuple(out_specs)),
        compiler_params=_params(("parallel",), nbytes),
    )(where, *args)


def _join_halves(ss):
    n = len(ss)

    def body(*refs):
        outs, send, recv = refs[n:2 * n], refs[2 * n], refs[2 * n + 1]
        x, y, c, _ = _place()
        cps = []
        for w in range(n):
            cps.append(pltpu.make_async_remote_copy(
                src_ref=outs[w].at[c], dst_ref=outs[w].at[c], send_sem=send.at[w], recv_sem=recv.at[w],
                device_id=(x, y, 1 - c), device_id_type=MESH))
        for cp in cps:
            cp.start()
        for w in range(n):
            got = outs[w].at[1 - c]
            pltpu.make_async_remote_copy(src_ref=got, dst_ref=got, send_sem=send.at[w], recv_sem=recv.at[w],
                                         device_id=(x, y, 1 - c), device_id_type=MESH).wait_recv()
        for cp in cps:
            cp.wait_send()

    dma = lambda k: pltpu.SemaphoreType.DMA((k,))
    return pl.pallas_call(
        body, name="grad_join_halves",
        out_shape=tuple(jax.ShapeDtypeStruct(s.shape, s.dtype) for s in ss),
        in_specs=[_ANY] * n, out_specs=tuple([_ANY] * n), input_output_aliases={w: w for w in range(n)},
        scratch_shapes=[dma(n), dma(n)],
    )(*ss)


def _gather_all(p):
    rows, cols = p.shape

    def body(p_ref, out_ref, send, recv):
        x, y, c, _ = _place()
        me = 4 * x + 2 * y + c
        flips = [(fx, fy, fc) for fx in (0, 1) for fy in (0, 1) for fc in (0, 1)][1:]
        peers = [(x ^ fx, y ^ fy, c ^ fc) for fx, fy, fc in flips]
        cps = [pltpu.make_async_remote_copy(src_ref=p_ref, dst_ref=out_ref.at[me], send_sem=send.at[j], recv_sem=recv.at[j],
                                            device_id=peer, device_id_type=MESH) for j, peer in enumerate(peers)]
        for cp in cps:
            cp.start()
        for j, (px, py, pc) in enumerate(peers):
            slot = out_ref.at[4 * px + 2 * py + pc]
            pltpu.make_async_remote_copy(src_ref=slot, dst_ref=slot, send_sem=send.at[j], recv_sem=recv.at[j],
                                         device_id=(px, py, pc), device_id_type=MESH).wait_recv()
        for cp in cps:
            cp.wait_send()

    dma7 = pltpu.SemaphoreType.DMA((7,))
    return pl.pallas_call(
        body, name="small_grads_gather", out_shape=jax.ShapeDtypeStruct((8, rows, cols), p.dtype),
        in_specs=[_ANY], out_specs=_ANY, scratch_shapes=[dma7, dma7],
    )(p)


def _rot_cols(w):
    a, b = jnp.split(w, 2, axis=-1)
    return jnp.concatenate([-b, a], axis=-1)


def _rot_cols_t(g):
    a, b = jnp.split(g, 2, axis=-1)
    return jnp.concatenate([b, -a], axis=-1)


def _cols_from_chips(a):
    n, r, cs = a.shape
    return jnp.transpose(a, (1, 0, 2)).reshape(r, n * cs)


def _cols_to_chips(a):
    r, cc = a.shape
    return jnp.transpose(a.reshape(r, N_CHIPS, cc // N_CHIPS), (1, 0, 2))


def _conv_w_split(cw):
    return jnp.swapaxes(cw.reshape(3, 2, D_FF), 0, 1)


def _conv_w_join(g):
    return jnp.swapaxes(g, 0, 1).reshape(3, 2 * D_FF)


_SEG =(D_MODEL, 2 * D_MODEL, 2 * D_MODEL + Q_RANK, 2 * D_MODEL + Q_RANK + KV_RANK, 2 * D_MODEL + Q_RANK + KV_RANK + ROPE,
        3 * D_MODEL + Q_RANK + KV_RANK + ROPE)


def _w_in_to_pad(w):
    u, v, cq, ckv, kr, ga, gb = jnp.split(w, _SEG, axis=1)
    return jnp.concatenate([u, v, ga, gb, cq, ckv, kr, _rot_cols(kr)], axis=1)


def _w_in_from_pad(g):
    u, v, ga, gb, cq, ckv, kr, krr = jnp.split(
        g, (D_MODEL, 2 * D_MODEL, 3 * D_MODEL, 4 * D_MODEL, 4 * D_MODEL + Q_RANK, 4 * D_MODEL + Q_RANK + KV_RANK,
            4 * D_MODEL + Q_RANK + KV_RANK + ROPE), axis=1)
    return jnp.concatenate([u, v, cq, ckv, kr + _rot_cols_t(krr), ga, gb], axis=1)


def _w_uq_to_pad(w):
    t = w.reshape(Q_RANK, HEADS, QK_DIM)
    nope, rope = t[..., :NOPE], t[..., NOPE:]
    return jnp.concatenate([nope, rope, _rot_cols(rope)], axis=-1).reshape(Q_RANK, HEADS * HEAD_PAD)


def _w_uq_from_pad(g):
    t = g.reshape(Q_RANK, HEADS, HEAD_PAD)
    nope, rope, rot = t[..., :NOPE], t[..., NOPE:QK_DIM], t[..., QK_DIM:]
    return jnp.concatenate([nope, rope + _rot_cols_t(rot)], axis=-1).reshape(Q_RANK, HEADS * QK_DIM)


def _w_ukv_to_pad(w):
    t = w.reshape(KV_RANK, HEADS, 2, NOPE)
    return jnp.swapaxes(t, 1, 2).reshape(KV_RANK, 2 * HEADS * NOPE)


def _w_ukv_from_pad(g):
    t = g.reshape(KV_RANK, 2, HEADS, NOPE)
    return jnp.swapaxes(t, 1, 2).reshape(KV_RANK, 2 * HEADS * NOPE)


def _rope_tables(positions):
    inv_freq = 1.0 / (ROPE_THETA ** (jnp.arange(0, ROPE, 2, dtype=F32) / ROPE))
    ang = positions.astype(F32).reshape(-1, 1) * inv_freq
    cos, sin = jnp.cos(ang), jnp.sin(ang)
    zero = jnp.zeros((ang.shape[0], 64), F32)
    return jnp.concatenate([cos, cos, zero], axis=1), jnp.concatenate([sin, sin, zero], axis=1)


_BIG = ("w_in", "w_uq", "w_ukv", "w_out", "w_up", "w_down")
UP_SHARD = 2 * D_FF // N_CHIPS


def _local_step(x, positions, tgt, wts):
    B, S, D = x.shape
    T = B * S
    xf = x.reshape(T, D)
    cos_a, sin_a = _rope_tables(positions)
    bs_t = jnp.pad(wts["a_spatial_b"].T, ((0, 0), (0, 128 - A_GROUPS)))

    h = _rms_fwd(xf, wts["mix_norm"], "norm1_fwd")
    z = _mm(h, wts["w_in"], "nn", "in_proj", tm=512, tn=1536, tk=D)
    q, k, v, cqn, ckvn = _lat_fwd(z, wts["q_a_norm"], wts["kv_a_norm"], wts["w_q"], wts["w_kv"], cos_a, sin_a)
    yb, *lses = _attn_fwd(q, k, v, B, S)
    merged = _mix_fwd(z, yb, wts["a_v_norm_g"], wts["a_v_norm_b"], wts["a_spatial_w"], bs_t)
    x1 = _mm(merged, wts["w_out"], "nn", "out_proj", tm=512, tn=D, tk=D, add=xf)
    h2 = _rms_fwd(x1, wts["ffn_norm"], "norm2_fwd")
    up_pre = _mm(h2, wts["w_up"], "nn", "up_proj", tm=512, tn=UP_SHARD, tk=D, dims=(T, 2 * D_FF, D),
                 b_spec=pl.BlockSpec((None, D, UP_SHARD), lambda i, j, k: (j, 0, 0)),
                 o_spec=pl.BlockSpec((None, 512, UP_SHARD), lambda i, j, k: (j // 2, i, j % 2)), out_shape=(2, T, D_FF))
    act = _gate_fwd(up_pre, wts["conv_w"], wts["conv_b"], B, S)
    x2 = _mm(act, wts["w_down"], "nn", "down_proj", tm=512, tn=D, tk=1408, add=x1)
    dx2, loss_row, g_final = _final(x2, tgt.reshape(T, D), wts["final_norm"])

    g = {"final_norm": g_final}
    dact = _mm(dx2, wts["w_down"], "nt", "down_proj_dx", tm=512, tn=1408, tk=D)
    g["w_down"] = _mm(act, dx2, "tn", "down_proj_dw", tm=1408, tn=D, tk=512)
    dup, g["conv_w"], g["conv_b"] = _gate_bwd(up_pre, dact, wts["conv_w"], wts["conv_b"], B, S)
    dh2 = _mm(dup, wts["w_up"], "nt", "up_proj_dx", tm=512, tn=D, tk=UP_SHARD, dims=(T, D, 2 * D_FF),
              a_spec=pl.BlockSpec((None, 512, UP_SHARD), lambda i, j, k: (k // 2, i, k % 2)),
              b_spec=pl.BlockSpec((None, D, UP_SHARD), lambda i, j, k: (k, 0, 0)))
    g["w_up"] = _mm(h2, dup, "tn", "up_proj_dw", tm=D, tn=UP_SHARD, tk=512, dims=(D, 2 * D_FF, T),
                    b_spec=pl.BlockSpec((None, 512, UP_SHARD), lambda i, j, k: (j // 2, k, j % 2)),
                    o_spec=pl.BlockSpec((None, D, UP_SHARD), lambda i, j, k: (j, 0, 0)), out_shape=(N_CHIPS, D, UP_SHARD))
    dx1, g["ffn_norm"] = _rms_bwd(x1, wts["ffn_norm"], dh2, dx2, "norm2_bwd")
    dm = _mm(dx1, wts["w_out"], "nt", "out_proj_dx", tm=512, tn=D, tk=D)
    g["w_out"] = _mm(merged, dx1, "tn", "out_proj_dw", tm=D, tn=D, tk=512)
    dz, dyb, dl, g["a_spatial_w"], gbs, g["a_v_norm_g"], g["a_v_norm_b"] = _mix_bwd(
        z, yb, dm, wts["a_v_norm_g"], wts["a_v_norm_b"], wts["a_spatial_w"], bs_t)
    g["a_spatial_b"] = gbs[:, :A_GROUPS].T
    delta = dl.reshape(HEADS * T // ATT_BLOCK, 1, ATT_BLOCK)
    dq, dk, dv = _attn_bwd(q, k, v, dyb, lses, delta, B, S)
    dz, dq_raw, dkv, g["q_a_norm"], g["kv_a_norm"] = _lat_bwd(
        dz, z, dq, dk, dv, wts["q_a_norm"], wts["kv_a_norm"], wts["w_q"], wts["w_kv"], cos_a, sin_a)
    g["w_q"] = _mm(cqn, dq_raw, "tn", "q_proj_dw", tm=Q_RANK, tn=HEADS * HEAD_PAD, tk=512)
    g["w_kv"] = _mm(ckvn, dkv, "tn", "kv_proj_dw", tm=KV_RANK, tn=2 * HEADS * NOPE, tk=512)
    dh = _mm(dz, wts["w_in"], "nt", "in_proj_dx", tm=512, tn=D, tk=1536)
    g["w_in"] = _mm(h, dz, "tn", "in_proj_dw", tm=D, tn=1536, tk=512)
    dx, g["mix_norm"] = _rms_bwd(xf, wts["mix_norm"], dh, dx1, "norm1_bwd")
    return loss_row[0, 0], dx.reshape(B, S, D), g


_SMALL = (("mix_norm", (1, D_MODEL)), ("a_v_norm_g", (1, D_MODEL)), ("a_v_norm_b", (1, D_MODEL)),
          ("a_spatial_w", (A_GROUPS * CHUNK, CHUNK)), ("a_spatial_b", (1, A_GROUPS * CHUNK)), ("q_a_norm", (1, Q_RANK)),
          ("kv_a_norm", (1, KV_RANK)), ("ffn_norm", (1, D_MODEL)), ("conv_b", (1, 2 * D_FF)), ("final_norm", (1, D_MODEL)),
          ("conv_w", (3, 2 * D_FF)))
_SMALL_ROWS = -(-sum(math.prod(s) for _, s in _SMALL) // (128 * 8)) * 8


def kernel(x, positions, mix_norm, w_in, a_v_norm_g, a_v_norm_b, a_spatial_w, a_spatial_b, q_a_norm, w_uq, kv_a_norm, w_ukv, w_out, ffn_norm, w_up, conv_w, conv_b, w_down, final_norm, loss_target, m_mix_norm, m_w_in, m_a_v_norm_g, m_a_v_norm_b, m_a_spatial_w, m_a_spatial_b, m_q_a_norm, m_w_uq, m_kv_a_norm, m_w_ukv, m_w_out, m_ffn_norm, m_w_up, m_conv_w, m_conv_b, m_w_down, m_final_norm, v_mix_norm, v_w_in, v_a_v_norm_g, v_a_v_norm_b, v_a_spatial_w, v_a_spatial_b, v_q_a_norm, v_w_uq, v_kv_a_norm, v_w_ukv, v_w_out, v_ffn_norm, v_w_up, v_conv_w, v_conv_b, v_w_down, v_final_norm):
    weights = dict(mix_norm=mix_norm, w_in=w_in, a_v_norm_g=a_v_norm_g, a_v_norm_b=a_v_norm_b, a_spatial_w=a_spatial_w,
                   a_spatial_b=a_spatial_b, q_a_norm=q_a_norm, w_uq=w_uq, kv_a_norm=kv_a_norm, w_ukv=w_ukv, w_out=w_out,
                   ffn_norm=ffn_norm, w_up=w_up, conv_w=conv_w, conv_b=conv_b, w_down=w_down, final_norm=final_norm)
    m_in = dict(mix_norm=m_mix_norm, w_in=m_w_in, a_v_norm_g=m_a_v_norm_g, a_v_norm_b=m_a_v_norm_b,
                a_spatial_w=m_a_spatial_w, a_spatial_b=m_a_spatial_b, q_a_norm=m_q_a_norm, w_uq=m_w_uq,
                kv_a_norm=m_kv_a_norm, w_ukv=m_w_ukv, w_out=m_w_out, ffn_norm=m_ffn_norm, w_up=m_w_up, conv_w=m_conv_w,
                conv_b=m_conv_b, w_down=m_w_down, final_norm=m_final_norm)
    v_in = dict(mix_norm=v_mix_norm, w_in=v_w_in, a_v_norm_g=v_a_v_norm_g, a_v_norm_b=v_a_v_norm_b,
                a_spatial_w=v_a_spatial_w, a_spatial_b=v_a_spatial_b, q_a_norm=v_q_a_norm, w_uq=v_w_uq,
                kv_a_norm=v_kv_a_norm, w_ukv=v_w_ukv, w_out=v_w_out, ffn_norm=v_ffn_norm, w_up=v_w_up, conv_w=v_conv_w,
                conv_b=v_conv_b, w_down=v_w_down, final_norm=v_final_norm)
    names = list(weights)
    chip = 2 * lax.axis_index("x") + lax.axis_index("y")

    def halves(a):
        return a.reshape(a.shape[:-2] + (2, a.shape[-2] // 2, a.shape[-1]))

    def whole(a):
        return a.reshape(a.shape[:-3] + (2 * a.shape[-2], a.shape[-1]))

    *gathered, cw_all = _gather_weights([halves(weights[n][0].astype(MXU_DTYPE)) for n in _BIG], conv_w[0])
    w_in_sh, w_uq_sh, w_ukv_sh, w_out_sh, w_up_sh, w_down_sh = (whole(a) for a in gathered)
    wts = dict(
        mix_norm=mix_norm, a_v_norm_g=a_v_norm_g, a_v_norm_b=a_v_norm_b, a_spatial_w=a_spatial_w[0],
        a_spatial_b=a_spatial_b[0], q_a_norm=q_a_norm, kv_a_norm=kv_a_norm, ffn_norm=ffn_norm,
        final_norm=final_norm.reshape(1, D_MODEL),
        w_in=_w_in_to_pad(_cols_from_chips(w_in_sh)), w_q=_w_uq_to_pad(_cols_from_chips(w_uq_sh)),
        w_kv=_w_ukv_to_pad(_cols_from_chips(w_ukv_sh)), w_out=w_out_sh.reshape(D_MODEL, D_MODEL), w_up=w_up_sh,
        w_down=w_down_sh.reshape(D_FF, D_MODEL), conv_w=_conv_w_split(_cols_from_chips(cw_all)),
        conv_b=conv_b.reshape(2, 1, D_FF))

    loss_part, grad_x, g = _local_step(x, positions, loss_target, wts)
    loss = lax.psum(loss_part, ("x", "y", "c"))

    slabs = [_cols_to_chips(_w_in_from_pad(g["w_in"])), _cols_to_chips(_w_uq_from_pad(g["w_q"])),
             _cols_to_chips(_w_ukv_from_pad(g["w_kv"])), g["w_out"].reshape(N_CHIPS, D_MODEL // N_CHIPS, D_MODEL),
             g["w_up"], g["w_down"].reshape(N_CHIPS, D_FF // N_CHIPS, D_MODEL)]
    pair_sums = _pair_sum(slabs, _swap_halves([halves(s) for s in slabs]))
    landed = _scatter_to_chips(pair_sums)
    g_big = dict(zip(_BIG, (whole(a) for a in _join_halves(_chip_sum(pair_sums, landed)))))

    g_small_parts = dict(g)
    g_small_parts["conv_w"] = _conv_w_join(g["conv_w"])
    g_small_parts["conv_b"] = g["conv_b"].reshape(1, 2 * D_FF)
    flat = jnp.concatenate([g_small_parts[n].reshape(-1) for n, _ in _SMALL])
    flat = jnp.pad(flat, (0, _SMALL_ROWS * 128 - flat.shape[0])).reshape(_SMALL_ROWS, 128)
    device = 2 * chip + lax.axis_index("c")
    everyone = lax.dynamic_update_slice(_gather_all(flat), flat[None], (device, 0, 0))
    total = _sum_slabs([everyone[j] for j in range(8)], "small_grads_sum", tr=_SMALL_ROWS).reshape(-1)
    g_small, o = {}, 0
    for n, shp in _SMALL:
        g_small[n] = total[o:o + math.prod(shp)].reshape(shp)
        o += math.prod(shp)
    g_small["conv_w"] = lax.dynamic_slice_in_dim(g_small["conv_w"], chip * 1408, 1408, axis=1)

    grads, deltas, new_m, new_v = {}, {}, {}, {}
    for n in names:
        w = weights[n]
        g2 = g_big[n] if n in g_big else g_small[n]
        shape2 = g2.shape
        d, nm, nv = _adamw(w.reshape(shape2), g2, m_in[n].reshape(shape2), v_in[n].reshape(shape2), "adamw_" + n)
        grads[n], deltas[n], new_m[n], new_v[n] = (t.reshape(w.shape) for t in (g2, d, nm, nv))
    return (loss, grad_x, *[grads[n] for n in names], *[deltas[n] for n in names], *[new_m[n] for n in names],
            *[new_v[n] for n in names])
```

```python
import functools
import math

import jax
import jax.numpy as jnp
from jax import lax
from jax.experimental import pallas as pl
from jax.experimental.pallas import tpu as pltpu

F32 = jnp.float32
MXU_DTYPE = jnp.bfloat16
MESH = pl.DeviceIdType.MESH

D_MODEL = 1024
EPS = 1e-6
A_GROUPS = 8
CHUNK = 128
HEADS = 8
NOPE = 128
ROPE = 64
QK_DIM = NOPE + ROPE
HEAD_PAD = 256
Q_RANK = 256
KV_RANK = 128
ROPE_THETA = 10000.0
D_FF = 2816
FF_TILE = 256
N_FF_TILES = D_FF // FF_TILE
LAT = 512
IN_PAD = 4 * D_MODEL + LAT
N_CHIPS = 4
ADAM_LR, ADAM_B1, ADAM_B2, ADAM_EPS, ADAM_WD, ADAM_STEP = 0.001, 0.9, 0.999, 1e-08, 0.01, 10

VMEM_CAP_V7X = 64 * 1024 * 1024
NEG = -1e30


def _params(sem, nbytes):
    limit = int(min(VMEM_CAP_V7X - (8 << 20), max(32 << 20, 3 * nbytes)))
    return pltpu.CompilerParams(dimension_semantics=sem, vmem_limit_bytes=limit)


def _nbytes(shape, dtype):
    return math.prod(shape) * jnp.dtype(dtype).itemsize


_DIMS = {"nn": (((1,), (0,)), ((), ())), "nt": (((1,), (1,)), ((), ())), "tn": (((0,), (0,)), ((), ()))}


def _mm(a, b, mode, name, *, tm, tn, tk, out_dtype=F32, add=None, dims=None, a_spec=None, b_spec=None,
        o_spec=None, out_shape=None):
    if dims is None:
        if mode == "nn":
            (M, K), (_, N) = a.shape, b.shape
        elif mode == "nt":
            (M, K), (N, _) = a.shape, b.shape
        else:
            (K, M), (_, N) = a.shape, b.shape
    else:
        M, N, K = dims
    a_blk = (tk, tm) if mode == "tn" else (tm, tk)
    b_blk = (tn, tk) if mode == "nt" else (tk, tn)
    if a_spec is None:
        a_spec = pl.BlockSpec(a_blk, (lambda i, j, k: (k, i)) if mode == "tn" else (lambda i, j, k: (i, k)))
    if b_spec is None:
        b_spec = pl.BlockSpec(b_blk, (lambda i, j, k: (j, k)) if mode == "nt" else (lambda i, j, k: (k, j)))
    if o_spec is None:
        o_spec = pl.BlockSpec((tm, tn), lambda i, j, k: (i, j))
    if out_shape is None:
        out_shape = (M, N)
    assert M % tm == 0 and N % tn == 0 and K % tk == 0, (name, M, N, K, tm, tn, tk)
    nk = K // tk
    contract = _DIMS[mode]
    has_add = add is not None

    def body(*refs):
        if has_add:
            a_ref, b_ref, add_ref, o_ref, acc = refs
        else:
            a_ref, b_ref, o_ref, acc = refs
        k = pl.program_id(2)

        @pl.when(k == 0)
        def _():
            acc[...] = jnp.zeros_like(acc)

        acc[...] += lax.dot_general(a_ref[...].astype(MXU_DTYPE), b_ref[...].astype(MXU_DTYPE), contract,
                                    preferred_element_type=F32)

        @pl.when(k == nk - 1)
        def _():
            r = acc[...]
            if has_add:
                r = r + add_ref[...]
            o_ref[...] = r.astype(out_dtype)

    in_specs = [a_spec, b_spec]
    args = [a, b]
    nbytes = _nbytes(a_blk, a.dtype) + _nbytes(b_blk, b.dtype) + 3 * _nbytes((tm, tn), F32)
    if has_add:
        in_specs.append(pl.BlockSpec((tm, tn), lambda i, j, k: (i, j)))
        args.append(add)
        nbytes += _nbytes((tm, tn), F32)
    return pl.pallas_call(
        body, name=name, out_shape=jax.ShapeDtypeStruct(out_shape, out_dtype),
        grid=(M // tm, N // tn, nk), in_specs=in_specs, out_specs=o_spec,
        scratch_shapes=[pltpu.VMEM((tm, tn), F32)],
        compiler_params=_params(("parallel", "parallel", "arbitrary"), nbytes),
    )(*args)


_GELU_C = math.sqrt(2.0 / math.pi)
_GELU_A = 0.044715


def _sigmoid(x):
    return 1.0 / (1.0 + jnp.exp(-x))


def _gelu(x):
    t = jnp.tanh(_GELU_C * (x + _GELU_A * (x * x * x)))
    return x * (0.5 * (1.0 + t))


def _gelu_and_grad(x):
    x2 = x * x
    t = jnp.tanh(_GELU_C * (x + _GELU_A * (x2 * x)))
    cdf = 0.5 * (1.0 + t)
    grad = cdf + 0.5 * x * (1.0 - t * t) * (_GELU_C * (1.0 + 3.0 * _GELU_A * x2))
    return x * cdf, grad


def _rope_mix(g, cos_a, sin_a):
    return g * cos_a + pltpu.roll(g, 64, 1) * sin_a


def _rope_mix_bwd(d, cos_a, sin_a):
    return d * cos_a + pltpu.roll(d * sin_a, 64, 1)


def _rms_fwd(x, g, name, tr=512):
    T, D = x.shape

    def body(x_ref, g_ref, h_ref):
        xv = x_ref[...]
        r = lax.rsqrt(jnp.mean(xv * xv, axis=-1, keepdims=True) + EPS)
        h_ref[...] = ((xv * r) * g_ref[...]).astype(h_ref.dtype)

    return pl.pallas_call(
        body, name=name, out_shape=jax.ShapeDtypeStruct((T, D), MXU_DTYPE), grid=(T // tr,),
        in_specs=[pl.BlockSpec((tr, D), lambda i: (i, 0)), pl.BlockSpec((1, D), lambda i: (0, 0))],
        out_specs=pl.BlockSpec((tr, D), lambda i: (i, 0)),
        compiler_params=_params(("parallel",), 3 * _nbytes((tr, D), F32)),
    )(x, g)


def _rms_bwd(x, g, dh, dres, name, tr=512):
    T, D = x.shape

    def body(x_ref, g_ref, dh_ref, dres_ref, dx_ref, gg_ref):
        @pl.when(pl.program_id(0) == 0)
        def _():
            gg_ref[...] = jnp.zeros_like(gg_ref)

        xv = x_ref[...]
        r = lax.rsqrt(jnp.mean(xv * xv, axis=-1, keepdims=True) + EPS)
        xn = xv * r
        dhv = dh_ref[...]
        dxn = dhv * g_ref[...]
        dx_ref[...] = dres_ref[...] + r * (dxn - xn * jnp.mean(dxn * xn, axis=-1, keepdims=True))
        gg_ref[...] += jnp.sum(dhv * xn, axis=0, keepdims=True)

    row = pl.BlockSpec((tr, D), lambda i: (i, 0))
    vec = pl.BlockSpec((1, D), lambda i: (0, 0))
    return pl.pallas_call(
        body, name=name,
        out_shape=(jax.ShapeDtypeStruct((T, D), F32), jax.ShapeDtypeStruct((1, D), F32)),
        grid=(T // tr,), in_specs=[row, vec, row, row], out_specs=(row, vec),
        compiler_params=_params(("arbitrary",), 6 * _nbytes((tr, D), F32)),
    )(x, g, dh, dres)


def _lat_fwd(z, gq, gkv, wq, wkv, cos_a, sin_a, tr=256):
    T = z.shape[0]
    lat_blk = (4 * D_MODEL) // LAT

    def body(z_ref, gq_ref, gkv_ref, wq_ref, wkv_ref, cos_ref, sin_ref, q_ref, k_ref, v_ref, cqn_ref, ckvn_ref):
        zl = z_ref[...]
        cos_v, sin_v = cos_ref[...], sin_ref[...]
        cq = zl[:, :Q_RANK]
        ckv = zl[:, Q_RANK:Q_RANK + KV_RANK]
        krb = zl[:, Q_RANK + KV_RANK:]
        cqn = ((cq * lax.rsqrt(jnp.mean(cq * cq, axis=-1, keepdims=True) + EPS)) * gq_ref[...]).astype(MXU_DTYPE)
        ckvn = ((ckv * lax.rsqrt(jnp.mean(ckv * ckv, axis=-1, keepdims=True) + EPS)) * gkv_ref[...]).astype(MXU_DTYPE)
        cqn_ref[...] = cqn
        ckvn_ref[...] = ckvn
        krr = _rope_mix(krb, cos_v, sin_v).astype(MXU_DTYPE)
        q = jnp.dot(cqn, wq_ref[...], preferred_element_type=F32)
        kv = jnp.dot(ckvn, wkv_ref[...], preferred_element_type=F32)
        for h in range(HEADS):
            o = h * HEAD_PAD
            q_ref[:, o:o + NOPE] = q[:, o:o + NOPE].astype(MXU_DTYPE)
            q_ref[:, o + NOPE:o + HEAD_PAD] = _rope_mix(q[:, o + NOPE:o + HEAD_PAD], cos_v, sin_v).astype(MXU_DTYPE)
            k_ref[:, o:o + NOPE] = kv[:, h * NOPE:(h + 1) * NOPE].astype(MXU_DTYPE)
            k_ref[:, o + NOPE:o + HEAD_PAD] = krr
        v_ref[...] = kv[:, HEADS * NOPE:].astype(MXU_DTYPE)

    def row(w):
        return pl.BlockSpec((tr, w), lambda i: (i, 0))

    def full(a):
        return pl.BlockSpec(a.shape, lambda i: (0, 0))

    return pl.pallas_call(
        body, name="lat_fwd",
        out_shape=(jax.ShapeDtypeStruct((T, HEADS * HEAD_PAD), MXU_DTYPE), jax.ShapeDtypeStruct((T, HEADS * HEAD_PAD), MXU_DTYPE),
                   jax.ShapeDtypeStruct((T, HEADS * NOPE), MXU_DTYPE), jax.ShapeDtypeStruct((T, Q_RANK), MXU_DTYPE),
                   jax.ShapeDtypeStruct((T, KV_RANK), MXU_DTYPE)),
        grid=(T // tr,),
        in_specs=[pl.BlockSpec((tr, LAT), lambda i: (i, lat_blk)), full(gq), full(gkv), full(wq), full(wkv), row(128), row(128)],
        out_specs=(row(HEADS * HEAD_PAD), row(HEADS * HEAD_PAD), row(HEADS * NOPE), row(Q_RANK), row(KV_RANK)),
        compiler_params=_params(("parallel",), 8 * _nbytes((tr, HEADS * HEAD_PAD), F32)),
    )(z, gq, gkv, wq, wkv, cos_a, sin_a)


ATT_BLOCK = 256
_SCALE = QK_DIM ** -0.5


def _causal_mask(n):
    return lax.broadcasted_iota(jnp.int32, (n, n), 1) <= lax.broadcasted_iota(jnp.int32, (n, n), 0)


def _causal_mask_t(n):
    return lax.broadcasted_iota(jnp.int32, (n, n), 0) <= lax.broadcasted_iota(jnp.int32, (n, n), 1)


ATT_HEADS = 2


def _attn_fwd(q, k, v, B, S):
    tq = ATT_BLOCK
    nq = S // tq
    T = B * S
    hp, groups = ATT_HEADS, HEADS // ATT_HEADS

    def body(q_ref, k_ref, v_ref, o_ref, *lse_refs):
        qi = pl.program_id(2)
        qs = [q_ref[:, t * HEAD_PAD:(t + 1) * HEAD_PAD] for t in range(hp)]

        def scores(j, t):
            rows = pl.ds(pl.multiple_of(j * tq, tq), tq)
            return lax.dot_general(k_ref[rows, t * HEAD_PAD:(t + 1) * HEAD_PAD], qs[t], _DIMS["nt"],
                                   preferred_element_type=F32)

        def step(j, carry, last):
            rows = pl.ds(pl.multiple_of(j * tq, tq), tq)
            out = []
            for t in range(hp):
                m, l, acc, st = carry[t]
                st_next = st if last else scores(j + 1, t)
                st = st * _SCALE
                if last:
                    st = jnp.where(_causal_mask_t(tq), st, NEG)
                m_new = jnp.maximum(m, jnp.max(st, axis=0, keepdims=True))
                alpha = jnp.exp(m - m_new)
                p = jnp.exp(st - m_new)
                l = alpha * l + jnp.sum(p, axis=0, keepdims=True)
                acc = alpha * acc + lax.dot_general(v_ref[rows, t * NOPE:(t + 1) * NOPE], p.astype(MXU_DTYPE),
                                                    _DIMS["tn"], preferred_element_type=F32)
                out.append((m_new, l, acc, st_next))
            return tuple(out)

        init = tuple((jnp.full((1, tq), NEG, F32), jnp.zeros((1, tq), F32), jnp.zeros((NOPE, tq), F32), scores(0, t))
                     for t in range(hp))
        carry = lax.fori_loop(0, qi, lambda j, c: step(j, c, False), init)
        carry = step(qi, carry, True)
        for t in range(hp):
            m, l, acc, _ = carry[t]
            o_ref[:, t * NOPE:(t + 1) * NOPE] = (acc / l).T
            lse_refs[t][0] = m + jnp.log(l)

    lse_sds = jax.ShapeDtypeStruct((groups * B * nq, 1, tq), F32)
    lse_spec = pl.BlockSpec((1, 1, tq), lambda b, h, i: ((h * B + b) * nq + i, 0, 0))
    return pl.pallas_call(
        body, name="attn_fwd",
        out_shape=(jax.ShapeDtypeStruct((T, HEADS * NOPE), F32),) + (lse_sds,) * hp,
        grid=(B, groups, nq),
        in_specs=[pl.BlockSpec((tq, hp * HEAD_PAD), lambda b, h, i: (b * nq + i, h)),
                  pl.BlockSpec((S, hp * HEAD_PAD), lambda b, h, i: (b, h)),
                  pl.BlockSpec((S, hp * NOPE), lambda b, h, i: (b, h))],
        out_specs=(pl.BlockSpec((tq, hp * NOPE), lambda b, h, i: (b * nq + i, h)),) + (lse_spec,) * hp,
        compiler_params=_params(("parallel", "parallel", "arbitrary"), 4 * hp * _nbytes((S, HEAD_PAD), MXU_DTYPE)),
    )(q, k, v)


def _attn_bwd(q, k, v, do, lses, delta, B, S):
    tq = ATT_BLOCK
    nq = S // tq
    T = B * S
    hp, groups = ATT_HEADS, HEADS // ATT_HEADS

    def body(q_ref, k_ref, v_ref, do_ref, *refs):
        lse_refs, dl_refs = refs[:hp], refs[hp:2 * hp]
        dq_ref, dk_ref, dv_ref = refs[2 * hp:]
        kj = pl.program_id(2)

        @pl.when(kj == 0)
        def _():
            dq_ref[...] = jnp.zeros_like(dq_ref)

        def products(i, t):
            rows = pl.ds(pl.multiple_of(i * tq, tq), tq)
            st = lax.dot_general(k_ref[:, t * HEAD_PAD:(t + 1) * HEAD_PAD], q_ref[rows, t * HEAD_PAD:(t + 1) * HEAD_PAD],
                                 _DIMS["nt"], preferred_element_type=F32)
            dpt = lax.dot_general(v_ref[:, t * NOPE:(t + 1) * NOPE], do_ref[rows, t * NOPE:(t + 1) * NOPE],
                                  _DIMS["nt"], preferred_element_type=F32)
            return st, dpt

        def step(i, carry, masked):
            rows = pl.ds(pl.multiple_of(i * tq, tq), tq)
            nxt = jnp.minimum(i + 1, nq - 1)
            out = []
            for t in range(hp):
                dk, dv, st, dpt = carry[t]
                st_next, dpt_next = products(nxt, t)
                qk_cols = slice(t * HEAD_PAD, (t + 1) * HEAD_PAD)
                v_cols = slice(t * NOPE, (t + 1) * NOPE)
                p = jnp.exp(st * _SCALE - lse_refs[t][i])
                if masked:
                    p = jnp.where(_causal_mask_t(tq), p, 0.0)
                dv = dv + jnp.dot(p.astype(MXU_DTYPE), do_ref[rows, v_cols], preferred_element_type=F32)
                ds = (p * (dpt - dl_refs[t][i]) * _SCALE).astype(MXU_DTYPE)
                dk = dk + jnp.dot(ds, q_ref[rows, qk_cols], preferred_element_type=F32)
                dq_ref[rows, qk_cols] += lax.dot_general(ds, k_ref[:, qk_cols], _DIMS["tn"], preferred_element_type=F32)
                out.append((dk, dv, st_next, dpt_next))
            return tuple(out)

        init = tuple((jnp.zeros((tq, HEAD_PAD), F32), jnp.zeros((tq, NOPE), F32)) + products(kj, t) for t in range(hp))
        carry = step(kj, init, True)
        carry = lax.fori_loop(kj + 1, nq, lambda i, c: step(i, c, False), carry)
        for t in range(hp):
            dk_ref[:, t * HEAD_PAD:(t + 1) * HEAD_PAD] = carry[t][0]
            dv_ref[:, t * NOPE:(t + 1) * NOPE] = carry[t][1].astype(dv_ref.dtype)

    seq = lambda w: pl.BlockSpec((S, w), lambda b, h, j: (b, h))
    blk = lambda w: pl.BlockSpec((tq, w), lambda b, h, j: (b * nq + j, h))
    lse_spec = pl.BlockSpec((nq, 1, tq), lambda b, h, j: (h * B + b, 0, 0))
    dl_specs = [pl.BlockSpec((nq, 1, tq), lambda b, h, j, t=t: ((h * hp + t) * B + b, 0, 0)) for t in range(hp)]
    return pl.pallas_call(
        body, name="attn_bwd",
        out_shape=(jax.ShapeDtypeStruct((T, HEADS * HEAD_PAD), F32), jax.ShapeDtypeStruct((T, HEADS * HEAD_PAD), F32),
                   jax.ShapeDtypeStruct((T, HEADS * NOPE), MXU_DTYPE)),
        grid=(B, groups, nq),
        in_specs=[seq(hp * HEAD_PAD), blk(hp * HEAD_PAD), blk(hp * NOPE), seq(hp * NOPE)] + [lse_spec] * hp + dl_specs,
        out_specs=(seq(hp * HEAD_PAD), blk(hp * HEAD_PAD), blk(hp * NOPE)),
        compiler_params=_params(("parallel", "parallel", "arbitrary"), 8 * hp * _nbytes((S, HEAD_PAD), F32)),
    )(q, k, v, do, *lses, *([delta] * hp))


MIX_ROWS = 256


def _tril_weights(ws_ref, g):
    return jnp.where(_causal_mask(CHUNK), ws_ref[g], 0.0).astype(MXU_DTYPE)


def _layer_norm_stats(va):
    mu = jnp.mean(va, axis=-1, keepdims=True)
    xc = va - mu
    rs = lax.rsqrt(jnp.mean(xc * xc, axis=-1, keepdims=True) + EPS)
    return xc * rs


def _mix_specs(tr):
    zcol = lambda c: pl.BlockSpec((tr, D_MODEL), lambda i, c=c: (i, c))
    row = pl.BlockSpec((tr, D_MODEL), lambda i: (i, 0))
    vec = pl.BlockSpec((1, D_MODEL), lambda i: (0, 0))
    ws = pl.BlockSpec((A_GROUPS, CHUNK, CHUNK), lambda i: (0, 0, 0))
    bs = pl.BlockSpec((CHUNK, 128), lambda i: (0, 0))
    return zcol, row, vec, ws, bs


def _mix_fwd(z, yb, ln_g, ln_b, ws, bs_t):
    T = z.shape[0]
    tr = MIX_ROWS
    zcol, row, vec, ws_spec, bs_spec = _mix_specs(tr)

    def body(zu_ref, zv_ref, zga_ref, zgb_ref, yb_ref, g_ref, b_ref, ws_ref, bs_ref, out_ref, vn_s):
        vhat = _layer_norm_stats(_gelu(zv_ref[...]))
        vn_s[...] = (vhat * g_ref[...] + b_ref[...]).astype(MXU_DTYPE)
        for g in range(A_GROUPS):
            w = _tril_weights(ws_ref, g)
            bias = bs_ref[:, g:g + 1]
            cols = slice(g * CHUNK, (g + 1) * CHUNK)
            for c in range(tr // CHUNK):
                rows = slice(c * CHUNK, (c + 1) * CHUNK)
                mixed = jnp.dot(w, vn_s[rows, cols], preferred_element_type=F32) + bias
                ya = _gelu(zu_ref[rows, cols]) * mixed
                merged = _sigmoid(zga_ref[rows, cols]) * ya + _sigmoid(zgb_ref[rows, cols]) * yb_ref[rows, cols]
                out_ref[rows, cols] = merged.astype(MXU_DTYPE)

    return pl.pallas_call(
        body, name="mix_fwd", out_shape=jax.ShapeDtypeStruct((T, D_MODEL), MXU_DTYPE), grid=(T // tr,),
        in_specs=[zcol(0), zcol(1), zcol(2), zcol(3), row, vec, vec, ws_spec, bs_spec], out_specs=row,
        scratch_shapes=[pltpu.VMEM((tr, D_MODEL), MXU_DTYPE)],
        compiler_params=_params(("parallel",), 8 * _nbytes((tr, D_MODEL), F32)),
    )(z, z, z, z, yb, ln_g, ln_b, ws, bs_t)


def _mix_bwd(z, yb, dm, ln_g, ln_b, ws, bs_t):
    T = z.shape[0]
    tr = MIX_ROWS
    zcol, row, vec, ws_spec, bs_spec = _mix_specs(tr)

    def body(zu_ref, zv_ref, zga_ref, zgb_ref, yb_ref, dm_ref, g_ref, b_ref, ws_ref, bs_ref,
             dz_ref, dyb_ref, dl_ref, gws_ref, gbs_ref, glg_ref, glb_ref, vn_s, dvn_s):
        @pl.when(pl.program_id(0) == 0)
        def _():
            gws_ref[...] = jnp.zeros_like(gws_ref)
            gbs_ref[...] = jnp.zeros_like(gbs_ref)
            glg_ref[...] = jnp.zeros_like(glg_ref)
            glb_ref[...] = jnp.zeros_like(glb_ref)

        lane = lax.broadcasted_iota(jnp.int32, (CHUNK, 128), 1)
        va, dgelu_v = _gelu_and_grad(zv_ref[...])
        mu = jnp.mean(va, axis=-1, keepdims=True)
        xc = va - mu
        rs = lax.rsqrt(jnp.mean(xc * xc, axis=-1, keepdims=True) + EPS)
        vhat = xc * rs
        vn_s[...] = (vhat * g_ref[...] + b_ref[...]).astype(MXU_DTYPE)
        gbs_acc = jnp.zeros((CHUNK, 128), F32)
        for g in range(A_GROUPS):
            w = _tril_weights(ws_ref, g)
            bias = bs_ref[:, g:g + 1]
            cols = slice(g * CHUNK, (g + 1) * CHUNK)
            gw_acc = jnp.zeros((CHUNK, CHUNK), F32)
            for c in range(tr // CHUNK):
                rows = slice(c * CHUNK, (c + 1) * CHUNK)
                vn = vn_s[rows, cols]
                mixed = jnp.dot(w, vn, preferred_element_type=F32) + bias
                ua, dgelu_u = _gelu_and_grad(zu_ref[rows, cols])
                dmv = dm_ref[rows, cols]
                sa = _sigmoid(zga_ref[rows, cols])
                dya = dmv * sa
                dz_ref[rows, 2 * D_MODEL + g * CHUNK:2 * D_MODEL + (g + 1) * CHUNK] = (
                    dmv * (ua * mixed) * (sa * (1.0 - sa))).astype(dz_ref.dtype)
                dz_ref[rows, cols] = (dya * mixed * dgelu_u).astype(dz_ref.dtype)
                dmix = dya * ua
                gbs_acc = gbs_acc + jnp.where(lane == g, jnp.sum(dmix, axis=-1, keepdims=True), 0.0)
                dmix_b = dmix.astype(MXU_DTYPE)
                gw_acc = gw_acc + lax.dot_general(dmix_b, vn, _DIMS["nt"], preferred_element_type=F32)
                dvn_s[rows, cols] = lax.dot_general(w, dmix_b, _DIMS["tn"], preferred_element_type=F32)
            gws_ref[g] += jnp.where(_causal_mask(CHUNK), gw_acc, 0.0)
        gbs_ref[...] += gbs_acc

        dvn = dvn_s[...]
        glg_ref[...] += jnp.sum(dvn * vhat, axis=0, keepdims=True)
        glb_ref[...] += jnp.sum(dvn, axis=0, keepdims=True)
        dvh = dvn * g_ref[...]
        dva = rs * (dvh - jnp.mean(dvh, axis=-1, keepdims=True) - vhat * jnp.mean(dvh * vhat, axis=-1, keepdims=True))
        dz_ref[:, D_MODEL:2 * D_MODEL] = (dva * dgelu_v).astype(dz_ref.dtype)

        dmv = dm_ref[...]
        ybv = yb_ref[...]
        sb = _sigmoid(zgb_ref[...])
        dyb = dmv * sb
        dyb_ref[...] = dyb.astype(dyb_ref.dtype)
        dz_ref[:, 3 * D_MODEL:4 * D_MODEL] = (dmv * ybv * (sb * (1.0 - sb))).astype(dz_ref.dtype)
        dz_ref[:, 4 * D_MODEL:] = jnp.zeros((tr, LAT), dz_ref.dtype)
        prod = dyb * ybv
        sel = (lax.broadcasted_iota(jnp.int32, (HEADS, D_MODEL), 1) // NOPE
               == lax.broadcasted_iota(jnp.int32, (HEADS, D_MODEL), 0)).astype(jnp.bfloat16)
        hi = prod.astype(jnp.bfloat16)
        rest = prod - hi.astype(F32)
        mid = rest.astype(jnp.bfloat16)
        lo = (rest - mid.astype(F32)).astype(jnp.bfloat16)
        dl_ref[...] = (lax.dot_general(sel, hi, _DIMS["nt"], preferred_element_type=F32)
                       + lax.dot_general(sel, mid, _DIMS["nt"], preferred_element_type=F32)
                       + lax.dot_general(sel, lo, _DIMS["nt"], preferred_element_type=F32))

    return pl.pallas_call(
        body, name="mix_bwd",
        out_shape=(jax.ShapeDtypeStruct((T, IN_PAD), MXU_DTYPE), jax.ShapeDtypeStruct((T, D_MODEL), MXU_DTYPE),
                   jax.ShapeDtypeStruct((HEADS, T), F32), jax.ShapeDtypeStruct((A_GROUPS, CHUNK, CHUNK), F32),
                   jax.ShapeDtypeStruct((CHUNK, 128), F32), jax.ShapeDtypeStruct((1, D_MODEL), F32),
                   jax.ShapeDtypeStruct((1, D_MODEL), F32)),
        grid=(T // tr,),
        in_specs=[zcol(0), zcol(1), zcol(2), zcol(3), row, row, vec, vec, ws_spec, bs_spec],
        out_specs=(pl.BlockSpec((tr, IN_PAD), lambda i: (i, 0)), row, pl.BlockSpec((HEADS, tr), lambda i: (0, i)),
                   ws_spec, bs_spec, vec, vec),
        scratch_shapes=[pltpu.VMEM((tr, D_MODEL), MXU_DTYPE), pltpu.VMEM((tr, D_MODEL), F32)],
        compiler_params=_params(("arbitrary",), 12 * _nbytes((tr, D_MODEL), F32)),
    )(z, z, z, z, yb, dm, ln_g, ln_b, ws, bs_t)


def _lat_bwd(dz, z, dq, dk, dv, gq, gkv, wq, wkv, cos_a, sin_a, tr=256):
    T = z.shape[0]
    lat_blk = (4 * D_MODEL) // LAT

    def body(dz_in, z_ref, dq_ref, dk_ref, dv_ref, gq_ref, gkv_ref, wq_ref, wkv_ref, cos_ref, sin_ref,
             dz_ref, dqr_ref, dkv_ref, ggq_ref, ggkv_ref):
        del dz_in

        @pl.when(pl.program_id(0) == 0)
        def _():
            ggq_ref[...] = jnp.zeros_like(ggq_ref)
            ggkv_ref[...] = jnp.zeros_like(ggkv_ref)

        cos_v, sin_v = cos_ref[...], sin_ref[...]
        dkr = jnp.zeros((tr, 128), F32)
        for h in range(HEADS):
            o = h * HEAD_PAD
            dqr_ref[:, o:o + NOPE] = dq_ref[:, o:o + NOPE].astype(MXU_DTYPE)
            dqr_ref[:, o + NOPE:o + HEAD_PAD] = _rope_mix_bwd(dq_ref[:, o + NOPE:o + HEAD_PAD], cos_v, sin_v).astype(MXU_DTYPE)
            dkv_ref[:, h * NOPE:(h + 1) * NOPE] = dk_ref[:, o:o + NOPE].astype(MXU_DTYPE)
            dkr = dkr + _rope_mix_bwd(dk_ref[:, o + NOPE:o + HEAD_PAD], cos_v, sin_v)
        dkv_ref[:, HEADS * NOPE:] = dv_ref[...]
        dcqn = lax.dot_general(dqr_ref[...], wq_ref[...], _DIMS["nt"], preferred_element_type=F32)
        dckvn = lax.dot_general(dkv_ref[...], wkv_ref[...], _DIMS["nt"], preferred_element_type=F32)

        zl = z_ref[...]

        def rms_bwd(c, dn, g_ref, gg_ref):
            r = lax.rsqrt(jnp.mean(c * c, axis=-1, keepdims=True) + EPS)
            ch = c * r
            gg_ref[...] += jnp.sum(dn * ch, axis=0, keepdims=True)
            dch = dn * g_ref[...]
            return r * (dch - ch * jnp.mean(dch * ch, axis=-1, keepdims=True))

        dz_ref[:, :Q_RANK] = rms_bwd(zl[:, :Q_RANK], dcqn, gq_ref, ggq_ref).astype(dz_ref.dtype)
        dz_ref[:, Q_RANK:Q_RANK + KV_RANK] = rms_bwd(zl[:, Q_RANK:Q_RANK + KV_RANK], dckvn, gkv_ref, ggkv_ref).astype(dz_ref.dtype)
        dz_ref[:, Q_RANK + KV_RANK:] = dkr.astype(dz_ref.dtype)

    def row(w):
        return pl.BlockSpec((tr, w), lambda i: (i, 0))

    def full(a):
        return pl.BlockSpec(a.shape, lambda i: (0, 0))

    lat = pl.BlockSpec((tr, LAT), lambda i: (i, lat_blk))
    return pl.pallas_call(
        body, name="lat_bwd",
        out_shape=(jax.ShapeDtypeStruct(dz.shape, dz.dtype), jax.ShapeDtypeStruct((T, HEADS * HEAD_PAD), MXU_DTYPE),
                   jax.ShapeDtypeStruct((T, 2 * HEADS * NOPE), MXU_DTYPE), jax.ShapeDtypeStruct(gq.shape, F32),
                   jax.ShapeDtypeStruct(gkv.shape, F32)),
        grid=(T // tr,),
        in_specs=[pl.BlockSpec(memory_space=pl.ANY), lat, row(HEADS * HEAD_PAD), row(HEADS * HEAD_PAD), row(HEADS * NOPE),
                  full(gq), full(gkv), full(wq), full(wkv), row(128), row(128)],
        out_specs=(lat, row(HEADS * HEAD_PAD), row(2 * HEADS * NOPE), full(gq), full(gkv)),
        input_output_aliases={0: 0},
        compiler_params=_params(("arbitrary",), 8 * _nbytes((tr, HEADS * HEAD_PAD), F32)),
    )(dz, z, dq, dk, dv, gq, gkv, wq, wkv, cos_a, sin_a)


GATE_ROWS = 64
HALO = 8


def _taps(ref, half, r, first):
    C = GATE_ROWS
    if first:
        xs = jnp.concatenate([jnp.zeros((HALO, ref.shape[-1]), F32), ref[half, 0:C, :]], axis=0)
    else:
        xs = ref[half, pl.ds(pl.multiple_of(r * C - HALO, HALO), C + HALO), :]
    return xs[HALO:, :], pltpu.roll(xs, 1, 0)[HALO:, :], pltpu.roll(xs, 2, 0)[HALO:, :]


def _conv_taps(taps, cw, cb):
    x0, x1, x2 = taps
    return cb + cw[0:1, :] * x2 + cw[1:2, :] * x1 + cw[2:3, :] * x0


def _fold8(x):
    acc = x[0:8, :]
    for i in range(1, x.shape[0] // 8):
        acc = acc + x[8 * i:8 * (i + 1), :]
    return acc


def _gate_fwd(up3, conv_w, conv_b, B, S):
    T = B * S
    W = FF_TILE
    C = GATE_ROWS

    def body(up_ref, cw_ref, cb_ref, act_ref):
        def chunk(r, first):
            gate = _conv_taps(_taps(up_ref, 0, r, first), cw_ref[0], cb_ref[0])
            val = _conv_taps(_taps(up_ref, 1, r, first), cw_ref[1], cb_ref[1])
            base = 0 if first else pl.multiple_of(r * C, C)
            act_ref[pl.ds(base, C), :] = (gate * _sigmoid(gate) * val).astype(act_ref.dtype)

        chunk(0, True)

        @pl.loop(1, S // C)
        def _(r):
            chunk(r, False)

    return pl.pallas_call(
        body, name="gate_fwd", out_shape=jax.ShapeDtypeStruct((T, D_FF), MXU_DTYPE), grid=(B, N_FF_TILES),
        in_specs=[pl.BlockSpec((2, S, W), lambda b, j: (0, b, j)), pl.BlockSpec((2, 3, W), lambda b, j: (0, 0, j)),
                  pl.BlockSpec((2, 1, W), lambda b, j: (0, 0, j))],
        out_specs=pl.BlockSpec((S, W), lambda b, j: (b, j)),
        compiler_params=_params(("parallel", "parallel"), 6 * _nbytes((S, W), F32)),
    )(up3, conv_w, conv_b)


def _gate_bwd(up3, dact, conv_w, conv_b, B, S):
    T = B * S
    W = FF_TILE
    C = GATE_ROWS

    def body(up_ref, da_ref, cw_ref, cb_ref, dup_ref, gcw_ref, gcb_ref, d_s):
        @pl.when(pl.program_id(1) == 0)
        def _():
            gcw_ref[...] = jnp.zeros_like(gcw_ref)
            gcb_ref[...] = jnp.zeros_like(gcb_ref)

        def chunk(r, first, sums):
            rows = pl.ds(0 if first else pl.multiple_of(r * C, C), C)
            taps = [_taps(up_ref, half, r, first) for half in (0, 1)]
            gate = _conv_taps(taps[0], cw_ref[0], cb_ref[0])
            val = _conv_taps(taps[1], cw_ref[1], cb_ref[1])
            sg = _sigmoid(gate)
            da = da_ref[rows, :]
            d_halves = (da * val * (sg * (1.0 + gate * (1.0 - sg))), da * (gate * sg))
            out = []
            for half, dup in enumerate(d_halves):
                d_s[half, rows, :] = dup
                x0, x1, x2 = taps[half]
                sb, s0, s1, s2 = sums[half]
                out.append((sb + _fold8(dup), s0 + _fold8(dup * x2), s1 + _fold8(dup * x1), s2 + _fold8(dup * x0)))
            return tuple(out)

        zeros = tuple(tuple(jnp.zeros((8, W), F32) for _ in range(4)) for _ in range(2))
        sums = chunk(0, True, zeros)
        sums = lax.fori_loop(1, S // C, lambda r, s: chunk(r, False, s), sums)
        for half in (0, 1):
            sb, s0, s1, s2 = sums[half]
            gcb_ref[half] += jnp.sum(sb, axis=0, keepdims=True)
            gcw_ref[half, 0:1, :] += jnp.sum(s0, axis=0, keepdims=True)
            gcw_ref[half, 1:2, :] += jnp.sum(s1, axis=0, keepdims=True)
            gcw_ref[half, 2:3, :] += jnp.sum(s2, axis=0, keepdims=True)

        d_s[:, S:S + HALO, :] = jnp.zeros((2, HALO, W), F32)

        @pl.loop(0, S // C)
        def _(r):
            base = pl.multiple_of(r * C, C)
            for half in (0, 1):
                ds_ = d_s[half, pl.ds(base, C + HALO), :]
                cw = cw_ref[half]
                dx = (cw[2:3, :] * ds_[:C, :] + cw[1:2, :] * pltpu.roll(ds_, C + HALO - 1, 0)[:C, :]
                      + cw[0:1, :] * pltpu.roll(ds_, C + HALO - 2, 0)[:C, :])
                dup_ref[half, pl.ds(base, C), :] = dx.astype(dup_ref.dtype)

    up_spec = pl.BlockSpec((2, S, W), lambda j, b: (0, b, j))
    cw_spec = pl.BlockSpec((2, 3, W), lambda j, b: (0, 0, j))
    cb_spec = pl.BlockSpec((2, 1, W), lambda j, b: (0, 0, j))
    return pl.pallas_call(
        body, name="gate_bwd",
        out_shape=(jax.ShapeDtypeStruct((2, T, D_FF), MXU_DTYPE), jax.ShapeDtypeStruct((2, 3, D_FF), F32),
                   jax.ShapeDtypeStruct((2, 1, D_FF), F32)),
        grid=(N_FF_TILES, B),
        in_specs=[up_spec, pl.BlockSpec((S, W), lambda j, b: (b, j)), cw_spec, cb_spec],
        out_specs=(up_spec, cw_spec, cb_spec),
        scratch_shapes=[pltpu.VMEM((2, S + HALO, W), F32)],
        compiler_params=_params(("parallel", "arbitrary"), 10 * _nbytes((S, W), F32)),
    )(up3, dact, conv_w, conv_b)


def _final(x2, tgt, g, tr=512):
    T, D = x2.shape

    def body(x_ref, t_ref, g_ref, dx_ref, loss_ref, gg_ref):
        @pl.when(pl.program_id(0) == 0)
        def _():
            loss_ref[...] = jnp.zeros_like(loss_ref)
            gg_ref[...] = jnp.zeros_like(gg_ref)

        xv = x_ref[...]
        gv = g_ref[...]
        r = lax.rsqrt(jnp.mean(xv * xv, axis=-1, keepdims=True) + EPS)
        xn = xv * r
        err = xn * gv - t_ref[...]
        loss_ref[...] += 0.5 * jnp.sum(jnp.mean(err * err, axis=-1, keepdims=True), axis=0, keepdims=True)
        dy = err * (1.0 / D)
        gg_ref[...] += jnp.sum(dy * xn, axis=0, keepdims=True)
        dxn = dy * gv
        dx_ref[...] = r * (dxn - xn * jnp.mean(dxn * xn, axis=-1, keepdims=True))

    row = pl.BlockSpec((tr, D), lambda i: (i, 0))
    vec = pl.BlockSpec((1, D), lambda i: (0, 0))
    return pl.pallas_call(
        body, name="final_loss",
        out_shape=(jax.ShapeDtypeStruct((T, D), F32), jax.ShapeDtypeStruct((1, 128), F32), jax.ShapeDtypeStruct((1, D), F32)),
        grid=(T // tr,), in_specs=[row, row, vec],
        out_specs=(row, pl.BlockSpec((1, 128), lambda i: (0, 0)), vec),
        compiler_params=_params(("arbitrary",), 6 * _nbytes((tr, D), F32)),
    )(x2, tgt, g)


def _sum_slabs(parts, name, tr):
    rows, cols = parts[0].shape
    n = len(parts)

    def body(*refs):
        acc = refs[0][...]
        for r in refs[1:n]:
            acc = acc + r[...]
        refs[n][...] = acc

    blk = pl.BlockSpec((tr, cols), lambda i: (i, 0))
    return pl.pallas_call(
        body, name=name, out_shape=jax.ShapeDtypeStruct((rows, cols), F32), grid=(rows // tr,),
        in_specs=[blk] * n, out_specs=blk,
        compiler_params=_params(("parallel",), (n + 1) * _nbytes((tr, cols), F32)),
    )(*parts)


def _adamw(w, g, m, v, name):
    rows, cols = w.shape
    tr = rows
    for cand in (256, 128, 64, 32, 16, 8):
        if rows % cand == 0:
            tr = cand
            break
    c1 = 1.0 - ADAM_B1 ** ADAM_STEP
    c2 = 1.0 - ADAM_B2 ** ADAM_STEP

    def body(w_ref, g_ref, m_ref, v_ref, d_ref, nm_ref, nv_ref):
        gv = g_ref[...]
        nm = ADAM_B1 * m_ref[...] + (1.0 - ADAM_B1) * gv
        nv = ADAM_B2 * v_ref[...] + (1.0 - ADAM_B2) * (gv * gv)
        nm_ref[...] = nm
        nv_ref[...] = nv
        d_ref[...] = -ADAM_LR * ((nm / c1) / (jnp.sqrt(nv / c2) + ADAM_EPS) + ADAM_WD * w_ref[...])

    blk = pl.BlockSpec((tr, cols), lambda i: (i, 0))
    sds = jax.ShapeDtypeStruct((rows, cols), F32)
    return pl.pallas_call(
        body, name=name, out_shape=(sds, sds, sds), grid=(rows // tr,), in_specs=[blk] * 4, out_specs=(blk, blk, blk),
        compiler_params=_params(("parallel",), 7 * _nbytes((tr, cols), F32)),
    )(w, g, m, v)


_ANY = pl.BlockSpec(memory_space=pl.ANY)


def _place():
    x, y, c = lax.axis_index("x"), lax.axis_index("y"), lax.axis_index("c")
    chips = [(1 - x, y), (x, 1 - y), (1 - x, 1 - y)]
    return x, y, c, chips


def _gather_weights(shards, conv_w):
    n = len(shards)

    def body(*refs):
        ins, cw_ref, outs, cwo_ref = refs[:n], refs[n], refs[n + 1:2 * n + 1], refs[2 * n + 1]
        send, recv, fsend, frecv, csend, crecv, osend, orecv = refs[2 * n + 2:]
        x, y, c, chips = _place()
        me = 2 * x + y
        sources = list(ins) + [cw_ref]
        targets = list(outs) + [cwo_ref]
        first, passed = [], []
        for w in range(n + 1):
            first.append(pltpu.make_async_remote_copy(
                src_ref=sources[w], dst_ref=targets[w].at[me], send_sem=osend.at[w], recv_sem=orecv.at[w],
                device_id=(x, y, 1 - c), device_id_type=MESH))
        for w in range(n):
            for j, (px, py) in enumerate(chips):
                first.append(pltpu.make_async_remote_copy(
                    src_ref=ins[w].at[c], dst_ref=outs[w].at[me, c], send_sem=send.at[3 * w + j],
                    recv_sem=recv.at[3 * w + j], device_id=(px, py, c), device_id_type=MESH))
        for j, (px, py) in enumerate(chips):
            first.append(pltpu.make_async_remote_copy(
                src_ref=cw_ref, dst_ref=cwo_ref.at[me], send_sem=csend.at[j], recv_sem=crecv.at[j],
                device_id=(px, py, c), device_id_type=MESH))
        for cp in first:
            cp.start()
        for w in range(n):
            for j, (px, py) in enumerate(chips):
                landed = outs[w].at[2 * px + py, c]
                pltpu.make_async_remote_copy(src_ref=landed, dst_ref=landed, send_sem=send.at[3 * w + j],
                                             recv_sem=recv.at[3 * w + j], device_id=(px, py, c),
                                             device_id_type=MESH).wait_recv()
                fw = pltpu.make_async_remote_copy(src_ref=landed, dst_ref=landed, send_sem=fsend.at[3 * w + j],
                                                  recv_sem=frecv.at[3 * w + j], device_id=(x, y, 1 - c),
                                                  device_id_type=MESH)
                fw.start()
                passed.append(fw)
        for w in range(n):
            for j, (px, py) in enumerate(chips):
                other = outs[w].at[2 * px + py, 1 - c]
                pltpu.make_async_remote_copy(src_ref=other, dst_ref=other, send_sem=fsend.at[3 * w + j],
                                             recv_sem=frecv.at[3 * w + j], device_id=(x, y, 1 - c),
                                             device_id_type=MESH).wait_recv()
        for j, (px, py) in enumerate(chips):
            pltpu.make_async_remote_copy(src_ref=cw_ref, dst_ref=cwo_ref.at[2 * px + py], send_sem=csend.at[j],
                                         recv_sem=crecv.at[j], device_id=(px, py, c), device_id_type=MESH).wait_recv()
        for w in range(n + 1):
            own = targets[w].at[me]
            pltpu.make_async_remote_copy(src_ref=own, dst_ref=own, send_sem=osend.at[w], recv_sem=orecv.at[w],
                                         device_id=(x, y, 1 - c), device_id_type=MESH).wait_recv()
        for cp in first + passed:
            cp.wait_send()

    dma = lambda k: pltpu.SemaphoreType.DMA((k,))
    return pl.pallas_call(
        body, name="gather_weights",
        out_shape=tuple(jax.ShapeDtypeStruct((N_CHIPS,) + s.shape, s.dtype) for s in list(shards) + [conv_w]),
        in_specs=[_ANY] * (n + 1), out_specs=tuple([_ANY] * (n + 1)),
        scratch_shapes=[dma(3 * n), dma(3 * n), dma(3 * n), dma(3 * n), dma(3), dma(3), dma(n + 1), dma(n + 1)],
    )(*shards, conv_w)


def _swap_halves(gs):
    n = len(gs)

    def body(*refs):
        ins, outs, send, recv = refs[:n], refs[n:2 * n], refs[2 * n], refs[2 * n + 1]
        x, y, c, _ = _place()
        cps = []
        for w in range(n):
            cps.append(pltpu.make_async_remote_copy(
                src_ref=ins[w].at[:, 1 - c], dst_ref=outs[w], send_sem=send.at[w], recv_sem=recv.at[w],
                device_id=(x, y, 1 - c), device_id_type=MESH))
        for cp in cps:
            cp.start()
        for cp in cps:
            cp.wait()

    return pl.pallas_call(
        body, name="grad_swap_halves",
        out_shape=tuple(jax.ShapeDtypeStruct((g.shape[0],) + g.shape[2:], g.dtype) for g in gs),
        in_specs=[_ANY] * n, out_specs=tuple([_ANY] * n),
        scratch_shapes=[pltpu.SemaphoreType.DMA((n,)), pltpu.SemaphoreType.DMA((n,))],
    )(*gs)


GRAD_PAYLOAD = jnp.bfloat16


def _pair_sum(gs, gots):
    n = len(gs)
    core = lax.axis_index("c").astype(jnp.int32).reshape(1)

    def body(core_ref, *refs):
        del core_ref
        for w in range(n):
            refs[2 * n + w][...] = (refs[w][...] + refs[n + w][...]).astype(GRAD_PAYLOAD)

    in_specs, out_specs, out_shape, nbytes = [], [], [], 0
    for g in gs:
        q = g.shape[1] // 4
        in_specs.append(pl.BlockSpec((1, q, g.shape[2]), lambda s, r, core: (s, 2 * core[0] + r, 0)))
        nbytes += 3 * _nbytes((q, g.shape[2]), F32)
    for g in gs:
        q = g.shape[1] // 4
        in_specs.append(pl.BlockSpec((1, q, g.shape[2]), lambda s, r, core: (s, r, 0)))
        out_specs.append(pl.BlockSpec((1, q, g.shape[2]), lambda s, r, core: (s, r, 0)))
        out_shape.append(jax.ShapeDtypeStruct((g.shape[0], g.shape[1] // 2, g.shape[2]), GRAD_PAYLOAD))
    return pl.pallas_call(
        body, name="grad_pair_sum", out_shape=tuple(out_shape),
        grid_spec=pltpu.PrefetchScalarGridSpec(num_scalar_prefetch=1, grid=(N_CHIPS, 2), in_specs=in_specs,
                                               out_specs=tuple(out_specs)),
        compiler_params=_params(("parallel", "parallel"), nbytes),
    )(core, *gs, *gots)


def _scatter_to_chips(ps):
    n = len(ps)

    def body(*refs):
        ins, outs, send, recv = refs[:n], refs[n:2 * n], refs[2 * n], refs[2 * n + 1]
        x, y, c, chips = _place()
        me = 2 * x + y
        cps = []
        for w in range(n):
            for j, (px, py) in enumerate(chips):
                cps.append(pltpu.make_async_remote_copy(
                    src_ref=ins[w].at[2 * px + py], dst_ref=outs[w].at[me], send_sem=send.at[3 * w + j],
                    recv_sem=recv.at[3 * w + j], device_id=(px, py, c), device_id_type=MESH))
        for cp in cps:
            cp.start()
        for w in range(n):
            for j, (px, py) in enumerate(chips):
                slot = outs[w].at[2 * px + py]
                pltpu.make_async_remote_copy(src_ref=slot, dst_ref=slot, send_sem=send.at[3 * w + j],
                                             recv_sem=recv.at[3 * w + j], device_id=(px, py, c),
                                             device_id_type=MESH).wait_recv()
        for cp in cps:
            cp.wait_send()

    dma = lambda k: pltpu.SemaphoreType.DMA((k,))
    return pl.pallas_call(
        body, name="grad_scatter_chips", out_shape=tuple(jax.ShapeDtypeStruct(p.shape, p.dtype) for p in ps),
        in_specs=[_ANY] * n, out_specs=tuple([_ANY] * n), scratch_shapes=[dma(3 * n), dma(3 * n)],
    )(*ps)


def _chip_sum(ps, landed):
    n = len(ps)
    x, y, c = lax.axis_index("x"), lax.axis_index("y"), lax.axis_index("c")
    where = jnp.stack([2 * x + y, 2 * (1 - x) + y, 2 * x + (1 - y), 2 * (1 - x) + (1 - y), c]).astype(jnp.int32)

    def body(where_ref, *refs):
        del where_ref
        for w in range(n):
            terms = [refs[4 * w + t][...].astype(F32) for t in range(4)]
            refs[4 * n + w][...] = ((terms[0] + terms[1]) + terms[2]) + terms[3]

    in_specs, out_specs, out_shape, args, nbytes = [], [], [], [], 0
    for p, a in zip(ps, landed):
        q = a.shape[1] // 2
        blk = (1, q, a.shape[2])
        in_specs.append(pl.BlockSpec(blk, lambda r, where: (where[0], r, 0)))
        args.append(p)
        for t in (1, 2, 3):
            in_specs.append(pl.BlockSpec(blk, lambda r, where, t=t: (where[t], r, 0)))
            args.append(a)
        out_specs.append(pl.BlockSpec(blk, lambda r, where: (where[4], r, 0)))
        out_shape.append(jax.ShapeDtypeStruct((2,) + a.shape[1:], F32))
        nbytes += 4 * _nbytes(blk, F32)
    return pl.pallas_call(
        body, name="grad_chip_sum", out_shape=tuple(out_shape),
        grid_spec=pltpu.PrefetchScalarGridSpec(num_scalar_prefetch=1, grid=(2,), in_specs=in_specs,
                                               out_specs=tuple(out_specs)),
        compiler_params=_params(("parallel",), nbytes),
    )(where, *args)


def _join_halves(ss):
    n = len(ss)

    def body(*refs):
        outs, send, recv = refs[n:2 * n], refs[2 * n], refs[2 * n + 1]
        x, y, c, _ = _place()
        cps = []
        for w in range(n):
            cps.append(pltpu.make_async_remote_copy(
                src_ref=outs[w].at[c], dst_ref=outs[w].at[c], send_sem=send.at[w], recv_sem=recv.at[w],
                device_id=(x, y, 1 - c), device_id_type=MESH))
        for cp in cps:
            cp.start()
        for w in range(n):
            got = outs[w].at[1 - c]
            pltpu.make_async_remote_copy(src_ref=got, dst_ref=got, send_sem=send.at[w], recv_sem=recv.at[w],
                                         device_id=(x, y, 1 - c), device_id_type=MESH).wait_recv()
        for cp in cps:
            cp.wait_send()

    dma = lambda k: pltpu.SemaphoreType.DMA((k,))
    return pl.pallas_call(
        body, name="grad_join_halves",
        out_shape=tuple(jax.ShapeDtypeStruct(s.shape, s.dtype) for s in ss),
        in_specs=[_ANY] * n, out_specs=tuple([_ANY] * n), input_output_aliases={w: w for w in range(n)},
        scratch_shapes=[dma(n), dma(n)],
    )(*ss)


def _gather_all(p):
    rows, cols = p.shape

    def body(p_ref, out_ref, send, recv):
        x, y, c, _ = _place()
        me = 4 * x + 2 * y + c
        flips = [(fx, fy, fc) for fx in (0, 1) for fy in (0, 1) for fc in (0, 1)][1:]
        peers = [(x ^ fx, y ^ fy, c ^ fc) for fx, fy, fc in flips]
        cps = [pltpu.make_async_remote_copy(src_ref=p_ref, dst_ref=out_ref.at[me], send_sem=send.at[j], recv_sem=recv.at[j],
                                            device_id=peer, device_id_type=MESH) for j, peer in enumerate(peers)]
        for cp in cps:
            cp.start()
        for j, (px, py, pc) in enumerate(peers):
            slot = out_ref.at[4 * px + 2 * py + pc]
            pltpu.make_async_remote_copy(src_ref=slot, dst_ref=slot, send_sem=send.at[j], recv_sem=recv.at[j],
                                         device_id=(px, py, pc), device_id_type=MESH).wait_recv()
        for cp in cps:
            cp.wait_send()

    dma7 = pltpu.SemaphoreType.DMA((7,))
    return pl.pallas_call(
        body, name="small_grads_gather", out_shape=jax.ShapeDtypeStruct((8, rows, cols), p.dtype),
        in_specs=[_ANY], out_specs=_ANY, scratch_shapes=[dma7, dma7],
    )(p)


def _rot_cols(w):
    a, b = jnp.split(w, 2, axis=-1)
    return jnp.concatenate([-b, a], axis=-1)


def _rot_cols_t(g):
    a, b = jnp.split(g, 2, axis=-1)
    return jnp.concatenate([b, -a], axis=-1)


def _cols_from_chips(a):
    n, r, cs = a.shape
    return jnp.transpose(a, (1, 0, 2)).reshape(r, n * cs)


def _cols_to_chips(a):
    r, cc = a.shape
    return jnp.transpose(a.reshape(r, N_CHIPS, cc // N_CHIPS), (1, 0, 2))


def _conv_w_split(cw):
    return jnp.swapaxes(cw.reshape(3, 2, D_FF), 0, 1)


def _conv_w_join(g):
    return jnp.swapaxes(g, 0, 1).reshape(3, 2 * D_FF)


_SEG =(D_MODEL, 2 * D_MODEL, 2 * D_MODEL + Q_RANK, 2 * D_MODEL + Q_RANK + KV_RANK, 2 * D_MODEL + Q_RANK + KV_RANK + ROPE,
        3 * D_MODEL + Q_RANK + KV_RANK + ROPE)


def _w_in_to_pad(w):
    u, v, cq, ckv, kr, ga, gb = jnp.split(w, _SEG, axis=1)
    return jnp.concatenate([u, v, ga, gb, cq, ckv, kr, _rot_cols(kr)], axis=1)


def _w_in_from_pad(g):
    u, v, ga, gb, cq, ckv, kr, krr = jnp.split(
        g, (D_MODEL, 2 * D_MODEL, 3 * D_MODEL, 4 * D_MODEL, 4 * D_MODEL + Q_RANK, 4 * D_MODEL + Q_RANK + KV_RANK,
            4 * D_MODEL + Q_RANK + KV_RANK + ROPE), axis=1)
    return jnp.concatenate([u, v, cq, ckv, kr + _rot_cols_t(krr), ga, gb], axis=1)


def _w_uq_to_pad(w):
    t = w.reshape(Q_RANK, HEADS, QK_DIM)
    nope, rope = t[..., :NOPE], t[..., NOPE:]
    return jnp.concatenate([nope, rope, _rot_cols(rope)], axis=-1).reshape(Q_RANK, HEADS * HEAD_PAD)


def _w_uq_from_pad(g):
    t = g.reshape(Q_RANK, HEADS, HEAD_PAD)
    nope, rope, rot = t[..., :NOPE], t[..., NOPE:QK_DIM], t[..., QK_DIM:]
    return jnp.concatenate([nope, rope + _rot_cols_t(rot)], axis=-1).reshape(Q_RANK, HEADS * QK_DIM)


def _w_ukv_to_pad(w):
    t = w.reshape(KV_RANK, HEADS, 2, NOPE)
    return jnp.swapaxes(t, 1, 2).reshape(KV_RANK, 2 * HEADS * NOPE)


def _w_ukv_from_pad(g):
    t = g.reshape(KV_RANK, 2, HEADS, NOPE)
    return jnp.swapaxes(t, 1, 2).reshape(KV_RANK, 2 * HEADS * NOPE)


def _rope_tables(positions):
    inv_freq = 1.0 / (ROPE_THETA ** (jnp.arange(0, ROPE, 2, dtype=F32) / ROPE))
    ang = positions.astype(F32).reshape(-1, 1) * inv_freq
    cos, sin = jnp.cos(ang), jnp.sin(ang)
    zero = jnp.zeros((ang.shape[0], 64), F32)
    return jnp.concatenate([cos, cos, zero], axis=1), jnp.concatenate([sin, sin, zero], axis=1)


_BIG = ("w_in", "w_uq", "w_ukv", "w_out", "w_up", "w_down")
UP_SHARD = 2 * D_FF // N_CHIPS


def _local_step(x, positions, tgt, wts):
    B, S, D = x.shape
    T = B * S
    xf = x.reshape(T, D)
    cos_a, sin_a = _rope_tables(positions)
    bs_t = jnp.pad(wts["a_spatial_b"].T, ((0, 0), (0, 128 - A_GROUPS)))

    h = _rms_fwd(xf, wts["mix_norm"], "norm1_fwd")
    z = _mm(h, wts["w_in"], "nn", "in_proj", tm=512, tn=1536, tk=D)
    q, k, v, cqn, ckvn = _lat_fwd(z, wts["q_a_norm"], wts["kv_a_norm"], wts["w_q"], wts["w_kv"], cos_a, sin_a)
    yb, *lses = _attn_fwd(q, k, v, B, S)
    merged = _mix_fwd(z, yb, wts["a_v_norm_g"], wts["a_v_norm_b"], wts["a_spatial_w"], bs_t)
    x1 = _mm(merged, wts["w_out"], "nn", "out_proj", tm=512, tn=D, tk=D, add=xf)
    h2 = _rms_fwd(x1, wts["ffn_norm"], "norm2_fwd")
    up_pre = _mm(h2, wts["w_up"], "nn", "up_proj", tm=512, tn=UP_SHARD, tk=D, dims=(T, 2 * D_FF, D),
                 b_spec=pl.BlockSpec((None, D, UP_SHARD), lambda i, j, k: (j, 0, 0)),
                 o_spec=pl.BlockSpec((None, 512, UP_SHARD), lambda i, j, k: (j // 2, i, j % 2)), out_shape=(2, T, D_FF))
    act = _gate_fwd(up_pre, wts["conv_w"], wts["conv_b"], B, S)
    x2 = _mm(act, wts["w_down"], "nn", "down_proj", tm=512, tn=D, tk=1408, add=x1)
    dx2, loss_row, g_final = _final(x2, tgt.reshape(T, D), wts["final_norm"])

    g = {"final_norm": g_final}
    dact = _mm(dx2, wts["w_down"], "nt", "down_proj_dx", tm=512, tn=1408, tk=D)
    g["w_down"] = _mm(act, dx2, "tn", "down_proj_dw", tm=1408, tn=D, tk=512)
    dup, g["conv_w"], g["conv_b"] = _gate_bwd(up_pre, dact, wts["conv_w"], wts["conv_b"], B, S)
    dh2 = _mm(dup, wts["w_up"], "nt", "up_proj_dx", tm=512, tn=D, tk=UP_SHARD, dims=(T, D, 2 * D_FF),
              a_spec=pl.BlockSpec((None, 512, UP_SHARD), lambda i, j, k: (k // 2, i, k % 2)),
              b_spec=pl.BlockSpec((None, D, UP_SHARD), lambda i, j, k: (k, 0, 0)))
    g["w_up"] = _mm(h2, dup, "tn", "up_proj_dw", tm=D, tn=UP_SHARD, tk=512, dims=(D, 2 * D_FF, T),
                    b_spec=pl.BlockSpec((None, 512, UP_SHARD), lambda i, j, k: (j // 2, k, j % 2)),
                    o_spec=pl.BlockSpec((None, D, UP_SHARD), lambda i, j, k: (j, 0, 0)), out_shape=(N_CHIPS, D, UP_SHARD))
    dx1, g["ffn_norm"] = _rms_bwd(x1, wts["ffn_norm"], dh2, dx2, "norm2_bwd")
    dm = _mm(dx1, wts["w_out"], "nt", "out_proj_dx", tm=512, tn=D, tk=D)
    g["w_out"] = _mm(merged, dx1, "tn", "out_proj_dw", tm=D, tn=D, tk=512)
    dz, dyb, dl, g["a_spatial_w"], gbs, g["a_v_norm_g"], g["a_v_norm_b"] = _mix_bwd(
        z, yb, dm, wts["a_v_norm_g"], wts["a_v_norm_b"], wts["a_spatial_w"], bs_t)
    g["a_spatial_b"] = gbs[:, :A_GROUPS].T
    delta = dl.reshape(HEADS * T // ATT_BLOCK, 1, ATT_BLOCK)
    dq, dk, dv = _attn_bwd(q, k, v, dyb, lses, delta, B, S)
    dz, dq_raw, dkv, g["q_a_norm"], g["kv_a_norm"] = _lat_bwd(
        dz, z, dq, dk, dv, wts["q_a_norm"], wts["kv_a_norm"], wts["w_q"], wts["w_kv"], cos_a, sin_a)
    g["w_q"] = _mm(cqn, dq_raw, "tn", "q_proj_dw", tm=Q_RANK, tn=HEADS * HEAD_PAD, tk=512)
    g["w_kv"] = _mm(ckvn, dkv, "tn", "kv_proj_dw", tm=KV_RANK, tn=2 * HEADS * NOPE, tk=512)
    dh = _mm(dz, wts["w_in"], "nt", "in_proj_dx", tm=512, tn=D, tk=1536)
    g["w_in"] = _mm(h, dz, "tn", "in_proj_dw", tm=D, tn=1536, tk=512)
    dx, g["mix_norm"] = _rms_bwd(xf, wts["mix_norm"], dh, dx1, "norm1_bwd")
    return loss_row[0, 0], dx.reshape(B, S, D), g


_SMALL = (("mix_norm", (1, D_MODEL)), ("a_v_norm_g", (1, D_MODEL)), ("a_v_norm_b", (1, D_MODEL)),
          ("a_spatial_w", (A_GROUPS * CHUNK, CHUNK)), ("a_spatial_b", (1, A_GROUPS * CHUNK)), ("q_a_norm", (1, Q_RANK)),
          ("kv_a_norm", (1, KV_RANK)), ("ffn_norm", (1, D_MODEL)), ("conv_b", (1, 2 * D_FF)), ("final_norm", (1, D_MODEL)),
          ("conv_w", (3, 2 * D_FF)))
_SMALL_ROWS = -(-sum(math.prod(s) for _, s in _SMALL) // (128 * 8)) * 8


def kernel(x, positions, mix_norm, w_in, a_v_norm_g, a_v_norm_b, a_spatial_w, a_spatial_b, q_a_norm, w_uq, kv_a_norm, w_ukv, w_out, ffn_norm, w_up, conv_w, conv_b, w_down, final_norm, loss_target, m_mix_norm, m_w_in, m_a_v_norm_g, m_a_v_norm_b, m_a_spatial_w, m_a_spatial_b, m_q_a_norm, m_w_uq, m_kv_a_norm, m_w_ukv, m_w_out, m_ffn_norm, m_w_up, m_conv_w, m_conv_b, m_w_down, m_final_norm, v_mix_norm, v_w_in, v_a_v_norm_g, v_a_v_norm_b, v_a_spatial_w, v_a_spatial_b, v_q_a_norm, v_w_uq, v_kv_a_norm, v_w_ukv, v_w_out, v_ffn_norm, v_w_up, v_conv_w, v_conv_b, v_w_down, v_final_norm):
    weights = dict(mix_norm=mix_norm, w_in=w_in, a_v_norm_g=a_v_norm_g, a_v_norm_b=a_v_norm_b, a_spatial_w=a_spatial_w,
                   a_spatial_b=a_spatial_b, q_a_norm=q_a_norm, w_uq=w_uq, kv_a_norm=kv_a_norm, w_ukv=w_ukv, w_out=w_out,
                   ffn_norm=ffn_norm, w_up=w_up, conv_w=conv_w, conv_b=conv_b, w_down=w_down, final_norm=final_norm)
    m_in = dict(mix_norm=m_mix_norm, w_in=m_w_in, a_v_norm_g=m_a_v_norm_g, a_v_norm_b=m_a_v_norm_b,
                a_spatial_w=m_a_spatial_w, a_spatial_b=m_a_spatial_b, q_a_norm=m_q_a_norm, w_uq=m_w_uq,
                kv_a_norm=m_kv_a_norm, w_ukv=m_w_ukv, w_out=m_w_out, ffn_norm=m_ffn_norm, w_up=m_w_up, conv_w=m_conv_w,
                conv_b=m_conv_b, w_down=m_w_down, final_norm=m_final_norm)
    v_in = dict(mix_norm=v_mix_norm, w_in=v_w_in, a_v_norm_g=v_a_v_norm_g, a_v_norm_b=v_a_v_norm_b,
                a_spatial_w=v_a_spatial_w, a_spatial_b=v_a_spatial_b, q_a_norm=v_q_a_norm, w_uq=v_w_uq,
                kv_a_norm=v_kv_a_norm, w_ukv=v_w_ukv, w_out=v_w_out, ffn_norm=v_ffn_norm, w_up=v_w_up, conv_w=v_conv_w,
                conv_b=v_conv_b, w_down=v_w_down, final_norm=v_final_norm)
    names = list(weights)
    chip = 2 * lax.axis_index("x") + lax.axis_index("y")

    def halves(a):
        return a.reshape(a.shape[:-2] + (2, a.shape[-2] // 2, a.shape[-1]))

    def whole(a):
        return a.reshape(a.shape[:-3] + (2 * a.shape[-2], a.shape[-1]))

    *gathered, cw_all = _gather_weights([halves(weights[n][0].astype(MXU_DTYPE)) for n in _BIG], conv_w[0])
    w_in_sh, w_uq_sh, w_ukv_sh, w_out_sh, w_up_sh, w_down_sh = (whole(a) for a in gathered)
    wts = dict(
        mix_norm=mix_norm, a_v_norm_g=a_v_norm_g, a_v_norm_b=a_v_norm_b, a_spatial_w=a_spatial_w[0],
        a_spatial_b=a_spatial_b[0], q_a_norm=q_a_norm, kv_a_norm=kv_a_norm, ffn_norm=ffn_norm,
        final_norm=final_norm.reshape(1, D_MODEL),
        w_in=_w_in_to_pad(_cols_from_chips(w_in_sh)), w_q=_w_uq_to_pad(_cols_from_chips(w_uq_sh)),
        w_kv=_w_ukv_to_pad(_cols_from_chips(w_ukv_sh)), w_out=w_out_sh.reshape(D_MODEL, D_MODEL), w_up=w_up_sh,
        w_down=w_down_sh.reshape(D_FF, D_MODEL), conv_w=_conv_w_split(_cols_from_chips(cw_all)),
        conv_b=conv_b.reshape(2, 1, D_FF))

    loss_part, grad_x, g = _local_step(x, positions, loss_target, wts)
    loss = lax.psum(loss_part, ("x", "y", "c"))

    slabs = [_cols_to_chips(_w_in_from_pad(g["w_in"])), _cols_to_chips(_w_uq_from_pad(g["w_q"])),
             _cols_to_chips(_w_ukv_from_pad(g["w_kv"])), g["w_out"].reshape(N_CHIPS, D_MODEL // N_CHIPS, D_MODEL),
             g["w_up"], g["w_down"].reshape(N_CHIPS, D_FF // N_CHIPS, D_MODEL)]
    pair_sums = _pair_sum(slabs, _swap_halves([halves(s) for s in slabs]))
    landed = _scatter_to_chips(pair_sums)
    g_big = dict(zip(_BIG, (whole(a) for a in _join_halves(_chip_sum(pair_sums, landed)))))

    g_small_parts = dict(g)
    g_small_parts["conv_w"] = _conv_w_join(g["conv_w"])
    g_small_parts["conv_b"] = g["conv_b"].reshape(1, 2 * D_FF)
    flat = jnp.concatenate([g_small_parts[n].reshape(-1) for n, _ in _SMALL])
    flat = jnp.pad(flat, (0, _SMALL_ROWS * 128 - flat.shape[0])).reshape(_SMALL_ROWS, 128)
    device = 2 * chip + lax.axis_index("c")
    everyone = lax.dynamic_update_slice(_gather_all(flat), flat[None], (device, 0, 0))
    total = _sum_slabs([everyone[j] for j in range(8)], "small_grads_sum", tr=_SMALL_ROWS).reshape(-1)
    g_small, o = {}, 0
    for n, shp in _SMALL:
        g_small[n] = total[o:o + math.prod(shp)].reshape(shp)
        o += math.prod(shp)
    g_small["conv_w"] = lax.dynamic_slice_in_dim(g_small["conv_w"], chip * 1408, 1408, axis=1)

    grads, deltas, new_m, new_v = {}, {}, {}, {}
    for n in names:
        w = weights[n]
        g2 = g_big[n] if n in g_big else g_small[n]
        shape2 = g2.shape
        d, nm, nv = _adamw(w.reshape(shape2), g2, m_in[n].reshape(shape2), v_in[n].reshape(shape2), "adamw_" + n)
        grads[n], deltas[n], new_m[n], new_v[n] = (t.reshape(w.shape) for t in (g2, d, nm, nv))
    return (loss, grad_x, *[grads[n] for n in names], *[deltas[n] for n in names], *[new_m[n] for n in names],
            *[new_v[n] for n in names])
```

```python
import functools
import math

import jax
import jax.numpy as jnp
from jax import lax
from jax.experimental import pallas as pl
from jax.experimental.pallas import tpu as pltpu

F32 = jnp.float32
MXU_DTYPE = jnp.bfloat16
MESH = pl.DeviceIdType.MESH

D_MODEL = 1024
EPS = 1e-6
A_GROUPS = 8
CHUNK = 128
HEADS = 8
NOPE = 128
ROPE = 64
QK_DIM = NOPE + ROPE
HEAD_PAD = 256
Q_RANK = 256
KV_RANK = 128
ROPE_THETA = 10000.0
D_FF = 2816
FF_TILE = 256
N_FF_TILES = D_FF // FF_TILE
LAT = 512
IN_PAD = 4 * D_MODEL + LAT
N_CHIPS = 4
ADAM_LR, ADAM_B1, ADAM_B2, ADAM_EPS, ADAM_WD, ADAM_STEP = 0.001, 0.9, 0.999, 1e-08, 0.01, 10

VMEM_CAP_V7X = 64 * 1024 * 1024
NEG = -1e30


def _params(sem, nbytes):
    limit = int(min(VMEM_CAP_V7X - (8 << 20), max(32 << 20, 3 * nbytes)))
    return pltpu.CompilerParams(dimension_semantics=sem, vmem_limit_bytes=limit)


def _nbytes(shape, dtype):
    return math.prod(shape) * jnp.dtype(dtype).itemsize


_DIMS = {"nn": (((1,), (0,)), ((), ())), "nt": (((1,), (1,)), ((), ())), "tn": (((0,), (0,)), ((), ()))}


def _mm(a, b, mode, name, *, tm, tn, tk, out_dtype=F32, add=None, dims=None, a_spec=None, b_spec=None,
        o_spec=None, out_shape=None):
    if dims is None:
        if mode == "nn":
            (M, K), (_, N) = a.shape, b.shape
        elif mode == "nt":
            (M, K), (N, _) = a.shape, b.shape
        else:
            (K, M), (_, N) = a.shape, b.shape
    else:
        M, N, K = dims
    a_blk = (tk, tm) if mode == "tn" else (tm, tk)
    b_blk = (tn, tk) if mode == "nt" else (tk, tn)
    if a_spec is None:
        a_spec = pl.BlockSpec(a_blk, (lambda i, j, k: (k, i)) if mode == "tn" else (lambda i, j, k: (i, k)))
    if b_spec is None:
        b_spec = pl.BlockSpec(b_blk, (lambda i, j, k: (j, k)) if mode == "nt" else (lambda i, j, k: (k, j)))
    if o_spec is None:
        o_spec = pl.BlockSpec((tm, tn), lambda i, j, k: (i, j))
    if out_shape is None:
        out_shape = (M, N)
    assert M % tm == 0 and N % tn == 0 and K % tk == 0, (name, M, N, K, tm, tn, tk)
    nk = K // tk
    contract = _DIMS[mode]
    has_add = add is not None

    def body(*refs):
        if has_add:
            a_ref, b_ref, add_ref, o_ref, acc = refs
        else:
            a_ref, b_ref, o_ref, acc = refs
        k = pl.program_id(2)

        @pl.when(k == 0)
        def _():
            acc[...] = jnp.zeros_like(acc)

        acc[...] += lax.dot_general(a_ref[...].astype(MXU_DTYPE), b_ref[...].astype(MXU_DTYPE), contract,
                                    preferred_element_type=F32)

        @pl.when(k == nk - 1)
        def _():
            r = acc[...]
            if has_add:
                r = r + add_ref[...]
            o_ref[...] = r.astype(out_dtype)

    in_specs = [a_spec, b_spec]
    args = [a, b]
    nbytes = _nbytes(a_blk, a.dtype) + _nbytes(b_blk, b.dtype) + 3 * _nbytes((tm, tn), F32)
    if has_add:
        in_specs.append(pl.BlockSpec((tm, tn), lambda i, j, k: (i, j)))
        args.append(add)
        nbytes += _nbytes((tm, tn), F32)
    return pl.pallas_call(
        body, name=name, out_shape=jax.ShapeDtypeStruct(out_shape, out_dtype),
        grid=(M // tm, N // tn, nk), in_specs=in_specs, out_specs=o_spec,
        scratch_shapes=[pltpu.VMEM((tm, tn), F32)],
        compiler_params=_params(("parallel", "parallel", "arbitrary"), nbytes),
    )(*args)


_GELU_C = math.sqrt(2.0 / math.pi)
_GELU_A = 0.044715


def _sigmoid(x):
    return 1.0 / (1.0 + jnp.exp(-x))


def _gelu(x):
    t = jnp.tanh(_GELU_C * (x + _GELU_A * (x * x * x)))
    return x * (0.5 * (1.0 + t))


def _gelu_and_grad(x):
    x2 = x * x
    t = jnp.tanh(_GELU_C * (x + _GELU_A * (x2 * x)))
    cdf = 0.5 * (1.0 + t)
    grad = cdf + 0.5 * x * (1.0 - t * t) * (_GELU_C * (1.0 + 3.0 * _GELU_A * x2))
    return x * cdf, grad


def _rope_mix(g, cos_a, sin_a):
    return g * cos_a + pltpu.roll(g, 64, 1) * sin_a


def _rope_mix_bwd(d, cos_a, sin_a):
    return d * cos_a + pltpu.roll(d * sin_a, 64, 1)


def _rms_fwd(x, g, name, tr=512):
    T, D = x.shape

    def body(x_ref, g_ref, h_ref):
        xv = x_ref[...]
        r = lax.rsqrt(jnp.mean(xv * xv, axis=-1, keepdims=True) + EPS)
        h_ref[...] = ((xv * r) * g_ref[...]).astype(h_ref.dtype)

    return pl.pallas_call(
        body, name=name, out_shape=jax.ShapeDtypeStruct((T, D), MXU_DTYPE), grid=(T // tr,),
        in_specs=[pl.BlockSpec((tr, D), lambda i: (i, 0)), pl.BlockSpec((1, D), lambda i: (0, 0))],
        out_specs=pl.BlockSpec((tr, D), lambda i: (i, 0)),
        compiler_params=_params(("parallel",), 3 * _nbytes((tr, D), F32)),
    )(x, g)


def _rms_bwd(x, g, dh, dres, name, tr=512):
    T, D = x.shape

    def body(x_ref, g_ref, dh_ref, dres_ref, dx_ref, gg_ref):
        @pl.when(pl.program_id(0) == 0)
        def _():
            gg_ref[...] = jnp.zeros_like(gg_ref)

        xv = x_ref[...]
        r = lax.rsqrt(jnp.mean(xv * xv, axis=-1, keepdims=True) + EPS)
        xn = xv * r
        dhv = dh_ref[...]
        dxn = dhv * g_ref[...]
        dx_ref[...] = dres_ref[...] + r * (dxn - xn * jnp.mean(dxn * xn, axis=-1, keepdims=True))
        gg_ref[...] += jnp.sum(dhv * xn, axis=0, keepdims=True)

    row = pl.BlockSpec((tr, D), lambda i: (i, 0))
    vec = pl.BlockSpec((1, D), lambda i: (0, 0))
    return pl.pallas_call(
        body, name=name,
        out_shape=(jax.ShapeDtypeStruct((T, D), F32), jax.ShapeDtypeStruct((1, D), F32)),
        grid=(T // tr,), in_specs=[row, vec, row, row], out_specs=(row, vec),
        compiler_params=_params(("arbitrary",), 6 * _nbytes((tr, D), F32)),
    )(x, g, dh, dres)


def _lat_fwd(z, gq, gkv, wq, wkv, cos_a, sin_a, tr=256):
    T = z.shape[0]
    lat_blk = (4 * D_MODEL) // LAT

    def body(z_ref, gq_ref, gkv_ref, wq_ref, wkv_ref, cos_ref, sin_ref, q_ref, k_ref, v_ref, cqn_ref, ckvn_ref):
        zl = z_ref[...]
        cos_v, sin_v = cos_ref[...], sin_ref[...]
        cq = zl[:, :Q_RANK]
        ckv = zl[:, Q_RANK:Q_RANK + KV_RANK]
        krb = zl[:, Q_RANK + KV_RANK:]
        cqn = ((cq * lax.rsqrt(jnp.mean(cq * cq, axis=-1, keepdims=True) + EPS)) * gq_ref[...]).astype(MXU_DTYPE)
        ckvn = ((ckv * lax.rsqrt(jnp.mean(ckv * ckv, axis=-1, keepdims=True) + EPS)) * gkv_ref[...]).astype(MXU_DTYPE)
        cqn_ref[...] = cqn
        ckvn_ref[...] = ckvn
        krr = _rope_mix(krb, cos_v, sin_v).astype(MXU_DTYPE)
        q = jnp.dot(cqn, wq_ref[...], preferred_element_type=F32)
        kv = jnp.dot(ckvn, wkv_ref[...], preferred_element_type=F32)
        for h in range(HEADS):
            o = h * HEAD_PAD
            q_ref[:, o:o + NOPE] = q[:, o:o + NOPE].astype(MXU_DTYPE)
            q_ref[:, o + NOPE:o + HEAD_PAD] = _rope_mix(q[:, o + NOPE:o + HEAD_PAD], cos_v, sin_v).astype(MXU_DTYPE)
            k_ref[:, o:o + NOPE] = kv[:, h * NOPE:(h + 1) * NOPE].astype(MXU_DTYPE)
            k_ref[:, o + NOPE:o + HEAD_PAD] = krr
        v_ref[...] = kv[:, HEADS * NOPE:].astype(MXU_DTYPE)

    def row(w):
        return pl.BlockSpec((tr, w), lambda i: (i, 0))

    def full(a):
        return pl.BlockSpec(a.shape, lambda i: (0, 0))

    return pl.pallas_call(
        body, name="lat_fwd",
        out_shape=(jax.ShapeDtypeStruct((T, HEADS * HEAD_PAD), MXU_DTYPE), jax.ShapeDtypeStruct((T, HEADS * HEAD_PAD), MXU_DTYPE),
                   jax.ShapeDtypeStruct((T, HEADS * NOPE), MXU_DTYPE), jax.ShapeDtypeStruct((T, Q_RANK), MXU_DTYPE),
                   jax.ShapeDtypeStruct((T, KV_RANK), MXU_DTYPE)),
        grid=(T // tr,),
        in_specs=[pl.BlockSpec((tr, LAT), lambda i: (i, lat_blk)), full(gq), full(gkv), full(wq), full(wkv), row(128), row(128)],
        out_specs=(row(HEADS * HEAD_PAD), row(HEADS * HEAD_PAD), row(HEADS * NOPE), row(Q_RANK), row(KV_RANK)),
        compiler_params=_params(("parallel",), 8 * _nbytes((tr, HEADS * HEAD_PAD), F32)),
    )(z, gq, gkv, wq, wkv, cos_a, sin_a)


ATT_BLOCK = 256
_SCALE = QK_DIM ** -0.5


def _causal_mask(n):
    return lax.broadcasted_iota(jnp.int32, (n, n), 1) <= lax.broadcasted_iota(jnp.int32, (n, n), 0)


def _causal_mask_t(n):
    return lax.broadcasted_iota(jnp.int32, (n, n), 0) <= lax.broadcasted_iota(jnp.int32, (n, n), 1)


ATT_HEADS = 2


def _attn_fwd(q, k, v, B, S):
    tq = ATT_BLOCK
    nq = S // tq
    T = B * S
    hp, groups = ATT_HEADS, HEADS // ATT_HEADS

    def body(q_ref, k_ref, v_ref, o_ref, *lse_refs):
        qi = pl.program_id(2)
        qs = [q_ref[:, t * HEAD_PAD:(t + 1) * HEAD_PAD] for t in range(hp)]

        def scores(j, t):
            rows = pl.ds(pl.multiple_of(j * tq, tq), tq)
            return lax.dot_general(k_ref[rows, t * HEAD_PAD:(t + 1) * HEAD_PAD], qs[t], _DIMS["nt"],
                                   preferred_element_type=F32)

        def step(j, carry, last):
            rows = pl.ds(pl.multiple_of(j * tq, tq), tq)
            out = []
            for t in range(hp):
                m, l, acc, st = carry[t]
                st_next = st if last else scores(j + 1, t)
                st = st * _SCALE
                if last:
                    st = jnp.where(_causal_mask_t(tq), st, NEG)
                m_new = jnp.maximum(m, jnp.max(st, axis=0, keepdims=True))
                alpha = jnp.exp(m - m_new)
                p = jnp.exp(st - m_new)
                l = alpha * l + jnp.sum(p, axis=0, keepdims=True)
                acc = alpha * acc + lax.dot_general(v_ref[rows, t * NOPE:(t + 1) * NOPE], p.astype(MXU_DTYPE),
                                                    _DIMS["tn"], preferred_element_type=F32)
                out.append((m_new, l, acc, st_next))
            return tuple(out)

        init = tuple((jnp.full((1, tq), NEG, F32), jnp.zeros((1, tq), F32), jnp.zeros((NOPE, tq), F32), scores(0, t))
                     for t in range(hp))
        carry = lax.fori_loop(0, qi, lambda j, c: step(j, c, False), init)
        carry = step(qi, carry, True)
        for t in range(hp):
            m, l, acc, _ = carry[t]
            o_ref[:, t * NOPE:(t + 1) * NOPE] = (acc / l).T
            lse_refs[t][0] = m + jnp.log(l)

    lse_sds = jax.ShapeDtypeStruct((groups * B * nq, 1, tq), F32)
    lse_spec = pl.BlockSpec((1, 1, tq), lambda b, h, i: ((h * B + b) * nq + i, 0, 0))
    return pl.pallas_call(
        body, name="attn_fwd",
        out_shape=(jax.ShapeDtypeStruct((T, HEADS * NOPE), F32),) + (lse_sds,) * hp,
        grid=(B, groups, nq),
        in_specs=[pl.BlockSpec((tq, hp * HEAD_PAD), lambda b, h, i: (b * nq + i, h)),
                  pl.BlockSpec((S, hp * HEAD_PAD), lambda b, h, i: (b, h)),
                  pl.BlockSpec((S, hp * NOPE), lambda b, h, i: (b, h))],
        out_specs=(pl.BlockSpec((tq, hp * NOPE), lambda b, h, i: (b * nq + i, h)),) + (lse_spec,) * hp,
        compiler_params=_params(("parallel", "parallel", "arbitrary"), 4 * hp * _nbytes((S, HEAD_PAD), MXU_DTYPE)),
    )(q, k, v)


def _attn_bwd(q, k, v, do, lses, delta, B, S):
    tq = ATT_BLOCK
    nq = S // tq
    T = B * S
    hp, groups = ATT_HEADS, HEADS // ATT_HEADS

    def body(q_ref, k_ref, v_ref, do_ref, *refs):
        lse_refs, dl_refs = refs[:hp], refs[hp:2 * hp]
        dq_ref, dk_ref, dv_ref = refs[2 * hp:]
        kj = pl.program_id(2)

        @pl.when(kj == 0)
        def _():
            dq_ref[...] = jnp.zeros_like(dq_ref)

        def products(i, t):
            rows = pl.ds(pl.multiple_of(i * tq, tq), tq)
            st = lax.dot_general(k_ref[:, t * HEAD_PAD:(t + 1) * HEAD_PAD], q_ref[rows, t * HEAD_PAD:(t + 1) * HEAD_PAD],
                                 _DIMS["nt"], preferred_element_type=F32)
            dpt = lax.dot_general(v_ref[:, t * NOPE:(t + 1) * NOPE], do_ref[rows, t * NOPE:(t + 1) * NOPE],
                                  _DIMS["nt"], preferred_element_type=F32)
            return st, dpt

        def step(i, carry, masked):
            rows = pl.ds(pl.multiple_of(i * tq, tq), tq)
            nxt = jnp.minimum(i + 1, nq - 1)
            out = []
            for t in range(hp):
                dk, dv, st, dpt = carry[t]
                st_next, dpt_next = products(nxt, t)
                qk_cols = slice(t * HEAD_PAD, (t + 1) * HEAD_PAD)
                v_cols = slice(t * NOPE, (t + 1) * NOPE)
                p = jnp.exp(st * _SCALE - lse_refs[t][i])
                if masked:
                    p = jnp.where(_causal_mask_t(tq), p, 0.0)
                dv = dv + jnp.dot(p.astype(MXU_DTYPE), do_ref[rows, v_cols], preferred_element_type=F32)
                ds = (p * (dpt - dl_refs[t][i]) * _SCALE).astype(MXU_DTYPE)
                dk = dk + jnp.dot(ds, q_ref[rows, qk_cols], preferred_element_type=F32)
                dq_ref[rows, qk_cols] += lax.dot_general(ds, k_ref[:, qk_cols], _DIMS["tn"], preferred_element_type=F32)
                out.append((dk, dv, st_next, dpt_next))
            return tuple(out)

        init = tuple((jnp.zeros((tq, HEAD_PAD), F32), jnp.zeros((tq, NOPE), F32)) + products(kj, t) for t in range(hp))
        carry = step(kj, init, True)
        carry = lax.fori_loop(kj + 1, nq, lambda i, c: step(i, c, False), carry)
        for t in range(hp):
            dk_ref[:, t * HEAD_PAD:(t + 1) * HEAD_PAD] = carry[t][0]
            dv_ref[:, t * NOPE:(t + 1) * NOPE] = carry[t][1].astype(dv_ref.dtype)

    seq = lambda w: pl.BlockSpec((S, w), lambda b, h, j: (b, h))
    blk = lambda w: pl.BlockSpec((tq, w), lambda b, h, j: (b * nq + j, h))
    lse_spec = pl.BlockSpec((nq, 1, tq), lambda b, h, j: (h * B + b, 0, 0))
    dl_specs = [pl.BlockSpec((nq, 1, tq), lambda b, h, j, t=t: ((h * hp + t) * B + b, 0, 0)) for t in range(hp)]
    return pl.pallas_call(
        body, name="attn_bwd",
        out_shape=(jax.ShapeDtypeStruct((T, HEADS * HEAD_PAD), F32), jax.ShapeDtypeStruct((T, HEADS * HEAD_PAD), F32),
                   jax.ShapeDtypeStruct((T, HEADS * NOPE), MXU_DTYPE)),
        grid=(B, groups, nq),
        in_specs=[seq(hp * HEAD_PAD), blk(hp * HEAD_PAD), blk(hp * NOPE), seq(hp * NOPE)] + [lse_spec] * hp + dl_specs,
        out_specs=(seq(hp * HEAD_PAD), blk(hp * HEAD_PAD), blk(hp * NOPE)),
        compiler_params=_params(("parallel", "parallel", "arbitrary"), 8 * hp * _nbytes((S, HEAD_PAD), F32)),
    )(q, k, v, do, *lses, *([delta] * hp))


MIX_ROWS = 256


def _tril_weights(ws_ref, g):
    return jnp.where(_causal_mask(CHUNK), ws_ref[g], 0.0).astype(MXU_DTYPE)


def _layer_norm_stats(va):
    mu = jnp.mean(va, axis=-1, keepdims=True)
    xc = va - mu
    rs = lax.rsqrt(jnp.mean(xc * xc, axis=-1, keepdims=True) + EPS)
    return xc * rs


def _mix_specs(tr):
    zcol = lambda c: pl.BlockSpec((tr, D_MODEL), lambda i, c=c: (i, c))
    row = pl.BlockSpec((tr, D_MODEL), lambda i: (i, 0))
    vec = pl.BlockSpec((1, D_MODEL), lambda i: (0, 0))
    ws = pl.BlockSpec((A_GROUPS, CHUNK, CHUNK), lambda i: (0, 0, 0))
    bs = pl.BlockSpec((CHUNK, 128), lambda i: (0, 0))
    return zcol, row, vec, ws, bs


def _mix_fwd(z, yb, ln_g, ln_b, ws, bs_t):
    T = z.shape[0]
    tr = MIX_ROWS
    zcol, row, vec, ws_spec, bs_spec = _mix_specs(tr)

    def body(zu_ref, zv_ref, zga_ref, zgb_ref, yb_ref, g_ref, b_ref, ws_ref, bs_ref, out_ref, vn_s):
        vhat = _layer_norm_stats(_gelu(zv_ref[...]))
        vn_s[...] = (vhat * g_ref[...] + b_ref[...]).astype(MXU_DTYPE)
        for g in range(A_GROUPS):
            w = _tril_weights(ws_ref, g)
            bias = bs_ref[:, g:g + 1]
            cols = slice(g * CHUNK, (g + 1) * CHUNK)
            for c in range(tr // CHUNK):
                rows = slice(c * CHUNK, (c + 1) * CHUNK)
                mixed = jnp.dot(w, vn_s[rows, cols], preferred_element_type=F32) + bias
                ya = _gelu(zu_ref[rows, cols]) * mixed
                merged = _sigmoid(zga_ref[rows, cols]) * ya + _sigmoid(zgb_ref[rows, cols]) * yb_ref[rows, cols]
                out_ref[rows, cols] = merged.astype(MXU_DTYPE)

    return pl.pallas_call(
        body, name="mix_fwd", out_shape=jax.ShapeDtypeStruct((T, D_MODEL), MXU_DTYPE), grid=(T // tr,),
        in_specs=[zcol(0), zcol(1), zcol(2), zcol(3), row, vec, vec, ws_spec, bs_spec], out_specs=row,
        scratch_shapes=[pltpu.VMEM((tr, D_MODEL), MXU_DTYPE)],
        compiler_params=_params(("parallel",), 8 * _nbytes((tr, D_MODEL), F32)),
    )(z, z, z, z, yb, ln_g, ln_b, ws, bs_t)


def _mix_bwd(z, yb, dm, ln_g, ln_b, ws, bs_t):
    T = z.shape[0]
    tr = MIX_ROWS
    zcol, row, vec, ws_spec, bs_spec = _mix_specs(tr)

    def body(zu_ref, zv_ref, zga_ref, zgb_ref, yb_ref, dm_ref, g_ref, b_ref, ws_ref, bs_ref,
             dz_ref, dyb_ref, dl_ref, gws_ref, gbs_ref, glg_ref, glb_ref, vn_s, dvn_s):
        @pl.when(pl.program_id(0) == 0)
        def _():
            gws_ref[...] = jnp.zeros_like(gws_ref)
            gbs_ref[...] = jnp.zeros_like(gbs_ref)
            glg_ref[...] = jnp.zeros_like(glg_ref)
            glb_ref[...] = jnp.zeros_like(glb_ref)

        lane = lax.broadcasted_iota(jnp.int32, (CHUNK, 128), 1)
        va, dgelu_v = _gelu_and_grad(zv_ref[...])
        mu = jnp.mean(va, axis=-1, keepdims=True)
        xc = va - mu
        rs = lax.rsqrt(jnp.mean(xc * xc, axis=-1, keepdims=True) + EPS)
        vhat = xc * rs
        vn_s[...] = (vhat * g_ref[...] + b_ref[...]).astype(MXU_DTYPE)
        gbs_acc = jnp.zeros((CHUNK, 128), F32)
        for g in range(A_GROUPS):
            w = _tril_weights(ws_ref, g)
            bias = bs_ref[:, g:g + 1]
            cols = slice(g * CHUNK, (g + 1) * CHUNK)
            gw_acc = jnp.zeros((CHUNK, CHUNK), F32)
            for c in range(tr // CHUNK):
                rows = slice(c * CHUNK, (c + 1) * CHUNK)
                vn = vn_s[rows, cols]
                mixed = jnp.dot(w, vn, preferred_element_type=F32) + bias
                ua, dgelu_u = _gelu_and_grad(zu_ref[rows, cols])
                dmv = dm_ref[rows, cols]
                sa = _sigmoid(zga_ref[rows, cols])
                dya = dmv * sa
                dz_ref[rows, 2 * D_MODEL + g * CHUNK:2 * D_MODEL + (g + 1) * CHUNK] = (
                    dmv * (ua * mixed) * (sa * (1.0 - sa))).astype(dz_ref.dtype)
                dz_ref[rows, cols] = (dya * mixed * dgelu_u).astype(dz_ref.dtype)
                dmix = dya * ua
                gbs_acc = gbs_acc + jnp.where(lane == g, jnp.sum(dmix, axis=-1, keepdims=True), 0.0)
                dmix_b = dmix.astype(MXU_DTYPE)
                gw_acc = gw_acc + lax.dot_general(dmix_b, vn, _DIMS["nt"], preferred_element_type=F32)
                dvn_s[rows, cols] = lax.dot_general(w, dmix_b, _DIMS["tn"], preferred_element_type=F32)
            gws_ref[g] += jnp.where(_causal_mask(CHUNK), gw_acc, 0.0)
        gbs_ref[...] += gbs_acc

        dvn = dvn_s[...]
        glg_ref[...] += jnp.sum(dvn * vhat, axis=0, keepdims=True)
        glb_ref[...] += jnp.sum(dvn, axis=0, keepdims=True)
        dvh = dvn * g_ref[...]
        dva = rs * (dvh - jnp.mean(dvh, axis=-1, keepdims=True) - vhat * jnp.mean(dvh * vhat, axis=-1, keepdims=True))
        dz_ref[:, D_MODEL:2 * D_MODEL] = (dva * dgelu_v).astype(dz_ref.dtype)

        dmv = dm_ref[...]
        ybv = yb_ref[...]
        sb = _sigmoid(zgb_ref[...])
        dyb = dmv * sb
        dyb_ref[...] = dyb.astype(dyb_ref.dtype)
        dz_ref[:, 3 * D_MODEL:4 * D_MODEL] = (dmv * ybv * (sb * (1.0 - sb))).astype(dz_ref.dtype)
        dz_ref[:, 4 * D_MODEL:] = jnp.zeros((tr, LAT), dz_ref.dtype)
        prod = dyb * ybv
        sel = (lax.broadcasted_iota(jnp.int32, (HEADS, D_MODEL), 1) // NOPE
               == lax.broadcasted_iota(jnp.int32, (HEADS, D_MODEL), 0)).astype(jnp.bfloat16)
        hi = prod.astype(jnp.bfloat16)
        rest = prod - hi.astype(F32)
        mid = rest.astype(jnp.bfloat16)
        lo = (rest - mid.astype(F32)).astype(jnp.bfloat16)
        dl_ref[...] = (lax.dot_general(sel, hi, _DIMS["nt"], preferred_element_type=F32)
                       + lax.dot_general(sel, mid, _DIMS["nt"], preferred_element_type=F32)
                       + lax.dot_general(sel, lo, _DIMS["nt"], preferred_element_type=F32))

    return pl.pallas_call(
        body, name="mix_bwd",
        out_shape=(jax.ShapeDtypeStruct((T, IN_PAD), MXU_DTYPE), jax.ShapeDtypeStruct((T, D_MODEL), MXU_DTYPE),
                   jax.ShapeDtypeStruct((HEADS, T), F32), jax.ShapeDtypeStruct((A_GROUPS, CHUNK, CHUNK), F32),
                   jax.ShapeDtypeStruct((CHUNK, 128), F32), jax.ShapeDtypeStruct((1, D_MODEL), F32),
                   jax.ShapeDtypeStruct((1, D_MODEL), F32)),
        grid=(T // tr,),
        in_specs=[zcol(0), zcol(1), zcol(2), zcol(3), row, row, vec, vec, ws_spec, bs_spec],
        out_specs=(pl.BlockSpec((tr, IN_PAD), lambda i: (i, 0)), row, pl.BlockSpec((HEADS, tr), lambda i: (0, i)),
                   ws_spec, bs_spec, vec, vec),
        scratch_shapes=[pltpu.VMEM((tr, D_MODEL), MXU_DTYPE), pltpu.VMEM((tr, D_MODEL), F32)],
        compiler_params=_params(("arbitrary",), 12 * _nbytes((tr, D_MODEL), F32)),
    )(z, z, z, z, yb, dm, ln_g, ln_b, ws, bs_t)


def _lat_bwd(dz, z, dq, dk, dv, gq, gkv, wq, wkv, cos_a, sin_a, tr=256):
    T = z.shape[0]
    lat_blk = (4 * D_MODEL) // LAT

    def body(dz_in, z_ref, dq_ref, dk_ref, dv_ref, gq_ref, gkv_ref, wq_ref, wkv_ref, cos_ref, sin_ref,
             dz_ref, dqr_ref, dkv_ref, ggq_ref, ggkv_ref):
        del dz_in

        @pl.when(pl.program_id(0) == 0)
        def _():
            ggq_ref[...] = jnp.zeros_like(ggq_ref)
            ggkv_ref[...] = jnp.zeros_like(ggkv_ref)

        cos_v, sin_v = cos_ref[...], sin_ref[...]
        dkr = jnp.zeros((tr, 128), F32)
        for h in range(HEADS):
            o = h * HEAD_PAD
            dqr_ref[:, o:o + NOPE] = dq_ref[:, o:o + NOPE].astype(MXU_DTYPE)
            dqr_ref[:, o + NOPE:o + HEAD_PAD] = _rope_mix_bwd(dq_ref[:, o + NOPE:o + HEAD_PAD], cos_v, sin_v).astype(MXU_DTYPE)
            dkv_ref[:, h * NOPE:(h + 1) * NOPE] = dk_ref[:, o:o + NOPE].astype(MXU_DTYPE)
            dkr = dkr + _rope_mix_bwd(dk_ref[:, o + NOPE:o + HEAD_PAD], cos_v, sin_v)
        dkv_ref[:, HEADS * NOPE:] = dv_ref[...]
        dcqn = lax.dot_general(dqr_ref[...], wq_ref[...], _DIMS["nt"], preferred_element_type=F32)
        dckvn = lax.dot_general(dkv_ref[...], wkv_ref[...], _DIMS["nt"], preferred_element_type=F32)

        zl = z_ref[...]

        def rms_bwd(c, dn, g_ref, gg_ref):
            r = lax.rsqrt(jnp.mean(c * c, axis=-1, keepdims=True) + EPS)
            ch = c * r
            gg_ref[...] += jnp.sum(dn * ch, axis=0, keepdims=True)
            dch = dn * g_ref[...]
            return r * (dch - ch * jnp.mean(dch * ch, axis=-1, keepdims=True))

        dz_ref[:, :Q_RANK] = rms_bwd(zl[:, :Q_RANK], dcqn, gq_ref, ggq_ref).astype(dz_ref.dtype)
        dz_ref[:, Q_RANK:Q_RANK + KV_RANK] = rms_bwd(zl[:, Q_RANK:Q_RANK + KV_RANK], dckvn, gkv_ref, ggkv_ref).astype(dz_ref.dtype)
        dz_ref[:, Q_RANK + KV_RANK:] = dkr.astype(dz_ref.dtype)

    def row(w):
        return pl.BlockSpec((tr, w), lambda i: (i, 0))

    def full(a):
        return pl.BlockSpec(a.shape, lambda i: (0, 0))

    lat = pl.BlockSpec((tr, LAT), lambda i: (i, lat_blk))
    return pl.pallas_call(
        body, name="lat_bwd",
        out_shape=(jax.ShapeDtypeStruct(dz.shape, dz.dtype), jax.ShapeDtypeStruct((T, HEADS * HEAD_PAD), MXU_DTYPE),
                   jax.ShapeDtypeStruct((T, 2 * HEADS * NOPE), MXU_DTYPE), jax.ShapeDtypeStruct(gq.shape, F32),
                   jax.ShapeDtypeStruct(gkv.shape, F32)),
        grid=(T // tr,),
        in_specs=[pl.BlockSpec(memory_space=pl.ANY), lat, row(HEADS * HEAD_PAD), row(HEADS * HEAD_PAD), row(HEADS * NOPE),
                  full(gq), full(gkv), full(wq), full(wkv), row(128), row(128)],
        out_specs=(lat, row(HEADS * HEAD_PAD), row(2 * HEADS * NOPE), full(gq), full(gkv)),
        input_output_aliases={0: 0},
        compiler_params=_params(("arbitrary",), 8 * _nbytes((tr, HEADS * HEAD_PAD), F32)),
    )(dz, z, dq, dk, dv, gq, gkv, wq, wkv, cos_a, sin_a)


GATE_ROWS = 64
HALO = 8


def _taps(ref, half, r, first):
    C = GATE_ROWS
    if first:
        xs = jnp.concatenate([jnp.zeros((HALO, ref.shape[-1]), F32), ref[half, 0:C, :]], axis=0)
    else:
        xs = ref[half, pl.ds(pl.multiple_of(r * C - HALO, HALO), C + HALO), :]
    return xs[HALO:, :], pltpu.roll(xs, 1, 0)[HALO:, :], pltpu.roll(xs, 2, 0)[HALO:, :]


def _conv_taps(taps, cw, cb):
    x0, x1, x2 = taps
    return cb + cw[0:1, :] * x2 + cw[1:2, :] * x1 + cw[2:3, :] * x0


def _fold8(x):
    acc = x[0:8, :]
    for i in range(1, x.shape[0] // 8):
        acc = acc + x[8 * i:8 * (i + 1), :]
    return acc


def _gate_fwd(up3, conv_w, conv_b, B, S):
    T = B * S
    W = FF_TILE
    C = GATE_ROWS

    def body(up_ref, cw_ref, cb_ref, act_ref):
        def chunk(r, first):
            gate = _conv_taps(_taps(up_ref, 0, r, first), cw_ref[0], cb_ref[0])
            val = _conv_taps(_taps(up_ref, 1, r, first), cw_ref[1], cb_ref[1])
            base = 0 if first else pl.multiple_of(r * C, C)
            act_ref[pl.ds(base, C), :] = (gate * _sigmoid(gate) * val).astype(act_ref.dtype)

        chunk(0, True)

        @pl.loop(1, S // C)
        def _(r):
            chunk(r, False)

    return pl.pallas_call(
        body, name="gate_fwd", out_shape=jax.ShapeDtypeStruct((T, D_FF), MXU_DTYPE), grid=(B, N_FF_TILES),
        in_specs=[pl.BlockSpec((2, S, W), lambda b, j: (0, b, j)), pl.BlockSpec((2, 3, W), lambda b, j: (0, 0, j)),
                  pl.BlockSpec((2, 1, W), lambda b, j: (0, 0, j))],
        out_specs=pl.BlockSpec((S, W), lambda b, j: (b, j)),
        compiler_params=_params(("parallel", "parallel"), 6 * _nbytes((S, W), F32)),
    )(up3, conv_w, conv_b)


def _gate_bwd(up3, dact, conv_w, conv_b, B, S):
    T = B * S
    W = FF_TILE
    C = GATE_ROWS

    def body(up_ref, da_ref, cw_ref, cb_ref, dup_ref, gcw_ref, gcb_ref, d_s):
        @pl.when(pl.program_id(1) == 0)
        def _():
            gcw_ref[...] = jnp.zeros_like(gcw_ref)
            gcb_ref[...] = jnp.zeros_like(gcb_ref)

        def chunk(r, first, sums):
            rows = pl.ds(0 if first else pl.multiple_of(r * C, C), C)
            taps = [_taps(up_ref, half, r, first) for half in (0, 1)]
            gate = _conv_taps(taps[0], cw_ref[0], cb_ref[0])
            val = _conv_taps(taps[1], cw_ref[1], cb_ref[1])
            sg = _sigmoid(gate)
            da = da_ref[rows, :]
            d_halves = (da * val * (sg * (1.0 + gate * (1.0 - sg))), da * (gate * sg))
            out = []
            for half, dup in enumerate(d_halves):
                d_s[half, rows, :] = dup
                x0, x1, x2 = taps[half]
                sb, s0, s1, s2 = sums[half]
                out.append((sb + _fold8(dup), s0 + _fold8(dup * x2), s1 + _fold8(dup * x1), s2 + _fold8(dup * x0)))
            return tuple(out)

        zeros = tuple(tuple(jnp.zeros((8, W), F32) for _ in range(4)) for _ in range(2))
        sums = chunk(0, True, zeros)
        sums = lax.fori_loop(1, S // C, lambda r, s: chunk(r, False, s), sums)
        for half in (0, 1):
            sb, s0, s1, s2 = sums[half]
            gcb_ref[half] += jnp.sum(sb, axis=0, keepdims=True)
            gcw_ref[half, 0:1, :] += jnp.sum(s0, axis=0, keepdims=True)
            gcw_ref[half, 1:2, :] += jnp.sum(s1, axis=0, keepdims=True)
            gcw_ref[half, 2:3, :] += jnp.sum(s2, axis=0, keepdims=True)

        d_s[:, S:S + HALO, :] = jnp.zeros((2, HALO, W), F32)

        @pl.loop(0, S // C)
        def _(r):
            base = pl.multiple_of(r * C, C)
            for half in (0, 1):
                ds_ = d_s[half, pl.ds(base, C + HALO), :]
                cw = cw_ref[half]
                dx = (cw[2:3, :] * ds_[:C, :] + cw[1:2, :] * pltpu.roll(ds_, C + HALO - 1, 0)[:C, :]
                      + cw[0:1, :] * pltpu.roll(ds_, C + HALO - 2, 0)[:C, :])
                dup_ref[half, pl.ds(base, C), :] = dx.astype(dup_ref.dtype)

    up_spec = pl.BlockSpec((2, S, W), lambda j, b: (0, b, j))
    cw_spec = pl.BlockSpec((2, 3, W), lambda j, b: (0, 0, j))
    cb_spec = pl.BlockSpec((2, 1, W), lambda j, b: (0, 0, j))
    return pl.pallas_call(
        body, name="gate_bwd",
        out_shape=(jax.ShapeDtypeStruct((2, T, D_FF), MXU_DTYPE), jax.ShapeDtypeStruct((2, 3, D_FF), F32),
                   jax.ShapeDtypeStruct((2, 1, D_FF), F32)),
        grid=(N_FF_TILES, B),
        in_specs=[up_spec, pl.BlockSpec((S, W), lambda j, b: (b, j)), cw_spec, cb_spec],
        out_specs=(up_spec, cw_spec, cb_spec),
        scratch_shapes=[pltpu.VMEM((2, S + HALO, W), F32)],
        compiler_params=_params(("parallel", "arbitrary"), 10 * _nbytes((S, W), F32)),
    )(up3, dact, conv_w, conv_b)


def _final(x2, tgt, g, tr=512):
    T, D = x2.shape

    def body(x_ref, t_ref, g_ref, dx_ref, loss_ref, gg_ref):
        @pl.when(pl.program_id(0) == 0)
        def _():
            loss_ref[...] = jnp.zeros_like(loss_ref)
            gg_ref[...] = jnp.zeros_like(gg_ref)

        xv = x_ref[...]
        gv = g_ref[...]
        r = lax.rsqrt(jnp.mean(xv * xv, axis=-1, keepdims=True) + EPS)
        xn = xv * r
        err = xn * gv - t_ref[...]
        loss_ref[...] += 0.5 * jnp.sum(jnp.mean(err * err, axis=-1, keepdims=True), axis=0, keepdims=True)
        dy = err * (1.0 / D)
        gg_ref[...] += jnp.sum(dy * xn, axis=0, keepdims=True)
        dxn = dy * gv
        dx_ref[...] = r * (dxn - xn * jnp.mean(dxn * xn, axis=-1, keepdims=True))

    row = pl.BlockSpec((tr, D), lambda i: (i, 0))
    vec = pl.BlockSpec((1, D), lambda i: (0, 0))
    return pl.pallas_call(
        body, name="final_loss",
        out_shape=(jax.ShapeDtypeStruct((T, D), F32), jax.ShapeDtypeStruct((1, 128), F32), jax.ShapeDtypeStruct((1, D), F32)),
        grid=(T // tr,), in_specs=[row, row, vec],
        out_specs=(row, pl.BlockSpec((1, 128), lambda i: (0, 0)), vec),
        compiler_params=_params(("arbitrary",), 6 * _nbytes((tr, D), F32)),
    )(x2, tgt, g)


def _sum_slabs(parts, name, tr):
    rows, cols = parts[0].shape
    n = len(parts)

    def body(*refs):
        acc = refs[0][...]
        for r in refs[1:n]:
            acc = acc + r[...]
        refs[n][...] = acc

    blk = pl.BlockSpec((tr, cols), lambda i: (i, 0))
    return pl.pallas_call(
        body, name=name, out_shape=jax.ShapeDtypeStruct((rows, cols), F32), grid=(rows // tr,),
        in_specs=[blk] * n, out_specs=blk,
        compiler_params=_params(("parallel",), (n + 1) * _nbytes((tr, cols), F32)),
    )(*parts)


def _adamw(w, g, m, v, name):
    rows, cols = w.shape
    tr = rows
    for cand in (256, 128, 64, 32, 16, 8):
        if rows % cand == 0:
            tr = cand
            break
    c1 = 1.0 - ADAM_B1 ** ADAM_STEP
    c2 = 1.0 - ADAM_B2 ** ADAM_STEP

    def body(w_ref, g_ref, m_ref, v_ref, d_ref, nm_ref, nv_ref):
        gv = g_ref[...]
        nm = ADAM_B1 * m_ref[...] + (1.0 - ADAM_B1) * gv
        nv = ADAM_B2 * v_ref[...] + (1.0 - ADAM_B2) * (gv * gv)
        nm_ref[...] = nm
        nv_ref[...] = nv
        d_ref[...] = -ADAM_LR * ((nm / c1) / (jnp.sqrt(nv / c2) + ADAM_EPS) + ADAM_WD * w_ref[...])

    blk = pl.BlockSpec((tr, cols), lambda i: (i, 0))
    sds = jax.ShapeDtypeStruct((rows, cols), F32)
    return pl.pallas_call(
        body, name=name, out_shape=(sds, sds, sds), grid=(rows // tr,), in_specs=[blk] * 4, out_specs=(blk, blk, blk),
        compiler_params=_params(("parallel",), 7 * _nbytes((tr, cols), F32)),
    )(w, g, m, v)


_ANY = pl.BlockSpec(memory_space=pl.ANY)


def _place():
    x, y, c = lax.axis_index("x"), lax.axis_index("y"), lax.axis_index("c")
    chips = [(1 - x, y), (x, 1 - y), (1 - x, 1 - y)]
    return x, y, c, chips


def _gather_weights(shards):
    n = len(shards)

    def body(*refs):
        ins, outs = refs[:n], refs[n:2 * n]
        send, recv, fsend, frecv, osend, orecv = refs[2 * n:]
        x, y, c, chips = _place()
        me = 2 * x + y
        first, passed = [], []
        for w in range(n):
            first.append(pltpu.make_async_remote_copy(
                src_ref=ins[w], dst_ref=outs[w].at[me], send_sem=osend.at[w], recv_sem=orecv.at[w],
                device_id=(x, y, 1 - c), device_id_type=MESH))
        for w in range(n):
            for j, (px, py) in enumerate(chips):
                first.append(pltpu.make_async_remote_copy(
                    src_ref=ins[w].at[c], dst_ref=outs[w].at[me, c], send_sem=send.at[3 * w + j],
                    recv_sem=recv.at[3 * w + j], device_id=(px, py, c), device_id_type=MESH))
        for cp in first:
            cp.start()
        for w in range(n):
            for j, (px, py) in enumerate(chips):
                landed = outs[w].at[2 * px + py, c]
                pltpu.make_async_remote_copy(src_ref=landed, dst_ref=landed, send_sem=send.at[3 * w + j],
                                             recv_sem=recv.at[3 * w + j], device_id=(px, py, c),
                                             device_id_type=MESH).wait_recv()
                fw = pltpu.make_async_remote_copy(src_ref=landed, dst_ref=landed, send_sem=fsend.at[3 * w + j],
                                                  recv_sem=frecv.at[3 * w + j], device_id=(x, y, 1 - c),
                                                  device_id_type=MESH)
                fw.start()
                passed.append(fw)
        for w in range(n):
            for j, (px, py) in enumerate(chips):
                other = outs[w].at[2 * px + py, 1 - c]
                pltpu.make_async_remote_copy(src_ref=other, dst_ref=other, send_sem=fsend.at[3 * w + j],
                                             recv_sem=frecv.at[3 * w + j], device_id=(x, y, 1 - c),
                                             device_id_type=MESH).wait_recv()
        for w in range(n):
            own = outs[w].at[me]
            pltpu.make_async_remote_copy(src_ref=own, dst_ref=own, send_sem=osend.at[w], recv_sem=orecv.at[w],
                                         device_id=(x, y, 1 - c), device_id_type=MESH).wait_recv()
        for cp in first + passed:
            cp.wait_send()

    dma = lambda k: pltpu.SemaphoreType.DMA((k,))
    return pl.pallas_call(
        body, name="gather_weights",
        out_shape=tuple(jax.ShapeDtypeStruct((N_CHIPS,) + s.shape, s.dtype) for s in shards),
        in_specs=[_ANY] * n, out_specs=tuple([_ANY] * n),
        scratch_shapes=[dma(3 * n), dma(3 * n), dma(3 * n), dma(3 * n), dma(n), dma(n)],
    )(*shards)


_HBM = pl.BlockSpec(memory_space=pltpu.HBM)
_SEM = pl.BlockSpec(memory_space=pltpu.SEMAPHORE)
_EFFECT = pltpu.SideEffectType.DATAFLOW_SIDE_EFFECTING


def _copies_to_chips(srcs, lands, send, recv, same_core_too):
    x, y, c, chips = _place()
    me = 2 * x + y
    cps = []
    for w, (src, land) in enumerate(zip(srcs, lands)):
        peers = [(px, py, c) for px, py in chips] + ([(x, y, 1 - c)] if same_core_too else [])
        for k, peer in enumerate(peers):
            piece = src if same_core_too else src.at[2 * peer[0] + peer[1]]
            cps.append(pltpu.make_async_remote_copy(src_ref=piece, dst_ref=land.at[me], send_sem=send.at[4 * w + k],
                                                    recv_sem=recv.at[4 * w + k], device_id=peer, device_id_type=MESH))
    return cps


def _exchange_start(srcs, name, same_core_too, after):
    n = len(srcs)
    land_shapes = [((N_CHIPS,) + s.shape) if same_core_too else s.shape for s in srcs]

    def body(*refs):
        src_refs, land_refs = refs[:n], refs[n:2 * n]
        send, recv = refs[2 * n + 1], refs[2 * n + 2]
        token = refs[-1]
        for cp in _copies_to_chips(src_refs, land_refs, send, recv, same_core_too):
            cp.start()
        token[...] = jnp.zeros_like(token)

    sems = pltpu.SemaphoreType.DMA((4 * n,))
    out = pl.pallas_call(
        body, name=name,
        out_shape=(sems, sems, *[pltpu.HBM(s.shape, s.dtype) for s in srcs],
                   *[pltpu.HBM(shp, s.dtype) for shp, s in zip(land_shapes, srcs)], jax.ShapeDtypeStruct((8, 128), F32)),
        in_specs=[_HBM] * (2 * n) + [_ANY],
        out_specs=(_SEM, _SEM, *[_HBM] * (2 * n), pl.BlockSpec(memory_space=pltpu.VMEM)),
        input_output_aliases={i: 2 + i for i in range(2 * n)},
        compiler_params=pltpu.CompilerParams(has_side_effects=_EFFECT),
    )(*[pltpu.with_memory_space_constraint(s, pltpu.HBM) for s in srcs],
      *[pltpu.with_memory_space_constraint(lax.empty(shp, s.dtype), pltpu.HBM) for shp, s in zip(land_shapes, srcs)],
      after)
    return out[0], out[1], out[2:2 + n], out[2 + n:2 + 2 * n], out[-1]


def _exchange_wait(started, name, same_core_too, after):
    send, recv, src_thru, land_thru, _ = started
    n = len(src_thru)

    def body(*refs):
        src_refs, land_refs, send_ref, recv_ref = refs[:n], refs[n:2 * n], refs[2 * n], refs[2 * n + 1]
        for cp in _copies_to_chips(src_refs, land_refs, send_ref, recv_ref, same_core_too):
            cp.wait_send()
            cp.wait_recv()

    out = pl.pallas_call(
        body, name=name,
        out_shape=tuple(pltpu.HBM(a.shape, a.dtype) for a in list(src_thru) + list(land_thru)),
        in_specs=[_HBM] * (2 * n) + [_SEM, _SEM, _ANY], out_specs=tuple([_HBM] * (2 * n)),
        input_output_aliases={i: i for i in range(2 * n)},
        compiler_params=pltpu.CompilerParams(has_side_effects=_EFFECT),
    )(*src_thru, *land_thru, send, recv, after)
    return out[:n], out[n:]


def _swap_halves(gs, name):
    n = len(gs)

    def body(*refs):
        ins, outs, send, recv = refs[:n], refs[n:2 * n], refs[2 * n], refs[2 * n + 1]
        x, y, c, _ = _place()
        cps = []
        for w in range(n):
            cps.append(pltpu.make_async_remote_copy(
                src_ref=ins[w].at[:, 1 - c], dst_ref=outs[w], send_sem=send.at[w], recv_sem=recv.at[w],
                device_id=(x, y, 1 - c), device_id_type=MESH))
        for cp in cps:
            cp.start()
        for cp in cps:
            cp.wait()

    return pl.pallas_call(
        body, name=name,
        out_shape=tuple(jax.ShapeDtypeStruct((g.shape[0],) + g.shape[2:], g.dtype) for g in gs),
        in_specs=[_ANY] * n, out_specs=tuple([_ANY] * n),
        scratch_shapes=[pltpu.SemaphoreType.DMA((n,)), pltpu.SemaphoreType.DMA((n,))],
    )(*gs)


GRAD_PAYLOAD = jnp.bfloat16


def _pair_sum(gs, gots, name):
    n = len(gs)
    core = lax.axis_index("c").astype(jnp.int32).reshape(1)

    def body(core_ref, *refs):
        del core_ref
        for w in range(n):
            refs[2 * n + w][...] = (refs[w][...] + refs[n + w][...]).astype(GRAD_PAYLOAD)

    in_specs, out_specs, out_shape, nbytes = [], [], [], 0
    for g in gs:
        q = g.shape[1] // 4
        in_specs.append(pl.BlockSpec((1, q, g.shape[2]), lambda s, r, core: (s, 2 * core[0] + r, 0)))
        nbytes += 3 * _nbytes((q, g.shape[2]), F32)
    for g in gs:
        q = g.shape[1] // 4
        in_specs.append(pl.BlockSpec((1, q, g.shape[2]), lambda s, r, core: (s, r, 0)))
        out_specs.append(pl.BlockSpec((1, q, g.shape[2]), lambda s, r, core: (s, r, 0)))
        out_shape.append(jax.ShapeDtypeStruct((g.shape[0], g.shape[1] // 2, g.shape[2]), GRAD_PAYLOAD))
    return pl.pallas_call(
        body, name=name, out_shape=tuple(out_shape),
        grid_spec=pltpu.PrefetchScalarGridSpec(num_scalar_prefetch=1, grid=(N_CHIPS, 2), in_specs=in_specs,
                                               out_specs=tuple(out_specs)),
        compiler_params=_params(("parallel", "parallel"), nbytes),
    )(core, *gs, *gots)


def _scatter_to_chips(ps):
    n = len(ps)

    def body(*refs):
        ins, outs, send, recv = refs[:n], refs[n:2 * n], refs[2 * n], refs[2 * n + 1]
        x, y, c, chips = _place()
        me = 2 * x + y
        cps = []
        for w in range(n):
            for j, (px, py) in enumerate(chips):
                cps.append(pltpu.make_async_remote_copy(
                    src_ref=ins[w].at[2 * px + py], dst_ref=outs[w].at[me], send_sem=send.at[3 * w + j],
                    recv_sem=recv.at[3 * w + j], device_id=(px, py, c), device_id_type=MESH))
        for cp in cps:
            cp.start()
        for w in range(n):
            for j, (px, py) in enumerate(chips):
                slot = outs[w].at[2 * px + py]
                pltpu.make_async_remote_copy(src_ref=slot, dst_ref=slot, send_sem=send.at[3 * w + j],
                                             recv_sem=recv.at[3 * w + j], device_id=(px, py, c),
                                             device_id_type=MESH).wait_recv()
        for cp in cps:
            cp.wait_send()

    dma = lambda k: pltpu.SemaphoreType.DMA((k,))
    return pl.pallas_call(
        body, name="grad_scatter_chips", out_shape=tuple(jax.ShapeDtypeStruct(p.shape, p.dtype) for p in ps),
        in_specs=[_ANY] * n, out_specs=tuple([_ANY] * n), scratch_shapes=[dma(3 * n), dma(3 * n)],
    )(*ps)


def _chip_sum(ps, landed):
    n = len(ps)
    x, y, c = lax.axis_index("x"), lax.axis_index("y"), lax.axis_index("c")
    where = jnp.stack([2 * x + y, 2 * (1 - x) + y, 2 * x + (1 - y), 2 * (1 - x) + (1 - y), c]).astype(jnp.int32)

    def body(where_ref, *refs):
        del where_ref
        for w in range(n):
            terms = [refs[4 * w + t][...].astype(F32) for t in range(4)]
            refs[4 * n + w][...] = ((terms[0] + terms[1]) + terms[2]) + terms[3]

    in_specs, out_specs, out_shape, args, nbytes = [], [], [], [], 0
    for p, a in zip(ps, landed):
        q = a.shape[1] // 2
        blk = (1, q, a.shape[2])
        in_specs.append(pl.BlockSpec(blk, lambda r, where: (where[0], r, 0)))
        args.append(p)
        for t in (1, 2, 3):
            in_specs.append(pl.BlockSpec(blk, lambda r, where, t=t: (where[t], r, 0)))
            args.append(a)
        out_specs.append(pl.BlockSpec(blk, lambda r, where: (where[4], r, 0)))
        out_shape.append(jax.ShapeDtypeStruct((2,) + a.shape[1:], F32))
        nbytes += 4 * _nbytes(blk, F32)
    return pl.pallas_call(
        body, name="grad_chip_sum", out_shape=tuple(out_shape),
        grid_spec=pltpu.PrefetchScalarGridSpec(num_scalar_prefetch=1, grid=(2,), in_specs=in_specs,
                                               out_specs=tuple(out_specs)),
        compiler_params=_params(("parallel",), nbytes),
    )(where, *args)


def _join_halves(ss):
    n = len(ss)

    def body(*refs):
        outs, send, recv = refs[n:2 * n], refs[2 * n], refs[2 * n + 1]
        x, y, c, _ = _place()
        cps = []
        for w in range(n):
            cps.append(pltpu.make_async_remote_copy(
                src_ref=outs[w].at[c], dst_ref=outs[w].at[c], send_sem=send.at[w], recv_sem=recv.at[w],
                device_id=(x, y, 1 - c), device_id_type=MESH))
        for cp in cps:
            cp.start()
        for w in range(n):
            got = outs[w].at[1 - c]
            pltpu.make_async_remote_copy(src_ref=got, dst_ref=got, send_sem=send.at[w], recv_sem=recv.at[w],
                                         device_id=(x, y, 1 - c), device_id_type=MESH).wait_recv()
        for cp in cps:
            cp.wait_send()

    dma = lambda k: pltpu.SemaphoreType.DMA((k,))
    return pl.pallas_call(
        body, name="grad_join_halves",
        out_shape=tuple(jax.ShapeDtypeStruct(s.shape, s.dtype) for s in ss),
        in_specs=[_ANY] * n, out_specs=tuple([_ANY] * n), input_output_aliases={w: w for w in range(n)},
        scratch_shapes=[dma(n), dma(n)],
    )(*ss)


def _gather_all(p):
    rows, cols = p.shape

    def body(p_ref, out_ref, send, recv):
        x, y, c, _ = _place()
        me = 4 * x + 2 * y + c
        flips = [(fx, fy, fc) for fx in (0, 1) for fy in (0, 1) for fc in (0, 1)][1:]
        peers = [(x ^ fx, y ^ fy, c ^ fc) for fx, fy, fc in flips]
        cps = [pltpu.make_async_remote_copy(src_ref=p_ref, dst_ref=out_ref.at[me], send_sem=send.at[j], recv_sem=recv.at[j],
                                            device_id=peer, device_id_type=MESH) for j, peer in enumerate(peers)]
        for cp in cps:
            cp.start()
        for j, (px, py, pc) in enumerate(peers):
            slot = out_ref.at[4 * px + 2 * py + pc]
            pltpu.make_async_remote_copy(src_ref=slot, dst_ref=slot, send_sem=send.at[j], recv_sem=recv.at[j],
                                         device_id=(px, py, pc), device_id_type=MESH).wait_recv()
        for cp in cps:
            cp.wait_send()

    dma7 = pltpu.SemaphoreType.DMA((7,))
    return pl.pallas_call(
        body, name="small_grads_gather", out_shape=jax.ShapeDtypeStruct((8, rows, cols), p.dtype),
        in_specs=[_ANY], out_specs=_ANY, scratch_shapes=[dma7, dma7],
    )(p)


def _rot_cols(w):
    a, b = jnp.split(w, 2, axis=-1)
    return jnp.concatenate([-b, a], axis=-1)


def _rot_cols_t(g):
    a, b = jnp.split(g, 2, axis=-1)
    return jnp.concatenate([b, -a], axis=-1)


def _cols_from_chips(a):
    n, r, cs = a.shape
    return jnp.transpose(a, (1, 0, 2)).reshape(r, n * cs)


def _cols_to_chips(a):
    r, cc = a.shape
    return jnp.transpose(a.reshape(r, N_CHIPS, cc // N_CHIPS), (1, 0, 2))


def _conv_w_split(cw):
    return jnp.swapaxes(cw.reshape(3, 2, D_FF), 0, 1)


def _conv_w_join(g):
    return jnp.swapaxes(g, 0, 1).reshape(3, 2 * D_FF)


_SEG =(D_MODEL, 2 * D_MODEL, 2 * D_MODEL + Q_RANK, 2 * D_MODEL + Q_RANK + KV_RANK, 2 * D_MODEL + Q_RANK + KV_RANK + ROPE,
        3 * D_MODEL + Q_RANK + KV_RANK + ROPE)


def _w_in_to_pad(w):
    u, v, cq, ckv, kr, ga, gb = jnp.split(w, _SEG, axis=1)
    return jnp.concatenate([u, v, ga, gb, cq, ckv, kr, _rot_cols(kr)], axis=1)


def _w_in_from_pad(g):
    u, v, ga, gb, cq, ckv, kr, krr = jnp.split(
        g, (D_MODEL, 2 * D_MODEL, 3 * D_MODEL, 4 * D_MODEL, 4 * D_MODEL + Q_RANK, 4 * D_MODEL + Q_RANK + KV_RANK,
            4 * D_MODEL + Q_RANK + KV_RANK + ROPE), axis=1)
    return jnp.concatenate([u, v, cq, ckv, kr + _rot_cols_t(krr), ga, gb], axis=1)


def _w_uq_to_pad(w):
    t = w.reshape(Q_RANK, HEADS, QK_DIM)
    nope, rope = t[..., :NOPE], t[..., NOPE:]
    return jnp.concatenate([nope, rope, _rot_cols(rope)], axis=-1).reshape(Q_RANK, HEADS * HEAD_PAD)


def _w_uq_from_pad(g):
    t = g.reshape(Q_RANK, HEADS, HEAD_PAD)
    nope, rope, rot = t[..., :NOPE], t[..., NOPE:QK_DIM], t[..., QK_DIM:]
    return jnp.concatenate([nope, rope + _rot_cols_t(rot)], axis=-1).reshape(Q_RANK, HEADS * QK_DIM)


def _w_ukv_to_pad(w):
    t = w.reshape(KV_RANK, HEADS, 2, NOPE)
    return jnp.swapaxes(t, 1, 2).reshape(KV_RANK, 2 * HEADS * NOPE)


def _w_ukv_from_pad(g):
    t = g.reshape(KV_RANK, 2, HEADS, NOPE)
    return jnp.swapaxes(t, 1, 2).reshape(KV_RANK, 2 * HEADS * NOPE)


def _rope_tables(positions):
    inv_freq = 1.0 / (ROPE_THETA ** (jnp.arange(0, ROPE, 2, dtype=F32) / ROPE))
    ang = positions.astype(F32).reshape(-1, 1) * inv_freq
    cos, sin = jnp.cos(ang), jnp.sin(ang)
    zero = jnp.zeros((ang.shape[0], 64), F32)
    return jnp.concatenate([cos, cos, zero], axis=1), jnp.concatenate([sin, sin, zero], axis=1)


_BIG = ("w_in", "w_uq", "w_ukv", "w_out", "w_up", "w_down")
UP_SHARD = 2 * D_FF // N_CHIPS


def _local_step(x, positions, tgt, wts, ffn_weights, on_ffn_grads):
    B, S, D = x.shape
    T = B * S
    xf = x.reshape(T, D)
    cos_a, sin_a = _rope_tables(positions)
    bs_t = jnp.pad(wts["a_spatial_b"].T, ((0, 0), (0, 128 - A_GROUPS)))

    h = _rms_fwd(xf, wts["mix_norm"], "norm1_fwd")
    z = _mm(h, wts["w_in"], "nn", "in_proj", tm=512, tn=1536, tk=D)
    q, k, v, cqn, ckvn = _lat_fwd(z, wts["q_a_norm"], wts["kv_a_norm"], wts["w_q"], wts["w_kv"], cos_a, sin_a)
    yb, *lses = _attn_fwd(q, k, v, B, S)
    merged = _mix_fwd(z, yb, wts["a_v_norm_g"], wts["a_v_norm_b"], wts["a_spatial_w"], bs_t)
    x1 = _mm(merged, wts["w_out"], "nn", "out_proj", tm=512, tn=D, tk=D, add=xf)
    h2 = _rms_fwd(x1, wts["ffn_norm"], "norm2_fwd")
    wts = dict(wts)
    wts["w_up"], wts["w_down"], wts["conv_w"] = ffn_weights(h2)
    up_pre = _mm(h2, wts["w_up"], "nn", "up_proj", tm=512, tn=UP_SHARD, tk=D, dims=(T, 2 * D_FF, D),
                 b_spec=pl.BlockSpec((None, D, UP_SHARD), lambda i, j, k: (j, 0, 0)),
                 o_spec=pl.BlockSpec((None, 512, UP_SHARD), lambda i, j, k: (j // 2, i, j % 2)), out_shape=(2, T, D_FF))
    act = _gate_fwd(up_pre, wts["conv_w"], wts["conv_b"], B, S)
    x2 = _mm(act, wts["w_down"], "nn", "down_proj", tm=512, tn=D, tk=1408, add=x1)
    dx2, loss_row, g_final = _final(x2, tgt.reshape(T, D), wts["final_norm"])

    g = {"final_norm": g_final}
    dact = _mm(dx2, wts["w_down"], "nt", "down_proj_dx", tm=512, tn=1408, tk=D)
    g["w_down"] = _mm(act, dx2, "tn", "down_proj_dw", tm=1408, tn=D, tk=512)
    dup, g["conv_w"], g["conv_b"] = _gate_bwd(up_pre, dact, wts["conv_w"], wts["conv_b"], B, S)
    dh2 = _mm(dup, wts["w_up"], "nt", "up_proj_dx", tm=512, tn=D, tk=UP_SHARD, dims=(T, D, 2 * D_FF),
              a_spec=pl.BlockSpec((None, 512, UP_SHARD), lambda i, j, k: (k // 2, i, k % 2)),
              b_spec=pl.BlockSpec((None, D, UP_SHARD), lambda i, j, k: (k, 0, 0)))
    g["w_up"] = _mm(h2, dup, "tn", "up_proj_dw", tm=D, tn=UP_SHARD, tk=512, dims=(D, 2 * D_FF, T),
                    b_spec=pl.BlockSpec((None, 512, UP_SHARD), lambda i, j, k: (j // 2, k, j % 2)),
                    o_spec=pl.BlockSpec((None, D, UP_SHARD), lambda i, j, k: (j, 0, 0)), out_shape=(N_CHIPS, D, UP_SHARD))
    token = on_ffn_grads(g["w_up"], g["w_down"])
    ffn_norm = wts["ffn_norm"] if token is None else wts["ffn_norm"] + token[0:1, 0:1]
    dx1, g["ffn_norm"] = _rms_bwd(x1, ffn_norm, dh2, dx2, "norm2_bwd")
    dm = _mm(dx1, wts["w_out"], "nt", "out_proj_dx", tm=512, tn=D, tk=D)
    g["w_out"] = _mm(merged, dx1, "tn", "out_proj_dw", tm=D, tn=D, tk=512)
    dz, dyb, dl, g["a_spatial_w"], gbs, g["a_v_norm_g"], g["a_v_norm_b"] = _mix_bwd(
        z, yb, dm, wts["a_v_norm_g"], wts["a_v_norm_b"], wts["a_spatial_w"], bs_t)
    g["a_spatial_b"] = gbs[:, :A_GROUPS].T
    delta = dl.reshape(HEADS * T // ATT_BLOCK, 1, ATT_BLOCK)
    dq, dk, dv = _attn_bwd(q, k, v, dyb, lses, delta, B, S)
    dz, dq_raw, dkv, g["q_a_norm"], g["kv_a_norm"] = _lat_bwd(
        dz, z, dq, dk, dv, wts["q_a_norm"], wts["kv_a_norm"], wts["w_q"], wts["w_kv"], cos_a, sin_a)
    g["w_q"] = _mm(cqn, dq_raw, "tn", "q_proj_dw", tm=Q_RANK, tn=HEADS * HEAD_PAD, tk=512)
    g["w_kv"] = _mm(ckvn, dkv, "tn", "kv_proj_dw", tm=KV_RANK, tn=2 * HEADS * NOPE, tk=512)
    dh = _mm(dz, wts["w_in"], "nt", "in_proj_dx", tm=512, tn=D, tk=1536)
    g["w_in"] = _mm(h, dz, "tn", "in_proj_dw", tm=D, tn=1536, tk=512)
    dx, g["mix_norm"] = _rms_bwd(xf, wts["mix_norm"], dh, dx1, "norm1_bwd")
    return loss_row[0, 0], dx.reshape(B, S, D), g


_SMALL = (("mix_norm", (1, D_MODEL)), ("a_v_norm_g", (1, D_MODEL)), ("a_v_norm_b", (1, D_MODEL)),
          ("a_spatial_w", (A_GROUPS * CHUNK, CHUNK)), ("a_spatial_b", (1, A_GROUPS * CHUNK)), ("q_a_norm", (1, Q_RANK)),
          ("kv_a_norm", (1, KV_RANK)), ("ffn_norm", (1, D_MODEL)), ("conv_b", (1, 2 * D_FF)), ("final_norm", (1, D_MODEL)),
          ("conv_w", (3, 2 * D_FF)))
_SMALL_ROWS = -(-sum(math.prod(s) for _, s in _SMALL) // (128 * 8)) * 8


def kernel(x, positions, mix_norm, w_in, a_v_norm_g, a_v_norm_b, a_spatial_w, a_spatial_b, q_a_norm, w_uq, kv_a_norm, w_ukv, w_out, ffn_norm, w_up, conv_w, conv_b, w_down, final_norm, loss_target, m_mix_norm, m_w_in, m_a_v_norm_g, m_a_v_norm_b, m_a_spatial_w, m_a_spatial_b, m_q_a_norm, m_w_uq, m_kv_a_norm, m_w_ukv, m_w_out, m_ffn_norm, m_w_up, m_conv_w, m_conv_b, m_w_down, m_final_norm, v_mix_norm, v_w_in, v_a_v_norm_g, v_a_v_norm_b, v_a_spatial_w, v_a_spatial_b, v_q_a_norm, v_w_uq, v_kv_a_norm, v_w_ukv, v_w_out, v_ffn_norm, v_w_up, v_conv_w, v_conv_b, v_w_down, v_final_norm):
    weights = dict(mix_norm=mix_norm, w_in=w_in, a_v_norm_g=a_v_norm_g, a_v_norm_b=a_v_norm_b, a_spatial_w=a_spatial_w,
                   a_spatial_b=a_spatial_b, q_a_norm=q_a_norm, w_uq=w_uq, kv_a_norm=kv_a_norm, w_ukv=w_ukv, w_out=w_out,
                   ffn_norm=ffn_norm, w_up=w_up, conv_w=conv_w, conv_b=conv_b, w_down=w_down, final_norm=final_norm)
    m_in = dict(mix_norm=m_mix_norm, w_in=m_w_in, a_v_norm_g=m_a_v_norm_g, a_v_norm_b=m_a_v_norm_b,
                a_spatial_w=m_a_spatial_w, a_spatial_b=m_a_spatial_b, q_a_norm=m_q_a_norm, w_uq=m_w_uq,
                kv_a_norm=m_kv_a_norm, w_ukv=m_w_ukv, w_out=m_w_out, ffn_norm=m_ffn_norm, w_up=m_w_up, conv_w=m_conv_w,
                conv_b=m_conv_b, w_down=m_w_down, final_norm=m_final_norm)
    v_in = dict(mix_norm=v_mix_norm, w_in=v_w_in, a_v_norm_g=v_a_v_norm_g, a_v_norm_b=v_a_v_norm_b,
                a_spatial_w=v_a_spatial_w, a_spatial_b=v_a_spatial_b, q_a_norm=v_q_a_norm, w_uq=v_w_uq,
                kv_a_norm=v_kv_a_norm, w_ukv=v_w_ukv, w_out=v_w_out, ffn_norm=v_ffn_norm, w_up=v_w_up, conv_w=v_conv_w,
                conv_b=v_conv_b, w_down=v_w_down, final_norm=v_final_norm)
    names = list(weights)
    chip = 2 * lax.axis_index("x") + lax.axis_index("y")

    def halves(a):
        return a.reshape(a.shape[:-2] + (2, a.shape[-2] // 2, a.shape[-1]))

    def whole(a):
        return a.reshape(a.shape[:-3] + (2 * a.shape[-2], a.shape[-1]))

    gathered = _gather_weights([halves(weights[n][0].astype(MXU_DTYPE)) for n in _BIG[:4]])
    w_in_sh, w_uq_sh, w_ukv_sh, w_out_sh = (whole(a) for a in gathered)
    ffn_gather = _exchange_start([w_up[0].astype(MXU_DTYPE), w_down[0].astype(MXU_DTYPE), conv_w[0]],
                                 "ffn_gather_start", True, after=gathered[3])
    wts = dict(
        mix_norm=mix_norm + ffn_gather[4][0:1, 0:1], a_v_norm_g=a_v_norm_g, a_v_norm_b=a_v_norm_b,
        a_spatial_w=a_spatial_w[0], a_spatial_b=a_spatial_b[0], q_a_norm=q_a_norm, kv_a_norm=kv_a_norm,
        ffn_norm=ffn_norm, final_norm=final_norm.reshape(1, D_MODEL),
        w_in=_w_in_to_pad(_cols_from_chips(w_in_sh)), w_q=_w_uq_to_pad(_cols_from_chips(w_uq_sh)),
        w_kv=_w_ukv_to_pad(_cols_from_chips(w_ukv_sh)), w_out=w_out_sh.reshape(D_MODEL, D_MODEL),
        conv_b=conv_b.reshape(2, 1, D_FF))

    def ffn_weights(after):
        _, (w_up_sh, w_down_sh, cw_all) = _exchange_wait(ffn_gather, "ffn_gather_wait", True, after)
        return w_up_sh, w_down_sh.reshape(D_FF, D_MODEL), _conv_w_split(_cols_from_chips(cw_all))

    ffn_scatter = {}

    def on_ffn_grads(g_w_up, g_w_down):
        slabs = [g_w_up, g_w_down.reshape(N_CHIPS, D_FF // N_CHIPS, D_MODEL)]
        sums = _pair_sum(slabs, _swap_halves([halves(s) for s in slabs], "ffn_grad_swap_halves"), "ffn_grad_pair_sum")
        ffn_scatter["started"] = _exchange_start(list(sums), "ffn_scatter_start", False, after=slabs[1])
        return ffn_scatter["started"][4]

    loss_part, grad_x, g = _local_step(x, positions, loss_target, wts, ffn_weights, on_ffn_grads)
    loss = lax.psum(loss_part, ("x", "y", "c"))

    slabs = [_cols_to_chips(_w_in_from_pad(g["w_in"])), _cols_to_chips(_w_uq_from_pad(g["w_q"])),
             _cols_to_chips(_w_ukv_from_pad(g["w_kv"])), g["w_out"].reshape(N_CHIPS, D_MODEL // N_CHIPS, D_MODEL)]
    pair_sums = _pair_sum(slabs, _swap_halves([halves(s) for s in slabs], "grad_swap_halves"), "grad_pair_sum")
    landed = _scatter_to_chips(pair_sums)
    ffn_sums, ffn_landed = _exchange_wait(ffn_scatter["started"], "ffn_scatter_wait", False, after=landed[0])
    reduced = _chip_sum(list(pair_sums) + list(ffn_sums), list(landed) + list(ffn_landed))
    g_big = dict(zip(_BIG, (whole(a) for a in _join_halves(reduced))))

    g_small_parts = dict(g)
    g_small_parts["conv_w"] = _conv_w_join(g["conv_w"])
    g_small_parts["conv_b"] = g["conv_b"].reshape(1, 2 * D_FF)
    flat = jnp.concatenate([g_small_parts[n].reshape(-1) for n, _ in _SMALL])
    flat = jnp.pad(flat, (0, _SMALL_ROWS * 128 - flat.shape[0])).reshape(_SMALL_ROWS, 128)
    device = 2 * chip + lax.axis_index("c")
    everyone = lax.dynamic_update_slice(_gather_all(flat), flat[None], (device, 0, 0))
    total = _sum_slabs([everyone[j] for j in range(8)], "small_grads_sum", tr=_SMALL_ROWS).reshape(-1)
    g_small, o = {}, 0
    for n, shp in _SMALL:
        g_small[n] = total[o:o + math.prod(shp)].reshape(shp)
        o += math.prod(shp)
    g_small["conv_w"] = lax.dynamic_slice_in_dim(g_small["conv_w"], chip * 1408, 1408, axis=1)

    grads, deltas, new_m, new_v = {}, {}, {}, {}
    for n in names:
        w = weights[n]
        g2 = g_big[n] if n in g_big else g_small[n]
        shape2 = g2.shape
        d, nm, nv = _adamw(w.reshape(shape2), g2, m_in[n].reshape(shape2), v_in[n].reshape(shape2), "adamw_" + n)
        grads[n], deltas[n], new_m[n], new_v[n] = (t.reshape(w.shape) for t in (g2, d, nm, nv))
    return (loss, grad_x, *[grads[n] for n in names], *[deltas[n] for n in names], *[new_m[n] for n in names],
            *[new_v[n] for n in names])
```

```python
import functools
import math

import jax
import jax.numpy as jnp
from jax import lax
from jax.experimental import pallas as pl
from jax.experimental.pallas import tpu as pltpu

F32 = jnp.float32
MXU_DTYPE = jnp.bfloat16
MESH = pl.DeviceIdType.MESH

D_MODEL = 1024
EPS = 1e-6
A_GROUPS = 8
CHUNK = 128
HEADS = 8
NOPE = 128
ROPE = 64
QK_DIM = NOPE + ROPE
HEAD_PAD = 256
Q_RANK = 256
KV_RANK = 128
ROPE_THETA = 10000.0
D_FF = 2816
FF_TILE = 256
N_FF_TILES = D_FF // FF_TILE
LAT = 512
IN_PAD = 4 * D_MODEL + LAT
N_CHIPS = 4
ADAM_LR, ADAM_B1, ADAM_B2, ADAM_EPS, ADAM_WD, ADAM_STEP = 0.001, 0.9, 0.999, 1e-08, 0.01, 10

VMEM_CAP_V7X = 64 * 1024 * 1024
NEG = -1e30


def _params(sem, nbytes):
    limit = int(min(VMEM_CAP_V7X - (8 << 20), max(32 << 20, 3 * nbytes)))
    return pltpu.CompilerParams(dimension_semantics=sem, vmem_limit_bytes=limit)


def _nbytes(shape, dtype):
    return math.prod(shape) * jnp.dtype(dtype).itemsize


_DIMS = {"nn": (((1,), (0,)), ((), ())), "nt": (((1,), (1,)), ((), ())), "tn": (((0,), (0,)), ((), ()))}


def _mm(a, b, mode, name, *, tm, tn, tk, out_dtype=F32, add=None, dims=None, a_spec=None, b_spec=None,
        o_spec=None, out_shape=None):
    if dims is None:
        if mode == "nn":
            (M, K), (_, N) = a.shape, b.shape
        elif mode == "nt":
            (M, K), (N, _) = a.shape, b.shape
        else:
            (K, M), (_, N) = a.shape, b.shape
    else:
        M, N, K = dims
    a_blk = (tk, tm) if mode == "tn" else (tm, tk)
    b_blk = (tn, tk) if mode == "nt" else (tk, tn)
    if a_spec is None:
        a_spec = pl.BlockSpec(a_blk, (lambda i, j, k: (k, i)) if mode == "tn" else (lambda i, j, k: (i, k)))
    if b_spec is None:
        b_spec = pl.BlockSpec(b_blk, (lambda i, j, k: (j, k)) if mode == "nt" else (lambda i, j, k: (k, j)))
    if o_spec is None:
        o_spec = pl.BlockSpec((tm, tn), lambda i, j, k: (i, j))
    if out_shape is None:
        out_shape = (M, N)
    assert M % tm == 0 and N % tn == 0 and K % tk == 0, (name, M, N, K, tm, tn, tk)
    nk = K // tk
    contract = _DIMS[mode]
    has_add = add is not None

    def body(*refs):
        if has_add:
            a_ref, b_ref, add_ref, o_ref, acc = refs
        else:
            a_ref, b_ref, o_ref, acc = refs
        k = pl.program_id(2)

        @pl.when(k == 0)
        def _():
            acc[...] = jnp.zeros_like(acc)

        acc[...] += lax.dot_general(a_ref[...].astype(MXU_DTYPE), b_ref[...].astype(MXU_DTYPE), contract,
                                    preferred_element_type=F32)

        @pl.when(k == nk - 1)
        def _():
            r = acc[...]
            if has_add:
                r = r + add_ref[...]
            o_ref[...] = r.astype(out_dtype)

    in_specs = [a_spec, b_spec]
    args = [a, b]
    nbytes = _nbytes(a_blk, a.dtype) + _nbytes(b_blk, b.dtype) + 3 * _nbytes((tm, tn), F32)
    if has_add:
        in_specs.append(pl.BlockSpec((tm, tn), lambda i, j, k: (i, j)))
        args.append(add)
        nbytes += _nbytes((tm, tn), F32)
    return pl.pallas_call(
        body, name=name, out_shape=jax.ShapeDtypeStruct(out_shape, out_dtype),
        grid=(M // tm, N // tn, nk), in_specs=in_specs, out_specs=o_spec,
        scratch_shapes=[pltpu.VMEM((tm, tn), F32)],
        compiler_params=_params(("parallel", "parallel", "arbitrary"), nbytes),
    )(*args)


_GELU_C = math.sqrt(2.0 / math.pi)
_GELU_A = 0.044715


def _sigmoid(x):
    return 1.0 / (1.0 + jnp.exp(-x))


def _gelu(x):
    t = jnp.tanh(_GELU_C * (x + _GELU_A * (x * x * x)))
    return x * (0.5 * (1.0 + t))


def _gelu_and_grad(x):
    x2 = x * x
    t = jnp.tanh(_GELU_C * (x + _GELU_A * (x2 * x)))
    cdf = 0.5 * (1.0 + t)
    grad = cdf + 0.5 * x * (1.0 - t * t) * (_GELU_C * (1.0 + 3.0 * _GELU_A * x2))
    return x * cdf, grad


def _rope_mix(g, cos_a, sin_a):
    return g * cos_a + pltpu.roll(g, 64, 1) * sin_a


def _rope_mix_bwd(d, cos_a, sin_a):
    return d * cos_a + pltpu.roll(d * sin_a, 64, 1)


def _rms_fwd(x, g, name, tr=512):
    T, D = x.shape

    def body(x_ref, g_ref, h_ref):
        xv = x_ref[...]
        r = lax.rsqrt(jnp.mean(xv * xv, axis=-1, keepdims=True) + EPS)
        h_ref[...] = ((xv * r) * g_ref[...]).astype(h_ref.dtype)

    return pl.pallas_call(
        body, name=name, out_shape=jax.ShapeDtypeStruct((T, D), MXU_DTYPE), grid=(T // tr,),
        in_specs=[pl.BlockSpec((tr, D), lambda i: (i, 0)), pl.BlockSpec((1, D), lambda i: (0, 0))],
        out_specs=pl.BlockSpec((tr, D), lambda i: (i, 0)),
        compiler_params=_params(("parallel",), 3 * _nbytes((tr, D), F32)),
    )(x, g)


def _rms_bwd(x, g, dh, dres, name, tr=512):
    T, D = x.shape

    def body(x_ref, g_ref, dh_ref, dres_ref, dx_ref, gg_ref):
        @pl.when(pl.program_id(0) == 0)
        def _():
            gg_ref[...] = jnp.zeros_like(gg_ref)

        xv = x_ref[...]
        r = lax.rsqrt(jnp.mean(xv * xv, axis=-1, keepdims=True) + EPS)
        xn = xv * r
        dhv = dh_ref[...]
        dxn = dhv * g_ref[...]
        dx_ref[...] = dres_ref[...] + r * (dxn - xn * jnp.mean(dxn * xn, axis=-1, keepdims=True))
        gg_ref[...] += jnp.sum(dhv * xn, axis=0, keepdims=True)

    row = pl.BlockSpec((tr, D), lambda i: (i, 0))
    vec = pl.BlockSpec((1, D), lambda i: (0, 0))
    return pl.pallas_call(
        body, name=name,
        out_shape=(jax.ShapeDtypeStruct((T, D), F32), jax.ShapeDtypeStruct((1, D), F32)),
        grid=(T // tr,), in_specs=[row, vec, row, row], out_specs=(row, vec),
        compiler_params=_params(("arbitrary",), 6 * _nbytes((tr, D), F32)),
    )(x, g, dh, dres)


def _lat_fwd(z, gq, gkv, wq, wkv, cos_a, sin_a, tr=256):
    T = z.shape[0]
    lat_blk = (4 * D_MODEL) // LAT

    def body(z_ref, gq_ref, gkv_ref, wq_ref, wkv_ref, cos_ref, sin_ref, q_ref, k_ref, v_ref, cqn_ref, ckvn_ref):
        zl = z_ref[...]
        cos_v, sin_v = cos_ref[...], sin_ref[...]
        cq = zl[:, :Q_RANK]
        ckv = zl[:, Q_RANK:Q_RANK + KV_RANK]
        krb = zl[:, Q_RANK + KV_RANK:]
        cqn = ((cq * lax.rsqrt(jnp.mean(cq * cq, axis=-1, keepdims=True) + EPS)) * gq_ref[...]).astype(MXU_DTYPE)
        ckvn = ((ckv * lax.rsqrt(jnp.mean(ckv * ckv, axis=-1, keepdims=True) + EPS)) * gkv_ref[...]).astype(MXU_DTYPE)
        cqn_ref[...] = cqn
        ckvn_ref[...] = ckvn
        krr = _rope_mix(krb, cos_v, sin_v).astype(MXU_DTYPE)
        q = jnp.dot(cqn, wq_ref[...], preferred_element_type=F32)
        kv = jnp.dot(ckvn, wkv_ref[...], preferred_element_type=F32)
        for h in range(HEADS):
            o = h * HEAD_PAD
            q_ref[:, o:o + NOPE] = q[:, o:o + NOPE].astype(MXU_DTYPE)
            q_ref[:, o + NOPE:o + HEAD_PAD] = _rope_mix(q[:, o + NOPE:o + HEAD_PAD], cos_v, sin_v).astype(MXU_DTYPE)
            k_ref[:, o:o + NOPE] = kv[:, h * NOPE:(h + 1) * NOPE].astype(MXU_DTYPE)
            k_ref[:, o + NOPE:o + HEAD_PAD] = krr
        v_ref[...] = kv[:, HEADS * NOPE:].astype(MXU_DTYPE)

    def row(w):
        return pl.BlockSpec((tr, w), lambda i: (i, 0))

    def full(a):
        return pl.BlockSpec(a.shape, lambda i: (0, 0))

    return pl.pallas_call(
        body, name="lat_fwd",
        out_shape=(jax.ShapeDtypeStruct((T, HEADS * HEAD_PAD), MXU_DTYPE), jax.ShapeDtypeStruct((T, HEADS * HEAD_PAD), MXU_DTYPE),
                   jax.ShapeDtypeStruct((T, HEADS * NOPE), MXU_DTYPE), jax.ShapeDtypeStruct((T, Q_RANK), MXU_DTYPE),
                   jax.ShapeDtypeStruct((T, KV_RANK), MXU_DTYPE)),
        grid=(T // tr,),
        in_specs=[pl.BlockSpec((tr, LAT), lambda i: (i, lat_blk)), full(gq), full(gkv), full(wq), full(wkv), row(128), row(128)],
        out_specs=(row(HEADS * HEAD_PAD), row(HEADS * HEAD_PAD), row(HEADS * NOPE), row(Q_RANK), row(KV_RANK)),
        compiler_params=_params(("parallel",), 8 * _nbytes((tr, HEADS * HEAD_PAD), F32)),
    )(z, gq, gkv, wq, wkv, cos_a, sin_a)


ATT_BLOCK = 256
_SCALE = QK_DIM ** -0.5


def _causal_mask(n):
    return lax.broadcasted_iota(jnp.int32, (n, n), 1) <= lax.broadcasted_iota(jnp.int32, (n, n), 0)


def _causal_mask_t(n):
    return lax.broadcasted_iota(jnp.int32, (n, n), 0) <= lax.broadcasted_iota(jnp.int32, (n, n), 1)


ATT_HEADS = 2


def _attn_fwd(q, k, v, B, S):
    tq = ATT_BLOCK
    nq = S // tq
    T = B * S
    hp, groups = ATT_HEADS, HEADS // ATT_HEADS

    def body(q_ref, k_ref, v_ref, o_ref, *lse_refs):
        qi = pl.program_id(2)
        qs = [q_ref[:, t * HEAD_PAD:(t + 1) * HEAD_PAD] for t in range(hp)]

        def scores(j, t):
            rows = pl.ds(pl.multiple_of(j * tq, tq), tq)
            return lax.dot_general(k_ref[rows, t * HEAD_PAD:(t + 1) * HEAD_PAD], qs[t], _DIMS["nt"],
                                   preferred_element_type=F32)

        def step(j, carry, last):
            rows = pl.ds(pl.multiple_of(j * tq, tq), tq)
            out = []
            for t in range(hp):
                m, l, acc, st = carry[t]
                st_next = st if last else scores(j + 1, t)
                st = st * _SCALE
                if last:
                    st = jnp.where(_causal_mask_t(tq), st, NEG)
                m_new = jnp.maximum(m, jnp.max(st, axis=0, keepdims=True))
                alpha = jnp.exp(m - m_new)
                p = jnp.exp(st - m_new)
                l = alpha * l + jnp.sum(p, axis=0, keepdims=True)
                acc = alpha * acc + lax.dot_general(v_ref[rows, t * NOPE:(t + 1) * NOPE], p.astype(MXU_DTYPE),
                                                    _DIMS["tn"], preferred_element_type=F32)
                out.append((m_new, l, acc, st_next))
            return tuple(out)

        init = tuple((jnp.full((1, tq), NEG, F32), jnp.zeros((1, tq), F32), jnp.zeros((NOPE, tq), F32), scores(0, t))
                     for t in range(hp))
        carry = lax.fori_loop(0, qi, lambda j, c: step(j, c, False), init)
        carry = step(qi, carry, True)
        for t in range(hp):
            m, l, acc, _ = carry[t]
            o_ref[:, t * NOPE:(t + 1) * NOPE] = (acc / l).T
            lse_refs[t][0] = m + jnp.log(l)

    lse_sds = jax.ShapeDtypeStruct((groups * B * nq, 1, tq), F32)
    lse_spec = pl.BlockSpec((1, 1, tq), lambda b, h, i: ((h * B + b) * nq + i, 0, 0))
    return pl.pallas_call(
        body, name="attn_fwd",
        out_shape=(jax.ShapeDtypeStruct((T, HEADS * NOPE), F32),) + (lse_sds,) * hp,
        grid=(B, groups, nq),
        in_specs=[pl.BlockSpec((tq, hp * HEAD_PAD), lambda b, h, i: (b * nq + i, h)),
                  pl.BlockSpec((S, hp * HEAD_PAD), lambda b, h, i: (b, h)),
                  pl.BlockSpec((S, hp * NOPE), lambda b, h, i: (b, h))],
        out_specs=(pl.BlockSpec((tq, hp * NOPE), lambda b, h, i: (b * nq + i, h)),) + (lse_spec,) * hp,
        compiler_params=_params(("parallel", "parallel", "arbitrary"), 4 * hp * _nbytes((S, HEAD_PAD), MXU_DTYPE)),
    )(q, k, v)


def _attn_bwd(q, k, v, do, lses, delta, B, S):
    tq = ATT_BLOCK
    nq = S // tq
    T = B * S
    hp, groups = ATT_HEADS, HEADS // ATT_HEADS

    def body(q_ref, k_ref, v_ref, do_ref, *refs):
        lse_refs, dl_refs = refs[:hp], refs[hp:2 * hp]
        dq_ref, dk_ref, dv_ref = refs[2 * hp:]
        kj = pl.program_id(2)

        @pl.when(kj == 0)
        def _():
            dq_ref[...] = jnp.zeros_like(dq_ref)

        def products(i, t):
            rows = pl.ds(pl.multiple_of(i * tq, tq), tq)
            st = lax.dot_general(k_ref[:, t * HEAD_PAD:(t + 1) * HEAD_PAD], q_ref[rows, t * HEAD_PAD:(t + 1) * HEAD_PAD],
                                 _DIMS["nt"], preferred_element_type=F32)
            dpt = lax.dot_general(v_ref[:, t * NOPE:(t + 1) * NOPE], do_ref[rows, t * NOPE:(t + 1) * NOPE],
                                  _DIMS["nt"], preferred_element_type=F32)
            return st, dpt

        def step(i, carry, masked):
            rows = pl.ds(pl.multiple_of(i * tq, tq), tq)
            nxt = jnp.minimum(i + 1, nq - 1)
            out = []
            for t in range(hp):
                dk, dv, st, dpt = carry[t]
                st_next, dpt_next = products(nxt, t)
                qk_cols = slice(t * HEAD_PAD, (t + 1) * HEAD_PAD)
                v_cols = slice(t * NOPE, (t + 1) * NOPE)
                p = jnp.exp(st * _SCALE - lse_refs[t][i])
                if masked:
                    p = jnp.where(_causal_mask_t(tq), p, 0.0)
                dv = dv + jnp.dot(p.astype(MXU_DTYPE), do_ref[rows, v_cols], preferred_element_type=F32)
                ds = (p * (dpt - dl_refs[t][i]) * _SCALE).astype(MXU_DTYPE)
                dk = dk + jnp.dot(ds, q_ref[rows, qk_cols], preferred_element_type=F32)
                dq_ref[rows, qk_cols] += lax.dot_general(ds, k_ref[:, qk_cols], _DIMS["tn"], preferred_element_type=F32)
                out.append((dk, dv, st_next, dpt_next))
            return tuple(out)

        init = tuple((jnp.zeros((tq, HEAD_PAD), F32), jnp.zeros((tq, NOPE), F32)) + products(kj, t) for t in range(hp))
        carry = step(kj, init, True)
        carry = lax.fori_loop(kj + 1, nq, lambda i, c: step(i, c, False), carry)
        for t in range(hp):
            dk_ref[:, t * HEAD_PAD:(t + 1) * HEAD_PAD] = carry[t][0]
            dv_ref[:, t * NOPE:(t + 1) * NOPE] = carry[t][1].astype(dv_ref.dtype)

    seq = lambda w: pl.BlockSpec((S, w), lambda b, h, j: (b, h))
    blk = lambda w: pl.BlockSpec((tq, w), lambda b, h, j: (b * nq + j, h))
    lse_spec = pl.BlockSpec((nq, 1, tq), lambda b, h, j: (h * B + b, 0, 0))
    dl_specs = [pl.BlockSpec((nq, 1, tq), lambda b, h, j, t=t: ((h * hp + t) * B + b, 0, 0)) for t in range(hp)]
    return pl.pallas_call(
        body, name="attn_bwd",
        out_shape=(jax.ShapeDtypeStruct((T, HEADS * HEAD_PAD), F32), jax.ShapeDtypeStruct((T, HEADS * HEAD_PAD), F32),
                   jax.ShapeDtypeStruct((T, HEADS * NOPE), MXU_DTYPE)),
        grid=(B, groups, nq),
        in_specs=[seq(hp * HEAD_PAD), blk(hp * HEAD_PAD), blk(hp * NOPE), seq(hp * NOPE)] + [lse_spec] * hp + dl_specs,
        out_specs=(seq(hp * HEAD_PAD), blk(hp * HEAD_PAD), blk(hp * NOPE)),
        compiler_params=_params(("parallel", "parallel", "arbitrary"), 8 * hp * _nbytes((S, HEAD_PAD), F32)),
    )(q, k, v, do, *lses, *([delta] * hp))


MIX_ROWS = 256


def _tril_weights(ws_ref, g):
    return jnp.where(_causal_mask(CHUNK), ws_ref[g], 0.0).astype(MXU_DTYPE)


def _layer_norm_stats(va):
    mu = jnp.mean(va, axis=-1, keepdims=True)
    xc = va - mu
    rs = lax.rsqrt(jnp.mean(xc * xc, axis=-1, keepdims=True) + EPS)
    return xc * rs


def _mix_specs(tr):
    zcol = lambda c: pl.BlockSpec((tr, D_MODEL), lambda i, c=c: (i, c))
    row = pl.BlockSpec((tr, D_MODEL), lambda i: (i, 0))
    vec = pl.BlockSpec((1, D_MODEL), lambda i: (0, 0))
    ws = pl.BlockSpec((A_GROUPS, CHUNK, CHUNK), lambda i: (0, 0, 0))
    bs = pl.BlockSpec((CHUNK, 128), lambda i: (0, 0))
    return zcol, row, vec, ws, bs


def _mix_fwd(z, yb, ln_g, ln_b, ws, bs_t):
    T = z.shape[0]
    tr = MIX_ROWS
    zcol, row, vec, ws_spec, bs_spec = _mix_specs(tr)

    def body(zu_ref, zv_ref, zga_ref, zgb_ref, yb_ref, g_ref, b_ref, ws_ref, bs_ref, out_ref, vn_s):
        vhat = _layer_norm_stats(_gelu(zv_ref[...]))
        vn_s[...] = (vhat * g_ref[...] + b_ref[...]).astype(MXU_DTYPE)
        for g in range(A_GROUPS):
            w = _tril_weights(ws_ref, g)
            bias = bs_ref[:, g:g + 1]
            cols = slice(g * CHUNK, (g + 1) * CHUNK)
            for c in range(tr // CHUNK):
                rows = slice(c * CHUNK, (c + 1) * CHUNK)
                mixed = jnp.dot(w, vn_s[rows, cols], preferred_element_type=F32) + bias
                ya = _gelu(zu_ref[rows, cols]) * mixed
                merged = _sigmoid(zga_ref[rows, cols]) * ya + _sigmoid(zgb_ref[rows, cols]) * yb_ref[rows, cols]
                out_ref[rows, cols] = merged.astype(MXU_DTYPE)

    return pl.pallas_call(
        body, name="mix_fwd", out_shape=jax.ShapeDtypeStruct((T, D_MODEL), MXU_DTYPE), grid=(T // tr,),
        in_specs=[zcol(0), zcol(1), zcol(2), zcol(3), row, vec, vec, ws_spec, bs_spec], out_specs=row,
        scratch_shapes=[pltpu.VMEM((tr, D_MODEL), MXU_DTYPE)],
        compiler_params=_params(("parallel",), 8 * _nbytes((tr, D_MODEL), F32)),
    )(z, z, z, z, yb, ln_g, ln_b, ws, bs_t)


def _mix_bwd(z, yb, dm, ln_g, ln_b, ws, bs_t):
    T = z.shape[0]
    tr = MIX_ROWS
    zcol, row, vec, ws_spec, bs_spec = _mix_specs(tr)

    def body(zu_ref, zv_ref, zga_ref, zgb_ref, yb_ref, dm_ref, g_ref, b_ref, ws_ref, bs_ref,
             dz_ref, dyb_ref, dl_ref, gws_ref, gbs_ref, glg_ref, glb_ref, vn_s, dvn_s):
        @pl.when(pl.program_id(0) == 0)
        def _():
            gws_ref[...] = jnp.zeros_like(gws_ref)
            gbs_ref[...] = jnp.zeros_like(gbs_ref)
            glg_ref[...] = jnp.zeros_like(glg_ref)
            glb_ref[...] = jnp.zeros_like(glb_ref)

        lane = lax.broadcasted_iota(jnp.int32, (CHUNK, 128), 1)
        va, dgelu_v = _gelu_and_grad(zv_ref[...])
        mu = jnp.mean(va, axis=-1, keepdims=True)
        xc = va - mu
        rs = lax.rsqrt(jnp.mean(xc * xc, axis=-1, keepdims=True) + EPS)
        vhat = xc * rs
        vn_s[...] = (vhat * g_ref[...] + b_ref[...]).astype(MXU_DTYPE)
        gbs_acc = jnp.zeros((CHUNK, 128), F32)
        for g in range(A_GROUPS):
            w = _tril_weights(ws_ref, g)
            bias = bs_ref[:, g:g + 1]
            cols = slice(g * CHUNK, (g + 1) * CHUNK)
            gw_acc = jnp.zeros((CHUNK, CHUNK), F32)
            for c in range(tr // CHUNK):
                rows = slice(c * CHUNK, (c + 1) * CHUNK)
                vn = vn_s[rows, cols]
                mixed = jnp.dot(w, vn, preferred_element_type=F32) + bias
                ua, dgelu_u = _gelu_and_grad(zu_ref[rows, cols])
                dmv = dm_ref[rows, cols]
                sa = _sigmoid(zga_ref[rows, cols])
                dya = dmv * sa
                dz_ref[rows, 2 * D_MODEL + g * CHUNK:2 * D_MODEL + (g + 1) * CHUNK] = (
                    dmv * (ua * mixed) * (sa * (1.0 - sa))).astype(dz_ref.dtype)
                dz_ref[rows, cols] = (dya * mixed * dgelu_u).astype(dz_ref.dtype)
                dmix = dya * ua
                gbs_acc = gbs_acc + jnp.where(lane == g, jnp.sum(dmix, axis=-1, keepdims=True), 0.0)
                dmix_b = dmix.astype(MXU_DTYPE)
                gw_acc = gw_acc + lax.dot_general(dmix_b, vn, _DIMS["nt"], preferred_element_type=F32)
                dvn_s[rows, cols] = lax.dot_general(w, dmix_b, _DIMS["tn"], preferred_element_type=F32)
            gws_ref[g] += jnp.where(_causal_mask(CHUNK), gw_acc, 0.0)
        gbs_ref[...] += gbs_acc

        dvn = dvn_s[...]
        glg_ref[...] += jnp.sum(dvn * vhat, axis=0, keepdims=True)
        glb_ref[...] += jnp.sum(dvn, axis=0, keepdims=True)
        dvh = dvn * g_ref[...]
        dva = rs * (dvh - jnp.mean(dvh, axis=-1, keepdims=True) - vhat * jnp.mean(dvh * vhat, axis=-1, keepdims=True))
        dz_ref[:, D_MODEL:2 * D_MODEL] = (dva * dgelu_v).astype(dz_ref.dtype)

        dmv = dm_ref[...]
        ybv = yb_ref[...]
        sb = _sigmoid(zgb_ref[...])
        dyb = dmv * sb
        dyb_ref[...] = dyb.astype(dyb_ref.dtype)
        dz_ref[:, 3 * D_MODEL:4 * D_MODEL] = (dmv * ybv * (sb * (1.0 - sb))).astype(dz_ref.dtype)
        dz_ref[:, 4 * D_MODEL:] = jnp.zeros((tr, LAT), dz_ref.dtype)
        prod = dyb * ybv
        sel = (lax.broadcasted_iota(jnp.int32, (HEADS, D_MODEL), 1) // NOPE
               == lax.broadcasted_iota(jnp.int32, (HEADS, D_MODEL), 0)).astype(jnp.bfloat16)
        hi = prod.astype(jnp.bfloat16)
        rest = prod - hi.astype(F32)
        mid = rest.astype(jnp.bfloat16)
        lo = (rest - mid.astype(F32)).astype(jnp.bfloat16)
        dl_ref[...] = (lax.dot_general(sel, hi, _DIMS["nt"], preferred_element_type=F32)
                       + lax.dot_general(sel, mid, _DIMS["nt"], preferred_element_type=F32)
                       + lax.dot_general(sel, lo, _DIMS["nt"], preferred_element_type=F32))

    return pl.pallas_call(
        body, name="mix_bwd",
        out_shape=(jax.ShapeDtypeStruct((T, IN_PAD), MXU_DTYPE), jax.ShapeDtypeStruct((T, D_MODEL), MXU_DTYPE),
                   jax.ShapeDtypeStruct((HEADS, T), F32), jax.ShapeDtypeStruct((A_GROUPS, CHUNK, CHUNK), F32),
                   jax.ShapeDtypeStruct((CHUNK, 128), F32), jax.ShapeDtypeStruct((1, D_MODEL), F32),
                   jax.ShapeDtypeStruct((1, D_MODEL), F32)),
        grid=(T // tr,),
        in_specs=[zcol(0), zcol(1), zcol(2), zcol(3), row, row, vec, vec, ws_spec, bs_spec],
        out_specs=(pl.BlockSpec((tr, IN_PAD), lambda i: (i, 0)), row, pl.BlockSpec((HEADS, tr), lambda i: (0, i)),
                   ws_spec, bs_spec, vec, vec),
        scratch_shapes=[pltpu.VMEM((tr, D_MODEL), MXU_DTYPE), pltpu.VMEM((tr, D_MODEL), F32)],
        compiler_params=_params(("arbitrary",), 12 * _nbytes((tr, D_MODEL), F32)),
    )(z, z, z, z, yb, dm, ln_g, ln_b, ws, bs_t)


def _lat_bwd(dz, z, dq, dk, dv, gq, gkv, wq, wkv, cos_a, sin_a, tr=256):
    T = z.shape[0]
    lat_blk = (4 * D_MODEL) // LAT

    def body(dz_in, z_ref, dq_ref, dk_ref, dv_ref, gq_ref, gkv_ref, wq_ref, wkv_ref, cos_ref, sin_ref,
             dz_ref, dqr_ref, dkv_ref, ggq_ref, ggkv_ref):
        del dz_in

        @pl.when(pl.program_id(0) == 0)
        def _():
            ggq_ref[...] = jnp.zeros_like(ggq_ref)
            ggkv_ref[...] = jnp.zeros_like(ggkv_ref)

        cos_v, sin_v = cos_ref[...], sin_ref[...]
        dkr = jnp.zeros((tr, 128), F32)
        for h in range(HEADS):
            o = h * HEAD_PAD
            dqr_ref[:, o:o + NOPE] = dq_ref[:, o:o + NOPE].astype(MXU_DTYPE)
            dqr_ref[:, o + NOPE:o + HEAD_PAD] = _rope_mix_bwd(dq_ref[:, o + NOPE:o + HEAD_PAD], cos_v, sin_v).astype(MXU_DTYPE)
            dkv_ref[:, h * NOPE:(h + 1) * NOPE] = dk_ref[:, o:o + NOPE].astype(MXU_DTYPE)
            dkr = dkr + _rope_mix_bwd(dk_ref[:, o + NOPE:o + HEAD_PAD], cos_v, sin_v)
        dkv_ref[:, HEADS * NOPE:] = dv_ref[...]
        dcqn = lax.dot_general(dqr_ref[...], wq_ref[...], _DIMS["nt"], preferred_element_type=F32)
        dckvn = lax.dot_general(dkv_ref[...], wkv_ref[...], _DIMS["nt"], preferred_element_type=F32)

        zl = z_ref[...]

        def rms_bwd(c, dn, g_ref, gg_ref):
            r = lax.rsqrt(jnp.mean(c * c, axis=-1, keepdims=True) + EPS)
            ch = c * r
            gg_ref[...] += jnp.sum(dn * ch, axis=0, keepdims=True)
            dch = dn * g_ref[...]
            return r * (dch - ch * jnp.mean(dch * ch, axis=-1, keepdims=True))

        dz_ref[:, :Q_RANK] = rms_bwd(zl[:, :Q_RANK], dcqn, gq_ref, ggq_ref).astype(dz_ref.dtype)
        dz_ref[:, Q_RANK:Q_RANK + KV_RANK] = rms_bwd(zl[:, Q_RANK:Q_RANK + KV_RANK], dckvn, gkv_ref, ggkv_ref).astype(dz_ref.dtype)
        dz_ref[:, Q_RANK + KV_RANK:] = dkr.astype(dz_ref.dtype)

    def row(w):
        return pl.BlockSpec((tr, w), lambda i: (i, 0))

    def full(a):
        return pl.BlockSpec(a.shape, lambda i: (0, 0))

    lat = pl.BlockSpec((tr, LAT), lambda i: (i, lat_blk))
    return pl.pallas_call(
        body, name="lat_bwd",
        out_shape=(jax.ShapeDtypeStruct(dz.shape, dz.dtype), jax.ShapeDtypeStruct((T, HEADS * HEAD_PAD), MXU_DTYPE),
                   jax.ShapeDtypeStruct((T, 2 * HEADS * NOPE), MXU_DTYPE), jax.ShapeDtypeStruct(gq.shape, F32),
                   jax.ShapeDtypeStruct(gkv.shape, F32)),
        grid=(T // tr,),
        in_specs=[pl.BlockSpec(memory_space=pl.ANY), lat, row(HEADS * HEAD_PAD), row(HEADS * HEAD_PAD), row(HEADS * NOPE),
                  full(gq), full(gkv), full(wq), full(wkv), row(128), row(128)],
        out_specs=(lat, row(HEADS * HEAD_PAD), row(2 * HEADS * NOPE), full(gq), full(gkv)),
        input_output_aliases={0: 0},
        compiler_params=_params(("arbitrary",), 8 * _nbytes((tr, HEADS * HEAD_PAD), F32)),
    )(dz, z, dq, dk, dv, gq, gkv, wq, wkv, cos_a, sin_a)


GATE_ROWS = 64
HALO = 8


def _taps(ref, half, r, first):
    C = GATE_ROWS
    if first:
        xs = jnp.concatenate([jnp.zeros((HALO, ref.shape[-1]), F32), ref[half, 0:C, :]], axis=0)
    else:
        xs = ref[half, pl.ds(pl.multiple_of(r * C - HALO, HALO), C + HALO), :]
    return xs[HALO:, :], pltpu.roll(xs, 1, 0)[HALO:, :], pltpu.roll(xs, 2, 0)[HALO:, :]


def _conv_taps(taps, cw, cb):
    x0, x1, x2 = taps
    return cb + cw[0:1, :] * x2 + cw[1:2, :] * x1 + cw[2:3, :] * x0


def _fold8(x):
    acc = x[0:8, :]
    for i in range(1, x.shape[0] // 8):
        acc = acc + x[8 * i:8 * (i + 1), :]
    return acc


def _gate_fwd(up3, conv_w, conv_b, B, S):
    T = B * S
    W = FF_TILE
    C = GATE_ROWS

    def body(up_ref, cw_ref, cb_ref, act_ref):
        def chunk(r, first):
            gate = _conv_taps(_taps(up_ref, 0, r, first), cw_ref[0], cb_ref[0])
            val = _conv_taps(_taps(up_ref, 1, r, first), cw_ref[1], cb_ref[1])
            base = 0 if first else pl.multiple_of(r * C, C)
            act_ref[pl.ds(base, C), :] = (gate * _sigmoid(gate) * val).astype(act_ref.dtype)

        chunk(0, True)

        @pl.loop(1, S // C)
        def _(r):
            chunk(r, False)

    return pl.pallas_call(
        body, name="gate_fwd", out_shape=jax.ShapeDtypeStruct((T, D_FF), MXU_DTYPE), grid=(B, N_FF_TILES),
        in_specs=[pl.BlockSpec((2, S, W), lambda b, j: (0, b, j)), pl.BlockSpec((2, 3, W), lambda b, j: (0, 0, j)),
                  pl.BlockSpec((2, 1, W), lambda b, j: (0, 0, j))],
        out_specs=pl.BlockSpec((S, W), lambda b, j: (b, j)),
        compiler_params=_params(("parallel", "parallel"), 6 * _nbytes((S, W), F32)),
    )(up3, conv_w, conv_b)


def _gate_bwd(up3, dact, conv_w, conv_b, B, S):
    T = B * S
    W = FF_TILE
    C = GATE_ROWS

    def body(up_ref, da_ref, cw_ref, cb_ref, dup_ref, gcw_ref, gcb_ref, d_s):
        @pl.when(pl.program_id(1) == 0)
        def _():
            gcw_ref[...] = jnp.zeros_like(gcw_ref)
            gcb_ref[...] = jnp.zeros_like(gcb_ref)

        def chunk(r, first, sums):
            rows = pl.ds(0 if first else pl.multiple_of(r * C, C), C)
            taps = [_taps(up_ref, half, r, first) for half in (0, 1)]
            gate = _conv_taps(taps[0], cw_ref[0], cb_ref[0])
            val = _conv_taps(taps[1], cw_ref[1], cb_ref[1])
            sg = _sigmoid(gate)
            da = da_ref[rows, :]
            d_halves = (da * val * (sg * (1.0 + gate * (1.0 - sg))), da * (gate * sg))
            out = []
            for half, dup in enumerate(d_halves):
                d_s[half, rows, :] = dup
                x0, x1, x2 = taps[half]
                sb, s0, s1, s2 = sums[half]
                out.append((sb + _fold8(dup), s0 + _fold8(dup * x2), s1 + _fold8(dup * x1), s2 + _fold8(dup * x0)))
            return tuple(out)

        zeros = tuple(tuple(jnp.zeros((8, W), F32) for _ in range(4)) for _ in range(2))
        sums = chunk(0, True, zeros)
        sums = lax.fori_loop(1, S // C, lambda r, s: chunk(r, False, s), sums)
        for half in (0, 1):
            sb, s0, s1, s2 = sums[half]
            gcb_ref[half] += jnp.sum(sb, axis=0, keepdims=True)
            gcw_ref[half, 0:1, :] += jnp.sum(s0, axis=0, keepdims=True)
            gcw_ref[half, 1:2, :] += jnp.sum(s1, axis=0, keepdims=True)
            gcw_ref[half, 2:3, :] += jnp.sum(s2, axis=0, keepdims=True)

        d_s[:, S:S + HALO, :] = jnp.zeros((2, HALO, W), F32)

        @pl.loop(0, S // C)
        def _(r):
            base = pl.multiple_of(r * C, C)
            for half in (0, 1):
                ds_ = d_s[half, pl.ds(base, C + HALO), :]
                cw = cw_ref[half]
                dx = (cw[2:3, :] * ds_[:C, :] + cw[1:2, :] * pltpu.roll(ds_, C + HALO - 1, 0)[:C, :]
                      + cw[0:1, :] * pltpu.roll(ds_, C + HALO - 2, 0)[:C, :])
                dup_ref[half, pl.ds(base, C), :] = dx.astype(dup_ref.dtype)

    up_spec = pl.BlockSpec((2, S, W), lambda j, b: (0, b, j))
    cw_spec = pl.BlockSpec((2, 3, W), lambda j, b: (0, 0, j))
    cb_spec = pl.BlockSpec((2, 1, W), lambda j, b: (0, 0, j))
    return pl.pallas_call(
        body, name="gate_bwd",
        out_shape=(jax.ShapeDtypeStruct((2, T, D_FF), MXU_DTYPE), jax.ShapeDtypeStruct((2, 3, D_FF), F32),
                   jax.ShapeDtypeStruct((2, 1, D_FF), F32)),
        grid=(N_FF_TILES, B),
        in_specs=[up_spec, pl.BlockSpec((S, W), lambda j, b: (b, j)), cw_spec, cb_spec],
        out_specs=(up_spec, cw_spec, cb_spec),
        scratch_shapes=[pltpu.VMEM((2, S + HALO, W), F32)],
        compiler_params=_params(("parallel", "arbitrary"), 10 * _nbytes((S, W), F32)),
    )(up3, dact, conv_w, conv_b)


def _final(x2, tgt, g, tr=512):
    T, D = x2.shape

    def body(x_ref, t_ref, g_ref, dx_ref, loss_ref, gg_ref):
        @pl.when(pl.program_id(0) == 0)
        def _():
            loss_ref[...] = jnp.zeros_like(loss_ref)
            gg_ref[...] = jnp.zeros_like(gg_ref)

        xv = x_ref[...]
        gv = g_ref[...]
        r = lax.rsqrt(jnp.mean(xv * xv, axis=-1, keepdims=True) + EPS)
        xn = xv * r
        err = xn * gv - t_ref[...]
        loss_ref[...] += 0.5 * jnp.sum(jnp.mean(err * err, axis=-1, keepdims=True), axis=0, keepdims=True)
        dy = err * (1.0 / D)
        gg_ref[...] += jnp.sum(dy * xn, axis=0, keepdims=True)
        dxn = dy * gv
        dx_ref[...] = r * (dxn - xn * jnp.mean(dxn * xn, axis=-1, keepdims=True))

    row = pl.BlockSpec((tr, D), lambda i: (i, 0))
    vec = pl.BlockSpec((1, D), lambda i: (0, 0))
    return pl.pallas_call(
        body, name="final_loss",
        out_shape=(jax.ShapeDtypeStruct((T, D), F32), jax.ShapeDtypeStruct((1, 128), F32), jax.ShapeDtypeStruct((1, D), F32)),
        grid=(T // tr,), in_specs=[row, row, vec],
        out_specs=(row, pl.BlockSpec((1, 128), lambda i: (0, 0)), vec),
        compiler_params=_params(("arbitrary",), 6 * _nbytes((tr, D), F32)),
    )(x2, tgt, g)


def _sum_slabs(parts, name, tr):
    rows, cols = parts[0].shape
    n = len(parts)

    def body(*refs):
        acc = refs[0][...]
        for r in refs[1:n]:
            acc = acc + r[...]
        refs[n][...] = acc

    blk = pl.BlockSpec((tr, cols), lambda i: (i, 0))
    return pl.pallas_call(
        body, name=name, out_shape=jax.ShapeDtypeStruct((rows, cols), F32), grid=(rows // tr,),
        in_specs=[blk] * n, out_specs=blk,
        compiler_params=_params(("parallel",), (n + 1) * _nbytes((tr, cols), F32)),
    )(*parts)


def _adamw(w, g, m, v, name):
    lead = w.ndim == 3
    rows, cols = w.shape[-2:]
    tr = rows
    for cand in (256, 128, 64, 32, 16, 8):
        if rows % cand == 0:
            tr = cand
            break
    c1 = 1.0 - ADAM_B1 ** ADAM_STEP
    c2 = 1.0 - ADAM_B2 ** ADAM_STEP

    def body(w_ref, g_ref, m_ref, v_ref, d_ref, nm_ref, nv_ref):
        gv = g_ref[...]
        nm = ADAM_B1 * m_ref[...] + (1.0 - ADAM_B1) * gv
        nv = ADAM_B2 * v_ref[...] + (1.0 - ADAM_B2) * (gv * gv)
        nm_ref[...] = nm
        nv_ref[...] = nv
        d_ref[...] = -ADAM_LR * ((nm / c1) / (jnp.sqrt(nv / c2) + ADAM_EPS) + ADAM_WD * w_ref[...])

    blk = pl.BlockSpec((None, tr, cols), lambda i: (0, i, 0)) if lead else pl.BlockSpec((tr, cols), lambda i: (i, 0))
    sds = jax.ShapeDtypeStruct(w.shape, F32)
    return pl.pallas_call(
        body, name=name, out_shape=(sds, sds, sds), grid=(rows // tr,), in_specs=[blk] * 4, out_specs=(blk, blk, blk),
        compiler_params=_params(("parallel",), 7 * _nbytes((tr, cols), F32)),
    )(w, g, m, v)


_ANY = pl.BlockSpec(memory_space=pl.ANY)


def _place():
    x, y, c = lax.axis_index("x"), lax.axis_index("y"), lax.axis_index("c")
    chips = [(1 - x, y), (x, 1 - y), (1 - x, 1 - y)]
    return x, y, c, chips


def _gather_weights(shards):
    n = len(shards)

    def body(*refs):
        ins, outs = refs[:n], refs[n:2 * n]
        send, recv, fsend, frecv, osend, orecv = refs[2 * n:]
        x, y, c, chips = _place()
        me = 2 * x + y
        first, passed = [], []
        for w in range(n):
            first.append(pltpu.make_async_remote_copy(
                src_ref=ins[w], dst_ref=outs[w].at[me], send_sem=osend.at[w], recv_sem=orecv.at[w],
                device_id=(x, y, 1 - c), device_id_type=MESH))
        for w in range(n):
            for j, (px, py) in enumerate(chips):
                first.append(pltpu.make_async_remote_copy(
                    src_ref=ins[w].at[c], dst_ref=outs[w].at[me, c], send_sem=send.at[3 * w + j],
                    recv_sem=recv.at[3 * w + j], device_id=(px, py, c), device_id_type=MESH))
        for cp in first:
            cp.start()
        for w in range(n):
            for j, (px, py) in enumerate(chips):
                landed = outs[w].at[2 * px + py, c]
                pltpu.make_async_remote_copy(src_ref=landed, dst_ref=landed, send_sem=send.at[3 * w + j],
                                             recv_sem=recv.at[3 * w + j], device_id=(px, py, c),
                                             device_id_type=MESH).wait_recv()
                fw = pltpu.make_async_remote_copy(src_ref=landed, dst_ref=landed, send_sem=fsend.at[3 * w + j],
                                                  recv_sem=frecv.at[3 * w + j], device_id=(x, y, 1 - c),
                                                  device_id_type=MESH)
                fw.start()
                passed.append(fw)
        for w in range(n):
            for j, (px, py) in enumerate(chips):
                other = outs[w].at[2 * px + py, 1 - c]
                pltpu.make_async_remote_copy(src_ref=other, dst_ref=other, send_sem=fsend.at[3 * w + j],
                                             recv_sem=frecv.at[3 * w + j], device_id=(x, y, 1 - c),
                                             device_id_type=MESH).wait_recv()
        for w in range(n):
            own = outs[w].at[me]
            pltpu.make_async_remote_copy(src_ref=own, dst_ref=own, send_sem=osend.at[w], recv_sem=orecv.at[w],
                                         device_id=(x, y, 1 - c), device_id_type=MESH).wait_recv()
        for cp in first + passed:
            cp.wait_send()

    dma = lambda k: pltpu.SemaphoreType.DMA((k,))
    return pl.pallas_call(
        body, name="gather_weights",
        out_shape=tuple(jax.ShapeDtypeStruct((N_CHIPS,) + s.shape, s.dtype) for s in shards),
        in_specs=[_ANY] * n, out_specs=tuple([_ANY] * n),
        scratch_shapes=[dma(3 * n), dma(3 * n), dma(3 * n), dma(3 * n), dma(n), dma(n)],
    )(*shards)


_HBM = pl.BlockSpec(memory_space=pltpu.HBM)
_SEM = pl.BlockSpec(memory_space=pltpu.SEMAPHORE)
_EFFECT = pltpu.SideEffectType.DATAFLOW_SIDE_EFFECTING


SEMS_PER_ARRAY = 8


def _exchange_copies(srcs, lands, send, recv, mode):
    x, y, c, chips = _place()
    if mode == "all":
        flips = [(fx, fy, fc) for fx in (0, 1) for fy in (0, 1) for fc in (0, 1)][1:]
        peers = [(x ^ fx, y ^ fy, c ^ fc) for fx, fy, fc in flips]
        slot = 4 * x + 2 * y + c
    else:
        peers = [(px, py, c) for px, py in chips] + ([(x, y, 1 - c)] if mode == "gather" else [])
        slot = 2 * x + y
    cps = []
    for w, (src, land) in enumerate(zip(srcs, lands)):
        for k, peer in enumerate(peers):
            piece = src.at[2 * peer[0] + peer[1]] if mode == "scatter" else src
            cps.append(pltpu.make_async_remote_copy(
                src_ref=piece, dst_ref=land.at[slot], send_sem=send.at[SEMS_PER_ARRAY * w + k],
                recv_sem=recv.at[SEMS_PER_ARRAY * w + k], device_id=peer, device_id_type=MESH))
    return cps


def _exchange_start(srcs, name, mode, after):
    n = len(srcs)
    lead = {"gather": (N_CHIPS,), "scatter": (), "all": (2 * N_CHIPS,)}[mode]
    land_shapes = [lead + s.shape for s in srcs]

    def body(*refs):
        src_refs, land_refs = refs[:n], refs[n:2 * n]
        send, recv = refs[2 * n + 1], refs[2 * n + 2]
        token = refs[-1]
        for cp in _exchange_copies(src_refs, land_refs, send, recv, mode):
            cp.start()
        token[...] = jnp.zeros_like(token)

    sems = pltpu.SemaphoreType.DMA((SEMS_PER_ARRAY * n,))
    out = pl.pallas_call(
        body, name=name,
        out_shape=(sems, sems, *[pltpu.HBM(s.shape, s.dtype) for s in srcs],
                   *[pltpu.HBM(shp, s.dtype) for shp, s in zip(land_shapes, srcs)], jax.ShapeDtypeStruct((8, 128), F32)),
        in_specs=[_HBM] * (2 * n) + [_ANY],
        out_specs=(_SEM, _SEM, *[_HBM] * (2 * n), pl.BlockSpec(memory_space=pltpu.VMEM)),
        input_output_aliases={i: 2 + i for i in range(2 * n)},
        compiler_params=pltpu.CompilerParams(has_side_effects=_EFFECT),
    )(*[pltpu.with_memory_space_constraint(s, pltpu.HBM) for s in srcs],
      *[pltpu.with_memory_space_constraint(lax.empty(shp, s.dtype), pltpu.HBM) for shp, s in zip(land_shapes, srcs)],
      after)
    return out[0], out[1], out[2:2 + n], out[2 + n:2 + 2 * n], out[-1]


def _exchange_wait(started, name, mode, after):
    send, recv, src_thru, land_thru, _ = started
    n = len(src_thru)

    def body(*refs):
        src_refs, land_refs, send_ref, recv_ref = refs[:n], refs[n:2 * n], refs[2 * n], refs[2 * n + 1]
        for cp in _exchange_copies(src_refs, land_refs, send_ref, recv_ref, mode):
            cp.wait_send()
            cp.wait_recv()

    out = pl.pallas_call(
        body, name=name,
        out_shape=tuple(pltpu.HBM(a.shape, a.dtype) for a in list(src_thru) + list(land_thru)),
        in_specs=[_HBM] * (2 * n) + [_SEM, _SEM, _ANY], out_specs=tuple([_HBM] * (2 * n)),
        input_output_aliases={i: i for i in range(2 * n)},
        compiler_params=pltpu.CompilerParams(has_side_effects=_EFFECT),
    )(*src_thru, *land_thru, send, recv, after)
    return out[:n], out[n:]


def _swap_halves(gs, name):
    n = len(gs)

    def body(*refs):
        ins, outs, send, recv = refs[:n], refs[n:2 * n], refs[2 * n], refs[2 * n + 1]
        x, y, c, _ = _place()
        cps = []
        for w in range(n):
            cps.append(pltpu.make_async_remote_copy(
                src_ref=ins[w].at[:, 1 - c], dst_ref=outs[w], send_sem=send.at[w], recv_sem=recv.at[w],
                device_id=(x, y, 1 - c), device_id_type=MESH))
        for cp in cps:
            cp.start()
        for cp in cps:
            cp.wait()

    return pl.pallas_call(
        body, name=name,
        out_shape=tuple(jax.ShapeDtypeStruct((g.shape[0],) + g.shape[2:], g.dtype) for g in gs),
        in_specs=[_ANY] * n, out_specs=tuple([_ANY] * n),
        scratch_shapes=[pltpu.SemaphoreType.DMA((n,)), pltpu.SemaphoreType.DMA((n,))],
    )(*gs)


GRAD_PAYLOAD = jnp.bfloat16


def _pair_sum(gs, gots, name):
    n = len(gs)
    core = lax.axis_index("c").astype(jnp.int32).reshape(1)

    def body(core_ref, *refs):
        del core_ref
        for w in range(n):
            refs[2 * n + w][...] = (refs[w][...] + refs[n + w][...]).astype(GRAD_PAYLOAD)

    in_specs, out_specs, out_shape, nbytes = [], [], [], 0
    for g in gs:
        q = g.shape[1] // 4
        in_specs.append(pl.BlockSpec((1, q, g.shape[2]), lambda s, r, core: (s, 2 * core[0] + r, 0)))
        nbytes += 3 * _nbytes((q, g.shape[2]), F32)
    for g in gs:
        q = g.shape[1] // 4
        in_specs.append(pl.BlockSpec((1, q, g.shape[2]), lambda s, r, core: (s, r, 0)))
        out_specs.append(pl.BlockSpec((1, q, g.shape[2]), lambda s, r, core: (s, r, 0)))
        out_shape.append(jax.ShapeDtypeStruct((g.shape[0], g.shape[1] // 2, g.shape[2]), GRAD_PAYLOAD))
    return pl.pallas_call(
        body, name=name, out_shape=tuple(out_shape),
        grid_spec=pltpu.PrefetchScalarGridSpec(num_scalar_prefetch=1, grid=(N_CHIPS, 2), in_specs=in_specs,
                                               out_specs=tuple(out_specs)),
        compiler_params=_params(("parallel", "parallel"), nbytes),
    )(core, *gs, *gots)


def _chip_sum(ps, landed):
    n = len(ps)
    x, y, c = lax.axis_index("x"), lax.axis_index("y"), lax.axis_index("c")
    where = jnp.stack([2 * x + y, 2 * (1 - x) + y, 2 * x + (1 - y), 2 * (1 - x) + (1 - y), c]).astype(jnp.int32)

    def body(where_ref, *refs):
        del where_ref
        for w in range(n):
            terms = [refs[4 * w + t][...].astype(F32) for t in range(4)]
            refs[4 * n + w][...] = ((terms[0] + terms[1]) + terms[2]) + terms[3]

    in_specs, out_specs, out_shape, args, nbytes = [], [], [], [], 0
    for p, a in zip(ps, landed):
        q = a.shape[1] // 2
        blk = (1, q, a.shape[2])
        in_specs.append(pl.BlockSpec(blk, lambda r, where: (where[0], r, 0)))
        args.append(p)
        for t in (1, 2, 3):
            in_specs.append(pl.BlockSpec(blk, lambda r, where, t=t: (where[t], r, 0)))
            args.append(a)
        out_specs.append(pl.BlockSpec(blk, lambda r, where: (where[4], r, 0)))
        out_shape.append(jax.ShapeDtypeStruct((2,) + a.shape[1:], F32))
        nbytes += 4 * _nbytes(blk, F32)
    return pl.pallas_call(
        body, name="grad_chip_sum", out_shape=tuple(out_shape),
        grid_spec=pltpu.PrefetchScalarGridSpec(num_scalar_prefetch=1, grid=(2,), in_specs=in_specs,
                                               out_specs=tuple(out_specs)),
        compiler_params=_params(("parallel",), nbytes),
    )(where, *args)


def _join_halves(ss):
    n = len(ss)

    def body(*refs):
        outs, send, recv = refs[n:2 * n], refs[2 * n], refs[2 * n + 1]
        x, y, c, _ = _place()
        cps = []
        for w in range(n):
            cps.append(pltpu.make_async_remote_copy(
                src_ref=outs[w].at[c], dst_ref=outs[w].at[c], send_sem=send.at[w], recv_sem=recv.at[w],
                device_id=(x, y, 1 - c), device_id_type=MESH))
        for cp in cps:
            cp.start()
        for w in range(n):
            got = outs[w].at[1 - c]
            pltpu.make_async_remote_copy(src_ref=got, dst_ref=got, send_sem=send.at[w], recv_sem=recv.at[w],
                                         device_id=(x, y, 1 - c), device_id_type=MESH).wait_recv()
        for cp in cps:
            cp.wait_send()

    dma = lambda k: pltpu.SemaphoreType.DMA((k,))
    return pl.pallas_call(
        body, name="grad_join_halves",
        out_shape=tuple(jax.ShapeDtypeStruct(s.shape, s.dtype) for s in ss),
        in_specs=[_ANY] * n, out_specs=tuple([_ANY] * n), input_output_aliases={w: w for w in range(n)},
        scratch_shapes=[dma(n), dma(n)],
    )(*ss)


def _rot_cols(w):
    a, b = jnp.split(w, 2, axis=-1)
    return jnp.concatenate([-b, a], axis=-1)


def _rot_cols_t(g):
    a, b = jnp.split(g, 2, axis=-1)
    return jnp.concatenate([b, -a], axis=-1)


def _cols_from_chips(a):
    n, r, cs = a.shape
    return jnp.transpose(a, (1, 0, 2)).reshape(r, n * cs)


def _cols_to_chips(a):
    r, cc = a.shape
    return jnp.transpose(a.reshape(r, N_CHIPS, cc // N_CHIPS), (1, 0, 2))


def _conv_w_split(cw):
    return jnp.swapaxes(cw.reshape(3, 2, D_FF), 0, 1)


def _conv_w_join(g):
    return jnp.swapaxes(g, 0, 1).reshape(3, 2 * D_FF)


_SEG =(D_MODEL, 2 * D_MODEL, 2 * D_MODEL + Q_RANK, 2 * D_MODEL + Q_RANK + KV_RANK, 2 * D_MODEL + Q_RANK + KV_RANK + ROPE,
        3 * D_MODEL + Q_RANK + KV_RANK + ROPE)


def _w_in_to_pad(w):
    u, v, cq, ckv, kr, ga, gb = jnp.split(w, _SEG, axis=1)
    return jnp.concatenate([u, v, ga, gb, cq, ckv, kr, _rot_cols(kr)], axis=1)


def _w_in_from_pad(g):
    u, v, ga, gb, cq, ckv, kr, krr = jnp.split(
        g, (D_MODEL, 2 * D_MODEL, 3 * D_MODEL, 4 * D_MODEL, 4 * D_MODEL + Q_RANK, 4 * D_MODEL + Q_RANK + KV_RANK,
            4 * D_MODEL + Q_RANK + KV_RANK + ROPE), axis=1)
    return jnp.concatenate([u, v, cq, ckv, kr + _rot_cols_t(krr), ga, gb], axis=1)


def _w_uq_to_pad(w):
    t = w.reshape(Q_RANK, HEADS, QK_DIM)
    nope, rope = t[..., :NOPE], t[..., NOPE:]
    return jnp.concatenate([nope, rope, _rot_cols(rope)], axis=-1).reshape(Q_RANK, HEADS * HEAD_PAD)


def _w_uq_from_pad(g):
    t = g.reshape(Q_RANK, HEADS, HEAD_PAD)
    nope, rope, rot = t[..., :NOPE], t[..., NOPE:QK_DIM], t[..., QK_DIM:]
    return jnp.concatenate([nope, rope + _rot_cols_t(rot)], axis=-1).reshape(Q_RANK, HEADS * QK_DIM)


def _w_ukv_to_pad(w):
    t = w.reshape(KV_RANK, HEADS, 2, NOPE)
    return jnp.swapaxes(t, 1, 2).reshape(KV_RANK, 2 * HEADS * NOPE)


def _w_ukv_from_pad(g):
    t = g.reshape(KV_RANK, 2, HEADS, NOPE)
    return jnp.swapaxes(t, 1, 2).reshape(KV_RANK, 2 * HEADS * NOPE)


def _rope_tables(positions):
    inv_freq = 1.0 / (ROPE_THETA ** (jnp.arange(0, ROPE, 2, dtype=F32) / ROPE))
    ang = positions.astype(F32).reshape(-1, 1) * inv_freq
    cos, sin = jnp.cos(ang), jnp.sin(ang)
    zero = jnp.zeros((ang.shape[0], 64), F32)
    return jnp.concatenate([cos, cos, zero], axis=1), jnp.concatenate([sin, sin, zero], axis=1)


_BIG = ("w_in", "w_uq", "w_ukv", "w_out", "w_up", "w_down")
UP_SHARD = 2 * D_FF // N_CHIPS


def _local_step(x, positions, tgt, wts, ffn_weights, on_ffn_grads, on_mixer_grads):
    B, S, D = x.shape
    T = B * S
    xf = x.reshape(T, D)
    cos_a, sin_a = _rope_tables(positions)
    bs_t = jnp.pad(wts["a_spatial_b"].T, ((0, 0), (0, 128 - A_GROUPS)))

    h = _rms_fwd(xf, wts["mix_norm"], "norm1_fwd")
    z = _mm(h, wts["w_in"], "nn", "in_proj", tm=512, tn=1536, tk=D)
    q, k, v, cqn, ckvn = _lat_fwd(z, wts["q_a_norm"], wts["kv_a_norm"], wts["w_q"], wts["w_kv"], cos_a, sin_a)
    yb, *lses = _attn_fwd(q, k, v, B, S)
    merged = _mix_fwd(z, yb, wts["a_v_norm_g"], wts["a_v_norm_b"], wts["a_spatial_w"], bs_t)
    x1 = _mm(merged, wts["w_out"], "nn", "out_proj", tm=512, tn=D, tk=D, add=xf)
    h2 = _rms_fwd(x1, wts["ffn_norm"], "norm2_fwd")
    wts = dict(wts)
    wts["w_up"], wts["w_down"], wts["conv_w"] = ffn_weights(h2)
    up_pre = _mm(h2, wts["w_up"], "nn", "up_proj", tm=512, tn=UP_SHARD, tk=D, dims=(T, 2 * D_FF, D),
                 b_spec=pl.BlockSpec((None, D, UP_SHARD), lambda i, j, k: (j, 0, 0)),
                 o_spec=pl.BlockSpec((None, 512, UP_SHARD), lambda i, j, k: (j // 2, i, j % 2)), out_shape=(2, T, D_FF))
    act = _gate_fwd(up_pre, wts["conv_w"], wts["conv_b"], B, S)
    x2 = _mm(act, wts["w_down"], "nn", "down_proj", tm=512, tn=D, tk=1408, add=x1)
    dx2, loss_row, g_final = _final(x2, tgt.reshape(T, D), wts["final_norm"])

    g = {"final_norm": g_final}
    dact = _mm(dx2, wts["w_down"], "nt", "down_proj_dx", tm=512, tn=1408, tk=D)
    g["w_down"] = _mm(act, dx2, "tn", "down_proj_dw", tm=1408, tn=D, tk=512)
    dup, g["conv_w"], g["conv_b"] = _gate_bwd(up_pre, dact, wts["conv_w"], wts["conv_b"], B, S)
    dh2 = _mm(dup, wts["w_up"], "nt", "up_proj_dx", tm=512, tn=D, tk=UP_SHARD, dims=(T, D, 2 * D_FF),
              a_spec=pl.BlockSpec((None, 512, UP_SHARD), lambda i, j, k: (k // 2, i, k % 2)),
              b_spec=pl.BlockSpec((None, D, UP_SHARD), lambda i, j, k: (k, 0, 0)))
    g["w_up"] = _mm(h2, dup, "tn", "up_proj_dw", tm=D, tn=UP_SHARD, tk=512, dims=(D, 2 * D_FF, T),
                    b_spec=pl.BlockSpec((None, 512, UP_SHARD), lambda i, j, k: (j // 2, k, j % 2)),
                    o_spec=pl.BlockSpec((None, D, UP_SHARD), lambda i, j, k: (j, 0, 0)), out_shape=(N_CHIPS, D, UP_SHARD))
    token = on_ffn_grads(g["w_up"], g["w_down"])
    ffn_norm = wts["ffn_norm"] if token is None else wts["ffn_norm"] + token[0:1, 0:1]
    dx1, g["ffn_norm"] = _rms_bwd(x1, ffn_norm, dh2, dx2, "norm2_bwd")
    dm = _mm(dx1, wts["w_out"], "nt", "out_proj_dx", tm=512, tn=D, tk=D)
    g["w_out"] = _mm(merged, dx1, "tn", "out_proj_dw", tm=D, tn=D, tk=512)
    dz, dyb, dl, g["a_spatial_w"], gbs, g["a_v_norm_g"], g["a_v_norm_b"] = _mix_bwd(
        z, yb, dm, wts["a_v_norm_g"], wts["a_v_norm_b"], wts["a_spatial_w"], bs_t)
    g["a_spatial_b"] = gbs[:, :A_GROUPS].T
    delta = dl.reshape(HEADS * T // ATT_BLOCK, 1, ATT_BLOCK)
    dq, dk, dv = _attn_bwd(q, k, v, dyb, lses, delta, B, S)
    dz, dq_raw, dkv, g["q_a_norm"], g["kv_a_norm"] = _lat_bwd(
        dz, z, dq, dk, dv, wts["q_a_norm"], wts["kv_a_norm"], wts["w_q"], wts["w_kv"], cos_a, sin_a)
    g["w_q"] = _mm(cqn, dq_raw, "tn", "q_proj_dw", tm=Q_RANK, tn=HEADS * HEAD_PAD, tk=512)
    g["w_kv"] = _mm(ckvn, dkv, "tn", "kv_proj_dw", tm=KV_RANK, tn=2 * HEADS * NOPE, tk=512)
    g["w_in"] = _mm(h, dz, "tn", "in_proj_dw", tm=D, tn=1536, tk=512)
    token = on_mixer_grads(g)
    mix_norm = wts["mix_norm"] if token is None else wts["mix_norm"] + token[0:1, 0:1]
    dh = _mm(dz, wts["w_in"], "nt", "in_proj_dx", tm=512, tn=D, tk=1536)
    dx, g["mix_norm"] = _rms_bwd(xf, mix_norm, dh, dx1, "norm1_bwd")
    return loss_row[0, 0], dx.reshape(B, S, D), g


_SMALL = (("mix_norm", (1, D_MODEL)), ("a_v_norm_g", (1, D_MODEL)), ("a_v_norm_b", (1, D_MODEL)),
          ("a_spatial_w", (A_GROUPS * CHUNK, CHUNK)), ("a_spatial_b", (1, A_GROUPS * CHUNK)), ("q_a_norm", (1, Q_RANK)),
          ("kv_a_norm", (1, KV_RANK)), ("ffn_norm", (1, D_MODEL)), ("conv_b", (1, 2 * D_FF)), ("final_norm", (1, D_MODEL)),
          ("conv_w", (3, 2 * D_FF)))
_SMALL_ROWS = -(-sum(math.prod(s) for _, s in _SMALL) // (128 * 8)) * 8


def kernel(x, positions, mix_norm, w_in, a_v_norm_g, a_v_norm_b, a_spatial_w, a_spatial_b, q_a_norm, w_uq, kv_a_norm, w_ukv, w_out, ffn_norm, w_up, conv_w, conv_b, w_down, final_norm, loss_target, m_mix_norm, m_w_in, m_a_v_norm_g, m_a_v_norm_b, m_a_spatial_w, m_a_spatial_b, m_q_a_norm, m_w_uq, m_kv_a_norm, m_w_ukv, m_w_out, m_ffn_norm, m_w_up, m_conv_w, m_conv_b, m_w_down, m_final_norm, v_mix_norm, v_w_in, v_a_v_norm_g, v_a_v_norm_b, v_a_spatial_w, v_a_spatial_b, v_q_a_norm, v_w_uq, v_kv_a_norm, v_w_ukv, v_w_out, v_ffn_norm, v_w_up, v_conv_w, v_conv_b, v_w_down, v_final_norm):
    weights = dict(mix_norm=mix_norm, w_in=w_in, a_v_norm_g=a_v_norm_g, a_v_norm_b=a_v_norm_b, a_spatial_w=a_spatial_w,
                   a_spatial_b=a_spatial_b, q_a_norm=q_a_norm, w_uq=w_uq, kv_a_norm=kv_a_norm, w_ukv=w_ukv, w_out=w_out,
                   ffn_norm=ffn_norm, w_up=w_up, conv_w=conv_w, conv_b=conv_b, w_down=w_down, final_norm=final_norm)
    m_in = dict(mix_norm=m_mix_norm, w_in=m_w_in, a_v_norm_g=m_a_v_norm_g, a_v_norm_b=m_a_v_norm_b,
                a_spatial_w=m_a_spatial_w, a_spatial_b=m_a_spatial_b, q_a_norm=m_q_a_norm, w_uq=m_w_uq,
                kv_a_norm=m_kv_a_norm, w_ukv=m_w_ukv, w_out=m_w_out, ffn_norm=m_ffn_norm, w_up=m_w_up, conv_w=m_conv_w,
                conv_b=m_conv_b, w_down=m_w_down, final_norm=m_final_norm)
    v_in = dict(mix_norm=v_mix_norm, w_in=v_w_in, a_v_norm_g=v_a_v_norm_g, a_v_norm_b=v_a_v_norm_b,
                a_spatial_w=v_a_spatial_w, a_spatial_b=v_a_spatial_b, q_a_norm=v_q_a_norm, w_uq=v_w_uq,
                kv_a_norm=v_kv_a_norm, w_ukv=v_w_ukv, w_out=v_w_out, ffn_norm=v_ffn_norm, w_up=v_w_up, conv_w=v_conv_w,
                conv_b=v_conv_b, w_down=v_w_down, final_norm=v_final_norm)
    names = list(weights)
    chip = 2 * lax.axis_index("x") + lax.axis_index("y")

    def halves(a):
        return a.reshape(a.shape[:-2] + (2, a.shape[-2] // 2, a.shape[-1]))

    def whole(a):
        return a.reshape(a.shape[:-3] + (2 * a.shape[-2], a.shape[-1]))

    gathered = _gather_weights([halves(weights[n][0].astype(MXU_DTYPE)) for n in _BIG[:4]])
    w_in_sh, w_uq_sh, w_ukv_sh, w_out_sh = (whole(a) for a in gathered)
    ffn_gather = _exchange_start([w_up[0].astype(MXU_DTYPE), w_down[0].astype(MXU_DTYPE), conv_w[0]],
                                 "ffn_gather_start", "gather", after=gathered[3])
    wts = dict(
        mix_norm=mix_norm + ffn_gather[4][0:1, 0:1], a_v_norm_g=a_v_norm_g, a_v_norm_b=a_v_norm_b,
        a_spatial_w=a_spatial_w[0], a_spatial_b=a_spatial_b[0], q_a_norm=q_a_norm, kv_a_norm=kv_a_norm,
        ffn_norm=ffn_norm, final_norm=final_norm.reshape(1, D_MODEL),
        w_in=_w_in_to_pad(_cols_from_chips(w_in_sh)), w_q=_w_uq_to_pad(_cols_from_chips(w_uq_sh)),
        w_kv=_w_ukv_to_pad(_cols_from_chips(w_ukv_sh)), w_out=w_out_sh.reshape(D_MODEL, D_MODEL),
        conv_b=conv_b.reshape(2, 1, D_FF))

    def ffn_weights(after):
        _, (w_up_sh, w_down_sh, cw_all) = _exchange_wait(ffn_gather, "ffn_gather_wait", "gather", after)
        return w_up_sh, w_down_sh.reshape(D_FF, D_MODEL), _conv_w_split(_cols_from_chips(cw_all))

    scatters = {}

    def start_scatter(slabs, tag):
        sums = _pair_sum(slabs, _swap_halves([halves(s) for s in slabs], tag + "_grad_swap_halves"), tag + "_grad_pair_sum")
        scatters[tag] = _exchange_start(list(sums), tag + "_scatter_start", "scatter", after=slabs[-1])
        return scatters[tag][4]

    def on_ffn_grads(g_w_up, g_w_down):
        return start_scatter([g_w_up, g_w_down.reshape(N_CHIPS, D_FF // N_CHIPS, D_MODEL)], "ffn")

    def on_mixer_grads(g):
        return start_scatter(
            [_cols_to_chips(_w_in_from_pad(g["w_in"])), _cols_to_chips(_w_uq_from_pad(g["w_q"])),
             _cols_to_chips(_w_ukv_from_pad(g["w_kv"])), g["w_out"].reshape(N_CHIPS, D_MODEL // N_CHIPS, D_MODEL)], "mixer")

    loss_part, grad_x, g = _local_step(x, positions, loss_target, wts, ffn_weights, on_ffn_grads, on_mixer_grads)
    loss = lax.psum(loss_part, ("x", "y", "c"))

    g_small_parts = dict(g)
    g_small_parts["conv_w"] = _conv_w_join(g["conv_w"])
    g_small_parts["conv_b"] = g["conv_b"].reshape(1, 2 * D_FF)
    flat = jnp.concatenate([g_small_parts[n].reshape(-1) for n, _ in _SMALL])
    flat = jnp.pad(flat, (0, _SMALL_ROWS * 128 - flat.shape[0])).reshape(_SMALL_ROWS, 128)
    small_gather = _exchange_start([flat], "small_gather_start", "all", after=grad_x)

    mixer_sums, mixer_landed = _exchange_wait(scatters["mixer"], "mixer_scatter_wait", "scatter", after=small_gather[4])
    ffn_sums, ffn_landed = _exchange_wait(scatters["ffn"], "ffn_scatter_wait", "scatter", after=mixer_landed[0])
    reduced = _chip_sum(list(mixer_sums) + list(ffn_sums), list(mixer_landed) + list(ffn_landed))
    g_big = dict(zip(_BIG, _join_halves(reduced)))

    grads, deltas, new_m, new_v = {}, {}, {}, {}

    def update(n, grad):
        w = weights[n]
        shape2 = grad.shape
        d, nm, nv = _adamw(w.reshape(shape2), grad, m_in[n].reshape(shape2), v_in[n].reshape(shape2), "adamw_" + n)
        grads[n], deltas[n], new_m[n], new_v[n] = (t.reshape(w.shape) for t in (grad, d, nm, nv))

    for n in _BIG:
        update(n, g_big[n].reshape((1, -1, g_big[n].shape[-1])))

    (own,), (everyone,) = _exchange_wait(small_gather, "small_gather_wait", "all", after=deltas["w_up"])
    device = 2 * chip + lax.axis_index("c")
    everyone = lax.dynamic_update_slice(everyone, own[None], (device, 0, 0))
    total = _sum_slabs([everyone[j] for j in range(8)], "small_grads_sum", tr=_SMALL_ROWS).reshape(-1)
    o = 0
    for n, shp in _SMALL:
        piece = total[o:o + math.prod(shp)].reshape(shp)
        o += math.prod(shp)
        if n == "conv_w":
            piece = lax.dynamic_slice_in_dim(piece, chip * UP_SHARD, UP_SHARD, axis=1)
        update(n, piece)
    return (loss, grad_x, *[grads[n] for n in names], *[deltas[n] for n in names], *[new_m[n] for n in names],
            *[new_v[n] for n in names])
```

```python
import functools
import math

import jax
import jax.numpy as jnp
from jax import lax
from jax.experimental import pallas as pl
from jax.experimental.pallas import tpu as pltpu

F32 = jnp.float32
MXU_DTYPE = jnp.bfloat16
MESH = pl.DeviceIdType.MESH

D_MODEL = 1024
EPS = 1e-6
A_GROUPS = 8
CHUNK = 128
HEADS = 8
NOPE = 128
ROPE = 64
QK_DIM = NOPE + ROPE
HEAD_PAD = 256
Q_RANK = 256
KV_RANK = 128
ROPE_THETA = 10000.0
D_FF = 2816
FF_TILE = 256
N_FF_TILES = D_FF // FF_TILE
LAT = 512
IN_PAD = 4 * D_MODEL + LAT
N_CHIPS = 4
ADAM_LR, ADAM_B1, ADAM_B2, ADAM_EPS, ADAM_WD, ADAM_STEP = 0.001, 0.9, 0.999, 1e-08, 0.01, 10

VMEM_CAP_V7X = 64 * 1024 * 1024
NEG = -1e30


def _params(sem, nbytes):
    limit = int(min(VMEM_CAP_V7X - (8 << 20), max(32 << 20, 3 * nbytes)))
    return pltpu.CompilerParams(dimension_semantics=sem, vmem_limit_bytes=limit)


def _nbytes(shape, dtype):
    return math.prod(shape) * jnp.dtype(dtype).itemsize


_DIMS = {"nn": (((1,), (0,)), ((), ())), "nt": (((1,), (1,)), ((), ())), "tn": (((0,), (0,)), ((), ()))}


def _mm(a, b, mode, name, *, tm, tn, tk, out_dtype=F32, add=None, dims=None, a_spec=None, b_spec=None,
        o_spec=None, out_shape=None):
    if dims is None:
        if mode == "nn":
            (M, K), (_, N) = a.shape, b.shape
        elif mode == "nt":
            (M, K), (N, _) = a.shape, b.shape
        else:
            (K, M), (_, N) = a.shape, b.shape
    else:
        M, N, K = dims
    a_blk = (tk, tm) if mode == "tn" else (tm, tk)
    b_blk = (tn, tk) if mode == "nt" else (tk, tn)
    if a_spec is None:
        a_spec = pl.BlockSpec(a_blk, (lambda i, j, k: (k, i)) if mode == "tn" else (lambda i, j, k: (i, k)))
    if b_spec is None:
        b_spec = pl.BlockSpec(b_blk, (lambda i, j, k: (j, k)) if mode == "nt" else (lambda i, j, k: (k, j)))
    if o_spec is None:
        o_spec = pl.BlockSpec((tm, tn), lambda i, j, k: (i, j))
    if out_shape is None:
        out_shape = (M, N)
    assert M % tm == 0 and N % tn == 0 and K % tk == 0, (name, M, N, K, tm, tn, tk)
    nk = K // tk
    contract = _DIMS[mode]
    has_add = add is not None

    def body(*refs):
        if has_add:
            a_ref, b_ref, add_ref, o_ref, acc = refs
        else:
            a_ref, b_ref, o_ref, acc = refs
        k = pl.program_id(2)

        @pl.when(k == 0)
        def _():
            acc[...] = jnp.zeros_like(acc)

        acc[...] += lax.dot_general(a_ref[...].astype(MXU_DTYPE), b_ref[...].astype(MXU_DTYPE), contract,
                                    preferred_element_type=F32)

        @pl.when(k == nk - 1)
        def _():
            r = acc[...]
            if has_add:
                r = r + add_ref[...]
            o_ref[...] = r.astype(out_dtype)

    in_specs = [a_spec, b_spec]
    args = [a, b]
    nbytes = _nbytes(a_blk, a.dtype) + _nbytes(b_blk, b.dtype) + 3 * _nbytes((tm, tn), F32)
    if has_add:
        in_specs.append(pl.BlockSpec((tm, tn), lambda i, j, k: (i, j)))
        args.append(add)
        nbytes += _nbytes((tm, tn), F32)
    return pl.pallas_call(
        body, name=name, out_shape=jax.ShapeDtypeStruct(out_shape, out_dtype),
        grid=(M // tm, N // tn, nk), in_specs=in_specs, out_specs=o_spec,
        scratch_shapes=[pltpu.VMEM((tm, tn), F32)],
        compiler_params=_params(("parallel", "parallel", "arbitrary"), nbytes),
    )(*args)


_GELU_C = math.sqrt(2.0 / math.pi)
_GELU_A = 0.044715


def _sigmoid(x):
    return 1.0 / (1.0 + jnp.exp(-x))


def _gelu(x):
    t = jnp.tanh(_GELU_C * (x + _GELU_A * (x * x * x)))
    return x * (0.5 * (1.0 + t))


def _gelu_and_grad(x):
    x2 = x * x
    t = jnp.tanh(_GELU_C * (x + _GELU_A * (x2 * x)))
    cdf = 0.5 * (1.0 + t)
    grad = cdf + 0.5 * x * (1.0 - t * t) * (_GELU_C * (1.0 + 3.0 * _GELU_A * x2))
    return x * cdf, grad


def _rope_mix(g, cos_a, sin_a):
    return g * cos_a + pltpu.roll(g, 64, 1) * sin_a


def _rope_mix_bwd(d, cos_a, sin_a):
    return d * cos_a + pltpu.roll(d * sin_a, 64, 1)


def _rms_fwd(x, g, name, tr=512):
    T, D = x.shape

    def body(x_ref, g_ref, h_ref):
        xv = x_ref[...]
        r = lax.rsqrt(jnp.mean(xv * xv, axis=-1, keepdims=True) + EPS)
        h_ref[...] = ((xv * r) * g_ref[...]).astype(h_ref.dtype)

    return pl.pallas_call(
        body, name=name, out_shape=jax.ShapeDtypeStruct((T, D), MXU_DTYPE), grid=(T // tr,),
        in_specs=[pl.BlockSpec((tr, D), lambda i: (i, 0)), pl.BlockSpec((1, D), lambda i: (0, 0))],
        out_specs=pl.BlockSpec((tr, D), lambda i: (i, 0)),
        compiler_params=_params(("parallel",), 3 * _nbytes((tr, D), F32)),
    )(x, g)


def _rms_bwd(x, g, dh, dres, name, tr=512):
    T, D = x.shape

    def body(x_ref, g_ref, dh_ref, dres_ref, dx_ref, gg_ref):
        @pl.when(pl.program_id(0) == 0)
        def _():
            gg_ref[...] = jnp.zeros_like(gg_ref)

        xv = x_ref[...]
        r = lax.rsqrt(jnp.mean(xv * xv, axis=-1, keepdims=True) + EPS)
        xn = xv * r
        dhv = dh_ref[...]
        dxn = dhv * g_ref[...]
        dx_ref[...] = dres_ref[...] + r * (dxn - xn * jnp.mean(dxn * xn, axis=-1, keepdims=True))
        gg_ref[...] += jnp.sum(dhv * xn, axis=0, keepdims=True)

    row = pl.BlockSpec((tr, D), lambda i: (i, 0))
    vec = pl.BlockSpec((1, D), lambda i: (0, 0))
    return pl.pallas_call(
        body, name=name,
        out_shape=(jax.ShapeDtypeStruct((T, D), F32), jax.ShapeDtypeStruct((1, D), F32)),
        grid=(T // tr,), in_specs=[row, vec, row, row], out_specs=(row, vec),
        compiler_params=_params(("arbitrary",), 6 * _nbytes((tr, D), F32)),
    )(x, g, dh, dres)


def _lat_fwd(z, gq, gkv, wq, wkv, cos_a, sin_a, tr=256):
    T = z.shape[0]
    lat_blk = (4 * D_MODEL) // LAT

    def body(z_ref, gq_ref, gkv_ref, wq_ref, wkv_ref, cos_ref, sin_ref, q_ref, k_ref, v_ref, cqn_ref, ckvn_ref):
        zl = z_ref[...]
        cos_v, sin_v = cos_ref[...], sin_ref[...]
        cq = zl[:, :Q_RANK]
        ckv = zl[:, Q_RANK:Q_RANK + KV_RANK]
        krb = zl[:, Q_RANK + KV_RANK:]
        cqn = ((cq * lax.rsqrt(jnp.mean(cq * cq, axis=-1, keepdims=True) + EPS)) * gq_ref[...]).astype(MXU_DTYPE)
        ckvn = ((ckv * lax.rsqrt(jnp.mean(ckv * ckv, axis=-1, keepdims=True) + EPS)) * gkv_ref[...]).astype(MXU_DTYPE)
        cqn_ref[...] = cqn
        ckvn_ref[...] = ckvn
        krr = _rope_mix(krb, cos_v, sin_v).astype(MXU_DTYPE)
        q = jnp.dot(cqn, wq_ref[...], preferred_element_type=F32)
        kv = jnp.dot(ckvn, wkv_ref[...], preferred_element_type=F32)
        for h in range(HEADS):
            o = h * HEAD_PAD
            q_ref[:, o:o + NOPE] = q[:, o:o + NOPE].astype(MXU_DTYPE)
            q_ref[:, o + NOPE:o + HEAD_PAD] = _rope_mix(q[:, o + NOPE:o + HEAD_PAD], cos_v, sin_v).astype(MXU_DTYPE)
            k_ref[:, o:o + NOPE] = kv[:, h * NOPE:(h + 1) * NOPE].astype(MXU_DTYPE)
            k_ref[:, o + NOPE:o + HEAD_PAD] = krr
        v_ref[...] = kv[:, HEADS * NOPE:].astype(MXU_DTYPE)

    def row(w):
        return pl.BlockSpec((tr, w), lambda i: (i, 0))

    def full(a):
        return pl.BlockSpec(a.shape, lambda i: (0, 0))

    return pl.pallas_call(
        body, name="lat_fwd",
        out_shape=(jax.ShapeDtypeStruct((T, HEADS * HEAD_PAD), MXU_DTYPE), jax.ShapeDtypeStruct((T, HEADS * HEAD_PAD), MXU_DTYPE),
                   jax.ShapeDtypeStruct((T, HEADS * NOPE), MXU_DTYPE), jax.ShapeDtypeStruct((T, Q_RANK), MXU_DTYPE),
                   jax.ShapeDtypeStruct((T, KV_RANK), MXU_DTYPE)),
        grid=(T // tr,),
        in_specs=[pl.BlockSpec((tr, LAT), lambda i: (i, lat_blk)), full(gq), full(gkv), full(wq), full(wkv), row(128), row(128)],
        out_specs=(row(HEADS * HEAD_PAD), row(HEADS * HEAD_PAD), row(HEADS * NOPE), row(Q_RANK), row(KV_RANK)),
        compiler_params=_params(("parallel",), 8 * _nbytes((tr, HEADS * HEAD_PAD), F32)),
    )(z, gq, gkv, wq, wkv, cos_a, sin_a)


ATT_BLOCK = 256
_SCALE = QK_DIM ** -0.5


def _causal_mask(n):
    return lax.broadcasted_iota(jnp.int32, (n, n), 1) <= lax.broadcasted_iota(jnp.int32, (n, n), 0)


def _causal_mask_t(n):
    return lax.broadcasted_iota(jnp.int32, (n, n), 0) <= lax.broadcasted_iota(jnp.int32, (n, n), 1)


ATT_HEADS = 4


def _attn_fwd(q, k, v, B, S):
    tq = ATT_BLOCK
    nq = S // tq
    T = B * S
    hp, groups = ATT_HEADS, HEADS // ATT_HEADS

    def body(q_ref, k_ref, v_ref, o_ref, *lse_refs):
        qi = pl.program_id(2)
        qs = [q_ref[:, t * HEAD_PAD:(t + 1) * HEAD_PAD] for t in range(hp)]

        def scores(j, t):
            rows = pl.ds(pl.multiple_of(j * tq, tq), tq)
            return lax.dot_general(k_ref[rows, t * HEAD_PAD:(t + 1) * HEAD_PAD], qs[t], _DIMS["nt"],
                                   preferred_element_type=F32)

        def step(j, carry, last):
            rows = pl.ds(pl.multiple_of(j * tq, tq), tq)
            out = []
            for t in range(hp):
                m, l, acc, st = carry[t]
                st_next = st if last else scores(j + 1, t)
                st = st * _SCALE
                if last:
                    st = jnp.where(_causal_mask_t(tq), st, NEG)
                m_new = jnp.maximum(m, jnp.max(st, axis=0, keepdims=True))
                alpha = jnp.exp(m - m_new)
                p = jnp.exp(st - m_new)
                l = alpha * l + jnp.sum(p, axis=0, keepdims=True)
                acc = alpha * acc + lax.dot_general(v_ref[rows, t * NOPE:(t + 1) * NOPE], p.astype(MXU_DTYPE),
                                                    _DIMS["tn"], preferred_element_type=F32)
                out.append((m_new, l, acc, st_next))
            return tuple(out)

        init = tuple((jnp.full((1, tq), NEG, F32), jnp.zeros((1, tq), F32), jnp.zeros((NOPE, tq), F32), scores(0, t))
                     for t in range(hp))
        carry = lax.fori_loop(0, qi, lambda j, c: step(j, c, False), init)
        carry = step(qi, carry, True)
        for t in range(hp):
            m, l, acc, _ = carry[t]
            o_ref[:, t * NOPE:(t + 1) * NOPE] = (acc / l).T
            lse_refs[t][0] = m + jnp.log(l)

    lse_sds = jax.ShapeDtypeStruct((groups * B * nq, 1, tq), F32)
    lse_spec = pl.BlockSpec((1, 1, tq), lambda b, h, i: ((h * B + b) * nq + i, 0, 0))
    return pl.pallas_call(
        body, name="attn_fwd",
        out_shape=(jax.ShapeDtypeStruct((T, HEADS * NOPE), F32),) + (lse_sds,) * hp,
        grid=(B, groups, nq),
        in_specs=[pl.BlockSpec((tq, hp * HEAD_PAD), lambda b, h, i: (b * nq + i, h)),
                  pl.BlockSpec((S, hp * HEAD_PAD), lambda b, h, i: (b, h)),
                  pl.BlockSpec((S, hp * NOPE), lambda b, h, i: (b, h))],
        out_specs=(pl.BlockSpec((tq, hp * NOPE), lambda b, h, i: (b * nq + i, h)),) + (lse_spec,) * hp,
        compiler_params=_params(("parallel", "parallel", "arbitrary"), 4 * hp * _nbytes((S, HEAD_PAD), MXU_DTYPE)),
    )(q, k, v)


def _attn_bwd(q, k, v, do, lses, delta, B, S):
    tq = ATT_BLOCK
    nq = S // tq
    T = B * S
    hp, groups = ATT_HEADS, HEADS // ATT_HEADS

    def body(q_ref, k_ref, v_ref, do_ref, *refs):
        lse_refs, dl_refs = refs[:hp], refs[hp:2 * hp]
        dq_ref, dk_ref, dv_ref = refs[2 * hp:]
        kj = pl.program_id(2)

        @pl.when(kj == 0)
        def _():
            dq_ref[...] = jnp.zeros_like(dq_ref)

        def products(i, t):
            rows = pl.ds(pl.multiple_of(i * tq, tq), tq)
            st = lax.dot_general(k_ref[:, t * HEAD_PAD:(t + 1) * HEAD_PAD], q_ref[rows, t * HEAD_PAD:(t + 1) * HEAD_PAD],
                                 _DIMS["nt"], preferred_element_type=F32)
            dpt = lax.dot_general(v_ref[:, t * NOPE:(t + 1) * NOPE], do_ref[rows, t * NOPE:(t + 1) * NOPE],
                                  _DIMS["nt"], preferred_element_type=F32)
            return st, dpt

        def step(i, carry, masked):
            rows = pl.ds(pl.multiple_of(i * tq, tq), tq)
            nxt = jnp.minimum(i + 1, nq - 1)
            out = []
            for t in range(hp):
                dk, dv, st, dpt = carry[t]
                st_next, dpt_next = products(nxt, t)
                qk_cols = slice(t * HEAD_PAD, (t + 1) * HEAD_PAD)
                v_cols = slice(t * NOPE, (t + 1) * NOPE)
                p = jnp.exp(st * _SCALE - lse_refs[t][i])
                if masked:
                    p = jnp.where(_causal_mask_t(tq), p, 0.0)
                dv = dv + jnp.dot(p.astype(MXU_DTYPE), do_ref[rows, v_cols], preferred_element_type=F32)
                ds = (p * (dpt - dl_refs[t][i]) * _SCALE).astype(MXU_DTYPE)
                dk = dk + jnp.dot(ds, q_ref[rows, qk_cols], preferred_element_type=F32)
                dq_ref[rows, qk_cols] += lax.dot_general(ds, k_ref[:, qk_cols], _DIMS["tn"], preferred_element_type=F32)
                out.append((dk, dv, st_next, dpt_next))
            return tuple(out)

        init = tuple((jnp.zeros((tq, HEAD_PAD), F32), jnp.zeros((tq, NOPE), F32)) + products(kj, t) for t in range(hp))
        carry = step(kj, init, True)
        carry = lax.fori_loop(kj + 1, nq, lambda i, c: step(i, c, False), carry)
        for t in range(hp):
            dk_ref[:, t * HEAD_PAD:(t + 1) * HEAD_PAD] = carry[t][0]
            dv_ref[:, t * NOPE:(t + 1) * NOPE] = carry[t][1].astype(dv_ref.dtype)

    seq = lambda w: pl.BlockSpec((S, w), lambda b, h, j: (b, h))
    blk = lambda w: pl.BlockSpec((tq, w), lambda b, h, j: (b * nq + j, h))
    lse_spec = pl.BlockSpec((nq, 1, tq), lambda b, h, j: (h * B + b, 0, 0))
    dl_specs = [pl.BlockSpec((nq, 1, tq), lambda b, h, j, t=t: ((h * hp + t) * B + b, 0, 0)) for t in range(hp)]
    return pl.pallas_call(
        body, name="attn_bwd",
        out_shape=(jax.ShapeDtypeStruct((T, HEADS * HEAD_PAD), F32), jax.ShapeDtypeStruct((T, HEADS * HEAD_PAD), F32),
                   jax.ShapeDtypeStruct((T, HEADS * NOPE), MXU_DTYPE)),
        grid=(B, groups, nq),
        in_specs=[seq(hp * HEAD_PAD), blk(hp * HEAD_PAD), blk(hp * NOPE), seq(hp * NOPE)] + [lse_spec] * hp + dl_specs,
        out_specs=(seq(hp * HEAD_PAD), blk(hp * HEAD_PAD), blk(hp * NOPE)),
        compiler_params=_params(("parallel", "parallel", "arbitrary"), 8 * hp * _nbytes((S, HEAD_PAD), F32)),
    )(q, k, v, do, *lses, *([delta] * hp))


MIX_ROWS = 256


def _tril_weights(ws_ref, g):
    return jnp.where(_causal_mask(CHUNK), ws_ref[g], 0.0).astype(MXU_DTYPE)


def _layer_norm_stats(va):
    mu = jnp.mean(va, axis=-1, keepdims=True)
    xc = va - mu
    rs = lax.rsqrt(jnp.mean(xc * xc, axis=-1, keepdims=True) + EPS)
    return xc * rs


def _mix_specs(tr):
    zcol = lambda c: pl.BlockSpec((tr, D_MODEL), lambda i, c=c: (i, c))
    row = pl.BlockSpec((tr, D_MODEL), lambda i: (i, 0))
    vec = pl.BlockSpec((1, D_MODEL), lambda i: (0, 0))
    ws = pl.BlockSpec((A_GROUPS, CHUNK, CHUNK), lambda i: (0, 0, 0))
    bs = pl.BlockSpec((CHUNK, 128), lambda i: (0, 0))
    return zcol, row, vec, ws, bs


def _mix_fwd(z, yb, ln_g, ln_b, ws, bs_t):
    T = z.shape[0]
    tr = MIX_ROWS
    zcol, row, vec, ws_spec, bs_spec = _mix_specs(tr)

    def body(zu_ref, zv_ref, zga_ref, zgb_ref, yb_ref, g_ref, b_ref, ws_ref, bs_ref, out_ref, vn_s):
        vhat = _layer_norm_stats(_gelu(zv_ref[...]))
        vn_s[...] = (vhat * g_ref[...] + b_ref[...]).astype(MXU_DTYPE)
        for g in range(A_GROUPS):
            w = _tril_weights(ws_ref, g)
            bias = bs_ref[:, g:g + 1]
            cols = slice(g * CHUNK, (g + 1) * CHUNK)
            for c in range(tr // CHUNK):
                rows = slice(c * CHUNK, (c + 1) * CHUNK)
                mixed = jnp.dot(w, vn_s[rows, cols], preferred_element_type=F32) + bias
                ya = _gelu(zu_ref[rows, cols]) * mixed
                merged = _sigmoid(zga_ref[rows, cols]) * ya + _sigmoid(zgb_ref[rows, cols]) * yb_ref[rows, cols]
                out_ref[rows, cols] = merged.astype(MXU_DTYPE)

    return pl.pallas_call(
        body, name="mix_fwd", out_shape=jax.ShapeDtypeStruct((T, D_MODEL), MXU_DTYPE), grid=(T // tr,),
        in_specs=[zcol(0), zcol(1), zcol(2), zcol(3), row, vec, vec, ws_spec, bs_spec], out_specs=row,
        scratch_shapes=[pltpu.VMEM((tr, D_MODEL), MXU_DTYPE)],
        compiler_params=_params(("parallel",), 8 * _nbytes((tr, D_MODEL), F32)),
    )(z, z, z, z, yb, ln_g, ln_b, ws, bs_t)


def _mix_bwd(z, yb, dm, ln_g, ln_b, ws, bs_t):
    T = z.shape[0]
    tr = MIX_ROWS
    zcol, row, vec, ws_spec, bs_spec = _mix_specs(tr)

    def body(zu_ref, zv_ref, zga_ref, zgb_ref, yb_ref, dm_ref, g_ref, b_ref, ws_ref, bs_ref,
             dz_ref, dyb_ref, dl_ref, gws_ref, gbs_ref, glg_ref, glb_ref, vn_s, dvn_s):
        @pl.when(pl.program_id(0) == 0)
        def _():
            gws_ref[...] = jnp.zeros_like(gws_ref)
            gbs_ref[...] = jnp.zeros_like(gbs_ref)
            glg_ref[...] = jnp.zeros_like(glg_ref)
            glb_ref[...] = jnp.zeros_like(glb_ref)

        lane = lax.broadcasted_iota(jnp.int32, (CHUNK, 128), 1)
        va, dgelu_v = _gelu_and_grad(zv_ref[...])
        mu = jnp.mean(va, axis=-1, keepdims=True)
        xc = va - mu
        rs = lax.rsqrt(jnp.mean(xc * xc, axis=-1, keepdims=True) + EPS)
        vhat = xc * rs
        vn_s[...] = (vhat * g_ref[...] + b_ref[...]).astype(MXU_DTYPE)
        gbs_acc = jnp.zeros((CHUNK, 128), F32)
        for g in range(A_GROUPS):
            w = _tril_weights(ws_ref, g)
            bias = bs_ref[:, g:g + 1]
            cols = slice(g * CHUNK, (g + 1) * CHUNK)
            gw_acc = jnp.zeros((CHUNK, CHUNK), F32)
            for c in range(tr // CHUNK):
                rows = slice(c * CHUNK, (c + 1) * CHUNK)
                vn = vn_s[rows, cols]
                mixed = jnp.dot(w, vn, preferred_element_type=F32) + bias
                ua, dgelu_u = _gelu_and_grad(zu_ref[rows, cols])
                dmv = dm_ref[rows, cols]
                sa = _sigmoid(zga_ref[rows, cols])
                dya = dmv * sa
                dz_ref[rows, 2 * D_MODEL + g * CHUNK:2 * D_MODEL + (g + 1) * CHUNK] = (
                    dmv * (ua * mixed) * (sa * (1.0 - sa))).astype(dz_ref.dtype)
                dz_ref[rows, cols] = (dya * mixed * dgelu_u).astype(dz_ref.dtype)
                dmix = dya * ua
                gbs_acc = gbs_acc + jnp.where(lane == g, jnp.sum(dmix, axis=-1, keepdims=True), 0.0)
                dmix_b = dmix.astype(MXU_DTYPE)
                gw_acc = gw_acc + lax.dot_general(dmix_b, vn, _DIMS["nt"], preferred_element_type=F32)
                dvn_s[rows, cols] = lax.dot_general(w, dmix_b, _DIMS["tn"], preferred_element_type=F32)
            gws_ref[g] += jnp.where(_causal_mask(CHUNK), gw_acc, 0.0)
        gbs_ref[...] += gbs_acc

        dvn = dvn_s[...]
        glg_ref[...] += jnp.sum(dvn * vhat, axis=0, keepdims=True)
        glb_ref[...] += jnp.sum(dvn, axis=0, keepdims=True)
        dvh = dvn * g_ref[...]
        dva = rs * (dvh - jnp.mean(dvh, axis=-1, keepdims=True) - vhat * jnp.mean(dvh * vhat, axis=-1, keepdims=True))
        dz_ref[:, D_MODEL:2 * D_MODEL] = (dva * dgelu_v).astype(dz_ref.dtype)

        dmv = dm_ref[...]
        ybv = yb_ref[...]
        sb = _sigmoid(zgb_ref[...])
        dyb = dmv * sb
        dyb_ref[...] = dyb.astype(dyb_ref.dtype)
        dz_ref[:, 3 * D_MODEL:4 * D_MODEL] = (dmv * ybv * (sb * (1.0 - sb))).astype(dz_ref.dtype)
        dz_ref[:, 4 * D_MODEL:] = jnp.zeros((tr, LAT), dz_ref.dtype)
        prod = dyb * ybv
        sel = (lax.broadcasted_iota(jnp.int32, (HEADS, D_MODEL), 1) // NOPE
               == lax.broadcasted_iota(jnp.int32, (HEADS, D_MODEL), 0)).astype(jnp.bfloat16)
        hi = prod.astype(jnp.bfloat16)
        rest = prod - hi.astype(F32)
        mid = rest.astype(jnp.bfloat16)
        lo = (rest - mid.astype(F32)).astype(jnp.bfloat16)
        dl_ref[...] = (lax.dot_general(sel, hi, _DIMS["nt"], preferred_element_type=F32)
                       + lax.dot_general(sel, mid, _DIMS["nt"], preferred_element_type=F32)
                       + lax.dot_general(sel, lo, _DIMS["nt"], preferred_element_type=F32))

    return pl.pallas_call(
        body, name="mix_bwd",
        out_shape=(jax.ShapeDtypeStruct((T, IN_PAD), MXU_DTYPE), jax.ShapeDtypeStruct((T, D_MODEL), MXU_DTYPE),
                   jax.ShapeDtypeStruct((HEADS, T), F32), jax.ShapeDtypeStruct((A_GROUPS, CHUNK, CHUNK), F32),
                   jax.ShapeDtypeStruct((CHUNK, 128), F32), jax.ShapeDtypeStruct((1, D_MODEL), F32),
                   jax.ShapeDtypeStruct((1, D_MODEL), F32)),
        grid=(T // tr,),
        in_specs=[zcol(0), zcol(1), zcol(2), zcol(3), row, row, vec, vec, ws_spec, bs_spec],
        out_specs=(pl.BlockSpec((tr, IN_PAD), lambda i: (i, 0)), row, pl.BlockSpec((HEADS, tr), lambda i: (0, i)),
                   ws_spec, bs_spec, vec, vec),
        scratch_shapes=[pltpu.VMEM((tr, D_MODEL), MXU_DTYPE), pltpu.VMEM((tr, D_MODEL), F32)],
        compiler_params=_params(("arbitrary",), 12 * _nbytes((tr, D_MODEL), F32)),
    )(z, z, z, z, yb, dm, ln_g, ln_b, ws, bs_t)


def _lat_bwd(dz, z, dq, dk, dv, gq, gkv, wq, wkv, cos_a, sin_a, tr=256):
    T = z.shape[0]
    lat_blk = (4 * D_MODEL) // LAT

    def body(dz_in, z_ref, dq_ref, dk_ref, dv_ref, gq_ref, gkv_ref, wq_ref, wkv_ref, cos_ref, sin_ref,
             dz_ref, dqr_ref, dkv_ref, ggq_ref, ggkv_ref):
        del dz_in

        @pl.when(pl.program_id(0) == 0)
        def _():
            ggq_ref[...] = jnp.zeros_like(ggq_ref)
            ggkv_ref[...] = jnp.zeros_like(ggkv_ref)

        cos_v, sin_v = cos_ref[...], sin_ref[...]
        dkr = jnp.zeros((tr, 128), F32)
        for h in range(HEADS):
            o = h * HEAD_PAD
            dqr_ref[:, o:o + NOPE] = dq_ref[:, o:o + NOPE].astype(MXU_DTYPE)
            dqr_ref[:, o + NOPE:o + HEAD_PAD] = _rope_mix_bwd(dq_ref[:, o + NOPE:o + HEAD_PAD], cos_v, sin_v).astype(MXU_DTYPE)
            dkv_ref[:, h * NOPE:(h + 1) * NOPE] = dk_ref[:, o:o + NOPE].astype(MXU_DTYPE)
            dkr = dkr + _rope_mix_bwd(dk_ref[:, o + NOPE:o + HEAD_PAD], cos_v, sin_v)
        dkv_ref[:, HEADS * NOPE:] = dv_ref[...]
        dcqn = lax.dot_general(dqr_ref[...], wq_ref[...], _DIMS["nt"], preferred_element_type=F32)
        dckvn = lax.dot_general(dkv_ref[...], wkv_ref[...], _DIMS["nt"], preferred_element_type=F32)

        zl = z_ref[...]

        def rms_bwd(c, dn, g_ref, gg_ref):
            r = lax.rsqrt(jnp.mean(c * c, axis=-1, keepdims=True) + EPS)
            ch = c * r
            gg_ref[...] += jnp.sum(dn * ch, axis=0, keepdims=True)
            dch = dn * g_ref[...]
            return r * (dch - ch * jnp.mean(dch * ch, axis=-1, keepdims=True))

        dz_ref[:, :Q_RANK] = rms_bwd(zl[:, :Q_RANK], dcqn, gq_ref, ggq_ref).astype(dz_ref.dtype)
        dz_ref[:, Q_RANK:Q_RANK + KV_RANK] = rms_bwd(zl[:, Q_RANK:Q_RANK + KV_RANK], dckvn, gkv_ref, ggkv_ref).astype(dz_ref.dtype)
        dz_ref[:, Q_RANK + KV_RANK:] = dkr.astype(dz_ref.dtype)

    def row(w):
        return pl.BlockSpec((tr, w), lambda i: (i, 0))

    def full(a):
        return pl.BlockSpec(a.shape, lambda i: (0, 0))

    lat = pl.BlockSpec((tr, LAT), lambda i: (i, lat_blk))
    return pl.pallas_call(
        body, name="lat_bwd",
        out_shape=(jax.ShapeDtypeStruct(dz.shape, dz.dtype), jax.ShapeDtypeStruct((T, HEADS * HEAD_PAD), MXU_DTYPE),
                   jax.ShapeDtypeStruct((T, 2 * HEADS * NOPE), MXU_DTYPE), jax.ShapeDtypeStruct(gq.shape, F32),
                   jax.ShapeDtypeStruct(gkv.shape, F32)),
        grid=(T // tr,),
        in_specs=[pl.BlockSpec(memory_space=pl.ANY), lat, row(HEADS * HEAD_PAD), row(HEADS * HEAD_PAD), row(HEADS * NOPE),
                  full(gq), full(gkv), full(wq), full(wkv), row(128), row(128)],
        out_specs=(lat, row(HEADS * HEAD_PAD), row(2 * HEADS * NOPE), full(gq), full(gkv)),
        input_output_aliases={0: 0},
        compiler_params=_params(("arbitrary",), 8 * _nbytes((tr, HEADS * HEAD_PAD), F32)),
    )(dz, z, dq, dk, dv, gq, gkv, wq, wkv, cos_a, sin_a)


GATE_ROWS = 64
HALO = 8


def _taps(ref, half, r, first):
    C = GATE_ROWS
    if first:
        xs = jnp.concatenate([jnp.zeros((HALO, ref.shape[-1]), F32), ref[half, 0:C, :]], axis=0)
    else:
        xs = ref[half, pl.ds(pl.multiple_of(r * C - HALO, HALO), C + HALO), :]
    return xs[HALO:, :], pltpu.roll(xs, 1, 0)[HALO:, :], pltpu.roll(xs, 2, 0)[HALO:, :]


def _conv_taps(taps, cw, cb):
    x0, x1, x2 = taps
    return cb + cw[0:1, :] * x2 + cw[1:2, :] * x1 + cw[2:3, :] * x0


def _fold8(x):
    acc = x[0:8, :]
    for i in range(1, x.shape[0] // 8):
        acc = acc + x[8 * i:8 * (i + 1), :]
    return acc


def _gate_fwd(up3, conv_w, conv_b, B, S):
    T = B * S
    W = FF_TILE
    C = GATE_ROWS

    def body(up_ref, cw_ref, cb_ref, act_ref):
        def chunk(r, first):
            gate = _conv_taps(_taps(up_ref, 0, r, first), cw_ref[0], cb_ref[0])
            val = _conv_taps(_taps(up_ref, 1, r, first), cw_ref[1], cb_ref[1])
            base = 0 if first else pl.multiple_of(r * C, C)
            act_ref[pl.ds(base, C), :] = (gate * _sigmoid(gate) * val).astype(act_ref.dtype)

        chunk(0, True)

        @pl.loop(1, S // C)
        def _(r):
            chunk(r, False)

    return pl.pallas_call(
        body, name="gate_fwd", out_shape=jax.ShapeDtypeStruct((T, D_FF), MXU_DTYPE), grid=(B, N_FF_TILES),
        in_specs=[pl.BlockSpec((2, S, W), lambda b, j: (0, b, j)), pl.BlockSpec((2, 3, W), lambda b, j: (0, 0, j)),
                  pl.BlockSpec((2, 1, W), lambda b, j: (0, 0, j))],
        out_specs=pl.BlockSpec((S, W), lambda b, j: (b, j)),
        compiler_params=_params(("parallel", "parallel"), 6 * _nbytes((S, W), F32)),
    )(up3, conv_w, conv_b)


def _gate_bwd(up3, dact, conv_w, conv_b, B, S):
    T = B * S
    W = FF_TILE
    C = GATE_ROWS

    def body(up_ref, da_ref, cw_ref, cb_ref, dup_ref, gcw_ref, gcb_ref, d_s):
        @pl.when(pl.program_id(1) == 0)
        def _():
            gcw_ref[...] = jnp.zeros_like(gcw_ref)
            gcb_ref[...] = jnp.zeros_like(gcb_ref)

        def chunk(r, first, sums):
            rows = pl.ds(0 if first else pl.multiple_of(r * C, C), C)
            taps = [_taps(up_ref, half, r, first) for half in (0, 1)]
            gate = _conv_taps(taps[0], cw_ref[0], cb_ref[0])
            val = _conv_taps(taps[1], cw_ref[1], cb_ref[1])
            sg = _sigmoid(gate)
            da = da_ref[rows, :]
            d_halves = (da * val * (sg * (1.0 + gate * (1.0 - sg))), da * (gate * sg))
            out = []
            for half, dup in enumerate(d_halves):
                d_s[half, rows, :] = dup
                x0, x1, x2 = taps[half]
                sb, s0, s1, s2 = sums[half]
                out.append((sb + _fold8(dup), s0 + _fold8(dup * x2), s1 + _fold8(dup * x1), s2 + _fold8(dup * x0)))
            return tuple(out)

        zeros = tuple(tuple(jnp.zeros((8, W), F32) for _ in range(4)) for _ in range(2))
        sums = chunk(0, True, zeros)
        sums = lax.fori_loop(1, S // C, lambda r, s: chunk(r, False, s), sums)
        for half in (0, 1):
            sb, s0, s1, s2 = sums[half]
            gcb_ref[half] += jnp.sum(sb, axis=0, keepdims=True)
            gcw_ref[half, 0:1, :] += jnp.sum(s0, axis=0, keepdims=True)
            gcw_ref[half, 1:2, :] += jnp.sum(s1, axis=0, keepdims=True)
            gcw_ref[half, 2:3, :] += jnp.sum(s2, axis=0, keepdims=True)

        d_s[:, S:S + HALO, :] = jnp.zeros((2, HALO, W), F32)

        @pl.loop(0, S // C)
        def _(r):
            base = pl.multiple_of(r * C, C)
            for half in (0, 1):
                ds_ = d_s[half, pl.ds(base, C + HALO), :]
                cw = cw_ref[half]
                dx = (cw[2:3, :] * ds_[:C, :] + cw[1:2, :] * pltpu.roll(ds_, C + HALO - 1, 0)[:C, :]
                      + cw[0:1, :] * pltpu.roll(ds_, C + HALO - 2, 0)[:C, :])
                dup_ref[half, pl.ds(base, C), :] = dx.astype(dup_ref.dtype)

    up_spec = pl.BlockSpec((2, S, W), lambda j, b: (0, b, j))
    cw_spec = pl.BlockSpec((2, 3, W), lambda j, b: (0, 0, j))
    cb_spec = pl.BlockSpec((2, 1, W), lambda j, b: (0, 0, j))
    return pl.pallas_call(
        body, name="gate_bwd",
        out_shape=(jax.ShapeDtypeStruct((2, T, D_FF), MXU_DTYPE), jax.ShapeDtypeStruct((2, 3, D_FF), F32),
                   jax.ShapeDtypeStruct((2, 1, D_FF), F32)),
        grid=(N_FF_TILES, B),
        in_specs=[up_spec, pl.BlockSpec((S, W), lambda j, b: (b, j)), cw_spec, cb_spec],
        out_specs=(up_spec, cw_spec, cb_spec),
        scratch_shapes=[pltpu.VMEM((2, S + HALO, W), F32)],
        compiler_params=_params(("parallel", "arbitrary"), 10 * _nbytes((S, W), F32)),
    )(up3, dact, conv_w, conv_b)


def _final(x2, tgt, g, tr=512):
    T, D = x2.shape

    def body(x_ref, t_ref, g_ref, dx_ref, loss_ref, gg_ref):
        @pl.when(pl.program_id(0) == 0)
        def _():
            loss_ref[...] = jnp.zeros_like(loss_ref)
            gg_ref[...] = jnp.zeros_like(gg_ref)

        xv = x_ref[...]
        gv = g_ref[...]
        r = lax.rsqrt(jnp.mean(xv * xv, axis=-1, keepdims=True) + EPS)
        xn = xv * r
        err = xn * gv - t_ref[...]
        loss_ref[...] += 0.5 * jnp.sum(jnp.mean(err * err, axis=-1, keepdims=True), axis=0, keepdims=True)
        dy = err * (1.0 / D)
        gg_ref[...] += jnp.sum(dy * xn, axis=0, keepdims=True)
        dxn = dy * gv
        dx_ref[...] = r * (dxn - xn * jnp.mean(dxn * xn, axis=-1, keepdims=True))

    row = pl.BlockSpec((tr, D), lambda i: (i, 0))
    vec = pl.BlockSpec((1, D), lambda i: (0, 0))
    return pl.pallas_call(
        body, name="final_loss",
        out_shape=(jax.ShapeDtypeStruct((T, D), F32), jax.ShapeDtypeStruct((1, 128), F32), jax.ShapeDtypeStruct((1, D), F32)),
        grid=(T // tr,), in_specs=[row, row, vec],
        out_specs=(row, pl.BlockSpec((1, 128), lambda i: (0, 0)), vec),
        compiler_params=_params(("arbitrary",), 6 * _nbytes((tr, D), F32)),
    )(x2, tgt, g)


def _sum_slabs(parts, name, tr):
    rows, cols = parts[0].shape
    n = len(parts)

    def body(*refs):
        acc = refs[0][...]
        for r in refs[1:n]:
            acc = acc + r[...]
        refs[n][...] = acc

    blk = pl.BlockSpec((tr, cols), lambda i: (i, 0))
    return pl.pallas_call(
        body, name=name, out_shape=jax.ShapeDtypeStruct((rows, cols), F32), grid=(rows // tr,),
        in_specs=[blk] * n, out_specs=blk,
        compiler_params=_params(("parallel",), (n + 1) * _nbytes((tr, cols), F32)),
    )(*parts)


ADAMW_BLOCK_BYTES = 2400 * 1024


def _adamw(w, g, m, v, name):
    lead = w.ndim == 3
    rows, cols = w.shape[-2:]
    fits = [d for d in range(8, rows + 1, 8) if rows % d == 0 and d * cols * 4 <= ADAMW_BLOCK_BYTES]
    tr = max(fits) if fits else rows
    c1 = 1.0 - ADAM_B1 ** ADAM_STEP
    c2 = 1.0 - ADAM_B2 ** ADAM_STEP

    def body(w_ref, g_ref, m_ref, v_ref, d_ref, nm_ref, nv_ref):
        gv = g_ref[...]
        nm = ADAM_B1 * m_ref[...] + (1.0 - ADAM_B1) * gv
        nv = ADAM_B2 * v_ref[...] + (1.0 - ADAM_B2) * (gv * gv)
        nm_ref[...] = nm
        nv_ref[...] = nv
        d_ref[...] = -ADAM_LR * ((nm / c1) / (jnp.sqrt(nv / c2) + ADAM_EPS) + ADAM_WD * w_ref[...])

    blk = pl.BlockSpec((None, tr, cols), lambda i: (0, i, 0)) if lead else pl.BlockSpec((tr, cols), lambda i: (i, 0))
    sds = jax.ShapeDtypeStruct(w.shape, F32)
    return pl.pallas_call(
        body, name=name, out_shape=(sds, sds, sds), grid=(rows // tr,), in_specs=[blk] * 4, out_specs=(blk, blk, blk),
        compiler_params=_params(("parallel",), 7 * _nbytes((tr, cols), F32)),
    )(w, g, m, v)


_ANY = pl.BlockSpec(memory_space=pl.ANY)


def _place():
    x, y, c = lax.axis_index("x"), lax.axis_index("y"), lax.axis_index("c")
    chips = [(1 - x, y), (x, 1 - y), (1 - x, 1 - y)]
    return x, y, c, chips


def _gather_weights(shards):
    n = len(shards)

    def body(*refs):
        ins, outs = refs[:n], refs[n:2 * n]
        send, recv, fsend, frecv, osend, orecv = refs[2 * n:]
        x, y, c, chips = _place()
        me = 2 * x + y
        first, passed = [], []
        for w in range(n):
            first.append(pltpu.make_async_remote_copy(
                src_ref=ins[w], dst_ref=outs[w].at[me], send_sem=osend.at[w], recv_sem=orecv.at[w],
                device_id=(x, y, 1 - c), device_id_type=MESH))
        for w in range(n):
            for j, (px, py) in enumerate(chips):
                first.append(pltpu.make_async_remote_copy(
                    src_ref=ins[w].at[c], dst_ref=outs[w].at[me, c], send_sem=send.at[3 * w + j],
                    recv_sem=recv.at[3 * w + j], device_id=(px, py, c), device_id_type=MESH))
        for cp in first:
            cp.start()
        for w in range(n):
            for j, (px, py) in enumerate(chips):
                landed = outs[w].at[2 * px + py, c]
                pltpu.make_async_remote_copy(src_ref=landed, dst_ref=landed, send_sem=send.at[3 * w + j],
                                             recv_sem=recv.at[3 * w + j], device_id=(px, py, c),
                                             device_id_type=MESH).wait_recv()
                fw = pltpu.make_async_remote_copy(src_ref=landed, dst_ref=landed, send_sem=fsend.at[3 * w + j],
                                                  recv_sem=frecv.at[3 * w + j], device_id=(x, y, 1 - c),
                                                  device_id_type=MESH)
                fw.start()
                passed.append(fw)
        for w in range(n):
            for j, (px, py) in enumerate(chips):
                other = outs[w].at[2 * px + py, 1 - c]
                pltpu.make_async_remote_copy(src_ref=other, dst_ref=other, send_sem=fsend.at[3 * w + j],
                                             recv_sem=frecv.at[3 * w + j], device_id=(x, y, 1 - c),
                                             device_id_type=MESH).wait_recv()
        for w in range(n):
            own = outs[w].at[me]
            pltpu.make_async_remote_copy(src_ref=own, dst_ref=own, send_sem=osend.at[w], recv_sem=orecv.at[w],
                                         device_id=(x, y, 1 - c), device_id_type=MESH).wait_recv()
        for cp in first + passed:
            cp.wait_send()

    dma = lambda k: pltpu.SemaphoreType.DMA((k,))
    return pl.pallas_call(
        body, name="gather_weights",
        out_shape=tuple(jax.ShapeDtypeStruct((N_CHIPS,) + s.shape, s.dtype) for s in shards),
        in_specs=[_ANY] * n, out_specs=tuple([_ANY] * n),
        scratch_shapes=[dma(3 * n), dma(3 * n), dma(3 * n), dma(3 * n), dma(n), dma(n)],
    )(*shards)


_HBM = pl.BlockSpec(memory_space=pltpu.HBM)
_SEM = pl.BlockSpec(memory_space=pltpu.SEMAPHORE)
_EFFECT = pltpu.SideEffectType.DATAFLOW_SIDE_EFFECTING


SEMS_PER_ARRAY = 8


def _exchange_copies(srcs, lands, send, recv, mode):
    x, y, c, chips = _place()
    if mode == "all":
        flips = [(fx, fy, fc) for fx in (0, 1) for fy in (0, 1) for fc in (0, 1)][1:]
        peers = [(x ^ fx, y ^ fy, c ^ fc) for fx, fy, fc in flips]
        slot = 4 * x + 2 * y + c
    else:
        peers = [(px, py, c) for px, py in chips] + ([(x, y, 1 - c)] if mode == "gather" else [])
        slot = 2 * x + y
    cps = []
    for w, (src, land) in enumerate(zip(srcs, lands)):
        for k, peer in enumerate(peers):
            piece = src.at[2 * peer[0] + peer[1]] if mode == "scatter" else src
            cps.append(pltpu.make_async_remote_copy(
                src_ref=piece, dst_ref=land.at[slot], send_sem=send.at[SEMS_PER_ARRAY * w + k],
                recv_sem=recv.at[SEMS_PER_ARRAY * w + k], device_id=peer, device_id_type=MESH))
    return cps


def _exchange_start(srcs, name, mode, after):
    n = len(srcs)
    lead = {"gather": (N_CHIPS,), "scatter": (), "all": (2 * N_CHIPS,)}[mode]
    land_shapes = [lead + s.shape for s in srcs]

    def body(*refs):
        src_refs, land_refs = refs[:n], refs[n:2 * n]
        send, recv = refs[2 * n + 1], refs[2 * n + 2]
        token = refs[-1]
        for cp in _exchange_copies(src_refs, land_refs, send, recv, mode):
            cp.start()
        token[...] = jnp.zeros_like(token)

    sems = pltpu.SemaphoreType.DMA((SEMS_PER_ARRAY * n,))
    out = pl.pallas_call(
        body, name=name,
        out_shape=(sems, sems, *[pltpu.HBM(s.shape, s.dtype) for s in srcs],
                   *[pltpu.HBM(shp, s.dtype) for shp, s in zip(land_shapes, srcs)], jax.ShapeDtypeStruct((8, 128), F32)),
        in_specs=[_HBM] * (2 * n) + [_ANY],
        out_specs=(_SEM, _SEM, *[_HBM] * (2 * n), pl.BlockSpec(memory_space=pltpu.VMEM)),
        input_output_aliases={i: 2 + i for i in range(2 * n)},
        compiler_params=pltpu.CompilerParams(has_side_effects=_EFFECT),
    )(*[pltpu.with_memory_space_constraint(s, pltpu.HBM) for s in srcs],
      *[pltpu.with_memory_space_constraint(lax.empty(shp, s.dtype), pltpu.HBM) for shp, s in zip(land_shapes, srcs)],
      after)
    return out[0], out[1], out[2:2 + n], out[2 + n:2 + 2 * n], out[-1]


def _exchange_wait(started, name, mode, after):
    send, recv, src_thru, land_thru, _ = started
    n = len(src_thru)

    def body(*refs):
        src_refs, land_refs, send_ref, recv_ref = refs[:n], refs[n:2 * n], refs[2 * n], refs[2 * n + 1]
        for cp in _exchange_copies(src_refs, land_refs, send_ref, recv_ref, mode):
            cp.wait_send()
            cp.wait_recv()

    out = pl.pallas_call(
        body, name=name,
        out_shape=tuple(pltpu.HBM(a.shape, a.dtype) for a in list(src_thru) + list(land_thru)),
        in_specs=[_HBM] * (2 * n) + [_SEM, _SEM, _ANY], out_specs=tuple([_HBM] * (2 * n)),
        input_output_aliases={i: i for i in range(2 * n)},
        compiler_params=pltpu.CompilerParams(has_side_effects=_EFFECT),
    )(*src_thru, *land_thru, send, recv, after)
    return out[:n], out[n:]


def _swap_halves(gs, name):
    n = len(gs)

    def body(*refs):
        ins, outs, send, recv = refs[:n], refs[n:2 * n], refs[2 * n], refs[2 * n + 1]
        x, y, c, _ = _place()
        cps = []
        for w in range(n):
            cps.append(pltpu.make_async_remote_copy(
                src_ref=ins[w].at[:, 1 - c], dst_ref=outs[w], send_sem=send.at[w], recv_sem=recv.at[w],
                device_id=(x, y, 1 - c), device_id_type=MESH))
        for cp in cps:
            cp.start()
        for cp in cps:
            cp.wait()

    return pl.pallas_call(
        body, name=name,
        out_shape=tuple(jax.ShapeDtypeStruct((g.shape[0],) + g.shape[2:], g.dtype) for g in gs),
        in_specs=[_ANY] * n, out_specs=tuple([_ANY] * n),
        scratch_shapes=[pltpu.SemaphoreType.DMA((n,)), pltpu.SemaphoreType.DMA((n,))],
    )(*gs)


GRAD_PAYLOAD = jnp.bfloat16


def _pair_sum(gs, gots, name):
    n = len(gs)
    core = lax.axis_index("c").astype(jnp.int32).reshape(1)

    def body(core_ref, *refs):
        del core_ref
        for w in range(n):
            refs[2 * n + w][...] = (refs[w][...] + refs[n + w][...]).astype(GRAD_PAYLOAD)

    in_specs, out_specs, out_shape, nbytes = [], [], [], 0
    for g in gs:
        q = g.shape[1] // 4
        in_specs.append(pl.BlockSpec((1, q, g.shape[2]), lambda s, r, core: (s, 2 * core[0] + r, 0)))
        nbytes += 3 * _nbytes((q, g.shape[2]), F32)
    for g in gs:
        q = g.shape[1] // 4
        in_specs.append(pl.BlockSpec((1, q, g.shape[2]), lambda s, r, core: (s, r, 0)))
        out_specs.append(pl.BlockSpec((1, q, g.shape[2]), lambda s, r, core: (s, r, 0)))
        out_shape.append(jax.ShapeDtypeStruct((g.shape[0], g.shape[1] // 2, g.shape[2]), GRAD_PAYLOAD))
    return pl.pallas_call(
        body, name=name, out_shape=tuple(out_shape),
        grid_spec=pltpu.PrefetchScalarGridSpec(num_scalar_prefetch=1, grid=(N_CHIPS, 2), in_specs=in_specs,
                                               out_specs=tuple(out_specs)),
        compiler_params=_params(("parallel", "parallel"), nbytes),
    )(core, *gs, *gots)


def _chip_sum(ps, landed):
    n = len(ps)
    x, y, c = lax.axis_index("x"), lax.axis_index("y"), lax.axis_index("c")
    where = jnp.stack([2 * x + y, 2 * (1 - x) + y, 2 * x + (1 - y), 2 * (1 - x) + (1 - y), c]).astype(jnp.int32)

    def body(where_ref, *refs):
        del where_ref
        for w in range(n):
            terms = [refs[4 * w + t][...].astype(F32) for t in range(4)]
            refs[4 * n + w][...] = ((terms[0] + terms[1]) + terms[2]) + terms[3]

    in_specs, out_specs, out_shape, args, nbytes = [], [], [], [], 0
    for p, a in zip(ps, landed):
        q = a.shape[1] // 2
        blk = (1, q, a.shape[2])
        in_specs.append(pl.BlockSpec(blk, lambda r, where: (where[0], r, 0)))
        args.append(p)
        for t in (1, 2, 3):
            in_specs.append(pl.BlockSpec(blk, lambda r, where, t=t: (where[t], r, 0)))
            args.append(a)
        out_specs.append(pl.BlockSpec(blk, lambda r, where: (where[4], r, 0)))
        out_shape.append(jax.ShapeDtypeStruct((2,) + a.shape[1:], F32))
        nbytes += 4 * _nbytes(blk, F32)
    return pl.pallas_call(
        body, name="grad_chip_sum", out_shape=tuple(out_shape),
        grid_spec=pltpu.PrefetchScalarGridSpec(num_scalar_prefetch=1, grid=(2,), in_specs=in_specs,
                                               out_specs=tuple(out_specs)),
        compiler_params=_params(("parallel",), nbytes),
    )(where, *args)


def _join_halves(ss):
    n = len(ss)

    def body(*refs):
        outs, send, recv = refs[n:2 * n], refs[2 * n], refs[2 * n + 1]
        x, y, c, _ = _place()
        cps = []
        for w in range(n):
            cps.append(pltpu.make_async_remote_copy(
                src_ref=outs[w].at[c], dst_ref=outs[w].at[c], send_sem=send.at[w], recv_sem=recv.at[w],
                device_id=(x, y, 1 - c), device_id_type=MESH))
        for cp in cps:
            cp.start()
        for w in range(n):
            got = outs[w].at[1 - c]
            pltpu.make_async_remote_copy(src_ref=got, dst_ref=got, send_sem=send.at[w], recv_sem=recv.at[w],
                                         device_id=(x, y, 1 - c), device_id_type=MESH).wait_recv()
        for cp in cps:
            cp.wait_send()

    dma = lambda k: pltpu.SemaphoreType.DMA((k,))
    return pl.pallas_call(
        body, name="grad_join_halves",
        out_shape=tuple(jax.ShapeDtypeStruct(s.shape, s.dtype) for s in ss),
        in_specs=[_ANY] * n, out_specs=tuple([_ANY] * n), input_output_aliases={w: w for w in range(n)},
        scratch_shapes=[dma(n), dma(n)],
    )(*ss)


def _rot_cols(w):
    a, b = jnp.split(w, 2, axis=-1)
    return jnp.concatenate([-b, a], axis=-1)


def _rot_cols_t(g):
    a, b = jnp.split(g, 2, axis=-1)
    return jnp.concatenate([b, -a], axis=-1)


def _cols_from_chips(a):
    n, r, cs = a.shape
    return jnp.transpose(a, (1, 0, 2)).reshape(r, n * cs)


def _cols_to_chips(a):
    r, cc = a.shape
    return jnp.transpose(a.reshape(r, N_CHIPS, cc // N_CHIPS), (1, 0, 2))


def _conv_w_split(cw):
    return jnp.swapaxes(cw.reshape(3, 2, D_FF), 0, 1)


def _conv_w_join(g):
    return jnp.swapaxes(g, 0, 1).reshape(3, 2 * D_FF)


_SEG =(D_MODEL, 2 * D_MODEL, 2 * D_MODEL + Q_RANK, 2 * D_MODEL + Q_RANK + KV_RANK, 2 * D_MODEL + Q_RANK + KV_RANK + ROPE,
        3 * D_MODEL + Q_RANK + KV_RANK + ROPE)


def _w_in_to_pad(w):
    u, v, cq, ckv, kr, ga, gb = jnp.split(w, _SEG, axis=1)
    return jnp.concatenate([u, v, ga, gb, cq, ckv, kr, _rot_cols(kr)], axis=1)


def _w_in_from_pad(g):
    u, v, ga, gb, cq, ckv, kr, krr = jnp.split(
        g, (D_MODEL, 2 * D_MODEL, 3 * D_MODEL, 4 * D_MODEL, 4 * D_MODEL + Q_RANK, 4 * D_MODEL + Q_RANK + KV_RANK,
            4 * D_MODEL + Q_RANK + KV_RANK + ROPE), axis=1)
    return jnp.concatenate([u, v, cq, ckv, kr + _rot_cols_t(krr), ga, gb], axis=1)


def _w_uq_to_pad(w):
    t = w.reshape(Q_RANK, HEADS, QK_DIM)
    nope, rope = t[..., :NOPE], t[..., NOPE:]
    return jnp.concatenate([nope, rope, _rot_cols(rope)], axis=-1).reshape(Q_RANK, HEADS * HEAD_PAD)


def _w_uq_from_pad(g):
    t = g.reshape(Q_RANK, HEADS, HEAD_PAD)
    nope, rope, rot = t[..., :NOPE], t[..., NOPE:QK_DIM], t[..., QK_DIM:]
    return jnp.concatenate([nope, rope + _rot_cols_t(rot)], axis=-1).reshape(Q_RANK, HEADS * QK_DIM)


def _w_ukv_to_pad(w):
    t = w.reshape(KV_RANK, HEADS, 2, NOPE)
    return jnp.swapaxes(t, 1, 2).reshape(KV_RANK, 2 * HEADS * NOPE)


def _w_ukv_from_pad(g):
    t = g.reshape(KV_RANK, 2, HEADS, NOPE)
    return jnp.swapaxes(t, 1, 2).reshape(KV_RANK, 2 * HEADS * NOPE)


def _rope_tables(positions):
    inv_freq = 1.0 / (ROPE_THETA ** (jnp.arange(0, ROPE, 2, dtype=F32) / ROPE))
    ang = positions.astype(F32).reshape(-1, 1) * inv_freq
    cos, sin = jnp.cos(ang), jnp.sin(ang)
    zero = jnp.zeros((ang.shape[0], 64), F32)
    return jnp.concatenate([cos, cos, zero], axis=1), jnp.concatenate([sin, sin, zero], axis=1)


_BIG = ("w_in", "w_uq", "w_ukv", "w_out", "w_up", "w_down")
UP_SHARD = 2 * D_FF // N_CHIPS


def _local_step(x, positions, tgt, wts, ffn_weights, on_ffn_grads, on_mixer_grads):
    B, S, D = x.shape
    T = B * S
    xf = x.reshape(T, D)
    cos_a, sin_a = _rope_tables(positions)
    bs_t = jnp.pad(wts["a_spatial_b"].T, ((0, 0), (0, 128 - A_GROUPS)))

    h = _rms_fwd(xf, wts["mix_norm"], "norm1_fwd")
    z = _mm(h, wts["w_in"], "nn", "in_proj", tm=512, tn=1536, tk=D)
    q, k, v, cqn, ckvn = _lat_fwd(z, wts["q_a_norm"], wts["kv_a_norm"], wts["w_q"], wts["w_kv"], cos_a, sin_a)
    yb, *lses = _attn_fwd(q, k, v, B, S)
    merged = _mix_fwd(z, yb, wts["a_v_norm_g"], wts["a_v_norm_b"], wts["a_spatial_w"], bs_t)
    x1 = _mm(merged, wts["w_out"], "nn", "out_proj", tm=512, tn=D, tk=D, add=xf)
    h2 = _rms_fwd(x1, wts["ffn_norm"], "norm2_fwd")
    wts = dict(wts)
    wts["w_up"], wts["w_down"], wts["conv_w"] = ffn_weights(h2)
    up_pre = _mm(h2, wts["w_up"], "nn", "up_proj", tm=512, tn=UP_SHARD, tk=D, dims=(T, 2 * D_FF, D),
                 b_spec=pl.BlockSpec((None, D, UP_SHARD), lambda i, j, k: (j, 0, 0)),
                 o_spec=pl.BlockSpec((None, 512, UP_SHARD), lambda i, j, k: (j // 2, i, j % 2)), out_shape=(2, T, D_FF))
    act = _gate_fwd(up_pre, wts["conv_w"], wts["conv_b"], B, S)
    x2 = _mm(act, wts["w_down"], "nn", "down_proj", tm=512, tn=D, tk=1408, add=x1)
    dx2, loss_row, g_final = _final(x2, tgt.reshape(T, D), wts["final_norm"])

    g = {"final_norm": g_final}
    dact = _mm(dx2, wts["w_down"], "nt", "down_proj_dx", tm=512, tn=1408, tk=D)
    g["w_down"] = _mm(act, dx2, "tn", "down_proj_dw", tm=1408, tn=D, tk=512)
    dup, g["conv_w"], g["conv_b"] = _gate_bwd(up_pre, dact, wts["conv_w"], wts["conv_b"], B, S)
    dh2 = _mm(dup, wts["w_up"], "nt", "up_proj_dx", tm=512, tn=D, tk=UP_SHARD, dims=(T, D, 2 * D_FF),
              a_spec=pl.BlockSpec((None, 512, UP_SHARD), lambda i, j, k: (k // 2, i, k % 2)),
              b_spec=pl.BlockSpec((None, D, UP_SHARD), lambda i, j, k: (k, 0, 0)))
    g["w_up"] = _mm(h2, dup, "tn", "up_proj_dw", tm=D, tn=UP_SHARD, tk=512, dims=(D, 2 * D_FF, T),
                    b_spec=pl.BlockSpec((None, 512, UP_SHARD), lambda i, j, k: (j // 2, k, j % 2)),
                    o_spec=pl.BlockSpec((None, D, UP_SHARD), lambda i, j, k: (j, 0, 0)), out_shape=(N_CHIPS, D, UP_SHARD))
    token = on_ffn_grads(g["w_up"], g["w_down"])
    ffn_norm = wts["ffn_norm"] if token is None else wts["ffn_norm"] + token[0:1, 0:1]
    dx1, g["ffn_norm"] = _rms_bwd(x1, ffn_norm, dh2, dx2, "norm2_bwd")
    dm = _mm(dx1, wts["w_out"], "nt", "out_proj_dx", tm=512, tn=D, tk=D)
    g["w_out"] = _mm(merged, dx1, "tn", "out_proj_dw", tm=D, tn=D, tk=512)
    dz, dyb, dl, g["a_spatial_w"], gbs, g["a_v_norm_g"], g["a_v_norm_b"] = _mix_bwd(
        z, yb, dm, wts["a_v_norm_g"], wts["a_v_norm_b"], wts["a_spatial_w"], bs_t)
    g["a_spatial_b"] = gbs[:, :A_GROUPS].T
    delta = dl.reshape(HEADS * T // ATT_BLOCK, 1, ATT_BLOCK)
    dq, dk, dv = _attn_bwd(q, k, v, dyb, lses, delta, B, S)
    dz, dq_raw, dkv, g["q_a_norm"], g["kv_a_norm"] = _lat_bwd(
        dz, z, dq, dk, dv, wts["q_a_norm"], wts["kv_a_norm"], wts["w_q"], wts["w_kv"], cos_a, sin_a)
    g["w_q"] = _mm(cqn, dq_raw, "tn", "q_proj_dw", tm=Q_RANK, tn=HEADS * HEAD_PAD, tk=512)
    g["w_kv"] = _mm(ckvn, dkv, "tn", "kv_proj_dw", tm=KV_RANK, tn=2 * HEADS * NOPE, tk=512)
    g["w_in"] = _mm(h, dz, "tn", "in_proj_dw", tm=D, tn=1536, tk=512)
    token = on_mixer_grads(g)
    mix_norm = wts["mix_norm"] if token is None else wts["mix_norm"] + token[0:1, 0:1]
    dh = _mm(dz, wts["w_in"], "nt", "in_proj_dx", tm=512, tn=D, tk=1536)
    dx, g["mix_norm"] = _rms_bwd(xf, mix_norm, dh, dx1, "norm1_bwd")
    return loss_row[0, 0], dx.reshape(B, S, D), g


_SMALL = (("mix_norm", (1, D_MODEL)), ("a_v_norm_g", (1, D_MODEL)), ("a_v_norm_b", (1, D_MODEL)),
          ("a_spatial_w", (A_GROUPS * CHUNK, CHUNK)), ("a_spatial_b", (1, A_GROUPS * CHUNK)), ("q_a_norm", (1, Q_RANK)),
          ("kv_a_norm", (1, KV_RANK)), ("ffn_norm", (1, D_MODEL)), ("conv_b", (1, 2 * D_FF)), ("final_norm", (1, D_MODEL)),
          ("conv_w", (3, 2 * D_FF)))
_SMALL_ROWS = -(-sum(math.prod(s) for _, s in _SMALL) // (128 * 8)) * 8


def kernel(x, positions, mix_norm, w_in, a_v_norm_g, a_v_norm_b, a_spatial_w, a_spatial_b, q_a_norm, w_uq, kv_a_norm, w_ukv, w_out, ffn_norm, w_up, conv_w, conv_b, w_down, final_norm, loss_target, m_mix_norm, m_w_in, m_a_v_norm_g, m_a_v_norm_b, m_a_spatial_w, m_a_spatial_b, m_q_a_norm, m_w_uq, m_kv_a_norm, m_w_ukv, m_w_out, m_ffn_norm, m_w_up, m_conv_w, m_conv_b, m_w_down, m_final_norm, v_mix_norm, v_w_in, v_a_v_norm_g, v_a_v_norm_b, v_a_spatial_w, v_a_spatial_b, v_q_a_norm, v_w_uq, v_kv_a_norm, v_w_ukv, v_w_out, v_ffn_norm, v_w_up, v_conv_w, v_conv_b, v_w_down, v_final_norm):
    weights = dict(mix_norm=mix_norm, w_in=w_in, a_v_norm_g=a_v_norm_g, a_v_norm_b=a_v_norm_b, a_spatial_w=a_spatial_w,
                   a_spatial_b=a_spatial_b, q_a_norm=q_a_norm, w_uq=w_uq, kv_a_norm=kv_a_norm, w_ukv=w_ukv, w_out=w_out,
                   ffn_norm=ffn_norm, w_up=w_up, conv_w=conv_w, conv_b=conv_b, w_down=w_down, final_norm=final_norm)
    m_in = dict(mix_norm=m_mix_norm, w_in=m_w_in, a_v_norm_g=m_a_v_norm_g, a_v_norm_b=m_a_v_norm_b,
                a_spatial_w=m_a_spatial_w, a_spatial_b=m_a_spatial_b, q_a_norm=m_q_a_norm, w_uq=m_w_uq,
                kv_a_norm=m_kv_a_norm, w_ukv=m_w_ukv, w_out=m_w_out, ffn_norm=m_ffn_norm, w_up=m_w_up, conv_w=m_conv_w,
                conv_b=m_conv_b, w_down=m_w_down, final_norm=m_final_norm)
    v_in = dict(mix_norm=v_mix_norm, w_in=v_w_in, a_v_norm_g=v_a_v_norm_g, a_v_norm_b=v_a_v_norm_b,
                a_spatial_w=v_a_spatial_w, a_spatial_b=v_a_spatial_b, q_a_norm=v_q_a_norm, w_uq=v_w_uq,
                kv_a_norm=v_kv_a_norm, w_ukv=v_w_ukv, w_out=v_w_out, ffn_norm=v_ffn_norm, w_up=v_w_up, conv_w=v_conv_w,
                conv_b=v_conv_b, w_down=v_w_down, final_norm=v_final_norm)
    names = list(weights)
    chip = 2 * lax.axis_index("x") + lax.axis_index("y")

    def halves(a):
        return a.reshape(a.shape[:-2] + (2, a.shape[-2] // 2, a.shape[-1]))

    def whole(a):
        return a.reshape(a.shape[:-3] + (2 * a.shape[-2], a.shape[-1]))

    gathered = _gather_weights([halves(weights[n][0].astype(MXU_DTYPE)) for n in _BIG[:4]])
    w_in_sh, w_uq_sh, w_ukv_sh, w_out_sh = (whole(a) for a in gathered)
    ffn_gather = _exchange_start([w_up[0].astype(MXU_DTYPE), w_down[0].astype(MXU_DTYPE), conv_w[0]],
                                 "ffn_gather_start", "gather", after=gathered[3])
    wts = dict(
        mix_norm=mix_norm + ffn_gather[4][0:1, 0:1], a_v_norm_g=a_v_norm_g, a_v_norm_b=a_v_norm_b,
        a_spatial_w=a_spatial_w[0], a_spatial_b=a_spatial_b[0], q_a_norm=q_a_norm, kv_a_norm=kv_a_norm,
        ffn_norm=ffn_norm, final_norm=final_norm.reshape(1, D_MODEL),
        w_in=_w_in_to_pad(_cols_from_chips(w_in_sh)), w_q=_w_uq_to_pad(_cols_from_chips(w_uq_sh)),
        w_kv=_w_ukv_to_pad(_cols_from_chips(w_ukv_sh)), w_out=w_out_sh.reshape(D_MODEL, D_MODEL),
        conv_b=conv_b.reshape(2, 1, D_FF))

    def ffn_weights(after):
        _, (w_up_sh, w_down_sh, cw_all) = _exchange_wait(ffn_gather, "ffn_gather_wait", "gather", after)
        return w_up_sh, w_down_sh.reshape(D_FF, D_MODEL), _conv_w_split(_cols_from_chips(cw_all))

    scatters = {}

    def start_scatter(slabs, tag):
        sums = _pair_sum(slabs, _swap_halves([halves(s) for s in slabs], tag + "_grad_swap_halves"), tag + "_grad_pair_sum")
        scatters[tag] = _exchange_start(list(sums), tag + "_scatter_start", "scatter", after=slabs[-1])
        return scatters[tag][4]

    def on_ffn_grads(g_w_up, g_w_down):
        return start_scatter([g_w_up, g_w_down.reshape(N_CHIPS, D_FF // N_CHIPS, D_MODEL)], "ffn")

    def on_mixer_grads(g):
        return start_scatter(
            [_cols_to_chips(_w_in_from_pad(g["w_in"])), _cols_to_chips(_w_uq_from_pad(g["w_q"])),
             _cols_to_chips(_w_ukv_from_pad(g["w_kv"])), g["w_out"].reshape(N_CHIPS, D_MODEL // N_CHIPS, D_MODEL)], "mixer")

    loss_part, grad_x, g = _local_step(x, positions, loss_target, wts, ffn_weights, on_ffn_grads, on_mixer_grads)
    loss = lax.psum(loss_part, ("x", "y", "c"))

    g_small_parts = dict(g)
    g_small_parts["conv_w"] = _conv_w_join(g["conv_w"])
    g_small_parts["conv_b"] = g["conv_b"].reshape(1, 2 * D_FF)
    flat = jnp.concatenate([g_small_parts[n].reshape(-1) for n, _ in _SMALL])
    flat = jnp.pad(flat, (0, _SMALL_ROWS * 128 - flat.shape[0])).reshape(_SMALL_ROWS, 128)
    small_gather = _exchange_start([flat], "small_gather_start", "all", after=grad_x)

    mixer_sums, mixer_landed = _exchange_wait(scatters["mixer"], "mixer_scatter_wait", "scatter", after=small_gather[4])
    ffn_sums, ffn_landed = _exchange_wait(scatters["ffn"], "ffn_scatter_wait", "scatter", after=mixer_landed[0])
    reduced = _chip_sum(list(mixer_sums) + list(ffn_sums), list(mixer_landed) + list(ffn_landed))
    g_big = dict(zip(_BIG, _join_halves(reduced)))

    grads, deltas, new_m, new_v = {}, {}, {}, {}

    def update(n, grad):
        w = weights[n]
        shape2 = grad.shape
        d, nm, nv = _adamw(w.reshape(shape2), grad, m_in[n].reshape(shape2), v_in[n].reshape(shape2), "adamw_" + n)
        grads[n], deltas[n], new_m[n], new_v[n] = (t.reshape(w.shape) for t in (grad, d, nm, nv))

    def update_transposed(n, grad):
        t = lambda a: jnp.swapaxes(a, 1, 2)
        d, nm, nv = _adamw(t(weights[n]), t(grad), t(m_in[n]), t(v_in[n]), "adamw_" + n)
        grads[n], deltas[n], new_m[n], new_v[n] = grad, t(d), t(nm), t(nv)

    for n in _BIG:
        g3 = g_big[n].reshape((1, -1, g_big[n].shape[-1]))
        if n == "w_in":
            update_transposed(n, g3)
        else:
            update(n, g3)

    (own,), (everyone,) = _exchange_wait(small_gather, "small_gather_wait", "all", after=deltas["w_up"])
    device = 2 * chip + lax.axis_index("c")
    everyone = lax.dynamic_update_slice(everyone, own[None], (device, 0, 0))
    total = _sum_slabs([everyone[j] for j in range(8)], "small_grads_sum", tr=_SMALL_ROWS).reshape(-1)
    o = 0
    for n, shp in _SMALL:
        piece = total[o:o + math.prod(shp)].reshape(shp)
        o += math.prod(shp)
        if n == "conv_w":
            piece = lax.dynamic_slice_in_dim(piece, chip * UP_SHARD, UP_SHARD, axis=1)
        update(n, piece)
    return (loss, grad_x, *[grads[n] for n in names], *[deltas[n] for n in names], *[new_m[n] for n in names],
            *[new_v[n] for n in names])
```

```python
import functools
import math

import jax
import jax.numpy as jnp
from jax import lax
from jax.experimental import pallas as pl
from jax.experimental.pallas import tpu as pltpu

F32 = jnp.float32
MXU_DTYPE = jnp.bfloat16
MESH = pl.DeviceIdType.MESH

D_MODEL = 1024
EPS = 1e-6
A_GROUPS = 8
CHUNK = 128
HEADS = 8
NOPE = 128
ROPE = 64
QK_DIM = NOPE + ROPE
HEAD_PAD = 256
Q_RANK = 256
KV_RANK = 128
ROPE_THETA = 10000.0
D_FF = 2816
FF_TILE = 256
N_FF_TILES = D_FF // FF_TILE
LAT = 512
IN_PAD = 4 * D_MODEL + LAT
N_CHIPS = 4
ADAM_LR, ADAM_B1, ADAM_B2, ADAM_EPS, ADAM_WD, ADAM_STEP = 0.001, 0.9, 0.999, 1e-08, 0.01, 10

VMEM_CAP_V7X = 64 * 1024 * 1024
NEG = -1e30


def _params(sem, nbytes):
    limit = int(min(VMEM_CAP_V7X - (8 << 20), max(32 << 20, 3 * nbytes)))
    return pltpu.CompilerParams(dimension_semantics=sem, vmem_limit_bytes=limit)


def _nbytes(shape, dtype):
    return math.prod(shape) * jnp.dtype(dtype).itemsize


_DIMS = {"nn": (((1,), (0,)), ((), ())), "nt": (((1,), (1,)), ((), ())), "tn": (((0,), (0,)), ((), ()))}


def _mm(a, b, mode, name, *, tm, tn, tk, out_dtype=F32, add=None, dims=None, a_spec=None, b_spec=None,
        o_spec=None, out_shape=None):
    if dims is None:
        if mode == "nn":
            (M, K), (_, N) = a.shape, b.shape
        elif mode == "nt":
            (M, K), (N, _) = a.shape, b.shape
        else:
            (K, M), (_, N) = a.shape, b.shape
    else:
        M, N, K = dims
    a_blk = (tk, tm) if mode == "tn" else (tm, tk)
    b_blk = (tn, tk) if mode == "nt" else (tk, tn)
    if a_spec is None:
        a_spec = pl.BlockSpec(a_blk, (lambda i, j, k: (k, i)) if mode == "tn" else (lambda i, j, k: (i, k)))
    if b_spec is None:
        b_spec = pl.BlockSpec(b_blk, (lambda i, j, k: (j, k)) if mode == "nt" else (lambda i, j, k: (k, j)))
    if o_spec is None:
        o_spec = pl.BlockSpec((tm, tn), lambda i, j, k: (i, j))
    if out_shape is None:
        out_shape = (M, N)
    assert M % tm == 0 and N % tn == 0 and K % tk == 0, (name, M, N, K, tm, tn, tk)
    nk = K // tk
    contract = _DIMS[mode]
    has_add = add is not None

    def body(*refs):
        if has_add:
            a_ref, b_ref, add_ref, o_ref, acc = refs
        else:
            a_ref, b_ref, o_ref, acc = refs
        k = pl.program_id(2)

        @pl.when(k == 0)
        def _():
            acc[...] = jnp.zeros_like(acc)

        acc[...] += lax.dot_general(a_ref[...].astype(MXU_DTYPE), b_ref[...].astype(MXU_DTYPE), contract,
                                    preferred_element_type=F32)

        @pl.when(k == nk - 1)
        def _():
            r = acc[...]
            if has_add:
                r = r + add_ref[...]
            o_ref[...] = r.astype(out_dtype)

    in_specs = [a_spec, b_spec]
    args = [a, b]
    nbytes = _nbytes(a_blk, a.dtype) + _nbytes(b_blk, b.dtype) + 3 * _nbytes((tm, tn), F32)
    if has_add:
        in_specs.append(pl.BlockSpec((tm, tn), lambda i, j, k: (i, j)))
        args.append(add)
        nbytes += _nbytes((tm, tn), F32)
    return pl.pallas_call(
        body, name=name, out_shape=jax.ShapeDtypeStruct(out_shape, out_dtype),
        grid=(M // tm, N // tn, nk), in_specs=in_specs, out_specs=o_spec,
        scratch_shapes=[pltpu.VMEM((tm, tn), F32)],
        compiler_params=_params(("parallel", "parallel", "arbitrary"), nbytes),
    )(*args)


_GELU_C = math.sqrt(2.0 / math.pi)
_GELU_A = 0.044715


def _sigmoid(x):
    return 1.0 / (1.0 + jnp.exp(-x))


def _tanh(y):
    return 1.0 - 2.0 / (1.0 + jnp.exp(2.0 * y))


def _gelu(x):
    t = _tanh(_GELU_C * (x + _GELU_A * (x * x * x)))
    return x * (0.5 * (1.0 + t))


def _gelu_and_grad(x):
    x2 = x * x
    t = _tanh(_GELU_C * (x + _GELU_A * (x2 * x)))
    cdf = 0.5 * (1.0 + t)
    grad = cdf + 0.5 * x * (1.0 - t * t) * (_GELU_C * (1.0 + 3.0 * _GELU_A * x2))
    return x * cdf, grad


def _rope_mix(g, cos_a, sin_a):
    return g * cos_a + pltpu.roll(g, 64, 1) * sin_a


def _rope_mix_bwd(d, cos_a, sin_a):
    return d * cos_a + pltpu.roll(d * sin_a, 64, 1)


def _rms_fwd(x, g, name, tr=512):
    T, D = x.shape

    def body(x_ref, g_ref, h_ref):
        xv = x_ref[...]
        r = lax.rsqrt(jnp.mean(xv * xv, axis=-1, keepdims=True) + EPS)
        h_ref[...] = ((xv * r) * g_ref[...]).astype(h_ref.dtype)

    return pl.pallas_call(
        body, name=name, out_shape=jax.ShapeDtypeStruct((T, D), MXU_DTYPE), grid=(T // tr,),
        in_specs=[pl.BlockSpec((tr, D), lambda i: (i, 0)), pl.BlockSpec((1, D), lambda i: (0, 0))],
        out_specs=pl.BlockSpec((tr, D), lambda i: (i, 0)),
        compiler_params=_params(("parallel",), 3 * _nbytes((tr, D), F32)),
    )(x, g)


def _rms_bwd(x, g, dh, dres, name, tr=512):
    T, D = x.shape

    def body(x_ref, g_ref, dh_ref, dres_ref, dx_ref, gg_ref):
        @pl.when(pl.program_id(0) == 0)
        def _():
            gg_ref[...] = jnp.zeros_like(gg_ref)

        xv = x_ref[...]
        r = lax.rsqrt(jnp.mean(xv * xv, axis=-1, keepdims=True) + EPS)
        xn = xv * r
        dhv = dh_ref[...]
        dxn = dhv * g_ref[...]
        dx_ref[...] = dres_ref[...] + r * (dxn - xn * jnp.mean(dxn * xn, axis=-1, keepdims=True))
        gg_ref[...] += jnp.sum(dhv * xn, axis=0, keepdims=True)

    row = pl.BlockSpec((tr, D), lambda i: (i, 0))
    vec = pl.BlockSpec((1, D), lambda i: (0, 0))
    return pl.pallas_call(
        body, name=name,
        out_shape=(jax.ShapeDtypeStruct((T, D), F32), jax.ShapeDtypeStruct((1, D), F32)),
        grid=(T // tr,), in_specs=[row, vec, row, row], out_specs=(row, vec),
        compiler_params=_params(("arbitrary",), 6 * _nbytes((tr, D), F32)),
    )(x, g, dh, dres)


def _lat_fwd(z, gq, gkv, wq, wkv, cos_a, sin_a, tr=256):
    T = z.shape[0]
    lat_blk = (4 * D_MODEL) // LAT

    def body(z_ref, gq_ref, gkv_ref, wq_ref, wkv_ref, cos_ref, sin_ref, q_ref, k_ref, v_ref, cqn_ref, ckvn_ref):
        zl = z_ref[...]
        cos_v, sin_v = cos_ref[...], sin_ref[...]
        cq = zl[:, :Q_RANK]
        ckv = zl[:, Q_RANK:Q_RANK + KV_RANK]
        krb = zl[:, Q_RANK + KV_RANK:]
        cqn = ((cq * lax.rsqrt(jnp.mean(cq * cq, axis=-1, keepdims=True) + EPS)) * gq_ref[...]).astype(MXU_DTYPE)
        ckvn = ((ckv * lax.rsqrt(jnp.mean(ckv * ckv, axis=-1, keepdims=True) + EPS)) * gkv_ref[...]).astype(MXU_DTYPE)
        cqn_ref[...] = cqn
        ckvn_ref[...] = ckvn
        krr = _rope_mix(krb, cos_v, sin_v).astype(MXU_DTYPE)
        q = jnp.dot(cqn, wq_ref[...], preferred_element_type=F32)
        kv = jnp.dot(ckvn, wkv_ref[...], preferred_element_type=F32)
        for h in range(HEADS):
            o = h * HEAD_PAD
            q_ref[:, o:o + NOPE] = q[:, o:o + NOPE].astype(MXU_DTYPE)
            q_ref[:, o + NOPE:o + HEAD_PAD] = _rope_mix(q[:, o + NOPE:o + HEAD_PAD], cos_v, sin_v).astype(MXU_DTYPE)
            k_ref[:, o:o + NOPE] = kv[:, h * NOPE:(h + 1) * NOPE].astype(MXU_DTYPE)
            k_ref[:, o + NOPE:o + HEAD_PAD] = krr
        v_ref[...] = kv[:, HEADS * NOPE:].astype(MXU_DTYPE)

    def row(w):
        return pl.BlockSpec((tr, w), lambda i: (i, 0))

    def full(a):
        return pl.BlockSpec(a.shape, lambda i: (0, 0))

    return pl.pallas_call(
        body, name="lat_fwd",
        out_shape=(jax.ShapeDtypeStruct((T, HEADS * HEAD_PAD), MXU_DTYPE), jax.ShapeDtypeStruct((T, HEADS * HEAD_PAD), MXU_DTYPE),
                   jax.ShapeDtypeStruct((T, HEADS * NOPE), MXU_DTYPE), jax.ShapeDtypeStruct((T, Q_RANK), MXU_DTYPE),
                   jax.ShapeDtypeStruct((T, KV_RANK), MXU_DTYPE)),
        grid=(T // tr,),
        in_specs=[pl.BlockSpec((tr, LAT), lambda i: (i, lat_blk)), full(gq), full(gkv), full(wq), full(wkv), row(128), row(128)],
        out_specs=(row(HEADS * HEAD_PAD), row(HEADS * HEAD_PAD), row(HEADS * NOPE), row(Q_RANK), row(KV_RANK)),
        compiler_params=_params(("parallel",), 8 * _nbytes((tr, HEADS * HEAD_PAD), F32)),
    )(z, gq, gkv, wq, wkv, cos_a, sin_a)


ATT_BLOCK = 256
_SCALE = QK_DIM ** -0.5


def _causal_mask(n):
    return lax.broadcasted_iota(jnp.int32, (n, n), 1) <= lax.broadcasted_iota(jnp.int32, (n, n), 0)


def _causal_mask_t(n):
    return lax.broadcasted_iota(jnp.int32, (n, n), 0) <= lax.broadcasted_iota(jnp.int32, (n, n), 1)


ATT_HEADS = 4


def _attn_fwd(q, k, v, B, S):
    tq = ATT_BLOCK
    nq = S // tq
    T = B * S
    hp, groups = ATT_HEADS, HEADS // ATT_HEADS

    def body(q_ref, k_ref, v_ref, o_ref, *lse_refs):
        qi = pl.program_id(2)
        qs = [q_ref[:, t * HEAD_PAD:(t + 1) * HEAD_PAD] for t in range(hp)]

        def scores(j, t):
            rows = pl.ds(pl.multiple_of(j * tq, tq), tq)
            return lax.dot_general(k_ref[rows, t * HEAD_PAD:(t + 1) * HEAD_PAD], qs[t], _DIMS["nt"],
                                   preferred_element_type=F32)

        def step(j, carry, last):
            rows = pl.ds(pl.multiple_of(j * tq, tq), tq)
            out = []
            for t in range(hp):
                m, l, acc, st = carry[t]
                st_next = st if last else scores(j + 1, t)
                st = st * _SCALE
                if last:
                    st = jnp.where(_causal_mask_t(tq), st, NEG)
                m_new = jnp.maximum(m, jnp.max(st, axis=0, keepdims=True))
                alpha = jnp.exp(m - m_new)
                p = jnp.exp(st - m_new)
                l = alpha * l + jnp.sum(p, axis=0, keepdims=True)
                acc = alpha * acc + lax.dot_general(v_ref[rows, t * NOPE:(t + 1) * NOPE], p.astype(MXU_DTYPE),
                                                    _DIMS["tn"], preferred_element_type=F32)
                out.append((m_new, l, acc, st_next))
            return tuple(out)

        init = tuple((jnp.full((1, tq), NEG, F32), jnp.zeros((1, tq), F32), jnp.zeros((NOPE, tq), F32), scores(0, t))
                     for t in range(hp))
        carry = lax.fori_loop(0, qi, lambda j, c: step(j, c, False), init)
        carry = step(qi, carry, True)
        for t in range(hp):
            m, l, acc, _ = carry[t]
            o_ref[:, t * NOPE:(t + 1) * NOPE] = (acc / l).T
            lse_refs[t][0] = m + jnp.log(l)

    lse_sds = jax.ShapeDtypeStruct((groups * B * nq, 1, tq), F32)
    lse_spec = pl.BlockSpec((1, 1, tq), lambda b, h, i: ((h * B + b) * nq + i, 0, 0))
    return pl.pallas_call(
        body, name="attn_fwd",
        out_shape=(jax.ShapeDtypeStruct((T, HEADS * NOPE), F32),) + (lse_sds,) * hp,
        grid=(B, groups, nq),
        in_specs=[pl.BlockSpec((tq, hp * HEAD_PAD), lambda b, h, i: (b * nq + i, h)),
                  pl.BlockSpec((S, hp * HEAD_PAD), lambda b, h, i: (b, h)),
                  pl.BlockSpec((S, hp * NOPE), lambda b, h, i: (b, h))],
        out_specs=(pl.BlockSpec((tq, hp * NOPE), lambda b, h, i: (b * nq + i, h)),) + (lse_spec,) * hp,
        compiler_params=_params(("parallel", "parallel", "arbitrary"), 4 * hp * _nbytes((S, HEAD_PAD), MXU_DTYPE)),
    )(q, k, v)


def _attn_bwd(q, k, v, do, lses, delta, B, S):
    tq = ATT_BLOCK
    nq = S // tq
    T = B * S
    hp, groups = ATT_HEADS, HEADS // ATT_HEADS

    def body(q_ref, k_ref, v_ref, do_ref, *refs):
        lse_refs, dl_refs = refs[:hp], refs[hp:2 * hp]
        dq_ref, dk_ref, dv_ref = refs[2 * hp:]
        kj = pl.program_id(2)

        @pl.when(kj == 0)
        def _():
            dq_ref[...] = jnp.zeros_like(dq_ref)

        def products(i, t):
            rows = pl.ds(pl.multiple_of(i * tq, tq), tq)
            st = lax.dot_general(k_ref[:, t * HEAD_PAD:(t + 1) * HEAD_PAD], q_ref[rows, t * HEAD_PAD:(t + 1) * HEAD_PAD],
                                 _DIMS["nt"], preferred_element_type=F32)
            dpt = lax.dot_general(v_ref[:, t * NOPE:(t + 1) * NOPE], do_ref[rows, t * NOPE:(t + 1) * NOPE],
                                  _DIMS["nt"], preferred_element_type=F32)
            return st, dpt

        def step(i, carry, masked):
            rows = pl.ds(pl.multiple_of(i * tq, tq), tq)
            nxt = jnp.minimum(i + 1, nq - 1)
            out = []
            for t in range(hp):
                dk, dv, st, dpt = carry[t]
                st_next, dpt_next = products(nxt, t)
                qk_cols = slice(t * HEAD_PAD, (t + 1) * HEAD_PAD)
                v_cols = slice(t * NOPE, (t + 1) * NOPE)
                p = jnp.exp(st * _SCALE - lse_refs[t][i])
                if masked:
                    p = jnp.where(_causal_mask_t(tq), p, 0.0)
                dv = dv + jnp.dot(p.astype(MXU_DTYPE), do_ref[rows, v_cols], preferred_element_type=F32)
                ds = (p * (dpt - dl_refs[t][i]) * _SCALE).astype(MXU_DTYPE)
                dk = dk + jnp.dot(ds, q_ref[rows, qk_cols], preferred_element_type=F32)
                dq_ref[rows, qk_cols] += lax.dot_general(ds, k_ref[:, qk_cols], _DIMS["tn"], preferred_element_type=F32)
                out.append((dk, dv, st_next, dpt_next))
            return tuple(out)

        init = tuple((jnp.zeros((tq, HEAD_PAD), F32), jnp.zeros((tq, NOPE), F32)) + products(kj, t) for t in range(hp))
        carry = step(kj, init, True)
        carry = lax.fori_loop(kj + 1, nq, lambda i, c: step(i, c, False), carry)
        for t in range(hp):
            dk_ref[:, t * HEAD_PAD:(t + 1) * HEAD_PAD] = carry[t][0]
            dv_ref[:, t * NOPE:(t + 1) * NOPE] = carry[t][1].astype(dv_ref.dtype)

    seq = lambda w: pl.BlockSpec((S, w), lambda b, h, j: (b, h))
    blk = lambda w: pl.BlockSpec((tq, w), lambda b, h, j: (b * nq + j, h))
    lse_spec = pl.BlockSpec((nq, 1, tq), lambda b, h, j: (h * B + b, 0, 0))
    dl_specs = [pl.BlockSpec((nq, 1, tq), lambda b, h, j, t=t: ((h * hp + t) * B + b, 0, 0)) for t in range(hp)]
    return pl.pallas_call(
        body, name="attn_bwd",
        out_shape=(jax.ShapeDtypeStruct((T, HEADS * HEAD_PAD), F32), jax.ShapeDtypeStruct((T, HEADS * HEAD_PAD), F32),
                   jax.ShapeDtypeStruct((T, HEADS * NOPE), MXU_DTYPE)),
        grid=(B, groups, nq),
        in_specs=[seq(hp * HEAD_PAD), blk(hp * HEAD_PAD), blk(hp * NOPE), seq(hp * NOPE)] + [lse_spec] * hp + dl_specs,
        out_specs=(seq(hp * HEAD_PAD), blk(hp * HEAD_PAD), blk(hp * NOPE)),
        compiler_params=_params(("parallel", "parallel", "arbitrary"), 8 * hp * _nbytes((S, HEAD_PAD), F32)),
    )(q, k, v, do, *lses, *([delta] * hp))


MIX_ROWS = 256


def _tril_weights(ws_ref, g):
    return jnp.where(_causal_mask(CHUNK), ws_ref[g], 0.0).astype(MXU_DTYPE)


def _layer_norm_stats(va):
    mu = jnp.mean(va, axis=-1, keepdims=True)
    xc = va - mu
    rs = lax.rsqrt(jnp.mean(xc * xc, axis=-1, keepdims=True) + EPS)
    return xc * rs


def _mix_specs(tr):
    zcol = lambda c: pl.BlockSpec((tr, D_MODEL), lambda i, c=c: (i, c))
    row = pl.BlockSpec((tr, D_MODEL), lambda i: (i, 0))
    vec = pl.BlockSpec((1, D_MODEL), lambda i: (0, 0))
    ws = pl.BlockSpec((A_GROUPS, CHUNK, CHUNK), lambda i: (0, 0, 0))
    bs = pl.BlockSpec((CHUNK, 128), lambda i: (0, 0))
    return zcol, row, vec, ws, bs


def _mix_fwd(z, yb, ln_g, ln_b, ws, bs_t):
    T = z.shape[0]
    tr = MIX_ROWS
    zcol, row, vec, ws_spec, bs_spec = _mix_specs(tr)

    def body(zu_ref, zv_ref, zga_ref, zgb_ref, yb_ref, g_ref, b_ref, ws_ref, bs_ref, out_ref, vn_s):
        vhat = _layer_norm_stats(_gelu(zv_ref[...]))
        vn_s[...] = (vhat * g_ref[...] + b_ref[...]).astype(MXU_DTYPE)
        for g in range(A_GROUPS):
            w = _tril_weights(ws_ref, g)
            bias = bs_ref[:, g:g + 1]
            cols = slice(g * CHUNK, (g + 1) * CHUNK)
            for c in range(tr // CHUNK):
                rows = slice(c * CHUNK, (c + 1) * CHUNK)
                mixed = jnp.dot(w, vn_s[rows, cols], preferred_element_type=F32) + bias
                ya = _gelu(zu_ref[rows, cols]) * mixed
                merged = _sigmoid(zga_ref[rows, cols]) * ya + _sigmoid(zgb_ref[rows, cols]) * yb_ref[rows, cols]
                out_ref[rows, cols] = merged.astype(MXU_DTYPE)

    return pl.pallas_call(
        body, name="mix_fwd", out_shape=jax.ShapeDtypeStruct((T, D_MODEL), MXU_DTYPE), grid=(T // tr,),
        in_specs=[zcol(0), zcol(1), zcol(2), zcol(3), row, vec, vec, ws_spec, bs_spec], out_specs=row,
        scratch_shapes=[pltpu.VMEM((tr, D_MODEL), MXU_DTYPE)],
        compiler_params=_params(("parallel",), 8 * _nbytes((tr, D_MODEL), F32)),
    )(z, z, z, z, yb, ln_g, ln_b, ws, bs_t)


def _mix_bwd(z, yb, dm, ln_g, ln_b, ws, bs_t):
    T = z.shape[0]
    tr = MIX_ROWS
    zcol, row, vec, ws_spec, bs_spec = _mix_specs(tr)

    def body(zu_ref, zv_ref, zga_ref, zgb_ref, yb_ref, dm_ref, g_ref, b_ref, ws_ref, bs_ref,
             dz_ref, dyb_ref, dl_ref, gws_ref, gbs_ref, glg_ref, glb_ref, vn_s, dvn_s):
        @pl.when(pl.program_id(0) == 0)
        def _():
            gws_ref[...] = jnp.zeros_like(gws_ref)
            gbs_ref[...] = jnp.zeros_like(gbs_ref)
            glg_ref[...] = jnp.zeros_like(glg_ref)
            glb_ref[...] = jnp.zeros_like(glb_ref)

        lane = lax.broadcasted_iota(jnp.int32, (CHUNK, 128), 1)
        va, dgelu_v = _gelu_and_grad(zv_ref[...])
        mu = jnp.mean(va, axis=-1, keepdims=True)
        xc = va - mu
        rs = lax.rsqrt(jnp.mean(xc * xc, axis=-1, keepdims=True) + EPS)
        vhat = xc * rs
        vn_s[...] = (vhat * g_ref[...] + b_ref[...]).astype(MXU_DTYPE)
        gbs_acc = jnp.zeros((CHUNK, 128), F32)
        for g in range(A_GROUPS):
            w = _tril_weights(ws_ref, g)
            bias = bs_ref[:, g:g + 1]
            cols = slice(g * CHUNK, (g + 1) * CHUNK)
            gw_acc = jnp.zeros((CHUNK, CHUNK), F32)
            for c in range(tr // CHUNK):
                rows = slice(c * CHUNK, (c + 1) * CHUNK)
                vn = vn_s[rows, cols]
                mixed = jnp.dot(w, vn, preferred_element_type=F32) + bias
                ua, dgelu_u = _gelu_and_grad(zu_ref[rows, cols])
                dmv = dm_ref[rows, cols]
                sa = _sigmoid(zga_ref[rows, cols])
                dya = dmv * sa
                dz_ref[rows, 2 * D_MODEL + g * CHUNK:2 * D_MODEL + (g + 1) * CHUNK] = (
                    dmv * (ua * mixed) * (sa * (1.0 - sa))).astype(dz_ref.dtype)
                dz_ref[rows, cols] = (dya * mixed * dgelu_u).astype(dz_ref.dtype)
                dmix = dya * ua
                gbs_acc = gbs_acc + jnp.where(lane == g, jnp.sum(dmix, axis=-1, keepdims=True), 0.0)
                dmix_b = dmix.astype(MXU_DTYPE)
                gw_acc = gw_acc + lax.dot_general(dmix_b, vn, _DIMS["nt"], preferred_element_type=F32)
                dvn_s[rows, cols] = lax.dot_general(w, dmix_b, _DIMS["tn"], preferred_element_type=F32)
            gws_ref[g] += jnp.where(_causal_mask(CHUNK), gw_acc, 0.0)
        gbs_ref[...] += gbs_acc

        dvn = dvn_s[...]
        glg_ref[...] += jnp.sum(dvn * vhat, axis=0, keepdims=True)
        glb_ref[...] += jnp.sum(dvn, axis=0, keepdims=True)
        dvh = dvn * g_ref[...]
        dva = rs * (dvh - jnp.mean(dvh, axis=-1, keepdims=True) - vhat * jnp.mean(dvh * vhat, axis=-1, keepdims=True))
        dz_ref[:, D_MODEL:2 * D_MODEL] = (dva * dgelu_v).astype(dz_ref.dtype)

        dmv = dm_ref[...]
        ybv = yb_ref[...]
        sb = _sigmoid(zgb_ref[...])
        dyb = dmv * sb
        dyb_ref[...] = dyb.astype(dyb_ref.dtype)
        dz_ref[:, 3 * D_MODEL:4 * D_MODEL] = (dmv * ybv * (sb * (1.0 - sb))).astype(dz_ref.dtype)
        dz_ref[:, 4 * D_MODEL:] = jnp.zeros((tr, LAT), dz_ref.dtype)
        prod = dyb * ybv
        sel = (lax.broadcasted_iota(jnp.int32, (HEADS, D_MODEL), 1) // NOPE
               == lax.broadcasted_iota(jnp.int32, (HEADS, D_MODEL), 0)).astype(jnp.bfloat16)
        hi = prod.astype(jnp.bfloat16)
        rest = prod - hi.astype(F32)
        mid = rest.astype(jnp.bfloat16)
        lo = (rest - mid.astype(F32)).astype(jnp.bfloat16)
        dl_ref[...] = (lax.dot_general(sel, hi, _DIMS["nt"], preferred_element_type=F32)
                       + lax.dot_general(sel, mid, _DIMS["nt"], preferred_element_type=F32)
                       + lax.dot_general(sel, lo, _DIMS["nt"], preferred_element_type=F32))

    return pl.pallas_call(
        body, name="mix_bwd",
        out_shape=(jax.ShapeDtypeStruct((T, IN_PAD), MXU_DTYPE), jax.ShapeDtypeStruct((T, D_MODEL), MXU_DTYPE),
                   jax.ShapeDtypeStruct((HEADS, T), F32), jax.ShapeDtypeStruct((A_GROUPS, CHUNK, CHUNK), F32),
                   jax.ShapeDtypeStruct((CHUNK, 128), F32), jax.ShapeDtypeStruct((1, D_MODEL), F32),
                   jax.ShapeDtypeStruct((1, D_MODEL), F32)),
        grid=(T // tr,),
        in_specs=[zcol(0), zcol(1), zcol(2), zcol(3), row, row, vec, vec, ws_spec, bs_spec],
        out_specs=(pl.BlockSpec((tr, IN_PAD), lambda i: (i, 0)), row, pl.BlockSpec((HEADS, tr), lambda i: (0, i)),
                   ws_spec, bs_spec, vec, vec),
        scratch_shapes=[pltpu.VMEM((tr, D_MODEL), MXU_DTYPE), pltpu.VMEM((tr, D_MODEL), F32)],
        compiler_params=_params(("arbitrary",), 12 * _nbytes((tr, D_MODEL), F32)),
    )(z, z, z, z, yb, dm, ln_g, ln_b, ws, bs_t)


def _lat_bwd(dz, z, dq, dk, dv, gq, gkv, wq, wkv, cos_a, sin_a, tr=256):
    T = z.shape[0]
    lat_blk = (4 * D_MODEL) // LAT

    def body(dz_in, z_ref, dq_ref, dk_ref, dv_ref, gq_ref, gkv_ref, wq_ref, wkv_ref, cos_ref, sin_ref,
             dz_ref, dqr_ref, dkv_ref, ggq_ref, ggkv_ref):
        del dz_in

        @pl.when(pl.program_id(0) == 0)
        def _():
            ggq_ref[...] = jnp.zeros_like(ggq_ref)
            ggkv_ref[...] = jnp.zeros_like(ggkv_ref)

        cos_v, sin_v = cos_ref[...], sin_ref[...]
        dkr = jnp.zeros((tr, 128), F32)
        for h in range(HEADS):
            o = h * HEAD_PAD
            dqr_ref[:, o:o + NOPE] = dq_ref[:, o:o + NOPE].astype(MXU_DTYPE)
            dqr_ref[:, o + NOPE:o + HEAD_PAD] = _rope_mix_bwd(dq_ref[:, o + NOPE:o + HEAD_PAD], cos_v, sin_v).astype(MXU_DTYPE)
            dkv_ref[:, h * NOPE:(h + 1) * NOPE] = dk_ref[:, o:o + NOPE].astype(MXU_DTYPE)
            dkr = dkr + _rope_mix_bwd(dk_ref[:, o + NOPE:o + HEAD_PAD], cos_v, sin_v)
        dkv_ref[:, HEADS * NOPE:] = dv_ref[...]
        dcqn = lax.dot_general(dqr_ref[...], wq_ref[...], _DIMS["nt"], preferred_element_type=F32)
        dckvn = lax.dot_general(dkv_ref[...], wkv_ref[...], _DIMS["nt"], preferred_element_type=F32)

        zl = z_ref[...]

        def rms_bwd(c, dn, g_ref, gg_ref):
            r = lax.rsqrt(jnp.mean(c * c, axis=-1, keepdims=True) + EPS)
            ch = c * r
            gg_ref[...] += jnp.sum(dn * ch, axis=0, keepdims=True)
            dch = dn * g_ref[...]
            return r * (dch - ch * jnp.mean(dch * ch, axis=-1, keepdims=True))

        dz_ref[:, :Q_RANK] = rms_bwd(zl[:, :Q_RANK], dcqn, gq_ref, ggq_ref).astype(dz_ref.dtype)
        dz_ref[:, Q_RANK:Q_RANK + KV_RANK] = rms_bwd(zl[:, Q_RANK:Q_RANK + KV_RANK], dckvn, gkv_ref, ggkv_ref).astype(dz_ref.dtype)
        dz_ref[:, Q_RANK + KV_RANK:] = dkr.astype(dz_ref.dtype)

    def row(w):
        return pl.BlockSpec((tr, w), lambda i: (i, 0))

    def full(a):
        return pl.BlockSpec(a.shape, lambda i: (0, 0))

    lat = pl.BlockSpec((tr, LAT), lambda i: (i, lat_blk))
    return pl.pallas_call(
        body, name="lat_bwd",
        out_shape=(jax.ShapeDtypeStruct(dz.shape, dz.dtype), jax.ShapeDtypeStruct((T, HEADS * HEAD_PAD), MXU_DTYPE),
                   jax.ShapeDtypeStruct((T, 2 * HEADS * NOPE), MXU_DTYPE), jax.ShapeDtypeStruct(gq.shape, F32),
                   jax.ShapeDtypeStruct(gkv.shape, F32)),
        grid=(T // tr,),
        in_specs=[pl.BlockSpec(memory_space=pl.ANY), lat, row(HEADS * HEAD_PAD), row(HEADS * HEAD_PAD), row(HEADS * NOPE),
                  full(gq), full(gkv), full(wq), full(wkv), row(128), row(128)],
        out_specs=(lat, row(HEADS * HEAD_PAD), row(2 * HEADS * NOPE), full(gq), full(gkv)),
        input_output_aliases={0: 0},
        compiler_params=_params(("arbitrary",), 8 * _nbytes((tr, HEADS * HEAD_PAD), F32)),
    )(dz, z, dq, dk, dv, gq, gkv, wq, wkv, cos_a, sin_a)


GATE_ROWS = 64
HALO = 8


def _taps(ref, half, r, first):
    C = GATE_ROWS
    if first:
        xs = jnp.concatenate([jnp.zeros((HALO, ref.shape[-1]), F32), ref[half, 0:C, :]], axis=0)
    else:
        xs = ref[half, pl.ds(pl.multiple_of(r * C - HALO, HALO), C + HALO), :]
    return xs[HALO:, :], pltpu.roll(xs, 1, 0)[HALO:, :], pltpu.roll(xs, 2, 0)[HALO:, :]


def _conv_taps(taps, cw, cb):
    x0, x1, x2 = taps
    return cb + cw[0:1, :] * x2 + cw[1:2, :] * x1 + cw[2:3, :] * x0


def _fold8(x):
    acc = x[0:8, :]
    for i in range(1, x.shape[0] // 8):
        acc = acc + x[8 * i:8 * (i + 1), :]
    return acc


def _gate_fwd(up3, conv_w, conv_b, B, S):
    T = B * S
    W = FF_TILE
    C = GATE_ROWS

    def body(up_ref, cw_ref, cb_ref, act_ref):
        def chunk(r, first):
            gate = _conv_taps(_taps(up_ref, 0, r, first), cw_ref[0], cb_ref[0])
            val = _conv_taps(_taps(up_ref, 1, r, first), cw_ref[1], cb_ref[1])
            base = 0 if first else pl.multiple_of(r * C, C)
            act_ref[pl.ds(base, C), :] = (gate * _sigmoid(gate) * val).astype(act_ref.dtype)

        chunk(0, True)

        @pl.loop(1, S // C)
        def _(r):
            chunk(r, False)

    return pl.pallas_call(
        body, name="gate_fwd", out_shape=jax.ShapeDtypeStruct((T, D_FF), MXU_DTYPE), grid=(B, N_FF_TILES),
        in_specs=[pl.BlockSpec((2, S, W), lambda b, j: (0, b, j)), pl.BlockSpec((2, 3, W), lambda b, j: (0, 0, j)),
                  pl.BlockSpec((2, 1, W), lambda b, j: (0, 0, j))],
        out_specs=pl.BlockSpec((S, W), lambda b, j: (b, j)),
        compiler_params=_params(("parallel", "parallel"), 6 * _nbytes((S, W), F32)),
    )(up3, conv_w, conv_b)


def _gate_bwd(up3, dact, conv_w, conv_b, B, S):
    T = B * S
    W = FF_TILE
    C = GATE_ROWS

    def body(up_ref, da_ref, cw_ref, cb_ref, dup_ref, gcw_ref, gcb_ref, d_s):
        @pl.when(pl.program_id(1) == 0)
        def _():
            gcw_ref[...] = jnp.zeros_like(gcw_ref)
            gcb_ref[...] = jnp.zeros_like(gcb_ref)

        def chunk(r, first, sums):
            rows = pl.ds(0 if first else pl.multiple_of(r * C, C), C)
            taps = [_taps(up_ref, half, r, first) for half in (0, 1)]
            gate = _conv_taps(taps[0], cw_ref[0], cb_ref[0])
            val = _conv_taps(taps[1], cw_ref[1], cb_ref[1])
            sg = _sigmoid(gate)
            da = da_ref[rows, :]
            d_halves = (da * val * (sg * (1.0 + gate * (1.0 - sg))), da * (gate * sg))
            out = []
            for half, dup in enumerate(d_halves):
                d_s[half, rows, :] = dup
                x0, x1, x2 = taps[half]
                sb, s0, s1, s2 = sums[half]
                out.append((sb + _fold8(dup), s0 + _fold8(dup * x2), s1 + _fold8(dup * x1), s2 + _fold8(dup * x0)))
            return tuple(out)

        zeros = tuple(tuple(jnp.zeros((8, W), F32) for _ in range(4)) for _ in range(2))
        sums = chunk(0, True, zeros)
        sums = lax.fori_loop(1, S // C, lambda r, s: chunk(r, False, s), sums)
        for half in (0, 1):
            sb, s0, s1, s2 = sums[half]
            gcb_ref[half] += jnp.sum(sb, axis=0, keepdims=True)
            gcw_ref[half, 0:1, :] += jnp.sum(s0, axis=0, keepdims=True)
            gcw_ref[half, 1:2, :] += jnp.sum(s1, axis=0, keepdims=True)
            gcw_ref[half, 2:3, :] += jnp.sum(s2, axis=0, keepdims=True)

        d_s[:, S:S + HALO, :] = jnp.zeros((2, HALO, W), F32)

        @pl.loop(0, S // C)
        def _(r):
            base = pl.multiple_of(r * C, C)
            for half in (0, 1):
                ds_ = d_s[half, pl.ds(base, C + HALO), :]
                cw = cw_ref[half]
                dx = (cw[2:3, :] * ds_[:C, :] + cw[1:2, :] * pltpu.roll(ds_, C + HALO - 1, 0)[:C, :]
                      + cw[0:1, :] * pltpu.roll(ds_, C + HALO - 2, 0)[:C, :])
                dup_ref[half, pl.ds(base, C), :] = dx.astype(dup_ref.dtype)

    up_spec = pl.BlockSpec((2, S, W), lambda j, b: (0, b, j))
    cw_spec = pl.BlockSpec((2, 3, W), lambda j, b: (0, 0, j))
    cb_spec = pl.BlockSpec((2, 1, W), lambda j, b: (0, 0, j))
    return pl.pallas_call(
        body, name="gate_bwd",
        out_shape=(jax.ShapeDtypeStruct((2, T, D_FF), MXU_DTYPE), jax.ShapeDtypeStruct((2, 3, D_FF), F32),
                   jax.ShapeDtypeStruct((2, 1, D_FF), F32)),
        grid=(N_FF_TILES, B),
        in_specs=[up_spec, pl.BlockSpec((S, W), lambda j, b: (b, j)), cw_spec, cb_spec],
        out_specs=(up_spec, cw_spec, cb_spec),
        scratch_shapes=[pltpu.VMEM((2, S + HALO, W), F32)],
        compiler_params=_params(("parallel", "arbitrary"), 10 * _nbytes((S, W), F32)),
    )(up3, dact, conv_w, conv_b)


def _final(x2, tgt, g, tr=512):
    T, D = x2.shape

    def body(x_ref, t_ref, g_ref, dx_ref, loss_ref, gg_ref):
        @pl.when(pl.program_id(0) == 0)
        def _():
            loss_ref[...] = jnp.zeros_like(loss_ref)
            gg_ref[...] = jnp.zeros_like(gg_ref)

        xv = x_ref[...]
        gv = g_ref[...]
        r = lax.rsqrt(jnp.mean(xv * xv, axis=-1, keepdims=True) + EPS)
        xn = xv * r
        err = xn * gv - t_ref[...]
        loss_ref[...] += 0.5 * jnp.sum(jnp.mean(err * err, axis=-1, keepdims=True), axis=0, keepdims=True)
        dy = err * (1.0 / D)
        gg_ref[...] += jnp.sum(dy * xn, axis=0, keepdims=True)
        dxn = dy * gv
        dx_ref[...] = r * (dxn - xn * jnp.mean(dxn * xn, axis=-1, keepdims=True))

    row = pl.BlockSpec((tr, D), lambda i: (i, 0))
    vec = pl.BlockSpec((1, D), lambda i: (0, 0))
    return pl.pallas_call(
        body, name="final_loss",
        out_shape=(jax.ShapeDtypeStruct((T, D), F32), jax.ShapeDtypeStruct((1, 128), F32), jax.ShapeDtypeStruct((1, D), F32)),
        grid=(T // tr,), in_specs=[row, row, vec],
        out_specs=(row, pl.BlockSpec((1, 128), lambda i: (0, 0)), vec),
        compiler_params=_params(("arbitrary",), 6 * _nbytes((tr, D), F32)),
    )(x2, tgt, g)


def _sum_slabs(parts, name, tr):
    rows, cols = parts[0].shape
    n = len(parts)

    def body(*refs):
        acc = refs[0][...]
        for r in refs[1:n]:
            acc = acc + r[...]
        refs[n][...] = acc

    blk = pl.BlockSpec((tr, cols), lambda i: (i, 0))
    return pl.pallas_call(
        body, name=name, out_shape=jax.ShapeDtypeStruct((rows, cols), F32), grid=(rows // tr,),
        in_specs=[blk] * n, out_specs=blk,
        compiler_params=_params(("parallel",), (n + 1) * _nbytes((tr, cols), F32)),
    )(*parts)


ADAMW_BLOCK_BYTES = 2400 * 1024


def _adamw(w, g, m, v, name):
    lead = w.ndim == 3
    rows, cols = w.shape[-2:]
    fits = [d for d in range(8, rows + 1, 8) if rows % d == 0 and d * cols * 4 <= ADAMW_BLOCK_BYTES]
    tr = max(fits) if fits else rows
    c1 = 1.0 - ADAM_B1 ** ADAM_STEP
    c2 = 1.0 - ADAM_B2 ** ADAM_STEP

    def body(w_ref, g_ref, m_ref, v_ref, d_ref, nm_ref, nv_ref):
        gv = g_ref[...]
        nm = ADAM_B1 * m_ref[...] + (1.0 - ADAM_B1) * gv
        nv = ADAM_B2 * v_ref[...] + (1.0 - ADAM_B2) * (gv * gv)
        nm_ref[...] = nm
        nv_ref[...] = nv
        d_ref[...] = -ADAM_LR * ((nm / c1) / (jnp.sqrt(nv / c2) + ADAM_EPS) + ADAM_WD * w_ref[...])

    blk = pl.BlockSpec((None, tr, cols), lambda i: (0, i, 0)) if lead else pl.BlockSpec((tr, cols), lambda i: (i, 0))
    sds = jax.ShapeDtypeStruct(w.shape, F32)
    return pl.pallas_call(
        body, name=name, out_shape=(sds, sds, sds), grid=(rows // tr,), in_specs=[blk] * 4, out_specs=(blk, blk, blk),
        compiler_params=_params(("parallel",), 7 * _nbytes((tr, cols), F32)),
    )(w, g, m, v)


_ANY = pl.BlockSpec(memory_space=pl.ANY)


def _place():
    x, y, c = lax.axis_index("x"), lax.axis_index("y"), lax.axis_index("c")
    chips = [(1 - x, y), (x, 1 - y), (1 - x, 1 - y)]
    return x, y, c, chips


def _gather_weights(shards):
    n = len(shards)

    def body(*refs):
        ins, outs = refs[:n], refs[n:2 * n]
        send, recv, fsend, frecv, osend, orecv = refs[2 * n:]
        x, y, c, chips = _place()
        me = 2 * x + y
        first, passed = [], []
        for w in range(n):
            first.append(pltpu.make_async_remote_copy(
                src_ref=ins[w], dst_ref=outs[w].at[me], send_sem=osend.at[w], recv_sem=orecv.at[w],
                device_id=(x, y, 1 - c), device_id_type=MESH))
        for w in range(n):
            for j, (px, py) in enumerate(chips):
                first.append(pltpu.make_async_remote_copy(
                    src_ref=ins[w].at[c], dst_ref=outs[w].at[me, c], send_sem=send.at[3 * w + j],
                    recv_sem=recv.at[3 * w + j], device_id=(px, py, c), device_id_type=MESH))
        for cp in first:
            cp.start()
        for w in range(n):
            for j, (px, py) in enumerate(chips):
                landed = outs[w].at[2 * px + py, c]
                pltpu.make_async_remote_copy(src_ref=landed, dst_ref=landed, send_sem=send.at[3 * w + j],
                                             recv_sem=recv.at[3 * w + j], device_id=(px, py, c),
                                             device_id_type=MESH).wait_recv()
                fw = pltpu.make_async_remote_copy(src_ref=landed, dst_ref=landed, send_sem=fsend.at[3 * w + j],
                                                  recv_sem=frecv.at[3 * w + j], device_id=(x, y, 1 - c),
                                                  device_id_type=MESH)
                fw.start()
                passed.append(fw)
        for w in range(n):
            for j, (px, py) in enumerate(chips):
                other = outs[w].at[2 * px + py, 1 - c]
                pltpu.make_async_remote_copy(src_ref=other, dst_ref=other, send_sem=fsend.at[3 * w + j],
                                             recv_sem=frecv.at[3 * w + j], device_id=(x, y, 1 - c),
                                             device_id_type=MESH).wait_recv()
        for w in range(n):
            own = outs[w].at[me]
            pltpu.make_async_remote_copy(src_ref=own, dst_ref=own, send_sem=osend.at[w], recv_sem=orecv.at[w],
                                         device_id=(x, y, 1 - c), device_id_type=MESH).wait_recv()
        for cp in first + passed:
            cp.wait_send()

    dma = lambda k: pltpu.SemaphoreType.DMA((k,))
    return pl.pallas_call(
        body, name="gather_weights",
        out_shape=tuple(jax.ShapeDtypeStruct((N_CHIPS,) + s.shape, s.dtype) for s in shards),
        in_specs=[_ANY] * n, out_specs=tuple([_ANY] * n),
        scratch_shapes=[dma(3 * n), dma(3 * n), dma(3 * n), dma(3 * n), dma(n), dma(n)],
    )(*shards)


_HBM = pl.BlockSpec(memory_space=pltpu.HBM)
_SEM = pl.BlockSpec(memory_space=pltpu.SEMAPHORE)
_EFFECT = pltpu.SideEffectType.DATAFLOW_SIDE_EFFECTING


SEMS_PER_ARRAY = 8


def _exchange_copies(srcs, lands, send, recv, mode):
    x, y, c, chips = _place()
    if mode == "all":
        flips = [(fx, fy, fc) for fx in (0, 1) for fy in (0, 1) for fc in (0, 1)][1:]
        peers = [(x ^ fx, y ^ fy, c ^ fc) for fx, fy, fc in flips]
        slot = 4 * x + 2 * y + c
    else:
        peers = [(px, py, c) for px, py in chips] + ([(x, y, 1 - c)] if mode == "gather" else [])
        slot = 2 * x + y
    cps = []
    for w, (src, land) in enumerate(zip(srcs, lands)):
        for k, peer in enumerate(peers):
            piece = src.at[2 * peer[0] + peer[1]] if mode == "scatter" else src
            cps.append(pltpu.make_async_remote_copy(
                src_ref=piece, dst_ref=land.at[slot], send_sem=send.at[SEMS_PER_ARRAY * w + k],
                recv_sem=recv.at[SEMS_PER_ARRAY * w + k], device_id=peer, device_id_type=MESH))
    return cps


def _exchange_start(srcs, name, mode, after):
    n = len(srcs)
    lead = {"gather": (N_CHIPS,), "scatter": (), "all": (2 * N_CHIPS,)}[mode]
    land_shapes = [lead + s.shape for s in srcs]

    def body(*refs):
        src_refs, land_refs = refs[:n], refs[n:2 * n]
        send, recv = refs[2 * n + 1], refs[2 * n + 2]
        token = refs[-1]
        for cp in _exchange_copies(src_refs, land_refs, send, recv, mode):
            cp.start()
        token[...] = jnp.zeros_like(token)

    sems = pltpu.SemaphoreType.DMA((SEMS_PER_ARRAY * n,))
    out = pl.pallas_call(
        body, name=name,
        out_shape=(sems, sems, *[pltpu.HBM(s.shape, s.dtype) for s in srcs],
                   *[pltpu.HBM(shp, s.dtype) for shp, s in zip(land_shapes, srcs)], jax.ShapeDtypeStruct((8, 128), F32)),
        in_specs=[_HBM] * (2 * n) + [_ANY],
        out_specs=(_SEM, _SEM, *[_HBM] * (2 * n), pl.BlockSpec(memory_space=pltpu.VMEM)),
        input_output_aliases={i: 2 + i for i in range(2 * n)},
        compiler_params=pltpu.CompilerParams(has_side_effects=_EFFECT),
    )(*[pltpu.with_memory_space_constraint(s, pltpu.HBM) for s in srcs],
      *[pltpu.with_memory_space_constraint(lax.empty(shp, s.dtype), pltpu.HBM) for shp, s in zip(land_shapes, srcs)],
      after)
    return out[0], out[1], out[2:2 + n], out[2 + n:2 + 2 * n], out[-1]


def _exchange_wait(started, name, mode, after):
    send, recv, src_thru, land_thru, _ = started
    n = len(src_thru)

    def body(*refs):
        src_refs, land_refs, send_ref, recv_ref = refs[:n], refs[n:2 * n], refs[2 * n], refs[2 * n + 1]
        for cp in _exchange_copies(src_refs, land_refs, send_ref, recv_ref, mode):
            cp.wait_send()
            cp.wait_recv()

    out = pl.pallas_call(
        body, name=name,
        out_shape=tuple(pltpu.HBM(a.shape, a.dtype) for a in list(src_thru) + list(land_thru)),
        in_specs=[_HBM] * (2 * n) + [_SEM, _SEM, _ANY], out_specs=tuple([_HBM] * (2 * n)),
        input_output_aliases={i: i for i in range(2 * n)},
        compiler_params=pltpu.CompilerParams(has_side_effects=_EFFECT),
    )(*src_thru, *land_thru, send, recv, after)
    return out[:n], out[n:]


def _swap_halves(gs, name):
    n = len(gs)

    def body(*refs):
        ins, outs, send, recv = refs[:n], refs[n:2 * n], refs[2 * n], refs[2 * n + 1]
        x, y, c, _ = _place()
        cps = []
        for w in range(n):
            cps.append(pltpu.make_async_remote_copy(
                src_ref=ins[w].at[:, 1 - c], dst_ref=outs[w], send_sem=send.at[w], recv_sem=recv.at[w],
                device_id=(x, y, 1 - c), device_id_type=MESH))
        for cp in cps:
            cp.start()
        for cp in cps:
            cp.wait()

    return pl.pallas_call(
        body, name=name,
        out_shape=tuple(jax.ShapeDtypeStruct((g.shape[0],) + g.shape[2:], g.dtype) for g in gs),
        in_specs=[_ANY] * n, out_specs=tuple([_ANY] * n),
        scratch_shapes=[pltpu.SemaphoreType.DMA((n,)), pltpu.SemaphoreType.DMA((n,))],
    )(*gs)


GRAD_PAYLOAD = jnp.bfloat16


def _pair_sum(gs, gots, name):
    n = len(gs)
    core = lax.axis_index("c").astype(jnp.int32).reshape(1)

    def body(core_ref, *refs):
        del core_ref
        for w in range(n):
            refs[2 * n + w][...] = (refs[w][...] + refs[n + w][...]).astype(GRAD_PAYLOAD)

    in_specs, out_specs, out_shape, nbytes = [], [], [], 0
    for g in gs:
        q = g.shape[1] // 4
        in_specs.append(pl.BlockSpec((1, q, g.shape[2]), lambda s, r, core: (s, 2 * core[0] + r, 0)))
        nbytes += 3 * _nbytes((q, g.shape[2]), F32)
    for g in gs:
        q = g.shape[1] // 4
        in_specs.append(pl.BlockSpec((1, q, g.shape[2]), lambda s, r, core: (s, r, 0)))
        out_specs.append(pl.BlockSpec((1, q, g.shape[2]), lambda s, r, core: (s, r, 0)))
        out_shape.append(jax.ShapeDtypeStruct((g.shape[0], g.shape[1] // 2, g.shape[2]), GRAD_PAYLOAD))
    return pl.pallas_call(
        body, name=name, out_shape=tuple(out_shape),
        grid_spec=pltpu.PrefetchScalarGridSpec(num_scalar_prefetch=1, grid=(N_CHIPS, 2), in_specs=in_specs,
                                               out_specs=tuple(out_specs)),
        compiler_params=_params(("parallel", "parallel"), nbytes),
    )(core, *gs, *gots)


def _chip_sum(ps, landed):
    n = len(ps)
    x, y, c = lax.axis_index("x"), lax.axis_index("y"), lax.axis_index("c")
    where = jnp.stack([2 * x + y, 2 * (1 - x) + y, 2 * x + (1 - y), 2 * (1 - x) + (1 - y), c]).astype(jnp.int32)

    def body(where_ref, *refs):
        del where_ref
        for w in range(n):
            terms = [refs[4 * w + t][...].astype(F32) for t in range(4)]
            refs[4 * n + w][...] = ((terms[0] + terms[1]) + terms[2]) + terms[3]

    in_specs, out_specs, out_shape, args, nbytes = [], [], [], [], 0
    for p, a in zip(ps, landed):
        q = a.shape[1] // 2
        blk = (1, q, a.shape[2])
        in_specs.append(pl.BlockSpec(blk, lambda r, where: (where[0], r, 0)))
        args.append(p)
        for t in (1, 2, 3):
            in_specs.append(pl.BlockSpec(blk, lambda r, where, t=t: (where[t], r, 0)))
            args.append(a)
        out_specs.append(pl.BlockSpec(blk, lambda r, where: (where[4], r, 0)))
        out_shape.append(jax.ShapeDtypeStruct((2,) + a.shape[1:], F32))
        nbytes += 4 * _nbytes(blk, F32)
    return pl.pallas_call(
        body, name="grad_chip_sum", out_shape=tuple(out_shape),
        grid_spec=pltpu.PrefetchScalarGridSpec(num_scalar_prefetch=1, grid=(2,), in_specs=in_specs,
                                               out_specs=tuple(out_specs)),
        compiler_params=_params(("parallel",), nbytes),
    )(where, *args)


def _join_halves(ss):
    n = len(ss)

    def body(*refs):
        outs, send, recv = refs[n:2 * n], refs[2 * n], refs[2 * n + 1]
        x, y, c, _ = _place()
        cps = []
        for w in range(n):
            cps.append(pltpu.make_async_remote_copy(
                src_ref=outs[w].at[c], dst_ref=outs[w].at[c], send_sem=send.at[w], recv_sem=recv.at[w],
                device_id=(x, y, 1 - c), device_id_type=MESH))
        for cp in cps:
            cp.start()
        for w in range(n):
            got = outs[w].at[1 - c]
            pltpu.make_async_remote_copy(src_ref=got, dst_ref=got, send_sem=send.at[w], recv_sem=recv.at[w],
                                         device_id=(x, y, 1 - c), device_id_type=MESH).wait_recv()
        for cp in cps:
            cp.wait_send()

    dma = lambda k: pltpu.SemaphoreType.DMA((k,))
    return pl.pallas_call(
        body, name="grad_join_halves",
        out_shape=tuple(jax.ShapeDtypeStruct(s.shape, s.dtype) for s in ss),
        in_specs=[_ANY] * n, out_specs=tuple([_ANY] * n), input_output_aliases={w: w for w in range(n)},
        scratch_shapes=[dma(n), dma(n)],
    )(*ss)


def _rot_cols(w):
    a, b = jnp.split(w, 2, axis=-1)
    return jnp.concatenate([-b, a], axis=-1)


def _rot_cols_t(g):
    a, b = jnp.split(g, 2, axis=-1)
    return jnp.concatenate([b, -a], axis=-1)


def _cols_from_chips(a):
    n, r, cs = a.shape
    return jnp.transpose(a, (1, 0, 2)).reshape(r, n * cs)


def _cols_to_chips(a):
    r, cc = a.shape
    return jnp.transpose(a.reshape(r, N_CHIPS, cc // N_CHIPS), (1, 0, 2))


def _conv_w_split(cw):
    return jnp.swapaxes(cw.reshape(3, 2, D_FF), 0, 1)


def _conv_w_join(g):
    return jnp.swapaxes(g, 0, 1).reshape(3, 2 * D_FF)


_SEG =(D_MODEL, 2 * D_MODEL, 2 * D_MODEL + Q_RANK, 2 * D_MODEL + Q_RANK + KV_RANK, 2 * D_MODEL + Q_RANK + KV_RANK + ROPE,
        3 * D_MODEL + Q_RANK + KV_RANK + ROPE)


def _w_in_to_pad(w):
    u, v, cq, ckv, kr, ga, gb = jnp.split(w, _SEG, axis=1)
    return jnp.concatenate([u, v, ga, gb, cq, ckv, kr, _rot_cols(kr)], axis=1)


def _w_in_from_pad(g):
    u, v, ga, gb, cq, ckv, kr, krr = jnp.split(
        g, (D_MODEL, 2 * D_MODEL, 3 * D_MODEL, 4 * D_MODEL, 4 * D_MODEL + Q_RANK, 4 * D_MODEL + Q_RANK + KV_RANK,
            4 * D_MODEL + Q_RANK + KV_RANK + ROPE), axis=1)
    return jnp.concatenate([u, v, cq, ckv, kr + _rot_cols_t(krr), ga, gb], axis=1)


def _w_uq_to_pad(w):
    t = w.reshape(Q_RANK, HEADS, QK_DIM)
    nope, rope = t[..., :NOPE], t[..., NOPE:]
    return jnp.concatenate([nope, rope, _rot_cols(rope)], axis=-1).reshape(Q_RANK, HEADS * HEAD_PAD)


def _w_uq_from_pad(g):
    t = g.reshape(Q_RANK, HEADS, HEAD_PAD)
    nope, rope, rot = t[..., :NOPE], t[..., NOPE:QK_DIM], t[..., QK_DIM:]
    return jnp.concatenate([nope, rope + _rot_cols_t(rot)], axis=-1).reshape(Q_RANK, HEADS * QK_DIM)


def _w_ukv_to_pad(w):
    t = w.reshape(KV_RANK, HEADS, 2, NOPE)
    return jnp.swapaxes(t, 1, 2).reshape(KV_RANK, 2 * HEADS * NOPE)


def _w_ukv_from_pad(g):
    t = g.reshape(KV_RANK, 2, HEADS, NOPE)
    return jnp.swapaxes(t, 1, 2).reshape(KV_RANK, 2 * HEADS * NOPE)


def _rope_tables(positions):
    inv_freq = 1.0 / (ROPE_THETA ** (jnp.arange(0, ROPE, 2, dtype=F32) / ROPE))
    ang = positions.astype(F32).reshape(-1, 1) * inv_freq
    cos, sin = jnp.cos(ang), jnp.sin(ang)
    zero = jnp.zeros((ang.shape[0], 64), F32)
    return jnp.concatenate([cos, cos, zero], axis=1), jnp.concatenate([sin, sin, zero], axis=1)


_BIG = ("w_in", "w_uq", "w_ukv", "w_out", "w_up", "w_down")
UP_SHARD = 2 * D_FF // N_CHIPS


def _local_step(x, positions, tgt, wts, ffn_weights, on_ffn_grads, on_mixer_grads):
    B, S, D = x.shape
    T = B * S
    xf = x.reshape(T, D)
    cos_a, sin_a = _rope_tables(positions)
    bs_t = jnp.pad(wts["a_spatial_b"].T, ((0, 0), (0, 128 - A_GROUPS)))

    h = _rms_fwd(xf, wts["mix_norm"], "norm1_fwd")
    z = _mm(h, wts["w_in"], "nn", "in_proj", tm=512, tn=1536, tk=D)
    q, k, v, cqn, ckvn = _lat_fwd(z, wts["q_a_norm"], wts["kv_a_norm"], wts["w_q"], wts["w_kv"], cos_a, sin_a)
    yb, *lses = _attn_fwd(q, k, v, B, S)
    merged = _mix_fwd(z, yb, wts["a_v_norm_g"], wts["a_v_norm_b"], wts["a_spatial_w"], bs_t)
    x1 = _mm(merged, wts["w_out"], "nn", "out_proj", tm=512, tn=D, tk=D, add=xf)
    h2 = _rms_fwd(x1, wts["ffn_norm"], "norm2_fwd")
    wts = dict(wts)
    wts["w_up"], wts["w_down"], wts["conv_w"] = ffn_weights(h2)
    up_pre = _mm(h2, wts["w_up"], "nn", "up_proj", tm=512, tn=UP_SHARD, tk=D, dims=(T, 2 * D_FF, D),
                 b_spec=pl.BlockSpec((None, D, UP_SHARD), lambda i, j, k: (j, 0, 0)),
                 o_spec=pl.BlockSpec((None, 512, UP_SHARD), lambda i, j, k: (j // 2, i, j % 2)), out_shape=(2, T, D_FF))
    act = _gate_fwd(up_pre, wts["conv_w"], wts["conv_b"], B, S)
    x2 = _mm(act, wts["w_down"], "nn", "down_proj", tm=512, tn=D, tk=1408, add=x1)
    dx2, loss_row, g_final = _final(x2, tgt.reshape(T, D), wts["final_norm"])

    g = {"final_norm": g_final}
    dact = _mm(dx2, wts["w_down"], "nt", "down_proj_dx", tm=512, tn=1408, tk=D)
    g["w_down"] = _mm(act, dx2, "tn", "down_proj_dw", tm=1408, tn=D, tk=512)
    dup, g["conv_w"], g["conv_b"] = _gate_bwd(up_pre, dact, wts["conv_w"], wts["conv_b"], B, S)
    dh2 = _mm(dup, wts["w_up"], "nt", "up_proj_dx", tm=512, tn=D, tk=UP_SHARD, dims=(T, D, 2 * D_FF),
              a_spec=pl.BlockSpec((None, 512, UP_SHARD), lambda i, j, k: (k // 2, i, k % 2)),
              b_spec=pl.BlockSpec((None, D, UP_SHARD), lambda i, j, k: (k, 0, 0)))
    g["w_up"] = _mm(h2, dup, "tn", "up_proj_dw", tm=D, tn=UP_SHARD, tk=512, dims=(D, 2 * D_FF, T),
                    b_spec=pl.BlockSpec((None, 512, UP_SHARD), lambda i, j, k: (j // 2, k, j % 2)),
                    o_spec=pl.BlockSpec((None, D, UP_SHARD), lambda i, j, k: (j, 0, 0)), out_shape=(N_CHIPS, D, UP_SHARD))
    token = on_ffn_grads(g["w_up"], g["w_down"])
    ffn_norm = wts["ffn_norm"] if token is None else wts["ffn_norm"] + token[0:1, 0:1]
    dx1, g["ffn_norm"] = _rms_bwd(x1, ffn_norm, dh2, dx2, "norm2_bwd")
    dm = _mm(dx1, wts["w_out"], "nt", "out_proj_dx", tm=512, tn=D, tk=D)
    g["w_out"] = _mm(merged, dx1, "tn", "out_proj_dw", tm=D, tn=D, tk=512)
    dz, dyb, dl, g["a_spatial_w"], gbs, g["a_v_norm_g"], g["a_v_norm_b"] = _mix_bwd(
        z, yb, dm, wts["a_v_norm_g"], wts["a_v_norm_b"], wts["a_spatial_w"], bs_t)
    g["a_spatial_b"] = gbs[:, :A_GROUPS].T
    delta = dl.reshape(HEADS * T // ATT_BLOCK, 1, ATT_BLOCK)
    dq, dk, dv = _attn_bwd(q, k, v, dyb, lses, delta, B, S)
    dz, dq_raw, dkv, g["q_a_norm"], g["kv_a_norm"] = _lat_bwd(
        dz, z, dq, dk, dv, wts["q_a_norm"], wts["kv_a_norm"], wts["w_q"], wts["w_kv"], cos_a, sin_a)
    g["w_q"] = _mm(cqn, dq_raw, "tn", "q_proj_dw", tm=Q_RANK, tn=HEADS * HEAD_PAD, tk=512)
    g["w_kv"] = _mm(ckvn, dkv, "tn", "kv_proj_dw", tm=KV_RANK, tn=2 * HEADS * NOPE, tk=512)
    g["w_in"] = _mm(h, dz, "tn", "in_proj_dw", tm=D, tn=1536, tk=512)
    token = on_mixer_grads(g)
    mix_norm = wts["mix_norm"] if token is None else wts["mix_norm"] + token[0:1, 0:1]
    dh = _mm(dz, wts["w_in"], "nt", "in_proj_dx", tm=512, tn=D, tk=1536)
    dx, g["mix_norm"] = _rms_bwd(xf, mix_norm, dh, dx1, "norm1_bwd")
    return loss_row[0, 0], dx.reshape(B, S, D), g


_SMALL = (("mix_norm", (1, D_MODEL)), ("a_v_norm_g", (1, D_MODEL)), ("a_v_norm_b", (1, D_MODEL)),
          ("a_spatial_w", (A_GROUPS * CHUNK, CHUNK)), ("a_spatial_b", (1, A_GROUPS * CHUNK)), ("q_a_norm", (1, Q_RANK)),
          ("kv_a_norm", (1, KV_RANK)), ("ffn_norm", (1, D_MODEL)), ("conv_b", (1, 2 * D_FF)), ("final_norm", (1, D_MODEL)),
          ("conv_w", (3, 2 * D_FF)))
_SMALL_SIZE = sum(math.prod(s) for _, s in _SMALL)
_SMALL_ROWS = -(-(_SMALL_SIZE + 1) // (128 * 8)) * 8


def kernel(x, positions, mix_norm, w_in, a_v_norm_g, a_v_norm_b, a_spatial_w, a_spatial_b, q_a_norm, w_uq, kv_a_norm, w_ukv, w_out, ffn_norm, w_up, conv_w, conv_b, w_down, final_norm, loss_target, m_mix_norm, m_w_in, m_a_v_norm_g, m_a_v_norm_b, m_a_spatial_w, m_a_spatial_b, m_q_a_norm, m_w_uq, m_kv_a_norm, m_w_ukv, m_w_out, m_ffn_norm, m_w_up, m_conv_w, m_conv_b, m_w_down, m_final_norm, v_mix_norm, v_w_in, v_a_v_norm_g, v_a_v_norm_b, v_a_spatial_w, v_a_spatial_b, v_q_a_norm, v_w_uq, v_kv_a_norm, v_w_ukv, v_w_out, v_ffn_norm, v_w_up, v_conv_w, v_conv_b, v_w_down, v_final_norm):
    weights = dict(mix_norm=mix_norm, w_in=w_in, a_v_norm_g=a_v_norm_g, a_v_norm_b=a_v_norm_b, a_spatial_w=a_spatial_w,
                   a_spatial_b=a_spatial_b, q_a_norm=q_a_norm, w_uq=w_uq, kv_a_norm=kv_a_norm, w_ukv=w_ukv, w_out=w_out,
                   ffn_norm=ffn_norm, w_up=w_up, conv_w=conv_w, conv_b=conv_b, w_down=w_down, final_norm=final_norm)
    m_in = dict(mix_norm=m_mix_norm, w_in=m_w_in, a_v_norm_g=m_a_v_norm_g, a_v_norm_b=m_a_v_norm_b,
                a_spatial_w=m_a_spatial_w, a_spatial_b=m_a_spatial_b, q_a_norm=m_q_a_norm, w_uq=m_w_uq,
                kv_a_norm=m_kv_a_norm, w_ukv=m_w_ukv, w_out=m_w_out, ffn_norm=m_ffn_norm, w_up=m_w_up, conv_w=m_conv_w,
                conv_b=m_conv_b, w_down=m_w_down, final_norm=m_final_norm)
    v_in = dict(mix_norm=v_mix_norm, w_in=v_w_in, a_v_norm_g=v_a_v_norm_g, a_v_norm_b=v_a_v_norm_b,
                a_spatial_w=v_a_spatial_w, a_spatial_b=v_a_spatial_b, q_a_norm=v_q_a_norm, w_uq=v_w_uq,
                kv_a_norm=v_kv_a_norm, w_ukv=v_w_ukv, w_out=v_w_out, ffn_norm=v_ffn_norm, w_up=v_w_up, conv_w=v_conv_w,
                conv_b=v_conv_b, w_down=v_w_down, final_norm=v_final_norm)
    names = list(weights)
    chip = 2 * lax.axis_index("x") + lax.axis_index("y")

    def halves(a):
        return a.reshape(a.shape[:-2] + (2, a.shape[-2] // 2, a.shape[-1]))

    def whole(a):
        return a.reshape(a.shape[:-3] + (2 * a.shape[-2], a.shape[-1]))

    gathered = _gather_weights([halves(weights[n][0].astype(MXU_DTYPE)) for n in _BIG[:4]])
    w_in_sh, w_uq_sh, w_ukv_sh, w_out_sh = (whole(a) for a in gathered)
    ffn_gather = _exchange_start([w_up[0].astype(MXU_DTYPE), w_down[0].astype(MXU_DTYPE), conv_w[0]],
                                 "ffn_gather_start", "gather", after=gathered[3])
    wts = dict(
        mix_norm=mix_norm + ffn_gather[4][0:1, 0:1], a_v_norm_g=a_v_norm_g, a_v_norm_b=a_v_norm_b,
        a_spatial_w=a_spatial_w[0], a_spatial_b=a_spatial_b[0], q_a_norm=q_a_norm, kv_a_norm=kv_a_norm,
        ffn_norm=ffn_norm, final_norm=final_norm.reshape(1, D_MODEL),
        w_in=_w_in_to_pad(_cols_from_chips(w_in_sh)), w_q=_w_uq_to_pad(_cols_from_chips(w_uq_sh)),
        w_kv=_w_ukv_to_pad(_cols_from_chips(w_ukv_sh)), w_out=w_out_sh.reshape(D_MODEL, D_MODEL),
        conv_b=conv_b.reshape(2, 1, D_FF))

    def ffn_weights(after):
        _, (w_up_sh, w_down_sh, cw_all) = _exchange_wait(ffn_gather, "ffn_gather_wait", "gather", after)
        return w_up_sh, w_down_sh.reshape(D_FF, D_MODEL), _conv_w_split(_cols_from_chips(cw_all))

    scatters = {}

    def start_scatter(slabs, tag):
        sums = _pair_sum(slabs, _swap_halves([halves(s) for s in slabs], tag + "_grad_swap_halves"), tag + "_grad_pair_sum")
        scatters[tag] = _exchange_start(list(sums), tag + "_scatter_start", "scatter", after=slabs[-1])
        return scatters[tag][4]

    def on_ffn_grads(g_w_up, g_w_down):
        return start_scatter([g_w_up, g_w_down.reshape(N_CHIPS, D_FF // N_CHIPS, D_MODEL)], "ffn")

    def on_mixer_grads(g):
        return start_scatter(
            [_cols_to_chips(_w_in_from_pad(g["w_in"])), _cols_to_chips(_w_uq_from_pad(g["w_q"])),
             _cols_to_chips(_w_ukv_from_pad(g["w_kv"])), g["w_out"].reshape(N_CHIPS, D_MODEL // N_CHIPS, D_MODEL)], "mixer")

    loss_part, grad_x, g = _local_step(x, positions, loss_target, wts, ffn_weights, on_ffn_grads, on_mixer_grads)

    g_small_parts = dict(g)
    g_small_parts["conv_w"] = _conv_w_join(g["conv_w"])
    g_small_parts["conv_b"] = g["conv_b"].reshape(1, 2 * D_FF)
    flat = jnp.concatenate([g_small_parts[n].reshape(-1) for n, _ in _SMALL] + [loss_part.reshape(1)])
    flat = jnp.pad(flat, (0, _SMALL_ROWS * 128 - flat.shape[0])).reshape(_SMALL_ROWS, 128)
    small_gather = _exchange_start([flat], "small_gather_start", "all", after=grad_x)

    mixer_sums, mixer_landed = _exchange_wait(scatters["mixer"], "mixer_scatter_wait", "scatter", after=small_gather[4])
    ffn_sums, ffn_landed = _exchange_wait(scatters["ffn"], "ffn_scatter_wait", "scatter", after=mixer_landed[0])
    reduced = _chip_sum(list(mixer_sums) + list(ffn_sums), list(mixer_landed) + list(ffn_landed))
    g_big = dict(zip(_BIG, _join_halves(reduced)))

    grads, deltas, new_m, new_v = {}, {}, {}, {}

    def update(n, grad):
        w = weights[n]
        shape2 = grad.shape
        d, nm, nv = _adamw(w.reshape(shape2), grad, m_in[n].reshape(shape2), v_in[n].reshape(shape2), "adamw_" + n)
        grads[n], deltas[n], new_m[n], new_v[n] = (t.reshape(w.shape) for t in (grad, d, nm, nv))

    def update_transposed(n, grad):
        t = lambda a: jnp.swapaxes(a, 1, 2)
        d, nm, nv = _adamw(t(weights[n]), t(grad), t(m_in[n]), t(v_in[n]), "adamw_" + n)
        grads[n], deltas[n], new_m[n], new_v[n] = grad, t(d), t(nm), t(nv)

    for n in _BIG:
        g3 = g_big[n].reshape((1, -1, g_big[n].shape[-1]))
        if n == "w_in":
            update_transposed(n, g3)
        else:
            update(n, g3)

    (own,), (everyone,) = _exchange_wait(small_gather, "small_gather_wait", "all", after=deltas["w_up"])
    device = 2 * chip + lax.axis_index("c")
    everyone = lax.dynamic_update_slice(everyone, own[None], (device, 0, 0))
    total = _sum_slabs([everyone[j] for j in range(8)], "small_grads_sum", tr=_SMALL_ROWS).reshape(-1)
    o = 0
    for n, shp in _SMALL:
        piece = total[o:o + math.prod(shp)].reshape(shp)
        o += math.prod(shp)
        if n == "conv_w":
            piece = lax.dynamic_slice_in_dim(piece, chip * UP_SHARD, UP_SHARD, axis=1)
        update(n, piece)
    loss = total[_SMALL_SIZE]
    return (loss, grad_x, *[grads[n] for n in names], *[deltas[n] for n in names], *[new_m[n] for n in names],
            *[new_v[n] for n in names])
```

```python
import functools
import math

import jax
import jax.numpy as jnp
from jax import lax
from jax.experimental import pallas as pl
from jax.experimental.pallas import tpu as pltpu

F32 = jnp.float32
MXU_DTYPE = jnp.bfloat16
MESH = pl.DeviceIdType.MESH

D_MODEL = 1024
EPS = 1e-6
A_GROUPS = 8
CHUNK = 128
HEADS = 8
NOPE = 128
ROPE = 64
QK_DIM = NOPE + ROPE
HEAD_PAD = 256
Q_RANK = 256
KV_RANK = 128
ROPE_THETA = 10000.0
D_FF = 2816
FF_TILE = 256
N_FF_TILES = D_FF // FF_TILE
LAT = 512
IN_PAD = 4 * D_MODEL + LAT
N_CHIPS = 4
ADAM_LR, ADAM_B1, ADAM_B2, ADAM_EPS, ADAM_WD, ADAM_STEP = 0.001, 0.9, 0.999, 1e-08, 0.01, 10

VMEM_CAP_V7X = 64 * 1024 * 1024
NEG = -1e30


def _params(sem, nbytes):
    limit = int(min(VMEM_CAP_V7X - (8 << 20), max(32 << 20, 3 * nbytes)))
    return pltpu.CompilerParams(dimension_semantics=sem, vmem_limit_bytes=limit)


def _nbytes(shape, dtype):
    return math.prod(shape) * jnp.dtype(dtype).itemsize


_DIMS = {"nn": (((1,), (0,)), ((), ())), "nt": (((1,), (1,)), ((), ())), "tn": (((0,), (0,)), ((), ()))}


def _mm(a, b, mode, name, *, tm, tn, tk, out_dtype=F32, add=None, dims=None, a_spec=None, b_spec=None,
        o_spec=None, out_shape=None):
    if dims is None:
        if mode == "nn":
            (M, K), (_, N) = a.shape, b.shape
        elif mode == "nt":
            (M, K), (N, _) = a.shape, b.shape
        else:
            (K, M), (_, N) = a.shape, b.shape
    else:
        M, N, K = dims
    a_blk = (tk, tm) if mode == "tn" else (tm, tk)
    b_blk = (tn, tk) if mode == "nt" else (tk, tn)
    if a_spec is None:
        a_spec = pl.BlockSpec(a_blk, (lambda i, j, k: (k, i)) if mode == "tn" else (lambda i, j, k: (i, k)))
    if b_spec is None:
        b_spec = pl.BlockSpec(b_blk, (lambda i, j, k: (j, k)) if mode == "nt" else (lambda i, j, k: (k, j)))
    if o_spec is None:
        o_spec = pl.BlockSpec((tm, tn), lambda i, j, k: (i, j))
    if out_shape is None:
        out_shape = (M, N)
    assert M % tm == 0 and N % tn == 0 and K % tk == 0, (name, M, N, K, tm, tn, tk)
    nk = K // tk
    contract = _DIMS[mode]
    has_add = add is not None

    def body(*refs):
        a_ref, b_ref = refs[0], refs[1]
        add_ref = refs[2] if has_add else None
        o_ref = refs[3] if has_add else refs[2]
        part = lax.dot_general(a_ref[...].astype(MXU_DTYPE), b_ref[...].astype(MXU_DTYPE), contract,
                               preferred_element_type=F32)
        if nk == 1:
            o_ref[...] = (part + add_ref[...] if has_add else part).astype(out_dtype)
            return
        acc = refs[-1]
        k = pl.program_id(2)

        @pl.when(k == 0)
        def _():
            acc[...] = part

        @pl.when(k > 0)
        def _():
            acc[...] += part

        @pl.when(k == nk - 1)
        def _():
            r = acc[...]
            if has_add:
                r = r + add_ref[...]
            o_ref[...] = r.astype(out_dtype)

    in_specs = [a_spec, b_spec]
    args = [a, b]
    nbytes = _nbytes(a_blk, a.dtype) + _nbytes(b_blk, b.dtype) + 3 * _nbytes((tm, tn), F32)
    if has_add:
        in_specs.append(pl.BlockSpec((tm, tn), lambda i, j, k: (i, j)))
        args.append(add)
        nbytes += _nbytes((tm, tn), F32)
    return pl.pallas_call(
        body, name=name, out_shape=jax.ShapeDtypeStruct(out_shape, out_dtype),
        grid=(M // tm, N // tn, nk), in_specs=in_specs, out_specs=o_spec,
        scratch_shapes=[pltpu.VMEM((tm, tn), F32)] if nk > 1 else [],
        compiler_params=_params(("parallel", "parallel", "arbitrary"), nbytes),
    )(*args)


_GELU_C = math.sqrt(2.0 / math.pi)
_GELU_A = 0.044715


def _sigmoid(x):
    return 1.0 / (1.0 + jnp.exp(-x))


def _gelu(x):
    t = jnp.tanh(_GELU_C * (x + _GELU_A * (x * x * x)))
    return x * (0.5 * (1.0 + t))


def _gelu_and_grad(x):
    x2 = x * x
    t = jnp.tanh(_GELU_C * (x + _GELU_A * (x2 * x)))
    cdf = 0.5 * (1.0 + t)
    grad = cdf + 0.5 * x * (1.0 - t * t) * (_GELU_C * (1.0 + 3.0 * _GELU_A * x2))
    return x * cdf, grad


def _rope_mix(g, cos_a, sin_a):
    return g * cos_a + pltpu.roll(g, 64, 1) * sin_a


def _rope_mix_bwd(d, cos_a, sin_a):
    return d * cos_a + pltpu.roll(d * sin_a, 64, 1)


def _rms_fwd(x, g, name, tr=512):
    T, D = x.shape

    def body(x_ref, g_ref, h_ref):
        xv = x_ref[...]
        r = lax.rsqrt(jnp.mean(xv * xv, axis=-1, keepdims=True) + EPS)
        h_ref[...] = ((xv * r) * g_ref[...]).astype(h_ref.dtype)

    return pl.pallas_call(
        body, name=name, out_shape=jax.ShapeDtypeStruct((T, D), MXU_DTYPE), grid=(T // tr,),
        in_specs=[pl.BlockSpec((tr, D), lambda i: (i, 0)), pl.BlockSpec((1, D), lambda i: (0, 0))],
        out_specs=pl.BlockSpec((tr, D), lambda i: (i, 0)),
        compiler_params=_params(("parallel",), 3 * _nbytes((tr, D), F32)),
    )(x, g)


def _rms_bwd(x, g, dh, dres, name, tr=512):
    T, D = x.shape

    def body(x_ref, g_ref, dh_ref, dres_ref, dx_ref, gg_ref):
        @pl.when(pl.program_id(0) == 0)
        def _():
            gg_ref[...] = jnp.zeros_like(gg_ref)

        xv = x_ref[...]
        r = lax.rsqrt(jnp.mean(xv * xv, axis=-1, keepdims=True) + EPS)
        xn = xv * r
        dhv = dh_ref[...]
        dxn = dhv * g_ref[...]
        dx_ref[...] = dres_ref[...] + r * (dxn - xn * jnp.mean(dxn * xn, axis=-1, keepdims=True))
        gg_ref[...] += jnp.sum(dhv * xn, axis=0, keepdims=True)

    row = pl.BlockSpec((tr, D), lambda i: (i, 0))
    vec = pl.BlockSpec((1, D), lambda i: (0, 0))
    return pl.pallas_call(
        body, name=name,
        out_shape=(jax.ShapeDtypeStruct((T, D), F32), jax.ShapeDtypeStruct((1, D), F32)),
        grid=(T // tr,), in_specs=[row, vec, row, row], out_specs=(row, vec),
        compiler_params=_params(("arbitrary",), 6 * _nbytes((tr, D), F32)),
    )(x, g, dh, dres)


def _lat_fwd(z, gq, gkv, wq, wkv, cos_a, sin_a, tr=256):
    T = z.shape[0]
    lat_blk = (4 * D_MODEL) // LAT

    def body(z_ref, gq_ref, gkv_ref, wq_ref, wkv_ref, cos_ref, sin_ref, q_ref, k_ref, v_ref, cqn_ref, ckvn_ref):
        zl = z_ref[...]
        cos_v, sin_v = cos_ref[...], sin_ref[...]
        cq = zl[:, :Q_RANK]
        ckv = zl[:, Q_RANK:Q_RANK + KV_RANK]
        krb = zl[:, Q_RANK + KV_RANK:]
        cqn = ((cq * lax.rsqrt(jnp.mean(cq * cq, axis=-1, keepdims=True) + EPS)) * gq_ref[...]).astype(MXU_DTYPE)
        ckvn = ((ckv * lax.rsqrt(jnp.mean(ckv * ckv, axis=-1, keepdims=True) + EPS)) * gkv_ref[...]).astype(MXU_DTYPE)
        cqn_ref[...] = cqn
        ckvn_ref[...] = ckvn
        krr = _rope_mix(krb, cos_v, sin_v).astype(MXU_DTYPE)
        q = jnp.dot(cqn, wq_ref[...], preferred_element_type=F32)
        kv = jnp.dot(ckvn, wkv_ref[...], preferred_element_type=F32)
        for h in range(HEADS):
            o = h * HEAD_PAD
            q_ref[:, o:o + NOPE] = q[:, o:o + NOPE].astype(MXU_DTYPE)
            q_ref[:, o + NOPE:o + HEAD_PAD] = _rope_mix(q[:, o + NOPE:o + HEAD_PAD], cos_v, sin_v).astype(MXU_DTYPE)
            k_ref[:, o:o + NOPE] = kv[:, h * NOPE:(h + 1) * NOPE].astype(MXU_DTYPE)
            k_ref[:, o + NOPE:o + HEAD_PAD] = krr
        v_ref[...] = kv[:, HEADS * NOPE:].astype(MXU_DTYPE)

    def row(w):
        return pl.BlockSpec((tr, w), lambda i: (i, 0))

    def full(a):
        return pl.BlockSpec(a.shape, lambda i: (0, 0))

    return pl.pallas_call(
        body, name="lat_fwd",
        out_shape=(jax.ShapeDtypeStruct((T, HEADS * HEAD_PAD), MXU_DTYPE), jax.ShapeDtypeStruct((T, HEADS * HEAD_PAD), MXU_DTYPE),
                   jax.ShapeDtypeStruct((T, HEADS * NOPE), MXU_DTYPE), jax.ShapeDtypeStruct((T, Q_RANK), MXU_DTYPE),
                   jax.ShapeDtypeStruct((T, KV_RANK), MXU_DTYPE)),
        grid=(T // tr,),
        in_specs=[pl.BlockSpec((tr, LAT), lambda i: (i, lat_blk)), full(gq), full(gkv), full(wq), full(wkv), row(128), row(128)],
        out_specs=(row(HEADS * HEAD_PAD), row(HEADS * HEAD_PAD), row(HEADS * NOPE), row(Q_RANK), row(KV_RANK)),
        compiler_params=_params(("parallel",), 8 * _nbytes((tr, HEADS * HEAD_PAD), F32)),
    )(z, gq, gkv, wq, wkv, cos_a, sin_a)


ATT_BLOCK = 256
_SCALE = QK_DIM ** -0.5


def _causal_mask(n):
    return lax.broadcasted_iota(jnp.int32, (n, n), 1) <= lax.broadcasted_iota(jnp.int32, (n, n), 0)


def _causal_mask_t(n):
    return lax.broadcasted_iota(jnp.int32, (n, n), 0) <= lax.broadcasted_iota(jnp.int32, (n, n), 1)


ATT_HEADS = 4


def _attn_fwd(q, k, v, B, S):
    tq = ATT_BLOCK
    nq = S // tq
    T = B * S
    hp, groups = ATT_HEADS, HEADS // ATT_HEADS

    def body(q_ref, k_ref, v_ref, o_ref, *lse_refs):
        qi = pl.program_id(2)
        qs = [q_ref[:, t * HEAD_PAD:(t + 1) * HEAD_PAD] for t in range(hp)]

        def scores(j, t):
            rows = pl.ds(pl.multiple_of(j * tq, tq), tq)
            return lax.dot_general(k_ref[rows, t * HEAD_PAD:(t + 1) * HEAD_PAD], qs[t], _DIMS["nt"],
                                   preferred_element_type=F32)

        def step(j, carry, last):
            rows = pl.ds(pl.multiple_of(j * tq, tq), tq)
            out = []
            for t in range(hp):
                m, l, acc, st = carry[t]
                st_next = st if last else scores(j + 1, t)
                st = st * _SCALE
                if last:
                    st = jnp.where(_causal_mask_t(tq), st, NEG)
                m_new = jnp.maximum(m, jnp.max(st, axis=0, keepdims=True))
                alpha = jnp.exp(m - m_new)
                p = jnp.exp(st - m_new)
                l = alpha * l + jnp.sum(p, axis=0, keepdims=True)
                acc = alpha * acc + lax.dot_general(v_ref[rows, t * NOPE:(t + 1) * NOPE], p.astype(MXU_DTYPE),
                                                    _DIMS["tn"], preferred_element_type=F32)
                out.append((m_new, l, acc, st_next))
            return tuple(out)

        init = tuple((jnp.full((1, tq), NEG, F32), jnp.zeros((1, tq), F32), jnp.zeros((NOPE, tq), F32), scores(0, t))
                     for t in range(hp))
        carry = lax.fori_loop(0, qi, lambda j, c: step(j, c, False), init)
        carry = step(qi, carry, True)
        for t in range(hp):
            m, l, acc, _ = carry[t]
            o_ref[:, t * NOPE:(t + 1) * NOPE] = (acc / l).T
            lse_refs[t][0] = m + jnp.log(l)

    lse_sds = jax.ShapeDtypeStruct((groups * B * nq, 1, tq), F32)
    lse_spec = pl.BlockSpec((1, 1, tq), lambda b, h, i: ((h * B + b) * nq + i, 0, 0))
    return pl.pallas_call(
        body, name="attn_fwd",
        out_shape=(jax.ShapeDtypeStruct((T, HEADS * NOPE), F32),) + (lse_sds,) * hp,
        grid=(B, groups, nq),
        in_specs=[pl.BlockSpec((tq, hp * HEAD_PAD), lambda b, h, i: (b * nq + i, h)),
                  pl.BlockSpec((S, hp * HEAD_PAD), lambda b, h, i: (b, h)),
                  pl.BlockSpec((S, hp * NOPE), lambda b, h, i: (b, h))],
        out_specs=(pl.BlockSpec((tq, hp * NOPE), lambda b, h, i: (b * nq + i, h)),) + (lse_spec,) * hp,
        compiler_params=_params(("parallel", "parallel", "arbitrary"), 4 * hp * _nbytes((S, HEAD_PAD), MXU_DTYPE)),
    )(q, k, v)


def _attn_bwd(q, k, v, do, lses, delta, B, S):
    tq = ATT_BLOCK
    nq = S // tq
    T = B * S
    hp, groups = ATT_HEADS, HEADS // ATT_HEADS

    def body(q_ref, k_ref, v_ref, do_ref, *refs):
        lse_refs, dl_refs = refs[:hp], refs[hp:2 * hp]
        dq_ref, dk_ref, dv_ref = refs[2 * hp:]
        kj = pl.program_id(2)

        @pl.when(kj == 0)
        def _():
            dq_ref[...] = jnp.zeros_like(dq_ref)

        def products(i, t):
            rows = pl.ds(pl.multiple_of(i * tq, tq), tq)
            st = lax.dot_general(k_ref[:, t * HEAD_PAD:(t + 1) * HEAD_PAD], q_ref[rows, t * HEAD_PAD:(t + 1) * HEAD_PAD],
                                 _DIMS["nt"], preferred_element_type=F32)
            dpt = lax.dot_general(v_ref[:, t * NOPE:(t + 1) * NOPE], do_ref[rows, t * NOPE:(t + 1) * NOPE],
                                  _DIMS["nt"], preferred_element_type=F32)
            return st, dpt

        def step(i, carry, masked):
            rows = pl.ds(pl.multiple_of(i * tq, tq), tq)
            nxt = jnp.minimum(i + 1, nq - 1)
            out = []
            for t in range(hp):
                dk, dv, st, dpt = carry[t]
                st_next, dpt_next = products(nxt, t)
                qk_cols = slice(t * HEAD_PAD, (t + 1) * HEAD_PAD)
                v_cols = slice(t * NOPE, (t + 1) * NOPE)
                p = jnp.exp(st * _SCALE - lse_refs[t][i])
                if masked:
                    p = jnp.where(_causal_mask_t(tq), p, 0.0)
                dv = dv + jnp.dot(p.astype(MXU_DTYPE), do_ref[rows, v_cols], preferred_element_type=F32)
                ds = (p * (dpt - dl_refs[t][i]) * _SCALE).astype(MXU_DTYPE)
                dk = dk + jnp.dot(ds, q_ref[rows, qk_cols], preferred_element_type=F32)
                dq_ref[rows, qk_cols] += lax.dot_general(ds, k_ref[:, qk_cols], _DIMS["tn"], preferred_element_type=F32)
                out.append((dk, dv, st_next, dpt_next))
            return tuple(out)

        init = tuple((jnp.zeros((tq, HEAD_PAD), F32), jnp.zeros((tq, NOPE), F32)) + products(kj, t) for t in range(hp))
        carry = step(kj, init, True)
        carry = lax.fori_loop(kj + 1, nq, lambda i, c: step(i, c, False), carry)
        for t in range(hp):
            dk_ref[:, t * HEAD_PAD:(t + 1) * HEAD_PAD] = carry[t][0]
            dv_ref[:, t * NOPE:(t + 1) * NOPE] = carry[t][1].astype(dv_ref.dtype)

    seq = lambda w: pl.BlockSpec((S, w), lambda b, h, j: (b, h))
    blk = lambda w: pl.BlockSpec((tq, w), lambda b, h, j: (b * nq + j, h))
    lse_spec = pl.BlockSpec((nq, 1, tq), lambda b, h, j: (h * B + b, 0, 0))
    dl_specs = [pl.BlockSpec((nq, 1, tq), lambda b, h, j, t=t: ((h * hp + t) * B + b, 0, 0)) for t in range(hp)]
    return pl.pallas_call(
        body, name="attn_bwd",
        out_shape=(jax.ShapeDtypeStruct((T, HEADS * HEAD_PAD), F32), jax.ShapeDtypeStruct((T, HEADS * HEAD_PAD), F32),
                   jax.ShapeDtypeStruct((T, HEADS * NOPE), MXU_DTYPE)),
        grid=(B, groups, nq),
        in_specs=[seq(hp * HEAD_PAD), blk(hp * HEAD_PAD), blk(hp * NOPE), seq(hp * NOPE)] + [lse_spec] * hp + dl_specs,
        out_specs=(seq(hp * HEAD_PAD), blk(hp * HEAD_PAD), blk(hp * NOPE)),
        compiler_params=_params(("parallel", "parallel", "arbitrary"), 8 * hp * _nbytes((S, HEAD_PAD), F32)),
    )(q, k, v, do, *lses, *([delta] * hp))


MIX_ROWS = 256


def _tril_weights(ws_ref, g):
    return jnp.where(_causal_mask(CHUNK), ws_ref[g], 0.0).astype(MXU_DTYPE)


def _layer_norm_stats(va):
    mu = jnp.mean(va, axis=-1, keepdims=True)
    xc = va - mu
    rs = lax.rsqrt(jnp.mean(xc * xc, axis=-1, keepdims=True) + EPS)
    return xc * rs


def _mix_specs(tr):
    zcol = lambda c: pl.BlockSpec((tr, D_MODEL), lambda i, c=c: (i, c))
    row = pl.BlockSpec((tr, D_MODEL), lambda i: (i, 0))
    vec = pl.BlockSpec((1, D_MODEL), lambda i: (0, 0))
    ws = pl.BlockSpec((A_GROUPS, CHUNK, CHUNK), lambda i: (0, 0, 0))
    bs = pl.BlockSpec((CHUNK, 128), lambda i: (0, 0))
    return zcol, row, vec, ws, bs


def _mix_fwd(z, yb, ln_g, ln_b, ws, bs_t):
    T = z.shape[0]
    tr = MIX_ROWS
    zcol, row, vec, ws_spec, bs_spec = _mix_specs(tr)

    def body(zu_ref, zv_ref, zga_ref, zgb_ref, yb_ref, g_ref, b_ref, ws_ref, bs_ref, out_ref, vn_s):
        vhat = _layer_norm_stats(_gelu(zv_ref[...]))
        vn_s[...] = (vhat * g_ref[...] + b_ref[...]).astype(MXU_DTYPE)
        for g in range(A_GROUPS):
            w = _tril_weights(ws_ref, g)
            bias = bs_ref[:, g:g + 1]
            cols = slice(g * CHUNK, (g + 1) * CHUNK)
            for c in range(tr // CHUNK):
                rows = slice(c * CHUNK, (c + 1) * CHUNK)
                mixed = jnp.dot(w, vn_s[rows, cols], preferred_element_type=F32) + bias
                ya = _gelu(zu_ref[rows, cols]) * mixed
                merged = _sigmoid(zga_ref[rows, cols]) * ya + _sigmoid(zgb_ref[rows, cols]) * yb_ref[rows, cols]
                out_ref[rows, cols] = merged.astype(MXU_DTYPE)

    return pl.pallas_call(
        body, name="mix_fwd", out_shape=jax.ShapeDtypeStruct((T, D_MODEL), MXU_DTYPE), grid=(T // tr,),
        in_specs=[zcol(0), zcol(1), zcol(2), zcol(3), row, vec, vec, ws_spec, bs_spec], out_specs=row,
        scratch_shapes=[pltpu.VMEM((tr, D_MODEL), MXU_DTYPE)],
        compiler_params=_params(("parallel",), 8 * _nbytes((tr, D_MODEL), F32)),
    )(z, z, z, z, yb, ln_g, ln_b, ws, bs_t)


def _mix_bwd(z, yb, dm, ln_g, ln_b, ws, bs_t):
    T = z.shape[0]
    tr = MIX_ROWS
    zcol, row, vec, ws_spec, bs_spec = _mix_specs(tr)

    def body(zu_ref, zv_ref, zga_ref, zgb_ref, yb_ref, dm_ref, g_ref, b_ref, ws_ref, bs_ref,
             dz_ref, dyb_ref, dl_ref, gws_ref, gbs_ref, glg_ref, glb_ref, vn_s, dvn_s):
        @pl.when(pl.program_id(0) == 0)
        def _():
            gws_ref[...] = jnp.zeros_like(gws_ref)
            gbs_ref[...] = jnp.zeros_like(gbs_ref)
            glg_ref[...] = jnp.zeros_like(glg_ref)
            glb_ref[...] = jnp.zeros_like(glb_ref)

        lane = lax.broadcasted_iota(jnp.int32, (CHUNK, 128), 1)
        va, dgelu_v = _gelu_and_grad(zv_ref[...])
        mu = jnp.mean(va, axis=-1, keepdims=True)
        xc = va - mu
        rs = lax.rsqrt(jnp.mean(xc * xc, axis=-1, keepdims=True) + EPS)
        vhat = xc * rs
        vn_s[...] = (vhat * g_ref[...] + b_ref[...]).astype(MXU_DTYPE)
        gbs_acc = jnp.zeros((CHUNK, 128), F32)
        for g in range(A_GROUPS):
            w = _tril_weights(ws_ref, g)
            bias = bs_ref[:, g:g + 1]
            cols = slice(g * CHUNK, (g + 1) * CHUNK)
            gw_acc = jnp.zeros((CHUNK, CHUNK), F32)
            for c in range(tr // CHUNK):
                rows = slice(c * CHUNK, (c + 1) * CHUNK)
                vn = vn_s[rows, cols]
                mixed = jnp.dot(w, vn, preferred_element_type=F32) + bias
                ua, dgelu_u = _gelu_and_grad(zu_ref[rows, cols])
                dmv = dm_ref[rows, cols]
                sa = _sigmoid(zga_ref[rows, cols])
                dya = dmv * sa
                dz_ref[rows, 2 * D_MODEL + g * CHUNK:2 * D_MODEL + (g + 1) * CHUNK] = (
                    dmv * (ua * mixed) * (sa * (1.0 - sa))).astype(dz_ref.dtype)
                dz_ref[rows, cols] = (dya * mixed * dgelu_u).astype(dz_ref.dtype)
                dmix = dya * ua
                gbs_acc = gbs_acc + jnp.where(lane == g, jnp.sum(dmix, axis=-1, keepdims=True), 0.0)
                dmix_b = dmix.astype(MXU_DTYPE)
                gw_acc = gw_acc + lax.dot_general(dmix_b, vn, _DIMS["nt"], preferred_element_type=F32)
                dvn_s[rows, cols] = lax.dot_general(w, dmix_b, _DIMS["tn"], preferred_element_type=F32)
            gws_ref[g] += jnp.where(_causal_mask(CHUNK), gw_acc, 0.0)
        gbs_ref[...] += gbs_acc

        dvn = dvn_s[...]
        glg_ref[...] += jnp.sum(dvn * vhat, axis=0, keepdims=True)
        glb_ref[...] += jnp.sum(dvn, axis=0, keepdims=True)
        dvh = dvn * g_ref[...]
        dva = rs * (dvh - jnp.mean(dvh, axis=-1, keepdims=True) - vhat * jnp.mean(dvh * vhat, axis=-1, keepdims=True))
        dz_ref[:, D_MODEL:2 * D_MODEL] = (dva * dgelu_v).astype(dz_ref.dtype)

        dmv = dm_ref[...]
        ybv = yb_ref[...]
        sb = _sigmoid(zgb_ref[...])
        dyb = dmv * sb
        dyb_ref[...] = dyb.astype(dyb_ref.dtype)
        dz_ref[:, 3 * D_MODEL:4 * D_MODEL] = (dmv * ybv * (sb * (1.0 - sb))).astype(dz_ref.dtype)
        dz_ref[:, 4 * D_MODEL:] = jnp.zeros((tr, LAT), dz_ref.dtype)
        prod = dyb * ybv
        sel = (lax.broadcasted_iota(jnp.int32, (HEADS, D_MODEL), 1) // NOPE
               == lax.broadcasted_iota(jnp.int32, (HEADS, D_MODEL), 0)).astype(jnp.bfloat16)
        hi = prod.astype(jnp.bfloat16)
        rest = prod - hi.astype(F32)
        mid = rest.astype(jnp.bfloat16)
        lo = (rest - mid.astype(F32)).astype(jnp.bfloat16)
        dl_ref[...] = (lax.dot_general(sel, hi, _DIMS["nt"], preferred_element_type=F32)
                       + lax.dot_general(sel, mid, _DIMS["nt"], preferred_element_type=F32)
                       + lax.dot_general(sel, lo, _DIMS["nt"], preferred_element_type=F32))

    return pl.pallas_call(
        body, name="mix_bwd",
        out_shape=(jax.ShapeDtypeStruct((T, IN_PAD), MXU_DTYPE), jax.ShapeDtypeStruct((T, D_MODEL), MXU_DTYPE),
                   jax.ShapeDtypeStruct((HEADS, T), F32), jax.ShapeDtypeStruct((A_GROUPS, CHUNK, CHUNK), F32),
                   jax.ShapeDtypeStruct((CHUNK, 128), F32), jax.ShapeDtypeStruct((1, D_MODEL), F32),
                   jax.ShapeDtypeStruct((1, D_MODEL), F32)),
        grid=(T // tr,),
        in_specs=[zcol(0), zcol(1), zcol(2), zcol(3), row, row, vec, vec, ws_spec, bs_spec],
        out_specs=(pl.BlockSpec((tr, IN_PAD), lambda i: (i, 0)), row, pl.BlockSpec((HEADS, tr), lambda i: (0, i)),
                   ws_spec, bs_spec, vec, vec),
        scratch_shapes=[pltpu.VMEM((tr, D_MODEL), MXU_DTYPE), pltpu.VMEM((tr, D_MODEL), F32)],
        compiler_params=_params(("arbitrary",), 12 * _nbytes((tr, D_MODEL), F32)),
    )(z, z, z, z, yb, dm, ln_g, ln_b, ws, bs_t)


def _lat_bwd(dz, z, dq, dk, dv, gq, gkv, wq, wkv, cos_a, sin_a, tr=256):
    T = z.shape[0]
    lat_blk = (4 * D_MODEL) // LAT

    def body(dz_in, z_ref, dq_ref, dk_ref, dv_ref, gq_ref, gkv_ref, wq_ref, wkv_ref, cos_ref, sin_ref,
             dz_ref, dqr_ref, dkv_ref, ggq_ref, ggkv_ref):
        del dz_in

        @pl.when(pl.program_id(0) == 0)
        def _():
            ggq_ref[...] = jnp.zeros_like(ggq_ref)
            ggkv_ref[...] = jnp.zeros_like(ggkv_ref)

        cos_v, sin_v = cos_ref[...], sin_ref[...]
        dkr = jnp.zeros((tr, 128), F32)
        for h in range(HEADS):
            o = h * HEAD_PAD
            dqr_ref[:, o:o + NOPE] = dq_ref[:, o:o + NOPE].astype(MXU_DTYPE)
            dqr_ref[:, o + NOPE:o + HEAD_PAD] = _rope_mix_bwd(dq_ref[:, o + NOPE:o + HEAD_PAD], cos_v, sin_v).astype(MXU_DTYPE)
            dkv_ref[:, h * NOPE:(h + 1) * NOPE] = dk_ref[:, o:o + NOPE].astype(MXU_DTYPE)
            dkr = dkr + _rope_mix_bwd(dk_ref[:, o + NOPE:o + HEAD_PAD], cos_v, sin_v)
        dkv_ref[:, HEADS * NOPE:] = dv_ref[...]
        dcqn = lax.dot_general(dqr_ref[...], wq_ref[...], _DIMS["nt"], preferred_element_type=F32)
        dckvn = lax.dot_general(dkv_ref[...], wkv_ref[...], _DIMS["nt"], preferred_element_type=F32)

        zl = z_ref[...]

        def rms_bwd(c, dn, g_ref, gg_ref):
            r = lax.rsqrt(jnp.mean(c * c, axis=-1, keepdims=True) + EPS)
            ch = c * r
            gg_ref[...] += jnp.sum(dn * ch, axis=0, keepdims=True)
            dch = dn * g_ref[...]
            return r * (dch - ch * jnp.mean(dch * ch, axis=-1, keepdims=True))

        dz_ref[:, :Q_RANK] = rms_bwd(zl[:, :Q_RANK], dcqn, gq_ref, ggq_ref).astype(dz_ref.dtype)
        dz_ref[:, Q_RANK:Q_RANK + KV_RANK] = rms_bwd(zl[:, Q_RANK:Q_RANK + KV_RANK], dckvn, gkv_ref, ggkv_ref).astype(dz_ref.dtype)
        dz_ref[:, Q_RANK + KV_RANK:] = dkr.astype(dz_ref.dtype)

    def row(w):
        return pl.BlockSpec((tr, w), lambda i: (i, 0))

    def full(a):
        return pl.BlockSpec(a.shape, lambda i: (0, 0))

    lat = pl.BlockSpec((tr, LAT), lambda i: (i, lat_blk))
    return pl.pallas_call(
        body, name="lat_bwd",
        out_shape=(jax.ShapeDtypeStruct(dz.shape, dz.dtype), jax.ShapeDtypeStruct((T, HEADS * HEAD_PAD), MXU_DTYPE),
                   jax.ShapeDtypeStruct((T, 2 * HEADS * NOPE), MXU_DTYPE), jax.ShapeDtypeStruct(gq.shape, F32),
                   jax.ShapeDtypeStruct(gkv.shape, F32)),
        grid=(T // tr,),
        in_specs=[pl.BlockSpec(memory_space=pl.ANY), lat, row(HEADS * HEAD_PAD), row(HEADS * HEAD_PAD), row(HEADS * NOPE),
                  full(gq), full(gkv), full(wq), full(wkv), row(128), row(128)],
        out_specs=(lat, row(HEADS * HEAD_PAD), row(2 * HEADS * NOPE), full(gq), full(gkv)),
        input_output_aliases={0: 0},
        compiler_params=_params(("arbitrary",), 8 * _nbytes((tr, HEADS * HEAD_PAD), F32)),
    )(dz, z, dq, dk, dv, gq, gkv, wq, wkv, cos_a, sin_a)


GATE_ROWS = 64
HALO = 8


def _taps(ref, half, r, first):
    C = GATE_ROWS
    if first:
        xs = jnp.concatenate([jnp.zeros((HALO, ref.shape[-1]), F32), ref[half, 0:C, :]], axis=0)
    else:
        xs = ref[half, pl.ds(pl.multiple_of(r * C - HALO, HALO), C + HALO), :]
    return xs[HALO:, :], pltpu.roll(xs, 1, 0)[HALO:, :], pltpu.roll(xs, 2, 0)[HALO:, :]


def _conv_taps(taps, cw, cb):
    x0, x1, x2 = taps
    return cb + cw[0:1, :] * x2 + cw[1:2, :] * x1 + cw[2:3, :] * x0


def _fold8(x):
    acc = x[0:8, :]
    for i in range(1, x.shape[0] // 8):
        acc = acc + x[8 * i:8 * (i + 1), :]
    return acc


def _gate_fwd(up3, conv_w, conv_b, B, S):
    T = B * S
    W = FF_TILE
    C = GATE_ROWS

    def body(up_ref, cw_ref, cb_ref, act_ref):
        def chunk(r, first):
            gate = _conv_taps(_taps(up_ref, 0, r, first), cw_ref[0], cb_ref[0])
            val = _conv_taps(_taps(up_ref, 1, r, first), cw_ref[1], cb_ref[1])
            base = 0 if first else pl.multiple_of(r * C, C)
            act_ref[pl.ds(base, C), :] = (gate * _sigmoid(gate) * val).astype(act_ref.dtype)

        chunk(0, True)

        @pl.loop(1, S // C)
        def _(r):
            chunk(r, False)

    return pl.pallas_call(
        body, name="gate_fwd", out_shape=jax.ShapeDtypeStruct((T, D_FF), MXU_DTYPE), grid=(B, N_FF_TILES),
        in_specs=[pl.BlockSpec((2, S, W), lambda b, j: (0, b, j)), pl.BlockSpec((2, 3, W), lambda b, j: (0, 0, j)),
                  pl.BlockSpec((2, 1, W), lambda b, j: (0, 0, j))],
        out_specs=pl.BlockSpec((S, W), lambda b, j: (b, j)),
        compiler_params=_params(("parallel", "parallel"), 6 * _nbytes((S, W), F32)),
    )(up3, conv_w, conv_b)


def _gate_bwd(up3, dact, conv_w, conv_b, B, S):
    T = B * S
    W = FF_TILE
    C = GATE_ROWS

    def body(up_ref, da_ref, cw_ref, cb_ref, dup_ref, gcw_ref, gcb_ref, d_s):
        @pl.when(pl.program_id(1) == 0)
        def _():
            gcw_ref[...] = jnp.zeros_like(gcw_ref)
            gcb_ref[...] = jnp.zeros_like(gcb_ref)

        def chunk(r, first, sums):
            rows = pl.ds(0 if first else pl.multiple_of(r * C, C), C)
            taps = [_taps(up_ref, half, r, first) for half in (0, 1)]
            gate = _conv_taps(taps[0], cw_ref[0], cb_ref[0])
            val = _conv_taps(taps[1], cw_ref[1], cb_ref[1])
            sg = _sigmoid(gate)
            da = da_ref[rows, :]
            d_halves = (da * val * (sg * (1.0 + gate * (1.0 - sg))), da * (gate * sg))
            out = []
            for half, dup in enumerate(d_halves):
                d_s[half, rows, :] = dup
                x0, x1, x2 = taps[half]
                sb, s0, s1, s2 = sums[half]
                out.append((sb + _fold8(dup), s0 + _fold8(dup * x2), s1 + _fold8(dup * x1), s2 + _fold8(dup * x0)))
            return tuple(out)

        zeros = tuple(tuple(jnp.zeros((8, W), F32) for _ in range(4)) for _ in range(2))
        sums = chunk(0, True, zeros)
        sums = lax.fori_loop(1, S // C, lambda r, s: chunk(r, False, s), sums)
        for half in (0, 1):
            sb, s0, s1, s2 = sums[half]
            gcb_ref[half] += jnp.sum(sb, axis=0, keepdims=True)
            gcw_ref[half, 0:1, :] += jnp.sum(s0, axis=0, keepdims=True)
            gcw_ref[half, 1:2, :] += jnp.sum(s1, axis=0, keepdims=True)
            gcw_ref[half, 2:3, :] += jnp.sum(s2, axis=0, keepdims=True)

        d_s[:, S:S + HALO, :] = jnp.zeros((2, HALO, W), F32)

        @pl.loop(0, S // C)
        def _(r):
            base = pl.multiple_of(r * C, C)
            for half in (0, 1):
                ds_ = d_s[half, pl.ds(base, C + HALO), :]
                cw = cw_ref[half]
                dx = (cw[2:3, :] * ds_[:C, :] + cw[1:2, :] * pltpu.roll(ds_, C + HALO - 1, 0)[:C, :]
                      + cw[0:1, :] * pltpu.roll(ds_, C + HALO - 2, 0)[:C, :])
                dup_ref[half, pl.ds(base, C), :] = dx.astype(dup_ref.dtype)

    up_spec = pl.BlockSpec((2, S, W), lambda j, b: (0, b, j))
    cw_spec = pl.BlockSpec((2, 3, W), lambda j, b: (0, 0, j))
    cb_spec = pl.BlockSpec((2, 1, W), lambda j, b: (0, 0, j))
    return pl.pallas_call(
        body, name="gate_bwd",
        out_shape=(jax.ShapeDtypeStruct((2, T, D_FF), MXU_DTYPE), jax.ShapeDtypeStruct((2, 3, D_FF), F32),
                   jax.ShapeDtypeStruct((2, 1, D_FF), F32)),
        grid=(N_FF_TILES, B),
        in_specs=[up_spec, pl.BlockSpec((S, W), lambda j, b: (b, j)), cw_spec, cb_spec],
        out_specs=(up_spec, cw_spec, cb_spec),
        scratch_shapes=[pltpu.VMEM((2, S + HALO, W), F32)],
        compiler_params=_params(("parallel", "arbitrary"), 10 * _nbytes((S, W), F32)),
    )(up3, dact, conv_w, conv_b)


def _final(x2, tgt, g, tr=512):
    T, D = x2.shape

    def body(x_ref, t_ref, g_ref, dx_ref, loss_ref, gg_ref):
        @pl.when(pl.program_id(0) == 0)
        def _():
            loss_ref[...] = jnp.zeros_like(loss_ref)
            gg_ref[...] = jnp.zeros_like(gg_ref)

        xv = x_ref[...]
        gv = g_ref[...]
        r = lax.rsqrt(jnp.mean(xv * xv, axis=-1, keepdims=True) + EPS)
        xn = xv * r
        err = xn * gv - t_ref[...]
        loss_ref[...] += 0.5 * jnp.sum(jnp.mean(err * err, axis=-1, keepdims=True), axis=0, keepdims=True)
        dy = err * (1.0 / D)
        gg_ref[...] += jnp.sum(dy * xn, axis=0, keepdims=True)
        dxn = dy * gv
        dx_ref[...] = r * (dxn - xn * jnp.mean(dxn * xn, axis=-1, keepdims=True))

    row = pl.BlockSpec((tr, D), lambda i: (i, 0))
    vec = pl.BlockSpec((1, D), lambda i: (0, 0))
    return pl.pallas_call(
        body, name="final_loss",
        out_shape=(jax.ShapeDtypeStruct((T, D), F32), jax.ShapeDtypeStruct((1, 128), F32), jax.ShapeDtypeStruct((1, D), F32)),
        grid=(T // tr,), in_specs=[row, row, vec],
        out_specs=(row, pl.BlockSpec((1, 128), lambda i: (0, 0)), vec),
        compiler_params=_params(("arbitrary",), 6 * _nbytes((tr, D), F32)),
    )(x2, tgt, g)


def _sum_slabs(parts, name, tr):
    rows, cols = parts[0].shape
    n = len(parts)

    def body(*refs):
        acc = refs[0][...]
        for r in refs[1:n]:
            acc = acc + r[...]
        refs[n][...] = acc

    blk = pl.BlockSpec((tr, cols), lambda i: (i, 0))
    return pl.pallas_call(
        body, name=name, out_shape=jax.ShapeDtypeStruct((rows, cols), F32), grid=(rows // tr,),
        in_specs=[blk] * n, out_specs=blk,
        compiler_params=_params(("parallel",), (n + 1) * _nbytes((tr, cols), F32)),
    )(*parts)


ADAMW_BLOCK_BYTES = 2400 * 1024


def _adamw(w, g, m, v, name):
    lead = w.ndim == 3
    rows, cols = w.shape[-2:]
    fits = [d for d in range(8, rows + 1, 8) if rows % d == 0 and d * cols * 4 <= ADAMW_BLOCK_BYTES]
    tr = max(fits) if fits else rows
    c1 = 1.0 - ADAM_B1 ** ADAM_STEP
    c2 = 1.0 - ADAM_B2 ** ADAM_STEP

    def body(w_ref, g_ref, m_ref, v_ref, d_ref, nm_ref, nv_ref):
        gv = g_ref[...]
        nm = ADAM_B1 * m_ref[...] + (1.0 - ADAM_B1) * gv
        nv = ADAM_B2 * v_ref[...] + (1.0 - ADAM_B2) * (gv * gv)
        nm_ref[...] = nm
        nv_ref[...] = nv
        d_ref[...] = -ADAM_LR * ((nm / c1) / (jnp.sqrt(nv / c2) + ADAM_EPS) + ADAM_WD * w_ref[...])

    blk = pl.BlockSpec((None, tr, cols), lambda i: (0, i, 0)) if lead else pl.BlockSpec((tr, cols), lambda i: (i, 0))
    sds = jax.ShapeDtypeStruct(w.shape, F32)
    return pl.pallas_call(
        body, name=name, out_shape=(sds, sds, sds), grid=(rows // tr,), in_specs=[blk] * 4, out_specs=(blk, blk, blk),
        compiler_params=_params(("parallel",), 7 * _nbytes((tr, cols), F32)),
    )(w, g, m, v)


_ANY = pl.BlockSpec(memory_space=pl.ANY)


def _place():
    x, y, c = lax.axis_index("x"), lax.axis_index("y"), lax.axis_index("c")
    chips = [(1 - x, y), (x, 1 - y), (1 - x, 1 - y)]
    return x, y, c, chips


def _gather_weights(shards):
    n = len(shards)

    def body(*refs):
        ins, outs = refs[:n], refs[n:2 * n]
        send, recv, fsend, frecv, osend, orecv = refs[2 * n:]
        x, y, c, chips = _place()
        me = 2 * x + y
        first, passed = [], []
        for w in range(n):
            first.append(pltpu.make_async_remote_copy(
                src_ref=ins[w], dst_ref=outs[w].at[me], send_sem=osend.at[w], recv_sem=orecv.at[w],
                device_id=(x, y, 1 - c), device_id_type=MESH))
        for w in range(n):
            for j, (px, py) in enumerate(chips):
                first.append(pltpu.make_async_remote_copy(
                    src_ref=ins[w].at[c], dst_ref=outs[w].at[me, c], send_sem=send.at[3 * w + j],
                    recv_sem=recv.at[3 * w + j], device_id=(px, py, c), device_id_type=MESH))
        for cp in first:
            cp.start()
        for w in range(n):
            for j, (px, py) in enumerate(chips):
                landed = outs[w].at[2 * px + py, c]
                pltpu.make_async_remote_copy(src_ref=landed, dst_ref=landed, send_sem=send.at[3 * w + j],
                                             recv_sem=recv.at[3 * w + j], device_id=(px, py, c),
                                             device_id_type=MESH).wait_recv()
                fw = pltpu.make_async_remote_copy(src_ref=landed, dst_ref=landed, send_sem=fsend.at[3 * w + j],
                                                  recv_sem=frecv.at[3 * w + j], device_id=(x, y, 1 - c),
                                                  device_id_type=MESH)
                fw.start()
                passed.append(fw)
        for w in range(n):
            for j, (px, py) in enumerate(chips):
                other = outs[w].at[2 * px + py, 1 - c]
                pltpu.make_async_remote_copy(src_ref=other, dst_ref=other, send_sem=fsend.at[3 * w + j],
                                             recv_sem=frecv.at[3 * w + j], device_id=(x, y, 1 - c),
                                             device_id_type=MESH).wait_recv()
        for w in range(n):
            own = outs[w].at[me]
            pltpu.make_async_remote_copy(src_ref=own, dst_ref=own, send_sem=osend.at[w], recv_sem=orecv.at[w],
                                         device_id=(x, y, 1 - c), device_id_type=MESH).wait_recv()
        for cp in first + passed:
            cp.wait_send()

    dma = lambda k: pltpu.SemaphoreType.DMA((k,))
    return pl.pallas_call(
        body, name="gather_weights",
        out_shape=tuple(jax.ShapeDtypeStruct((N_CHIPS,) + s.shape, s.dtype) for s in shards),
        in_specs=[_ANY] * n, out_specs=tuple([_ANY] * n),
        scratch_shapes=[dma(3 * n), dma(3 * n), dma(3 * n), dma(3 * n), dma(n), dma(n)],
    )(*shards)


_HBM = pl.BlockSpec(memory_space=pltpu.HBM)
_SEM = pl.BlockSpec(memory_space=pltpu.SEMAPHORE)
_EFFECT = pltpu.SideEffectType.DATAFLOW_SIDE_EFFECTING


SEMS_PER_ARRAY = 8


def _exchange_copies(srcs, lands, send, recv, mode):
    x, y, c, chips = _place()
    if mode == "swap":
        return [pltpu.make_async_remote_copy(
            src_ref=src.at[:, 1 - c], dst_ref=land, send_sem=send.at[SEMS_PER_ARRAY * w],
            recv_sem=recv.at[SEMS_PER_ARRAY * w], device_id=(x, y, 1 - c), device_id_type=MESH)
            for w, (src, land) in enumerate(zip(srcs, lands))]
    if mode == "all":
        flips = [(fx, fy, fc) for fx in (0, 1) for fy in (0, 1) for fc in (0, 1)][1:]
        peers = [(x ^ fx, y ^ fy, c ^ fc) for fx, fy, fc in flips]
        slot = 4 * x + 2 * y + c
    else:
        peers = [(px, py, c) for px, py in chips] + ([(x, y, 1 - c)] if mode == "gather" else [])
        slot = 2 * x + y
    cps = []
    for w, (src, land) in enumerate(zip(srcs, lands)):
        for k, peer in enumerate(peers):
            piece = src.at[2 * peer[0] + peer[1]] if mode == "scatter" else src
            cps.append(pltpu.make_async_remote_copy(
                src_ref=piece, dst_ref=land.at[slot], send_sem=send.at[SEMS_PER_ARRAY * w + k],
                recv_sem=recv.at[SEMS_PER_ARRAY * w + k], device_id=peer, device_id_type=MESH))
    return cps


def _exchange_start(srcs, name, mode, after):
    n = len(srcs)
    if mode == "swap":
        land_shapes = [(s.shape[0],) + s.shape[2:] for s in srcs]
    else:
        lead = {"gather": (N_CHIPS,), "scatter": (), "all": (2 * N_CHIPS,)}[mode]
        land_shapes = [lead + s.shape for s in srcs]

    def body(*refs):
        src_refs, land_refs = refs[:n], refs[n:2 * n]
        send, recv = refs[2 * n + 1], refs[2 * n + 2]
        token = refs[-1]
        for cp in _exchange_copies(src_refs, land_refs, send, recv, mode):
            cp.start()
        token[...] = jnp.zeros_like(token)

    sems = pltpu.SemaphoreType.DMA((SEMS_PER_ARRAY * n,))
    out = pl.pallas_call(
        body, name=name,
        out_shape=(sems, sems, *[pltpu.HBM(s.shape, s.dtype) for s in srcs],
                   *[pltpu.HBM(shp, s.dtype) for shp, s in zip(land_shapes, srcs)], jax.ShapeDtypeStruct((8, 128), F32)),
        in_specs=[_HBM] * (2 * n) + [_ANY],
        out_specs=(_SEM, _SEM, *[_HBM] * (2 * n), pl.BlockSpec(memory_space=pltpu.VMEM)),
        input_output_aliases={i: 2 + i for i in range(2 * n)},
        compiler_params=pltpu.CompilerParams(has_side_effects=_EFFECT),
    )(*[pltpu.with_memory_space_constraint(s, pltpu.HBM) for s in srcs],
      *[pltpu.with_memory_space_constraint(lax.empty(shp, s.dtype), pltpu.HBM) for shp, s in zip(land_shapes, srcs)],
      after)
    return out[0], out[1], out[2:2 + n], out[2 + n:2 + 2 * n], out[-1]


def _exchange_wait(started, name, mode, after):
    send, recv, src_thru, land_thru, _ = started
    n = len(src_thru)

    def body(*refs):
        src_refs, land_refs, send_ref, recv_ref = refs[:n], refs[n:2 * n], refs[2 * n], refs[2 * n + 1]
        for cp in _exchange_copies(src_refs, land_refs, send_ref, recv_ref, mode):
            cp.wait_send()
            cp.wait_recv()

    out = pl.pallas_call(
        body, name=name,
        out_shape=tuple(pltpu.HBM(a.shape, a.dtype) for a in list(src_thru) + list(land_thru)),
        in_specs=[_HBM] * (2 * n) + [_SEM, _SEM, _ANY], out_specs=tuple([_HBM] * (2 * n)),
        input_output_aliases={i: i for i in range(2 * n)},
        compiler_params=pltpu.CompilerParams(has_side_effects=_EFFECT),
    )(*src_thru, *land_thru, send, recv, after)
    return out[:n], out[n:]


def _swap_halves(gs, name):
    n = len(gs)

    def body(*refs):
        ins, outs, send, recv = refs[:n], refs[n:2 * n], refs[2 * n], refs[2 * n + 1]
        x, y, c, _ = _place()
        cps = []
        for w in range(n):
            cps.append(pltpu.make_async_remote_copy(
                src_ref=ins[w].at[:, 1 - c], dst_ref=outs[w], send_sem=send.at[w], recv_sem=recv.at[w],
                device_id=(x, y, 1 - c), device_id_type=MESH))
        for cp in cps:
            cp.start()
        for cp in cps:
            cp.wait()

    return pl.pallas_call(
        body, name=name,
        out_shape=tuple(jax.ShapeDtypeStruct((g.shape[0],) + g.shape[2:], g.dtype) for g in gs),
        in_specs=[_ANY] * n, out_specs=tuple([_ANY] * n),
        scratch_shapes=[pltpu.SemaphoreType.DMA((n,)), pltpu.SemaphoreType.DMA((n,))],
    )(*gs)


GRAD_PAYLOAD = jnp.bfloat16


def _pair_sum(gs, gots, name):
    n = len(gs)
    core = lax.axis_index("c").astype(jnp.int32).reshape(1)

    def body(core_ref, *refs):
        del core_ref
        for w in range(n):
            refs[2 * n + w][...] = (refs[w][...] + refs[n + w][...]).astype(GRAD_PAYLOAD)

    in_specs, out_specs, out_shape, nbytes = [], [], [], 0
    for g in gs:
        q = g.shape[1] // 4
        in_specs.append(pl.BlockSpec((1, q, g.shape[2]), lambda s, r, core: (s, 2 * core[0] + r, 0)))
        nbytes += 3 * _nbytes((q, g.shape[2]), F32)
    for g in gs:
        q = g.shape[1] // 4
        in_specs.append(pl.BlockSpec((1, q, g.shape[2]), lambda s, r, core: (s, r, 0)))
        out_specs.append(pl.BlockSpec((1, q, g.shape[2]), lambda s, r, core: (s, r, 0)))
        out_shape.append(jax.ShapeDtypeStruct((g.shape[0], g.shape[1] // 2, g.shape[2]), GRAD_PAYLOAD))
    return pl.pallas_call(
        body, name=name, out_shape=tuple(out_shape),
        grid_spec=pltpu.PrefetchScalarGridSpec(num_scalar_prefetch=1, grid=(N_CHIPS, 2), in_specs=in_specs,
                                               out_specs=tuple(out_specs)),
        compiler_params=_params(("parallel", "parallel"), nbytes),
    )(core, *gs, *gots)


def _chip_sum(ps, landed):
    n = len(ps)
    x, y, c = lax.axis_index("x"), lax.axis_index("y"), lax.axis_index("c")
    where = jnp.stack([2 * x + y, 2 * (1 - x) + y, 2 * x + (1 - y), 2 * (1 - x) + (1 - y), c]).astype(jnp.int32)

    def body(where_ref, *refs):
        del where_ref
        for w in range(n):
            terms = [refs[4 * w + t][...].astype(F32) for t in range(4)]
            refs[4 * n + w][...] = ((terms[0] + terms[1]) + terms[2]) + terms[3]

    in_specs, out_specs, out_shape, args, nbytes = [], [], [], [], 0
    for p, a in zip(ps, landed):
        q = a.shape[1] // 2
        blk = (1, q, a.shape[2])
        in_specs.append(pl.BlockSpec(blk, lambda r, where: (where[0], r, 0)))
        args.append(p)
        for t in (1, 2, 3):
            in_specs.append(pl.BlockSpec(blk, lambda r, where, t=t: (where[t], r, 0)))
            args.append(a)
        out_specs.append(pl.BlockSpec(blk, lambda r, where: (where[4], r, 0)))
        out_shape.append(jax.ShapeDtypeStruct((2,) + a.shape[1:], F32))
        nbytes += 4 * _nbytes(blk, F32)
    return pl.pallas_call(
        body, name="grad_chip_sum", out_shape=tuple(out_shape),
        grid_spec=pltpu.PrefetchScalarGridSpec(num_scalar_prefetch=1, grid=(2,), in_specs=in_specs,
                                               out_specs=tuple(out_specs)),
        compiler_params=_params(("parallel",), nbytes),
    )(where, *args)


def _join_halves(ss):
    n = len(ss)

    def body(*refs):
        outs, send, recv = refs[n:2 * n], refs[2 * n], refs[2 * n + 1]
        x, y, c, _ = _place()
        cps = []
        for w in range(n):
            cps.append(pltpu.make_async_remote_copy(
                src_ref=outs[w].at[c], dst_ref=outs[w].at[c], send_sem=send.at[w], recv_sem=recv.at[w],
                device_id=(x, y, 1 - c), device_id_type=MESH))
        for cp in cps:
            cp.start()
        for w in range(n):
            got = outs[w].at[1 - c]
            pltpu.make_async_remote_copy(src_ref=got, dst_ref=got, send_sem=send.at[w], recv_sem=recv.at[w],
                                         device_id=(x, y, 1 - c), device_id_type=MESH).wait_recv()
        for cp in cps:
            cp.wait_send()

    dma = lambda k: pltpu.SemaphoreType.DMA((k,))
    return pl.pallas_call(
        body, name="grad_join_halves",
        out_shape=tuple(jax.ShapeDtypeStruct(s.shape, s.dtype) for s in ss),
        in_specs=[_ANY] * n, out_specs=tuple([_ANY] * n), input_output_aliases={w: w for w in range(n)},
        scratch_shapes=[dma(n), dma(n)],
    )(*ss)


def _rot_cols(w):
    a, b = jnp.split(w, 2, axis=-1)
    return jnp.concatenate([-b, a], axis=-1)


def _rot_cols_t(g):
    a, b = jnp.split(g, 2, axis=-1)
    return jnp.concatenate([b, -a], axis=-1)


def _cols_from_chips(a):
    n, r, cs = a.shape
    return jnp.transpose(a, (1, 0, 2)).reshape(r, n * cs)


def _cols_to_chips(a):
    r, cc = a.shape
    return jnp.transpose(a.reshape(r, N_CHIPS, cc // N_CHIPS), (1, 0, 2))


def _conv_w_split(cw):
    return jnp.swapaxes(cw.reshape(3, 2, D_FF), 0, 1)


def _conv_w_join(g):
    return jnp.swapaxes(g, 0, 1).reshape(3, 2 * D_FF)


_SEG =(D_MODEL, 2 * D_MODEL, 2 * D_MODEL + Q_RANK, 2 * D_MODEL + Q_RANK + KV_RANK, 2 * D_MODEL + Q_RANK + KV_RANK + ROPE,
        3 * D_MODEL + Q_RANK + KV_RANK + ROPE)


def _w_in_to_pad(w):
    u, v, cq, ckv, kr, ga, gb = jnp.split(w, _SEG, axis=1)
    return jnp.concatenate([u, v, ga, gb, cq, ckv, kr, _rot_cols(kr)], axis=1)


def _w_in_from_pad(g):
    u, v, ga, gb, cq, ckv, kr, krr = jnp.split(
        g, (D_MODEL, 2 * D_MODEL, 3 * D_MODEL, 4 * D_MODEL, 4 * D_MODEL + Q_RANK, 4 * D_MODEL + Q_RANK + KV_RANK,
            4 * D_MODEL + Q_RANK + KV_RANK + ROPE), axis=1)
    return jnp.concatenate([u, v, cq, ckv, kr + _rot_cols_t(krr), ga, gb], axis=1)


def _w_uq_to_pad(w):
    t = w.reshape(Q_RANK, HEADS, QK_DIM)
    nope, rope = t[..., :NOPE], t[..., NOPE:]
    return jnp.concatenate([nope, rope, _rot_cols(rope)], axis=-1).reshape(Q_RANK, HEADS * HEAD_PAD)


def _w_uq_from_pad(g):
    t = g.reshape(Q_RANK, HEADS, HEAD_PAD)
    nope, rope, rot = t[..., :NOPE], t[..., NOPE:QK_DIM], t[..., QK_DIM:]
    return jnp.concatenate([nope, rope + _rot_cols_t(rot)], axis=-1).reshape(Q_RANK, HEADS * QK_DIM)


def _w_ukv_to_pad(w):
    t = w.reshape(KV_RANK, HEADS, 2, NOPE)
    return jnp.swapaxes(t, 1, 2).reshape(KV_RANK, 2 * HEADS * NOPE)


def _w_ukv_from_pad(g):
    t = g.reshape(KV_RANK, 2, HEADS, NOPE)
    return jnp.swapaxes(t, 1, 2).reshape(KV_RANK, 2 * HEADS * NOPE)


def _rope_tables(positions):
    inv_freq = 1.0 / (ROPE_THETA ** (jnp.arange(0, ROPE, 2, dtype=F32) / ROPE))
    ang = positions.astype(F32).reshape(-1, 1) * inv_freq
    cos, sin = jnp.cos(ang), jnp.sin(ang)
    zero = jnp.zeros((ang.shape[0], 64), F32)
    return jnp.concatenate([cos, cos, zero], axis=1), jnp.concatenate([sin, sin, zero], axis=1)


_BIG = ("w_in", "w_uq", "w_ukv", "w_out", "w_up", "w_down")
UP_SHARD = 2 * D_FF // N_CHIPS


def _local_step(x, positions, tgt, wts, ffn_weights, on_ffn_grads, on_mixer_grads):
    B, S, D = x.shape
    T = B * S
    xf = x.reshape(T, D)
    cos_a, sin_a = _rope_tables(positions)
    bs_t = jnp.pad(wts["a_spatial_b"].T, ((0, 0), (0, 128 - A_GROUPS)))

    h = _rms_fwd(xf, wts["mix_norm"], "norm1_fwd")
    z = _mm(h, wts["w_in"], "nn", "in_proj", tm=512, tn=1536, tk=D)
    q, k, v, cqn, ckvn = _lat_fwd(z, wts["q_a_norm"], wts["kv_a_norm"], wts["w_q"], wts["w_kv"], cos_a, sin_a)
    yb, *lses = _attn_fwd(q, k, v, B, S)
    merged = _mix_fwd(z, yb, wts["a_v_norm_g"], wts["a_v_norm_b"], wts["a_spatial_w"], bs_t)
    x1 = _mm(merged, wts["w_out"], "nn", "out_proj", tm=512, tn=D, tk=D, add=xf)
    h2 = _rms_fwd(x1, wts["ffn_norm"], "norm2_fwd")
    wts = dict(wts)
    wts["w_up"], wts["w_down"], wts["conv_w"] = ffn_weights(h2)
    up_pre = _mm(h2, wts["w_up"], "nn", "up_proj", tm=512, tn=UP_SHARD, tk=D, dims=(T, 2 * D_FF, D),
                 b_spec=pl.BlockSpec((None, D, UP_SHARD), lambda i, j, k: (j, 0, 0)),
                 o_spec=pl.BlockSpec((None, 512, UP_SHARD), lambda i, j, k: (j // 2, i, j % 2)), out_shape=(2, T, D_FF))
    act = _gate_fwd(up_pre, wts["conv_w"], wts["conv_b"], B, S)
    x2 = _mm(act, wts["w_down"], "nn", "down_proj", tm=512, tn=D, tk=1408, add=x1)
    dx2, loss_row, g_final = _final(x2, tgt.reshape(T, D), wts["final_norm"])

    g = {"final_norm": g_final}
    dact = _mm(dx2, wts["w_down"], "nt", "down_proj_dx", tm=512, tn=1408, tk=D)
    tk2, tk1 = min(2048, T), min(1024, T)
    g["w_down"] = _mm(act, dx2, "tn", "down_proj_dw", tm=1408, tn=D, tk=tk1)
    dup, g["conv_w"], g["conv_b"] = _gate_bwd(up_pre, dact, wts["conv_w"], wts["conv_b"], B, S)
    g["w_up"] = _mm(h2, dup, "tn", "up_proj_dw", tm=D, tn=UP_SHARD, tk=tk2, dims=(D, 2 * D_FF, T),
                    b_spec=pl.BlockSpec((None, tk2, UP_SHARD), lambda i, j, k: (j // 2, k, j % 2)),
                    o_spec=pl.BlockSpec((None, D, UP_SHARD), lambda i, j, k: (j, 0, 0)), out_shape=(N_CHIPS, D, UP_SHARD))
    ffn_sent = on_ffn_grads(g["w_up"], g["w_down"])
    dh2 = _mm(dup, wts["w_up"], "nt", "up_proj_dx", tm=512, tn=D, tk=UP_SHARD, dims=(T, D, 2 * D_FF),
              a_spec=pl.BlockSpec((None, 512, UP_SHARD), lambda i, j, k: (k // 2, i, k % 2)),
              b_spec=pl.BlockSpec((None, D, UP_SHARD), lambda i, j, k: (k, 0, 0)))
    token = None if ffn_sent is None else ffn_sent(dh2)
    ffn_norm = wts["ffn_norm"] if token is None else wts["ffn_norm"] + token[0:1, 0:1]
    dx1, g["ffn_norm"] = _rms_bwd(x1, ffn_norm, dh2, dx2, "norm2_bwd")
    dm = _mm(dx1, wts["w_out"], "nt", "out_proj_dx", tm=512, tn=D, tk=D)
    g["w_out"] = _mm(merged, dx1, "tn", "out_proj_dw", tm=D, tn=D, tk=tk1)
    dz, dyb, dl, g["a_spatial_w"], gbs, g["a_v_norm_g"], g["a_v_norm_b"] = _mix_bwd(
        z, yb, dm, wts["a_v_norm_g"], wts["a_v_norm_b"], wts["a_spatial_w"], bs_t)
    g["a_spatial_b"] = gbs[:, :A_GROUPS].T
    delta = dl.reshape(HEADS * T // ATT_BLOCK, 1, ATT_BLOCK)
    dq, dk, dv = _attn_bwd(q, k, v, dyb, lses, delta, B, S)
    dz, dq_raw, dkv, g["q_a_norm"], g["kv_a_norm"] = _lat_bwd(
        dz, z, dq, dk, dv, wts["q_a_norm"], wts["kv_a_norm"], wts["w_q"], wts["w_kv"], cos_a, sin_a)
    g["w_q"] = _mm(cqn, dq_raw, "tn", "q_proj_dw", tm=Q_RANK, tn=HEADS * HEAD_PAD, tk=tk2)
    g["w_kv"] = _mm(ckvn, dkv, "tn", "kv_proj_dw", tm=KV_RANK, tn=2 * HEADS * NOPE, tk=tk2)
    g["w_in"] = _mm(h, dz, "tn", "in_proj_dw", tm=D, tn=1536, tk=tk2)
    token = on_mixer_grads(g)
    mix_norm = wts["mix_norm"] if token is None else wts["mix_norm"] + token[0:1, 0:1]
    dh = _mm(dz, wts["w_in"], "nt", "in_proj_dx", tm=512, tn=D, tk=1536)
    dx, g["mix_norm"] = _rms_bwd(xf, mix_norm, dh, dx1, "norm1_bwd")
    return loss_row[0, 0], dx.reshape(B, S, D), g


_SMALL = (("mix_norm", (1, D_MODEL)), ("a_v_norm_g", (1, D_MODEL)), ("a_v_norm_b", (1, D_MODEL)),
          ("a_spatial_w", (A_GROUPS * CHUNK, CHUNK)), ("a_spatial_b", (1, A_GROUPS * CHUNK)), ("q_a_norm", (1, Q_RANK)),
          ("kv_a_norm", (1, KV_RANK)), ("ffn_norm", (1, D_MODEL)), ("conv_b", (1, 2 * D_FF)), ("final_norm", (1, D_MODEL)),
          ("conv_w", (3, 2 * D_FF)))
_SMALL_SIZE = sum(math.prod(s) for _, s in _SMALL)
_SMALL_ROWS = -(-(_SMALL_SIZE + 1) // (128 * 8)) * 8


def kernel(x, positions, mix_norm, w_in, a_v_norm_g, a_v_norm_b, a_spatial_w, a_spatial_b, q_a_norm, w_uq, kv_a_norm, w_ukv, w_out, ffn_norm, w_up, conv_w, conv_b, w_down, final_norm, loss_target, m_mix_norm, m_w_in, m_a_v_norm_g, m_a_v_norm_b, m_a_spatial_w, m_a_spatial_b, m_q_a_norm, m_w_uq, m_kv_a_norm, m_w_ukv, m_w_out, m_ffn_norm, m_w_up, m_conv_w, m_conv_b, m_w_down, m_final_norm, v_mix_norm, v_w_in, v_a_v_norm_g, v_a_v_norm_b, v_a_spatial_w, v_a_spatial_b, v_q_a_norm, v_w_uq, v_kv_a_norm, v_w_ukv, v_w_out, v_ffn_norm, v_w_up, v_conv_w, v_conv_b, v_w_down, v_final_norm):
    weights = dict(mix_norm=mix_norm, w_in=w_in, a_v_norm_g=a_v_norm_g, a_v_norm_b=a_v_norm_b, a_spatial_w=a_spatial_w,
                   a_spatial_b=a_spatial_b, q_a_norm=q_a_norm, w_uq=w_uq, kv_a_norm=kv_a_norm, w_ukv=w_ukv, w_out=w_out,
                   ffn_norm=ffn_norm, w_up=w_up, conv_w=conv_w, conv_b=conv_b, w_down=w_down, final_norm=final_norm)
    m_in = dict(mix_norm=m_mix_norm, w_in=m_w_in, a_v_norm_g=m_a_v_norm_g, a_v_norm_b=m_a_v_norm_b,
                a_spatial_w=m_a_spatial_w, a_spatial_b=m_a_spatial_b, q_a_norm=m_q_a_norm, w_uq=m_w_uq,
                kv_a_norm=m_kv_a_norm, w_ukv=m_w_ukv, w_out=m_w_out, ffn_norm=m_ffn_norm, w_up=m_w_up, conv_w=m_conv_w,
                conv_b=m_conv_b, w_down=m_w_down, final_norm=m_final_norm)
    v_in = dict(mix_norm=v_mix_norm, w_in=v_w_in, a_v_norm_g=v_a_v_norm_g, a_v_norm_b=v_a_v_norm_b,
                a_spatial_w=v_a_spatial_w, a_spatial_b=v_a_spatial_b, q_a_norm=v_q_a_norm, w_uq=v_w_uq,
                kv_a_norm=v_kv_a_norm, w_ukv=v_w_ukv, w_out=v_w_out, ffn_norm=v_ffn_norm, w_up=v_w_up, conv_w=v_conv_w,
                conv_b=v_conv_b, w_down=v_w_down, final_norm=v_final_norm)
    names = list(weights)
    chip = 2 * lax.axis_index("x") + lax.axis_index("y")

    def halves(a):
        return a.reshape(a.shape[:-2] + (2, a.shape[-2] // 2, a.shape[-1]))

    def whole(a):
        return a.reshape(a.shape[:-3] + (2 * a.shape[-2], a.shape[-1]))

    gathered = _gather_weights([halves(weights[n][0].astype(MXU_DTYPE)) for n in _BIG[:4]])
    w_in_sh, w_uq_sh, w_ukv_sh, w_out_sh = (whole(a) for a in gathered)
    ffn_gather = _exchange_start([w_up[0].astype(MXU_DTYPE), w_down[0].astype(MXU_DTYPE), conv_w[0]],
                                 "ffn_gather_start", "gather", after=gathered[3])
    wts = dict(
        mix_norm=mix_norm + ffn_gather[4][0:1, 0:1], a_v_norm_g=a_v_norm_g, a_v_norm_b=a_v_norm_b,
        a_spatial_w=a_spatial_w[0], a_spatial_b=a_spatial_b[0], q_a_norm=q_a_norm, kv_a_norm=kv_a_norm,
        ffn_norm=ffn_norm, final_norm=final_norm.reshape(1, D_MODEL),
        w_in=_w_in_to_pad(_cols_from_chips(w_in_sh)), w_q=_w_uq_to_pad(_cols_from_chips(w_uq_sh)),
        w_kv=_w_ukv_to_pad(_cols_from_chips(w_ukv_sh)), w_out=w_out_sh.reshape(D_MODEL, D_MODEL),
        conv_b=conv_b.reshape(2, 1, D_FF))

    def ffn_weights(after):
        _, (w_up_sh, w_down_sh, cw_all) = _exchange_wait(ffn_gather, "ffn_gather_wait", "gather", after)
        return w_up_sh, w_down_sh.reshape(D_FF, D_MODEL), _conv_w_split(_cols_from_chips(cw_all))

    scatters = {}

    def start_scatter(slabs, tag):
        sums = _pair_sum(slabs, _swap_halves([halves(s) for s in slabs], tag + "_grad_swap_halves"), tag + "_grad_pair_sum")
        scatters[tag] = _exchange_start(list(sums), tag + "_scatter_start", "scatter", after=slabs[-1])
        return scatters[tag][4]

    def on_ffn_grads(g_w_up, g_w_down):
        slabs = [g_w_up, g_w_down.reshape(N_CHIPS, D_FF // N_CHIPS, D_MODEL)]
        swap = _exchange_start([halves(s) for s in slabs], "ffn_swap_start", "swap", after=slabs[1])

        def sent(after):
            views, gots = _exchange_wait(swap, "ffn_swap_wait", "swap", after)
            sums = _pair_sum([whole(v) for v in views], gots, "ffn_grad_pair_sum")
            scatters["ffn"] = _exchange_start(list(sums), "ffn_scatter_start", "scatter", after=gots[0])
            return scatters["ffn"][4]

        return sent

    def on_mixer_grads(g):
        return start_scatter(
            [_cols_to_chips(_w_in_from_pad(g["w_in"])), _cols_to_chips(_w_uq_from_pad(g["w_q"])),
             _cols_to_chips(_w_ukv_from_pad(g["w_kv"])), g["w_out"].reshape(N_CHIPS, D_MODEL // N_CHIPS, D_MODEL)], "mixer")

    loss_part, grad_x, g = _local_step(x, positions, loss_target, wts, ffn_weights, on_ffn_grads, on_mixer_grads)

    g_small_parts = dict(g)
    g_small_parts["conv_w"] = _conv_w_join(g["conv_w"])
    g_small_parts["conv_b"] = g["conv_b"].reshape(1, 2 * D_FF)
    flat = jnp.concatenate([g_small_parts[n].reshape(-1) for n, _ in _SMALL] + [loss_part.reshape(1)])
    flat = jnp.pad(flat, (0, _SMALL_ROWS * 128 - flat.shape[0])).reshape(_SMALL_ROWS, 128)
    small_gather = _exchange_start([flat], "small_gather_start", "all", after=grad_x)

    mixer_sums, mixer_landed = _exchange_wait(scatters["mixer"], "mixer_scatter_wait", "scatter", after=small_gather[4])
    ffn_sums, ffn_landed = _exchange_wait(scatters["ffn"], "ffn_scatter_wait", "scatter", after=mixer_landed[0])
    reduced = _chip_sum(list(mixer_sums) + list(ffn_sums), list(mixer_landed) + list(ffn_landed))
    g_big = dict(zip(_BIG, _join_halves(reduced)))

    grads, deltas, new_m, new_v = {}, {}, {}, {}

    def update(n, grad):
        w = weights[n]
        shape2 = grad.shape
        d, nm, nv = _adamw(w.reshape(shape2), grad, m_in[n].reshape(shape2), v_in[n].reshape(shape2), "adamw_" + n)
        grads[n], deltas[n], new_m[n], new_v[n] = (t.reshape(w.shape) for t in (grad, d, nm, nv))

    def update_transposed(n, grad):
        t = lambda a: jnp.swapaxes(a, 1, 2)
        d, nm, nv = _adamw(t(weights[n]), t(grad), t(m_in[n]), t(v_in[n]), "adamw_" + n)
        grads[n], deltas[n], new_m[n], new_v[n] = grad, t(d), t(nm), t(nv)

    for n in _BIG:
        g3 = g_big[n].reshape((1, -1, g_big[n].shape[-1]))
        if n == "w_in":
            update_transposed(n, g3)
        else:
            update(n, g3)

    (own,), (everyone,) = _exchange_wait(small_gather, "small_gather_wait", "all", after=deltas["w_up"])
    device = 2 * chip + lax.axis_index("c")
    everyone = lax.dynamic_update_slice(everyone, own[None], (device, 0, 0))
    total = _sum_slabs([everyone[j] for j in range(8)], "small_grads_sum", tr=_SMALL_ROWS).reshape(-1)
    o = 0
    for n, shp in _SMALL:
        piece = total[o:o + math.prod(shp)].reshape(shp)
        o += math.prod(shp)
        if n == "conv_w":
            piece = lax.dynamic_slice_in_dim(piece, chip * UP_SHARD, UP_SHARD, axis=1)
        update(n, piece)
    loss = total[_SMALL_SIZE]
    return (loss, grad_x, *[grads[n] for n in names], *[deltas[n] for n in names], *[new_m[n] for n in names],
            *[new_v[n] for n in names])
```

```python
import functools
import math

import jax
import jax.numpy as jnp
from jax import lax
from jax.experimental import pallas as pl
from jax.experimental.pallas import tpu as pltpu

F32 = jnp.float32
MXU_DTYPE = jnp.bfloat16
MESH = pl.DeviceIdType.MESH

D_MODEL = 1024
EPS = 1e-6
A_GROUPS = 8
CHUNK = 128
HEADS = 8
NOPE = 128
ROPE = 64
QK_DIM = NOPE + ROPE
HEAD_PAD = 256
Q_RANK = 256
KV_RANK = 128
ROPE_THETA = 10000.0
D_FF = 2816
FF_TILE = 256
N_FF_TILES = D_FF // FF_TILE
LAT = 512
IN_PAD = 4 * D_MODEL + LAT
N_CHIPS = 4
ADAM_LR, ADAM_B1, ADAM_B2, ADAM_EPS, ADAM_WD, ADAM_STEP = 0.001, 0.9, 0.999, 1e-08, 0.01, 10

VMEM_CAP_V7X = 64 * 1024 * 1024
NEG = -1e30


def _params(sem, nbytes):
    limit = int(min(VMEM_CAP_V7X - (8 << 20), max(32 << 20, 3 * nbytes)))
    return pltpu.CompilerParams(dimension_semantics=sem, vmem_limit_bytes=limit)


def _nbytes(shape, dtype):
    return math.prod(shape) * jnp.dtype(dtype).itemsize


_DIMS = {"nn": (((1,), (0,)), ((), ())), "nt": (((1,), (1,)), ((), ())), "tn": (((0,), (0,)), ((), ()))}


def _mm(a, b, mode, name, *, tm, tn, tk, out_dtype=F32, add=None, dims=None, a_spec=None, b_spec=None,
        o_spec=None, out_shape=None):
    if dims is None:
        if mode == "nn":
            (M, K), (_, N) = a.shape, b.shape
        elif mode == "nt":
            (M, K), (N, _) = a.shape, b.shape
        else:
            (K, M), (_, N) = a.shape, b.shape
    else:
        M, N, K = dims
    a_blk = (tk, tm) if mode == "tn" else (tm, tk)
    b_blk = (tn, tk) if mode == "nt" else (tk, tn)
    if a_spec is None:
        a_spec = pl.BlockSpec(a_blk, (lambda i, j, k: (k, i)) if mode == "tn" else (lambda i, j, k: (i, k)))
    if b_spec is None:
        b_spec = pl.BlockSpec(b_blk, (lambda i, j, k: (j, k)) if mode == "nt" else (lambda i, j, k: (k, j)))
    if o_spec is None:
        o_spec = pl.BlockSpec((tm, tn), lambda i, j, k: (i, j))
    if out_shape is None:
        out_shape = (M, N)
    assert M % tm == 0 and N % tn == 0 and K % tk == 0, (name, M, N, K, tm, tn, tk)
    nk = K // tk
    contract = _DIMS[mode]
    has_add = add is not None

    def body(*refs):
        a_ref, b_ref = refs[0], refs[1]
        add_ref = refs[2] if has_add else None
        o_ref = refs[3] if has_add else refs[2]

        def product():
            return lax.dot_general(a_ref[...].astype(MXU_DTYPE), b_ref[...].astype(MXU_DTYPE), contract,
                                   preferred_element_type=F32)

        if nk == 1:
            o_ref[...] = (product() + add_ref[...] if has_add else product()).astype(out_dtype)
            return
        acc = refs[-1]
        k = pl.program_id(2)

        @pl.when(k == 0)
        def _():
            acc[...] = jnp.zeros_like(acc)

        acc[...] += product()

        @pl.when(k == nk - 1)
        def _():
            r = acc[...]
            if has_add:
                r = r + add_ref[...]
            o_ref[...] = r.astype(out_dtype)

    in_specs = [a_spec, b_spec]
    args = [a, b]
    nbytes = _nbytes(a_blk, a.dtype) + _nbytes(b_blk, b.dtype) + 3 * _nbytes((tm, tn), F32)
    if has_add:
        in_specs.append(pl.BlockSpec((tm, tn), lambda i, j, k: (i, j)))
        args.append(add)
        nbytes += _nbytes((tm, tn), F32)
    return pl.pallas_call(
        body, name=name, out_shape=jax.ShapeDtypeStruct(out_shape, out_dtype),
        grid=(M // tm, N // tn, nk), in_specs=in_specs, out_specs=o_spec,
        scratch_shapes=[pltpu.VMEM((tm, tn), F32)] if nk > 1 else [],
        compiler_params=_params(("parallel", "parallel", "arbitrary"), nbytes),
    )(*args)


_GELU_C = math.sqrt(2.0 / math.pi)
_GELU_A = 0.044715


def _sigmoid(x):
    return 1.0 / (1.0 + jnp.exp(-x))


def _gelu(x):
    t = jnp.tanh(_GELU_C * (x + _GELU_A * (x * x * x)))
    return x * (0.5 * (1.0 + t))


def _gelu_and_grad(x):
    x2 = x * x
    t = jnp.tanh(_GELU_C * (x + _GELU_A * (x2 * x)))
    cdf = 0.5 * (1.0 + t)
    grad = cdf + 0.5 * x * (1.0 - t * t) * (_GELU_C * (1.0 + 3.0 * _GELU_A * x2))
    return x * cdf, grad


def _rope_mix(g, cos_a, sin_a):
    return g * cos_a + pltpu.roll(g, 64, 1) * sin_a


def _rope_mix_bwd(d, cos_a, sin_a):
    return d * cos_a + pltpu.roll(d * sin_a, 64, 1)


def _rms_fwd(x, g, name, tr=512):
    T, D = x.shape

    def body(x_ref, g_ref, h_ref):
        xv = x_ref[...]
        r = lax.rsqrt(jnp.mean(xv * xv, axis=-1, keepdims=True) + EPS)
        h_ref[...] = ((xv * r) * g_ref[...]).astype(h_ref.dtype)

    return pl.pallas_call(
        body, name=name, out_shape=jax.ShapeDtypeStruct((T, D), MXU_DTYPE), grid=(T // tr,),
        in_specs=[pl.BlockSpec((tr, D), lambda i: (i, 0)), pl.BlockSpec((1, D), lambda i: (0, 0))],
        out_specs=pl.BlockSpec((tr, D), lambda i: (i, 0)),
        compiler_params=_params(("parallel",), 3 * _nbytes((tr, D), F32)),
    )(x, g)


def _rms_bwd(x, g, dh, dres, name, tr=512):
    T, D = x.shape

    def body(x_ref, g_ref, dh_ref, dres_ref, dx_ref, gg_ref):
        @pl.when(pl.program_id(0) == 0)
        def _():
            gg_ref[...] = jnp.zeros_like(gg_ref)

        xv = x_ref[...]
        r = lax.rsqrt(jnp.mean(xv * xv, axis=-1, keepdims=True) + EPS)
        xn = xv * r
        dhv = dh_ref[...]
        dxn = dhv * g_ref[...]
        dx_ref[...] = dres_ref[...] + r * (dxn - xn * jnp.mean(dxn * xn, axis=-1, keepdims=True))
        gg_ref[...] += jnp.sum(dhv * xn, axis=0, keepdims=True)

    row = pl.BlockSpec((tr, D), lambda i: (i, 0))
    vec = pl.BlockSpec((1, D), lambda i: (0, 0))
    return pl.pallas_call(
        body, name=name,
        out_shape=(jax.ShapeDtypeStruct((T, D), F32), jax.ShapeDtypeStruct((1, D), F32)),
        grid=(T // tr,), in_specs=[row, vec, row, row], out_specs=(row, vec),
        compiler_params=_params(("arbitrary",), 6 * _nbytes((tr, D), F32)),
    )(x, g, dh, dres)


def _lat_fwd(z, gq, gkv, wq, wkv, cos_a, sin_a, tr=256):
    T = z.shape[0]
    lat_blk = (4 * D_MODEL) // LAT

    def body(z_ref, gq_ref, gkv_ref, wq_ref, wkv_ref, cos_ref, sin_ref, q_ref, k_ref, v_ref, cqn_ref, ckvn_ref):
        zl = z_ref[...]
        cos_v, sin_v = cos_ref[...], sin_ref[...]
        cq = zl[:, :Q_RANK]
        ckv = zl[:, Q_RANK:Q_RANK + KV_RANK]
        krb = zl[:, Q_RANK + KV_RANK:]
        cqn = ((cq * lax.rsqrt(jnp.mean(cq * cq, axis=-1, keepdims=True) + EPS)) * gq_ref[...]).astype(MXU_DTYPE)
        ckvn = ((ckv * lax.rsqrt(jnp.mean(ckv * ckv, axis=-1, keepdims=True) + EPS)) * gkv_ref[...]).astype(MXU_DTYPE)
        cqn_ref[...] = cqn
        ckvn_ref[...] = ckvn
        krr = _rope_mix(krb, cos_v, sin_v).astype(MXU_DTYPE)
        q = jnp.dot(cqn, wq_ref[...], preferred_element_type=F32)
        kv = jnp.dot(ckvn, wkv_ref[...], preferred_element_type=F32)
        for h in range(HEADS):
            o = h * HEAD_PAD
            q_ref[:, o:o + NOPE] = q[:, o:o + NOPE].astype(MXU_DTYPE)
            q_ref[:, o + NOPE:o + HEAD_PAD] = _rope_mix(q[:, o + NOPE:o + HEAD_PAD], cos_v, sin_v).astype(MXU_DTYPE)
            k_ref[:, o:o + NOPE] = kv[:, h * NOPE:(h + 1) * NOPE].astype(MXU_DTYPE)
            k_ref[:, o + NOPE:o + HEAD_PAD] = krr
        v_ref[...] = kv[:, HEADS * NOPE:].astype(MXU_DTYPE)

    def row(w):
        return pl.BlockSpec((tr, w), lambda i: (i, 0))

    def full(a):
        return pl.BlockSpec(a.shape, lambda i: (0, 0))

    return pl.pallas_call(
        body, name="lat_fwd",
        out_shape=(jax.ShapeDtypeStruct((T, HEADS * HEAD_PAD), MXU_DTYPE), jax.ShapeDtypeStruct((T, HEADS * HEAD_PAD), MXU_DTYPE),
                   jax.ShapeDtypeStruct((T, HEADS * NOPE), MXU_DTYPE), jax.ShapeDtypeStruct((T, Q_RANK), MXU_DTYPE),
                   jax.ShapeDtypeStruct((T, KV_RANK), MXU_DTYPE)),
        grid=(T // tr,),
        in_specs=[pl.BlockSpec((tr, LAT), lambda i: (i, lat_blk)), full(gq), full(gkv), full(wq), full(wkv), row(128), row(128)],
        out_specs=(row(HEADS * HEAD_PAD), row(HEADS * HEAD_PAD), row(HEADS * NOPE), row(Q_RANK), row(KV_RANK)),
        compiler_params=_params(("parallel",), 8 * _nbytes((tr, HEADS * HEAD_PAD), F32)),
    )(z, gq, gkv, wq, wkv, cos_a, sin_a)


ATT_BLOCK = 256
_SCALE = QK_DIM ** -0.5


def _causal_mask(n):
    return lax.broadcasted_iota(jnp.int32, (n, n), 1) <= lax.broadcasted_iota(jnp.int32, (n, n), 0)


def _causal_mask_t(n):
    return lax.broadcasted_iota(jnp.int32, (n, n), 0) <= lax.broadcasted_iota(jnp.int32, (n, n), 1)


ATT_HEADS = 4


def _attn_fwd(q, k, v, B, S):
    tq = ATT_BLOCK
    nq = S // tq
    T = B * S
    hp, groups = ATT_HEADS, HEADS // ATT_HEADS

    def body(q_ref, k_ref, v_ref, o_ref, *lse_refs):
        qi = pl.program_id(2)
        qs = [q_ref[:, t * HEAD_PAD:(t + 1) * HEAD_PAD] for t in range(hp)]

        def scores(j, t):
            rows = pl.ds(pl.multiple_of(j * tq, tq), tq)
            return lax.dot_general(k_ref[rows, t * HEAD_PAD:(t + 1) * HEAD_PAD], qs[t], _DIMS["nt"],
                                   preferred_element_type=F32)

        def step(j, carry, last):
            rows = pl.ds(pl.multiple_of(j * tq, tq), tq)
            out = []
            for t in range(hp):
                m, l, acc, st = carry[t]
                st_next = st if last else scores(j + 1, t)
                st = st * _SCALE
                if last:
                    st = jnp.where(_causal_mask_t(tq), st, NEG)
                m_new = jnp.maximum(m, jnp.max(st, axis=0, keepdims=True))
                alpha = jnp.exp(m - m_new)
                p = jnp.exp(st - m_new)
                l = alpha * l + jnp.sum(p, axis=0, keepdims=True)
                acc = alpha * acc + lax.dot_general(v_ref[rows, t * NOPE:(t + 1) * NOPE], p.astype(MXU_DTYPE),
                                                    _DIMS["tn"], preferred_element_type=F32)
                out.append((m_new, l, acc, st_next))
            return tuple(out)

        init = tuple((jnp.full((1, tq), NEG, F32), jnp.zeros((1, tq), F32), jnp.zeros((NOPE, tq), F32), scores(0, t))
                     for t in range(hp))
        carry = lax.fori_loop(0, qi, lambda j, c: step(j, c, False), init)
        carry = step(qi, carry, True)
        for t in range(hp):
            m, l, acc, _ = carry[t]
            o_ref[:, t * NOPE:(t + 1) * NOPE] = (acc / l).T
            lse_refs[t][0] = m + jnp.log(l)

    lse_sds = jax.ShapeDtypeStruct((groups * B * nq, 1, tq), F32)
    lse_spec = pl.BlockSpec((1, 1, tq), lambda b, h, i: ((h * B + b) * nq + i, 0, 0))
    return pl.pallas_call(
        body, name="attn_fwd",
        out_shape=(jax.ShapeDtypeStruct((T, HEADS * NOPE), F32),) + (lse_sds,) * hp,
        grid=(B, groups, nq),
        in_specs=[pl.BlockSpec((tq, hp * HEAD_PAD), lambda b, h, i: (b * nq + i, h)),
                  pl.BlockSpec((S, hp * HEAD_PAD), lambda b, h, i: (b, h)),
                  pl.BlockSpec((S, hp * NOPE), lambda b, h, i: (b, h))],
        out_specs=(pl.BlockSpec((tq, hp * NOPE), lambda b, h, i: (b * nq + i, h)),) + (lse_spec,) * hp,
        compiler_params=_params(("parallel", "parallel", "arbitrary"), 4 * hp * _nbytes((S, HEAD_PAD), MXU_DTYPE)),
    )(q, k, v)


def _attn_bwd(q, k, v, do, lses, delta, B, S):
    tq = ATT_BLOCK
    nq = S // tq
    T = B * S
    hp, groups = ATT_HEADS, HEADS // ATT_HEADS

    def body(q_ref, k_ref, v_ref, do_ref, *refs):
        lse_refs, dl_refs = refs[:hp], refs[hp:2 * hp]
        dq_ref, dk_ref, dv_ref = refs[2 * hp:]
        kj = pl.program_id(2)

        @pl.when(kj == 0)
        def _():
            dq_ref[...] = jnp.zeros_like(dq_ref)

        def products(i, t):
            rows = pl.ds(pl.multiple_of(i * tq, tq), tq)
            st = lax.dot_general(k_ref[:, t * HEAD_PAD:(t + 1) * HEAD_PAD], q_ref[rows, t * HEAD_PAD:(t + 1) * HEAD_PAD],
                                 _DIMS["nt"], preferred_element_type=F32)
            dpt = lax.dot_general(v_ref[:, t * NOPE:(t + 1) * NOPE], do_ref[rows, t * NOPE:(t + 1) * NOPE],
                                  _DIMS["nt"], preferred_element_type=F32)
            return st, dpt

        def step(i, carry, masked):
            rows = pl.ds(pl.multiple_of(i * tq, tq), tq)
            nxt = jnp.minimum(i + 1, nq - 1)
            out = []
            for t in range(hp):
                dk, dv, st, dpt = carry[t]
                st_next, dpt_next = products(nxt, t)
                qk_cols = slice(t * HEAD_PAD, (t + 1) * HEAD_PAD)
                v_cols = slice(t * NOPE, (t + 1) * NOPE)
                p = jnp.exp(st * _SCALE - lse_refs[t][i])
                if masked:
                    p = jnp.where(_causal_mask_t(tq), p, 0.0)
                dv = dv + jnp.dot(p.astype(MXU_DTYPE), do_ref[rows, v_cols], preferred_element_type=F32)
                ds = (p * (dpt - dl_refs[t][i]) * _SCALE).astype(MXU_DTYPE)
                dk = dk + jnp.dot(ds, q_ref[rows, qk_cols], preferred_element_type=F32)
                dq_ref[rows, qk_cols] += lax.dot_general(ds, k_ref[:, qk_cols], _DIMS["tn"], preferred_element_type=F32)
                out.append((dk, dv, st_next, dpt_next))
            return tuple(out)

        init = tuple((jnp.zeros((tq, HEAD_PAD), F32), jnp.zeros((tq, NOPE), F32)) + products(kj, t) for t in range(hp))
        carry = step(kj, init, True)
        carry = lax.fori_loop(kj + 1, nq, lambda i, c: step(i, c, False), carry)
        for t in range(hp):
            dk_ref[:, t * HEAD_PAD:(t + 1) * HEAD_PAD] = carry[t][0]
            dv_ref[:, t * NOPE:(t + 1) * NOPE] = carry[t][1].astype(dv_ref.dtype)

    seq = lambda w: pl.BlockSpec((S, w), lambda b, h, j: (b, h))
    blk = lambda w: pl.BlockSpec((tq, w), lambda b, h, j: (b * nq + j, h))
    lse_spec = pl.BlockSpec((nq, 1, tq), lambda b, h, j: (h * B + b, 0, 0))
    dl_specs = [pl.BlockSpec((nq, 1, tq), lambda b, h, j, t=t: ((h * hp + t) * B + b, 0, 0)) for t in range(hp)]
    return pl.pallas_call(
        body, name="attn_bwd",
        out_shape=(jax.ShapeDtypeStruct((T, HEADS * HEAD_PAD), F32), jax.ShapeDtypeStruct((T, HEADS * HEAD_PAD), F32),
                   jax.ShapeDtypeStruct((T, HEADS * NOPE), MXU_DTYPE)),
        grid=(B, groups, nq),
        in_specs=[seq(hp * HEAD_PAD), blk(hp * HEAD_PAD), blk(hp * NOPE), seq(hp * NOPE)] + [lse_spec] * hp + dl_specs,
        out_specs=(seq(hp * HEAD_PAD), blk(hp * HEAD_PAD), blk(hp * NOPE)),
        compiler_params=_params(("parallel", "parallel", "arbitrary"), 8 * hp * _nbytes((S, HEAD_PAD), F32)),
    )(q, k, v, do, *lses, *([delta] * hp))


MIX_ROWS = 256


def _tril_weights(ws_ref, g):
    return jnp.where(_causal_mask(CHUNK), ws_ref[g], 0.0).astype(MXU_DTYPE)


def _layer_norm_stats(va):
    mu = jnp.mean(va, axis=-1, keepdims=True)
    xc = va - mu
    rs = lax.rsqrt(jnp.mean(xc * xc, axis=-1, keepdims=True) + EPS)
    return xc * rs


def _mix_specs(tr):
    zcol = lambda c: pl.BlockSpec((tr, D_MODEL), lambda i, c=c: (i, c))
    row = pl.BlockSpec((tr, D_MODEL), lambda i: (i, 0))
    vec = pl.BlockSpec((1, D_MODEL), lambda i: (0, 0))
    ws = pl.BlockSpec((A_GROUPS, CHUNK, CHUNK), lambda i: (0, 0, 0))
    bs = pl.BlockSpec((CHUNK, 128), lambda i: (0, 0))
    return zcol, row, vec, ws, bs


def _mix_fwd(z, yb, ln_g, ln_b, ws, bs_t):
    T = z.shape[0]
    tr = MIX_ROWS
    zcol, row, vec, ws_spec, bs_spec = _mix_specs(tr)

    def body(zu_ref, zv_ref, zga_ref, zgb_ref, yb_ref, g_ref, b_ref, ws_ref, bs_ref, out_ref, vn_s):
        vhat = _layer_norm_stats(_gelu(zv_ref[...]))
        vn_s[...] = (vhat * g_ref[...] + b_ref[...]).astype(MXU_DTYPE)
        for g in range(A_GROUPS):
            w = _tril_weights(ws_ref, g)
            bias = bs_ref[:, g:g + 1]
            cols = slice(g * CHUNK, (g + 1) * CHUNK)
            for c in range(tr // CHUNK):
                rows = slice(c * CHUNK, (c + 1) * CHUNK)
                mixed = jnp.dot(w, vn_s[rows, cols], preferred_element_type=F32) + bias
                ya = _gelu(zu_ref[rows, cols]) * mixed
                merged = _sigmoid(zga_ref[rows, cols]) * ya + _sigmoid(zgb_ref[rows, cols]) * yb_ref[rows, cols]
                out_ref[rows, cols] = merged.astype(MXU_DTYPE)

    return pl.pallas_call(
        body, name="mix_fwd", out_shape=jax.ShapeDtypeStruct((T, D_MODEL), MXU_DTYPE), grid=(T // tr,),
        in_specs=[zcol(0), zcol(1), zcol(2), zcol(3), row, vec, vec, ws_spec, bs_spec], out_specs=row,
        scratch_shapes=[pltpu.VMEM((tr, D_MODEL), MXU_DTYPE)],
        compiler_params=_params(("parallel",), 8 * _nbytes((tr, D_MODEL), F32)),
    )(z, z, z, z, yb, ln_g, ln_b, ws, bs_t)


def _mix_bwd(z, yb, dm, ln_g, ln_b, ws, bs_t):
    T = z.shape[0]
    tr = MIX_ROWS
    zcol, row, vec, ws_spec, bs_spec = _mix_specs(tr)

    def body(zu_ref, zv_ref, zga_ref, zgb_ref, yb_ref, dm_ref, g_ref, b_ref, ws_ref, bs_ref,
             dz_ref, dyb_ref, dl_ref, gws_ref, gbs_ref, glg_ref, glb_ref, vn_s, dvn_s):
        @pl.when(pl.program_id(0) == 0)
        def _():
            gws_ref[...] = jnp.zeros_like(gws_ref)
            gbs_ref[...] = jnp.zeros_like(gbs_ref)
            glg_ref[...] = jnp.zeros_like(glg_ref)
            glb_ref[...] = jnp.zeros_like(glb_ref)

        lane = lax.broadcasted_iota(jnp.int32, (CHUNK, 128), 1)
        va, dgelu_v = _gelu_and_grad(zv_ref[...])
        mu = jnp.mean(va, axis=-1, keepdims=True)
        xc = va - mu
        rs = lax.rsqrt(jnp.mean(xc * xc, axis=-1, keepdims=True) + EPS)
        vhat = xc * rs
        vn_s[...] = (vhat * g_ref[...] + b_ref[...]).astype(MXU_DTYPE)
        gbs_acc = jnp.zeros((CHUNK, 128), F32)
        for g in range(A_GROUPS):
            w = _tril_weights(ws_ref, g)
            bias = bs_ref[:, g:g + 1]
            cols = slice(g * CHUNK, (g + 1) * CHUNK)
            gw_acc = jnp.zeros((CHUNK, CHUNK), F32)
            for c in range(tr // CHUNK):
                rows = slice(c * CHUNK, (c + 1) * CHUNK)
                vn = vn_s[rows, cols]
                mixed = jnp.dot(w, vn, preferred_element_type=F32) + bias
                ua, dgelu_u = _gelu_and_grad(zu_ref[rows, cols])
                dmv = dm_ref[rows, cols]
                sa = _sigmoid(zga_ref[rows, cols])
                dya = dmv * sa
                dz_ref[rows, 2 * D_MODEL + g * CHUNK:2 * D_MODEL + (g + 1) * CHUNK] = (
                    dmv * (ua * mixed) * (sa * (1.0 - sa))).astype(dz_ref.dtype)
                dz_ref[rows, cols] = (dya * mixed * dgelu_u).astype(dz_ref.dtype)
                dmix = dya * ua
                gbs_acc = gbs_acc + jnp.where(lane == g, jnp.sum(dmix, axis=-1, keepdims=True), 0.0)
                dmix_b = dmix.astype(MXU_DTYPE)
                gw_acc = gw_acc + lax.dot_general(dmix_b, vn, _DIMS["nt"], preferred_element_type=F32)
                dvn_s[rows, cols] = lax.dot_general(w, dmix_b, _DIMS["tn"], preferred_element_type=F32)
            gws_ref[g] += jnp.where(_causal_mask(CHUNK), gw_acc, 0.0)
        gbs_ref[...] += gbs_acc

        dvn = dvn_s[...]
        glg_ref[...] += jnp.sum(dvn * vhat, axis=0, keepdims=True)
        glb_ref[...] += jnp.sum(dvn, axis=0, keepdims=True)
        dvh = dvn * g_ref[...]
        dva = rs * (dvh - jnp.mean(dvh, axis=-1, keepdims=True) - vhat * jnp.mean(dvh * vhat, axis=-1, keepdims=True))
        dz_ref[:, D_MODEL:2 * D_MODEL] = (dva * dgelu_v).astype(dz_ref.dtype)

        dmv = dm_ref[...]
        ybv = yb_ref[...]
        sb = _sigmoid(zgb_ref[...])
        dyb = dmv * sb
        dyb_ref[...] = dyb.astype(dyb_ref.dtype)
        dz_ref[:, 3 * D_MODEL:4 * D_MODEL] = (dmv * ybv * (sb * (1.0 - sb))).astype(dz_ref.dtype)
        dz_ref[:, 4 * D_MODEL:] = jnp.zeros((tr, LAT), dz_ref.dtype)
        prod = dyb * ybv
        sel = (lax.broadcasted_iota(jnp.int32, (HEADS, D_MODEL), 1) // NOPE
               == lax.broadcasted_iota(jnp.int32, (HEADS, D_MODEL), 0)).astype(jnp.bfloat16)
        hi = prod.astype(jnp.bfloat16)
        rest = prod - hi.astype(F32)
        mid = rest.astype(jnp.bfloat16)
        lo = (rest - mid.astype(F32)).astype(jnp.bfloat16)
        dl_ref[...] = (lax.dot_general(sel, hi, _DIMS["nt"], preferred_element_type=F32)
                       + lax.dot_general(sel, mid, _DIMS["nt"], preferred_element_type=F32)
                       + lax.dot_general(sel, lo, _DIMS["nt"], preferred_element_type=F32))

    return pl.pallas_call(
        body, name="mix_bwd",
        out_shape=(jax.ShapeDtypeStruct((T, IN_PAD), MXU_DTYPE), jax.ShapeDtypeStruct((T, D_MODEL), MXU_DTYPE),
                   jax.ShapeDtypeStruct((HEADS, T), F32), jax.ShapeDtypeStruct((A_GROUPS, CHUNK, CHUNK), F32),
                   jax.ShapeDtypeStruct((CHUNK, 128), F32), jax.ShapeDtypeStruct((1, D_MODEL), F32),
                   jax.ShapeDtypeStruct((1, D_MODEL), F32)),
        grid=(T // tr,),
        in_specs=[zcol(0), zcol(1), zcol(2), zcol(3), row, row, vec, vec, ws_spec, bs_spec],
        out_specs=(pl.BlockSpec((tr, IN_PAD), lambda i: (i, 0)), row, pl.BlockSpec((HEADS, tr), lambda i: (0, i)),
                   ws_spec, bs_spec, vec, vec),
        scratch_shapes=[pltpu.VMEM((tr, D_MODEL), MXU_DTYPE), pltpu.VMEM((tr, D_MODEL), F32)],
        compiler_params=_params(("arbitrary",), 12 * _nbytes((tr, D_MODEL), F32)),
    )(z, z, z, z, yb, dm, ln_g, ln_b, ws, bs_t)


def _lat_bwd(dz, z, dq, dk, dv, gq, gkv, wq, wkv, cos_a, sin_a, tr=256):
    T = z.shape[0]
    lat_blk = (4 * D_MODEL) // LAT

    def body(dz_in, z_ref, dq_ref, dk_ref, dv_ref, gq_ref, gkv_ref, wq_ref, wkv_ref, cos_ref, sin_ref,
             dz_ref, dqr_ref, dkv_ref, ggq_ref, ggkv_ref):
        del dz_in

        @pl.when(pl.program_id(0) == 0)
        def _():
            ggq_ref[...] = jnp.zeros_like(ggq_ref)
            ggkv_ref[...] = jnp.zeros_like(ggkv_ref)

        cos_v, sin_v = cos_ref[...], sin_ref[...]
        dkr = jnp.zeros((tr, 128), F32)
        for h in range(HEADS):
            o = h * HEAD_PAD
            dqr_ref[:, o:o + NOPE] = dq_ref[:, o:o + NOPE].astype(MXU_DTYPE)
            dqr_ref[:, o + NOPE:o + HEAD_PAD] = _rope_mix_bwd(dq_ref[:, o + NOPE:o + HEAD_PAD], cos_v, sin_v).astype(MXU_DTYPE)
            dkv_ref[:, h * NOPE:(h + 1) * NOPE] = dk_ref[:, o:o + NOPE].astype(MXU_DTYPE)
            dkr = dkr + _rope_mix_bwd(dk_ref[:, o + NOPE:o + HEAD_PAD], cos_v, sin_v)
        dkv_ref[:, HEADS * NOPE:] = dv_ref[...]
        dcqn = lax.dot_general(dqr_ref[...], wq_ref[...], _DIMS["nt"], preferred_element_type=F32)
        dckvn = lax.dot_general(dkv_ref[...], wkv_ref[...], _DIMS["nt"], preferred_element_type=F32)

        zl = z_ref[...]

        def rms_bwd(c, dn, g_ref, gg_ref):
            r = lax.rsqrt(jnp.mean(c * c, axis=-1, keepdims=True) + EPS)
            ch = c * r
            gg_ref[...] += jnp.sum(dn * ch, axis=0, keepdims=True)
            dch = dn * g_ref[...]
            return r * (dch - ch * jnp.mean(dch * ch, axis=-1, keepdims=True))

        dz_ref[:, :Q_RANK] = rms_bwd(zl[:, :Q_RANK], dcqn, gq_ref, ggq_ref).astype(dz_ref.dtype)
        dz_ref[:, Q_RANK:Q_RANK + KV_RANK] = rms_bwd(zl[:, Q_RANK:Q_RANK + KV_RANK], dckvn, gkv_ref, ggkv_ref).astype(dz_ref.dtype)
        dz_ref[:, Q_RANK + KV_RANK:] = dkr.astype(dz_ref.dtype)

    def row(w):
        return pl.BlockSpec((tr, w), lambda i: (i, 0))

    def full(a):
        return pl.BlockSpec(a.shape, lambda i: (0, 0))

    lat = pl.BlockSpec((tr, LAT), lambda i: (i, lat_blk))
    return pl.pallas_call(
        body, name="lat_bwd",
        out_shape=(jax.ShapeDtypeStruct(dz.shape, dz.dtype), jax.ShapeDtypeStruct((T, HEADS * HEAD_PAD), MXU_DTYPE),
                   jax.ShapeDtypeStruct((T, 2 * HEADS * NOPE), MXU_DTYPE), jax.ShapeDtypeStruct(gq.shape, F32),
                   jax.ShapeDtypeStruct(gkv.shape, F32)),
        grid=(T // tr,),
        in_specs=[pl.BlockSpec(memory_space=pl.ANY), lat, row(HEADS * HEAD_PAD), row(HEADS * HEAD_PAD), row(HEADS * NOPE),
                  full(gq), full(gkv), full(wq), full(wkv), row(128), row(128)],
        out_specs=(lat, row(HEADS * HEAD_PAD), row(2 * HEADS * NOPE), full(gq), full(gkv)),
        input_output_aliases={0: 0},
        compiler_params=_params(("arbitrary",), 8 * _nbytes((tr, HEADS * HEAD_PAD), F32)),
    )(dz, z, dq, dk, dv, gq, gkv, wq, wkv, cos_a, sin_a)


GATE_ROWS = 64
HALO = 8


def _taps(ref, half, r, first):
    C = GATE_ROWS
    if first:
        xs = jnp.concatenate([jnp.zeros((HALO, ref.shape[-1]), F32), ref[half, 0:C, :]], axis=0)
    else:
        xs = ref[half, pl.ds(pl.multiple_of(r * C - HALO, HALO), C + HALO), :]
    return xs[HALO:, :], pltpu.roll(xs, 1, 0)[HALO:, :], pltpu.roll(xs, 2, 0)[HALO:, :]


def _conv_taps(taps, cw, cb):
    x0, x1, x2 = taps
    return cb + cw[0:1, :] * x2 + cw[1:2, :] * x1 + cw[2:3, :] * x0


def _fold8(x):
    acc = x[0:8, :]
    for i in range(1, x.shape[0] // 8):
        acc = acc + x[8 * i:8 * (i + 1), :]
    return acc


def _gate_fwd(up3, conv_w, conv_b, B, S):
    T = B * S
    W = FF_TILE
    C = GATE_ROWS

    def body(up_ref, cw_ref, cb_ref, act_ref):
        def chunk(r, first):
            gate = _conv_taps(_taps(up_ref, 0, r, first), cw_ref[0], cb_ref[0])
            val = _conv_taps(_taps(up_ref, 1, r, first), cw_ref[1], cb_ref[1])
            base = 0 if first else pl.multiple_of(r * C, C)
            act_ref[pl.ds(base, C), :] = (gate * _sigmoid(gate) * val).astype(act_ref.dtype)

        chunk(0, True)

        @pl.loop(1, S // C)
        def _(r):
            chunk(r, False)

    return pl.pallas_call(
        body, name="gate_fwd", out_shape=jax.ShapeDtypeStruct((T, D_FF), MXU_DTYPE), grid=(B, N_FF_TILES),
        in_specs=[pl.BlockSpec((2, S, W), lambda b, j: (0, b, j)), pl.BlockSpec((2, 3, W), lambda b, j: (0, 0, j)),
                  pl.BlockSpec((2, 1, W), lambda b, j: (0, 0, j))],
        out_specs=pl.BlockSpec((S, W), lambda b, j: (b, j)),
        compiler_params=_params(("parallel", "parallel"), 6 * _nbytes((S, W), F32)),
    )(up3, conv_w, conv_b)


def _gate_bwd(up3, dact, conv_w, conv_b, B, S):
    T = B * S
    W = FF_TILE
    C = GATE_ROWS

    def body(up_ref, da_ref, cw_ref, cb_ref, dup_ref, gcw_ref, gcb_ref, d_s):
        @pl.when(pl.program_id(1) == 0)
        def _():
            gcw_ref[...] = jnp.zeros_like(gcw_ref)
            gcb_ref[...] = jnp.zeros_like(gcb_ref)

        def chunk(r, first, sums):
            rows = pl.ds(0 if first else pl.multiple_of(r * C, C), C)
            taps = [_taps(up_ref, half, r, first) for half in (0, 1)]
            gate = _conv_taps(taps[0], cw_ref[0], cb_ref[0])
            val = _conv_taps(taps[1], cw_ref[1], cb_ref[1])
            sg = _sigmoid(gate)
            da = da_ref[rows, :]
            d_halves = (da * val * (sg * (1.0 + gate * (1.0 - sg))), da * (gate * sg))
            out = []
            for half, dup in enumerate(d_halves):
                d_s[half, rows, :] = dup
                x0, x1, x2 = taps[half]
                sb, s0, s1, s2 = sums[half]
                out.append((sb + _fold8(dup), s0 + _fold8(dup * x2), s1 + _fold8(dup * x1), s2 + _fold8(dup * x0)))
            return tuple(out)

        zeros = tuple(tuple(jnp.zeros((8, W), F32) for _ in range(4)) for _ in range(2))
        sums = chunk(0, True, zeros)
        sums = lax.fori_loop(1, S // C, lambda r, s: chunk(r, False, s), sums)
        for half in (0, 1):
            sb, s0, s1, s2 = sums[half]
            gcb_ref[half] += jnp.sum(sb, axis=0, keepdims=True)
            gcw_ref[half, 0:1, :] += jnp.sum(s0, axis=0, keepdims=True)
            gcw_ref[half, 1:2, :] += jnp.sum(s1, axis=0, keepdims=True)
            gcw_ref[half, 2:3, :] += jnp.sum(s2, axis=0, keepdims=True)

        d_s[:, S:S + HALO, :] = jnp.zeros((2, HALO, W), F32)

        @pl.loop(0, S // C)
        def _(r):
            base = pl.multiple_of(r * C, C)
            for half in (0, 1):
                ds_ = d_s[half, pl.ds(base, C + HALO), :]
                cw = cw_ref[half]
                dx = (cw[2:3, :] * ds_[:C, :] + cw[1:2, :] * pltpu.roll(ds_, C + HALO - 1, 0)[:C, :]
                      + cw[0:1, :] * pltpu.roll(ds_, C + HALO - 2, 0)[:C, :])
                dup_ref[half, pl.ds(base, C), :] = dx.astype(dup_ref.dtype)

    up_spec = pl.BlockSpec((2, S, W), lambda j, b: (0, b, j))
    cw_spec = pl.BlockSpec((2, 3, W), lambda j, b: (0, 0, j))
    cb_spec = pl.BlockSpec((2, 1, W), lambda j, b: (0, 0, j))
    return pl.pallas_call(
        body, name="gate_bwd",
        out_shape=(jax.ShapeDtypeStruct((2, T, D_FF), MXU_DTYPE), jax.ShapeDtypeStruct((2, 3, D_FF), F32),
                   jax.ShapeDtypeStruct((2, 1, D_FF), F32)),
        grid=(N_FF_TILES, B),
        in_specs=[up_spec, pl.BlockSpec((S, W), lambda j, b: (b, j)), cw_spec, cb_spec],
        out_specs=(up_spec, cw_spec, cb_spec),
        scratch_shapes=[pltpu.VMEM((2, S + HALO, W), F32)],
        compiler_params=_params(("parallel", "arbitrary"), 10 * _nbytes((S, W), F32)),
    )(up3, dact, conv_w, conv_b)


def _final(x2, tgt, g, tr=512):
    T, D = x2.shape

    def body(x_ref, t_ref, g_ref, dx_ref, loss_ref, gg_ref):
        @pl.when(pl.program_id(0) == 0)
        def _():
            loss_ref[...] = jnp.zeros_like(loss_ref)
            gg_ref[...] = jnp.zeros_like(gg_ref)

        xv = x_ref[...]
        gv = g_ref[...]
        r = lax.rsqrt(jnp.mean(xv * xv, axis=-1, keepdims=True) + EPS)
        xn = xv * r
        err = xn * gv - t_ref[...]
        loss_ref[...] += 0.5 * jnp.sum(jnp.mean(err * err, axis=-1, keepdims=True), axis=0, keepdims=True)
        dy = err * (1.0 / D)
        gg_ref[...] += jnp.sum(dy * xn, axis=0, keepdims=True)
        dxn = dy * gv
        dx_ref[...] = r * (dxn - xn * jnp.mean(dxn * xn, axis=-1, keepdims=True))

    row = pl.BlockSpec((tr, D), lambda i: (i, 0))
    vec = pl.BlockSpec((1, D), lambda i: (0, 0))
    return pl.pallas_call(
        body, name="final_loss",
        out_shape=(jax.ShapeDtypeStruct((T, D), F32), jax.ShapeDtypeStruct((1, 128), F32), jax.ShapeDtypeStruct((1, D), F32)),
        grid=(T // tr,), in_specs=[row, row, vec],
        out_specs=(row, pl.BlockSpec((1, 128), lambda i: (0, 0)), vec),
        compiler_params=_params(("arbitrary",), 6 * _nbytes((tr, D), F32)),
    )(x2, tgt, g)


def _sum_slabs(parts, name, tr):
    rows, cols = parts[0].shape
    n = len(parts)

    def body(*refs):
        acc = refs[0][...]
        for r in refs[1:n]:
            acc = acc + r[...]
        refs[n][...] = acc

    blk = pl.BlockSpec((tr, cols), lambda i: (i, 0))
    return pl.pallas_call(
        body, name=name, out_shape=jax.ShapeDtypeStruct((rows, cols), F32), grid=(rows // tr,),
        in_specs=[blk] * n, out_specs=blk,
        compiler_params=_params(("parallel",), (n + 1) * _nbytes((tr, cols), F32)),
    )(*parts)


ADAMW_BLOCK_BYTES = 2400 * 1024


def _adamw(w, g, m, v, name):
    lead = w.ndim == 3
    rows, cols = w.shape[-2:]
    fits = [d for d in range(8, rows + 1, 8) if rows % d == 0 and d * cols * 4 <= ADAMW_BLOCK_BYTES]
    tr = max(fits) if fits else rows
    c1 = 1.0 - ADAM_B1 ** ADAM_STEP
    c2 = 1.0 - ADAM_B2 ** ADAM_STEP

    def body(w_ref, g_ref, m_ref, v_ref, d_ref, nm_ref, nv_ref):
        gv = g_ref[...]
        nm = ADAM_B1 * m_ref[...] + (1.0 - ADAM_B1) * gv
        nv = ADAM_B2 * v_ref[...] + (1.0 - ADAM_B2) * (gv * gv)
        nm_ref[...] = nm
        nv_ref[...] = nv
        d_ref[...] = -ADAM_LR * ((nm / c1) / (jnp.sqrt(nv / c2) + ADAM_EPS) + ADAM_WD * w_ref[...])

    blk = pl.BlockSpec((None, tr, cols), lambda i: (0, i, 0)) if lead else pl.BlockSpec((tr, cols), lambda i: (i, 0))
    sds = jax.ShapeDtypeStruct(w.shape, F32)
    return pl.pallas_call(
        body, name=name, out_shape=(sds, sds, sds), grid=(rows // tr,), in_specs=[blk] * 4, out_specs=(blk, blk, blk),
        compiler_params=_params(("parallel",), 7 * _nbytes((tr, cols), F32)),
    )(w, g, m, v)


_ANY = pl.BlockSpec(memory_space=pl.ANY)


def _place():
    x, y, c = lax.axis_index("x"), lax.axis_index("y"), lax.axis_index("c")
    chips = [(1 - x, y), (x, 1 - y), (1 - x, 1 - y)]
    return x, y, c, chips


def _gather_weights(shards):
    n = len(shards)

    def body(*refs):
        ins, outs = refs[:n], refs[n:2 * n]
        send, recv, fsend, frecv, osend, orecv = refs[2 * n:]
        x, y, c, chips = _place()
        me = 2 * x + y
        first, passed = [], []
        for w in range(n):
            first.append(pltpu.make_async_remote_copy(
                src_ref=ins[w], dst_ref=outs[w].at[me], send_sem=osend.at[w], recv_sem=orecv.at[w],
                device_id=(x, y, 1 - c), device_id_type=MESH))
        for w in range(n):
            for j, (px, py) in enumerate(chips):
                first.append(pltpu.make_async_remote_copy(
                    src_ref=ins[w].at[c], dst_ref=outs[w].at[me, c], send_sem=send.at[3 * w + j],
                    recv_sem=recv.at[3 * w + j], device_id=(px, py, c), device_id_type=MESH))
        for cp in first:
            cp.start()
        for w in range(n):
            for j, (px, py) in enumerate(chips):
                landed = outs[w].at[2 * px + py, c]
                pltpu.make_async_remote_copy(src_ref=landed, dst_ref=landed, send_sem=send.at[3 * w + j],
                                             recv_sem=recv.at[3 * w + j], device_id=(px, py, c),
                                             device_id_type=MESH).wait_recv()
                fw = pltpu.make_async_remote_copy(src_ref=landed, dst_ref=landed, send_sem=fsend.at[3 * w + j],
                                                  recv_sem=frecv.at[3 * w + j], device_id=(x, y, 1 - c),
                                                  device_id_type=MESH)
                fw.start()
                passed.append(fw)
        for w in range(n):
            for j, (px, py) in enumerate(chips):
                other = outs[w].at[2 * px + py, 1 - c]
                pltpu.make_async_remote_copy(src_ref=other, dst_ref=other, send_sem=fsend.at[3 * w + j],
                                             recv_sem=frecv.at[3 * w + j], device_id=(x, y, 1 - c),
                                             device_id_type=MESH).wait_recv()
        for w in range(n):
            own = outs[w].at[me]
            pltpu.make_async_remote_copy(src_ref=own, dst_ref=own, send_sem=osend.at[w], recv_sem=orecv.at[w],
                                         device_id=(x, y, 1 - c), device_id_type=MESH).wait_recv()
        for cp in first + passed:
            cp.wait_send()

    dma = lambda k: pltpu.SemaphoreType.DMA((k,))
    return pl.pallas_call(
        body, name="gather_weights",
        out_shape=tuple(jax.ShapeDtypeStruct((N_CHIPS,) + s.shape, s.dtype) for s in shards),
        in_specs=[_ANY] * n, out_specs=tuple([_ANY] * n),
        scratch_shapes=[dma(3 * n), dma(3 * n), dma(3 * n), dma(3 * n), dma(n), dma(n)],
    )(*shards)


_HBM = pl.BlockSpec(memory_space=pltpu.HBM)
_SEM = pl.BlockSpec(memory_space=pltpu.SEMAPHORE)
_EFFECT = pltpu.SideEffectType.DATAFLOW_SIDE_EFFECTING


SEMS_PER_ARRAY = 8


def _exchange_copies(srcs, lands, send, recv, mode):
    x, y, c, chips = _place()
    if mode == "all":
        flips = [(fx, fy, fc) for fx in (0, 1) for fy in (0, 1) for fc in (0, 1)][1:]
        peers = [(x ^ fx, y ^ fy, c ^ fc) for fx, fy, fc in flips]
        slot = 4 * x + 2 * y + c
    else:
        peers = [(px, py, c) for px, py in chips] + ([(x, y, 1 - c)] if mode == "gather" else [])
        slot = 2 * x + y
    cps = []
    for w, (src, land) in enumerate(zip(srcs, lands)):
        for k, peer in enumerate(peers):
            piece = src.at[2 * peer[0] + peer[1]] if mode == "scatter" else src
            cps.append(pltpu.make_async_remote_copy(
                src_ref=piece, dst_ref=land.at[slot], send_sem=send.at[SEMS_PER_ARRAY * w + k],
                recv_sem=recv.at[SEMS_PER_ARRAY * w + k], device_id=peer, device_id_type=MESH))
    return cps


def _exchange_start(srcs, name, mode, after):
    n = len(srcs)
    lead = {"gather": (N_CHIPS,), "scatter": (), "all": (2 * N_CHIPS,)}[mode]
    land_shapes = [lead + s.shape for s in srcs]

    def body(*refs):
        src_refs, land_refs = refs[:n], refs[n:2 * n]
        send, recv = refs[2 * n + 1], refs[2 * n + 2]
        token = refs[-1]
        for cp in _exchange_copies(src_refs, land_refs, send, recv, mode):
            cp.start()
        token[...] = jnp.zeros_like(token)

    sems = pltpu.SemaphoreType.DMA((SEMS_PER_ARRAY * n,))
    out = pl.pallas_call(
        body, name=name,
        out_shape=(sems, sems, *[pltpu.HBM(s.shape, s.dtype) for s in srcs],
                   *[pltpu.HBM(shp, s.dtype) for shp, s in zip(land_shapes, srcs)], jax.ShapeDtypeStruct((8, 128), F32)),
        in_specs=[_HBM] * (2 * n) + [_ANY],
        out_specs=(_SEM, _SEM, *[_HBM] * (2 * n), pl.BlockSpec(memory_space=pltpu.VMEM)),
        input_output_aliases={i: 2 + i for i in range(2 * n)},
        compiler_params=pltpu.CompilerParams(has_side_effects=_EFFECT),
    )(*[pltpu.with_memory_space_constraint(s, pltpu.HBM) for s in srcs],
      *[pltpu.with_memory_space_constraint(lax.empty(shp, s.dtype), pltpu.HBM) for shp, s in zip(land_shapes, srcs)],
      after)
    return out[0], out[1], out[2:2 + n], out[2 + n:2 + 2 * n], out[-1]


def _exchange_wait(started, name, mode, after):
    send, recv, src_thru, land_thru, _ = started
    n = len(src_thru)

    def body(*refs):
        src_refs, land_refs, send_ref, recv_ref = refs[:n], refs[n:2 * n], refs[2 * n], refs[2 * n + 1]
        for cp in _exchange_copies(src_refs, land_refs, send_ref, recv_ref, mode):
            cp.wait_send()
            cp.wait_recv()

    out = pl.pallas_call(
        body, name=name,
        out_shape=tuple(pltpu.HBM(a.shape, a.dtype) for a in list(src_thru) + list(land_thru)),
        in_specs=[_HBM] * (2 * n) + [_SEM, _SEM, _ANY], out_specs=tuple([_HBM] * (2 * n)),
        input_output_aliases={i: i for i in range(2 * n)},
        compiler_params=pltpu.CompilerParams(has_side_effects=_EFFECT),
    )(*src_thru, *land_thru, send, recv, after)
    return out[:n], out[n:]


def _swap_halves(gs, name):
    n = len(gs)

    def body(*refs):
        ins, outs, send, recv = refs[:n], refs[n:2 * n], refs[2 * n], refs[2 * n + 1]
        x, y, c, _ = _place()
        cps = []
        for w in range(n):
            cps.append(pltpu.make_async_remote_copy(
                src_ref=ins[w].at[:, 1 - c], dst_ref=outs[w], send_sem=send.at[w], recv_sem=recv.at[w],
                device_id=(x, y, 1 - c), device_id_type=MESH))
        for cp in cps:
            cp.start()
        for cp in cps:
            cp.wait()

    return pl.pallas_call(
        body, name=name,
        out_shape=tuple(jax.ShapeDtypeStruct((g.shape[0],) + g.shape[2:], g.dtype) for g in gs),
        in_specs=[_ANY] * n, out_specs=tuple([_ANY] * n),
        scratch_shapes=[pltpu.SemaphoreType.DMA((n,)), pltpu.SemaphoreType.DMA((n,))],
    )(*gs)


GRAD_PAYLOAD = jnp.bfloat16


def _pair_sum(gs, gots, name):
    n = len(gs)
    core = lax.axis_index("c").astype(jnp.int32).reshape(1)

    def body(core_ref, *refs):
        del core_ref
        for w in range(n):
            refs[2 * n + w][...] = (refs[w][...] + refs[n + w][...]).astype(GRAD_PAYLOAD)

    in_specs, out_specs, out_shape, nbytes = [], [], [], 0
    for g in gs:
        q = g.shape[1] // 4
        in_specs.append(pl.BlockSpec((1, q, g.shape[2]), lambda s, r, core: (s, 2 * core[0] + r, 0)))
        nbytes += 3 * _nbytes((q, g.shape[2]), F32)
    for g in gs:
        q = g.shape[1] // 4
        in_specs.append(pl.BlockSpec((1, q, g.shape[2]), lambda s, r, core: (s, r, 0)))
        out_specs.append(pl.BlockSpec((1, q, g.shape[2]), lambda s, r, core: (s, r, 0)))
        out_shape.append(jax.ShapeDtypeStruct((g.shape[0], g.shape[1] // 2, g.shape[2]), GRAD_PAYLOAD))
    return pl.pallas_call(
        body, name=name, out_shape=tuple(out_shape),
        grid_spec=pltpu.PrefetchScalarGridSpec(num_scalar_prefetch=1, grid=(N_CHIPS, 2), in_specs=in_specs,
                                               out_specs=tuple(out_specs)),
        compiler_params=_params(("parallel", "parallel"), nbytes),
    )(core, *gs, *gots)


def _chip_sum(ps, landed):
    n = len(ps)
    x, y, c = lax.axis_index("x"), lax.axis_index("y"), lax.axis_index("c")
    where = jnp.stack([2 * x + y, 2 * (1 - x) + y, 2 * x + (1 - y), 2 * (1 - x) + (1 - y), c]).astype(jnp.int32)

    def body(where_ref, *refs):
        del where_ref
        for w in range(n):
            terms = [refs[4 * w + t][...].astype(F32) for t in range(4)]
            refs[4 * n + w][...] = ((terms[0] + terms[1]) + terms[2]) + terms[3]

    in_specs, out_specs, out_shape, args, nbytes = [], [], [], [], 0
    for p, a in zip(ps, landed):
        q = a.shape[1] // 2
        blk = (1, q, a.shape[2])
        in_specs.append(pl.BlockSpec(blk, lambda r, where: (where[0], r, 0)))
        args.append(p)
        for t in (1, 2, 3):
            in_specs.append(pl.BlockSpec(blk, lambda r, where, t=t: (where[t], r, 0)))
            args.append(a)
        out_specs.append(pl.BlockSpec(blk, lambda r, where: (where[4], r, 0)))
        out_shape.append(jax.ShapeDtypeStruct((2,) + a.shape[1:], F32))
        nbytes += 4 * _nbytes(blk, F32)
    return pl.pallas_call(
        body, name="grad_chip_sum", out_shape=tuple(out_shape),
        grid_spec=pltpu.PrefetchScalarGridSpec(num_scalar_prefetch=1, grid=(2,), in_specs=in_specs,
                                               out_specs=tuple(out_specs)),
        compiler_params=_params(("parallel",), nbytes),
    )(where, *args)


def _join_halves(ss):
    n = len(ss)

    def body(*refs):
        outs, send, recv = refs[n:2 * n], refs[2 * n], refs[2 * n + 1]
        x, y, c, _ = _place()
        cps = []
        for w in range(n):
            cps.append(pltpu.make_async_remote_copy(
                src_ref=outs[w].at[c], dst_ref=outs[w].at[c], send_sem=send.at[w], recv_sem=recv.at[w],
                device_id=(x, y, 1 - c), device_id_type=MESH))
        for cp in cps:
            cp.start()
        for w in range(n):
            got = outs[w].at[1 - c]
            pltpu.make_async_remote_copy(src_ref=got, dst_ref=got, send_sem=send.at[w], recv_sem=recv.at[w],
                                         device_id=(x, y, 1 - c), device_id_type=MESH).wait_recv()
        for cp in cps:
            cp.wait_send()

    dma = lambda k: pltpu.SemaphoreType.DMA((k,))
    return pl.pallas_call(
        body, name="grad_join_halves",
        out_shape=tuple(jax.ShapeDtypeStruct(s.shape, s.dtype) for s in ss),
        in_specs=[_ANY] * n, out_specs=tuple([_ANY] * n), input_output_aliases={w: w for w in range(n)},
        scratch_shapes=[dma(n), dma(n)],
    )(*ss)


def _rot_cols(w):
    a, b = jnp.split(w, 2, axis=-1)
    return jnp.concatenate([-b, a], axis=-1)


def _rot_cols_t(g):
    a, b = jnp.split(g, 2, axis=-1)
    return jnp.concatenate([b, -a], axis=-1)


def _cols_from_chips(a):
    n, r, cs = a.shape
    return jnp.transpose(a, (1, 0, 2)).reshape(r, n * cs)


def _cols_to_chips(a):
    r, cc = a.shape
    return jnp.transpose(a.reshape(r, N_CHIPS, cc // N_CHIPS), (1, 0, 2))


def _conv_w_split(cw):
    return jnp.swapaxes(cw.reshape(3, 2, D_FF), 0, 1)


def _conv_w_join(g):
    return jnp.swapaxes(g, 0, 1).reshape(3, 2 * D_FF)


_SEG =(D_MODEL, 2 * D_MODEL, 2 * D_MODEL + Q_RANK, 2 * D_MODEL + Q_RANK + KV_RANK, 2 * D_MODEL + Q_RANK + KV_RANK + ROPE,
        3 * D_MODEL + Q_RANK + KV_RANK + ROPE)


def _w_in_to_pad(w):
    u, v, cq, ckv, kr, ga, gb = jnp.split(w, _SEG, axis=1)
    return jnp.concatenate([u, v, ga, gb, cq, ckv, kr, _rot_cols(kr)], axis=1)


def _w_in_from_pad(g):
    u, v, ga, gb, cq, ckv, kr, krr = jnp.split(
        g, (D_MODEL, 2 * D_MODEL, 3 * D_MODEL, 4 * D_MODEL, 4 * D_MODEL + Q_RANK, 4 * D_MODEL + Q_RANK + KV_RANK,
            4 * D_MODEL + Q_RANK + KV_RANK + ROPE), axis=1)
    return jnp.concatenate([u, v, cq, ckv, kr + _rot_cols_t(krr), ga, gb], axis=1)


def _w_uq_to_pad(w):
    t = w.reshape(Q_RANK, HEADS, QK_DIM)
    nope, rope = t[..., :NOPE], t[..., NOPE:]
    return jnp.concatenate([nope, rope, _rot_cols(rope)], axis=-1).reshape(Q_RANK, HEADS * HEAD_PAD)


def _w_uq_from_pad(g):
    t = g.reshape(Q_RANK, HEADS, HEAD_PAD)
    nope, rope, rot = t[..., :NOPE], t[..., NOPE:QK_DIM], t[..., QK_DIM:]
    return jnp.concatenate([nope, rope + _rot_cols_t(rot)], axis=-1).reshape(Q_RANK, HEADS * QK_DIM)


def _w_ukv_to_pad(w):
    t = w.reshape(KV_RANK, HEADS, 2, NOPE)
    return jnp.swapaxes(t, 1, 2).reshape(KV_RANK, 2 * HEADS * NOPE)


def _w_ukv_from_pad(g):
    t = g.reshape(KV_RANK, 2, HEADS, NOPE)
    return jnp.swapaxes(t, 1, 2).reshape(KV_RANK, 2 * HEADS * NOPE)


def _rope_tables(positions):
    inv_freq = 1.0 / (ROPE_THETA ** (jnp.arange(0, ROPE, 2, dtype=F32) / ROPE))
    ang = positions.astype(F32).reshape(-1, 1) * inv_freq
    cos, sin = jnp.cos(ang), jnp.sin(ang)
    zero = jnp.zeros((ang.shape[0], 64), F32)
    return jnp.concatenate([cos, cos, zero], axis=1), jnp.concatenate([sin, sin, zero], axis=1)


_BIG = ("w_in", "w_uq", "w_ukv", "w_out", "w_up", "w_down")
UP_SHARD = 2 * D_FF // N_CHIPS


def _local_step(x, positions, tgt, wts, mixer_weights, ffn_weights, on_ffn_grads, on_mixer_grads):
    B, S, D = x.shape
    T = B * S
    xf = x.reshape(T, D)
    cos_a, sin_a = _rope_tables(positions)
    bs_t = jnp.pad(wts["a_spatial_b"].T, ((0, 0), (0, 128 - A_GROUPS)))

    h = _rms_fwd(xf, wts["mix_norm"], "norm1_fwd")
    z = _mm(h, wts["w_in"], "nn", "in_proj", tm=512, tn=1536, tk=D)
    wts = dict(wts)
    wts["w_q"], wts["w_kv"], wts["w_out"] = mixer_weights(z)
    q, k, v, cqn, ckvn = _lat_fwd(z, wts["q_a_norm"], wts["kv_a_norm"], wts["w_q"], wts["w_kv"], cos_a, sin_a)
    yb, *lses = _attn_fwd(q, k, v, B, S)
    merged = _mix_fwd(z, yb, wts["a_v_norm_g"], wts["a_v_norm_b"], wts["a_spatial_w"], bs_t)
    x1 = _mm(merged, wts["w_out"], "nn", "out_proj", tm=512, tn=D, tk=D, add=xf)
    h2 = _rms_fwd(x1, wts["ffn_norm"], "norm2_fwd")
    wts["w_up"], wts["w_down"], wts["conv_w"] = ffn_weights(h2)
    up_pre = _mm(h2, wts["w_up"], "nn", "up_proj", tm=512, tn=UP_SHARD, tk=D, dims=(T, 2 * D_FF, D),
                 b_spec=pl.BlockSpec((None, D, UP_SHARD), lambda i, j, k: (j, 0, 0)),
                 o_spec=pl.BlockSpec((None, 512, UP_SHARD), lambda i, j, k: (j // 2, i, j % 2)), out_shape=(2, T, D_FF))
    act = _gate_fwd(up_pre, wts["conv_w"], wts["conv_b"], B, S)
    x2 = _mm(act, wts["w_down"], "nn", "down_proj", tm=512, tn=D, tk=1408, add=x1)
    dx2, loss_row, g_final = _final(x2, tgt.reshape(T, D), wts["final_norm"])

    g = {"final_norm": g_final}
    dact = _mm(dx2, wts["w_down"], "nt", "down_proj_dx", tm=512, tn=1408, tk=D)
    tk2, tk1 = min(2048, T), min(1024, T)
    g["w_down"] = _mm(act, dx2, "tn", "down_proj_dw", tm=1408, tn=D, tk=tk1)
    dup, g["conv_w"], g["conv_b"] = _gate_bwd(up_pre, dact, wts["conv_w"], wts["conv_b"], B, S)
    g["w_up"] = _mm(h2, dup, "tn", "up_proj_dw", tm=D, tn=UP_SHARD, tk=tk2, dims=(D, 2 * D_FF, T),
                    b_spec=pl.BlockSpec((None, tk2, UP_SHARD), lambda i, j, k: (j // 2, k, j % 2)),
                    o_spec=pl.BlockSpec((None, D, UP_SHARD), lambda i, j, k: (j, 0, 0)), out_shape=(N_CHIPS, D, UP_SHARD))
    ffn_sent = on_ffn_grads(g["w_up"], g["w_down"])
    dh2 = _mm(dup, wts["w_up"], "nt", "up_proj_dx", tm=512, tn=D, tk=UP_SHARD, dims=(T, D, 2 * D_FF),
              a_spec=pl.BlockSpec((None, 512, UP_SHARD), lambda i, j, k: (k // 2, i, k % 2)),
              b_spec=pl.BlockSpec((None, D, UP_SHARD), lambda i, j, k: (k, 0, 0)))
    token = None if ffn_sent is None else ffn_sent(dh2)
    ffn_norm = wts["ffn_norm"] if token is None else wts["ffn_norm"] + token[0:1, 0:1]
    dx1, g["ffn_norm"] = _rms_bwd(x1, ffn_norm, dh2, dx2, "norm2_bwd")
    dm = _mm(dx1, wts["w_out"], "nt", "out_proj_dx", tm=512, tn=D, tk=D)
    g["w_out"] = _mm(merged, dx1, "tn", "out_proj_dw", tm=D, tn=D, tk=tk1)
    dz, dyb, dl, g["a_spatial_w"], gbs, g["a_v_norm_g"], g["a_v_norm_b"] = _mix_bwd(
        z, yb, dm, wts["a_v_norm_g"], wts["a_v_norm_b"], wts["a_spatial_w"], bs_t)
    g["a_spatial_b"] = gbs[:, :A_GROUPS].T
    delta = dl.reshape(HEADS * T // ATT_BLOCK, 1, ATT_BLOCK)
    dq, dk, dv = _attn_bwd(q, k, v, dyb, lses, delta, B, S)
    dz, dq_raw, dkv, g["q_a_norm"], g["kv_a_norm"] = _lat_bwd(
        dz, z, dq, dk, dv, wts["q_a_norm"], wts["kv_a_norm"], wts["w_q"], wts["w_kv"], cos_a, sin_a)
    g["w_q"] = _mm(cqn, dq_raw, "tn", "q_proj_dw", tm=Q_RANK, tn=HEADS * HEAD_PAD, tk=tk2)
    g["w_kv"] = _mm(ckvn, dkv, "tn", "kv_proj_dw", tm=KV_RANK, tn=2 * HEADS * NOPE, tk=tk2)
    g["w_in"] = _mm(h, dz, "tn", "in_proj_dw", tm=D, tn=1536, tk=tk2)
    token = on_mixer_grads(g)
    mix_norm = wts["mix_norm"] if token is None else wts["mix_norm"] + token[0:1, 0:1]
    dh = _mm(dz, wts["w_in"], "nt", "in_proj_dx", tm=512, tn=D, tk=1536)
    dx, g["mix_norm"] = _rms_bwd(xf, mix_norm, dh, dx1, "norm1_bwd")
    return loss_row[0, 0], dx.reshape(B, S, D), g


_SMALL = (("mix_norm", (1, D_MODEL)), ("a_v_norm_g", (1, D_MODEL)), ("a_v_norm_b", (1, D_MODEL)),
          ("a_spatial_w", (A_GROUPS * CHUNK, CHUNK)), ("a_spatial_b", (1, A_GROUPS * CHUNK)), ("q_a_norm", (1, Q_RANK)),
          ("kv_a_norm", (1, KV_RANK)), ("ffn_norm", (1, D_MODEL)), ("conv_b", (1, 2 * D_FF)), ("final_norm", (1, D_MODEL)),
          ("conv_w", (3, 2 * D_FF)))
_SMALL_SIZE = sum(math.prod(s) for _, s in _SMALL)
_SMALL_ROWS = -(-(_SMALL_SIZE + 1) // (128 * 8)) * 8


def kernel(x, positions, mix_norm, w_in, a_v_norm_g, a_v_norm_b, a_spatial_w, a_spatial_b, q_a_norm, w_uq, kv_a_norm, w_ukv, w_out, ffn_norm, w_up, conv_w, conv_b, w_down, final_norm, loss_target, m_mix_norm, m_w_in, m_a_v_norm_g, m_a_v_norm_b, m_a_spatial_w, m_a_spatial_b, m_q_a_norm, m_w_uq, m_kv_a_norm, m_w_ukv, m_w_out, m_ffn_norm, m_w_up, m_conv_w, m_conv_b, m_w_down, m_final_norm, v_mix_norm, v_w_in, v_a_v_norm_g, v_a_v_norm_b, v_a_spatial_w, v_a_spatial_b, v_q_a_norm, v_w_uq, v_kv_a_norm, v_w_ukv, v_w_out, v_ffn_norm, v_w_up, v_conv_w, v_conv_b, v_w_down, v_final_norm):
    weights = dict(mix_norm=mix_norm, w_in=w_in, a_v_norm_g=a_v_norm_g, a_v_norm_b=a_v_norm_b, a_spatial_w=a_spatial_w,
                   a_spatial_b=a_spatial_b, q_a_norm=q_a_norm, w_uq=w_uq, kv_a_norm=kv_a_norm, w_ukv=w_ukv, w_out=w_out,
                   ffn_norm=ffn_norm, w_up=w_up, conv_w=conv_w, conv_b=conv_b, w_down=w_down, final_norm=final_norm)
    m_in = dict(mix_norm=m_mix_norm, w_in=m_w_in, a_v_norm_g=m_a_v_norm_g, a_v_norm_b=m_a_v_norm_b,
                a_spatial_w=m_a_spatial_w, a_spatial_b=m_a_spatial_b, q_a_norm=m_q_a_norm, w_uq=m_w_uq,
                kv_a_norm=m_kv_a_norm, w_ukv=m_w_ukv, w_out=m_w_out, ffn_norm=m_ffn_norm, w_up=m_w_up, conv_w=m_conv_w,
                conv_b=m_conv_b, w_down=m_w_down, final_norm=m_final_norm)
    v_in = dict(mix_norm=v_mix_norm, w_in=v_w_in, a_v_norm_g=v_a_v_norm_g, a_v_norm_b=v_a_v_norm_b,
                a_spatial_w=v_a_spatial_w, a_spatial_b=v_a_spatial_b, q_a_norm=v_q_a_norm, w_uq=v_w_uq,
                kv_a_norm=v_kv_a_norm, w_ukv=v_w_ukv, w_out=v_w_out, ffn_norm=v_ffn_norm, w_up=v_w_up, conv_w=v_conv_w,
                conv_b=v_conv_b, w_down=v_w_down, final_norm=v_final_norm)
    names = list(weights)
    chip = 2 * lax.axis_index("x") + lax.axis_index("y")

    def halves(a):
        return a.reshape(a.shape[:-2] + (2, a.shape[-2] // 2, a.shape[-1]))

    def whole(a):
        return a.reshape(a.shape[:-3] + (2 * a.shape[-2], a.shape[-1]))

    (w_in_sh,) = _gather_weights([halves(w_in[0].astype(MXU_DTYPE))])
    mixer_gather = _exchange_start([weights[n][0].astype(MXU_DTYPE) for n in _BIG[1:4]],
                                   "mixer_gather_start", "gather", after=w_in_sh)
    ffn_gather = _exchange_start([w_up[0].astype(MXU_DTYPE), w_down[0].astype(MXU_DTYPE), conv_w[0]],
                                 "ffn_gather_start", "gather", after=mixer_gather[4])
    wts = dict(
        mix_norm=mix_norm + ffn_gather[4][0:1, 0:1], a_v_norm_g=a_v_norm_g, a_v_norm_b=a_v_norm_b,
        a_spatial_w=a_spatial_w[0], a_spatial_b=a_spatial_b[0], q_a_norm=q_a_norm, kv_a_norm=kv_a_norm,
        ffn_norm=ffn_norm, final_norm=final_norm.reshape(1, D_MODEL),
        w_in=_w_in_to_pad(_cols_from_chips(whole(w_in_sh))), conv_b=conv_b.reshape(2, 1, D_FF))

    def mixer_weights(after):
        _, (w_uq_sh, w_ukv_sh, w_out_sh) = _exchange_wait(mixer_gather, "mixer_gather_wait", "gather", after)
        return (_w_uq_to_pad(_cols_from_chips(w_uq_sh)), _w_ukv_to_pad(_cols_from_chips(w_ukv_sh)),
                w_out_sh.reshape(D_MODEL, D_MODEL))

    def ffn_weights(after):
        _, (w_up_sh, w_down_sh, cw_all) = _exchange_wait(ffn_gather, "ffn_gather_wait", "gather", after)
        return w_up_sh, w_down_sh.reshape(D_FF, D_MODEL), _conv_w_split(_cols_from_chips(cw_all))

    scatters = {}

    def start_scatter(slabs, tag):
        sums = _pair_sum(slabs, _swap_halves([halves(s) for s in slabs], tag + "_grad_swap_halves"), tag + "_grad_pair_sum")
        scatters[tag] = _exchange_start(list(sums), tag + "_scatter_start", "scatter", after=slabs[-1])
        return scatters[tag][4]

    def on_ffn_grads(g_w_up, g_w_down):
        token = start_scatter([g_w_up, g_w_down.reshape(N_CHIPS, D_FF // N_CHIPS, D_MODEL)], "ffn")
        return lambda after: token

    def on_mixer_grads(g):
        return start_scatter(
            [_cols_to_chips(_w_in_from_pad(g["w_in"])), _cols_to_chips(_w_uq_from_pad(g["w_q"])),
             _cols_to_chips(_w_ukv_from_pad(g["w_kv"])), g["w_out"].reshape(N_CHIPS, D_MODEL // N_CHIPS, D_MODEL)], "mixer")

    loss_part, grad_x, g = _local_step(x, positions, loss_target, wts, mixer_weights, ffn_weights, on_ffn_grads,
                                       on_mixer_grads)

    g_small_parts = dict(g)
    g_small_parts["conv_w"] = _conv_w_join(g["conv_w"])
    g_small_parts["conv_b"] = g["conv_b"].reshape(1, 2 * D_FF)
    flat = jnp.concatenate([g_small_parts[n].reshape(-1) for n, _ in _SMALL] + [loss_part.reshape(1)])
    flat = jnp.pad(flat, (0, _SMALL_ROWS * 128 - flat.shape[0])).reshape(_SMALL_ROWS, 128)
    small_gather = _exchange_start([flat], "small_gather_start", "all", after=grad_x)

    mixer_sums, mixer_landed = _exchange_wait(scatters["mixer"], "mixer_scatter_wait", "scatter", after=small_gather[4])
    ffn_sums, ffn_landed = _exchange_wait(scatters["ffn"], "ffn_scatter_wait", "scatter", after=mixer_landed[0])
    reduced = _chip_sum(list(mixer_sums) + list(ffn_sums), list(mixer_landed) + list(ffn_landed))
    g_big = dict(zip(_BIG, _join_halves(reduced)))

    grads, deltas, new_m, new_v = {}, {}, {}, {}

    def update(n, grad):
        w = weights[n]
        shape2 = grad.shape
        d, nm, nv = _adamw(w.reshape(shape2), grad, m_in[n].reshape(shape2), v_in[n].reshape(shape2), "adamw_" + n)
        grads[n], deltas[n], new_m[n], new_v[n] = (t.reshape(w.shape) for t in (grad, d, nm, nv))

    def update_transposed(n, grad):
        t = lambda a: jnp.swapaxes(a, 1, 2)
        d, nm, nv = _adamw(t(weights[n]), t(grad), t(m_in[n]), t(v_in[n]), "adamw_" + n)
        grads[n], deltas[n], new_m[n], new_v[n] = grad, t(d), t(nm), t(nv)

    for n in _BIG:
        g3 = g_big[n].reshape((1, -1, g_big[n].shape[-1]))
        if n == "w_in":
            update_transposed(n, g3)
        else:
            update(n, g3)

    (own,), (everyone,) = _exchange_wait(small_gather, "small_gather_wait", "all", after=deltas["w_up"])
    device = 2 * chip + lax.axis_index("c")
    everyone = lax.dynamic_update_slice(everyone, own[None], (device, 0, 0))
    total = _sum_slabs([everyone[j] for j in range(8)], "small_grads_sum", tr=_SMALL_ROWS).reshape(-1)
    o = 0
    for n, shp in _SMALL:
        piece = total[o:o + math.prod(shp)].reshape(shp)
        o += math.prod(shp)
        if n == "conv_w":
            piece = lax.dynamic_slice_in_dim(piece, chip * UP_SHARD, UP_SHARD, axis=1)
        update(n, piece)
    loss = total[_SMALL_SIZE]
    return (loss, grad_x, *[grads[n] for n in names], *[deltas[n] for n in names], *[new_m[n] for n in names],
            *[new_v[n] for n in names])
```

```python
import functools
import math

import jax
import jax.numpy as jnp
from jax import lax
from jax.experimental import pallas as pl
from jax.experimental.pallas import tpu as pltpu

F32 = jnp.float32
MXU_DTYPE = jnp.bfloat16
MESH = pl.DeviceIdType.MESH

D_MODEL = 1024
EPS = 1e-6
A_GROUPS = 8
CHUNK = 128
HEADS = 8
NOPE = 128
ROPE = 64
QK_DIM = NOPE + ROPE
HEAD_PAD = 256
Q_RANK = 256
KV_RANK = 128
ROPE_THETA = 10000.0
D_FF = 2816
FF_TILE = 256
N_FF_TILES = D_FF // FF_TILE
LAT = 512
IN_PAD = 4 * D_MODEL + LAT
N_CHIPS = 4
ADAM_LR, ADAM_B1, ADAM_B2, ADAM_EPS, ADAM_WD, ADAM_STEP = 0.001, 0.9, 0.999, 1e-08, 0.01, 10

VMEM_CAP_V7X = 64 * 1024 * 1024
NEG = -1e30


def _params(sem, nbytes):
    limit = int(min(VMEM_CAP_V7X - (8 << 20), max(32 << 20, 3 * nbytes)))
    return pltpu.CompilerParams(dimension_semantics=sem, vmem_limit_bytes=limit)


def _nbytes(shape, dtype):
    return math.prod(shape) * jnp.dtype(dtype).itemsize


_DIMS = {"nn": (((1,), (0,)), ((), ())), "nt": (((1,), (1,)), ((), ())), "tn": (((0,), (0,)), ((), ()))}


def _mm(a, b, mode, name, *, tm, tn, tk, out_dtype=F32, add=None, dims=None, a_spec=None, b_spec=None,
        o_spec=None, out_shape=None):
    if dims is None:
        if mode == "nn":
            (M, K), (_, N) = a.shape, b.shape
        elif mode == "nt":
            (M, K), (N, _) = a.shape, b.shape
        else:
            (K, M), (_, N) = a.shape, b.shape
    else:
        M, N, K = dims
    a_blk = (tk, tm) if mode == "tn" else (tm, tk)
    b_blk = (tn, tk) if mode == "nt" else (tk, tn)
    if a_spec is None:
        a_spec = pl.BlockSpec(a_blk, (lambda i, j, k: (k, i)) if mode == "tn" else (lambda i, j, k: (i, k)))
    if b_spec is None:
        b_spec = pl.BlockSpec(b_blk, (lambda i, j, k: (j, k)) if mode == "nt" else (lambda i, j, k: (k, j)))
    if o_spec is None:
        o_spec = pl.BlockSpec((tm, tn), lambda i, j, k: (i, j))
    if out_shape is None:
        out_shape = (M, N)
    assert M % tm == 0 and N % tn == 0 and K % tk == 0, (name, M, N, K, tm, tn, tk)
    nk = K // tk
    contract = _DIMS[mode]
    has_add = add is not None

    def body(*refs):
        a_ref, b_ref = refs[0], refs[1]
        add_ref = refs[2] if has_add else None
        o_ref = refs[3] if has_add else refs[2]

        def product():
            return lax.dot_general(a_ref[...].astype(MXU_DTYPE), b_ref[...].astype(MXU_DTYPE), contract,
                                   preferred_element_type=F32)

        if nk == 1:
            o_ref[...] = (product() + add_ref[...] if has_add else product()).astype(out_dtype)
            return
        acc = refs[-1]
        k = pl.program_id(2)

        @pl.when(k == 0)
        def _():
            acc[...] = jnp.zeros_like(acc)

        acc[...] += product()

        @pl.when(k == nk - 1)
        def _():
            r = acc[...]
            if has_add:
                r = r + add_ref[...]
            o_ref[...] = r.astype(out_dtype)

    in_specs = [a_spec, b_spec]
    args = [a, b]
    nbytes = _nbytes(a_blk, a.dtype) + _nbytes(b_blk, b.dtype) + 3 * _nbytes((tm, tn), F32)
    if has_add:
        in_specs.append(pl.BlockSpec((tm, tn), lambda i, j, k: (i, j)))
        args.append(add)
        nbytes += _nbytes((tm, tn), F32)
    return pl.pallas_call(
        body, name=name, out_shape=jax.ShapeDtypeStruct(out_shape, out_dtype),
        grid=(M // tm, N // tn, nk), in_specs=in_specs, out_specs=o_spec,
        scratch_shapes=[pltpu.VMEM((tm, tn), F32)] if nk > 1 else [],
        compiler_params=_params(("parallel", "parallel", "arbitrary"), nbytes),
    )(*args)


_GELU_C = math.sqrt(2.0 / math.pi)
_GELU_A = 0.044715


def _sigmoid(x):
    return 1.0 / (1.0 + jnp.exp(-x))


def _gelu(x):
    t = jnp.tanh(_GELU_C * (x + _GELU_A * (x * x * x)))
    return x * (0.5 * (1.0 + t))


def _gelu_and_grad(x):
    x2 = x * x
    t = jnp.tanh(_GELU_C * (x + _GELU_A * (x2 * x)))
    cdf = 0.5 * (1.0 + t)
    grad = cdf + 0.5 * x * (1.0 - t * t) * (_GELU_C * (1.0 + 3.0 * _GELU_A * x2))
    return x * cdf, grad


def _rope_mix(g, cos_a, sin_a):
    return g * cos_a + pltpu.roll(g, 64, 1) * sin_a


def _rope_mix_bwd(d, cos_a, sin_a):
    return d * cos_a + pltpu.roll(d * sin_a, 64, 1)


def _rms_fwd(x, g, name, tr=512):
    T, D = x.shape

    def body(x_ref, g_ref, h_ref):
        xv = x_ref[...]
        r = lax.rsqrt(jnp.mean(xv * xv, axis=-1, keepdims=True) + EPS)
        h_ref[...] = ((xv * r) * g_ref[...]).astype(h_ref.dtype)

    return pl.pallas_call(
        body, name=name, out_shape=jax.ShapeDtypeStruct((T, D), MXU_DTYPE), grid=(T // tr,),
        in_specs=[pl.BlockSpec((tr, D), lambda i: (i, 0)), pl.BlockSpec((1, D), lambda i: (0, 0))],
        out_specs=pl.BlockSpec((tr, D), lambda i: (i, 0)),
        compiler_params=_params(("parallel",), 3 * _nbytes((tr, D), F32)),
    )(x, g)


def _rms_bwd(x, g, dh, dres, name, tr=512):
    T, D = x.shape

    def body(x_ref, g_ref, dh_ref, dres_ref, dx_ref, gg_ref):
        @pl.when(pl.program_id(0) == 0)
        def _():
            gg_ref[...] = jnp.zeros_like(gg_ref)

        xv = x_ref[...]
        r = lax.rsqrt(jnp.mean(xv * xv, axis=-1, keepdims=True) + EPS)
        xn = xv * r
        dhv = dh_ref[...]
        dxn = dhv * g_ref[...]
        dx_ref[...] = dres_ref[...] + r * (dxn - xn * jnp.mean(dxn * xn, axis=-1, keepdims=True))
        gg_ref[...] += jnp.sum(dhv * xn, axis=0, keepdims=True)

    row = pl.BlockSpec((tr, D), lambda i: (i, 0))
    vec = pl.BlockSpec((1, D), lambda i: (0, 0))
    return pl.pallas_call(
        body, name=name,
        out_shape=(jax.ShapeDtypeStruct((T, D), F32), jax.ShapeDtypeStruct((1, D), F32)),
        grid=(T // tr,), in_specs=[row, vec, row, row], out_specs=(row, vec),
        compiler_params=_params(("arbitrary",), 6 * _nbytes((tr, D), F32)),
    )(x, g, dh, dres)


def _lat_fwd(z, gq, gkv, wq, wkv, cos_a, sin_a, tr=256):
    T = z.shape[0]
    lat_blk = (4 * D_MODEL) // LAT

    def body(z_ref, gq_ref, gkv_ref, wq_ref, wkv_ref, cos_ref, sin_ref, q_ref, k_ref, v_ref, cqn_ref, ckvn_ref):
        zl = z_ref[...]
        cos_v, sin_v = cos_ref[...], sin_ref[...]
        cq = zl[:, :Q_RANK]
        ckv = zl[:, Q_RANK:Q_RANK + KV_RANK]
        krb = zl[:, Q_RANK + KV_RANK:]
        cqn = ((cq * lax.rsqrt(jnp.mean(cq * cq, axis=-1, keepdims=True) + EPS)) * gq_ref[...]).astype(MXU_DTYPE)
        ckvn = ((ckv * lax.rsqrt(jnp.mean(ckv * ckv, axis=-1, keepdims=True) + EPS)) * gkv_ref[...]).astype(MXU_DTYPE)
        cqn_ref[...] = cqn
        ckvn_ref[...] = ckvn
        krr = _rope_mix(krb, cos_v, sin_v).astype(MXU_DTYPE)
        q = jnp.dot(cqn, wq_ref[...], preferred_element_type=F32)
        kv = jnp.dot(ckvn, wkv_ref[...], preferred_element_type=F32)
        for h in range(HEADS):
            o = h * HEAD_PAD
            q_ref[:, o:o + NOPE] = q[:, o:o + NOPE].astype(MXU_DTYPE)
            q_ref[:, o + NOPE:o + HEAD_PAD] = _rope_mix(q[:, o + NOPE:o + HEAD_PAD], cos_v, sin_v).astype(MXU_DTYPE)
            k_ref[:, o:o + NOPE] = kv[:, h * NOPE:(h + 1) * NOPE].astype(MXU_DTYPE)
            k_ref[:, o + NOPE:o + HEAD_PAD] = krr
        v_ref[...] = kv[:, HEADS * NOPE:].astype(MXU_DTYPE)

    def row(w):
        return pl.BlockSpec((tr, w), lambda i: (i, 0))

    def full(a):
        return pl.BlockSpec(a.shape, lambda i: (0, 0))

    return pl.pallas_call(
        body, name="lat_fwd",
        out_shape=(jax.ShapeDtypeStruct((T, HEADS * HEAD_PAD), MXU_DTYPE), jax.ShapeDtypeStruct((T, HEADS * HEAD_PAD), MXU_DTYPE),
                   jax.ShapeDtypeStruct((T, HEADS * NOPE), MXU_DTYPE), jax.ShapeDtypeStruct((T, Q_RANK), MXU_DTYPE),
                   jax.ShapeDtypeStruct((T, KV_RANK), MXU_DTYPE)),
        grid=(T // tr,),
        in_specs=[pl.BlockSpec((tr, LAT), lambda i: (i, lat_blk)), full(gq), full(gkv), full(wq), full(wkv), row(128), row(128)],
        out_specs=(row(HEADS * HEAD_PAD), row(HEADS * HEAD_PAD), row(HEADS * NOPE), row(Q_RANK), row(KV_RANK)),
        compiler_params=_params(("parallel",), 8 * _nbytes((tr, HEADS * HEAD_PAD), F32)),
    )(z, gq, gkv, wq, wkv, cos_a, sin_a)


ATT_BLOCK = 256
_SCALE = QK_DIM ** -0.5


def _causal_mask(n):
    return lax.broadcasted_iota(jnp.int32, (n, n), 1) <= lax.broadcasted_iota(jnp.int32, (n, n), 0)


def _causal_mask_t(n):
    return lax.broadcasted_iota(jnp.int32, (n, n), 0) <= lax.broadcasted_iota(jnp.int32, (n, n), 1)


ATT_HEADS = 4


def _attn_fwd(q, k, v, B, S):
    tq = ATT_BLOCK
    nq = S // tq
    T = B * S
    hp, groups = ATT_HEADS, HEADS // ATT_HEADS

    def body(q_ref, k_ref, v_ref, o_ref, *lse_refs):
        qi = pl.program_id(2)
        qs = [q_ref[:, t * HEAD_PAD:(t + 1) * HEAD_PAD] for t in range(hp)]

        def scores(j, t):
            rows = pl.ds(pl.multiple_of(j * tq, tq), tq)
            return lax.dot_general(k_ref[rows, t * HEAD_PAD:(t + 1) * HEAD_PAD], qs[t], _DIMS["nt"],
                                   preferred_element_type=F32)

        def step(j, carry, last):
            rows = pl.ds(pl.multiple_of(j * tq, tq), tq)
            out = []
            for t in range(hp):
                m, l, acc, st = carry[t]
                st_next = st if last else scores(j + 1, t)
                st = st * _SCALE
                if last:
                    st = jnp.where(_causal_mask_t(tq), st, NEG)
                m_new = jnp.maximum(m, jnp.max(st, axis=0, keepdims=True))
                alpha = jnp.exp(m - m_new)
                p = jnp.exp(st - m_new)
                l = alpha * l + jnp.sum(p, axis=0, keepdims=True)
                acc = alpha * acc + lax.dot_general(v_ref[rows, t * NOPE:(t + 1) * NOPE], p.astype(MXU_DTYPE),
                                                    _DIMS["tn"], preferred_element_type=F32)
                out.append((m_new, l, acc, st_next))
            return tuple(out)

        init = tuple((jnp.full((1, tq), NEG, F32), jnp.zeros((1, tq), F32), jnp.zeros((NOPE, tq), F32), scores(0, t))
                     for t in range(hp))
        carry = lax.fori_loop(0, qi, lambda j, c: step(j, c, False), init)
        carry = step(qi, carry, True)
        for t in range(hp):
            m, l, acc, _ = carry[t]
            o_ref[:, t * NOPE:(t + 1) * NOPE] = (acc / l).T
            lse_refs[t][0] = m + jnp.log(l)

    lse_sds = jax.ShapeDtypeStruct((groups * B * nq, 1, tq), F32)
    lse_spec = pl.BlockSpec((1, 1, tq), lambda b, h, i: ((h * B + b) * nq + i, 0, 0))
    return pl.pallas_call(
        body, name="attn_fwd",
        out_shape=(jax.ShapeDtypeStruct((T, HEADS * NOPE), F32),) + (lse_sds,) * hp,
        grid=(B, groups, nq),
        in_specs=[pl.BlockSpec((tq, hp * HEAD_PAD), lambda b, h, i: (b * nq + i, h)),
                  pl.BlockSpec((S, hp * HEAD_PAD), lambda b, h, i: (b, h)),
                  pl.BlockSpec((S, hp * NOPE), lambda b, h, i: (b, h))],
        out_specs=(pl.BlockSpec((tq, hp * NOPE), lambda b, h, i: (b * nq + i, h)),) + (lse_spec,) * hp,
        compiler_params=_params(("parallel", "parallel", "arbitrary"), 4 * hp * _nbytes((S, HEAD_PAD), MXU_DTYPE)),
    )(q, k, v)


def _attn_bwd(q, k, v, do, lses, delta, B, S):
    tq = ATT_BLOCK
    nq = S // tq
    T = B * S
    hp, groups = ATT_HEADS, HEADS // ATT_HEADS

    def body(q_ref, k_ref, v_ref, do_ref, *refs):
        lse_refs, dl_refs = refs[:hp], refs[hp:2 * hp]
        dq_ref, dk_ref, dv_ref = refs[2 * hp:]
        kj = pl.program_id(2)

        @pl.when(kj == 0)
        def _():
            dq_ref[...] = jnp.zeros_like(dq_ref)

        def products(i, t):
            rows = pl.ds(pl.multiple_of(i * tq, tq), tq)
            st = lax.dot_general(k_ref[:, t * HEAD_PAD:(t + 1) * HEAD_PAD], q_ref[rows, t * HEAD_PAD:(t + 1) * HEAD_PAD],
                                 _DIMS["nt"], preferred_element_type=F32)
            dpt = lax.dot_general(v_ref[:, t * NOPE:(t + 1) * NOPE], do_ref[rows, t * NOPE:(t + 1) * NOPE],
                                  _DIMS["nt"], preferred_element_type=F32)
            return st, dpt

        def step(i, carry, masked):
            rows = pl.ds(pl.multiple_of(i * tq, tq), tq)
            nxt = jnp.minimum(i + 1, nq - 1)
            out = []
            for t in range(hp):
                dk, dv, st, dpt = carry[t]
                st_next, dpt_next = products(nxt, t)
                qk_cols = slice(t * HEAD_PAD, (t + 1) * HEAD_PAD)
                v_cols = slice(t * NOPE, (t + 1) * NOPE)
                p = jnp.exp(st * _SCALE - lse_refs[t][i])
                if masked:
                    p = jnp.where(_causal_mask_t(tq), p, 0.0)
                dv = dv + jnp.dot(p.astype(MXU_DTYPE), do_ref[rows, v_cols], preferred_element_type=F32)
                ds = (p * (dpt - dl_refs[t][i]) * _SCALE).astype(MXU_DTYPE)
                dk = dk + jnp.dot(ds, q_ref[rows, qk_cols], preferred_element_type=F32)
                dq_ref[rows, qk_cols] += lax.dot_general(ds, k_ref[:, qk_cols], _DIMS["tn"], preferred_element_type=F32)
                out.append((dk, dv, st_next, dpt_next))
            return tuple(out)

        init = tuple((jnp.zeros((tq, HEAD_PAD), F32), jnp.zeros((tq, NOPE), F32)) + products(kj, t) for t in range(hp))
        carry = step(kj, init, True)
        carry = lax.fori_loop(kj + 1, nq, lambda i, c: step(i, c, False), carry)
        for t in range(hp):
            dk_ref[:, t * HEAD_PAD:(t + 1) * HEAD_PAD] = carry[t][0]
            dv_ref[:, t * NOPE:(t + 1) * NOPE] = carry[t][1].astype(dv_ref.dtype)

    seq = lambda w: pl.BlockSpec((S, w), lambda b, h, j: (b, h))
    blk = lambda w: pl.BlockSpec((tq, w), lambda b, h, j: (b * nq + j, h))
    lse_spec = pl.BlockSpec((nq, 1, tq), lambda b, h, j: (h * B + b, 0, 0))
    dl_specs = [pl.BlockSpec((nq, 1, tq), lambda b, h, j, t=t: ((h * hp + t) * B + b, 0, 0)) for t in range(hp)]
    return pl.pallas_call(
        body, name="attn_bwd",
        out_shape=(jax.ShapeDtypeStruct((T, HEADS * HEAD_PAD), F32), jax.ShapeDtypeStruct((T, HEADS * HEAD_PAD), F32),
                   jax.ShapeDtypeStruct((T, HEADS * NOPE), MXU_DTYPE)),
        grid=(B, groups, nq),
        in_specs=[seq(hp * HEAD_PAD), blk(hp * HEAD_PAD), blk(hp * NOPE), seq(hp * NOPE)] + [lse_spec] * hp + dl_specs,
        out_specs=(seq(hp * HEAD_PAD), blk(hp * HEAD_PAD), blk(hp * NOPE)),
        compiler_params=_params(("parallel", "parallel", "arbitrary"), 8 * hp * _nbytes((S, HEAD_PAD), F32)),
    )(q, k, v, do, *lses, *([delta] * hp))


MIX_ROWS = 256


def _tril_weights(ws_ref, g):
    return jnp.where(_causal_mask(CHUNK), ws_ref[g], 0.0).astype(MXU_DTYPE)


def _layer_norm_stats(va):
    mu = jnp.mean(va, axis=-1, keepdims=True)
    xc = va - mu
    rs = lax.rsqrt(jnp.mean(xc * xc, axis=-1, keepdims=True) + EPS)
    return xc * rs


def _mix_specs(tr):
    zcol = lambda c: pl.BlockSpec((tr, D_MODEL), lambda i, c=c: (i, c))
    row = pl.BlockSpec((tr, D_MODEL), lambda i: (i, 0))
    vec = pl.BlockSpec((1, D_MODEL), lambda i: (0, 0))
    ws = pl.BlockSpec((A_GROUPS, CHUNK, CHUNK), lambda i: (0, 0, 0))
    bs = pl.BlockSpec((CHUNK, 128), lambda i: (0, 0))
    return zcol, row, vec, ws, bs


def _mix_fwd(z, yb, ln_g, ln_b, ws, bs_t):
    T = z.shape[0]
    tr = MIX_ROWS
    zcol, row, vec, ws_spec, bs_spec = _mix_specs(tr)

    def body(zu_ref, zv_ref, zga_ref, zgb_ref, yb_ref, g_ref, b_ref, ws_ref, bs_ref, out_ref, vn_s):
        vhat = _layer_norm_stats(_gelu(zv_ref[...]))
        vn_s[...] = (vhat * g_ref[...] + b_ref[...]).astype(MXU_DTYPE)
        for g in range(A_GROUPS):
            w = _tril_weights(ws_ref, g)
            bias = bs_ref[:, g:g + 1]
            cols = slice(g * CHUNK, (g + 1) * CHUNK)
            for c in range(tr // CHUNK):
                rows = slice(c * CHUNK, (c + 1) * CHUNK)
                mixed = jnp.dot(w, vn_s[rows, cols], preferred_element_type=F32) + bias
                ya = _gelu(zu_ref[rows, cols]) * mixed
                merged = _sigmoid(zga_ref[rows, cols]) * ya + _sigmoid(zgb_ref[rows, cols]) * yb_ref[rows, cols]
                out_ref[rows, cols] = merged.astype(MXU_DTYPE)

    return pl.pallas_call(
        body, name="mix_fwd", out_shape=jax.ShapeDtypeStruct((T, D_MODEL), MXU_DTYPE), grid=(T // tr,),
        in_specs=[zcol(0), zcol(1), zcol(2), zcol(3), row, vec, vec, ws_spec, bs_spec], out_specs=row,
        scratch_shapes=[pltpu.VMEM((tr, D_MODEL), MXU_DTYPE)],
        compiler_params=_params(("parallel",), 8 * _nbytes((tr, D_MODEL), F32)),
    )(z, z, z, z, yb, ln_g, ln_b, ws, bs_t)


def _mix_bwd(z, yb, dm, ln_g, ln_b, ws, bs_t):
    T = z.shape[0]
    tr = MIX_ROWS
    zcol, row, vec, ws_spec, bs_spec = _mix_specs(tr)

    def body(zu_ref, zv_ref, zga_ref, zgb_ref, yb_ref, dm_ref, g_ref, b_ref, ws_ref, bs_ref,
             dz_ref, dyb_ref, dl_ref, gws_ref, gbs_ref, glg_ref, glb_ref, vn_s, dvn_s):
        @pl.when(pl.program_id(0) == 0)
        def _():
            gws_ref[...] = jnp.zeros_like(gws_ref)
            gbs_ref[...] = jnp.zeros_like(gbs_ref)
            glg_ref[...] = jnp.zeros_like(glg_ref)
            glb_ref[...] = jnp.zeros_like(glb_ref)

        lane = lax.broadcasted_iota(jnp.int32, (CHUNK, 128), 1)
        va, dgelu_v = _gelu_and_grad(zv_ref[...])
        mu = jnp.mean(va, axis=-1, keepdims=True)
        xc = va - mu
        rs = lax.rsqrt(jnp.mean(xc * xc, axis=-1, keepdims=True) + EPS)
        vhat = xc * rs
        vn_s[...] = (vhat * g_ref[...] + b_ref[...]).astype(MXU_DTYPE)
        gbs_acc = jnp.zeros((CHUNK, 128), F32)
        for g in range(A_GROUPS):
            w = _tril_weights(ws_ref, g)
            bias = bs_ref[:, g:g + 1]
            cols = slice(g * CHUNK, (g + 1) * CHUNK)
            gw_acc = jnp.zeros((CHUNK, CHUNK), F32)
            for c in range(tr // CHUNK):
                rows = slice(c * CHUNK, (c + 1) * CHUNK)
                vn = vn_s[rows, cols]
                mixed = jnp.dot(w, vn, preferred_element_type=F32) + bias
                ua, dgelu_u = _gelu_and_grad(zu_ref[rows, cols])
                dmv = dm_ref[rows, cols]
                sa = _sigmoid(zga_ref[rows, cols])
                dya = dmv * sa
                dz_ref[rows, 2 * D_MODEL + g * CHUNK:2 * D_MODEL + (g + 1) * CHUNK] = (
                    dmv * (ua * mixed) * (sa * (1.0 - sa))).astype(dz_ref.dtype)
                dz_ref[rows, cols] = (dya * mixed * dgelu_u).astype(dz_ref.dtype)
                dmix = dya * ua
                gbs_acc = gbs_acc + jnp.where(lane == g, jnp.sum(dmix, axis=-1, keepdims=True), 0.0)
                dmix_b = dmix.astype(MXU_DTYPE)
                gw_acc = gw_acc + lax.dot_general(dmix_b, vn, _DIMS["nt"], preferred_element_type=F32)
                dvn_s[rows, cols] = lax.dot_general(w, dmix_b, _DIMS["tn"], preferred_element_type=F32)
            gws_ref[g] += jnp.where(_causal_mask(CHUNK), gw_acc, 0.0)
        gbs_ref[...] += gbs_acc

        dvn = dvn_s[...]
        glg_ref[...] += jnp.sum(dvn * vhat, axis=0, keepdims=True)
        glb_ref[...] += jnp.sum(dvn, axis=0, keepdims=True)
        dvh = dvn * g_ref[...]
        dva = rs * (dvh - jnp.mean(dvh, axis=-1, keepdims=True) - vhat * jnp.mean(dvh * vhat, axis=-1, keepdims=True))
        dz_ref[:, D_MODEL:2 * D_MODEL] = (dva * dgelu_v).astype(dz_ref.dtype)

        dmv = dm_ref[...]
        ybv = yb_ref[...]
        sb = _sigmoid(zgb_ref[...])
        dyb = dmv * sb
        dyb_ref[...] = dyb.astype(dyb_ref.dtype)
        dz_ref[:, 3 * D_MODEL:4 * D_MODEL] = (dmv * ybv * (sb * (1.0 - sb))).astype(dz_ref.dtype)
        dz_ref[:, 4 * D_MODEL:] = jnp.zeros((tr, LAT), dz_ref.dtype)
        prod = dyb * ybv
        sel = (lax.broadcasted_iota(jnp.int32, (HEADS, D_MODEL), 1) // NOPE
               == lax.broadcasted_iota(jnp.int32, (HEADS, D_MODEL), 0)).astype(jnp.bfloat16)
        hi = prod.astype(jnp.bfloat16)
        rest = prod - hi.astype(F32)
        mid = rest.astype(jnp.bfloat16)
        lo = (rest - mid.astype(F32)).astype(jnp.bfloat16)
        dl_ref[...] = (lax.dot_general(sel, hi, _DIMS["nt"], preferred_element_type=F32)
                       + lax.dot_general(sel, mid, _DIMS["nt"], preferred_element_type=F32)
                       + lax.dot_general(sel, lo, _DIMS["nt"], preferred_element_type=F32))

    return pl.pallas_call(
        body, name="mix_bwd",
        out_shape=(jax.ShapeDtypeStruct((T, IN_PAD), MXU_DTYPE), jax.ShapeDtypeStruct((T, D_MODEL), MXU_DTYPE),
                   jax.ShapeDtypeStruct((HEADS, T), F32), jax.ShapeDtypeStruct((A_GROUPS, CHUNK, CHUNK), F32),
                   jax.ShapeDtypeStruct((CHUNK, 128), F32), jax.ShapeDtypeStruct((1, D_MODEL), F32),
                   jax.ShapeDtypeStruct((1, D_MODEL), F32)),
        grid=(T // tr,),
        in_specs=[zcol(0), zcol(1), zcol(2), zcol(3), row, row, vec, vec, ws_spec, bs_spec],
        out_specs=(pl.BlockSpec((tr, IN_PAD), lambda i: (i, 0)), row, pl.BlockSpec((HEADS, tr), lambda i: (0, i)),
                   ws_spec, bs_spec, vec, vec),
        scratch_shapes=[pltpu.VMEM((tr, D_MODEL), MXU_DTYPE), pltpu.VMEM((tr, D_MODEL), F32)],
        compiler_params=_params(("arbitrary",), 12 * _nbytes((tr, D_MODEL), F32)),
    )(z, z, z, z, yb, dm, ln_g, ln_b, ws, bs_t)


def _lat_bwd(dz, z, dq, dk, dv, gq, gkv, wq, wkv, cos_a, sin_a, tr=256):
    T = z.shape[0]
    lat_blk = (4 * D_MODEL) // LAT

    def body(dz_in, z_ref, dq_ref, dk_ref, dv_ref, gq_ref, gkv_ref, wq_ref, wkv_ref, cos_ref, sin_ref,
             dz_ref, dqr_ref, dkv_ref, ggq_ref, ggkv_ref):
        del dz_in

        @pl.when(pl.program_id(0) == 0)
        def _():
            ggq_ref[...] = jnp.zeros_like(ggq_ref)
            ggkv_ref[...] = jnp.zeros_like(ggkv_ref)

        cos_v, sin_v = cos_ref[...], sin_ref[...]
        dkr = jnp.zeros((tr, 128), F32)
        for h in range(HEADS):
            o = h * HEAD_PAD
            dqr_ref[:, o:o + NOPE] = dq_ref[:, o:o + NOPE].astype(MXU_DTYPE)
            dqr_ref[:, o + NOPE:o + HEAD_PAD] = _rope_mix_bwd(dq_ref[:, o + NOPE:o + HEAD_PAD], cos_v, sin_v).astype(MXU_DTYPE)
            dkv_ref[:, h * NOPE:(h + 1) * NOPE] = dk_ref[:, o:o + NOPE].astype(MXU_DTYPE)
            dkr = dkr + _rope_mix_bwd(dk_ref[:, o + NOPE:o + HEAD_PAD], cos_v, sin_v)
        dkv_ref[:, HEADS * NOPE:] = dv_ref[...]
        dcqn = lax.dot_general(dqr_ref[...], wq_ref[...], _DIMS["nt"], preferred_element_type=F32)
        dckvn = lax.dot_general(dkv_ref[...], wkv_ref[...], _DIMS["nt"], preferred_element_type=F32)

        zl = z_ref[...]

        def rms_bwd(c, dn, g_ref, gg_ref):
            r = lax.rsqrt(jnp.mean(c * c, axis=-1, keepdims=True) + EPS)
            ch = c * r
            gg_ref[...] += jnp.sum(dn * ch, axis=0, keepdims=True)
            dch = dn * g_ref[...]
            return r * (dch - ch * jnp.mean(dch * ch, axis=-1, keepdims=True))

        dz_ref[:, :Q_RANK] = rms_bwd(zl[:, :Q_RANK], dcqn, gq_ref, ggq_ref).astype(dz_ref.dtype)
        dz_ref[:, Q_RANK:Q_RANK + KV_RANK] = rms_bwd(zl[:, Q_RANK:Q_RANK + KV_RANK], dckvn, gkv_ref, ggkv_ref).astype(dz_ref.dtype)
        dz_ref[:, Q_RANK + KV_RANK:] = dkr.astype(dz_ref.dtype)

    def row(w):
        return pl.BlockSpec((tr, w), lambda i: (i, 0))

    def full(a):
        return pl.BlockSpec(a.shape, lambda i: (0, 0))

    lat = pl.BlockSpec((tr, LAT), lambda i: (i, lat_blk))
    return pl.pallas_call(
        body, name="lat_bwd",
        out_shape=(jax.ShapeDtypeStruct(dz.shape, dz.dtype), jax.ShapeDtypeStruct((T, HEADS * HEAD_PAD), MXU_DTYPE),
                   jax.ShapeDtypeStruct((T, 2 * HEADS * NOPE), MXU_DTYPE), jax.ShapeDtypeStruct(gq.shape, F32),
                   jax.ShapeDtypeStruct(gkv.shape, F32)),
        grid=(T // tr,),
        in_specs=[pl.BlockSpec(memory_space=pl.ANY), lat, row(HEADS * HEAD_PAD), row(HEADS * HEAD_PAD), row(HEADS * NOPE),
                  full(gq), full(gkv), full(wq), full(wkv), row(128), row(128)],
        out_specs=(lat, row(HEADS * HEAD_PAD), row(2 * HEADS * NOPE), full(gq), full(gkv)),
        input_output_aliases={0: 0},
        compiler_params=_params(("arbitrary",), 8 * _nbytes((tr, HEADS * HEAD_PAD), F32)),
    )(dz, z, dq, dk, dv, gq, gkv, wq, wkv, cos_a, sin_a)


GATE_ROWS = 64
HALO = 8


def _taps(ref, half, r, first):
    C = GATE_ROWS
    if first:
        xs = jnp.concatenate([jnp.zeros((HALO, ref.shape[-1]), F32), ref[half, 0:C, :]], axis=0)
    else:
        xs = ref[half, pl.ds(pl.multiple_of(r * C - HALO, HALO), C + HALO), :]
    return xs[HALO:, :], pltpu.roll(xs, 1, 0)[HALO:, :], pltpu.roll(xs, 2, 0)[HALO:, :]


def _conv_taps(taps, cw, cb):
    x0, x1, x2 = taps
    return cb + cw[0:1, :] * x2 + cw[1:2, :] * x1 + cw[2:3, :] * x0


def _fold8(x):
    acc = x[0:8, :]
    for i in range(1, x.shape[0] // 8):
        acc = acc + x[8 * i:8 * (i + 1), :]
    return acc


def _gate_fwd(up3, conv_w, conv_b, B, S):
    T = B * S
    W = FF_TILE
    C = GATE_ROWS

    def body(up_ref, cw_ref, cb_ref, act_ref):
        def chunk(r, first):
            gate = _conv_taps(_taps(up_ref, 0, r, first), cw_ref[0], cb_ref[0])
            val = _conv_taps(_taps(up_ref, 1, r, first), cw_ref[1], cb_ref[1])
            base = 0 if first else pl.multiple_of(r * C, C)
            act_ref[pl.ds(base, C), :] = (gate * _sigmoid(gate) * val).astype(act_ref.dtype)

        chunk(0, True)

        @pl.loop(1, S // C)
        def _(r):
            chunk(r, False)

    return pl.pallas_call(
        body, name="gate_fwd", out_shape=jax.ShapeDtypeStruct((T, D_FF), MXU_DTYPE), grid=(B, N_FF_TILES),
        in_specs=[pl.BlockSpec((2, S, W), lambda b, j: (0, b, j)), pl.BlockSpec((2, 3, W), lambda b, j: (0, 0, j)),
                  pl.BlockSpec((2, 1, W), lambda b, j: (0, 0, j))],
        out_specs=pl.BlockSpec((S, W), lambda b, j: (b, j)),
        compiler_params=_params(("parallel", "parallel"), 6 * _nbytes((S, W), F32)),
    )(up3, conv_w, conv_b)


def _gate_bwd(up3, dact, conv_w, conv_b, B, S):
    T = B * S
    W = FF_TILE
    C = GATE_ROWS

    def body(up_ref, da_ref, cw_ref, cb_ref, dup_ref, gcw_ref, gcb_ref, d_s):
        @pl.when(pl.program_id(1) == 0)
        def _():
            gcw_ref[...] = jnp.zeros_like(gcw_ref)
            gcb_ref[...] = jnp.zeros_like(gcb_ref)

        def chunk(r, first, sums):
            rows = pl.ds(0 if first else pl.multiple_of(r * C, C), C)
            taps = [_taps(up_ref, half, r, first) for half in (0, 1)]
            gate = _conv_taps(taps[0], cw_ref[0], cb_ref[0])
            val = _conv_taps(taps[1], cw_ref[1], cb_ref[1])
            sg = _sigmoid(gate)
            da = da_ref[rows, :]
            d_halves = (da * val * (sg * (1.0 + gate * (1.0 - sg))), da * (gate * sg))
            out = []
            for half, dup in enumerate(d_halves):
                d_s[half, rows, :] = dup
                x0, x1, x2 = taps[half]
                sb, s0, s1, s2 = sums[half]
                out.append((sb + _fold8(dup), s0 + _fold8(dup * x2), s1 + _fold8(dup * x1), s2 + _fold8(dup * x0)))
            return tuple(out)

        zeros = tuple(tuple(jnp.zeros((8, W), F32) for _ in range(4)) for _ in range(2))
        sums = chunk(0, True, zeros)
        sums = lax.fori_loop(1, S // C, lambda r, s: chunk(r, False, s), sums)
        for half in (0, 1):
            sb, s0, s1, s2 = sums[half]
            gcb_ref[half] += jnp.sum(sb, axis=0, keepdims=True)
            gcw_ref[half, 0:1, :] += jnp.sum(s0, axis=0, keepdims=True)
            gcw_ref[half, 1:2, :] += jnp.sum(s1, axis=0, keepdims=True)
            gcw_ref[half, 2:3, :] += jnp.sum(s2, axis=0, keepdims=True)

        d_s[:, S:S + HALO, :] = jnp.zeros((2, HALO, W), F32)

        @pl.loop(0, S // C)
        def _(r):
            base = pl.multiple_of(r * C, C)
            for half in (0, 1):
                ds_ = d_s[half, pl.ds(base, C + HALO), :]
                cw = cw_ref[half]
                dx = (cw[2:3, :] * ds_[:C, :] + cw[1:2, :] * pltpu.roll(ds_, C + HALO - 1, 0)[:C, :]
                      + cw[0:1, :] * pltpu.roll(ds_, C + HALO - 2, 0)[:C, :])
                dup_ref[half, pl.ds(base, C), :] = dx.astype(dup_ref.dtype)

    up_spec = pl.BlockSpec((2, S, W), lambda j, b: (0, b, j))
    cw_spec = pl.BlockSpec((2, 3, W), lambda j, b: (0, 0, j))
    cb_spec = pl.BlockSpec((2, 1, W), lambda j, b: (0, 0, j))
    return pl.pallas_call(
        body, name="gate_bwd",
        out_shape=(jax.ShapeDtypeStruct((2, T, D_FF), MXU_DTYPE), jax.ShapeDtypeStruct((2, 3, D_FF), F32),
                   jax.ShapeDtypeStruct((2, 1, D_FF), F32)),
        grid=(N_FF_TILES, B),
        in_specs=[up_spec, pl.BlockSpec((S, W), lambda j, b: (b, j)), cw_spec, cb_spec],
        out_specs=(up_spec, cw_spec, cb_spec),
        scratch_shapes=[pltpu.VMEM((2, S + HALO, W), F32)],
        compiler_params=_params(("parallel", "arbitrary"), 10 * _nbytes((S, W), F32)),
    )(up3, dact, conv_w, conv_b)


def _final(x2, tgt, g, tr=512):
    T, D = x2.shape

    def body(x_ref, t_ref, g_ref, dx_ref, loss_ref, gg_ref):
        @pl.when(pl.program_id(0) == 0)
        def _():
            loss_ref[...] = jnp.zeros_like(loss_ref)
            gg_ref[...] = jnp.zeros_like(gg_ref)

        xv = x_ref[...]
        gv = g_ref[...]
        r = lax.rsqrt(jnp.mean(xv * xv, axis=-1, keepdims=True) + EPS)
        xn = xv * r
        err = xn * gv - t_ref[...]
        loss_ref[...] += 0.5 * jnp.sum(jnp.mean(err * err, axis=-1, keepdims=True), axis=0, keepdims=True)
        dy = err * (1.0 / D)
        gg_ref[...] += jnp.sum(dy * xn, axis=0, keepdims=True)
        dxn = dy * gv
        dx_ref[...] = r * (dxn - xn * jnp.mean(dxn * xn, axis=-1, keepdims=True))

    row = pl.BlockSpec((tr, D), lambda i: (i, 0))
    vec = pl.BlockSpec((1, D), lambda i: (0, 0))
    return pl.pallas_call(
        body, name="final_loss",
        out_shape=(jax.ShapeDtypeStruct((T, D), F32), jax.ShapeDtypeStruct((1, 128), F32), jax.ShapeDtypeStruct((1, D), F32)),
        grid=(T // tr,), in_specs=[row, row, vec],
        out_specs=(row, pl.BlockSpec((1, 128), lambda i: (0, 0)), vec),
        compiler_params=_params(("arbitrary",), 6 * _nbytes((tr, D), F32)),
    )(x2, tgt, g)


def _sum_slabs(parts, name, tr):
    rows, cols = parts[0].shape
    n = len(parts)

    def body(*refs):
        acc = refs[0][...]
        for r in refs[1:n]:
            acc = acc + r[...]
        refs[n][...] = acc

    blk = pl.BlockSpec((tr, cols), lambda i: (i, 0))
    return pl.pallas_call(
        body, name=name, out_shape=jax.ShapeDtypeStruct((rows, cols), F32), grid=(rows // tr,),
        in_specs=[blk] * n, out_specs=blk,
        compiler_params=_params(("parallel",), (n + 1) * _nbytes((tr, cols), F32)),
    )(*parts)


ADAMW_BLOCK_BYTES = 2400 * 1024


def _adamw(w, g, m, v, name):
    lead = w.ndim == 3
    rows, cols = w.shape[-2:]
    fits = [d for d in range(8, rows + 1, 8) if rows % d == 0 and d * cols * 4 <= ADAMW_BLOCK_BYTES]
    tr = max(fits) if fits else rows
    c1 = 1.0 - ADAM_B1 ** ADAM_STEP
    c2 = 1.0 - ADAM_B2 ** ADAM_STEP

    def body(w_ref, g_ref, m_ref, v_ref, d_ref, nm_ref, nv_ref):
        gv = g_ref[...]
        nm = ADAM_B1 * m_ref[...] + (1.0 - ADAM_B1) * gv
        nv = ADAM_B2 * v_ref[...] + (1.0 - ADAM_B2) * (gv * gv)
        nm_ref[...] = nm
        nv_ref[...] = nv
        d_ref[...] = -ADAM_LR * ((nm / c1) / (jnp.sqrt(nv / c2) + ADAM_EPS) + ADAM_WD * w_ref[...])

    blk = pl.BlockSpec((None, tr, cols), lambda i: (0, i, 0)) if lead else pl.BlockSpec((tr, cols), lambda i: (i, 0))
    sds = jax.ShapeDtypeStruct(w.shape, F32)
    return pl.pallas_call(
        body, name=name, out_shape=(sds, sds, sds), grid=(rows // tr,), in_specs=[blk] * 4, out_specs=(blk, blk, blk),
        compiler_params=_params(("parallel",), 7 * _nbytes((tr, cols), F32)),
    )(w, g, m, v)


_ANY = pl.BlockSpec(memory_space=pl.ANY)


def _place():
    x, y, c = lax.axis_index("x"), lax.axis_index("y"), lax.axis_index("c")
    chips = [(1 - x, y), (x, 1 - y), (1 - x, 1 - y)]
    return x, y, c, chips


def _gather_weights(shards):
    n = len(shards)

    def body(*refs):
        ins, outs = refs[:n], refs[n:2 * n]
        send, recv, fsend, frecv, osend, orecv = refs[2 * n:]
        x, y, c, chips = _place()
        me = 2 * x + y
        first, passed = [], []
        for w in range(n):
            first.append(pltpu.make_async_remote_copy(
                src_ref=ins[w], dst_ref=outs[w].at[me], send_sem=osend.at[w], recv_sem=orecv.at[w],
                device_id=(x, y, 1 - c), device_id_type=MESH))
        for w in range(n):
            for j, (px, py) in enumerate(chips):
                first.append(pltpu.make_async_remote_copy(
                    src_ref=ins[w].at[c], dst_ref=outs[w].at[me, c], send_sem=send.at[3 * w + j],
                    recv_sem=recv.at[3 * w + j], device_id=(px, py, c), device_id_type=MESH))
        for cp in first:
            cp.start()
        for w in range(n):
            for j, (px, py) in enumerate(chips):
                landed = outs[w].at[2 * px + py, c]
                pltpu.make_async_remote_copy(src_ref=landed, dst_ref=landed, send_sem=send.at[3 * w + j],
                                             recv_sem=recv.at[3 * w + j], device_id=(px, py, c),
                                             device_id_type=MESH).wait_recv()
                fw = pltpu.make_async_remote_copy(src_ref=landed, dst_ref=landed, send_sem=fsend.at[3 * w + j],
                                                  recv_sem=frecv.at[3 * w + j], device_id=(x, y, 1 - c),
                                                  device_id_type=MESH)
                fw.start()
                passed.append(fw)
        for w in range(n):
            for j, (px, py) in enumerate(chips):
                other = outs[w].at[2 * px + py, 1 - c]
                pltpu.make_async_remote_copy(src_ref=other, dst_ref=other, send_sem=fsend.at[3 * w + j],
                                             recv_sem=frecv.at[3 * w + j], device_id=(x, y, 1 - c),
                                             device_id_type=MESH).wait_recv()
        for w in range(n):
            own = outs[w].at[me]
            pltpu.make_async_remote_copy(src_ref=own, dst_ref=own, send_sem=osend.at[w], recv_sem=orecv.at[w],
                                         device_id=(x, y, 1 - c), device_id_type=MESH).wait_recv()
        for cp in first + passed:
            cp.wait_send()

    dma = lambda k: pltpu.SemaphoreType.DMA((k,))
    return pl.pallas_call(
        body, name="gather_weights",
        out_shape=tuple(jax.ShapeDtypeStruct((N_CHIPS,) + s.shape, s.dtype) for s in shards),
        in_specs=[_ANY] * n, out_specs=tuple([_ANY] * n),
        scratch_shapes=[dma(3 * n), dma(3 * n), dma(3 * n), dma(3 * n), dma(n), dma(n)],
    )(*shards)


_HBM = pl.BlockSpec(memory_space=pltpu.HBM)
_SEM = pl.BlockSpec(memory_space=pltpu.SEMAPHORE)
_EFFECT = pltpu.SideEffectType.DATAFLOW_SIDE_EFFECTING


SEMS_PER_ARRAY = 8


def _exchange_copies(srcs, lands, send, recv, mode):
    x, y, c, chips = _place()
    if mode == "all":
        flips = [(fx, fy, fc) for fx in (0, 1) for fy in (0, 1) for fc in (0, 1)][1:]
        peers = [(x ^ fx, y ^ fy, c ^ fc) for fx, fy, fc in flips]
        slot = 4 * x + 2 * y + c
    else:
        peers = [(px, py, c) for px, py in chips] + ([(x, y, 1 - c)] if mode == "gather" else [])
        slot = 2 * x + y
    cps = []
    for w, (src, land) in enumerate(zip(srcs, lands)):
        for k, peer in enumerate(peers):
            piece = src.at[2 * peer[0] + peer[1]] if mode == "scatter" else src
            cps.append(pltpu.make_async_remote_copy(
                src_ref=piece, dst_ref=land.at[slot], send_sem=send.at[SEMS_PER_ARRAY * w + k],
                recv_sem=recv.at[SEMS_PER_ARRAY * w + k], device_id=peer, device_id_type=MESH))
    return cps


def _exchange_start(srcs, name, mode, after):
    n = len(srcs)
    lead = {"gather": (N_CHIPS,), "scatter": (), "all": (2 * N_CHIPS,)}[mode]
    land_shapes = [lead + s.shape for s in srcs]

    def body(*refs):
        src_refs, land_refs = refs[:n], refs[n:2 * n]
        send, recv = refs[2 * n + 1], refs[2 * n + 2]
        token = refs[-1]
        for cp in _exchange_copies(src_refs, land_refs, send, recv, mode):
            cp.start()
        token[...] = jnp.zeros_like(token)

    sems = pltpu.SemaphoreType.DMA((SEMS_PER_ARRAY * n,))
    out = pl.pallas_call(
        body, name=name,
        out_shape=(sems, sems, *[pltpu.HBM(s.shape, s.dtype) for s in srcs],
                   *[pltpu.HBM(shp, s.dtype) for shp, s in zip(land_shapes, srcs)], jax.ShapeDtypeStruct((8, 128), F32)),
        in_specs=[_HBM] * (2 * n) + [_ANY],
        out_specs=(_SEM, _SEM, *[_HBM] * (2 * n), pl.BlockSpec(memory_space=pltpu.VMEM)),
        input_output_aliases={i: 2 + i for i in range(2 * n)},
        compiler_params=pltpu.CompilerParams(has_side_effects=_EFFECT),
    )(*[pltpu.with_memory_space_constraint(s, pltpu.HBM) for s in srcs],
      *[pltpu.with_memory_space_constraint(lax.empty(shp, s.dtype), pltpu.HBM) for shp, s in zip(land_shapes, srcs)],
      after)
    return out[0], out[1], out[2:2 + n], out[2 + n:2 + 2 * n], out[-1]


def _exchange_wait(started, name, mode, after):
    send, recv, src_thru, land_thru, _ = started
    n = len(src_thru)

    def body(*refs):
        src_refs, land_refs, send_ref, recv_ref = refs[:n], refs[n:2 * n], refs[2 * n], refs[2 * n + 1]
        for cp in _exchange_copies(src_refs, land_refs, send_ref, recv_ref, mode):
            cp.wait_send()
            cp.wait_recv()

    out = pl.pallas_call(
        body, name=name,
        out_shape=tuple(pltpu.HBM(a.shape, a.dtype) for a in list(src_thru) + list(land_thru)),
        in_specs=[_HBM] * (2 * n) + [_SEM, _SEM, _ANY], out_specs=tuple([_HBM] * (2 * n)),
        input_output_aliases={i: i for i in range(2 * n)},
        compiler_params=pltpu.CompilerParams(has_side_effects=_EFFECT),
    )(*src_thru, *land_thru, send, recv, after)
    return out[:n], out[n:]


def _swap_halves(gs, name):
    n = len(gs)

    def body(*refs):
        ins, outs, send, recv = refs[:n], refs[n:2 * n], refs[2 * n], refs[2 * n + 1]
        x, y, c, _ = _place()
        cps = []
        for w in range(n):
            cps.append(pltpu.make_async_remote_copy(
                src_ref=ins[w].at[:, 1 - c], dst_ref=outs[w], send_sem=send.at[w], recv_sem=recv.at[w],
                device_id=(x, y, 1 - c), device_id_type=MESH))
        for cp in cps:
            cp.start()
        for cp in cps:
            cp.wait()

    return pl.pallas_call(
        body, name=name,
        out_shape=tuple(jax.ShapeDtypeStruct((g.shape[0],) + g.shape[2:], g.dtype) for g in gs),
        in_specs=[_ANY] * n, out_specs=tuple([_ANY] * n),
        scratch_shapes=[pltpu.SemaphoreType.DMA((n,)), pltpu.SemaphoreType.DMA((n,))],
    )(*gs)


GRAD_PAYLOAD = jnp.bfloat16


def _half_blocks(half_rows, cols):
    if (half_rows // 2) % 16 == 0:
        return (half_rows // 2, cols), (lambda r: (r, 0))
    assert cols % 256 == 0, (half_rows, cols)
    return (half_rows, cols // 2), (lambda r: (0, r))


def _pair_sum(gs, gots, name):
    n = len(gs)
    core = lax.axis_index("c").astype(jnp.int32).reshape(1)

    def body(core_ref, *refs):
        del core_ref
        for w in range(n):
            refs[2 * n + w][...] = (refs[w][...] + refs[n + w][...]).astype(GRAD_PAYLOAD)

    in_specs, out_specs, out_shape, nbytes = [], [], [], 0
    cuts = [_half_blocks(g.shape[1] // 2, g.shape[2]) for g in gs]
    for g, ((br, bc), at) in zip(gs, cuts):
        per_half = (g.shape[1] // 2) // br
        in_specs.append(pl.BlockSpec((1, br, bc), lambda s, r, core, at=at, per_half=per_half:
                                     (s, per_half * core[0] + at(r)[0], at(r)[1])))
        nbytes += 3 * _nbytes((br, bc), F32)
    for g, ((br, bc), at) in zip(gs, cuts):
        in_specs.append(pl.BlockSpec((1, br, bc), lambda s, r, core, at=at: (s,) + at(r)))
        out_specs.append(pl.BlockSpec((1, br, bc), lambda s, r, core, at=at: (s,) + at(r)))
        out_shape.append(jax.ShapeDtypeStruct((g.shape[0], g.shape[1] // 2, g.shape[2]), GRAD_PAYLOAD))
    return pl.pallas_call(
        body, name=name, out_shape=tuple(out_shape),
        grid_spec=pltpu.PrefetchScalarGridSpec(num_scalar_prefetch=1, grid=(N_CHIPS, 2), in_specs=in_specs,
                                               out_specs=tuple(out_specs)),
        compiler_params=_params(("parallel", "parallel"), nbytes),
    )(core, *gs, *gots)


def _chip_sum(ps, landed):
    n = len(ps)
    x, y, c = lax.axis_index("x"), lax.axis_index("y"), lax.axis_index("c")
    where = jnp.stack([2 * x + y, 2 * (1 - x) + y, 2 * x + (1 - y), 2 * (1 - x) + (1 - y), c]).astype(jnp.int32)

    def body(where_ref, *refs):
        del where_ref
        for w in range(n):
            terms = [refs[4 * w + t][...].astype(F32) for t in range(4)]
            refs[4 * n + w][...] = ((terms[0] + terms[1]) + terms[2]) + terms[3]

    in_specs, out_specs, out_shape, args, nbytes = [], [], [], [], 0
    for p, a in zip(ps, landed):
        (br, bc), at = _half_blocks(a.shape[1], a.shape[2])
        blk = (1, br, bc)
        in_specs.append(pl.BlockSpec(blk, lambda r, where, at=at: (where[0],) + at(r)))
        args.append(p)
        for t in (1, 2, 3):
            in_specs.append(pl.BlockSpec(blk, lambda r, where, t=t, at=at: (where[t],) + at(r)))
            args.append(a)
        out_specs.append(pl.BlockSpec(blk, lambda r, where, at=at: (where[4],) + at(r)))
        out_shape.append(jax.ShapeDtypeStruct((2,) + a.shape[1:], F32))
        nbytes += 4 * _nbytes(blk, F32)
    return pl.pallas_call(
        body, name="grad_chip_sum", out_shape=tuple(out_shape),
        grid_spec=pltpu.PrefetchScalarGridSpec(num_scalar_prefetch=1, grid=(2,), in_specs=in_specs,
                                               out_specs=tuple(out_specs)),
        compiler_params=_params(("parallel",), nbytes),
    )(where, *args)


def _join_halves(ss):
    n = len(ss)

    def body(*refs):
        outs, send, recv = refs[n:2 * n], refs[2 * n], refs[2 * n + 1]
        x, y, c, _ = _place()
        cps = []
        for w in range(n):
            cps.append(pltpu.make_async_remote_copy(
                src_ref=outs[w].at[c], dst_ref=outs[w].at[c], send_sem=send.at[w], recv_sem=recv.at[w],
                device_id=(x, y, 1 - c), device_id_type=MESH))
        for cp in cps:
            cp.start()
        for w in range(n):
            got = outs[w].at[1 - c]
            pltpu.make_async_remote_copy(src_ref=got, dst_ref=got, send_sem=send.at[w], recv_sem=recv.at[w],
                                         device_id=(x, y, 1 - c), device_id_type=MESH).wait_recv()
        for cp in cps:
            cp.wait_send()

    dma = lambda k: pltpu.SemaphoreType.DMA((k,))
    return pl.pallas_call(
        body, name="grad_join_halves",
        out_shape=tuple(jax.ShapeDtypeStruct(s.shape, s.dtype) for s in ss),
        in_specs=[_ANY] * n, out_specs=tuple([_ANY] * n), input_output_aliases={w: w for w in range(n)},
        scratch_shapes=[dma(n), dma(n)],
    )(*ss)


def _rot_cols(w, axis=-1):
    a, b = jnp.split(w, 2, axis=axis)
    return jnp.concatenate([-b, a], axis=axis)


def _rot_cols_t(g, axis=-1):
    a, b = jnp.split(g, 2, axis=axis)
    return jnp.concatenate([b, -a], axis=axis)


def _cols_from_chips(a):
    n, r, cs = a.shape
    return jnp.transpose(a, (1, 0, 2)).reshape(r, n * cs)


def _cols_to_chips(a):
    r, cc = a.shape
    return jnp.transpose(a.reshape(r, N_CHIPS, cc // N_CHIPS), (1, 0, 2))


def _conv_w_split(cw):
    return jnp.swapaxes(cw.reshape(3, 2, D_FF), 0, 1)


def _conv_w_join(g):
    return jnp.swapaxes(g, 0, 1).reshape(3, 2 * D_FF)


_SEG =(D_MODEL, 2 * D_MODEL, 2 * D_MODEL + Q_RANK, 2 * D_MODEL + Q_RANK + KV_RANK, 2 * D_MODEL + Q_RANK + KV_RANK + ROPE,
        3 * D_MODEL + Q_RANK + KV_RANK + ROPE)


def _w_in_t_to_pad(wt):
    u, v, cq, ckv, kr, ga, gb = jnp.split(wt, _SEG, axis=0)
    return jnp.concatenate([u, v, ga, gb, cq, ckv, kr, _rot_cols(kr, axis=0)], axis=0)


def _w_in_t_from_pad(gt):
    u, v, ga, gb, cq, ckv, kr, krr = jnp.split(
        gt, (D_MODEL, 2 * D_MODEL, 3 * D_MODEL, 4 * D_MODEL, 4 * D_MODEL + Q_RANK, 4 * D_MODEL + Q_RANK + KV_RANK,
             4 * D_MODEL + Q_RANK + KV_RANK + ROPE), axis=0)
    return jnp.concatenate([u, v, cq, ckv, kr + _rot_cols_t(krr, axis=0), ga, gb], axis=0)


def _w_uq_to_pad(w):
    t = w.reshape(Q_RANK, HEADS, QK_DIM)
    nope, rope = t[..., :NOPE], t[..., NOPE:]
    return jnp.concatenate([nope, rope, _rot_cols(rope)], axis=-1).reshape(Q_RANK, HEADS * HEAD_PAD)


def _w_uq_from_pad(g):
    t = g.reshape(Q_RANK, HEADS, HEAD_PAD)
    nope, rope, rot = t[..., :NOPE], t[..., NOPE:QK_DIM], t[..., QK_DIM:]
    return jnp.concatenate([nope, rope + _rot_cols_t(rot)], axis=-1).reshape(Q_RANK, HEADS * QK_DIM)


def _w_ukv_to_pad(w):
    t = w.reshape(KV_RANK, HEADS, 2, NOPE)
    return jnp.swapaxes(t, 1, 2).reshape(KV_RANK, 2 * HEADS * NOPE)


def _w_ukv_from_pad(g):
    t = g.reshape(KV_RANK, 2, HEADS, NOPE)
    return jnp.swapaxes(t, 1, 2).reshape(KV_RANK, 2 * HEADS * NOPE)


def _rope_tables(positions):
    inv_freq = 1.0 / (ROPE_THETA ** (jnp.arange(0, ROPE, 2, dtype=F32) / ROPE))
    ang = positions.astype(F32).reshape(-1, 1) * inv_freq
    cos, sin = jnp.cos(ang), jnp.sin(ang)
    zero = jnp.zeros((ang.shape[0], 64), F32)
    return jnp.concatenate([cos, cos, zero], axis=1), jnp.concatenate([sin, sin, zero], axis=1)


_BIG = ("w_in", "w_uq", "w_ukv", "w_out", "w_up", "w_down")
UP_SHARD = 2 * D_FF // N_CHIPS


def _local_step(x, positions, tgt, wts, mixer_weights, ffn_weights, on_ffn_grads, on_mixer_grads):
    B, S, D = x.shape
    T = B * S
    xf = x.reshape(T, D)
    cos_a, sin_a = _rope_tables(positions)
    bs_t = jnp.pad(wts["a_spatial_b"].T, ((0, 0), (0, 128 - A_GROUPS)))

    h = _rms_fwd(xf, wts["mix_norm"], "norm1_fwd")
    z = _mm(h, wts["w_in"], "nt", "in_proj", tm=512, tn=1536, tk=D)
    wts = dict(wts)
    wts["w_q"], wts["w_kv"], wts["w_out"] = mixer_weights(z)
    q, k, v, cqn, ckvn = _lat_fwd(z, wts["q_a_norm"], wts["kv_a_norm"], wts["w_q"], wts["w_kv"], cos_a, sin_a)
    yb, *lses = _attn_fwd(q, k, v, B, S)
    merged = _mix_fwd(z, yb, wts["a_v_norm_g"], wts["a_v_norm_b"], wts["a_spatial_w"], bs_t)
    x1 = _mm(merged, wts["w_out"], "nn", "out_proj", tm=512, tn=D, tk=D, add=xf)
    h2 = _rms_fwd(x1, wts["ffn_norm"], "norm2_fwd")
    wts["w_up"], wts["w_down"], wts["conv_w"] = ffn_weights(h2)
    up_pre = _mm(h2, wts["w_up"], "nn", "up_proj", tm=512, tn=UP_SHARD, tk=D, dims=(T, 2 * D_FF, D),
                 b_spec=pl.BlockSpec((None, D, UP_SHARD), lambda i, j, k: (j, 0, 0)),
                 o_spec=pl.BlockSpec((None, 512, UP_SHARD), lambda i, j, k: (j // 2, i, j % 2)), out_shape=(2, T, D_FF))
    act = _gate_fwd(up_pre, wts["conv_w"], wts["conv_b"], B, S)
    x2 = _mm(act, wts["w_down"], "nn", "down_proj", tm=512, tn=D, tk=1408, add=x1)
    dx2, loss_row, g_final = _final(x2, tgt.reshape(T, D), wts["final_norm"])

    g = {"final_norm": g_final}
    dact = _mm(dx2, wts["w_down"], "nt", "down_proj_dx", tm=512, tn=1408, tk=D)
    tk2, tk1 = min(2048, T), min(1024, T)
    g["w_down"] = _mm(act, dx2, "tn", "down_proj_dw", tm=1408, tn=D, tk=tk1)
    dup, g["conv_w"], g["conv_b"] = _gate_bwd(up_pre, dact, wts["conv_w"], wts["conv_b"], B, S)
    g["w_up"] = _mm(h2, dup, "tn", "up_proj_dw", tm=D, tn=UP_SHARD, tk=tk2, dims=(D, 2 * D_FF, T),
                    b_spec=pl.BlockSpec((None, tk2, UP_SHARD), lambda i, j, k: (j // 2, k, j % 2)),
                    o_spec=pl.BlockSpec((None, D, UP_SHARD), lambda i, j, k: (j, 0, 0)), out_shape=(N_CHIPS, D, UP_SHARD))
    ffn_sent = on_ffn_grads(g["w_up"], g["w_down"])
    dh2 = _mm(dup, wts["w_up"], "nt", "up_proj_dx", tm=512, tn=D, tk=UP_SHARD, dims=(T, D, 2 * D_FF),
              a_spec=pl.BlockSpec((None, 512, UP_SHARD), lambda i, j, k: (k // 2, i, k % 2)),
              b_spec=pl.BlockSpec((None, D, UP_SHARD), lambda i, j, k: (k, 0, 0)))
    token = None if ffn_sent is None else ffn_sent(dh2)
    ffn_norm = wts["ffn_norm"] if token is None else wts["ffn_norm"] + token[0:1, 0:1]
    dx1, g["ffn_norm"] = _rms_bwd(x1, ffn_norm, dh2, dx2, "norm2_bwd")
    dm = _mm(dx1, wts["w_out"], "nt", "out_proj_dx", tm=512, tn=D, tk=D)
    g["w_out"] = _mm(merged, dx1, "tn", "out_proj_dw", tm=D, tn=D, tk=tk1)
    dz, dyb, dl, g["a_spatial_w"], gbs, g["a_v_norm_g"], g["a_v_norm_b"] = _mix_bwd(
        z, yb, dm, wts["a_v_norm_g"], wts["a_v_norm_b"], wts["a_spatial_w"], bs_t)
    g["a_spatial_b"] = gbs[:, :A_GROUPS].T
    delta = dl.reshape(HEADS * T // ATT_BLOCK, 1, ATT_BLOCK)
    dq, dk, dv = _attn_bwd(q, k, v, dyb, lses, delta, B, S)
    dz, dq_raw, dkv, g["q_a_norm"], g["kv_a_norm"] = _lat_bwd(
        dz, z, dq, dk, dv, wts["q_a_norm"], wts["kv_a_norm"], wts["w_q"], wts["w_kv"], cos_a, sin_a)
    g["w_q"] = _mm(cqn, dq_raw, "tn", "q_proj_dw", tm=Q_RANK, tn=HEADS * HEAD_PAD, tk=tk2)
    g["w_kv"] = _mm(ckvn, dkv, "tn", "kv_proj_dw", tm=KV_RANK, tn=2 * HEADS * NOPE, tk=tk2)
    g["w_in"] = _mm(dz, h, "tn", "in_proj_dw", tm=1536, tn=D, tk=tk2)
    token = on_mixer_grads(g)
    mix_norm = wts["mix_norm"] if token is None else wts["mix_norm"] + token[0:1, 0:1]
    dh = _mm(dz, wts["w_in"], "nn", "in_proj_dx", tm=512, tn=D, tk=1536)
    dx, g["mix_norm"] = _rms_bwd(xf, mix_norm, dh, dx1, "norm1_bwd")
    return loss_row[0, 0], dx.reshape(B, S, D), g


_SMALL = (("mix_norm", (1, D_MODEL)), ("a_v_norm_g", (1, D_MODEL)), ("a_v_norm_b", (1, D_MODEL)),
          ("a_spatial_w", (A_GROUPS * CHUNK, CHUNK)), ("a_spatial_b", (1, A_GROUPS * CHUNK)), ("q_a_norm", (1, Q_RANK)),
          ("kv_a_norm", (1, KV_RANK)), ("ffn_norm", (1, D_MODEL)), ("conv_b", (1, 2 * D_FF)), ("final_norm", (1, D_MODEL)),
          ("conv_w", (3, 2 * D_FF)))
_SMALL_SIZE = sum(math.prod(s) for _, s in _SMALL)
_SMALL_ROWS = -(-(_SMALL_SIZE + 1) // (128 * 8)) * 8


def kernel(x, positions, mix_norm, w_in, a_v_norm_g, a_v_norm_b, a_spatial_w, a_spatial_b, q_a_norm, w_uq, kv_a_norm, w_ukv, w_out, ffn_norm, w_up, conv_w, conv_b, w_down, final_norm, loss_target, m_mix_norm, m_w_in, m_a_v_norm_g, m_a_v_norm_b, m_a_spatial_w, m_a_spatial_b, m_q_a_norm, m_w_uq, m_kv_a_norm, m_w_ukv, m_w_out, m_ffn_norm, m_w_up, m_conv_w, m_conv_b, m_w_down, m_final_norm, v_mix_norm, v_w_in, v_a_v_norm_g, v_a_v_norm_b, v_a_spatial_w, v_a_spatial_b, v_q_a_norm, v_w_uq, v_kv_a_norm, v_w_ukv, v_w_out, v_ffn_norm, v_w_up, v_conv_w, v_conv_b, v_w_down, v_final_norm):
    weights = dict(mix_norm=mix_norm, w_in=w_in, a_v_norm_g=a_v_norm_g, a_v_norm_b=a_v_norm_b, a_spatial_w=a_spatial_w,
                   a_spatial_b=a_spatial_b, q_a_norm=q_a_norm, w_uq=w_uq, kv_a_norm=kv_a_norm, w_ukv=w_ukv, w_out=w_out,
                   ffn_norm=ffn_norm, w_up=w_up, conv_w=conv_w, conv_b=conv_b, w_down=w_down, final_norm=final_norm)
    m_in = dict(mix_norm=m_mix_norm, w_in=m_w_in, a_v_norm_g=m_a_v_norm_g, a_v_norm_b=m_a_v_norm_b,
                a_spatial_w=m_a_spatial_w, a_spatial_b=m_a_spatial_b, q_a_norm=m_q_a_norm, w_uq=m_w_uq,
                kv_a_norm=m_kv_a_norm, w_ukv=m_w_ukv, w_out=m_w_out, ffn_norm=m_ffn_norm, w_up=m_w_up, conv_w=m_conv_w,
                conv_b=m_conv_b, w_down=m_w_down, final_norm=m_final_norm)
    v_in = dict(mix_norm=v_mix_norm, w_in=v_w_in, a_v_norm_g=v_a_v_norm_g, a_v_norm_b=v_a_v_norm_b,
                a_spatial_w=v_a_spatial_w, a_spatial_b=v_a_spatial_b, q_a_norm=v_q_a_norm, w_uq=v_w_uq,
                kv_a_norm=v_kv_a_norm, w_ukv=v_w_ukv, w_out=v_w_out, ffn_norm=v_ffn_norm, w_up=v_w_up, conv_w=v_conv_w,
                conv_b=v_conv_b, w_down=v_w_down, final_norm=v_final_norm)
    names = list(weights)
    chip = 2 * lax.axis_index("x") + lax.axis_index("y")

    def halves(a):
        return a.reshape(a.shape[:-2] + (2, a.shape[-2] // 2, a.shape[-1]))

    def whole(a):
        return a.reshape(a.shape[:-3] + (2 * a.shape[-2], a.shape[-1]))

    w_in_t = jnp.swapaxes(w_in[0], 0, 1).astype(MXU_DTYPE)
    (w_in_sh,) = _gather_weights([jnp.stack(jnp.split(w_in_t, 2, axis=1))])
    mixer_gather = _exchange_start([weights[n][0].astype(MXU_DTYPE) for n in _BIG[1:4]],
                                   "mixer_gather_start", "gather", after=w_in_sh)
    ffn_gather = _exchange_start([w_up[0].astype(MXU_DTYPE), w_down[0].astype(MXU_DTYPE), conv_w[0]],
                                 "ffn_gather_start", "gather", after=mixer_gather[4])
    wts = dict(
        mix_norm=mix_norm + ffn_gather[4][0:1, 0:1], a_v_norm_g=a_v_norm_g, a_v_norm_b=a_v_norm_b,
        a_spatial_w=a_spatial_w[0], a_spatial_b=a_spatial_b[0], q_a_norm=q_a_norm, kv_a_norm=kv_a_norm,
        ffn_norm=ffn_norm, final_norm=final_norm.reshape(1, D_MODEL),
        w_in=_w_in_t_to_pad(jnp.concatenate([w_in_sh[:, 0], w_in_sh[:, 1]], axis=-1).reshape(-1, D_MODEL)),
        conv_b=conv_b.reshape(2, 1, D_FF))

    def mixer_weights(after):
        _, (w_uq_sh, w_ukv_sh, w_out_sh) = _exchange_wait(mixer_gather, "mixer_gather_wait", "gather", after)
        return (_w_uq_to_pad(_cols_from_chips(w_uq_sh)), _w_ukv_to_pad(_cols_from_chips(w_ukv_sh)),
                w_out_sh.reshape(D_MODEL, D_MODEL))

    def ffn_weights(after):
        _, (w_up_sh, w_down_sh, cw_all) = _exchange_wait(ffn_gather, "ffn_gather_wait", "gather", after)
        return w_up_sh, w_down_sh.reshape(D_FF, D_MODEL), _conv_w_split(_cols_from_chips(cw_all))

    scatters = {}

    def start_scatter(slabs, tag):
        sums = _pair_sum(slabs, _swap_halves([halves(s) for s in slabs], tag + "_grad_swap_halves"), tag + "_grad_pair_sum")
        scatters[tag] = _exchange_start(list(sums), tag + "_scatter_start", "scatter", after=slabs[-1])
        return scatters[tag][4]

    def on_ffn_grads(g_w_up, g_w_down):
        token = start_scatter([g_w_up, g_w_down.reshape(N_CHIPS, D_FF // N_CHIPS, D_MODEL)], "ffn")
        return lambda after: token

    def on_mixer_grads(g):
        return start_scatter(
            [_w_in_t_from_pad(g["w_in"]).reshape(N_CHIPS, -1, D_MODEL), _cols_to_chips(_w_uq_from_pad(g["w_q"])),
             _cols_to_chips(_w_ukv_from_pad(g["w_kv"])), g["w_out"].reshape(N_CHIPS, D_MODEL // N_CHIPS, D_MODEL)], "mixer")

    loss_part, grad_x, g = _local_step(x, positions, loss_target, wts, mixer_weights, ffn_weights, on_ffn_grads,
                                       on_mixer_grads)

    g_small_parts = dict(g)
    g_small_parts["conv_w"] = _conv_w_join(g["conv_w"])
    g_small_parts["conv_b"] = g["conv_b"].reshape(1, 2 * D_FF)
    flat = jnp.concatenate([g_small_parts[n].reshape(-1) for n, _ in _SMALL] + [loss_part.reshape(1)])
    flat = jnp.pad(flat, (0, _SMALL_ROWS * 128 - flat.shape[0])).reshape(_SMALL_ROWS, 128)
    small_gather = _exchange_start([flat], "small_gather_start", "all", after=grad_x)

    mixer_sums, mixer_landed = _exchange_wait(scatters["mixer"], "mixer_scatter_wait", "scatter", after=small_gather[4])
    ffn_sums, ffn_landed = _exchange_wait(scatters["ffn"], "ffn_scatter_wait", "scatter", after=mixer_landed[0])
    reduced = _chip_sum(list(mixer_sums) + list(ffn_sums), list(mixer_landed) + list(ffn_landed))
    g_big = dict(zip(_BIG, _join_halves(reduced)))

    grads, deltas, new_m, new_v = {}, {}, {}, {}

    def update(n, grad):
        w = weights[n]
        shape2 = grad.shape
        d, nm, nv = _adamw(w.reshape(shape2), grad, m_in[n].reshape(shape2), v_in[n].reshape(shape2), "adamw_" + n)
        grads[n], deltas[n], new_m[n], new_v[n] = (t.reshape(w.shape) for t in (grad, d, nm, nv))

    def update_transposed(n, grad_t):
        t = lambda a: jnp.swapaxes(a, 1, 2)
        d, nm, nv = _adamw(t(weights[n]), grad_t, t(m_in[n]), t(v_in[n]), "adamw_" + n)
        grads[n], deltas[n], new_m[n], new_v[n] = t(grad_t), t(d), t(nm), t(nv)

    for n in _BIG:
        g3 = g_big[n].reshape((1, -1, g_big[n].shape[-1]))
        if n == "w_in":
            update_transposed(n, g3)
        else:
            update(n, g3)

    (own,), (everyone,) = _exchange_wait(small_gather, "small_gather_wait", "all", after=deltas["w_up"])
    device = 2 * chip + lax.axis_index("c")
    everyone = lax.dynamic_update_slice(everyone, own[None], (device, 0, 0))
    total = _sum_slabs([everyone[j] for j in range(8)], "small_grads_sum", tr=_SMALL_ROWS).reshape(-1)
    o = 0
    for n, shp in _SMALL:
        piece = total[o:o + math.prod(shp)].reshape(shp)
        o += math.prod(shp)
        if n == "conv_w":
            piece = lax.dynamic_slice_in_dim(piece, chip * UP_SHARD, UP_SHARD, axis=1)
        update(n, piece)
    loss = total[_SMALL_SIZE]
    return (loss, grad_x, *[grads[n] for n in names], *[deltas[n] for n in names], *[new_m[n] for n in names],
            *[new_v[n] for n in names])
```

```python
import functools
import math

import jax
import jax.numpy as jnp
from jax import lax
from jax.experimental import pallas as pl
from jax.experimental.pallas import tpu as pltpu

F32 = jnp.float32
MXU_DTYPE = jnp.bfloat16
MESH = pl.DeviceIdType.MESH

D_MODEL = 1024
EPS = 1e-6
A_GROUPS = 8
CHUNK = 128
HEADS = 8
NOPE = 128
ROPE = 64
QK_DIM = NOPE + ROPE
HEAD_PAD = 256
Q_RANK = 256
KV_RANK = 128
ROPE_THETA = 10000.0
D_FF = 2816
FF_TILE = 256
N_FF_TILES = D_FF // FF_TILE
LAT = 512
IN_PAD = 4 * D_MODEL + LAT
N_CHIPS = 4
ADAM_LR, ADAM_B1, ADAM_B2, ADAM_EPS, ADAM_WD, ADAM_STEP = 0.001, 0.9, 0.999, 1e-08, 0.01, 10

VMEM_CAP_V7X = 64 * 1024 * 1024
NEG = -1e30


def _params(sem, nbytes):
    limit = int(min(VMEM_CAP_V7X - (8 << 20), max(32 << 20, 3 * nbytes)))
    return pltpu.CompilerParams(dimension_semantics=sem, vmem_limit_bytes=limit)


def _nbytes(shape, dtype):
    return math.prod(shape) * jnp.dtype(dtype).itemsize


_DIMS = {"nn": (((1,), (0,)), ((), ())), "nt": (((1,), (1,)), ((), ())), "tn": (((0,), (0,)), ((), ()))}


def _mm(a, b, mode, name, *, tm, tn, tk, out_dtype=F32, add=None, dims=None, a_spec=None, b_spec=None,
        o_spec=None, out_shape=None):
    if dims is None:
        if mode == "nn":
            (M, K), (_, N) = a.shape, b.shape
        elif mode == "nt":
            (M, K), (N, _) = a.shape, b.shape
        else:
            (K, M), (_, N) = a.shape, b.shape
    else:
        M, N, K = dims
    a_blk = (tk, tm) if mode == "tn" else (tm, tk)
    b_blk = (tn, tk) if mode == "nt" else (tk, tn)
    if a_spec is None:
        a_spec = pl.BlockSpec(a_blk, (lambda i, j, k: (k, i)) if mode == "tn" else (lambda i, j, k: (i, k)))
    if b_spec is None:
        b_spec = pl.BlockSpec(b_blk, (lambda i, j, k: (j, k)) if mode == "nt" else (lambda i, j, k: (k, j)))
    if o_spec is None:
        o_spec = pl.BlockSpec((tm, tn), lambda i, j, k: (i, j))
    if out_shape is None:
        out_shape = (M, N)
    assert M % tm == 0 and N % tn == 0 and K % tk == 0, (name, M, N, K, tm, tn, tk)
    nk = K // tk
    contract = _DIMS[mode]
    has_add = add is not None

    def body(*refs):
        a_ref, b_ref = refs[0], refs[1]
        add_ref = refs[2] if has_add else None
        o_ref = refs[3] if has_add else refs[2]

        def product():
            return lax.dot_general(a_ref[...].astype(MXU_DTYPE), b_ref[...].astype(MXU_DTYPE), contract,
                                   preferred_element_type=F32)

        if nk == 1:
            o_ref[...] = (product() + add_ref[...] if has_add else product()).astype(out_dtype)
            return
        acc = refs[-1]
        k = pl.program_id(2)

        @pl.when(k == 0)
        def _():
            acc[...] = jnp.zeros_like(acc)

        acc[...] += product()

        @pl.when(k == nk - 1)
        def _():
            r = acc[...]
            if has_add:
                r = r + add_ref[...]
            o_ref[...] = r.astype(out_dtype)

    in_specs = [a_spec, b_spec]
    args = [a, b]
    nbytes = _nbytes(a_blk, a.dtype) + _nbytes(b_blk, b.dtype) + 3 * _nbytes((tm, tn), F32)
    if has_add:
        in_specs.append(pl.BlockSpec((tm, tn), lambda i, j, k: (i, j)))
        args.append(add)
        nbytes += _nbytes((tm, tn), F32)
    return pl.pallas_call(
        body, name=name, out_shape=jax.ShapeDtypeStruct(out_shape, out_dtype),
        grid=(M // tm, N // tn, nk), in_specs=in_specs, out_specs=o_spec,
        scratch_shapes=[pltpu.VMEM((tm, tn), F32)] if nk > 1 else [],
        compiler_params=_params(("parallel", "parallel", "arbitrary"), nbytes),
    )(*args)


_GELU_C = math.sqrt(2.0 / math.pi)
_GELU_A = 0.044715


def _sigmoid(x):
    return 0.5 * jnp.tanh(0.5 * x) + 0.5


def _gelu(x):
    t = jnp.tanh(x * (_GELU_C + (_GELU_C * _GELU_A) * (x * x)))
    return x * (0.5 + 0.5 * t)


def _gelu_and_grad(x):
    x2 = x * x
    t = jnp.tanh(x * (_GELU_C + (_GELU_C * _GELU_A) * x2))
    cdf = 0.5 + 0.5 * t
    grad = cdf + (0.5 * x) * (1.0 - t * t) * (_GELU_C + (3.0 * _GELU_C * _GELU_A) * x2)
    return x * cdf, grad


def _rope_mix(g, cos_a, sin_a):
    return g * cos_a + pltpu.roll(g, 64, 1) * sin_a


def _rope_mix_bwd(d, cos_a, sin_a):
    return d * cos_a + pltpu.roll(d * sin_a, 64, 1)


def _rms_fwd(x, g, name, tr=512):
    T, D = x.shape

    def body(x_ref, g_ref, h_ref):
        xv = x_ref[...]
        r = lax.rsqrt(jnp.mean(xv * xv, axis=-1, keepdims=True) + EPS)
        h_ref[...] = ((xv * r) * g_ref[...]).astype(h_ref.dtype)

    return pl.pallas_call(
        body, name=name, out_shape=jax.ShapeDtypeStruct((T, D), MXU_DTYPE), grid=(T // tr,),
        in_specs=[pl.BlockSpec((tr, D), lambda i: (i, 0)), pl.BlockSpec((1, D), lambda i: (0, 0))],
        out_specs=pl.BlockSpec((tr, D), lambda i: (i, 0)),
        compiler_params=_params(("parallel",), 3 * _nbytes((tr, D), F32)),
    )(x, g)


def _rms_bwd(x, g, dh, dres, name, tr=512):
    T, D = x.shape

    def body(x_ref, g_ref, dh_ref, dres_ref, dx_ref, gg_ref):
        @pl.when(pl.program_id(0) == 0)
        def _():
            gg_ref[...] = jnp.zeros_like(gg_ref)

        xv = x_ref[...]
        r = lax.rsqrt(jnp.mean(xv * xv, axis=-1, keepdims=True) + EPS)
        xn = xv * r
        dhv = dh_ref[...]
        dxn = dhv * g_ref[...]
        dx_ref[...] = dres_ref[...] + r * (dxn - xn * jnp.mean(dxn * xn, axis=-1, keepdims=True))
        gg_ref[...] += jnp.sum(dhv * xn, axis=0, keepdims=True)

    row = pl.BlockSpec((tr, D), lambda i: (i, 0))
    vec = pl.BlockSpec((1, D), lambda i: (0, 0))
    return pl.pallas_call(
        body, name=name,
        out_shape=(jax.ShapeDtypeStruct((T, D), F32), jax.ShapeDtypeStruct((1, D), F32)),
        grid=(T // tr,), in_specs=[row, vec, row, row], out_specs=(row, vec),
        compiler_params=_params(("arbitrary",), 6 * _nbytes((tr, D), F32)),
    )(x, g, dh, dres)


def _lat_fwd(z, gq, gkv, wq, wkv, cos_a, sin_a, tr=256):
    T = z.shape[0]
    lat_blk = (4 * D_MODEL) // LAT

    def body(z_ref, gq_ref, gkv_ref, wq_ref, wkv_ref, cos_ref, sin_ref, q_ref, k_ref, v_ref, cqn_ref, ckvn_ref):
        zl = z_ref[...]
        cos_v, sin_v = cos_ref[...], sin_ref[...]
        cq = zl[:, :Q_RANK]
        ckv = zl[:, Q_RANK:Q_RANK + KV_RANK]
        krb = zl[:, Q_RANK + KV_RANK:]
        cqn = ((cq * lax.rsqrt(jnp.mean(cq * cq, axis=-1, keepdims=True) + EPS)) * gq_ref[...]).astype(MXU_DTYPE)
        ckvn = ((ckv * lax.rsqrt(jnp.mean(ckv * ckv, axis=-1, keepdims=True) + EPS)) * gkv_ref[...]).astype(MXU_DTYPE)
        cqn_ref[...] = cqn
        ckvn_ref[...] = ckvn
        krr = _rope_mix(krb, cos_v, sin_v).astype(MXU_DTYPE)
        q = jnp.dot(cqn, wq_ref[...], preferred_element_type=F32)
        kv = jnp.dot(ckvn, wkv_ref[...], preferred_element_type=F32)
        for h in range(HEADS):
            o = h * HEAD_PAD
            q_ref[:, o:o + NOPE] = q[:, o:o + NOPE].astype(MXU_DTYPE)
            q_ref[:, o + NOPE:o + HEAD_PAD] = _rope_mix(q[:, o + NOPE:o + HEAD_PAD], cos_v, sin_v).astype(MXU_DTYPE)
            k_ref[:, o:o + NOPE] = kv[:, h * NOPE:(h + 1) * NOPE].astype(MXU_DTYPE)
            k_ref[:, o + NOPE:o + HEAD_PAD] = krr
        v_ref[...] = kv[:, HEADS * NOPE:].astype(MXU_DTYPE)

    def row(w):
        return pl.BlockSpec((tr, w), lambda i: (i, 0))

    def full(a):
        return pl.BlockSpec(a.shape, lambda i: (0, 0))

    return pl.pallas_call(
        body, name="lat_fwd",
        out_shape=(jax.ShapeDtypeStruct((T, HEADS * HEAD_PAD), MXU_DTYPE), jax.ShapeDtypeStruct((T, HEADS * HEAD_PAD), MXU_DTYPE),
                   jax.ShapeDtypeStruct((T, HEADS * NOPE), MXU_DTYPE), jax.ShapeDtypeStruct((T, Q_RANK), MXU_DTYPE),
                   jax.ShapeDtypeStruct((T, KV_RANK), MXU_DTYPE)),
        grid=(T // tr,),
        in_specs=[pl.BlockSpec((tr, LAT), lambda i: (i, lat_blk)), full(gq), full(gkv), full(wq), full(wkv), row(128), row(128)],
        out_specs=(row(HEADS * HEAD_PAD), row(HEADS * HEAD_PAD), row(HEADS * NOPE), row(Q_RANK), row(KV_RANK)),
        compiler_params=_params(("parallel",), 8 * _nbytes((tr, HEADS * HEAD_PAD), F32)),
    )(z, gq, gkv, wq, wkv, cos_a, sin_a)


ATT_BLOCK = 256
_SCALE = QK_DIM ** -0.5


def _causal_mask(n):
    return lax.broadcasted_iota(jnp.int32, (n, n), 1) <= lax.broadcasted_iota(jnp.int32, (n, n), 0)


def _causal_mask_t(n):
    return lax.broadcasted_iota(jnp.int32, (n, n), 0) <= lax.broadcasted_iota(jnp.int32, (n, n), 1)


ATT_HEADS = 4


def _attn_fwd(q, k, v, B, S):
    tq = ATT_BLOCK
    nq = S // tq
    T = B * S
    hp, groups = ATT_HEADS, HEADS // ATT_HEADS

    def body(q_ref, k_ref, v_ref, o_ref, *lse_refs):
        qi = pl.program_id(2)
        qs = [q_ref[:, t * HEAD_PAD:(t + 1) * HEAD_PAD] for t in range(hp)]

        def scores(j, t):
            rows = pl.ds(pl.multiple_of(j * tq, tq), tq)
            return lax.dot_general(k_ref[rows, t * HEAD_PAD:(t + 1) * HEAD_PAD], qs[t], _DIMS["nt"],
                                   preferred_element_type=F32)

        def step(j, carry, last):
            rows = pl.ds(pl.multiple_of(j * tq, tq), tq)
            out = []
            for t in range(hp):
                m, l, acc, st = carry[t]
                st_next = st if last else scores(j + 1, t)
                st = st * _SCALE
                if last:
                    st = jnp.where(_causal_mask_t(tq), st, NEG)
                m_new = jnp.maximum(m, jnp.max(st, axis=0, keepdims=True))
                alpha = jnp.exp(m - m_new)
                p = jnp.exp(st - m_new)
                l = alpha * l + jnp.sum(p, axis=0, keepdims=True)
                acc = alpha * acc + lax.dot_general(v_ref[rows, t * NOPE:(t + 1) * NOPE], p.astype(MXU_DTYPE),
                                                    _DIMS["tn"], preferred_element_type=F32)
                out.append((m_new, l, acc, st_next))
            return tuple(out)

        init = tuple((jnp.full((1, tq), NEG, F32), jnp.zeros((1, tq), F32), jnp.zeros((NOPE, tq), F32), scores(0, t))
                     for t in range(hp))
        carry = lax.fori_loop(0, qi, lambda j, c: step(j, c, False), init)
        carry = step(qi, carry, True)
        for t in range(hp):
            m, l, acc, _ = carry[t]
            o_ref[:, t * NOPE:(t + 1) * NOPE] = (acc / l).T
            lse_refs[t][0] = m + jnp.log(l)

    lse_sds = jax.ShapeDtypeStruct((groups * B * nq, 1, tq), F32)
    lse_spec = pl.BlockSpec((1, 1, tq), lambda b, h, i: ((h * B + b) * nq + i, 0, 0))
    return pl.pallas_call(
        body, name="attn_fwd",
        out_shape=(jax.ShapeDtypeStruct((T, HEADS * NOPE), F32),) + (lse_sds,) * hp,
        grid=(B, groups, nq),
        in_specs=[pl.BlockSpec((tq, hp * HEAD_PAD), lambda b, h, i: (b * nq + i, h)),
                  pl.BlockSpec((S, hp * HEAD_PAD), lambda b, h, i: (b, h)),
                  pl.BlockSpec((S, hp * NOPE), lambda b, h, i: (b, h))],
        out_specs=(pl.BlockSpec((tq, hp * NOPE), lambda b, h, i: (b * nq + i, h)),) + (lse_spec,) * hp,
        compiler_params=_params(("parallel", "parallel", "arbitrary"), 4 * hp * _nbytes((S, HEAD_PAD), MXU_DTYPE)),
    )(q, k, v)


def _attn_bwd(q, k, v, do, lses, delta, B, S):
    tq = ATT_BLOCK
    nq = S // tq
    T = B * S
    hp, groups = ATT_HEADS, HEADS // ATT_HEADS

    def body(q_ref, k_ref, v_ref, do_ref, *refs):
        lse_refs, dl_refs = refs[:hp], refs[hp:2 * hp]
        dq_out, dk_ref, dv_ref, dq_ref = refs[2 * hp:]
        kj = pl.program_id(2)

        @pl.when(kj == 0)
        def _():
            dq_ref[...] = jnp.zeros_like(dq_ref)

        def products(i, t):
            rows = pl.ds(pl.multiple_of(i * tq, tq), tq)
            st = lax.dot_general(k_ref[:, t * HEAD_PAD:(t + 1) * HEAD_PAD], q_ref[rows, t * HEAD_PAD:(t + 1) * HEAD_PAD],
                                 _DIMS["nt"], preferred_element_type=F32)
            dpt = lax.dot_general(v_ref[:, t * NOPE:(t + 1) * NOPE], do_ref[rows, t * NOPE:(t + 1) * NOPE],
                                  _DIMS["nt"], preferred_element_type=F32)
            return st, dpt

        def step(i, carry, masked):
            rows = pl.ds(pl.multiple_of(i * tq, tq), tq)
            nxt = jnp.minimum(i + 1, nq - 1)
            out = []
            for t in range(hp):
                dk, dv, st, dpt = carry[t]
                st_next, dpt_next = products(nxt, t)
                qk_cols = slice(t * HEAD_PAD, (t + 1) * HEAD_PAD)
                v_cols = slice(t * NOPE, (t + 1) * NOPE)
                p = jnp.exp(st * _SCALE - lse_refs[t][i])
                if masked:
                    p = jnp.where(_causal_mask_t(tq), p, 0.0)
                dv = dv + jnp.dot(p.astype(MXU_DTYPE), do_ref[rows, v_cols], preferred_element_type=F32)
                ds = (p * (dpt - dl_refs[t][i]) * _SCALE).astype(MXU_DTYPE)
                dk = dk + jnp.dot(ds, q_ref[rows, qk_cols], preferred_element_type=F32)
                dq_ref[rows, qk_cols] += lax.dot_general(ds, k_ref[:, qk_cols], _DIMS["tn"], preferred_element_type=F32)
                out.append((dk, dv, st_next, dpt_next))
            return tuple(out)

        init = tuple((jnp.zeros((tq, HEAD_PAD), F32), jnp.zeros((tq, NOPE), F32)) + products(kj, t) for t in range(hp))
        carry = step(kj, init, True)
        carry = lax.fori_loop(kj + 1, nq, lambda i, c: step(i, c, False), carry)
        for t in range(hp):
            dk_ref[:, t * HEAD_PAD:(t + 1) * HEAD_PAD] = carry[t][0].astype(dk_ref.dtype)
            dv_ref[:, t * NOPE:(t + 1) * NOPE] = carry[t][1].astype(dv_ref.dtype)

        @pl.when(kj == nq - 1)
        def _():
            dq_out[...] = dq_ref[...].astype(dq_out.dtype)

    seq = lambda w: pl.BlockSpec((S, w), lambda b, h, j: (b, h))
    blk = lambda w: pl.BlockSpec((tq, w), lambda b, h, j: (b * nq + j, h))
    lse_spec = pl.BlockSpec((nq, 1, tq), lambda b, h, j: (h * B + b, 0, 0))
    dl_specs = [pl.BlockSpec((nq, 1, tq), lambda b, h, j, t=t: ((h * hp + t) * B + b, 0, 0)) for t in range(hp)]
    return pl.pallas_call(
        body, name="attn_bwd",
        out_shape=(jax.ShapeDtypeStruct((T, HEADS * HEAD_PAD), MXU_DTYPE), jax.ShapeDtypeStruct((T, HEADS * HEAD_PAD), MXU_DTYPE),
                   jax.ShapeDtypeStruct((T, HEADS * NOPE), MXU_DTYPE)),
        grid=(B, groups, nq),
        in_specs=[seq(hp * HEAD_PAD), blk(hp * HEAD_PAD), blk(hp * NOPE), seq(hp * NOPE)] + [lse_spec] * hp + dl_specs,
        out_specs=(seq(hp * HEAD_PAD), blk(hp * HEAD_PAD), blk(hp * NOPE)),
        scratch_shapes=[pltpu.VMEM((S, hp * HEAD_PAD), F32)],
        compiler_params=_params(("parallel", "parallel", "arbitrary"), 8 * hp * _nbytes((S, HEAD_PAD), F32)),
    )(q, k, v, do, *lses, *([delta] * hp))


MIX_ROWS = 256


def _tril_weights(ws_ref, g):
    return jnp.where(_causal_mask(CHUNK), ws_ref[g], 0.0).astype(MXU_DTYPE)


def _layer_norm_stats(va):
    mu = jnp.mean(va, axis=-1, keepdims=True)
    xc = va - mu
    rs = lax.rsqrt(jnp.mean(xc * xc, axis=-1, keepdims=True) + EPS)
    return xc * rs


def _mix_specs(tr):
    zcol = lambda c: pl.BlockSpec((tr, D_MODEL), lambda i, c=c: (i, c))
    row = pl.BlockSpec((tr, D_MODEL), lambda i: (i, 0))
    vec = pl.BlockSpec((1, D_MODEL), lambda i: (0, 0))
    ws = pl.BlockSpec((A_GROUPS, CHUNK, CHUNK), lambda i: (0, 0, 0))
    bs = pl.BlockSpec((CHUNK, 128), lambda i: (0, 0))
    return zcol, row, vec, ws, bs


def _mix_fwd(z, yb, ln_g, ln_b, ws, bs_t):
    T = z.shape[0]
    tr = MIX_ROWS
    zcol, row, vec, ws_spec, bs_spec = _mix_specs(tr)

    def body(zu_ref, zv_ref, zga_ref, zgb_ref, yb_ref, g_ref, b_ref, ws_ref, bs_ref, out_ref, vn_s):
        vhat = _layer_norm_stats(_gelu(zv_ref[...]))
        vn_s[...] = (vhat * g_ref[...] + b_ref[...]).astype(MXU_DTYPE)
        for g in range(A_GROUPS):
            w = _tril_weights(ws_ref, g)
            bias = bs_ref[:, g:g + 1]
            cols = slice(g * CHUNK, (g + 1) * CHUNK)
            for c in range(tr // CHUNK):
                rows = slice(c * CHUNK, (c + 1) * CHUNK)
                mixed = jnp.dot(w, vn_s[rows, cols], preferred_element_type=F32) + bias
                ya = _gelu(zu_ref[rows, cols]) * mixed
                merged = _sigmoid(zga_ref[rows, cols]) * ya + _sigmoid(zgb_ref[rows, cols]) * yb_ref[rows, cols]
                out_ref[rows, cols] = merged.astype(MXU_DTYPE)

    return pl.pallas_call(
        body, name="mix_fwd", out_shape=jax.ShapeDtypeStruct((T, D_MODEL), MXU_DTYPE), grid=(T // tr,),
        in_specs=[zcol(0), zcol(1), zcol(2), zcol(3), row, vec, vec, ws_spec, bs_spec], out_specs=row,
        scratch_shapes=[pltpu.VMEM((tr, D_MODEL), MXU_DTYPE)],
        compiler_params=_params(("parallel",), 8 * _nbytes((tr, D_MODEL), F32)),
    )(z, z, z, z, yb, ln_g, ln_b, ws, bs_t)


def _mix_bwd(z, yb, dm, ln_g, ln_b, ws, bs_t):
    T = z.shape[0]
    tr = MIX_ROWS
    zcol, row, vec, ws_spec, bs_spec = _mix_specs(tr)

    def body(zu_ref, zv_ref, zga_ref, zgb_ref, yb_ref, dm_ref, g_ref, b_ref, ws_ref, bs_ref,
             dz_ref, dyb_ref, dl_ref, gws_ref, gbs_ref, glg_ref, glb_ref, vn_s, dvn_s):
        @pl.when(pl.program_id(0) == 0)
        def _():
            gws_ref[...] = jnp.zeros_like(gws_ref)
            gbs_ref[...] = jnp.zeros_like(gbs_ref)
            glg_ref[...] = jnp.zeros_like(glg_ref)
            glb_ref[...] = jnp.zeros_like(glb_ref)

        lane = lax.broadcasted_iota(jnp.int32, (CHUNK, 128), 1)
        va, dgelu_v = _gelu_and_grad(zv_ref[...])
        mu = jnp.mean(va, axis=-1, keepdims=True)
        xc = va - mu
        rs = lax.rsqrt(jnp.mean(xc * xc, axis=-1, keepdims=True) + EPS)
        vhat = xc * rs
        vn_s[...] = (vhat * g_ref[...] + b_ref[...]).astype(MXU_DTYPE)
        gbs_acc = jnp.zeros((CHUNK, 128), F32)
        for g in range(A_GROUPS):
            w = _tril_weights(ws_ref, g)
            bias = bs_ref[:, g:g + 1]
            cols = slice(g * CHUNK, (g + 1) * CHUNK)
            gw_acc = jnp.zeros((CHUNK, CHUNK), F32)
            for c in range(tr // CHUNK):
                rows = slice(c * CHUNK, (c + 1) * CHUNK)
                vn = vn_s[rows, cols]
                mixed = jnp.dot(w, vn, preferred_element_type=F32) + bias
                ua, dgelu_u = _gelu_and_grad(zu_ref[rows, cols])
                dmv = dm_ref[rows, cols]
                sa = _sigmoid(zga_ref[rows, cols])
                dya = dmv * sa
                dz_ref[rows, 2 * D_MODEL + g * CHUNK:2 * D_MODEL + (g + 1) * CHUNK] = (
                    dmv * (ua * mixed) * (sa * (1.0 - sa))).astype(dz_ref.dtype)
                dz_ref[rows, cols] = (dya * mixed * dgelu_u).astype(dz_ref.dtype)
                dmix = dya * ua
                gbs_acc = gbs_acc + jnp.where(lane == g, jnp.sum(dmix, axis=-1, keepdims=True), 0.0)
                dmix_b = dmix.astype(MXU_DTYPE)
                gw_acc = gw_acc + lax.dot_general(dmix_b, vn, _DIMS["nt"], preferred_element_type=F32)
                dvn_s[rows, cols] = lax.dot_general(w, dmix_b, _DIMS["tn"], preferred_element_type=F32)
            gws_ref[g] += jnp.where(_causal_mask(CHUNK), gw_acc, 0.0)
        gbs_ref[...] += gbs_acc

        dvn = dvn_s[...]
        glg_ref[...] += jnp.sum(dvn * vhat, axis=0, keepdims=True)
        glb_ref[...] += jnp.sum(dvn, axis=0, keepdims=True)
        dvh = dvn * g_ref[...]
        dva = rs * (dvh - jnp.mean(dvh, axis=-1, keepdims=True) - vhat * jnp.mean(dvh * vhat, axis=-1, keepdims=True))
        dz_ref[:, D_MODEL:2 * D_MODEL] = (dva * dgelu_v).astype(dz_ref.dtype)

        dmv = dm_ref[...]
        ybv = yb_ref[...]
        sb = _sigmoid(zgb_ref[...])
        dyb = dmv * sb
        dyb_ref[...] = dyb.astype(dyb_ref.dtype)
        dz_ref[:, 3 * D_MODEL:4 * D_MODEL] = (dmv * ybv * (sb * (1.0 - sb))).astype(dz_ref.dtype)
        dz_ref[:, 4 * D_MODEL:] = jnp.zeros((tr, LAT), dz_ref.dtype)
        prod = dyb * ybv
        sel = (lax.broadcasted_iota(jnp.int32, (HEADS, D_MODEL), 1) // NOPE
               == lax.broadcasted_iota(jnp.int32, (HEADS, D_MODEL), 0)).astype(jnp.bfloat16)
        hi = prod.astype(jnp.bfloat16)
        rest = prod - hi.astype(F32)
        mid = rest.astype(jnp.bfloat16)
        lo = (rest - mid.astype(F32)).astype(jnp.bfloat16)
        dl_ref[...] = (lax.dot_general(sel, hi, _DIMS["nt"], preferred_element_type=F32)
                       + lax.dot_general(sel, mid, _DIMS["nt"], preferred_element_type=F32)
                       + lax.dot_general(sel, lo, _DIMS["nt"], preferred_element_type=F32))

    return pl.pallas_call(
        body, name="mix_bwd",
        out_shape=(jax.ShapeDtypeStruct((T, IN_PAD), MXU_DTYPE), jax.ShapeDtypeStruct((T, D_MODEL), MXU_DTYPE),
                   jax.ShapeDtypeStruct((HEADS, T), F32), jax.ShapeDtypeStruct((A_GROUPS, CHUNK, CHUNK), F32),
                   jax.ShapeDtypeStruct((CHUNK, 128), F32), jax.ShapeDtypeStruct((1, D_MODEL), F32),
                   jax.ShapeDtypeStruct((1, D_MODEL), F32)),
        grid=(T // tr,),
        in_specs=[zcol(0), zcol(1), zcol(2), zcol(3), row, row, vec, vec, ws_spec, bs_spec],
        out_specs=(pl.BlockSpec((tr, IN_PAD), lambda i: (i, 0)), row, pl.BlockSpec((HEADS, tr), lambda i: (0, i)),
                   ws_spec, bs_spec, vec, vec),
        scratch_shapes=[pltpu.VMEM((tr, D_MODEL), MXU_DTYPE), pltpu.VMEM((tr, D_MODEL), F32)],
        compiler_params=_params(("arbitrary",), 12 * _nbytes((tr, D_MODEL), F32)),
    )(z, z, z, z, yb, dm, ln_g, ln_b, ws, bs_t)


def _lat_bwd(dz, z, dq, dk, dv, gq, gkv, wq, wkv, cos_a, sin_a, tr=256):
    T = z.shape[0]
    lat_blk = (4 * D_MODEL) // LAT

    def body(dz_in, z_ref, dq_ref, dk_ref, dv_ref, gq_ref, gkv_ref, wq_ref, wkv_ref, cos_ref, sin_ref,
             dz_ref, dqr_ref, dkv_ref, ggq_ref, ggkv_ref):
        del dz_in

        @pl.when(pl.program_id(0) == 0)
        def _():
            ggq_ref[...] = jnp.zeros_like(ggq_ref)
            ggkv_ref[...] = jnp.zeros_like(ggkv_ref)

        cos_v, sin_v = cos_ref[...], sin_ref[...]
        dkr = jnp.zeros((tr, 128), F32)
        for h in range(HEADS):
            o = h * HEAD_PAD
            dqr_ref[:, o:o + NOPE] = dq_ref[:, o:o + NOPE].astype(MXU_DTYPE)
            dqr_ref[:, o + NOPE:o + HEAD_PAD] = _rope_mix_bwd(dq_ref[:, o + NOPE:o + HEAD_PAD], cos_v, sin_v).astype(MXU_DTYPE)
            dkv_ref[:, h * NOPE:(h + 1) * NOPE] = dk_ref[:, o:o + NOPE].astype(MXU_DTYPE)
            dkr = dkr + _rope_mix_bwd(dk_ref[:, o + NOPE:o + HEAD_PAD], cos_v, sin_v)
        dkv_ref[:, HEADS * NOPE:] = dv_ref[...]
        dcqn = lax.dot_general(dqr_ref[...], wq_ref[...], _DIMS["nt"], preferred_element_type=F32)
        dckvn = lax.dot_general(dkv_ref[...], wkv_ref[...], _DIMS["nt"], preferred_element_type=F32)

        zl = z_ref[...]

        def rms_bwd(c, dn, g_ref, gg_ref):
            r = lax.rsqrt(jnp.mean(c * c, axis=-1, keepdims=True) + EPS)
            ch = c * r
            gg_ref[...] += jnp.sum(dn * ch, axis=0, keepdims=True)
            dch = dn * g_ref[...]
            return r * (dch - ch * jnp.mean(dch * ch, axis=-1, keepdims=True))

        dz_ref[:, :Q_RANK] = rms_bwd(zl[:, :Q_RANK], dcqn, gq_ref, ggq_ref).astype(dz_ref.dtype)
        dz_ref[:, Q_RANK:Q_RANK + KV_RANK] = rms_bwd(zl[:, Q_RANK:Q_RANK + KV_RANK], dckvn, gkv_ref, ggkv_ref).astype(dz_ref.dtype)
        dz_ref[:, Q_RANK + KV_RANK:] = dkr.astype(dz_ref.dtype)

    def row(w):
        return pl.BlockSpec((tr, w), lambda i: (i, 0))

    def full(a):
        return pl.BlockSpec(a.shape, lambda i: (0, 0))

    lat = pl.BlockSpec((tr, LAT), lambda i: (i, lat_blk))
    return pl.pallas_call(
        body, name="lat_bwd",
        out_shape=(jax.ShapeDtypeStruct(dz.shape, dz.dtype), jax.ShapeDtypeStruct((T, HEADS * HEAD_PAD), MXU_DTYPE),
                   jax.ShapeDtypeStruct((T, 2 * HEADS * NOPE), MXU_DTYPE), jax.ShapeDtypeStruct(gq.shape, F32),
                   jax.ShapeDtypeStruct(gkv.shape, F32)),
        grid=(T // tr,),
        in_specs=[pl.BlockSpec(memory_space=pl.ANY), lat, row(HEADS * HEAD_PAD), row(HEADS * HEAD_PAD), row(HEADS * NOPE),
                  full(gq), full(gkv), full(wq), full(wkv), row(128), row(128)],
        out_specs=(lat, row(HEADS * HEAD_PAD), row(2 * HEADS * NOPE), full(gq), full(gkv)),
        input_output_aliases={0: 0},
        compiler_params=_params(("arbitrary",), 8 * _nbytes((tr, HEADS * HEAD_PAD), F32)),
    )(dz, z, dq, dk, dv, gq, gkv, wq, wkv, cos_a, sin_a)


GATE_ROWS = 64
HALO = 8


def _taps(ref, half, r, first):
    C = GATE_ROWS
    if first:
        xs = jnp.concatenate([jnp.zeros((HALO, ref.shape[-1]), F32), ref[half, 0:C, :]], axis=0)
    else:
        xs = ref[half, pl.ds(pl.multiple_of(r * C - HALO, HALO), C + HALO), :]
    return xs[HALO:, :], pltpu.roll(xs, 1, 0)[HALO:, :], pltpu.roll(xs, 2, 0)[HALO:, :]


def _conv_taps(taps, cw, cb):
    x0, x1, x2 = taps
    return cb + cw[0:1, :] * x2 + cw[1:2, :] * x1 + cw[2:3, :] * x0


def _fold8(x):
    acc = x[0:8, :]
    for i in range(1, x.shape[0] // 8):
        acc = acc + x[8 * i:8 * (i + 1), :]
    return acc


def _gate_fwd(up3, conv_w, conv_b, B, S):
    T = B * S
    W = FF_TILE
    C = GATE_ROWS

    def body(up_ref, cw_ref, cb_ref, act_ref):
        def chunk(r, first):
            gate = _conv_taps(_taps(up_ref, 0, r, first), cw_ref[0], cb_ref[0])
            val = _conv_taps(_taps(up_ref, 1, r, first), cw_ref[1], cb_ref[1])
            base = 0 if first else pl.multiple_of(r * C, C)
            act_ref[pl.ds(base, C), :] = (gate * _sigmoid(gate) * val).astype(act_ref.dtype)

        chunk(0, True)

        @pl.loop(1, S // C)
        def _(r):
            chunk(r, False)

    return pl.pallas_call(
        body, name="gate_fwd", out_shape=jax.ShapeDtypeStruct((T, D_FF), MXU_DTYPE), grid=(B, N_FF_TILES),
        in_specs=[pl.BlockSpec((2, S, W), lambda b, j: (0, b, j)), pl.BlockSpec((2, 3, W), lambda b, j: (0, 0, j)),
                  pl.BlockSpec((2, 1, W), lambda b, j: (0, 0, j))],
        out_specs=pl.BlockSpec((S, W), lambda b, j: (b, j)),
        compiler_params=_params(("parallel", "parallel"), 6 * _nbytes((S, W), F32)),
    )(up3, conv_w, conv_b)


def _gate_bwd(up3, dact, conv_w, conv_b, B, S):
    T = B * S
    W = FF_TILE
    C = GATE_ROWS

    def body(up_ref, da_ref, cw_ref, cb_ref, dup_ref, gcw_ref, gcb_ref, d_s):
        @pl.when(pl.program_id(1) == 0)
        def _():
            gcw_ref[...] = jnp.zeros_like(gcw_ref)
            gcb_ref[...] = jnp.zeros_like(gcb_ref)

        def chunk(r, first, sums):
            rows = pl.ds(0 if first else pl.multiple_of(r * C, C), C)
            taps = [_taps(up_ref, half, r, first) for half in (0, 1)]
            gate = _conv_taps(taps[0], cw_ref[0], cb_ref[0])
            val = _conv_taps(taps[1], cw_ref[1], cb_ref[1])
            sg = _sigmoid(gate)
            da = da_ref[rows, :]
            d_halves = (da * val * (sg * (1.0 + gate * (1.0 - sg))), da * (gate * sg))
            out = []
            for half, dup in enumerate(d_halves):
                d_s[half, rows, :] = dup
                x0, x1, x2 = taps[half]
                sb, s0, s1, s2 = sums[half]
                out.append((sb + _fold8(dup), s0 + _fold8(dup * x2), s1 + _fold8(dup * x1), s2 + _fold8(dup * x0)))
            return tuple(out)

        zeros = tuple(tuple(jnp.zeros((8, W), F32) for _ in range(4)) for _ in range(2))
        sums = chunk(0, True, zeros)
        sums = lax.fori_loop(1, S // C, lambda r, s: chunk(r, False, s), sums)
        for half in (0, 1):
            sb, s0, s1, s2 = sums[half]
            gcb_ref[half] += jnp.sum(sb, axis=0, keepdims=True)
            gcw_ref[half, 0:1, :] += jnp.sum(s0, axis=0, keepdims=True)
            gcw_ref[half, 1:2, :] += jnp.sum(s1, axis=0, keepdims=True)
            gcw_ref[half, 2:3, :] += jnp.sum(s2, axis=0, keepdims=True)

        d_s[:, S:S + HALO, :] = jnp.zeros((2, HALO, W), F32)

        @pl.loop(0, S // C)
        def _(r):
            base = pl.multiple_of(r * C, C)
            for half in (0, 1):
                ds_ = d_s[half, pl.ds(base, C + HALO), :]
                cw = cw_ref[half]
                dx = (cw[2:3, :] * ds_[:C, :] + cw[1:2, :] * pltpu.roll(ds_, C + HALO - 1, 0)[:C, :]
                      + cw[0:1, :] * pltpu.roll(ds_, C + HALO - 2, 0)[:C, :])
                dup_ref[half, pl.ds(base, C), :] = dx.astype(dup_ref.dtype)

    up_spec = pl.BlockSpec((2, S, W), lambda j, b: (0, b, j))
    cw_spec = pl.BlockSpec((2, 3, W), lambda j, b: (0, 0, j))
    cb_spec = pl.BlockSpec((2, 1, W), lambda j, b: (0, 0, j))
    return pl.pallas_call(
        body, name="gate_bwd",
        out_shape=(jax.ShapeDtypeStruct((2, T, D_FF), MXU_DTYPE), jax.ShapeDtypeStruct((2, 3, D_FF), F32),
                   jax.ShapeDtypeStruct((2, 1, D_FF), F32)),
        grid=(N_FF_TILES, B),
        in_specs=[up_spec, pl.BlockSpec((S, W), lambda j, b: (b, j)), cw_spec, cb_spec],
        out_specs=(up_spec, cw_spec, cb_spec),
        scratch_shapes=[pltpu.VMEM((2, S + HALO, W), F32)],
        compiler_params=_params(("parallel", "arbitrary"), 10 * _nbytes((S, W), F32)),
    )(up3, dact, conv_w, conv_b)


def _final(x2, tgt, g, tr=512):
    T, D = x2.shape

    def body(x_ref, t_ref, g_ref, dx_ref, loss_ref, gg_ref):
        @pl.when(pl.program_id(0) == 0)
        def _():
            loss_ref[...] = jnp.zeros_like(loss_ref)
            gg_ref[...] = jnp.zeros_like(gg_ref)

        xv = x_ref[...]
        gv = g_ref[...]
        r = lax.rsqrt(jnp.mean(xv * xv, axis=-1, keepdims=True) + EPS)
        xn = xv * r
        err = xn * gv - t_ref[...]
        loss_ref[...] += 0.5 * jnp.sum(jnp.mean(err * err, axis=-1, keepdims=True), axis=0, keepdims=True)
        dy = err * (1.0 / D)
        gg_ref[...] += jnp.sum(dy * xn, axis=0, keepdims=True)
        dxn = dy * gv
        dx_ref[...] = r * (dxn - xn * jnp.mean(dxn * xn, axis=-1, keepdims=True))

    row = pl.BlockSpec((tr, D), lambda i: (i, 0))
    vec = pl.BlockSpec((1, D), lambda i: (0, 0))
    return pl.pallas_call(
        body, name="final_loss",
        out_shape=(jax.ShapeDtypeStruct((T, D), F32), jax.ShapeDtypeStruct((1, 128), F32), jax.ShapeDtypeStruct((1, D), F32)),
        grid=(T // tr,), in_specs=[row, row, vec],
        out_specs=(row, pl.BlockSpec((1, 128), lambda i: (0, 0)), vec),
        compiler_params=_params(("arbitrary",), 6 * _nbytes((tr, D), F32)),
    )(x2, tgt, g)


def _sum_slabs(parts, name, tr):
    rows, cols = parts[0].shape
    n = len(parts)

    def body(*refs):
        acc = refs[0][...]
        for r in refs[1:n]:
            acc = acc + r[...]
        refs[n][...] = acc

    blk = pl.BlockSpec((tr, cols), lambda i: (i, 0))
    return pl.pallas_call(
        body, name=name, out_shape=jax.ShapeDtypeStruct((rows, cols), F32), grid=(rows // tr,),
        in_specs=[blk] * n, out_specs=blk,
        compiler_params=_params(("parallel",), (n + 1) * _nbytes((tr, cols), F32)),
    )(*parts)


ADAMW_BLOCK_BYTES = 2400 * 1024


def _adamw(w, g, m, v, name):
    lead = w.ndim == 3
    rows, cols = w.shape[-2:]
    fits = [d for d in range(8, rows + 1, 8) if rows % d == 0 and d * cols * 4 <= ADAMW_BLOCK_BYTES]
    tr = max(fits) if fits else rows
    c1 = 1.0 - ADAM_B1 ** ADAM_STEP
    c2 = 1.0 - ADAM_B2 ** ADAM_STEP

    def body(w_ref, g_ref, m_ref, v_ref, d_ref, nm_ref, nv_ref):
        gv = g_ref[...]
        nm = ADAM_B1 * m_ref[...] + (1.0 - ADAM_B1) * gv
        nv = ADAM_B2 * v_ref[...] + (1.0 - ADAM_B2) * (gv * gv)
        nm_ref[...] = nm
        nv_ref[...] = nv
        d_ref[...] = -ADAM_LR * ((nm / c1) / (jnp.sqrt(nv / c2) + ADAM_EPS) + ADAM_WD * w_ref[...])

    blk = pl.BlockSpec((None, tr, cols), lambda i: (0, i, 0)) if lead else pl.BlockSpec((tr, cols), lambda i: (i, 0))
    sds = jax.ShapeDtypeStruct(w.shape, F32)
    return pl.pallas_call(
        body, name=name, out_shape=(sds, sds, sds), grid=(rows // tr,), in_specs=[blk] * 4, out_specs=(blk, blk, blk),
        compiler_params=_params(("parallel",), 7 * _nbytes((tr, cols), F32)),
    )(w, g, m, v)


_ANY = pl.BlockSpec(memory_space=pl.ANY)


def _place():
    x, y, c = lax.axis_index("x"), lax.axis_index("y"), lax.axis_index("c")
    chips = [(1 - x, y), (x, 1 - y), (1 - x, 1 - y)]
    return x, y, c, chips


def _gather_weights(shards):
    n = len(shards)

    def body(*refs):
        ins, outs = refs[:n], refs[n:2 * n]
        send, recv, fsend, frecv, osend, orecv = refs[2 * n:]
        x, y, c, chips = _place()
        me = 2 * x + y
        first, passed = [], []
        for w in range(n):
            first.append(pltpu.make_async_remote_copy(
                src_ref=ins[w], dst_ref=outs[w].at[me], send_sem=osend.at[w], recv_sem=orecv.at[w],
                device_id=(x, y, 1 - c), device_id_type=MESH))
        for w in range(n):
            for j, (px, py) in enumerate(chips):
                first.append(pltpu.make_async_remote_copy(
                    src_ref=ins[w].at[c], dst_ref=outs[w].at[me, c], send_sem=send.at[3 * w + j],
                    recv_sem=recv.at[3 * w + j], device_id=(px, py, c), device_id_type=MESH))
        for cp in first:
            cp.start()
        for w in range(n):
            for j, (px, py) in enumerate(chips):
                landed = outs[w].at[2 * px + py, c]
                pltpu.make_async_remote_copy(src_ref=landed, dst_ref=landed, send_sem=send.at[3 * w + j],
                                             recv_sem=recv.at[3 * w + j], device_id=(px, py, c),
                                             device_id_type=MESH).wait_recv()
                fw = pltpu.make_async_remote_copy(src_ref=landed, dst_ref=landed, send_sem=fsend.at[3 * w + j],
                                                  recv_sem=frecv.at[3 * w + j], device_id=(x, y, 1 - c),
                                                  device_id_type=MESH)
                fw.start()
                passed.append(fw)
        for w in range(n):
            for j, (px, py) in enumerate(chips):
                other = outs[w].at[2 * px + py, 1 - c]
                pltpu.make_async_remote_copy(src_ref=other, dst_ref=other, send_sem=fsend.at[3 * w + j],
                                             recv_sem=frecv.at[3 * w + j], device_id=(x, y, 1 - c),
                                             device_id_type=MESH).wait_recv()
        for w in range(n):
            own = outs[w].at[me]
            pltpu.make_async_remote_copy(src_ref=own, dst_ref=own, send_sem=osend.at[w], recv_sem=orecv.at[w],
                                         device_id=(x, y, 1 - c), device_id_type=MESH).wait_recv()
        for cp in first + passed:
            cp.wait_send()

    dma = lambda k: pltpu.SemaphoreType.DMA((k,))
    return pl.pallas_call(
        body, name="gather_weights",
        out_shape=tuple(jax.ShapeDtypeStruct((N_CHIPS,) + s.shape, s.dtype) for s in shards),
        in_specs=[_ANY] * n, out_specs=tuple([_ANY] * n),
        scratch_shapes=[dma(3 * n), dma(3 * n), dma(3 * n), dma(3 * n), dma(n), dma(n)],
    )(*shards)


_HBM = pl.BlockSpec(memory_space=pltpu.HBM)
_SEM = pl.BlockSpec(memory_space=pltpu.SEMAPHORE)
_EFFECT = pltpu.SideEffectType.DATAFLOW_SIDE_EFFECTING


SEMS_PER_ARRAY = 8


def _exchange_copies(srcs, lands, send, recv, mode):
    x, y, c, chips = _place()
    if mode == "all":
        flips = [(fx, fy, fc) for fx in (0, 1) for fy in (0, 1) for fc in (0, 1)][1:]
        peers = [(x ^ fx, y ^ fy, c ^ fc) for fx, fy, fc in flips]
        slot = 4 * x + 2 * y + c
    else:
        peers = [(px, py, c) for px, py in chips] + ([(x, y, 1 - c)] if mode == "gather" else [])
        slot = 2 * x + y
    cps = []
    for w, (src, land) in enumerate(zip(srcs, lands)):
        for k, peer in enumerate(peers):
            piece = src.at[2 * peer[0] + peer[1]] if mode == "scatter" else src
            cps.append(pltpu.make_async_remote_copy(
                src_ref=piece, dst_ref=land.at[slot], send_sem=send.at[SEMS_PER_ARRAY * w + k],
                recv_sem=recv.at[SEMS_PER_ARRAY * w + k], device_id=peer, device_id_type=MESH))
    return cps


def _exchange_start(srcs, name, mode, after):
    n = len(srcs)
    lead = {"gather": (N_CHIPS,), "scatter": (), "all": (2 * N_CHIPS,)}[mode]
    land_shapes = [lead + s.shape for s in srcs]

    def body(*refs):
        src_refs, land_refs = refs[:n], refs[n:2 * n]
        send, recv = refs[2 * n + 1], refs[2 * n + 2]
        token = refs[-1]
        for cp in _exchange_copies(src_refs, land_refs, send, recv, mode):
            cp.start()
        token[...] = jnp.zeros_like(token)

    sems = pltpu.SemaphoreType.DMA((SEMS_PER_ARRAY * n,))
    out = pl.pallas_call(
        body, name=name,
        out_shape=(sems, sems, *[pltpu.HBM(s.shape, s.dtype) for s in srcs],
                   *[pltpu.HBM(shp, s.dtype) for shp, s in zip(land_shapes, srcs)], jax.ShapeDtypeStruct((8, 128), F32)),
        in_specs=[_HBM] * (2 * n) + [_ANY],
        out_specs=(_SEM, _SEM, *[_HBM] * (2 * n), pl.BlockSpec(memory_space=pltpu.VMEM)),
        input_output_aliases={i: 2 + i for i in range(2 * n)},
        compiler_params=pltpu.CompilerParams(has_side_effects=_EFFECT),
    )(*[pltpu.with_memory_space_constraint(s, pltpu.HBM) for s in srcs],
      *[pltpu.with_memory_space_constraint(lax.empty(shp, s.dtype), pltpu.HBM) for shp, s in zip(land_shapes, srcs)],
      after)
    return out[0], out[1], out[2:2 + n], out[2 + n:2 + 2 * n], out[-1]


def _exchange_wait(started, name, mode, after):
    send, recv, src_thru, land_thru, _ = started
    n = len(src_thru)

    def body(*refs):
        src_refs, land_refs, send_ref, recv_ref = refs[:n], refs[n:2 * n], refs[2 * n], refs[2 * n + 1]
        for cp in _exchange_copies(src_refs, land_refs, send_ref, recv_ref, mode):
            cp.wait_send()
            cp.wait_recv()

    out = pl.pallas_call(
        body, name=name,
        out_shape=tuple(pltpu.HBM(a.shape, a.dtype) for a in list(src_thru) + list(land_thru)),
        in_specs=[_HBM] * (2 * n) + [_SEM, _SEM, _ANY], out_specs=tuple([_HBM] * (2 * n)),
        input_output_aliases={i: i for i in range(2 * n)},
        compiler_params=pltpu.CompilerParams(has_side_effects=_EFFECT),
    )(*src_thru, *land_thru, send, recv, after)
    return out[:n], out[n:]


def _swap_halves(gs, name):
    n = len(gs)

    def body(*refs):
        ins, outs, send, recv = refs[:n], refs[n:2 * n], refs[2 * n], refs[2 * n + 1]
        x, y, c, _ = _place()
        cps = []
        for w in range(n):
            cps.append(pltpu.make_async_remote_copy(
                src_ref=ins[w].at[:, 1 - c], dst_ref=outs[w], send_sem=send.at[w], recv_sem=recv.at[w],
                device_id=(x, y, 1 - c), device_id_type=MESH))
        for cp in cps:
            cp.start()
        for cp in cps:
            cp.wait()

    return pl.pallas_call(
        body, name=name,
        out_shape=tuple(jax.ShapeDtypeStruct((g.shape[0],) + g.shape[2:], g.dtype) for g in gs),
        in_specs=[_ANY] * n, out_specs=tuple([_ANY] * n),
        scratch_shapes=[pltpu.SemaphoreType.DMA((n,)), pltpu.SemaphoreType.DMA((n,))],
    )(*gs)


GRAD_PAYLOAD = jnp.bfloat16


def _half_blocks(half_rows, cols):
    if (half_rows // 2) % 16 == 0:
        return (half_rows // 2, cols), (lambda r: (r, 0))
    assert cols % 256 == 0, (half_rows, cols)
    return (half_rows, cols // 2), (lambda r: (0, r))


def _pair_sum(gs, gots, name):
    n = len(gs)
    core = lax.axis_index("c").astype(jnp.int32).reshape(1)

    def body(core_ref, *refs):
        del core_ref
        for w in range(n):
            refs[2 * n + w][...] = (refs[w][...] + refs[n + w][...]).astype(GRAD_PAYLOAD)

    in_specs, out_specs, out_shape, nbytes = [], [], [], 0
    cuts = [_half_blocks(g.shape[1] // 2, g.shape[2]) for g in gs]
    for g, ((br, bc), at) in zip(gs, cuts):
        per_half = (g.shape[1] // 2) // br
        in_specs.append(pl.BlockSpec((1, br, bc), lambda s, r, core, at=at, per_half=per_half:
                                     (s, per_half * core[0] + at(r)[0], at(r)[1])))
        nbytes += 3 * _nbytes((br, bc), F32)
    for g, ((br, bc), at) in zip(gs, cuts):
        in_specs.append(pl.BlockSpec((1, br, bc), lambda s, r, core, at=at: (s,) + at(r)))
        out_specs.append(pl.BlockSpec((1, br, bc), lambda s, r, core, at=at: (s,) + at(r)))
        out_shape.append(jax.ShapeDtypeStruct((g.shape[0], g.shape[1] // 2, g.shape[2]), GRAD_PAYLOAD))
    return pl.pallas_call(
        body, name=name, out_shape=tuple(out_shape),
        grid_spec=pltpu.PrefetchScalarGridSpec(num_scalar_prefetch=1, grid=(N_CHIPS, 2), in_specs=in_specs,
                                               out_specs=tuple(out_specs)),
        compiler_params=_params(("parallel", "parallel"), nbytes),
    )(core, *gs, *gots)


def _chip_sum(ps, landed):
    n = len(ps)
    x, y, c = lax.axis_index("x"), lax.axis_index("y"), lax.axis_index("c")
    where = jnp.stack([2 * x + y, 2 * (1 - x) + y, 2 * x + (1 - y), 2 * (1 - x) + (1 - y), c]).astype(jnp.int32)

    def body(where_ref, *refs):
        del where_ref
        for w in range(n):
            terms = [refs[4 * w + t][...].astype(F32) for t in range(4)]
            refs[4 * n + w][...] = ((terms[0] + terms[1]) + terms[2]) + terms[3]

    in_specs, out_specs, out_shape, args, nbytes = [], [], [], [], 0
    for p, a in zip(ps, landed):
        (br, bc), at = _half_blocks(a.shape[1], a.shape[2])
        blk = (1, br, bc)
        in_specs.append(pl.BlockSpec(blk, lambda r, where, at=at: (where[0],) + at(r)))
        args.append(p)
        for t in (1, 2, 3):
            in_specs.append(pl.BlockSpec(blk, lambda r, where, t=t, at=at: (where[t],) + at(r)))
            args.append(a)
        out_specs.append(pl.BlockSpec(blk, lambda r, where, at=at: (where[4],) + at(r)))
        out_shape.append(jax.ShapeDtypeStruct((2,) + a.shape[1:], F32))
        nbytes += 4 * _nbytes(blk, F32)
    return pl.pallas_call(
        body, name="grad_chip_sum", out_shape=tuple(out_shape),
        grid_spec=pltpu.PrefetchScalarGridSpec(num_scalar_prefetch=1, grid=(2,), in_specs=in_specs,
                                               out_specs=tuple(out_specs)),
        compiler_params=_params(("parallel",), nbytes),
    )(where, *args)


def _join_halves(ss):
    n = len(ss)

    def body(*refs):
        outs, send, recv = refs[n:2 * n], refs[2 * n], refs[2 * n + 1]
        x, y, c, _ = _place()
        cps = []
        for w in range(n):
            cps.append(pltpu.make_async_remote_copy(
                src_ref=outs[w].at[c], dst_ref=outs[w].at[c], send_sem=send.at[w], recv_sem=recv.at[w],
                device_id=(x, y, 1 - c), device_id_type=MESH))
        for cp in cps:
            cp.start()
        for w in range(n):
            got = outs[w].at[1 - c]
            pltpu.make_async_remote_copy(src_ref=got, dst_ref=got, send_sem=send.at[w], recv_sem=recv.at[w],
                                         device_id=(x, y, 1 - c), device_id_type=MESH).wait_recv()
        for cp in cps:
            cp.wait_send()

    dma = lambda k: pltpu.SemaphoreType.DMA((k,))
    return pl.pallas_call(
        body, name="grad_join_halves",
        out_shape=tuple(jax.ShapeDtypeStruct(s.shape, s.dtype) for s in ss),
        in_specs=[_ANY] * n, out_specs=tuple([_ANY] * n), input_output_aliases={w: w for w in range(n)},
        scratch_shapes=[dma(n), dma(n)],
    )(*ss)


def _rot_cols(w, axis=-1):
    a, b = jnp.split(w, 2, axis=axis)
    return jnp.concatenate([-b, a], axis=axis)


def _rot_cols_t(g, axis=-1):
    a, b = jnp.split(g, 2, axis=axis)
    return jnp.concatenate([b, -a], axis=axis)


def _cols_from_chips(a):
    n, r, cs = a.shape
    return jnp.transpose(a, (1, 0, 2)).reshape(r, n * cs)


def _cols_to_chips(a):
    r, cc = a.shape
    return jnp.transpose(a.reshape(r, N_CHIPS, cc // N_CHIPS), (1, 0, 2))


def _conv_w_split(cw):
    return jnp.swapaxes(cw.reshape(3, 2, D_FF), 0, 1)


def _conv_w_join(g):
    return jnp.swapaxes(g, 0, 1).reshape(3, 2 * D_FF)


_SEG =(D_MODEL, 2 * D_MODEL, 2 * D_MODEL + Q_RANK, 2 * D_MODEL + Q_RANK + KV_RANK, 2 * D_MODEL + Q_RANK + KV_RANK + ROPE,
        3 * D_MODEL + Q_RANK + KV_RANK + ROPE)


def _w_in_t_to_pad(wt):
    u, v, cq, ckv, kr, ga, gb = jnp.split(wt, _SEG, axis=0)
    return jnp.concatenate([u, v, ga, gb, cq, ckv, kr, _rot_cols(kr, axis=0)], axis=0)


def _w_in_t_from_pad(gt):
    u, v, ga, gb, cq, ckv, kr, krr = jnp.split(
        gt, (D_MODEL, 2 * D_MODEL, 3 * D_MODEL, 4 * D_MODEL, 4 * D_MODEL + Q_RANK, 4 * D_MODEL + Q_RANK + KV_RANK,
             4 * D_MODEL + Q_RANK + KV_RANK + ROPE), axis=0)
    return jnp.concatenate([u, v, cq, ckv, kr + _rot_cols_t(krr, axis=0), ga, gb], axis=0)


def _w_uq_to_pad(w):
    t = w.reshape(Q_RANK, HEADS, QK_DIM)
    nope, rope = t[..., :NOPE], t[..., NOPE:]
    return jnp.concatenate([nope, rope, _rot_cols(rope)], axis=-1).reshape(Q_RANK, HEADS * HEAD_PAD)


def _w_uq_from_pad(g):
    t = g.reshape(Q_RANK, HEADS, HEAD_PAD)
    nope, rope, rot = t[..., :NOPE], t[..., NOPE:QK_DIM], t[..., QK_DIM:]
    return jnp.concatenate([nope, rope + _rot_cols_t(rot)], axis=-1).reshape(Q_RANK, HEADS * QK_DIM)


def _w_ukv_to_pad(w):
    t = w.reshape(KV_RANK, HEADS, 2, NOPE)
    return jnp.swapaxes(t, 1, 2).reshape(KV_RANK, 2 * HEADS * NOPE)


def _w_ukv_from_pad(g):
    t = g.reshape(KV_RANK, 2, HEADS, NOPE)
    return jnp.swapaxes(t, 1, 2).reshape(KV_RANK, 2 * HEADS * NOPE)


def _rope_tables(positions):
    inv_freq = 1.0 / (ROPE_THETA ** (jnp.arange(0, ROPE, 2, dtype=F32) / ROPE))
    ang = positions.astype(F32).reshape(-1, 1) * inv_freq
    cos, sin = jnp.cos(ang), jnp.sin(ang)
    zero = jnp.zeros((ang.shape[0], 64), F32)
    return jnp.concatenate([cos, cos, zero], axis=1), jnp.concatenate([sin, sin, zero], axis=1)


_BIG = ("w_in", "w_uq", "w_ukv", "w_out", "w_up", "w_down")
UP_SHARD = 2 * D_FF // N_CHIPS


def _local_step(x, positions, tgt, wts, mixer_weights, ffn_weights, on_ffn_grads, on_mixer_grads):
    B, S, D = x.shape
    T = B * S
    xf = x.reshape(T, D)
    cos_a, sin_a = _rope_tables(positions)
    bs_t = jnp.pad(wts["a_spatial_b"].T, ((0, 0), (0, 128 - A_GROUPS)))

    h = _rms_fwd(xf, wts["mix_norm"], "norm1_fwd")
    z = _mm(h, wts["w_in"], "nt", "in_proj", tm=512, tn=1536, tk=D)
    wts = dict(wts)
    wts["w_q"], wts["w_kv"], wts["w_out"] = mixer_weights(z)
    q, k, v, cqn, ckvn = _lat_fwd(z, wts["q_a_norm"], wts["kv_a_norm"], wts["w_q"], wts["w_kv"], cos_a, sin_a)
    yb, *lses = _attn_fwd(q, k, v, B, S)
    merged = _mix_fwd(z, yb, wts["a_v_norm_g"], wts["a_v_norm_b"], wts["a_spatial_w"], bs_t)
    x1 = _mm(merged, wts["w_out"], "nn", "out_proj", tm=512, tn=D, tk=D, add=xf)
    h2 = _rms_fwd(x1, wts["ffn_norm"], "norm2_fwd")
    wts["w_up"], wts["w_down"], wts["conv_w"] = ffn_weights(h2)
    up_pre = _mm(h2, wts["w_up"], "nn", "up_proj", tm=512, tn=UP_SHARD, tk=D, dims=(T, 2 * D_FF, D),
                 b_spec=pl.BlockSpec((None, D, UP_SHARD), lambda i, j, k: (j, 0, 0)),
                 o_spec=pl.BlockSpec((None, 512, UP_SHARD), lambda i, j, k: (j // 2, i, j % 2)), out_shape=(2, T, D_FF))
    act = _gate_fwd(up_pre, wts["conv_w"], wts["conv_b"], B, S)
    x2 = _mm(act, wts["w_down"], "nn", "down_proj", tm=512, tn=D, tk=1408, add=x1)
    dx2, loss_row, g_final = _final(x2, tgt.reshape(T, D), wts["final_norm"])

    g = {"final_norm": g_final}
    dact = _mm(dx2, wts["w_down"], "nt", "down_proj_dx", tm=512, tn=1408, tk=D)
    tk2, tk1 = min(2048, T), min(1024, T)
    g["w_down"] = _mm(act, dx2, "tn", "down_proj_dw", tm=1408, tn=D, tk=tk1)
    dup, g["conv_w"], g["conv_b"] = _gate_bwd(up_pre, dact, wts["conv_w"], wts["conv_b"], B, S)
    g["w_up"] = _mm(h2, dup, "tn", "up_proj_dw", tm=D, tn=UP_SHARD, tk=tk2, dims=(D, 2 * D_FF, T),
                    b_spec=pl.BlockSpec((None, tk2, UP_SHARD), lambda i, j, k: (j // 2, k, j % 2)),
                    o_spec=pl.BlockSpec((None, D, UP_SHARD), lambda i, j, k: (j, 0, 0)), out_shape=(N_CHIPS, D, UP_SHARD))
    ffn_sent = on_ffn_grads(g["w_up"], g["w_down"])
    dh2 = _mm(dup, wts["w_up"], "nt", "up_proj_dx", tm=512, tn=D, tk=UP_SHARD, dims=(T, D, 2 * D_FF),
              a_spec=pl.BlockSpec((None, 512, UP_SHARD), lambda i, j, k: (k // 2, i, k % 2)),
              b_spec=pl.BlockSpec((None, D, UP_SHARD), lambda i, j, k: (k, 0, 0)))
    token = None if ffn_sent is None else ffn_sent(dh2)
    ffn_norm = wts["ffn_norm"] if token is None else wts["ffn_norm"] + token[0:1, 0:1]
    dx1, g["ffn_norm"] = _rms_bwd(x1, ffn_norm, dh2, dx2, "norm2_bwd")
    dm = _mm(dx1, wts["w_out"], "nt", "out_proj_dx", tm=512, tn=D, tk=D)
    g["w_out"] = _mm(merged, dx1, "tn", "out_proj_dw", tm=D, tn=D, tk=tk1)
    dz, dyb, dl, g["a_spatial_w"], gbs, g["a_v_norm_g"], g["a_v_norm_b"] = _mix_bwd(
        z, yb, dm, wts["a_v_norm_g"], wts["a_v_norm_b"], wts["a_spatial_w"], bs_t)
    g["a_spatial_b"] = gbs[:, :A_GROUPS].T
    delta = dl.reshape(HEADS * T // ATT_BLOCK, 1, ATT_BLOCK)
    dq, dk, dv = _attn_bwd(q, k, v, dyb, lses, delta, B, S)
    dz, dq_raw, dkv, g["q_a_norm"], g["kv_a_norm"] = _lat_bwd(
        dz, z, dq, dk, dv, wts["q_a_norm"], wts["kv_a_norm"], wts["w_q"], wts["w_kv"], cos_a, sin_a)
    g["w_q"] = _mm(cqn, dq_raw, "tn", "q_proj_dw", tm=Q_RANK, tn=HEADS * HEAD_PAD, tk=tk2)
    g["w_kv"] = _mm(ckvn, dkv, "tn", "kv_proj_dw", tm=KV_RANK, tn=2 * HEADS * NOPE, tk=tk2)
    g["w_in"] = _mm(dz, h, "tn", "in_proj_dw", tm=1536, tn=D, tk=tk2)
    token = on_mixer_grads(g)
    mix_norm = wts["mix_norm"] if token is None else wts["mix_norm"] + token[0:1, 0:1]
    dh = _mm(dz, wts["w_in"], "nn", "in_proj_dx", tm=512, tn=D, tk=1536)
    dx, g["mix_norm"] = _rms_bwd(xf, mix_norm, dh, dx1, "norm1_bwd")
    return loss_row[0, 0], dx.reshape(B, S, D), g


_SMALL = (("mix_norm", (1, D_MODEL)), ("a_v_norm_g", (1, D_MODEL)), ("a_v_norm_b", (1, D_MODEL)),
          ("a_spatial_w", (A_GROUPS * CHUNK, CHUNK)), ("a_spatial_b", (1, A_GROUPS * CHUNK)), ("q_a_norm", (1, Q_RANK)),
          ("kv_a_norm", (1, KV_RANK)), ("ffn_norm", (1, D_MODEL)), ("conv_b", (1, 2 * D_FF)), ("final_norm", (1, D_MODEL)),
          ("conv_w", (3, 2 * D_FF)))
_SMALL_SIZE = sum(math.prod(s) for _, s in _SMALL)
_SMALL_ROWS = -(-(_SMALL_SIZE + 1) // (128 * 8)) * 8


def kernel(x, positions, mix_norm, w_in, a_v_norm_g, a_v_norm_b, a_spatial_w, a_spatial_b, q_a_norm, w_uq, kv_a_norm, w_ukv, w_out, ffn_norm, w_up, conv_w, conv_b, w_down, final_norm, loss_target, m_mix_norm, m_w_in, m_a_v_norm_g, m_a_v_norm_b, m_a_spatial_w, m_a_spatial_b, m_q_a_norm, m_w_uq, m_kv_a_norm, m_w_ukv, m_w_out, m_ffn_norm, m_w_up, m_conv_w, m_conv_b, m_w_down, m_final_norm, v_mix_norm, v_w_in, v_a_v_norm_g, v_a_v_norm_b, v_a_spatial_w, v_a_spatial_b, v_q_a_norm, v_w_uq, v_kv_a_norm, v_w_ukv, v_w_out, v_ffn_norm, v_w_up, v_conv_w, v_conv_b, v_w_down, v_final_norm):
    weights = dict(mix_norm=mix_norm, w_in=w_in, a_v_norm_g=a_v_norm_g, a_v_norm_b=a_v_norm_b, a_spatial_w=a_spatial_w,
                   a_spatial_b=a_spatial_b, q_a_norm=q_a_norm, w_uq=w_uq, kv_a_norm=kv_a_norm, w_ukv=w_ukv, w_out=w_out,
                   ffn_norm=ffn_norm, w_up=w_up, conv_w=conv_w, conv_b=conv_b, w_down=w_down, final_norm=final_norm)
    m_in = dict(mix_norm=m_mix_norm, w_in=m_w_in, a_v_norm_g=m_a_v_norm_g, a_v_norm_b=m_a_v_norm_b,
                a_spatial_w=m_a_spatial_w, a_spatial_b=m_a_spatial_b, q_a_norm=m_q_a_norm, w_uq=m_w_uq,
                kv_a_norm=m_kv_a_norm, w_ukv=m_w_ukv, w_out=m_w_out, ffn_norm=m_ffn_norm, w_up=m_w_up, conv_w=m_conv_w,
                conv_b=m_conv_b, w_down=m_w_down, final_norm=m_final_norm)
    v_in = dict(mix_norm=v_mix_norm, w_in=v_w_in, a_v_norm_g=v_a_v_norm_g, a_v_norm_b=v_a_v_norm_b,
                a_spatial_w=v_a_spatial_w, a_spatial_b=v_a_spatial_b, q_a_norm=v_q_a_norm, w_uq=v_w_uq,
                kv_a_norm=v_kv_a_norm, w_ukv=v_w_ukv, w_out=v_w_out, ffn_norm=v_ffn_norm, w_up=v_w_up, conv_w=v_conv_w,
                conv_b=v_conv_b, w_down=v_w_down, final_norm=v_final_norm)
    names = list(weights)
    chip = 2 * lax.axis_index("x") + lax.axis_index("y")

    def halves(a):
        return a.reshape(a.shape[:-2] + (2, a.shape[-2] // 2, a.shape[-1]))

    def whole(a):
        return a.reshape(a.shape[:-3] + (2 * a.shape[-2], a.shape[-1]))

    w_in_t = jnp.swapaxes(w_in[0], 0, 1).astype(MXU_DTYPE)
    (w_in_sh,) = _gather_weights([jnp.stack(jnp.split(w_in_t, 2, axis=1))])
    mixer_gather = _exchange_start([weights[n][0].astype(MXU_DTYPE) for n in _BIG[1:4]],
                                   "mixer_gather_start", "gather", after=w_in_sh)
    ffn_gather = _exchange_start([w_up[0].astype(MXU_DTYPE), w_down[0].astype(MXU_DTYPE), conv_w[0]],
                                 "ffn_gather_start", "gather", after=mixer_gather[4])
    wts = dict(
        mix_norm=mix_norm + ffn_gather[4][0:1, 0:1], a_v_norm_g=a_v_norm_g, a_v_norm_b=a_v_norm_b,
        a_spatial_w=a_spatial_w[0], a_spatial_b=a_spatial_b[0], q_a_norm=q_a_norm, kv_a_norm=kv_a_norm,
        ffn_norm=ffn_norm, final_norm=final_norm.reshape(1, D_MODEL),
        w_in=_w_in_t_to_pad(jnp.concatenate([w_in_sh[:, 0], w_in_sh[:, 1]], axis=-1).reshape(-1, D_MODEL)),
        conv_b=conv_b.reshape(2, 1, D_FF))

    def mixer_weights(after):
        _, (w_uq_sh, w_ukv_sh, w_out_sh) = _exchange_wait(mixer_gather, "mixer_gather_wait", "gather", after)
        return (_w_uq_to_pad(_cols_from_chips(w_uq_sh)), _w_ukv_to_pad(_cols_from_chips(w_ukv_sh)),
                w_out_sh.reshape(D_MODEL, D_MODEL))

    def ffn_weights(after):
        _, (w_up_sh, w_down_sh, cw_all) = _exchange_wait(ffn_gather, "ffn_gather_wait", "gather", after)
        return w_up_sh, w_down_sh.reshape(D_FF, D_MODEL), _conv_w_split(_cols_from_chips(cw_all))

    scatters = {}

    def start_scatter(slabs, tag):
        sums = _pair_sum(slabs, _swap_halves([halves(s) for s in slabs], tag + "_grad_swap_halves"), tag + "_grad_pair_sum")
        scatters[tag] = _exchange_start(list(sums), tag + "_scatter_start", "scatter", after=slabs[-1])
        return scatters[tag][4]

    def on_ffn_grads(g_w_up, g_w_down):
        token = start_scatter([g_w_up, g_w_down.reshape(N_CHIPS, D_FF // N_CHIPS, D_MODEL)], "ffn")
        return lambda after: token

    def on_mixer_grads(g):
        return start_scatter(
            [_w_in_t_from_pad(g["w_in"]).reshape(N_CHIPS, -1, D_MODEL), _cols_to_chips(_w_uq_from_pad(g["w_q"])),
             _cols_to_chips(_w_ukv_from_pad(g["w_kv"])), g["w_out"].reshape(N_CHIPS, D_MODEL // N_CHIPS, D_MODEL)], "mixer")

    loss_part, grad_x, g = _local_step(x, positions, loss_target, wts, mixer_weights, ffn_weights, on_ffn_grads,
                                       on_mixer_grads)

    g_small_parts = dict(g)
    g_small_parts["conv_w"] = _conv_w_join(g["conv_w"])
    g_small_parts["conv_b"] = g["conv_b"].reshape(1, 2 * D_FF)
    flat = jnp.concatenate([g_small_parts[n].reshape(-1) for n, _ in _SMALL] + [loss_part.reshape(1)])
    flat = jnp.pad(flat, (0, _SMALL_ROWS * 128 - flat.shape[0])).reshape(_SMALL_ROWS, 128)
    small_gather = _exchange_start([flat], "small_gather_start", "all", after=grad_x)

    mixer_sums, mixer_landed = _exchange_wait(scatters["mixer"], "mixer_scatter_wait", "scatter", after=small_gather[4])
    ffn_sums, ffn_landed = _exchange_wait(scatters["ffn"], "ffn_scatter_wait", "scatter", after=mixer_landed[0])
    reduced = _chip_sum(list(mixer_sums) + list(ffn_sums), list(mixer_landed) + list(ffn_landed))
    g_big = dict(zip(_BIG, _join_halves(reduced)))

    grads, deltas, new_m, new_v = {}, {}, {}, {}

    def update(n, grad):
        w = weights[n]
        shape2 = grad.shape
        d, nm, nv = _adamw(w.reshape(shape2), grad, m_in[n].reshape(shape2), v_in[n].reshape(shape2), "adamw_" + n)
        grads[n], deltas[n], new_m[n], new_v[n] = (t.reshape(w.shape) for t in (grad, d, nm, nv))

    def update_transposed(n, grad_t):
        t = lambda a: jnp.swapaxes(a, 1, 2)
        d, nm, nv = _adamw(t(weights[n]), grad_t, t(m_in[n]), t(v_in[n]), "adamw_" + n)
        grads[n], deltas[n], new_m[n], new_v[n] = t(grad_t), t(d), t(nm), t(nv)

    for n in _BIG:
        g3 = g_big[n].reshape((1, -1, g_big[n].shape[-1]))
        if n == "w_in":
            update_transposed(n, g3)
        else:
            update(n, g3)

    (own,), (everyone,) = _exchange_wait(small_gather, "small_gather_wait", "all", after=deltas["w_up"])
    device = 2 * chip + lax.axis_index("c")
    everyone = lax.dynamic_update_slice(everyone, own[None], (device, 0, 0))
    total = _sum_slabs([everyone[j] for j in range(8)], "small_grads_sum", tr=_SMALL_ROWS).reshape(-1)
    o = 0
    for n, shp in _SMALL:
        piece = total[o:o + math.prod(shp)].reshape(shp)
        o += math.prod(shp)
        if n == "conv_w":
            piece = lax.dynamic_slice_in_dim(piece, chip * UP_SHARD, UP_SHARD, axis=1)
        update(n, piece)
    loss = total[_SMALL_SIZE]
    return (loss, grad_x, *[grads[n] for n in names], *[deltas[n] for n in names], *[new_m[n] for n in names],
            *[new_v[n] for n in names])
```

```python
import functools
import math

import jax
import jax.numpy as jnp
from jax import lax
from jax.experimental import pallas as pl
from jax.experimental.pallas import tpu as pltpu

F32 = jnp.float32
MXU_DTYPE = jnp.bfloat16
MESH = pl.DeviceIdType.MESH

D_MODEL = 1024
EPS = 1e-6
A_GROUPS = 8
CHUNK = 128
HEADS = 8
NOPE = 128
ROPE = 64
QK_DIM = NOPE + ROPE
HEAD_PAD = 256
Q_RANK = 256
KV_RANK = 128
ROPE_THETA = 10000.0
D_FF = 2816
FF_TILE = 256
N_FF_TILES = D_FF // FF_TILE
LAT = 512
IN_PAD = 4 * D_MODEL + LAT
N_CHIPS = 4
ADAM_LR, ADAM_B1, ADAM_B2, ADAM_EPS, ADAM_WD, ADAM_STEP = 0.001, 0.9, 0.999, 1e-08, 0.01, 10

VMEM_CAP_V7X = 64 * 1024 * 1024
NEG = -1e30


def _params(sem, nbytes):
    limit = int(min(VMEM_CAP_V7X - (8 << 20), max(32 << 20, 3 * nbytes)))
    return pltpu.CompilerParams(dimension_semantics=sem, vmem_limit_bytes=limit)


def _nbytes(shape, dtype):
    return math.prod(shape) * jnp.dtype(dtype).itemsize


_DIMS = {"nn": (((1,), (0,)), ((), ())), "nt": (((1,), (1,)), ((), ())), "tn": (((0,), (0,)), ((), ()))}


def _mm(a, b, mode, name, *, tm, tn, tk, out_dtype=F32, add=None, dims=None, a_spec=None, b_spec=None,
        o_spec=None, out_shape=None, n_outer=False, copy_dtype=None):
    if dims is None:
        if mode == "nn":
            (M, K), (_, N) = a.shape, b.shape
        elif mode == "nt":
            (M, K), (N, _) = a.shape, b.shape
        else:
            (K, M), (_, N) = a.shape, b.shape
    else:
        M, N, K = dims
    a_blk = (tk, tm) if mode == "tn" else (tm, tk)
    b_blk = (tn, tk) if mode == "nt" else (tk, tn)
    if a_spec is None:
        a_spec = pl.BlockSpec(a_blk, (lambda i, j, k: (k, i)) if mode == "tn" else (lambda i, j, k: (i, k)))
    if b_spec is None:
        b_spec = pl.BlockSpec(b_blk, (lambda i, j, k: (j, k)) if mode == "nt" else (lambda i, j, k: (k, j)))
    if o_spec is None:
        o_spec = pl.BlockSpec((tm, tn), lambda i, j, k: (i, j))
    if out_shape is None:
        out_shape = (M, N)
    assert M % tm == 0 and N % tn == 0 and K % tk == 0, (name, M, N, K, tm, tn, tk)
    nk = K // tk
    contract = _DIMS[mode]
    has_add = add is not None

    def body(*refs):
        a_ref, b_ref = refs[0], refs[1]
        add_ref = refs[2] if has_add else None
        o_ref = refs[3] if has_add else refs[2]
        copy_ref = (refs[4] if has_add else refs[3]) if copy_dtype is not None else None

        def product():
            return lax.dot_general(a_ref[...].astype(MXU_DTYPE), b_ref[...].astype(MXU_DTYPE), contract,
                                   preferred_element_type=F32)

        def finish(r):
            if has_add:
                r = r + add_ref[...]
            o_ref[...] = r.astype(out_dtype)
            if copy_ref is not None:
                copy_ref[...] = r.astype(copy_dtype)

        if nk == 1:
            finish(product())
            return
        acc = refs[-1]
        k = pl.program_id(2)

        @pl.when(k == 0)
        def _():
            acc[...] = jnp.zeros_like(acc)

        acc[...] += product()

        @pl.when(k == nk - 1)
        def _():
            finish(acc[...])

    in_specs = [a_spec, b_spec]
    args = [a, b]
    nbytes = _nbytes(a_blk, a.dtype) + _nbytes(b_blk, b.dtype) + 3 * _nbytes((tm, tn), F32)
    if has_add:
        in_specs.append(pl.BlockSpec((tm, tn), lambda i, j, k: (i, j)))
        args.append(add)
        nbytes += _nbytes((tm, tn), F32)
    grid = (M // tm, N // tn, nk)
    if n_outer:
        def swapped(spec):
            return pl.BlockSpec(spec.block_shape, lambda j, i, k, at=spec.index_map: at(i, j, k))

        grid = (N // tn, M // tm, nk)
        in_specs = [swapped(s) for s in in_specs]
        o_spec = swapped(o_spec)
    out_sds, out_specs = jax.ShapeDtypeStruct(out_shape, out_dtype), o_spec
    if copy_dtype is not None:
        out_sds, out_specs = (out_sds, jax.ShapeDtypeStruct(out_shape, copy_dtype)), (o_spec, o_spec)
    return pl.pallas_call(
        body, name=name, out_shape=out_sds, grid=grid, in_specs=in_specs, out_specs=out_specs,
        scratch_shapes=[pltpu.VMEM((tm, tn), F32)] if nk > 1 else [],
        compiler_params=_params(("parallel", "parallel", "arbitrary"), nbytes),
    )(*args)


_GELU_C = math.sqrt(2.0 / math.pi)
_GELU_A = 0.044715


def _sigmoid(x):
    return 0.5 * jnp.tanh(0.5 * x) + 0.5


def _gelu(x):
    t = jnp.tanh(x * (_GELU_C + (_GELU_C * _GELU_A) * (x * x)))
    return x * (0.5 + 0.5 * t)


def _gelu_and_grad(x):
    x2 = x * x
    t = jnp.tanh(x * (_GELU_C + (_GELU_C * _GELU_A) * x2))
    cdf = 0.5 + 0.5 * t
    grad = cdf + (0.5 * x) * (1.0 - t * t) * (_GELU_C + (3.0 * _GELU_C * _GELU_A) * x2)
    return x * cdf, grad


def _rope_mix(g, cos_a, sin_a):
    return g * cos_a + pltpu.roll(g, 64, 1) * sin_a


def _rope_mix_bwd(d, cos_a, sin_a):
    return d * cos_a + pltpu.roll(d * sin_a, 64, 1)


def _rms_fwd(x, g, name, tr=512):
    T, D = x.shape

    def body(x_ref, g_ref, h_ref):
        xv = x_ref[...]
        r = lax.rsqrt(jnp.mean(xv * xv, axis=-1, keepdims=True) + EPS)
        h_ref[...] = ((xv * r) * g_ref[...]).astype(h_ref.dtype)

    return pl.pallas_call(
        body, name=name, out_shape=jax.ShapeDtypeStruct((T, D), MXU_DTYPE), grid=(T // tr,),
        in_specs=[pl.BlockSpec((tr, D), lambda i: (i, 0)), pl.BlockSpec((1, D), lambda i: (0, 0))],
        out_specs=pl.BlockSpec((tr, D), lambda i: (i, 0)),
        compiler_params=_params(("parallel",), 3 * _nbytes((tr, D), F32)),
    )(x, g)


def _rms_bwd(x, g, dh, dres, name, tr=512):
    T, D = x.shape

    def body(x_ref, g_ref, dh_ref, dres_ref, dx_ref, gg_ref):
        @pl.when(pl.program_id(0) == 0)
        def _():
            gg_ref[...] = jnp.zeros_like(gg_ref)

        xv = x_ref[...]
        r = lax.rsqrt(jnp.mean(xv * xv, axis=-1, keepdims=True) + EPS)
        xn = xv * r
        dhv = dh_ref[...]
        dxn = dhv * g_ref[...]
        dx_ref[...] = dres_ref[...] + r * (dxn - xn * jnp.mean(dxn * xn, axis=-1, keepdims=True))
        gg_ref[...] += jnp.sum(dhv * xn, axis=0, keepdims=True)

    row = pl.BlockSpec((tr, D), lambda i: (i, 0))
    vec = pl.BlockSpec((1, D), lambda i: (0, 0))
    return pl.pallas_call(
        body, name=name,
        out_shape=(jax.ShapeDtypeStruct((T, D), F32), jax.ShapeDtypeStruct((1, D), F32)),
        grid=(T // tr,), in_specs=[row, vec, row, row], out_specs=(row, vec),
        compiler_params=_params(("arbitrary",), 6 * _nbytes((tr, D), F32)),
    )(x, g, dh, dres)


def _lat_fwd(z, gq, gkv, wq, wkv, cos_a, sin_a, tr=256):
    T = z.shape[0]
    lat_blk = (4 * D_MODEL) // LAT

    def body(z_ref, gq_ref, gkv_ref, wq_ref, wkv_ref, cos_ref, sin_ref, q_ref, k_ref, v_ref, cqn_ref, ckvn_ref):
        zl = z_ref[...]
        cos_v, sin_v = cos_ref[...], sin_ref[...]
        cq = zl[:, :Q_RANK]
        ckv = zl[:, Q_RANK:Q_RANK + KV_RANK]
        krb = zl[:, Q_RANK + KV_RANK:]
        cqn = ((cq * lax.rsqrt(jnp.mean(cq * cq, axis=-1, keepdims=True) + EPS)) * gq_ref[...]).astype(MXU_DTYPE)
        ckvn = ((ckv * lax.rsqrt(jnp.mean(ckv * ckv, axis=-1, keepdims=True) + EPS)) * gkv_ref[...]).astype(MXU_DTYPE)
        cqn_ref[...] = cqn
        ckvn_ref[...] = ckvn
        krr = _rope_mix(krb, cos_v, sin_v).astype(MXU_DTYPE)
        q = jnp.dot(cqn, wq_ref[...], preferred_element_type=F32)
        kv = jnp.dot(ckvn, wkv_ref[...], preferred_element_type=F32)
        for h in range(HEADS):
            o = h * HEAD_PAD
            q_ref[:, o:o + NOPE] = q[:, o:o + NOPE].astype(MXU_DTYPE)
            q_ref[:, o + NOPE:o + HEAD_PAD] = _rope_mix(q[:, o + NOPE:o + HEAD_PAD], cos_v, sin_v).astype(MXU_DTYPE)
            k_ref[:, o:o + NOPE] = kv[:, h * NOPE:(h + 1) * NOPE].astype(MXU_DTYPE)
            k_ref[:, o + NOPE:o + HEAD_PAD] = krr
        v_ref[...] = kv[:, HEADS * NOPE:].astype(MXU_DTYPE)

    def row(w):
        return pl.BlockSpec((tr, w), lambda i: (i, 0))

    def full(a):
        return pl.BlockSpec(a.shape, lambda i: (0, 0))

    return pl.pallas_call(
        body, name="lat_fwd",
        out_shape=(jax.ShapeDtypeStruct((T, HEADS * HEAD_PAD), MXU_DTYPE), jax.ShapeDtypeStruct((T, HEADS * HEAD_PAD), MXU_DTYPE),
                   jax.ShapeDtypeStruct((T, HEADS * NOPE), MXU_DTYPE), jax.ShapeDtypeStruct((T, Q_RANK), MXU_DTYPE),
                   jax.ShapeDtypeStruct((T, KV_RANK), MXU_DTYPE)),
        grid=(T // tr,),
        in_specs=[pl.BlockSpec((tr, LAT), lambda i: (i, lat_blk)), full(gq), full(gkv), full(wq), full(wkv), row(128), row(128)],
        out_specs=(row(HEADS * HEAD_PAD), row(HEADS * HEAD_PAD), row(HEADS * NOPE), row(Q_RANK), row(KV_RANK)),
        compiler_params=_params(("parallel",), 8 * _nbytes((tr, HEADS * HEAD_PAD), F32)),
    )(z, gq, gkv, wq, wkv, cos_a, sin_a)


ATT_BLOCK = 256
_SCALE = QK_DIM ** -0.5


def _causal_mask(n):
    return lax.broadcasted_iota(jnp.int32, (n, n), 1) <= lax.broadcasted_iota(jnp.int32, (n, n), 0)


def _causal_mask_t(n):
    return lax.broadcasted_iota(jnp.int32, (n, n), 0) <= lax.broadcasted_iota(jnp.int32, (n, n), 1)


ATT_HEADS = 4


def _attn_fwd(q, k, v, B, S):
    tq = ATT_BLOCK
    nq = S // tq
    T = B * S
    hp, groups = ATT_HEADS, HEADS // ATT_HEADS

    def body(q_ref, k_ref, v_ref, o_ref, *lse_refs):
        qi = pl.program_id(2)
        qs = [q_ref[:, t * HEAD_PAD:(t + 1) * HEAD_PAD] for t in range(hp)]

        def scores(j, t):
            rows = pl.ds(pl.multiple_of(j * tq, tq), tq)
            return lax.dot_general(k_ref[rows, t * HEAD_PAD:(t + 1) * HEAD_PAD], qs[t], _DIMS["nt"],
                                   preferred_element_type=F32)

        def step(j, carry, last):
            rows = pl.ds(pl.multiple_of(j * tq, tq), tq)
            out = []
            for t in range(hp):
                m, l, acc, st = carry[t]
                st_next = st if last else scores(j + 1, t)
                st = st * _SCALE
                if last:
                    st = jnp.where(_causal_mask_t(tq), st, NEG)
                m_new = jnp.maximum(m, jnp.max(st, axis=0, keepdims=True))
                alpha = jnp.exp(m - m_new)
                p = jnp.exp(st - m_new)
                l = alpha * l + jnp.sum(p, axis=0, keepdims=True)
                acc = alpha * acc + lax.dot_general(v_ref[rows, t * NOPE:(t + 1) * NOPE], p.astype(MXU_DTYPE),
                                                    _DIMS["tn"], preferred_element_type=F32)
                out.append((m_new, l, acc, st_next))
            return tuple(out)

        init = tuple((jnp.full((1, tq), NEG, F32), jnp.zeros((1, tq), F32), jnp.zeros((NOPE, tq), F32), scores(0, t))
                     for t in range(hp))
        carry = lax.fori_loop(0, qi, lambda j, c: step(j, c, False), init)
        carry = step(qi, carry, True)
        for t in range(hp):
            m, l, acc, _ = carry[t]
            o_ref[:, t * NOPE:(t + 1) * NOPE] = (acc / l).T
            lse_refs[t][0] = m + jnp.log(l)

    lse_sds = jax.ShapeDtypeStruct((groups * B * nq, 1, tq), F32)
    lse_spec = pl.BlockSpec((1, 1, tq), lambda b, h, i: ((h * B + b) * nq + i, 0, 0))
    return pl.pallas_call(
        body, name="attn_fwd",
        out_shape=(jax.ShapeDtypeStruct((T, HEADS * NOPE), F32),) + (lse_sds,) * hp,
        grid=(B, groups, nq),
        in_specs=[pl.BlockSpec((tq, hp * HEAD_PAD), lambda b, h, i: (b * nq + i, h)),
                  pl.BlockSpec((S, hp * HEAD_PAD), lambda b, h, i: (b, h)),
                  pl.BlockSpec((S, hp * NOPE), lambda b, h, i: (b, h))],
        out_specs=(pl.BlockSpec((tq, hp * NOPE), lambda b, h, i: (b * nq + i, h)),) + (lse_spec,) * hp,
        compiler_params=_params(("parallel", "parallel", "arbitrary"), 4 * hp * _nbytes((S, HEAD_PAD), MXU_DTYPE)),
    )(q, k, v)


def _attn_bwd(q, k, v, do, lses, delta, B, S):
    tq = ATT_BLOCK
    nq = S // tq
    T = B * S
    hp, groups = ATT_HEADS, HEADS // ATT_HEADS

    def body(q_ref, k_ref, v_ref, do_ref, *refs):
        lse_refs, dl_refs = refs[:hp], refs[hp:2 * hp]
        dq_out, dk_ref, dv_ref, dq_ref = refs[2 * hp:]
        kj = pl.program_id(2)

        @pl.when(kj == 0)
        def _():
            dq_ref[...] = jnp.zeros_like(dq_ref)

        def products(i, t):
            rows = pl.ds(pl.multiple_of(i * tq, tq), tq)
            st = lax.dot_general(k_ref[:, t * HEAD_PAD:(t + 1) * HEAD_PAD], q_ref[rows, t * HEAD_PAD:(t + 1) * HEAD_PAD],
                                 _DIMS["nt"], preferred_element_type=F32)
            dpt = lax.dot_general(v_ref[:, t * NOPE:(t + 1) * NOPE], do_ref[rows, t * NOPE:(t + 1) * NOPE],
                                  _DIMS["nt"], preferred_element_type=F32)
            return st, dpt

        def step(i, carry, masked):
            rows = pl.ds(pl.multiple_of(i * tq, tq), tq)
            nxt = jnp.minimum(i + 1, nq - 1)
            out = []
            for t in range(hp):
                dk, dv, st, dpt = carry[t]
                st_next, dpt_next = products(nxt, t)
                qk_cols = slice(t * HEAD_PAD, (t + 1) * HEAD_PAD)
                v_cols = slice(t * NOPE, (t + 1) * NOPE)
                p = jnp.exp(st * _SCALE - lse_refs[t][i])
                if masked:
                    p = jnp.where(_causal_mask_t(tq), p, 0.0)
                dv = dv + jnp.dot(p.astype(MXU_DTYPE), do_ref[rows, v_cols], preferred_element_type=F32)
                ds = (p * (dpt - dl_refs[t][i]) * _SCALE).astype(MXU_DTYPE)
                dk = dk + jnp.dot(ds, q_ref[rows, qk_cols], preferred_element_type=F32)
                dq_ref[rows, qk_cols] += lax.dot_general(ds, k_ref[:, qk_cols], _DIMS["tn"], preferred_element_type=F32)
                out.append((dk, dv, st_next, dpt_next))
            return tuple(out)

        init = tuple((jnp.zeros((tq, HEAD_PAD), F32), jnp.zeros((tq, NOPE), F32)) + products(kj, t) for t in range(hp))
        carry = step(kj, init, True)
        carry = lax.fori_loop(kj + 1, nq, lambda i, c: step(i, c, False), carry)
        for t in range(hp):
            dk_ref[:, t * HEAD_PAD:(t + 1) * HEAD_PAD] = carry[t][0].astype(dk_ref.dtype)
            dv_ref[:, t * NOPE:(t + 1) * NOPE] = carry[t][1].astype(dv_ref.dtype)

        @pl.when(kj == nq - 1)
        def _():
            dq_out[...] = dq_ref[...].astype(dq_out.dtype)

    seq = lambda w: pl.BlockSpec((S, w), lambda b, h, j: (b, h))
    blk = lambda w: pl.BlockSpec((tq, w), lambda b, h, j: (b * nq + j, h))
    lse_spec = pl.BlockSpec((nq, 1, tq), lambda b, h, j: (h * B + b, 0, 0))
    dl_specs = [pl.BlockSpec((nq, 1, tq), lambda b, h, j, t=t: ((h * hp + t) * B + b, 0, 0)) for t in range(hp)]
    return pl.pallas_call(
        body, name="attn_bwd",
        out_shape=(jax.ShapeDtypeStruct((T, HEADS * HEAD_PAD), MXU_DTYPE), jax.ShapeDtypeStruct((T, HEADS * HEAD_PAD), MXU_DTYPE),
                   jax.ShapeDtypeStruct((T, HEADS * NOPE), MXU_DTYPE)),
        grid=(B, groups, nq),
        in_specs=[seq(hp * HEAD_PAD), blk(hp * HEAD_PAD), blk(hp * NOPE), seq(hp * NOPE)] + [lse_spec] * hp + dl_specs,
        out_specs=(seq(hp * HEAD_PAD), blk(hp * HEAD_PAD), blk(hp * NOPE)),
        scratch_shapes=[pltpu.VMEM((S, hp * HEAD_PAD), F32)],
        compiler_params=_params(("parallel", "parallel", "arbitrary"), 8 * hp * _nbytes((S, HEAD_PAD), F32)),
    )(q, k, v, do, *lses, *([delta] * hp))


MIX_ROWS = 256


def _tril_weights(ws_ref, g):
    return jnp.where(_causal_mask(CHUNK), ws_ref[g], 0.0).astype(MXU_DTYPE)


def _layer_norm_stats(va):
    mu = jnp.mean(va, axis=-1, keepdims=True)
    xc = va - mu
    rs = lax.rsqrt(jnp.mean(xc * xc, axis=-1, keepdims=True) + EPS)
    return xc * rs


def _mix_specs(tr):
    zcol = lambda c: pl.BlockSpec((tr, D_MODEL), lambda i, c=c: (i, c))
    row = pl.BlockSpec((tr, D_MODEL), lambda i: (i, 0))
    vec = pl.BlockSpec((1, D_MODEL), lambda i: (0, 0))
    ws = pl.BlockSpec((A_GROUPS, CHUNK, CHUNK), lambda i: (0, 0, 0))
    bs = pl.BlockSpec((CHUNK, 128), lambda i: (0, 0))
    return zcol, row, vec, ws, bs


def _mix_fwd(z, yb, ln_g, ln_b, ws, bs_t):
    T = z.shape[0]
    tr = MIX_ROWS
    zcol, row, vec, ws_spec, bs_spec = _mix_specs(tr)

    def body(zu_ref, zv_ref, zga_ref, zgb_ref, yb_ref, g_ref, b_ref, ws_ref, bs_ref, out_ref, vn_s):
        vhat = _layer_norm_stats(_gelu(zv_ref[...]))
        vn_s[...] = (vhat * g_ref[...] + b_ref[...]).astype(MXU_DTYPE)
        for g in range(A_GROUPS):
            w = _tril_weights(ws_ref, g)
            bias = bs_ref[:, g:g + 1]
            cols = slice(g * CHUNK, (g + 1) * CHUNK)
            for c in range(tr // CHUNK):
                rows = slice(c * CHUNK, (c + 1) * CHUNK)
                mixed = jnp.dot(w, vn_s[rows, cols], preferred_element_type=F32) + bias
                ya = _gelu(zu_ref[rows, cols]) * mixed
                merged = _sigmoid(zga_ref[rows, cols]) * ya + _sigmoid(zgb_ref[rows, cols]) * yb_ref[rows, cols]
                out_ref[rows, cols] = merged.astype(MXU_DTYPE)

    return pl.pallas_call(
        body, name="mix_fwd", out_shape=jax.ShapeDtypeStruct((T, D_MODEL), MXU_DTYPE), grid=(T // tr,),
        in_specs=[zcol(0), zcol(1), zcol(2), zcol(3), row, vec, vec, ws_spec, bs_spec], out_specs=row,
        scratch_shapes=[pltpu.VMEM((tr, D_MODEL), MXU_DTYPE)],
        compiler_params=_params(("parallel",), 8 * _nbytes((tr, D_MODEL), F32)),
    )(z, z, z, z, yb, ln_g, ln_b, ws, bs_t)


def _mix_bwd(z, yb, dm, ln_g, ln_b, ws, bs_t):
    T = z.shape[0]
    tr = MIX_ROWS
    zcol, row, vec, ws_spec, bs_spec = _mix_specs(tr)

    def body(zu_ref, zv_ref, zga_ref, zgb_ref, yb_ref, dm_ref, g_ref, b_ref, ws_ref, bs_ref,
             dz_ref, dyb_ref, dl_ref, gws_ref, gbs_ref, glg_ref, glb_ref, vn_s, dvn_s):
        @pl.when(pl.program_id(0) == 0)
        def _():
            gws_ref[...] = jnp.zeros_like(gws_ref)
            gbs_ref[...] = jnp.zeros_like(gbs_ref)
            glg_ref[...] = jnp.zeros_like(glg_ref)
            glb_ref[...] = jnp.zeros_like(glb_ref)

        lane = lax.broadcasted_iota(jnp.int32, (CHUNK, 128), 1)
        va, dgelu_v = _gelu_and_grad(zv_ref[...])
        mu = jnp.mean(va, axis=-1, keepdims=True)
        xc = va - mu
        rs = lax.rsqrt(jnp.mean(xc * xc, axis=-1, keepdims=True) + EPS)
        vhat = xc * rs
        vn_s[...] = (vhat * g_ref[...] + b_ref[...]).astype(MXU_DTYPE)
        gbs_acc = jnp.zeros((CHUNK, 128), F32)
        for g in range(A_GROUPS):
            w = _tril_weights(ws_ref, g)
            bias = bs_ref[:, g:g + 1]
            cols = slice(g * CHUNK, (g + 1) * CHUNK)
            gw_acc = jnp.zeros((CHUNK, CHUNK), F32)
            for c in range(tr // CHUNK):
                rows = slice(c * CHUNK, (c + 1) * CHUNK)
                vn = vn_s[rows, cols]
                mixed = jnp.dot(w, vn, preferred_element_type=F32) + bias
                ua, dgelu_u = _gelu_and_grad(zu_ref[rows, cols])
                dmv = dm_ref[rows, cols]
                sa = _sigmoid(zga_ref[rows, cols])
                dya = dmv * sa
                dz_ref[rows, 2 * D_MODEL + g * CHUNK:2 * D_MODEL + (g + 1) * CHUNK] = (
                    dmv * (ua * mixed) * (sa * (1.0 - sa))).astype(dz_ref.dtype)
                dz_ref[rows, cols] = (dya * mixed * dgelu_u).astype(dz_ref.dtype)
                dmix = dya * ua
                gbs_acc = gbs_acc + jnp.where(lane == g, jnp.sum(dmix, axis=-1, keepdims=True), 0.0)
                dmix_b = dmix.astype(MXU_DTYPE)
                gw_acc = gw_acc + lax.dot_general(dmix_b, vn, _DIMS["nt"], preferred_element_type=F32)
                dvn_s[rows, cols] = lax.dot_general(w, dmix_b, _DIMS["tn"], preferred_element_type=F32)
            gws_ref[g] += jnp.where(_causal_mask(CHUNK), gw_acc, 0.0)
        gbs_ref[...] += gbs_acc

        dvn = dvn_s[...]
        glg_ref[...] += jnp.sum(dvn * vhat, axis=0, keepdims=True)
        glb_ref[...] += jnp.sum(dvn, axis=0, keepdims=True)
        dvh = dvn * g_ref[...]
        dva = rs * (dvh - jnp.mean(dvh, axis=-1, keepdims=True) - vhat * jnp.mean(dvh * vhat, axis=-1, keepdims=True))
        dz_ref[:, D_MODEL:2 * D_MODEL] = (dva * dgelu_v).astype(dz_ref.dtype)

        dmv = dm_ref[...]
        ybv = yb_ref[...]
        sb = _sigmoid(zgb_ref[...])
        dyb = dmv * sb
        dyb_ref[...] = dyb.astype(dyb_ref.dtype)
        dz_ref[:, 3 * D_MODEL:4 * D_MODEL] = (dmv * ybv * (sb * (1.0 - sb))).astype(dz_ref.dtype)
        dz_ref[:, 4 * D_MODEL:] = jnp.zeros((tr, LAT), dz_ref.dtype)
        prod = dyb * ybv
        sel = (lax.broadcasted_iota(jnp.int32, (HEADS, D_MODEL), 1) // NOPE
               == lax.broadcasted_iota(jnp.int32, (HEADS, D_MODEL), 0)).astype(jnp.bfloat16)
        hi = prod.astype(jnp.bfloat16)
        rest = prod - hi.astype(F32)
        mid = rest.astype(jnp.bfloat16)
        lo = (rest - mid.astype(F32)).astype(jnp.bfloat16)
        dl_ref[...] = (lax.dot_general(sel, hi, _DIMS["nt"], preferred_element_type=F32)
                       + lax.dot_general(sel, mid, _DIMS["nt"], preferred_element_type=F32)
                       + lax.dot_general(sel, lo, _DIMS["nt"], preferred_element_type=F32))

    return pl.pallas_call(
        body, name="mix_bwd",
        out_shape=(jax.ShapeDtypeStruct((T, IN_PAD), MXU_DTYPE), jax.ShapeDtypeStruct((T, D_MODEL), MXU_DTYPE),
                   jax.ShapeDtypeStruct((HEADS, T), F32), jax.ShapeDtypeStruct((A_GROUPS, CHUNK, CHUNK), F32),
                   jax.ShapeDtypeStruct((CHUNK, 128), F32), jax.ShapeDtypeStruct((1, D_MODEL), F32),
                   jax.ShapeDtypeStruct((1, D_MODEL), F32)),
        grid=(T // tr,),
        in_specs=[zcol(0), zcol(1), zcol(2), zcol(3), row, row, vec, vec, ws_spec, bs_spec],
        out_specs=(pl.BlockSpec((tr, IN_PAD), lambda i: (i, 0)), row, pl.BlockSpec((HEADS, tr), lambda i: (0, i)),
                   ws_spec, bs_spec, vec, vec),
        scratch_shapes=[pltpu.VMEM((tr, D_MODEL), MXU_DTYPE), pltpu.VMEM((tr, D_MODEL), F32)],
        compiler_params=_params(("arbitrary",), 12 * _nbytes((tr, D_MODEL), F32)),
    )(z, z, z, z, yb, dm, ln_g, ln_b, ws, bs_t)


def _lat_bwd(dz, z, dq, dk, dv, gq, gkv, wq, wkv, cos_a, sin_a, tr=256):
    T = z.shape[0]
    lat_blk = (4 * D_MODEL) // LAT

    def body(dz_in, z_ref, dq_ref, dk_ref, dv_ref, gq_ref, gkv_ref, wq_ref, wkv_ref, cos_ref, sin_ref,
             dz_ref, dqr_ref, dkv_ref, ggq_ref, ggkv_ref):
        del dz_in

        @pl.when(pl.program_id(0) == 0)
        def _():
            ggq_ref[...] = jnp.zeros_like(ggq_ref)
            ggkv_ref[...] = jnp.zeros_like(ggkv_ref)

        cos_v, sin_v = cos_ref[...], sin_ref[...]
        dkr = jnp.zeros((tr, 128), F32)
        for h in range(HEADS):
            o = h * HEAD_PAD
            dqr_ref[:, o:o + NOPE] = dq_ref[:, o:o + NOPE].astype(MXU_DTYPE)
            dqr_ref[:, o + NOPE:o + HEAD_PAD] = _rope_mix_bwd(dq_ref[:, o + NOPE:o + HEAD_PAD], cos_v, sin_v).astype(MXU_DTYPE)
            dkv_ref[:, h * NOPE:(h + 1) * NOPE] = dk_ref[:, o:o + NOPE].astype(MXU_DTYPE)
            dkr = dkr + _rope_mix_bwd(dk_ref[:, o + NOPE:o + HEAD_PAD], cos_v, sin_v)
        dkv_ref[:, HEADS * NOPE:] = dv_ref[...]
        dcqn = lax.dot_general(dqr_ref[...], wq_ref[...], _DIMS["nt"], preferred_element_type=F32)
        dckvn = lax.dot_general(dkv_ref[...], wkv_ref[...], _DIMS["nt"], preferred_element_type=F32)

        zl = z_ref[...]

        def rms_bwd(c, dn, g_ref, gg_ref):
            r = lax.rsqrt(jnp.mean(c * c, axis=-1, keepdims=True) + EPS)
            ch = c * r
            gg_ref[...] += jnp.sum(dn * ch, axis=0, keepdims=True)
            dch = dn * g_ref[...]
            return r * (dch - ch * jnp.mean(dch * ch, axis=-1, keepdims=True))

        dz_ref[:, :Q_RANK] = rms_bwd(zl[:, :Q_RANK], dcqn, gq_ref, ggq_ref).astype(dz_ref.dtype)
        dz_ref[:, Q_RANK:Q_RANK + KV_RANK] = rms_bwd(zl[:, Q_RANK:Q_RANK + KV_RANK], dckvn, gkv_ref, ggkv_ref).astype(dz_ref.dtype)
        dz_ref[:, Q_RANK + KV_RANK:] = dkr.astype(dz_ref.dtype)

    def row(w):
        return pl.BlockSpec((tr, w), lambda i: (i, 0))

    def full(a):
        return pl.BlockSpec(a.shape, lambda i: (0, 0))

    lat = pl.BlockSpec((tr, LAT), lambda i: (i, lat_blk))
    return pl.pallas_call(
        body, name="lat_bwd",
        out_shape=(jax.ShapeDtypeStruct(dz.shape, dz.dtype), jax.ShapeDtypeStruct((T, HEADS * HEAD_PAD), MXU_DTYPE),
                   jax.ShapeDtypeStruct((T, 2 * HEADS * NOPE), MXU_DTYPE), jax.ShapeDtypeStruct(gq.shape, F32),
                   jax.ShapeDtypeStruct(gkv.shape, F32)),
        grid=(T // tr,),
        in_specs=[pl.BlockSpec(memory_space=pl.ANY), lat, row(HEADS * HEAD_PAD), row(HEADS * HEAD_PAD), row(HEADS * NOPE),
                  full(gq), full(gkv), full(wq), full(wkv), row(128), row(128)],
        out_specs=(lat, row(HEADS * HEAD_PAD), row(2 * HEADS * NOPE), full(gq), full(gkv)),
        input_output_aliases={0: 0},
        compiler_params=_params(("arbitrary",), 8 * _nbytes((tr, HEADS * HEAD_PAD), F32)),
    )(dz, z, dq, dk, dv, gq, gkv, wq, wkv, cos_a, sin_a)


GATE_ROWS = 64
HALO = 8


def _taps(ref, half, r, first):
    C = GATE_ROWS
    if first:
        xs = jnp.concatenate([jnp.zeros((HALO, ref.shape[-1]), F32), ref[half, 0:C, :]], axis=0)
    else:
        xs = ref[half, pl.ds(pl.multiple_of(r * C - HALO, HALO), C + HALO), :]
    return xs[HALO:, :], pltpu.roll(xs, 1, 0)[HALO:, :], pltpu.roll(xs, 2, 0)[HALO:, :]


def _conv_taps(taps, cw, cb):
    x0, x1, x2 = taps
    return cb + cw[0:1, :] * x2 + cw[1:2, :] * x1 + cw[2:3, :] * x0


def _fold8(x):
    acc = x[0:8, :]
    for i in range(1, x.shape[0] // 8):
        acc = acc + x[8 * i:8 * (i + 1), :]
    return acc


def _gate_fwd(up3, conv_w, conv_b, B, S):
    T = B * S
    W = FF_TILE
    C = GATE_ROWS

    def body(up_ref, cw_ref, cb_ref, act_ref):
        def chunk(r, first):
            gate = _conv_taps(_taps(up_ref, 0, r, first), cw_ref[0], cb_ref[0])
            val = _conv_taps(_taps(up_ref, 1, r, first), cw_ref[1], cb_ref[1])
            base = 0 if first else pl.multiple_of(r * C, C)
            act_ref[pl.ds(base, C), :] = (gate * _sigmoid(gate) * val).astype(act_ref.dtype)

        chunk(0, True)

        @pl.loop(1, S // C)
        def _(r):
            chunk(r, False)

    return pl.pallas_call(
        body, name="gate_fwd", out_shape=jax.ShapeDtypeStruct((T, D_FF), MXU_DTYPE), grid=(B, N_FF_TILES),
        in_specs=[pl.BlockSpec((2, S, W), lambda b, j: (0, b, j)), pl.BlockSpec((2, 3, W), lambda b, j: (0, 0, j)),
                  pl.BlockSpec((2, 1, W), lambda b, j: (0, 0, j))],
        out_specs=pl.BlockSpec((S, W), lambda b, j: (b, j)),
        compiler_params=_params(("parallel", "parallel"), 6 * _nbytes((S, W), F32)),
    )(up3, conv_w, conv_b)


def _gate_bwd(up3, dact, conv_w, conv_b, B, S):
    T = B * S
    W = FF_TILE
    C = GATE_ROWS

    def body(up_ref, da_ref, cw_ref, cb_ref, dup_ref, gcw_ref, gcb_ref, d_s):
        @pl.when(pl.program_id(1) == 0)
        def _():
            gcw_ref[...] = jnp.zeros_like(gcw_ref)
            gcb_ref[...] = jnp.zeros_like(gcb_ref)

        def chunk(r, first, sums):
            rows = pl.ds(0 if first else pl.multiple_of(r * C, C), C)
            taps = [_taps(up_ref, half, r, first) for half in (0, 1)]
            gate = _conv_taps(taps[0], cw_ref[0], cb_ref[0])
            val = _conv_taps(taps[1], cw_ref[1], cb_ref[1])
            sg = _sigmoid(gate)
            da = da_ref[rows, :]
            d_halves = (da * val * (sg * (1.0 + gate * (1.0 - sg))), da * (gate * sg))
            out = []
            for half, dup in enumerate(d_halves):
                d_s[half, rows, :] = dup
                x0, x1, x2 = taps[half]
                sb, s0, s1, s2 = sums[half]
                out.append((sb + _fold8(dup), s0 + _fold8(dup * x2), s1 + _fold8(dup * x1), s2 + _fold8(dup * x0)))
            return tuple(out)

        zeros = tuple(tuple(jnp.zeros((8, W), F32) for _ in range(4)) for _ in range(2))
        sums = chunk(0, True, zeros)
        sums = lax.fori_loop(1, S // C, lambda r, s: chunk(r, False, s), sums)
        for half in (0, 1):
            sb, s0, s1, s2 = sums[half]
            gcb_ref[half] += jnp.sum(sb, axis=0, keepdims=True)
            gcw_ref[half, 0:1, :] += jnp.sum(s0, axis=0, keepdims=True)
            gcw_ref[half, 1:2, :] += jnp.sum(s1, axis=0, keepdims=True)
            gcw_ref[half, 2:3, :] += jnp.sum(s2, axis=0, keepdims=True)

        d_s[:, S:S + HALO, :] = jnp.zeros((2, HALO, W), F32)

        @pl.loop(0, S // C)
        def _(r):
            base = pl.multiple_of(r * C, C)
            for half in (0, 1):
                ds_ = d_s[half, pl.ds(base, C + HALO), :]
                cw = cw_ref[half]
                dx = (cw[2:3, :] * ds_[:C, :] + cw[1:2, :] * pltpu.roll(ds_, C + HALO - 1, 0)[:C, :]
                      + cw[0:1, :] * pltpu.roll(ds_, C + HALO - 2, 0)[:C, :])
                dup_ref[half, pl.ds(base, C), :] = dx.astype(dup_ref.dtype)

    up_spec = pl.BlockSpec((2, S, W), lambda j, b: (0, b, j))
    cw_spec = pl.BlockSpec((2, 3, W), lambda j, b: (0, 0, j))
    cb_spec = pl.BlockSpec((2, 1, W), lambda j, b: (0, 0, j))
    return pl.pallas_call(
        body, name="gate_bwd",
        out_shape=(jax.ShapeDtypeStruct((2, T, D_FF), MXU_DTYPE), jax.ShapeDtypeStruct((2, 3, D_FF), F32),
                   jax.ShapeDtypeStruct((2, 1, D_FF), F32)),
        grid=(N_FF_TILES, B),
        in_specs=[up_spec, pl.BlockSpec((S, W), lambda j, b: (b, j)), cw_spec, cb_spec],
        out_specs=(up_spec, cw_spec, cb_spec),
        scratch_shapes=[pltpu.VMEM((2, S + HALO, W), F32)],
        compiler_params=_params(("parallel", "arbitrary"), 10 * _nbytes((S, W), F32)),
    )(up3, dact, conv_w, conv_b)


def _final(x2, tgt, g, tr=512):
    T, D = x2.shape

    def body(x_ref, t_ref, g_ref, dx_ref, loss_ref, gg_ref):
        @pl.when(pl.program_id(0) == 0)
        def _():
            loss_ref[...] = jnp.zeros_like(loss_ref)
            gg_ref[...] = jnp.zeros_like(gg_ref)

        xv = x_ref[...]
        gv = g_ref[...]
        r = lax.rsqrt(jnp.mean(xv * xv, axis=-1, keepdims=True) + EPS)
        xn = xv * r
        err = xn * gv - t_ref[...]
        loss_ref[...] += 0.5 * jnp.sum(jnp.mean(err * err, axis=-1, keepdims=True), axis=0, keepdims=True)
        dy = err * (1.0 / D)
        gg_ref[...] += jnp.sum(dy * xn, axis=0, keepdims=True)
        dxn = dy * gv
        dx_ref[...] = r * (dxn - xn * jnp.mean(dxn * xn, axis=-1, keepdims=True))

    row = pl.BlockSpec((tr, D), lambda i: (i, 0))
    vec = pl.BlockSpec((1, D), lambda i: (0, 0))
    return pl.pallas_call(
        body, name="final_loss",
        out_shape=(jax.ShapeDtypeStruct((T, D), F32), jax.ShapeDtypeStruct((1, 128), F32), jax.ShapeDtypeStruct((1, D), F32)),
        grid=(T // tr,), in_specs=[row, row, vec],
        out_specs=(row, pl.BlockSpec((1, 128), lambda i: (0, 0)), vec),
        compiler_params=_params(("arbitrary",), 6 * _nbytes((tr, D), F32)),
    )(x2, tgt, g)


def _sum_slabs(parts, name, tr):
    rows, cols = parts[0].shape
    n = len(parts)

    def body(*refs):
        acc = refs[0][...]
        for r in refs[1:n]:
            acc = acc + r[...]
        refs[n][...] = acc

    blk = pl.BlockSpec((tr, cols), lambda i: (i, 0))
    return pl.pallas_call(
        body, name=name, out_shape=jax.ShapeDtypeStruct((rows, cols), F32), grid=(rows // tr,),
        in_specs=[blk] * n, out_specs=blk,
        compiler_params=_params(("parallel",), (n + 1) * _nbytes((tr, cols), F32)),
    )(*parts)


ADAMW_BLOCK_BYTES = 2400 * 1024


def _adamw(w, g, m, v, name):
    lead = w.ndim == 3
    rows, cols = w.shape[-2:]
    fits = [d for d in range(8, rows + 1, 8) if rows % d == 0 and d * cols * 4 <= ADAMW_BLOCK_BYTES]
    tr = max(fits) if fits else rows
    c1 = 1.0 - ADAM_B1 ** ADAM_STEP
    c2 = 1.0 - ADAM_B2 ** ADAM_STEP

    def body(w_ref, g_ref, m_ref, v_ref, d_ref, nm_ref, nv_ref):
        gv = g_ref[...]
        nm = ADAM_B1 * m_ref[...] + (1.0 - ADAM_B1) * gv
        nv = ADAM_B2 * v_ref[...] + (1.0 - ADAM_B2) * (gv * gv)
        nm_ref[...] = nm
        nv_ref[...] = nv
        d_ref[...] = -ADAM_LR * ((nm / c1) / (jnp.sqrt(nv / c2) + ADAM_EPS) + ADAM_WD * w_ref[...])

    blk = pl.BlockSpec((None, tr, cols), lambda i: (0, i, 0)) if lead else pl.BlockSpec((tr, cols), lambda i: (i, 0))
    sds = jax.ShapeDtypeStruct(w.shape, F32)
    return pl.pallas_call(
        body, name=name, out_shape=(sds, sds, sds), grid=(rows // tr,), in_specs=[blk] * 4, out_specs=(blk, blk, blk),
        compiler_params=_params(("parallel",), 7 * _nbytes((tr, cols), F32)),
    )(w, g, m, v)


_ANY = pl.BlockSpec(memory_space=pl.ANY)


def _place():
    x, y, c = lax.axis_index("x"), lax.axis_index("y"), lax.axis_index("c")
    chips = [(1 - x, y), (x, 1 - y), (1 - x, 1 - y)]
    return x, y, c, chips


def _gather_weights(shards):
    n = len(shards)

    def body(*refs):
        ins, outs = refs[:n], refs[n:2 * n]
        send, recv, fsend, frecv, osend, orecv = refs[2 * n:]
        x, y, c, chips = _place()
        me = 2 * x + y
        first, passed = [], []
        for w in range(n):
            first.append(pltpu.make_async_remote_copy(
                src_ref=ins[w], dst_ref=outs[w].at[me], send_sem=osend.at[w], recv_sem=orecv.at[w],
                device_id=(x, y, 1 - c), device_id_type=MESH))
        for w in range(n):
            for j, (px, py) in enumerate(chips):
                first.append(pltpu.make_async_remote_copy(
                    src_ref=ins[w].at[c], dst_ref=outs[w].at[me, c], send_sem=send.at[3 * w + j],
                    recv_sem=recv.at[3 * w + j], device_id=(px, py, c), device_id_type=MESH))
        for cp in first:
            cp.start()
        for w in range(n):
            for j, (px, py) in enumerate(chips):
                landed = outs[w].at[2 * px + py, c]
                pltpu.make_async_remote_copy(src_ref=landed, dst_ref=landed, send_sem=send.at[3 * w + j],
                                             recv_sem=recv.at[3 * w + j], device_id=(px, py, c),
                                             device_id_type=MESH).wait_recv()
                fw = pltpu.make_async_remote_copy(src_ref=landed, dst_ref=landed, send_sem=fsend.at[3 * w + j],
                                                  recv_sem=frecv.at[3 * w + j], device_id=(x, y, 1 - c),
                                                  device_id_type=MESH)
                fw.start()
                passed.append(fw)
        for w in range(n):
            for j, (px, py) in enumerate(chips):
                other = outs[w].at[2 * px + py, 1 - c]
                pltpu.make_async_remote_copy(src_ref=other, dst_ref=other, send_sem=fsend.at[3 * w + j],
                                             recv_sem=frecv.at[3 * w + j], device_id=(x, y, 1 - c),
                                             device_id_type=MESH).wait_recv()
        for w in range(n):
            own = outs[w].at[me]
            pltpu.make_async_remote_copy(src_ref=own, dst_ref=own, send_sem=osend.at[w], recv_sem=orecv.at[w],
                                         device_id=(x, y, 1 - c), device_id_type=MESH).wait_recv()
        for cp in first + passed:
            cp.wait_send()

    dma = lambda k: pltpu.SemaphoreType.DMA((k,))
    return pl.pallas_call(
        body, name="gather_weights",
        out_shape=tuple(jax.ShapeDtypeStruct((N_CHIPS,) + s.shape, s.dtype) for s in shards),
        in_specs=[_ANY] * n, out_specs=tuple([_ANY] * n),
        scratch_shapes=[dma(3 * n), dma(3 * n), dma(3 * n), dma(3 * n), dma(n), dma(n)],
    )(*shards)


_HBM = pl.BlockSpec(memory_space=pltpu.HBM)
_SEM = pl.BlockSpec(memory_space=pltpu.SEMAPHORE)
_EFFECT = pltpu.SideEffectType.DATAFLOW_SIDE_EFFECTING


SEMS_PER_ARRAY = 8


def _exchange_copies(srcs, lands, send, recv, mode):
    x, y, c, chips = _place()
    if mode == "all":
        flips = [(fx, fy, fc) for fx in (0, 1) for fy in (0, 1) for fc in (0, 1)][1:]
        peers = [(x ^ fx, y ^ fy, c ^ fc) for fx, fy, fc in flips]
        slot = 4 * x + 2 * y + c
    else:
        peers = [(px, py, c) for px, py in chips] + ([(x, y, 1 - c)] if mode == "gather" else [])
        slot = 2 * x + y
    cps = []
    for w, (src, land) in enumerate(zip(srcs, lands)):
        for k, peer in enumerate(peers):
            piece = src.at[2 * peer[0] + peer[1]] if mode == "scatter" else src
            cps.append(pltpu.make_async_remote_copy(
                src_ref=piece, dst_ref=land.at[slot], send_sem=send.at[SEMS_PER_ARRAY * w + k],
                recv_sem=recv.at[SEMS_PER_ARRAY * w + k], device_id=peer, device_id_type=MESH))
    return cps


def _exchange_start(srcs, name, mode, after):
    n = len(srcs)
    lead = {"gather": (N_CHIPS,), "scatter": (), "all": (2 * N_CHIPS,)}[mode]
    land_shapes = [lead + s.shape for s in srcs]

    def body(*refs):
        src_refs, land_refs = refs[:n], refs[n:2 * n]
        send, recv = refs[2 * n + 1], refs[2 * n + 2]
        token = refs[-1]
        for cp in _exchange_copies(src_refs, land_refs, send, recv, mode):
            cp.start()
        token[...] = jnp.zeros_like(token)

    sems = pltpu.SemaphoreType.DMA((SEMS_PER_ARRAY * n,))
    out = pl.pallas_call(
        body, name=name,
        out_shape=(sems, sems, *[pltpu.HBM(s.shape, s.dtype) for s in srcs],
                   *[pltpu.HBM(shp, s.dtype) for shp, s in zip(land_shapes, srcs)], jax.ShapeDtypeStruct((8, 128), F32)),
        in_specs=[_HBM] * (2 * n) + [_ANY],
        out_specs=(_SEM, _SEM, *[_HBM] * (2 * n), pl.BlockSpec(memory_space=pltpu.VMEM)),
        input_output_aliases={i: 2 + i for i in range(2 * n)},
        compiler_params=pltpu.CompilerParams(has_side_effects=_EFFECT),
    )(*[pltpu.with_memory_space_constraint(s, pltpu.HBM) for s in srcs],
      *[pltpu.with_memory_space_constraint(lax.empty(shp, s.dtype), pltpu.HBM) for shp, s in zip(land_shapes, srcs)],
      after)
    return out[0], out[1], out[2:2 + n], out[2 + n:2 + 2 * n], out[-1]


def _exchange_wait(started, name, mode, after):
    send, recv, src_thru, land_thru, _ = started
    n = len(src_thru)

    def body(*refs):
        src_refs, land_refs, send_ref, recv_ref = refs[:n], refs[n:2 * n], refs[2 * n], refs[2 * n + 1]
        for cp in _exchange_copies(src_refs, land_refs, send_ref, recv_ref, mode):
            cp.wait_send()
            cp.wait_recv()

    out = pl.pallas_call(
        body, name=name,
        out_shape=tuple(pltpu.HBM(a.shape, a.dtype) for a in list(src_thru) + list(land_thru)),
        in_specs=[_HBM] * (2 * n) + [_SEM, _SEM, _ANY], out_specs=tuple([_HBM] * (2 * n)),
        input_output_aliases={i: i for i in range(2 * n)},
        compiler_params=pltpu.CompilerParams(has_side_effects=_EFFECT),
    )(*src_thru, *land_thru, send, recv, after)
    return out[:n], out[n:]


def _swap_halves(gs, name):
    n = len(gs)

    def body(*refs):
        ins, outs, send, recv = refs[:n], refs[n:2 * n], refs[2 * n], refs[2 * n + 1]
        x, y, c, _ = _place()
        cps = []
        for w in range(n):
            cps.append(pltpu.make_async_remote_copy(
                src_ref=ins[w].at[:, 1 - c], dst_ref=outs[w], send_sem=send.at[w], recv_sem=recv.at[w],
                device_id=(x, y, 1 - c), device_id_type=MESH))
        for cp in cps:
            cp.start()
        for cp in cps:
            cp.wait()

    return pl.pallas_call(
        body, name=name,
        out_shape=tuple(jax.ShapeDtypeStruct((g.shape[0],) + g.shape[2:], g.dtype) for g in gs),
        in_specs=[_ANY] * n, out_specs=tuple([_ANY] * n),
        scratch_shapes=[pltpu.SemaphoreType.DMA((n,)), pltpu.SemaphoreType.DMA((n,))],
    )(*gs)


GRAD_PAYLOAD = jnp.bfloat16


def _half_blocks(half_rows, cols):
    if (half_rows // 2) % 16 == 0:
        return (half_rows // 2, cols), (lambda r: (r, 0))
    assert cols % 256 == 0, (half_rows, cols)
    return (half_rows, cols // 2), (lambda r: (0, r))


def _pair_sum(gs, gots, name):
    n = len(gs)
    core = lax.axis_index("c").astype(jnp.int32).reshape(1)

    def body(core_ref, *refs):
        del core_ref
        for w in range(n):
            refs[2 * n + w][...] = (refs[w][...] + refs[n + w][...]).astype(GRAD_PAYLOAD)

    in_specs, out_specs, out_shape, nbytes = [], [], [], 0
    cuts = [_half_blocks(g.shape[1] // 2, g.shape[2]) for g in gs]
    for g, ((br, bc), at) in zip(gs, cuts):
        per_half = (g.shape[1] // 2) // br
        in_specs.append(pl.BlockSpec((1, br, bc), lambda s, r, core, at=at, per_half=per_half:
                                     (s, per_half * core[0] + at(r)[0], at(r)[1])))
        nbytes += 3 * _nbytes((br, bc), F32)
    for g, ((br, bc), at) in zip(gs, cuts):
        in_specs.append(pl.BlockSpec((1, br, bc), lambda s, r, core, at=at: (s,) + at(r)))
        out_specs.append(pl.BlockSpec((1, br, bc), lambda s, r, core, at=at: (s,) + at(r)))
        out_shape.append(jax.ShapeDtypeStruct((g.shape[0], g.shape[1] // 2, g.shape[2]), GRAD_PAYLOAD))
    return pl.pallas_call(
        body, name=name, out_shape=tuple(out_shape),
        grid_spec=pltpu.PrefetchScalarGridSpec(num_scalar_prefetch=1, grid=(N_CHIPS, 2), in_specs=in_specs,
                                               out_specs=tuple(out_specs)),
        compiler_params=_params(("parallel", "parallel"), nbytes),
    )(core, *gs, *gots)


def _chip_sum(ps, landed):
    n = len(ps)
    x, y, c = lax.axis_index("x"), lax.axis_index("y"), lax.axis_index("c")
    where = jnp.stack([2 * x + y, 2 * (1 - x) + y, 2 * x + (1 - y), 2 * (1 - x) + (1 - y), c]).astype(jnp.int32)

    def body(where_ref, *refs):
        del where_ref
        for w in range(n):
            terms = [refs[4 * w + t][...].astype(F32) for t in range(4)]
            refs[4 * n + w][...] = ((terms[0] + terms[1]) + terms[2]) + terms[3]

    in_specs, out_specs, out_shape, args, nbytes = [], [], [], [], 0
    for p, a in zip(ps, landed):
        (br, bc), at = _half_blocks(a.shape[1], a.shape[2])
        blk = (1, br, bc)
        in_specs.append(pl.BlockSpec(blk, lambda r, where, at=at: (where[0],) + at(r)))
        args.append(p)
        for t in (1, 2, 3):
            in_specs.append(pl.BlockSpec(blk, lambda r, where, t=t, at=at: (where[t],) + at(r)))
            args.append(a)
        out_specs.append(pl.BlockSpec(blk, lambda r, where, at=at: (where[4],) + at(r)))
        out_shape.append(jax.ShapeDtypeStruct((2,) + a.shape[1:], F32))
        nbytes += 4 * _nbytes(blk, F32)
    return pl.pallas_call(
        body, name="grad_chip_sum", out_shape=tuple(out_shape),
        grid_spec=pltpu.PrefetchScalarGridSpec(num_scalar_prefetch=1, grid=(2,), in_specs=in_specs,
                                               out_specs=tuple(out_specs)),
        compiler_params=_params(("parallel",), nbytes),
    )(where, *args)


def _join_halves(ss):
    n = len(ss)

    def body(*refs):
        outs, send, recv = refs[n:2 * n], refs[2 * n], refs[2 * n + 1]
        x, y, c, _ = _place()
        cps = []
        for w in range(n):
            cps.append(pltpu.make_async_remote_copy(
                src_ref=outs[w].at[c], dst_ref=outs[w].at[c], send_sem=send.at[w], recv_sem=recv.at[w],
                device_id=(x, y, 1 - c), device_id_type=MESH))
        for cp in cps:
            cp.start()
        for w in range(n):
            got = outs[w].at[1 - c]
            pltpu.make_async_remote_copy(src_ref=got, dst_ref=got, send_sem=send.at[w], recv_sem=recv.at[w],
                                         device_id=(x, y, 1 - c), device_id_type=MESH).wait_recv()
        for cp in cps:
            cp.wait_send()

    dma = lambda k: pltpu.SemaphoreType.DMA((k,))
    return pl.pallas_call(
        body, name="grad_join_halves",
        out_shape=tuple(jax.ShapeDtypeStruct(s.shape, s.dtype) for s in ss),
        in_specs=[_ANY] * n, out_specs=tuple([_ANY] * n), input_output_aliases={w: w for w in range(n)},
        scratch_shapes=[dma(n), dma(n)],
    )(*ss)


def _rot_cols(w, axis=-1):
    a, b = jnp.split(w, 2, axis=axis)
    return jnp.concatenate([-b, a], axis=axis)


def _rot_cols_t(g, axis=-1):
    a, b = jnp.split(g, 2, axis=axis)
    return jnp.concatenate([b, -a], axis=axis)


def _cols_from_chips(a):
    n, r, cs = a.shape
    return jnp.transpose(a, (1, 0, 2)).reshape(r, n * cs)


def _cols_to_chips(a):
    r, cc = a.shape
    return jnp.transpose(a.reshape(r, N_CHIPS, cc // N_CHIPS), (1, 0, 2))


def _conv_w_split(cw):
    return jnp.swapaxes(cw.reshape(3, 2, D_FF), 0, 1)


def _conv_w_join(g):
    return jnp.swapaxes(g, 0, 1).reshape(3, 2 * D_FF)


_SEG =(D_MODEL, 2 * D_MODEL, 2 * D_MODEL + Q_RANK, 2 * D_MODEL + Q_RANK + KV_RANK, 2 * D_MODEL + Q_RANK + KV_RANK + ROPE,
        3 * D_MODEL + Q_RANK + KV_RANK + ROPE)


def _w_in_t_to_pad(wt):
    u, v, cq, ckv, kr, ga, gb = jnp.split(wt, _SEG, axis=0)
    return jnp.concatenate([u, v, ga, gb, cq, ckv, kr, _rot_cols(kr, axis=0)], axis=0)


def _w_in_t_from_pad(gt):
    u, v, ga, gb, cq, ckv, kr, krr = jnp.split(
        gt, (D_MODEL, 2 * D_MODEL, 3 * D_MODEL, 4 * D_MODEL, 4 * D_MODEL + Q_RANK, 4 * D_MODEL + Q_RANK + KV_RANK,
             4 * D_MODEL + Q_RANK + KV_RANK + ROPE), axis=0)
    return jnp.concatenate([u, v, cq, ckv, kr + _rot_cols_t(krr, axis=0), ga, gb], axis=0)


def _w_uq_to_pad(w):
    t = w.reshape(Q_RANK, HEADS, QK_DIM)
    nope, rope = t[..., :NOPE], t[..., NOPE:]
    return jnp.concatenate([nope, rope, _rot_cols(rope)], axis=-1).reshape(Q_RANK, HEADS * HEAD_PAD)


def _w_uq_from_pad(g):
    t = g.reshape(Q_RANK, HEADS, HEAD_PAD)
    nope, rope, rot = t[..., :NOPE], t[..., NOPE:QK_DIM], t[..., QK_DIM:]
    return jnp.concatenate([nope, rope + _rot_cols_t(rot)], axis=-1).reshape(Q_RANK, HEADS * QK_DIM)


def _w_ukv_to_pad(w):
    t = w.reshape(KV_RANK, HEADS, 2, NOPE)
    return jnp.swapaxes(t, 1, 2).reshape(KV_RANK, 2 * HEADS * NOPE)


def _w_ukv_from_pad(g):
    t = g.reshape(KV_RANK, 2, HEADS, NOPE)
    return jnp.swapaxes(t, 1, 2).reshape(KV_RANK, 2 * HEADS * NOPE)


def _rope_tables(positions):
    inv_freq = 1.0 / (ROPE_THETA ** (jnp.arange(0, ROPE, 2, dtype=F32) / ROPE))
    ang = positions.astype(F32).reshape(-1, 1) * inv_freq
    cos, sin = jnp.cos(ang), jnp.sin(ang)
    zero = jnp.zeros((ang.shape[0], 64), F32)
    return jnp.concatenate([cos, cos, zero], axis=1), jnp.concatenate([sin, sin, zero], axis=1)


_BIG = ("w_in", "w_uq", "w_ukv", "w_out", "w_up", "w_down")
UP_SHARD = 2 * D_FF // N_CHIPS


def _local_step(x, positions, tgt, wts, mixer_weights, ffn_weights, on_ffn_grads, on_mixer_grads):
    B, S, D = x.shape
    T = B * S
    xf = x.reshape(T, D)
    cos_a, sin_a = _rope_tables(positions)
    bs_t = jnp.pad(wts["a_spatial_b"].T, ((0, 0), (0, 128 - A_GROUPS)))

    h = _rms_fwd(xf, wts["mix_norm"], "norm1_fwd")
    z = _mm(h, wts["w_in"], "nt", "in_proj", tm=512, tn=1536, tk=D, n_outer=True)
    wts = dict(wts)
    wts["w_q"], wts["w_kv"], wts["w_out"] = mixer_weights(z)
    q, k, v, cqn, ckvn = _lat_fwd(z, wts["q_a_norm"], wts["kv_a_norm"], wts["w_q"], wts["w_kv"], cos_a, sin_a)
    yb, *lses = _attn_fwd(q, k, v, B, S)
    merged = _mix_fwd(z, yb, wts["a_v_norm_g"], wts["a_v_norm_b"], wts["a_spatial_w"], bs_t)
    x1 = _mm(merged, wts["w_out"], "nn", "out_proj", tm=512, tn=D, tk=D, add=xf)
    h2 = _rms_fwd(x1, wts["ffn_norm"], "norm2_fwd")
    wts["w_up"], wts["w_down"], wts["conv_w"] = ffn_weights(h2)
    up_pre = _mm(h2, wts["w_up"], "nn", "up_proj", tm=512, tn=UP_SHARD, tk=D, dims=(T, 2 * D_FF, D),
                 b_spec=pl.BlockSpec((None, D, UP_SHARD), lambda i, j, k: (j, 0, 0)),
                 o_spec=pl.BlockSpec((None, 512, UP_SHARD), lambda i, j, k: (j // 2, i, j % 2)), out_shape=(2, T, D_FF),
                 n_outer=True)
    act = _gate_fwd(up_pre, wts["conv_w"], wts["conv_b"], B, S)
    x2 = _mm(act, wts["w_down"], "nn", "down_proj", tm=512, tn=D, tk=1408, add=x1)
    dx2, loss_row, g_final = _final(x2, tgt.reshape(T, D), wts["final_norm"])

    g = {"final_norm": g_final}
    dact = _mm(dx2, wts["w_down"], "nt", "down_proj_dx", tm=512, tn=1408, tk=D, n_outer=True)
    tk2, tk1 = min(2048, T), min(1024, T)
    g["w_down"], g["w_down_lo"] = _mm(act, dx2, "tn", "down_proj_dw", tm=1408, tn=D, tk=tk1, copy_dtype=GRAD_PAYLOAD)
    dup, g["conv_w"], g["conv_b"] = _gate_bwd(up_pre, dact, wts["conv_w"], wts["conv_b"], B, S)
    g["w_up"], g["w_up_lo"] = _mm(
        h2, dup, "tn", "up_proj_dw", tm=D, tn=UP_SHARD, tk=tk2, dims=(D, 2 * D_FF, T), copy_dtype=GRAD_PAYLOAD,
        b_spec=pl.BlockSpec((None, tk2, UP_SHARD), lambda i, j, k: (j // 2, k, j % 2)),
        o_spec=pl.BlockSpec((None, D, UP_SHARD), lambda i, j, k: (j, 0, 0)), out_shape=(N_CHIPS, D, UP_SHARD))
    ffn_sent = on_ffn_grads(g)
    dh2 = _mm(dup, wts["w_up"], "nt", "up_proj_dx", tm=512, tn=D, tk=UP_SHARD, dims=(T, D, 2 * D_FF),
              a_spec=pl.BlockSpec((None, 512, UP_SHARD), lambda i, j, k: (k // 2, i, k % 2)),
              b_spec=pl.BlockSpec((None, D, UP_SHARD), lambda i, j, k: (k, 0, 0)))
    token = None if ffn_sent is None else ffn_sent(dh2)
    ffn_norm = wts["ffn_norm"] if token is None else wts["ffn_norm"] + token[0:1, 0:1]
    dx1, g["ffn_norm"] = _rms_bwd(x1, ffn_norm, dh2, dx2, "norm2_bwd")
    dm = _mm(dx1, wts["w_out"], "nt", "out_proj_dx", tm=512, tn=D, tk=D)
    g["w_out"], g["w_out_lo"] = _mm(merged, dx1, "tn", "out_proj_dw", tm=D, tn=D, tk=tk1, copy_dtype=GRAD_PAYLOAD)
    dz, dyb, dl, g["a_spatial_w"], gbs, g["a_v_norm_g"], g["a_v_norm_b"] = _mix_bwd(
        z, yb, dm, wts["a_v_norm_g"], wts["a_v_norm_b"], wts["a_spatial_w"], bs_t)
    g["a_spatial_b"] = gbs[:, :A_GROUPS].T
    delta = dl.reshape(HEADS * T // ATT_BLOCK, 1, ATT_BLOCK)
    dq, dk, dv = _attn_bwd(q, k, v, dyb, lses, delta, B, S)
    dz, dq_raw, dkv, g["q_a_norm"], g["kv_a_norm"] = _lat_bwd(
        dz, z, dq, dk, dv, wts["q_a_norm"], wts["kv_a_norm"], wts["w_q"], wts["w_kv"], cos_a, sin_a)
    g["w_q"], g["w_q_lo"] = _mm(cqn, dq_raw, "tn", "q_proj_dw", tm=Q_RANK, tn=HEADS * HEAD_PAD, tk=tk2,
                                copy_dtype=GRAD_PAYLOAD)
    g["w_kv"], g["w_kv_lo"] = _mm(ckvn, dkv, "tn", "kv_proj_dw", tm=KV_RANK, tn=2 * HEADS * NOPE, tk=tk2,
                                  copy_dtype=GRAD_PAYLOAD)
    g["w_in"], g["w_in_lo"] = _mm(dz, h, "tn", "in_proj_dw", tm=1536, tn=D, tk=tk2,
                                  copy_dtype=GRAD_PAYLOAD)
    token = on_mixer_grads(g)
    mix_norm = wts["mix_norm"] if token is None else wts["mix_norm"] + token[0:1, 0:1]
    dh = _mm(dz, wts["w_in"], "nn", "in_proj_dx", tm=512, tn=D, tk=1536)
    dx, g["mix_norm"] = _rms_bwd(xf, mix_norm, dh, dx1, "norm1_bwd")
    return loss_row[0, 0], dx.reshape(B, S, D), g


_SMALL = (("mix_norm", (1, D_MODEL)), ("a_v_norm_g", (1, D_MODEL)), ("a_v_norm_b", (1, D_MODEL)),
          ("a_spatial_w", (A_GROUPS * CHUNK, CHUNK)), ("a_spatial_b", (1, A_GROUPS * CHUNK)), ("q_a_norm", (1, Q_RANK)),
          ("kv_a_norm", (1, KV_RANK)), ("ffn_norm", (1, D_MODEL)), ("conv_b", (1, 2 * D_FF)), ("final_norm", (1, D_MODEL)),
          ("conv_w", (3, 2 * D_FF)))
_SMALL_SIZE = sum(math.prod(s) for _, s in _SMALL)
_SMALL_ROWS = -(-(_SMALL_SIZE + 1) // (128 * 8)) * 8


def kernel(x, positions, mix_norm, w_in, a_v_norm_g, a_v_norm_b, a_spatial_w, a_spatial_b, q_a_norm, w_uq, kv_a_norm, w_ukv, w_out, ffn_norm, w_up, conv_w, conv_b, w_down, final_norm, loss_target, m_mix_norm, m_w_in, m_a_v_norm_g, m_a_v_norm_b, m_a_spatial_w, m_a_spatial_b, m_q_a_norm, m_w_uq, m_kv_a_norm, m_w_ukv, m_w_out, m_ffn_norm, m_w_up, m_conv_w, m_conv_b, m_w_down, m_final_norm, v_mix_norm, v_w_in, v_a_v_norm_g, v_a_v_norm_b, v_a_spatial_w, v_a_spatial_b, v_q_a_norm, v_w_uq, v_kv_a_norm, v_w_ukv, v_w_out, v_ffn_norm, v_w_up, v_conv_w, v_conv_b, v_w_down, v_final_norm):
    weights = dict(mix_norm=mix_norm, w_in=w_in, a_v_norm_g=a_v_norm_g, a_v_norm_b=a_v_norm_b, a_spatial_w=a_spatial_w,
                   a_spatial_b=a_spatial_b, q_a_norm=q_a_norm, w_uq=w_uq, kv_a_norm=kv_a_norm, w_ukv=w_ukv, w_out=w_out,
                   ffn_norm=ffn_norm, w_up=w_up, conv_w=conv_w, conv_b=conv_b, w_down=w_down, final_norm=final_norm)
    m_in = dict(mix_norm=m_mix_norm, w_in=m_w_in, a_v_norm_g=m_a_v_norm_g, a_v_norm_b=m_a_v_norm_b,
                a_spatial_w=m_a_spatial_w, a_spatial_b=m_a_spatial_b, q_a_norm=m_q_a_norm, w_uq=m_w_uq,
                kv_a_norm=m_kv_a_norm, w_ukv=m_w_ukv, w_out=m_w_out, ffn_norm=m_ffn_norm, w_up=m_w_up, conv_w=m_conv_w,
                conv_b=m_conv_b, w_down=m_w_down, final_norm=m_final_norm)
    v_in = dict(mix_norm=v_mix_norm, w_in=v_w_in, a_v_norm_g=v_a_v_norm_g, a_v_norm_b=v_a_v_norm_b,
                a_spatial_w=v_a_spatial_w, a_spatial_b=v_a_spatial_b, q_a_norm=v_q_a_norm, w_uq=v_w_uq,
                kv_a_norm=v_kv_a_norm, w_ukv=v_w_ukv, w_out=v_w_out, ffn_norm=v_ffn_norm, w_up=v_w_up, conv_w=v_conv_w,
                conv_b=v_conv_b, w_down=v_w_down, final_norm=v_final_norm)
    names = list(weights)
    chip = 2 * lax.axis_index("x") + lax.axis_index("y")

    def halves(a):
        return a.reshape(a.shape[:-2] + (2, a.shape[-2] // 2, a.shape[-1]))

    def whole(a):
        return a.reshape(a.shape[:-3] + (2 * a.shape[-2], a.shape[-1]))

    w_in_t = jnp.swapaxes(w_in[0], 0, 1).astype(MXU_DTYPE)
    (w_in_sh,) = _gather_weights([jnp.stack(jnp.split(w_in_t, 2, axis=1))])
    mixer_gather = _exchange_start([weights[n][0].astype(MXU_DTYPE) for n in _BIG[1:4]],
                                   "mixer_gather_start", "gather", after=w_in_sh)
    ffn_gather = _exchange_start([w_up[0].astype(MXU_DTYPE), w_down[0].astype(MXU_DTYPE), conv_w[0]],
                                 "ffn_gather_start", "gather", after=mixer_gather[4])
    wts = dict(
        mix_norm=mix_norm + ffn_gather[4][0:1, 0:1], a_v_norm_g=a_v_norm_g, a_v_norm_b=a_v_norm_b,
        a_spatial_w=a_spatial_w[0], a_spatial_b=a_spatial_b[0], q_a_norm=q_a_norm, kv_a_norm=kv_a_norm,
        ffn_norm=ffn_norm, final_norm=final_norm.reshape(1, D_MODEL),
        w_in=_w_in_t_to_pad(jnp.concatenate([w_in_sh[:, 0], w_in_sh[:, 1]], axis=-1).reshape(-1, D_MODEL)),
        conv_b=conv_b.reshape(2, 1, D_FF))

    def mixer_weights(after):
        _, (w_uq_sh, w_ukv_sh, w_out_sh) = _exchange_wait(mixer_gather, "mixer_gather_wait", "gather", after)
        return (_w_uq_to_pad(_cols_from_chips(w_uq_sh)), _w_ukv_to_pad(_cols_from_chips(w_ukv_sh)),
                w_out_sh.reshape(D_MODEL, D_MODEL))

    def ffn_weights(after):
        _, (w_up_sh, w_down_sh, cw_all) = _exchange_wait(ffn_gather, "ffn_gather_wait", "gather", after)
        return w_up_sh, w_down_sh.reshape(D_FF, D_MODEL), _conv_w_split(_cols_from_chips(cw_all))

    scatters = {}

    def start_scatter(slabs, slabs_lo, tag):
        got = _swap_halves([halves(s) for s in slabs_lo], tag + "_grad_swap_halves")
        sums = _pair_sum(slabs, got, tag + "_grad_pair_sum")
        scatters[tag] = _exchange_start(list(sums), tag + "_scatter_start", "scatter", after=slabs[-1])
        return scatters[tag][4]

    def on_ffn_grads(g):
        token = start_scatter(*[[g["w_up" + lo], g["w_down" + lo].reshape(N_CHIPS, D_FF // N_CHIPS, D_MODEL)]
                                for lo in ("", "_lo")], "ffn")
        return lambda after: token

    def on_mixer_grads(g):
        return start_scatter(*[
            [_w_in_t_from_pad(g["w_in" + lo]).reshape(N_CHIPS, -1, D_MODEL), _cols_to_chips(_w_uq_from_pad(g["w_q" + lo])),
             _cols_to_chips(_w_ukv_from_pad(g["w_kv" + lo])), g["w_out" + lo].reshape(N_CHIPS, D_MODEL // N_CHIPS, D_MODEL)]
            for lo in ("", "_lo")], "mixer")

    loss_part, grad_x, g = _local_step(x, positions, loss_target, wts, mixer_weights, ffn_weights, on_ffn_grads,
                                       on_mixer_grads)

    g_small_parts = dict(g)
    g_small_parts["conv_w"] = _conv_w_join(g["conv_w"])
    g_small_parts["conv_b"] = g["conv_b"].reshape(1, 2 * D_FF)
    flat = jnp.concatenate([g_small_parts[n].reshape(-1) for n, _ in _SMALL] + [loss_part.reshape(1)])
    flat = jnp.pad(flat, (0, _SMALL_ROWS * 128 - flat.shape[0])).reshape(_SMALL_ROWS, 128)
    small_gather = _exchange_start([flat], "small_gather_start", "all", after=grad_x)

    mixer_sums, mixer_landed = _exchange_wait(scatters["mixer"], "mixer_scatter_wait", "scatter", after=small_gather[4])
    ffn_sums, ffn_landed = _exchange_wait(scatters["ffn"], "ffn_scatter_wait", "scatter", after=mixer_landed[0])
    reduced = _chip_sum(list(mixer_sums) + list(ffn_sums), list(mixer_landed) + list(ffn_landed))
    g_big = dict(zip(_BIG, _join_halves(reduced)))

    grads, deltas, new_m, new_v = {}, {}, {}, {}

    def update(n, grad):
        w = weights[n]
        shape2 = grad.shape
        d, nm, nv = _adamw(w.reshape(shape2), grad, m_in[n].reshape(shape2), v_in[n].reshape(shape2), "adamw_" + n)
        grads[n], deltas[n], new_m[n], new_v[n] = (t.reshape(w.shape) for t in (grad, d, nm, nv))

    def update_transposed(n, grad_t):
        t = lambda a: jnp.swapaxes(a, 1, 2)
        d, nm, nv = _adamw(t(weights[n]), grad_t, t(m_in[n]), t(v_in[n]), "adamw_" + n)
        grads[n], deltas[n], new_m[n], new_v[n] = t(grad_t), t(d), t(nm), t(nv)

    for n in _BIG:
        g3 = g_big[n].reshape((1, -1, g_big[n].shape[-1]))
        if n == "w_in":
            update_transposed(n, g3)
        else:
            update(n, g3)

    (own,), (everyone,) = _exchange_wait(small_gather, "small_gather_wait", "all", after=deltas["w_up"])
    device = 2 * chip + lax.axis_index("c")
    everyone = lax.dynamic_update_slice(everyone, own[None], (device, 0, 0))
    total = _sum_slabs([everyone[j] for j in range(8)], "small_grads_sum", tr=_SMALL_ROWS).reshape(-1)
    o = 0
    for n, shp in _SMALL:
        piece = total[o:o + math.prod(shp)].reshape(shp)
        o += math.prod(shp)
        if n == "conv_w":
            piece = lax.dynamic_slice_in_dim(piece, chip * UP_SHARD, UP_SHARD, axis=1)
        update(n, piece)
    loss = total[_SMALL_SIZE]
    return (loss, grad_x, *[grads[n] for n in names], *[deltas[n] for n in names], *[new_m[n] for n in names],
            *[new_v[n] for n in names])
```

```python
import functools
import math

import jax
import jax.numpy as jnp
from jax import lax
from jax.experimental import pallas as pl
from jax.experimental.pallas import tpu as pltpu

F32 = jnp.float32
MXU_DTYPE = jnp.bfloat16
MESH = pl.DeviceIdType.MESH

D_MODEL = 1024
EPS = 1e-6
A_GROUPS = 8
CHUNK = 128
HEADS = 8
NOPE = 128
ROPE = 64
QK_DIM = NOPE + ROPE
HEAD_PAD = 256
Q_RANK = 256
KV_RANK = 128
ROPE_THETA = 10000.0
D_FF = 2816
FF_TILE = 256
N_FF_TILES = D_FF // FF_TILE
LAT = 512
IN_PAD = 4 * D_MODEL + LAT
N_CHIPS = 4
ADAM_LR, ADAM_B1, ADAM_B2, ADAM_EPS, ADAM_WD, ADAM_STEP = 0.001, 0.9, 0.999, 1e-08, 0.01, 10

VMEM_CAP_V7X = 64 * 1024 * 1024
NEG = -1e30


def _params(sem, nbytes):
    limit = int(min(VMEM_CAP_V7X - (8 << 20), max(32 << 20, 3 * nbytes)))
    return pltpu.CompilerParams(dimension_semantics=sem, vmem_limit_bytes=limit)


def _nbytes(shape, dtype):
    return math.prod(shape) * jnp.dtype(dtype).itemsize


_DIMS = {"nn": (((1,), (0,)), ((), ())), "nt": (((1,), (1,)), ((), ())), "tn": (((0,), (0,)), ((), ()))}


def _mm(a, b, mode, name, *, tm, tn, tk, out_dtype=F32, add=None, dims=None, a_spec=None, b_spec=None,
        o_spec=None, out_shape=None, n_outer=False, copy_dtype=None):
    if dims is None:
        if mode == "nn":
            (M, K), (_, N) = a.shape, b.shape
        elif mode == "nt":
            (M, K), (N, _) = a.shape, b.shape
        else:
            (K, M), (_, N) = a.shape, b.shape
    else:
        M, N, K = dims
    a_blk = (tk, tm) if mode == "tn" else (tm, tk)
    b_blk = (tn, tk) if mode == "nt" else (tk, tn)
    if a_spec is None:
        a_spec = pl.BlockSpec(a_blk, (lambda i, j, k: (k, i)) if mode == "tn" else (lambda i, j, k: (i, k)))
    if b_spec is None:
        b_spec = pl.BlockSpec(b_blk, (lambda i, j, k: (j, k)) if mode == "nt" else (lambda i, j, k: (k, j)))
    if o_spec is None:
        o_spec = pl.BlockSpec((tm, tn), lambda i, j, k: (i, j))
    if out_shape is None:
        out_shape = (M, N)
    assert M % tm == 0 and N % tn == 0 and K % tk == 0, (name, M, N, K, tm, tn, tk)
    nk = K // tk
    contract = _DIMS[mode]
    has_add = add is not None

    def body(*refs):
        a_ref, b_ref = refs[0], refs[1]
        add_ref = refs[2] if has_add else None
        o_ref = refs[3] if has_add else refs[2]
        copy_ref = (refs[4] if has_add else refs[3]) if copy_dtype is not None else None

        def product():
            return lax.dot_general(a_ref[...].astype(MXU_DTYPE), b_ref[...].astype(MXU_DTYPE), contract,
                                   preferred_element_type=F32)

        def finish(r):
            if has_add:
                r = r + add_ref[...]
            o_ref[...] = r.astype(out_dtype)
            if copy_ref is not None:
                copy_ref[...] = r.astype(copy_dtype)

        if nk == 1:
            finish(product())
            return
        acc = refs[-1]
        k = pl.program_id(2)

        @pl.when(k == 0)
        def _():
            acc[...] = jnp.zeros_like(acc)

        acc[...] += product()

        @pl.when(k == nk - 1)
        def _():
            finish(acc[...])

    in_specs = [a_spec, b_spec]
    args = [a, b]
    nbytes = _nbytes(a_blk, a.dtype) + _nbytes(b_blk, b.dtype) + 3 * _nbytes((tm, tn), F32)
    if has_add:
        in_specs.append(pl.BlockSpec((tm, tn), lambda i, j, k: (i, j)))
        args.append(add)
        nbytes += _nbytes((tm, tn), F32)
    grid = (M // tm, N // tn, nk)
    if n_outer:
        def swapped(spec):
            return pl.BlockSpec(spec.block_shape, lambda j, i, k, at=spec.index_map: at(i, j, k))

        grid = (N // tn, M // tm, nk)
        in_specs = [swapped(s) for s in in_specs]
        o_spec = swapped(o_spec)
    out_sds, out_specs = jax.ShapeDtypeStruct(out_shape, out_dtype), o_spec
    if copy_dtype is not None:
        out_sds, out_specs = (out_sds, jax.ShapeDtypeStruct(out_shape, copy_dtype)), (o_spec, o_spec)
    return pl.pallas_call(
        body, name=name, out_shape=out_sds, grid=grid, in_specs=in_specs, out_specs=out_specs,
        scratch_shapes=[pltpu.VMEM((tm, tn), F32)] if nk > 1 else [],
        compiler_params=_params(("parallel", "parallel", "arbitrary"), nbytes),
    )(*args)


_GELU_C = math.sqrt(2.0 / math.pi)
_GELU_A = 0.044715


def _sigmoid(x):
    return 0.5 * jnp.tanh(0.5 * x) + 0.5


def _gelu(x):
    t = jnp.tanh(x * (_GELU_C + (_GELU_C * _GELU_A) * (x * x)))
    return x * (0.5 + 0.5 * t)


def _gelu_and_grad(x):
    x2 = x * x
    t = jnp.tanh(x * (_GELU_C + (_GELU_C * _GELU_A) * x2))
    cdf = 0.5 + 0.5 * t
    grad = cdf + (0.5 * x) * (1.0 - t * t) * (_GELU_C + (3.0 * _GELU_C * _GELU_A) * x2)
    return x * cdf, grad


def _rope_mix(g, cos_a, sin_a):
    return g * cos_a + pltpu.roll(g, 64, 1) * sin_a


def _rope_mix_bwd(d, cos_a, sin_a):
    return d * cos_a + pltpu.roll(d * sin_a, 64, 1)


def _rms_fwd(x, g, name, tr=512):
    T, D = x.shape

    def body(x_ref, g_ref, h_ref):
        xv = x_ref[...]
        r = lax.rsqrt(jnp.mean(xv * xv, axis=-1, keepdims=True) + EPS)
        h_ref[...] = ((xv * r) * g_ref[...]).astype(h_ref.dtype)

    return pl.pallas_call(
        body, name=name, out_shape=jax.ShapeDtypeStruct((T, D), MXU_DTYPE), grid=(T // tr,),
        in_specs=[pl.BlockSpec((tr, D), lambda i: (i, 0)), pl.BlockSpec((1, D), lambda i: (0, 0))],
        out_specs=pl.BlockSpec((tr, D), lambda i: (i, 0)),
        compiler_params=_params(("parallel",), 3 * _nbytes((tr, D), F32)),
    )(x, g)


def _rms_bwd(x, g, dh, dres, name, tr=512):
    T, D = x.shape

    def body(x_ref, g_ref, dh_ref, dres_ref, dx_ref, gg_ref):
        @pl.when(pl.program_id(0) == 0)
        def _():
            gg_ref[...] = jnp.zeros_like(gg_ref)

        xv = x_ref[...]
        r = lax.rsqrt(jnp.mean(xv * xv, axis=-1, keepdims=True) + EPS)
        xn = xv * r
        dhv = dh_ref[...]
        dxn = dhv * g_ref[...]
        dx_ref[...] = dres_ref[...] + r * (dxn - xn * jnp.mean(dxn * xn, axis=-1, keepdims=True))
        gg_ref[...] += jnp.sum(dhv * xn, axis=0, keepdims=True)

    row = pl.BlockSpec((tr, D), lambda i: (i, 0))
    vec = pl.BlockSpec((1, D), lambda i: (0, 0))
    return pl.pallas_call(
        body, name=name,
        out_shape=(jax.ShapeDtypeStruct((T, D), F32), jax.ShapeDtypeStruct((1, D), F32)),
        grid=(T // tr,), in_specs=[row, vec, row, row], out_specs=(row, vec),
        compiler_params=_params(("arbitrary",), 6 * _nbytes((tr, D), F32)),
    )(x, g, dh, dres)


def _lat_fwd(z, gq, gkv, wq, wkv, cos_a, sin_a, tr=256):
    T = z.shape[0]
    lat_blk = (4 * D_MODEL) // LAT

    def body(z_ref, gq_ref, gkv_ref, wq_ref, wkv_ref, cos_ref, sin_ref, q_ref, k_ref, v_ref, cqn_ref, ckvn_ref):
        zl = z_ref[...]
        cos_v, sin_v = cos_ref[...], sin_ref[...]
        cq = zl[:, :Q_RANK]
        ckv = zl[:, Q_RANK:Q_RANK + KV_RANK]
        krb = zl[:, Q_RANK + KV_RANK:]
        cqn = ((cq * lax.rsqrt(jnp.mean(cq * cq, axis=-1, keepdims=True) + EPS)) * gq_ref[...]).astype(MXU_DTYPE)
        ckvn = ((ckv * lax.rsqrt(jnp.mean(ckv * ckv, axis=-1, keepdims=True) + EPS)) * gkv_ref[...]).astype(MXU_DTYPE)
        cqn_ref[...] = cqn
        ckvn_ref[...] = ckvn
        krr = _rope_mix(krb, cos_v, sin_v).astype(MXU_DTYPE)
        q = jnp.dot(cqn, wq_ref[...], preferred_element_type=F32)
        kv = jnp.dot(ckvn, wkv_ref[...], preferred_element_type=F32)
        for h in range(HEADS):
            o = h * HEAD_PAD
            q_ref[:, o:o + NOPE] = q[:, o:o + NOPE].astype(MXU_DTYPE)
            q_ref[:, o + NOPE:o + HEAD_PAD] = _rope_mix(q[:, o + NOPE:o + HEAD_PAD], cos_v, sin_v).astype(MXU_DTYPE)
            k_ref[:, o:o + NOPE] = kv[:, h * NOPE:(h + 1) * NOPE].astype(MXU_DTYPE)
            k_ref[:, o + NOPE:o + HEAD_PAD] = krr
        v_ref[...] = kv[:, HEADS * NOPE:].astype(MXU_DTYPE)

    def row(w):
        return pl.BlockSpec((tr, w), lambda i: (i, 0))

    def full(a):
        return pl.BlockSpec(a.shape, lambda i: (0, 0))

    return pl.pallas_call(
        body, name="lat_fwd",
        out_shape=(jax.ShapeDtypeStruct((T, HEADS * HEAD_PAD), MXU_DTYPE), jax.ShapeDtypeStruct((T, HEADS * HEAD_PAD), MXU_DTYPE),
                   jax.ShapeDtypeStruct((T, HEADS * NOPE), MXU_DTYPE), jax.ShapeDtypeStruct((T, Q_RANK), MXU_DTYPE),
                   jax.ShapeDtypeStruct((T, KV_RANK), MXU_DTYPE)),
        grid=(T // tr,),
        in_specs=[pl.BlockSpec((tr, LAT), lambda i: (i, lat_blk)), full(gq), full(gkv), full(wq), full(wkv), row(128), row(128)],
        out_specs=(row(HEADS * HEAD_PAD), row(HEADS * HEAD_PAD), row(HEADS * NOPE), row(Q_RANK), row(KV_RANK)),
        compiler_params=_params(("parallel",), 8 * _nbytes((tr, HEADS * HEAD_PAD), F32)),
    )(z, gq, gkv, wq, wkv, cos_a, sin_a)


ATT_BLOCK = 256
_SCALE = QK_DIM ** -0.5


def _causal_mask(n):
    return lax.broadcasted_iota(jnp.int32, (n, n), 1) <= lax.broadcasted_iota(jnp.int32, (n, n), 0)


def _causal_mask_t(n):
    return lax.broadcasted_iota(jnp.int32, (n, n), 0) <= lax.broadcasted_iota(jnp.int32, (n, n), 1)


ATT_HEADS = 4
ATT_HEADS_FWD = 8


def _attn_fwd(q, k, v, B, S):
    tq = ATT_BLOCK
    nq = S // tq
    T = B * S
    hp, groups = ATT_HEADS_FWD, HEADS // ATT_HEADS_FWD

    def body(q_ref, k_ref, v_ref, o_ref, *lse_refs):
        qi = pl.program_id(2)
        qs = [q_ref[:, t * HEAD_PAD:(t + 1) * HEAD_PAD] for t in range(hp)]

        def scores(j, t):
            rows = pl.ds(pl.multiple_of(j * tq, tq), tq)
            return lax.dot_general(k_ref[rows, t * HEAD_PAD:(t + 1) * HEAD_PAD], qs[t], _DIMS["nt"],
                                   preferred_element_type=F32)

        def step(j, carry, last):
            rows = pl.ds(pl.multiple_of(j * tq, tq), tq)
            out = []
            for t in range(hp):
                m, l, acc, st = carry[t]
                st_next = st if last else scores(j + 1, t)
                st = st * _SCALE
                if last:
                    st = jnp.where(_causal_mask_t(tq), st, NEG)
                m_new = jnp.maximum(m, jnp.max(st, axis=0, keepdims=True))
                alpha = jnp.exp(m - m_new)
                p = jnp.exp(st - m_new)
                l = alpha * l + jnp.sum(p, axis=0, keepdims=True)
                acc = alpha * acc + lax.dot_general(v_ref[rows, t * NOPE:(t + 1) * NOPE], p.astype(MXU_DTYPE),
                                                    _DIMS["tn"], preferred_element_type=F32)
                out.append((m_new, l, acc, st_next))
            return tuple(out)

        init = tuple((jnp.full((1, tq), NEG, F32), jnp.zeros((1, tq), F32), jnp.zeros((NOPE, tq), F32), scores(0, t))
                     for t in range(hp))
        carry = lax.fori_loop(0, qi, lambda j, c: step(j, c, False), init)
        carry = step(qi, carry, True)
        for t in range(hp):
            m, l, acc, _ = carry[t]
            o_ref[:, t * NOPE:(t + 1) * NOPE] = (acc / l).T
            lse_refs[t][0] = m + jnp.log(l)

    lse_sds = jax.ShapeDtypeStruct((groups * B * nq, 1, tq), F32)
    lse_spec = pl.BlockSpec((1, 1, tq), lambda b, h, i: ((h * B + b) * nq + i, 0, 0))
    o, *lses = pl.pallas_call(
        body, name="attn_fwd",
        out_shape=(jax.ShapeDtypeStruct((T, HEADS * NOPE), F32),) + (lse_sds,) * hp,
        grid=(B, groups, nq),
        in_specs=[pl.BlockSpec((tq, hp * HEAD_PAD), lambda b, h, i: (b * nq + i, h)),
                  pl.BlockSpec((S, hp * HEAD_PAD), lambda b, h, i: (b, h)),
                  pl.BlockSpec((S, hp * NOPE), lambda b, h, i: (b, h))],
        out_specs=(pl.BlockSpec((tq, hp * NOPE), lambda b, h, i: (b * nq + i, h)),) + (lse_spec,) * hp,
        compiler_params=_params(("parallel", "parallel", "arbitrary"), 4 * hp * _nbytes((S, HEAD_PAD), MXU_DTYPE)),
    )(q, k, v)
    lse = jnp.stack([a.reshape(groups, B * nq, 1, tq) for a in lses], axis=1).reshape(HEADS * B * nq, 1, tq)
    return o, lse


def _attn_bwd(q, k, v, do, lse, delta, B, S):
    tq = ATT_BLOCK
    nq = S // tq
    T = B * S
    hp, groups = ATT_HEADS, HEADS // ATT_HEADS

    def body(q_ref, k_ref, v_ref, do_ref, *refs):
        lse_refs, dl_refs = refs[:hp], refs[hp:2 * hp]
        dq_out, dk_ref, dv_ref, dq_ref = refs[2 * hp:]
        kj = pl.program_id(2)

        @pl.when(kj == 0)
        def _():
            dq_ref[...] = jnp.zeros_like(dq_ref)

        def products(i, t):
            rows = pl.ds(pl.multiple_of(i * tq, tq), tq)
            st = lax.dot_general(k_ref[:, t * HEAD_PAD:(t + 1) * HEAD_PAD], q_ref[rows, t * HEAD_PAD:(t + 1) * HEAD_PAD],
                                 _DIMS["nt"], preferred_element_type=F32)
            dpt = lax.dot_general(v_ref[:, t * NOPE:(t + 1) * NOPE], do_ref[rows, t * NOPE:(t + 1) * NOPE],
                                  _DIMS["nt"], preferred_element_type=F32)
            return st, dpt

        def step(i, carry, masked):
            rows = pl.ds(pl.multiple_of(i * tq, tq), tq)
            nxt = jnp.minimum(i + 1, nq - 1)
            out = []
            for t in range(hp):
                dk, dv, st, dpt = carry[t]
                st_next, dpt_next = products(nxt, t)
                qk_cols = slice(t * HEAD_PAD, (t + 1) * HEAD_PAD)
                v_cols = slice(t * NOPE, (t + 1) * NOPE)
                p = jnp.exp(st * _SCALE - lse_refs[t][i])
                if masked:
                    p = jnp.where(_causal_mask_t(tq), p, 0.0)
                dv = dv + jnp.dot(p.astype(MXU_DTYPE), do_ref[rows, v_cols], preferred_element_type=F32)
                ds = (p * (dpt - dl_refs[t][i]) * _SCALE).astype(MXU_DTYPE)
                dk = dk + jnp.dot(ds, q_ref[rows, qk_cols], preferred_element_type=F32)
                dq_ref[rows, qk_cols] += lax.dot_general(ds, k_ref[:, qk_cols], _DIMS["tn"], preferred_element_type=F32)
                out.append((dk, dv, st_next, dpt_next))
            return tuple(out)

        init = tuple((jnp.zeros((tq, HEAD_PAD), F32), jnp.zeros((tq, NOPE), F32)) + products(kj, t) for t in range(hp))
        carry = step(kj, init, True)
        carry = lax.fori_loop(kj + 1, nq, lambda i, c: step(i, c, False), carry)
        for t in range(hp):
            dk_ref[:, t * HEAD_PAD:(t + 1) * HEAD_PAD] = carry[t][0].astype(dk_ref.dtype)
            dv_ref[:, t * NOPE:(t + 1) * NOPE] = carry[t][1].astype(dv_ref.dtype)

        @pl.when(kj == nq - 1)
        def _():
            dq_out[...] = dq_ref[...].astype(dq_out.dtype)

    seq = lambda w: pl.BlockSpec((S, w), lambda b, h, j: (b, h))
    blk = lambda w: pl.BlockSpec((tq, w), lambda b, h, j: (b * nq + j, h))
    stat_specs = [pl.BlockSpec((nq, 1, tq), lambda b, h, j, t=t: ((h * hp + t) * B + b, 0, 0)) for t in range(hp)]
    return pl.pallas_call(
        body, name="attn_bwd",
        out_shape=(jax.ShapeDtypeStruct((T, HEADS * HEAD_PAD), MXU_DTYPE), jax.ShapeDtypeStruct((T, HEADS * HEAD_PAD), MXU_DTYPE),
                   jax.ShapeDtypeStruct((T, HEADS * NOPE), MXU_DTYPE)),
        grid=(B, groups, nq),
        in_specs=[seq(hp * HEAD_PAD), blk(hp * HEAD_PAD), blk(hp * NOPE), seq(hp * NOPE)] + stat_specs + stat_specs,
        out_specs=(seq(hp * HEAD_PAD), blk(hp * HEAD_PAD), blk(hp * NOPE)),
        scratch_shapes=[pltpu.VMEM((S, hp * HEAD_PAD), F32)],
        compiler_params=_params(("parallel", "parallel", "arbitrary"), 8 * hp * _nbytes((S, HEAD_PAD), F32)),
    )(q, k, v, do, *([lse] * hp), *([delta] * hp))


MIX_ROWS = 256


def _tril_weights(ws_ref, g):
    return jnp.where(_causal_mask(CHUNK), ws_ref[g], 0.0).astype(MXU_DTYPE)


def _layer_norm_stats(va):
    mu = jnp.mean(va, axis=-1, keepdims=True)
    xc = va - mu
    rs = lax.rsqrt(jnp.mean(xc * xc, axis=-1, keepdims=True) + EPS)
    return xc * rs


def _mix_specs(tr):
    zcol = lambda c: pl.BlockSpec((tr, D_MODEL), lambda i, c=c: (i, c))
    row = pl.BlockSpec((tr, D_MODEL), lambda i: (i, 0))
    vec = pl.BlockSpec((1, D_MODEL), lambda i: (0, 0))
    ws = pl.BlockSpec((A_GROUPS, CHUNK, CHUNK), lambda i: (0, 0, 0))
    bs = pl.BlockSpec((CHUNK, 128), lambda i: (0, 0))
    return zcol, row, vec, ws, bs


def _mix_fwd(z, yb, ln_g, ln_b, ws, bs_t):
    T = z.shape[0]
    tr = MIX_ROWS
    zcol, row, vec, ws_spec, bs_spec = _mix_specs(tr)

    def body(zu_ref, zv_ref, zga_ref, zgb_ref, yb_ref, g_ref, b_ref, ws_ref, bs_ref, out_ref, vn_s):
        vhat = _layer_norm_stats(_gelu(zv_ref[...]))
        vn_s[...] = (vhat * g_ref[...] + b_ref[...]).astype(MXU_DTYPE)
        for g in range(A_GROUPS):
            w = _tril_weights(ws_ref, g)
            bias = bs_ref[:, g:g + 1]
            cols = slice(g * CHUNK, (g + 1) * CHUNK)
            for c in range(tr // CHUNK):
                rows = slice(c * CHUNK, (c + 1) * CHUNK)
                mixed = jnp.dot(w, vn_s[rows, cols], preferred_element_type=F32) + bias
                ya = _gelu(zu_ref[rows, cols]) * mixed
                merged = _sigmoid(zga_ref[rows, cols]) * ya + _sigmoid(zgb_ref[rows, cols]) * yb_ref[rows, cols]
                out_ref[rows, cols] = merged.astype(MXU_DTYPE)

    return pl.pallas_call(
        body, name="mix_fwd", out_shape=jax.ShapeDtypeStruct((T, D_MODEL), MXU_DTYPE), grid=(T // tr,),
        in_specs=[zcol(0), zcol(1), zcol(2), zcol(3), row, vec, vec, ws_spec, bs_spec], out_specs=row,
        scratch_shapes=[pltpu.VMEM((tr, D_MODEL), MXU_DTYPE)],
        compiler_params=_params(("parallel",), 8 * _nbytes((tr, D_MODEL), F32)),
    )(z, z, z, z, yb, ln_g, ln_b, ws, bs_t)


def _mix_bwd(z, yb, dm, ln_g, ln_b, ws, bs_t):
    T = z.shape[0]
    tr = MIX_ROWS
    zcol, row, vec, ws_spec, bs_spec = _mix_specs(tr)

    def body(zu_ref, zv_ref, zga_ref, zgb_ref, yb_ref, dm_ref, g_ref, b_ref, ws_ref, bs_ref,
             dz_ref, dyb_ref, dl_ref, gws_ref, gbs_ref, glg_ref, glb_ref, vn_s, dvn_s):
        @pl.when(pl.program_id(0) == 0)
        def _():
            gws_ref[...] = jnp.zeros_like(gws_ref)
            gbs_ref[...] = jnp.zeros_like(gbs_ref)
            glg_ref[...] = jnp.zeros_like(glg_ref)
            glb_ref[...] = jnp.zeros_like(glb_ref)

        lane = lax.broadcasted_iota(jnp.int32, (CHUNK, 128), 1)
        va, dgelu_v = _gelu_and_grad(zv_ref[...])
        mu = jnp.mean(va, axis=-1, keepdims=True)
        xc = va - mu
        rs = lax.rsqrt(jnp.mean(xc * xc, axis=-1, keepdims=True) + EPS)
        vhat = xc * rs
        vn_s[...] = (vhat * g_ref[...] + b_ref[...]).astype(MXU_DTYPE)
        gbs_acc = jnp.zeros((CHUNK, 128), F32)
        for g in range(A_GROUPS):
            w = _tril_weights(ws_ref, g)
            bias = bs_ref[:, g:g + 1]
            cols = slice(g * CHUNK, (g + 1) * CHUNK)
            gw_acc = jnp.zeros((CHUNK, CHUNK), F32)
            for c in range(tr // CHUNK):
                rows = slice(c * CHUNK, (c + 1) * CHUNK)
                vn = vn_s[rows, cols]
                mixed = jnp.dot(w, vn, preferred_element_type=F32) + bias
                ua, dgelu_u = _gelu_and_grad(zu_ref[rows, cols])
                dmv = dm_ref[rows, cols]
                sa = _sigmoid(zga_ref[rows, cols])
                dya = dmv * sa
                dz_ref[rows, 2 * D_MODEL + g * CHUNK:2 * D_MODEL + (g + 1) * CHUNK] = (
                    dmv * (ua * mixed) * (sa * (1.0 - sa))).astype(dz_ref.dtype)
                dz_ref[rows, cols] = (dya * mixed * dgelu_u).astype(dz_ref.dtype)
                dmix = dya * ua
                gbs_acc = gbs_acc + jnp.where(lane == g, jnp.sum(dmix, axis=-1, keepdims=True), 0.0)
                dmix_b = dmix.astype(MXU_DTYPE)
                gw_acc = gw_acc + lax.dot_general(dmix_b, vn, _DIMS["nt"], preferred_element_type=F32)
                dvn_s[rows, cols] = lax.dot_general(w, dmix_b, _DIMS["tn"], preferred_element_type=F32)
            gws_ref[g] += jnp.where(_causal_mask(CHUNK), gw_acc, 0.0)
        gbs_ref[...] += gbs_acc

        dvn = dvn_s[...]
        glg_ref[...] += jnp.sum(dvn * vhat, axis=0, keepdims=True)
        glb_ref[...] += jnp.sum(dvn, axis=0, keepdims=True)
        dvh = dvn * g_ref[...]
        dva = rs * (dvh - jnp.mean(dvh, axis=-1, keepdims=True) - vhat * jnp.mean(dvh * vhat, axis=-1, keepdims=True))
        dz_ref[:, D_MODEL:2 * D_MODEL] = (dva * dgelu_v).astype(dz_ref.dtype)

        dmv = dm_ref[...]
        ybv = yb_ref[...]
        sb = _sigmoid(zgb_ref[...])
        dyb = dmv * sb
        dyb_ref[...] = dyb.astype(dyb_ref.dtype)
        dz_ref[:, 3 * D_MODEL:4 * D_MODEL] = (dmv * ybv * (sb * (1.0 - sb))).astype(dz_ref.dtype)
        dz_ref[:, 4 * D_MODEL:] = jnp.zeros((tr, LAT), dz_ref.dtype)
        prod = dyb * ybv
        sel = (lax.broadcasted_iota(jnp.int32, (HEADS, D_MODEL), 1) // NOPE
               == lax.broadcasted_iota(jnp.int32, (HEADS, D_MODEL), 0)).astype(jnp.bfloat16)
        hi = prod.astype(jnp.bfloat16)
        rest = prod - hi.astype(F32)
        mid = rest.astype(jnp.bfloat16)
        lo = (rest - mid.astype(F32)).astype(jnp.bfloat16)
        dl_ref[...] = (lax.dot_general(sel, hi, _DIMS["nt"], preferred_element_type=F32)
                       + lax.dot_general(sel, mid, _DIMS["nt"], preferred_element_type=F32)
                       + lax.dot_general(sel, lo, _DIMS["nt"], preferred_element_type=F32))

    return pl.pallas_call(
        body, name="mix_bwd",
        out_shape=(jax.ShapeDtypeStruct((T, IN_PAD), MXU_DTYPE), jax.ShapeDtypeStruct((T, D_MODEL), MXU_DTYPE),
                   jax.ShapeDtypeStruct((HEADS, T), F32), jax.ShapeDtypeStruct((A_GROUPS, CHUNK, CHUNK), F32),
                   jax.ShapeDtypeStruct((CHUNK, 128), F32), jax.ShapeDtypeStruct((1, D_MODEL), F32),
                   jax.ShapeDtypeStruct((1, D_MODEL), F32)),
        grid=(T // tr,),
        in_specs=[zcol(0), zcol(1), zcol(2), zcol(3), row, row, vec, vec, ws_spec, bs_spec],
        out_specs=(pl.BlockSpec((tr, IN_PAD), lambda i: (i, 0)), row, pl.BlockSpec((HEADS, tr), lambda i: (0, i)),
                   ws_spec, bs_spec, vec, vec),
        scratch_shapes=[pltpu.VMEM((tr, D_MODEL), MXU_DTYPE), pltpu.VMEM((tr, D_MODEL), F32)],
        compiler_params=_params(("arbitrary",), 12 * _nbytes((tr, D_MODEL), F32)),
    )(z, z, z, z, yb, dm, ln_g, ln_b, ws, bs_t)


def _lat_bwd(dz, z, dq, dk, dv, gq, gkv, wq, wkv, cos_a, sin_a, tr=256):
    T = z.shape[0]
    lat_blk = (4 * D_MODEL) // LAT

    def body(dz_in, z_ref, dq_ref, dk_ref, dv_ref, gq_ref, gkv_ref, wq_ref, wkv_ref, cos_ref, sin_ref,
             dz_ref, dqr_ref, dkv_ref, ggq_ref, ggkv_ref):
        del dz_in

        @pl.when(pl.program_id(0) == 0)
        def _():
            ggq_ref[...] = jnp.zeros_like(ggq_ref)
            ggkv_ref[...] = jnp.zeros_like(ggkv_ref)

        cos_v, sin_v = cos_ref[...], sin_ref[...]
        dkr = jnp.zeros((tr, 128), F32)
        for h in range(HEADS):
            o = h * HEAD_PAD
            dqr_ref[:, o:o + NOPE] = dq_ref[:, o:o + NOPE].astype(MXU_DTYPE)
            dqr_ref[:, o + NOPE:o + HEAD_PAD] = _rope_mix_bwd(dq_ref[:, o + NOPE:o + HEAD_PAD], cos_v, sin_v).astype(MXU_DTYPE)
            dkv_ref[:, h * NOPE:(h + 1) * NOPE] = dk_ref[:, o:o + NOPE].astype(MXU_DTYPE)
            dkr = dkr + _rope_mix_bwd(dk_ref[:, o + NOPE:o + HEAD_PAD], cos_v, sin_v)
        dkv_ref[:, HEADS * NOPE:] = dv_ref[...]
        dcqn = lax.dot_general(dqr_ref[...], wq_ref[...], _DIMS["nt"], preferred_element_type=F32)
        dckvn = lax.dot_general(dkv_ref[...], wkv_ref[...], _DIMS["nt"], preferred_element_type=F32)

        zl = z_ref[...]

        def rms_bwd(c, dn, g_ref, gg_ref):
            r = lax.rsqrt(jnp.mean(c * c, axis=-1, keepdims=True) + EPS)
            ch = c * r
            gg_ref[...] += jnp.sum(dn * ch, axis=0, keepdims=True)
            dch = dn * g_ref[...]
            return r * (dch - ch * jnp.mean(dch * ch, axis=-1, keepdims=True))

        dz_ref[:, :Q_RANK] = rms_bwd(zl[:, :Q_RANK], dcqn, gq_ref, ggq_ref).astype(dz_ref.dtype)
        dz_ref[:, Q_RANK:Q_RANK + KV_RANK] = rms_bwd(zl[:, Q_RANK:Q_RANK + KV_RANK], dckvn, gkv_ref, ggkv_ref).astype(dz_ref.dtype)
        dz_ref[:, Q_RANK + KV_RANK:] = dkr.astype(dz_ref.dtype)

    def row(w):
        return pl.BlockSpec((tr, w), lambda i: (i, 0))

    def full(a):
        return pl.BlockSpec(a.shape, lambda i: (0, 0))

    lat = pl.BlockSpec((tr, LAT), lambda i: (i, lat_blk))
    return pl.pallas_call(
        body, name="lat_bwd",
        out_shape=(jax.ShapeDtypeStruct(dz.shape, dz.dtype), jax.ShapeDtypeStruct((T, HEADS * HEAD_PAD), MXU_DTYPE),
                   jax.ShapeDtypeStruct((T, 2 * HEADS * NOPE), MXU_DTYPE), jax.ShapeDtypeStruct(gq.shape, F32),
                   jax.ShapeDtypeStruct(gkv.shape, F32)),
        grid=(T // tr,),
        in_specs=[pl.BlockSpec(memory_space=pl.ANY), lat, row(HEADS * HEAD_PAD), row(HEADS * HEAD_PAD), row(HEADS * NOPE),
                  full(gq), full(gkv), full(wq), full(wkv), row(128), row(128)],
        out_specs=(lat, row(HEADS * HEAD_PAD), row(2 * HEADS * NOPE), full(gq), full(gkv)),
        input_output_aliases={0: 0},
        compiler_params=_params(("arbitrary",), 8 * _nbytes((tr, HEADS * HEAD_PAD), F32)),
    )(dz, z, dq, dk, dv, gq, gkv, wq, wkv, cos_a, sin_a)


GATE_ROWS = 64
HALO = 8


def _taps(ref, half, r, first):
    C = GATE_ROWS
    if first:
        xs = jnp.concatenate([jnp.zeros((HALO, ref.shape[-1]), F32), ref[half, 0:C, :]], axis=0)
    else:
        xs = ref[half, pl.ds(pl.multiple_of(r * C - HALO, HALO), C + HALO), :]
    return xs[HALO:, :], pltpu.roll(xs, 1, 0)[HALO:, :], pltpu.roll(xs, 2, 0)[HALO:, :]


def _conv_taps(taps, cw, cb):
    x0, x1, x2 = taps
    return cb + cw[0:1, :] * x2 + cw[1:2, :] * x1 + cw[2:3, :] * x0


def _fold8(x):
    acc = x[0:8, :]
    for i in range(1, x.shape[0] // 8):
        acc = acc + x[8 * i:8 * (i + 1), :]
    return acc


def _gate_fwd(up3, conv_w, conv_b, B, S):
    T = B * S
    W = FF_TILE
    C = GATE_ROWS

    def body(up_ref, cw_ref, cb_ref, act_ref):
        def chunk(r, first):
            gate = _conv_taps(_taps(up_ref, 0, r, first), cw_ref[0], cb_ref[0])
            val = _conv_taps(_taps(up_ref, 1, r, first), cw_ref[1], cb_ref[1])
            base = 0 if first else pl.multiple_of(r * C, C)
            act_ref[pl.ds(base, C), :] = (gate * _sigmoid(gate) * val).astype(act_ref.dtype)

        chunk(0, True)

        @pl.loop(1, S // C)
        def _(r):
            chunk(r, False)

    return pl.pallas_call(
        body, name="gate_fwd", out_shape=jax.ShapeDtypeStruct((T, D_FF), MXU_DTYPE), grid=(B, N_FF_TILES),
        in_specs=[pl.BlockSpec((2, S, W), lambda b, j: (0, b, j)), pl.BlockSpec((2, 3, W), lambda b, j: (0, 0, j)),
                  pl.BlockSpec((2, 1, W), lambda b, j: (0, 0, j))],
        out_specs=pl.BlockSpec((S, W), lambda b, j: (b, j)),
        compiler_params=_params(("parallel", "parallel"), 6 * _nbytes((S, W), F32)),
    )(up3, conv_w, conv_b)


def _gate_bwd(up3, dact, conv_w, conv_b, B, S):
    T = B * S
    W = FF_TILE
    C = GATE_ROWS

    def body(up_ref, da_ref, cw_ref, cb_ref, dup_ref, gcw_ref, gcb_ref, d_s):
        @pl.when(pl.program_id(1) == 0)
        def _():
            gcw_ref[...] = jnp.zeros_like(gcw_ref)
            gcb_ref[...] = jnp.zeros_like(gcb_ref)

        def chunk(r, first, sums):
            rows = pl.ds(0 if first else pl.multiple_of(r * C, C), C)
            taps = [_taps(up_ref, half, r, first) for half in (0, 1)]
            gate = _conv_taps(taps[0], cw_ref[0], cb_ref[0])
            val = _conv_taps(taps[1], cw_ref[1], cb_ref[1])
            sg = _sigmoid(gate)
            da = da_ref[rows, :]
            d_halves = (da * val * (sg * (1.0 + gate * (1.0 - sg))), da * (gate * sg))
            out = []
            for half, dup in enumerate(d_halves):
                d_s[half, rows, :] = dup
                x0, x1, x2 = taps[half]
                sb, s0, s1, s2 = sums[half]
                out.append((sb + _fold8(dup), s0 + _fold8(dup * x2), s1 + _fold8(dup * x1), s2 + _fold8(dup * x0)))
            return tuple(out)

        zeros = tuple(tuple(jnp.zeros((8, W), F32) for _ in range(4)) for _ in range(2))
        sums = chunk(0, True, zeros)
        sums = lax.fori_loop(1, S // C, lambda r, s: chunk(r, False, s), sums)
        for half in (0, 1):
            sb, s0, s1, s2 = sums[half]
            gcb_ref[half] += jnp.sum(sb, axis=0, keepdims=True)
            gcw_ref[half, 0:1, :] += jnp.sum(s0, axis=0, keepdims=True)
            gcw_ref[half, 1:2, :] += jnp.sum(s1, axis=0, keepdims=True)
            gcw_ref[half, 2:3, :] += jnp.sum(s2, axis=0, keepdims=True)

        d_s[:, S:S + HALO, :] = jnp.zeros((2, HALO, W), F32)

        @pl.loop(0, S // C)
        def _(r):
            base = pl.multiple_of(r * C, C)
            for half in (0, 1):
                ds_ = d_s[half, pl.ds(base, C + HALO), :]
                cw = cw_ref[half]
                dx = (cw[2:3, :] * ds_[:C, :] + cw[1:2, :] * pltpu.roll(ds_, C + HALO - 1, 0)[:C, :]
                      + cw[0:1, :] * pltpu.roll(ds_, C + HALO - 2, 0)[:C, :])
                dup_ref[half, pl.ds(base, C), :] = dx.astype(dup_ref.dtype)

    up_spec = pl.BlockSpec((2, S, W), lambda j, b: (0, b, j))
    cw_spec = pl.BlockSpec((2, 3, W), lambda j, b: (0, 0, j))
    cb_spec = pl.BlockSpec((2, 1, W), lambda j, b: (0, 0, j))
    return pl.pallas_call(
        body, name="gate_bwd",
        out_shape=(jax.ShapeDtypeStruct((2, T, D_FF), MXU_DTYPE), jax.ShapeDtypeStruct((2, 3, D_FF), F32),
                   jax.ShapeDtypeStruct((2, 1, D_FF), F32)),
        grid=(N_FF_TILES, B),
        in_specs=[up_spec, pl.BlockSpec((S, W), lambda j, b: (b, j)), cw_spec, cb_spec],
        out_specs=(up_spec, cw_spec, cb_spec),
        scratch_shapes=[pltpu.VMEM((2, S + HALO, W), F32)],
        compiler_params=_params(("parallel", "arbitrary"), 10 * _nbytes((S, W), F32)),
    )(up3, dact, conv_w, conv_b)


def _final(x2, tgt, g, tr=512):
    T, D = x2.shape

    def body(x_ref, t_ref, g_ref, dx_ref, loss_ref, gg_ref):
        @pl.when(pl.program_id(0) == 0)
        def _():
            loss_ref[...] = jnp.zeros_like(loss_ref)
            gg_ref[...] = jnp.zeros_like(gg_ref)

        xv = x_ref[...]
        gv = g_ref[...]
        r = lax.rsqrt(jnp.mean(xv * xv, axis=-1, keepdims=True) + EPS)
        xn = xv * r
        err = xn * gv - t_ref[...]
        loss_ref[...] += 0.5 * jnp.sum(jnp.mean(err * err, axis=-1, keepdims=True), axis=0, keepdims=True)
        dy = err * (1.0 / D)
        gg_ref[...] += jnp.sum(dy * xn, axis=0, keepdims=True)
        dxn = dy * gv
        dx_ref[...] = r * (dxn - xn * jnp.mean(dxn * xn, axis=-1, keepdims=True))

    row = pl.BlockSpec((tr, D), lambda i: (i, 0))
    vec = pl.BlockSpec((1, D), lambda i: (0, 0))
    return pl.pallas_call(
        body, name="final_loss",
        out_shape=(jax.ShapeDtypeStruct((T, D), F32), jax.ShapeDtypeStruct((1, 128), F32), jax.ShapeDtypeStruct((1, D), F32)),
        grid=(T // tr,), in_specs=[row, row, vec],
        out_specs=(row, pl.BlockSpec((1, 128), lambda i: (0, 0)), vec),
        compiler_params=_params(("arbitrary",), 6 * _nbytes((tr, D), F32)),
    )(x2, tgt, g)


def _sum_slabs(parts, name, tr):
    rows, cols = parts[0].shape
    n = len(parts)

    def body(*refs):
        acc = refs[0][...]
        for r in refs[1:n]:
            acc = acc + r[...]
        refs[n][...] = acc

    blk = pl.BlockSpec((tr, cols), lambda i: (i, 0))
    return pl.pallas_call(
        body, name=name, out_shape=jax.ShapeDtypeStruct((rows, cols), F32), grid=(rows // tr,),
        in_specs=[blk] * n, out_specs=blk,
        compiler_params=_params(("parallel",), (n + 1) * _nbytes((tr, cols), F32)),
    )(*parts)


ADAMW_BLOCK_BYTES = 2400 * 1024


def _adamw(w, g, m, v, name):
    lead = w.ndim == 3
    rows, cols = w.shape[-2:]
    fits = [d for d in range(8, rows + 1, 8) if rows % d == 0 and d * cols * 4 <= ADAMW_BLOCK_BYTES]
    tr = max(fits) if fits else rows
    c1 = 1.0 - ADAM_B1 ** ADAM_STEP
    c2 = 1.0 - ADAM_B2 ** ADAM_STEP

    def body(w_ref, g_ref, m_ref, v_ref, d_ref, nm_ref, nv_ref):
        gv = g_ref[...]
        nm = ADAM_B1 * m_ref[...] + (1.0 - ADAM_B1) * gv
        nv = ADAM_B2 * v_ref[...] + (1.0 - ADAM_B2) * (gv * gv)
        nm_ref[...] = nm
        nv_ref[...] = nv
        d_ref[...] = -ADAM_LR * ((nm / c1) / (jnp.sqrt(nv / c2) + ADAM_EPS) + ADAM_WD * w_ref[...])

    blk = pl.BlockSpec((None, tr, cols), lambda i: (0, i, 0)) if lead else pl.BlockSpec((tr, cols), lambda i: (i, 0))
    sds = jax.ShapeDtypeStruct(w.shape, F32)
    return pl.pallas_call(
        body, name=name, out_shape=(sds, sds, sds), grid=(rows // tr,), in_specs=[blk] * 4, out_specs=(blk, blk, blk),
        compiler_params=_params(("parallel",), 7 * _nbytes((tr, cols), F32)),
    )(w, g, m, v)


_ANY = pl.BlockSpec(memory_space=pl.ANY)


def _place():
    x, y, c = lax.axis_index("x"), lax.axis_index("y"), lax.axis_index("c")
    chips = [(1 - x, y), (x, 1 - y), (1 - x, 1 - y)]
    return x, y, c, chips


def _forward_halves(lands):
    n = len(lands)

    def body(*refs):
        outs, send, recv = refs[n:2 * n], refs[2 * n], refs[2 * n + 1]
        x, y, c, chips = _place()
        cps = []
        for w in range(n):
            for j, (px, py) in enumerate(chips):
                landed = outs[w].at[2 * px + py, c]
                cps.append(pltpu.make_async_remote_copy(
                    src_ref=landed, dst_ref=landed, send_sem=send.at[3 * w + j], recv_sem=recv.at[3 * w + j],
                    device_id=(x, y, 1 - c), device_id_type=MESH))
        for cp in cps:
            cp.start()
        for w in range(n):
            for j, (px, py) in enumerate(chips):
                other = outs[w].at[2 * px + py, 1 - c]
                pltpu.make_async_remote_copy(src_ref=other, dst_ref=other, send_sem=send.at[3 * w + j],
                                             recv_sem=recv.at[3 * w + j], device_id=(x, y, 1 - c),
                                             device_id_type=MESH).wait_recv()
        for cp in cps:
            cp.wait_send()

    dma = lambda k: pltpu.SemaphoreType.DMA((k,))
    return pl.pallas_call(
        body, name="gather_forward_halves", out_shape=tuple(jax.ShapeDtypeStruct(a.shape, a.dtype) for a in lands),
        in_specs=[_ANY] * n, out_specs=tuple([_ANY] * n), input_output_aliases={w: w for w in range(n)},
        scratch_shapes=[dma(3 * n), dma(3 * n)],
    )(*lands)


_HBM = pl.BlockSpec(memory_space=pltpu.HBM)
_SEM = pl.BlockSpec(memory_space=pltpu.SEMAPHORE)
_EFFECT = pltpu.SideEffectType.DATAFLOW_SIDE_EFFECTING


SEMS_PER_ARRAY = 8


def _exchange_copies(srcs, lands, send, recv, mode):
    x, y, c, chips = _place()
    if mode == "halves":
        cps = []
        for w, (src, land) in enumerate(zip(srcs, lands)):
            pieces = [(src.at[c], land.at[2 * x + y, c], (px, py, c)) for px, py in chips]
            pieces.append((src, land.at[2 * x + y], (x, y, 1 - c)))
            for k, (piece, dst, peer) in enumerate(pieces):
                cps.append(pltpu.make_async_remote_copy(
                    src_ref=piece, dst_ref=dst, send_sem=send.at[SEMS_PER_ARRAY * w + k],
                    recv_sem=recv.at[SEMS_PER_ARRAY * w + k], device_id=peer, device_id_type=MESH))
        return cps
    if mode == "all":
        flips = [(fx, fy, fc) for fx in (0, 1) for fy in (0, 1) for fc in (0, 1)][1:]
        peers = [(x ^ fx, y ^ fy, c ^ fc) for fx, fy, fc in flips]
        slot = 4 * x + 2 * y + c
    else:
        peers = [(px, py, c) for px, py in chips] + ([(x, y, 1 - c)] if mode == "gather" else [])
        slot = 2 * x + y
    cps = []
    for w, (src, land) in enumerate(zip(srcs, lands)):
        for k, peer in enumerate(peers):
            piece = src.at[2 * peer[0] + peer[1]] if mode == "scatter" else src
            cps.append(pltpu.make_async_remote_copy(
                src_ref=piece, dst_ref=land.at[slot], send_sem=send.at[SEMS_PER_ARRAY * w + k],
                recv_sem=recv.at[SEMS_PER_ARRAY * w + k], device_id=peer, device_id_type=MESH))
    return cps


def _exchange_start(srcs, name, mode, after):
    n = len(srcs)
    lead = {"gather": (N_CHIPS,), "halves": (N_CHIPS,), "scatter": (), "all": (2 * N_CHIPS,)}[mode]
    land_shapes = [lead + s.shape for s in srcs]

    def body(*refs):
        src_refs, land_refs = refs[:n], refs[n:2 * n]
        send, recv = refs[2 * n + 1], refs[2 * n + 2]
        token = refs[-1]
        for cp in _exchange_copies(src_refs, land_refs, send, recv, mode):
            cp.start()
        token[...] = jnp.zeros_like(token)

    sems = pltpu.SemaphoreType.DMA((SEMS_PER_ARRAY * n,))
    out = pl.pallas_call(
        body, name=name,
        out_shape=(sems, sems, *[pltpu.HBM(s.shape, s.dtype) for s in srcs],
                   *[pltpu.HBM(shp, s.dtype) for shp, s in zip(land_shapes, srcs)], jax.ShapeDtypeStruct((8, 128), F32)),
        in_specs=[_HBM] * (2 * n) + [_ANY],
        out_specs=(_SEM, _SEM, *[_HBM] * (2 * n), pl.BlockSpec(memory_space=pltpu.VMEM)),
        input_output_aliases={i: 2 + i for i in range(2 * n)},
        compiler_params=pltpu.CompilerParams(has_side_effects=_EFFECT),
    )(*[pltpu.with_memory_space_constraint(s, pltpu.HBM) for s in srcs],
      *[pltpu.with_memory_space_constraint(lax.empty(shp, s.dtype), pltpu.HBM) for shp, s in zip(land_shapes, srcs)],
      after)
    return out[0], out[1], out[2:2 + n], out[2 + n:2 + 2 * n], out[-1]


def _exchange_wait(started, name, mode, after):
    send, recv, src_thru, land_thru, _ = started
    n = len(src_thru)

    def body(*refs):
        src_refs, land_refs, send_ref, recv_ref = refs[:n], refs[n:2 * n], refs[2 * n], refs[2 * n + 1]
        for cp in _exchange_copies(src_refs, land_refs, send_ref, recv_ref, mode):
            cp.wait_send()
            cp.wait_recv()

    out = pl.pallas_call(
        body, name=name,
        out_shape=tuple(pltpu.HBM(a.shape, a.dtype) for a in list(src_thru) + list(land_thru)),
        in_specs=[_HBM] * (2 * n) + [_SEM, _SEM, _ANY], out_specs=tuple([_HBM] * (2 * n)),
        input_output_aliases={i: i for i in range(2 * n)},
        compiler_params=pltpu.CompilerParams(has_side_effects=_EFFECT),
    )(*src_thru, *land_thru, send, recv, after)
    return out[:n], out[n:]


def _swap_halves(gs, name):
    n = len(gs)

    def body(*refs):
        ins, outs, send, recv = refs[:n], refs[n:2 * n], refs[2 * n], refs[2 * n + 1]
        x, y, c, _ = _place()
        cps = []
        for w in range(n):
            cps.append(pltpu.make_async_remote_copy(
                src_ref=ins[w].at[:, 1 - c], dst_ref=outs[w], send_sem=send.at[w], recv_sem=recv.at[w],
                device_id=(x, y, 1 - c), device_id_type=MESH))
        for cp in cps:
            cp.start()
        for cp in cps:
            cp.wait()

    return pl.pallas_call(
        body, name=name,
        out_shape=tuple(jax.ShapeDtypeStruct((g.shape[0],) + g.shape[2:], g.dtype) for g in gs),
        in_specs=[_ANY] * n, out_specs=tuple([_ANY] * n),
        scratch_shapes=[pltpu.SemaphoreType.DMA((n,)), pltpu.SemaphoreType.DMA((n,))],
    )(*gs)


GRAD_PAYLOAD = jnp.bfloat16


def _half_blocks(half_rows, cols):
    if (half_rows // 2) % 16 == 0:
        return (half_rows // 2, cols), (lambda r: (r, 0))
    assert cols % 256 == 0, (half_rows, cols)
    return (half_rows, cols // 2), (lambda r: (0, r))


def _pair_sum(gs, gots, name):
    n = len(gs)
    core = lax.axis_index("c").astype(jnp.int32).reshape(1)

    def body(core_ref, *refs):
        del core_ref
        for w in range(n):
            refs[2 * n + w][...] = (refs[w][...] + refs[n + w][...]).astype(GRAD_PAYLOAD)

    in_specs, out_specs, out_shape, nbytes = [], [], [], 0
    cuts = [_half_blocks(g.shape[1] // 2, g.shape[2]) for g in gs]
    for g, ((br, bc), at) in zip(gs, cuts):
        per_half = (g.shape[1] // 2) // br
        in_specs.append(pl.BlockSpec((1, br, bc), lambda s, r, core, at=at, per_half=per_half:
                                     (s, per_half * core[0] + at(r)[0], at(r)[1])))
        nbytes += 3 * _nbytes((br, bc), F32)
    for g, ((br, bc), at) in zip(gs, cuts):
        in_specs.append(pl.BlockSpec((1, br, bc), lambda s, r, core, at=at: (s,) + at(r)))
        out_specs.append(pl.BlockSpec((1, br, bc), lambda s, r, core, at=at: (s,) + at(r)))
        out_shape.append(jax.ShapeDtypeStruct((g.shape[0], g.shape[1] // 2, g.shape[2]), GRAD_PAYLOAD))
    return pl.pallas_call(
        body, name=name, out_shape=tuple(out_shape),
        grid_spec=pltpu.PrefetchScalarGridSpec(num_scalar_prefetch=1, grid=(N_CHIPS, 2), in_specs=in_specs,
                                               out_specs=tuple(out_specs)),
        compiler_params=_params(("parallel", "parallel"), nbytes),
    )(core, *gs, *gots)


def _chip_sum(ps, landed):
    n = len(ps)
    x, y, c = lax.axis_index("x"), lax.axis_index("y"), lax.axis_index("c")
    where = jnp.stack([2 * x + y, 2 * (1 - x) + y, 2 * x + (1 - y), 2 * (1 - x) + (1 - y), c]).astype(jnp.int32)

    def body(where_ref, *refs):
        del where_ref
        for w in range(n):
            terms = [refs[4 * w + t][...].astype(F32) for t in range(4)]
            refs[4 * n + w][...] = ((terms[0] + terms[1]) + terms[2]) + terms[3]

    in_specs, out_specs, out_shape, args, nbytes = [], [], [], [], 0
    for p, a in zip(ps, landed):
        (br, bc), at = _half_blocks(a.shape[1], a.shape[2])
        blk = (1, br, bc)
        in_specs.append(pl.BlockSpec(blk, lambda r, where, at=at: (where[0],) + at(r)))
        args.append(p)
        for t in (1, 2, 3):
            in_specs.append(pl.BlockSpec(blk, lambda r, where, t=t, at=at: (where[t],) + at(r)))
            args.append(a)
        out_specs.append(pl.BlockSpec(blk, lambda r, where, at=at: (where[4],) + at(r)))
        out_shape.append(jax.ShapeDtypeStruct((2,) + a.shape[1:], F32))
        nbytes += 4 * _nbytes(blk, F32)
    return pl.pallas_call(
        body, name="grad_chip_sum", out_shape=tuple(out_shape),
        grid_spec=pltpu.PrefetchScalarGridSpec(num_scalar_prefetch=1, grid=(2,), in_specs=in_specs,
                                               out_specs=tuple(out_specs)),
        compiler_params=_params(("parallel",), nbytes),
    )(where, *args)


def _join_halves(ss):
    n = len(ss)

    def body(*refs):
        outs, send, recv = refs[n:2 * n], refs[2 * n], refs[2 * n + 1]
        x, y, c, _ = _place()
        cps = []
        for w in range(n):
            cps.append(pltpu.make_async_remote_copy(
                src_ref=outs[w].at[c], dst_ref=outs[w].at[c], send_sem=send.at[w], recv_sem=recv.at[w],
                device_id=(x, y, 1 - c), device_id_type=MESH))
        for cp in cps:
            cp.start()
        for w in range(n):
            got = outs[w].at[1 - c]
            pltpu.make_async_remote_copy(src_ref=got, dst_ref=got, send_sem=send.at[w], recv_sem=recv.at[w],
                                         device_id=(x, y, 1 - c), device_id_type=MESH).wait_recv()
        for cp in cps:
            cp.wait_send()

    dma = lambda k: pltpu.SemaphoreType.DMA((k,))
    return pl.pallas_call(
        body, name="grad_join_halves",
        out_shape=tuple(jax.ShapeDtypeStruct(s.shape, s.dtype) for s in ss),
        in_specs=[_ANY] * n, out_specs=tuple([_ANY] * n), input_output_aliases={w: w for w in range(n)},
        scratch_shapes=[dma(n), dma(n)],
    )(*ss)


def _rot_cols(w, axis=-1):
    a, b = jnp.split(w, 2, axis=axis)
    return jnp.concatenate([-b, a], axis=axis)


def _rot_cols_t(g, axis=-1):
    a, b = jnp.split(g, 2, axis=axis)
    return jnp.concatenate([b, -a], axis=axis)


def _cols_from_chips(a):
    n, r, cs = a.shape
    return jnp.transpose(a, (1, 0, 2)).reshape(r, n * cs)


def _cols_to_chips(a):
    r, cc = a.shape
    return jnp.transpose(a.reshape(r, N_CHIPS, cc // N_CHIPS), (1, 0, 2))


def _conv_w_split(cw):
    return jnp.swapaxes(cw.reshape(3, 2, D_FF), 0, 1)


def _conv_w_join(g):
    return jnp.swapaxes(g, 0, 1).reshape(3, 2 * D_FF)


_SEG =(D_MODEL, 2 * D_MODEL, 2 * D_MODEL + Q_RANK, 2 * D_MODEL + Q_RANK + KV_RANK, 2 * D_MODEL + Q_RANK + KV_RANK + ROPE,
        3 * D_MODEL + Q_RANK + KV_RANK + ROPE)


def _w_in_t_to_pad(wt):
    u, v, cq, ckv, kr, ga, gb = jnp.split(wt, _SEG, axis=0)
    return jnp.concatenate([u, v, ga, gb, cq, ckv, kr, _rot_cols(kr, axis=0)], axis=0)


def _w_in_t_from_pad(gt):
    u, v, ga, gb, cq, ckv, kr, krr = jnp.split(
        gt, (D_MODEL, 2 * D_MODEL, 3 * D_MODEL, 4 * D_MODEL, 4 * D_MODEL + Q_RANK, 4 * D_MODEL + Q_RANK + KV_RANK,
             4 * D_MODEL + Q_RANK + KV_RANK + ROPE), axis=0)
    return jnp.concatenate([u, v, cq, ckv, kr + _rot_cols_t(krr, axis=0), ga, gb], axis=0)


def _w_uq_to_pad(w):
    t = w.reshape(Q_RANK, HEADS, QK_DIM)
    nope, rope = t[..., :NOPE], t[..., NOPE:]
    return jnp.concatenate([nope, rope, _rot_cols(rope)], axis=-1).reshape(Q_RANK, HEADS * HEAD_PAD)


def _w_uq_from_pad(g):
    t = g.reshape(Q_RANK, HEADS, HEAD_PAD)
    nope, rope, rot = t[..., :NOPE], t[..., NOPE:QK_DIM], t[..., QK_DIM:]
    return jnp.concatenate([nope, rope + _rot_cols_t(rot)], axis=-1).reshape(Q_RANK, HEADS * QK_DIM)


def _w_ukv_to_pad(w):
    t = w.reshape(KV_RANK, HEADS, 2, NOPE)
    return jnp.swapaxes(t, 1, 2).reshape(KV_RANK, 2 * HEADS * NOPE)


def _w_ukv_from_pad(g):
    t = g.reshape(KV_RANK, 2, HEADS, NOPE)
    return jnp.swapaxes(t, 1, 2).reshape(KV_RANK, 2 * HEADS * NOPE)


def _rope_tables(positions):
    inv_freq = 1.0 / (ROPE_THETA ** (jnp.arange(0, ROPE, 2, dtype=F32) / ROPE))
    ang = positions.astype(F32).reshape(-1, 1) * inv_freq
    cos, sin = jnp.cos(ang), jnp.sin(ang)
    zero = jnp.zeros((ang.shape[0], 64), F32)
    return jnp.concatenate([cos, cos, zero], axis=1), jnp.concatenate([sin, sin, zero], axis=1)


_BIG = ("w_in", "w_uq", "w_ukv", "w_out", "w_up", "w_down")
UP_SHARD = 2 * D_FF // N_CHIPS


def _local_step(x, positions, tgt, wts, in_weights, mixer_weights, ffn_weights, on_ffn_grads, on_mixer_grads):
    B, S, D = x.shape
    T = B * S
    xf = x.reshape(T, D)
    cos_a, sin_a = _rope_tables(positions)
    bs_t = jnp.pad(wts["a_spatial_b"].T, ((0, 0), (0, 128 - A_GROUPS)))

    h = _rms_fwd(xf, wts["mix_norm"], "norm1_fwd")
    wts = dict(wts)
    wts["w_in"] = in_weights(h)
    z = _mm(h, wts["w_in"], "nt", "in_proj", tm=512, tn=1536, tk=D, n_outer=True)
    wts["w_q"], wts["w_kv"], wts["w_out"] = mixer_weights(z)
    q, k, v, cqn, ckvn = _lat_fwd(z, wts["q_a_norm"], wts["kv_a_norm"], wts["w_q"], wts["w_kv"], cos_a, sin_a)
    yb, lse = _attn_fwd(q, k, v, B, S)
    merged = _mix_fwd(z, yb, wts["a_v_norm_g"], wts["a_v_norm_b"], wts["a_spatial_w"], bs_t)
    x1 = _mm(merged, wts["w_out"], "nn", "out_proj", tm=512, tn=D, tk=D, add=xf)
    h2 = _rms_fwd(x1, wts["ffn_norm"], "norm2_fwd")
    wts["w_up"], wts["w_down"], wts["conv_w"] = ffn_weights(h2)
    up_pre = _mm(h2, wts["w_up"], "nn", "up_proj", tm=512, tn=UP_SHARD, tk=D, dims=(T, 2 * D_FF, D),
                 b_spec=pl.BlockSpec((None, D, UP_SHARD), lambda i, j, k: (j, 0, 0)),
                 o_spec=pl.BlockSpec((None, 512, UP_SHARD), lambda i, j, k: (j // 2, i, j % 2)), out_shape=(2, T, D_FF),
                 n_outer=True)
    act = _gate_fwd(up_pre, wts["conv_w"], wts["conv_b"], B, S)
    x2 = _mm(act, wts["w_down"], "nn", "down_proj", tm=512, tn=D, tk=1408, add=x1)
    dx2, loss_row, g_final = _final(x2, tgt.reshape(T, D), wts["final_norm"])

    g = {"final_norm": g_final}
    dact = _mm(dx2, wts["w_down"], "nt", "down_proj_dx", tm=512, tn=1408, tk=D, n_outer=True)
    tk2, tk1 = min(2048, T), min(1024, T)
    g["w_down"], g["w_down_lo"] = _mm(act, dx2, "tn", "down_proj_dw", tm=1408, tn=D, tk=tk1, copy_dtype=GRAD_PAYLOAD)
    dup, g["conv_w"], g["conv_b"] = _gate_bwd(up_pre, dact, wts["conv_w"], wts["conv_b"], B, S)
    g["w_up"], g["w_up_lo"] = _mm(
        h2, dup, "tn", "up_proj_dw", tm=D, tn=UP_SHARD, tk=tk2, dims=(D, 2 * D_FF, T), copy_dtype=GRAD_PAYLOAD,
        b_spec=pl.BlockSpec((None, tk2, UP_SHARD), lambda i, j, k: (j // 2, k, j % 2)),
        o_spec=pl.BlockSpec((None, D, UP_SHARD), lambda i, j, k: (j, 0, 0)), out_shape=(N_CHIPS, D, UP_SHARD))
    ffn_sent = on_ffn_grads(g)
    dh2 = _mm(dup, wts["w_up"], "nt", "up_proj_dx", tm=512, tn=D, tk=UP_SHARD, dims=(T, D, 2 * D_FF),
              a_spec=pl.BlockSpec((None, 512, UP_SHARD), lambda i, j, k: (k // 2, i, k % 2)),
              b_spec=pl.BlockSpec((None, D, UP_SHARD), lambda i, j, k: (k, 0, 0)))
    token = None if ffn_sent is None else ffn_sent(dh2)
    ffn_norm = wts["ffn_norm"] if token is None else wts["ffn_norm"] + token[0:1, 0:1]
    dx1, g["ffn_norm"] = _rms_bwd(x1, ffn_norm, dh2, dx2, "norm2_bwd")
    dm = _mm(dx1, wts["w_out"], "nt", "out_proj_dx", tm=512, tn=D, tk=D)
    g["w_out"], g["w_out_lo"] = _mm(merged, dx1, "tn", "out_proj_dw", tm=D, tn=D, tk=tk1, copy_dtype=GRAD_PAYLOAD)
    dz, dyb, dl, g["a_spatial_w"], gbs, g["a_v_norm_g"], g["a_v_norm_b"] = _mix_bwd(
        z, yb, dm, wts["a_v_norm_g"], wts["a_v_norm_b"], wts["a_spatial_w"], bs_t)
    g["a_spatial_b"] = gbs[:, :A_GROUPS].T
    delta = dl.reshape(HEADS * T // ATT_BLOCK, 1, ATT_BLOCK)
    dq, dk, dv = _attn_bwd(q, k, v, dyb, lse, delta, B, S)
    dz, dq_raw, dkv, g["q_a_norm"], g["kv_a_norm"] = _lat_bwd(
        dz, z, dq, dk, dv, wts["q_a_norm"], wts["kv_a_norm"], wts["w_q"], wts["w_kv"], cos_a, sin_a)
    g["w_q"], g["w_q_lo"] = _mm(cqn, dq_raw, "tn", "q_proj_dw", tm=Q_RANK, tn=HEADS * HEAD_PAD, tk=tk2,
                                copy_dtype=GRAD_PAYLOAD)
    g["w_kv"], g["w_kv_lo"] = _mm(ckvn, dkv, "tn", "kv_proj_dw", tm=KV_RANK, tn=2 * HEADS * NOPE, tk=tk2,
                                  copy_dtype=GRAD_PAYLOAD)
    g["w_in"], g["w_in_lo"] = _mm(dz, h, "tn", "in_proj_dw", tm=1536, tn=D, tk=tk2,
                                  copy_dtype=GRAD_PAYLOAD)
    token = on_mixer_grads(g)
    mix_norm = wts["mix_norm"] if token is None else wts["mix_norm"] + token[0:1, 0:1]
    dh = _mm(dz, wts["w_in"], "nn", "in_proj_dx", tm=512, tn=D, tk=1536)
    dx, g["mix_norm"] = _rms_bwd(xf, mix_norm, dh, dx1, "norm1_bwd")
    return loss_row[0, 0], dx.reshape(B, S, D), g


_SMALL = (("mix_norm", (1, D_MODEL)), ("a_v_norm_g", (1, D_MODEL)), ("a_v_norm_b", (1, D_MODEL)),
          ("a_spatial_w", (A_GROUPS * CHUNK, CHUNK)), ("a_spatial_b", (1, A_GROUPS * CHUNK)), ("q_a_norm", (1, Q_RANK)),
          ("kv_a_norm", (1, KV_RANK)), ("ffn_norm", (1, D_MODEL)), ("conv_b", (1, 2 * D_FF)), ("final_norm", (1, D_MODEL)),
          ("conv_w", (3, 2 * D_FF)))
_SMALL_SIZE = sum(math.prod(s) for _, s in _SMALL)
_SMALL_ROWS = -(-(_SMALL_SIZE + 1) // (128 * 8)) * 8


def kernel(x, positions, mix_norm, w_in, a_v_norm_g, a_v_norm_b, a_spatial_w, a_spatial_b, q_a_norm, w_uq, kv_a_norm, w_ukv, w_out, ffn_norm, w_up, conv_w, conv_b, w_down, final_norm, loss_target, m_mix_norm, m_w_in, m_a_v_norm_g, m_a_v_norm_b, m_a_spatial_w, m_a_spatial_b, m_q_a_norm, m_w_uq, m_kv_a_norm, m_w_ukv, m_w_out, m_ffn_norm, m_w_up, m_conv_w, m_conv_b, m_w_down, m_final_norm, v_mix_norm, v_w_in, v_a_v_norm_g, v_a_v_norm_b, v_a_spatial_w, v_a_spatial_b, v_q_a_norm, v_w_uq, v_kv_a_norm, v_w_ukv, v_w_out, v_ffn_norm, v_w_up, v_conv_w, v_conv_b, v_w_down, v_final_norm):
    weights = dict(mix_norm=mix_norm, w_in=w_in, a_v_norm_g=a_v_norm_g, a_v_norm_b=a_v_norm_b, a_spatial_w=a_spatial_w,
                   a_spatial_b=a_spatial_b, q_a_norm=q_a_norm, w_uq=w_uq, kv_a_norm=kv_a_norm, w_ukv=w_ukv, w_out=w_out,
                   ffn_norm=ffn_norm, w_up=w_up, conv_w=conv_w, conv_b=conv_b, w_down=w_down, final_norm=final_norm)
    m_in = dict(mix_norm=m_mix_norm, w_in=m_w_in, a_v_norm_g=m_a_v_norm_g, a_v_norm_b=m_a_v_norm_b,
                a_spatial_w=m_a_spatial_w, a_spatial_b=m_a_spatial_b, q_a_norm=m_q_a_norm, w_uq=m_w_uq,
                kv_a_norm=m_kv_a_norm, w_ukv=m_w_ukv, w_out=m_w_out, ffn_norm=m_ffn_norm, w_up=m_w_up, conv_w=m_conv_w,
                conv_b=m_conv_b, w_down=m_w_down, final_norm=m_final_norm)
    v_in = dict(mix_norm=v_mix_norm, w_in=v_w_in, a_v_norm_g=v_a_v_norm_g, a_v_norm_b=v_a_v_norm_b,
                a_spatial_w=v_a_spatial_w, a_spatial_b=v_a_spatial_b, q_a_norm=v_q_a_norm, w_uq=v_w_uq,
                kv_a_norm=v_kv_a_norm, w_ukv=v_w_ukv, w_out=v_w_out, ffn_norm=v_ffn_norm, w_up=v_w_up, conv_w=v_conv_w,
                conv_b=v_conv_b, w_down=v_w_down, final_norm=v_final_norm)
    names = list(weights)
    chip = 2 * lax.axis_index("x") + lax.axis_index("y")

    def halves(a):
        return a.reshape(a.shape[:-2] + (2, a.shape[-2] // 2, a.shape[-1]))

    w_in_t = jnp.swapaxes(w_in[0], 0, 1).astype(MXU_DTYPE)
    w_in_gather = _exchange_start([jnp.stack(jnp.split(w_in_t, 2, axis=1))], "w_in_gather_start", "halves",
                                  after=positions)
    gathers = {}
    wts = dict(
        mix_norm=mix_norm, a_v_norm_g=a_v_norm_g, a_v_norm_b=a_v_norm_b, a_spatial_w=a_spatial_w[0],
        a_spatial_b=a_spatial_b[0], q_a_norm=q_a_norm, kv_a_norm=kv_a_norm, ffn_norm=ffn_norm,
        final_norm=final_norm.reshape(1, D_MODEL), conv_b=conv_b.reshape(2, 1, D_FF))

    def in_weights(after):
        _, landed = _exchange_wait(w_in_gather, "w_in_gather_wait", "halves", after)
        (w_in_sh,) = _forward_halves(list(landed))
        gathers["mixer"] = _exchange_start([weights[n][0].astype(MXU_DTYPE) for n in _BIG[1:4]],
                                           "mixer_gather_start", "gather", after=w_in_sh)
        gathers["ffn"] = _exchange_start([w_up[0].astype(MXU_DTYPE), w_down[0].astype(MXU_DTYPE), conv_w[0]],
                                         "ffn_gather_start", "gather", after=gathers["mixer"][4])
        return _w_in_t_to_pad(jnp.concatenate([w_in_sh[:, 0], w_in_sh[:, 1]], axis=-1).reshape(-1, D_MODEL))

    def mixer_weights(after):
        _, (w_uq_sh, w_ukv_sh, w_out_sh) = _exchange_wait(gathers["mixer"], "mixer_gather_wait", "gather", after)
        return (_w_uq_to_pad(_cols_from_chips(w_uq_sh)), _w_ukv_to_pad(_cols_from_chips(w_ukv_sh)),
                w_out_sh.reshape(D_MODEL, D_MODEL))

    def ffn_weights(after):
        _, (w_up_sh, w_down_sh, cw_all) = _exchange_wait(gathers["ffn"], "ffn_gather_wait", "gather", after)
        return w_up_sh, w_down_sh.reshape(D_FF, D_MODEL), _conv_w_split(_cols_from_chips(cw_all))

    scatters = {}

    def start_scatter(slabs, slabs_lo, tag):
        got = _swap_halves([halves(s) for s in slabs_lo], tag + "_grad_swap_halves")
        sums = _pair_sum(slabs, got, tag + "_grad_pair_sum")
        scatters[tag] = _exchange_start(list(sums), tag + "_scatter_start", "scatter", after=slabs[-1])
        return scatters[tag][4]

    def on_ffn_grads(g):
        token = start_scatter(*[[g["w_up" + lo], g["w_down" + lo].reshape(N_CHIPS, D_FF // N_CHIPS, D_MODEL)]
                                for lo in ("", "_lo")], "ffn")
        return lambda after: token

    def on_mixer_grads(g):
        return start_scatter(*[
            [_w_in_t_from_pad(g["w_in" + lo]).reshape(N_CHIPS, -1, D_MODEL), _cols_to_chips(_w_uq_from_pad(g["w_q" + lo])),
             _cols_to_chips(_w_ukv_from_pad(g["w_kv" + lo])), g["w_out" + lo].reshape(N_CHIPS, D_MODEL // N_CHIPS, D_MODEL)]
            for lo in ("", "_lo")], "mixer")

    loss_part, grad_x, g = _local_step(x, positions, loss_target, wts, in_weights, mixer_weights, ffn_weights,
                                       on_ffn_grads, on_mixer_grads)

    g_small_parts = dict(g)
    g_small_parts["conv_w"] = _conv_w_join(g["conv_w"])
    g_small_parts["conv_b"] = g["conv_b"].reshape(1, 2 * D_FF)
    flat = jnp.concatenate([g_small_parts[n].reshape(-1) for n, _ in _SMALL] + [loss_part.reshape(1)])
    flat = jnp.pad(flat, (0, _SMALL_ROWS * 128 - flat.shape[0])).reshape(_SMALL_ROWS, 128)
    small_gather = _exchange_start([flat], "small_gather_start", "all", after=grad_x)

    mixer_sums, mixer_landed = _exchange_wait(scatters["mixer"], "mixer_scatter_wait", "scatter", after=small_gather[4])
    ffn_sums, ffn_landed = _exchange_wait(scatters["ffn"], "ffn_scatter_wait", "scatter", after=mixer_landed[0])
    reduced = _chip_sum(list(mixer_sums) + list(ffn_sums), list(mixer_landed) + list(ffn_landed))
    g_big = dict(zip(_BIG, _join_halves(reduced)))

    grads, deltas, new_m, new_v = {}, {}, {}, {}

    def update(n, grad):
        w = weights[n]
        shape2 = grad.shape
        d, nm, nv = _adamw(w.reshape(shape2), grad, m_in[n].reshape(shape2), v_in[n].reshape(shape2), "adamw_" + n)
        grads[n], deltas[n], new_m[n], new_v[n] = (t.reshape(w.shape) for t in (grad, d, nm, nv))

    def update_transposed(n, grad_t):
        t = lambda a: jnp.swapaxes(a, 1, 2)
        d, nm, nv = _adamw(t(weights[n]), grad_t, t(m_in[n]), t(v_in[n]), "adamw_" + n)
        grads[n], deltas[n], new_m[n], new_v[n] = t(grad_t), t(d), t(nm), t(nv)

    for n in _BIG:
        g3 = g_big[n].reshape((1, -1, g_big[n].shape[-1]))
        if n == "w_in":
            update_transposed(n, g3)
        else:
            update(n, g3)

    (own,), (everyone,) = _exchange_wait(small_gather, "small_gather_wait", "all", after=deltas["w_up"])
    device = 2 * chip + lax.axis_index("c")
    everyone = lax.dynamic_update_slice(everyone, own[None], (device, 0, 0))
    total = _sum_slabs([everyone[j] for j in range(8)], "small_grads_sum", tr=_SMALL_ROWS).reshape(-1)
    o = 0
    for n, shp in _SMALL:
        piece = total[o:o + math.prod(shp)].reshape(shp)
        o += math.prod(shp)
        if n == "conv_w":
            piece = lax.dynamic_slice_in_dim(piece, chip * UP_SHARD, UP_SHARD, axis=1)
        update(n, piece)
    loss = total[_SMALL_SIZE]
    return (loss, grad_x, *[grads[n] for n in names], *[deltas[n] for n in names], *[new_m[n] for n in names],
            *[new_v[n] for n in names])
```

```python
import functools
import math

import jax
import jax.numpy as jnp
from jax import lax
from jax.experimental import pallas as pl
from jax.experimental.pallas import tpu as pltpu

F32 = jnp.float32
MXU_DTYPE = jnp.bfloat16
MESH = pl.DeviceIdType.MESH

D_MODEL = 1024
EPS = 1e-6
A_GROUPS = 8
CHUNK = 128
HEADS = 8
NOPE = 128
ROPE = 64
QK_DIM = NOPE + ROPE
HEAD_PAD = 256
Q_RANK = 256
KV_RANK = 128
ROPE_THETA = 10000.0
D_FF = 2816
FF_TILE = 256
N_FF_TILES = D_FF // FF_TILE
LAT = 512
IN_PAD = 4 * D_MODEL + LAT
N_CHIPS = 4
ADAM_LR, ADAM_B1, ADAM_B2, ADAM_EPS, ADAM_WD, ADAM_STEP = 0.001, 0.9, 0.999, 1e-08, 0.01, 10

VMEM_CAP_V7X = 64 * 1024 * 1024
NEG = -1e30


def _params(sem, nbytes):
    limit = int(min(VMEM_CAP_V7X - (8 << 20), max(32 << 20, 3 * nbytes)))
    return pltpu.CompilerParams(dimension_semantics=sem, vmem_limit_bytes=limit)


def _nbytes(shape, dtype):
    return math.prod(shape) * jnp.dtype(dtype).itemsize


_DIMS = {"nn": (((1,), (0,)), ((), ())), "nt": (((1,), (1,)), ((), ())), "tn": (((0,), (0,)), ((), ()))}


def _mm(a, b, mode, name, *, tm, tn, tk, out_dtype=F32, add=None, dims=None, a_spec=None, b_spec=None,
        o_spec=None, out_shape=None, n_outer=False, copy_dtype=None):
    if dims is None:
        if mode == "nn":
            (M, K), (_, N) = a.shape, b.shape
        elif mode == "nt":
            (M, K), (N, _) = a.shape, b.shape
        else:
            (K, M), (_, N) = a.shape, b.shape
    else:
        M, N, K = dims
    a_blk = (tk, tm) if mode == "tn" else (tm, tk)
    b_blk = (tn, tk) if mode == "nt" else (tk, tn)
    if a_spec is None:
        a_spec = pl.BlockSpec(a_blk, (lambda i, j, k: (k, i)) if mode == "tn" else (lambda i, j, k: (i, k)))
    if b_spec is None:
        b_spec = pl.BlockSpec(b_blk, (lambda i, j, k: (j, k)) if mode == "nt" else (lambda i, j, k: (k, j)))
    if o_spec is None:
        o_spec = pl.BlockSpec((tm, tn), lambda i, j, k: (i, j))
    if out_shape is None:
        out_shape = (M, N)
    assert M % tm == 0 and N % tn == 0 and K % tk == 0, (name, M, N, K, tm, tn, tk)
    nk = K // tk
    contract = _DIMS[mode]
    has_add = add is not None

    def body(*refs):
        a_ref, b_ref = refs[0], refs[1]
        add_ref = refs[2] if has_add else None
        o_ref = refs[3] if has_add else refs[2]
        copy_ref = (refs[4] if has_add else refs[3]) if copy_dtype is not None else None

        def product():
            return lax.dot_general(a_ref[...].astype(MXU_DTYPE), b_ref[...].astype(MXU_DTYPE), contract,
                                   preferred_element_type=F32)

        def finish(r):
            if has_add:
                r = r + add_ref[...]
            o_ref[...] = r.astype(out_dtype)
            if copy_ref is not None:
                copy_ref[...] = r.astype(copy_dtype)

        if nk == 1:
            finish(product())
            return
        acc = refs[-1]
        k = pl.program_id(2)

        @pl.when(k == 0)
        def _():
            acc[...] = jnp.zeros_like(acc)

        acc[...] += product()

        @pl.when(k == nk - 1)
        def _():
            finish(acc[...])

    in_specs = [a_spec, b_spec]
    args = [a, b]
    nbytes = _nbytes(a_blk, a.dtype) + _nbytes(b_blk, b.dtype) + 3 * _nbytes((tm, tn), F32)
    if has_add:
        in_specs.append(pl.BlockSpec((tm, tn), lambda i, j, k: (i, j)))
        args.append(add)
        nbytes += _nbytes((tm, tn), F32)
    grid = (M // tm, N // tn, nk)
    if n_outer:
        def swapped(spec):
            return pl.BlockSpec(spec.block_shape, lambda j, i, k, at=spec.index_map: at(i, j, k))

        grid = (N // tn, M // tm, nk)
        in_specs = [swapped(s) for s in in_specs]
        o_spec = swapped(o_spec)
    out_sds, out_specs = jax.ShapeDtypeStruct(out_shape, out_dtype), o_spec
    if copy_dtype is not None:
        out_sds, out_specs = (out_sds, jax.ShapeDtypeStruct(out_shape, copy_dtype)), (o_spec, o_spec)
    return pl.pallas_call(
        body, name=name, out_shape=out_sds, grid=grid, in_specs=in_specs, out_specs=out_specs,
        scratch_shapes=[pltpu.VMEM((tm, tn), F32)] if nk > 1 else [],
        compiler_params=_params(("parallel", "parallel", "arbitrary"), nbytes),
    )(*args)


_GELU_C = math.sqrt(2.0 / math.pi)
_GELU_A = 0.044715


def _sigmoid(x):
    return 0.5 * jnp.tanh(0.5 * x) + 0.5


def _gelu(x):
    t = jnp.tanh(x * (_GELU_C + (_GELU_C * _GELU_A) * (x * x)))
    return x * (0.5 + 0.5 * t)


def _gelu_and_grad(x):
    x2 = x * x
    t = jnp.tanh(x * (_GELU_C + (_GELU_C * _GELU_A) * x2))
    cdf = 0.5 + 0.5 * t
    grad = cdf + (0.5 * x) * (1.0 - t * t) * (_GELU_C + (3.0 * _GELU_C * _GELU_A) * x2)
    return x * cdf, grad


def _rope_mix(g, cos_a, sin_a):
    return g * cos_a + pltpu.roll(g, 64, 1) * sin_a


def _rope_mix_bwd(d, cos_a, sin_a):
    return d * cos_a + pltpu.roll(d * sin_a, 64, 1)


def _rms_fwd(x, g, name, tr=512):
    T, D = x.shape

    def body(x_ref, g_ref, h_ref):
        xv = x_ref[...]
        r = lax.rsqrt(jnp.mean(xv * xv, axis=-1, keepdims=True) + EPS)
        h_ref[...] = ((xv * r) * g_ref[...]).astype(h_ref.dtype)

    return pl.pallas_call(
        body, name=name, out_shape=jax.ShapeDtypeStruct((T, D), MXU_DTYPE), grid=(T // tr,),
        in_specs=[pl.BlockSpec((tr, D), lambda i: (i, 0)), pl.BlockSpec((1, D), lambda i: (0, 0))],
        out_specs=pl.BlockSpec((tr, D), lambda i: (i, 0)),
        compiler_params=_params(("parallel",), 3 * _nbytes((tr, D), F32)),
    )(x, g)


def _rms_bwd(x, g, dh, dres, name, tr=512):
    T, D = x.shape

    def body(x_ref, g_ref, dh_ref, dres_ref, dx_ref, gg_ref):
        @pl.when(pl.program_id(0) == 0)
        def _():
            gg_ref[...] = jnp.zeros_like(gg_ref)

        xv = x_ref[...]
        r = lax.rsqrt(jnp.mean(xv * xv, axis=-1, keepdims=True) + EPS)
        xn = xv * r
        dhv = dh_ref[...]
        dxn = dhv * g_ref[...]
        dx_ref[...] = dres_ref[...] + r * (dxn - xn * jnp.mean(dxn * xn, axis=-1, keepdims=True))
        gg_ref[...] += jnp.sum(dhv * xn, axis=0, keepdims=True)

    row = pl.BlockSpec((tr, D), lambda i: (i, 0))
    vec = pl.BlockSpec((1, D), lambda i: (0, 0))
    return pl.pallas_call(
        body, name=name,
        out_shape=(jax.ShapeDtypeStruct((T, D), F32), jax.ShapeDtypeStruct((1, D), F32)),
        grid=(T // tr,), in_specs=[row, vec, row, row], out_specs=(row, vec),
        compiler_params=_params(("arbitrary",), 6 * _nbytes((tr, D), F32)),
    )(x, g, dh, dres)


def _lat_fwd(z, gq, gkv, wq, wkv, cos_a, sin_a, tr=256):
    T = z.shape[0]
    lat_blk = (4 * D_MODEL) // LAT

    def body(z_ref, gq_ref, gkv_ref, wq_ref, wkv_ref, cos_ref, sin_ref, q_ref, k_ref, v_ref, cqn_ref, ckvn_ref):
        zl = z_ref[...]
        cos_v, sin_v = cos_ref[...], sin_ref[...]
        cq = zl[:, :Q_RANK]
        ckv = zl[:, Q_RANK:Q_RANK + KV_RANK]
        krb = zl[:, Q_RANK + KV_RANK:]
        cqn = ((cq * lax.rsqrt(jnp.mean(cq * cq, axis=-1, keepdims=True) + EPS)) * gq_ref[...]).astype(MXU_DTYPE)
        ckvn = ((ckv * lax.rsqrt(jnp.mean(ckv * ckv, axis=-1, keepdims=True) + EPS)) * gkv_ref[...]).astype(MXU_DTYPE)
        cqn_ref[...] = cqn
        ckvn_ref[...] = ckvn
        krr = _rope_mix(krb, cos_v, sin_v).astype(MXU_DTYPE)
        q = jnp.dot(cqn, wq_ref[...], preferred_element_type=F32)
        kv = jnp.dot(ckvn, wkv_ref[...], preferred_element_type=F32)
        for h in range(HEADS):
            o = h * HEAD_PAD
            q_ref[:, o:o + NOPE] = q[:, o:o + NOPE].astype(MXU_DTYPE)
            q_ref[:, o + NOPE:o + HEAD_PAD] = _rope_mix(q[:, o + NOPE:o + HEAD_PAD], cos_v, sin_v).astype(MXU_DTYPE)
            k_ref[:, o:o + NOPE] = kv[:, h * NOPE:(h + 1) * NOPE].astype(MXU_DTYPE)
            k_ref[:, o + NOPE:o + HEAD_PAD] = krr
        v_ref[...] = kv[:, HEADS * NOPE:].astype(MXU_DTYPE)

    def row(w):
        return pl.BlockSpec((tr, w), lambda i: (i, 0))

    def full(a):
        return pl.BlockSpec(a.shape, lambda i: (0, 0))

    return pl.pallas_call(
        body, name="lat_fwd",
        out_shape=(jax.ShapeDtypeStruct((T, HEADS * HEAD_PAD), MXU_DTYPE), jax.ShapeDtypeStruct((T, HEADS * HEAD_PAD), MXU_DTYPE),
                   jax.ShapeDtypeStruct((T, HEADS * NOPE), MXU_DTYPE), jax.ShapeDtypeStruct((T, Q_RANK), MXU_DTYPE),
                   jax.ShapeDtypeStruct((T, KV_RANK), MXU_DTYPE)),
        grid=(T // tr,),
        in_specs=[pl.BlockSpec((tr, LAT), lambda i: (i, lat_blk)), full(gq), full(gkv), full(wq), full(wkv), row(128), row(128)],
        out_specs=(row(HEADS * HEAD_PAD), row(HEADS * HEAD_PAD), row(HEADS * NOPE), row(Q_RANK), row(KV_RANK)),
        compiler_params=_params(("parallel",), 8 * _nbytes((tr, HEADS * HEAD_PAD), F32)),
    )(z, gq, gkv, wq, wkv, cos_a, sin_a)


ATT_BLOCK = 256
_SCALE = QK_DIM ** -0.5


def _causal_mask(n):
    return lax.broadcasted_iota(jnp.int32, (n, n), 1) <= lax.broadcasted_iota(jnp.int32, (n, n), 0)


def _causal_mask_t(n):
    return lax.broadcasted_iota(jnp.int32, (n, n), 0) <= lax.broadcasted_iota(jnp.int32, (n, n), 1)


ATT_HEADS = 4
ATT_HEADS_FWD = 8


def _attn_fwd(q, k, v, B, S):
    tq = ATT_BLOCK
    nq = S // tq
    T = B * S
    hp, groups = ATT_HEADS_FWD, HEADS // ATT_HEADS_FWD

    def body(q_ref, k_ref, v_ref, o_ref, *lse_refs):
        qi = pl.program_id(2)
        qs = [q_ref[:, t * HEAD_PAD:(t + 1) * HEAD_PAD] for t in range(hp)]

        def scores(j, t):
            rows = pl.ds(pl.multiple_of(j * tq, tq), tq)
            return lax.dot_general(k_ref[rows, t * HEAD_PAD:(t + 1) * HEAD_PAD], qs[t], _DIMS["nt"],
                                   preferred_element_type=F32)

        def step(j, carry, last):
            rows = pl.ds(pl.multiple_of(j * tq, tq), tq)
            out = []
            for t in range(hp):
                m, l, acc, st = carry[t]
                st_next = st if last else scores(j + 1, t)
                st = st * _SCALE
                if last:
                    st = jnp.where(_causal_mask_t(tq), st, NEG)
                m_new = jnp.maximum(m, jnp.max(st, axis=0, keepdims=True))
                alpha = jnp.exp(m - m_new)
                p = jnp.exp(st - m_new)
                l = alpha * l + jnp.sum(p, axis=0, keepdims=True)
                acc = alpha * acc + lax.dot_general(v_ref[rows, t * NOPE:(t + 1) * NOPE], p.astype(MXU_DTYPE),
                                                    _DIMS["tn"], preferred_element_type=F32)
                out.append((m_new, l, acc, st_next))
            return tuple(out)

        init = tuple((jnp.full((1, tq), NEG, F32), jnp.zeros((1, tq), F32), jnp.zeros((NOPE, tq), F32), scores(0, t))
                     for t in range(hp))
        carry = lax.fori_loop(0, qi, lambda j, c: step(j, c, False), init)
        carry = step(qi, carry, True)
        for t in range(hp):
            m, l, acc, _ = carry[t]
            o_ref[:, t * NOPE:(t + 1) * NOPE] = (acc / l).T
            lse_refs[t][0] = m + jnp.log(l)

    lse_sds = jax.ShapeDtypeStruct((groups * B * nq, 1, tq), F32)
    lse_spec = pl.BlockSpec((1, 1, tq), lambda b, h, i: ((h * B + b) * nq + i, 0, 0))
    o, *lses = pl.pallas_call(
        body, name="attn_fwd",
        out_shape=(jax.ShapeDtypeStruct((T, HEADS * NOPE), F32),) + (lse_sds,) * hp,
        grid=(B, groups, nq),
        in_specs=[pl.BlockSpec((tq, hp * HEAD_PAD), lambda b, h, i: (b * nq + i, h)),
                  pl.BlockSpec((S, hp * HEAD_PAD), lambda b, h, i: (b, h)),
                  pl.BlockSpec((S, hp * NOPE), lambda b, h, i: (b, h))],
        out_specs=(pl.BlockSpec((tq, hp * NOPE), lambda b, h, i: (b * nq + i, h)),) + (lse_spec,) * hp,
        compiler_params=_params(("parallel", "parallel", "arbitrary"), 4 * hp * _nbytes((S, HEAD_PAD), MXU_DTYPE)),
    )(q, k, v)
    lse = jnp.stack([a.reshape(groups, B * nq, 1, tq) for a in lses], axis=1).reshape(HEADS * B * nq, 1, tq)
    return o, lse


def _attn_bwd(q, k, v, do, lse, delta, B, S):
    tq = ATT_BLOCK
    nq = S // tq
    T = B * S
    hp, groups = ATT_HEADS, HEADS // ATT_HEADS

    def body(q_ref, k_ref, v_ref, do_ref, *refs):
        lse_refs, dl_refs = refs[:hp], refs[hp:2 * hp]
        dq_out, dk_ref, dv_ref, dq_ref = refs[2 * hp:]
        kj = pl.program_id(2)

        @pl.when(kj == 0)
        def _():
            dq_ref[...] = jnp.zeros_like(dq_ref)

        def products(i, t):
            rows = pl.ds(pl.multiple_of(i * tq, tq), tq)
            st = lax.dot_general(k_ref[:, t * HEAD_PAD:(t + 1) * HEAD_PAD], q_ref[rows, t * HEAD_PAD:(t + 1) * HEAD_PAD],
                                 _DIMS["nt"], preferred_element_type=F32)
            dpt = lax.dot_general(v_ref[:, t * NOPE:(t + 1) * NOPE], do_ref[rows, t * NOPE:(t + 1) * NOPE],
                                  _DIMS["nt"], preferred_element_type=F32)
            return st, dpt

        def step(i, carry, masked):
            rows = pl.ds(pl.multiple_of(i * tq, tq), tq)
            nxt = jnp.minimum(i + 1, nq - 1)
            out = []
            for t in range(hp):
                dk, dv, st, dpt = carry[t]
                st_next, dpt_next = products(nxt, t)
                qk_cols = slice(t * HEAD_PAD, (t + 1) * HEAD_PAD)
                v_cols = slice(t * NOPE, (t + 1) * NOPE)
                p = jnp.exp(st * _SCALE - lse_refs[t][i])
                if masked:
                    p = jnp.where(_causal_mask_t(tq), p, 0.0)
                dv = dv + jnp.dot(p.astype(MXU_DTYPE), do_ref[rows, v_cols], preferred_element_type=F32)
                ds = (p * (dpt - dl_refs[t][i]) * _SCALE).astype(MXU_DTYPE)
                dk = dk + jnp.dot(ds, q_ref[rows, qk_cols], preferred_element_type=F32)
                dq_ref[rows, qk_cols] += lax.dot_general(ds, k_ref[:, qk_cols], _DIMS["tn"], preferred_element_type=F32)
                out.append((dk, dv, st_next, dpt_next))
            return tuple(out)

        init = tuple((jnp.zeros((tq, HEAD_PAD), F32), jnp.zeros((tq, NOPE), F32)) + products(kj, t) for t in range(hp))
        carry = step(kj, init, True)
        carry = lax.fori_loop(kj + 1, nq, lambda i, c: step(i, c, False), carry)
        for t in range(hp):
            dk_ref[:, t * HEAD_PAD:(t + 1) * HEAD_PAD] = carry[t][0].astype(dk_ref.dtype)
            dv_ref[:, t * NOPE:(t + 1) * NOPE] = carry[t][1].astype(dv_ref.dtype)

        @pl.when(kj == nq - 1)
        def _():
            dq_out[...] = dq_ref[...].astype(dq_out.dtype)

    seq = lambda w: pl.BlockSpec((S, w), lambda b, h, j: (b, h))
    blk = lambda w: pl.BlockSpec((tq, w), lambda b, h, j: (b * nq + j, h))
    stat_specs = [pl.BlockSpec((nq, 1, tq), lambda b, h, j, t=t: ((h * hp + t) * B + b, 0, 0)) for t in range(hp)]
    return pl.pallas_call(
        body, name="attn_bwd",
        out_shape=(jax.ShapeDtypeStruct((T, HEADS * HEAD_PAD), MXU_DTYPE), jax.ShapeDtypeStruct((T, HEADS * HEAD_PAD), MXU_DTYPE),
                   jax.ShapeDtypeStruct((T, HEADS * NOPE), MXU_DTYPE)),
        grid=(B, groups, nq),
        in_specs=[seq(hp * HEAD_PAD), blk(hp * HEAD_PAD), blk(hp * NOPE), seq(hp * NOPE)] + stat_specs + stat_specs,
        out_specs=(seq(hp * HEAD_PAD), blk(hp * HEAD_PAD), blk(hp * NOPE)),
        scratch_shapes=[pltpu.VMEM((S, hp * HEAD_PAD), F32)],
        compiler_params=_params(("parallel", "parallel", "arbitrary"), 8 * hp * _nbytes((S, HEAD_PAD), F32)),
    )(q, k, v, do, *([lse] * hp), *([delta] * hp))


MIX_ROWS = 256


def _tril_weights(ws_ref, g):
    return jnp.where(_causal_mask(CHUNK), ws_ref[g], 0.0).astype(MXU_DTYPE)


def _layer_norm_stats(va):
    mu = jnp.mean(va, axis=-1, keepdims=True)
    xc = va - mu
    rs = lax.rsqrt(jnp.mean(xc * xc, axis=-1, keepdims=True) + EPS)
    return xc * rs


def _mix_specs(tr):
    zcol = lambda c: pl.BlockSpec((tr, D_MODEL), lambda i, c=c: (i, c))
    row = pl.BlockSpec((tr, D_MODEL), lambda i: (i, 0))
    vec = pl.BlockSpec((1, D_MODEL), lambda i: (0, 0))
    ws = pl.BlockSpec((A_GROUPS, CHUNK, CHUNK), lambda i: (0, 0, 0))
    bs = pl.BlockSpec((CHUNK, 128), lambda i: (0, 0))
    return zcol, row, vec, ws, bs


def _mix_fwd(z, yb, ln_g, ln_b, ws, bs_t):
    T = z.shape[0]
    tr = MIX_ROWS
    zcol, row, vec, ws_spec, bs_spec = _mix_specs(tr)

    def body(zu_ref, zv_ref, zga_ref, zgb_ref, yb_ref, g_ref, b_ref, ws_ref, bs_ref, out_ref, vn_s):
        vhat = _layer_norm_stats(_gelu(zv_ref[...]))
        vn_s[...] = (vhat * g_ref[...] + b_ref[...]).astype(MXU_DTYPE)
        for g in range(A_GROUPS):
            w = _tril_weights(ws_ref, g)
            bias = bs_ref[:, g:g + 1]
            cols = slice(g * CHUNK, (g + 1) * CHUNK)
            for c in range(tr // CHUNK):
                rows = slice(c * CHUNK, (c + 1) * CHUNK)
                mixed = jnp.dot(w, vn_s[rows, cols], preferred_element_type=F32) + bias
                ya = _gelu(zu_ref[rows, cols]) * mixed
                merged = _sigmoid(zga_ref[rows, cols]) * ya + _sigmoid(zgb_ref[rows, cols]) * yb_ref[rows, cols]
                out_ref[rows, cols] = merged.astype(MXU_DTYPE)

    return pl.pallas_call(
        body, name="mix_fwd", out_shape=jax.ShapeDtypeStruct((T, D_MODEL), MXU_DTYPE), grid=(T // tr,),
        in_specs=[zcol(0), zcol(1), zcol(2), zcol(3), row, vec, vec, ws_spec, bs_spec], out_specs=row,
        scratch_shapes=[pltpu.VMEM((tr, D_MODEL), MXU_DTYPE)],
        compiler_params=_params(("parallel",), 8 * _nbytes((tr, D_MODEL), F32)),
    )(z, z, z, z, yb, ln_g, ln_b, ws, bs_t)


def _mix_bwd(z, yb, dm, ln_g, ln_b, ws, bs_t):
    T = z.shape[0]
    tr = MIX_ROWS
    zcol, row, vec, ws_spec, bs_spec = _mix_specs(tr)

    def body(zu_ref, zv_ref, zga_ref, zgb_ref, yb_ref, dm_ref, g_ref, b_ref, ws_ref, bs_ref,
             dz_ref, dyb_ref, dl_ref, gws_ref, gbs_ref, glg_ref, glb_ref, vn_s, dvn_s):
        @pl.when(pl.program_id(0) == 0)
        def _():
            gws_ref[...] = jnp.zeros_like(gws_ref)
            gbs_ref[...] = jnp.zeros_like(gbs_ref)
            glg_ref[...] = jnp.zeros_like(glg_ref)
            glb_ref[...] = jnp.zeros_like(glb_ref)

        lane = lax.broadcasted_iota(jnp.int32, (CHUNK, 128), 1)
        va, dgelu_v = _gelu_and_grad(zv_ref[...])
        mu = jnp.mean(va, axis=-1, keepdims=True)
        xc = va - mu
        rs = lax.rsqrt(jnp.mean(xc * xc, axis=-1, keepdims=True) + EPS)
        vhat = xc * rs
        vn_s[...] = (vhat * g_ref[...] + b_ref[...]).astype(MXU_DTYPE)
        gbs_acc = jnp.zeros((CHUNK, 128), F32)
        for g in range(A_GROUPS):
            w = _tril_weights(ws_ref, g)
            bias = bs_ref[:, g:g + 1]
            cols = slice(g * CHUNK, (g + 1) * CHUNK)
            gw_acc = jnp.zeros((CHUNK, CHUNK), F32)
            for c in range(tr // CHUNK):
                rows = slice(c * CHUNK, (c + 1) * CHUNK)
                vn = vn_s[rows, cols]
                mixed = jnp.dot(w, vn, preferred_element_type=F32) + bias
                ua, dgelu_u = _gelu_and_grad(zu_ref[rows, cols])
                dmv = dm_ref[rows, cols]
                sa = _sigmoid(zga_ref[rows, cols])
                dya = dmv * sa
                dz_ref[rows, 2 * D_MODEL + g * CHUNK:2 * D_MODEL + (g + 1) * CHUNK] = (
                    dmv * (ua * mixed) * (sa * (1.0 - sa))).astype(dz_ref.dtype)
                dz_ref[rows, cols] = (dya * mixed * dgelu_u).astype(dz_ref.dtype)
                dmix = dya * ua
                gbs_acc = gbs_acc + jnp.where(lane == g, jnp.sum(dmix, axis=-1, keepdims=True), 0.0)
                dmix_b = dmix.astype(MXU_DTYPE)
                gw_acc = gw_acc + lax.dot_general(dmix_b, vn, _DIMS["nt"], preferred_element_type=F32)
                dvn_s[rows, cols] = lax.dot_general(w, dmix_b, _DIMS["tn"], preferred_element_type=F32)
            gws_ref[g] += jnp.where(_causal_mask(CHUNK), gw_acc, 0.0)
        gbs_ref[...] += gbs_acc

        dvn = dvn_s[...]
        glg_ref[...] += jnp.sum(dvn * vhat, axis=0, keepdims=True)
        glb_ref[...] += jnp.sum(dvn, axis=0, keepdims=True)
        dvh = dvn * g_ref[...]
        dva = rs * (dvh - jnp.mean(dvh, axis=-1, keepdims=True) - vhat * jnp.mean(dvh * vhat, axis=-1, keepdims=True))
        dz_ref[:, D_MODEL:2 * D_MODEL] = (dva * dgelu_v).astype(dz_ref.dtype)

        dmv = dm_ref[...]
        ybv = yb_ref[...]
        sb = _sigmoid(zgb_ref[...])
        dyb = dmv * sb
        dyb_ref[...] = dyb.astype(dyb_ref.dtype)
        dz_ref[:, 3 * D_MODEL:4 * D_MODEL] = (dmv * ybv * (sb * (1.0 - sb))).astype(dz_ref.dtype)
        dz_ref[:, 4 * D_MODEL:] = jnp.zeros((tr, LAT), dz_ref.dtype)
        prod = dyb * ybv
        sel = (lax.broadcasted_iota(jnp.int32, (HEADS, D_MODEL), 1) // NOPE
               == lax.broadcasted_iota(jnp.int32, (HEADS, D_MODEL), 0)).astype(jnp.bfloat16)
        hi = prod.astype(jnp.bfloat16)
        rest = prod - hi.astype(F32)
        mid = rest.astype(jnp.bfloat16)
        lo = (rest - mid.astype(F32)).astype(jnp.bfloat16)
        dl_ref[...] = (lax.dot_general(sel, hi, _DIMS["nt"], preferred_element_type=F32)
                       + lax.dot_general(sel, mid, _DIMS["nt"], preferred_element_type=F32)
                       + lax.dot_general(sel, lo, _DIMS["nt"], preferred_element_type=F32))

    return pl.pallas_call(
        body, name="mix_bwd",
        out_shape=(jax.ShapeDtypeStruct((T, IN_PAD), MXU_DTYPE), jax.ShapeDtypeStruct((T, D_MODEL), MXU_DTYPE),
                   jax.ShapeDtypeStruct((HEADS, T), F32), jax.ShapeDtypeStruct((A_GROUPS, CHUNK, CHUNK), F32),
                   jax.ShapeDtypeStruct((CHUNK, 128), F32), jax.ShapeDtypeStruct((1, D_MODEL), F32),
                   jax.ShapeDtypeStruct((1, D_MODEL), F32)),
        grid=(T // tr,),
        in_specs=[zcol(0), zcol(1), zcol(2), zcol(3), row, row, vec, vec, ws_spec, bs_spec],
        out_specs=(pl.BlockSpec((tr, IN_PAD), lambda i: (i, 0)), row, pl.BlockSpec((HEADS, tr), lambda i: (0, i)),
                   ws_spec, bs_spec, vec, vec),
        scratch_shapes=[pltpu.VMEM((tr, D_MODEL), MXU_DTYPE), pltpu.VMEM((tr, D_MODEL), F32)],
        compiler_params=_params(("arbitrary",), 12 * _nbytes((tr, D_MODEL), F32)),
    )(z, z, z, z, yb, dm, ln_g, ln_b, ws, bs_t)


def _lat_bwd(dz, z, dq, dk, dv, gq, gkv, wq, wkv, cos_a, sin_a, tr=256):
    T = z.shape[0]
    lat_blk = (4 * D_MODEL) // LAT

    def body(dz_in, z_ref, dq_ref, dk_ref, dv_ref, gq_ref, gkv_ref, wq_ref, wkv_ref, cos_ref, sin_ref,
             dz_ref, dqr_ref, dkv_ref, ggq_ref, ggkv_ref):
        del dz_in

        @pl.when(pl.program_id(0) == 0)
        def _():
            ggq_ref[...] = jnp.zeros_like(ggq_ref)
            ggkv_ref[...] = jnp.zeros_like(ggkv_ref)

        cos_v, sin_v = cos_ref[...], sin_ref[...]
        dkr = jnp.zeros((tr, 128), F32)
        for h in range(HEADS):
            o = h * HEAD_PAD
            dqr_ref[:, o:o + NOPE] = dq_ref[:, o:o + NOPE].astype(MXU_DTYPE)
            dqr_ref[:, o + NOPE:o + HEAD_PAD] = _rope_mix_bwd(dq_ref[:, o + NOPE:o + HEAD_PAD], cos_v, sin_v).astype(MXU_DTYPE)
            dkv_ref[:, h * NOPE:(h + 1) * NOPE] = dk_ref[:, o:o + NOPE].astype(MXU_DTYPE)
            dkr = dkr + _rope_mix_bwd(dk_ref[:, o + NOPE:o + HEAD_PAD], cos_v, sin_v)
        dkv_ref[:, HEADS * NOPE:] = dv_ref[...]
        dcqn = lax.dot_general(dqr_ref[...], wq_ref[...], _DIMS["nt"], preferred_element_type=F32)
        dckvn = lax.dot_general(dkv_ref[...], wkv_ref[...], _DIMS["nt"], preferred_element_type=F32)

        zl = z_ref[...]

        def rms_bwd(c, dn, g_ref, gg_ref):
            r = lax.rsqrt(jnp.mean(c * c, axis=-1, keepdims=True) + EPS)
            ch = c * r
            gg_ref[...] += jnp.sum(dn * ch, axis=0, keepdims=True)
            dch = dn * g_ref[...]
            return r * (dch - ch * jnp.mean(dch * ch, axis=-1, keepdims=True))

        dz_ref[:, :Q_RANK] = rms_bwd(zl[:, :Q_RANK], dcqn, gq_ref, ggq_ref).astype(dz_ref.dtype)
        dz_ref[:, Q_RANK:Q_RANK + KV_RANK] = rms_bwd(zl[:, Q_RANK:Q_RANK + KV_RANK], dckvn, gkv_ref, ggkv_ref).astype(dz_ref.dtype)
        dz_ref[:, Q_RANK + KV_RANK:] = dkr.astype(dz_ref.dtype)

    def row(w):
        return pl.BlockSpec((tr, w), lambda i: (i, 0))

    def full(a):
        return pl.BlockSpec(a.shape, lambda i: (0, 0))

    lat = pl.BlockSpec((tr, LAT), lambda i: (i, lat_blk))
    return pl.pallas_call(
        body, name="lat_bwd",
        out_shape=(jax.ShapeDtypeStruct(dz.shape, dz.dtype), jax.ShapeDtypeStruct((T, HEADS * HEAD_PAD), MXU_DTYPE),
                   jax.ShapeDtypeStruct((T, 2 * HEADS * NOPE), MXU_DTYPE), jax.ShapeDtypeStruct(gq.shape, F32),
                   jax.ShapeDtypeStruct(gkv.shape, F32)),
        grid=(T // tr,),
        in_specs=[pl.BlockSpec(memory_space=pl.ANY), lat, row(HEADS * HEAD_PAD), row(HEADS * HEAD_PAD), row(HEADS * NOPE),
                  full(gq), full(gkv), full(wq), full(wkv), row(128), row(128)],
        out_specs=(lat, row(HEADS * HEAD_PAD), row(2 * HEADS * NOPE), full(gq), full(gkv)),
        input_output_aliases={0: 0},
        compiler_params=_params(("arbitrary",), 8 * _nbytes((tr, HEADS * HEAD_PAD), F32)),
    )(dz, z, dq, dk, dv, gq, gkv, wq, wkv, cos_a, sin_a)


GATE_ROWS = 64
HALO = 8


def _taps(ref, half, r, first):
    C = GATE_ROWS
    if first:
        xs = jnp.concatenate([jnp.zeros((HALO, ref.shape[-1]), F32), ref[half, 0:C, :]], axis=0)
    else:
        xs = ref[half, pl.ds(pl.multiple_of(r * C - HALO, HALO), C + HALO), :]
    return xs[HALO:, :], pltpu.roll(xs, 1, 0)[HALO:, :], pltpu.roll(xs, 2, 0)[HALO:, :]


def _conv_taps(taps, cw, cb):
    x0, x1, x2 = taps
    return cb + cw[0:1, :] * x2 + cw[1:2, :] * x1 + cw[2:3, :] * x0


def _fold8(x):
    acc = x[0:8, :]
    for i in range(1, x.shape[0] // 8):
        acc = acc + x[8 * i:8 * (i + 1), :]
    return acc


def _gate_fwd(up3, conv_w, conv_b, B, S):
    T = B * S
    W = FF_TILE
    C = GATE_ROWS

    def body(up_ref, cw_ref, cb_ref, act_ref):
        def chunk(r, first):
            gate = _conv_taps(_taps(up_ref, 0, r, first), cw_ref[0], cb_ref[0])
            val = _conv_taps(_taps(up_ref, 1, r, first), cw_ref[1], cb_ref[1])
            base = 0 if first else pl.multiple_of(r * C, C)
            act_ref[pl.ds(base, C), :] = (gate * _sigmoid(gate) * val).astype(act_ref.dtype)

        chunk(0, True)

        @pl.loop(1, S // C)
        def _(r):
            chunk(r, False)

    return pl.pallas_call(
        body, name="gate_fwd", out_shape=jax.ShapeDtypeStruct((T, D_FF), MXU_DTYPE), grid=(B, N_FF_TILES),
        in_specs=[pl.BlockSpec((2, S, W), lambda b, j: (0, b, j)), pl.BlockSpec((2, 3, W), lambda b, j: (0, 0, j)),
                  pl.BlockSpec((2, 1, W), lambda b, j: (0, 0, j))],
        out_specs=pl.BlockSpec((S, W), lambda b, j: (b, j)),
        compiler_params=_params(("parallel", "parallel"), 6 * _nbytes((S, W), F32)),
    )(up3, conv_w, conv_b)


def _gate_bwd(up3, dact, conv_w, conv_b, B, S):
    T = B * S
    W = FF_TILE
    C = GATE_ROWS

    def body(up_ref, da_ref, cw_ref, cb_ref, dup_ref, gcw_ref, gcb_ref, d_s):
        @pl.when(pl.program_id(1) == 0)
        def _():
            gcw_ref[...] = jnp.zeros_like(gcw_ref)
            gcb_ref[...] = jnp.zeros_like(gcb_ref)

        def chunk(r, first, sums):
            rows = pl.ds(0 if first else pl.multiple_of(r * C, C), C)
            taps = [_taps(up_ref, half, r, first) for half in (0, 1)]
            gate = _conv_taps(taps[0], cw_ref[0], cb_ref[0])
            val = _conv_taps(taps[1], cw_ref[1], cb_ref[1])
            sg = _sigmoid(gate)
            da = da_ref[rows, :]
            d_halves = (da * val * (sg * (1.0 + gate * (1.0 - sg))), da * (gate * sg))
            out = []
            for half, dup in enumerate(d_halves):
                d_s[half, rows, :] = dup
                x0, x1, x2 = taps[half]
                sb, s0, s1, s2 = sums[half]
                out.append((sb + _fold8(dup), s0 + _fold8(dup * x2), s1 + _fold8(dup * x1), s2 + _fold8(dup * x0)))
            return tuple(out)

        zeros = tuple(tuple(jnp.zeros((8, W), F32) for _ in range(4)) for _ in range(2))
        sums = chunk(0, True, zeros)
        sums = lax.fori_loop(1, S // C, lambda r, s: chunk(r, False, s), sums)
        for half in (0, 1):
            sb, s0, s1, s2 = sums[half]
            gcb_ref[half] += jnp.sum(sb, axis=0, keepdims=True)
            gcw_ref[half, 0:1, :] += jnp.sum(s0, axis=0, keepdims=True)
            gcw_ref[half, 1:2, :] += jnp.sum(s1, axis=0, keepdims=True)
            gcw_ref[half, 2:3, :] += jnp.sum(s2, axis=0, keepdims=True)

        d_s[:, S:S + HALO, :] = jnp.zeros((2, HALO, W), F32)

        @pl.loop(0, S // C)
        def _(r):
            base = pl.multiple_of(r * C, C)
            for half in (0, 1):
                ds_ = d_s[half, pl.ds(base, C + HALO), :]
                cw = cw_ref[half]
                dx = (cw[2:3, :] * ds_[:C, :] + cw[1:2, :] * pltpu.roll(ds_, C + HALO - 1, 0)[:C, :]
                      + cw[0:1, :] * pltpu.roll(ds_, C + HALO - 2, 0)[:C, :])
                dup_ref[half, pl.ds(base, C), :] = dx.astype(dup_ref.dtype)

    up_spec = pl.BlockSpec((2, S, W), lambda j, b: (0, b, j))
    cw_spec = pl.BlockSpec((2, 3, W), lambda j, b: (0, 0, j))
    cb_spec = pl.BlockSpec((2, 1, W), lambda j, b: (0, 0, j))
    return pl.pallas_call(
        body, name="gate_bwd",
        out_shape=(jax.ShapeDtypeStruct((2, T, D_FF), MXU_DTYPE), jax.ShapeDtypeStruct((2, 3, D_FF), F32),
                   jax.ShapeDtypeStruct((2, 1, D_FF), F32)),
        grid=(N_FF_TILES, B),
        in_specs=[up_spec, pl.BlockSpec((S, W), lambda j, b: (b, j)), cw_spec, cb_spec],
        out_specs=(up_spec, cw_spec, cb_spec),
        scratch_shapes=[pltpu.VMEM((2, S + HALO, W), F32)],
        compiler_params=_params(("parallel", "arbitrary"), 10 * _nbytes((S, W), F32)),
    )(up3, dact, conv_w, conv_b)


def _final(x2, tgt, g, tr=512):
    T, D = x2.shape

    def body(x_ref, t_ref, g_ref, dx_ref, loss_ref, gg_ref):
        @pl.when(pl.program_id(0) == 0)
        def _():
            loss_ref[...] = jnp.zeros_like(loss_ref)
            gg_ref[...] = jnp.zeros_like(gg_ref)

        xv = x_ref[...]
        gv = g_ref[...]
        r = lax.rsqrt(jnp.mean(xv * xv, axis=-1, keepdims=True) + EPS)
        xn = xv * r
        err = xn * gv - t_ref[...]
        loss_ref[...] += 0.5 * jnp.sum(jnp.mean(err * err, axis=-1, keepdims=True), axis=0, keepdims=True)
        dy = err * (1.0 / D)
        gg_ref[...] += jnp.sum(dy * xn, axis=0, keepdims=True)
        dxn = dy * gv
        dx_ref[...] = r * (dxn - xn * jnp.mean(dxn * xn, axis=-1, keepdims=True))

    row = pl.BlockSpec((tr, D), lambda i: (i, 0))
    vec = pl.BlockSpec((1, D), lambda i: (0, 0))
    return pl.pallas_call(
        body, name="final_loss",
        out_shape=(jax.ShapeDtypeStruct((T, D), F32), jax.ShapeDtypeStruct((1, 128), F32), jax.ShapeDtypeStruct((1, D), F32)),
        grid=(T // tr,), in_specs=[row, row, vec],
        out_specs=(row, pl.BlockSpec((1, 128), lambda i: (0, 0)), vec),
        compiler_params=_params(("arbitrary",), 6 * _nbytes((tr, D), F32)),
    )(x2, tgt, g)


def _sum_slabs(parts, name, tr):
    rows, cols = parts[0].shape
    n = len(parts)

    def body(*refs):
        acc = refs[0][...]
        for r in refs[1:n]:
            acc = acc + r[...]
        refs[n][...] = acc

    blk = pl.BlockSpec((tr, cols), lambda i: (i, 0))
    return pl.pallas_call(
        body, name=name, out_shape=jax.ShapeDtypeStruct((rows, cols), F32), grid=(rows // tr,),
        in_specs=[blk] * n, out_specs=blk,
        compiler_params=_params(("parallel",), (n + 1) * _nbytes((tr, cols), F32)),
    )(*parts)


ADAMW_BLOCK_BYTES = 2400 * 1024


def _adamw(w, g, m, v, name):
    lead = w.ndim == 3
    rows, cols = w.shape[-2:]
    fits = [d for d in range(8, rows + 1, 8) if rows % d == 0 and d * cols * 4 <= ADAMW_BLOCK_BYTES]
    tr = max(fits) if fits else rows
    c1 = 1.0 - ADAM_B1 ** ADAM_STEP
    c2 = 1.0 - ADAM_B2 ** ADAM_STEP

    def body(w_ref, g_ref, m_ref, v_ref, d_ref, nm_ref, nv_ref):
        gv = g_ref[...]
        nm = ADAM_B1 * m_ref[...] + (1.0 - ADAM_B1) * gv
        nv = ADAM_B2 * v_ref[...] + (1.0 - ADAM_B2) * (gv * gv)
        nm_ref[...] = nm
        nv_ref[...] = nv
        d_ref[...] = -ADAM_LR * ((nm / c1) / (jnp.sqrt(nv / c2) + ADAM_EPS) + ADAM_WD * w_ref[...])

    blk = pl.BlockSpec((None, tr, cols), lambda i: (0, i, 0)) if lead else pl.BlockSpec((tr, cols), lambda i: (i, 0))
    sds = jax.ShapeDtypeStruct(w.shape, F32)
    return pl.pallas_call(
        body, name=name, out_shape=(sds, sds, sds), grid=(rows // tr,), in_specs=[blk] * 4, out_specs=(blk, blk, blk),
        compiler_params=_params(("parallel",), 7 * _nbytes((tr, cols), F32)),
    )(w, g, m, v)


_ANY = pl.BlockSpec(memory_space=pl.ANY)


def _place():
    x, y, c = lax.axis_index("x"), lax.axis_index("y"), lax.axis_index("c")
    chips = [(1 - x, y), (x, 1 - y), (1 - x, 1 - y)]
    return x, y, c, chips


def _forward_halves(lands):
    n = len(lands)

    def body(*refs):
        outs, send, recv = refs[n:2 * n], refs[2 * n], refs[2 * n + 1]
        x, y, c, chips = _place()
        cps = []
        for w in range(n):
            for j, (px, py) in enumerate(chips):
                landed = outs[w].at[2 * px + py, c]
                cps.append(pltpu.make_async_remote_copy(
                    src_ref=landed, dst_ref=landed, send_sem=send.at[3 * w + j], recv_sem=recv.at[3 * w + j],
                    device_id=(x, y, 1 - c), device_id_type=MESH))
        for cp in cps:
            cp.start()
        for w in range(n):
            for j, (px, py) in enumerate(chips):
                other = outs[w].at[2 * px + py, 1 - c]
                pltpu.make_async_remote_copy(src_ref=other, dst_ref=other, send_sem=send.at[3 * w + j],
                                             recv_sem=recv.at[3 * w + j], device_id=(x, y, 1 - c),
                                             device_id_type=MESH).wait_recv()
        for cp in cps:
            cp.wait_send()

    dma = lambda k: pltpu.SemaphoreType.DMA((k,))
    return pl.pallas_call(
        body, name="gather_forward_halves", out_shape=tuple(jax.ShapeDtypeStruct(a.shape, a.dtype) for a in lands),
        in_specs=[_ANY] * n, out_specs=tuple([_ANY] * n), input_output_aliases={w: w for w in range(n)},
        scratch_shapes=[dma(3 * n), dma(3 * n)],
    )(*lands)


_HBM = pl.BlockSpec(memory_space=pltpu.HBM)
_SEM = pl.BlockSpec(memory_space=pltpu.SEMAPHORE)
_EFFECT = pltpu.SideEffectType.DATAFLOW_SIDE_EFFECTING


SEMS_PER_ARRAY = 8


def _exchange_copies(srcs, lands, send, recv, mode):
    x, y, c, chips = _place()
    if mode == "halves":
        cps = []
        for w, (src, land) in enumerate(zip(srcs, lands)):
            pieces = [(src.at[c], land.at[2 * x + y, c], (px, py, c)) for px, py in chips]
            pieces.append((src, land.at[2 * x + y], (x, y, 1 - c)))
            for k, (piece, dst, peer) in enumerate(pieces):
                cps.append(pltpu.make_async_remote_copy(
                    src_ref=piece, dst_ref=dst, send_sem=send.at[SEMS_PER_ARRAY * w + k],
                    recv_sem=recv.at[SEMS_PER_ARRAY * w + k], device_id=peer, device_id_type=MESH))
        return cps
    if mode == "all":
        flips = [(fx, fy, fc) for fx in (0, 1) for fy in (0, 1) for fc in (0, 1)][1:]
        peers = [(x ^ fx, y ^ fy, c ^ fc) for fx, fy, fc in flips]
        slot = 4 * x + 2 * y + c
    else:
        peers = [(px, py, c) for px, py in chips] + ([(x, y, 1 - c)] if mode == "gather" else [])
        slot = 2 * x + y
    cps = []
    for w, (src, land) in enumerate(zip(srcs, lands)):
        for k, peer in enumerate(peers):
            piece = src.at[2 * peer[0] + peer[1]] if mode == "scatter" else src
            cps.append(pltpu.make_async_remote_copy(
                src_ref=piece, dst_ref=land.at[slot], send_sem=send.at[SEMS_PER_ARRAY * w + k],
                recv_sem=recv.at[SEMS_PER_ARRAY * w + k], device_id=peer, device_id_type=MESH))
    return cps


def _exchange_start(srcs, name, mode, after):
    n = len(srcs)
    lead = {"gather": (N_CHIPS,), "halves": (N_CHIPS,), "scatter": (), "all": (2 * N_CHIPS,)}[mode]
    land_shapes = [lead + s.shape for s in srcs]

    def body(*refs):
        src_refs, land_refs = refs[:n], refs[n:2 * n]
        send, recv = refs[2 * n + 1], refs[2 * n + 2]
        token = refs[-1]
        for cp in _exchange_copies(src_refs, land_refs, send, recv, mode):
            cp.start()
        token[...] = jnp.zeros_like(token)

    sems = pltpu.SemaphoreType.DMA((SEMS_PER_ARRAY * n,))
    out = pl.pallas_call(
        body, name=name,
        out_shape=(sems, sems, *[pltpu.HBM(s.shape, s.dtype) for s in srcs],
                   *[pltpu.HBM(shp, s.dtype) for shp, s in zip(land_shapes, srcs)], jax.ShapeDtypeStruct((8, 128), F32)),
        in_specs=[_HBM] * (2 * n) + [_ANY],
        out_specs=(_SEM, _SEM, *[_HBM] * (2 * n), pl.BlockSpec(memory_space=pltpu.VMEM)),
        input_output_aliases={i: 2 + i for i in range(2 * n)},
        compiler_params=pltpu.CompilerParams(has_side_effects=_EFFECT),
    )(*[pltpu.with_memory_space_constraint(s, pltpu.HBM) for s in srcs],
      *[pltpu.with_memory_space_constraint(lax.empty(shp, s.dtype), pltpu.HBM) for shp, s in zip(land_shapes, srcs)],
      after)
    return out[0], out[1], out[2:2 + n], out[2 + n:2 + 2 * n], out[-1]


def _exchange_wait(started, name, mode, after):
    send, recv, src_thru, land_thru, _ = started
    n = len(src_thru)

    def body(*refs):
        src_refs, land_refs, send_ref, recv_ref = refs[:n], refs[n:2 * n], refs[2 * n], refs[2 * n + 1]
        for cp in _exchange_copies(src_refs, land_refs, send_ref, recv_ref, mode):
            cp.wait_send()
            cp.wait_recv()

    out = pl.pallas_call(
        body, name=name,
        out_shape=tuple(pltpu.HBM(a.shape, a.dtype) for a in list(src_thru) + list(land_thru)),
        in_specs=[_HBM] * (2 * n) + [_SEM, _SEM, _ANY], out_specs=tuple([_HBM] * (2 * n)),
        input_output_aliases={i: i for i in range(2 * n)},
        compiler_params=pltpu.CompilerParams(has_side_effects=_EFFECT),
    )(*src_thru, *land_thru, send, recv, after)
    return out[:n], out[n:]


def _swap_halves(gs, name):
    n = len(gs)

    def body(*refs):
        ins, outs, send, recv = refs[:n], refs[n:2 * n], refs[2 * n], refs[2 * n + 1]
        x, y, c, _ = _place()
        cps = []
        for w in range(n):
            cps.append(pltpu.make_async_remote_copy(
                src_ref=ins[w].at[:, 1 - c], dst_ref=outs[w], send_sem=send.at[w], recv_sem=recv.at[w],
                device_id=(x, y, 1 - c), device_id_type=MESH))
        for cp in cps:
            cp.start()
        for cp in cps:
            cp.wait()

    return pl.pallas_call(
        body, name=name,
        out_shape=tuple(jax.ShapeDtypeStruct((g.shape[0],) + g.shape[2:], g.dtype) for g in gs),
        in_specs=[_ANY] * n, out_specs=tuple([_ANY] * n),
        scratch_shapes=[pltpu.SemaphoreType.DMA((n,)), pltpu.SemaphoreType.DMA((n,))],
    )(*gs)


GRAD_PAYLOAD = jnp.bfloat16


def _half_blocks(half_rows, cols):
    if (half_rows // 2) % 16 == 0:
        return (half_rows // 2, cols), (lambda r: (r, 0))
    assert cols % 256 == 0, (half_rows, cols)
    return (half_rows, cols // 2), (lambda r: (0, r))


def _pair_sum(gs, gots, name):
    n = len(gs)
    core = lax.axis_index("c").astype(jnp.int32).reshape(1)

    def body(core_ref, *refs):
        del core_ref
        for w in range(n):
            refs[2 * n + w][...] = (refs[w][...] + refs[n + w][...]).astype(GRAD_PAYLOAD)

    in_specs, out_specs, out_shape, nbytes = [], [], [], 0
    cuts = [_half_blocks(g.shape[1] // 2, g.shape[2]) for g in gs]
    for g, ((br, bc), at) in zip(gs, cuts):
        per_half = (g.shape[1] // 2) // br
        in_specs.append(pl.BlockSpec((1, br, bc), lambda s, r, core, at=at, per_half=per_half:
                                     (s, per_half * core[0] + at(r)[0], at(r)[1])))
        nbytes += 3 * _nbytes((br, bc), F32)
    for g, ((br, bc), at) in zip(gs, cuts):
        in_specs.append(pl.BlockSpec((1, br, bc), lambda s, r, core, at=at: (s,) + at(r)))
        out_specs.append(pl.BlockSpec((1, br, bc), lambda s, r, core, at=at: (s,) + at(r)))
        out_shape.append(jax.ShapeDtypeStruct((g.shape[0], g.shape[1] // 2, g.shape[2]), GRAD_PAYLOAD))
    return pl.pallas_call(
        body, name=name, out_shape=tuple(out_shape),
        grid_spec=pltpu.PrefetchScalarGridSpec(num_scalar_prefetch=1, grid=(N_CHIPS, 2), in_specs=in_specs,
                                               out_specs=tuple(out_specs)),
        compiler_params=_params(("parallel", "parallel"), nbytes),
    )(core, *gs, *gots)


def _chip_sum(ps, landed):
    n = len(ps)
    x, y, c = lax.axis_index("x"), lax.axis_index("y"), lax.axis_index("c")
    where = jnp.stack([2 * x + y, 2 * (1 - x) + y, 2 * x + (1 - y), 2 * (1 - x) + (1 - y), c]).astype(jnp.int32)

    def body(where_ref, *refs):
        del where_ref
        for w in range(n):
            terms = [refs[4 * w + t][...].astype(F32) for t in range(4)]
            refs[4 * n + w][...] = ((terms[0] + terms[1]) + terms[2]) + terms[3]

    in_specs, out_specs, out_shape, args, nbytes = [], [], [], [], 0
    for p, a in zip(ps, landed):
        (br, bc), at = _half_blocks(a.shape[1], a.shape[2])
        blk = (1, br, bc)
        in_specs.append(pl.BlockSpec(blk, lambda r, where, at=at: (where[0],) + at(r)))
        args.append(p)
        for t in (1, 2, 3):
            in_specs.append(pl.BlockSpec(blk, lambda r, where, t=t, at=at: (where[t],) + at(r)))
            args.append(a)
        out_specs.append(pl.BlockSpec(blk, lambda r, where, at=at: (where[4],) + at(r)))
        out_shape.append(jax.ShapeDtypeStruct((2,) + a.shape[1:], F32))
        nbytes += 4 * _nbytes(blk, F32)
    return pl.pallas_call(
        body, name="grad_chip_sum", out_shape=tuple(out_shape),
        grid_spec=pltpu.PrefetchScalarGridSpec(num_scalar_prefetch=1, grid=(2,), in_specs=in_specs,
                                               out_specs=tuple(out_specs)),
        compiler_params=_params(("parallel",), nbytes),
    )(where, *args)


def _join_halves(ss):
    n = len(ss)

    def body(*refs):
        outs, send, recv = refs[n:2 * n], refs[2 * n], refs[2 * n + 1]
        x, y, c, _ = _place()
        cps = []
        for w in range(n):
            cps.append(pltpu.make_async_remote_copy(
                src_ref=outs[w].at[c], dst_ref=outs[w].at[c], send_sem=send.at[w], recv_sem=recv.at[w],
                device_id=(x, y, 1 - c), device_id_type=MESH))
        for cp in cps:
            cp.start()
        for w in range(n):
            got = outs[w].at[1 - c]
            pltpu.make_async_remote_copy(src_ref=got, dst_ref=got, send_sem=send.at[w], recv_sem=recv.at[w],
                                         device_id=(x, y, 1 - c), device_id_type=MESH).wait_recv()
        for cp in cps:
            cp.wait_send()

    dma = lambda k: pltpu.SemaphoreType.DMA((k,))
    return pl.pallas_call(
        body, name="grad_join_halves",
        out_shape=tuple(jax.ShapeDtypeStruct(s.shape, s.dtype) for s in ss),
        in_specs=[_ANY] * n, out_specs=tuple([_ANY] * n), input_output_aliases={w: w for w in range(n)},
        scratch_shapes=[dma(n), dma(n)],
    )(*ss)


def _rot_cols(w, axis=-1):
    a, b = jnp.split(w, 2, axis=axis)
    return jnp.concatenate([-b, a], axis=axis)


def _rot_cols_t(g, axis=-1):
    a, b = jnp.split(g, 2, axis=axis)
    return jnp.concatenate([b, -a], axis=axis)


def _cols_from_chips(a):
    n, r, cs = a.shape
    return jnp.transpose(a, (1, 0, 2)).reshape(r, n * cs)


def _cols_to_chips(a):
    r, cc = a.shape
    return jnp.transpose(a.reshape(r, N_CHIPS, cc // N_CHIPS), (1, 0, 2))


def _conv_w_split(cw):
    return jnp.swapaxes(cw.reshape(3, 2, D_FF), 0, 1)


def _conv_w_join(g):
    return jnp.swapaxes(g, 0, 1).reshape(3, 2 * D_FF)


_SEG =(D_MODEL, 2 * D_MODEL, 2 * D_MODEL + Q_RANK, 2 * D_MODEL + Q_RANK + KV_RANK, 2 * D_MODEL + Q_RANK + KV_RANK + ROPE,
        3 * D_MODEL + Q_RANK + KV_RANK + ROPE)


def _w_in_t_to_pad(wt, zero=None):
    u, v, cq, ckv, kr, ga, gb = jnp.split(wt, _SEG, axis=0)
    rot = _rot_cols(kr, axis=0)
    if zero is not None:
        rot = rot + zero.astype(rot.dtype)
    return jnp.concatenate([u, v, ga, gb, cq, ckv, kr, rot], axis=0)


def _w_in_t_from_pad(gt):
    u, v, ga, gb, cq, ckv, kr, krr = jnp.split(
        gt, (D_MODEL, 2 * D_MODEL, 3 * D_MODEL, 4 * D_MODEL, 4 * D_MODEL + Q_RANK, 4 * D_MODEL + Q_RANK + KV_RANK,
             4 * D_MODEL + Q_RANK + KV_RANK + ROPE), axis=0)
    return jnp.concatenate([u, v, cq, ckv, kr + _rot_cols_t(krr, axis=0), ga, gb], axis=0)


def _w_uq_to_pad(w):
    t = w.reshape(Q_RANK, HEADS, QK_DIM)
    nope, rope = t[..., :NOPE], t[..., NOPE:]
    return jnp.concatenate([nope, rope, _rot_cols(rope)], axis=-1).reshape(Q_RANK, HEADS * HEAD_PAD)


def _w_uq_from_pad(g):
    t = g.reshape(Q_RANK, HEADS, HEAD_PAD)
    nope, rope, rot = t[..., :NOPE], t[..., NOPE:QK_DIM], t[..., QK_DIM:]
    return jnp.concatenate([nope, rope + _rot_cols_t(rot)], axis=-1).reshape(Q_RANK, HEADS * QK_DIM)


def _w_ukv_to_pad(w):
    t = w.reshape(KV_RANK, HEADS, 2, NOPE)
    return jnp.swapaxes(t, 1, 2).reshape(KV_RANK, 2 * HEADS * NOPE)


def _w_ukv_from_pad(g):
    t = g.reshape(KV_RANK, 2, HEADS, NOPE)
    return jnp.swapaxes(t, 1, 2).reshape(KV_RANK, 2 * HEADS * NOPE)


def _rope_tables(positions):
    inv_freq = 1.0 / (ROPE_THETA ** (jnp.arange(0, ROPE, 2, dtype=F32) / ROPE))
    ang = positions.astype(F32).reshape(-1, 1) * inv_freq
    cos, sin = jnp.cos(ang), jnp.sin(ang)
    zero = jnp.zeros((ang.shape[0], 64), F32)
    return jnp.concatenate([cos, cos, zero], axis=1), jnp.concatenate([sin, sin, zero], axis=1)


_BIG = ("w_in", "w_uq", "w_ukv", "w_out", "w_up", "w_down")
UP_SHARD = 2 * D_FF // N_CHIPS


def _local_step(x, positions, tgt, wts, in_weights, mixer_weights, ffn_weights, on_ffn_grads, on_mixer_grads):
    B, S, D = x.shape
    T = B * S
    xf = x.reshape(T, D)
    cos_a, sin_a = _rope_tables(positions)
    bs_t = jnp.pad(wts["a_spatial_b"].T, ((0, 0), (0, 128 - A_GROUPS)))

    h = _rms_fwd(xf, wts["mix_norm"], "norm1_fwd")
    wts = dict(wts)
    wts["w_in"] = in_weights(h)
    z = _mm(h, wts["w_in"], "nt", "in_proj", tm=512, tn=1536, tk=D, n_outer=True)
    wts["w_q"], wts["w_kv"], wts["w_out"] = mixer_weights(z)
    q, k, v, cqn, ckvn = _lat_fwd(z, wts["q_a_norm"], wts["kv_a_norm"], wts["w_q"], wts["w_kv"], cos_a, sin_a)
    yb, lse = _attn_fwd(q, k, v, B, S)
    merged = _mix_fwd(z, yb, wts["a_v_norm_g"], wts["a_v_norm_b"], wts["a_spatial_w"], bs_t)
    x1 = _mm(merged, wts["w_out"], "nn", "out_proj", tm=512, tn=D, tk=D, add=xf)
    h2 = _rms_fwd(x1, wts["ffn_norm"], "norm2_fwd")
    wts["w_up"], wts["w_down"], wts["conv_w"] = ffn_weights(h2)
    up_pre = _mm(h2, wts["w_up"], "nn", "up_proj", tm=512, tn=UP_SHARD, tk=D, dims=(T, 2 * D_FF, D),
                 b_spec=pl.BlockSpec((None, D, UP_SHARD), lambda i, j, k: (j, 0, 0)),
                 o_spec=pl.BlockSpec((None, 512, UP_SHARD), lambda i, j, k: (j // 2, i, j % 2)), out_shape=(2, T, D_FF),
                 n_outer=True)
    act = _gate_fwd(up_pre, wts["conv_w"], wts["conv_b"], B, S)
    x2 = _mm(act, wts["w_down"], "nn", "down_proj", tm=512, tn=D, tk=1408, add=x1)
    dx2, loss_row, g_final = _final(x2, tgt.reshape(T, D), wts["final_norm"])

    g = {"final_norm": g_final}
    dact = _mm(dx2, wts["w_down"], "nt", "down_proj_dx", tm=512, tn=1408, tk=D, n_outer=True)
    tk2, tk1 = min(2048, T), min(1024, T)
    g["w_down"], g["w_down_lo"] = _mm(act, dx2, "tn", "down_proj_dw", tm=1408, tn=D, tk=tk1, copy_dtype=GRAD_PAYLOAD)
    dup, g["conv_w"], g["conv_b"] = _gate_bwd(up_pre, dact, wts["conv_w"], wts["conv_b"], B, S)
    g["w_up"], g["w_up_lo"] = _mm(
        h2, dup, "tn", "up_proj_dw", tm=D, tn=UP_SHARD, tk=tk2, dims=(D, 2 * D_FF, T), copy_dtype=GRAD_PAYLOAD,
        b_spec=pl.BlockSpec((None, tk2, UP_SHARD), lambda i, j, k: (j // 2, k, j % 2)),
        o_spec=pl.BlockSpec((None, D, UP_SHARD), lambda i, j, k: (j, 0, 0)), out_shape=(N_CHIPS, D, UP_SHARD))
    ffn_sent = on_ffn_grads(g)
    dh2 = _mm(dup, wts["w_up"], "nt", "up_proj_dx", tm=512, tn=D, tk=UP_SHARD, dims=(T, D, 2 * D_FF),
              a_spec=pl.BlockSpec((None, 512, UP_SHARD), lambda i, j, k: (k // 2, i, k % 2)),
              b_spec=pl.BlockSpec((None, D, UP_SHARD), lambda i, j, k: (k, 0, 0)))
    token = None if ffn_sent is None else ffn_sent(dh2)
    ffn_norm = wts["ffn_norm"] if token is None else wts["ffn_norm"] + token[0:1, 0:1]
    dx1, g["ffn_norm"] = _rms_bwd(x1, ffn_norm, dh2, dx2, "norm2_bwd")
    dm = _mm(dx1, wts["w_out"], "nt", "out_proj_dx", tm=512, tn=D, tk=D)
    g["w_out"], g["w_out_lo"] = _mm(merged, dx1, "tn", "out_proj_dw", tm=D, tn=D, tk=tk1, copy_dtype=GRAD_PAYLOAD)
    dz, dyb, dl, g["a_spatial_w"], gbs, g["a_v_norm_g"], g["a_v_norm_b"] = _mix_bwd(
        z, yb, dm, wts["a_v_norm_g"], wts["a_v_norm_b"], wts["a_spatial_w"], bs_t)
    g["a_spatial_b"] = gbs[:, :A_GROUPS].T
    delta = dl.reshape(HEADS * T // ATT_BLOCK, 1, ATT_BLOCK)
    dq, dk, dv = _attn_bwd(q, k, v, dyb, lse, delta, B, S)
    dz, dq_raw, dkv, g["q_a_norm"], g["kv_a_norm"] = _lat_bwd(
        dz, z, dq, dk, dv, wts["q_a_norm"], wts["kv_a_norm"], wts["w_q"], wts["w_kv"], cos_a, sin_a)
    g["w_q"], g["w_q_lo"] = _mm(cqn, dq_raw, "tn", "q_proj_dw", tm=Q_RANK, tn=HEADS * HEAD_PAD, tk=tk2,
                                copy_dtype=GRAD_PAYLOAD)
    g["w_kv"], g["w_kv_lo"] = _mm(ckvn, dkv, "tn", "kv_proj_dw", tm=KV_RANK, tn=2 * HEADS * NOPE, tk=tk2,
                                  copy_dtype=GRAD_PAYLOAD)
    g["w_in"], g["w_in_lo"] = _mm(dz, h, "tn", "in_proj_dw", tm=1536, tn=D, tk=tk2,
                                  copy_dtype=GRAD_PAYLOAD)
    token = on_mixer_grads(g)
    mix_norm = wts["mix_norm"] if token is None else wts["mix_norm"] + token[0:1, 0:1]
    dh = _mm(dz, wts["w_in"], "nn", "in_proj_dx", tm=512, tn=D, tk=1536)
    dx, g["mix_norm"] = _rms_bwd(xf, mix_norm, dh, dx1, "norm1_bwd")
    return loss_row[0, 0], dx.reshape(B, S, D), g


_SMALL = (("mix_norm", (1, D_MODEL)), ("a_v_norm_g", (1, D_MODEL)), ("a_v_norm_b", (1, D_MODEL)),
          ("a_spatial_w", (A_GROUPS * CHUNK, CHUNK)), ("a_spatial_b", (1, A_GROUPS * CHUNK)), ("q_a_norm", (1, Q_RANK)),
          ("kv_a_norm", (1, KV_RANK)), ("ffn_norm", (1, D_MODEL)), ("conv_b", (1, 2 * D_FF)), ("final_norm", (1, D_MODEL)),
          ("conv_w", (3, 2 * D_FF)))
_SMALL_SIZE = sum(math.prod(s) for _, s in _SMALL)
_SMALL_ROWS = -(-(_SMALL_SIZE + 1) // (128 * 8)) * 8


def kernel(x, positions, mix_norm, w_in, a_v_norm_g, a_v_norm_b, a_spatial_w, a_spatial_b, q_a_norm, w_uq, kv_a_norm, w_ukv, w_out, ffn_norm, w_up, conv_w, conv_b, w_down, final_norm, loss_target, m_mix_norm, m_w_in, m_a_v_norm_g, m_a_v_norm_b, m_a_spatial_w, m_a_spatial_b, m_q_a_norm, m_w_uq, m_kv_a_norm, m_w_ukv, m_w_out, m_ffn_norm, m_w_up, m_conv_w, m_conv_b, m_w_down, m_final_norm, v_mix_norm, v_w_in, v_a_v_norm_g, v_a_v_norm_b, v_a_spatial_w, v_a_spatial_b, v_q_a_norm, v_w_uq, v_kv_a_norm, v_w_ukv, v_w_out, v_ffn_norm, v_w_up, v_conv_w, v_conv_b, v_w_down, v_final_norm):
    weights = dict(mix_norm=mix_norm, w_in=w_in, a_v_norm_g=a_v_norm_g, a_v_norm_b=a_v_norm_b, a_spatial_w=a_spatial_w,
                   a_spatial_b=a_spatial_b, q_a_norm=q_a_norm, w_uq=w_uq, kv_a_norm=kv_a_norm, w_ukv=w_ukv, w_out=w_out,
                   ffn_norm=ffn_norm, w_up=w_up, conv_w=conv_w, conv_b=conv_b, w_down=w_down, final_norm=final_norm)
    m_in = dict(mix_norm=m_mix_norm, w_in=m_w_in, a_v_norm_g=m_a_v_norm_g, a_v_norm_b=m_a_v_norm_b,
                a_spatial_w=m_a_spatial_w, a_spatial_b=m_a_spatial_b, q_a_norm=m_q_a_norm, w_uq=m_w_uq,
                kv_a_norm=m_kv_a_norm, w_ukv=m_w_ukv, w_out=m_w_out, ffn_norm=m_ffn_norm, w_up=m_w_up, conv_w=m_conv_w,
                conv_b=m_conv_b, w_down=m_w_down, final_norm=m_final_norm)
    v_in = dict(mix_norm=v_mix_norm, w_in=v_w_in, a_v_norm_g=v_a_v_norm_g, a_v_norm_b=v_a_v_norm_b,
                a_spatial_w=v_a_spatial_w, a_spatial_b=v_a_spatial_b, q_a_norm=v_q_a_norm, w_uq=v_w_uq,
                kv_a_norm=v_kv_a_norm, w_ukv=v_w_ukv, w_out=v_w_out, ffn_norm=v_ffn_norm, w_up=v_w_up, conv_w=v_conv_w,
                conv_b=v_conv_b, w_down=v_w_down, final_norm=v_final_norm)
    names = list(weights)
    chip = 2 * lax.axis_index("x") + lax.axis_index("y")

    def halves(a):
        return a.reshape(a.shape[:-2] + (2, a.shape[-2] // 2, a.shape[-1]))

    w_in_t = jnp.swapaxes(w_in[0], 0, 1).astype(MXU_DTYPE)
    w_in_gather = _exchange_start([jnp.stack(jnp.split(w_in_t, 2, axis=1))], "w_in_gather_start", "halves",
                                  after=positions)
    gathers = {}
    wts = dict(
        mix_norm=mix_norm, a_v_norm_g=a_v_norm_g, a_v_norm_b=a_v_norm_b, a_spatial_w=a_spatial_w[0],
        a_spatial_b=a_spatial_b[0], q_a_norm=q_a_norm, kv_a_norm=kv_a_norm, ffn_norm=ffn_norm,
        final_norm=final_norm.reshape(1, D_MODEL), conv_b=conv_b.reshape(2, 1, D_FF))

    def in_weights(after):
        _, landed = _exchange_wait(w_in_gather, "w_in_gather_wait", "halves", after)
        (w_in_sh,) = _forward_halves(list(landed))
        gathers["mixer"] = _exchange_start([weights[n][0].astype(MXU_DTYPE) for n in _BIG[1:4]],
                                           "mixer_gather_start", "gather", after=w_in_sh)
        gathers["ffn"] = _exchange_start([w_up[0].astype(MXU_DTYPE), w_down[0].astype(MXU_DTYPE), conv_w[0]],
                                         "ffn_gather_start", "gather", after=gathers["mixer"][4])
        return _w_in_t_to_pad(jnp.concatenate([w_in_sh[:, 0], w_in_sh[:, 1]], axis=-1).reshape(-1, D_MODEL),
                              zero=gathers["ffn"][4][0, 0])

    def mixer_weights(after):
        _, (w_uq_sh, w_ukv_sh, w_out_sh) = _exchange_wait(gathers["mixer"], "mixer_gather_wait", "gather", after)
        return (_w_uq_to_pad(_cols_from_chips(w_uq_sh)), _w_ukv_to_pad(_cols_from_chips(w_ukv_sh)),
                w_out_sh.reshape(D_MODEL, D_MODEL))

    def ffn_weights(after):
        _, (w_up_sh, w_down_sh, cw_all) = _exchange_wait(gathers["ffn"], "ffn_gather_wait", "gather", after)
        return w_up_sh, w_down_sh.reshape(D_FF, D_MODEL), _conv_w_split(_cols_from_chips(cw_all))

    scatters = {}

    def start_scatter(slabs, slabs_lo, tag):
        got = _swap_halves([halves(s) for s in slabs_lo], tag + "_grad_swap_halves")
        sums = _pair_sum(slabs, got, tag + "_grad_pair_sum")
        scatters[tag] = _exchange_start(list(sums), tag + "_scatter_start", "scatter", after=slabs[-1])
        return scatters[tag][4]

    def on_ffn_grads(g):
        token = start_scatter(*[[g["w_up" + lo], g["w_down" + lo].reshape(N_CHIPS, D_FF // N_CHIPS, D_MODEL)]
                                for lo in ("", "_lo")], "ffn")
        return lambda after: token

    def on_mixer_grads(g):
        return start_scatter(*[
            [_w_in_t_from_pad(g["w_in" + lo]).reshape(N_CHIPS, -1, D_MODEL), _cols_to_chips(_w_uq_from_pad(g["w_q" + lo])),
             _cols_to_chips(_w_ukv_from_pad(g["w_kv" + lo])), g["w_out" + lo].reshape(N_CHIPS, D_MODEL // N_CHIPS, D_MODEL)]
            for lo in ("", "_lo")], "mixer")

    loss_part, grad_x, g = _local_step(x, positions, loss_target, wts, in_weights, mixer_weights, ffn_weights,
                                       on_ffn_grads, on_mixer_grads)

    g_small_parts = dict(g)
    g_small_parts["conv_w"] = _conv_w_join(g["conv_w"])
    g_small_parts["conv_b"] = g["conv_b"].reshape(1, 2 * D_FF)
    flat = jnp.concatenate([g_small_parts[n].reshape(-1) for n, _ in _SMALL] + [loss_part.reshape(1)])
    flat = jnp.pad(flat, (0, _SMALL_ROWS * 128 - flat.shape[0])).reshape(_SMALL_ROWS, 128)
    small_gather = _exchange_start([flat], "small_gather_start", "all", after=grad_x)

    mixer_sums, mixer_landed = _exchange_wait(scatters["mixer"], "mixer_scatter_wait", "scatter", after=small_gather[4])
    ffn_sums, ffn_landed = _exchange_wait(scatters["ffn"], "ffn_scatter_wait", "scatter", after=mixer_landed[0])
    reduced = _chip_sum(list(mixer_sums) + list(ffn_sums), list(mixer_landed) + list(ffn_landed))
    g_big = dict(zip(_BIG, _join_halves(reduced)))

    grads, deltas, new_m, new_v = {}, {}, {}, {}

    def update(n, grad):
        w = weights[n]
        shape2 = grad.shape
        d, nm, nv = _adamw(w.reshape(shape2), grad, m_in[n].reshape(shape2), v_in[n].reshape(shape2), "adamw_" + n)
        grads[n], deltas[n], new_m[n], new_v[n] = (t.reshape(w.shape) for t in (grad, d, nm, nv))

    def update_transposed(n, grad_t):
        t = lambda a: jnp.swapaxes(a, 1, 2)
        d, nm, nv = _adamw(t(weights[n]), grad_t, t(m_in[n]), t(v_in[n]), "adamw_" + n)
        grads[n], deltas[n], new_m[n], new_v[n] = t(grad_t), t(d), t(nm), t(nv)

    for n in _BIG:
        g3 = g_big[n].reshape((1, -1, g_big[n].shape[-1]))
        if n == "w_in":
            update_transposed(n, g3)
        else:
            update(n, g3)

    (own,), (everyone,) = _exchange_wait(small_gather, "small_gather_wait", "all", after=deltas["w_up"])
    device = 2 * chip + lax.axis_index("c")
    everyone = lax.dynamic_update_slice(everyone, own[None], (device, 0, 0))
    total = _sum_slabs([everyone[j] for j in range(8)], "small_grads_sum", tr=_SMALL_ROWS).reshape(-1)
    o = 0
    for n, shp in _SMALL:
        piece = total[o:o + math.prod(shp)].reshape(shp)
        o += math.prod(shp)
        if n == "conv_w":
            piece = lax.dynamic_slice_in_dim(piece, chip * UP_SHARD, UP_SHARD, axis=1)
        update(n, piece)
    loss = total[_SMALL_SIZE]
    return (loss, grad_x, *[grads[n] for n in names], *[deltas[n] for n in names], *[new_m[n] for n in names],
            *[new_v[n] for n in names])
```

```python
import functools
import math

import jax
import jax.numpy as jnp
from jax import lax
from jax.experimental import pallas as pl
from jax.experimental.pallas import tpu as pltpu

F32 = jnp.float32
MXU_DTYPE = jnp.bfloat16
MESH = pl.DeviceIdType.MESH

D_MODEL = 1024
EPS = 1e-6
A_GROUPS = 8
CHUNK = 128
HEADS = 8
NOPE = 128
ROPE = 64
QK_DIM = NOPE + ROPE
HEAD_PAD = 256
Q_RANK = 256
KV_RANK = 128
ROPE_THETA = 10000.0
D_FF = 2816
FF_TILE = 256
N_FF_TILES = D_FF // FF_TILE
LAT = 512
IN_PAD = 4 * D_MODEL + LAT
N_CHIPS = 4
ADAM_LR, ADAM_B1, ADAM_B2, ADAM_EPS, ADAM_WD, ADAM_STEP = 0.001, 0.9, 0.999, 1e-08, 0.01, 10

VMEM_CAP_V7X = 64 * 1024 * 1024
NEG = -1e30


def _params(sem, nbytes):
    limit = int(min(VMEM_CAP_V7X - (8 << 20), max(32 << 20, 3 * nbytes)))
    return pltpu.CompilerParams(dimension_semantics=sem, vmem_limit_bytes=limit)


def _nbytes(shape, dtype):
    return math.prod(shape) * jnp.dtype(dtype).itemsize


_DIMS = {"nn": (((1,), (0,)), ((), ())), "nt": (((1,), (1,)), ((), ())), "tn": (((0,), (0,)), ((), ()))}


def _mm(a, b, mode, name, *, tm, tn, tk, out_dtype=F32, add=None, dims=None, a_spec=None, b_spec=None,
        o_spec=None, out_shape=None, n_outer=False, copy_dtype=None):
    if dims is None:
        if mode == "nn":
            (M, K), (_, N) = a.shape, b.shape
        elif mode == "nt":
            (M, K), (N, _) = a.shape, b.shape
        else:
            (K, M), (_, N) = a.shape, b.shape
    else:
        M, N, K = dims
    a_blk = (tk, tm) if mode == "tn" else (tm, tk)
    b_blk = (tn, tk) if mode == "nt" else (tk, tn)
    if a_spec is None:
        a_spec = pl.BlockSpec(a_blk, (lambda i, j, k: (k, i)) if mode == "tn" else (lambda i, j, k: (i, k)))
    if b_spec is None:
        b_spec = pl.BlockSpec(b_blk, (lambda i, j, k: (j, k)) if mode == "nt" else (lambda i, j, k: (k, j)))
    if o_spec is None:
        o_spec = pl.BlockSpec((tm, tn), lambda i, j, k: (i, j))
    if out_shape is None:
        out_shape = (M, N)
    assert M % tm == 0 and N % tn == 0 and K % tk == 0, (name, M, N, K, tm, tn, tk)
    nk = K // tk
    contract = _DIMS[mode]
    has_add = add is not None

    def body(*refs):
        a_ref, b_ref = refs[0], refs[1]
        add_ref = refs[2] if has_add else None
        o_ref = refs[3] if has_add else refs[2]
        copy_ref = (refs[4] if has_add else refs[3]) if copy_dtype is not None else None

        def product():
            return lax.dot_general(a_ref[...].astype(MXU_DTYPE), b_ref[...].astype(MXU_DTYPE), contract,
                                   preferred_element_type=F32)

        def finish(r):
            if has_add:
                r = r + add_ref[...]
            o_ref[...] = r.astype(out_dtype)
            if copy_ref is not None:
                copy_ref[...] = r.astype(copy_dtype)

        if nk == 1:
            finish(product())
            return
        acc = refs[-1]
        k = pl.program_id(2)

        @pl.when(k == 0)
        def _():
            acc[...] = jnp.zeros_like(acc)

        acc[...] += product()

        @pl.when(k == nk - 1)
        def _():
            finish(acc[...])

    in_specs = [a_spec, b_spec]
    args = [a, b]
    nbytes = _nbytes(a_blk, a.dtype) + _nbytes(b_blk, b.dtype) + 3 * _nbytes((tm, tn), F32)
    if has_add:
        in_specs.append(pl.BlockSpec((tm, tn), lambda i, j, k: (i, j)))
        args.append(add)
        nbytes += _nbytes((tm, tn), F32)
    grid = (M // tm, N // tn, nk)
    if n_outer:
        def swapped(spec):
            return pl.BlockSpec(spec.block_shape, lambda j, i, k, at=spec.index_map: at(i, j, k))

        grid = (N // tn, M // tm, nk)
        in_specs = [swapped(s) for s in in_specs]
        o_spec = swapped(o_spec)
    out_sds, out_specs = jax.ShapeDtypeStruct(out_shape, out_dtype), o_spec
    if copy_dtype is not None:
        out_sds, out_specs = (out_sds, jax.ShapeDtypeStruct(out_shape, copy_dtype)), (o_spec, o_spec)
    return pl.pallas_call(
        body, name=name, out_shape=out_sds, grid=grid, in_specs=in_specs, out_specs=out_specs,
        scratch_shapes=[pltpu.VMEM((tm, tn), F32)] if nk > 1 else [],
        compiler_params=_params(("parallel", "parallel", "arbitrary"), nbytes),
    )(*args)


_GELU_C = math.sqrt(2.0 / math.pi)
_GELU_A = 0.044715


def _sigmoid(x):
    return 0.5 * jnp.tanh(0.5 * x) + 0.5


def _gelu(x):
    t = jnp.tanh(x * (_GELU_C + (_GELU_C * _GELU_A) * (x * x)))
    return x * (0.5 + 0.5 * t)


def _gelu_and_grad(x):
    x2 = x * x
    t = jnp.tanh(x * (_GELU_C + (_GELU_C * _GELU_A) * x2))
    cdf = 0.5 + 0.5 * t
    grad = cdf + (0.5 * x) * (1.0 - t * t) * (_GELU_C + (3.0 * _GELU_C * _GELU_A) * x2)
    return x * cdf, grad


def _rope_mix(g, cos_a, sin_a):
    return g * cos_a + pltpu.roll(g, 64, 1) * sin_a


def _rope_mix_bwd(d, cos_a, sin_a):
    return d * cos_a + pltpu.roll(d * sin_a, 64, 1)


def _rms_fwd(x, g, name, tr=512):
    T, D = x.shape

    def body(x_ref, g_ref, h_ref):
        xv = x_ref[...]
        r = lax.rsqrt(jnp.mean(xv * xv, axis=-1, keepdims=True) + EPS)
        h_ref[...] = ((xv * r) * g_ref[...]).astype(h_ref.dtype)

    return pl.pallas_call(
        body, name=name, out_shape=jax.ShapeDtypeStruct((T, D), MXU_DTYPE), grid=(T // tr,),
        in_specs=[pl.BlockSpec((tr, D), lambda i: (i, 0)), pl.BlockSpec((1, D), lambda i: (0, 0))],
        out_specs=pl.BlockSpec((tr, D), lambda i: (i, 0)),
        compiler_params=_params(("parallel",), 3 * _nbytes((tr, D), F32)),
    )(x, g)


def _rms_bwd(x, g, dh, dres, name, tr=512):
    T, D = x.shape

    def body(x_ref, g_ref, dh_ref, dres_ref, dx_ref, gg_ref):
        @pl.when(pl.program_id(0) == 0)
        def _():
            gg_ref[...] = jnp.zeros_like(gg_ref)

        xv = x_ref[...]
        r = lax.rsqrt(jnp.mean(xv * xv, axis=-1, keepdims=True) + EPS)
        xn = xv * r
        dhv = dh_ref[...]
        dxn = dhv * g_ref[...]
        dx_ref[...] = dres_ref[...] + r * (dxn - xn * jnp.mean(dxn * xn, axis=-1, keepdims=True))
        gg_ref[...] += jnp.sum(dhv * xn, axis=0, keepdims=True)

    row = pl.BlockSpec((tr, D), lambda i: (i, 0))
    vec = pl.BlockSpec((1, D), lambda i: (0, 0))
    return pl.pallas_call(
        body, name=name,
        out_shape=(jax.ShapeDtypeStruct((T, D), F32), jax.ShapeDtypeStruct((1, D), F32)),
        grid=(T // tr,), in_specs=[row, vec, row, row], out_specs=(row, vec),
        compiler_params=_params(("arbitrary",), 6 * _nbytes((tr, D), F32)),
    )(x, g, dh, dres)


def _lat_fwd(z, gq, gkv, wq, wkv, cos_a, sin_a, tr=256):
    T = z.shape[0]
    lat_blk = (4 * D_MODEL) // LAT

    def body(z_ref, gq_ref, gkv_ref, wq_ref, wkv_ref, cos_ref, sin_ref, q_ref, k_ref, v_ref, cqn_ref, ckvn_ref):
        zl = z_ref[...]
        cos_v, sin_v = cos_ref[...], sin_ref[...]
        cq = zl[:, :Q_RANK]
        ckv = zl[:, Q_RANK:Q_RANK + KV_RANK]
        krb = zl[:, Q_RANK + KV_RANK:]
        cqn = ((cq * lax.rsqrt(jnp.mean(cq * cq, axis=-1, keepdims=True) + EPS)) * gq_ref[...]).astype(MXU_DTYPE)
        ckvn = ((ckv * lax.rsqrt(jnp.mean(ckv * ckv, axis=-1, keepdims=True) + EPS)) * gkv_ref[...]).astype(MXU_DTYPE)
        cqn_ref[...] = cqn
        ckvn_ref[...] = ckvn
        krr = _rope_mix(krb, cos_v, sin_v).astype(MXU_DTYPE)
        q = jnp.dot(cqn, wq_ref[...], preferred_element_type=F32)
        kv = jnp.dot(ckvn, wkv_ref[...], preferred_element_type=F32)
        for h in range(HEADS):
            o = h * HEAD_PAD
            q_ref[:, o:o + NOPE] = q[:, o:o + NOPE].astype(MXU_DTYPE)
            q_ref[:, o + NOPE:o + HEAD_PAD] = _rope_mix(q[:, o + NOPE:o + HEAD_PAD], cos_v, sin_v).astype(MXU_DTYPE)
            k_ref[:, o:o + NOPE] = kv[:, h * NOPE:(h + 1) * NOPE].astype(MXU_DTYPE)
            k_ref[:, o + NOPE:o + HEAD_PAD] = krr
        v_ref[...] = kv[:, HEADS * NOPE:].astype(MXU_DTYPE)

    def row(w):
        return pl.BlockSpec((tr, w), lambda i: (i, 0))

    def full(a):
        return pl.BlockSpec(a.shape, lambda i: (0, 0))

    return pl.pallas_call(
        body, name="lat_fwd",
        out_shape=(jax.ShapeDtypeStruct((T, HEADS * HEAD_PAD), MXU_DTYPE), jax.ShapeDtypeStruct((T, HEADS * HEAD_PAD), MXU_DTYPE),
                   jax.ShapeDtypeStruct((T, HEADS * NOPE), MXU_DTYPE), jax.ShapeDtypeStruct((T, Q_RANK), MXU_DTYPE),
                   jax.ShapeDtypeStruct((T, KV_RANK), MXU_DTYPE)),
        grid=(T // tr,),
        in_specs=[pl.BlockSpec((tr, LAT), lambda i: (i, lat_blk)), full(gq), full(gkv), full(wq), full(wkv), row(128), row(128)],
        out_specs=(row(HEADS * HEAD_PAD), row(HEADS * HEAD_PAD), row(HEADS * NOPE), row(Q_RANK), row(KV_RANK)),
        compiler_params=_params(("parallel",), 8 * _nbytes((tr, HEADS * HEAD_PAD), F32)),
    )(z, gq, gkv, wq, wkv, cos_a, sin_a)


ATT_BLOCK = 256
_SCALE = QK_DIM ** -0.5


def _causal_mask(n):
    return lax.broadcasted_iota(jnp.int32, (n, n), 1) <= lax.broadcasted_iota(jnp.int32, (n, n), 0)


def _causal_mask_t(n):
    return lax.broadcasted_iota(jnp.int32, (n, n), 0) <= lax.broadcasted_iota(jnp.int32, (n, n), 1)


ATT_HEADS = 4
ATT_HEADS_FWD = 8


def _attn_fwd(q, k, v, B, S):
    tq = ATT_BLOCK
    nq = S // tq
    T = B * S
    hp, groups = ATT_HEADS_FWD, HEADS // ATT_HEADS_FWD

    def body(q_ref, k_ref, v_ref, o_ref, *lse_refs):
        qi = pl.program_id(2)
        qs = [q_ref[:, t * HEAD_PAD:(t + 1) * HEAD_PAD] for t in range(hp)]

        def scores(j, t):
            rows = pl.ds(pl.multiple_of(j * tq, tq), tq)
            return lax.dot_general(k_ref[rows, t * HEAD_PAD:(t + 1) * HEAD_PAD], qs[t], _DIMS["nt"],
                                   preferred_element_type=F32)

        def step(j, carry, last):
            rows = pl.ds(pl.multiple_of(j * tq, tq), tq)
            out = []
            for t in range(hp):
                m, l, acc, st = carry[t]
                st_next = st if last else scores(j + 1, t)
                st = st * _SCALE
                if last:
                    st = jnp.where(_causal_mask_t(tq), st, NEG)
                m_new = jnp.maximum(m, jnp.max(st, axis=0, keepdims=True))
                alpha = jnp.exp(m - m_new)
                p = jnp.exp(st - m_new)
                l = alpha * l + jnp.sum(p, axis=0, keepdims=True)
                acc = alpha * acc + lax.dot_general(v_ref[rows, t * NOPE:(t + 1) * NOPE], p.astype(MXU_DTYPE),
                                                    _DIMS["tn"], preferred_element_type=F32)
                out.append((m_new, l, acc, st_next))
            return tuple(out)

        init = tuple((jnp.full((1, tq), NEG, F32), jnp.zeros((1, tq), F32), jnp.zeros((NOPE, tq), F32), scores(0, t))
                     for t in range(hp))
        carry = lax.fori_loop(0, qi, lambda j, c: step(j, c, False), init)
        carry = step(qi, carry, True)
        for t in range(hp):
            m, l, acc, _ = carry[t]
            o_ref[:, t * NOPE:(t + 1) * NOPE] = (acc / l).T
            lse_refs[t][0] = m + jnp.log(l)

    lse_sds = jax.ShapeDtypeStruct((groups * B * nq, 1, tq), F32)
    lse_spec = pl.BlockSpec((1, 1, tq), lambda b, h, i: ((h * B + b) * nq + i, 0, 0))
    o, *lses = pl.pallas_call(
        body, name="attn_fwd",
        out_shape=(jax.ShapeDtypeStruct((T, HEADS * NOPE), F32),) + (lse_sds,) * hp,
        grid=(B, groups, nq),
        in_specs=[pl.BlockSpec((tq, hp * HEAD_PAD), lambda b, h, i: (b * nq + i, h)),
                  pl.BlockSpec((S, hp * HEAD_PAD), lambda b, h, i: (b, h)),
                  pl.BlockSpec((S, hp * NOPE), lambda b, h, i: (b, h))],
        out_specs=(pl.BlockSpec((tq, hp * NOPE), lambda b, h, i: (b * nq + i, h)),) + (lse_spec,) * hp,
        compiler_params=_params(("parallel", "parallel", "arbitrary"), 4 * hp * _nbytes((S, HEAD_PAD), MXU_DTYPE)),
    )(q, k, v)
    lse = jnp.stack([a.reshape(groups, B * nq, 1, tq) for a in lses], axis=1).reshape(HEADS * B * nq, 1, tq)
    return o, lse


def _attn_bwd(q, k, v, do, lse, delta, B, S):
    tq = ATT_BLOCK
    nq = S // tq
    T = B * S
    hp, groups = ATT_HEADS, HEADS // ATT_HEADS

    def body(q_ref, k_ref, v_ref, do_ref, *refs):
        lse_refs, dl_refs = refs[:hp], refs[hp:2 * hp]
        dq_out, dk_ref, dv_ref, dq_ref = refs[2 * hp:]
        kj = pl.program_id(2)

        @pl.when(kj == 0)
        def _():
            dq_ref[...] = jnp.zeros_like(dq_ref)

        def products(i, t):
            rows = pl.ds(pl.multiple_of(i * tq, tq), tq)
            st = lax.dot_general(k_ref[:, t * HEAD_PAD:(t + 1) * HEAD_PAD], q_ref[rows, t * HEAD_PAD:(t + 1) * HEAD_PAD],
                                 _DIMS["nt"], preferred_element_type=F32)
            dpt = lax.dot_general(v_ref[:, t * NOPE:(t + 1) * NOPE], do_ref[rows, t * NOPE:(t + 1) * NOPE],
                                  _DIMS["nt"], preferred_element_type=F32)
            return st, dpt

        def step(i, carry, masked):
            rows = pl.ds(pl.multiple_of(i * tq, tq), tq)
            nxt = jnp.minimum(i + 1, nq - 1)
            out = []
            for t in range(hp):
                dk, dv, st, dpt = carry[t]
                st_next, dpt_next = products(nxt, t)
                qk_cols = slice(t * HEAD_PAD, (t + 1) * HEAD_PAD)
                v_cols = slice(t * NOPE, (t + 1) * NOPE)
                p = jnp.exp(st * _SCALE - lse_refs[t][i])
                if masked:
                    p = jnp.where(_causal_mask_t(tq), p, 0.0)
                dv = dv + jnp.dot(p.astype(MXU_DTYPE), do_ref[rows, v_cols], preferred_element_type=F32)
                ds = (p * (dpt - dl_refs[t][i]) * _SCALE).astype(MXU_DTYPE)
                dk = dk + jnp.dot(ds, q_ref[rows, qk_cols], preferred_element_type=F32)
                dq_ref[rows, qk_cols] += lax.dot_general(ds, k_ref[:, qk_cols], _DIMS["tn"], preferred_element_type=F32)
                out.append((dk, dv, st_next, dpt_next))
            return tuple(out)

        init = tuple((jnp.zeros((tq, HEAD_PAD), F32), jnp.zeros((tq, NOPE), F32)) + products(kj, t) for t in range(hp))
        carry = step(kj, init, True)
        carry = lax.fori_loop(kj + 1, nq, lambda i, c: step(i, c, False), carry)
        for t in range(hp):
            dk_ref[:, t * HEAD_PAD:(t + 1) * HEAD_PAD] = carry[t][0].astype(dk_ref.dtype)
            dv_ref[:, t * NOPE:(t + 1) * NOPE] = carry[t][1].astype(dv_ref.dtype)

        @pl.when(kj == nq - 1)
        def _():
            dq_out[...] = dq_ref[...].astype(dq_out.dtype)

    seq = lambda w: pl.BlockSpec((S, w), lambda b, h, j: (b, h))
    blk = lambda w: pl.BlockSpec((tq, w), lambda b, h, j: (b * nq + j, h))
    stat_specs = [pl.BlockSpec((nq, 1, tq), lambda b, h, j, t=t: ((h * hp + t) * B + b, 0, 0)) for t in range(hp)]
    return pl.pallas_call(
        body, name="attn_bwd",
        out_shape=(jax.ShapeDtypeStruct((T, HEADS * HEAD_PAD), MXU_DTYPE), jax.ShapeDtypeStruct((T, HEADS * HEAD_PAD), MXU_DTYPE),
                   jax.ShapeDtypeStruct((T, HEADS * NOPE), MXU_DTYPE)),
        grid=(B, groups, nq),
        in_specs=[seq(hp * HEAD_PAD), blk(hp * HEAD_PAD), blk(hp * NOPE), seq(hp * NOPE)] + stat_specs + stat_specs,
        out_specs=(seq(hp * HEAD_PAD), blk(hp * HEAD_PAD), blk(hp * NOPE)),
        scratch_shapes=[pltpu.VMEM((S, hp * HEAD_PAD), F32)],
        compiler_params=_params(("parallel", "parallel", "arbitrary"), 8 * hp * _nbytes((S, HEAD_PAD), F32)),
    )(q, k, v, do, *([lse] * hp), *([delta] * hp))


MIX_ROWS = 256


def _tril_weights(ws_ref, g):
    return jnp.where(_causal_mask(CHUNK), ws_ref[g], 0.0).astype(MXU_DTYPE)


def _layer_norm_stats(va):
    mu = jnp.mean(va, axis=-1, keepdims=True)
    xc = va - mu
    rs = lax.rsqrt(jnp.mean(xc * xc, axis=-1, keepdims=True) + EPS)
    return xc * rs


def _mix_specs(tr):
    zcol = lambda c: pl.BlockSpec((tr, D_MODEL), lambda i, c=c: (i, c))
    row = pl.BlockSpec((tr, D_MODEL), lambda i: (i, 0))
    vec = pl.BlockSpec((1, D_MODEL), lambda i: (0, 0))
    ws = pl.BlockSpec((A_GROUPS, CHUNK, CHUNK), lambda i: (0, 0, 0))
    bs = pl.BlockSpec((CHUNK, 128), lambda i: (0, 0))
    return zcol, row, vec, ws, bs


def _mix_fwd(z, yb, ln_g, ln_b, ws, bs_t):
    T = z.shape[0]
    tr = MIX_ROWS
    zcol, row, vec, ws_spec, bs_spec = _mix_specs(tr)

    def body(zu_ref, zv_ref, zga_ref, zgb_ref, yb_ref, g_ref, b_ref, ws_ref, bs_ref, out_ref, vn_s):
        vhat = _layer_norm_stats(_gelu(zv_ref[...]))
        vn_s[...] = (vhat * g_ref[...] + b_ref[...]).astype(MXU_DTYPE)
        for g in range(A_GROUPS):
            w = _tril_weights(ws_ref, g)
            bias = bs_ref[:, g:g + 1]
            cols = slice(g * CHUNK, (g + 1) * CHUNK)
            for c in range(tr // CHUNK):
                rows = slice(c * CHUNK, (c + 1) * CHUNK)
                mixed = jnp.dot(w, vn_s[rows, cols], preferred_element_type=F32) + bias
                ya = _gelu(zu_ref[rows, cols]) * mixed
                merged = _sigmoid(zga_ref[rows, cols]) * ya + _sigmoid(zgb_ref[rows, cols]) * yb_ref[rows, cols]
                out_ref[rows, cols] = merged.astype(MXU_DTYPE)

    return pl.pallas_call(
        body, name="mix_fwd", out_shape=jax.ShapeDtypeStruct((T, D_MODEL), MXU_DTYPE), grid=(T // tr,),
        in_specs=[zcol(0), zcol(1), zcol(2), zcol(3), row, vec, vec, ws_spec, bs_spec], out_specs=row,
        scratch_shapes=[pltpu.VMEM((tr, D_MODEL), MXU_DTYPE)],
        compiler_params=_params(("parallel",), 8 * _nbytes((tr, D_MODEL), F32)),
    )(z, z, z, z, yb, ln_g, ln_b, ws, bs_t)


def _mix_bwd(z, yb, dm, ln_g, ln_b, ws, bs_t):
    T = z.shape[0]
    tr = MIX_ROWS
    zcol, row, vec, ws_spec, bs_spec = _mix_specs(tr)

    def body(zu_ref, zv_ref, zga_ref, zgb_ref, yb_ref, dm_ref, g_ref, b_ref, ws_ref, bs_ref,
             dz_ref, dyb_ref, dl_ref, gws_ref, gbs_ref, glg_ref, glb_ref, vn_s, dvn_s):
        @pl.when(pl.program_id(0) == 0)
        def _():
            gws_ref[...] = jnp.zeros_like(gws_ref)
            gbs_ref[...] = jnp.zeros_like(gbs_ref)
            glg_ref[...] = jnp.zeros_like(glg_ref)
            glb_ref[...] = jnp.zeros_like(glb_ref)

        lane = lax.broadcasted_iota(jnp.int32, (CHUNK, 128), 1)
        va, dgelu_v = _gelu_and_grad(zv_ref[...])
        mu = jnp.mean(va, axis=-1, keepdims=True)
        xc = va - mu
        rs = lax.rsqrt(jnp.mean(xc * xc, axis=-1, keepdims=True) + EPS)
        vhat = xc * rs
        vn_s[...] = (vhat * g_ref[...] + b_ref[...]).astype(MXU_DTYPE)
        gbs_acc = jnp.zeros((CHUNK, 128), F32)
        for g in range(A_GROUPS):
            w = _tril_weights(ws_ref, g)
            bias = bs_ref[:, g:g + 1]
            cols = slice(g * CHUNK, (g + 1) * CHUNK)
            gw_acc = jnp.zeros((CHUNK, CHUNK), F32)
            for c in range(tr // CHUNK):
                rows = slice(c * CHUNK, (c + 1) * CHUNK)
                vn = vn_s[rows, cols]
                mixed = jnp.dot(w, vn, preferred_element_type=F32) + bias
                ua, dgelu_u = _gelu_and_grad(zu_ref[rows, cols])
                dmv = dm_ref[rows, cols]
                sa = _sigmoid(zga_ref[rows, cols])
                dya = dmv * sa
                dz_ref[rows, 2 * D_MODEL + g * CHUNK:2 * D_MODEL + (g + 1) * CHUNK] = (
                    dmv * (ua * mixed) * (sa * (1.0 - sa))).astype(dz_ref.dtype)
                dz_ref[rows, cols] = (dya * mixed * dgelu_u).astype(dz_ref.dtype)
                dmix = dya * ua
                gbs_acc = gbs_acc + jnp.where(lane == g, jnp.sum(dmix, axis=-1, keepdims=True), 0.0)
                dmix_b = dmix.astype(MXU_DTYPE)
                gw_acc = gw_acc + lax.dot_general(dmix_b, vn, _DIMS["nt"], preferred_element_type=F32)
                dvn_s[rows, cols] = lax.dot_general(w, dmix_b, _DIMS["tn"], preferred_element_type=F32)
            gws_ref[g] += jnp.where(_causal_mask(CHUNK), gw_acc, 0.0)
        gbs_ref[...] += gbs_acc

        dvn = dvn_s[...]
        glg_ref[...] += jnp.sum(dvn * vhat, axis=0, keepdims=True)
        glb_ref[...] += jnp.sum(dvn, axis=0, keepdims=True)
        dvh = dvn * g_ref[...]
        dva = rs * (dvh - jnp.mean(dvh, axis=-1, keepdims=True) - vhat * jnp.mean(dvh * vhat, axis=-1, keepdims=True))
        dz_ref[:, D_MODEL:2 * D_MODEL] = (dva * dgelu_v).astype(dz_ref.dtype)

        dmv = dm_ref[...]
        ybv = yb_ref[...]
        sb = _sigmoid(zgb_ref[...])
        dyb = dmv * sb
        dyb_ref[...] = dyb.astype(dyb_ref.dtype)
        dz_ref[:, 3 * D_MODEL:4 * D_MODEL] = (dmv * ybv * (sb * (1.0 - sb))).astype(dz_ref.dtype)
        dz_ref[:, 4 * D_MODEL:] = jnp.zeros((tr, LAT), dz_ref.dtype)
        prod = dyb * ybv
        sel = (lax.broadcasted_iota(jnp.int32, (HEADS, D_MODEL), 1) // NOPE
               == lax.broadcasted_iota(jnp.int32, (HEADS, D_MODEL), 0)).astype(jnp.bfloat16)
        hi = prod.astype(jnp.bfloat16)
        rest = prod - hi.astype(F32)
        mid = rest.astype(jnp.bfloat16)
        lo = (rest - mid.astype(F32)).astype(jnp.bfloat16)
        dl_ref[...] = (lax.dot_general(sel, hi, _DIMS["nt"], preferred_element_type=F32)
                       + lax.dot_general(sel, mid, _DIMS["nt"], preferred_element_type=F32)
                       + lax.dot_general(sel, lo, _DIMS["nt"], preferred_element_type=F32))

    return pl.pallas_call(
        body, name="mix_bwd",
        out_shape=(jax.ShapeDtypeStruct((T, IN_PAD), MXU_DTYPE), jax.ShapeDtypeStruct((T, D_MODEL), MXU_DTYPE),
                   jax.ShapeDtypeStruct((HEADS, T), F32), jax.ShapeDtypeStruct((A_GROUPS, CHUNK, CHUNK), F32),
                   jax.ShapeDtypeStruct((CHUNK, 128), F32), jax.ShapeDtypeStruct((1, D_MODEL), F32),
                   jax.ShapeDtypeStruct((1, D_MODEL), F32)),
        grid=(T // tr,),
        in_specs=[zcol(0), zcol(1), zcol(2), zcol(3), row, row, vec, vec, ws_spec, bs_spec],
        out_specs=(pl.BlockSpec((tr, IN_PAD), lambda i: (i, 0)), row, pl.BlockSpec((HEADS, tr), lambda i: (0, i)),
                   ws_spec, bs_spec, vec, vec),
        scratch_shapes=[pltpu.VMEM((tr, D_MODEL), MXU_DTYPE), pltpu.VMEM((tr, D_MODEL), F32)],
        compiler_params=_params(("arbitrary",), 12 * _nbytes((tr, D_MODEL), F32)),
    )(z, z, z, z, yb, dm, ln_g, ln_b, ws, bs_t)


def _lat_bwd(dz, z, dq, dk, dv, gq, gkv, wq, wkv, cos_a, sin_a, tr=256):
    T = z.shape[0]
    lat_blk = (4 * D_MODEL) // LAT

    def body(dz_in, z_ref, dq_ref, dk_ref, dv_ref, gq_ref, gkv_ref, wq_ref, wkv_ref, cos_ref, sin_ref,
             dz_ref, dqr_ref, dkv_ref, ggq_ref, ggkv_ref):
        del dz_in

        @pl.when(pl.program_id(0) == 0)
        def _():
            ggq_ref[...] = jnp.zeros_like(ggq_ref)
            ggkv_ref[...] = jnp.zeros_like(ggkv_ref)

        cos_v, sin_v = cos_ref[...], sin_ref[...]
        dkr = jnp.zeros((tr, 128), F32)
        for h in range(HEADS):
            o = h * HEAD_PAD
            dqr_ref[:, o:o + NOPE] = dq_ref[:, o:o + NOPE].astype(MXU_DTYPE)
            dqr_ref[:, o + NOPE:o + HEAD_PAD] = _rope_mix_bwd(dq_ref[:, o + NOPE:o + HEAD_PAD], cos_v, sin_v).astype(MXU_DTYPE)
            dkv_ref[:, h * NOPE:(h + 1) * NOPE] = dk_ref[:, o:o + NOPE].astype(MXU_DTYPE)
            dkr = dkr + _rope_mix_bwd(dk_ref[:, o + NOPE:o + HEAD_PAD], cos_v, sin_v)
        dkv_ref[:, HEADS * NOPE:] = dv_ref[...]
        dcqn = lax.dot_general(dqr_ref[...], wq_ref[...], _DIMS["nt"], preferred_element_type=F32)
        dckvn = lax.dot_general(dkv_ref[...], wkv_ref[...], _DIMS["nt"], preferred_element_type=F32)

        zl = z_ref[...]

        def rms_bwd(c, dn, g_ref, gg_ref):
            r = lax.rsqrt(jnp.mean(c * c, axis=-1, keepdims=True) + EPS)
            ch = c * r
            gg_ref[...] += jnp.sum(dn * ch, axis=0, keepdims=True)
            dch = dn * g_ref[...]
            return r * (dch - ch * jnp.mean(dch * ch, axis=-1, keepdims=True))

        dz_ref[:, :Q_RANK] = rms_bwd(zl[:, :Q_RANK], dcqn, gq_ref, ggq_ref).astype(dz_ref.dtype)
        dz_ref[:, Q_RANK:Q_RANK + KV_RANK] = rms_bwd(zl[:, Q_RANK:Q_RANK + KV_RANK], dckvn, gkv_ref, ggkv_ref).astype(dz_ref.dtype)
        dz_ref[:, Q_RANK + KV_RANK:] = dkr.astype(dz_ref.dtype)

    def row(w):
        return pl.BlockSpec((tr, w), lambda i: (i, 0))

    def full(a):
        return pl.BlockSpec(a.shape, lambda i: (0, 0))

    lat = pl.BlockSpec((tr, LAT), lambda i: (i, lat_blk))
    return pl.pallas_call(
        body, name="lat_bwd",
        out_shape=(jax.ShapeDtypeStruct(dz.shape, dz.dtype), jax.ShapeDtypeStruct((T, HEADS * HEAD_PAD), MXU_DTYPE),
                   jax.ShapeDtypeStruct((T, 2 * HEADS * NOPE), MXU_DTYPE), jax.ShapeDtypeStruct(gq.shape, F32),
                   jax.ShapeDtypeStruct(gkv.shape, F32)),
        grid=(T // tr,),
        in_specs=[pl.BlockSpec(memory_space=pl.ANY), lat, row(HEADS * HEAD_PAD), row(HEADS * HEAD_PAD), row(HEADS * NOPE),
                  full(gq), full(gkv), full(wq), full(wkv), row(128), row(128)],
        out_specs=(lat, row(HEADS * HEAD_PAD), row(2 * HEADS * NOPE), full(gq), full(gkv)),
        input_output_aliases={0: 0},
        compiler_params=_params(("arbitrary",), 8 * _nbytes((tr, HEADS * HEAD_PAD), F32)),
    )(dz, z, dq, dk, dv, gq, gkv, wq, wkv, cos_a, sin_a)


GATE_ROWS = 64
HALO = 8


def _taps(ref, half, r, first):
    C = GATE_ROWS
    if first:
        xs = jnp.concatenate([jnp.zeros((HALO, ref.shape[-1]), F32), ref[half, 0:C, :]], axis=0)
    else:
        xs = ref[half, pl.ds(pl.multiple_of(r * C - HALO, HALO), C + HALO), :]
    return xs[HALO:, :], pltpu.roll(xs, 1, 0)[HALO:, :], pltpu.roll(xs, 2, 0)[HALO:, :]


def _conv_taps(taps, cw, cb):
    x0, x1, x2 = taps
    return cb + cw[0:1, :] * x2 + cw[1:2, :] * x1 + cw[2:3, :] * x0


def _fold8(x):
    acc = x[0:8, :]
    for i in range(1, x.shape[0] // 8):
        acc = acc + x[8 * i:8 * (i + 1), :]
    return acc


def _gate_fwd(up3, conv_w, conv_b, B, S):
    T = B * S
    W = FF_TILE
    C = GATE_ROWS

    def body(up_ref, cw_ref, cb_ref, act_ref):
        def chunk(r, first):
            gate = _conv_taps(_taps(up_ref, 0, r, first), cw_ref[0], cb_ref[0])
            val = _conv_taps(_taps(up_ref, 1, r, first), cw_ref[1], cb_ref[1])
            base = 0 if first else pl.multiple_of(r * C, C)
            act_ref[pl.ds(base, C), :] = (gate * _sigmoid(gate) * val).astype(act_ref.dtype)

        chunk(0, True)

        @pl.loop(1, S // C)
        def _(r):
            chunk(r, False)

    return pl.pallas_call(
        body, name="gate_fwd", out_shape=jax.ShapeDtypeStruct((T, D_FF), MXU_DTYPE), grid=(B, N_FF_TILES),
        in_specs=[pl.BlockSpec((2, S, W), lambda b, j: (0, b, j)), pl.BlockSpec((2, 3, W), lambda b, j: (0, 0, j)),
                  pl.BlockSpec((2, 1, W), lambda b, j: (0, 0, j))],
        out_specs=pl.BlockSpec((S, W), lambda b, j: (b, j)),
        compiler_params=_params(("parallel", "parallel"), 6 * _nbytes((S, W), F32)),
    )(up3, conv_w, conv_b)


def _gate_bwd(up3, dact, conv_w, conv_b, B, S):
    T = B * S
    W = FF_TILE
    C = GATE_ROWS

    def body(up_ref, da_ref, cw_ref, cb_ref, dup_ref, gcw_ref, gcb_ref, d_s):
        @pl.when(pl.program_id(1) == 0)
        def _():
            gcw_ref[...] = jnp.zeros_like(gcw_ref)
            gcb_ref[...] = jnp.zeros_like(gcb_ref)

        def chunk(r, first, sums):
            rows = pl.ds(0 if first else pl.multiple_of(r * C, C), C)
            taps = [_taps(up_ref, half, r, first) for half in (0, 1)]
            gate = _conv_taps(taps[0], cw_ref[0], cb_ref[0])
            val = _conv_taps(taps[1], cw_ref[1], cb_ref[1])
            sg = _sigmoid(gate)
            da = da_ref[rows, :]
            d_halves = (da * val * (sg * (1.0 + gate * (1.0 - sg))), da * (gate * sg))
            out = []
            for half, dup in enumerate(d_halves):
                d_s[half, rows, :] = dup
                x0, x1, x2 = taps[half]
                sb, s0, s1, s2 = sums[half]
                out.append((sb + _fold8(dup), s0 + _fold8(dup * x2), s1 + _fold8(dup * x1), s2 + _fold8(dup * x0)))
            return tuple(out)

        zeros = tuple(tuple(jnp.zeros((8, W), F32) for _ in range(4)) for _ in range(2))
        sums = chunk(0, True, zeros)
        sums = lax.fori_loop(1, S // C, lambda r, s: chunk(r, False, s), sums)
        for half in (0, 1):
            sb, s0, s1, s2 = sums[half]
            gcb_ref[half] += jnp.sum(sb, axis=0, keepdims=True)
            gcw_ref[half, 0:1, :] += jnp.sum(s0, axis=0, keepdims=True)
            gcw_ref[half, 1:2, :] += jnp.sum(s1, axis=0, keepdims=True)
            gcw_ref[half, 2:3, :] += jnp.sum(s2, axis=0, keepdims=True)

        d_s[:, S:S + HALO, :] = jnp.zeros((2, HALO, W), F32)

        @pl.loop(0, S // C)
        def _(r):
            base = pl.multiple_of(r * C, C)
            for half in (0, 1):
                ds_ = d_s[half, pl.ds(base, C + HALO), :]
                cw = cw_ref[half]
                dx = (cw[2:3, :] * ds_[:C, :] + cw[1:2, :] * pltpu.roll(ds_, C + HALO - 1, 0)[:C, :]
                      + cw[0:1, :] * pltpu.roll(ds_, C + HALO - 2, 0)[:C, :])
                dup_ref[half, pl.ds(base, C), :] = dx.astype(dup_ref.dtype)

    up_spec = pl.BlockSpec((2, S, W), lambda j, b: (0, b, j))
    cw_spec = pl.BlockSpec((2, 3, W), lambda j, b: (0, 0, j))
    cb_spec = pl.BlockSpec((2, 1, W), lambda j, b: (0, 0, j))
    return pl.pallas_call(
        body, name="gate_bwd",
        out_shape=(jax.ShapeDtypeStruct((2, T, D_FF), MXU_DTYPE), jax.ShapeDtypeStruct((2, 3, D_FF), F32),
                   jax.ShapeDtypeStruct((2, 1, D_FF), F32)),
        grid=(N_FF_TILES, B),
        in_specs=[up_spec, pl.BlockSpec((S, W), lambda j, b: (b, j)), cw_spec, cb_spec],
        out_specs=(up_spec, cw_spec, cb_spec),
        scratch_shapes=[pltpu.VMEM((2, S + HALO, W), F32)],
        compiler_params=_params(("parallel", "arbitrary"), 10 * _nbytes((S, W), F32)),
    )(up3, dact, conv_w, conv_b)


def _final(x2, tgt, g, tr=512):
    T, D = x2.shape

    def body(x_ref, t_ref, g_ref, dx_ref, loss_ref, gg_ref):
        @pl.when(pl.program_id(0) == 0)
        def _():
            loss_ref[...] = jnp.zeros_like(loss_ref)
            gg_ref[...] = jnp.zeros_like(gg_ref)

        xv = x_ref[...]
        gv = g_ref[...]
        r = lax.rsqrt(jnp.mean(xv * xv, axis=-1, keepdims=True) + EPS)
        xn = xv * r
        err = xn * gv - t_ref[...]
        loss_ref[...] += 0.5 * jnp.sum(jnp.mean(err * err, axis=-1, keepdims=True), axis=0, keepdims=True)
        dy = err * (1.0 / D)
        gg_ref[...] += jnp.sum(dy * xn, axis=0, keepdims=True)
        dxn = dy * gv
        dx_ref[...] = r * (dxn - xn * jnp.mean(dxn * xn, axis=-1, keepdims=True))

    row = pl.BlockSpec((tr, D), lambda i: (i, 0))
    vec = pl.BlockSpec((1, D), lambda i: (0, 0))
    return pl.pallas_call(
        body, name="final_loss",
        out_shape=(jax.ShapeDtypeStruct((T, D), F32), jax.ShapeDtypeStruct((1, 128), F32), jax.ShapeDtypeStruct((1, D), F32)),
        grid=(T // tr,), in_specs=[row, row, vec],
        out_specs=(row, pl.BlockSpec((1, 128), lambda i: (0, 0)), vec),
        compiler_params=_params(("arbitrary",), 6 * _nbytes((tr, D), F32)),
    )(x2, tgt, g)


def _sum_slabs(parts, name, tr):
    rows, cols = parts[0].shape
    n = len(parts)

    def body(*refs):
        acc = refs[0][...]
        for r in refs[1:n]:
            acc = acc + r[...]
        refs[n][...] = acc

    blk = pl.BlockSpec((tr, cols), lambda i: (i, 0))
    return pl.pallas_call(
        body, name=name, out_shape=jax.ShapeDtypeStruct((rows, cols), F32), grid=(rows // tr,),
        in_specs=[blk] * n, out_specs=blk,
        compiler_params=_params(("parallel",), (n + 1) * _nbytes((tr, cols), F32)),
    )(*parts)


ADAMW_BLOCK_BYTES = 2400 * 1024


def _adamw(w, g, m, v, name):
    lead = w.ndim == 3
    rows, cols = w.shape[-2:]
    fits = [d for d in range(8, rows + 1, 8) if rows % d == 0 and d * cols * 4 <= ADAMW_BLOCK_BYTES]
    tr = max(fits) if fits else rows
    c1 = 1.0 - ADAM_B1 ** ADAM_STEP
    c2 = 1.0 - ADAM_B2 ** ADAM_STEP

    def body(w_ref, g_ref, m_ref, v_ref, d_ref, nm_ref, nv_ref):
        gv = g_ref[...]
        nm = ADAM_B1 * m_ref[...] + (1.0 - ADAM_B1) * gv
        nv = ADAM_B2 * v_ref[...] + (1.0 - ADAM_B2) * (gv * gv)
        nm_ref[...] = nm
        nv_ref[...] = nv
        d_ref[...] = -ADAM_LR * ((nm / c1) / (jnp.sqrt(nv / c2) + ADAM_EPS) + ADAM_WD * w_ref[...])

    blk = pl.BlockSpec((None, tr, cols), lambda i: (0, i, 0)) if lead else pl.BlockSpec((tr, cols), lambda i: (i, 0))
    sds = jax.ShapeDtypeStruct(w.shape, F32)
    return pl.pallas_call(
        body, name=name, out_shape=(sds, sds, sds), grid=(rows // tr,), in_specs=[blk] * 4, out_specs=(blk, blk, blk),
        compiler_params=_params(("parallel",), 7 * _nbytes((tr, cols), F32)),
    )(w, g, m, v)


_ANY = pl.BlockSpec(memory_space=pl.ANY)


def _place():
    x, y, c = lax.axis_index("x"), lax.axis_index("y"), lax.axis_index("c")
    chips = [(1 - x, y), (x, 1 - y), (1 - x, 1 - y)]
    return x, y, c, chips


def _gather_weights(shards):
    n = len(shards)

    def body(*refs):
        ins, outs = refs[:n], refs[n:2 * n]
        send, recv, fsend, frecv, osend, orecv = refs[2 * n:]
        x, y, c, chips = _place()
        me = 2 * x + y
        first, passed = [], []
        for w in range(n):
            first.append(pltpu.make_async_remote_copy(
                src_ref=ins[w], dst_ref=outs[w].at[me], send_sem=osend.at[w], recv_sem=orecv.at[w],
                device_id=(x, y, 1 - c), device_id_type=MESH))
        for w in range(n):
            for j, (px, py) in enumerate(chips):
                first.append(pltpu.make_async_remote_copy(
                    src_ref=ins[w].at[c], dst_ref=outs[w].at[me, c], send_sem=send.at[3 * w + j],
                    recv_sem=recv.at[3 * w + j], device_id=(px, py, c), device_id_type=MESH))
        for cp in first:
            cp.start()
        for w in range(n):
            for j, (px, py) in enumerate(chips):
                landed = outs[w].at[2 * px + py, c]
                pltpu.make_async_remote_copy(src_ref=landed, dst_ref=landed, send_sem=send.at[3 * w + j],
                                             recv_sem=recv.at[3 * w + j], device_id=(px, py, c),
                                             device_id_type=MESH).wait_recv()
                fw = pltpu.make_async_remote_copy(src_ref=landed, dst_ref=landed, send_sem=fsend.at[3 * w + j],
                                                  recv_sem=frecv.at[3 * w + j], device_id=(x, y, 1 - c),
                                                  device_id_type=MESH)
                fw.start()
                passed.append(fw)
        for w in range(n):
            for j, (px, py) in enumerate(chips):
                other = outs[w].at[2 * px + py, 1 - c]
                pltpu.make_async_remote_copy(src_ref=other, dst_ref=other, send_sem=fsend.at[3 * w + j],
                                             recv_sem=frecv.at[3 * w + j], device_id=(x, y, 1 - c),
                                             device_id_type=MESH).wait_recv()
        for w in range(n):
            own = outs[w].at[me]
            pltpu.make_async_remote_copy(src_ref=own, dst_ref=own, send_sem=osend.at[w], recv_sem=orecv.at[w],
                                         device_id=(x, y, 1 - c), device_id_type=MESH).wait_recv()
        for cp in first + passed:
            cp.wait_send()

    dma = lambda k: pltpu.SemaphoreType.DMA((k,))
    return pl.pallas_call(
        body, name="gather_weights",
        out_shape=tuple(jax.ShapeDtypeStruct((N_CHIPS,) + s.shape, s.dtype) for s in shards),
        in_specs=[_ANY] * n, out_specs=tuple([_ANY] * n),
        scratch_shapes=[dma(3 * n), dma(3 * n), dma(3 * n), dma(3 * n), dma(n), dma(n)],
    )(*shards)


_HBM = pl.BlockSpec(memory_space=pltpu.HBM)
_SEM = pl.BlockSpec(memory_space=pltpu.SEMAPHORE)
_EFFECT = pltpu.SideEffectType.DATAFLOW_SIDE_EFFECTING


SEMS_PER_ARRAY = 8


def _exchange_copies(srcs, lands, send, recv, mode):
    x, y, c, chips = _place()
    if mode == "all":
        flips = [(fx, fy, fc) for fx in (0, 1) for fy in (0, 1) for fc in (0, 1)][1:]
        peers = [(x ^ fx, y ^ fy, c ^ fc) for fx, fy, fc in flips]
        slot = 4 * x + 2 * y + c
    else:
        peers = [(px, py, c) for px, py in chips] + ([(x, y, 1 - c)] if mode == "gather" else [])
        slot = 2 * x + y
    cps = []
    for w, (src, land) in enumerate(zip(srcs, lands)):
        for k, peer in enumerate(peers):
            piece = src.at[2 * peer[0] + peer[1]] if mode == "scatter" else src
            cps.append(pltpu.make_async_remote_copy(
                src_ref=piece, dst_ref=land.at[slot], send_sem=send.at[SEMS_PER_ARRAY * w + k],
                recv_sem=recv.at[SEMS_PER_ARRAY * w + k], device_id=peer, device_id_type=MESH))
    return cps


def _exchange_start(srcs, name, mode, after):
    n = len(srcs)
    lead = {"gather": (N_CHIPS,), "scatter": (), "all": (2 * N_CHIPS,)}[mode]
    land_shapes = [lead + s.shape for s in srcs]

    def body(*refs):
        src_refs, land_refs = refs[:n], refs[n:2 * n]
        send, recv = refs[2 * n + 1], refs[2 * n + 2]
        token = refs[-1]
        for cp in _exchange_copies(src_refs, land_refs, send, recv, mode):
            cp.start()
        token[...] = jnp.zeros_like(token)

    sems = pltpu.SemaphoreType.DMA((SEMS_PER_ARRAY * n,))
    out = pl.pallas_call(
        body, name=name,
        out_shape=(sems, sems, *[pltpu.HBM(s.shape, s.dtype) for s in srcs],
                   *[pltpu.HBM(shp, s.dtype) for shp, s in zip(land_shapes, srcs)], jax.ShapeDtypeStruct((8, 128), F32)),
        in_specs=[_HBM] * (2 * n) + [_ANY],
        out_specs=(_SEM, _SEM, *[_HBM] * (2 * n), pl.BlockSpec(memory_space=pltpu.VMEM)),
        input_output_aliases={i: 2 + i for i in range(2 * n)},
        compiler_params=pltpu.CompilerParams(has_side_effects=_EFFECT),
    )(*[pltpu.with_memory_space_constraint(s, pltpu.HBM) for s in srcs],
      *[pltpu.with_memory_space_constraint(lax.empty(shp, s.dtype), pltpu.HBM) for shp, s in zip(land_shapes, srcs)],
      after)
    return out[0], out[1], out[2:2 + n], out[2 + n:2 + 2 * n], out[-1]


def _exchange_wait(started, name, mode, after):
    send, recv, src_thru, land_thru, _ = started
    n = len(src_thru)

    def body(*refs):
        src_refs, land_refs, send_ref, recv_ref = refs[:n], refs[n:2 * n], refs[2 * n], refs[2 * n + 1]
        for cp in _exchange_copies(src_refs, land_refs, send_ref, recv_ref, mode):
            cp.wait_send()
            cp.wait_recv()

    out = pl.pallas_call(
        body, name=name,
        out_shape=tuple(pltpu.HBM(a.shape, a.dtype) for a in list(src_thru) + list(land_thru)),
        in_specs=[_HBM] * (2 * n) + [_SEM, _SEM, _ANY], out_specs=tuple([_HBM] * (2 * n)),
        input_output_aliases={i: i for i in range(2 * n)},
        compiler_params=pltpu.CompilerParams(has_side_effects=_EFFECT),
    )(*src_thru, *land_thru, send, recv, after)
    return out[:n], out[n:]


def _swap_halves(gs, name):
    n = len(gs)

    def body(*refs):
        ins, outs, send, recv = refs[:n], refs[n:2 * n], refs[2 * n], refs[2 * n + 1]
        x, y, c, _ = _place()
        cps = []
        for w in range(n):
            cps.append(pltpu.make_async_remote_copy(
                src_ref=ins[w].at[:, 1 - c], dst_ref=outs[w], send_sem=send.at[w], recv_sem=recv.at[w],
                device_id=(x, y, 1 - c), device_id_type=MESH))
        for cp in cps:
            cp.start()
        for cp in cps:
            cp.wait()

    return pl.pallas_call(
        body, name=name,
        out_shape=tuple(jax.ShapeDtypeStruct((g.shape[0],) + g.shape[2:], g.dtype) for g in gs),
        in_specs=[_ANY] * n, out_specs=tuple([_ANY] * n),
        scratch_shapes=[pltpu.SemaphoreType.DMA((n,)), pltpu.SemaphoreType.DMA((n,))],
    )(*gs)


GRAD_PAYLOAD = jnp.bfloat16


def _half_blocks(half_rows, cols):
    if (half_rows // 2) % 16 == 0:
        return (half_rows // 2, cols), (lambda r: (r, 0))
    assert cols % 256 == 0, (half_rows, cols)
    return (half_rows, cols // 2), (lambda r: (0, r))


def _pair_sum(gs, gots, name):
    n = len(gs)
    core = lax.axis_index("c").astype(jnp.int32).reshape(1)

    def body(core_ref, *refs):
        del core_ref
        for w in range(n):
            refs[2 * n + w][...] = (refs[w][...] + refs[n + w][...]).astype(GRAD_PAYLOAD)

    in_specs, out_specs, out_shape, nbytes = [], [], [], 0
    cuts = [_half_blocks(g.shape[1] // 2, g.shape[2]) for g in gs]
    for g, ((br, bc), at) in zip(gs, cuts):
        per_half = (g.shape[1] // 2) // br
        in_specs.append(pl.BlockSpec((1, br, bc), lambda s, r, core, at=at, per_half=per_half:
                                     (s, per_half * core[0] + at(r)[0], at(r)[1])))
        nbytes += 3 * _nbytes((br, bc), F32)
    for g, ((br, bc), at) in zip(gs, cuts):
        in_specs.append(pl.BlockSpec((1, br, bc), lambda s, r, core, at=at: (s,) + at(r)))
        out_specs.append(pl.BlockSpec((1, br, bc), lambda s, r, core, at=at: (s,) + at(r)))
        out_shape.append(jax.ShapeDtypeStruct((g.shape[0], g.shape[1] // 2, g.shape[2]), GRAD_PAYLOAD))
    return pl.pallas_call(
        body, name=name, out_shape=tuple(out_shape),
        grid_spec=pltpu.PrefetchScalarGridSpec(num_scalar_prefetch=1, grid=(N_CHIPS, 2), in_specs=in_specs,
                                               out_specs=tuple(out_specs)),
        compiler_params=_params(("parallel", "parallel"), nbytes),
    )(core, *gs, *gots)


def _chip_sum(ps, landed):
    n = len(ps)
    x, y, c = lax.axis_index("x"), lax.axis_index("y"), lax.axis_index("c")
    where = jnp.stack([2 * x + y, 2 * (1 - x) + y, 2 * x + (1 - y), 2 * (1 - x) + (1 - y), c]).astype(jnp.int32)

    def body(where_ref, *refs):
        del where_ref
        for w in range(n):
            terms = [refs[4 * w + t][...].astype(F32) for t in range(4)]
            refs[4 * n + w][...] = ((terms[0] + terms[1]) + terms[2]) + terms[3]

    in_specs, out_specs, out_shape, args, nbytes = [], [], [], [], 0
    for p, a in zip(ps, landed):
        (br, bc), at = _half_blocks(a.shape[1], a.shape[2])
        blk = (1, br, bc)
        in_specs.append(pl.BlockSpec(blk, lambda r, where, at=at: (where[0],) + at(r)))
        args.append(p)
        for t in (1, 2, 3):
            in_specs.append(pl.BlockSpec(blk, lambda r, where, t=t, at=at: (where[t],) + at(r)))
            args.append(a)
        out_specs.append(pl.BlockSpec(blk, lambda r, where, at=at: (where[4],) + at(r)))
        out_shape.append(jax.ShapeDtypeStruct((2,) + a.shape[1:], F32))
        nbytes += 4 * _nbytes(blk, F32)
    return pl.pallas_call(
        body, name="grad_chip_sum", out_shape=tuple(out_shape),
        grid_spec=pltpu.PrefetchScalarGridSpec(num_scalar_prefetch=1, grid=(2,), in_specs=in_specs,
                                               out_specs=tuple(out_specs)),
        compiler_params=_params(("parallel",), nbytes),
    )(where, *args)


def _join_halves(ss):
    n = len(ss)

    def body(*refs):
        outs, send, recv = refs[n:2 * n], refs[2 * n], refs[2 * n + 1]
        x, y, c, _ = _place()
        cps = []
        for w in range(n):
            cps.append(pltpu.make_async_remote_copy(
                src_ref=outs[w].at[c], dst_ref=outs[w].at[c], send_sem=send.at[w], recv_sem=recv.at[w],
                device_id=(x, y, 1 - c), device_id_type=MESH))
        for cp in cps:
            cp.start()
        for w in range(n):
            got = outs[w].at[1 - c]
            pltpu.make_async_remote_copy(src_ref=got, dst_ref=got, send_sem=send.at[w], recv_sem=recv.at[w],
                                         device_id=(x, y, 1 - c), device_id_type=MESH).wait_recv()
        for cp in cps:
            cp.wait_send()

    dma = lambda k: pltpu.SemaphoreType.DMA((k,))
    return pl.pallas_call(
        body, name="grad_join_halves",
        out_shape=tuple(jax.ShapeDtypeStruct(s.shape, s.dtype) for s in ss),
        in_specs=[_ANY] * n, out_specs=tuple([_ANY] * n), input_output_aliases={w: w for w in range(n)},
        scratch_shapes=[dma(n), dma(n)],
    )(*ss)


def _rot_cols(w, axis=-1):
    a, b = jnp.split(w, 2, axis=axis)
    return jnp.concatenate([-b, a], axis=axis)


def _rot_cols_t(g, axis=-1):
    a, b = jnp.split(g, 2, axis=axis)
    return jnp.concatenate([b, -a], axis=axis)


def _cols_from_chips(a):
    n, r, cs = a.shape
    return jnp.transpose(a, (1, 0, 2)).reshape(r, n * cs)


def _cols_to_chips(a):
    r, cc = a.shape
    return jnp.transpose(a.reshape(r, N_CHIPS, cc // N_CHIPS), (1, 0, 2))


def _conv_w_split(cw):
    return jnp.swapaxes(cw.reshape(3, 2, D_FF), 0, 1)


def _conv_w_join(g):
    return jnp.swapaxes(g, 0, 1).reshape(3, 2 * D_FF)


_SEG =(D_MODEL, 2 * D_MODEL, 2 * D_MODEL + Q_RANK, 2 * D_MODEL + Q_RANK + KV_RANK, 2 * D_MODEL + Q_RANK + KV_RANK + ROPE,
        3 * D_MODEL + Q_RANK + KV_RANK + ROPE)


def _w_in_t_to_pad(wt):
    u, v, cq, ckv, kr, ga, gb = jnp.split(wt, _SEG, axis=0)
    return jnp.concatenate([u, v, ga, gb, cq, ckv, kr, _rot_cols(kr, axis=0)], axis=0)


def _w_in_t_from_pad(gt):
    u, v, ga, gb, cq, ckv, kr, krr = jnp.split(
        gt, (D_MODEL, 2 * D_MODEL, 3 * D_MODEL, 4 * D_MODEL, 4 * D_MODEL + Q_RANK, 4 * D_MODEL + Q_RANK + KV_RANK,
             4 * D_MODEL + Q_RANK + KV_RANK + ROPE), axis=0)
    return jnp.concatenate([u, v, cq, ckv, kr + _rot_cols_t(krr, axis=0), ga, gb], axis=0)


def _w_uq_to_pad(w):
    t = w.reshape(Q_RANK, HEADS, QK_DIM)
    nope, rope = t[..., :NOPE], t[..., NOPE:]
    return jnp.concatenate([nope, rope, _rot_cols(rope)], axis=-1).reshape(Q_RANK, HEADS * HEAD_PAD)


def _w_uq_from_pad(g):
    t = g.reshape(Q_RANK, HEADS, HEAD_PAD)
    nope, rope, rot = t[..., :NOPE], t[..., NOPE:QK_DIM], t[..., QK_DIM:]
    return jnp.concatenate([nope, rope + _rot_cols_t(rot)], axis=-1).reshape(Q_RANK, HEADS * QK_DIM)


def _w_ukv_to_pad(w):
    t = w.reshape(KV_RANK, HEADS, 2, NOPE)
    return jnp.swapaxes(t, 1, 2).reshape(KV_RANK, 2 * HEADS * NOPE)


def _w_ukv_from_pad(g):
    t = g.reshape(KV_RANK, 2, HEADS, NOPE)
    return jnp.swapaxes(t, 1, 2).reshape(KV_RANK, 2 * HEADS * NOPE)


def _rope_tables(positions):
    inv_freq = 1.0 / (ROPE_THETA ** (jnp.arange(0, ROPE, 2, dtype=F32) / ROPE))
    ang = positions.astype(F32).reshape(-1, 1) * inv_freq
    cos, sin = jnp.cos(ang), jnp.sin(ang)
    zero = jnp.zeros((ang.shape[0], 64), F32)
    return jnp.concatenate([cos, cos, zero], axis=1), jnp.concatenate([sin, sin, zero], axis=1)


_BIG = ("w_in", "w_uq", "w_ukv", "w_out", "w_up", "w_down")
UP_SHARD = 2 * D_FF // N_CHIPS


def _local_step(x, positions, tgt, wts, mixer_weights, ffn_weights, on_ffn_grads, on_mixer_grads):
    B, S, D = x.shape
    T = B * S
    xf = x.reshape(T, D)
    cos_a, sin_a = _rope_tables(positions)
    bs_t = jnp.pad(wts["a_spatial_b"].T, ((0, 0), (0, 128 - A_GROUPS)))

    h = _rms_fwd(xf, wts["mix_norm"], "norm1_fwd")
    z = _mm(h, wts["w_in"], "nt", "in_proj", tm=512, tn=1536, tk=D, n_outer=True)
    wts = dict(wts)
    wts["w_q"], wts["w_kv"], wts["w_out"] = mixer_weights(z)
    q, k, v, cqn, ckvn = _lat_fwd(z, wts["q_a_norm"], wts["kv_a_norm"], wts["w_q"], wts["w_kv"], cos_a, sin_a)
    yb, lse = _attn_fwd(q, k, v, B, S)
    merged = _mix_fwd(z, yb, wts["a_v_norm_g"], wts["a_v_norm_b"], wts["a_spatial_w"], bs_t)
    x1 = _mm(merged, wts["w_out"], "nn", "out_proj", tm=512, tn=D, tk=D, add=xf)
    h2 = _rms_fwd(x1, wts["ffn_norm"], "norm2_fwd")
    wts["w_up"], wts["w_down"], wts["conv_w"] = ffn_weights(h2)
    up_pre = _mm(h2, wts["w_up"], "nn", "up_proj", tm=512, tn=UP_SHARD, tk=D, dims=(T, 2 * D_FF, D),
                 b_spec=pl.BlockSpec((None, D, UP_SHARD), lambda i, j, k: (j, 0, 0)),
                 o_spec=pl.BlockSpec((None, 512, UP_SHARD), lambda i, j, k: (j // 2, i, j % 2)), out_shape=(2, T, D_FF),
                 n_outer=True)
    act = _gate_fwd(up_pre, wts["conv_w"], wts["conv_b"], B, S)
    x2 = _mm(act, wts["w_down"], "nn", "down_proj", tm=512, tn=D, tk=1408, add=x1)
    dx2, loss_row, g_final = _final(x2, tgt.reshape(T, D), wts["final_norm"])

    g = {"final_norm": g_final}
    dact = _mm(dx2, wts["w_down"], "nt", "down_proj_dx", tm=512, tn=1408, tk=D, n_outer=True)
    tk2, tk1 = min(2048, T), min(1024, T)
    g["w_down"], g["w_down_lo"] = _mm(act, dx2, "tn", "down_proj_dw", tm=1408, tn=D, tk=tk1, copy_dtype=GRAD_PAYLOAD)
    dup, g["conv_w"], g["conv_b"] = _gate_bwd(up_pre, dact, wts["conv_w"], wts["conv_b"], B, S)
    g["w_up"], g["w_up_lo"] = _mm(
        h2, dup, "tn", "up_proj_dw", tm=D, tn=UP_SHARD, tk=tk2, dims=(D, 2 * D_FF, T), copy_dtype=GRAD_PAYLOAD,
        b_spec=pl.BlockSpec((None, tk2, UP_SHARD), lambda i, j, k: (j // 2, k, j % 2)),
        o_spec=pl.BlockSpec((None, D, UP_SHARD), lambda i, j, k: (j, 0, 0)), out_shape=(N_CHIPS, D, UP_SHARD))
    ffn_sent = on_ffn_grads(g)
    dh2 = _mm(dup, wts["w_up"], "nt", "up_proj_dx", tm=512, tn=D, tk=UP_SHARD, dims=(T, D, 2 * D_FF),
              a_spec=pl.BlockSpec((None, 512, UP_SHARD), lambda i, j, k: (k // 2, i, k % 2)),
              b_spec=pl.BlockSpec((None, D, UP_SHARD), lambda i, j, k: (k, 0, 0)))
    token = None if ffn_sent is None else ffn_sent(dh2)
    ffn_norm = wts["ffn_norm"] if token is None else wts["ffn_norm"] + token[0:1, 0:1]
    dx1, g["ffn_norm"] = _rms_bwd(x1, ffn_norm, dh2, dx2, "norm2_bwd")
    dm = _mm(dx1, wts["w_out"], "nt", "out_proj_dx", tm=512, tn=D, tk=D)
    g["w_out"], g["w_out_lo"] = _mm(merged, dx1, "tn", "out_proj_dw", tm=D, tn=D, tk=tk1, copy_dtype=GRAD_PAYLOAD)
    dz, dyb, dl, g["a_spatial_w"], gbs, g["a_v_norm_g"], g["a_v_norm_b"] = _mix_bwd(
        z, yb, dm, wts["a_v_norm_g"], wts["a_v_norm_b"], wts["a_spatial_w"], bs_t)
    g["a_spatial_b"] = gbs[:, :A_GROUPS].T
    delta = dl.reshape(HEADS * T // ATT_BLOCK, 1, ATT_BLOCK)
    dq, dk, dv = _attn_bwd(q, k, v, dyb, lse, delta, B, S)
    dz, dq_raw, dkv, g["q_a_norm"], g["kv_a_norm"] = _lat_bwd(
        dz, z, dq, dk, dv, wts["q_a_norm"], wts["kv_a_norm"], wts["w_q"], wts["w_kv"], cos_a, sin_a)
    g["w_q"], g["w_q_lo"] = _mm(cqn, dq_raw, "tn", "q_proj_dw", tm=Q_RANK, tn=HEADS * HEAD_PAD, tk=tk2,
                                copy_dtype=GRAD_PAYLOAD)
    g["w_kv"], g["w_kv_lo"] = _mm(ckvn, dkv, "tn", "kv_proj_dw", tm=KV_RANK, tn=2 * HEADS * NOPE, tk=tk2,
                                  copy_dtype=GRAD_PAYLOAD)
    g["w_in"], g["w_in_lo"] = _mm(dz, h, "tn", "in_proj_dw", tm=1536, tn=D, tk=tk2,
                                  copy_dtype=GRAD_PAYLOAD)
    token = on_mixer_grads(g)
    mix_norm = wts["mix_norm"] if token is None else wts["mix_norm"] + token[0:1, 0:1]
    dh = _mm(dz, wts["w_in"], "nn", "in_proj_dx", tm=512, tn=D, tk=1536)
    dx, g["mix_norm"] = _rms_bwd(xf, mix_norm, dh, dx1, "norm1_bwd")
    return loss_row[0, 0], dx.reshape(B, S, D), g


_SMALL = (("mix_norm", (1, D_MODEL)), ("a_v_norm_g", (1, D_MODEL)), ("a_v_norm_b", (1, D_MODEL)),
          ("a_spatial_w", (A_GROUPS * CHUNK, CHUNK)), ("a_spatial_b", (1, A_GROUPS * CHUNK)), ("q_a_norm", (1, Q_RANK)),
          ("kv_a_norm", (1, KV_RANK)), ("ffn_norm", (1, D_MODEL)), ("conv_b", (1, 2 * D_FF)), ("final_norm", (1, D_MODEL)),
          ("conv_w", (3, 2 * D_FF)))
_SMALL_SIZE = sum(math.prod(s) for _, s in _SMALL)
_SMALL_ROWS = -(-(_SMALL_SIZE + 1) // (128 * 8)) * 8


def kernel(x, positions, mix_norm, w_in, a_v_norm_g, a_v_norm_b, a_spatial_w, a_spatial_b, q_a_norm, w_uq, kv_a_norm, w_ukv, w_out, ffn_norm, w_up, conv_w, conv_b, w_down, final_norm, loss_target, m_mix_norm, m_w_in, m_a_v_norm_g, m_a_v_norm_b, m_a_spatial_w, m_a_spatial_b, m_q_a_norm, m_w_uq, m_kv_a_norm, m_w_ukv, m_w_out, m_ffn_norm, m_w_up, m_conv_w, m_conv_b, m_w_down, m_final_norm, v_mix_norm, v_w_in, v_a_v_norm_g, v_a_v_norm_b, v_a_spatial_w, v_a_spatial_b, v_q_a_norm, v_w_uq, v_kv_a_norm, v_w_ukv, v_w_out, v_ffn_norm, v_w_up, v_conv_w, v_conv_b, v_w_down, v_final_norm):
    weights = dict(mix_norm=mix_norm, w_in=w_in, a_v_norm_g=a_v_norm_g, a_v_norm_b=a_v_norm_b, a_spatial_w=a_spatial_w,
                   a_spatial_b=a_spatial_b, q_a_norm=q_a_norm, w_uq=w_uq, kv_a_norm=kv_a_norm, w_ukv=w_ukv, w_out=w_out,
                   ffn_norm=ffn_norm, w_up=w_up, conv_w=conv_w, conv_b=conv_b, w_down=w_down, final_norm=final_norm)
    m_in = dict(mix_norm=m_mix_norm, w_in=m_w_in, a_v_norm_g=m_a_v_norm_g, a_v_norm_b=m_a_v_norm_b,
                a_spatial_w=m_a_spatial_w, a_spatial_b=m_a_spatial_b, q_a_norm=m_q_a_norm, w_uq=m_w_uq,
                kv_a_norm=m_kv_a_norm, w_ukv=m_w_ukv, w_out=m_w_out, ffn_norm=m_ffn_norm, w_up=m_w_up, conv_w=m_conv_w,
                conv_b=m_conv_b, w_down=m_w_down, final_norm=m_final_norm)
    v_in = dict(mix_norm=v_mix_norm, w_in=v_w_in, a_v_norm_g=v_a_v_norm_g, a_v_norm_b=v_a_v_norm_b,
                a_spatial_w=v_a_spatial_w, a_spatial_b=v_a_spatial_b, q_a_norm=v_q_a_norm, w_uq=v_w_uq,
                kv_a_norm=v_kv_a_norm, w_ukv=v_w_ukv, w_out=v_w_out, ffn_norm=v_ffn_norm, w_up=v_w_up, conv_w=v_conv_w,
                conv_b=v_conv_b, w_down=v_w_down, final_norm=v_final_norm)
    names = list(weights)
    chip = 2 * lax.axis_index("x") + lax.axis_index("y")

    def halves(a):
        return a.reshape(a.shape[:-2] + (2, a.shape[-2] // 2, a.shape[-1]))

    w_in_t = jnp.swapaxes(w_in[0], 0, 1).astype(MXU_DTYPE)
    (w_in_sh,) = _gather_weights([jnp.stack(jnp.split(w_in_t, 2, axis=1))])
    mixer_gather = _exchange_start([weights[n][0].astype(MXU_DTYPE) for n in _BIG[1:4]],
                                   "mixer_gather_start", "gather", after=w_in_sh)
    ffn_gather = _exchange_start([w_up[0].astype(MXU_DTYPE), w_down[0].astype(MXU_DTYPE), conv_w[0]],
                                 "ffn_gather_start", "gather", after=mixer_gather[4])
    wts = dict(
        mix_norm=mix_norm + ffn_gather[4][0:1, 0:1], a_v_norm_g=a_v_norm_g, a_v_norm_b=a_v_norm_b,
        a_spatial_w=a_spatial_w[0], a_spatial_b=a_spatial_b[0], q_a_norm=q_a_norm, kv_a_norm=kv_a_norm,
        ffn_norm=ffn_norm, final_norm=final_norm.reshape(1, D_MODEL),
        w_in=_w_in_t_to_pad(jnp.concatenate([w_in_sh[:, 0], w_in_sh[:, 1]], axis=-1).reshape(-1, D_MODEL)),
        conv_b=conv_b.reshape(2, 1, D_FF))

    def mixer_weights(after):
        _, (w_uq_sh, w_ukv_sh, w_out_sh) = _exchange_wait(mixer_gather, "mixer_gather_wait", "gather", after)
        return (_w_uq_to_pad(_cols_from_chips(w_uq_sh)), _w_ukv_to_pad(_cols_from_chips(w_ukv_sh)),
                w_out_sh.reshape(D_MODEL, D_MODEL))

    def ffn_weights(after):
        _, (w_up_sh, w_down_sh, cw_all) = _exchange_wait(ffn_gather, "ffn_gather_wait", "gather", after)
        return w_up_sh, w_down_sh.reshape(D_FF, D_MODEL), _conv_w_split(_cols_from_chips(cw_all))

    scatters = {}

    def start_scatter(slabs, slabs_lo, tag):
        got = _swap_halves([halves(s) for s in slabs_lo], tag + "_grad_swap_halves")
        sums = _pair_sum(slabs, got, tag + "_grad_pair_sum")
        scatters[tag] = _exchange_start(list(sums), tag + "_scatter_start", "scatter", after=slabs[-1])
        return scatters[tag][4]

    def on_ffn_grads(g):
        token = start_scatter(*[[g["w_up" + lo], g["w_down" + lo].reshape(N_CHIPS, D_FF // N_CHIPS, D_MODEL)]
                                for lo in ("", "_lo")], "ffn")
        return lambda after: token

    def on_mixer_grads(g):
        return start_scatter(*[
            [_w_in_t_from_pad(g["w_in" + lo]).reshape(N_CHIPS, -1, D_MODEL), _cols_to_chips(_w_uq_from_pad(g["w_q" + lo])),
             _cols_to_chips(_w_ukv_from_pad(g["w_kv" + lo])), g["w_out" + lo].reshape(N_CHIPS, D_MODEL // N_CHIPS, D_MODEL)]
            for lo in ("", "_lo")], "mixer")

    loss_part, grad_x, g = _local_step(x, positions, loss_target, wts, mixer_weights, ffn_weights, on_ffn_grads,
                                       on_mixer_grads)

    g_small_parts = dict(g)
    g_small_parts["conv_w"] = _conv_w_join(g["conv_w"])
    g_small_parts["conv_b"] = g["conv_b"].reshape(1, 2 * D_FF)
    flat = jnp.concatenate([g_small_parts[n].reshape(-1) for n, _ in _SMALL] + [loss_part.reshape(1)])
    flat = jnp.pad(flat, (0, _SMALL_ROWS * 128 - flat.shape[0])).reshape(_SMALL_ROWS, 128)
    small_gather = _exchange_start([flat], "small_gather_start", "all", after=grad_x)

    mixer_sums, mixer_landed = _exchange_wait(scatters["mixer"], "mixer_scatter_wait", "scatter", after=small_gather[4])
    ffn_sums, ffn_landed = _exchange_wait(scatters["ffn"], "ffn_scatter_wait", "scatter", after=mixer_landed[0])
    reduced = _chip_sum(list(mixer_sums) + list(ffn_sums), list(mixer_landed) + list(ffn_landed))
    g_big = dict(zip(_BIG, _join_halves(reduced)))

    grads, deltas, new_m, new_v = {}, {}, {}, {}

    def update(n, grad):
        w = weights[n]
        shape2 = grad.shape
        d, nm, nv = _adamw(w.reshape(shape2), grad, m_in[n].reshape(shape2), v_in[n].reshape(shape2), "adamw_" + n)
        grads[n], deltas[n], new_m[n], new_v[n] = (t.reshape(w.shape) for t in (grad, d, nm, nv))

    def update_transposed(n, grad_t):
        t = lambda a: jnp.swapaxes(a, 1, 2)
        d, nm, nv = _adamw(t(weights[n]), grad_t, t(m_in[n]), t(v_in[n]), "adamw_" + n)
        grads[n], deltas[n], new_m[n], new_v[n] = t(grad_t), t(d), t(nm), t(nv)

    for n in _BIG:
        g3 = g_big[n].reshape((1, -1, g_big[n].shape[-1]))
        if n == "w_in":
            update_transposed(n, g3)
        else:
            update(n, g3)

    (own,), (everyone,) = _exchange_wait(small_gather, "small_gather_wait", "all", after=deltas["w_up"])
    device = 2 * chip + lax.axis_index("c")
    everyone = lax.dynamic_update_slice(everyone, own[None], (device, 0, 0))
    total = _sum_slabs([everyone[j] for j in range(8)], "small_grads_sum", tr=_SMALL_ROWS).reshape(-1)
    o = 0
    for n, shp in _SMALL:
        piece = total[o:o + math.prod(shp)].reshape(shp)
        o += math.prod(shp)
        if n == "conv_w":
            piece = lax.dynamic_slice_in_dim(piece, chip * UP_SHARD, UP_SHARD, axis=1)
        update(n, piece)
    loss = total[_SMALL_SIZE]
    return (loss, grad_x, *[grads[n] for n in names], *[deltas[n] for n in names], *[new_m[n] for n in names],
            *[new_v[n] for n in names])
```

```python
import functools
import math

import jax
import jax.numpy as jnp
from jax import lax
from jax.experimental import pallas as pl
from jax.experimental.pallas import tpu as pltpu

F32 = jnp.float32
MXU_DTYPE = jnp.bfloat16
MESH = pl.DeviceIdType.MESH

D_MODEL = 1024
EPS = 1e-6
A_GROUPS = 8
CHUNK = 128
HEADS = 8
NOPE = 128
ROPE = 64
QK_DIM = NOPE + ROPE
HEAD_PAD = 256
Q_RANK = 256
KV_RANK = 128
ROPE_THETA = 10000.0
D_FF = 2816
FF_TILE = 256
N_FF_TILES = D_FF // FF_TILE
LAT = 512
IN_PAD = 4 * D_MODEL + LAT
N_CHIPS = 4
ADAM_LR, ADAM_B1, ADAM_B2, ADAM_EPS, ADAM_WD, ADAM_STEP = 0.001, 0.9, 0.999, 1e-08, 0.01, 10

VMEM_CAP_V7X = 64 * 1024 * 1024
NEG = -1e30


def _params(sem, nbytes):
    limit = int(min(VMEM_CAP_V7X - (8 << 20), max(32 << 20, 3 * nbytes)))
    return pltpu.CompilerParams(dimension_semantics=sem, vmem_limit_bytes=limit)


def _nbytes(shape, dtype):
    return math.prod(shape) * jnp.dtype(dtype).itemsize


_DIMS = {"nn": (((1,), (0,)), ((), ())), "nt": (((1,), (1,)), ((), ())), "tn": (((0,), (0,)), ((), ()))}


def _mm(a, b, mode, name, *, tm, tn, tk, out_dtype=F32, add=None, dims=None, a_spec=None, b_spec=None,
        o_spec=None, out_shape=None, n_outer=False, copy_dtype=None, after=None):
    if dims is None:
        if mode == "nn":
            (M, K), (_, N) = a.shape, b.shape
        elif mode == "nt":
            (M, K), (N, _) = a.shape, b.shape
        else:
            (K, M), (_, N) = a.shape, b.shape
    else:
        M, N, K = dims
    a_blk = (tk, tm) if mode == "tn" else (tm, tk)
    b_blk = (tn, tk) if mode == "nt" else (tk, tn)
    if a_spec is None:
        a_spec = pl.BlockSpec(a_blk, (lambda i, j, k: (k, i)) if mode == "tn" else (lambda i, j, k: (i, k)))
    if b_spec is None:
        b_spec = pl.BlockSpec(b_blk, (lambda i, j, k: (j, k)) if mode == "nt" else (lambda i, j, k: (k, j)))
    if o_spec is None:
        o_spec = pl.BlockSpec((tm, tn), lambda i, j, k: (i, j))
    if out_shape is None:
        out_shape = (M, N)
    assert M % tm == 0 and N % tn == 0 and K % tk == 0, (name, M, N, K, tm, tn, tk)
    nk = K // tk
    contract = _DIMS[mode]
    has_add = add is not None

    def body(*refs):
        a_ref, b_ref = refs[0], refs[1]
        add_ref = refs[2] if has_add else None
        n_in = 2 + has_add + (after is not None)
        o_ref = refs[n_in]
        copy_ref = refs[n_in + 1] if copy_dtype is not None else None

        def product():
            return lax.dot_general(a_ref[...].astype(MXU_DTYPE), b_ref[...].astype(MXU_DTYPE), contract,
                                   preferred_element_type=F32)

        def finish(r):
            if has_add:
                r = r + add_ref[...]
            o_ref[...] = r.astype(out_dtype)
            if copy_ref is not None:
                copy_ref[...] = r.astype(copy_dtype)

        if nk == 1:
            finish(product())
            return
        acc = refs[-1]
        k = pl.program_id(2)

        @pl.when(k == 0)
        def _():
            acc[...] = jnp.zeros_like(acc)

        acc[...] += product()

        @pl.when(k == nk - 1)
        def _():
            finish(acc[...])

    in_specs = [a_spec, b_spec]
    args = [a, b]
    nbytes = _nbytes(a_blk, a.dtype) + _nbytes(b_blk, b.dtype) + 3 * _nbytes((tm, tn), F32)
    if has_add:
        in_specs.append(pl.BlockSpec((tm, tn), lambda i, j, k: (i, j)))
        args.append(add)
        nbytes += _nbytes((tm, tn), F32)
    if after is not None:
        in_specs.append(pl.BlockSpec(after.shape, lambda i, j, k: (0, 0)))
        args.append(after)
    grid = (M // tm, N // tn, nk)
    if n_outer:
        def swapped(spec):
            return pl.BlockSpec(spec.block_shape, lambda j, i, k, at=spec.index_map: at(i, j, k))

        grid = (N // tn, M // tm, nk)
        in_specs = [swapped(s) for s in in_specs]
        o_spec = swapped(o_spec)
    out_sds, out_specs = jax.ShapeDtypeStruct(out_shape, out_dtype), o_spec
    if copy_dtype is not None:
        out_sds, out_specs = (out_sds, jax.ShapeDtypeStruct(out_shape, copy_dtype)), (o_spec, o_spec)
    return pl.pallas_call(
        body, name=name, out_shape=out_sds, grid=grid, in_specs=in_specs, out_specs=out_specs,
        scratch_shapes=[pltpu.VMEM((tm, tn), F32)] if nk > 1 else [],
        compiler_params=_params(("parallel", "parallel", "arbitrary"), nbytes),
    )(*args)


_GELU_C = math.sqrt(2.0 / math.pi)
_GELU_A = 0.044715


def _sigmoid(x):
    return 0.5 * jnp.tanh(0.5 * x) + 0.5


def _gelu(x):
    t = jnp.tanh(x * (_GELU_C + (_GELU_C * _GELU_A) * (x * x)))
    return x * (0.5 + 0.5 * t)


def _gelu_and_grad(x):
    x2 = x * x
    t = jnp.tanh(x * (_GELU_C + (_GELU_C * _GELU_A) * x2))
    cdf = 0.5 + 0.5 * t
    grad = cdf + (0.5 * x) * (1.0 - t * t) * (_GELU_C + (3.0 * _GELU_C * _GELU_A) * x2)
    return x * cdf, grad


def _rope_mix(g, cos_a, sin_a):
    return g * cos_a + pltpu.roll(g, 64, 1) * sin_a


def _rope_mix_bwd(d, cos_a, sin_a):
    return d * cos_a + pltpu.roll(d * sin_a, 64, 1)


def _rms_fwd(x, g, name, tr=512):
    T, D = x.shape

    def body(x_ref, g_ref, h_ref):
        xv = x_ref[...]
        r = lax.rsqrt(jnp.mean(xv * xv, axis=-1, keepdims=True) + EPS)
        h_ref[...] = ((xv * r) * g_ref[...]).astype(h_ref.dtype)

    return pl.pallas_call(
        body, name=name, out_shape=jax.ShapeDtypeStruct((T, D), MXU_DTYPE), grid=(T // tr,),
        in_specs=[pl.BlockSpec((tr, D), lambda i: (i, 0)), pl.BlockSpec((1, D), lambda i: (0, 0))],
        out_specs=pl.BlockSpec((tr, D), lambda i: (i, 0)),
        compiler_params=_params(("parallel",), 3 * _nbytes((tr, D), F32)),
    )(x, g)


def _rms_bwd(x, g, dh, dres, name, tr=512):
    T, D = x.shape

    def body(x_ref, g_ref, dh_ref, dres_ref, dx_ref, gg_ref):
        @pl.when(pl.program_id(0) == 0)
        def _():
            gg_ref[...] = jnp.zeros_like(gg_ref)

        xv = x_ref[...]
        r = lax.rsqrt(jnp.mean(xv * xv, axis=-1, keepdims=True) + EPS)
        xn = xv * r
        dhv = dh_ref[...]
        dxn = dhv * g_ref[...]
        dx_ref[...] = dres_ref[...] + r * (dxn - xn * jnp.mean(dxn * xn, axis=-1, keepdims=True))
        gg_ref[...] += jnp.sum(dhv * xn, axis=0, keepdims=True)

    row = pl.BlockSpec((tr, D), lambda i: (i, 0))
    vec = pl.BlockSpec((1, D), lambda i: (0, 0))
    return pl.pallas_call(
        body, name=name,
        out_shape=(jax.ShapeDtypeStruct((T, D), F32), jax.ShapeDtypeStruct((1, D), F32)),
        grid=(T // tr,), in_specs=[row, vec, row, row], out_specs=(row, vec),
        compiler_params=_params(("arbitrary",), 6 * _nbytes((tr, D), F32)),
    )(x, g, dh, dres)


def _lat_fwd(z, gq, gkv, wq, wkv, cos_a, sin_a, tr=256):
    T = z.shape[0]
    lat_blk = (4 * D_MODEL) // LAT

    def body(z_ref, gq_ref, gkv_ref, wq_ref, wkv_ref, cos_ref, sin_ref, q_ref, k_ref, v_ref, cqn_ref, ckvn_ref):
        zl = z_ref[...]
        cos_v, sin_v = cos_ref[...], sin_ref[...]
        cq = zl[:, :Q_RANK]
        ckv = zl[:, Q_RANK:Q_RANK + KV_RANK]
        krb = zl[:, Q_RANK + KV_RANK:]
        cqn = ((cq * lax.rsqrt(jnp.mean(cq * cq, axis=-1, keepdims=True) + EPS)) * gq_ref[...]).astype(MXU_DTYPE)
        ckvn = ((ckv * lax.rsqrt(jnp.mean(ckv * ckv, axis=-1, keepdims=True) + EPS)) * gkv_ref[...]).astype(MXU_DTYPE)
        cqn_ref[...] = cqn
        ckvn_ref[...] = ckvn
        krr = _rope_mix(krb, cos_v, sin_v).astype(MXU_DTYPE)
        q = jnp.dot(cqn, wq_ref[...], preferred_element_type=F32)
        kv = jnp.dot(ckvn, wkv_ref[...], preferred_element_type=F32)
        for h in range(HEADS):
            o = h * HEAD_PAD
            q_ref[:, o:o + NOPE] = q[:, o:o + NOPE].astype(MXU_DTYPE)
            q_ref[:, o + NOPE:o + HEAD_PAD] = _rope_mix(q[:, o + NOPE:o + HEAD_PAD], cos_v, sin_v).astype(MXU_DTYPE)
            k_ref[:, o:o + NOPE] = kv[:, h * NOPE:(h + 1) * NOPE].astype(MXU_DTYPE)
            k_ref[:, o + NOPE:o + HEAD_PAD] = krr
        v_ref[...] = kv[:, HEADS * NOPE:].astype(MXU_DTYPE)

    def row(w):
        return pl.BlockSpec((tr, w), lambda i: (i, 0))

    def full(a):
        return pl.BlockSpec(a.shape, lambda i: (0, 0))

    return pl.pallas_call(
        body, name="lat_fwd",
        out_shape=(jax.ShapeDtypeStruct((T, HEADS * HEAD_PAD), MXU_DTYPE), jax.ShapeDtypeStruct((T, HEADS * HEAD_PAD), MXU_DTYPE),
                   jax.ShapeDtypeStruct((T, HEADS * NOPE), MXU_DTYPE), jax.ShapeDtypeStruct((T, Q_RANK), MXU_DTYPE),
                   jax.ShapeDtypeStruct((T, KV_RANK), MXU_DTYPE)),
        grid=(T // tr,),
        in_specs=[pl.BlockSpec((tr, LAT), lambda i: (i, lat_blk)), full(gq), full(gkv), full(wq), full(wkv), row(128), row(128)],
        out_specs=(row(HEADS * HEAD_PAD), row(HEADS * HEAD_PAD), row(HEADS * NOPE), row(Q_RANK), row(KV_RANK)),
        compiler_params=_params(("parallel",), 8 * _nbytes((tr, HEADS * HEAD_PAD), F32)),
    )(z, gq, gkv, wq, wkv, cos_a, sin_a)


ATT_BLOCK = 256
_SCALE = QK_DIM ** -0.5


def _causal_mask(n):
    return lax.broadcasted_iota(jnp.int32, (n, n), 1) <= lax.broadcasted_iota(jnp.int32, (n, n), 0)


def _causal_mask_t(n):
    return lax.broadcasted_iota(jnp.int32, (n, n), 0) <= lax.broadcasted_iota(jnp.int32, (n, n), 1)


ATT_HEADS = 4


def _attn_fwd(q, k, v, B, S):
    tq = ATT_BLOCK
    nq = S // tq
    T = B * S
    hp, groups = ATT_HEADS, HEADS // ATT_HEADS

    def body(q_ref, k_ref, v_ref, o_ref, *lse_refs):
        qi = pl.program_id(2)
        qs = [q_ref[:, t * HEAD_PAD:(t + 1) * HEAD_PAD] for t in range(hp)]

        def scores(j, t):
            rows = pl.ds(pl.multiple_of(j * tq, tq), tq)
            return lax.dot_general(k_ref[rows, t * HEAD_PAD:(t + 1) * HEAD_PAD], qs[t], _DIMS["nt"],
                                   preferred_element_type=F32)

        def step(j, carry, last):
            rows = pl.ds(pl.multiple_of(j * tq, tq), tq)
            out = []
            for t in range(hp):
                m, l, acc, st = carry[t]
                st_next = st if last else scores(j + 1, t)
                st = st * _SCALE
                if last:
                    st = jnp.where(_causal_mask_t(tq), st, NEG)
                m_new = jnp.maximum(m, jnp.max(st, axis=0, keepdims=True))
                alpha = jnp.exp(m - m_new)
                p = jnp.exp(st - m_new)
                l = alpha * l + jnp.sum(p, axis=0, keepdims=True)
                acc = alpha * acc + lax.dot_general(v_ref[rows, t * NOPE:(t + 1) * NOPE], p.astype(MXU_DTYPE),
                                                    _DIMS["tn"], preferred_element_type=F32)
                out.append((m_new, l, acc, st_next))
            return tuple(out)

        init = tuple((jnp.full((1, tq), NEG, F32), jnp.zeros((1, tq), F32), jnp.zeros((NOPE, tq), F32), scores(0, t))
                     for t in range(hp))
        carry = lax.fori_loop(0, qi, lambda j, c: step(j, c, False), init)
        carry = step(qi, carry, True)
        for t in range(hp):
            m, l, acc, _ = carry[t]
            o_ref[:, t * NOPE:(t + 1) * NOPE] = (acc / l).T
            lse_refs[t][0] = m + jnp.log(l)

    lse_sds = jax.ShapeDtypeStruct((groups * B * nq, 1, tq), F32)
    lse_spec = pl.BlockSpec((1, 1, tq), lambda b, h, i: ((h * B + b) * nq + i, 0, 0))
    return pl.pallas_call(
        body, name="attn_fwd",
        out_shape=(jax.ShapeDtypeStruct((T, HEADS * NOPE), F32),) + (lse_sds,) * hp,
        grid=(B, groups, nq),
        in_specs=[pl.BlockSpec((tq, hp * HEAD_PAD), lambda b, h, i: (b * nq + i, h)),
                  pl.BlockSpec((S, hp * HEAD_PAD), lambda b, h, i: (b, h)),
                  pl.BlockSpec((S, hp * NOPE), lambda b, h, i: (b, h))],
        out_specs=(pl.BlockSpec((tq, hp * NOPE), lambda b, h, i: (b * nq + i, h)),) + (lse_spec,) * hp,
        compiler_params=_params(("parallel", "parallel", "arbitrary"), 4 * hp * _nbytes((S, HEAD_PAD), MXU_DTYPE)),
    )(q, k, v)


def _attn_bwd(q, k, v, do, lses, delta, B, S):
    tq = ATT_BLOCK
    nq = S // tq
    T = B * S
    hp, groups = ATT_HEADS, HEADS // ATT_HEADS

    def body(q_ref, k_ref, v_ref, do_ref, *refs):
        lse_refs, dl_refs = refs[:hp], refs[hp:2 * hp]
        dq_out, dk_ref, dv_ref, dq_ref = refs[2 * hp:]
        kj = pl.program_id(2)

        @pl.when(kj == 0)
        def _():
            dq_ref[...] = jnp.zeros_like(dq_ref)

        def products(i, t):
            rows = pl.ds(pl.multiple_of(i * tq, tq), tq)
            st = lax.dot_general(k_ref[:, t * HEAD_PAD:(t + 1) * HEAD_PAD], q_ref[rows, t * HEAD_PAD:(t + 1) * HEAD_PAD],
                                 _DIMS["nt"], preferred_element_type=F32)
            dpt = lax.dot_general(v_ref[:, t * NOPE:(t + 1) * NOPE], do_ref[rows, t * NOPE:(t + 1) * NOPE],
                                  _DIMS["nt"], preferred_element_type=F32)
            return st, dpt

        def step(i, carry, masked):
            rows = pl.ds(pl.multiple_of(i * tq, tq), tq)
            nxt = jnp.minimum(i + 1, nq - 1)
            out = []
            for t in range(hp):
                dk, dv, st, dpt = carry[t]
                st_next, dpt_next = products(nxt, t)
                qk_cols = slice(t * HEAD_PAD, (t + 1) * HEAD_PAD)
                v_cols = slice(t * NOPE, (t + 1) * NOPE)
                p = jnp.exp(st * _SCALE - lse_refs[t][i])
                if masked:
                    p = jnp.where(_causal_mask_t(tq), p, 0.0)
                dv = dv + jnp.dot(p.astype(MXU_DTYPE), do_ref[rows, v_cols], preferred_element_type=F32)
                ds = (p * (dpt - dl_refs[t][i]) * _SCALE).astype(MXU_DTYPE)
                dk = dk + jnp.dot(ds, q_ref[rows, qk_cols], preferred_element_type=F32)
                dq_ref[rows, qk_cols] += lax.dot_general(ds, k_ref[:, qk_cols], _DIMS["tn"], preferred_element_type=F32)
                out.append((dk, dv, st_next, dpt_next))
            return tuple(out)

        init = tuple((jnp.zeros((tq, HEAD_PAD), F32), jnp.zeros((tq, NOPE), F32)) + products(kj, t) for t in range(hp))
        carry = step(kj, init, True)
        carry = lax.fori_loop(kj + 1, nq, lambda i, c: step(i, c, False), carry)
        for t in range(hp):
            dk_ref[:, t * HEAD_PAD:(t + 1) * HEAD_PAD] = carry[t][0].astype(dk_ref.dtype)
            dv_ref[:, t * NOPE:(t + 1) * NOPE] = carry[t][1].astype(dv_ref.dtype)

        @pl.when(kj == nq - 1)
        def _():
            dq_out[...] = dq_ref[...].astype(dq_out.dtype)

    seq = lambda w: pl.BlockSpec((S, w), lambda b, h, j: (b, h))
    blk = lambda w: pl.BlockSpec((tq, w), lambda b, h, j: (b * nq + j, h))
    lse_spec = pl.BlockSpec((nq, 1, tq), lambda b, h, j: (h * B + b, 0, 0))
    dl_specs = [pl.BlockSpec((nq, 1, tq), lambda b, h, j, t=t: ((h * hp + t) * B + b, 0, 0)) for t in range(hp)]
    return pl.pallas_call(
        body, name="attn_bwd",
        out_shape=(jax.ShapeDtypeStruct((T, HEADS * HEAD_PAD), MXU_DTYPE), jax.ShapeDtypeStruct((T, HEADS * HEAD_PAD), MXU_DTYPE),
                   jax.ShapeDtypeStruct((T, HEADS * NOPE), MXU_DTYPE)),
        grid=(B, groups, nq),
        in_specs=[seq(hp * HEAD_PAD), blk(hp * HEAD_PAD), blk(hp * NOPE), seq(hp * NOPE)] + [lse_spec] * hp + dl_specs,
        out_specs=(seq(hp * HEAD_PAD), blk(hp * HEAD_PAD), blk(hp * NOPE)),
        scratch_shapes=[pltpu.VMEM((S, hp * HEAD_PAD), F32)],
        compiler_params=_params(("parallel", "parallel", "arbitrary"), 8 * hp * _nbytes((S, HEAD_PAD), F32)),
    )(q, k, v, do, *lses, *([delta] * hp))


MIX_ROWS = 256


def _tril_weights(ws_ref, g):
    return jnp.where(_causal_mask(CHUNK), ws_ref[g], 0.0).astype(MXU_DTYPE)


def _layer_norm_stats(va):
    mu = jnp.mean(va, axis=-1, keepdims=True)
    xc = va - mu
    rs = lax.rsqrt(jnp.mean(xc * xc, axis=-1, keepdims=True) + EPS)
    return xc * rs


def _mix_specs(tr):
    zcol = lambda c: pl.BlockSpec((tr, D_MODEL), lambda i, c=c: (i, c))
    row = pl.BlockSpec((tr, D_MODEL), lambda i: (i, 0))
    vec = pl.BlockSpec((1, D_MODEL), lambda i: (0, 0))
    ws = pl.BlockSpec((A_GROUPS, CHUNK, CHUNK), lambda i: (0, 0, 0))
    bs = pl.BlockSpec((CHUNK, 128), lambda i: (0, 0))
    return zcol, row, vec, ws, bs


def _mix_fwd(z, yb, ln_g, ln_b, ws, bs_t):
    T = z.shape[0]
    tr = MIX_ROWS
    zcol, row, vec, ws_spec, bs_spec = _mix_specs(tr)

    def body(zu_ref, zv_ref, zga_ref, zgb_ref, yb_ref, g_ref, b_ref, ws_ref, bs_ref, out_ref, vn_s):
        vhat = _layer_norm_stats(_gelu(zv_ref[...]))
        vn_s[...] = (vhat * g_ref[...] + b_ref[...]).astype(MXU_DTYPE)
        for g in range(A_GROUPS):
            w = _tril_weights(ws_ref, g)
            bias = bs_ref[:, g:g + 1]
            cols = slice(g * CHUNK, (g + 1) * CHUNK)
            for c in range(tr // CHUNK):
                rows = slice(c * CHUNK, (c + 1) * CHUNK)
                mixed = jnp.dot(w, vn_s[rows, cols], preferred_element_type=F32) + bias
                ya = _gelu(zu_ref[rows, cols]) * mixed
                merged = _sigmoid(zga_ref[rows, cols]) * ya + _sigmoid(zgb_ref[rows, cols]) * yb_ref[rows, cols]
                out_ref[rows, cols] = merged.astype(MXU_DTYPE)

    return pl.pallas_call(
        body, name="mix_fwd", out_shape=jax.ShapeDtypeStruct((T, D_MODEL), MXU_DTYPE), grid=(T // tr,),
        in_specs=[zcol(0), zcol(1), zcol(2), zcol(3), row, vec, vec, ws_spec, bs_spec], out_specs=row,
        scratch_shapes=[pltpu.VMEM((tr, D_MODEL), MXU_DTYPE)],
        compiler_params=_params(("parallel",), 8 * _nbytes((tr, D_MODEL), F32)),
    )(z, z, z, z, yb, ln_g, ln_b, ws, bs_t)


def _mix_bwd(z, yb, dm, ln_g, ln_b, ws, bs_t):
    T = z.shape[0]
    tr = MIX_ROWS
    zcol, row, vec, ws_spec, bs_spec = _mix_specs(tr)

    def body(zu_ref, zv_ref, zga_ref, zgb_ref, yb_ref, dm_ref, g_ref, b_ref, ws_ref, bs_ref,
             dz_ref, dyb_ref, dl_ref, gws_ref, gbs_ref, glg_ref, glb_ref, vn_s, dvn_s):
        @pl.when(pl.program_id(0) == 0)
        def _():
            gws_ref[...] = jnp.zeros_like(gws_ref)
            gbs_ref[...] = jnp.zeros_like(gbs_ref)
            glg_ref[...] = jnp.zeros_like(glg_ref)
            glb_ref[...] = jnp.zeros_like(glb_ref)

        lane = lax.broadcasted_iota(jnp.int32, (CHUNK, 128), 1)
        va, dgelu_v = _gelu_and_grad(zv_ref[...])
        mu = jnp.mean(va, axis=-1, keepdims=True)
        xc = va - mu
        rs = lax.rsqrt(jnp.mean(xc * xc, axis=-1, keepdims=True) + EPS)
        vhat = xc * rs
        vn_s[...] = (vhat * g_ref[...] + b_ref[...]).astype(MXU_DTYPE)
        gbs_acc = jnp.zeros((CHUNK, 128), F32)
        for g in range(A_GROUPS):
            w = _tril_weights(ws_ref, g)
            bias = bs_ref[:, g:g + 1]
            cols = slice(g * CHUNK, (g + 1) * CHUNK)
            gw_acc = jnp.zeros((CHUNK, CHUNK), F32)
            for c in range(tr // CHUNK):
                rows = slice(c * CHUNK, (c + 1) * CHUNK)
                vn = vn_s[rows, cols]
                mixed = jnp.dot(w, vn, preferred_element_type=F32) + bias
                ua, dgelu_u = _gelu_and_grad(zu_ref[rows, cols])
                dmv = dm_ref[rows, cols]
                sa = _sigmoid(zga_ref[rows, cols])
                dya = dmv * sa
                dz_ref[rows, 2 * D_MODEL + g * CHUNK:2 * D_MODEL + (g + 1) * CHUNK] = (
                    dmv * (ua * mixed) * (sa * (1.0 - sa))).astype(dz_ref.dtype)
                dz_ref[rows, cols] = (dya * mixed * dgelu_u).astype(dz_ref.dtype)
                dmix = dya * ua
                gbs_acc = gbs_acc + jnp.where(lane == g, jnp.sum(dmix, axis=-1, keepdims=True), 0.0)
                dmix_b = dmix.astype(MXU_DTYPE)
                gw_acc = gw_acc + lax.dot_general(dmix_b, vn, _DIMS["nt"], preferred_element_type=F32)
                dvn_s[rows, cols] = lax.dot_general(w, dmix_b, _DIMS["tn"], preferred_element_type=F32)
            gws_ref[g] += jnp.where(_causal_mask(CHUNK), gw_acc, 0.0)
        gbs_ref[...] += gbs_acc

        dvn = dvn_s[...]
        glg_ref[...] += jnp.sum(dvn * vhat, axis=0, keepdims=True)
        glb_ref[...] += jnp.sum(dvn, axis=0, keepdims=True)
        dvh = dvn * g_ref[...]
        dva = rs * (dvh - jnp.mean(dvh, axis=-1, keepdims=True) - vhat * jnp.mean(dvh * vhat, axis=-1, keepdims=True))
        dz_ref[:, D_MODEL:2 * D_MODEL] = (dva * dgelu_v).astype(dz_ref.dtype)

        dmv = dm_ref[...]
        ybv = yb_ref[...]
        sb = _sigmoid(zgb_ref[...])
        dyb = dmv * sb
        dyb_ref[...] = dyb.astype(dyb_ref.dtype)
        dz_ref[:, 3 * D_MODEL:4 * D_MODEL] = (dmv * ybv * (sb * (1.0 - sb))).astype(dz_ref.dtype)
        dz_ref[:, 4 * D_MODEL:] = jnp.zeros((tr, LAT), dz_ref.dtype)
        prod = dyb * ybv
        sel = (lax.broadcasted_iota(jnp.int32, (HEADS, D_MODEL), 1) // NOPE
               == lax.broadcasted_iota(jnp.int32, (HEADS, D_MODEL), 0)).astype(jnp.bfloat16)
        hi = prod.astype(jnp.bfloat16)
        rest = prod - hi.astype(F32)
        mid = rest.astype(jnp.bfloat16)
        lo = (rest - mid.astype(F32)).astype(jnp.bfloat16)
        dl_ref[...] = (lax.dot_general(sel, hi, _DIMS["nt"], preferred_element_type=F32)
                       + lax.dot_general(sel, mid, _DIMS["nt"], preferred_element_type=F32)
                       + lax.dot_general(sel, lo, _DIMS["nt"], preferred_element_type=F32))

    return pl.pallas_call(
        body, name="mix_bwd",
        out_shape=(jax.ShapeDtypeStruct((T, IN_PAD), MXU_DTYPE), jax.ShapeDtypeStruct((T, D_MODEL), MXU_DTYPE),
                   jax.ShapeDtypeStruct((HEADS, T), F32), jax.ShapeDtypeStruct((A_GROUPS, CHUNK, CHUNK), F32),
                   jax.ShapeDtypeStruct((CHUNK, 128), F32), jax.ShapeDtypeStruct((1, D_MODEL), F32),
                   jax.ShapeDtypeStruct((1, D_MODEL), F32)),
        grid=(T // tr,),
        in_specs=[zcol(0), zcol(1), zcol(2), zcol(3), row, row, vec, vec, ws_spec, bs_spec],
        out_specs=(pl.BlockSpec((tr, IN_PAD), lambda i: (i, 0)), row, pl.BlockSpec((HEADS, tr), lambda i: (0, i)),
                   ws_spec, bs_spec, vec, vec),
        scratch_shapes=[pltpu.VMEM((tr, D_MODEL), MXU_DTYPE), pltpu.VMEM((tr, D_MODEL), F32)],
        compiler_params=_params(("arbitrary",), 12 * _nbytes((tr, D_MODEL), F32)),
    )(z, z, z, z, yb, dm, ln_g, ln_b, ws, bs_t)


def _lat_bwd(dz, z, dq, dk, dv, gq, gkv, wq, wkv, cos_a, sin_a, tr=256):
    T = z.shape[0]
    lat_blk = (4 * D_MODEL) // LAT

    def body(dz_in, z_ref, dq_ref, dk_ref, dv_ref, gq_ref, gkv_ref, wq_ref, wkv_ref, cos_ref, sin_ref,
             dz_ref, dqr_ref, dkv_ref, ggq_ref, ggkv_ref):
        del dz_in

        @pl.when(pl.program_id(0) == 0)
        def _():
            ggq_ref[...] = jnp.zeros_like(ggq_ref)
            ggkv_ref[...] = jnp.zeros_like(ggkv_ref)

        cos_v, sin_v = cos_ref[...], sin_ref[...]
        dkr = jnp.zeros((tr, 128), F32)
        for h in range(HEADS):
            o = h * HEAD_PAD
            dqr_ref[:, o:o + NOPE] = dq_ref[:, o:o + NOPE].astype(MXU_DTYPE)
            dqr_ref[:, o + NOPE:o + HEAD_PAD] = _rope_mix_bwd(dq_ref[:, o + NOPE:o + HEAD_PAD], cos_v, sin_v).astype(MXU_DTYPE)
            dkv_ref[:, h * NOPE:(h + 1) * NOPE] = dk_ref[:, o:o + NOPE].astype(MXU_DTYPE)
            dkr = dkr + _rope_mix_bwd(dk_ref[:, o + NOPE:o + HEAD_PAD], cos_v, sin_v)
        dkv_ref[:, HEADS * NOPE:] = dv_ref[...]
        dcqn = lax.dot_general(dqr_ref[...], wq_ref[...], _DIMS["nt"], preferred_element_type=F32)
        dckvn = lax.dot_general(dkv_ref[...], wkv_ref[...], _DIMS["nt"], preferred_element_type=F32)

        zl = z_ref[...]

        def rms_bwd(c, dn, g_ref, gg_ref):
            r = lax.rsqrt(jnp.mean(c * c, axis=-1, keepdims=True) + EPS)
            ch = c * r
            gg_ref[...] += jnp.sum(dn * ch, axis=0, keepdims=True)
            dch = dn * g_ref[...]
            return r * (dch - ch * jnp.mean(dch * ch, axis=-1, keepdims=True))

        dz_ref[:, :Q_RANK] = rms_bwd(zl[:, :Q_RANK], dcqn, gq_ref, ggq_ref).astype(dz_ref.dtype)
        dz_ref[:, Q_RANK:Q_RANK + KV_RANK] = rms_bwd(zl[:, Q_RANK:Q_RANK + KV_RANK], dckvn, gkv_ref, ggkv_ref).astype(dz_ref.dtype)
        dz_ref[:, Q_RANK + KV_RANK:] = dkr.astype(dz_ref.dtype)

    def row(w):
        return pl.BlockSpec((tr, w), lambda i: (i, 0))

    def full(a):
        return pl.BlockSpec(a.shape, lambda i: (0, 0))

    lat = pl.BlockSpec((tr, LAT), lambda i: (i, lat_blk))
    return pl.pallas_call(
        body, name="lat_bwd",
        out_shape=(jax.ShapeDtypeStruct(dz.shape, dz.dtype), jax.ShapeDtypeStruct((T, HEADS * HEAD_PAD), MXU_DTYPE),
                   jax.ShapeDtypeStruct((T, 2 * HEADS * NOPE), MXU_DTYPE), jax.ShapeDtypeStruct(gq.shape, F32),
                   jax.ShapeDtypeStruct(gkv.shape, F32)),
        grid=(T // tr,),
        in_specs=[pl.BlockSpec(memory_space=pl.ANY), lat, row(HEADS * HEAD_PAD), row(HEADS * HEAD_PAD), row(HEADS * NOPE),
                  full(gq), full(gkv), full(wq), full(wkv), row(128), row(128)],
        out_specs=(lat, row(HEADS * HEAD_PAD), row(2 * HEADS * NOPE), full(gq), full(gkv)),
        input_output_aliases={0: 0},
        compiler_params=_params(("arbitrary",), 8 * _nbytes((tr, HEADS * HEAD_PAD), F32)),
    )(dz, z, dq, dk, dv, gq, gkv, wq, wkv, cos_a, sin_a)


GATE_ROWS = 64
HALO = 8


def _taps(ref, half, r, first):
    C = GATE_ROWS
    if first:
        xs = jnp.concatenate([jnp.zeros((HALO, ref.shape[-1]), F32), ref[half, 0:C, :]], axis=0)
    else:
        xs = ref[half, pl.ds(pl.multiple_of(r * C - HALO, HALO), C + HALO), :]
    return xs[HALO:, :], pltpu.roll(xs, 1, 0)[HALO:, :], pltpu.roll(xs, 2, 0)[HALO:, :]


def _conv_taps(taps, cw, cb):
    x0, x1, x2 = taps
    return cb + cw[0:1, :] * x2 + cw[1:2, :] * x1 + cw[2:3, :] * x0


def _fold8(x):
    acc = x[0:8, :]
    for i in range(1, x.shape[0] // 8):
        acc = acc + x[8 * i:8 * (i + 1), :]
    return acc


def _gate_fwd(up3, conv_w, conv_b, B, S):
    T = B * S
    W = FF_TILE
    C = GATE_ROWS

    def body(up_ref, cw_ref, cb_ref, act_ref):
        def chunk(r, first):
            gate = _conv_taps(_taps(up_ref, 0, r, first), cw_ref[0], cb_ref[0])
            val = _conv_taps(_taps(up_ref, 1, r, first), cw_ref[1], cb_ref[1])
            base = 0 if first else pl.multiple_of(r * C, C)
            act_ref[pl.ds(base, C), :] = (gate * _sigmoid(gate) * val).astype(act_ref.dtype)

        chunk(0, True)

        @pl.loop(1, S // C)
        def _(r):
            chunk(r, False)

    return pl.pallas_call(
        body, name="gate_fwd", out_shape=jax.ShapeDtypeStruct((T, D_FF), MXU_DTYPE), grid=(B, N_FF_TILES),
        in_specs=[pl.BlockSpec((2, S, W), lambda b, j: (0, b, j)), pl.BlockSpec((2, 3, W), lambda b, j: (0, 0, j)),
                  pl.BlockSpec((2, 1, W), lambda b, j: (0, 0, j))],
        out_specs=pl.BlockSpec((S, W), lambda b, j: (b, j)),
        compiler_params=_params(("parallel", "parallel"), 6 * _nbytes((S, W), F32)),
    )(up3, conv_w, conv_b)


def _gate_bwd(up3, dact, conv_w, conv_b, B, S):
    T = B * S
    W = FF_TILE
    C = GATE_ROWS

    def body(up_ref, da_ref, cw_ref, cb_ref, dup_ref, gcw_ref, gcb_ref, d_s):
        @pl.when(pl.program_id(1) == 0)
        def _():
            gcw_ref[...] = jnp.zeros_like(gcw_ref)
            gcb_ref[...] = jnp.zeros_like(gcb_ref)

        def chunk(r, first, sums):
            rows = pl.ds(0 if first else pl.multiple_of(r * C, C), C)
            taps = [_taps(up_ref, half, r, first) for half in (0, 1)]
            gate = _conv_taps(taps[0], cw_ref[0], cb_ref[0])
            val = _conv_taps(taps[1], cw_ref[1], cb_ref[1])
            sg = _sigmoid(gate)
            da = da_ref[rows, :]
            d_halves = (da * val * (sg * (1.0 + gate * (1.0 - sg))), da * (gate * sg))
            out = []
            for half, dup in enumerate(d_halves):
                d_s[half, rows, :] = dup
                x0, x1, x2 = taps[half]
                sb, s0, s1, s2 = sums[half]
                out.append((sb + _fold8(dup), s0 + _fold8(dup * x2), s1 + _fold8(dup * x1), s2 + _fold8(dup * x0)))
            return tuple(out)

        zeros = tuple(tuple(jnp.zeros((8, W), F32) for _ in range(4)) for _ in range(2))
        sums = chunk(0, True, zeros)
        sums = lax.fori_loop(1, S // C, lambda r, s: chunk(r, False, s), sums)
        for half in (0, 1):
            sb, s0, s1, s2 = sums[half]
            gcb_ref[half] += jnp.sum(sb, axis=0, keepdims=True)
            gcw_ref[half, 0:1, :] += jnp.sum(s0, axis=0, keepdims=True)
            gcw_ref[half, 1:2, :] += jnp.sum(s1, axis=0, keepdims=True)
            gcw_ref[half, 2:3, :] += jnp.sum(s2, axis=0, keepdims=True)

        d_s[:, S:S + HALO, :] = jnp.zeros((2, HALO, W), F32)

        @pl.loop(0, S // C)
        def _(r):
            base = pl.multiple_of(r * C, C)
            for half in (0, 1):
                ds_ = d_s[half, pl.ds(base, C + HALO), :]
                cw = cw_ref[half]
                dx = (cw[2:3, :] * ds_[:C, :] + cw[1:2, :] * pltpu.roll(ds_, C + HALO - 1, 0)[:C, :]
                      + cw[0:1, :] * pltpu.roll(ds_, C + HALO - 2, 0)[:C, :])
                dup_ref[half, pl.ds(base, C), :] = dx.astype(dup_ref.dtype)

    up_spec = pl.BlockSpec((2, S, W), lambda j, b: (0, b, j))
    cw_spec = pl.BlockSpec((2, 3, W), lambda j, b: (0, 0, j))
    cb_spec = pl.BlockSpec((2, 1, W), lambda j, b: (0, 0, j))
    return pl.pallas_call(
        body, name="gate_bwd",
        out_shape=(jax.ShapeDtypeStruct((2, T, D_FF), MXU_DTYPE), jax.ShapeDtypeStruct((2, 3, D_FF), F32),
                   jax.ShapeDtypeStruct((2, 1, D_FF), F32)),
        grid=(N_FF_TILES, B),
        in_specs=[up_spec, pl.BlockSpec((S, W), lambda j, b: (b, j)), cw_spec, cb_spec],
        out_specs=(up_spec, cw_spec, cb_spec),
        scratch_shapes=[pltpu.VMEM((2, S + HALO, W), F32)],
        compiler_params=_params(("parallel", "arbitrary"), 10 * _nbytes((S, W), F32)),
    )(up3, dact, conv_w, conv_b)


def _final(x2, tgt, g, tr=512):
    T, D = x2.shape

    def body(x_ref, t_ref, g_ref, dx_ref, loss_ref, gg_ref):
        @pl.when(pl.program_id(0) == 0)
        def _():
            loss_ref[...] = jnp.zeros_like(loss_ref)
            gg_ref[...] = jnp.zeros_like(gg_ref)

        xv = x_ref[...]
        gv = g_ref[...]
        r = lax.rsqrt(jnp.mean(xv * xv, axis=-1, keepdims=True) + EPS)
        xn = xv * r
        err = xn * gv - t_ref[...]
        loss_ref[...] += 0.5 * jnp.sum(jnp.mean(err * err, axis=-1, keepdims=True), axis=0, keepdims=True)
        dy = err * (1.0 / D)
        gg_ref[...] += jnp.sum(dy * xn, axis=0, keepdims=True)
        dxn = dy * gv
        dx_ref[...] = r * (dxn - xn * jnp.mean(dxn * xn, axis=-1, keepdims=True))

    row = pl.BlockSpec((tr, D), lambda i: (i, 0))
    vec = pl.BlockSpec((1, D), lambda i: (0, 0))
    return pl.pallas_call(
        body, name="final_loss",
        out_shape=(jax.ShapeDtypeStruct((T, D), F32), jax.ShapeDtypeStruct((1, 128), F32), jax.ShapeDtypeStruct((1, D), F32)),
        grid=(T // tr,), in_specs=[row, row, vec],
        out_specs=(row, pl.BlockSpec((1, 128), lambda i: (0, 0)), vec),
        compiler_params=_params(("arbitrary",), 6 * _nbytes((tr, D), F32)),
    )(x2, tgt, g)


def _sum_slabs(parts, name, tr):
    rows, cols = parts[0].shape
    n = len(parts)

    def body(*refs):
        acc = refs[0][...]
        for r in refs[1:n]:
            acc = acc + r[...]
        refs[n][...] = acc

    blk = pl.BlockSpec((tr, cols), lambda i: (i, 0))
    return pl.pallas_call(
        body, name=name, out_shape=jax.ShapeDtypeStruct((rows, cols), F32), grid=(rows // tr,),
        in_specs=[blk] * n, out_specs=blk,
        compiler_params=_params(("parallel",), (n + 1) * _nbytes((tr, cols), F32)),
    )(*parts)


ADAMW_BLOCK_BYTES = 2400 * 1024


def _adamw(w, g, m, v, name):
    lead = w.ndim == 3
    rows, cols = w.shape[-2:]
    fits = [d for d in range(8, rows + 1, 8) if rows % d == 0 and d * cols * 4 <= ADAMW_BLOCK_BYTES]
    tr = max(fits) if fits else rows
    c1 = 1.0 - ADAM_B1 ** ADAM_STEP
    c2 = 1.0 - ADAM_B2 ** ADAM_STEP

    def body(w_ref, g_ref, m_ref, v_ref, d_ref, nm_ref, nv_ref):
        gv = g_ref[...]
        nm = ADAM_B1 * m_ref[...] + (1.0 - ADAM_B1) * gv
        nv = ADAM_B2 * v_ref[...] + (1.0 - ADAM_B2) * (gv * gv)
        nm_ref[...] = nm
        nv_ref[...] = nv
        d_ref[...] = -ADAM_LR * ((nm / c1) / (jnp.sqrt(nv / c2) + ADAM_EPS) + ADAM_WD * w_ref[...])

    blk = pl.BlockSpec((None, tr, cols), lambda i: (0, i, 0)) if lead else pl.BlockSpec((tr, cols), lambda i: (i, 0))
    sds = jax.ShapeDtypeStruct(w.shape, F32)
    return pl.pallas_call(
        body, name=name, out_shape=(sds, sds, sds), grid=(rows // tr,), in_specs=[blk] * 4, out_specs=(blk, blk, blk),
        compiler_params=_params(("parallel",), 7 * _nbytes((tr, cols), F32)),
    )(w, g, m, v)


_ANY = pl.BlockSpec(memory_space=pl.ANY)


def _place():
    x, y, c = lax.axis_index("x"), lax.axis_index("y"), lax.axis_index("c")
    chips = [(1 - x, y), (x, 1 - y), (1 - x, 1 - y)]
    return x, y, c, chips


def _gather_weights(shards):
    n = len(shards)

    def body(*refs):
        ins, outs = refs[:n], refs[n:2 * n]
        send, recv, fsend, frecv, osend, orecv = refs[2 * n:]
        x, y, c, chips = _place()
        me = 2 * x + y
        first, passed = [], []
        for w in range(n):
            first.append(pltpu.make_async_remote_copy(
                src_ref=ins[w], dst_ref=outs[w].at[me], send_sem=osend.at[w], recv_sem=orecv.at[w],
                device_id=(x, y, 1 - c), device_id_type=MESH))
        for w in range(n):
            for j, (px, py) in enumerate(chips):
                first.append(pltpu.make_async_remote_copy(
                    src_ref=ins[w].at[c], dst_ref=outs[w].at[me, c], send_sem=send.at[3 * w + j],
                    recv_sem=recv.at[3 * w + j], device_id=(px, py, c), device_id_type=MESH))
        for cp in first:
            cp.start()
        for w in range(n):
            for j, (px, py) in enumerate(chips):
                landed = outs[w].at[2 * px + py, c]
                pltpu.make_async_remote_copy(src_ref=landed, dst_ref=landed, send_sem=send.at[3 * w + j],
                                             recv_sem=recv.at[3 * w + j], device_id=(px, py, c),
                                             device_id_type=MESH).wait_recv()
                fw = pltpu.make_async_remote_copy(src_ref=landed, dst_ref=landed, send_sem=fsend.at[3 * w + j],
                                                  recv_sem=frecv.at[3 * w + j], device_id=(x, y, 1 - c),
                                                  device_id_type=MESH)
                fw.start()
                passed.append(fw)
        for w in range(n):
            for j, (px, py) in enumerate(chips):
                other = outs[w].at[2 * px + py, 1 - c]
                pltpu.make_async_remote_copy(src_ref=other, dst_ref=other, send_sem=fsend.at[3 * w + j],
                                             recv_sem=frecv.at[3 * w + j], device_id=(x, y, 1 - c),
                                             device_id_type=MESH).wait_recv()
        for w in range(n):
            own = outs[w].at[me]
            pltpu.make_async_remote_copy(src_ref=own, dst_ref=own, send_sem=osend.at[w], recv_sem=orecv.at[w],
                                         device_id=(x, y, 1 - c), device_id_type=MESH).wait_recv()
        for cp in first + passed:
            cp.wait_send()

    dma = lambda k: pltpu.SemaphoreType.DMA((k,))
    return pl.pallas_call(
        body, name="gather_weights",
        out_shape=tuple(jax.ShapeDtypeStruct((N_CHIPS,) + s.shape, s.dtype) for s in shards),
        in_specs=[_ANY] * n, out_specs=tuple([_ANY] * n),
        scratch_shapes=[dma(3 * n), dma(3 * n), dma(3 * n), dma(3 * n), dma(n), dma(n)],
    )(*shards)


_HBM = pl.BlockSpec(memory_space=pltpu.HBM)
_SEM = pl.BlockSpec(memory_space=pltpu.SEMAPHORE)
_EFFECT = pltpu.SideEffectType.DATAFLOW_SIDE_EFFECTING


SEMS_PER_ARRAY = 8


def _exchange_copies(srcs, lands, send, recv, mode):
    x, y, c, chips = _place()
    if mode == "all":
        flips = [(fx, fy, fc) for fx in (0, 1) for fy in (0, 1) for fc in (0, 1)][1:]
        peers = [(x ^ fx, y ^ fy, c ^ fc) for fx, fy, fc in flips]
        slot = 4 * x + 2 * y + c
    else:
        peers = [(px, py, c) for px, py in chips] + ([(x, y, 1 - c)] if mode == "gather" else [])
        slot = 2 * x + y
    cps = []
    for w, (src, land) in enumerate(zip(srcs, lands)):
        for k, peer in enumerate(peers):
            piece = src.at[2 * peer[0] + peer[1]] if mode == "scatter" else src
            cps.append(pltpu.make_async_remote_copy(
                src_ref=piece, dst_ref=land.at[slot], send_sem=send.at[SEMS_PER_ARRAY * w + k],
                recv_sem=recv.at[SEMS_PER_ARRAY * w + k], device_id=peer, device_id_type=MESH))
    return cps


def _exchange_start(srcs, name, mode, after):
    n = len(srcs)
    lead = {"gather": (N_CHIPS,), "scatter": (), "all": (2 * N_CHIPS,)}[mode]
    land_shapes = [lead + s.shape for s in srcs]

    def body(*refs):
        src_refs, land_refs = refs[:n], refs[n:2 * n]
        send, recv = refs[2 * n + 1], refs[2 * n + 2]
        token = refs[-1]
        for cp in _exchange_copies(src_refs, land_refs, send, recv, mode):
            cp.start()
        token[...] = jnp.zeros_like(token)

    sems = pltpu.SemaphoreType.DMA((SEMS_PER_ARRAY * n,))
    out = pl.pallas_call(
        body, name=name,
        out_shape=(sems, sems, *[pltpu.HBM(s.shape, s.dtype) for s in srcs],
                   *[pltpu.HBM(shp, s.dtype) for shp, s in zip(land_shapes, srcs)], jax.ShapeDtypeStruct((8, 128), F32)),
        in_specs=[_HBM] * (2 * n) + [_ANY],
        out_specs=(_SEM, _SEM, *[_HBM] * (2 * n), pl.BlockSpec(memory_space=pltpu.VMEM)),
        input_output_aliases={i: 2 + i for i in range(2 * n)},
        compiler_params=pltpu.CompilerParams(has_side_effects=_EFFECT),
    )(*[pltpu.with_memory_space_constraint(s, pltpu.HBM) for s in srcs],
      *[pltpu.with_memory_space_constraint(lax.empty(shp, s.dtype), pltpu.HBM) for shp, s in zip(land_shapes, srcs)],
      after)
    return out[0], out[1], out[2:2 + n], out[2 + n:2 + 2 * n], out[-1]


def _exchange_wait(started, name, mode, after):
    send, recv, src_thru, land_thru, _ = started
    n = len(src_thru)

    def body(*refs):
        src_refs, land_refs, send_ref, recv_ref = refs[:n], refs[n:2 * n], refs[2 * n], refs[2 * n + 1]
        for cp in _exchange_copies(src_refs, land_refs, send_ref, recv_ref, mode):
            cp.wait_send()
            cp.wait_recv()

    out = pl.pallas_call(
        body, name=name,
        out_shape=tuple(pltpu.HBM(a.shape, a.dtype) for a in list(src_thru) + list(land_thru)),
        in_specs=[_HBM] * (2 * n) + [_SEM, _SEM, _ANY], out_specs=tuple([_HBM] * (2 * n)),
        input_output_aliases={i: i for i in range(2 * n)},
        compiler_params=pltpu.CompilerParams(has_side_effects=_EFFECT),
    )(*src_thru, *land_thru, send, recv, after)
    return out[:n], out[n:]


def _swap_halves(gs, name):
    n = len(gs)

    def body(*refs):
        ins, outs, send, recv = refs[:n], refs[n:2 * n], refs[2 * n], refs[2 * n + 1]
        x, y, c, _ = _place()
        cps = []
        for w in range(n):
            cps.append(pltpu.make_async_remote_copy(
                src_ref=ins[w].at[:, 1 - c], dst_ref=outs[w], send_sem=send.at[w], recv_sem=recv.at[w],
                device_id=(x, y, 1 - c), device_id_type=MESH))
        for cp in cps:
            cp.start()
        for cp in cps:
            cp.wait()

    return pl.pallas_call(
        body, name=name,
        out_shape=tuple(jax.ShapeDtypeStruct((g.shape[0],) + g.shape[2:], g.dtype) for g in gs),
        in_specs=[_ANY] * n, out_specs=tuple([_ANY] * n),
        scratch_shapes=[pltpu.SemaphoreType.DMA((n,)), pltpu.SemaphoreType.DMA((n,))],
    )(*gs)


GRAD_PAYLOAD = jnp.bfloat16


def _half_blocks(half_rows, cols):
    if (half_rows // 2) % 16 == 0:
        return (half_rows // 2, cols), (lambda r: (r, 0))
    assert cols % 256 == 0, (half_rows, cols)
    return (half_rows, cols // 2), (lambda r: (0, r))


def _pair_sum(gs, gots, name):
    n = len(gs)
    core = lax.axis_index("c").astype(jnp.int32).reshape(1)

    def body(core_ref, *refs):
        del core_ref
        for w in range(n):
            refs[2 * n + w][...] = (refs[w][...] + refs[n + w][...]).astype(GRAD_PAYLOAD)

    in_specs, out_specs, out_shape, nbytes = [], [], [], 0
    cuts = [_half_blocks(g.shape[1] // 2, g.shape[2]) for g in gs]
    for g, ((br, bc), at) in zip(gs, cuts):
        per_half = (g.shape[1] // 2) // br
        in_specs.append(pl.BlockSpec((1, br, bc), lambda s, r, core, at=at, per_half=per_half:
                                     (s, per_half * core[0] + at(r)[0], at(r)[1])))
        nbytes += 3 * _nbytes((br, bc), F32)
    for g, ((br, bc), at) in zip(gs, cuts):
        in_specs.append(pl.BlockSpec((1, br, bc), lambda s, r, core, at=at: (s,) + at(r)))
        out_specs.append(pl.BlockSpec((1, br, bc), lambda s, r, core, at=at: (s,) + at(r)))
        out_shape.append(jax.ShapeDtypeStruct((g.shape[0], g.shape[1] // 2, g.shape[2]), GRAD_PAYLOAD))
    return pl.pallas_call(
        body, name=name, out_shape=tuple(out_shape),
        grid_spec=pltpu.PrefetchScalarGridSpec(num_scalar_prefetch=1, grid=(N_CHIPS, 2), in_specs=in_specs,
                                               out_specs=tuple(out_specs)),
        compiler_params=_params(("parallel", "parallel"), nbytes),
    )(core, *gs, *gots)


def _chip_sum(ps, landed):
    n = len(ps)
    x, y, c = lax.axis_index("x"), lax.axis_index("y"), lax.axis_index("c")
    where = jnp.stack([2 * x + y, 2 * (1 - x) + y, 2 * x + (1 - y), 2 * (1 - x) + (1 - y), c]).astype(jnp.int32)

    def body(where_ref, *refs):
        del where_ref
        for w in range(n):
            terms = [refs[4 * w + t][...].astype(F32) for t in range(4)]
            refs[4 * n + w][...] = ((terms[0] + terms[1]) + terms[2]) + terms[3]

    in_specs, out_specs, out_shape, args, nbytes = [], [], [], [], 0
    for p, a in zip(ps, landed):
        (br, bc), at = _half_blocks(a.shape[1], a.shape[2])
        blk = (1, br, bc)
        in_specs.append(pl.BlockSpec(blk, lambda r, where, at=at: (where[0],) + at(r)))
        args.append(p)
        for t in (1, 2, 3):
            in_specs.append(pl.BlockSpec(blk, lambda r, where, t=t, at=at: (where[t],) + at(r)))
            args.append(a)
        out_specs.append(pl.BlockSpec(blk, lambda r, where, at=at: (where[4],) + at(r)))
        out_shape.append(jax.ShapeDtypeStruct((2,) + a.shape[1:], F32))
        nbytes += 4 * _nbytes(blk, F32)
    return pl.pallas_call(
        body, name="grad_chip_sum", out_shape=tuple(out_shape),
        grid_spec=pltpu.PrefetchScalarGridSpec(num_scalar_prefetch=1, grid=(2,), in_specs=in_specs,
                                               out_specs=tuple(out_specs)),
        compiler_params=_params(("parallel",), nbytes),
    )(where, *args)


def _join_halves(ss):
    n = len(ss)

    def body(*refs):
        outs, send, recv = refs[n:2 * n], refs[2 * n], refs[2 * n + 1]
        x, y, c, _ = _place()
        cps = []
        for w in range(n):
            cps.append(pltpu.make_async_remote_copy(
                src_ref=outs[w].at[c], dst_ref=outs[w].at[c], send_sem=send.at[w], recv_sem=recv.at[w],
                device_id=(x, y, 1 - c), device_id_type=MESH))
        for cp in cps:
            cp.start()
        for w in range(n):
            got = outs[w].at[1 - c]
            pltpu.make_async_remote_copy(src_ref=got, dst_ref=got, send_sem=send.at[w], recv_sem=recv.at[w],
                                         device_id=(x, y, 1 - c), device_id_type=MESH).wait_recv()
        for cp in cps:
            cp.wait_send()

    dma = lambda k: pltpu.SemaphoreType.DMA((k,))
    return pl.pallas_call(
        body, name="grad_join_halves",
        out_shape=tuple(jax.ShapeDtypeStruct(s.shape, s.dtype) for s in ss),
        in_specs=[_ANY] * n, out_specs=tuple([_ANY] * n), input_output_aliases={w: w for w in range(n)},
        scratch_shapes=[dma(n), dma(n)],
    )(*ss)


def _rot_cols(w, axis=-1):
    a, b = jnp.split(w, 2, axis=axis)
    return jnp.concatenate([-b, a], axis=axis)


def _rot_cols_t(g, axis=-1):
    a, b = jnp.split(g, 2, axis=axis)
    return jnp.concatenate([b, -a], axis=axis)


def _cols_from_chips(a):
    n, r, cs = a.shape
    return jnp.transpose(a, (1, 0, 2)).reshape(r, n * cs)


def _cols_to_chips(a):
    r, cc = a.shape
    return jnp.transpose(a.reshape(r, N_CHIPS, cc // N_CHIPS), (1, 0, 2))


def _conv_w_split(cw):
    return jnp.swapaxes(cw.reshape(3, 2, D_FF), 0, 1)


def _conv_w_join(g):
    return jnp.swapaxes(g, 0, 1).reshape(3, 2 * D_FF)


_SEG =(D_MODEL, 2 * D_MODEL, 2 * D_MODEL + Q_RANK, 2 * D_MODEL + Q_RANK + KV_RANK, 2 * D_MODEL + Q_RANK + KV_RANK + ROPE,
        3 * D_MODEL + Q_RANK + KV_RANK + ROPE)


def _w_in_t_to_pad(wt):
    u, v, cq, ckv, kr, ga, gb = jnp.split(wt, _SEG, axis=0)
    return jnp.concatenate([u, v, ga, gb, cq, ckv, kr, _rot_cols(kr, axis=0)], axis=0)


def _w_in_t_from_pad(gt):
    u, v, ga, gb, cq, ckv, kr, krr = jnp.split(
        gt, (D_MODEL, 2 * D_MODEL, 3 * D_MODEL, 4 * D_MODEL, 4 * D_MODEL + Q_RANK, 4 * D_MODEL + Q_RANK + KV_RANK,
             4 * D_MODEL + Q_RANK + KV_RANK + ROPE), axis=0)
    return jnp.concatenate([u, v, cq, ckv, kr + _rot_cols_t(krr, axis=0), ga, gb], axis=0)


def _w_uq_to_pad(w):
    t = w.reshape(Q_RANK, HEADS, QK_DIM)
    nope, rope = t[..., :NOPE], t[..., NOPE:]
    return jnp.concatenate([nope, rope, _rot_cols(rope)], axis=-1).reshape(Q_RANK, HEADS * HEAD_PAD)


def _w_uq_from_pad(g):
    t = g.reshape(Q_RANK, HEADS, HEAD_PAD)
    nope, rope, rot = t[..., :NOPE], t[..., NOPE:QK_DIM], t[..., QK_DIM:]
    return jnp.concatenate([nope, rope + _rot_cols_t(rot)], axis=-1).reshape(Q_RANK, HEADS * QK_DIM)


def _w_ukv_to_pad(w):
    t = w.reshape(KV_RANK, HEADS, 2, NOPE)
    return jnp.swapaxes(t, 1, 2).reshape(KV_RANK, 2 * HEADS * NOPE)


def _w_ukv_from_pad(g):
    t = g.reshape(KV_RANK, 2, HEADS, NOPE)
    return jnp.swapaxes(t, 1, 2).reshape(KV_RANK, 2 * HEADS * NOPE)


def _rope_tables(positions):
    inv_freq = 1.0 / (ROPE_THETA ** (jnp.arange(0, ROPE, 2, dtype=F32) / ROPE))
    ang = positions.astype(F32).reshape(-1, 1) * inv_freq
    cos, sin = jnp.cos(ang), jnp.sin(ang)
    zero = jnp.zeros((ang.shape[0], 64), F32)
    return jnp.concatenate([cos, cos, zero], axis=1), jnp.concatenate([sin, sin, zero], axis=1)


_BIG = ("w_in", "w_uq", "w_ukv", "w_out", "w_up", "w_down")
UP_SHARD = 2 * D_FF // N_CHIPS


def _local_step(x, positions, tgt, wts, mixer_weights, ffn_weights, on_ffn_grads, on_mixer_grads):
    B, S, D = x.shape
    T = B * S
    xf = x.reshape(T, D)
    cos_a, sin_a = _rope_tables(positions)
    bs_t = jnp.pad(wts["a_spatial_b"].T, ((0, 0), (0, 128 - A_GROUPS)))

    h = _rms_fwd(xf, wts["mix_norm"], "norm1_fwd")
    z = _mm(h, wts["w_in"], "nt", "in_proj", tm=512, tn=1536, tk=D, n_outer=True)
    wts = dict(wts)
    wts["w_q"], wts["w_kv"], wts["w_out"] = mixer_weights(z)
    q, k, v, cqn, ckvn = _lat_fwd(z, wts["q_a_norm"], wts["kv_a_norm"], wts["w_q"], wts["w_kv"], cos_a, sin_a)
    yb, *lses = _attn_fwd(q, k, v, B, S)
    merged = _mix_fwd(z, yb, wts["a_v_norm_g"], wts["a_v_norm_b"], wts["a_spatial_w"], bs_t)
    x1 = _mm(merged, wts["w_out"], "nn", "out_proj", tm=512, tn=D, tk=D, add=xf)
    h2 = _rms_fwd(x1, wts["ffn_norm"], "norm2_fwd")
    wts["w_up"], wts["w_down"], wts["conv_w"] = ffn_weights(h2)
    up_pre = _mm(h2, wts["w_up"], "nn", "up_proj", tm=512, tn=UP_SHARD, tk=D, dims=(T, 2 * D_FF, D),
                 b_spec=pl.BlockSpec((None, D, UP_SHARD), lambda i, j, k: (j, 0, 0)),
                 o_spec=pl.BlockSpec((None, 512, UP_SHARD), lambda i, j, k: (j // 2, i, j % 2)), out_shape=(2, T, D_FF),
                 n_outer=True)
    act = _gate_fwd(up_pre, wts["conv_w"], wts["conv_b"], B, S)
    x2 = _mm(act, wts["w_down"], "nn", "down_proj", tm=512, tn=D, tk=1408, add=x1)
    dx2, loss_row, g_final = _final(x2, tgt.reshape(T, D), wts["final_norm"])

    g = {"final_norm": g_final}
    dact = _mm(dx2, wts["w_down"], "nt", "down_proj_dx", tm=512, tn=1408, tk=D, n_outer=True)
    tk2, tk1 = min(2048, T), min(1024, T)
    g["w_down"], g["w_down_lo"] = _mm(act, dx2, "tn", "down_proj_dw", tm=1408, tn=D, tk=tk1, copy_dtype=GRAD_PAYLOAD)
    dup, g["conv_w"], g["conv_b"] = _gate_bwd(up_pre, dact, wts["conv_w"], wts["conv_b"], B, S)
    g["w_up"], g["w_up_lo"] = _mm(
        h2, dup, "tn", "up_proj_dw", tm=D, tn=UP_SHARD, tk=tk2, dims=(D, 2 * D_FF, T), copy_dtype=GRAD_PAYLOAD,
        b_spec=pl.BlockSpec((None, tk2, UP_SHARD), lambda i, j, k: (j // 2, k, j % 2)),
        o_spec=pl.BlockSpec((None, D, UP_SHARD), lambda i, j, k: (j, 0, 0)), out_shape=(N_CHIPS, D, UP_SHARD))
    token = on_ffn_grads(g)
    dh2 = _mm(dup, wts["w_up"], "nt", "up_proj_dx", tm=512, tn=D, tk=UP_SHARD, dims=(T, D, 2 * D_FF), after=token,
              a_spec=pl.BlockSpec((None, 512, UP_SHARD), lambda i, j, k: (k // 2, i, k % 2)),
              b_spec=pl.BlockSpec((None, D, UP_SHARD), lambda i, j, k: (k, 0, 0)))
    dx1, g["ffn_norm"] = _rms_bwd(x1, wts["ffn_norm"], dh2, dx2, "norm2_bwd")
    dm = _mm(dx1, wts["w_out"], "nt", "out_proj_dx", tm=512, tn=D, tk=D)
    g["w_out"], g["w_out_lo"] = _mm(merged, dx1, "tn", "out_proj_dw", tm=D, tn=D, tk=tk1, copy_dtype=GRAD_PAYLOAD)
    dz, dyb, dl, g["a_spatial_w"], gbs, g["a_v_norm_g"], g["a_v_norm_b"] = _mix_bwd(
        z, yb, dm, wts["a_v_norm_g"], wts["a_v_norm_b"], wts["a_spatial_w"], bs_t)
    g["a_spatial_b"] = gbs[:, :A_GROUPS].T
    delta = dl.reshape(HEADS * T // ATT_BLOCK, 1, ATT_BLOCK)
    dq, dk, dv = _attn_bwd(q, k, v, dyb, lses, delta, B, S)
    dz, dq_raw, dkv, g["q_a_norm"], g["kv_a_norm"] = _lat_bwd(
        dz, z, dq, dk, dv, wts["q_a_norm"], wts["kv_a_norm"], wts["w_q"], wts["w_kv"], cos_a, sin_a)
    g["w_q"], g["w_q_lo"] = _mm(cqn, dq_raw, "tn", "q_proj_dw", tm=Q_RANK, tn=HEADS * HEAD_PAD, tk=tk2,
                                copy_dtype=GRAD_PAYLOAD)
    g["w_kv"], g["w_kv_lo"] = _mm(ckvn, dkv, "tn", "kv_proj_dw", tm=KV_RANK, tn=2 * HEADS * NOPE, tk=tk2,
                                  copy_dtype=GRAD_PAYLOAD)
    g["w_in"] = _mm(dz, h, "tn", "in_proj_dw", tm=1536, tn=D, tk=tk2)
    token = on_mixer_grads(g)
    dh = _mm(dz, wts["w_in"], "nn", "in_proj_dx", tm=512, tn=D, tk=1536, after=token)
    dx, g["mix_norm"] = _rms_bwd(xf, wts["mix_norm"], dh, dx1, "norm1_bwd")
    return loss_row[0, 0], dx.reshape(B, S, D), g


_SMALL = (("mix_norm", (1, D_MODEL)), ("a_v_norm_g", (1, D_MODEL)), ("a_v_norm_b", (1, D_MODEL)),
          ("a_spatial_w", (A_GROUPS * CHUNK, CHUNK)), ("a_spatial_b", (1, A_GROUPS * CHUNK)), ("q_a_norm", (1, Q_RANK)),
          ("kv_a_norm", (1, KV_RANK)), ("ffn_norm", (1, D_MODEL)), ("conv_b", (1, 2 * D_FF)), ("final_norm", (1, D_MODEL)),
          ("conv_w", (3, 2 * D_FF)))
_SMALL_SIZE = sum(math.prod(s) for _, s in _SMALL)
_SMALL_ROWS = -(-(_SMALL_SIZE + 1) // (128 * 8)) * 8


def kernel(x, positions, mix_norm, w_in, a_v_norm_g, a_v_norm_b, a_spatial_w, a_spatial_b, q_a_norm, w_uq, kv_a_norm, w_ukv, w_out, ffn_norm, w_up, conv_w, conv_b, w_down, final_norm, loss_target, m_mix_norm, m_w_in, m_a_v_norm_g, m_a_v_norm_b, m_a_spatial_w, m_a_spatial_b, m_q_a_norm, m_w_uq, m_kv_a_norm, m_w_ukv, m_w_out, m_ffn_norm, m_w_up, m_conv_w, m_conv_b, m_w_down, m_final_norm, v_mix_norm, v_w_in, v_a_v_norm_g, v_a_v_norm_b, v_a_spatial_w, v_a_spatial_b, v_q_a_norm, v_w_uq, v_kv_a_norm, v_w_ukv, v_w_out, v_ffn_norm, v_w_up, v_conv_w, v_conv_b, v_w_down, v_final_norm):
    weights = dict(mix_norm=mix_norm, w_in=w_in, a_v_norm_g=a_v_norm_g, a_v_norm_b=a_v_norm_b, a_spatial_w=a_spatial_w,
                   a_spatial_b=a_spatial_b, q_a_norm=q_a_norm, w_uq=w_uq, kv_a_norm=kv_a_norm, w_ukv=w_ukv, w_out=w_out,
                   ffn_norm=ffn_norm, w_up=w_up, conv_w=conv_w, conv_b=conv_b, w_down=w_down, final_norm=final_norm)
    m_in = dict(mix_norm=m_mix_norm, w_in=m_w_in, a_v_norm_g=m_a_v_norm_g, a_v_norm_b=m_a_v_norm_b,
                a_spatial_w=m_a_spatial_w, a_spatial_b=m_a_spatial_b, q_a_norm=m_q_a_norm, w_uq=m_w_uq,
                kv_a_norm=m_kv_a_norm, w_ukv=m_w_ukv, w_out=m_w_out, ffn_norm=m_ffn_norm, w_up=m_w_up, conv_w=m_conv_w,
                conv_b=m_conv_b, w_down=m_w_down, final_norm=m_final_norm)
    v_in = dict(mix_norm=v_mix_norm, w_in=v_w_in, a_v_norm_g=v_a_v_norm_g, a_v_norm_b=v_a_v_norm_b,
                a_spatial_w=v_a_spatial_w, a_spatial_b=v_a_spatial_b, q_a_norm=v_q_a_norm, w_uq=v_w_uq,
                kv_a_norm=v_kv_a_norm, w_ukv=v_w_ukv, w_out=v_w_out, ffn_norm=v_ffn_norm, w_up=v_w_up, conv_w=v_conv_w,
                conv_b=v_conv_b, w_down=v_w_down, final_norm=v_final_norm)
    names = list(weights)
    chip = 2 * lax.axis_index("x") + lax.axis_index("y")

    def halves(a):
        return a.reshape(a.shape[:-2] + (2, a.shape[-2] // 2, a.shape[-1]))

    w_in_t = jnp.swapaxes(w_in[0], 0, 1).astype(MXU_DTYPE)
    (w_in_sh,) = _gather_weights([jnp.stack(jnp.split(w_in_t, 2, axis=1))])
    mixer_gather = _exchange_start([weights[n][0].astype(MXU_DTYPE) for n in _BIG[1:4]],
                                   "mixer_gather_start", "gather", after=w_in_sh)
    ffn_gather = _exchange_start([w_up[0].astype(MXU_DTYPE), w_down[0].astype(MXU_DTYPE), conv_w[0]],
                                 "ffn_gather_start", "gather", after=mixer_gather[4])
    wts = dict(
        mix_norm=mix_norm + ffn_gather[4][0:1, 0:1], a_v_norm_g=a_v_norm_g, a_v_norm_b=a_v_norm_b,
        a_spatial_w=a_spatial_w[0], a_spatial_b=a_spatial_b[0], q_a_norm=q_a_norm, kv_a_norm=kv_a_norm,
        ffn_norm=ffn_norm, final_norm=final_norm.reshape(1, D_MODEL),
        w_in=_w_in_t_to_pad(jnp.concatenate([w_in_sh[:, 0], w_in_sh[:, 1]], axis=-1).reshape(-1, D_MODEL)),
        conv_b=conv_b.reshape(2, 1, D_FF))

    def mixer_weights(after):
        _, (w_uq_sh, w_ukv_sh, w_out_sh) = _exchange_wait(mixer_gather, "mixer_gather_wait", "gather", after)
        return (_w_uq_to_pad(_cols_from_chips(w_uq_sh)), _w_ukv_to_pad(_cols_from_chips(w_ukv_sh)),
                w_out_sh.reshape(D_MODEL, D_MODEL))

    def ffn_weights(after):
        _, (w_up_sh, w_down_sh, cw_all) = _exchange_wait(ffn_gather, "ffn_gather_wait", "gather", after)
        return w_up_sh, w_down_sh.reshape(D_FF, D_MODEL), _conv_w_split(_cols_from_chips(cw_all))

    scatters = {}

    def start_scatter(slabs, slabs_lo, tag):
        got = _swap_halves([halves(s) for s in slabs_lo], tag + "_grad_swap_halves")
        sums = _pair_sum(slabs, got, tag + "_grad_pair_sum")
        scatters[tag] = _exchange_start(list(sums), tag + "_scatter_start", "scatter", after=slabs[-1])
        return scatters[tag][4]

    def on_ffn_grads(g):
        return start_scatter(*[[g["w_up" + lo], g["w_down" + lo].reshape(N_CHIPS, D_FF // N_CHIPS, D_MODEL)]
                               for lo in ("", "_lo")], "ffn")

    def on_mixer_grads(g):
        w_in_slabs = _w_in_t_from_pad(g["w_in"]).reshape(N_CHIPS, -1, D_MODEL)
        rest = [[_cols_to_chips(_w_uq_from_pad(g["w_q" + lo])), _cols_to_chips(_w_ukv_from_pad(g["w_kv" + lo])),
                 g["w_out" + lo].reshape(N_CHIPS, D_MODEL // N_CHIPS, D_MODEL)] for lo in ("", "_lo")]
        return start_scatter([w_in_slabs] + rest[0], [w_in_slabs.astype(GRAD_PAYLOAD)] + rest[1], "mixer")

    loss_part, grad_x, g = _local_step(x, positions, loss_target, wts, mixer_weights, ffn_weights, on_ffn_grads,
                                       on_mixer_grads)

    g_small_parts = dict(g)
    g_small_parts["conv_w"] = _conv_w_join(g["conv_w"])
    g_small_parts["conv_b"] = g["conv_b"].reshape(1, 2 * D_FF)
    flat = jnp.concatenate([g_small_parts[n].reshape(-1) for n, _ in _SMALL] + [loss_part.reshape(1)])
    flat = jnp.pad(flat, (0, _SMALL_ROWS * 128 - flat.shape[0])).reshape(_SMALL_ROWS, 128)
    small_gather = _exchange_start([flat], "small_gather_start", "all", after=grad_x)

    mixer_sums, mixer_landed = _exchange_wait(scatters["mixer"], "mixer_scatter_wait", "scatter", after=small_gather[4])
    ffn_sums, ffn_landed = _exchange_wait(scatters["ffn"], "ffn_scatter_wait", "scatter", after=mixer_landed[0])
    reduced = _chip_sum(list(mixer_sums) + list(ffn_sums), list(mixer_landed) + list(ffn_landed))
    g_big = dict(zip(_BIG, _join_halves(reduced)))

    grads, deltas, new_m, new_v = {}, {}, {}, {}

    def update(n, grad):
        w = weights[n]
        shape2 = grad.shape
        d, nm, nv = _adamw(w.reshape(shape2), grad, m_in[n].reshape(shape2), v_in[n].reshape(shape2), "adamw_" + n)
        grads[n], deltas[n], new_m[n], new_v[n] = (t.reshape(w.shape) for t in (grad, d, nm, nv))

    def update_transposed(n, grad_t):
        t = lambda a: jnp.swapaxes(a, 1, 2)
        d, nm, nv = _adamw(t(weights[n]), grad_t, t(m_in[n]), t(v_in[n]), "adamw_" + n)
        grads[n], deltas[n], new_m[n], new_v[n] = t(grad_t), t(d), t(nm), t(nv)

    for n in _BIG:
        g3 = g_big[n].reshape((1, -1, g_big[n].shape[-1]))
        if n == "w_in":
            update_transposed(n, g3)
        else:
            update(n, g3)

    (own,), (everyone,) = _exchange_wait(small_gather, "small_gather_wait", "all", after=deltas["w_up"])
    device = 2 * chip + lax.axis_index("c")
    everyone = lax.dynamic_update_slice(everyone, own[None], (device, 0, 0))
    total = _sum_slabs([everyone[j] for j in range(8)], "small_grads_sum", tr=_SMALL_ROWS).reshape(-1)
    o = 0
    for n, shp in _SMALL:
        piece = total[o:o + math.prod(shp)].reshape(shp)
        o += math.prod(shp)
        if n == "conv_w":
            piece = lax.dynamic_slice_in_dim(piece, chip * UP_SHARD, UP_SHARD, axis=1)
        update(n, piece)
    loss = total[_SMALL_SIZE]
    return (loss, grad_x, *[grads[n] for n in names], *[deltas[n] for n in names], *[new_m[n] for n in names],
            *[new_v[n] for n in names])
```

```python
import functools
import math

import jax
import jax.numpy as jnp
from jax import lax
from jax.experimental import pallas as pl
from jax.experimental.pallas import tpu as pltpu

F32 = jnp.float32
MXU_DTYPE = jnp.bfloat16
MESH = pl.DeviceIdType.MESH

D_MODEL = 1024
EPS = 1e-6
A_GROUPS = 8
CHUNK = 128
HEADS = 8
NOPE = 128
ROPE = 64
QK_DIM = NOPE + ROPE
HEAD_PAD = 256
Q_RANK = 256
KV_RANK = 128
ROPE_THETA = 10000.0
D_FF = 2816
FF_TILE = 256
N_FF_TILES = D_FF // FF_TILE
LAT = 512
IN_PAD = 4 * D_MODEL + LAT
N_CHIPS = 4
ADAM_LR, ADAM_B1, ADAM_B2, ADAM_EPS, ADAM_WD, ADAM_STEP = 0.001, 0.9, 0.999, 1e-08, 0.01, 10

VMEM_CAP_V7X = 64 * 1024 * 1024
NEG = -1e30


def _params(sem, nbytes):
    limit = int(min(VMEM_CAP_V7X - (8 << 20), max(32 << 20, 3 * nbytes)))
    return pltpu.CompilerParams(dimension_semantics=sem, vmem_limit_bytes=limit)


def _nbytes(shape, dtype):
    return math.prod(shape) * jnp.dtype(dtype).itemsize


_DIMS = {"nn": (((1,), (0,)), ((), ())), "nt": (((1,), (1,)), ((), ())), "tn": (((0,), (0,)), ((), ()))}


def _mm(a, b, mode, name, *, tm, tn, tk, out_dtype=F32, add=None, dims=None, a_spec=None, b_spec=None,
        o_spec=None, out_shape=None, n_outer=False, copy_dtype=None, after=None):
    if dims is None:
        if mode == "nn":
            (M, K), (_, N) = a.shape, b.shape
        elif mode == "nt":
            (M, K), (N, _) = a.shape, b.shape
        else:
            (K, M), (_, N) = a.shape, b.shape
    else:
        M, N, K = dims
    a_blk = (tk, tm) if mode == "tn" else (tm, tk)
    b_blk = (tn, tk) if mode == "nt" else (tk, tn)
    if a_spec is None:
        a_spec = pl.BlockSpec(a_blk, (lambda i, j, k: (k, i)) if mode == "tn" else (lambda i, j, k: (i, k)))
    if b_spec is None:
        b_spec = pl.BlockSpec(b_blk, (lambda i, j, k: (j, k)) if mode == "nt" else (lambda i, j, k: (k, j)))
    if o_spec is None:
        o_spec = pl.BlockSpec((tm, tn), lambda i, j, k: (i, j))
    if out_shape is None:
        out_shape = (M, N)
    assert M % tm == 0 and N % tn == 0 and K % tk == 0, (name, M, N, K, tm, tn, tk)
    nk = K // tk
    contract = _DIMS[mode]
    has_add = add is not None

    def body(*refs):
        a_ref, b_ref = refs[0], refs[1]
        add_ref = refs[2] if has_add else None
        n_in = 2 + has_add + (after is not None)
        o_ref = refs[n_in]
        copy_ref = refs[n_in + 1] if copy_dtype is not None else None

        def product():
            return lax.dot_general(a_ref[...].astype(MXU_DTYPE), b_ref[...].astype(MXU_DTYPE), contract,
                                   preferred_element_type=F32)

        def finish(r):
            if has_add:
                r = r + add_ref[...]
            o_ref[...] = r.astype(out_dtype)
            if copy_ref is not None:
                copy_ref[...] = r.astype(copy_dtype)

        if nk == 1:
            finish(product())
            return
        acc = refs[-1]
        k = pl.program_id(2)

        @pl.when(k == 0)
        def _():
            acc[...] = jnp.zeros_like(acc)

        acc[...] += product()

        @pl.when(k == nk - 1)
        def _():
            finish(acc[...])

    in_specs = [a_spec, b_spec]
    args = [a, b]
    nbytes = _nbytes(a_blk, a.dtype) + _nbytes(b_blk, b.dtype) + 3 * _nbytes((tm, tn), F32)
    if has_add:
        in_specs.append(pl.BlockSpec((tm, tn), lambda i, j, k: (i, j)))
        args.append(add)
        nbytes += _nbytes((tm, tn), F32)
    if after is not None:
        in_specs.append(pl.BlockSpec(after.shape, lambda i, j, k: (0, 0)))
        args.append(after)
    grid = (M // tm, N // tn, nk)
    if n_outer:
        def swapped(spec):
            return pl.BlockSpec(spec.block_shape, lambda j, i, k, at=spec.index_map: at(i, j, k))

        grid = (N // tn, M // tm, nk)
        in_specs = [swapped(s) for s in in_specs]
        o_spec = swapped(o_spec)
    out_sds, out_specs = jax.ShapeDtypeStruct(out_shape, out_dtype), o_spec
    if copy_dtype is not None:
        out_sds, out_specs = (out_sds, jax.ShapeDtypeStruct(out_shape, copy_dtype)), (o_spec, o_spec)
    return pl.pallas_call(
        body, name=name, out_shape=out_sds, grid=grid, in_specs=in_specs, out_specs=out_specs,
        scratch_shapes=[pltpu.VMEM((tm, tn), F32)] if nk > 1 else [],
        compiler_params=_params(("parallel", "parallel", "arbitrary"), nbytes),
    )(*args)


_GELU_C = math.sqrt(2.0 / math.pi)
_GELU_A = 0.044715


def _sigmoid(x):
    return 0.5 * jnp.tanh(0.5 * x) + 0.5


def _gelu(x):
    t = jnp.tanh(x * (_GELU_C + (_GELU_C * _GELU_A) * (x * x)))
    return x * (0.5 + 0.5 * t)


def _gelu_and_grad(x):
    x2 = x * x
    t = jnp.tanh(x * (_GELU_C + (_GELU_C * _GELU_A) * x2))
    cdf = 0.5 + 0.5 * t
    grad = cdf + (0.5 * x) * (1.0 - t * t) * (_GELU_C + (3.0 * _GELU_C * _GELU_A) * x2)
    return x * cdf, grad


def _rope_mix(g, cos_a, sin_a):
    return g * cos_a + pltpu.roll(g, 64, 1) * sin_a


def _rope_mix_bwd(d, cos_a, sin_a):
    return d * cos_a + pltpu.roll(d * sin_a, 64, 1)


def _rms_fwd(x, g, name, tr=512):
    T, D = x.shape

    def body(x_ref, g_ref, h_ref):
        xv = x_ref[...]
        r = lax.rsqrt(jnp.mean(xv * xv, axis=-1, keepdims=True) + EPS)
        h_ref[...] = ((xv * r) * g_ref[...]).astype(h_ref.dtype)

    return pl.pallas_call(
        body, name=name, out_shape=jax.ShapeDtypeStruct((T, D), MXU_DTYPE), grid=(T // tr,),
        in_specs=[pl.BlockSpec((tr, D), lambda i: (i, 0)), pl.BlockSpec((1, D), lambda i: (0, 0))],
        out_specs=pl.BlockSpec((tr, D), lambda i: (i, 0)),
        compiler_params=_params(("parallel",), 3 * _nbytes((tr, D), F32)),
    )(x, g)


def _rms_bwd(x, g, dh, dres, name, tr=512):
    T, D = x.shape

    def body(x_ref, g_ref, dh_ref, dres_ref, dx_ref, gg_ref):
        @pl.when(pl.program_id(0) == 0)
        def _():
            gg_ref[...] = jnp.zeros_like(gg_ref)

        xv = x_ref[...]
        r = lax.rsqrt(jnp.mean(xv * xv, axis=-1, keepdims=True) + EPS)
        xn = xv * r
        dhv = dh_ref[...]
        dxn = dhv * g_ref[...]
        dx_ref[...] = dres_ref[...] + r * (dxn - xn * jnp.mean(dxn * xn, axis=-1, keepdims=True))
        gg_ref[...] += jnp.sum(dhv * xn, axis=0, keepdims=True)

    row = pl.BlockSpec((tr, D), lambda i: (i, 0))
    vec = pl.BlockSpec((1, D), lambda i: (0, 0))
    return pl.pallas_call(
        body, name=name,
        out_shape=(jax.ShapeDtypeStruct((T, D), F32), jax.ShapeDtypeStruct((1, D), F32)),
        grid=(T // tr,), in_specs=[row, vec, row, row], out_specs=(row, vec),
        compiler_params=_params(("arbitrary",), 6 * _nbytes((tr, D), F32)),
    )(x, g, dh, dres)


def _lat_fwd(z, gq, gkv, wq, wkv, cos_a, sin_a, tr=256):
    T = z.shape[0]
    lat_blk = (4 * D_MODEL) // LAT

    def body(z_ref, gq_ref, gkv_ref, wq_ref, wkv_ref, cos_ref, sin_ref, q_ref, k_ref, v_ref, cqn_ref, ckvn_ref):
        zl = z_ref[...]
        cos_v, sin_v = cos_ref[...], sin_ref[...]
        cq = zl[:, :Q_RANK]
        ckv = zl[:, Q_RANK:Q_RANK + KV_RANK]
        krb = zl[:, Q_RANK + KV_RANK:]
        cqn = ((cq * lax.rsqrt(jnp.mean(cq * cq, axis=-1, keepdims=True) + EPS)) * gq_ref[...]).astype(MXU_DTYPE)
        ckvn = ((ckv * lax.rsqrt(jnp.mean(ckv * ckv, axis=-1, keepdims=True) + EPS)) * gkv_ref[...]).astype(MXU_DTYPE)
        cqn_ref[...] = cqn
        ckvn_ref[...] = ckvn
        krr = _rope_mix(krb, cos_v, sin_v).astype(MXU_DTYPE)
        q = jnp.dot(cqn, wq_ref[...], preferred_element_type=F32)
        kv = jnp.dot(ckvn, wkv_ref[...], preferred_element_type=F32)
        for h in range(HEADS):
            o = h * HEAD_PAD
            q_ref[:, o:o + NOPE] = q[:, o:o + NOPE].astype(MXU_DTYPE)
            q_ref[:, o + NOPE:o + HEAD_PAD] = _rope_mix(q[:, o + NOPE:o + HEAD_PAD], cos_v, sin_v).astype(MXU_DTYPE)
            k_ref[:, o:o + NOPE] = kv[:, h * NOPE:(h + 1) * NOPE].astype(MXU_DTYPE)
            k_ref[:, o + NOPE:o + HEAD_PAD] = krr
        v_ref[...] = kv[:, HEADS * NOPE:].astype(MXU_DTYPE)

    def row(w):
        return pl.BlockSpec((tr, w), lambda i: (i, 0))

    def full(a):
        return pl.BlockSpec(a.shape, lambda i: (0, 0))

    return pl.pallas_call(
        body, name="lat_fwd",
        out_shape=(jax.ShapeDtypeStruct((T, HEADS * HEAD_PAD), MXU_DTYPE), jax.ShapeDtypeStruct((T, HEADS * HEAD_PAD), MXU_DTYPE),
                   jax.ShapeDtypeStruct((T, HEADS * NOPE), MXU_DTYPE), jax.ShapeDtypeStruct((T, Q_RANK), MXU_DTYPE),
                   jax.ShapeDtypeStruct((T, KV_RANK), MXU_DTYPE)),
        grid=(T // tr,),
        in_specs=[pl.BlockSpec((tr, LAT), lambda i: (i, lat_blk)), full(gq), full(gkv), full(wq), full(wkv), row(128), row(128)],
        out_specs=(row(HEADS * HEAD_PAD), row(HEADS * HEAD_PAD), row(HEADS * NOPE), row(Q_RANK), row(KV_RANK)),
        compiler_params=_params(("parallel",), 8 * _nbytes((tr, HEADS * HEAD_PAD), F32)),
    )(z, gq, gkv, wq, wkv, cos_a, sin_a)


ATT_BLOCK = 256
_SCALE = QK_DIM ** -0.5


def _causal_mask(n):
    return lax.broadcasted_iota(jnp.int32, (n, n), 1) <= lax.broadcasted_iota(jnp.int32, (n, n), 0)


def _causal_mask_t(n):
    return lax.broadcasted_iota(jnp.int32, (n, n), 0) <= lax.broadcasted_iota(jnp.int32, (n, n), 1)


ATT_HEADS = 4


def _attn_fwd(q, k, v, B, S):
    tq = ATT_BLOCK
    nq = S // tq
    T = B * S
    hp, groups = ATT_HEADS, HEADS // ATT_HEADS

    def body(q_ref, k_ref, v_ref, o_ref, *lse_refs):
        qi = pl.program_id(2)
        qs = [q_ref[:, t * HEAD_PAD:(t + 1) * HEAD_PAD] for t in range(hp)]

        def scores(j, t):
            rows = pl.ds(pl.multiple_of(j * tq, tq), tq)
            return lax.dot_general(k_ref[rows, t * HEAD_PAD:(t + 1) * HEAD_PAD], qs[t], _DIMS["nt"],
                                   preferred_element_type=F32)

        def step(j, carry, last):
            rows = pl.ds(pl.multiple_of(j * tq, tq), tq)
            out = []
            for t in range(hp):
                m, l, acc, st = carry[t]
                st_next = st if last else scores(j + 1, t)
                st = st * _SCALE
                if last:
                    st = jnp.where(_causal_mask_t(tq), st, NEG)
                m_new = jnp.maximum(m, jnp.max(st, axis=0, keepdims=True))
                alpha = jnp.exp(m - m_new)
                p = jnp.exp(st - m_new)
                l = alpha * l + jnp.sum(p, axis=0, keepdims=True)
                acc = alpha * acc + lax.dot_general(v_ref[rows, t * NOPE:(t + 1) * NOPE], p.astype(MXU_DTYPE),
                                                    _DIMS["tn"], preferred_element_type=F32)
                out.append((m_new, l, acc, st_next))
            return tuple(out)

        init = tuple((jnp.full((1, tq), NEG, F32), jnp.zeros((1, tq), F32), jnp.zeros((NOPE, tq), F32), scores(0, t))
                     for t in range(hp))
        carry = lax.fori_loop(0, qi, lambda j, c: step(j, c, False), init)
        carry = step(qi, carry, True)
        for t in range(hp):
            m, l, acc, _ = carry[t]
            o_ref[:, t * NOPE:(t + 1) * NOPE] = (acc / l).T
            lse_refs[t][0] = m + jnp.log(l)

    lse_sds = jax.ShapeDtypeStruct((groups * B * nq, 1, tq), F32)
    lse_spec = pl.BlockSpec((1, 1, tq), lambda b, h, i: ((h * B + b) * nq + i, 0, 0))
    return pl.pallas_call(
        body, name="attn_fwd",
        out_shape=(jax.ShapeDtypeStruct((T, HEADS * NOPE), F32),) + (lse_sds,) * hp,
        grid=(B, groups, nq),
        in_specs=[pl.BlockSpec((tq, hp * HEAD_PAD), lambda b, h, i: (b * nq + i, h)),
                  pl.BlockSpec((S, hp * HEAD_PAD), lambda b, h, i: (b, h)),
                  pl.BlockSpec((S, hp * NOPE), lambda b, h, i: (b, h))],
        out_specs=(pl.BlockSpec((tq, hp * NOPE), lambda b, h, i: (b * nq + i, h)),) + (lse_spec,) * hp,
        compiler_params=_params(("parallel", "parallel", "arbitrary"), 4 * hp * _nbytes((S, HEAD_PAD), MXU_DTYPE)),
    )(q, k, v)


def _attn_bwd(q, k, v, do, lses, delta, B, S):
    tq = ATT_BLOCK
    nq = S // tq
    T = B * S
    hp, groups = ATT_HEADS, HEADS // ATT_HEADS

    def body(q_ref, k_ref, v_ref, do_ref, *refs):
        lse_refs, dl_refs = refs[:hp], refs[hp:2 * hp]
        dq_out, dk_ref, dv_ref, dq_ref = refs[2 * hp:]
        kj = pl.program_id(2)

        @pl.when(kj == 0)
        def _():
            dq_ref[...] = jnp.zeros_like(dq_ref)

        def products(i, t):
            rows = pl.ds(pl.multiple_of(i * tq, tq), tq)
            st = lax.dot_general(k_ref[:, t * HEAD_PAD:(t + 1) * HEAD_PAD], q_ref[rows, t * HEAD_PAD:(t + 1) * HEAD_PAD],
                                 _DIMS["nt"], preferred_element_type=F32)
            dpt = lax.dot_general(v_ref[:, t * NOPE:(t + 1) * NOPE], do_ref[rows, t * NOPE:(t + 1) * NOPE],
                                  _DIMS["nt"], preferred_element_type=F32)
            return st, dpt

        def step(i, carry, masked):
            rows = pl.ds(pl.multiple_of(i * tq, tq), tq)
            nxt = jnp.minimum(i + 1, nq - 1)
            out = []
            for t in range(hp):
                dk, dv, st, dpt = carry[t]
                st_next, dpt_next = products(nxt, t)
                qk_cols = slice(t * HEAD_PAD, (t + 1) * HEAD_PAD)
                v_cols = slice(t * NOPE, (t + 1) * NOPE)
                p = jnp.exp(st * _SCALE - lse_refs[t][i])
                if masked:
                    p = jnp.where(_causal_mask_t(tq), p, 0.0)
                dv = dv + jnp.dot(p.astype(MXU_DTYPE), do_ref[rows, v_cols], preferred_element_type=F32)
                ds = (p * (dpt - dl_refs[t][i]) * _SCALE).astype(MXU_DTYPE)
                dk = dk + jnp.dot(ds, q_ref[rows, qk_cols], preferred_element_type=F32)
                dq_ref[rows, qk_cols] += lax.dot_general(ds, k_ref[:, qk_cols], _DIMS["tn"], preferred_element_type=F32)
                out.append((dk, dv, st_next, dpt_next))
            return tuple(out)

        init = tuple((jnp.zeros((tq, HEAD_PAD), F32), jnp.zeros((tq, NOPE), F32)) + products(kj, t) for t in range(hp))
        carry = step(kj, init, True)
        carry = lax.fori_loop(kj + 1, nq, lambda i, c: step(i, c, False), carry)
        for t in range(hp):
            dk_ref[:, t * HEAD_PAD:(t + 1) * HEAD_PAD] = carry[t][0].astype(dk_ref.dtype)
            dv_ref[:, t * NOPE:(t + 1) * NOPE] = carry[t][1].astype(dv_ref.dtype)

        @pl.when(kj == nq - 1)
        def _():
            dq_out[...] = dq_ref[...].astype(dq_out.dtype)

    seq = lambda w: pl.BlockSpec((S, w), lambda b, h, j: (b, h))
    blk = lambda w: pl.BlockSpec((tq, w), lambda b, h, j: (b * nq + j, h))
    lse_spec = pl.BlockSpec((nq, 1, tq), lambda b, h, j: (h * B + b, 0, 0))
    dl_specs = [pl.BlockSpec((nq, 1, tq), lambda b, h, j, t=t: ((h * hp + t) * B + b, 0, 0)) for t in range(hp)]
    return pl.pallas_call(
        body, name="attn_bwd",
        out_shape=(jax.ShapeDtypeStruct((T, HEADS * HEAD_PAD), MXU_DTYPE), jax.ShapeDtypeStruct((T, HEADS * HEAD_PAD), MXU_DTYPE),
                   jax.ShapeDtypeStruct((T, HEADS * NOPE), MXU_DTYPE)),
        grid=(B, groups, nq),
        in_specs=[seq(hp * HEAD_PAD), blk(hp * HEAD_PAD), blk(hp * NOPE), seq(hp * NOPE)] + [lse_spec] * hp + dl_specs,
        out_specs=(seq(hp * HEAD_PAD), blk(hp * HEAD_PAD), blk(hp * NOPE)),
        scratch_shapes=[pltpu.VMEM((S, hp * HEAD_PAD), F32)],
        compiler_params=_params(("parallel", "parallel", "arbitrary"), 8 * hp * _nbytes((S, HEAD_PAD), F32)),
    )(q, k, v, do, *lses, *([delta] * hp))


MIX_ROWS = 256


def _tril_weights(ws_ref, g):
    return jnp.where(_causal_mask(CHUNK), ws_ref[g], 0.0).astype(MXU_DTYPE)


def _layer_norm_stats(va):
    mu = jnp.mean(va, axis=-1, keepdims=True)
    xc = va - mu
    rs = lax.rsqrt(jnp.mean(xc * xc, axis=-1, keepdims=True) + EPS)
    return xc * rs


def _mix_specs(tr):
    zcol = lambda c: pl.BlockSpec((tr, D_MODEL), lambda i, c=c: (i, c))
    row = pl.BlockSpec((tr, D_MODEL), lambda i: (i, 0))
    vec = pl.BlockSpec((1, D_MODEL), lambda i: (0, 0))
    ws = pl.BlockSpec((A_GROUPS, CHUNK, CHUNK), lambda i: (0, 0, 0))
    bs = pl.BlockSpec((CHUNK, 128), lambda i: (0, 0))
    return zcol, row, vec, ws, bs


def _mix_fwd(z, yb, ln_g, ln_b, ws, bs_t):
    T = z.shape[0]
    tr = MIX_ROWS
    zcol, row, vec, ws_spec, bs_spec = _mix_specs(tr)

    def body(zu_ref, zv_ref, zga_ref, zgb_ref, yb_ref, g_ref, b_ref, ws_ref, bs_ref, out_ref, vn_s):
        vhat = _layer_norm_stats(_gelu(zv_ref[...]))
        vn_s[...] = (vhat * g_ref[...] + b_ref[...]).astype(MXU_DTYPE)
        for g in range(A_GROUPS):
            w = _tril_weights(ws_ref, g)
            bias = bs_ref[:, g:g + 1]
            cols = slice(g * CHUNK, (g + 1) * CHUNK)
            for c in range(tr // CHUNK):
                rows = slice(c * CHUNK, (c + 1) * CHUNK)
                mixed = jnp.dot(w, vn_s[rows, cols], preferred_element_type=F32) + bias
                ya = _gelu(zu_ref[rows, cols]) * mixed
                merged = _sigmoid(zga_ref[rows, cols]) * ya + _sigmoid(zgb_ref[rows, cols]) * yb_ref[rows, cols]
                out_ref[rows, cols] = merged.astype(MXU_DTYPE)

    return pl.pallas_call(
        body, name="mix_fwd", out_shape=jax.ShapeDtypeStruct((T, D_MODEL), MXU_DTYPE), grid=(T // tr,),
        in_specs=[zcol(0), zcol(1), zcol(2), zcol(3), row, vec, vec, ws_spec, bs_spec], out_specs=row,
        scratch_shapes=[pltpu.VMEM((tr, D_MODEL), MXU_DTYPE)],
        compiler_params=_params(("parallel",), 8 * _nbytes((tr, D_MODEL), F32)),
    )(z, z, z, z, yb, ln_g, ln_b, ws, bs_t)


def _mix_bwd(z, yb, dm, ln_g, ln_b, ws, bs_t):
    T = z.shape[0]
    tr = MIX_ROWS
    zcol, row, vec, ws_spec, bs_spec = _mix_specs(tr)

    def body(zu_ref, zv_ref, zga_ref, zgb_ref, yb_ref, dm_ref, g_ref, b_ref, ws_ref, bs_ref,
             dz_ref, dyb_ref, dl_ref, gws_ref, gbs_ref, glg_ref, glb_ref, vn_s, dvn_s):
        @pl.when(pl.program_id(0) == 0)
        def _():
            gws_ref[...] = jnp.zeros_like(gws_ref)
            gbs_ref[...] = jnp.zeros_like(gbs_ref)
            glg_ref[...] = jnp.zeros_like(glg_ref)
            glb_ref[...] = jnp.zeros_like(glb_ref)

        lane = lax.broadcasted_iota(jnp.int32, (CHUNK, 128), 1)
        va, dgelu_v = _gelu_and_grad(zv_ref[...])
        mu = jnp.mean(va, axis=-1, keepdims=True)
        xc = va - mu
        rs = lax.rsqrt(jnp.mean(xc * xc, axis=-1, keepdims=True) + EPS)
        vhat = xc * rs
        vn_s[...] = (vhat * g_ref[...] + b_ref[...]).astype(MXU_DTYPE)
        gbs_acc = jnp.zeros((CHUNK, 128), F32)
        for g in range(A_GROUPS):
            w = _tril_weights(ws_ref, g)
            bias = bs_ref[:, g:g + 1]
            cols = slice(g * CHUNK, (g + 1) * CHUNK)
            gw_acc = jnp.zeros((CHUNK, CHUNK), F32)
            for c in range(tr // CHUNK):
                rows = slice(c * CHUNK, (c + 1) * CHUNK)
                vn = vn_s[rows, cols]
                mixed = jnp.dot(w, vn, preferred_element_type=F32) + bias
                ua, dgelu_u = _gelu_and_grad(zu_ref[rows, cols])
                dmv = dm_ref[rows, cols]
                sa = _sigmoid(zga_ref[rows, cols])
                dya = dmv * sa
                dz_ref[rows, 2 * D_MODEL + g * CHUNK:2 * D_MODEL + (g + 1) * CHUNK] = (
                    dmv * (ua * mixed) * (sa * (1.0 - sa))).astype(dz_ref.dtype)
                dz_ref[rows, cols] = (dya * mixed * dgelu_u).astype(dz_ref.dtype)
                dmix = dya * ua
                gbs_acc = gbs_acc + jnp.where(lane == g, jnp.sum(dmix, axis=-1, keepdims=True), 0.0)
                dmix_b = dmix.astype(MXU_DTYPE)
                gw_acc = gw_acc + lax.dot_general(dmix_b, vn, _DIMS["nt"], preferred_element_type=F32)
                dvn_s[rows, cols] = lax.dot_general(w, dmix_b, _DIMS["tn"], preferred_element_type=F32)
            gws_ref[g] += jnp.where(_causal_mask(CHUNK), gw_acc, 0.0)
        gbs_ref[...] += gbs_acc

        dvn = dvn_s[...]
        glg_ref[...] += jnp.sum(dvn * vhat, axis=0, keepdims=True)
        glb_ref[...] += jnp.sum(dvn, axis=0, keepdims=True)
        dvh = dvn * g_ref[...]
        dva = rs * (dvh - jnp.mean(dvh, axis=-1, keepdims=True) - vhat * jnp.mean(dvh * vhat, axis=-1, keepdims=True))
        dz_ref[:, D_MODEL:2 * D_MODEL] = (dva * dgelu_v).astype(dz_ref.dtype)

        dmv = dm_ref[...]
        ybv = yb_ref[...]
        sb = _sigmoid(zgb_ref[...])
        dyb = dmv * sb
        dyb_ref[...] = dyb.astype(dyb_ref.dtype)
        dz_ref[:, 3 * D_MODEL:4 * D_MODEL] = (dmv * ybv * (sb * (1.0 - sb))).astype(dz_ref.dtype)
        dz_ref[:, 4 * D_MODEL:] = jnp.zeros((tr, LAT), dz_ref.dtype)
        prod = dyb * ybv
        sel = (lax.broadcasted_iota(jnp.int32, (HEADS, D_MODEL), 1) // NOPE
               == lax.broadcasted_iota(jnp.int32, (HEADS, D_MODEL), 0)).astype(jnp.bfloat16)
        hi = prod.astype(jnp.bfloat16)
        rest = prod - hi.astype(F32)
        mid = rest.astype(jnp.bfloat16)
        lo = (rest - mid.astype(F32)).astype(jnp.bfloat16)
        dl_ref[...] = (lax.dot_general(sel, hi, _DIMS["nt"], preferred_element_type=F32)
                       + lax.dot_general(sel, mid, _DIMS["nt"], preferred_element_type=F32)
                       + lax.dot_general(sel, lo, _DIMS["nt"], preferred_element_type=F32))

    return pl.pallas_call(
        body, name="mix_bwd",
        out_shape=(jax.ShapeDtypeStruct((T, IN_PAD), MXU_DTYPE), jax.ShapeDtypeStruct((T, D_MODEL), MXU_DTYPE),
                   jax.ShapeDtypeStruct((HEADS, T), F32), jax.ShapeDtypeStruct((A_GROUPS, CHUNK, CHUNK), F32),
                   jax.ShapeDtypeStruct((CHUNK, 128), F32), jax.ShapeDtypeStruct((1, D_MODEL), F32),
                   jax.ShapeDtypeStruct((1, D_MODEL), F32)),
        grid=(T // tr,),
        in_specs=[zcol(0), zcol(1), zcol(2), zcol(3), row, row, vec, vec, ws_spec, bs_spec],
        out_specs=(pl.BlockSpec((tr, IN_PAD), lambda i: (i, 0)), row, pl.BlockSpec((HEADS, tr), lambda i: (0, i)),
                   ws_spec, bs_spec, vec, vec),
        scratch_shapes=[pltpu.VMEM((tr, D_MODEL), MXU_DTYPE), pltpu.VMEM((tr, D_MODEL), F32)],
        compiler_params=_params(("arbitrary",), 12 * _nbytes((tr, D_MODEL), F32)),
    )(z, z, z, z, yb, dm, ln_g, ln_b, ws, bs_t)


def _lat_bwd(dz, z, dq, dk, dv, gq, gkv, wq, wkv, cos_a, sin_a, tr=256):
    T = z.shape[0]
    lat_blk = (4 * D_MODEL) // LAT

    def body(dz_in, z_ref, dq_ref, dk_ref, dv_ref, gq_ref, gkv_ref, wq_ref, wkv_ref, cos_ref, sin_ref,
             dz_ref, dqr_ref, dkv_ref, ggq_ref, ggkv_ref):
        del dz_in

        @pl.when(pl.program_id(0) == 0)
        def _():
            ggq_ref[...] = jnp.zeros_like(ggq_ref)
            ggkv_ref[...] = jnp.zeros_like(ggkv_ref)

        cos_v, sin_v = cos_ref[...], sin_ref[...]
        dkr = jnp.zeros((tr, 128), F32)
        for h in range(HEADS):
            o = h * HEAD_PAD
            dqr_ref[:, o:o + NOPE] = dq_ref[:, o:o + NOPE].astype(MXU_DTYPE)
            dqr_ref[:, o + NOPE:o + HEAD_PAD] = _rope_mix_bwd(dq_ref[:, o + NOPE:o + HEAD_PAD], cos_v, sin_v).astype(MXU_DTYPE)
            dkv_ref[:, h * NOPE:(h + 1) * NOPE] = dk_ref[:, o:o + NOPE].astype(MXU_DTYPE)
            dkr = dkr + _rope_mix_bwd(dk_ref[:, o + NOPE:o + HEAD_PAD], cos_v, sin_v)
        dkv_ref[:, HEADS * NOPE:] = dv_ref[...]
        dcqn = lax.dot_general(dqr_ref[...], wq_ref[...], _DIMS["nt"], preferred_element_type=F32)
        dckvn = lax.dot_general(dkv_ref[...], wkv_ref[...], _DIMS["nt"], preferred_element_type=F32)

        zl = z_ref[...]

        def rms_bwd(c, dn, g_ref, gg_ref):
            r = lax.rsqrt(jnp.mean(c * c, axis=-1, keepdims=True) + EPS)
            ch = c * r
            gg_ref[...] += jnp.sum(dn * ch, axis=0, keepdims=True)
            dch = dn * g_ref[...]
            return r * (dch - ch * jnp.mean(dch * ch, axis=-1, keepdims=True))

        dz_ref[:, :Q_RANK] = rms_bwd(zl[:, :Q_RANK], dcqn, gq_ref, ggq_ref).astype(dz_ref.dtype)
        dz_ref[:, Q_RANK:Q_RANK + KV_RANK] = rms_bwd(zl[:, Q_RANK:Q_RANK + KV_RANK], dckvn, gkv_ref, ggkv_ref).astype(dz_ref.dtype)
        dz_ref[:, Q_RANK + KV_RANK:] = dkr.astype(dz_ref.dtype)

    def row(w):
        return pl.BlockSpec((tr, w), lambda i: (i, 0))

    def full(a):
        return pl.BlockSpec(a.shape, lambda i: (0, 0))

    lat = pl.BlockSpec((tr, LAT), lambda i: (i, lat_blk))
    return pl.pallas_call(
        body, name="lat_bwd",
        out_shape=(jax.ShapeDtypeStruct(dz.shape, dz.dtype), jax.ShapeDtypeStruct((T, HEADS * HEAD_PAD), MXU_DTYPE),
                   jax.ShapeDtypeStruct((T, 2 * HEADS * NOPE), MXU_DTYPE), jax.ShapeDtypeStruct(gq.shape, F32),
                   jax.ShapeDtypeStruct(gkv.shape, F32)),
        grid=(T // tr,),
        in_specs=[pl.BlockSpec(memory_space=pl.ANY), lat, row(HEADS * HEAD_PAD), row(HEADS * HEAD_PAD), row(HEADS * NOPE),
                  full(gq), full(gkv), full(wq), full(wkv), row(128), row(128)],
        out_specs=(lat, row(HEADS * HEAD_PAD), row(2 * HEADS * NOPE), full(gq), full(gkv)),
        input_output_aliases={0: 0},
        compiler_params=_params(("arbitrary",), 8 * _nbytes((tr, HEADS * HEAD_PAD), F32)),
    )(dz, z, dq, dk, dv, gq, gkv, wq, wkv, cos_a, sin_a)


GATE_ROWS = 64
HALO = 8


def _taps(ref, half, r, first):
    C = GATE_ROWS
    if first:
        xs = jnp.concatenate([jnp.zeros((HALO, ref.shape[-1]), F32), ref[half, 0:C, :]], axis=0)
    else:
        xs = ref[half, pl.ds(pl.multiple_of(r * C - HALO, HALO), C + HALO), :]
    return xs[HALO:, :], pltpu.roll(xs, 1, 0)[HALO:, :], pltpu.roll(xs, 2, 0)[HALO:, :]


def _conv_taps(taps, cw, cb):
    x0, x1, x2 = taps
    return cb + cw[0:1, :] * x2 + cw[1:2, :] * x1 + cw[2:3, :] * x0


def _fold8(x):
    acc = x[0:8, :]
    for i in range(1, x.shape[0] // 8):
        acc = acc + x[8 * i:8 * (i + 1), :]
    return acc


def _gate_fwd(up3, conv_w, conv_b, B, S):
    T = B * S
    W = FF_TILE
    C = GATE_ROWS

    def body(up_ref, cw_ref, cb_ref, act_ref):
        def chunk(r, first):
            gate = _conv_taps(_taps(up_ref, 0, r, first), cw_ref[0], cb_ref[0])
            val = _conv_taps(_taps(up_ref, 1, r, first), cw_ref[1], cb_ref[1])
            base = 0 if first else pl.multiple_of(r * C, C)
            act_ref[pl.ds(base, C), :] = (gate * _sigmoid(gate) * val).astype(act_ref.dtype)

        chunk(0, True)

        @pl.loop(1, S // C)
        def _(r):
            chunk(r, False)

    return pl.pallas_call(
        body, name="gate_fwd", out_shape=jax.ShapeDtypeStruct((T, D_FF), MXU_DTYPE), grid=(B, N_FF_TILES),
        in_specs=[pl.BlockSpec((2, S, W), lambda b, j: (0, b, j)), pl.BlockSpec((2, 3, W), lambda b, j: (0, 0, j)),
                  pl.BlockSpec((2, 1, W), lambda b, j: (0, 0, j))],
        out_specs=pl.BlockSpec((S, W), lambda b, j: (b, j)),
        compiler_params=_params(("parallel", "parallel"), 6 * _nbytes((S, W), F32)),
    )(up3, conv_w, conv_b)


def _gate_bwd(up3, dact, conv_w, conv_b, B, S):
    T = B * S
    W = FF_TILE
    C = GATE_ROWS

    def body(up_ref, da_ref, cw_ref, cb_ref, dup_ref, gcw_ref, gcb_ref, d_s):
        @pl.when(pl.program_id(1) == 0)
        def _():
            gcw_ref[...] = jnp.zeros_like(gcw_ref)
            gcb_ref[...] = jnp.zeros_like(gcb_ref)

        def chunk(r, first, sums):
            rows = pl.ds(0 if first else pl.multiple_of(r * C, C), C)
            taps = [_taps(up_ref, half, r, first) for half in (0, 1)]
            gate = _conv_taps(taps[0], cw_ref[0], cb_ref[0])
            val = _conv_taps(taps[1], cw_ref[1], cb_ref[1])
            sg = _sigmoid(gate)
            da = da_ref[rows, :]
            d_halves = (da * val * (sg * (1.0 + gate * (1.0 - sg))), da * (gate * sg))
            out = []
            for half, dup in enumerate(d_halves):
                d_s[half, rows, :] = dup
                x0, x1, x2 = taps[half]
                sb, s0, s1, s2 = sums[half]
                out.append((sb + _fold8(dup), s0 + _fold8(dup * x2), s1 + _fold8(dup * x1), s2 + _fold8(dup * x0)))
            return tuple(out)

        zeros = tuple(tuple(jnp.zeros((8, W), F32) for _ in range(4)) for _ in range(2))
        sums = chunk(0, True, zeros)
        sums = lax.fori_loop(1, S // C, lambda r, s: chunk(r, False, s), sums)
        for half in (0, 1):
            sb, s0, s1, s2 = sums[half]
            gcb_ref[half] += jnp.sum(sb, axis=0, keepdims=True)
            gcw_ref[half, 0:1, :] += jnp.sum(s0, axis=0, keepdims=True)
            gcw_ref[half, 1:2, :] += jnp.sum(s1, axis=0, keepdims=True)
            gcw_ref[half, 2:3, :] += jnp.sum(s2, axis=0, keepdims=True)

        d_s[:, S:S + HALO, :] = jnp.zeros((2, HALO, W), F32)

        @pl.loop(0, S // C)
        def _(r):
            base = pl.multiple_of(r * C, C)
            for half in (0, 1):
                ds_ = d_s[half, pl.ds(base, C + HALO), :]
                cw = cw_ref[half]
                dx = (cw[2:3, :] * ds_[:C, :] + cw[1:2, :] * pltpu.roll(ds_, C + HALO - 1, 0)[:C, :]
                      + cw[0:1, :] * pltpu.roll(ds_, C + HALO - 2, 0)[:C, :])
                dup_ref[half, pl.ds(base, C), :] = dx.astype(dup_ref.dtype)

    up_spec = pl.BlockSpec((2, S, W), lambda j, b: (0, b, j))
    cw_spec = pl.BlockSpec((2, 3, W), lambda j, b: (0, 0, j))
    cb_spec = pl.BlockSpec((2, 1, W), lambda j, b: (0, 0, j))
    return pl.pallas_call(
        body, name="gate_bwd",
        out_shape=(jax.ShapeDtypeStruct((2, T, D_FF), MXU_DTYPE), jax.ShapeDtypeStruct((2, 3, D_FF), F32),
                   jax.ShapeDtypeStruct((2, 1, D_FF), F32)),
        grid=(N_FF_TILES, B),
        in_specs=[up_spec, pl.BlockSpec((S, W), lambda j, b: (b, j)), cw_spec, cb_spec],
        out_specs=(up_spec, cw_spec, cb_spec),
        scratch_shapes=[pltpu.VMEM((2, S + HALO, W), F32)],
        compiler_params=_params(("parallel", "arbitrary"), 10 * _nbytes((S, W), F32)),
    )(up3, dact, conv_w, conv_b)


def _final(x2, tgt, g, tr=512):
    T, D = x2.shape

    def body(x_ref, t_ref, g_ref, dx_ref, loss_ref, gg_ref):
        @pl.when(pl.program_id(0) == 0)
        def _():
            loss_ref[...] = jnp.zeros_like(loss_ref)
            gg_ref[...] = jnp.zeros_like(gg_ref)

        xv = x_ref[...]
        gv = g_ref[...]
        r = lax.rsqrt(jnp.mean(xv * xv, axis=-1, keepdims=True) + EPS)
        xn = xv * r
        err = xn * gv - t_ref[...]
        loss_ref[...] += 0.5 * jnp.sum(jnp.mean(err * err, axis=-1, keepdims=True), axis=0, keepdims=True)
        dy = err * (1.0 / D)
        gg_ref[...] += jnp.sum(dy * xn, axis=0, keepdims=True)
        dxn = dy * gv
        dx_ref[...] = r * (dxn - xn * jnp.mean(dxn * xn, axis=-1, keepdims=True))

    row = pl.BlockSpec((tr, D), lambda i: (i, 0))
    vec = pl.BlockSpec((1, D), lambda i: (0, 0))
    return pl.pallas_call(
        body, name="final_loss",
        out_shape=(jax.ShapeDtypeStruct((T, D), F32), jax.ShapeDtypeStruct((1, 128), F32), jax.ShapeDtypeStruct((1, D), F32)),
        grid=(T // tr,), in_specs=[row, row, vec],
        out_specs=(row, pl.BlockSpec((1, 128), lambda i: (0, 0)), vec),
        compiler_params=_params(("arbitrary",), 6 * _nbytes((tr, D), F32)),
    )(x2, tgt, g)


def _sum_slabs(parts, name, tr):
    rows, cols = parts[0].shape
    n = len(parts)

    def body(*refs):
        acc = refs[0][...]
        for r in refs[1:n]:
            acc = acc + r[...]
        refs[n][...] = acc

    blk = pl.BlockSpec((tr, cols), lambda i: (i, 0))
    return pl.pallas_call(
        body, name=name, out_shape=jax.ShapeDtypeStruct((rows, cols), F32), grid=(rows // tr,),
        in_specs=[blk] * n, out_specs=blk,
        compiler_params=_params(("parallel",), (n + 1) * _nbytes((tr, cols), F32)),
    )(*parts)


ADAMW_BLOCK_BYTES = 2400 * 1024


def _adamw(w, g, m, v, name):
    lead = w.ndim == 3
    rows, cols = w.shape[-2:]
    fits = [d for d in range(8, rows + 1, 8) if rows % d == 0 and d * cols * 4 <= ADAMW_BLOCK_BYTES]
    tr = max(fits) if fits else rows
    c1 = 1.0 - ADAM_B1 ** ADAM_STEP
    c2 = 1.0 - ADAM_B2 ** ADAM_STEP

    def body(w_ref, g_ref, m_ref, v_ref, d_ref, nm_ref, nv_ref):
        gv = g_ref[...]
        nm = ADAM_B1 * m_ref[...] + (1.0 - ADAM_B1) * gv
        nv = ADAM_B2 * v_ref[...] + (1.0 - ADAM_B2) * (gv * gv)
        nm_ref[...] = nm
        nv_ref[...] = nv
        d_ref[...] = -ADAM_LR * ((nm / c1) / (jnp.sqrt(nv / c2) + ADAM_EPS) + ADAM_WD * w_ref[...])

    blk = pl.BlockSpec((None, tr, cols), lambda i: (0, i, 0)) if lead else pl.BlockSpec((tr, cols), lambda i: (i, 0))
    sds = jax.ShapeDtypeStruct(w.shape, F32)
    return pl.pallas_call(
        body, name=name, out_shape=(sds, sds, sds), grid=(rows // tr,), in_specs=[blk] * 4, out_specs=(blk, blk, blk),
        compiler_params=_params(("parallel",), 7 * _nbytes((tr, cols), F32)),
    )(w, g, m, v)


_ANY = pl.BlockSpec(memory_space=pl.ANY)


def _place():
    x, y, c = lax.axis_index("x"), lax.axis_index("y"), lax.axis_index("c")
    chips = [(1 - x, y), (x, 1 - y), (1 - x, 1 - y)]
    return x, y, c, chips


def _gather_weights(shards):
    n = len(shards)

    def body(*refs):
        ins, outs = refs[:n], refs[n:2 * n]
        send, recv, fsend, frecv, osend, orecv = refs[2 * n:]
        x, y, c, chips = _place()
        me = 2 * x + y
        first, passed = [], []
        for w in range(n):
            first.append(pltpu.make_async_remote_copy(
                src_ref=ins[w], dst_ref=outs[w].at[me], send_sem=osend.at[w], recv_sem=orecv.at[w],
                device_id=(x, y, 1 - c), device_id_type=MESH))
        for w in range(n):
            for j, (px, py) in enumerate(chips):
                first.append(pltpu.make_async_remote_copy(
                    src_ref=ins[w].at[c], dst_ref=outs[w].at[me, c], send_sem=send.at[3 * w + j],
                    recv_sem=recv.at[3 * w + j], device_id=(px, py, c), device_id_type=MESH))
        for cp in first:
            cp.start()
        for w in range(n):
            for j, (px, py) in enumerate(chips):
                landed = outs[w].at[2 * px + py, c]
                pltpu.make_async_remote_copy(src_ref=landed, dst_ref=landed, send_sem=send.at[3 * w + j],
                                             recv_sem=recv.at[3 * w + j], device_id=(px, py, c),
                                             device_id_type=MESH).wait_recv()
                fw = pltpu.make_async_remote_copy(src_ref=landed, dst_ref=landed, send_sem=fsend.at[3 * w + j],
                                                  recv_sem=frecv.at[3 * w + j], device_id=(x, y, 1 - c),
                                                  device_id_type=MESH)
                fw.start()
                passed.append(fw)
        for w in range(n):
            for j, (px, py) in enumerate(chips):
                other = outs[w].at[2 * px + py, 1 - c]
                pltpu.make_async_remote_copy(src_ref=other, dst_ref=other, send_sem=fsend.at[3 * w + j],
                                             recv_sem=frecv.at[3 * w + j], device_id=(x, y, 1 - c),
                                             device_id_type=MESH).wait_recv()
        for w in range(n):
            own = outs[w].at[me]
            pltpu.make_async_remote_copy(src_ref=own, dst_ref=own, send_sem=osend.at[w], recv_sem=orecv.at[w],
                                         device_id=(x, y, 1 - c), device_id_type=MESH).wait_recv()
        for cp in first + passed:
            cp.wait_send()

    dma = lambda k: pltpu.SemaphoreType.DMA((k,))
    return pl.pallas_call(
        body, name="gather_weights",
        out_shape=tuple(jax.ShapeDtypeStruct((N_CHIPS,) + s.shape, s.dtype) for s in shards),
        in_specs=[_ANY] * n, out_specs=tuple([_ANY] * n),
        scratch_shapes=[dma(3 * n), dma(3 * n), dma(3 * n), dma(3 * n), dma(n), dma(n)],
    )(*shards)


_HBM = pl.BlockSpec(memory_space=pltpu.HBM)
_SEM = pl.BlockSpec(memory_space=pltpu.SEMAPHORE)
_EFFECT = pltpu.SideEffectType.DATAFLOW_SIDE_EFFECTING


SEMS_PER_ARRAY = 8


def _exchange_copies(srcs, lands, send, recv, mode):
    x, y, c, chips = _place()
    if mode == "swap":
        return [pltpu.make_async_remote_copy(
            src_ref=src.at[:, 1 - c], dst_ref=land, send_sem=send.at[SEMS_PER_ARRAY * w],
            recv_sem=recv.at[SEMS_PER_ARRAY * w], device_id=(x, y, 1 - c), device_id_type=MESH)
            for w, (src, land) in enumerate(zip(srcs, lands))]
    if mode == "all":
        flips = [(fx, fy, fc) for fx in (0, 1) for fy in (0, 1) for fc in (0, 1)][1:]
        peers = [(x ^ fx, y ^ fy, c ^ fc) for fx, fy, fc in flips]
        slot = 4 * x + 2 * y + c
    else:
        peers = [(px, py, c) for px, py in chips] + ([(x, y, 1 - c)] if mode == "gather" else [])
        slot = 2 * x + y
    cps = []
    for w, (src, land) in enumerate(zip(srcs, lands)):
        for k, peer in enumerate(peers):
            piece = src.at[2 * peer[0] + peer[1]] if mode == "scatter" else src
            cps.append(pltpu.make_async_remote_copy(
                src_ref=piece, dst_ref=land.at[slot], send_sem=send.at[SEMS_PER_ARRAY * w + k],
                recv_sem=recv.at[SEMS_PER_ARRAY * w + k], device_id=peer, device_id_type=MESH))
    return cps


def _exchange_start(srcs, name, mode, after):
    n = len(srcs)
    if mode == "swap":
        land_shapes = [(s.shape[0],) + s.shape[2:] for s in srcs]
    else:
        lead = {"gather": (N_CHIPS,), "scatter": (), "all": (2 * N_CHIPS,)}[mode]
        land_shapes = [lead + s.shape for s in srcs]

    def body(*refs):
        src_refs, land_refs = refs[:n], refs[n:2 * n]
        send, recv = refs[2 * n + 1], refs[2 * n + 2]
        token = refs[-1]
        for cp in _exchange_copies(src_refs, land_refs, send, recv, mode):
            cp.start()
        token[...] = jnp.zeros_like(token)

    sems = pltpu.SemaphoreType.DMA((SEMS_PER_ARRAY * n,))
    out = pl.pallas_call(
        body, name=name,
        out_shape=(sems, sems, *[pltpu.HBM(s.shape, s.dtype) for s in srcs],
                   *[pltpu.HBM(shp, s.dtype) for shp, s in zip(land_shapes, srcs)], jax.ShapeDtypeStruct((8, 128), F32)),
        in_specs=[_HBM] * (2 * n) + [_ANY],
        out_specs=(_SEM, _SEM, *[_HBM] * (2 * n), pl.BlockSpec(memory_space=pltpu.VMEM)),
        input_output_aliases={i: 2 + i for i in range(2 * n)},
        compiler_params=pltpu.CompilerParams(has_side_effects=_EFFECT),
    )(*[pltpu.with_memory_space_constraint(s, pltpu.HBM) for s in srcs],
      *[pltpu.with_memory_space_constraint(lax.empty(shp, s.dtype), pltpu.HBM) for shp, s in zip(land_shapes, srcs)],
      after)
    return out[0], out[1], out[2:2 + n], out[2 + n:2 + 2 * n], out[-1]


def _exchange_wait(started, name, mode, after):
    send, recv, src_thru, land_thru, _ = started
    n = len(src_thru)

    def body(*refs):
        src_refs, land_refs, send_ref, recv_ref = refs[:n], refs[n:2 * n], refs[2 * n], refs[2 * n + 1]
        for cp in _exchange_copies(src_refs, land_refs, send_ref, recv_ref, mode):
            cp.wait_send()
            cp.wait_recv()

    out = pl.pallas_call(
        body, name=name,
        out_shape=tuple(pltpu.HBM(a.shape, a.dtype) for a in list(src_thru) + list(land_thru)),
        in_specs=[_HBM] * (2 * n) + [_SEM, _SEM, _ANY], out_specs=tuple([_HBM] * (2 * n)),
        input_output_aliases={i: i for i in range(2 * n)},
        compiler_params=pltpu.CompilerParams(has_side_effects=_EFFECT),
    )(*src_thru, *land_thru, send, recv, after)
    return out[:n], out[n:]


def _swap_halves(gs, name):
    n = len(gs)

    def body(*refs):
        ins, outs, send, recv = refs[:n], refs[n:2 * n], refs[2 * n], refs[2 * n + 1]
        x, y, c, _ = _place()
        cps = []
        for w in range(n):
            cps.append(pltpu.make_async_remote_copy(
                src_ref=ins[w].at[:, 1 - c], dst_ref=outs[w], send_sem=send.at[w], recv_sem=recv.at[w],
                device_id=(x, y, 1 - c), device_id_type=MESH))
        for cp in cps:
            cp.start()
        for cp in cps:
            cp.wait()

    return pl.pallas_call(
        body, name=name,
        out_shape=tuple(jax.ShapeDtypeStruct((g.shape[0],) + g.shape[2:], g.dtype) for g in gs),
        in_specs=[_ANY] * n, out_specs=tuple([_ANY] * n),
        scratch_shapes=[pltpu.SemaphoreType.DMA((n,)), pltpu.SemaphoreType.DMA((n,))],
    )(*gs)


GRAD_PAYLOAD = jnp.bfloat16


def _half_blocks(half_rows, cols):
    if (half_rows // 2) % 16 == 0:
        return (half_rows // 2, cols), (lambda r: (r, 0))
    assert cols % 256 == 0, (half_rows, cols)
    return (half_rows, cols // 2), (lambda r: (0, r))


def _pair_sum(gs, gots, name):
    n = len(gs)
    core = lax.axis_index("c").astype(jnp.int32).reshape(1)

    def body(core_ref, *refs):
        del core_ref
        for w in range(n):
            refs[2 * n + w][...] = (refs[w][...] + refs[n + w][...]).astype(GRAD_PAYLOAD)

    in_specs, out_specs, out_shape, nbytes = [], [], [], 0
    cuts = [_half_blocks(g.shape[1] // 2, g.shape[2]) for g in gs]
    for g, ((br, bc), at) in zip(gs, cuts):
        per_half = (g.shape[1] // 2) // br
        in_specs.append(pl.BlockSpec((1, br, bc), lambda s, r, core, at=at, per_half=per_half:
                                     (s, per_half * core[0] + at(r)[0], at(r)[1])))
        nbytes += 3 * _nbytes((br, bc), F32)
    for g, ((br, bc), at) in zip(gs, cuts):
        in_specs.append(pl.BlockSpec((1, br, bc), lambda s, r, core, at=at: (s,) + at(r)))
        out_specs.append(pl.BlockSpec((1, br, bc), lambda s, r, core, at=at: (s,) + at(r)))
        out_shape.append(jax.ShapeDtypeStruct((g.shape[0], g.shape[1] // 2, g.shape[2]), GRAD_PAYLOAD))
    return pl.pallas_call(
        body, name=name, out_shape=tuple(out_shape),
        grid_spec=pltpu.PrefetchScalarGridSpec(num_scalar_prefetch=1, grid=(N_CHIPS, 2), in_specs=in_specs,
                                               out_specs=tuple(out_specs)),
        compiler_params=_params(("parallel", "parallel"), nbytes),
    )(core, *gs, *gots)


def _chip_sum(ps, landed):
    n = len(ps)
    x, y, c = lax.axis_index("x"), lax.axis_index("y"), lax.axis_index("c")
    where = jnp.stack([2 * x + y, 2 * (1 - x) + y, 2 * x + (1 - y), 2 * (1 - x) + (1 - y), c]).astype(jnp.int32)

    def body(where_ref, *refs):
        del where_ref
        for w in range(n):
            terms = [refs[4 * w + t][...].astype(F32) for t in range(4)]
            refs[4 * n + w][...] = ((terms[0] + terms[1]) + terms[2]) + terms[3]

    in_specs, out_specs, out_shape, args, nbytes = [], [], [], [], 0
    for p, a in zip(ps, landed):
        (br, bc), at = _half_blocks(a.shape[1], a.shape[2])
        blk = (1, br, bc)
        in_specs.append(pl.BlockSpec(blk, lambda r, where, at=at: (where[0],) + at(r)))
        args.append(p)
        for t in (1, 2, 3):
            in_specs.append(pl.BlockSpec(blk, lambda r, where, t=t, at=at: (where[t],) + at(r)))
            args.append(a)
        out_specs.append(pl.BlockSpec(blk, lambda r, where, at=at: (where[4],) + at(r)))
        out_shape.append(jax.ShapeDtypeStruct((2,) + a.shape[1:], F32))
        nbytes += 4 * _nbytes(blk, F32)
    return pl.pallas_call(
        body, name="grad_chip_sum", out_shape=tuple(out_shape),
        grid_spec=pltpu.PrefetchScalarGridSpec(num_scalar_prefetch=1, grid=(2,), in_specs=in_specs,
                                               out_specs=tuple(out_specs)),
        compiler_params=_params(("parallel",), nbytes),
    )(where, *args)


def _join_halves(ss):
    n = len(ss)

    def body(*refs):
        outs, send, recv = refs[n:2 * n], refs[2 * n], refs[2 * n + 1]
        x, y, c, _ = _place()
        cps = []
        for w in range(n):
            cps.append(pltpu.make_async_remote_copy(
                src_ref=outs[w].at[c], dst_ref=outs[w].at[c], send_sem=send.at[w], recv_sem=recv.at[w],
                device_id=(x, y, 1 - c), device_id_type=MESH))
        for cp in cps:
            cp.start()
        for w in range(n):
            got = outs[w].at[1 - c]
            pltpu.make_async_remote_copy(src_ref=got, dst_ref=got, send_sem=send.at[w], recv_sem=recv.at[w],
                                         device_id=(x, y, 1 - c), device_id_type=MESH).wait_recv()
        for cp in cps:
            cp.wait_send()

    dma = lambda k: pltpu.SemaphoreType.DMA((k,))
    return pl.pallas_call(
        body, name="grad_join_halves",
        out_shape=tuple(jax.ShapeDtypeStruct(s.shape, s.dtype) for s in ss),
        in_specs=[_ANY] * n, out_specs=tuple([_ANY] * n), input_output_aliases={w: w for w in range(n)},
        scratch_shapes=[dma(n), dma(n)],
    )(*ss)


def _rot_cols(w, axis=-1):
    a, b = jnp.split(w, 2, axis=axis)
    return jnp.concatenate([-b, a], axis=axis)


def _rot_cols_t(g, axis=-1):
    a, b = jnp.split(g, 2, axis=axis)
    return jnp.concatenate([b, -a], axis=axis)


def _cols_from_chips(a):
    n, r, cs = a.shape
    return jnp.transpose(a, (1, 0, 2)).reshape(r, n * cs)


def _cols_to_chips(a):
    r, cc = a.shape
    return jnp.transpose(a.reshape(r, N_CHIPS, cc // N_CHIPS), (1, 0, 2))


def _conv_w_split(cw):
    return jnp.swapaxes(cw.reshape(3, 2, D_FF), 0, 1)


def _conv_w_join(g):
    return jnp.swapaxes(g, 0, 1).reshape(3, 2 * D_FF)


_SEG =(D_MODEL, 2 * D_MODEL, 2 * D_MODEL + Q_RANK, 2 * D_MODEL + Q_RANK + KV_RANK, 2 * D_MODEL + Q_RANK + KV_RANK + ROPE,
        3 * D_MODEL + Q_RANK + KV_RANK + ROPE)


def _w_in_t_to_pad(wt):
    u, v, cq, ckv, kr, ga, gb = jnp.split(wt, _SEG, axis=0)
    return jnp.concatenate([u, v, ga, gb, cq, ckv, kr, _rot_cols(kr, axis=0)], axis=0)


def _w_in_t_from_pad(gt):
    u, v, ga, gb, cq, ckv, kr, krr = jnp.split(
        gt, (D_MODEL, 2 * D_MODEL, 3 * D_MODEL, 4 * D_MODEL, 4 * D_MODEL + Q_RANK, 4 * D_MODEL + Q_RANK + KV_RANK,
             4 * D_MODEL + Q_RANK + KV_RANK + ROPE), axis=0)
    return jnp.concatenate([u, v, cq, ckv, kr + _rot_cols_t(krr, axis=0), ga, gb], axis=0)


def _w_uq_to_pad(w):
    t = w.reshape(Q_RANK, HEADS, QK_DIM)
    nope, rope = t[..., :NOPE], t[..., NOPE:]
    return jnp.concatenate([nope, rope, _rot_cols(rope)], axis=-1).reshape(Q_RANK, HEADS * HEAD_PAD)


def _w_uq_from_pad(g):
    t = g.reshape(Q_RANK, HEADS, HEAD_PAD)
    nope, rope, rot = t[..., :NOPE], t[..., NOPE:QK_DIM], t[..., QK_DIM:]
    return jnp.concatenate([nope, rope + _rot_cols_t(rot)], axis=-1).reshape(Q_RANK, HEADS * QK_DIM)


def _w_ukv_to_pad(w):
    t = w.reshape(KV_RANK, HEADS, 2, NOPE)
    return jnp.swapaxes(t, 1, 2).reshape(KV_RANK, 2 * HEADS * NOPE)


def _w_ukv_from_pad(g):
    t = g.reshape(KV_RANK, 2, HEADS, NOPE)
    return jnp.swapaxes(t, 1, 2).reshape(KV_RANK, 2 * HEADS * NOPE)


def _rope_tables(positions):
    inv_freq = 1.0 / (ROPE_THETA ** (jnp.arange(0, ROPE, 2, dtype=F32) / ROPE))
    ang = positions.astype(F32).reshape(-1, 1) * inv_freq
    cos, sin = jnp.cos(ang), jnp.sin(ang)
    zero = jnp.zeros((ang.shape[0], 64), F32)
    return jnp.concatenate([cos, cos, zero], axis=1), jnp.concatenate([sin, sin, zero], axis=1)


_BIG = ("w_in", "w_uq", "w_ukv", "w_out", "w_up", "w_down")
UP_SHARD = 2 * D_FF // N_CHIPS


def _local_step(x, positions, tgt, wts, mixer_weights, ffn_weights, on_ffn_grads, on_mixer_grads):
    B, S, D = x.shape
    T = B * S
    xf = x.reshape(T, D)
    cos_a, sin_a = _rope_tables(positions)
    bs_t = jnp.pad(wts["a_spatial_b"].T, ((0, 0), (0, 128 - A_GROUPS)))

    h = _rms_fwd(xf, wts["mix_norm"], "norm1_fwd")
    z = _mm(h, wts["w_in"], "nt", "in_proj", tm=512, tn=1536, tk=D, n_outer=True)
    wts = dict(wts)
    wts["w_q"], wts["w_kv"], wts["w_out"] = mixer_weights(z)
    q, k, v, cqn, ckvn = _lat_fwd(z, wts["q_a_norm"], wts["kv_a_norm"], wts["w_q"], wts["w_kv"], cos_a, sin_a)
    yb, *lses = _attn_fwd(q, k, v, B, S)
    merged = _mix_fwd(z, yb, wts["a_v_norm_g"], wts["a_v_norm_b"], wts["a_spatial_w"], bs_t)
    x1 = _mm(merged, wts["w_out"], "nn", "out_proj", tm=512, tn=D, tk=D, add=xf)
    h2 = _rms_fwd(x1, wts["ffn_norm"], "norm2_fwd")
    wts["w_up"], wts["w_down"], wts["conv_w"] = ffn_weights(h2)
    up_pre = _mm(h2, wts["w_up"], "nn", "up_proj", tm=512, tn=UP_SHARD, tk=D, dims=(T, 2 * D_FF, D),
                 b_spec=pl.BlockSpec((None, D, UP_SHARD), lambda i, j, k: (j, 0, 0)),
                 o_spec=pl.BlockSpec((None, 512, UP_SHARD), lambda i, j, k: (j // 2, i, j % 2)), out_shape=(2, T, D_FF),
                 n_outer=True)
    act = _gate_fwd(up_pre, wts["conv_w"], wts["conv_b"], B, S)
    x2 = _mm(act, wts["w_down"], "nn", "down_proj", tm=512, tn=D, tk=1408, add=x1)
    dx2, loss_row, g_final = _final(x2, tgt.reshape(T, D), wts["final_norm"])

    g = {"final_norm": g_final}
    dact = _mm(dx2, wts["w_down"], "nt", "down_proj_dx", tm=512, tn=1408, tk=D, n_outer=True)
    tk2, tk1 = min(2048, T), min(1024, T)
    g["w_down"], g["w_down_lo"] = _mm(act, dx2, "tn", "down_proj_dw", tm=1408, tn=D, tk=tk1, copy_dtype=GRAD_PAYLOAD)
    dup, g["conv_w"], g["conv_b"] = _gate_bwd(up_pre, dact, wts["conv_w"], wts["conv_b"], B, S)
    g["w_up"], g["w_up_lo"] = _mm(
        h2, dup, "tn", "up_proj_dw", tm=D, tn=UP_SHARD, tk=tk2, dims=(D, 2 * D_FF, T), copy_dtype=GRAD_PAYLOAD,
        b_spec=pl.BlockSpec((None, tk2, UP_SHARD), lambda i, j, k: (j // 2, k, j % 2)),
        o_spec=pl.BlockSpec((None, D, UP_SHARD), lambda i, j, k: (j, 0, 0)), out_shape=(N_CHIPS, D, UP_SHARD))
    token, ffn_sent = on_ffn_grads(g)
    dh2 = _mm(dup, wts["w_up"], "nt", "up_proj_dx", tm=512, tn=D, tk=UP_SHARD, dims=(T, D, 2 * D_FF), after=token,
              a_spec=pl.BlockSpec((None, 512, UP_SHARD), lambda i, j, k: (k // 2, i, k % 2)),
              b_spec=pl.BlockSpec((None, D, UP_SHARD), lambda i, j, k: (k, 0, 0)))
    token = ffn_sent(dh2)
    dx1, g["ffn_norm"] = _rms_bwd(x1, wts["ffn_norm"], dh2, dx2, "norm2_bwd")
    dm = _mm(dx1, wts["w_out"], "nt", "out_proj_dx", tm=512, tn=D, tk=D, after=token)
    g["w_out"], g["w_out_lo"] = _mm(merged, dx1, "tn", "out_proj_dw", tm=D, tn=D, tk=tk1, copy_dtype=GRAD_PAYLOAD)
    dz, dyb, dl, g["a_spatial_w"], gbs, g["a_v_norm_g"], g["a_v_norm_b"] = _mix_bwd(
        z, yb, dm, wts["a_v_norm_g"], wts["a_v_norm_b"], wts["a_spatial_w"], bs_t)
    g["a_spatial_b"] = gbs[:, :A_GROUPS].T
    delta = dl.reshape(HEADS * T // ATT_BLOCK, 1, ATT_BLOCK)
    dq, dk, dv = _attn_bwd(q, k, v, dyb, lses, delta, B, S)
    dz, dq_raw, dkv, g["q_a_norm"], g["kv_a_norm"] = _lat_bwd(
        dz, z, dq, dk, dv, wts["q_a_norm"], wts["kv_a_norm"], wts["w_q"], wts["w_kv"], cos_a, sin_a)
    g["w_q"], g["w_q_lo"] = _mm(cqn, dq_raw, "tn", "q_proj_dw", tm=Q_RANK, tn=HEADS * HEAD_PAD, tk=tk2,
                                copy_dtype=GRAD_PAYLOAD)
    g["w_kv"], g["w_kv_lo"] = _mm(ckvn, dkv, "tn", "kv_proj_dw", tm=KV_RANK, tn=2 * HEADS * NOPE, tk=tk2,
                                  copy_dtype=GRAD_PAYLOAD)
    g["w_in"] = _mm(dz, h, "tn", "in_proj_dw", tm=1536, tn=D, tk=tk2)
    token = on_mixer_grads(g)
    dh = _mm(dz, wts["w_in"], "nn", "in_proj_dx", tm=512, tn=D, tk=1536, after=token)
    dx, g["mix_norm"] = _rms_bwd(xf, wts["mix_norm"], dh, dx1, "norm1_bwd")
    return loss_row[0, 0], dx.reshape(B, S, D), g


_SMALL = (("mix_norm", (1, D_MODEL)), ("a_v_norm_g", (1, D_MODEL)), ("a_v_norm_b", (1, D_MODEL)),
          ("a_spatial_w", (A_GROUPS * CHUNK, CHUNK)), ("a_spatial_b", (1, A_GROUPS * CHUNK)), ("q_a_norm", (1, Q_RANK)),
          ("kv_a_norm", (1, KV_RANK)), ("ffn_norm", (1, D_MODEL)), ("conv_b", (1, 2 * D_FF)), ("final_norm", (1, D_MODEL)),
          ("conv_w", (3, 2 * D_FF)))
_SMALL_SIZE = sum(math.prod(s) for _, s in _SMALL)
_SMALL_ROWS = -(-(_SMALL_SIZE + 1) // (128 * 8)) * 8


def kernel(x, positions, mix_norm, w_in, a_v_norm_g, a_v_norm_b, a_spatial_w, a_spatial_b, q_a_norm, w_uq, kv_a_norm, w_ukv, w_out, ffn_norm, w_up, conv_w, conv_b, w_down, final_norm, loss_target, m_mix_norm, m_w_in, m_a_v_norm_g, m_a_v_norm_b, m_a_spatial_w, m_a_spatial_b, m_q_a_norm, m_w_uq, m_kv_a_norm, m_w_ukv, m_w_out, m_ffn_norm, m_w_up, m_conv_w, m_conv_b, m_w_down, m_final_norm, v_mix_norm, v_w_in, v_a_v_norm_g, v_a_v_norm_b, v_a_spatial_w, v_a_spatial_b, v_q_a_norm, v_w_uq, v_kv_a_norm, v_w_ukv, v_w_out, v_ffn_norm, v_w_up, v_conv_w, v_conv_b, v_w_down, v_final_norm):
    weights = dict(mix_norm=mix_norm, w_in=w_in, a_v_norm_g=a_v_norm_g, a_v_norm_b=a_v_norm_b, a_spatial_w=a_spatial_w,
                   a_spatial_b=a_spatial_b, q_a_norm=q_a_norm, w_uq=w_uq, kv_a_norm=kv_a_norm, w_ukv=w_ukv, w_out=w_out,
                   ffn_norm=ffn_norm, w_up=w_up, conv_w=conv_w, conv_b=conv_b, w_down=w_down, final_norm=final_norm)
    m_in = dict(mix_norm=m_mix_norm, w_in=m_w_in, a_v_norm_g=m_a_v_norm_g, a_v_norm_b=m_a_v_norm_b,
                a_spatial_w=m_a_spatial_w, a_spatial_b=m_a_spatial_b, q_a_norm=m_q_a_norm, w_uq=m_w_uq,
                kv_a_norm=m_kv_a_norm, w_ukv=m_w_ukv, w_out=m_w_out, ffn_norm=m_ffn_norm, w_up=m_w_up, conv_w=m_conv_w,
                conv_b=m_conv_b, w_down=m_w_down, final_norm=m_final_norm)
    v_in = dict(mix_norm=v_mix_norm, w_in=v_w_in, a_v_norm_g=v_a_v_norm_g, a_v_norm_b=v_a_v_norm_b,
                a_spatial_w=v_a_spatial_w, a_spatial_b=v_a_spatial_b, q_a_norm=v_q_a_norm, w_uq=v_w_uq,
                kv_a_norm=v_kv_a_norm, w_ukv=v_w_ukv, w_out=v_w_out, ffn_norm=v_ffn_norm, w_up=v_w_up, conv_w=v_conv_w,
                conv_b=v_conv_b, w_down=v_w_down, final_norm=v_final_norm)
    names = list(weights)
    chip = 2 * lax.axis_index("x") + lax.axis_index("y")

    def halves(a):
        return a.reshape(a.shape[:-2] + (2, a.shape[-2] // 2, a.shape[-1]))

    w_in_t = jnp.swapaxes(w_in[0], 0, 1).astype(MXU_DTYPE)
    (w_in_sh,) = _gather_weights([jnp.stack(jnp.split(w_in_t, 2, axis=1))])
    mixer_gather = _exchange_start([weights[n][0].astype(MXU_DTYPE) for n in _BIG[1:4]],
                                   "mixer_gather_start", "gather", after=w_in_sh)
    ffn_gather = _exchange_start([w_up[0].astype(MXU_DTYPE), w_down[0].astype(MXU_DTYPE), conv_w[0]],
                                 "ffn_gather_start", "gather", after=mixer_gather[4])
    wts = dict(
        mix_norm=mix_norm + ffn_gather[4][0:1, 0:1], a_v_norm_g=a_v_norm_g, a_v_norm_b=a_v_norm_b,
        a_spatial_w=a_spatial_w[0], a_spatial_b=a_spatial_b[0], q_a_norm=q_a_norm, kv_a_norm=kv_a_norm,
        ffn_norm=ffn_norm, final_norm=final_norm.reshape(1, D_MODEL),
        w_in=_w_in_t_to_pad(jnp.concatenate([w_in_sh[:, 0], w_in_sh[:, 1]], axis=-1).reshape(-1, D_MODEL)),
        conv_b=conv_b.reshape(2, 1, D_FF))

    def mixer_weights(after):
        _, (w_uq_sh, w_ukv_sh, w_out_sh) = _exchange_wait(mixer_gather, "mixer_gather_wait", "gather", after)
        return (_w_uq_to_pad(_cols_from_chips(w_uq_sh)), _w_ukv_to_pad(_cols_from_chips(w_ukv_sh)),
                w_out_sh.reshape(D_MODEL, D_MODEL))

    def ffn_weights(after):
        _, (w_up_sh, w_down_sh, cw_all) = _exchange_wait(ffn_gather, "ffn_gather_wait", "gather", after)
        return w_up_sh, w_down_sh.reshape(D_FF, D_MODEL), _conv_w_split(_cols_from_chips(cw_all))

    scatters = {}

    def start_scatter(slabs, slabs_lo, tag):
        got = _swap_halves([halves(s) for s in slabs_lo], tag + "_grad_swap_halves")
        sums = _pair_sum(slabs, got, tag + "_grad_pair_sum")
        scatters[tag] = _exchange_start(list(sums), tag + "_scatter_start", "scatter", after=slabs[-1])
        return scatters[tag][4]

    def on_ffn_grads(g):
        slabs, slabs_lo = [[g["w_up" + lo], g["w_down" + lo].reshape(N_CHIPS, D_FF // N_CHIPS, D_MODEL)]
                           for lo in ("", "_lo")]
        swap = _exchange_start([halves(s) for s in slabs_lo], "ffn_swap_start", "swap", after=slabs[1])

        def sent(after):
            _, got = _exchange_wait(swap, "ffn_swap_wait", "swap", after)
            sums = _pair_sum(slabs, got, "ffn_grad_pair_sum")
            scatters["ffn"] = _exchange_start(list(sums), "ffn_scatter_start", "scatter", after=got[0])
            return scatters["ffn"][4]

        return swap[4], sent

    def on_mixer_grads(g):
        w_in_slabs = _w_in_t_from_pad(g["w_in"]).reshape(N_CHIPS, -1, D_MODEL)
        rest = [[_cols_to_chips(_w_uq_from_pad(g["w_q" + lo])), _cols_to_chips(_w_ukv_from_pad(g["w_kv" + lo])),
                 g["w_out" + lo].reshape(N_CHIPS, D_MODEL // N_CHIPS, D_MODEL)] for lo in ("", "_lo")]
        return start_scatter([w_in_slabs] + rest[0], [w_in_slabs.astype(GRAD_PAYLOAD)] + rest[1], "mixer")

    loss_part, grad_x, g = _local_step(x, positions, loss_target, wts, mixer_weights, ffn_weights, on_ffn_grads,
                                       on_mixer_grads)

    g_small_parts = dict(g)
    g_small_parts["conv_w"] = _conv_w_join(g["conv_w"])
    g_small_parts["conv_b"] = g["conv_b"].reshape(1, 2 * D_FF)
    flat = jnp.concatenate([g_small_parts[n].reshape(-1) for n, _ in _SMALL] + [loss_part.reshape(1)])
    flat = jnp.pad(flat, (0, _SMALL_ROWS * 128 - flat.shape[0])).reshape(_SMALL_ROWS, 128)
    small_gather = _exchange_start([flat], "small_gather_start", "all", after=grad_x)

    mixer_sums, mixer_landed = _exchange_wait(scatters["mixer"], "mixer_scatter_wait", "scatter", after=small_gather[4])
    ffn_sums, ffn_landed = _exchange_wait(scatters["ffn"], "ffn_scatter_wait", "scatter", after=mixer_landed[0])
    reduced = _chip_sum(list(mixer_sums) + list(ffn_sums), list(mixer_landed) + list(ffn_landed))
    g_big = dict(zip(_BIG, _join_halves(reduced)))

    grads, deltas, new_m, new_v = {}, {}, {}, {}

    def update(n, grad):
        w = weights[n]
        shape2 = grad.shape
        d, nm, nv = _adamw(w.reshape(shape2), grad, m_in[n].reshape(shape2), v_in[n].reshape(shape2), "adamw_" + n)
        grads[n], deltas[n], new_m[n], new_v[n] = (t.reshape(w.shape) for t in (grad, d, nm, nv))

    def update_transposed(n, grad_t):
        t = lambda a: jnp.swapaxes(a, 1, 2)
        d, nm, nv = _adamw(t(weights[n]), grad_t, t(m_in[n]), t(v_in[n]), "adamw_" + n)
        grads[n], deltas[n], new_m[n], new_v[n] = t(grad_t), t(d), t(nm), t(nv)

    for n in _BIG:
        g3 = g_big[n].reshape((1, -1, g_big[n].shape[-1]))
        if n == "w_in":
            update_transposed(n, g3)
        else:
            update(n, g3)

    (own,), (everyone,) = _exchange_wait(small_gather, "small_gather_wait", "all", after=deltas["w_up"])
    device = 2 * chip + lax.axis_index("c")
    everyone = lax.dynamic_update_slice(everyone, own[None], (device, 0, 0))
    total = _sum_slabs([everyone[j] for j in range(8)], "small_grads_sum", tr=_SMALL_ROWS).reshape(-1)
    o = 0
    for n, shp in _SMALL:
        piece = total[o:o + math.prod(shp)].reshape(shp)
        o += math.prod(shp)
        if n == "conv_w":
            piece = lax.dynamic_slice_in_dim(piece, chip * UP_SHARD, UP_SHARD, axis=1)
        update(n, piece)
    loss = total[_SMALL_SIZE]
    return (loss, grad_x, *[grads[n] for n in names], *[deltas[n] for n in names], *[new_m[n] for n in names],
            *[new_v[n] for n in names])
```

```python
import functools
import math

import jax
import jax.numpy as jnp
from jax import lax
from jax.experimental import pallas as pl
from jax.experimental.pallas import tpu as pltpu

F32 = jnp.float32
MXU_DTYPE = jnp.bfloat16
MESH = pl.DeviceIdType.MESH

D_MODEL = 1024
EPS = 1e-6
A_GROUPS = 8
CHUNK = 128
HEADS = 8
NOPE = 128
ROPE = 64
QK_DIM = NOPE + ROPE
HEAD_PAD = 256
Q_RANK = 256
KV_RANK = 128
ROPE_THETA = 10000.0
D_FF = 2816
FF_TILE = 256
N_FF_TILES = D_FF // FF_TILE
LAT = 512
IN_PAD = 4 * D_MODEL + LAT
N_CHIPS = 4
ADAM_LR, ADAM_B1, ADAM_B2, ADAM_EPS, ADAM_WD, ADAM_STEP = 0.001, 0.9, 0.999, 1e-08, 0.01, 10

VMEM_CAP_V7X = 64 * 1024 * 1024
NEG = -1e30


def _params(sem, nbytes):
    limit = int(min(VMEM_CAP_V7X - (8 << 20), max(32 << 20, 3 * nbytes)))
    return pltpu.CompilerParams(dimension_semantics=sem, vmem_limit_bytes=limit)


def _nbytes(shape, dtype):
    return math.prod(shape) * jnp.dtype(dtype).itemsize


_DIMS = {"nn": (((1,), (0,)), ((), ())), "nt": (((1,), (1,)), ((), ())), "tn": (((0,), (0,)), ((), ()))}


def _mm(a, b, mode, name, *, tm, tn, tk, out_dtype=F32, add=None, dims=None, a_spec=None, b_spec=None,
        o_spec=None, out_shape=None, n_outer=False, copy_dtype=None, after=None):
    if dims is None:
        if mode == "nn":
            (M, K), (_, N) = a.shape, b.shape
        elif mode == "nt":
            (M, K), (N, _) = a.shape, b.shape
        else:
            (K, M), (_, N) = a.shape, b.shape
    else:
        M, N, K = dims
    a_blk = (tk, tm) if mode == "tn" else (tm, tk)
    b_blk = (tn, tk) if mode == "nt" else (tk, tn)
    if a_spec is None:
        a_spec = pl.BlockSpec(a_blk, (lambda i, j, k: (k, i)) if mode == "tn" else (lambda i, j, k: (i, k)))
    if b_spec is None:
        b_spec = pl.BlockSpec(b_blk, (lambda i, j, k: (j, k)) if mode == "nt" else (lambda i, j, k: (k, j)))
    if o_spec is None:
        o_spec = pl.BlockSpec((tm, tn), lambda i, j, k: (i, j))
    if out_shape is None:
        out_shape = (M, N)
    assert M % tm == 0 and N % tn == 0 and K % tk == 0, (name, M, N, K, tm, tn, tk)
    nk = K // tk
    contract = _DIMS[mode]
    has_add = add is not None

    def body(*refs):
        a_ref, b_ref = refs[0], refs[1]
        add_ref = refs[2] if has_add else None
        n_in = 2 + has_add + (after is not None)
        o_ref = refs[n_in]
        copy_ref = refs[n_in + 1] if copy_dtype is not None else None

        def product():
            return lax.dot_general(a_ref[...].astype(MXU_DTYPE), b_ref[...].astype(MXU_DTYPE), contract,
                                   preferred_element_type=F32)

        def finish(r):
            if has_add:
                r = r + add_ref[...]
            o_ref[...] = r.astype(out_dtype)
            if copy_ref is not None:
                copy_ref[...] = r.astype(copy_dtype)

        if nk == 1:
            finish(product())
            return
        acc = refs[-1]
        k = pl.program_id(2)

        @pl.when(k == 0)
        def _():
            acc[...] = jnp.zeros_like(acc)

        acc[...] += product()

        @pl.when(k == nk - 1)
        def _():
            finish(acc[...])

    in_specs = [a_spec, b_spec]
    args = [a, b]
    nbytes = _nbytes(a_blk, a.dtype) + _nbytes(b_blk, b.dtype) + 3 * _nbytes((tm, tn), F32)
    if has_add:
        in_specs.append(pl.BlockSpec((tm, tn), lambda i, j, k: (i, j)))
        args.append(add)
        nbytes += _nbytes((tm, tn), F32)
    if after is not None:
        in_specs.append(pl.BlockSpec(after.shape, lambda i, j, k: (0, 0)))
        args.append(after)
    grid = (M // tm, N // tn, nk)
    if n_outer:
        def swapped(spec):
            return pl.BlockSpec(spec.block_shape, lambda j, i, k, at=spec.index_map: at(i, j, k))

        grid = (N // tn, M // tm, nk)
        in_specs = [swapped(s) for s in in_specs]
        o_spec = swapped(o_spec)
    out_sds, out_specs = jax.ShapeDtypeStruct(out_shape, out_dtype), o_spec
    if copy_dtype is not None:
        out_sds, out_specs = (out_sds, jax.ShapeDtypeStruct(out_shape, copy_dtype)), (o_spec, o_spec)
    return pl.pallas_call(
        body, name=name, out_shape=out_sds, grid=grid, in_specs=in_specs, out_specs=out_specs,
        scratch_shapes=[pltpu.VMEM((tm, tn), F32)] if nk > 1 else [],
        compiler_params=_params(("parallel", "parallel", "arbitrary"), nbytes),
    )(*args)


_GELU_C = math.sqrt(2.0 / math.pi)
_GELU_A = 0.044715


def _sigmoid(x):
    return 0.5 * jnp.tanh(0.5 * x) + 0.5


def _gelu(x):
    t = jnp.tanh(x * (_GELU_C + (_GELU_C * _GELU_A) * (x * x)))
    return x * (0.5 + 0.5 * t)


def _gelu_and_grad(x):
    x2 = x * x
    t = jnp.tanh(x * (_GELU_C + (_GELU_C * _GELU_A) * x2))
    cdf = 0.5 + 0.5 * t
    grad = cdf + (0.5 * x) * (1.0 - t * t) * (_GELU_C + (3.0 * _GELU_C * _GELU_A) * x2)
    return x * cdf, grad


def _rope_mix(g, cos_a, sin_a):
    return g * cos_a + pltpu.roll(g, 64, 1) * sin_a


def _rope_mix_bwd(d, cos_a, sin_a):
    return d * cos_a + pltpu.roll(d * sin_a, 64, 1)


def _rms_fwd(x, g, name, tr=512):
    T, D = x.shape

    def body(x_ref, g_ref, h_ref):
        xv = x_ref[...]
        r = lax.rsqrt(jnp.mean(xv * xv, axis=-1, keepdims=True) + EPS)
        h_ref[...] = ((xv * r) * g_ref[...]).astype(h_ref.dtype)

    return pl.pallas_call(
        body, name=name, out_shape=jax.ShapeDtypeStruct((T, D), MXU_DTYPE), grid=(T // tr,),
        in_specs=[pl.BlockSpec((tr, D), lambda i: (i, 0)), pl.BlockSpec((1, D), lambda i: (0, 0))],
        out_specs=pl.BlockSpec((tr, D), lambda i: (i, 0)),
        compiler_params=_params(("parallel",), 3 * _nbytes((tr, D), F32)),
    )(x, g)


def _rms_bwd(x, g, dh, dres, name, tr=512):
    T, D = x.shape

    def body(x_ref, g_ref, dh_ref, dres_ref, dx_ref, gg_ref):
        @pl.when(pl.program_id(0) == 0)
        def _():
            gg_ref[...] = jnp.zeros_like(gg_ref)

        xv = x_ref[...]
        r = lax.rsqrt(jnp.mean(xv * xv, axis=-1, keepdims=True) + EPS)
        xn = xv * r
        dhv = dh_ref[...]
        dxn = dhv * g_ref[...]
        dx_ref[...] = dres_ref[...] + r * (dxn - xn * jnp.mean(dxn * xn, axis=-1, keepdims=True))
        gg_ref[...] += jnp.sum(dhv * xn, axis=0, keepdims=True)

    row = pl.BlockSpec((tr, D), lambda i: (i, 0))
    vec = pl.BlockSpec((1, D), lambda i: (0, 0))
    return pl.pallas_call(
        body, name=name,
        out_shape=(jax.ShapeDtypeStruct((T, D), F32), jax.ShapeDtypeStruct((1, D), F32)),
        grid=(T // tr,), in_specs=[row, vec, row, row], out_specs=(row, vec),
        compiler_params=_params(("arbitrary",), 6 * _nbytes((tr, D), F32)),
    )(x, g, dh, dres)


def _lat_fwd(z, gq, gkv, wq, wkv, cos_a, sin_a, tr=256):
    T = z.shape[0]
    lat_blk = (4 * D_MODEL) // LAT

    def body(z_ref, gq_ref, gkv_ref, wq_ref, wkv_ref, cos_ref, sin_ref, q_ref, k_ref, v_ref, cqn_ref, ckvn_ref):
        zl = z_ref[...]
        cos_v, sin_v = cos_ref[...], sin_ref[...]
        cq = zl[:, :Q_RANK]
        ckv = zl[:, Q_RANK:Q_RANK + KV_RANK]
        krb = zl[:, Q_RANK + KV_RANK:]
        cqn = ((cq * lax.rsqrt(jnp.mean(cq * cq, axis=-1, keepdims=True) + EPS)) * gq_ref[...]).astype(MXU_DTYPE)
        ckvn = ((ckv * lax.rsqrt(jnp.mean(ckv * ckv, axis=-1, keepdims=True) + EPS)) * gkv_ref[...]).astype(MXU_DTYPE)
        cqn_ref[...] = cqn
        ckvn_ref[...] = ckvn
        krr = _rope_mix(krb, cos_v, sin_v).astype(MXU_DTYPE)
        q = jnp.dot(cqn, wq_ref[...], preferred_element_type=F32)
        kv = jnp.dot(ckvn, wkv_ref[...], preferred_element_type=F32)
        for h in range(HEADS):
            o = h * HEAD_PAD
            q_ref[:, o:o + NOPE] = q[:, o:o + NOPE].astype(MXU_DTYPE)
            q_ref[:, o + NOPE:o + HEAD_PAD] = _rope_mix(q[:, o + NOPE:o + HEAD_PAD], cos_v, sin_v).astype(MXU_DTYPE)
            k_ref[:, o:o + NOPE] = kv[:, h * NOPE:(h + 1) * NOPE].astype(MXU_DTYPE)
            k_ref[:, o + NOPE:o + HEAD_PAD] = krr
        v_ref[...] = kv[:, HEADS * NOPE:].astype(MXU_DTYPE)

    def row(w):
        return pl.BlockSpec((tr, w), lambda i: (i, 0))

    def full(a):
        return pl.BlockSpec(a.shape, lambda i: (0, 0))

    return pl.pallas_call(
        body, name="lat_fwd",
        out_shape=(jax.ShapeDtypeStruct((T, HEADS * HEAD_PAD), MXU_DTYPE), jax.ShapeDtypeStruct((T, HEADS * HEAD_PAD), MXU_DTYPE),
                   jax.ShapeDtypeStruct((T, HEADS * NOPE), MXU_DTYPE), jax.ShapeDtypeStruct((T, Q_RANK), MXU_DTYPE),
                   jax.ShapeDtypeStruct((T, KV_RANK), MXU_DTYPE)),
        grid=(T // tr,),
        in_specs=[pl.BlockSpec((tr, LAT), lambda i: (i, lat_blk)), full(gq), full(gkv), full(wq), full(wkv), row(128), row(128)],
        out_specs=(row(HEADS * HEAD_PAD), row(HEADS * HEAD_PAD), row(HEADS * NOPE), row(Q_RANK), row(KV_RANK)),
        compiler_params=_params(("parallel",), 8 * _nbytes((tr, HEADS * HEAD_PAD), F32)),
    )(z, gq, gkv, wq, wkv, cos_a, sin_a)


ATT_BLOCK = 256
_SCALE = QK_DIM ** -0.5


def _causal_mask(n):
    return lax.broadcasted_iota(jnp.int32, (n, n), 1) <= lax.broadcasted_iota(jnp.int32, (n, n), 0)


def _causal_mask_t(n):
    return lax.broadcasted_iota(jnp.int32, (n, n), 0) <= lax.broadcasted_iota(jnp.int32, (n, n), 1)


ATT_HEADS = 4


def _attn_fwd(q, k, v, B, S):
    tq = ATT_BLOCK
    nq = S // tq
    T = B * S
    hp, groups = ATT_HEADS, HEADS // ATT_HEADS

    def body(q_ref, k_ref, v_ref, o_ref, *lse_refs):
        qi = pl.program_id(2)
        qs = [q_ref[:, t * HEAD_PAD:(t + 1) * HEAD_PAD] for t in range(hp)]

        def scores(j, t):
            rows = pl.ds(pl.multiple_of(j * tq, tq), tq)
            return lax.dot_general(k_ref[rows, t * HEAD_PAD:(t + 1) * HEAD_PAD], qs[t], _DIMS["nt"],
                                   preferred_element_type=F32)

        def step(j, carry, last):
            rows = pl.ds(pl.multiple_of(j * tq, tq), tq)
            out = []
            for t in range(hp):
                m, l, acc, st = carry[t]
                st_next = st if last else scores(j + 1, t)
                st = st * _SCALE
                if last:
                    st = jnp.where(_causal_mask_t(tq), st, NEG)
                m_new = jnp.maximum(m, jnp.max(st, axis=0, keepdims=True))
                alpha = jnp.exp(m - m_new)
                p = jnp.exp(st - m_new)
                l = alpha * l + jnp.sum(p, axis=0, keepdims=True)
                acc = alpha * acc + lax.dot_general(v_ref[rows, t * NOPE:(t + 1) * NOPE], p.astype(MXU_DTYPE),
                                                    _DIMS["tn"], preferred_element_type=F32)
                out.append((m_new, l, acc, st_next))
            return tuple(out)

        init = tuple((jnp.full((1, tq), NEG, F32), jnp.zeros((1, tq), F32), jnp.zeros((NOPE, tq), F32), scores(0, t))
                     for t in range(hp))
        carry = lax.fori_loop(0, qi, lambda j, c: step(j, c, False), init)
        carry = step(qi, carry, True)
        for t in range(hp):
            m, l, acc, _ = carry[t]
            o_ref[:, t * NOPE:(t + 1) * NOPE] = (acc / l).T
            lse_refs[t][0] = m + jnp.log(l)

    lse_sds = jax.ShapeDtypeStruct((groups * B * nq, 1, tq), F32)
    lse_spec = pl.BlockSpec((1, 1, tq), lambda b, h, i: ((h * B + b) * nq + i, 0, 0))
    return pl.pallas_call(
        body, name="attn_fwd",
        out_shape=(jax.ShapeDtypeStruct((T, HEADS * NOPE), F32),) + (lse_sds,) * hp,
        grid=(B, groups, nq),
        in_specs=[pl.BlockSpec((tq, hp * HEAD_PAD), lambda b, h, i: (b * nq + i, h)),
                  pl.BlockSpec((S, hp * HEAD_PAD), lambda b, h, i: (b, h)),
                  pl.BlockSpec((S, hp * NOPE), lambda b, h, i: (b, h))],
        out_specs=(pl.BlockSpec((tq, hp * NOPE), lambda b, h, i: (b * nq + i, h)),) + (lse_spec,) * hp,
        compiler_params=_params(("parallel", "parallel", "arbitrary"), 4 * hp * _nbytes((S, HEAD_PAD), MXU_DTYPE)),
    )(q, k, v)


def _attn_bwd(q, k, v, do, lses, delta, B, S):
    tq = ATT_BLOCK
    nq = S // tq
    T = B * S
    hp, groups = ATT_HEADS, HEADS // ATT_HEADS

    def body(q_ref, k_ref, v_ref, do_ref, *refs):
        lse_refs, dl_refs = refs[:hp], refs[hp:2 * hp]
        dq_out, dk_ref, dv_ref, dq_ref = refs[2 * hp:]
        kj = pl.program_id(2)

        @pl.when(kj == 0)
        def _():
            dq_ref[...] = jnp.zeros_like(dq_ref)

        def products(i, t):
            rows = pl.ds(pl.multiple_of(i * tq, tq), tq)
            st = lax.dot_general(k_ref[:, t * HEAD_PAD:(t + 1) * HEAD_PAD], q_ref[rows, t * HEAD_PAD:(t + 1) * HEAD_PAD],
                                 _DIMS["nt"], preferred_element_type=F32)
            dpt = lax.dot_general(v_ref[:, t * NOPE:(t + 1) * NOPE], do_ref[rows, t * NOPE:(t + 1) * NOPE],
                                  _DIMS["nt"], preferred_element_type=F32)
            return st, dpt

        def step(i, carry, masked):
            rows = pl.ds(pl.multiple_of(i * tq, tq), tq)
            nxt = jnp.minimum(i + 1, nq - 1)
            out = []
            for t in range(hp):
                dk, dv, st, dpt = carry[t]
                st_next, dpt_next = products(nxt, t)
                qk_cols = slice(t * HEAD_PAD, (t + 1) * HEAD_PAD)
                v_cols = slice(t * NOPE, (t + 1) * NOPE)
                p = jnp.exp(st * _SCALE - lse_refs[t][i])
                if masked:
                    p = jnp.where(_causal_mask_t(tq), p, 0.0)
                dv = dv + jnp.dot(p.astype(MXU_DTYPE), do_ref[rows, v_cols], preferred_element_type=F32)
                ds = (p * (dpt - dl_refs[t][i]) * _SCALE).astype(MXU_DTYPE)
                dk = dk + jnp.dot(ds, q_ref[rows, qk_cols], preferred_element_type=F32)
                dq_ref[rows, qk_cols] += lax.dot_general(ds, k_ref[:, qk_cols], _DIMS["tn"], preferred_element_type=F32)
                out.append((dk, dv, st_next, dpt_next))
            return tuple(out)

        init = tuple((jnp.zeros((tq, HEAD_PAD), F32), jnp.zeros((tq, NOPE), F32)) + products(kj, t) for t in range(hp))
        carry = step(kj, init, True)
        carry = lax.fori_loop(kj + 1, nq, lambda i, c: step(i, c, False), carry)
        for t in range(hp):
            dk_ref[:, t * HEAD_PAD:(t + 1) * HEAD_PAD] = carry[t][0].astype(dk_ref.dtype)
            dv_ref[:, t * NOPE:(t + 1) * NOPE] = carry[t][1].astype(dv_ref.dtype)

        @pl.when(kj == nq - 1)
        def _():
            dq_out[...] = dq_ref[...].astype(dq_out.dtype)

    seq = lambda w: pl.BlockSpec((S, w), lambda b, h, j: (b, h))
    blk = lambda w: pl.BlockSpec((tq, w), lambda b, h, j: (b * nq + j, h))
    lse_spec = pl.BlockSpec((nq, 1, tq), lambda b, h, j: (h * B + b, 0, 0))
    dl_specs = [pl.BlockSpec((nq, 1, tq), lambda b, h, j, t=t: ((h * hp + t) * B + b, 0, 0)) for t in range(hp)]
    return pl.pallas_call(
        body, name="attn_bwd",
        out_shape=(jax.ShapeDtypeStruct((T, HEADS * HEAD_PAD), MXU_DTYPE), jax.ShapeDtypeStruct((T, HEADS * HEAD_PAD), MXU_DTYPE),
                   jax.ShapeDtypeStruct((T, HEADS * NOPE), MXU_DTYPE)),
        grid=(B, groups, nq),
        in_specs=[seq(hp * HEAD_PAD), blk(hp * HEAD_PAD), blk(hp * NOPE), seq(hp * NOPE)] + [lse_spec] * hp + dl_specs,
        out_specs=(seq(hp * HEAD_PAD), blk(hp * HEAD_PAD), blk(hp * NOPE)),
        scratch_shapes=[pltpu.VMEM((S, hp * HEAD_PAD), F32)],
        compiler_params=_params(("parallel", "parallel", "arbitrary"), 8 * hp * _nbytes((S, HEAD_PAD), F32)),
    )(q, k, v, do, *lses, *([delta] * hp))


MIX_ROWS = 256


def _tril_weights(ws_ref, g):
    return jnp.where(_causal_mask(CHUNK), ws_ref[g], 0.0).astype(MXU_DTYPE)


def _layer_norm_stats(va):
    mu = jnp.mean(va, axis=-1, keepdims=True)
    xc = va - mu
    rs = lax.rsqrt(jnp.mean(xc * xc, axis=-1, keepdims=True) + EPS)
    return xc * rs


def _mix_specs(tr):
    zcol = lambda c: pl.BlockSpec((tr, D_MODEL), lambda i, c=c: (i, c))
    row = pl.BlockSpec((tr, D_MODEL), lambda i: (i, 0))
    vec = pl.BlockSpec((1, D_MODEL), lambda i: (0, 0))
    ws = pl.BlockSpec((A_GROUPS, CHUNK, CHUNK), lambda i: (0, 0, 0))
    bs = pl.BlockSpec((CHUNK, 128), lambda i: (0, 0))
    return zcol, row, vec, ws, bs


def _mix_fwd(z, yb, ln_g, ln_b, ws, bs_t):
    T = z.shape[0]
    tr = MIX_ROWS
    zcol, row, vec, ws_spec, bs_spec = _mix_specs(tr)

    def body(zu_ref, zv_ref, zga_ref, zgb_ref, yb_ref, g_ref, b_ref, ws_ref, bs_ref, out_ref, vn_s):
        vhat = _layer_norm_stats(_gelu(zv_ref[...]))
        vn_s[...] = (vhat * g_ref[...] + b_ref[...]).astype(MXU_DTYPE)
        for g in range(A_GROUPS):
            w = _tril_weights(ws_ref, g)
            bias = bs_ref[:, g:g + 1]
            cols = slice(g * CHUNK, (g + 1) * CHUNK)
            for c in range(tr // CHUNK):
                rows = slice(c * CHUNK, (c + 1) * CHUNK)
                mixed = jnp.dot(w, vn_s[rows, cols], preferred_element_type=F32) + bias
                ya = _gelu(zu_ref[rows, cols]) * mixed
                merged = _sigmoid(zga_ref[rows, cols]) * ya + _sigmoid(zgb_ref[rows, cols]) * yb_ref[rows, cols]
                out_ref[rows, cols] = merged.astype(MXU_DTYPE)

    return pl.pallas_call(
        body, name="mix_fwd", out_shape=jax.ShapeDtypeStruct((T, D_MODEL), MXU_DTYPE), grid=(T // tr,),
        in_specs=[zcol(0), zcol(1), zcol(2), zcol(3), row, vec, vec, ws_spec, bs_spec], out_specs=row,
        scratch_shapes=[pltpu.VMEM((tr, D_MODEL), MXU_DTYPE)],
        compiler_params=_params(("parallel",), 8 * _nbytes((tr, D_MODEL), F32)),
    )(z, z, z, z, yb, ln_g, ln_b, ws, bs_t)


def _mix_bwd(z, yb, dm, ln_g, ln_b, ws, bs_t):
    T = z.shape[0]
    tr = MIX_ROWS
    zcol, row, vec, ws_spec, bs_spec = _mix_specs(tr)

    def body(zu_ref, zv_ref, zga_ref, zgb_ref, yb_ref, dm_ref, g_ref, b_ref, ws_ref, bs_ref,
             dz_ref, dyb_ref, dl_ref, gws_ref, gbs_ref, glg_ref, glb_ref, vn_s, dvn_s):
        @pl.when(pl.program_id(0) == 0)
        def _():
            gws_ref[...] = jnp.zeros_like(gws_ref)
            gbs_ref[...] = jnp.zeros_like(gbs_ref)
            glg_ref[...] = jnp.zeros_like(glg_ref)
            glb_ref[...] = jnp.zeros_like(glb_ref)

        lane = lax.broadcasted_iota(jnp.int32, (CHUNK, 128), 1)
        va, dgelu_v = _gelu_and_grad(zv_ref[...])
        mu = jnp.mean(va, axis=-1, keepdims=True)
        xc = va - mu
        rs = lax.rsqrt(jnp.mean(xc * xc, axis=-1, keepdims=True) + EPS)
        vhat = xc * rs
        vn_s[...] = (vhat * g_ref[...] + b_ref[...]).astype(MXU_DTYPE)
        gbs_acc = jnp.zeros((CHUNK, 128), F32)
        for g in range(A_GROUPS):
            w = _tril_weights(ws_ref, g)
            bias = bs_ref[:, g:g + 1]
            cols = slice(g * CHUNK, (g + 1) * CHUNK)
            gw_acc = jnp.zeros((CHUNK, CHUNK), F32)
            for c in range(tr // CHUNK):
                rows = slice(c * CHUNK, (c + 1) * CHUNK)
                vn = vn_s[rows, cols]
                mixed = jnp.dot(w, vn, preferred_element_type=F32) + bias
                ua, dgelu_u = _gelu_and_grad(zu_ref[rows, cols])
                dmv = dm_ref[rows, cols]
                sa = _sigmoid(zga_ref[rows, cols])
                dya = dmv * sa
                dz_ref[rows, 2 * D_MODEL + g * CHUNK:2 * D_MODEL + (g + 1) * CHUNK] = (
                    dmv * (ua * mixed) * (sa * (1.0 - sa))).astype(dz_ref.dtype)
                dz_ref[rows, cols] = (dya * mixed * dgelu_u).astype(dz_ref.dtype)
                dmix = dya * ua
                gbs_acc = gbs_acc + jnp.where(lane == g, jnp.sum(dmix, axis=-1, keepdims=True), 0.0)
                dmix_b = dmix.astype(MXU_DTYPE)
                gw_acc = gw_acc + lax.dot_general(dmix_b, vn, _DIMS["nt"], preferred_element_type=F32)
                dvn_s[rows, cols] = lax.dot_general(w, dmix_b, _DIMS["tn"], preferred_element_type=F32)
            gws_ref[g] += jnp.where(_causal_mask(CHUNK), gw_acc, 0.0)
        gbs_ref[...] += gbs_acc

        dvn = dvn_s[...]
        glg_ref[...] += jnp.sum(dvn * vhat, axis=0, keepdims=True)
        glb_ref[...] += jnp.sum(dvn, axis=0, keepdims=True)
        dvh = dvn * g_ref[...]
        dva = rs * (dvh - jnp.mean(dvh, axis=-1, keepdims=True) - vhat * jnp.mean(dvh * vhat, axis=-1, keepdims=True))
        dz_ref[:, D_MODEL:2 * D_MODEL] = (dva * dgelu_v).astype(dz_ref.dtype)

        dmv = dm_ref[...]
        ybv = yb_ref[...]
        sb = _sigmoid(zgb_ref[...])
        dyb = dmv * sb
        dyb_ref[...] = dyb.astype(dyb_ref.dtype)
        dz_ref[:, 3 * D_MODEL:4 * D_MODEL] = (dmv * ybv * (sb * (1.0 - sb))).astype(dz_ref.dtype)
        dz_ref[:, 4 * D_MODEL:] = jnp.zeros((tr, LAT), dz_ref.dtype)
        prod = dyb * ybv
        sel = (lax.broadcasted_iota(jnp.int32, (HEADS, D_MODEL), 1) // NOPE
               == lax.broadcasted_iota(jnp.int32, (HEADS, D_MODEL), 0)).astype(jnp.bfloat16)
        hi = prod.astype(jnp.bfloat16)
        rest = prod - hi.astype(F32)
        mid = rest.astype(jnp.bfloat16)
        lo = (rest - mid.astype(F32)).astype(jnp.bfloat16)
        dl_ref[...] = (lax.dot_general(sel, hi, _DIMS["nt"], preferred_element_type=F32)
                       + lax.dot_general(sel, mid, _DIMS["nt"], preferred_element_type=F32)
                       + lax.dot_general(sel, lo, _DIMS["nt"], preferred_element_type=F32))

    return pl.pallas_call(
        body, name="mix_bwd",
        out_shape=(jax.ShapeDtypeStruct((T, IN_PAD), MXU_DTYPE), jax.ShapeDtypeStruct((T, D_MODEL), MXU_DTYPE),
                   jax.ShapeDtypeStruct((HEADS, T), F32), jax.ShapeDtypeStruct((A_GROUPS, CHUNK, CHUNK), F32),
                   jax.ShapeDtypeStruct((CHUNK, 128), F32), jax.ShapeDtypeStruct((1, D_MODEL), F32),
                   jax.ShapeDtypeStruct((1, D_MODEL), F32)),
        grid=(T // tr,),
        in_specs=[zcol(0), zcol(1), zcol(2), zcol(3), row, row, vec, vec, ws_spec, bs_spec],
        out_specs=(pl.BlockSpec((tr, IN_PAD), lambda i: (i, 0)), row, pl.BlockSpec((HEADS, tr), lambda i: (0, i)),
                   ws_spec, bs_spec, vec, vec),
        scratch_shapes=[pltpu.VMEM((tr, D_MODEL), MXU_DTYPE), pltpu.VMEM((tr, D_MODEL), F32)],
        compiler_params=_params(("arbitrary",), 12 * _nbytes((tr, D_MODEL), F32)),
    )(z, z, z, z, yb, dm, ln_g, ln_b, ws, bs_t)


def _lat_bwd(dz, z, dq, dk, dv, gq, gkv, wq, wkv, cos_a, sin_a, tr=256):
    T = z.shape[0]
    lat_blk = (4 * D_MODEL) // LAT

    def body(dz_in, z_ref, dq_ref, dk_ref, dv_ref, gq_ref, gkv_ref, wq_ref, wkv_ref, cos_ref, sin_ref,
             dz_ref, dqr_ref, dkv_ref, ggq_ref, ggkv_ref):
        del dz_in

        @pl.when(pl.program_id(0) == 0)
        def _():
            ggq_ref[...] = jnp.zeros_like(ggq_ref)
            ggkv_ref[...] = jnp.zeros_like(ggkv_ref)

        cos_v, sin_v = cos_ref[...], sin_ref[...]
        dkr = jnp.zeros((tr, 128), F32)
        for h in range(HEADS):
            o = h * HEAD_PAD
            dqr_ref[:, o:o + NOPE] = dq_ref[:, o:o + NOPE].astype(MXU_DTYPE)
            dqr_ref[:, o + NOPE:o + HEAD_PAD] = _rope_mix_bwd(dq_ref[:, o + NOPE:o + HEAD_PAD], cos_v, sin_v).astype(MXU_DTYPE)
            dkv_ref[:, h * NOPE:(h + 1) * NOPE] = dk_ref[:, o:o + NOPE].astype(MXU_DTYPE)
            dkr = dkr + _rope_mix_bwd(dk_ref[:, o + NOPE:o + HEAD_PAD], cos_v, sin_v)
        dkv_ref[:, HEADS * NOPE:] = dv_ref[...]
        dcqn = lax.dot_general(dqr_ref[...], wq_ref[...], _DIMS["nt"], preferred_element_type=F32)
        dckvn = lax.dot_general(dkv_ref[...], wkv_ref[...], _DIMS["nt"], preferred_element_type=F32)

        zl = z_ref[...]

        def rms_bwd(c, dn, g_ref, gg_ref):
            r = lax.rsqrt(jnp.mean(c * c, axis=-1, keepdims=True) + EPS)
            ch = c * r
            gg_ref[...] += jnp.sum(dn * ch, axis=0, keepdims=True)
            dch = dn * g_ref[...]
            return r * (dch - ch * jnp.mean(dch * ch, axis=-1, keepdims=True))

        dz_ref[:, :Q_RANK] = rms_bwd(zl[:, :Q_RANK], dcqn, gq_ref, ggq_ref).astype(dz_ref.dtype)
        dz_ref[:, Q_RANK:Q_RANK + KV_RANK] = rms_bwd(zl[:, Q_RANK:Q_RANK + KV_RANK], dckvn, gkv_ref, ggkv_ref).astype(dz_ref.dtype)
        dz_ref[:, Q_RANK + KV_RANK:] = dkr.astype(dz_ref.dtype)

    def row(w):
        return pl.BlockSpec((tr, w), lambda i: (i, 0))

    def full(a):
        return pl.BlockSpec(a.shape, lambda i: (0, 0))

    lat = pl.BlockSpec((tr, LAT), lambda i: (i, lat_blk))
    return pl.pallas_call(
        body, name="lat_bwd",
        out_shape=(jax.ShapeDtypeStruct(dz.shape, dz.dtype), jax.ShapeDtypeStruct((T, HEADS * HEAD_PAD), MXU_DTYPE),
                   jax.ShapeDtypeStruct((T, 2 * HEADS * NOPE), MXU_DTYPE), jax.ShapeDtypeStruct(gq.shape, F32),
                   jax.ShapeDtypeStruct(gkv.shape, F32)),
        grid=(T // tr,),
        in_specs=[pl.BlockSpec(memory_space=pl.ANY), lat, row(HEADS * HEAD_PAD), row(HEADS * HEAD_PAD), row(HEADS * NOPE),
                  full(gq), full(gkv), full(wq), full(wkv), row(128), row(128)],
        out_specs=(lat, row(HEADS * HEAD_PAD), row(2 * HEADS * NOPE), full(gq), full(gkv)),
        input_output_aliases={0: 0},
        compiler_params=_params(("arbitrary",), 8 * _nbytes((tr, HEADS * HEAD_PAD), F32)),
    )(dz, z, dq, dk, dv, gq, gkv, wq, wkv, cos_a, sin_a)


GATE_ROWS = 64
HALO = 8


def _taps(ref, half, r, first):
    C = GATE_ROWS
    if first:
        xs = jnp.concatenate([jnp.zeros((HALO, ref.shape[-1]), F32), ref[half, 0:C, :]], axis=0)
    else:
        xs = ref[half, pl.ds(pl.multiple_of(r * C - HALO, HALO), C + HALO), :]
    return xs[HALO:, :], pltpu.roll(xs, 1, 0)[HALO:, :], pltpu.roll(xs, 2, 0)[HALO:, :]


def _conv_taps(taps, cw, cb):
    x0, x1, x2 = taps
    return cb + cw[0:1, :] * x2 + cw[1:2, :] * x1 + cw[2:3, :] * x0


def _fold8(x):
    acc = x[0:8, :]
    for i in range(1, x.shape[0] // 8):
        acc = acc + x[8 * i:8 * (i + 1), :]
    return acc


def _gate_fwd(up3, conv_w, conv_b, B, S):
    T = B * S
    W = FF_TILE
    C = GATE_ROWS

    def body(up_ref, cw_ref, cb_ref, act_ref):
        def chunk(r, first):
            gate = _conv_taps(_taps(up_ref, 0, r, first), cw_ref[0], cb_ref[0])
            val = _conv_taps(_taps(up_ref, 1, r, first), cw_ref[1], cb_ref[1])
            base = 0 if first else pl.multiple_of(r * C, C)
            act_ref[pl.ds(base, C), :] = (gate * _sigmoid(gate) * val).astype(act_ref.dtype)

        chunk(0, True)

        @pl.loop(1, S // C)
        def _(r):
            chunk(r, False)

    return pl.pallas_call(
        body, name="gate_fwd", out_shape=jax.ShapeDtypeStruct((T, D_FF), MXU_DTYPE), grid=(B, N_FF_TILES),
        in_specs=[pl.BlockSpec((2, S, W), lambda b, j: (0, b, j)), pl.BlockSpec((2, 3, W), lambda b, j: (0, 0, j)),
                  pl.BlockSpec((2, 1, W), lambda b, j: (0, 0, j))],
        out_specs=pl.BlockSpec((S, W), lambda b, j: (b, j)),
        compiler_params=_params(("parallel", "parallel"), 6 * _nbytes((S, W), F32)),
    )(up3, conv_w, conv_b)


def _gate_bwd(up3, dact, conv_w, conv_b, B, S):
    T = B * S
    W = FF_TILE
    C = GATE_ROWS

    def body(up_ref, da_ref, cw_ref, cb_ref, dup_ref, gcw_ref, gcb_ref, d_s):
        @pl.when(pl.program_id(1) == 0)
        def _():
            gcw_ref[...] = jnp.zeros_like(gcw_ref)
            gcb_ref[...] = jnp.zeros_like(gcb_ref)

        def chunk(r, first, sums):
            rows = pl.ds(0 if first else pl.multiple_of(r * C, C), C)
            taps = [_taps(up_ref, half, r, first) for half in (0, 1)]
            gate = _conv_taps(taps[0], cw_ref[0], cb_ref[0])
            val = _conv_taps(taps[1], cw_ref[1], cb_ref[1])
            sg = _sigmoid(gate)
            da = da_ref[rows, :]
            d_halves = (da * val * (sg * (1.0 + gate * (1.0 - sg))), da * (gate * sg))
            out = []
            for half, dup in enumerate(d_halves):
                d_s[half, rows, :] = dup
                x0, x1, x2 = taps[half]
                sb, s0, s1, s2 = sums[half]
                out.append((sb + _fold8(dup), s0 + _fold8(dup * x2), s1 + _fold8(dup * x1), s2 + _fold8(dup * x0)))
            return tuple(out)

        zeros = tuple(tuple(jnp.zeros((8, W), F32) for _ in range(4)) for _ in range(2))
        sums = chunk(0, True, zeros)
        sums = lax.fori_loop(1, S // C, lambda r, s: chunk(r, False, s), sums)
        for half in (0, 1):
            sb, s0, s1, s2 = sums[half]
            gcb_ref[half] += jnp.sum(sb, axis=0, keepdims=True)
            gcw_ref[half, 0:1, :] += jnp.sum(s0, axis=0, keepdims=True)
            gcw_ref[half, 1:2, :] += jnp.sum(s1, axis=0, keepdims=True)
            gcw_ref[half, 2:3, :] += jnp.sum(s2, axis=0, keepdims=True)

        d_s[:, S:S + HALO, :] = jnp.zeros((2, HALO, W), F32)

        @pl.loop(0, S // C)
        def _(r):
            base = pl.multiple_of(r * C, C)
            for half in (0, 1):
                ds_ = d_s[half, pl.ds(base, C + HALO), :]
                cw = cw_ref[half]
                dx = (cw[2:3, :] * ds_[:C, :] + cw[1:2, :] * pltpu.roll(ds_, C + HALO - 1, 0)[:C, :]
                      + cw[0:1, :] * pltpu.roll(ds_, C + HALO - 2, 0)[:C, :])
                dup_ref[half, pl.ds(base, C), :] = dx.astype(dup_ref.dtype)

    up_spec = pl.BlockSpec((2, S, W), lambda j, b: (0, b, j))
    cw_spec = pl.BlockSpec((2, 3, W), lambda j, b: (0, 0, j))
    cb_spec = pl.BlockSpec((2, 1, W), lambda j, b: (0, 0, j))
    return pl.pallas_call(
        body, name="gate_bwd",
        out_shape=(jax.ShapeDtypeStruct((2, T, D_FF), MXU_DTYPE), jax.ShapeDtypeStruct((2, 3, D_FF), F32),
                   jax.ShapeDtypeStruct((2, 1, D_FF), F32)),
        grid=(N_FF_TILES, B),
        in_specs=[up_spec, pl.BlockSpec((S, W), lambda j, b: (b, j)), cw_spec, cb_spec],
        out_specs=(up_spec, cw_spec, cb_spec),
        scratch_shapes=[pltpu.VMEM((2, S + HALO, W), F32)],
        compiler_params=_params(("parallel", "arbitrary"), 10 * _nbytes((S, W), F32)),
    )(up3, dact, conv_w, conv_b)


def _final(x2, tgt, g, tr=512):
    T, D = x2.shape

    def body(x_ref, t_ref, g_ref, dx_ref, loss_ref, gg_ref):
        @pl.when(pl.program_id(0) == 0)
        def _():
            loss_ref[...] = jnp.zeros_like(loss_ref)
            gg_ref[...] = jnp.zeros_like(gg_ref)

        xv = x_ref[...]
        gv = g_ref[...]
        r = lax.rsqrt(jnp.mean(xv * xv, axis=-1, keepdims=True) + EPS)
        xn = xv * r
        err = xn * gv - t_ref[...]
        loss_ref[...] += 0.5 * jnp.sum(jnp.mean(err * err, axis=-1, keepdims=True), axis=0, keepdims=True)
        dy = err * (1.0 / D)
        gg_ref[...] += jnp.sum(dy * xn, axis=0, keepdims=True)
        dxn = dy * gv
        dx_ref[...] = r * (dxn - xn * jnp.mean(dxn * xn, axis=-1, keepdims=True))

    row = pl.BlockSpec((tr, D), lambda i: (i, 0))
    vec = pl.BlockSpec((1, D), lambda i: (0, 0))
    return pl.pallas_call(
        body, name="final_loss",
        out_shape=(jax.ShapeDtypeStruct((T, D), F32), jax.ShapeDtypeStruct((1, 128), F32), jax.ShapeDtypeStruct((1, D), F32)),
        grid=(T // tr,), in_specs=[row, row, vec],
        out_specs=(row, pl.BlockSpec((1, 128), lambda i: (0, 0)), vec),
        compiler_params=_params(("arbitrary",), 6 * _nbytes((tr, D), F32)),
    )(x2, tgt, g)


def _sum_slabs(parts, name, tr):
    rows, cols = parts[0].shape
    n = len(parts)

    def body(*refs):
        acc = refs[0][...]
        for r in refs[1:n]:
            acc = acc + r[...]
        refs[n][...] = acc

    blk = pl.BlockSpec((tr, cols), lambda i: (i, 0))
    return pl.pallas_call(
        body, name=name, out_shape=jax.ShapeDtypeStruct((rows, cols), F32), grid=(rows // tr,),
        in_specs=[blk] * n, out_specs=blk,
        compiler_params=_params(("parallel",), (n + 1) * _nbytes((tr, cols), F32)),
    )(*parts)


ADAMW_BLOCK_BYTES = 2400 * 1024


def _adamw(w, g, m, v, name):
    lead = w.ndim == 3
    rows, cols = w.shape[-2:]
    fits = [d for d in range(8, rows + 1, 8) if rows % d == 0 and d * cols * 4 <= ADAMW_BLOCK_BYTES]
    tr = max(fits) if fits else rows
    c1 = 1.0 - ADAM_B1 ** ADAM_STEP
    c2 = 1.0 - ADAM_B2 ** ADAM_STEP

    def body(w_ref, g_ref, m_ref, v_ref, d_ref, nm_ref, nv_ref):
        gv = g_ref[...]
        nm = ADAM_B1 * m_ref[...] + (1.0 - ADAM_B1) * gv
        nv = ADAM_B2 * v_ref[...] + (1.0 - ADAM_B2) * (gv * gv)
        nm_ref[...] = nm
        nv_ref[...] = nv
        d_ref[...] = -ADAM_LR * ((nm / c1) / (jnp.sqrt(nv / c2) + ADAM_EPS) + ADAM_WD * w_ref[...])

    blk = pl.BlockSpec((None, tr, cols), lambda i: (0, i, 0)) if lead else pl.BlockSpec((tr, cols), lambda i: (i, 0))
    sds = jax.ShapeDtypeStruct(w.shape, F32)
    return pl.pallas_call(
        body, name=name, out_shape=(sds, sds, sds), grid=(rows // tr,), in_specs=[blk] * 4, out_specs=(blk, blk, blk),
        compiler_params=_params(("parallel",), 7 * _nbytes((tr, cols), F32)),
    )(w, g, m, v)


_ANY = pl.BlockSpec(memory_space=pl.ANY)


def _place():
    x, y, c = lax.axis_index("x"), lax.axis_index("y"), lax.axis_index("c")
    chips = [(1 - x, y), (x, 1 - y), (1 - x, 1 - y)]
    return x, y, c, chips


def _forward_halves(lands):
    n = len(lands)

    def body(*refs):
        outs, send, recv = refs[n:2 * n], refs[2 * n], refs[2 * n + 1]
        x, y, c, chips = _place()
        cps = []
        for w in range(n):
            for j, (px, py) in enumerate(chips):
                landed = outs[w].at[2 * px + py, c]
                cps.append(pltpu.make_async_remote_copy(
                    src_ref=landed, dst_ref=landed, send_sem=send.at[3 * w + j], recv_sem=recv.at[3 * w + j],
                    device_id=(x, y, 1 - c), device_id_type=MESH))
        for cp in cps:
            cp.start()
        for w in range(n):
            for j, (px, py) in enumerate(chips):
                other = outs[w].at[2 * px + py, 1 - c]
                pltpu.make_async_remote_copy(src_ref=other, dst_ref=other, send_sem=send.at[3 * w + j],
                                             recv_sem=recv.at[3 * w + j], device_id=(x, y, 1 - c),
                                             device_id_type=MESH).wait_recv()
        for cp in cps:
            cp.wait_send()

    dma = lambda k: pltpu.SemaphoreType.DMA((k,))
    return pl.pallas_call(
        body, name="gather_forward_halves", out_shape=tuple(jax.ShapeDtypeStruct(a.shape, a.dtype) for a in lands),
        in_specs=[_ANY] * n, out_specs=tuple([_ANY] * n), input_output_aliases={w: w for w in range(n)},
        scratch_shapes=[dma(3 * n), dma(3 * n)],
    )(*lands)


_HBM = pl.BlockSpec(memory_space=pltpu.HBM)
_SEM = pl.BlockSpec(memory_space=pltpu.SEMAPHORE)
_EFFECT = pltpu.SideEffectType.DATAFLOW_SIDE_EFFECTING


SEMS_PER_ARRAY = 8


def _exchange_copies(srcs, lands, send, recv, mode):
    x, y, c, chips = _place()
    if mode == "halves":
        cps = []
        for w, (src, land) in enumerate(zip(srcs, lands)):
            pieces = [(src.at[c], land.at[2 * x + y, c], (px, py, c)) for px, py in chips]
            pieces.append((src, land.at[2 * x + y], (x, y, 1 - c)))
            for k, (piece, dst, peer) in enumerate(pieces):
                cps.append(pltpu.make_async_remote_copy(
                    src_ref=piece, dst_ref=dst, send_sem=send.at[SEMS_PER_ARRAY * w + k],
                    recv_sem=recv.at[SEMS_PER_ARRAY * w + k], device_id=peer, device_id_type=MESH))
        return cps
    if mode == "swap":
        return [pltpu.make_async_remote_copy(
            src_ref=src.at[:, 1 - c], dst_ref=land, send_sem=send.at[SEMS_PER_ARRAY * w],
            recv_sem=recv.at[SEMS_PER_ARRAY * w], device_id=(x, y, 1 - c), device_id_type=MESH)
            for w, (src, land) in enumerate(zip(srcs, lands))]
    if mode == "all":
        flips = [(fx, fy, fc) for fx in (0, 1) for fy in (0, 1) for fc in (0, 1)][1:]
        peers = [(x ^ fx, y ^ fy, c ^ fc) for fx, fy, fc in flips]
        slot = 4 * x + 2 * y + c
    else:
        peers = [(px, py, c) for px, py in chips] + ([(x, y, 1 - c)] if mode == "gather" else [])
        slot = 2 * x + y
    cps = []
    for w, (src, land) in enumerate(zip(srcs, lands)):
        for k, peer in enumerate(peers):
            piece = src.at[2 * peer[0] + peer[1]] if mode == "scatter" else src
            cps.append(pltpu.make_async_remote_copy(
                src_ref=piece, dst_ref=land.at[slot], send_sem=send.at[SEMS_PER_ARRAY * w + k],
                recv_sem=recv.at[SEMS_PER_ARRAY * w + k], device_id=peer, device_id_type=MESH))
    return cps


def _exchange_start(srcs, name, mode, after):
    n = len(srcs)
    if mode == "swap":
        land_shapes = [(s.shape[0],) + s.shape[2:] for s in srcs]
    else:
        lead = {"gather": (N_CHIPS,), "halves": (N_CHIPS,), "scatter": (), "all": (2 * N_CHIPS,)}[mode]
        land_shapes = [lead + s.shape for s in srcs]

    def body(*refs):
        src_refs, land_refs = refs[:n], refs[n:2 * n]
        send, recv = refs[2 * n + 1], refs[2 * n + 2]
        token = refs[-1]
        for cp in _exchange_copies(src_refs, land_refs, send, recv, mode):
            cp.start()
        token[...] = jnp.zeros_like(token)

    sems = pltpu.SemaphoreType.DMA((SEMS_PER_ARRAY * n,))
    out = pl.pallas_call(
        body, name=name,
        out_shape=(sems, sems, *[pltpu.HBM(s.shape, s.dtype) for s in srcs],
                   *[pltpu.HBM(shp, s.dtype) for shp, s in zip(land_shapes, srcs)], jax.ShapeDtypeStruct((8, 128), F32)),
        in_specs=[_HBM] * (2 * n) + [_ANY],
        out_specs=(_SEM, _SEM, *[_HBM] * (2 * n), pl.BlockSpec(memory_space=pltpu.VMEM)),
        input_output_aliases={i: 2 + i for i in range(2 * n)},
        compiler_params=pltpu.CompilerParams(has_side_effects=_EFFECT),
    )(*[pltpu.with_memory_space_constraint(s, pltpu.HBM) for s in srcs],
      *[pltpu.with_memory_space_constraint(lax.empty(shp, s.dtype), pltpu.HBM) for shp, s in zip(land_shapes, srcs)],
      after)
    return out[0], out[1], out[2:2 + n], out[2 + n:2 + 2 * n], out[-1]


def _exchange_wait(started, name, mode, after):
    send, recv, src_thru, land_thru, _ = started
    n = len(src_thru)

    def body(*refs):
        src_refs, land_refs, send_ref, recv_ref = refs[:n], refs[n:2 * n], refs[2 * n], refs[2 * n + 1]
        for cp in _exchange_copies(src_refs, land_refs, send_ref, recv_ref, mode):
            cp.wait_send()
            cp.wait_recv()

    out = pl.pallas_call(
        body, name=name,
        out_shape=tuple(pltpu.HBM(a.shape, a.dtype) for a in list(src_thru) + list(land_thru)),
        in_specs=[_HBM] * (2 * n) + [_SEM, _SEM, _ANY], out_specs=tuple([_HBM] * (2 * n)),
        input_output_aliases={i: i for i in range(2 * n)},
        compiler_params=pltpu.CompilerParams(has_side_effects=_EFFECT),
    )(*src_thru, *land_thru, send, recv, after)
    return out[:n], out[n:]


def _swap_halves(gs, name):
    n = len(gs)

    def body(*refs):
        ins, outs, send, recv = refs[:n], refs[n:2 * n], refs[2 * n], refs[2 * n + 1]
        x, y, c, _ = _place()
        cps = []
        for w in range(n):
            cps.append(pltpu.make_async_remote_copy(
                src_ref=ins[w].at[:, 1 - c], dst_ref=outs[w], send_sem=send.at[w], recv_sem=recv.at[w],
                device_id=(x, y, 1 - c), device_id_type=MESH))
        for cp in cps:
            cp.start()
        for cp in cps:
            cp.wait()

    return pl.pallas_call(
        body, name=name,
        out_shape=tuple(jax.ShapeDtypeStruct((g.shape[0],) + g.shape[2:], g.dtype) for g in gs),
        in_specs=[_ANY] * n, out_specs=tuple([_ANY] * n),
        scratch_shapes=[pltpu.SemaphoreType.DMA((n,)), pltpu.SemaphoreType.DMA((n,))],
    )(*gs)


GRAD_PAYLOAD = jnp.bfloat16


def _half_blocks(half_rows, cols):
    if (half_rows // 2) % 16 == 0:
        return (half_rows // 2, cols), (lambda r: (r, 0))
    assert cols % 256 == 0, (half_rows, cols)
    return (half_rows, cols // 2), (lambda r: (0, r))


def _pair_sum(gs, gots, name):
    n = len(gs)
    core = lax.axis_index("c").astype(jnp.int32).reshape(1)

    def body(core_ref, *refs):
        del core_ref
        for w in range(n):
            refs[2 * n + w][...] = (refs[w][...] + refs[n + w][...]).astype(GRAD_PAYLOAD)

    in_specs, out_specs, out_shape, nbytes = [], [], [], 0
    cuts = [_half_blocks(g.shape[1] // 2, g.shape[2]) for g in gs]
    for g, ((br, bc), at) in zip(gs, cuts):
        per_half = (g.shape[1] // 2) // br
        in_specs.append(pl.BlockSpec((1, br, bc), lambda s, r, core, at=at, per_half=per_half:
                                     (s, per_half * core[0] + at(r)[0], at(r)[1])))
        nbytes += 3 * _nbytes((br, bc), F32)
    for g, ((br, bc), at) in zip(gs, cuts):
        in_specs.append(pl.BlockSpec((1, br, bc), lambda s, r, core, at=at: (s,) + at(r)))
        out_specs.append(pl.BlockSpec((1, br, bc), lambda s, r, core, at=at: (s,) + at(r)))
        out_shape.append(jax.ShapeDtypeStruct((g.shape[0], g.shape[1] // 2, g.shape[2]), GRAD_PAYLOAD))
    return pl.pallas_call(
        body, name=name, out_shape=tuple(out_shape),
        grid_spec=pltpu.PrefetchScalarGridSpec(num_scalar_prefetch=1, grid=(N_CHIPS, 2), in_specs=in_specs,
                                               out_specs=tuple(out_specs)),
        compiler_params=_params(("parallel", "parallel"), nbytes),
    )(core, *gs, *gots)


def _chip_sum(ps, landed):
    n = len(ps)
    x, y, c = lax.axis_index("x"), lax.axis_index("y"), lax.axis_index("c")
    where = jnp.stack([2 * x + y, 2 * (1 - x) + y, 2 * x + (1 - y), 2 * (1 - x) + (1 - y), c]).astype(jnp.int32)

    def body(where_ref, *refs):
        del where_ref
        for w in range(n):
            terms = [refs[4 * w + t][...].astype(F32) for t in range(4)]
            refs[4 * n + w][...] = ((terms[0] + terms[1]) + terms[2]) + terms[3]

    in_specs, out_specs, out_shape, args, nbytes = [], [], [], [], 0
    for p, a in zip(ps, landed):
        (br, bc), at = _half_blocks(a.shape[1], a.shape[2])
        blk = (1, br, bc)
        in_specs.append(pl.BlockSpec(blk, lambda r, where, at=at: (where[0],) + at(r)))
        args.append(p)
        for t in (1, 2, 3):
            in_specs.append(pl.BlockSpec(blk, lambda r, where, t=t, at=at: (where[t],) + at(r)))
            args.append(a)
        out_specs.append(pl.BlockSpec(blk, lambda r, where, at=at: (where[4],) + at(r)))
        out_shape.append(jax.ShapeDtypeStruct((2,) + a.shape[1:], F32))
        nbytes += 4 * _nbytes(blk, F32)
    return pl.pallas_call(
        body, name="grad_chip_sum", out_shape=tuple(out_shape),
        grid_spec=pltpu.PrefetchScalarGridSpec(num_scalar_prefetch=1, grid=(2,), in_specs=in_specs,
                                               out_specs=tuple(out_specs)),
        compiler_params=_params(("parallel",), nbytes),
    )(where, *args)


def _join_halves(ss):
    n = len(ss)

    def body(*refs):
        outs, send, recv = refs[n:2 * n], refs[2 * n], refs[2 * n + 1]
        x, y, c, _ = _place()
        cps = []
        for w in range(n):
            cps.append(pltpu.make_async_remote_copy(
                src_ref=outs[w].at[c], dst_ref=outs[w].at[c], send_sem=send.at[w], recv_sem=recv.at[w],
                device_id=(x, y, 1 - c), device_id_type=MESH))
        for cp in cps:
            cp.start()
        for w in range(n):
            got = outs[w].at[1 - c]
            pltpu.make_async_remote_copy(src_ref=got, dst_ref=got, send_sem=send.at[w], recv_sem=recv.at[w],
                                         device_id=(x, y, 1 - c), device_id_type=MESH).wait_recv()
        for cp in cps:
            cp.wait_send()

    dma = lambda k: pltpu.SemaphoreType.DMA((k,))
    return pl.pallas_call(
        body, name="grad_join_halves",
        out_shape=tuple(jax.ShapeDtypeStruct(s.shape, s.dtype) for s in ss),
        in_specs=[_ANY] * n, out_specs=tuple([_ANY] * n), input_output_aliases={w: w for w in range(n)},
        scratch_shapes=[dma(n), dma(n)],
    )(*ss)


def _rot_cols(w, axis=-1):
    a, b = jnp.split(w, 2, axis=axis)
    return jnp.concatenate([-b, a], axis=axis)


def _rot_cols_t(g, axis=-1):
    a, b = jnp.split(g, 2, axis=axis)
    return jnp.concatenate([b, -a], axis=axis)


def _cols_from_chips(a):
    n, r, cs = a.shape
    return jnp.transpose(a, (1, 0, 2)).reshape(r, n * cs)


def _cols_to_chips(a):
    r, cc = a.shape
    return jnp.transpose(a.reshape(r, N_CHIPS, cc // N_CHIPS), (1, 0, 2))


def _conv_w_split(cw):
    return jnp.swapaxes(cw.reshape(3, 2, D_FF), 0, 1)


def _conv_w_join(g):
    return jnp.swapaxes(g, 0, 1).reshape(3, 2 * D_FF)


_SEG =(D_MODEL, 2 * D_MODEL, 2 * D_MODEL + Q_RANK, 2 * D_MODEL + Q_RANK + KV_RANK, 2 * D_MODEL + Q_RANK + KV_RANK + ROPE,
        3 * D_MODEL + Q_RANK + KV_RANK + ROPE)


def _w_in_t_to_pad(wt):
    u, v, cq, ckv, kr, ga, gb = jnp.split(wt, _SEG, axis=0)
    return jnp.concatenate([u, v, ga, gb, cq, ckv, kr, _rot_cols(kr, axis=0)], axis=0)


def _w_in_t_from_pad(gt):
    u, v, ga, gb, cq, ckv, kr, krr = jnp.split(
        gt, (D_MODEL, 2 * D_MODEL, 3 * D_MODEL, 4 * D_MODEL, 4 * D_MODEL + Q_RANK, 4 * D_MODEL + Q_RANK + KV_RANK,
             4 * D_MODEL + Q_RANK + KV_RANK + ROPE), axis=0)
    return jnp.concatenate([u, v, cq, ckv, kr + _rot_cols_t(krr, axis=0), ga, gb], axis=0)


def _w_uq_to_pad(w):
    t = w.reshape(Q_RANK, HEADS, QK_DIM)
    nope, rope = t[..., :NOPE], t[..., NOPE:]
    return jnp.concatenate([nope, rope, _rot_cols(rope)], axis=-1).reshape(Q_RANK, HEADS * HEAD_PAD)


def _w_uq_from_pad(g):
    t = g.reshape(Q_RANK, HEADS, HEAD_PAD)
    nope, rope, rot = t[..., :NOPE], t[..., NOPE:QK_DIM], t[..., QK_DIM:]
    return jnp.concatenate([nope, rope + _rot_cols_t(rot)], axis=-1).reshape(Q_RANK, HEADS * QK_DIM)


def _w_ukv_to_pad(w):
    t = w.reshape(KV_RANK, HEADS, 2, NOPE)
    return jnp.swapaxes(t, 1, 2).reshape(KV_RANK, 2 * HEADS * NOPE)


def _w_ukv_from_pad(g):
    t = g.reshape(KV_RANK, 2, HEADS, NOPE)
    return jnp.swapaxes(t, 1, 2).reshape(KV_RANK, 2 * HEADS * NOPE)


def _rope_tables(positions):
    inv_freq = 1.0 / (ROPE_THETA ** (jnp.arange(0, ROPE, 2, dtype=F32) / ROPE))
    ang = positions.astype(F32).reshape(-1, 1) * inv_freq
    cos, sin = jnp.cos(ang), jnp.sin(ang)
    zero = jnp.zeros((ang.shape[0], 64), F32)
    return jnp.concatenate([cos, cos, zero], axis=1), jnp.concatenate([sin, sin, zero], axis=1)


_BIG = ("w_in", "w_uq", "w_ukv", "w_out", "w_up", "w_down")
UP_SHARD = 2 * D_FF // N_CHIPS


def _local_step(x, positions, tgt, wts, in_weights, mixer_weights, ffn_weights, on_ffn_grads, on_mixer_grads):
    B, S, D = x.shape
    T = B * S
    xf = x.reshape(T, D)
    cos_a, sin_a = _rope_tables(positions)
    bs_t = jnp.pad(wts["a_spatial_b"].T, ((0, 0), (0, 128 - A_GROUPS)))

    h = _rms_fwd(xf, wts["mix_norm"], "norm1_fwd")
    wts = dict(wts)
    wts["w_in"], token = in_weights(h)
    z = _mm(h, wts["w_in"], "nt", "in_proj", tm=512, tn=1536, tk=D, n_outer=True, after=token)
    wts["w_q"], wts["w_kv"], wts["w_out"] = mixer_weights(z)
    q, k, v, cqn, ckvn = _lat_fwd(z, wts["q_a_norm"], wts["kv_a_norm"], wts["w_q"], wts["w_kv"], cos_a, sin_a)
    yb, *lses = _attn_fwd(q, k, v, B, S)
    merged = _mix_fwd(z, yb, wts["a_v_norm_g"], wts["a_v_norm_b"], wts["a_spatial_w"], bs_t)
    x1 = _mm(merged, wts["w_out"], "nn", "out_proj", tm=512, tn=D, tk=D, add=xf)
    h2 = _rms_fwd(x1, wts["ffn_norm"], "norm2_fwd")
    wts["w_up"], wts["w_down"], wts["conv_w"] = ffn_weights(h2)
    up_pre = _mm(h2, wts["w_up"], "nn", "up_proj", tm=512, tn=UP_SHARD, tk=D, dims=(T, 2 * D_FF, D),
                 b_spec=pl.BlockSpec((None, D, UP_SHARD), lambda i, j, k: (j, 0, 0)),
                 o_spec=pl.BlockSpec((None, 512, UP_SHARD), lambda i, j, k: (j // 2, i, j % 2)), out_shape=(2, T, D_FF),
                 n_outer=True)
    act = _gate_fwd(up_pre, wts["conv_w"], wts["conv_b"], B, S)
    x2 = _mm(act, wts["w_down"], "nn", "down_proj", tm=512, tn=D, tk=1408, add=x1)
    dx2, loss_row, g_final = _final(x2, tgt.reshape(T, D), wts["final_norm"])

    g = {"final_norm": g_final}
    dact = _mm(dx2, wts["w_down"], "nt", "down_proj_dx", tm=512, tn=1408, tk=D, n_outer=True)
    tk2, tk1 = min(2048, T), min(1024, T)
    g["w_down"], g["w_down_lo"] = _mm(act, dx2, "tn", "down_proj_dw", tm=1408, tn=D, tk=tk1, copy_dtype=GRAD_PAYLOAD)
    dup, g["conv_w"], g["conv_b"] = _gate_bwd(up_pre, dact, wts["conv_w"], wts["conv_b"], B, S)
    g["w_up"], g["w_up_lo"] = _mm(
        h2, dup, "tn", "up_proj_dw", tm=D, tn=UP_SHARD, tk=tk2, dims=(D, 2 * D_FF, T), copy_dtype=GRAD_PAYLOAD,
        b_spec=pl.BlockSpec((None, tk2, UP_SHARD), lambda i, j, k: (j // 2, k, j % 2)),
        o_spec=pl.BlockSpec((None, D, UP_SHARD), lambda i, j, k: (j, 0, 0)), out_shape=(N_CHIPS, D, UP_SHARD))
    token, ffn_sent = on_ffn_grads(g)
    dh2 = _mm(dup, wts["w_up"], "nt", "up_proj_dx", tm=512, tn=D, tk=UP_SHARD, dims=(T, D, 2 * D_FF), after=token,
              a_spec=pl.BlockSpec((None, 512, UP_SHARD), lambda i, j, k: (k // 2, i, k % 2)),
              b_spec=pl.BlockSpec((None, D, UP_SHARD), lambda i, j, k: (k, 0, 0)))
    token = ffn_sent(dh2)
    dx1, g["ffn_norm"] = _rms_bwd(x1, wts["ffn_norm"], dh2, dx2, "norm2_bwd")
    dm = _mm(dx1, wts["w_out"], "nt", "out_proj_dx", tm=512, tn=D, tk=D, after=token)
    g["w_out"], g["w_out_lo"] = _mm(merged, dx1, "tn", "out_proj_dw", tm=D, tn=D, tk=tk1, copy_dtype=GRAD_PAYLOAD)
    dz, dyb, dl, g["a_spatial_w"], gbs, g["a_v_norm_g"], g["a_v_norm_b"] = _mix_bwd(
        z, yb, dm, wts["a_v_norm_g"], wts["a_v_norm_b"], wts["a_spatial_w"], bs_t)
    g["a_spatial_b"] = gbs[:, :A_GROUPS].T
    delta = dl.reshape(HEADS * T // ATT_BLOCK, 1, ATT_BLOCK)
    dq, dk, dv = _attn_bwd(q, k, v, dyb, lses, delta, B, S)
    dz, dq_raw, dkv, g["q_a_norm"], g["kv_a_norm"] = _lat_bwd(
        dz, z, dq, dk, dv, wts["q_a_norm"], wts["kv_a_norm"], wts["w_q"], wts["w_kv"], cos_a, sin_a)
    g["w_q"] = _mm(cqn, dq_raw, "tn", "q_proj_dw", tm=Q_RANK, tn=HEADS * HEAD_PAD, tk=tk2)
    g["w_kv"] = _mm(ckvn, dkv, "tn", "kv_proj_dw", tm=KV_RANK, tn=2 * HEADS * NOPE, tk=tk2)
    g["w_in"] = _mm(dz, h, "tn", "in_proj_dw", tm=1536, tn=D, tk=tk2)
    token = on_mixer_grads(g)
    dh = _mm(dz, wts["w_in"], "nn", "in_proj_dx", tm=512, tn=D, tk=1536, after=token)
    dx, g["mix_norm"] = _rms_bwd(xf, wts["mix_norm"], dh, dx1, "norm1_bwd")
    return loss_row[0, 0], dx.reshape(B, S, D), g


_SMALL = (("mix_norm", (1, D_MODEL)), ("a_v_norm_g", (1, D_MODEL)), ("a_v_norm_b", (1, D_MODEL)),
          ("a_spatial_w", (A_GROUPS * CHUNK, CHUNK)), ("a_spatial_b", (1, A_GROUPS * CHUNK)), ("q_a_norm", (1, Q_RANK)),
          ("kv_a_norm", (1, KV_RANK)), ("ffn_norm", (1, D_MODEL)), ("conv_b", (1, 2 * D_FF)), ("final_norm", (1, D_MODEL)),
          ("conv_w", (3, 2 * D_FF)))
_SMALL_SIZE = sum(math.prod(s) for _, s in _SMALL)
_SMALL_ROWS = -(-(_SMALL_SIZE + 1) // (128 * 8)) * 8


def kernel(x, positions, mix_norm, w_in, a_v_norm_g, a_v_norm_b, a_spatial_w, a_spatial_b, q_a_norm, w_uq, kv_a_norm, w_ukv, w_out, ffn_norm, w_up, conv_w, conv_b, w_down, final_norm, loss_target, m_mix_norm, m_w_in, m_a_v_norm_g, m_a_v_norm_b, m_a_spatial_w, m_a_spatial_b, m_q_a_norm, m_w_uq, m_kv_a_norm, m_w_ukv, m_w_out, m_ffn_norm, m_w_up, m_conv_w, m_conv_b, m_w_down, m_final_norm, v_mix_norm, v_w_in, v_a_v_norm_g, v_a_v_norm_b, v_a_spatial_w, v_a_spatial_b, v_q_a_norm, v_w_uq, v_kv_a_norm, v_w_ukv, v_w_out, v_ffn_norm, v_w_up, v_conv_w, v_conv_b, v_w_down, v_final_norm):
    weights = dict(mix_norm=mix_norm, w_in=w_in, a_v_norm_g=a_v_norm_g, a_v_norm_b=a_v_norm_b, a_spatial_w=a_spatial_w,
                   a_spatial_b=a_spatial_b, q_a_norm=q_a_norm, w_uq=w_uq, kv_a_norm=kv_a_norm, w_ukv=w_ukv, w_out=w_out,
                   ffn_norm=ffn_norm, w_up=w_up, conv_w=conv_w, conv_b=conv_b, w_down=w_down, final_norm=final_norm)
    m_in = dict(mix_norm=m_mix_norm, w_in=m_w_in, a_v_norm_g=m_a_v_norm_g, a_v_norm_b=m_a_v_norm_b,
                a_spatial_w=m_a_spatial_w, a_spatial_b=m_a_spatial_b, q_a_norm=m_q_a_norm, w_uq=m_w_uq,
                kv_a_norm=m_kv_a_norm, w_ukv=m_w_ukv, w_out=m_w_out, ffn_norm=m_ffn_norm, w_up=m_w_up, conv_w=m_conv_w,
                conv_b=m_conv_b, w_down=m_w_down, final_norm=m_final_norm)
    v_in = dict(mix_norm=v_mix_norm, w_in=v_w_in, a_v_norm_g=v_a_v_norm_g, a_v_norm_b=v_a_v_norm_b,
                a_spatial_w=v_a_spatial_w, a_spatial_b=v_a_spatial_b, q_a_norm=v_q_a_norm, w_uq=v_w_uq,
                kv_a_norm=v_kv_a_norm, w_ukv=v_w_ukv, w_out=v_w_out, ffn_norm=v_ffn_norm, w_up=v_w_up, conv_w=v_conv_w,
                conv_b=v_conv_b, w_down=v_w_down, final_norm=v_final_norm)
    names = list(weights)
    chip = 2 * lax.axis_index("x") + lax.axis_index("y")

    def halves(a):
        return a.reshape(a.shape[:-2] + (2, a.shape[-2] // 2, a.shape[-1]))

    w_in_t = jnp.swapaxes(w_in[0], 0, 1).astype(MXU_DTYPE)
    w_in_gather = _exchange_start([jnp.stack(jnp.split(w_in_t, 2, axis=1))], "w_in_gather_start", "halves",
                                  after=positions)
    gathers = {}
    wts = dict(
        mix_norm=mix_norm, a_v_norm_g=a_v_norm_g, a_v_norm_b=a_v_norm_b, a_spatial_w=a_spatial_w[0],
        a_spatial_b=a_spatial_b[0], q_a_norm=q_a_norm, kv_a_norm=kv_a_norm, ffn_norm=ffn_norm,
        final_norm=final_norm.reshape(1, D_MODEL), conv_b=conv_b.reshape(2, 1, D_FF))

    def in_weights(after):
        _, landed = _exchange_wait(w_in_gather, "w_in_gather_wait", "halves", after)
        (w_in_sh,) = _forward_halves(list(landed))
        gathers["mixer"] = _exchange_start([weights[n][0].astype(MXU_DTYPE) for n in _BIG[1:4]],
                                           "mixer_gather_start", "gather", after=w_in_sh)
        gathers["ffn"] = _exchange_start([w_up[0].astype(MXU_DTYPE), w_down[0].astype(MXU_DTYPE), conv_w[0]],
                                         "ffn_gather_start", "gather", after=gathers["mixer"][4])
        w_in_pad = _w_in_t_to_pad(jnp.concatenate([w_in_sh[:, 0], w_in_sh[:, 1]], axis=-1).reshape(-1, D_MODEL))
        return w_in_pad, gathers["ffn"][4]

    def mixer_weights(after):
        _, (w_uq_sh, w_ukv_sh, w_out_sh) = _exchange_wait(gathers["mixer"], "mixer_gather_wait", "gather", after)
        return (_w_uq_to_pad(_cols_from_chips(w_uq_sh)), _w_ukv_to_pad(_cols_from_chips(w_ukv_sh)),
                w_out_sh.reshape(D_MODEL, D_MODEL))

    def ffn_weights(after):
        _, (w_up_sh, w_down_sh, cw_all) = _exchange_wait(gathers["ffn"], "ffn_gather_wait", "gather", after)
        return w_up_sh, w_down_sh.reshape(D_FF, D_MODEL), _conv_w_split(_cols_from_chips(cw_all))

    scatters = {}

    def start_scatter(slabs, slabs_lo, tag):
        got = _swap_halves([halves(s) for s in slabs_lo], tag + "_grad_swap_halves")
        sums = _pair_sum(slabs, got, tag + "_grad_pair_sum")
        scatters[tag] = _exchange_start(list(sums), tag + "_scatter_start", "scatter", after=slabs[-1])
        return scatters[tag][4]

    def on_ffn_grads(g):
        slabs, slabs_lo = [[g["w_up" + lo], g["w_down" + lo].reshape(N_CHIPS, D_FF // N_CHIPS, D_MODEL)]
                           for lo in ("", "_lo")]
        swap = _exchange_start([halves(s) for s in slabs_lo], "ffn_swap_start", "swap", after=slabs[1])

        def sent(after):
            _, got = _exchange_wait(swap, "ffn_swap_wait", "swap", after)
            sums = _pair_sum(slabs, got, "ffn_grad_pair_sum")
            scatters["ffn"] = _exchange_start(list(sums), "ffn_scatter_start", "scatter", after=got[0])
            return scatters["ffn"][4]

        return swap[4], sent

    def on_mixer_grads(g):
        slabs = [_w_in_t_from_pad(g["w_in"]).reshape(N_CHIPS, -1, D_MODEL), _cols_to_chips(_w_uq_from_pad(g["w_q"])),
                 _cols_to_chips(_w_ukv_from_pad(g["w_kv"]))]
        w_out_slabs = [g["w_out" + lo].reshape(N_CHIPS, D_MODEL // N_CHIPS, D_MODEL) for lo in ("", "_lo")]
        return start_scatter(slabs + w_out_slabs[:1], [s.astype(GRAD_PAYLOAD) for s in slabs] + w_out_slabs[1:], "mixer")

    loss_part, grad_x, g = _local_step(x, positions, loss_target, wts, in_weights, mixer_weights, ffn_weights,
                                       on_ffn_grads, on_mixer_grads)

    g_small_parts = dict(g)
    g_small_parts["conv_w"] = _conv_w_join(g["conv_w"])
    g_small_parts["conv_b"] = g["conv_b"].reshape(1, 2 * D_FF)
    flat = jnp.concatenate([g_small_parts[n].reshape(-1) for n, _ in _SMALL] + [loss_part.reshape(1)])
    flat = jnp.pad(flat, (0, _SMALL_ROWS * 128 - flat.shape[0])).reshape(_SMALL_ROWS, 128)
    small_gather = _exchange_start([flat], "small_gather_start", "all", after=grad_x)

    mixer_sums, mixer_landed = _exchange_wait(scatters["mixer"], "mixer_scatter_wait", "scatter", after=small_gather[4])
    ffn_sums, ffn_landed = _exchange_wait(scatters["ffn"], "ffn_scatter_wait", "scatter", after=mixer_landed[0])
    reduced = _chip_sum(list(mixer_sums) + list(ffn_sums), list(mixer_landed) + list(ffn_landed))
    g_big = dict(zip(_BIG, _join_halves(reduced)))

    grads, deltas, new_m, new_v = {}, {}, {}, {}

    def update(n, grad):
        w = weights[n]
        shape2 = grad.shape
        d, nm, nv = _adamw(w.reshape(shape2), grad, m_in[n].reshape(shape2), v_in[n].reshape(shape2), "adamw_" + n)
        grads[n], deltas[n], new_m[n], new_v[n] = (t.reshape(w.shape) for t in (grad, d, nm, nv))

    def update_transposed(n, grad_t):
        t = lambda a: jnp.swapaxes(a, 1, 2)
        d, nm, nv = _adamw(t(weights[n]), grad_t, t(m_in[n]), t(v_in[n]), "adamw_" + n)
        grads[n], deltas[n], new_m[n], new_v[n] = t(grad_t), t(d), t(nm), t(nv)

    for n in _BIG:
        g3 = g_big[n].reshape((1, -1, g_big[n].shape[-1]))
        if n == "w_in":
            update_transposed(n, g3)
        else:
            update(n, g3)

    (own,), (everyone,) = _exchange_wait(small_gather, "small_gather_wait", "all", after=deltas["w_up"])
    device = 2 * chip + lax.axis_index("c")
    everyone = lax.dynamic_update_slice(everyone, own[None], (device, 0, 0))
    total = _sum_slabs([everyone[j] for j in range(8)], "small_grads_sum", tr=_SMALL_ROWS).reshape(-1)
    o = 0
    for n, shp in _SMALL:
        piece = total[o:o + math.prod(shp)].reshape(shp)
        o += math.prod(shp)
        if n == "conv_w":
            piece = lax.dynamic_slice_in_dim(piece, chip * UP_SHARD, UP_SHARD, axis=1)
        update(n, piece)
    loss = total[_SMALL_SIZE]
    return (loss, grad_x, *[grads[n] for n in names], *[deltas[n] for n in names], *[new_m[n] for n in names],
            *[new_v[n] for n in names])
```

```python
import functools
import math

import jax
import jax.numpy as jnp
from jax import lax
from jax.experimental import pallas as pl
from jax.experimental.pallas import tpu as pltpu

F32 = jnp.float32
MXU_DTYPE = jnp.bfloat16
MESH = pl.DeviceIdType.MESH

D_MODEL = 1024
EPS = 1e-6
A_GROUPS = 8
CHUNK = 128
HEADS = 8
NOPE = 128
ROPE = 64
QK_DIM = NOPE + ROPE
HEAD_PAD = 256
Q_RANK = 256
KV_RANK = 128
ROPE_THETA = 10000.0
D_FF = 2816
FF_TILE = 256
N_FF_TILES = D_FF // FF_TILE
LAT = 512
IN_PAD = 4 * D_MODEL + LAT
N_CHIPS = 4
ADAM_LR, ADAM_B1, ADAM_B2, ADAM_EPS, ADAM_WD, ADAM_STEP = 0.001, 0.9, 0.999, 1e-08, 0.01, 10

VMEM_CAP_V7X = 64 * 1024 * 1024
NEG = -1e30


def _params(sem, nbytes):
    limit = int(min(VMEM_CAP_V7X - (8 << 20), max(32 << 20, 3 * nbytes)))
    return pltpu.CompilerParams(dimension_semantics=sem, vmem_limit_bytes=limit)


def _nbytes(shape, dtype):
    return math.prod(shape) * jnp.dtype(dtype).itemsize


_DIMS = {"nn": (((1,), (0,)), ((), ())), "nt": (((1,), (1,)), ((), ())), "tn": (((0,), (0,)), ((), ()))}


def _mm(a, b, mode, name, *, tm, tn, tk, out_dtype=F32, add=None, dims=None, a_spec=None, b_spec=None,
        o_spec=None, out_shape=None, n_outer=False, copy_dtype=None, after=None):
    if dims is None:
        if mode == "nn":
            (M, K), (_, N) = a.shape, b.shape
        elif mode == "nt":
            (M, K), (N, _) = a.shape, b.shape
        else:
            (K, M), (_, N) = a.shape, b.shape
    else:
        M, N, K = dims
    a_blk = (tk, tm) if mode == "tn" else (tm, tk)
    b_blk = (tn, tk) if mode == "nt" else (tk, tn)
    if a_spec is None:
        a_spec = pl.BlockSpec(a_blk, (lambda i, j, k: (k, i)) if mode == "tn" else (lambda i, j, k: (i, k)))
    if b_spec is None:
        b_spec = pl.BlockSpec(b_blk, (lambda i, j, k: (j, k)) if mode == "nt" else (lambda i, j, k: (k, j)))
    if o_spec is None:
        o_spec = pl.BlockSpec((tm, tn), lambda i, j, k: (i, j))
    if out_shape is None:
        out_shape = (M, N)
    assert M % tm == 0 and N % tn == 0 and K % tk == 0, (name, M, N, K, tm, tn, tk)
    nk = K // tk
    contract = _DIMS[mode]
    has_add = add is not None

    def body(*refs):
        a_ref, b_ref = refs[0], refs[1]
        add_ref = refs[2] if has_add else None
        n_in = 2 + has_add + (after is not None)
        o_ref = refs[n_in]
        copy_ref = refs[n_in + 1] if copy_dtype is not None else None

        def product():
            return lax.dot_general(a_ref[...].astype(MXU_DTYPE), b_ref[...].astype(MXU_DTYPE), contract,
                                   preferred_element_type=F32)

        def finish(r):
            if has_add:
                r = r + add_ref[...]
            o_ref[...] = r.astype(out_dtype)
            if copy_ref is not None:
                copy_ref[...] = r.astype(copy_dtype)

        if nk == 1:
            finish(product())
            return
        acc = refs[-1]
        k = pl.program_id(2)

        @pl.when(k == 0)
        def _():
            acc[...] = jnp.zeros_like(acc)

        acc[...] += product()

        @pl.when(k == nk - 1)
        def _():
            finish(acc[...])

    in_specs = [a_spec, b_spec]
    args = [a, b]
    nbytes = _nbytes(a_blk, a.dtype) + _nbytes(b_blk, b.dtype) + 3 * _nbytes((tm, tn), F32)
    if has_add:
        in_specs.append(pl.BlockSpec((tm, tn), lambda i, j, k: (i, j)))
        args.append(add)
        nbytes += _nbytes((tm, tn), F32)
    if after is not None:
        in_specs.append(pl.BlockSpec(after.shape, lambda i, j, k: (0, 0)))
        args.append(after)
    grid = (M // tm, N // tn, nk)
    if n_outer:
        def swapped(spec):
            return pl.BlockSpec(spec.block_shape, lambda j, i, k, at=spec.index_map: at(i, j, k))

        grid = (N // tn, M // tm, nk)
        in_specs = [swapped(s) for s in in_specs]
        o_spec = swapped(o_spec)
    out_sds, out_specs = jax.ShapeDtypeStruct(out_shape, out_dtype), o_spec
    if copy_dtype is not None:
        out_sds, out_specs = (out_sds, jax.ShapeDtypeStruct(out_shape, copy_dtype)), (o_spec, o_spec)
    return pl.pallas_call(
        body, name=name, out_shape=out_sds, grid=grid, in_specs=in_specs, out_specs=out_specs,
        scratch_shapes=[pltpu.VMEM((tm, tn), F32)] if nk > 1 else [],
        compiler_params=_params(("parallel", "parallel", "arbitrary"), nbytes),
    )(*args)


_GELU_C = math.sqrt(2.0 / math.pi)
_GELU_A = 0.044715


def _sigmoid(x):
    return 0.5 * jnp.tanh(0.5 * x) + 0.5


def _gelu(x):
    t = jnp.tanh(x * (_GELU_C + (_GELU_C * _GELU_A) * (x * x)))
    return x * (0.5 + 0.5 * t)


def _gelu_and_grad(x):
    x2 = x * x
    t = jnp.tanh(x * (_GELU_C + (_GELU_C * _GELU_A) * x2))
    cdf = 0.5 + 0.5 * t
    grad = cdf + (0.5 * x) * (1.0 - t * t) * (_GELU_C + (3.0 * _GELU_C * _GELU_A) * x2)
    return x * cdf, grad


def _rope_mix(g, cos_a, sin_a):
    return g * cos_a + pltpu.roll(g, 64, 1) * sin_a


def _rope_mix_bwd(d, cos_a, sin_a):
    return d * cos_a + pltpu.roll(d * sin_a, 64, 1)


def _rms_fwd(x, g, name, tr=512):
    T, D = x.shape

    def body(x_ref, g_ref, h_ref):
        xv = x_ref[...]
        r = lax.rsqrt(jnp.mean(xv * xv, axis=-1, keepdims=True) + EPS)
        h_ref[...] = ((xv * r) * g_ref[...]).astype(h_ref.dtype)

    return pl.pallas_call(
        body, name=name, out_shape=jax.ShapeDtypeStruct((T, D), MXU_DTYPE), grid=(T // tr,),
        in_specs=[pl.BlockSpec((tr, D), lambda i: (i, 0)), pl.BlockSpec((1, D), lambda i: (0, 0))],
        out_specs=pl.BlockSpec((tr, D), lambda i: (i, 0)),
        compiler_params=_params(("parallel",), 3 * _nbytes((tr, D), F32)),
    )(x, g)


def _rms_bwd(x, g, dh, dres, name, tr=512):
    T, D = x.shape

    def body(x_ref, g_ref, dh_ref, dres_ref, dx_ref, gg_ref):
        @pl.when(pl.program_id(0) == 0)
        def _():
            gg_ref[...] = jnp.zeros_like(gg_ref)

        xv = x_ref[...]
        r = lax.rsqrt(jnp.mean(xv * xv, axis=-1, keepdims=True) + EPS)
        xn = xv * r
        dhv = dh_ref[...]
        dxn = dhv * g_ref[...]
        dx_ref[...] = dres_ref[...] + r * (dxn - xn * jnp.mean(dxn * xn, axis=-1, keepdims=True))
        gg_ref[...] += jnp.sum(dhv * xn, axis=0, keepdims=True)

    row = pl.BlockSpec((tr, D), lambda i: (i, 0))
    vec = pl.BlockSpec((1, D), lambda i: (0, 0))
    return pl.pallas_call(
        body, name=name,
        out_shape=(jax.ShapeDtypeStruct((T, D), F32), jax.ShapeDtypeStruct((1, D), F32)),
        grid=(T // tr,), in_specs=[row, vec, row, row], out_specs=(row, vec),
        compiler_params=_params(("arbitrary",), 6 * _nbytes((tr, D), F32)),
    )(x, g, dh, dres)


def _lat_fwd(z, gq, gkv, wq, wkv, cos_a, sin_a, tr=256):
    T = z.shape[0]
    lat_blk = (4 * D_MODEL) // LAT

    def body(z_ref, gq_ref, gkv_ref, wq_ref, wkv_ref, cos_ref, sin_ref, q_ref, k_ref, v_ref, cqn_ref, ckvn_ref):
        zl = z_ref[...]
        cos_v, sin_v = cos_ref[...], sin_ref[...]
        cq = zl[:, :Q_RANK]
        ckv = zl[:, Q_RANK:Q_RANK + KV_RANK]
        krb = zl[:, Q_RANK + KV_RANK:]
        cqn = ((cq * lax.rsqrt(jnp.mean(cq * cq, axis=-1, keepdims=True) + EPS)) * gq_ref[...]).astype(MXU_DTYPE)
        ckvn = ((ckv * lax.rsqrt(jnp.mean(ckv * ckv, axis=-1, keepdims=True) + EPS)) * gkv_ref[...]).astype(MXU_DTYPE)
        cqn_ref[...] = cqn
        ckvn_ref[...] = ckvn
        krr = _rope_mix(krb, cos_v, sin_v).astype(MXU_DTYPE)
        q = jnp.dot(cqn, wq_ref[...], preferred_element_type=F32)
        kv = jnp.dot(ckvn, wkv_ref[...], preferred_element_type=F32)
        for h in range(HEADS):
            o = h * HEAD_PAD
            q_ref[:, o:o + NOPE] = q[:, o:o + NOPE].astype(MXU_DTYPE)
            q_ref[:, o + NOPE:o + HEAD_PAD] = _rope_mix(q[:, o + NOPE:o + HEAD_PAD], cos_v, sin_v).astype(MXU_DTYPE)
            k_ref[:, o:o + NOPE] = kv[:, h * NOPE:(h + 1) * NOPE].astype(MXU_DTYPE)
            k_ref[:, o + NOPE:o + HEAD_PAD] = krr
        v_ref[...] = kv[:, HEADS * NOPE:].astype(MXU_DTYPE)

    def row(w):
        return pl.BlockSpec((tr, w), lambda i: (i, 0))

    def full(a):
        return pl.BlockSpec(a.shape, lambda i: (0, 0))

    return pl.pallas_call(
        body, name="lat_fwd",
        out_shape=(jax.ShapeDtypeStruct((T, HEADS * HEAD_PAD), MXU_DTYPE), jax.ShapeDtypeStruct((T, HEADS * HEAD_PAD), MXU_DTYPE),
                   jax.ShapeDtypeStruct((T, HEADS * NOPE), MXU_DTYPE), jax.ShapeDtypeStruct((T, Q_RANK), MXU_DTYPE),
                   jax.ShapeDtypeStruct((T, KV_RANK), MXU_DTYPE)),
        grid=(T // tr,),
        in_specs=[pl.BlockSpec((tr, LAT), lambda i: (i, lat_blk)), full(gq), full(gkv), full(wq), full(wkv), row(128), row(128)],
        out_specs=(row(HEADS * HEAD_PAD), row(HEADS * HEAD_PAD), row(HEADS * NOPE), row(Q_RANK), row(KV_RANK)),
        compiler_params=_params(("parallel",), 8 * _nbytes((tr, HEADS * HEAD_PAD), F32)),
    )(z, gq, gkv, wq, wkv, cos_a, sin_a)


ATT_BLOCK = 256
_SCALE = QK_DIM ** -0.5


def _causal_mask(n):
    return lax.broadcasted_iota(jnp.int32, (n, n), 1) <= lax.broadcasted_iota(jnp.int32, (n, n), 0)


def _causal_mask_t(n):
    return lax.broadcasted_iota(jnp.int32, (n, n), 0) <= lax.broadcasted_iota(jnp.int32, (n, n), 1)


ATT_HEADS = 4


def _attn_fwd(q, k, v, B, S):
    tq = ATT_BLOCK
    nq = S // tq
    T = B * S
    hp, groups = ATT_HEADS, HEADS // ATT_HEADS

    def body(q_ref, k_ref, v_ref, o_ref, *lse_refs):
        qi = pl.program_id(2)
        qs = [q_ref[:, t * HEAD_PAD:(t + 1) * HEAD_PAD] for t in range(hp)]

        def scores(j, t):
            rows = pl.ds(pl.multiple_of(j * tq, tq), tq)
            return lax.dot_general(k_ref[rows, t * HEAD_PAD:(t + 1) * HEAD_PAD], qs[t], _DIMS["nt"],
                                   preferred_element_type=F32)

        def step(j, carry, last):
            rows = pl.ds(pl.multiple_of(j * tq, tq), tq)
            out = []
            for t in range(hp):
                m, l, acc, st = carry[t]
                st_next = st if last else scores(j + 1, t)
                st = st * _SCALE
                if last:
                    st = jnp.where(_causal_mask_t(tq), st, NEG)
                m_new = jnp.maximum(m, jnp.max(st, axis=0, keepdims=True))
                alpha = jnp.exp(m - m_new)
                p = jnp.exp(st - m_new)
                l = alpha * l + jnp.sum(p, axis=0, keepdims=True)
                acc = alpha * acc + lax.dot_general(v_ref[rows, t * NOPE:(t + 1) * NOPE], p.astype(MXU_DTYPE),
                                                    _DIMS["tn"], preferred_element_type=F32)
                out.append((m_new, l, acc, st_next))
            return tuple(out)

        init = tuple((jnp.full((1, tq), NEG, F32), jnp.zeros((1, tq), F32), jnp.zeros((NOPE, tq), F32), scores(0, t))
                     for t in range(hp))
        carry = lax.fori_loop(0, qi, lambda j, c: step(j, c, False), init)
        carry = step(qi, carry, True)
        for t in range(hp):
            m, l, acc, _ = carry[t]
            o_ref[:, t * NOPE:(t + 1) * NOPE] = (acc / l).T
            lse_refs[t][0] = m + jnp.log(l)

    lse_sds = jax.ShapeDtypeStruct((groups * B * nq, 1, tq), F32)
    lse_spec = pl.BlockSpec((1, 1, tq), lambda b, h, i: ((h * B + b) * nq + i, 0, 0))
    return pl.pallas_call(
        body, name="attn_fwd",
        out_shape=(jax.ShapeDtypeStruct((T, HEADS * NOPE), F32),) + (lse_sds,) * hp,
        grid=(B, groups, nq),
        in_specs=[pl.BlockSpec((tq, hp * HEAD_PAD), lambda b, h, i: (b * nq + i, h)),
                  pl.BlockSpec((S, hp * HEAD_PAD), lambda b, h, i: (b, h)),
                  pl.BlockSpec((S, hp * NOPE), lambda b, h, i: (b, h))],
        out_specs=(pl.BlockSpec((tq, hp * NOPE), lambda b, h, i: (b * nq + i, h)),) + (lse_spec,) * hp,
        compiler_params=_params(("parallel", "parallel", "arbitrary"), 4 * hp * _nbytes((S, HEAD_PAD), MXU_DTYPE)),
    )(q, k, v)


def _attn_bwd(q, k, v, do, lses, delta, B, S):
    tq = ATT_BLOCK
    nq = S // tq
    T = B * S
    hp, groups = ATT_HEADS, HEADS // ATT_HEADS

    def body(q_ref, k_ref, v_ref, do_ref, *refs):
        lse_refs, dl_refs = refs[:hp], refs[hp:2 * hp]
        dq_out, dk_ref, dv_ref, dq_ref = refs[2 * hp:]
        kj = pl.program_id(2)

        @pl.when(kj == 0)
        def _():
            dq_ref[...] = jnp.zeros_like(dq_ref)

        def products(i, t):
            rows = pl.ds(pl.multiple_of(i * tq, tq), tq)
            st = lax.dot_general(k_ref[:, t * HEAD_PAD:(t + 1) * HEAD_PAD], q_ref[rows, t * HEAD_PAD:(t + 1) * HEAD_PAD],
                                 _DIMS["nt"], preferred_element_type=F32)
            dpt = lax.dot_general(v_ref[:, t * NOPE:(t + 1) * NOPE], do_ref[rows, t * NOPE:(t + 1) * NOPE],
                                  _DIMS["nt"], preferred_element_type=F32)
            return st, dpt

        def step(i, carry, masked):
            rows = pl.ds(pl.multiple_of(i * tq, tq), tq)
            nxt = jnp.minimum(i + 1, nq - 1)
            out = []
            for t in range(hp):
                dk, dv, st, dpt = carry[t]
                st_next, dpt_next = products(nxt, t)
                qk_cols = slice(t * HEAD_PAD, (t + 1) * HEAD_PAD)
                v_cols = slice(t * NOPE, (t + 1) * NOPE)
                p = jnp.exp(st * _SCALE - lse_refs[t][i])
                if masked:
                    p = jnp.where(_causal_mask_t(tq), p, 0.0)
                dv = dv + jnp.dot(p.astype(MXU_DTYPE), do_ref[rows, v_cols], preferred_element_type=F32)
                ds = (p * (dpt - dl_refs[t][i]) * _SCALE).astype(MXU_DTYPE)
                dk = dk + jnp.dot(ds, q_ref[rows, qk_cols], preferred_element_type=F32)
                dq_ref[rows, qk_cols] += lax.dot_general(ds, k_ref[:, qk_cols], _DIMS["tn"], preferred_element_type=F32)
                out.append((dk, dv, st_next, dpt_next))
            return tuple(out)

        init = tuple((jnp.zeros((tq, HEAD_PAD), F32), jnp.zeros((tq, NOPE), F32)) + products(kj, t) for t in range(hp))
        carry = step(kj, init, True)
        carry = lax.fori_loop(kj + 1, nq, lambda i, c: step(i, c, False), carry)
        for t in range(hp):
            dk_ref[:, t * HEAD_PAD:(t + 1) * HEAD_PAD] = carry[t][0].astype(dk_ref.dtype)
            dv_ref[:, t * NOPE:(t + 1) * NOPE] = carry[t][1].astype(dv_ref.dtype)

        @pl.when(kj == nq - 1)
        def _():
            dq_out[...] = dq_ref[...].astype(dq_out.dtype)

    seq = lambda w: pl.BlockSpec((S, w), lambda b, h, j: (b, h))
    blk = lambda w: pl.BlockSpec((tq, w), lambda b, h, j: (b * nq + j, h))
    lse_spec = pl.BlockSpec((nq, 1, tq), lambda b, h, j: (h * B + b, 0, 0))
    dl_specs = [pl.BlockSpec((nq, 1, tq), lambda b, h, j, t=t: ((h * hp + t) * B + b, 0, 0)) for t in range(hp)]
    return pl.pallas_call(
        body, name="attn_bwd",
        out_shape=(jax.ShapeDtypeStruct((T, HEADS * HEAD_PAD), MXU_DTYPE), jax.ShapeDtypeStruct((T, HEADS * HEAD_PAD), MXU_DTYPE),
                   jax.ShapeDtypeStruct((T, HEADS * NOPE), MXU_DTYPE)),
        grid=(B, groups, nq),
        in_specs=[seq(hp * HEAD_PAD), blk(hp * HEAD_PAD), blk(hp * NOPE), seq(hp * NOPE)] + [lse_spec] * hp + dl_specs,
        out_specs=(seq(hp * HEAD_PAD), blk(hp * HEAD_PAD), blk(hp * NOPE)),
        scratch_shapes=[pltpu.VMEM((S, hp * HEAD_PAD), F32)],
        compiler_params=_params(("parallel", "parallel", "arbitrary"), 8 * hp * _nbytes((S, HEAD_PAD), F32)),
    )(q, k, v, do, *lses, *([delta] * hp))


MIX_ROWS = 256


def _tril_weights(ws_ref, g):
    return jnp.where(_causal_mask(CHUNK), ws_ref[g], 0.0).astype(MXU_DTYPE)


def _layer_norm_stats(va):
    mu = jnp.mean(va, axis=-1, keepdims=True)
    xc = va - mu
    rs = lax.rsqrt(jnp.mean(xc * xc, axis=-1, keepdims=True) + EPS)
    return xc * rs


def _mix_specs(tr):
    zcol = lambda c: pl.BlockSpec((tr, D_MODEL), lambda i, c=c: (i, c))
    row = pl.BlockSpec((tr, D_MODEL), lambda i: (i, 0))
    vec = pl.BlockSpec((1, D_MODEL), lambda i: (0, 0))
    ws = pl.BlockSpec((A_GROUPS, CHUNK, CHUNK), lambda i: (0, 0, 0))
    bs = pl.BlockSpec((CHUNK, 128), lambda i: (0, 0))
    return zcol, row, vec, ws, bs


def _mix_fwd(z, yb, ln_g, ln_b, ws, bs_t):
    T = z.shape[0]
    tr = MIX_ROWS
    zcol, row, vec, ws_spec, bs_spec = _mix_specs(tr)

    def body(zu_ref, zv_ref, zga_ref, zgb_ref, yb_ref, g_ref, b_ref, ws_ref, bs_ref, out_ref, vn_s):
        vhat = _layer_norm_stats(_gelu(zv_ref[...]))
        vn_s[...] = (vhat * g_ref[...] + b_ref[...]).astype(MXU_DTYPE)
        for g in range(A_GROUPS):
            w = _tril_weights(ws_ref, g)
            bias = bs_ref[:, g:g + 1]
            cols = slice(g * CHUNK, (g + 1) * CHUNK)
            for c in range(tr // CHUNK):
                rows = slice(c * CHUNK, (c + 1) * CHUNK)
                mixed = jnp.dot(w, vn_s[rows, cols], preferred_element_type=F32) + bias
                ya = _gelu(zu_ref[rows, cols]) * mixed
                merged = _sigmoid(zga_ref[rows, cols]) * ya + _sigmoid(zgb_ref[rows, cols]) * yb_ref[rows, cols]
                out_ref[rows, cols] = merged.astype(MXU_DTYPE)

    return pl.pallas_call(
        body, name="mix_fwd", out_shape=jax.ShapeDtypeStruct((T, D_MODEL), MXU_DTYPE), grid=(T // tr,),
        in_specs=[zcol(0), zcol(1), zcol(2), zcol(3), row, vec, vec, ws_spec, bs_spec], out_specs=row,
        scratch_shapes=[pltpu.VMEM((tr, D_MODEL), MXU_DTYPE)],
        compiler_params=_params(("parallel",), 8 * _nbytes((tr, D_MODEL), F32)),
    )(z, z, z, z, yb, ln_g, ln_b, ws, bs_t)


def _mix_bwd(z, yb, dm, ln_g, ln_b, ws, bs_t):
    T = z.shape[0]
    tr = MIX_ROWS
    zcol, row, vec, ws_spec, bs_spec = _mix_specs(tr)

    def body(zu_ref, zv_ref, zga_ref, zgb_ref, yb_ref, dm_ref, g_ref, b_ref, ws_ref, bs_ref,
             dz_ref, dyb_ref, dl_ref, gws_ref, gbs_ref, glg_ref, glb_ref, vn_s, dvn_s):
        @pl.when(pl.program_id(0) == 0)
        def _():
            gws_ref[...] = jnp.zeros_like(gws_ref)
            gbs_ref[...] = jnp.zeros_like(gbs_ref)
            glg_ref[...] = jnp.zeros_like(glg_ref)
            glb_ref[...] = jnp.zeros_like(glb_ref)

        lane = lax.broadcasted_iota(jnp.int32, (CHUNK, 128), 1)
        va, dgelu_v = _gelu_and_grad(zv_ref[...])
        mu = jnp.mean(va, axis=-1, keepdims=True)
        xc = va - mu
        rs = lax.rsqrt(jnp.mean(xc * xc, axis=-1, keepdims=True) + EPS)
        vhat = xc * rs
        vn_s[...] = (vhat * g_ref[...] + b_ref[...]).astype(MXU_DTYPE)
        gbs_acc = jnp.zeros((CHUNK, 128), F32)
        for g in range(A_GROUPS):
            w = _tril_weights(ws_ref, g)
            bias = bs_ref[:, g:g + 1]
            cols = slice(g * CHUNK, (g + 1) * CHUNK)
            gw_acc = jnp.zeros((CHUNK, CHUNK), F32)
            for c in range(tr // CHUNK):
                rows = slice(c * CHUNK, (c + 1) * CHUNK)
                vn = vn_s[rows, cols]
                mixed = jnp.dot(w, vn, preferred_element_type=F32) + bias
                ua, dgelu_u = _gelu_and_grad(zu_ref[rows, cols])
                dmv = dm_ref[rows, cols]
                sa = _sigmoid(zga_ref[rows, cols])
                dya = dmv * sa
                dz_ref[rows, 2 * D_MODEL + g * CHUNK:2 * D_MODEL + (g + 1) * CHUNK] = (
                    dmv * (ua * mixed) * (sa * (1.0 - sa))).astype(dz_ref.dtype)
                dz_ref[rows, cols] = (dya * mixed * dgelu_u).astype(dz_ref.dtype)
                dmix = dya * ua
                gbs_acc = gbs_acc + jnp.where(lane == g, jnp.sum(dmix, axis=-1, keepdims=True), 0.0)
                dmix_b = dmix.astype(MXU_DTYPE)
                gw_acc = gw_acc + lax.dot_general(dmix_b, vn, _DIMS["nt"], preferred_element_type=F32)
                dvn_s[rows, cols] = lax.dot_general(w, dmix_b, _DIMS["tn"], preferred_element_type=F32)
            gws_ref[g] += jnp.where(_causal_mask(CHUNK), gw_acc, 0.0)
        gbs_ref[...] += gbs_acc

        dvn = dvn_s[...]
        glg_ref[...] += jnp.sum(dvn * vhat, axis=0, keepdims=True)
        glb_ref[...] += jnp.sum(dvn, axis=0, keepdims=True)
        dvh = dvn * g_ref[...]
        dva = rs * (dvh - jnp.mean(dvh, axis=-1, keepdims=True) - vhat * jnp.mean(dvh * vhat, axis=-1, keepdims=True))
        dz_ref[:, D_MODEL:2 * D_MODEL] = (dva * dgelu_v).astype(dz_ref.dtype)

        dmv = dm_ref[...]
        ybv = yb_ref[...]
        sb = _sigmoid(zgb_ref[...])
        dyb = dmv * sb
        dyb_ref[...] = dyb.astype(dyb_ref.dtype)
        dz_ref[:, 3 * D_MODEL:4 * D_MODEL] = (dmv * ybv * (sb * (1.0 - sb))).astype(dz_ref.dtype)
        dz_ref[:, 4 * D_MODEL:] = jnp.zeros((tr, LAT), dz_ref.dtype)
        prod = dyb * ybv
        sel = (lax.broadcasted_iota(jnp.int32, (HEADS, D_MODEL), 1) // NOPE
               == lax.broadcasted_iota(jnp.int32, (HEADS, D_MODEL), 0)).astype(jnp.bfloat16)
        hi = prod.astype(jnp.bfloat16)
        rest = prod - hi.astype(F32)
        mid = rest.astype(jnp.bfloat16)
        lo = (rest - mid.astype(F32)).astype(jnp.bfloat16)
        dl_ref[...] = (lax.dot_general(sel, hi, _DIMS["nt"], preferred_element_type=F32)
                       + lax.dot_general(sel, mid, _DIMS["nt"], preferred_element_type=F32)
                       + lax.dot_general(sel, lo, _DIMS["nt"], preferred_element_type=F32))

    return pl.pallas_call(
        body, name="mix_bwd",
        out_shape=(jax.ShapeDtypeStruct((T, IN_PAD), MXU_DTYPE), jax.ShapeDtypeStruct((T, D_MODEL), MXU_DTYPE),
                   jax.ShapeDtypeStruct((HEADS, T), F32), jax.ShapeDtypeStruct((A_GROUPS, CHUNK, CHUNK), F32),
                   jax.ShapeDtypeStruct((CHUNK, 128), F32), jax.ShapeDtypeStruct((1, D_MODEL), F32),
                   jax.ShapeDtypeStruct((1, D_MODEL), F32)),
        grid=(T // tr,),
        in_specs=[zcol(0), zcol(1), zcol(2), zcol(3), row, row, vec, vec, ws_spec, bs_spec],
        out_specs=(pl.BlockSpec((tr, IN_PAD), lambda i: (i, 0)), row, pl.BlockSpec((HEADS, tr), lambda i: (0, i)),
                   ws_spec, bs_spec, vec, vec),
        scratch_shapes=[pltpu.VMEM((tr, D_MODEL), MXU_DTYPE), pltpu.VMEM((tr, D_MODEL), F32)],
        compiler_params=_params(("arbitrary",), 12 * _nbytes((tr, D_MODEL), F32)),
    )(z, z, z, z, yb, dm, ln_g, ln_b, ws, bs_t)


def _lat_bwd(dz, z, dq, dk, dv, gq, gkv, wq, wkv, cos_a, sin_a, tr=256):
    T = z.shape[0]
    lat_blk = (4 * D_MODEL) // LAT

    def body(dz_in, z_ref, dq_ref, dk_ref, dv_ref, gq_ref, gkv_ref, wq_ref, wkv_ref, cos_ref, sin_ref,
             dz_ref, dqr_ref, dkv_ref, ggq_ref, ggkv_ref):
        del dz_in

        @pl.when(pl.program_id(0) == 0)
        def _():
            ggq_ref[...] = jnp.zeros_like(ggq_ref)
            ggkv_ref[...] = jnp.zeros_like(ggkv_ref)

        cos_v, sin_v = cos_ref[...], sin_ref[...]
        dkr = jnp.zeros((tr, 128), F32)
        for h in range(HEADS):
            o = h * HEAD_PAD
            dqr_ref[:, o:o + NOPE] = dq_ref[:, o:o + NOPE].astype(MXU_DTYPE)
            dqr_ref[:, o + NOPE:o + HEAD_PAD] = _rope_mix_bwd(dq_ref[:, o + NOPE:o + HEAD_PAD], cos_v, sin_v).astype(MXU_DTYPE)
            dkv_ref[:, h * NOPE:(h + 1) * NOPE] = dk_ref[:, o:o + NOPE].astype(MXU_DTYPE)
            dkr = dkr + _rope_mix_bwd(dk_ref[:, o + NOPE:o + HEAD_PAD], cos_v, sin_v)
        dkv_ref[:, HEADS * NOPE:] = dv_ref[...]
        dcqn = lax.dot_general(dqr_ref[...], wq_ref[...], _DIMS["nt"], preferred_element_type=F32)
        dckvn = lax.dot_general(dkv_ref[...], wkv_ref[...], _DIMS["nt"], preferred_element_type=F32)

        zl = z_ref[...]

        def rms_bwd(c, dn, g_ref, gg_ref):
            r = lax.rsqrt(jnp.mean(c * c, axis=-1, keepdims=True) + EPS)
            ch = c * r
            gg_ref[...] += jnp.sum(dn * ch, axis=0, keepdims=True)
            dch = dn * g_ref[...]
            return r * (dch - ch * jnp.mean(dch * ch, axis=-1, keepdims=True))

        dz_ref[:, :Q_RANK] = rms_bwd(zl[:, :Q_RANK], dcqn, gq_ref, ggq_ref).astype(dz_ref.dtype)
        dz_ref[:, Q_RANK:Q_RANK + KV_RANK] = rms_bwd(zl[:, Q_RANK:Q_RANK + KV_RANK], dckvn, gkv_ref, ggkv_ref).astype(dz_ref.dtype)
        dz_ref[:, Q_RANK + KV_RANK:] = dkr.astype(dz_ref.dtype)

    def row(w):
        return pl.BlockSpec((tr, w), lambda i: (i, 0))

    def full(a):
        return pl.BlockSpec(a.shape, lambda i: (0, 0))

    lat = pl.BlockSpec((tr, LAT), lambda i: (i, lat_blk))
    return pl.pallas_call(
        body, name="lat_bwd",
        out_shape=(jax.ShapeDtypeStruct(dz.shape, dz.dtype), jax.ShapeDtypeStruct((T, HEADS * HEAD_PAD), MXU_DTYPE),
                   jax.ShapeDtypeStruct((T, 2 * HEADS * NOPE), MXU_DTYPE), jax.ShapeDtypeStruct(gq.shape, F32),
                   jax.ShapeDtypeStruct(gkv.shape, F32)),
        grid=(T // tr,),
        in_specs=[pl.BlockSpec(memory_space=pl.ANY), lat, row(HEADS * HEAD_PAD), row(HEADS * HEAD_PAD), row(HEADS * NOPE),
                  full(gq), full(gkv), full(wq), full(wkv), row(128), row(128)],
        out_specs=(lat, row(HEADS * HEAD_PAD), row(2 * HEADS * NOPE), full(gq), full(gkv)),
        input_output_aliases={0: 0},
        compiler_params=_params(("arbitrary",), 8 * _nbytes((tr, HEADS * HEAD_PAD), F32)),
    )(dz, z, dq, dk, dv, gq, gkv, wq, wkv, cos_a, sin_a)


GATE_ROWS = 64
HALO = 8


def _taps(ref, half, r, first):
    C = GATE_ROWS
    if first:
        xs = jnp.concatenate([jnp.zeros((HALO, ref.shape[-1]), F32), ref[half, 0:C, :]], axis=0)
    else:
        xs = ref[half, pl.ds(pl.multiple_of(r * C - HALO, HALO), C + HALO), :]
    return xs[HALO:, :], pltpu.roll(xs, 1, 0)[HALO:, :], pltpu.roll(xs, 2, 0)[HALO:, :]


def _conv_taps(taps, cw, cb):
    x0, x1, x2 = taps
    return cb + cw[0:1, :] * x2 + cw[1:2, :] * x1 + cw[2:3, :] * x0


def _fold8(x):
    acc = x[0:8, :]
    for i in range(1, x.shape[0] // 8):
        acc = acc + x[8 * i:8 * (i + 1), :]
    return acc


def _gate_fwd(up3, conv_w, conv_b, B, S):
    T = B * S
    W = FF_TILE
    C = GATE_ROWS

    def body(up_ref, cw_ref, cb_ref, act_ref):
        def chunk(r, first):
            gate = _conv_taps(_taps(up_ref, 0, r, first), cw_ref[0], cb_ref[0])
            val = _conv_taps(_taps(up_ref, 1, r, first), cw_ref[1], cb_ref[1])
            base = 0 if first else pl.multiple_of(r * C, C)
            act_ref[pl.ds(base, C), :] = (gate * _sigmoid(gate) * val).astype(act_ref.dtype)

        chunk(0, True)

        @pl.loop(1, S // C)
        def _(r):
            chunk(r, False)

    return pl.pallas_call(
        body, name="gate_fwd", out_shape=jax.ShapeDtypeStruct((T, D_FF), MXU_DTYPE), grid=(B, N_FF_TILES),
        in_specs=[pl.BlockSpec((2, S, W), lambda b, j: (0, b, j)), pl.BlockSpec((2, 3, W), lambda b, j: (0, 0, j)),
                  pl.BlockSpec((2, 1, W), lambda b, j: (0, 0, j))],
        out_specs=pl.BlockSpec((S, W), lambda b, j: (b, j)),
        compiler_params=_params(("parallel", "parallel"), 6 * _nbytes((S, W), F32)),
    )(up3, conv_w, conv_b)


def _gate_bwd(up3, dact, conv_w, conv_b, B, S):
    T = B * S
    W = FF_TILE
    C = GATE_ROWS

    def body(up_ref, da_ref, cw_ref, cb_ref, dup_ref, gcw_ref, gcb_ref, d_s):
        @pl.when(pl.program_id(1) == 0)
        def _():
            gcw_ref[...] = jnp.zeros_like(gcw_ref)
            gcb_ref[...] = jnp.zeros_like(gcb_ref)

        def chunk(r, first, sums):
            rows = pl.ds(0 if first else pl.multiple_of(r * C, C), C)
            taps = [_taps(up_ref, half, r, first) for half in (0, 1)]
            gate = _conv_taps(taps[0], cw_ref[0], cb_ref[0])
            val = _conv_taps(taps[1], cw_ref[1], cb_ref[1])
            sg = _sigmoid(gate)
            da = da_ref[rows, :]
            d_halves = (da * val * (sg * (1.0 + gate * (1.0 - sg))), da * (gate * sg))
            out = []
            for half, dup in enumerate(d_halves):
                d_s[half, rows, :] = dup
                x0, x1, x2 = taps[half]
                sb, s0, s1, s2 = sums[half]
                out.append((sb + _fold8(dup), s0 + _fold8(dup * x2), s1 + _fold8(dup * x1), s2 + _fold8(dup * x0)))
            return tuple(out)

        zeros = tuple(tuple(jnp.zeros((8, W), F32) for _ in range(4)) for _ in range(2))
        sums = chunk(0, True, zeros)
        sums = lax.fori_loop(1, S // C, lambda r, s: chunk(r, False, s), sums)
        for half in (0, 1):
            sb, s0, s1, s2 = sums[half]
            gcb_ref[half] += jnp.sum(sb, axis=0, keepdims=True)
            gcw_ref[half, 0:1, :] += jnp.sum(s0, axis=0, keepdims=True)
            gcw_ref[half, 1:2, :] += jnp.sum(s1, axis=0, keepdims=True)
            gcw_ref[half, 2:3, :] += jnp.sum(s2, axis=0, keepdims=True)

        d_s[:, S:S + HALO, :] = jnp.zeros((2, HALO, W), F32)

        @pl.loop(0, S // C)
        def _(r):
            base = pl.multiple_of(r * C, C)
            for half in (0, 1):
                ds_ = d_s[half, pl.ds(base, C + HALO), :]
                cw = cw_ref[half]
                dx = (cw[2:3, :] * ds_[:C, :] + cw[1:2, :] * pltpu.roll(ds_, C + HALO - 1, 0)[:C, :]
                      + cw[0:1, :] * pltpu.roll(ds_, C + HALO - 2, 0)[:C, :])
                dup_ref[half, pl.ds(base, C), :] = dx.astype(dup_ref.dtype)

    up_spec = pl.BlockSpec((2, S, W), lambda j, b: (0, b, j))
    cw_spec = pl.BlockSpec((2, 3, W), lambda j, b: (0, 0, j))
    cb_spec = pl.BlockSpec((2, 1, W), lambda j, b: (0, 0, j))
    return pl.pallas_call(
        body, name="gate_bwd",
        out_shape=(jax.ShapeDtypeStruct((2, T, D_FF), MXU_DTYPE), jax.ShapeDtypeStruct((2, 3, D_FF), F32),
                   jax.ShapeDtypeStruct((2, 1, D_FF), F32)),
        grid=(N_FF_TILES, B),
        in_specs=[up_spec, pl.BlockSpec((S, W), lambda j, b: (b, j)), cw_spec, cb_spec],
        out_specs=(up_spec, cw_spec, cb_spec),
        scratch_shapes=[pltpu.VMEM((2, S + HALO, W), F32)],
        compiler_params=_params(("parallel", "arbitrary"), 10 * _nbytes((S, W), F32)),
    )(up3, dact, conv_w, conv_b)


def _final(x2, tgt, g, tr=512):
    T, D = x2.shape

    def body(x_ref, t_ref, g_ref, dx_ref, loss_ref, gg_ref):
        @pl.when(pl.program_id(0) == 0)
        def _():
            loss_ref[...] = jnp.zeros_like(loss_ref)
            gg_ref[...] = jnp.zeros_like(gg_ref)

        xv = x_ref[...]
        gv = g_ref[...]
        r = lax.rsqrt(jnp.mean(xv * xv, axis=-1, keepdims=True) + EPS)
        xn = xv * r
        err = xn * gv - t_ref[...]
        loss_ref[...] += 0.5 * jnp.sum(jnp.mean(err * err, axis=-1, keepdims=True), axis=0, keepdims=True)
        dy = err * (1.0 / D)
        gg_ref[...] += jnp.sum(dy * xn, axis=0, keepdims=True)
        dxn = dy * gv
        dx_ref[...] = r * (dxn - xn * jnp.mean(dxn * xn, axis=-1, keepdims=True))

    row = pl.BlockSpec((tr, D), lambda i: (i, 0))
    vec = pl.BlockSpec((1, D), lambda i: (0, 0))
    return pl.pallas_call(
        body, name="final_loss",
        out_shape=(jax.ShapeDtypeStruct((T, D), F32), jax.ShapeDtypeStruct((1, 128), F32), jax.ShapeDtypeStruct((1, D), F32)),
        grid=(T // tr,), in_specs=[row, row, vec],
        out_specs=(row, pl.BlockSpec((1, 128), lambda i: (0, 0)), vec),
        compiler_params=_params(("arbitrary",), 6 * _nbytes((tr, D), F32)),
    )(x2, tgt, g)


def _sum_slabs(parts, name, tr):
    rows, cols = parts[0].shape
    n = len(parts)

    def body(*refs):
        acc = refs[0][...]
        for r in refs[1:n]:
            acc = acc + r[...]
        refs[n][...] = acc

    blk = pl.BlockSpec((tr, cols), lambda i: (i, 0))
    return pl.pallas_call(
        body, name=name, out_shape=jax.ShapeDtypeStruct((rows, cols), F32), grid=(rows // tr,),
        in_specs=[blk] * n, out_specs=blk,
        compiler_params=_params(("parallel",), (n + 1) * _nbytes((tr, cols), F32)),
    )(*parts)


ADAMW_BLOCK_BYTES = 2400 * 1024


def _adamw(w, g, m, v, name):
    lead = w.ndim == 3
    rows, cols = w.shape[-2:]
    fits = [d for d in range(8, rows + 1, 8) if rows % d == 0 and d * cols * 4 <= ADAMW_BLOCK_BYTES]
    tr = max(fits) if fits else rows
    c1 = 1.0 - ADAM_B1 ** ADAM_STEP
    c2 = 1.0 - ADAM_B2 ** ADAM_STEP

    def body(w_ref, g_ref, m_ref, v_ref, d_ref, nm_ref, nv_ref):
        gv = g_ref[...]
        nm = ADAM_B1 * m_ref[...] + (1.0 - ADAM_B1) * gv
        nv = ADAM_B2 * v_ref[...] + (1.0 - ADAM_B2) * (gv * gv)
        nm_ref[...] = nm
        nv_ref[...] = nv
        d_ref[...] = -ADAM_LR * ((nm / c1) / (jnp.sqrt(nv / c2) + ADAM_EPS) + ADAM_WD * w_ref[...])

    blk = pl.BlockSpec((None, tr, cols), lambda i: (0, i, 0)) if lead else pl.BlockSpec((tr, cols), lambda i: (i, 0))
    sds = jax.ShapeDtypeStruct(w.shape, F32)
    return pl.pallas_call(
        body, name=name, out_shape=(sds, sds, sds), grid=(rows // tr,), in_specs=[blk] * 4, out_specs=(blk, blk, blk),
        compiler_params=_params(("parallel",), 7 * _nbytes((tr, cols), F32)),
    )(w, g, m, v)


_ANY = pl.BlockSpec(memory_space=pl.ANY)


def _place():
    x, y, c = lax.axis_index("x"), lax.axis_index("y"), lax.axis_index("c")
    chips = [(1 - x, y), (x, 1 - y), (1 - x, 1 - y)]
    return x, y, c, chips


def _forward_halves(lands):
    n = len(lands)

    def body(*refs):
        outs, send, recv = refs[n:2 * n], refs[2 * n], refs[2 * n + 1]
        x, y, c, chips = _place()
        cps = []
        for w in range(n):
            for j, (px, py) in enumerate(chips):
                landed = outs[w].at[2 * px + py, c]
                cps.append(pltpu.make_async_remote_copy(
                    src_ref=landed, dst_ref=landed, send_sem=send.at[3 * w + j], recv_sem=recv.at[3 * w + j],
                    device_id=(x, y, 1 - c), device_id_type=MESH))
        for cp in cps:
            cp.start()
        for w in range(n):
            for j, (px, py) in enumerate(chips):
                other = outs[w].at[2 * px + py, 1 - c]
                pltpu.make_async_remote_copy(src_ref=other, dst_ref=other, send_sem=send.at[3 * w + j],
                                             recv_sem=recv.at[3 * w + j], device_id=(x, y, 1 - c),
                                             device_id_type=MESH).wait_recv()
        for cp in cps:
            cp.wait_send()

    dma = lambda k: pltpu.SemaphoreType.DMA((k,))
    return pl.pallas_call(
        body, name="gather_forward_halves", out_shape=tuple(jax.ShapeDtypeStruct(a.shape, a.dtype) for a in lands),
        in_specs=[_ANY] * n, out_specs=tuple([_ANY] * n), input_output_aliases={w: w for w in range(n)},
        scratch_shapes=[dma(3 * n), dma(3 * n)],
    )(*lands)


_HBM = pl.BlockSpec(memory_space=pltpu.HBM)
_SEM = pl.BlockSpec(memory_space=pltpu.SEMAPHORE)
_EFFECT = pltpu.SideEffectType.DATAFLOW_SIDE_EFFECTING


SEMS_PER_ARRAY = 8


def _exchange_copies(srcs, lands, send, recv, mode):
    x, y, c, chips = _place()
    if mode == "halves":
        cps = []
        for w, (src, land) in enumerate(zip(srcs, lands)):
            pieces = [(src.at[c], land.at[2 * x + y, c], (px, py, c)) for px, py in chips]
            pieces.append((src, land.at[2 * x + y], (x, y, 1 - c)))
            for k, (piece, dst, peer) in enumerate(pieces):
                cps.append(pltpu.make_async_remote_copy(
                    src_ref=piece, dst_ref=dst, send_sem=send.at[SEMS_PER_ARRAY * w + k],
                    recv_sem=recv.at[SEMS_PER_ARRAY * w + k], device_id=peer, device_id_type=MESH))
        return cps
    if mode == "swap":
        return [pltpu.make_async_remote_copy(
            src_ref=src.at[:, 1 - c], dst_ref=land, send_sem=send.at[SEMS_PER_ARRAY * w],
            recv_sem=recv.at[SEMS_PER_ARRAY * w], device_id=(x, y, 1 - c), device_id_type=MESH)
            for w, (src, land) in enumerate(zip(srcs, lands))]
    if mode == "all":
        flips = [(fx, fy, fc) for fx in (0, 1) for fy in (0, 1) for fc in (0, 1)][1:]
        peers = [(x ^ fx, y ^ fy, c ^ fc) for fx, fy, fc in flips]
        slot = 4 * x + 2 * y + c
    else:
        peers = [(px, py, c) for px, py in chips] + ([(x, y, 1 - c)] if mode == "gather" else [])
        slot = 2 * x + y
    cps = []
    for w, (src, land) in enumerate(zip(srcs, lands)):
        for k, peer in enumerate(peers):
            piece = src.at[2 * peer[0] + peer[1]] if mode == "scatter" else src
            cps.append(pltpu.make_async_remote_copy(
                src_ref=piece, dst_ref=land.at[slot], send_sem=send.at[SEMS_PER_ARRAY * w + k],
                recv_sem=recv.at[SEMS_PER_ARRAY * w + k], device_id=peer, device_id_type=MESH))
    return cps


def _exchange_start(srcs, name, mode, after):
    n = len(srcs)
    if mode == "swap":
        land_shapes = [(s.shape[0],) + s.shape[2:] for s in srcs]
    else:
        lead = {"gather": (N_CHIPS,), "halves": (N_CHIPS,), "scatter": (), "all": (2 * N_CHIPS,)}[mode]
        land_shapes = [lead + s.shape for s in srcs]

    def body(*refs):
        src_refs, land_refs = refs[:n], refs[n:2 * n]
        send, recv = refs[2 * n + 1], refs[2 * n + 2]
        token = refs[-1]
        for cp in _exchange_copies(src_refs, land_refs, send, recv, mode):
            cp.start()
        token[...] = jnp.zeros_like(token)

    sems = pltpu.SemaphoreType.DMA((SEMS_PER_ARRAY * n,))
    out = pl.pallas_call(
        body, name=name,
        out_shape=(sems, sems, *[pltpu.HBM(s.shape, s.dtype) for s in srcs],
                   *[pltpu.HBM(shp, s.dtype) for shp, s in zip(land_shapes, srcs)], jax.ShapeDtypeStruct((8, 128), F32)),
        in_specs=[_HBM] * (2 * n) + [_ANY],
        out_specs=(_SEM, _SEM, *[_HBM] * (2 * n), pl.BlockSpec(memory_space=pltpu.VMEM)),
        input_output_aliases={i: 2 + i for i in range(2 * n)},
        compiler_params=pltpu.CompilerParams(has_side_effects=_EFFECT),
    )(*[pltpu.with_memory_space_constraint(s, pltpu.HBM) for s in srcs],
      *[pltpu.with_memory_space_constraint(lax.empty(shp, s.dtype), pltpu.HBM) for shp, s in zip(land_shapes, srcs)],
      after)
    return out[0], out[1], out[2:2 + n], out[2 + n:2 + 2 * n], out[-1]


def _exchange_wait(started, name, mode, after):
    send, recv, src_thru, land_thru, _ = started
    n = len(src_thru)
    after = list(after) if isinstance(after, (list, tuple)) else [after]

    def body(*refs):
        src_refs, land_refs, send_ref, recv_ref = refs[:n], refs[n:2 * n], refs[2 * n], refs[2 * n + 1]
        for cp in _exchange_copies(src_refs, land_refs, send_ref, recv_ref, mode):
            cp.wait_send()
            cp.wait_recv()

    out = pl.pallas_call(
        body, name=name,
        out_shape=tuple(pltpu.HBM(a.shape, a.dtype) for a in list(src_thru) + list(land_thru)),
        in_specs=[_HBM] * (2 * n) + [_SEM, _SEM] + [_ANY] * len(after), out_specs=tuple([_HBM] * (2 * n)),
        input_output_aliases={i: i for i in range(2 * n)},
        compiler_params=pltpu.CompilerParams(has_side_effects=_EFFECT),
    )(*src_thru, *land_thru, send, recv, *after)
    return out[:n], out[n:]


def _swap_halves(gs, name):
    n = len(gs)

    def body(*refs):
        ins, outs, send, recv = refs[:n], refs[n:2 * n], refs[2 * n], refs[2 * n + 1]
        x, y, c, _ = _place()
        cps = []
        for w in range(n):
            cps.append(pltpu.make_async_remote_copy(
                src_ref=ins[w].at[:, 1 - c], dst_ref=outs[w], send_sem=send.at[w], recv_sem=recv.at[w],
                device_id=(x, y, 1 - c), device_id_type=MESH))
        for cp in cps:
            cp.start()
        for cp in cps:
            cp.wait()

    return pl.pallas_call(
        body, name=name,
        out_shape=tuple(jax.ShapeDtypeStruct((g.shape[0],) + g.shape[2:], g.dtype) for g in gs),
        in_specs=[_ANY] * n, out_specs=tuple([_ANY] * n),
        scratch_shapes=[pltpu.SemaphoreType.DMA((n,)), pltpu.SemaphoreType.DMA((n,))],
    )(*gs)


GRAD_PAYLOAD = jnp.bfloat16


def _half_blocks(half_rows, cols):
    if (half_rows // 2) % 16 == 0:
        return (half_rows // 2, cols), (lambda r: (r, 0))
    assert cols % 256 == 0, (half_rows, cols)
    return (half_rows, cols // 2), (lambda r: (0, r))


def _pair_sum(gs, gots, name):
    n = len(gs)
    core = lax.axis_index("c").astype(jnp.int32).reshape(1)

    def body(core_ref, *refs):
        del core_ref
        for w in range(n):
            refs[2 * n + w][...] = (refs[w][...] + refs[n + w][...]).astype(GRAD_PAYLOAD)

    in_specs, out_specs, out_shape, nbytes = [], [], [], 0
    cuts = [_half_blocks(g.shape[1] // 2, g.shape[2]) for g in gs]
    for g, ((br, bc), at) in zip(gs, cuts):
        per_half = (g.shape[1] // 2) // br
        in_specs.append(pl.BlockSpec((1, br, bc), lambda s, r, core, at=at, per_half=per_half:
                                     (s, per_half * core[0] + at(r)[0], at(r)[1])))
        nbytes += 3 * _nbytes((br, bc), F32)
    for g, ((br, bc), at) in zip(gs, cuts):
        in_specs.append(pl.BlockSpec((1, br, bc), lambda s, r, core, at=at: (s,) + at(r)))
        out_specs.append(pl.BlockSpec((1, br, bc), lambda s, r, core, at=at: (s,) + at(r)))
        out_shape.append(jax.ShapeDtypeStruct((g.shape[0], g.shape[1] // 2, g.shape[2]), GRAD_PAYLOAD))
    return pl.pallas_call(
        body, name=name, out_shape=tuple(out_shape),
        grid_spec=pltpu.PrefetchScalarGridSpec(num_scalar_prefetch=1, grid=(N_CHIPS, 2), in_specs=in_specs,
                                               out_specs=tuple(out_specs)),
        compiler_params=_params(("parallel", "parallel"), nbytes),
    )(core, *gs, *gots)


def _chip_sum(ps, landed):
    n = len(ps)
    x, y, c = lax.axis_index("x"), lax.axis_index("y"), lax.axis_index("c")
    where = jnp.stack([2 * x + y, 2 * (1 - x) + y, 2 * x + (1 - y), 2 * (1 - x) + (1 - y), c]).astype(jnp.int32)

    def body(where_ref, *refs):
        del where_ref
        for w in range(n):
            terms = [refs[4 * w + t][...].astype(F32) for t in range(4)]
            refs[4 * n + w][...] = ((terms[0] + terms[1]) + terms[2]) + terms[3]

    in_specs, out_specs, out_shape, args, nbytes = [], [], [], [], 0
    for p, a in zip(ps, landed):
        (br, bc), at = _half_blocks(a.shape[1], a.shape[2])
        blk = (1, br, bc)
        in_specs.append(pl.BlockSpec(blk, lambda r, where, at=at: (where[0],) + at(r)))
        args.append(p)
        for t in (1, 2, 3):
            in_specs.append(pl.BlockSpec(blk, lambda r, where, t=t, at=at: (where[t],) + at(r)))
            args.append(a)
        out_specs.append(pl.BlockSpec(blk, lambda r, where, at=at: (where[4],) + at(r)))
        out_shape.append(jax.ShapeDtypeStruct((2,) + a.shape[1:], F32))
        nbytes += 4 * _nbytes(blk, F32)
    return pl.pallas_call(
        body, name="grad_chip_sum", out_shape=tuple(out_shape),
        grid_spec=pltpu.PrefetchScalarGridSpec(num_scalar_prefetch=1, grid=(2,), in_specs=in_specs,
                                               out_specs=tuple(out_specs)),
        compiler_params=_params(("parallel",), nbytes),
    )(where, *args)


def _join_halves(ss):
    n = len(ss)

    def body(*refs):
        outs, send, recv = refs[n:2 * n], refs[2 * n], refs[2 * n + 1]
        x, y, c, _ = _place()
        cps = []
        for w in range(n):
            cps.append(pltpu.make_async_remote_copy(
                src_ref=outs[w].at[c], dst_ref=outs[w].at[c], send_sem=send.at[w], recv_sem=recv.at[w],
                device_id=(x, y, 1 - c), device_id_type=MESH))
        for cp in cps:
            cp.start()
        for w in range(n):
            got = outs[w].at[1 - c]
            pltpu.make_async_remote_copy(src_ref=got, dst_ref=got, send_sem=send.at[w], recv_sem=recv.at[w],
                                         device_id=(x, y, 1 - c), device_id_type=MESH).wait_recv()
        for cp in cps:
            cp.wait_send()

    dma = lambda k: pltpu.SemaphoreType.DMA((k,))
    return pl.pallas_call(
        body, name="grad_join_halves",
        out_shape=tuple(jax.ShapeDtypeStruct(s.shape, s.dtype) for s in ss),
        in_specs=[_ANY] * n, out_specs=tuple([_ANY] * n), input_output_aliases={w: w for w in range(n)},
        scratch_shapes=[dma(n), dma(n)],
    )(*ss)


def _rot_cols(w, axis=-1):
    a, b = jnp.split(w, 2, axis=axis)
    return jnp.concatenate([-b, a], axis=axis)


def _rot_cols_t(g, axis=-1):
    a, b = jnp.split(g, 2, axis=axis)
    return jnp.concatenate([b, -a], axis=axis)


def _cols_from_chips(a):
    n, r, cs = a.shape
    return jnp.transpose(a, (1, 0, 2)).reshape(r, n * cs)


def _cols_to_chips(a):
    r, cc = a.shape
    return jnp.transpose(a.reshape(r, N_CHIPS, cc // N_CHIPS), (1, 0, 2))


def _conv_w_split(cw):
    return jnp.swapaxes(cw.reshape(3, 2, D_FF), 0, 1)


def _conv_w_join(g):
    return jnp.swapaxes(g, 0, 1).reshape(3, 2 * D_FF)


_SEG =(D_MODEL, 2 * D_MODEL, 2 * D_MODEL + Q_RANK, 2 * D_MODEL + Q_RANK + KV_RANK, 2 * D_MODEL + Q_RANK + KV_RANK + ROPE,
        3 * D_MODEL + Q_RANK + KV_RANK + ROPE)


def _w_in_t_to_pad(wt):
    u, v, cq, ckv, kr, ga, gb = jnp.split(wt, _SEG, axis=0)
    return jnp.concatenate([u, v, ga, gb, cq, ckv, kr, _rot_cols(kr, axis=0)], axis=0)


def _w_in_t_from_pad(gt):
    u, v, ga, gb, cq, ckv, kr, krr = jnp.split(
        gt, (D_MODEL, 2 * D_MODEL, 3 * D_MODEL, 4 * D_MODEL, 4 * D_MODEL + Q_RANK, 4 * D_MODEL + Q_RANK + KV_RANK,
             4 * D_MODEL + Q_RANK + KV_RANK + ROPE), axis=0)
    return jnp.concatenate([u, v, cq, ckv, kr + _rot_cols_t(krr, axis=0), ga, gb], axis=0)


def _w_uq_to_pad(w):
    t = w.reshape(Q_RANK, HEADS, QK_DIM)
    nope, rope = t[..., :NOPE], t[..., NOPE:]
    return jnp.concatenate([nope, rope, _rot_cols(rope)], axis=-1).reshape(Q_RANK, HEADS * HEAD_PAD)


def _w_uq_from_pad(g):
    t = g.reshape(Q_RANK, HEADS, HEAD_PAD)
    nope, rope, rot = t[..., :NOPE], t[..., NOPE:QK_DIM], t[..., QK_DIM:]
    return jnp.concatenate([nope, rope + _rot_cols_t(rot)], axis=-1).reshape(Q_RANK, HEADS * QK_DIM)


def _w_ukv_to_pad(w):
    t = w.reshape(KV_RANK, HEADS, 2, NOPE)
    return jnp.swapaxes(t, 1, 2).reshape(KV_RANK, 2 * HEADS * NOPE)


def _w_ukv_from_pad(g):
    t = g.reshape(KV_RANK, 2, HEADS, NOPE)
    return jnp.swapaxes(t, 1, 2).reshape(KV_RANK, 2 * HEADS * NOPE)


def _rope_tables(positions):
    inv_freq = 1.0 / (ROPE_THETA ** (jnp.arange(0, ROPE, 2, dtype=F32) / ROPE))
    ang = positions.astype(F32).reshape(-1, 1) * inv_freq
    cos, sin = jnp.cos(ang), jnp.sin(ang)
    zero = jnp.zeros((ang.shape[0], 64), F32)
    return jnp.concatenate([cos, cos, zero], axis=1), jnp.concatenate([sin, sin, zero], axis=1)


_BIG = ("w_in", "w_uq", "w_ukv", "w_out", "w_up", "w_down")
UP_SHARD = 2 * D_FF // N_CHIPS


def _local_step(x, positions, tgt, wts, in_weights, mixer_weights, ffn_weights, on_ffn_grads, on_mixer_grads):
    B, S, D = x.shape
    T = B * S
    xf = x.reshape(T, D)
    cos_a, sin_a = _rope_tables(positions)
    bs_t = jnp.pad(wts["a_spatial_b"].T, ((0, 0), (0, 128 - A_GROUPS)))

    h = _rms_fwd(xf, wts["mix_norm"], "norm1_fwd")
    wts = dict(wts)
    wts["w_in"], token = in_weights([h, cos_a, sin_a])
    z = _mm(h, wts["w_in"], "nt", "in_proj", tm=512, tn=1536, tk=D, n_outer=True, after=token)
    wts["w_q"], wts["w_kv"], wts["w_out"] = mixer_weights(z)
    q, k, v, cqn, ckvn = _lat_fwd(z, wts["q_a_norm"], wts["kv_a_norm"], wts["w_q"], wts["w_kv"], cos_a, sin_a)
    yb, *lses = _attn_fwd(q, k, v, B, S)
    merged = _mix_fwd(z, yb, wts["a_v_norm_g"], wts["a_v_norm_b"], wts["a_spatial_w"], bs_t)
    x1 = _mm(merged, wts["w_out"], "nn", "out_proj", tm=512, tn=D, tk=D, add=xf)
    h2 = _rms_fwd(x1, wts["ffn_norm"], "norm2_fwd")
    wts["w_up"], wts["w_down"], wts["conv_w"] = ffn_weights(h2)
    up_pre = _mm(h2, wts["w_up"], "nn", "up_proj", tm=512, tn=UP_SHARD, tk=D, dims=(T, 2 * D_FF, D),
                 b_spec=pl.BlockSpec((None, D, UP_SHARD), lambda i, j, k: (j, 0, 0)),
                 o_spec=pl.BlockSpec((None, 512, UP_SHARD), lambda i, j, k: (j // 2, i, j % 2)), out_shape=(2, T, D_FF),
                 n_outer=True)
    act = _gate_fwd(up_pre, wts["conv_w"], wts["conv_b"], B, S)
    x2 = _mm(act, wts["w_down"], "nn", "down_proj", tm=512, tn=D, tk=1408, add=x1)
    dx2, loss_row, g_final = _final(x2, tgt.reshape(T, D), wts["final_norm"])

    g = {"final_norm": g_final}
    dact = _mm(dx2, wts["w_down"], "nt", "down_proj_dx", tm=512, tn=1408, tk=D, n_outer=True)
    tk2, tk1 = min(2048, T), min(1024, T)
    g["w_down"], g["w_down_lo"] = _mm(act, dx2, "tn", "down_proj_dw", tm=1408, tn=D, tk=tk1, copy_dtype=GRAD_PAYLOAD)
    dup, g["conv_w"], g["conv_b"] = _gate_bwd(up_pre, dact, wts["conv_w"], wts["conv_b"], B, S)
    g["w_up"], g["w_up_lo"] = _mm(
        h2, dup, "tn", "up_proj_dw", tm=D, tn=UP_SHARD, tk=tk2, dims=(D, 2 * D_FF, T), copy_dtype=GRAD_PAYLOAD,
        b_spec=pl.BlockSpec((None, tk2, UP_SHARD), lambda i, j, k: (j // 2, k, j % 2)),
        o_spec=pl.BlockSpec((None, D, UP_SHARD), lambda i, j, k: (j, 0, 0)), out_shape=(N_CHIPS, D, UP_SHARD))
    token, ffn_sent = on_ffn_grads(g)
    dh2 = _mm(dup, wts["w_up"], "nt", "up_proj_dx", tm=512, tn=D, tk=UP_SHARD, dims=(T, D, 2 * D_FF), after=token,
              a_spec=pl.BlockSpec((None, 512, UP_SHARD), lambda i, j, k: (k // 2, i, k % 2)),
              b_spec=pl.BlockSpec((None, D, UP_SHARD), lambda i, j, k: (k, 0, 0)))
    token = ffn_sent(dh2)
    dx1, g["ffn_norm"] = _rms_bwd(x1, wts["ffn_norm"], dh2, dx2, "norm2_bwd")
    dm = _mm(dx1, wts["w_out"], "nt", "out_proj_dx", tm=512, tn=D, tk=D, after=token)
    g["w_out"], g["w_out_lo"] = _mm(merged, dx1, "tn", "out_proj_dw", tm=D, tn=D, tk=tk1, copy_dtype=GRAD_PAYLOAD)
    dz, dyb, dl, g["a_spatial_w"], gbs, g["a_v_norm_g"], g["a_v_norm_b"] = _mix_bwd(
        z, yb, dm, wts["a_v_norm_g"], wts["a_v_norm_b"], wts["a_spatial_w"], bs_t)
    g["a_spatial_b"] = gbs[:, :A_GROUPS].T
    delta = dl.reshape(HEADS * T // ATT_BLOCK, 1, ATT_BLOCK)
    dq, dk, dv = _attn_bwd(q, k, v, dyb, lses, delta, B, S)
    dz, dq_raw, dkv, g["q_a_norm"], g["kv_a_norm"] = _lat_bwd(
        dz, z, dq, dk, dv, wts["q_a_norm"], wts["kv_a_norm"], wts["w_q"], wts["w_kv"], cos_a, sin_a)
    g["w_q"] = _mm(cqn, dq_raw, "tn", "q_proj_dw", tm=Q_RANK, tn=HEADS * HEAD_PAD, tk=tk2)
    g["w_kv"] = _mm(ckvn, dkv, "tn", "kv_proj_dw", tm=KV_RANK, tn=2 * HEADS * NOPE, tk=tk2)
    g["w_in"] = _mm(dz, h, "tn", "in_proj_dw", tm=1536, tn=D, tk=tk2)
    token = on_mixer_grads(g)
    dh = _mm(dz, wts["w_in"], "nn", "in_proj_dx", tm=512, tn=D, tk=1536, after=token)
    dx, g["mix_norm"] = _rms_bwd(xf, wts["mix_norm"], dh, dx1, "norm1_bwd")
    return loss_row[0, 0], dx.reshape(B, S, D), g


_SMALL = (("mix_norm", (1, D_MODEL)), ("a_v_norm_g", (1, D_MODEL)), ("a_v_norm_b", (1, D_MODEL)),
          ("a_spatial_w", (A_GROUPS * CHUNK, CHUNK)), ("a_spatial_b", (1, A_GROUPS * CHUNK)), ("q_a_norm", (1, Q_RANK)),
          ("kv_a_norm", (1, KV_RANK)), ("ffn_norm", (1, D_MODEL)), ("conv_b", (1, 2 * D_FF)), ("final_norm", (1, D_MODEL)),
          ("conv_w", (3, 2 * D_FF)))
_SMALL_SIZE = sum(math.prod(s) for _, s in _SMALL)
_SMALL_ROWS = -(-(_SMALL_SIZE + 1) // (128 * 8)) * 8


def kernel(x, positions, mix_norm, w_in, a_v_norm_g, a_v_norm_b, a_spatial_w, a_spatial_b, q_a_norm, w_uq, kv_a_norm, w_ukv, w_out, ffn_norm, w_up, conv_w, conv_b, w_down, final_norm, loss_target, m_mix_norm, m_w_in, m_a_v_norm_g, m_a_v_norm_b, m_a_spatial_w, m_a_spatial_b, m_q_a_norm, m_w_uq, m_kv_a_norm, m_w_ukv, m_w_out, m_ffn_norm, m_w_up, m_conv_w, m_conv_b, m_w_down, m_final_norm, v_mix_norm, v_w_in, v_a_v_norm_g, v_a_v_norm_b, v_a_spatial_w, v_a_spatial_b, v_q_a_norm, v_w_uq, v_kv_a_norm, v_w_ukv, v_w_out, v_ffn_norm, v_w_up, v_conv_w, v_conv_b, v_w_down, v_final_norm):
    weights = dict(mix_norm=mix_norm, w_in=w_in, a_v_norm_g=a_v_norm_g, a_v_norm_b=a_v_norm_b, a_spatial_w=a_spatial_w,
                   a_spatial_b=a_spatial_b, q_a_norm=q_a_norm, w_uq=w_uq, kv_a_norm=kv_a_norm, w_ukv=w_ukv, w_out=w_out,
                   ffn_norm=ffn_norm, w_up=w_up, conv_w=conv_w, conv_b=conv_b, w_down=w_down, final_norm=final_norm)
    m_in = dict(mix_norm=m_mix_norm, w_in=m_w_in, a_v_norm_g=m_a_v_norm_g, a_v_norm_b=m_a_v_norm_b,
                a_spatial_w=m_a_spatial_w, a_spatial_b=m_a_spatial_b, q_a_norm=m_q_a_norm, w_uq=m_w_uq,
                kv_a_norm=m_kv_a_norm, w_ukv=m_w_ukv, w_out=m_w_out, ffn_norm=m_ffn_norm, w_up=m_w_up, conv_w=m_conv_w,
                conv_b=m_conv_b, w_down=m_w_down, final_norm=m_final_norm)
    v_in = dict(mix_norm=v_mix_norm, w_in=v_w_in, a_v_norm_g=v_a_v_norm_g, a_v_norm_b=v_a_v_norm_b,
                a_spatial_w=v_a_spatial_w, a_spatial_b=v_a_spatial_b, q_a_norm=v_q_a_norm, w_uq=v_w_uq,
                kv_a_norm=v_kv_a_norm, w_ukv=v_w_ukv, w_out=v_w_out, ffn_norm=v_ffn_norm, w_up=v_w_up, conv_w=v_conv_w,
                conv_b=v_conv_b, w_down=v_w_down, final_norm=v_final_norm)
    names = list(weights)
    chip = 2 * lax.axis_index("x") + lax.axis_index("y")

    def halves(a):
        return a.reshape(a.shape[:-2] + (2, a.shape[-2] // 2, a.shape[-1]))

    w_in_t = jnp.swapaxes(w_in[0], 0, 1).astype(MXU_DTYPE)
    w_in_gather = _exchange_start([jnp.stack(jnp.split(w_in_t, 2, axis=1))], "w_in_gather_start", "halves",
                                  after=positions)
    gathers = {}
    wts = dict(
        mix_norm=mix_norm, a_v_norm_g=a_v_norm_g, a_v_norm_b=a_v_norm_b, a_spatial_w=a_spatial_w[0],
        a_spatial_b=a_spatial_b[0], q_a_norm=q_a_norm, kv_a_norm=kv_a_norm, ffn_norm=ffn_norm,
        final_norm=final_norm.reshape(1, D_MODEL), conv_b=conv_b.reshape(2, 1, D_FF))

    mixer_shards = [weights[n][0].astype(MXU_DTYPE) for n in _BIG[1:4]]
    ffn_shards = [w_up[0].astype(MXU_DTYPE), w_down[0].astype(MXU_DTYPE)]

    def in_weights(after):
        _, landed = _exchange_wait(w_in_gather, "w_in_gather_wait", "halves", list(after) + mixer_shards + ffn_shards)
        (w_in_sh,) = _forward_halves(list(landed))
        gathers["mixer"] = _exchange_start(mixer_shards, "mixer_gather_start", "gather", after=w_in_sh)
        gathers["ffn"] = _exchange_start(ffn_shards + [conv_w[0]], "ffn_gather_start", "gather",
                                         after=gathers["mixer"][4])
        w_in_pad = _w_in_t_to_pad(jnp.concatenate([w_in_sh[:, 0], w_in_sh[:, 1]], axis=-1).reshape(-1, D_MODEL))
        return w_in_pad, gathers["ffn"][4]

    def mixer_weights(after):
        _, (w_uq_sh, w_ukv_sh, w_out_sh) = _exchange_wait(gathers["mixer"], "mixer_gather_wait", "gather", after)
        return (_w_uq_to_pad(_cols_from_chips(w_uq_sh)), _w_ukv_to_pad(_cols_from_chips(w_ukv_sh)),
                w_out_sh.reshape(D_MODEL, D_MODEL))

    def ffn_weights(after):
        _, (w_up_sh, w_down_sh, cw_all) = _exchange_wait(gathers["ffn"], "ffn_gather_wait", "gather", after)
        return w_up_sh, w_down_sh.reshape(D_FF, D_MODEL), _conv_w_split(_cols_from_chips(cw_all))

    scatters = {}

    def start_scatter(slabs, slabs_lo, tag):
        got = _swap_halves([halves(s) for s in slabs_lo], tag + "_grad_swap_halves")
        sums = _pair_sum(slabs, got, tag + "_grad_pair_sum")
        scatters[tag] = _exchange_start(list(sums), tag + "_scatter_start", "scatter", after=slabs[-1])
        return scatters[tag][4]

    def on_ffn_grads(g):
        slabs, slabs_lo = [[g["w_up" + lo], g["w_down" + lo].reshape(N_CHIPS, D_FF // N_CHIPS, D_MODEL)]
                           for lo in ("", "_lo")]
        swap = _exchange_start([halves(s) for s in slabs_lo], "ffn_swap_start", "swap", after=slabs[1])

        def sent(after):
            _, got = _exchange_wait(swap, "ffn_swap_wait", "swap", after)
            sums = _pair_sum(slabs, got, "ffn_grad_pair_sum")
            scatters["ffn"] = _exchange_start(list(sums), "ffn_scatter_start", "scatter", after=got[0])
            return scatters["ffn"][4]

        return swap[4], sent

    def on_mixer_grads(g):
        slabs = [_w_in_t_from_pad(g["w_in"]).reshape(N_CHIPS, -1, D_MODEL), _cols_to_chips(_w_uq_from_pad(g["w_q"])),
                 _cols_to_chips(_w_ukv_from_pad(g["w_kv"]))]
        w_out_slabs = [g["w_out" + lo].reshape(N_CHIPS, D_MODEL // N_CHIPS, D_MODEL) for lo in ("", "_lo")]
        return start_scatter(slabs + w_out_slabs[:1], [s.astype(GRAD_PAYLOAD) for s in slabs] + w_out_slabs[1:], "mixer")

    loss_part, grad_x, g = _local_step(x, positions, loss_target, wts, in_weights, mixer_weights, ffn_weights,
                                       on_ffn_grads, on_mixer_grads)

    g_small_parts = dict(g)
    g_small_parts["conv_w"] = _conv_w_join(g["conv_w"])
    g_small_parts["conv_b"] = g["conv_b"].reshape(1, 2 * D_FF)
    flat = jnp.concatenate([g_small_parts[n].reshape(-1) for n, _ in _SMALL] + [loss_part.reshape(1)])
    flat = jnp.pad(flat, (0, _SMALL_ROWS * 128 - flat.shape[0])).reshape(_SMALL_ROWS, 128)
    small_gather = _exchange_start([flat], "small_gather_start", "all", after=grad_x)

    mixer_sums, mixer_landed = _exchange_wait(scatters["mixer"], "mixer_scatter_wait", "scatter", after=small_gather[4])
    ffn_sums, ffn_landed = _exchange_wait(scatters["ffn"], "ffn_scatter_wait", "scatter", after=mixer_landed[0])
    reduced = _chip_sum(list(mixer_sums) + list(ffn_sums), list(mixer_landed) + list(ffn_landed))
    g_big = dict(zip(_BIG, _join_halves(reduced)))

    grads, deltas, new_m, new_v = {}, {}, {}, {}

    def update(n, grad):
        w = weights[n]
        shape2 = grad.shape
        d, nm, nv = _adamw(w.reshape(shape2), grad, m_in[n].reshape(shape2), v_in[n].reshape(shape2), "adamw_" + n)
        grads[n], deltas[n], new_m[n], new_v[n] = (t.reshape(w.shape) for t in (grad, d, nm, nv))

    def update_transposed(n, grad_t):
        t = lambda a: jnp.swapaxes(a, 1, 2)
        d, nm, nv = _adamw(t(weights[n]), grad_t, t(m_in[n]), t(v_in[n]), "adamw_" + n)
        grads[n], deltas[n], new_m[n], new_v[n] = t(grad_t), t(d), t(nm), t(nv)

    for n in _BIG:
        g3 = g_big[n].reshape((1, -1, g_big[n].shape[-1]))
        if n == "w_in":
            update_transposed(n, g3)
        else:
            update(n, g3)

    (own,), (everyone,) = _exchange_wait(small_gather, "small_gather_wait", "all", after=deltas["w_up"])
    device = 2 * chip + lax.axis_index("c")
    everyone = lax.dynamic_update_slice(everyone, own[None], (device, 0, 0))
    total = _sum_slabs([everyone[j] for j in range(8)], "small_grads_sum", tr=_SMALL_ROWS).reshape(-1)
    o = 0
    for n, shp in _SMALL:
        piece = total[o:o + math.prod(shp)].reshape(shp)
        o += math.prod(shp)
        if n == "conv_w":
            piece = lax.dynamic_slice_in_dim(piece, chip * UP_SHARD, UP_SHARD, axis=1)
        update(n, piece)
    loss = total[_SMALL_SIZE]
    return (loss, grad_x, *[grads[n] for n in names], *[deltas[n] for n in names], *[new_m[n] for n in names],
            *[new_v[n] for n in names])
```

```python
import functools
import math

import jax
import jax.numpy as jnp
from jax import lax
from jax.experimental import pallas as pl
from jax.experimental.pallas import tpu as pltpu

F32 = jnp.float32
MXU_DTYPE = jnp.bfloat16
MESH = pl.DeviceIdType.MESH

D_MODEL = 1024
EPS = 1e-6
A_GROUPS = 8
CHUNK = 128
HEADS = 8
NOPE = 128
ROPE = 64
QK_DIM = NOPE + ROPE
HEAD_PAD = 256
Q_RANK = 256
KV_RANK = 128
ROPE_THETA = 10000.0
D_FF = 2816
FF_TILE = 256
N_FF_TILES = D_FF // FF_TILE
LAT = 512
IN_PAD = 4 * D_MODEL + LAT
N_CHIPS = 4
ADAM_LR, ADAM_B1, ADAM_B2, ADAM_EPS, ADAM_WD, ADAM_STEP = 0.001, 0.9, 0.999, 1e-08, 0.01, 10

VMEM_CAP_V7X = 64 * 1024 * 1024
NEG = -1e30


def _params(sem, nbytes):
    limit = int(min(VMEM_CAP_V7X - (8 << 20), max(32 << 20, 3 * nbytes)))
    return pltpu.CompilerParams(dimension_semantics=sem, vmem_limit_bytes=limit)


def _nbytes(shape, dtype):
    return math.prod(shape) * jnp.dtype(dtype).itemsize


_DIMS = {"nn": (((1,), (0,)), ((), ())), "nt": (((1,), (1,)), ((), ())), "tn": (((0,), (0,)), ((), ()))}


def _mm(a, b, mode, name, *, tm, tn, tk, out_dtype=F32, add=None, dims=None, a_spec=None, b_spec=None,
        o_spec=None, out_shape=None, n_outer=False, copy_dtype=None, after=None):
    if dims is None:
        if mode == "nn":
            (M, K), (_, N) = a.shape, b.shape
        elif mode == "nt":
            (M, K), (N, _) = a.shape, b.shape
        else:
            (K, M), (_, N) = a.shape, b.shape
    else:
        M, N, K = dims
    a_blk = (tk, tm) if mode == "tn" else (tm, tk)
    b_blk = (tn, tk) if mode == "nt" else (tk, tn)
    if a_spec is None:
        a_spec = pl.BlockSpec(a_blk, (lambda i, j, k: (k, i)) if mode == "tn" else (lambda i, j, k: (i, k)))
    if b_spec is None:
        b_spec = pl.BlockSpec(b_blk, (lambda i, j, k: (j, k)) if mode == "nt" else (lambda i, j, k: (k, j)))
    if o_spec is None:
        o_spec = pl.BlockSpec((tm, tn), lambda i, j, k: (i, j))
    if out_shape is None:
        out_shape = (M, N)
    assert M % tm == 0 and N % tn == 0 and K % tk == 0, (name, M, N, K, tm, tn, tk)
    nk = K // tk
    contract = _DIMS[mode]
    has_add = add is not None

    def body(*refs):
        a_ref, b_ref = refs[0], refs[1]
        add_ref = refs[2] if has_add else None
        n_in = 2 + has_add + (after is not None)
        o_ref = refs[n_in]
        copy_ref = refs[n_in + 1] if copy_dtype is not None else None

        def product():
            return lax.dot_general(a_ref[...].astype(MXU_DTYPE), b_ref[...].astype(MXU_DTYPE), contract,
                                   preferred_element_type=F32)

        def finish(r):
            if has_add:
                r = r + add_ref[...]
            o_ref[...] = r.astype(out_dtype)
            if copy_ref is not None:
                copy_ref[...] = r.astype(copy_dtype)

        if nk == 1:
            finish(product())
            return
        acc = refs[-1]
        k = pl.program_id(2)

        @pl.when(k == 0)
        def _():
            acc[...] = jnp.zeros_like(acc)

        acc[...] += product()

        @pl.when(k == nk - 1)
        def _():
            finish(acc[...])

    in_specs = [a_spec, b_spec]
    args = [a, b]
    nbytes = _nbytes(a_blk, a.dtype) + _nbytes(b_blk, b.dtype) + 3 * _nbytes((tm, tn), F32)
    if has_add:
        in_specs.append(pl.BlockSpec((tm, tn), lambda i, j, k: (i, j)))
        args.append(add)
        nbytes += _nbytes((tm, tn), F32)
    if after is not None:
        in_specs.append(pl.BlockSpec(after.shape, lambda i, j, k: (0, 0)))
        args.append(after)
    grid = (M // tm, N // tn, nk)
    if n_outer:
        def swapped(spec):
            return pl.BlockSpec(spec.block_shape, lambda j, i, k, at=spec.index_map: at(i, j, k))

        grid = (N // tn, M // tm, nk)
        in_specs = [swapped(s) for s in in_specs]
        o_spec = swapped(o_spec)
    out_sds, out_specs = jax.ShapeDtypeStruct(out_shape, out_dtype), o_spec
    if copy_dtype is not None:
        out_sds, out_specs = (out_sds, jax.ShapeDtypeStruct(out_shape, copy_dtype)), (o_spec, o_spec)
    return pl.pallas_call(
        body, name=name, out_shape=out_sds, grid=grid, in_specs=in_specs, out_specs=out_specs,
        scratch_shapes=[pltpu.VMEM((tm, tn), F32)] if nk > 1 else [],
        compiler_params=_params(("parallel", "parallel", "arbitrary"), nbytes),
    )(*args)


_GELU_C = math.sqrt(2.0 / math.pi)
_GELU_A = 0.044715


def _sigmoid(x):
    return 0.5 * jnp.tanh(0.5 * x) + 0.5


def _gelu(x):
    t = jnp.tanh(x * (_GELU_C + (_GELU_C * _GELU_A) * (x * x)))
    return x * (0.5 + 0.5 * t)


def _gelu_and_grad(x):
    x2 = x * x
    t = jnp.tanh(x * (_GELU_C + (_GELU_C * _GELU_A) * x2))
    cdf = 0.5 + 0.5 * t
    grad = cdf + (0.5 * x) * (1.0 - t * t) * (_GELU_C + (3.0 * _GELU_C * _GELU_A) * x2)
    return x * cdf, grad


def _rope_mix(g, cos_a, sin_a):
    return g * cos_a + pltpu.roll(g, 64, 1) * sin_a


def _rope_mix_bwd(d, cos_a, sin_a):
    return d * cos_a + pltpu.roll(d * sin_a, 64, 1)


def _rms_fwd(x, g, name, tr=512):
    T, D = x.shape

    def body(x_ref, g_ref, h_ref):
        xv = x_ref[...]
        r = lax.rsqrt(jnp.mean(xv * xv, axis=-1, keepdims=True) + EPS)
        h_ref[...] = ((xv * r) * g_ref[...]).astype(h_ref.dtype)

    return pl.pallas_call(
        body, name=name, out_shape=jax.ShapeDtypeStruct((T, D), MXU_DTYPE), grid=(T // tr,),
        in_specs=[pl.BlockSpec((tr, D), lambda i: (i, 0)), pl.BlockSpec((1, D), lambda i: (0, 0))],
        out_specs=pl.BlockSpec((tr, D), lambda i: (i, 0)),
        compiler_params=_params(("parallel",), 3 * _nbytes((tr, D), F32)),
    )(x, g)


def _rms_bwd(x, g, dh, dres, name, tr=512):
    T, D = x.shape

    def body(x_ref, g_ref, dh_ref, dres_ref, dx_ref, gg_ref):
        @pl.when(pl.program_id(0) == 0)
        def _():
            gg_ref[...] = jnp.zeros_like(gg_ref)

        xv = x_ref[...]
        r = lax.rsqrt(jnp.mean(xv * xv, axis=-1, keepdims=True) + EPS)
        xn = xv * r
        dhv = dh_ref[...]
        dxn = dhv * g_ref[...]
        dx_ref[...] = dres_ref[...] + r * (dxn - xn * jnp.mean(dxn * xn, axis=-1, keepdims=True))
        gg_ref[...] += jnp.sum(dhv * xn, axis=0, keepdims=True)

    row = pl.BlockSpec((tr, D), lambda i: (i, 0))
    vec = pl.BlockSpec((1, D), lambda i: (0, 0))
    return pl.pallas_call(
        body, name=name,
        out_shape=(jax.ShapeDtypeStruct((T, D), F32), jax.ShapeDtypeStruct((1, D), F32)),
        grid=(T // tr,), in_specs=[row, vec, row, row], out_specs=(row, vec),
        compiler_params=_params(("arbitrary",), 6 * _nbytes((tr, D), F32)),
    )(x, g, dh, dres)


def _lat_fwd(z, gq, gkv, wq, wkv, cos_a, sin_a, tr=256):
    T = z.shape[0]
    lat_blk = (4 * D_MODEL) // LAT

    def body(z_ref, gq_ref, gkv_ref, wq_ref, wkv_ref, cos_ref, sin_ref, q_ref, k_ref, v_ref, cqn_ref, ckvn_ref):
        zl = z_ref[...]
        cos_v, sin_v = cos_ref[...], sin_ref[...]
        cq = zl[:, :Q_RANK]
        ckv = zl[:, Q_RANK:Q_RANK + KV_RANK]
        krb = zl[:, Q_RANK + KV_RANK:]
        cqn = ((cq * lax.rsqrt(jnp.mean(cq * cq, axis=-1, keepdims=True) + EPS)) * gq_ref[...]).astype(MXU_DTYPE)
        ckvn = ((ckv * lax.rsqrt(jnp.mean(ckv * ckv, axis=-1, keepdims=True) + EPS)) * gkv_ref[...]).astype(MXU_DTYPE)
        cqn_ref[...] = cqn
        ckvn_ref[...] = ckvn
        krr = _rope_mix(krb, cos_v, sin_v).astype(MXU_DTYPE)
        q = jnp.dot(cqn, wq_ref[...], preferred_element_type=F32)
        kv = jnp.dot(ckvn, wkv_ref[...], preferred_element_type=F32)
        for h in range(HEADS):
            o = h * HEAD_PAD
            q_ref[:, o:o + NOPE] = q[:, o:o + NOPE].astype(MXU_DTYPE)
            q_ref[:, o + NOPE:o + HEAD_PAD] = _rope_mix(q[:, o + NOPE:o + HEAD_PAD], cos_v, sin_v).astype(MXU_DTYPE)
            k_ref[:, o:o + NOPE] = kv[:, h * NOPE:(h + 1) * NOPE].astype(MXU_DTYPE)
            k_ref[:, o + NOPE:o + HEAD_PAD] = krr
        v_ref[...] = kv[:, HEADS * NOPE:].astype(MXU_DTYPE)

    def row(w):
        return pl.BlockSpec((tr, w), lambda i: (i, 0))

    def full(a):
        return pl.BlockSpec(a.shape, lambda i: (0, 0))

    return pl.pallas_call(
        body, name="lat_fwd",
        out_shape=(jax.ShapeDtypeStruct((T, HEADS * HEAD_PAD), MXU_DTYPE), jax.ShapeDtypeStruct((T, HEADS * HEAD_PAD), MXU_DTYPE),
                   jax.ShapeDtypeStruct((T, HEADS * NOPE), MXU_DTYPE), jax.ShapeDtypeStruct((T, Q_RANK), MXU_DTYPE),
                   jax.ShapeDtypeStruct((T, KV_RANK), MXU_DTYPE)),
        grid=(T // tr,),
        in_specs=[pl.BlockSpec((tr, LAT), lambda i: (i, lat_blk)), full(gq), full(gkv), full(wq), full(wkv), row(128), row(128)],
        out_specs=(row(HEADS * HEAD_PAD), row(HEADS * HEAD_PAD), row(HEADS * NOPE), row(Q_RANK), row(KV_RANK)),
        compiler_params=_params(("parallel",), 8 * _nbytes((tr, HEADS * HEAD_PAD), F32)),
    )(z, gq, gkv, wq, wkv, cos_a, sin_a)


ATT_BLOCK = 256
_SCALE = QK_DIM ** -0.5


def _causal_mask(n):
    return lax.broadcasted_iota(jnp.int32, (n, n), 1) <= lax.broadcasted_iota(jnp.int32, (n, n), 0)


def _causal_mask_t(n):
    return lax.broadcasted_iota(jnp.int32, (n, n), 0) <= lax.broadcasted_iota(jnp.int32, (n, n), 1)


ATT_HEADS = 4


def _attn_fwd(q, k, v, B, S):
    tq = ATT_BLOCK
    nq = S // tq
    T = B * S
    hp, groups = ATT_HEADS, HEADS // ATT_HEADS

    def body(q_ref, k_ref, v_ref, o_ref, *lse_refs):
        qi = pl.program_id(2)
        qs = [q_ref[:, t * HEAD_PAD:(t + 1) * HEAD_PAD] for t in range(hp)]

        def scores(j, t):
            rows = pl.ds(pl.multiple_of(j * tq, tq), tq)
            return lax.dot_general(k_ref[rows, t * HEAD_PAD:(t + 1) * HEAD_PAD], qs[t], _DIMS["nt"],
                                   preferred_element_type=F32)

        def step(j, carry, last):
            rows = pl.ds(pl.multiple_of(j * tq, tq), tq)
            out = []
            for t in range(hp):
                m, l, acc, st = carry[t]
                st_next = st if last else scores(j + 1, t)
                st = st * _SCALE
                if last:
                    st = jnp.where(_causal_mask_t(tq), st, NEG)
                m_new = jnp.maximum(m, jnp.max(st, axis=0, keepdims=True))
                alpha = jnp.exp(m - m_new)
                p = jnp.exp(st - m_new)
                l = alpha * l + jnp.sum(p, axis=0, keepdims=True)
                acc = alpha * acc + lax.dot_general(v_ref[rows, t * NOPE:(t + 1) * NOPE], p.astype(MXU_DTYPE),
                                                    _DIMS["tn"], preferred_element_type=F32)
                out.append((m_new, l, acc, st_next))
            return tuple(out)

        init = tuple((jnp.full((1, tq), NEG, F32), jnp.zeros((1, tq), F32), jnp.zeros((NOPE, tq), F32), scores(0, t))
                     for t in range(hp))
        carry = lax.fori_loop(0, qi, lambda j, c: step(j, c, False), init)
        carry = step(qi, carry, True)
        for t in range(hp):
            m, l, acc, _ = carry[t]
            o_ref[:, t * NOPE:(t + 1) * NOPE] = (acc / l).T
            lse_refs[t][0] = m + jnp.log(l)

    lse_sds = jax.ShapeDtypeStruct((groups * B * nq, 1, tq), F32)
    lse_spec = pl.BlockSpec((1, 1, tq), lambda b, h, i: ((h * B + b) * nq + i, 0, 0))
    return pl.pallas_call(
        body, name="attn_fwd",
        out_shape=(jax.ShapeDtypeStruct((T, HEADS * NOPE), F32),) + (lse_sds,) * hp,
        grid=(B, groups, nq),
        in_specs=[pl.BlockSpec((tq, hp * HEAD_PAD), lambda b, h, i: (b * nq + i, h)),
                  pl.BlockSpec((S, hp * HEAD_PAD), lambda b, h, i: (b, h)),
                  pl.BlockSpec((S, hp * NOPE), lambda b, h, i: (b, h))],
        out_specs=(pl.BlockSpec((tq, hp * NOPE), lambda b, h, i: (b * nq + i, h)),) + (lse_spec,) * hp,
        compiler_params=_params(("parallel", "parallel", "arbitrary"), 4 * hp * _nbytes((S, HEAD_PAD), MXU_DTYPE)),
    )(q, k, v)


def _attn_bwd(q, k, v, do, lses, delta, B, S):
    tq = ATT_BLOCK
    nq = S // tq
    T = B * S
    hp, groups = ATT_HEADS, HEADS // ATT_HEADS

    def body(q_ref, k_ref, v_ref, do_ref, *refs):
        lse_refs, dl_refs = refs[:hp], refs[hp:2 * hp]
        dq_out, dk_ref, dv_ref, dq_ref = refs[2 * hp:]
        kj = pl.program_id(2)

        @pl.when(kj == 0)
        def _():
            dq_ref[...] = jnp.zeros_like(dq_ref)

        def products(i, t):
            rows = pl.ds(pl.multiple_of(i * tq, tq), tq)
            st = lax.dot_general(k_ref[:, t * HEAD_PAD:(t + 1) * HEAD_PAD], q_ref[rows, t * HEAD_PAD:(t + 1) * HEAD_PAD],
                                 _DIMS["nt"], preferred_element_type=F32)
            dpt = lax.dot_general(v_ref[:, t * NOPE:(t + 1) * NOPE], do_ref[rows, t * NOPE:(t + 1) * NOPE],
                                  _DIMS["nt"], preferred_element_type=F32)
            return st, dpt

        def step(i, carry, masked):
            rows = pl.ds(pl.multiple_of(i * tq, tq), tq)
            nxt = jnp.minimum(i + 1, nq - 1)
            out = []
            for t in range(hp):
                dk, dv, st, dpt = carry[t]
                st_next, dpt_next = products(nxt, t)
                qk_cols = slice(t * HEAD_PAD, (t + 1) * HEAD_PAD)
                v_cols = slice(t * NOPE, (t + 1) * NOPE)
                p = jnp.exp(st * _SCALE - lse_refs[t][i])
                if masked:
                    p = jnp.where(_causal_mask_t(tq), p, 0.0)
                dv = dv + jnp.dot(p.astype(MXU_DTYPE), do_ref[rows, v_cols], preferred_element_type=F32)
                ds = (p * (dpt - dl_refs[t][i]) * _SCALE).astype(MXU_DTYPE)
                dk = dk + jnp.dot(ds, q_ref[rows, qk_cols], preferred_element_type=F32)
                dq_ref[rows, qk_cols] += lax.dot_general(ds, k_ref[:, qk_cols], _DIMS["tn"], preferred_element_type=F32)
                out.append((dk, dv, st_next, dpt_next))
            return tuple(out)

        init = tuple((jnp.zeros((tq, HEAD_PAD), F32), jnp.zeros((tq, NOPE), F32)) + products(kj, t) for t in range(hp))
        carry = step(kj, init, True)
        carry = lax.fori_loop(kj + 1, nq, lambda i, c: step(i, c, False), carry)
        for t in range(hp):
            dk_ref[:, t * HEAD_PAD:(t + 1) * HEAD_PAD] = carry[t][0].astype(dk_ref.dtype)
            dv_ref[:, t * NOPE:(t + 1) * NOPE] = carry[t][1].astype(dv_ref.dtype)

        @pl.when(kj == nq - 1)
        def _():
            dq_out[...] = dq_ref[...].astype(dq_out.dtype)

    seq = lambda w: pl.BlockSpec((S, w), lambda b, h, j: (b, h))
    blk = lambda w: pl.BlockSpec((tq, w), lambda b, h, j: (b * nq + j, h))
    lse_spec = pl.BlockSpec((nq, 1, tq), lambda b, h, j: (h * B + b, 0, 0))
    dl_specs = [pl.BlockSpec((nq, 1, tq), lambda b, h, j, t=t: ((h * hp + t) * B + b, 0, 0)) for t in range(hp)]
    return pl.pallas_call(
        body, name="attn_bwd",
        out_shape=(jax.ShapeDtypeStruct((T, HEADS * HEAD_PAD), MXU_DTYPE), jax.ShapeDtypeStruct((T, HEADS * HEAD_PAD), MXU_DTYPE),
                   jax.ShapeDtypeStruct((T, HEADS * NOPE), MXU_DTYPE)),
        grid=(B, groups, nq),
        in_specs=[seq(hp * HEAD_PAD), blk(hp * HEAD_PAD), blk(hp * NOPE), seq(hp * NOPE)] + [lse_spec] * hp + dl_specs,
        out_specs=(seq(hp * HEAD_PAD), blk(hp * HEAD_PAD), blk(hp * NOPE)),
        scratch_shapes=[pltpu.VMEM((S, hp * HEAD_PAD), F32)],
        compiler_params=_params(("parallel", "parallel", "arbitrary"), 8 * hp * _nbytes((S, HEAD_PAD), F32)),
    )(q, k, v, do, *lses, *([delta] * hp))


MIX_ROWS = 256


def _tril_weights(ws_ref, g):
    return jnp.where(_causal_mask(CHUNK), ws_ref[g], 0.0).astype(MXU_DTYPE)


def _layer_norm_stats(va):
    mu = jnp.mean(va, axis=-1, keepdims=True)
    xc = va - mu
    rs = lax.rsqrt(jnp.mean(xc * xc, axis=-1, keepdims=True) + EPS)
    return xc * rs


def _mix_specs(tr):
    zcol = lambda c: pl.BlockSpec((tr, D_MODEL), lambda i, c=c: (i, c))
    row = pl.BlockSpec((tr, D_MODEL), lambda i: (i, 0))
    vec = pl.BlockSpec((1, D_MODEL), lambda i: (0, 0))
    ws = pl.BlockSpec((A_GROUPS, CHUNK, CHUNK), lambda i: (0, 0, 0))
    bs = pl.BlockSpec((CHUNK, 128), lambda i: (0, 0))
    return zcol, row, vec, ws, bs


def _mix_fwd(z, yb, ln_g, ln_b, ws, bs_t):
    T = z.shape[0]
    tr = MIX_ROWS
    zcol, row, vec, ws_spec, bs_spec = _mix_specs(tr)

    def body(zu_ref, zv_ref, zga_ref, zgb_ref, yb_ref, g_ref, b_ref, ws_ref, bs_ref, out_ref, vn_s):
        vhat = _layer_norm_stats(_gelu(zv_ref[...]))
        vn_s[...] = (vhat * g_ref[...] + b_ref[...]).astype(MXU_DTYPE)
        for g in range(A_GROUPS):
            w = _tril_weights(ws_ref, g)
            bias = bs_ref[:, g:g + 1]
            cols = slice(g * CHUNK, (g + 1) * CHUNK)
            for c in range(tr // CHUNK):
                rows = slice(c * CHUNK, (c + 1) * CHUNK)
                mixed = jnp.dot(w, vn_s[rows, cols], preferred_element_type=F32) + bias
                ya = _gelu(zu_ref[rows, cols]) * mixed
                merged = _sigmoid(zga_ref[rows, cols]) * ya + _sigmoid(zgb_ref[rows, cols]) * yb_ref[rows, cols]
                out_ref[rows, cols] = merged.astype(MXU_DTYPE)

    return pl.pallas_call(
        body, name="mix_fwd", out_shape=jax.ShapeDtypeStruct((T, D_MODEL), MXU_DTYPE), grid=(T // tr,),
        in_specs=[zcol(0), zcol(1), zcol(2), zcol(3), row, vec, vec, ws_spec, bs_spec], out_specs=row,
        scratch_shapes=[pltpu.VMEM((tr, D_MODEL), MXU_DTYPE)],
        compiler_params=_params(("parallel",), 8 * _nbytes((tr, D_MODEL), F32)),
    )(z, z, z, z, yb, ln_g, ln_b, ws, bs_t)


def _mix_bwd(z, yb, dm, ln_g, ln_b, ws, bs_t):
    T = z.shape[0]
    tr = MIX_ROWS
    zcol, row, vec, ws_spec, bs_spec = _mix_specs(tr)

    def body(zu_ref, zv_ref, zga_ref, zgb_ref, yb_ref, dm_ref, g_ref, b_ref, ws_ref, bs_ref,
             dz_ref, dyb_ref, dl_ref, gws_ref, gbs_ref, glg_ref, glb_ref, vn_s, dvn_s):
        @pl.when(pl.program_id(0) == 0)
        def _():
            gws_ref[...] = jnp.zeros_like(gws_ref)
            gbs_ref[...] = jnp.zeros_like(gbs_ref)
            glg_ref[...] = jnp.zeros_like(glg_ref)
            glb_ref[...] = jnp.zeros_like(glb_ref)

        lane = lax.broadcasted_iota(jnp.int32, (CHUNK, 128), 1)
        va, dgelu_v = _gelu_and_grad(zv_ref[...])
        mu = jnp.mean(va, axis=-1, keepdims=True)
        xc = va - mu
        rs = lax.rsqrt(jnp.mean(xc * xc, axis=-1, keepdims=True) + EPS)
        vhat = xc * rs
        vn_s[...] = (vhat * g_ref[...] + b_ref[...]).astype(MXU_DTYPE)
        gbs_acc = jnp.zeros((CHUNK, 128), F32)
        for g in range(A_GROUPS):
            w = _tril_weights(ws_ref, g)
            bias = bs_ref[:, g:g + 1]
            cols = slice(g * CHUNK, (g + 1) * CHUNK)
            gw_acc = jnp.zeros((CHUNK, CHUNK), F32)
            for c in range(tr // CHUNK):
                rows = slice(c * CHUNK, (c + 1) * CHUNK)
                vn = vn_s[rows, cols]
                mixed = jnp.dot(w, vn, preferred_element_type=F32) + bias
                ua, dgelu_u = _gelu_and_grad(zu_ref[rows, cols])
                dmv = dm_ref[rows, cols]
                sa = _sigmoid(zga_ref[rows, cols])
                dya = dmv * sa
                dz_ref[rows, 2 * D_MODEL + g * CHUNK:2 * D_MODEL + (g + 1) * CHUNK] = (
                    dmv * (ua * mixed) * (sa * (1.0 - sa))).astype(dz_ref.dtype)
                dz_ref[rows, cols] = (dya * mixed * dgelu_u).astype(dz_ref.dtype)
                dmix = dya * ua
                gbs_acc = gbs_acc + jnp.where(lane == g, jnp.sum(dmix, axis=-1, keepdims=True), 0.0)
                dmix_b = dmix.astype(MXU_DTYPE)
                gw_acc = gw_acc + lax.dot_general(dmix_b, vn, _DIMS["nt"], preferred_element_type=F32)
                dvn_s[rows, cols] = lax.dot_general(w, dmix_b, _DIMS["tn"], preferred_element_type=F32)
            gws_ref[g] += jnp.where(_causal_mask(CHUNK), gw_acc, 0.0)
        gbs_ref[...] += gbs_acc

        dvn = dvn_s[...]
        glg_ref[...] += jnp.sum(dvn * vhat, axis=0, keepdims=True)
        glb_ref[...] += jnp.sum(dvn, axis=0, keepdims=True)
        dvh = dvn * g_ref[...]
        dva = rs * (dvh - jnp.mean(dvh, axis=-1, keepdims=True) - vhat * jnp.mean(dvh * vhat, axis=-1, keepdims=True))
        dz_ref[:, D_MODEL:2 * D_MODEL] = (dva * dgelu_v).astype(dz_ref.dtype)

        dmv = dm_ref[...]
        ybv = yb_ref[...]
        sb = _sigmoid(zgb_ref[...])
        dyb = dmv * sb
        dyb_ref[...] = dyb.astype(dyb_ref.dtype)
        dz_ref[:, 3 * D_MODEL:4 * D_MODEL] = (dmv * ybv * (sb * (1.0 - sb))).astype(dz_ref.dtype)
        dz_ref[:, 4 * D_MODEL:] = jnp.zeros((tr, LAT), dz_ref.dtype)
        prod = dyb * ybv
        sel = (lax.broadcasted_iota(jnp.int32, (HEADS, D_MODEL), 1) // NOPE
               == lax.broadcasted_iota(jnp.int32, (HEADS, D_MODEL), 0)).astype(jnp.bfloat16)
        hi = prod.astype(jnp.bfloat16)
        rest = prod - hi.astype(F32)
        mid = rest.astype(jnp.bfloat16)
        lo = (rest - mid.astype(F32)).astype(jnp.bfloat16)
        dl_ref[...] = (lax.dot_general(sel, hi, _DIMS["nt"], preferred_element_type=F32)
                       + lax.dot_general(sel, mid, _DIMS["nt"], preferred_element_type=F32)
                       + lax.dot_general(sel, lo, _DIMS["nt"], preferred_element_type=F32))

    return pl.pallas_call(
        body, name="mix_bwd",
        out_shape=(jax.ShapeDtypeStruct((T, IN_PAD), MXU_DTYPE), jax.ShapeDtypeStruct((T, D_MODEL), MXU_DTYPE),
                   jax.ShapeDtypeStruct((HEADS, T), F32), jax.ShapeDtypeStruct((A_GROUPS, CHUNK, CHUNK), F32),
                   jax.ShapeDtypeStruct((CHUNK, 128), F32), jax.ShapeDtypeStruct((1, D_MODEL), F32),
                   jax.ShapeDtypeStruct((1, D_MODEL), F32)),
        grid=(T // tr,),
        in_specs=[zcol(0), zcol(1), zcol(2), zcol(3), row, row, vec, vec, ws_spec, bs_spec],
        out_specs=(pl.BlockSpec((tr, IN_PAD), lambda i: (i, 0)), row, pl.BlockSpec((HEADS, tr), lambda i: (0, i)),
                   ws_spec, bs_spec, vec, vec),
        scratch_shapes=[pltpu.VMEM((tr, D_MODEL), MXU_DTYPE), pltpu.VMEM((tr, D_MODEL), F32)],
        compiler_params=_params(("arbitrary",), 12 * _nbytes((tr, D_MODEL), F32)),
    )(z, z, z, z, yb, dm, ln_g, ln_b, ws, bs_t)


def _lat_bwd(dz, z, dq, dk, dv, gq, gkv, wq, wkv, cos_a, sin_a, tr=256):
    T = z.shape[0]
    lat_blk = (4 * D_MODEL) // LAT

    def body(dz_in, z_ref, dq_ref, dk_ref, dv_ref, gq_ref, gkv_ref, wq_ref, wkv_ref, cos_ref, sin_ref,
             dz_ref, dqr_ref, dkv_ref, ggq_ref, ggkv_ref):
        del dz_in

        @pl.when(pl.program_id(0) == 0)
        def _():
            ggq_ref[...] = jnp.zeros_like(ggq_ref)
            ggkv_ref[...] = jnp.zeros_like(ggkv_ref)

        cos_v, sin_v = cos_ref[...], sin_ref[...]
        dkr = jnp.zeros((tr, 128), F32)
        for h in range(HEADS):
            o = h * HEAD_PAD
            dqr_ref[:, o:o + NOPE] = dq_ref[:, o:o + NOPE].astype(MXU_DTYPE)
            dqr_ref[:, o + NOPE:o + HEAD_PAD] = _rope_mix_bwd(dq_ref[:, o + NOPE:o + HEAD_PAD], cos_v, sin_v).astype(MXU_DTYPE)
            dkv_ref[:, h * NOPE:(h + 1) * NOPE] = dk_ref[:, o:o + NOPE].astype(MXU_DTYPE)
            dkr = dkr + _rope_mix_bwd(dk_ref[:, o + NOPE:o + HEAD_PAD], cos_v, sin_v)
        dkv_ref[:, HEADS * NOPE:] = dv_ref[...]
        dcqn = lax.dot_general(dqr_ref[...], wq_ref[...], _DIMS["nt"], preferred_element_type=F32)
        dckvn = lax.dot_general(dkv_ref[...], wkv_ref[...], _DIMS["nt"], preferred_element_type=F32)

        zl = z_ref[...]

        def rms_bwd(c, dn, g_ref, gg_ref):
            r = lax.rsqrt(jnp.mean(c * c, axis=-1, keepdims=True) + EPS)
            ch = c * r
            gg_ref[...] += jnp.sum(dn * ch, axis=0, keepdims=True)
            dch = dn * g_ref[...]
            return r * (dch - ch * jnp.mean(dch * ch, axis=-1, keepdims=True))

        dz_ref[:, :Q_RANK] = rms_bwd(zl[:, :Q_RANK], dcqn, gq_ref, ggq_ref).astype(dz_ref.dtype)
        dz_ref[:, Q_RANK:Q_RANK + KV_RANK] = rms_bwd(zl[:, Q_RANK:Q_RANK + KV_RANK], dckvn, gkv_ref, ggkv_ref).astype(dz_ref.dtype)
        dz_ref[:, Q_RANK + KV_RANK:] = dkr.astype(dz_ref.dtype)

    def row(w):
        return pl.BlockSpec((tr, w), lambda i: (i, 0))

    def full(a):
        return pl.BlockSpec(a.shape, lambda i: (0, 0))

    lat = pl.BlockSpec((tr, LAT), lambda i: (i, lat_blk))
    return pl.pallas_call(
        body, name="lat_bwd",
        out_shape=(jax.ShapeDtypeStruct(dz.shape, dz.dtype), jax.ShapeDtypeStruct((T, HEADS * HEAD_PAD), MXU_DTYPE),
                   jax.ShapeDtypeStruct((T, 2 * HEADS * NOPE), MXU_DTYPE), jax.ShapeDtypeStruct(gq.shape, F32),
                   jax.ShapeDtypeStruct(gkv.shape, F32)),
        grid=(T // tr,),
        in_specs=[pl.BlockSpec(memory_space=pl.ANY), lat, row(HEADS * HEAD_PAD), row(HEADS * HEAD_PAD), row(HEADS * NOPE),
                  full(gq), full(gkv), full(wq), full(wkv), row(128), row(128)],
        out_specs=(lat, row(HEADS * HEAD_PAD), row(2 * HEADS * NOPE), full(gq), full(gkv)),
        input_output_aliases={0: 0},
        compiler_params=_params(("arbitrary",), 8 * _nbytes((tr, HEADS * HEAD_PAD), F32)),
    )(dz, z, dq, dk, dv, gq, gkv, wq, wkv, cos_a, sin_a)


GATE_ROWS = 64
HALO = 8


def _taps(ref, half, r, first):
    C = GATE_ROWS
    if first:
        xs = jnp.concatenate([jnp.zeros((HALO, ref.shape[-1]), F32), ref[half, 0:C, :]], axis=0)
    else:
        xs = ref[half, pl.ds(pl.multiple_of(r * C - HALO, HALO), C + HALO), :]
    return xs[HALO:, :], pltpu.roll(xs, 1, 0)[HALO:, :], pltpu.roll(xs, 2, 0)[HALO:, :]


def _conv_taps(taps, cw, cb):
    x0, x1, x2 = taps
    return cb + cw[0:1, :] * x2 + cw[1:2, :] * x1 + cw[2:3, :] * x0


def _fold8(x):
    acc = x[0:8, :]
    for i in range(1, x.shape[0] // 8):
        acc = acc + x[8 * i:8 * (i + 1), :]
    return acc


def _gate_fwd(up3, conv_w, conv_b, B, S):
    T = B * S
    W = FF_TILE
    C = GATE_ROWS

    def body(up_ref, cw_ref, cb_ref, act_ref):
        def chunk(r, first):
            gate = _conv_taps(_taps(up_ref, 0, r, first), cw_ref[0], cb_ref[0])
            val = _conv_taps(_taps(up_ref, 1, r, first), cw_ref[1], cb_ref[1])
            base = 0 if first else pl.multiple_of(r * C, C)
            act_ref[pl.ds(base, C), :] = (gate * _sigmoid(gate) * val).astype(act_ref.dtype)

        chunk(0, True)

        @pl.loop(1, S // C)
        def _(r):
            chunk(r, False)

    return pl.pallas_call(
        body, name="gate_fwd", out_shape=jax.ShapeDtypeStruct((T, D_FF), MXU_DTYPE), grid=(B, N_FF_TILES),
        in_specs=[pl.BlockSpec((2, S, W), lambda b, j: (0, b, j)), pl.BlockSpec((2, 3, W), lambda b, j: (0, 0, j)),
                  pl.BlockSpec((2, 1, W), lambda b, j: (0, 0, j))],
        out_specs=pl.BlockSpec((S, W), lambda b, j: (b, j)),
        compiler_params=_params(("parallel", "parallel"), 6 * _nbytes((S, W), F32)),
    )(up3, conv_w, conv_b)


def _gate_bwd(up3, dact, conv_w, conv_b, B, S):
    T = B * S
    W = FF_TILE
    C = GATE_ROWS

    def body(up_ref, da_ref, cw_ref, cb_ref, dup_ref, gcw_ref, gcb_ref, d_s):
        @pl.when(pl.program_id(1) == 0)
        def _():
            gcw_ref[...] = jnp.zeros_like(gcw_ref)
            gcb_ref[...] = jnp.zeros_like(gcb_ref)

        def chunk(r, first, sums):
            rows = pl.ds(0 if first else pl.multiple_of(r * C, C), C)
            taps = [_taps(up_ref, half, r, first) for half in (0, 1)]
            gate = _conv_taps(taps[0], cw_ref[0], cb_ref[0])
            val = _conv_taps(taps[1], cw_ref[1], cb_ref[1])
            sg = _sigmoid(gate)
            da = da_ref[rows, :]
            d_halves = (da * val * (sg * (1.0 + gate * (1.0 - sg))), da * (gate * sg))
            out = []
            for half, dup in enumerate(d_halves):
                d_s[half, rows, :] = dup
                x0, x1, x2 = taps[half]
                sb, s0, s1, s2 = sums[half]
                out.append((sb + _fold8(dup), s0 + _fold8(dup * x2), s1 + _fold8(dup * x1), s2 + _fold8(dup * x0)))
            return tuple(out)

        zeros = tuple(tuple(jnp.zeros((8, W), F32) for _ in range(4)) for _ in range(2))
        sums = chunk(0, True, zeros)
        sums = lax.fori_loop(1, S // C, lambda r, s: chunk(r, False, s), sums)
        for half in (0, 1):
            sb, s0, s1, s2 = sums[half]
            gcb_ref[half] += jnp.sum(sb, axis=0, keepdims=True)
            gcw_ref[half, 0:1, :] += jnp.sum(s0, axis=0, keepdims=True)
            gcw_ref[half, 1:2, :] += jnp.sum(s1, axis=0, keepdims=True)
            gcw_ref[half, 2:3, :] += jnp.sum(s2, axis=0, keepdims=True)

        d_s[:, S:S + HALO, :] = jnp.zeros((2, HALO, W), F32)

        @pl.loop(0, S // C)
        def _(r):
            base = pl.multiple_of(r * C, C)
            for half in (0, 1):
                ds_ = d_s[half, pl.ds(base, C + HALO), :]
                cw = cw_ref[half]
                dx = (cw[2:3, :] * ds_[:C, :] + cw[1:2, :] * pltpu.roll(ds_, C + HALO - 1, 0)[:C, :]
                      + cw[0:1, :] * pltpu.roll(ds_, C + HALO - 2, 0)[:C, :])
                dup_ref[half, pl.ds(base, C), :] = dx.astype(dup_ref.dtype)

    up_spec = pl.BlockSpec((2, S, W), lambda j, b: (0, b, j))
    cw_spec = pl.BlockSpec((2, 3, W), lambda j, b: (0, 0, j))
    cb_spec = pl.BlockSpec((2, 1, W), lambda j, b: (0, 0, j))
    return pl.pallas_call(
        body, name="gate_bwd",
        out_shape=(jax.ShapeDtypeStruct((2, T, D_FF), MXU_DTYPE), jax.ShapeDtypeStruct((2, 3, D_FF), F32),
                   jax.ShapeDtypeStruct((2, 1, D_FF), F32)),
        grid=(N_FF_TILES, B),
        in_specs=[up_spec, pl.BlockSpec((S, W), lambda j, b: (b, j)), cw_spec, cb_spec],
        out_specs=(up_spec, cw_spec, cb_spec),
        scratch_shapes=[pltpu.VMEM((2, S + HALO, W), F32)],
        compiler_params=_params(("parallel", "arbitrary"), 10 * _nbytes((S, W), F32)),
    )(up3, dact, conv_w, conv_b)


def _final(x2, tgt, g, tr=512):
    T, D = x2.shape

    def body(x_ref, t_ref, g_ref, dx_ref, loss_ref, gg_ref):
        @pl.when(pl.program_id(0) == 0)
        def _():
            loss_ref[...] = jnp.zeros_like(loss_ref)
            gg_ref[...] = jnp.zeros_like(gg_ref)

        xv = x_ref[...]
        gv = g_ref[...]
        r = lax.rsqrt(jnp.mean(xv * xv, axis=-1, keepdims=True) + EPS)
        xn = xv * r
        err = xn * gv - t_ref[...]
        loss_ref[...] += 0.5 * jnp.sum(jnp.mean(err * err, axis=-1, keepdims=True), axis=0, keepdims=True)
        dy = err * (1.0 / D)
        gg_ref[...] += jnp.sum(dy * xn, axis=0, keepdims=True)
        dxn = dy * gv
        dx_ref[...] = r * (dxn - xn * jnp.mean(dxn * xn, axis=-1, keepdims=True))

    row = pl.BlockSpec((tr, D), lambda i: (i, 0))
    vec = pl.BlockSpec((1, D), lambda i: (0, 0))
    return pl.pallas_call(
        body, name="final_loss",
        out_shape=(jax.ShapeDtypeStruct((T, D), F32), jax.ShapeDtypeStruct((1, 128), F32), jax.ShapeDtypeStruct((1, D), F32)),
        grid=(T // tr,), in_specs=[row, row, vec],
        out_specs=(row, pl.BlockSpec((1, 128), lambda i: (0, 0)), vec),
        compiler_params=_params(("arbitrary",), 6 * _nbytes((tr, D), F32)),
    )(x2, tgt, g)


def _sum_slabs(parts, name, tr):
    rows, cols = parts[0].shape
    n = len(parts)

    def body(*refs):
        acc = refs[0][...]
        for r in refs[1:n]:
            acc = acc + r[...]
        refs[n][...] = acc

    blk = pl.BlockSpec((tr, cols), lambda i: (i, 0))
    return pl.pallas_call(
        body, name=name, out_shape=jax.ShapeDtypeStruct((rows, cols), F32), grid=(rows // tr,),
        in_specs=[blk] * n, out_specs=blk,
        compiler_params=_params(("parallel",), (n + 1) * _nbytes((tr, cols), F32)),
    )(*parts)


ADAMW_BLOCK_BYTES = 2400 * 1024


def _adamw(w, g, m, v, name):
    lead = w.ndim == 3
    rows, cols = w.shape[-2:]
    fits = [d for d in range(8, rows + 1, 8) if rows % d == 0 and d * cols * 4 <= ADAMW_BLOCK_BYTES]
    tr = max(fits) if fits else rows
    c1 = 1.0 - ADAM_B1 ** ADAM_STEP
    c2 = 1.0 - ADAM_B2 ** ADAM_STEP

    def body(w_ref, g_ref, m_ref, v_ref, d_ref, nm_ref, nv_ref):
        gv = g_ref[...]
        nm = ADAM_B1 * m_ref[...] + (1.0 - ADAM_B1) * gv
        nv = ADAM_B2 * v_ref[...] + (1.0 - ADAM_B2) * (gv * gv)
        nm_ref[...] = nm
        nv_ref[...] = nv
        d_ref[...] = -ADAM_LR * ((nm / c1) / (jnp.sqrt(nv / c2) + ADAM_EPS) + ADAM_WD * w_ref[...])

    blk = pl.BlockSpec((None, tr, cols), lambda i: (0, i, 0)) if lead else pl.BlockSpec((tr, cols), lambda i: (i, 0))
    sds = jax.ShapeDtypeStruct(w.shape, F32)
    return pl.pallas_call(
        body, name=name, out_shape=(sds, sds, sds), grid=(rows // tr,), in_specs=[blk] * 4, out_specs=(blk, blk, blk),
        compiler_params=_params(("parallel",), 7 * _nbytes((tr, cols), F32)),
    )(w, g, m, v)


_ANY = pl.BlockSpec(memory_space=pl.ANY)


def _place():
    x, y, c = lax.axis_index("x"), lax.axis_index("y"), lax.axis_index("c")
    chips = [(1 - x, y), (x, 1 - y), (1 - x, 1 - y)]
    return x, y, c, chips


def _forward_halves(lands):
    n = len(lands)

    def body(*refs):
        outs, send, recv = refs[n:2 * n], refs[2 * n], refs[2 * n + 1]
        x, y, c, chips = _place()
        cps = []
        for w in range(n):
            for j, (px, py) in enumerate(chips):
                landed = outs[w].at[2 * px + py, c]
                cps.append(pltpu.make_async_remote_copy(
                    src_ref=landed, dst_ref=landed, send_sem=send.at[3 * w + j], recv_sem=recv.at[3 * w + j],
                    device_id=(x, y, 1 - c), device_id_type=MESH))
        for cp in cps:
            cp.start()
        for w in range(n):
            for j, (px, py) in enumerate(chips):
                other = outs[w].at[2 * px + py, 1 - c]
                pltpu.make_async_remote_copy(src_ref=other, dst_ref=other, send_sem=send.at[3 * w + j],
                                             recv_sem=recv.at[3 * w + j], device_id=(x, y, 1 - c),
                                             device_id_type=MESH).wait_recv()
        for cp in cps:
            cp.wait_send()

    dma = lambda k: pltpu.SemaphoreType.DMA((k,))
    return pl.pallas_call(
        body, name="gather_forward_halves", out_shape=tuple(jax.ShapeDtypeStruct(a.shape, a.dtype) for a in lands),
        in_specs=[_ANY] * n, out_specs=tuple([_ANY] * n), input_output_aliases={w: w for w in range(n)},
        scratch_shapes=[dma(3 * n), dma(3 * n)],
    )(*lands)


_HBM = pl.BlockSpec(memory_space=pltpu.HBM)
_SEM = pl.BlockSpec(memory_space=pltpu.SEMAPHORE)
_EFFECT = pltpu.SideEffectType.DATAFLOW_SIDE_EFFECTING


SEMS_PER_ARRAY = 8


def _exchange_copies(srcs, lands, send, recv, mode):
    x, y, c, chips = _place()
    if mode == "halves":
        cps = []
        for w, (src, land) in enumerate(zip(srcs, lands)):
            pieces = [(src.at[c], land.at[2 * x + y, c], (px, py, c)) for px, py in chips]
            pieces.append((src, land.at[2 * x + y], (x, y, 1 - c)))
            for k, (piece, dst, peer) in enumerate(pieces):
                cps.append(pltpu.make_async_remote_copy(
                    src_ref=piece, dst_ref=dst, send_sem=send.at[SEMS_PER_ARRAY * w + k],
                    recv_sem=recv.at[SEMS_PER_ARRAY * w + k], device_id=peer, device_id_type=MESH))
        return cps
    if mode == "swap":
        return [pltpu.make_async_remote_copy(
            src_ref=src.at[:, 1 - c], dst_ref=land, send_sem=send.at[SEMS_PER_ARRAY * w],
            recv_sem=recv.at[SEMS_PER_ARRAY * w], device_id=(x, y, 1 - c), device_id_type=MESH)
            for w, (src, land) in enumerate(zip(srcs, lands))]
    if mode == "all":
        flips = [(fx, fy, fc) for fx in (0, 1) for fy in (0, 1) for fc in (0, 1)][1:]
        peers = [(x ^ fx, y ^ fy, c ^ fc) for fx, fy, fc in flips]
        slot = 4 * x + 2 * y + c
    else:
        peers = [(px, py, c) for px, py in chips] + ([(x, y, 1 - c)] if mode == "gather" else [])
        slot = 2 * x + y
    cps = []
    for w, (src, land) in enumerate(zip(srcs, lands)):
        for k, peer in enumerate(peers):
            piece = src.at[2 * peer[0] + peer[1]] if mode == "scatter" else src
            cps.append(pltpu.make_async_remote_copy(
                src_ref=piece, dst_ref=land.at[slot], send_sem=send.at[SEMS_PER_ARRAY * w + k],
                recv_sem=recv.at[SEMS_PER_ARRAY * w + k], device_id=peer, device_id_type=MESH))
    return cps


def _exchange_start(srcs, name, mode, after):
    n = len(srcs)
    if mode == "swap":
        land_shapes = [(s.shape[0],) + s.shape[2:] for s in srcs]
    else:
        lead = {"gather": (N_CHIPS,), "halves": (N_CHIPS,), "scatter": (), "all": (2 * N_CHIPS,)}[mode]
        land_shapes = [lead + s.shape for s in srcs]

    def body(*refs):
        src_refs, land_refs = refs[:n], refs[n:2 * n]
        send, recv = refs[2 * n + 1], refs[2 * n + 2]
        token = refs[-1]
        for cp in _exchange_copies(src_refs, land_refs, send, recv, mode):
            cp.start()
        token[...] = jnp.zeros_like(token)

    sems = pltpu.SemaphoreType.DMA((SEMS_PER_ARRAY * n,))
    out = pl.pallas_call(
        body, name=name,
        out_shape=(sems, sems, *[pltpu.HBM(s.shape, s.dtype) for s in srcs],
                   *[pltpu.HBM(shp, s.dtype) for shp, s in zip(land_shapes, srcs)], jax.ShapeDtypeStruct((8, 128), F32)),
        in_specs=[_HBM] * (2 * n) + [_ANY],
        out_specs=(_SEM, _SEM, *[_HBM] * (2 * n), pl.BlockSpec(memory_space=pltpu.VMEM)),
        input_output_aliases={i: 2 + i for i in range(2 * n)},
        compiler_params=pltpu.CompilerParams(has_side_effects=_EFFECT),
    )(*[pltpu.with_memory_space_constraint(s, pltpu.HBM) for s in srcs],
      *[pltpu.with_memory_space_constraint(lax.empty(shp, s.dtype), pltpu.HBM) for shp, s in zip(land_shapes, srcs)],
      after)
    return out[0], out[1], out[2:2 + n], out[2 + n:2 + 2 * n], out[-1]


def _exchange_wait(started, name, mode, after):
    send, recv, src_thru, land_thru, _ = started
    n = len(src_thru)
    after = list(after) if isinstance(after, (list, tuple)) else [after]

    def body(*refs):
        src_refs, land_refs, send_ref, recv_ref = refs[:n], refs[n:2 * n], refs[2 * n], refs[2 * n + 1]
        for cp in _exchange_copies(src_refs, land_refs, send_ref, recv_ref, mode):
            cp.wait_send()
            cp.wait_recv()

    out = pl.pallas_call(
        body, name=name,
        out_shape=tuple(pltpu.HBM(a.shape, a.dtype) for a in list(src_thru) + list(land_thru)),
        in_specs=[_HBM] * (2 * n) + [_SEM, _SEM] + [_ANY] * len(after), out_specs=tuple([_HBM] * (2 * n)),
        input_output_aliases={i: i for i in range(2 * n)},
        compiler_params=pltpu.CompilerParams(has_side_effects=_EFFECT),
    )(*src_thru, *land_thru, send, recv, *after)
    return out[:n], out[n:]


def _swap_halves(gs, name):
    n = len(gs)

    def body(*refs):
        ins, outs, send, recv = refs[:n], refs[n:2 * n], refs[2 * n], refs[2 * n + 1]
        x, y, c, _ = _place()
        cps = []
        for w in range(n):
            cps.append(pltpu.make_async_remote_copy(
                src_ref=ins[w].at[:, 1 - c], dst_ref=outs[w], send_sem=send.at[w], recv_sem=recv.at[w],
                device_id=(x, y, 1 - c), device_id_type=MESH))
        for cp in cps:
            cp.start()
        for cp in cps:
            cp.wait()

    return pl.pallas_call(
        body, name=name,
        out_shape=tuple(jax.ShapeDtypeStruct((g.shape[0],) + g.shape[2:], g.dtype) for g in gs),
        in_specs=[_ANY] * n, out_specs=tuple([_ANY] * n),
        scratch_shapes=[pltpu.SemaphoreType.DMA((n,)), pltpu.SemaphoreType.DMA((n,))],
    )(*gs)


GRAD_PAYLOAD = jnp.bfloat16


def _half_blocks(half_rows, cols):
    if (half_rows // 2) % 16 == 0:
        return (half_rows // 2, cols), (lambda r: (r, 0))
    assert cols % 256 == 0, (half_rows, cols)
    return (half_rows, cols // 2), (lambda r: (0, r))


def _pair_sum(gs, gots, name):
    n = len(gs)
    core = lax.axis_index("c").astype(jnp.int32).reshape(1)

    def body(core_ref, *refs):
        del core_ref
        for w in range(n):
            refs[2 * n + w][...] = (refs[w][...] + refs[n + w][...]).astype(GRAD_PAYLOAD)

    in_specs, out_specs, out_shape, nbytes = [], [], [], 0
    cuts = [_half_blocks(g.shape[1] // 2, g.shape[2]) for g in gs]
    for g, ((br, bc), at) in zip(gs, cuts):
        per_half = (g.shape[1] // 2) // br
        in_specs.append(pl.BlockSpec((1, br, bc), lambda s, r, core, at=at, per_half=per_half:
                                     (s, per_half * core[0] + at(r)[0], at(r)[1])))
        nbytes += 3 * _nbytes((br, bc), F32)
    for g, ((br, bc), at) in zip(gs, cuts):
        in_specs.append(pl.BlockSpec((1, br, bc), lambda s, r, core, at=at: (s,) + at(r)))
        out_specs.append(pl.BlockSpec((1, br, bc), lambda s, r, core, at=at: (s,) + at(r)))
        out_shape.append(jax.ShapeDtypeStruct((g.shape[0], g.shape[1] // 2, g.shape[2]), GRAD_PAYLOAD))
    return pl.pallas_call(
        body, name=name, out_shape=tuple(out_shape),
        grid_spec=pltpu.PrefetchScalarGridSpec(num_scalar_prefetch=1, grid=(N_CHIPS, 2), in_specs=in_specs,
                                               out_specs=tuple(out_specs)),
        compiler_params=_params(("parallel", "parallel"), nbytes),
    )(core, *gs, *gots)


def _chip_sum(ps, landed):
    n = len(ps)
    x, y, c = lax.axis_index("x"), lax.axis_index("y"), lax.axis_index("c")
    where = jnp.stack([2 * x + y, 2 * (1 - x) + y, 2 * x + (1 - y), 2 * (1 - x) + (1 - y), c]).astype(jnp.int32)

    def body(where_ref, *refs):
        del where_ref
        for w in range(n):
            terms = [refs[4 * w + t][...].astype(F32) for t in range(4)]
            refs[4 * n + w][...] = ((terms[0] + terms[1]) + terms[2]) + terms[3]

    in_specs, out_specs, out_shape, args, nbytes = [], [], [], [], 0
    for p, a in zip(ps, landed):
        (br, bc), at = _half_blocks(a.shape[1], a.shape[2])
        blk = (1, br, bc)
        in_specs.append(pl.BlockSpec(blk, lambda r, where, at=at: (where[0],) + at(r)))
        args.append(p)
        for t in (1, 2, 3):
            in_specs.append(pl.BlockSpec(blk, lambda r, where, t=t, at=at: (where[t],) + at(r)))
            args.append(a)
        out_specs.append(pl.BlockSpec(blk, lambda r, where, at=at: (where[4],) + at(r)))
        out_shape.append(jax.ShapeDtypeStruct((2,) + a.shape[1:], F32))
        nbytes += 4 * _nbytes(blk, F32)
    return pl.pallas_call(
        body, name="grad_chip_sum", out_shape=tuple(out_shape),
        grid_spec=pltpu.PrefetchScalarGridSpec(num_scalar_prefetch=1, grid=(2,), in_specs=in_specs,
                                               out_specs=tuple(out_specs)),
        compiler_params=_params(("parallel",), nbytes),
    )(where, *args)


def _join_halves(ss):
    n = len(ss)

    def body(*refs):
        outs, send, recv = refs[n:2 * n], refs[2 * n], refs[2 * n + 1]
        x, y, c, _ = _place()
        cps = []
        for w in range(n):
            cps.append(pltpu.make_async_remote_copy(
                src_ref=outs[w].at[c], dst_ref=outs[w].at[c], send_sem=send.at[w], recv_sem=recv.at[w],
                device_id=(x, y, 1 - c), device_id_type=MESH))
        for cp in cps:
            cp.start()
        for w in range(n):
            got = outs[w].at[1 - c]
            pltpu.make_async_remote_copy(src_ref=got, dst_ref=got, send_sem=send.at[w], recv_sem=recv.at[w],
                                         device_id=(x, y, 1 - c), device_id_type=MESH).wait_recv()
        for cp in cps:
            cp.wait_send()

    dma = lambda k: pltpu.SemaphoreType.DMA((k,))
    return pl.pallas_call(
        body, name="grad_join_halves",
        out_shape=tuple(jax.ShapeDtypeStruct(s.shape, s.dtype) for s in ss),
        in_specs=[_ANY] * n, out_specs=tuple([_ANY] * n), input_output_aliases={w: w for w in range(n)},
        scratch_shapes=[dma(n), dma(n)],
    )(*ss)


def _rot_cols(w, axis=-1):
    a, b = jnp.split(w, 2, axis=axis)
    return jnp.concatenate([-b, a], axis=axis)


def _rot_cols_t(g, axis=-1):
    a, b = jnp.split(g, 2, axis=axis)
    return jnp.concatenate([b, -a], axis=axis)


def _cols_from_chips(a):
    n, r, cs = a.shape
    return jnp.transpose(a, (1, 0, 2)).reshape(r, n * cs)


def _cols_to_chips(a):
    r, cc = a.shape
    return jnp.transpose(a.reshape(r, N_CHIPS, cc // N_CHIPS), (1, 0, 2))


def _conv_w_split(cw):
    return jnp.swapaxes(cw.reshape(3, 2, D_FF), 0, 1)


def _conv_w_join(g):
    return jnp.swapaxes(g, 0, 1).reshape(3, 2 * D_FF)


_SEG =(D_MODEL, 2 * D_MODEL, 2 * D_MODEL + Q_RANK, 2 * D_MODEL + Q_RANK + KV_RANK, 2 * D_MODEL + Q_RANK + KV_RANK + ROPE,
        3 * D_MODEL + Q_RANK + KV_RANK + ROPE)


def _w_in_t_to_pad(wt):
    u, v, cq, ckv, kr, ga, gb = jnp.split(wt, _SEG, axis=0)
    return jnp.concatenate([u, v, ga, gb, cq, ckv, kr, _rot_cols(kr, axis=0)], axis=0)


def _w_in_t_from_pad(gt):
    u, v, ga, gb, cq, ckv, kr, krr = jnp.split(
        gt, (D_MODEL, 2 * D_MODEL, 3 * D_MODEL, 4 * D_MODEL, 4 * D_MODEL + Q_RANK, 4 * D_MODEL + Q_RANK + KV_RANK,
             4 * D_MODEL + Q_RANK + KV_RANK + ROPE), axis=0)
    return jnp.concatenate([u, v, cq, ckv, kr + _rot_cols_t(krr, axis=0), ga, gb], axis=0)


def _w_uq_to_pad(w):
    t = w.reshape(Q_RANK, HEADS, QK_DIM)
    nope, rope = t[..., :NOPE], t[..., NOPE:]
    return jnp.concatenate([nope, rope, _rot_cols(rope)], axis=-1).reshape(Q_RANK, HEADS * HEAD_PAD)


def _w_uq_from_pad(g):
    t = g.reshape(Q_RANK, HEADS, HEAD_PAD)
    nope, rope, rot = t[..., :NOPE], t[..., NOPE:QK_DIM], t[..., QK_DIM:]
    return jnp.concatenate([nope, rope + _rot_cols_t(rot)], axis=-1).reshape(Q_RANK, HEADS * QK_DIM)


def _w_ukv_to_pad(w):
    t = w.reshape(KV_RANK, HEADS, 2, NOPE)
    return jnp.swapaxes(t, 1, 2).reshape(KV_RANK, 2 * HEADS * NOPE)


def _w_ukv_from_pad(g):
    t = g.reshape(KV_RANK, 2, HEADS, NOPE)
    return jnp.swapaxes(t, 1, 2).reshape(KV_RANK, 2 * HEADS * NOPE)


def _rope_tables(positions):
    inv_freq = 1.0 / (ROPE_THETA ** (jnp.arange(0, ROPE, 2, dtype=F32) / ROPE))
    ang = positions.astype(F32).reshape(-1, 1) * inv_freq
    cos, sin = jnp.cos(ang), jnp.sin(ang)
    zero = jnp.zeros((ang.shape[0], 64), F32)
    return jnp.concatenate([cos, cos, zero], axis=1), jnp.concatenate([sin, sin, zero], axis=1)


_BIG = ("w_in", "w_uq", "w_ukv", "w_out", "w_up", "w_down")
UP_SHARD = 2 * D_FF // N_CHIPS


def _local_step(x, positions, tgt, wts, in_weights, mixer_weights, ffn_weights, on_ffn_grads, on_mixer_grads):
    B, S, D = x.shape
    T = B * S
    xf = x.reshape(T, D)
    cos_a, sin_a = _rope_tables(positions)
    bs_t = jnp.pad(wts["a_spatial_b"].T, ((0, 0), (0, 128 - A_GROUPS)))

    h = _rms_fwd(xf, wts["mix_norm"], "norm1_fwd")
    wts = dict(wts)
    wts["w_in"], token = in_weights([h, cos_a, sin_a])
    z = _mm(h, wts["w_in"], "nt", "in_proj", tm=512, tn=1536, tk=D, n_outer=True, after=token)
    wts["w_q"], wts["w_kv"], wts["w_out"] = mixer_weights(z)
    q, k, v, cqn, ckvn = _lat_fwd(z, wts["q_a_norm"], wts["kv_a_norm"], wts["w_q"], wts["w_kv"], cos_a, sin_a)
    yb, *lses = _attn_fwd(q, k, v, B, S)
    merged = _mix_fwd(z, yb, wts["a_v_norm_g"], wts["a_v_norm_b"], wts["a_spatial_w"], bs_t)
    x1 = _mm(merged, wts["w_out"], "nn", "out_proj", tm=512, tn=D, tk=D, add=xf)
    h2 = _rms_fwd(x1, wts["ffn_norm"], "norm2_fwd")
    wts["w_up"], wts["w_down"], wts["conv_w"] = ffn_weights(h2)
    up_pre = _mm(h2, wts["w_up"], "nn", "up_proj", tm=512, tn=UP_SHARD, tk=D, dims=(T, 2 * D_FF, D),
                 b_spec=pl.BlockSpec((None, D, UP_SHARD), lambda i, j, k: (j, 0, 0)),
                 o_spec=pl.BlockSpec((None, 512, UP_SHARD), lambda i, j, k: (j // 2, i, j % 2)), out_shape=(2, T, D_FF),
                 n_outer=True)
    act = _gate_fwd(up_pre, wts["conv_w"], wts["conv_b"], B, S)
    x2 = _mm(act, wts["w_down"], "nn", "down_proj", tm=512, tn=D, tk=1408, add=x1)
    dx2, loss_row, g_final = _final(x2, tgt.reshape(T, D), wts["final_norm"])

    g = {"final_norm": g_final}
    dact = _mm(dx2, wts["w_down"], "nt", "down_proj_dx", tm=512, tn=1408, tk=D, n_outer=True)
    tk2, tk1 = min(2048, T), min(1024, T)
    g["w_down"], g["w_down_lo"] = _mm(act, dx2, "tn", "down_proj_dw", tm=1408, tn=D, tk=tk1, copy_dtype=GRAD_PAYLOAD)
    dup, g["conv_w"], g["conv_b"] = _gate_bwd(up_pre, dact, wts["conv_w"], wts["conv_b"], B, S)
    g["w_up"], g["w_up_lo"] = _mm(
        h2, dup, "tn", "up_proj_dw", tm=D, tn=UP_SHARD, tk=tk2, dims=(D, 2 * D_FF, T), copy_dtype=GRAD_PAYLOAD,
        b_spec=pl.BlockSpec((None, tk2, UP_SHARD), lambda i, j, k: (j // 2, k, j % 2)),
        o_spec=pl.BlockSpec((None, D, UP_SHARD), lambda i, j, k: (j, 0, 0)), out_shape=(N_CHIPS, D, UP_SHARD))
    token, ffn_sent = on_ffn_grads(g)
    dh2 = _mm(dup, wts["w_up"], "nt", "up_proj_dx", tm=512, tn=D, tk=UP_SHARD, dims=(T, D, 2 * D_FF), after=token,
              a_spec=pl.BlockSpec((None, 512, UP_SHARD), lambda i, j, k: (k // 2, i, k % 2)),
              b_spec=pl.BlockSpec((None, D, UP_SHARD), lambda i, j, k: (k, 0, 0)))
    token = ffn_sent(dh2)
    dx1, g["ffn_norm"] = _rms_bwd(x1, wts["ffn_norm"], dh2, dx2, "norm2_bwd")
    dm = _mm(dx1, wts["w_out"], "nt", "out_proj_dx", tm=512, tn=D, tk=D, after=token)
    g["w_out"], g["w_out_lo"] = _mm(merged, dx1, "tn", "out_proj_dw", tm=D, tn=D, tk=tk1, copy_dtype=GRAD_PAYLOAD)
    dz, dyb, dl, g["a_spatial_w"], gbs, g["a_v_norm_g"], g["a_v_norm_b"] = _mix_bwd(
        z, yb, dm, wts["a_v_norm_g"], wts["a_v_norm_b"], wts["a_spatial_w"], bs_t)
    g["a_spatial_b"] = gbs[:, :A_GROUPS].T
    delta = dl.reshape(HEADS * T // ATT_BLOCK, 1, ATT_BLOCK)
    dq, dk, dv = _attn_bwd(q, k, v, dyb, lses, delta, B, S)
    dz, dq_raw, dkv, g["q_a_norm"], g["kv_a_norm"] = _lat_bwd(
        dz, z, dq, dk, dv, wts["q_a_norm"], wts["kv_a_norm"], wts["w_q"], wts["w_kv"], cos_a, sin_a)
    g["w_q"] = _mm(cqn, dq_raw, "tn", "q_proj_dw", tm=Q_RANK, tn=HEADS * HEAD_PAD, tk=tk2)
    g["w_kv"] = _mm(ckvn, dkv, "tn", "kv_proj_dw", tm=KV_RANK, tn=2 * HEADS * NOPE, tk=tk2)
    g["w_in"] = _mm(dz, h, "tn", "in_proj_dw", tm=1536, tn=D, tk=tk2)
    token = on_mixer_grads(g)
    dh = _mm(dz, wts["w_in"], "nn", "in_proj_dx", tm=512, tn=D, tk=1536, after=token)
    dx, g["mix_norm"] = _rms_bwd(xf, wts["mix_norm"], dh, dx1, "norm1_bwd")
    return loss_row[0, 0], dx.reshape(B, S, D), g


_SMALL = (("mix_norm", (1, D_MODEL)), ("a_v_norm_g", (1, D_MODEL)), ("a_v_norm_b", (1, D_MODEL)),
          ("a_spatial_w", (A_GROUPS * CHUNK, CHUNK)), ("a_spatial_b", (1, A_GROUPS * CHUNK)), ("q_a_norm", (1, Q_RANK)),
          ("kv_a_norm", (1, KV_RANK)), ("ffn_norm", (1, D_MODEL)), ("conv_b", (1, 2 * D_FF)), ("final_norm", (1, D_MODEL)),
          ("conv_w", (3, 2 * D_FF)))
_SMALL_SIZE = sum(math.prod(s) for _, s in _SMALL)
_SMALL_ROWS = -(-(_SMALL_SIZE + 1) // (128 * 8)) * 8


def kernel(x, positions, mix_norm, w_in, a_v_norm_g, a_v_norm_b, a_spatial_w, a_spatial_b, q_a_norm, w_uq, kv_a_norm, w_ukv, w_out, ffn_norm, w_up, conv_w, conv_b, w_down, final_norm, loss_target, m_mix_norm, m_w_in, m_a_v_norm_g, m_a_v_norm_b, m_a_spatial_w, m_a_spatial_b, m_q_a_norm, m_w_uq, m_kv_a_norm, m_w_ukv, m_w_out, m_ffn_norm, m_w_up, m_conv_w, m_conv_b, m_w_down, m_final_norm, v_mix_norm, v_w_in, v_a_v_norm_g, v_a_v_norm_b, v_a_spatial_w, v_a_spatial_b, v_q_a_norm, v_w_uq, v_kv_a_norm, v_w_ukv, v_w_out, v_ffn_norm, v_w_up, v_conv_w, v_conv_b, v_w_down, v_final_norm):
    weights = dict(mix_norm=mix_norm, w_in=w_in, a_v_norm_g=a_v_norm_g, a_v_norm_b=a_v_norm_b, a_spatial_w=a_spatial_w,
                   a_spatial_b=a_spatial_b, q_a_norm=q_a_norm, w_uq=w_uq, kv_a_norm=kv_a_norm, w_ukv=w_ukv, w_out=w_out,
                   ffn_norm=ffn_norm, w_up=w_up, conv_w=conv_w, conv_b=conv_b, w_down=w_down, final_norm=final_norm)
    m_in = dict(mix_norm=m_mix_norm, w_in=m_w_in, a_v_norm_g=m_a_v_norm_g, a_v_norm_b=m_a_v_norm_b,
                a_spatial_w=m_a_spatial_w, a_spatial_b=m_a_spatial_b, q_a_norm=m_q_a_norm, w_uq=m_w_uq,
                kv_a_norm=m_kv_a_norm, w_ukv=m_w_ukv, w_out=m_w_out, ffn_norm=m_ffn_norm, w_up=m_w_up, conv_w=m_conv_w,
                conv_b=m_conv_b, w_down=m_w_down, final_norm=m_final_norm)
    v_in = dict(mix_norm=v_mix_norm, w_in=v_w_in, a_v_norm_g=v_a_v_norm_g, a_v_norm_b=v_a_v_norm_b,
                a_spatial_w=v_a_spatial_w, a_spatial_b=v_a_spatial_b, q_a_norm=v_q_a_norm, w_uq=v_w_uq,
                kv_a_norm=v_kv_a_norm, w_ukv=v_w_ukv, w_out=v_w_out, ffn_norm=v_ffn_norm, w_up=v_w_up, conv_w=v_conv_w,
                conv_b=v_conv_b, w_down=v_w_down, final_norm=v_final_norm)
    names = list(weights)
    chip = 2 * lax.axis_index("x") + lax.axis_index("y")

    def halves(a):
        return a.reshape(a.shape[:-2] + (2, a.shape[-2] // 2, a.shape[-1]))

    w_in_t = jnp.swapaxes(w_in[0], 0, 1).astype(MXU_DTYPE)
    w_in_gather = _exchange_start([jnp.stack(jnp.split(w_in_t, 2, axis=1))], "w_in_gather_start", "halves",
                                  after=positions)
    gathers = {}
    wts = dict(
        mix_norm=mix_norm, a_v_norm_g=a_v_norm_g, a_v_norm_b=a_v_norm_b, a_spatial_w=a_spatial_w[0],
        a_spatial_b=a_spatial_b[0], q_a_norm=q_a_norm, kv_a_norm=kv_a_norm, ffn_norm=ffn_norm,
        final_norm=final_norm.reshape(1, D_MODEL), conv_b=conv_b.reshape(2, 1, D_FF))

    mixer_shards = [weights[n][0].astype(MXU_DTYPE) for n in _BIG[1:4]]
    ffn_shards = [w_up[0].astype(MXU_DTYPE), w_down[0].astype(MXU_DTYPE)]

    def in_weights(after):
        _, landed = _exchange_wait(w_in_gather, "w_in_gather_wait", "halves", list(after) + mixer_shards + ffn_shards)
        (w_in_sh,) = _forward_halves(list(landed))
        gathers["mixer"] = _exchange_start(mixer_shards, "mixer_gather_start", "gather", after=w_in_sh)
        gathers["ffn"] = _exchange_start(ffn_shards + [conv_w[0]], "ffn_gather_start", "gather",
                                         after=gathers["mixer"][4])
        w_in_pad = _w_in_t_to_pad(jnp.concatenate([w_in_sh[:, 0], w_in_sh[:, 1]], axis=-1).reshape(-1, D_MODEL))
        return w_in_pad, gathers["ffn"][4]

    def mixer_weights(after):
        _, (w_uq_sh, w_ukv_sh, w_out_sh) = _exchange_wait(gathers["mixer"], "mixer_gather_wait", "gather", after)
        return (_w_uq_to_pad(_cols_from_chips(w_uq_sh)), _w_ukv_to_pad(_cols_from_chips(w_ukv_sh)),
                w_out_sh.reshape(D_MODEL, D_MODEL))

    def ffn_weights(after):
        _, (w_up_sh, w_down_sh, cw_all) = _exchange_wait(gathers["ffn"], "ffn_gather_wait", "gather", after)
        return w_up_sh, w_down_sh.reshape(D_FF, D_MODEL), _conv_w_split(_cols_from_chips(cw_all))

    scatters = {}

    def start_scatter(slabs, slabs_lo, tag):
        got = _swap_halves([halves(s) for s in slabs_lo], tag + "_grad_swap_halves")
        sums = _pair_sum(slabs, got, tag + "_grad_pair_sum")
        scatters[tag] = _exchange_start(list(sums), tag + "_scatter_start", "scatter", after=slabs[-1])
        return scatters[tag][4]

    def on_ffn_grads(g):
        slabs, slabs_lo = [[g["w_up" + lo], g["w_down" + lo].reshape(N_CHIPS, D_FF // N_CHIPS, D_MODEL)]
                           for lo in ("", "_lo")]
        swap = _exchange_start([halves(s) for s in slabs_lo], "ffn_swap_start", "swap", after=slabs[1])

        def sent(after):
            _, got = _exchange_wait(swap, "ffn_swap_wait", "swap", after)
            sums = _pair_sum(slabs, got, "ffn_grad_pair_sum")
            scatters["ffn"] = _exchange_start(list(sums), "ffn_scatter_start", "scatter", after=got[0])
            return scatters["ffn"][4]

        return swap[4], sent

    def on_mixer_grads(g):
        slabs = [_w_in_t_from_pad(g["w_in"]).reshape(N_CHIPS, -1, D_MODEL), _cols_to_chips(_w_uq_from_pad(g["w_q"])),
                 _cols_to_chips(_w_ukv_from_pad(g["w_kv"]))]
        w_out_slabs = [g["w_out" + lo].reshape(N_CHIPS, D_MODEL // N_CHIPS, D_MODEL) for lo in ("", "_lo")]
        return start_scatter(slabs + w_out_slabs[:1], [s.astype(GRAD_PAYLOAD) for s in slabs] + w_out_slabs[1:], "mixer")

    loss_part, grad_x, g = _local_step(x, positions, loss_target, wts, in_weights, mixer_weights, ffn_weights,
                                       on_ffn_grads, on_mixer_grads)

    g_small_parts = dict(g)
    g_small_parts["conv_w"] = _conv_w_join(g["conv_w"])
    g_small_parts["conv_b"] = g["conv_b"].reshape(1, 2 * D_FF)
    flat = jnp.concatenate([g_small_parts[n].reshape(-1) for n, _ in _SMALL] + [loss_part.reshape(1)])
    flat = jnp.pad(flat, (0, _SMALL_ROWS * 128 - flat.shape[0])).reshape(_SMALL_ROWS, 128)
    small_gather = _exchange_start([flat], "small_gather_start", "all", after=grad_x)

    mixer_sums, mixer_landed = _exchange_wait(scatters["mixer"], "mixer_scatter_wait", "scatter", after=small_gather[4])
    ffn_sums, ffn_landed = _exchange_wait(scatters["ffn"], "ffn_scatter_wait", "scatter", after=mixer_landed[0])
    reduced = _chip_sum(list(mixer_sums) + list(ffn_sums), list(mixer_landed) + list(ffn_landed))
    g_big = dict(zip(_BIG, _join_halves(reduced)))

    grads, deltas, new_m, new_v = {}, {}, {}, {}

    def update(n, grad):
        w = weights[n]
        shape2 = grad.shape
        d, nm, nv = _adamw(w.reshape(shape2), grad, m_in[n].reshape(shape2), v_in[n].reshape(shape2), "adamw_" + n)
        grads[n], deltas[n], new_m[n], new_v[n] = (t.reshape(w.shape) for t in (grad, d, nm, nv))

    def update_transposed(n, grad_t):
        t = lambda a: jnp.swapaxes(a, 1, 2)
        d, nm, nv = _adamw(t(weights[n]), grad_t, t(m_in[n]), t(v_in[n]), "adamw_" + n)
        grads[n], deltas[n], new_m[n], new_v[n] = t(grad_t), t(d), t(nm), t(nv)

    for n in _BIG:
        g3 = g_big[n].reshape((1, -1, g_big[n].shape[-1]))
        if n == "w_in":
            update_transposed(n, g3)
        else:
            update(n, g3)

    (own,), (everyone,) = _exchange_wait(small_gather, "small_gather_wait", "all", after=[deltas[n] for n in _BIG])
    device = 2 * chip + lax.axis_index("c")
    everyone = lax.dynamic_update_slice(everyone, own[None], (device, 0, 0))
    total = _sum_slabs([everyone[j] for j in range(8)], "small_grads_sum", tr=_SMALL_ROWS).reshape(-1)
    o = 0
    for n, shp in _SMALL:
        piece = total[o:o + math.prod(shp)].reshape(shp)
        o += math.prod(shp)
        if n == "conv_w":
            piece = lax.dynamic_slice_in_dim(piece, chip * UP_SHARD, UP_SHARD, axis=1)
        update(n, piece)
    loss = total[_SMALL_SIZE]
    return (loss, grad_x, *[grads[n] for n in names], *[deltas[n] for n in names], *[new_m[n] for n in names],
            *[new_v[n] for n in names])
```

```python
import functools
import math

import jax
import jax.numpy as jnp
from jax import lax
from jax.experimental import pallas as pl
from jax.experimental.pallas import tpu as pltpu

F32 = jnp.float32
MXU_DTYPE = jnp.bfloat16
MESH = pl.DeviceIdType.MESH

D_MODEL = 1024
EPS = 1e-6
A_GROUPS = 8
CHUNK = 128
HEADS = 8
NOPE = 128
ROPE = 64
QK_DIM = NOPE + ROPE
HEAD_PAD = 256
Q_RANK = 256
KV_RANK = 128
ROPE_THETA = 10000.0
D_FF = 2816
FF_TILE = 256
N_FF_TILES = D_FF // FF_TILE
LAT = 512
IN_PAD = 4 * D_MODEL + LAT
N_CHIPS = 4
ADAM_LR, ADAM_B1, ADAM_B2, ADAM_EPS, ADAM_WD, ADAM_STEP = 0.001, 0.9, 0.999, 1e-08, 0.01, 10

VMEM_CAP_V7X = 64 * 1024 * 1024
NEG = -1e30


def _params(sem, nbytes):
    limit = int(min(VMEM_CAP_V7X - (8 << 20), max(32 << 20, 3 * nbytes)))
    return pltpu.CompilerParams(dimension_semantics=sem, vmem_limit_bytes=limit)


def _nbytes(shape, dtype):
    return math.prod(shape) * jnp.dtype(dtype).itemsize


_DIMS = {"nn": (((1,), (0,)), ((), ())), "nt": (((1,), (1,)), ((), ())), "tn": (((0,), (0,)), ((), ()))}


def _mm(a, b, mode, name, *, tm, tn, tk, out_dtype=F32, add=None, dims=None, a_spec=None, b_spec=None,
        o_spec=None, out_shape=None, n_outer=False, copy_dtype=None, after=None):
    if dims is None:
        if mode == "nn":
            (M, K), (_, N) = a.shape, b.shape
        elif mode == "nt":
            (M, K), (N, _) = a.shape, b.shape
        else:
            (K, M), (_, N) = a.shape, b.shape
    else:
        M, N, K = dims
    a_blk = (tk, tm) if mode == "tn" else (tm, tk)
    b_blk = (tn, tk) if mode == "nt" else (tk, tn)
    if a_spec is None:
        a_spec = pl.BlockSpec(a_blk, (lambda i, j, k: (k, i)) if mode == "tn" else (lambda i, j, k: (i, k)))
    if b_spec is None:
        b_spec = pl.BlockSpec(b_blk, (lambda i, j, k: (j, k)) if mode == "nt" else (lambda i, j, k: (k, j)))
    if o_spec is None:
        o_spec = pl.BlockSpec((tm, tn), lambda i, j, k: (i, j))
    if out_shape is None:
        out_shape = (M, N)
    assert M % tm == 0 and N % tn == 0 and K % tk == 0, (name, M, N, K, tm, tn, tk)
    nk = K // tk
    contract = _DIMS[mode]
    has_add = add is not None

    def body(*refs):
        a_ref, b_ref = refs[0], refs[1]
        add_ref = refs[2] if has_add else None
        n_in = 2 + has_add + (after is not None)
        o_ref = refs[n_in]
        copy_ref = refs[n_in + 1] if copy_dtype is not None else None

        def product():
            return lax.dot_general(a_ref[...].astype(MXU_DTYPE), b_ref[...].astype(MXU_DTYPE), contract,
                                   preferred_element_type=F32)

        def finish(r):
            if has_add:
                r = r + add_ref[...]
            o_ref[...] = r.astype(out_dtype)
            if copy_ref is not None:
                copy_ref[...] = r.astype(copy_dtype)

        if nk == 1:
            finish(product())
            return
        acc = refs[-1]
        k = pl.program_id(2)

        @pl.when(k == 0)
        def _():
            acc[...] = jnp.zeros_like(acc)

        acc[...] += product()

        @pl.when(k == nk - 1)
        def _():
            finish(acc[...])

    in_specs = [a_spec, b_spec]
    args = [a, b]
    nbytes = _nbytes(a_blk, a.dtype) + _nbytes(b_blk, b.dtype) + 3 * _nbytes((tm, tn), F32)
    if has_add:
        in_specs.append(pl.BlockSpec((tm, tn), lambda i, j, k: (i, j)))
        args.append(add)
        nbytes += _nbytes((tm, tn), F32)
    if after is not None:
        in_specs.append(pl.BlockSpec(after.shape, lambda i, j, k: (0, 0)))
        args.append(after)
    grid = (M // tm, N // tn, nk)
    if n_outer:
        def swapped(spec):
            return pl.BlockSpec(spec.block_shape, lambda j, i, k, at=spec.index_map: at(i, j, k))

        grid = (N // tn, M // tm, nk)
        in_specs = [swapped(s) for s in in_specs]
        o_spec = swapped(o_spec)
    out_sds, out_specs = jax.ShapeDtypeStruct(out_shape, out_dtype), o_spec
    if copy_dtype is not None:
        out_sds, out_specs = (out_sds, jax.ShapeDtypeStruct(out_shape, copy_dtype)), (o_spec, o_spec)
    return pl.pallas_call(
        body, name=name, out_shape=out_sds, grid=grid, in_specs=in_specs, out_specs=out_specs,
        scratch_shapes=[pltpu.VMEM((tm, tn), F32)] if nk > 1 else [],
        compiler_params=_params(("parallel", "parallel", "arbitrary"), nbytes),
    )(*args)


_GELU_C = math.sqrt(2.0 / math.pi)
_GELU_A = 0.044715


def _sigmoid(x):
    return 0.5 * jnp.tanh(0.5 * x) + 0.5


def _gelu(x):
    t = jnp.tanh(x * (_GELU_C + (_GELU_C * _GELU_A) * (x * x)))
    return x * (0.5 + 0.5 * t)


def _gelu_and_grad(x):
    x2 = x * x
    t = jnp.tanh(x * (_GELU_C + (_GELU_C * _GELU_A) * x2))
    cdf = 0.5 + 0.5 * t
    grad = cdf + (0.5 * x) * (1.0 - t * t) * (_GELU_C + (3.0 * _GELU_C * _GELU_A) * x2)
    return x * cdf, grad


def _rope_mix(g, cos_a, sin_a):
    return g * cos_a + pltpu.roll(g, 64, 1) * sin_a


def _rope_mix_bwd(d, cos_a, sin_a):
    return d * cos_a + pltpu.roll(d * sin_a, 64, 1)


def _rms_fwd(x, g, name, tr=512):
    T, D = x.shape

    def body(x_ref, g_ref, h_ref):
        xv = x_ref[...]
        r = lax.rsqrt(jnp.mean(xv * xv, axis=-1, keepdims=True) + EPS)
        h_ref[...] = ((xv * r) * g_ref[...]).astype(h_ref.dtype)

    return pl.pallas_call(
        body, name=name, out_shape=jax.ShapeDtypeStruct((T, D), MXU_DTYPE), grid=(T // tr,),
        in_specs=[pl.BlockSpec((tr, D), lambda i: (i, 0)), pl.BlockSpec((1, D), lambda i: (0, 0))],
        out_specs=pl.BlockSpec((tr, D), lambda i: (i, 0)),
        compiler_params=_params(("parallel",), 3 * _nbytes((tr, D), F32)),
    )(x, g)


def _rms_bwd(x, g, dh, dres, name, tr=512):
    T, D = x.shape

    def body(x_ref, g_ref, dh_ref, dres_ref, dx_ref, gg_ref):
        @pl.when(pl.program_id(0) == 0)
        def _():
            gg_ref[...] = jnp.zeros_like(gg_ref)

        xv = x_ref[...]
        r = lax.rsqrt(jnp.mean(xv * xv, axis=-1, keepdims=True) + EPS)
        xn = xv * r
        dhv = dh_ref[...]
        dxn = dhv * g_ref[...]
        dx_ref[...] = dres_ref[...] + r * (dxn - xn * jnp.mean(dxn * xn, axis=-1, keepdims=True))
        gg_ref[...] += jnp.sum(dhv * xn, axis=0, keepdims=True)

    row = pl.BlockSpec((tr, D), lambda i: (i, 0))
    vec = pl.BlockSpec((1, D), lambda i: (0, 0))
    return pl.pallas_call(
        body, name=name,
        out_shape=(jax.ShapeDtypeStruct((T, D), F32), jax.ShapeDtypeStruct((1, D), F32)),
        grid=(T // tr,), in_specs=[row, vec, row, row], out_specs=(row, vec),
        compiler_params=_params(("arbitrary",), 6 * _nbytes((tr, D), F32)),
    )(x, g, dh, dres)


def _lat_fwd(z, gq, gkv, wq, wkv, cos_a, sin_a, tr=256):
    T = z.shape[0]
    lat_blk = (4 * D_MODEL) // LAT

    def body(z_ref, gq_ref, gkv_ref, wq_ref, wkv_ref, cos_ref, sin_ref, q_ref, k_ref, v_ref, cqn_ref, ckvn_ref):
        zl = z_ref[...]
        cos_v, sin_v = cos_ref[...], sin_ref[...]
        cq = zl[:, :Q_RANK]
        ckv = zl[:, Q_RANK:Q_RANK + KV_RANK]
        krb = zl[:, Q_RANK + KV_RANK:]
        cqn = ((cq * lax.rsqrt(jnp.mean(cq * cq, axis=-1, keepdims=True) + EPS)) * gq_ref[...]).astype(MXU_DTYPE)
        ckvn = ((ckv * lax.rsqrt(jnp.mean(ckv * ckv, axis=-1, keepdims=True) + EPS)) * gkv_ref[...]).astype(MXU_DTYPE)
        cqn_ref[...] = cqn
        ckvn_ref[...] = ckvn
        krr = _rope_mix(krb, cos_v, sin_v).astype(MXU_DTYPE)
        q = jnp.dot(cqn, wq_ref[...], preferred_element_type=F32)
        kv = jnp.dot(ckvn, wkv_ref[...], preferred_element_type=F32)
        for h in range(HEADS):
            o = h * HEAD_PAD
            q_ref[:, o:o + NOPE] = q[:, o:o + NOPE].astype(MXU_DTYPE)
            q_ref[:, o + NOPE:o + HEAD_PAD] = _rope_mix(q[:, o + NOPE:o + HEAD_PAD], cos_v, sin_v).astype(MXU_DTYPE)
            k_ref[:, o:o + NOPE] = kv[:, h * NOPE:(h + 1) * NOPE].astype(MXU_DTYPE)
            k_ref[:, o + NOPE:o + HEAD_PAD] = krr
        v_ref[...] = kv[:, HEADS * NOPE:].astype(MXU_DTYPE)

    def row(w):
        return pl.BlockSpec((tr, w), lambda i: (i, 0))

    def full(a):
        return pl.BlockSpec(a.shape, lambda i: (0, 0))

    return pl.pallas_call(
        body, name="lat_fwd",
        out_shape=(jax.ShapeDtypeStruct((T, HEADS * HEAD_PAD), MXU_DTYPE), jax.ShapeDtypeStruct((T, HEADS * HEAD_PAD), MXU_DTYPE),
                   jax.ShapeDtypeStruct((T, HEADS * NOPE), MXU_DTYPE), jax.ShapeDtypeStruct((T, Q_RANK), MXU_DTYPE),
                   jax.ShapeDtypeStruct((T, KV_RANK), MXU_DTYPE)),
        grid=(T // tr,),
        in_specs=[pl.BlockSpec((tr, LAT), lambda i: (i, lat_blk)), full(gq), full(gkv), full(wq), full(wkv), row(128), row(128)],
        out_specs=(row(HEADS * HEAD_PAD), row(HEADS * HEAD_PAD), row(HEADS * NOPE), row(Q_RANK), row(KV_RANK)),
        compiler_params=_params(("parallel",), 8 * _nbytes((tr, HEADS * HEAD_PAD), F32)),
    )(z, gq, gkv, wq, wkv, cos_a, sin_a)


ATT_BLOCK = 256
_SCALE = QK_DIM ** -0.5


def _causal_mask(n):
    return lax.broadcasted_iota(jnp.int32, (n, n), 1) <= lax.broadcasted_iota(jnp.int32, (n, n), 0)


def _causal_mask_t(n):
    return lax.broadcasted_iota(jnp.int32, (n, n), 0) <= lax.broadcasted_iota(jnp.int32, (n, n), 1)


ATT_HEADS = 4


def _attn_fwd(q, k, v, B, S):
    tq = ATT_BLOCK
    nq = S // tq
    T = B * S
    hp, groups = ATT_HEADS, HEADS // ATT_HEADS

    def body(q_ref, k_ref, v_ref, o_ref, *lse_refs):
        qi = pl.program_id(2)
        qs = [q_ref[:, t * HEAD_PAD:(t + 1) * HEAD_PAD] for t in range(hp)]

        def scores(j, t):
            rows = pl.ds(pl.multiple_of(j * tq, tq), tq)
            return lax.dot_general(k_ref[rows, t * HEAD_PAD:(t + 1) * HEAD_PAD], qs[t], _DIMS["nt"],
                                   preferred_element_type=F32)

        def step(j, carry, last):
            rows = pl.ds(pl.multiple_of(j * tq, tq), tq)
            out = []
            for t in range(hp):
                m, l, acc, st = carry[t]
                st_next = st if last else scores(j + 1, t)
                st = st * _SCALE
                if last:
                    st = jnp.where(_causal_mask_t(tq), st, NEG)
                m_new = jnp.maximum(m, jnp.max(st, axis=0, keepdims=True))
                alpha = jnp.exp(m - m_new)
                p = jnp.exp(st - m_new)
                l = alpha * l + jnp.sum(p, axis=0, keepdims=True)
                acc = alpha * acc + lax.dot_general(v_ref[rows, t * NOPE:(t + 1) * NOPE], p.astype(MXU_DTYPE),
                                                    _DIMS["tn"], preferred_element_type=F32)
                out.append((m_new, l, acc, st_next))
            return tuple(out)

        init = tuple((jnp.full((1, tq), NEG, F32), jnp.zeros((1, tq), F32), jnp.zeros((NOPE, tq), F32), scores(0, t))
                     for t in range(hp))
        carry = lax.fori_loop(0, qi, lambda j, c: step(j, c, False), init)
        carry = step(qi, carry, True)
        for t in range(hp):
            m, l, acc, _ = carry[t]
            o_ref[:, t * NOPE:(t + 1) * NOPE] = (acc / l).T
            lse_refs[t][0] = m + jnp.log(l)

    lse_sds = jax.ShapeDtypeStruct((groups * B * nq, 1, tq), F32)
    lse_spec = pl.BlockSpec((1, 1, tq), lambda b, h, i: ((h * B + b) * nq + i, 0, 0))
    return pl.pallas_call(
        body, name="attn_fwd",
        out_shape=(jax.ShapeDtypeStruct((T, HEADS * NOPE), F32),) + (lse_sds,) * hp,
        grid=(B, groups, nq),
        in_specs=[pl.BlockSpec((tq, hp * HEAD_PAD), lambda b, h, i: (b * nq + i, h)),
                  pl.BlockSpec((S, hp * HEAD_PAD), lambda b, h, i: (b, h)),
                  pl.BlockSpec((S, hp * NOPE), lambda b, h, i: (b, h))],
        out_specs=(pl.BlockSpec((tq, hp * NOPE), lambda b, h, i: (b * nq + i, h)),) + (lse_spec,) * hp,
        compiler_params=_params(("parallel", "parallel", "arbitrary"), 4 * hp * _nbytes((S, HEAD_PAD), MXU_DTYPE)),
    )(q, k, v)


def _attn_bwd(q, k, v, do, lses, delta, B, S):
    tq = ATT_BLOCK
    nq = S // tq
    T = B * S
    hp, groups = ATT_HEADS, HEADS // ATT_HEADS

    def body(q_ref, k_ref, v_ref, do_ref, *refs):
        lse_refs, dl_refs = refs[:hp], refs[hp:2 * hp]
        dq_out, dk_ref, dv_ref, dq_ref = refs[2 * hp:]
        kj = pl.program_id(2)

        @pl.when(kj == 0)
        def _():
            dq_ref[...] = jnp.zeros_like(dq_ref)

        def products(i, t):
            rows = pl.ds(pl.multiple_of(i * tq, tq), tq)
            st = lax.dot_general(k_ref[:, t * HEAD_PAD:(t + 1) * HEAD_PAD], q_ref[rows, t * HEAD_PAD:(t + 1) * HEAD_PAD],
                                 _DIMS["nt"], preferred_element_type=F32)
            dpt = lax.dot_general(v_ref[:, t * NOPE:(t + 1) * NOPE], do_ref[rows, t * NOPE:(t + 1) * NOPE],
                                  _DIMS["nt"], preferred_element_type=F32)
            return st, dpt

        def step(i, carry, masked):
            rows = pl.ds(pl.multiple_of(i * tq, tq), tq)
            nxt = jnp.minimum(i + 1, nq - 1)
            out = []
            for t in range(hp):
                dk, dv, st, dpt = carry[t]
                st_next, dpt_next = products(nxt, t)
                qk_cols = slice(t * HEAD_PAD, (t + 1) * HEAD_PAD)
                v_cols = slice(t * NOPE, (t + 1) * NOPE)
                p = jnp.exp(st * _SCALE - lse_refs[t][i])
                if masked:
                    p = jnp.where(_causal_mask_t(tq), p, 0.0)
                dv = dv + jnp.dot(p.astype(MXU_DTYPE), do_ref[rows, v_cols], preferred_element_type=F32)
                ds = (p * (dpt - dl_refs[t][i]) * _SCALE).astype(MXU_DTYPE)
                dk = dk + jnp.dot(ds, q_ref[rows, qk_cols], preferred_element_type=F32)
                dq_ref[rows, qk_cols] += lax.dot_general(ds, k_ref[:, qk_cols], _DIMS["tn"], preferred_element_type=F32)
                out.append((dk, dv, st_next, dpt_next))
            return tuple(out)

        init = tuple((jnp.zeros((tq, HEAD_PAD), F32), jnp.zeros((tq, NOPE), F32)) + products(kj, t) for t in range(hp))
        carry = step(kj, init, True)
        carry = lax.fori_loop(kj + 1, nq, lambda i, c: step(i, c, False), carry)
        for t in range(hp):
            dk_ref[:, t * HEAD_PAD:(t + 1) * HEAD_PAD] = carry[t][0].astype(dk_ref.dtype)
            dv_ref[:, t * NOPE:(t + 1) * NOPE] = carry[t][1].astype(dv_ref.dtype)

        @pl.when(kj == nq - 1)
        def _():
            dq_out[...] = dq_ref[...].astype(dq_out.dtype)

    seq = lambda w: pl.BlockSpec((S, w), lambda b, h, j: (b, h))
    blk = lambda w: pl.BlockSpec((tq, w), lambda b, h, j: (b * nq + j, h))
    lse_spec = pl.BlockSpec((nq, 1, tq), lambda b, h, j: (h * B + b, 0, 0))
    dl_specs = [pl.BlockSpec((nq, 1, tq), lambda b, h, j, t=t: ((h * hp + t) * B + b, 0, 0)) for t in range(hp)]
    return pl.pallas_call(
        body, name="attn_bwd",
        out_shape=(jax.ShapeDtypeStruct((T, HEADS * HEAD_PAD), MXU_DTYPE), jax.ShapeDtypeStruct((T, HEADS * HEAD_PAD), MXU_DTYPE),
                   jax.ShapeDtypeStruct((T, HEADS * NOPE), MXU_DTYPE)),
        grid=(B, groups, nq),
        in_specs=[seq(hp * HEAD_PAD), blk(hp * HEAD_PAD), blk(hp * NOPE), seq(hp * NOPE)] + [lse_spec] * hp + dl_specs,
        out_specs=(seq(hp * HEAD_PAD), blk(hp * HEAD_PAD), blk(hp * NOPE)),
        scratch_shapes=[pltpu.VMEM((S, hp * HEAD_PAD), F32)],
        compiler_params=_params(("parallel", "parallel", "arbitrary"), 8 * hp * _nbytes((S, HEAD_PAD), F32)),
    )(q, k, v, do, *lses, *([delta] * hp))


MIX_ROWS = 256


def _tril_weights(ws_ref, g):
    return jnp.where(_causal_mask(CHUNK), ws_ref[g], 0.0).astype(MXU_DTYPE)


def _layer_norm_stats(va):
    mu = jnp.mean(va, axis=-1, keepdims=True)
    xc = va - mu
    rs = lax.rsqrt(jnp.mean(xc * xc, axis=-1, keepdims=True) + EPS)
    return xc * rs


def _mix_specs(tr):
    zcol = lambda c: pl.BlockSpec((tr, D_MODEL), lambda i, c=c: (i, c))
    row = pl.BlockSpec((tr, D_MODEL), lambda i: (i, 0))
    vec = pl.BlockSpec((1, D_MODEL), lambda i: (0, 0))
    ws = pl.BlockSpec((A_GROUPS, CHUNK, CHUNK), lambda i: (0, 0, 0))
    bs = pl.BlockSpec((CHUNK, 128), lambda i: (0, 0))
    return zcol, row, vec, ws, bs


def _mix_fwd(z, yb, ln_g, ln_b, ws, bs_t):
    T = z.shape[0]
    tr = MIX_ROWS
    zcol, row, vec, ws_spec, bs_spec = _mix_specs(tr)

    def body(zu_ref, zv_ref, zga_ref, zgb_ref, yb_ref, g_ref, b_ref, ws_ref, bs_ref, out_ref, vn_s):
        vhat = _layer_norm_stats(_gelu(zv_ref[...]))
        vn_s[...] = (vhat * g_ref[...] + b_ref[...]).astype(MXU_DTYPE)
        for g in range(A_GROUPS):
            w = _tril_weights(ws_ref, g)
            bias = bs_ref[:, g:g + 1]
            cols = slice(g * CHUNK, (g + 1) * CHUNK)
            for c in range(tr // CHUNK):
                rows = slice(c * CHUNK, (c + 1) * CHUNK)
                mixed = jnp.dot(w, vn_s[rows, cols], preferred_element_type=F32) + bias
                ya = _gelu(zu_ref[rows, cols]) * mixed
                merged = _sigmoid(zga_ref[rows, cols]) * ya + _sigmoid(zgb_ref[rows, cols]) * yb_ref[rows, cols]
                out_ref[rows, cols] = merged.astype(MXU_DTYPE)

    return pl.pallas_call(
        body, name="mix_fwd", out_shape=jax.ShapeDtypeStruct((T, D_MODEL), MXU_DTYPE), grid=(T // tr,),
        in_specs=[zcol(0), zcol(1), zcol(2), zcol(3), row, vec, vec, ws_spec, bs_spec], out_specs=row,
        scratch_shapes=[pltpu.VMEM((tr, D_MODEL), MXU_DTYPE)],
        compiler_params=_params(("parallel",), 8 * _nbytes((tr, D_MODEL), F32)),
    )(z, z, z, z, yb, ln_g, ln_b, ws, bs_t)


def _mix_bwd(z, yb, dm, ln_g, ln_b, ws, bs_t):
    T = z.shape[0]
    tr = MIX_ROWS
    zcol, row, vec, ws_spec, bs_spec = _mix_specs(tr)

    def body(zu_ref, zv_ref, zga_ref, zgb_ref, yb_ref, dm_ref, g_ref, b_ref, ws_ref, bs_ref,
             dz_ref, dyb_ref, dl_ref, gws_ref, gbs_ref, glg_ref, glb_ref, vn_s, dvn_s):
        @pl.when(pl.program_id(0) == 0)
        def _():
            gws_ref[...] = jnp.zeros_like(gws_ref)
            gbs_ref[...] = jnp.zeros_like(gbs_ref)
            glg_ref[...] = jnp.zeros_like(glg_ref)
            glb_ref[...] = jnp.zeros_like(glb_ref)

        lane = lax.broadcasted_iota(jnp.int32, (CHUNK, 128), 1)
        va, dgelu_v = _gelu_and_grad(zv_ref[...])
        mu = jnp.mean(va, axis=-1, keepdims=True)
        xc = va - mu
        rs = lax.rsqrt(jnp.mean(xc * xc, axis=-1, keepdims=True) + EPS)
        vhat = xc * rs
        vn_s[...] = (vhat * g_ref[...] + b_ref[...]).astype(MXU_DTYPE)
        gbs_acc = jnp.zeros((CHUNK, 128), F32)
        for g in range(A_GROUPS):
            w = _tril_weights(ws_ref, g)
            bias = bs_ref[:, g:g + 1]
            cols = slice(g * CHUNK, (g + 1) * CHUNK)
            gw_acc = jnp.zeros((CHUNK, CHUNK), F32)
            for c in range(tr // CHUNK):
                rows = slice(c * CHUNK, (c + 1) * CHUNK)
                vn = vn_s[rows, cols]
                mixed = jnp.dot(w, vn, preferred_element_type=F32) + bias
                ua, dgelu_u = _gelu_and_grad(zu_ref[rows, cols])
                dmv = dm_ref[rows, cols]
                sa = _sigmoid(zga_ref[rows, cols])
                dya = dmv * sa
                dz_ref[rows, 2 * D_MODEL + g * CHUNK:2 * D_MODEL + (g + 1) * CHUNK] = (
                    dmv * (ua * mixed) * (sa * (1.0 - sa))).astype(dz_ref.dtype)
                dz_ref[rows, cols] = (dya * mixed * dgelu_u).astype(dz_ref.dtype)
                dmix = dya * ua
                gbs_acc = gbs_acc + jnp.where(lane == g, jnp.sum(dmix, axis=-1, keepdims=True), 0.0)
                dmix_b = dmix.astype(MXU_DTYPE)
                gw_acc = gw_acc + lax.dot_general(dmix_b, vn, _DIMS["nt"], preferred_element_type=F32)
                dvn_s[rows, cols] = lax.dot_general(w, dmix_b, _DIMS["tn"], preferred_element_type=F32)
            gws_ref[g] += jnp.where(_causal_mask(CHUNK), gw_acc, 0.0)
        gbs_ref[...] += gbs_acc

        dvn = dvn_s[...]
        glg_ref[...] += jnp.sum(dvn * vhat, axis=0, keepdims=True)
        glb_ref[...] += jnp.sum(dvn, axis=0, keepdims=True)
        dvh = dvn * g_ref[...]
        dva = rs * (dvh - jnp.mean(dvh, axis=-1, keepdims=True) - vhat * jnp.mean(dvh * vhat, axis=-1, keepdims=True))
        dz_ref[:, D_MODEL:2 * D_MODEL] = (dva * dgelu_v).astype(dz_ref.dtype)

        dmv = dm_ref[...]
        ybv = yb_ref[...]
        sb = _sigmoid(zgb_ref[...])
        dyb = dmv * sb
        dyb_ref[...] = dyb.astype(dyb_ref.dtype)
        dz_ref[:, 3 * D_MODEL:4 * D_MODEL] = (dmv * ybv * (sb * (1.0 - sb))).astype(dz_ref.dtype)
        dz_ref[:, 4 * D_MODEL:] = jnp.zeros((tr, LAT), dz_ref.dtype)
        prod = dyb * ybv
        sel = (lax.broadcasted_iota(jnp.int32, (HEADS, D_MODEL), 1) // NOPE
               == lax.broadcasted_iota(jnp.int32, (HEADS, D_MODEL), 0)).astype(jnp.bfloat16)
        hi = prod.astype(jnp.bfloat16)
        rest = prod - hi.astype(F32)
        mid = rest.astype(jnp.bfloat16)
        lo = (rest - mid.astype(F32)).astype(jnp.bfloat16)
        dl_ref[...] = (lax.dot_general(sel, hi, _DIMS["nt"], preferred_element_type=F32)
                       + lax.dot_general(sel, mid, _DIMS["nt"], preferred_element_type=F32)
                       + lax.dot_general(sel, lo, _DIMS["nt"], preferred_element_type=F32))

    return pl.pallas_call(
        body, name="mix_bwd",
        out_shape=(jax.ShapeDtypeStruct((T, IN_PAD), MXU_DTYPE), jax.ShapeDtypeStruct((T, D_MODEL), MXU_DTYPE),
                   jax.ShapeDtypeStruct((HEADS, T), F32), jax.ShapeDtypeStruct((A_GROUPS, CHUNK, CHUNK), F32),
                   jax.ShapeDtypeStruct((CHUNK, 128), F32), jax.ShapeDtypeStruct((1, D_MODEL), F32),
                   jax.ShapeDtypeStruct((1, D_MODEL), F32)),
        grid=(T // tr,),
        in_specs=[zcol(0), zcol(1), zcol(2), zcol(3), row, row, vec, vec, ws_spec, bs_spec],
        out_specs=(pl.BlockSpec((tr, IN_PAD), lambda i: (i, 0)), row, pl.BlockSpec((HEADS, tr), lambda i: (0, i)),
                   ws_spec, bs_spec, vec, vec),
        scratch_shapes=[pltpu.VMEM((tr, D_MODEL), MXU_DTYPE), pltpu.VMEM((tr, D_MODEL), F32)],
        compiler_params=_params(("arbitrary",), 12 * _nbytes((tr, D_MODEL), F32)),
    )(z, z, z, z, yb, dm, ln_g, ln_b, ws, bs_t)


def _lat_bwd(dz, z, dq, dk, dv, gq, gkv, wq, wkv, cos_a, sin_a, tr=256):
    T = z.shape[0]
    lat_blk = (4 * D_MODEL) // LAT

    def body(dz_in, z_ref, dq_ref, dk_ref, dv_ref, gq_ref, gkv_ref, wq_ref, wkv_ref, cos_ref, sin_ref,
             dz_ref, dqr_ref, dkv_ref, ggq_ref, ggkv_ref):
        del dz_in

        @pl.when(pl.program_id(0) == 0)
        def _():
            ggq_ref[...] = jnp.zeros_like(ggq_ref)
            ggkv_ref[...] = jnp.zeros_like(ggkv_ref)

        cos_v, sin_v = cos_ref[...], sin_ref[...]
        dkr = jnp.zeros((tr, 128), F32)
        for h in range(HEADS):
            o = h * HEAD_PAD
            dqr_ref[:, o:o + NOPE] = dq_ref[:, o:o + NOPE].astype(MXU_DTYPE)
            dqr_ref[:, o + NOPE:o + HEAD_PAD] = _rope_mix_bwd(dq_ref[:, o + NOPE:o + HEAD_PAD], cos_v, sin_v).astype(MXU_DTYPE)
            dkv_ref[:, h * NOPE:(h + 1) * NOPE] = dk_ref[:, o:o + NOPE].astype(MXU_DTYPE)
            dkr = dkr + _rope_mix_bwd(dk_ref[:, o + NOPE:o + HEAD_PAD], cos_v, sin_v)
        dkv_ref[:, HEADS * NOPE:] = dv_ref[...]
        dcqn = lax.dot_general(dqr_ref[...], wq_ref[...], _DIMS["nt"], preferred_element_type=F32)
        dckvn = lax.dot_general(dkv_ref[...], wkv_ref[...], _DIMS["nt"], preferred_element_type=F32)

        zl = z_ref[...]

        def rms_bwd(c, dn, g_ref, gg_ref):
            r = lax.rsqrt(jnp.mean(c * c, axis=-1, keepdims=True) + EPS)
            ch = c * r
            gg_ref[...] += jnp.sum(dn * ch, axis=0, keepdims=True)
            dch = dn * g_ref[...]
            return r * (dch - ch * jnp.mean(dch * ch, axis=-1, keepdims=True))

        dz_ref[:, :Q_RANK] = rms_bwd(zl[:, :Q_RANK], dcqn, gq_ref, ggq_ref).astype(dz_ref.dtype)
        dz_ref[:, Q_RANK:Q_RANK + KV_RANK] = rms_bwd(zl[:, Q_RANK:Q_RANK + KV_RANK], dckvn, gkv_ref, ggkv_ref).astype(dz_ref.dtype)
        dz_ref[:, Q_RANK + KV_RANK:] = dkr.astype(dz_ref.dtype)

    def row(w):
        return pl.BlockSpec((tr, w), lambda i: (i, 0))

    def full(a):
        return pl.BlockSpec(a.shape, lambda i: (0, 0))

    lat = pl.BlockSpec((tr, LAT), lambda i: (i, lat_blk))
    return pl.pallas_call(
        body, name="lat_bwd",
        out_shape=(jax.ShapeDtypeStruct(dz.shape, dz.dtype), jax.ShapeDtypeStruct((T, HEADS * HEAD_PAD), MXU_DTYPE),
                   jax.ShapeDtypeStruct((T, 2 * HEADS * NOPE), MXU_DTYPE), jax.ShapeDtypeStruct(gq.shape, F32),
                   jax.ShapeDtypeStruct(gkv.shape, F32)),
        grid=(T // tr,),
        in_specs=[pl.BlockSpec(memory_space=pl.ANY), lat, row(HEADS * HEAD_PAD), row(HEADS * HEAD_PAD), row(HEADS * NOPE),
                  full(gq), full(gkv), full(wq), full(wkv), row(128), row(128)],
        out_specs=(lat, row(HEADS * HEAD_PAD), row(2 * HEADS * NOPE), full(gq), full(gkv)),
        input_output_aliases={0: 0},
        compiler_params=_params(("arbitrary",), 8 * _nbytes((tr, HEADS * HEAD_PAD), F32)),
    )(dz, z, dq, dk, dv, gq, gkv, wq, wkv, cos_a, sin_a)


GATE_ROWS = 64
HALO = 8


def _taps(ref, half, r, first):
    C = GATE_ROWS
    if first:
        xs = jnp.concatenate([jnp.zeros((HALO, ref.shape[-1]), F32), ref[half, 0:C, :]], axis=0)
    else:
        xs = ref[half, pl.ds(pl.multiple_of(r * C - HALO, HALO), C + HALO), :]
    return xs[HALO:, :], pltpu.roll(xs, 1, 0)[HALO:, :], pltpu.roll(xs, 2, 0)[HALO:, :]


def _conv_taps(taps, cw, cb):
    x0, x1, x2 = taps
    return cb + cw[0:1, :] * x2 + cw[1:2, :] * x1 + cw[2:3, :] * x0


def _fold8(x):
    acc = x[0:8, :]
    for i in range(1, x.shape[0] // 8):
        acc = acc + x[8 * i:8 * (i + 1), :]
    return acc


def _gate_fwd(up3, conv_w, conv_b, B, S):
    T = B * S
    W = FF_TILE
    C = GATE_ROWS

    def body(up_ref, cw_ref, cb_ref, act_ref, conv_ref):
        def chunk(r, first):
            gate = _conv_taps(_taps(up_ref, 0, r, first), cw_ref[0], cb_ref[0])
            val = _conv_taps(_taps(up_ref, 1, r, first), cw_ref[1], cb_ref[1])
            rows = pl.ds(0 if first else pl.multiple_of(r * C, C), C)
            conv_ref[0, rows, :] = gate
            conv_ref[1, rows, :] = val
            act_ref[rows, :] = (gate * _sigmoid(gate) * val).astype(act_ref.dtype)

        chunk(0, True)

        @pl.loop(1, S // C)
        def _(r):
            chunk(r, False)

    up_spec = pl.BlockSpec((2, S, W), lambda b, j: (0, b, j))
    return pl.pallas_call(
        body, name="gate_fwd",
        out_shape=(jax.ShapeDtypeStruct((T, D_FF), MXU_DTYPE), jax.ShapeDtypeStruct((2, T, D_FF), F32)),
        grid=(B, N_FF_TILES),
        in_specs=[up_spec, pl.BlockSpec((2, 3, W), lambda b, j: (0, 0, j)), pl.BlockSpec((2, 1, W), lambda b, j: (0, 0, j))],
        out_specs=(pl.BlockSpec((S, W), lambda b, j: (b, j)), up_spec),
        compiler_params=_params(("parallel", "parallel"), 8 * _nbytes((S, W), F32)),
    )(up3, conv_w, conv_b)


def _gate_bwd(up3, conv3, dact, conv_w, B, S):
    T = B * S
    W = FF_TILE
    C = GATE_ROWS

    def body(up_ref, conv_ref, da_ref, cw_ref, dup_ref, gcw_ref, gcb_ref, d_s):
        @pl.when(pl.program_id(1) == 0)
        def _():
            gcw_ref[...] = jnp.zeros_like(gcw_ref)
            gcb_ref[...] = jnp.zeros_like(gcb_ref)

        def chunk(r, first, sums):
            rows = pl.ds(0 if first else pl.multiple_of(r * C, C), C)
            taps = [_taps(up_ref, half, r, first) for half in (0, 1)]
            gate, val = conv_ref[0, rows, :], conv_ref[1, rows, :]
            sg = _sigmoid(gate)
            da = da_ref[rows, :]
            d_halves = (da * val * (sg * (1.0 + gate * (1.0 - sg))), da * (gate * sg))
            out = []
            for half, dup in enumerate(d_halves):
                d_s[half, rows, :] = dup
                x0, x1, x2 = taps[half]
                sb, s0, s1, s2 = sums[half]
                out.append((sb + _fold8(dup), s0 + _fold8(dup * x2), s1 + _fold8(dup * x1), s2 + _fold8(dup * x0)))
            return tuple(out)

        zeros = tuple(tuple(jnp.zeros((8, W), F32) for _ in range(4)) for _ in range(2))
        sums = chunk(0, True, zeros)
        sums = lax.fori_loop(1, S // C, lambda r, s: chunk(r, False, s), sums)
        for half in (0, 1):
            sb, s0, s1, s2 = sums[half]
            gcb_ref[half] += jnp.sum(sb, axis=0, keepdims=True)
            gcw_ref[half, 0:1, :] += jnp.sum(s0, axis=0, keepdims=True)
            gcw_ref[half, 1:2, :] += jnp.sum(s1, axis=0, keepdims=True)
            gcw_ref[half, 2:3, :] += jnp.sum(s2, axis=0, keepdims=True)

        d_s[:, S:S + HALO, :] = jnp.zeros((2, HALO, W), F32)

        @pl.loop(0, S // C)
        def _(r):
            base = pl.multiple_of(r * C, C)
            for half in (0, 1):
                ds_ = d_s[half, pl.ds(base, C + HALO), :]
                cw = cw_ref[half]
                dx = (cw[2:3, :] * ds_[:C, :] + cw[1:2, :] * pltpu.roll(ds_, C + HALO - 1, 0)[:C, :]
                      + cw[0:1, :] * pltpu.roll(ds_, C + HALO - 2, 0)[:C, :])
                dup_ref[half, pl.ds(base, C), :] = dx.astype(dup_ref.dtype)

    up_spec = pl.BlockSpec((2, S, W), lambda j, b: (0, b, j))
    cw_spec = pl.BlockSpec((2, 3, W), lambda j, b: (0, 0, j))
    cb_spec = pl.BlockSpec((2, 1, W), lambda j, b: (0, 0, j))
    return pl.pallas_call(
        body, name="gate_bwd",
        out_shape=(jax.ShapeDtypeStruct((2, T, D_FF), MXU_DTYPE), jax.ShapeDtypeStruct((2, 3, D_FF), F32),
                   jax.ShapeDtypeStruct((2, 1, D_FF), F32)),
        grid=(N_FF_TILES, B),
        in_specs=[up_spec, up_spec, pl.BlockSpec((S, W), lambda j, b: (b, j)), cw_spec],
        out_specs=(up_spec, cw_spec, cb_spec),
        scratch_shapes=[pltpu.VMEM((2, S + HALO, W), F32)],
        compiler_params=_params(("parallel", "arbitrary"), 12 * _nbytes((S, W), F32)),
    )(up3, conv3, dact, conv_w)


def _final(x2, tgt, g, tr=512):
    T, D = x2.shape

    def body(x_ref, t_ref, g_ref, dx_ref, loss_ref, gg_ref):
        @pl.when(pl.program_id(0) == 0)
        def _():
            loss_ref[...] = jnp.zeros_like(loss_ref)
            gg_ref[...] = jnp.zeros_like(gg_ref)

        xv = x_ref[...]
        gv = g_ref[...]
        r = lax.rsqrt(jnp.mean(xv * xv, axis=-1, keepdims=True) + EPS)
        xn = xv * r
        err = xn * gv - t_ref[...]
        loss_ref[...] += 0.5 * jnp.sum(jnp.mean(err * err, axis=-1, keepdims=True), axis=0, keepdims=True)
        dy = err * (1.0 / D)
        gg_ref[...] += jnp.sum(dy * xn, axis=0, keepdims=True)
        dxn = dy * gv
        dx_ref[...] = r * (dxn - xn * jnp.mean(dxn * xn, axis=-1, keepdims=True))

    row = pl.BlockSpec((tr, D), lambda i: (i, 0))
    vec = pl.BlockSpec((1, D), lambda i: (0, 0))
    return pl.pallas_call(
        body, name="final_loss",
        out_shape=(jax.ShapeDtypeStruct((T, D), F32), jax.ShapeDtypeStruct((1, 128), F32), jax.ShapeDtypeStruct((1, D), F32)),
        grid=(T // tr,), in_specs=[row, row, vec],
        out_specs=(row, pl.BlockSpec((1, 128), lambda i: (0, 0)), vec),
        compiler_params=_params(("arbitrary",), 6 * _nbytes((tr, D), F32)),
    )(x2, tgt, g)


def _sum_slabs(parts, name, tr):
    rows, cols = parts[0].shape
    n = len(parts)

    def body(*refs):
        acc = refs[0][...]
        for r in refs[1:n]:
            acc = acc + r[...]
        refs[n][...] = acc

    blk = pl.BlockSpec((tr, cols), lambda i: (i, 0))
    return pl.pallas_call(
        body, name=name, out_shape=jax.ShapeDtypeStruct((rows, cols), F32), grid=(rows // tr,),
        in_specs=[blk] * n, out_specs=blk,
        compiler_params=_params(("parallel",), (n + 1) * _nbytes((tr, cols), F32)),
    )(*parts)


ADAMW_BLOCK_BYTES = 2400 * 1024


def _adamw(w, g, m, v, name):
    lead = w.ndim == 3
    rows, cols = w.shape[-2:]
    fits = [d for d in range(8, rows + 1, 8) if rows % d == 0 and d * cols * 4 <= ADAMW_BLOCK_BYTES]
    tr = max(fits) if fits else rows
    c1 = 1.0 - ADAM_B1 ** ADAM_STEP
    c2 = 1.0 - ADAM_B2 ** ADAM_STEP

    def body(w_ref, g_ref, m_ref, v_ref, d_ref, nm_ref, nv_ref):
        gv = g_ref[...]
        nm = ADAM_B1 * m_ref[...] + (1.0 - ADAM_B1) * gv
        nv = ADAM_B2 * v_ref[...] + (1.0 - ADAM_B2) * (gv * gv)
        nm_ref[...] = nm
        nv_ref[...] = nv
        d_ref[...] = -ADAM_LR * ((nm / c1) / (jnp.sqrt(nv / c2) + ADAM_EPS) + ADAM_WD * w_ref[...])

    blk = pl.BlockSpec((None, tr, cols), lambda i: (0, i, 0)) if lead else pl.BlockSpec((tr, cols), lambda i: (i, 0))
    sds = jax.ShapeDtypeStruct(w.shape, F32)
    return pl.pallas_call(
        body, name=name, out_shape=(sds, sds, sds), grid=(rows // tr,), in_specs=[blk] * 4, out_specs=(blk, blk, blk),
        compiler_params=_params(("parallel",), 7 * _nbytes((tr, cols), F32)),
    )(w, g, m, v)


_ANY = pl.BlockSpec(memory_space=pl.ANY)


def _place():
    x, y, c = lax.axis_index("x"), lax.axis_index("y"), lax.axis_index("c")
    chips = [(1 - x, y), (x, 1 - y), (1 - x, 1 - y)]
    return x, y, c, chips


def _forward_halves(lands):
    n = len(lands)

    def body(*refs):
        outs, send, recv = refs[n:2 * n], refs[2 * n], refs[2 * n + 1]
        x, y, c, chips = _place()
        cps = []
        for w in range(n):
            for j, (px, py) in enumerate(chips):
                landed = outs[w].at[2 * px + py, c]
                cps.append(pltpu.make_async_remote_copy(
                    src_ref=landed, dst_ref=landed, send_sem=send.at[3 * w + j], recv_sem=recv.at[3 * w + j],
                    device_id=(x, y, 1 - c), device_id_type=MESH))
        for cp in cps:
            cp.start()
        for w in range(n):
            for j, (px, py) in enumerate(chips):
                other = outs[w].at[2 * px + py, 1 - c]
                pltpu.make_async_remote_copy(src_ref=other, dst_ref=other, send_sem=send.at[3 * w + j],
                                             recv_sem=recv.at[3 * w + j], device_id=(x, y, 1 - c),
                                             device_id_type=MESH).wait_recv()
        for cp in cps:
            cp.wait_send()

    dma = lambda k: pltpu.SemaphoreType.DMA((k,))
    return pl.pallas_call(
        body, name="gather_forward_halves", out_shape=tuple(jax.ShapeDtypeStruct(a.shape, a.dtype) for a in lands),
        in_specs=[_ANY] * n, out_specs=tuple([_ANY] * n), input_output_aliases={w: w for w in range(n)},
        scratch_shapes=[dma(3 * n), dma(3 * n)],
    )(*lands)


_HBM = pl.BlockSpec(memory_space=pltpu.HBM)
_SEM = pl.BlockSpec(memory_space=pltpu.SEMAPHORE)
_EFFECT = pltpu.SideEffectType.DATAFLOW_SIDE_EFFECTING


SEMS_PER_ARRAY = 8


def _exchange_copies(srcs, lands, send, recv, mode):
    x, y, c, chips = _place()
    if mode == "halves":
        cps = []
        for w, (src, land) in enumerate(zip(srcs, lands)):
            pieces = [(src.at[c], land.at[2 * x + y, c], (px, py, c)) for px, py in chips]
            pieces.append((src, land.at[2 * x + y], (x, y, 1 - c)))
            for k, (piece, dst, peer) in enumerate(pieces):
                cps.append(pltpu.make_async_remote_copy(
                    src_ref=piece, dst_ref=dst, send_sem=send.at[SEMS_PER_ARRAY * w + k],
                    recv_sem=recv.at[SEMS_PER_ARRAY * w + k], device_id=peer, device_id_type=MESH))
        return cps
    if mode == "swap":
        return [pltpu.make_async_remote_copy(
            src_ref=src.at[:, 1 - c], dst_ref=land, send_sem=send.at[SEMS_PER_ARRAY * w],
            recv_sem=recv.at[SEMS_PER_ARRAY * w], device_id=(x, y, 1 - c), device_id_type=MESH)
            for w, (src, land) in enumerate(zip(srcs, lands))]
    if mode == "all":
        flips = [(fx, fy, fc) for fx in (0, 1) for fy in (0, 1) for fc in (0, 1)][1:]
        peers = [(x ^ fx, y ^ fy, c ^ fc) for fx, fy, fc in flips]
        slot = 4 * x + 2 * y + c
    else:
        peers = [(px, py, c) for px, py in chips] + ([(x, y, 1 - c)] if mode == "gather" else [])
        slot = 2 * x + y
    cps = []
    for w, (src, land) in enumerate(zip(srcs, lands)):
        for k, peer in enumerate(peers):
            piece = src.at[2 * peer[0] + peer[1]] if mode == "scatter" else src
            cps.append(pltpu.make_async_remote_copy(
                src_ref=piece, dst_ref=land.at[slot], send_sem=send.at[SEMS_PER_ARRAY * w + k],
                recv_sem=recv.at[SEMS_PER_ARRAY * w + k], device_id=peer, device_id_type=MESH))
    return cps


def _exchange_start(srcs, name, mode, after):
    n = len(srcs)
    if mode == "swap":
        land_shapes = [(s.shape[0],) + s.shape[2:] for s in srcs]
    else:
        lead = {"gather": (N_CHIPS,), "halves": (N_CHIPS,), "scatter": (), "all": (2 * N_CHIPS,)}[mode]
        land_shapes = [lead + s.shape for s in srcs]

    def body(*refs):
        src_refs, land_refs = refs[:n], refs[n:2 * n]
        send, recv = refs[2 * n + 1], refs[2 * n + 2]
        token = refs[-1]
        for cp in _exchange_copies(src_refs, land_refs, send, recv, mode):
            cp.start()
        token[...] = jnp.zeros_like(token)

    sems = pltpu.SemaphoreType.DMA((SEMS_PER_ARRAY * n,))
    out = pl.pallas_call(
        body, name=name,
        out_shape=(sems, sems, *[pltpu.HBM(s.shape, s.dtype) for s in srcs],
                   *[pltpu.HBM(shp, s.dtype) for shp, s in zip(land_shapes, srcs)], jax.ShapeDtypeStruct((8, 128), F32)),
        in_specs=[_HBM] * (2 * n) + [_ANY],
        out_specs=(_SEM, _SEM, *[_HBM] * (2 * n), pl.BlockSpec(memory_space=pltpu.VMEM)),
        input_output_aliases={i: 2 + i for i in range(2 * n)},
        compiler_params=pltpu.CompilerParams(has_side_effects=_EFFECT),
    )(*[pltpu.with_memory_space_constraint(s, pltpu.HBM) for s in srcs],
      *[pltpu.with_memory_space_constraint(lax.empty(shp, s.dtype), pltpu.HBM) for shp, s in zip(land_shapes, srcs)],
      after)
    return out[0], out[1], out[2:2 + n], out[2 + n:2 + 2 * n], out[-1]


def _exchange_wait(started, name, mode, after):
    send, recv, src_thru, land_thru, _ = started
    n = len(src_thru)
    after = list(after) if isinstance(after, (list, tuple)) else [after]

    def body(*refs):
        src_refs, land_refs, send_ref, recv_ref = refs[:n], refs[n:2 * n], refs[2 * n], refs[2 * n + 1]
        for cp in _exchange_copies(src_refs, land_refs, send_ref, recv_ref, mode):
            cp.wait_send()
            cp.wait_recv()

    out = pl.pallas_call(
        body, name=name,
        out_shape=tuple(pltpu.HBM(a.shape, a.dtype) for a in list(src_thru) + list(land_thru)),
        in_specs=[_HBM] * (2 * n) + [_SEM, _SEM] + [_ANY] * len(after), out_specs=tuple([_HBM] * (2 * n)),
        input_output_aliases={i: i for i in range(2 * n)},
        compiler_params=pltpu.CompilerParams(has_side_effects=_EFFECT),
    )(*src_thru, *land_thru, send, recv, *after)
    return out[:n], out[n:]


def _swap_halves(gs, name):
    n = len(gs)

    def body(*refs):
        ins, outs, send, recv = refs[:n], refs[n:2 * n], refs[2 * n], refs[2 * n + 1]
        x, y, c, _ = _place()
        cps = []
        for w in range(n):
            cps.append(pltpu.make_async_remote_copy(
                src_ref=ins[w].at[:, 1 - c], dst_ref=outs[w], send_sem=send.at[w], recv_sem=recv.at[w],
                device_id=(x, y, 1 - c), device_id_type=MESH))
        for cp in cps:
            cp.start()
        for cp in cps:
            cp.wait()

    return pl.pallas_call(
        body, name=name,
        out_shape=tuple(jax.ShapeDtypeStruct((g.shape[0],) + g.shape[2:], g.dtype) for g in gs),
        in_specs=[_ANY] * n, out_specs=tuple([_ANY] * n),
        scratch_shapes=[pltpu.SemaphoreType.DMA((n,)), pltpu.SemaphoreType.DMA((n,))],
    )(*gs)


GRAD_PAYLOAD = jnp.bfloat16


def _half_blocks(half_rows, cols):
    if (half_rows // 2) % 16 == 0:
        return (half_rows // 2, cols), (lambda r: (r, 0))
    assert cols % 256 == 0, (half_rows, cols)
    return (half_rows, cols // 2), (lambda r: (0, r))


def _pair_sum(gs, gots, name):
    n = len(gs)
    core = lax.axis_index("c").astype(jnp.int32).reshape(1)

    def body(core_ref, *refs):
        del core_ref
        for w in range(n):
            refs[2 * n + w][...] = (refs[w][...] + refs[n + w][...]).astype(GRAD_PAYLOAD)

    in_specs, out_specs, out_shape, nbytes = [], [], [], 0
    cuts = [_half_blocks(g.shape[1] // 2, g.shape[2]) for g in gs]
    for g, ((br, bc), at) in zip(gs, cuts):
        per_half = (g.shape[1] // 2) // br
        in_specs.append(pl.BlockSpec((1, br, bc), lambda s, r, core, at=at, per_half=per_half:
                                     (s, per_half * core[0] + at(r)[0], at(r)[1])))
        nbytes += 3 * _nbytes((br, bc), F32)
    for g, ((br, bc), at) in zip(gs, cuts):
        in_specs.append(pl.BlockSpec((1, br, bc), lambda s, r, core, at=at: (s,) + at(r)))
        out_specs.append(pl.BlockSpec((1, br, bc), lambda s, r, core, at=at: (s,) + at(r)))
        out_shape.append(jax.ShapeDtypeStruct((g.shape[0], g.shape[1] // 2, g.shape[2]), GRAD_PAYLOAD))
    return pl.pallas_call(
        body, name=name, out_shape=tuple(out_shape),
        grid_spec=pltpu.PrefetchScalarGridSpec(num_scalar_prefetch=1, grid=(N_CHIPS, 2), in_specs=in_specs,
                                               out_specs=tuple(out_specs)),
        compiler_params=_params(("parallel", "parallel"), nbytes),
    )(core, *gs, *gots)


def _chip_sum(ps, landed):
    n = len(ps)
    x, y, c = lax.axis_index("x"), lax.axis_index("y"), lax.axis_index("c")
    where = jnp.stack([2 * x + y, 2 * (1 - x) + y, 2 * x + (1 - y), 2 * (1 - x) + (1 - y), c]).astype(jnp.int32)

    def body(where_ref, *refs):
        del where_ref
        for w in range(n):
            terms = [refs[4 * w + t][...].astype(F32) for t in range(4)]
            refs[4 * n + w][...] = ((terms[0] + terms[1]) + terms[2]) + terms[3]

    in_specs, out_specs, out_shape, args, nbytes = [], [], [], [], 0
    for p, a in zip(ps, landed):
        (br, bc), at = _half_blocks(a.shape[1], a.shape[2])
        blk = (1, br, bc)
        in_specs.append(pl.BlockSpec(blk, lambda r, where, at=at: (where[0],) + at(r)))
        args.append(p)
        for t in (1, 2, 3):
            in_specs.append(pl.BlockSpec(blk, lambda r, where, t=t, at=at: (where[t],) + at(r)))
            args.append(a)
        out_specs.append(pl.BlockSpec(blk, lambda r, where, at=at: (where[4],) + at(r)))
        out_shape.append(jax.ShapeDtypeStruct((2,) + a.shape[1:], F32))
        nbytes += 4 * _nbytes(blk, F32)
    return pl.pallas_call(
        body, name="grad_chip_sum", out_shape=tuple(out_shape),
        grid_spec=pltpu.PrefetchScalarGridSpec(num_scalar_prefetch=1, grid=(2,), in_specs=in_specs,
                                               out_specs=tuple(out_specs)),
        compiler_params=_params(("parallel",), nbytes),
    )(where, *args)


def _join_halves(ss):
    n = len(ss)

    def body(*refs):
        outs, send, recv = refs[n:2 * n], refs[2 * n], refs[2 * n + 1]
        x, y, c, _ = _place()
        cps = []
        for w in range(n):
            cps.append(pltpu.make_async_remote_copy(
                src_ref=outs[w].at[c], dst_ref=outs[w].at[c], send_sem=send.at[w], recv_sem=recv.at[w],
                device_id=(x, y, 1 - c), device_id_type=MESH))
        for cp in cps:
            cp.start()
        for w in range(n):
            got = outs[w].at[1 - c]
            pltpu.make_async_remote_copy(src_ref=got, dst_ref=got, send_sem=send.at[w], recv_sem=recv.at[w],
                                         device_id=(x, y, 1 - c), device_id_type=MESH).wait_recv()
        for cp in cps:
            cp.wait_send()

    dma = lambda k: pltpu.SemaphoreType.DMA((k,))
    return pl.pallas_call(
        body, name="grad_join_halves",
        out_shape=tuple(jax.ShapeDtypeStruct(s.shape, s.dtype) for s in ss),
        in_specs=[_ANY] * n, out_specs=tuple([_ANY] * n), input_output_aliases={w: w for w in range(n)},
        scratch_shapes=[dma(n), dma(n)],
    )(*ss)


def _rot_cols(w, axis=-1):
    a, b = jnp.split(w, 2, axis=axis)
    return jnp.concatenate([-b, a], axis=axis)


def _rot_cols_t(g, axis=-1):
    a, b = jnp.split(g, 2, axis=axis)
    return jnp.concatenate([b, -a], axis=axis)


def _cols_from_chips(a):
    n, r, cs = a.shape
    return jnp.transpose(a, (1, 0, 2)).reshape(r, n * cs)


def _cols_to_chips(a):
    r, cc = a.shape
    return jnp.transpose(a.reshape(r, N_CHIPS, cc // N_CHIPS), (1, 0, 2))


def _conv_w_split(cw):
    return jnp.swapaxes(cw.reshape(3, 2, D_FF), 0, 1)


def _conv_w_join(g):
    return jnp.swapaxes(g, 0, 1).reshape(3, 2 * D_FF)


_SEG =(D_MODEL, 2 * D_MODEL, 2 * D_MODEL + Q_RANK, 2 * D_MODEL + Q_RANK + KV_RANK, 2 * D_MODEL + Q_RANK + KV_RANK + ROPE,
        3 * D_MODEL + Q_RANK + KV_RANK + ROPE)


def _w_in_t_to_pad(wt):
    u, v, cq, ckv, kr, ga, gb = jnp.split(wt, _SEG, axis=0)
    return jnp.concatenate([u, v, ga, gb, cq, ckv, kr, _rot_cols(kr, axis=0)], axis=0)


def _w_in_t_from_pad(gt):
    u, v, ga, gb, cq, ckv, kr, krr = jnp.split(
        gt, (D_MODEL, 2 * D_MODEL, 3 * D_MODEL, 4 * D_MODEL, 4 * D_MODEL + Q_RANK, 4 * D_MODEL + Q_RANK + KV_RANK,
             4 * D_MODEL + Q_RANK + KV_RANK + ROPE), axis=0)
    return jnp.concatenate([u, v, cq, ckv, kr + _rot_cols_t(krr, axis=0), ga, gb], axis=0)


def _w_uq_to_pad(w):
    t = w.reshape(Q_RANK, HEADS, QK_DIM)
    nope, rope = t[..., :NOPE], t[..., NOPE:]
    return jnp.concatenate([nope, rope, _rot_cols(rope)], axis=-1).reshape(Q_RANK, HEADS * HEAD_PAD)


def _w_uq_from_pad(g):
    t = g.reshape(Q_RANK, HEADS, HEAD_PAD)
    nope, rope, rot = t[..., :NOPE], t[..., NOPE:QK_DIM], t[..., QK_DIM:]
    return jnp.concatenate([nope, rope + _rot_cols_t(rot)], axis=-1).reshape(Q_RANK, HEADS * QK_DIM)


def _w_ukv_to_pad(w):
    t = w.reshape(KV_RANK, HEADS, 2, NOPE)
    return jnp.swapaxes(t, 1, 2).reshape(KV_RANK, 2 * HEADS * NOPE)


def _w_ukv_from_pad(g):
    t = g.reshape(KV_RANK, 2, HEADS, NOPE)
    return jnp.swapaxes(t, 1, 2).reshape(KV_RANK, 2 * HEADS * NOPE)


def _rope_tables(positions):
    inv_freq = 1.0 / (ROPE_THETA ** (jnp.arange(0, ROPE, 2, dtype=F32) / ROPE))
    ang = positions.astype(F32).reshape(-1, 1) * inv_freq
    cos, sin = jnp.cos(ang), jnp.sin(ang)
    zero = jnp.zeros((ang.shape[0], 64), F32)
    return jnp.concatenate([cos, cos, zero], axis=1), jnp.concatenate([sin, sin, zero], axis=1)


_BIG = ("w_in", "w_uq", "w_ukv", "w_out", "w_up", "w_down")
UP_SHARD = 2 * D_FF // N_CHIPS


def _local_step(x, positions, tgt, wts, in_weights, mixer_weights, ffn_weights, on_ffn_grads, on_mixer_grads):
    B, S, D = x.shape
    T = B * S
    xf = x.reshape(T, D)
    cos_a, sin_a = _rope_tables(positions)
    bs_t = jnp.pad(wts["a_spatial_b"].T, ((0, 0), (0, 128 - A_GROUPS)))

    h = _rms_fwd(xf, wts["mix_norm"], "norm1_fwd")
    wts = dict(wts)
    wts["w_in"], token = in_weights([h, cos_a, sin_a])
    z = _mm(h, wts["w_in"], "nt", "in_proj", tm=512, tn=1536, tk=D, n_outer=True, after=token)
    wts["w_q"], wts["w_kv"], wts["w_out"] = mixer_weights(z)
    q, k, v, cqn, ckvn = _lat_fwd(z, wts["q_a_norm"], wts["kv_a_norm"], wts["w_q"], wts["w_kv"], cos_a, sin_a)
    yb, *lses = _attn_fwd(q, k, v, B, S)
    merged = _mix_fwd(z, yb, wts["a_v_norm_g"], wts["a_v_norm_b"], wts["a_spatial_w"], bs_t)
    x1 = _mm(merged, wts["w_out"], "nn", "out_proj", tm=512, tn=D, tk=D, add=xf)
    h2 = _rms_fwd(x1, wts["ffn_norm"], "norm2_fwd")
    wts["w_up"], wts["w_down"], wts["conv_w"] = ffn_weights(h2)
    up_pre = _mm(h2, wts["w_up"], "nn", "up_proj", tm=512, tn=UP_SHARD, tk=D, dims=(T, 2 * D_FF, D),
                 b_spec=pl.BlockSpec((None, D, UP_SHARD), lambda i, j, k: (j, 0, 0)),
                 o_spec=pl.BlockSpec((None, 512, UP_SHARD), lambda i, j, k: (j // 2, i, j % 2)), out_shape=(2, T, D_FF),
                 n_outer=True)
    act, up_conv = _gate_fwd(up_pre, wts["conv_w"], wts["conv_b"], B, S)
    x2 = _mm(act, wts["w_down"], "nn", "down_proj", tm=512, tn=D, tk=1408, add=x1)
    dx2, loss_row, g_final = _final(x2, tgt.reshape(T, D), wts["final_norm"])

    g = {"final_norm": g_final}
    dact = _mm(dx2, wts["w_down"], "nt", "down_proj_dx", tm=512, tn=1408, tk=D, n_outer=True)
    tk2, tk1 = min(2048, T), min(1024, T)
    g["w_down"], g["w_down_lo"] = _mm(act, dx2, "tn", "down_proj_dw", tm=1408, tn=D, tk=tk1, copy_dtype=GRAD_PAYLOAD)
    dup, g["conv_w"], g["conv_b"] = _gate_bwd(up_pre, up_conv, dact, wts["conv_w"], B, S)
    g["w_up"], g["w_up_lo"] = _mm(
        h2, dup, "tn", "up_proj_dw", tm=D, tn=UP_SHARD, tk=tk2, dims=(D, 2 * D_FF, T), copy_dtype=GRAD_PAYLOAD,
        b_spec=pl.BlockSpec((None, tk2, UP_SHARD), lambda i, j, k: (j // 2, k, j % 2)),
        o_spec=pl.BlockSpec((None, D, UP_SHARD), lambda i, j, k: (j, 0, 0)), out_shape=(N_CHIPS, D, UP_SHARD))
    token, ffn_sent = on_ffn_grads(g)
    dh2 = _mm(dup, wts["w_up"], "nt", "up_proj_dx", tm=512, tn=D, tk=UP_SHARD, dims=(T, D, 2 * D_FF), after=token,
              a_spec=pl.BlockSpec((None, 512, UP_SHARD), lambda i, j, k: (k // 2, i, k % 2)),
              b_spec=pl.BlockSpec((None, D, UP_SHARD), lambda i, j, k: (k, 0, 0)))
    token = ffn_sent(dh2)
    dx1, g["ffn_norm"] = _rms_bwd(x1, wts["ffn_norm"], dh2, dx2, "norm2_bwd")
    dm = _mm(dx1, wts["w_out"], "nt", "out_proj_dx", tm=512, tn=D, tk=D, after=token)
    g["w_out"], g["w_out_lo"] = _mm(merged, dx1, "tn", "out_proj_dw", tm=D, tn=D, tk=tk1, copy_dtype=GRAD_PAYLOAD)
    dz, dyb, dl, g["a_spatial_w"], gbs, g["a_v_norm_g"], g["a_v_norm_b"] = _mix_bwd(
        z, yb, dm, wts["a_v_norm_g"], wts["a_v_norm_b"], wts["a_spatial_w"], bs_t)
    g["a_spatial_b"] = gbs[:, :A_GROUPS].T
    delta = dl.reshape(HEADS * T // ATT_BLOCK, 1, ATT_BLOCK)
    dq, dk, dv = _attn_bwd(q, k, v, dyb, lses, delta, B, S)
    dz, dq_raw, dkv, g["q_a_norm"], g["kv_a_norm"] = _lat_bwd(
        dz, z, dq, dk, dv, wts["q_a_norm"], wts["kv_a_norm"], wts["w_q"], wts["w_kv"], cos_a, sin_a)
    g["w_q"] = _mm(cqn, dq_raw, "tn", "q_proj_dw", tm=Q_RANK, tn=HEADS * HEAD_PAD, tk=tk2)
    g["w_kv"] = _mm(ckvn, dkv, "tn", "kv_proj_dw", tm=KV_RANK, tn=2 * HEADS * NOPE, tk=tk2)
    g["w_in"] = _mm(dz, h, "tn", "in_proj_dw", tm=1536, tn=D, tk=tk2)
    token = on_mixer_grads(g)
    dh = _mm(dz, wts["w_in"], "nn", "in_proj_dx", tm=512, tn=D, tk=1536, after=token)
    dx, g["mix_norm"] = _rms_bwd(xf, wts["mix_norm"], dh, dx1, "norm1_bwd")
    return loss_row[0, 0], dx.reshape(B, S, D), g


_SMALL = (("mix_norm", (1, D_MODEL)), ("a_v_norm_g", (1, D_MODEL)), ("a_v_norm_b", (1, D_MODEL)),
          ("a_spatial_w", (A_GROUPS * CHUNK, CHUNK)), ("a_spatial_b", (1, A_GROUPS * CHUNK)), ("q_a_norm", (1, Q_RANK)),
          ("kv_a_norm", (1, KV_RANK)), ("ffn_norm", (1, D_MODEL)), ("conv_b", (1, 2 * D_FF)), ("final_norm", (1, D_MODEL)),
          ("conv_w", (3, 2 * D_FF)))
_SMALL_SIZE = sum(math.prod(s) for _, s in _SMALL)
_SMALL_ROWS = -(-(_SMALL_SIZE + 1) // (128 * 8)) * 8


def kernel(x, positions, mix_norm, w_in, a_v_norm_g, a_v_norm_b, a_spatial_w, a_spatial_b, q_a_norm, w_uq, kv_a_norm, w_ukv, w_out, ffn_norm, w_up, conv_w, conv_b, w_down, final_norm, loss_target, m_mix_norm, m_w_in, m_a_v_norm_g, m_a_v_norm_b, m_a_spatial_w, m_a_spatial_b, m_q_a_norm, m_w_uq, m_kv_a_norm, m_w_ukv, m_w_out, m_ffn_norm, m_w_up, m_conv_w, m_conv_b, m_w_down, m_final_norm, v_mix_norm, v_w_in, v_a_v_norm_g, v_a_v_norm_b, v_a_spatial_w, v_a_spatial_b, v_q_a_norm, v_w_uq, v_kv_a_norm, v_w_ukv, v_w_out, v_ffn_norm, v_w_up, v_conv_w, v_conv_b, v_w_down, v_final_norm):
    weights = dict(mix_norm=mix_norm, w_in=w_in, a_v_norm_g=a_v_norm_g, a_v_norm_b=a_v_norm_b, a_spatial_w=a_spatial_w,
                   a_spatial_b=a_spatial_b, q_a_norm=q_a_norm, w_uq=w_uq, kv_a_norm=kv_a_norm, w_ukv=w_ukv, w_out=w_out,
                   ffn_norm=ffn_norm, w_up=w_up, conv_w=conv_w, conv_b=conv_b, w_down=w_down, final_norm=final_norm)
    m_in = dict(mix_norm=m_mix_norm, w_in=m_w_in, a_v_norm_g=m_a_v_norm_g, a_v_norm_b=m_a_v_norm_b,
                a_spatial_w=m_a_spatial_w, a_spatial_b=m_a_spatial_b, q_a_norm=m_q_a_norm, w_uq=m_w_uq,
                kv_a_norm=m_kv_a_norm, w_ukv=m_w_ukv, w_out=m_w_out, ffn_norm=m_ffn_norm, w_up=m_w_up, conv_w=m_conv_w,
                conv_b=m_conv_b, w_down=m_w_down, final_norm=m_final_norm)
    v_in = dict(mix_norm=v_mix_norm, w_in=v_w_in, a_v_norm_g=v_a_v_norm_g, a_v_norm_b=v_a_v_norm_b,
                a_spatial_w=v_a_spatial_w, a_spatial_b=v_a_spatial_b, q_a_norm=v_q_a_norm, w_uq=v_w_uq,
                kv_a_norm=v_kv_a_norm, w_ukv=v_w_ukv, w_out=v_w_out, ffn_norm=v_ffn_norm, w_up=v_w_up, conv_w=v_conv_w,
                conv_b=v_conv_b, w_down=v_w_down, final_norm=v_final_norm)
    names = list(weights)
    chip = 2 * lax.axis_index("x") + lax.axis_index("y")

    def halves(a):
        return a.reshape(a.shape[:-2] + (2, a.shape[-2] // 2, a.shape[-1]))

    w_in_t = jnp.swapaxes(w_in[0], 0, 1).astype(MXU_DTYPE)
    w_in_gather = _exchange_start([jnp.stack(jnp.split(w_in_t, 2, axis=1))], "w_in_gather_start", "halves",
                                  after=positions)
    gathers = {}
    wts = dict(
        mix_norm=mix_norm, a_v_norm_g=a_v_norm_g, a_v_norm_b=a_v_norm_b, a_spatial_w=a_spatial_w[0],
        a_spatial_b=a_spatial_b[0], q_a_norm=q_a_norm, kv_a_norm=kv_a_norm, ffn_norm=ffn_norm,
        final_norm=final_norm.reshape(1, D_MODEL), conv_b=conv_b.reshape(2, 1, D_FF))

    mixer_shards = [weights[n][0].astype(MXU_DTYPE) for n in _BIG[1:4]]
    ffn_shards = [w_up[0].astype(MXU_DTYPE), w_down[0].astype(MXU_DTYPE)]

    def in_weights(after):
        _, landed = _exchange_wait(w_in_gather, "w_in_gather_wait", "halves", list(after) + mixer_shards + ffn_shards)
        (w_in_sh,) = _forward_halves(list(landed))
        gathers["mixer"] = _exchange_start(mixer_shards, "mixer_gather_start", "gather", after=w_in_sh)
        gathers["ffn"] = _exchange_start(ffn_shards + [conv_w[0]], "ffn_gather_start", "gather",
                                         after=gathers["mixer"][4])
        w_in_pad = _w_in_t_to_pad(jnp.concatenate([w_in_sh[:, 0], w_in_sh[:, 1]], axis=-1).reshape(-1, D_MODEL))
        return w_in_pad, gathers["ffn"][4]

    def mixer_weights(after):
        _, (w_uq_sh, w_ukv_sh, w_out_sh) = _exchange_wait(gathers["mixer"], "mixer_gather_wait", "gather", after)
        return (_w_uq_to_pad(_cols_from_chips(w_uq_sh)), _w_ukv_to_pad(_cols_from_chips(w_ukv_sh)),
                w_out_sh.reshape(D_MODEL, D_MODEL))

    def ffn_weights(after):
        _, (w_up_sh, w_down_sh, cw_all) = _exchange_wait(gathers["ffn"], "ffn_gather_wait", "gather", after)
        return w_up_sh, w_down_sh.reshape(D_FF, D_MODEL), _conv_w_split(_cols_from_chips(cw_all))

    scatters = {}

    def start_scatter(slabs, slabs_lo, tag):
        got = _swap_halves([halves(s) for s in slabs_lo], tag + "_grad_swap_halves")
        sums = _pair_sum(slabs, got, tag + "_grad_pair_sum")
        scatters[tag] = _exchange_start(list(sums), tag + "_scatter_start", "scatter", after=slabs[-1])
        return scatters[tag][4]

    def on_ffn_grads(g):
        slabs, slabs_lo = [[g["w_up" + lo], g["w_down" + lo].reshape(N_CHIPS, D_FF // N_CHIPS, D_MODEL)]
                           for lo in ("", "_lo")]
        swap = _exchange_start([halves(s) for s in slabs_lo], "ffn_swap_start", "swap", after=slabs[1])

        def sent(after):
            _, got = _exchange_wait(swap, "ffn_swap_wait", "swap", after)
            sums = _pair_sum(slabs, got, "ffn_grad_pair_sum")
            scatters["ffn"] = _exchange_start(list(sums), "ffn_scatter_start", "scatter", after=got[0])
            return scatters["ffn"][4]

        return swap[4], sent

    def on_mixer_grads(g):
        slabs = [_w_in_t_from_pad(g["w_in"]).reshape(N_CHIPS, -1, D_MODEL), _cols_to_chips(_w_uq_from_pad(g["w_q"])),
                 _cols_to_chips(_w_ukv_from_pad(g["w_kv"]))]
        w_out_slabs = [g["w_out" + lo].reshape(N_CHIPS, D_MODEL // N_CHIPS, D_MODEL) for lo in ("", "_lo")]
        return start_scatter(slabs + w_out_slabs[:1], [s.astype(GRAD_PAYLOAD) for s in slabs] + w_out_slabs[1:], "mixer")

    loss_part, grad_x, g = _local_step(x, positions, loss_target, wts, in_weights, mixer_weights, ffn_weights,
                                       on_ffn_grads, on_mixer_grads)

    g_small_parts = dict(g)
    g_small_parts["conv_w"] = _conv_w_join(g["conv_w"])
    g_small_parts["conv_b"] = g["conv_b"].reshape(1, 2 * D_FF)
    flat = jnp.concatenate([g_small_parts[n].reshape(-1) for n, _ in _SMALL] + [loss_part.reshape(1)])
    flat = jnp.pad(flat, (0, _SMALL_ROWS * 128 - flat.shape[0])).reshape(_SMALL_ROWS, 128)
    small_gather = _exchange_start([flat], "small_gather_start", "all", after=grad_x)

    mixer_sums, mixer_landed = _exchange_wait(scatters["mixer"], "mixer_scatter_wait", "scatter", after=small_gather[4])
    ffn_sums, ffn_landed = _exchange_wait(scatters["ffn"], "ffn_scatter_wait", "scatter", after=mixer_landed[0])
    reduced = _chip_sum(list(mixer_sums) + list(ffn_sums), list(mixer_landed) + list(ffn_landed))
    g_big = dict(zip(_BIG, _join_halves(reduced)))

    grads, deltas, new_m, new_v = {}, {}, {}, {}

    def update(n, grad):
        w = weights[n]
        shape2 = grad.shape
        d, nm, nv = _adamw(w.reshape(shape2), grad, m_in[n].reshape(shape2), v_in[n].reshape(shape2), "adamw_" + n)
        grads[n], deltas[n], new_m[n], new_v[n] = (t.reshape(w.shape) for t in (grad, d, nm, nv))

    def update_transposed(n, grad_t):
        t = lambda a: jnp.swapaxes(a, 1, 2)
        d, nm, nv = _adamw(t(weights[n]), grad_t, t(m_in[n]), t(v_in[n]), "adamw_" + n)
        grads[n], deltas[n], new_m[n], new_v[n] = t(grad_t), t(d), t(nm), t(nv)

    for n in _BIG:
        g3 = g_big[n].reshape((1, -1, g_big[n].shape[-1]))
        if n == "w_in":
            update_transposed(n, g3)
        else:
            update(n, g3)

    (own,), (everyone,) = _exchange_wait(small_gather, "small_gather_wait", "all", after=[deltas[n] for n in _BIG])
    device = 2 * chip + lax.axis_index("c")
    everyone = lax.dynamic_update_slice(everyone, own[None], (device, 0, 0))
    total = _sum_slabs([everyone[j] for j in range(8)], "small_grads_sum", tr=_SMALL_ROWS).reshape(-1)
    o = 0
    for n, shp in _SMALL:
        piece = total[o:o + math.prod(shp)].reshape(shp)
        o += math.prod(shp)
        if n == "conv_w":
            piece = lax.dynamic_slice_in_dim(piece, chip * UP_SHARD, UP_SHARD, axis=1)
        update(n, piece)
    loss = total[_SMALL_SIZE]
    return (loss, grad_x, *[grads[n] for n in names], *[deltas[n] for n in names], *[new_m[n] for n in names],
            *[new_v[n] for n in names])
```

```python
import functools
import math

import jax
import jax.numpy as jnp
from jax import lax
from jax.experimental import pallas as pl
from jax.experimental.pallas import tpu as pltpu

F32 = jnp.float32
MXU_DTYPE = jnp.bfloat16
MESH = pl.DeviceIdType.MESH

D_MODEL = 1024
EPS = 1e-6
A_GROUPS = 8
CHUNK = 128
HEADS = 8
NOPE = 128
ROPE = 64
QK_DIM = NOPE + ROPE
HEAD_PAD = 256
Q_RANK = 256
KV_RANK = 128
ROPE_THETA = 10000.0
D_FF = 2816
FF_TILE = 256
N_FF_TILES = D_FF // FF_TILE
LAT = 512
IN_PAD = 4 * D_MODEL + LAT
N_CHIPS = 4
ADAM_LR, ADAM_B1, ADAM_B2, ADAM_EPS, ADAM_WD, ADAM_STEP = 0.001, 0.9, 0.999, 1e-08, 0.01, 10

VMEM_CAP_V7X = 64 * 1024 * 1024
NEG = -1e30


def _params(sem, nbytes):
    limit = int(min(VMEM_CAP_V7X - (8 << 20), max(32 << 20, 3 * nbytes)))
    return pltpu.CompilerParams(dimension_semantics=sem, vmem_limit_bytes=limit)


def _nbytes(shape, dtype):
    return math.prod(shape) * jnp.dtype(dtype).itemsize


_DIMS = {"nn": (((1,), (0,)), ((), ())), "nt": (((1,), (1,)), ((), ())), "tn": (((0,), (0,)), ((), ()))}


def _mm(a, b, mode, name, *, tm, tn, tk, out_dtype=F32, add=None, dims=None, a_spec=None, b_spec=None,
        o_spec=None, out_shape=None, n_outer=False, copy_dtype=None, after=None):
    if dims is None:
        if mode == "nn":
            (M, K), (_, N) = a.shape, b.shape
        elif mode == "nt":
            (M, K), (N, _) = a.shape, b.shape
        else:
            (K, M), (_, N) = a.shape, b.shape
    else:
        M, N, K = dims
    a_blk = (tk, tm) if mode == "tn" else (tm, tk)
    b_blk = (tn, tk) if mode == "nt" else (tk, tn)
    if a_spec is None:
        a_spec = pl.BlockSpec(a_blk, (lambda i, j, k: (k, i)) if mode == "tn" else (lambda i, j, k: (i, k)))
    if b_spec is None:
        b_spec = pl.BlockSpec(b_blk, (lambda i, j, k: (j, k)) if mode == "nt" else (lambda i, j, k: (k, j)))
    if o_spec is None:
        o_spec = pl.BlockSpec((tm, tn), lambda i, j, k: (i, j))
    if out_shape is None:
        out_shape = (M, N)
    assert M % tm == 0 and N % tn == 0 and K % tk == 0, (name, M, N, K, tm, tn, tk)
    nk = K // tk
    contract = _DIMS[mode]
    has_add = add is not None

    def body(*refs):
        a_ref, b_ref = refs[0], refs[1]
        add_ref = refs[2] if has_add else None
        n_in = 2 + has_add + (after is not None)
        o_ref = refs[n_in]
        copy_ref = refs[n_in + 1] if copy_dtype is not None else None

        def product():
            return lax.dot_general(a_ref[...].astype(MXU_DTYPE), b_ref[...].astype(MXU_DTYPE), contract,
                                   preferred_element_type=F32)

        def finish(r):
            if has_add:
                r = r + add_ref[...]
            o_ref[...] = r.astype(out_dtype)
            if copy_ref is not None:
                copy_ref[...] = r.astype(copy_dtype)

        if nk == 1:
            finish(product())
            return
        acc = refs[-1]
        k = pl.program_id(2)

        @pl.when(k == 0)
        def _():
            acc[...] = jnp.zeros_like(acc)

        acc[...] += product()

        @pl.when(k == nk - 1)
        def _():
            finish(acc[...])

    in_specs = [a_spec, b_spec]
    args = [a, b]
    nbytes = _nbytes(a_blk, a.dtype) + _nbytes(b_blk, b.dtype) + 3 * _nbytes((tm, tn), F32)
    if has_add:
        in_specs.append(pl.BlockSpec((tm, tn), lambda i, j, k: (i, j)))
        args.append(add)
        nbytes += _nbytes((tm, tn), F32)
    if after is not None:
        in_specs.append(pl.BlockSpec(after.shape, lambda i, j, k: (0, 0)))
        args.append(after)
    grid = (M // tm, N // tn, nk)
    if n_outer:
        def swapped(spec):
            return pl.BlockSpec(spec.block_shape, lambda j, i, k, at=spec.index_map: at(i, j, k))

        grid = (N // tn, M // tm, nk)
        in_specs = [swapped(s) for s in in_specs]
        o_spec = swapped(o_spec)
    out_sds, out_specs = jax.ShapeDtypeStruct(out_shape, out_dtype), o_spec
    if copy_dtype is not None:
        out_sds, out_specs = (out_sds, jax.ShapeDtypeStruct(out_shape, copy_dtype)), (o_spec, o_spec)
    return pl.pallas_call(
        body, name=name, out_shape=out_sds, grid=grid, in_specs=in_specs, out_specs=out_specs,
        scratch_shapes=[pltpu.VMEM((tm, tn), F32)] if nk > 1 else [],
        compiler_params=_params(("parallel", "parallel", "arbitrary"), nbytes),
    )(*args)


_GELU_C = math.sqrt(2.0 / math.pi)
_GELU_A = 0.044715


def _sigmoid(x):
    return 0.5 * jnp.tanh(0.5 * x) + 0.5


def _gelu(x):
    t = jnp.tanh(x * (_GELU_C + (_GELU_C * _GELU_A) * (x * x)))
    return x * (0.5 + 0.5 * t)


def _gelu_and_grad(x):
    x2 = x * x
    t = jnp.tanh(x * (_GELU_C + (_GELU_C * _GELU_A) * x2))
    cdf = 0.5 + 0.5 * t
    grad = cdf + (0.5 * x) * (1.0 - t * t) * (_GELU_C + (3.0 * _GELU_C * _GELU_A) * x2)
    return x * cdf, grad


def _rope_mix(g, cos_a, sin_a):
    return g * cos_a + pltpu.roll(g, 64, 1) * sin_a


def _rope_mix_bwd(d, cos_a, sin_a):
    return d * cos_a + pltpu.roll(d * sin_a, 64, 1)


def _rms_fwd(x, g, name, tr=512):
    T, D = x.shape

    def body(x_ref, g_ref, h_ref):
        xv = x_ref[...]
        r = lax.rsqrt(jnp.mean(xv * xv, axis=-1, keepdims=True) + EPS)
        h_ref[...] = ((xv * r) * g_ref[...]).astype(h_ref.dtype)

    return pl.pallas_call(
        body, name=name, out_shape=jax.ShapeDtypeStruct((T, D), MXU_DTYPE), grid=(T // tr,),
        in_specs=[pl.BlockSpec((tr, D), lambda i: (i, 0)), pl.BlockSpec((1, D), lambda i: (0, 0))],
        out_specs=pl.BlockSpec((tr, D), lambda i: (i, 0)),
        compiler_params=_params(("parallel",), 3 * _nbytes((tr, D), F32)),
    )(x, g)


def _rms_bwd(x, g, dh, dres, name, tr=512):
    T, D = x.shape

    def body(x_ref, g_ref, dh_ref, dres_ref, dx_ref, gg_ref):
        @pl.when(pl.program_id(0) == 0)
        def _():
            gg_ref[...] = jnp.zeros_like(gg_ref)

        xv = x_ref[...]
        r = lax.rsqrt(jnp.mean(xv * xv, axis=-1, keepdims=True) + EPS)
        xn = xv * r
        dhv = dh_ref[...]
        dxn = dhv * g_ref[...]
        dx_ref[...] = dres_ref[...] + r * (dxn - xn * jnp.mean(dxn * xn, axis=-1, keepdims=True))
        gg_ref[...] += jnp.sum(dhv * xn, axis=0, keepdims=True)

    row = pl.BlockSpec((tr, D), lambda i: (i, 0))
    vec = pl.BlockSpec((1, D), lambda i: (0, 0))
    return pl.pallas_call(
        body, name=name,
        out_shape=(jax.ShapeDtypeStruct((T, D), F32), jax.ShapeDtypeStruct((1, D), F32)),
        grid=(T // tr,), in_specs=[row, vec, row, row], out_specs=(row, vec),
        compiler_params=_params(("arbitrary",), 6 * _nbytes((tr, D), F32)),
    )(x, g, dh, dres)


def _lat_fwd(z, gq, gkv, wq, wkv, cos_a, sin_a, tr=256):
    T = z.shape[0]
    lat_blk = (4 * D_MODEL) // LAT

    def body(z_ref, gq_ref, gkv_ref, wq_ref, wkv_ref, cos_ref, sin_ref, q_ref, k_ref, v_ref, cqn_ref, ckvn_ref):
        zl = z_ref[...]
        cos_v, sin_v = cos_ref[...], sin_ref[...]
        cq = zl[:, :Q_RANK]
        ckv = zl[:, Q_RANK:Q_RANK + KV_RANK]
        krb = zl[:, Q_RANK + KV_RANK:]
        cqn = ((cq * lax.rsqrt(jnp.mean(cq * cq, axis=-1, keepdims=True) + EPS)) * gq_ref[...]).astype(MXU_DTYPE)
        ckvn = ((ckv * lax.rsqrt(jnp.mean(ckv * ckv, axis=-1, keepdims=True) + EPS)) * gkv_ref[...]).astype(MXU_DTYPE)
        cqn_ref[...] = cqn
        ckvn_ref[...] = ckvn
        krr = _rope_mix(krb, cos_v, sin_v).astype(MXU_DTYPE)
        q = jnp.dot(cqn, wq_ref[...], preferred_element_type=F32)
        kv = jnp.dot(ckvn, wkv_ref[...], preferred_element_type=F32)
        for h in range(HEADS):
            o = h * HEAD_PAD
            q_ref[:, o:o + NOPE] = q[:, o:o + NOPE].astype(MXU_DTYPE)
            q_ref[:, o + NOPE:o + HEAD_PAD] = _rope_mix(q[:, o + NOPE:o + HEAD_PAD], cos_v, sin_v).astype(MXU_DTYPE)
            k_ref[:, o:o + NOPE] = kv[:, h * NOPE:(h + 1) * NOPE].astype(MXU_DTYPE)
            k_ref[:, o + NOPE:o + HEAD_PAD] = krr
        v_ref[...] = kv[:, HEADS * NOPE:].astype(MXU_DTYPE)

    def row(w):
        return pl.BlockSpec((tr, w), lambda i: (i, 0))

    def full(a):
        return pl.BlockSpec(a.shape, lambda i: (0, 0))

    return pl.pallas_call(
        body, name="lat_fwd",
        out_shape=(jax.ShapeDtypeStruct((T, HEADS * HEAD_PAD), MXU_DTYPE), jax.ShapeDtypeStruct((T, HEADS * HEAD_PAD), MXU_DTYPE),
                   jax.ShapeDtypeStruct((T, HEADS * NOPE), MXU_DTYPE), jax.ShapeDtypeStruct((T, Q_RANK), MXU_DTYPE),
                   jax.ShapeDtypeStruct((T, KV_RANK), MXU_DTYPE)),
        grid=(T // tr,),
        in_specs=[pl.BlockSpec((tr, LAT), lambda i: (i, lat_blk)), full(gq), full(gkv), full(wq), full(wkv), row(128), row(128)],
        out_specs=(row(HEADS * HEAD_PAD), row(HEADS * HEAD_PAD), row(HEADS * NOPE), row(Q_RANK), row(KV_RANK)),
        compiler_params=_params(("parallel",), 8 * _nbytes((tr, HEADS * HEAD_PAD), F32)),
    )(z, gq, gkv, wq, wkv, cos_a, sin_a)


ATT_BLOCK = 256
_SCALE = QK_DIM ** -0.5


def _causal_mask(n):
    return lax.broadcasted_iota(jnp.int32, (n, n), 1) <= lax.broadcasted_iota(jnp.int32, (n, n), 0)


def _causal_mask_t(n):
    return lax.broadcasted_iota(jnp.int32, (n, n), 0) <= lax.broadcasted_iota(jnp.int32, (n, n), 1)


ATT_HEADS = 4


def _attn_fwd(q, k, v, B, S):
    tq = ATT_BLOCK
    nq = S // tq
    T = B * S
    hp, groups = ATT_HEADS, HEADS // ATT_HEADS

    def body(q_ref, k_ref, v_ref, o_ref, *lse_refs):
        qi = pl.program_id(2)
        qs = [q_ref[:, t * HEAD_PAD:(t + 1) * HEAD_PAD] for t in range(hp)]

        def scores(j, t):
            rows = pl.ds(pl.multiple_of(j * tq, tq), tq)
            return lax.dot_general(k_ref[rows, t * HEAD_PAD:(t + 1) * HEAD_PAD], qs[t], _DIMS["nt"],
                                   preferred_element_type=F32)

        def step(j, carry, last):
            rows = pl.ds(pl.multiple_of(j * tq, tq), tq)
            out = []
            for t in range(hp):
                m, l, acc, st = carry[t]
                st_next = st if last else scores(j + 1, t)
                st = st * _SCALE
                if last:
                    st = jnp.where(_causal_mask_t(tq), st, NEG)
                m_new = jnp.maximum(m, jnp.max(st, axis=0, keepdims=True))
                alpha = jnp.exp(m - m_new)
                p = jnp.exp(st - m_new)
                l = alpha * l + jnp.sum(p, axis=0, keepdims=True)
                acc = alpha * acc + lax.dot_general(v_ref[rows, t * NOPE:(t + 1) * NOPE], p.astype(MXU_DTYPE),
                                                    _DIMS["tn"], preferred_element_type=F32)
                out.append((m_new, l, acc, st_next))
            return tuple(out)

        init = tuple((jnp.full((1, tq), NEG, F32), jnp.zeros((1, tq), F32), jnp.zeros((NOPE, tq), F32), scores(0, t))
                     for t in range(hp))
        carry = lax.fori_loop(0, qi, lambda j, c: step(j, c, False), init)
        carry = step(qi, carry, True)
        for t in range(hp):
            m, l, acc, _ = carry[t]
            o_ref[:, t * NOPE:(t + 1) * NOPE] = (acc / l).T
            lse_refs[t][0] = m + jnp.log(l)

    lse_sds = jax.ShapeDtypeStruct((groups * B * nq, 1, tq), F32)
    lse_spec = pl.BlockSpec((1, 1, tq), lambda b, h, i: ((h * B + b) * nq + i, 0, 0))
    return pl.pallas_call(
        body, name="attn_fwd",
        out_shape=(jax.ShapeDtypeStruct((T, HEADS * NOPE), F32),) + (lse_sds,) * hp,
        grid=(B, groups, nq),
        in_specs=[pl.BlockSpec((tq, hp * HEAD_PAD), lambda b, h, i: (b * nq + i, h)),
                  pl.BlockSpec((S, hp * HEAD_PAD), lambda b, h, i: (b, h)),
                  pl.BlockSpec((S, hp * NOPE), lambda b, h, i: (b, h))],
        out_specs=(pl.BlockSpec((tq, hp * NOPE), lambda b, h, i: (b * nq + i, h)),) + (lse_spec,) * hp,
        compiler_params=_params(("parallel", "parallel", "arbitrary"), 4 * hp * _nbytes((S, HEAD_PAD), MXU_DTYPE)),
    )(q, k, v)


def _attn_bwd(q, k, v, do, lses, delta, B, S):
    tq = ATT_BLOCK
    nq = S // tq
    T = B * S
    hp, groups = ATT_HEADS, HEADS // ATT_HEADS

    def body(q_ref, k_ref, v_ref, do_ref, *refs):
        lse_refs, dl_refs = refs[:hp], refs[hp:2 * hp]
        dq_out, dk_ref, dv_ref, dq_ref = refs[2 * hp:]
        kj = pl.program_id(2)

        @pl.when(kj == 0)
        def _():
            dq_ref[...] = jnp.zeros_like(dq_ref)

        def products(i, t):
            rows = pl.ds(pl.multiple_of(i * tq, tq), tq)
            st = lax.dot_general(k_ref[:, t * HEAD_PAD:(t + 1) * HEAD_PAD], q_ref[rows, t * HEAD_PAD:(t + 1) * HEAD_PAD],
                                 _DIMS["nt"], preferred_element_type=F32)
            dpt = lax.dot_general(v_ref[:, t * NOPE:(t + 1) * NOPE], do_ref[rows, t * NOPE:(t + 1) * NOPE],
                                  _DIMS["nt"], preferred_element_type=F32)
            return st, dpt

        def step(i, carry, masked):
            rows = pl.ds(pl.multiple_of(i * tq, tq), tq)
            nxt = jnp.minimum(i + 1, nq - 1)
            out = []
            for t in range(hp):
                dk, dv, st, dpt = carry[t]
                st_next, dpt_next = products(nxt, t)
                qk_cols = slice(t * HEAD_PAD, (t + 1) * HEAD_PAD)
                v_cols = slice(t * NOPE, (t + 1) * NOPE)
                p = jnp.exp(st * _SCALE - lse_refs[t][i])
                if masked:
                    p = jnp.where(_causal_mask_t(tq), p, 0.0)
                dv = dv + jnp.dot(p.astype(MXU_DTYPE), do_ref[rows, v_cols], preferred_element_type=F32)
                ds = (p * (dpt - dl_refs[t][i]) * _SCALE).astype(MXU_DTYPE)
                dk = dk + jnp.dot(ds, q_ref[rows, qk_cols], preferred_element_type=F32)
                dq_ref[rows, qk_cols] += lax.dot_general(ds, k_ref[:, qk_cols], _DIMS["tn"], preferred_element_type=F32)
                out.append((dk, dv, st_next, dpt_next))
            return tuple(out)

        init = tuple((jnp.zeros((tq, HEAD_PAD), F32), jnp.zeros((tq, NOPE), F32)) + products(kj, t) for t in range(hp))
        carry = step(kj, init, True)
        carry = lax.fori_loop(kj + 1, nq, lambda i, c: step(i, c, False), carry)
        for t in range(hp):
            dk_ref[:, t * HEAD_PAD:(t + 1) * HEAD_PAD] = carry[t][0].astype(dk_ref.dtype)
            dv_ref[:, t * NOPE:(t + 1) * NOPE] = carry[t][1].astype(dv_ref.dtype)

        @pl.when(kj == nq - 1)
        def _():
            dq_out[...] = dq_ref[...].astype(dq_out.dtype)

    seq = lambda w: pl.BlockSpec((S, w), lambda b, h, j: (b, h))
    blk = lambda w: pl.BlockSpec((tq, w), lambda b, h, j: (b * nq + j, h))
    lse_spec = pl.BlockSpec((nq, 1, tq), lambda b, h, j: (h * B + b, 0, 0))
    dl_specs = [pl.BlockSpec((nq, 1, tq), lambda b, h, j, t=t: ((h * hp + t) * B + b, 0, 0)) for t in range(hp)]
    return pl.pallas_call(
        body, name="attn_bwd",
        out_shape=(jax.ShapeDtypeStruct((T, HEADS * HEAD_PAD), MXU_DTYPE), jax.ShapeDtypeStruct((T, HEADS * HEAD_PAD), MXU_DTYPE),
                   jax.ShapeDtypeStruct((T, HEADS * NOPE), MXU_DTYPE)),
        grid=(B, groups, nq),
        in_specs=[seq(hp * HEAD_PAD), blk(hp * HEAD_PAD), blk(hp * NOPE), seq(hp * NOPE)] + [lse_spec] * hp + dl_specs,
        out_specs=(seq(hp * HEAD_PAD), blk(hp * HEAD_PAD), blk(hp * NOPE)),
        scratch_shapes=[pltpu.VMEM((S, hp * HEAD_PAD), F32)],
        compiler_params=_params(("parallel", "parallel", "arbitrary"), 8 * hp * _nbytes((S, HEAD_PAD), F32)),
    )(q, k, v, do, *lses, *([delta] * hp))


MIX_ROWS = 256


def _tril_weights(ws_ref, g):
    return jnp.where(_causal_mask(CHUNK), ws_ref[g], 0.0).astype(MXU_DTYPE)


def _layer_norm_stats(va):
    mu = jnp.mean(va, axis=-1, keepdims=True)
    xc = va - mu
    rs = lax.rsqrt(jnp.mean(xc * xc, axis=-1, keepdims=True) + EPS)
    return xc * rs


def _mix_specs(tr):
    zcol = lambda c: pl.BlockSpec((tr, D_MODEL), lambda i, c=c: (i, c))
    row = pl.BlockSpec((tr, D_MODEL), lambda i: (i, 0))
    vec = pl.BlockSpec((1, D_MODEL), lambda i: (0, 0))
    ws = pl.BlockSpec((A_GROUPS, CHUNK, CHUNK), lambda i: (0, 0, 0))
    bs = pl.BlockSpec((CHUNK, 128), lambda i: (0, 0))
    return zcol, row, vec, ws, bs


def _mix_fwd(z, yb, ln_g, ln_b, ws, bs_t):
    T = z.shape[0]
    tr = MIX_ROWS
    zcol, row, vec, ws_spec, bs_spec = _mix_specs(tr)

    def body(zu_ref, zv_ref, zga_ref, zgb_ref, yb_ref, g_ref, b_ref, ws_ref, bs_ref, out_ref, vn_s):
        vhat = _layer_norm_stats(_gelu(zv_ref[...]))
        vn_s[...] = (vhat * g_ref[...] + b_ref[...]).astype(MXU_DTYPE)
        for g in range(A_GROUPS):
            w = _tril_weights(ws_ref, g)
            bias = bs_ref[:, g:g + 1]
            cols = slice(g * CHUNK, (g + 1) * CHUNK)
            for c in range(tr // CHUNK):
                rows = slice(c * CHUNK, (c + 1) * CHUNK)
                mixed = jnp.dot(w, vn_s[rows, cols], preferred_element_type=F32) + bias
                ya = _gelu(zu_ref[rows, cols]) * mixed
                merged = _sigmoid(zga_ref[rows, cols]) * ya + _sigmoid(zgb_ref[rows, cols]) * yb_ref[rows, cols]
                out_ref[rows, cols] = merged.astype(MXU_DTYPE)

    return pl.pallas_call(
        body, name="mix_fwd", out_shape=jax.ShapeDtypeStruct((T, D_MODEL), MXU_DTYPE), grid=(T // tr,),
        in_specs=[zcol(0), zcol(1), zcol(2), zcol(3), row, vec, vec, ws_spec, bs_spec], out_specs=row,
        scratch_shapes=[pltpu.VMEM((tr, D_MODEL), MXU_DTYPE)],
        compiler_params=_params(("parallel",), 8 * _nbytes((tr, D_MODEL), F32)),
    )(z, z, z, z, yb, ln_g, ln_b, ws, bs_t)


def _mix_bwd(z, yb, dm, ln_g, ln_b, ws, bs_t):
    T = z.shape[0]
    tr = MIX_ROWS
    zcol, row, vec, ws_spec, bs_spec = _mix_specs(tr)

    def body(zu_ref, zv_ref, zga_ref, zgb_ref, yb_ref, dm_ref, g_ref, b_ref, ws_ref, bs_ref,
             dz_ref, dyb_ref, dl_ref, gws_ref, gbs_ref, glg_ref, glb_ref, vn_s, dvn_s):
        @pl.when(pl.program_id(0) == 0)
        def _():
            gws_ref[...] = jnp.zeros_like(gws_ref)
            gbs_ref[...] = jnp.zeros_like(gbs_ref)
            glg_ref[...] = jnp.zeros_like(glg_ref)
            glb_ref[...] = jnp.zeros_like(glb_ref)

        lane = lax.broadcasted_iota(jnp.int32, (CHUNK, 128), 1)
        va, dgelu_v = _gelu_and_grad(zv_ref[...])
        mu = jnp.mean(va, axis=-1, keepdims=True)
        xc = va - mu
        rs = lax.rsqrt(jnp.mean(xc * xc, axis=-1, keepdims=True) + EPS)
        vhat = xc * rs
        vn_s[...] = (vhat * g_ref[...] + b_ref[...]).astype(MXU_DTYPE)
        gbs_acc = jnp.zeros((CHUNK, 128), F32)
        for g in range(A_GROUPS):
            w = _tril_weights(ws_ref, g)
            bias = bs_ref[:, g:g + 1]
            cols = slice(g * CHUNK, (g + 1) * CHUNK)
            gw_acc = jnp.zeros((CHUNK, CHUNK), F32)
            for c in range(tr // CHUNK):
                rows = slice(c * CHUNK, (c + 1) * CHUNK)
                vn = vn_s[rows, cols]
                mixed = jnp.dot(w, vn, preferred_element_type=F32) + bias
                ua, dgelu_u = _gelu_and_grad(zu_ref[rows, cols])
                dmv = dm_ref[rows, cols]
                sa = _sigmoid(zga_ref[rows, cols])
                dya = dmv * sa
                dz_ref[rows, 2 * D_MODEL + g * CHUNK:2 * D_MODEL + (g + 1) * CHUNK] = (
                    dmv * (ua * mixed) * (sa * (1.0 - sa))).astype(dz_ref.dtype)
                dz_ref[rows, cols] = (dya * mixed * dgelu_u).astype(dz_ref.dtype)
                dmix = dya * ua
                gbs_acc = gbs_acc + jnp.where(lane == g, jnp.sum(dmix, axis=-1, keepdims=True), 0.0)
                dmix_b = dmix.astype(MXU_DTYPE)
                gw_acc = gw_acc + lax.dot_general(dmix_b, vn, _DIMS["nt"], preferred_element_type=F32)
                dvn_s[rows, cols] = lax.dot_general(w, dmix_b, _DIMS["tn"], preferred_element_type=F32)
            gws_ref[g] += jnp.where(_causal_mask(CHUNK), gw_acc, 0.0)
        gbs_ref[...] += gbs_acc

        dvn = dvn_s[...]
        glg_ref[...] += jnp.sum(dvn * vhat, axis=0, keepdims=True)
        glb_ref[...] += jnp.sum(dvn, axis=0, keepdims=True)
        dvh = dvn * g_ref[...]
        dva = rs * (dvh - jnp.mean(dvh, axis=-1, keepdims=True) - vhat * jnp.mean(dvh * vhat, axis=-1, keepdims=True))
        dz_ref[:, D_MODEL:2 * D_MODEL] = (dva * dgelu_v).astype(dz_ref.dtype)

        dmv = dm_ref[...]
        ybv = yb_ref[...]
        sb = _sigmoid(zgb_ref[...])
        dyb = dmv * sb
        dyb_ref[...] = dyb.astype(dyb_ref.dtype)
        dz_ref[:, 3 * D_MODEL:4 * D_MODEL] = (dmv * ybv * (sb * (1.0 - sb))).astype(dz_ref.dtype)
        dz_ref[:, 4 * D_MODEL:] = jnp.zeros((tr, LAT), dz_ref.dtype)
        prod = dyb * ybv
        sel = (lax.broadcasted_iota(jnp.int32, (HEADS, D_MODEL), 1) // NOPE
               == lax.broadcasted_iota(jnp.int32, (HEADS, D_MODEL), 0)).astype(jnp.bfloat16)
        hi = prod.astype(jnp.bfloat16)
        rest = prod - hi.astype(F32)
        mid = rest.astype(jnp.bfloat16)
        lo = (rest - mid.astype(F32)).astype(jnp.bfloat16)
        dl_ref[...] = (lax.dot_general(sel, hi, _DIMS["nt"], preferred_element_type=F32)
                       + lax.dot_general(sel, mid, _DIMS["nt"], preferred_element_type=F32)
                       + lax.dot_general(sel, lo, _DIMS["nt"], preferred_element_type=F32))

    return pl.pallas_call(
        body, name="mix_bwd",
        out_shape=(jax.ShapeDtypeStruct((T, IN_PAD), MXU_DTYPE), jax.ShapeDtypeStruct((T, D_MODEL), MXU_DTYPE),
                   jax.ShapeDtypeStruct((HEADS, T), F32), jax.ShapeDtypeStruct((A_GROUPS, CHUNK, CHUNK), F32),
                   jax.ShapeDtypeStruct((CHUNK, 128), F32), jax.ShapeDtypeStruct((1, D_MODEL), F32),
                   jax.ShapeDtypeStruct((1, D_MODEL), F32)),
        grid=(T // tr,),
        in_specs=[zcol(0), zcol(1), zcol(2), zcol(3), row, row, vec, vec, ws_spec, bs_spec],
        out_specs=(pl.BlockSpec((tr, IN_PAD), lambda i: (i, 0)), row, pl.BlockSpec((HEADS, tr), lambda i: (0, i)),
                   ws_spec, bs_spec, vec, vec),
        scratch_shapes=[pltpu.VMEM((tr, D_MODEL), MXU_DTYPE), pltpu.VMEM((tr, D_MODEL), F32)],
        compiler_params=_params(("arbitrary",), 12 * _nbytes((tr, D_MODEL), F32)),
    )(z, z, z, z, yb, dm, ln_g, ln_b, ws, bs_t)


def _lat_bwd(dz, z, dq, dk, dv, gq, gkv, wq, wkv, cos_a, sin_a, tr=256):
    T = z.shape[0]
    lat_blk = (4 * D_MODEL) // LAT

    def body(dz_in, z_ref, dq_ref, dk_ref, dv_ref, gq_ref, gkv_ref, wq_ref, wkv_ref, cos_ref, sin_ref,
             dz_ref, dqr_ref, dkv_ref, ggq_ref, ggkv_ref):
        del dz_in

        @pl.when(pl.program_id(0) == 0)
        def _():
            ggq_ref[...] = jnp.zeros_like(ggq_ref)
            ggkv_ref[...] = jnp.zeros_like(ggkv_ref)

        cos_v, sin_v = cos_ref[...], sin_ref[...]
        dkr = jnp.zeros((tr, 128), F32)
        for h in range(HEADS):
            o = h * HEAD_PAD
            dqr_ref[:, o:o + NOPE] = dq_ref[:, o:o + NOPE].astype(MXU_DTYPE)
            dqr_ref[:, o + NOPE:o + HEAD_PAD] = _rope_mix_bwd(dq_ref[:, o + NOPE:o + HEAD_PAD], cos_v, sin_v).astype(MXU_DTYPE)
            dkv_ref[:, h * NOPE:(h + 1) * NOPE] = dk_ref[:, o:o + NOPE].astype(MXU_DTYPE)
            dkr = dkr + _rope_mix_bwd(dk_ref[:, o + NOPE:o + HEAD_PAD], cos_v, sin_v)
        dkv_ref[:, HEADS * NOPE:] = dv_ref[...]
        dcqn = lax.dot_general(dqr_ref[...], wq_ref[...], _DIMS["nt"], preferred_element_type=F32)
        dckvn = lax.dot_general(dkv_ref[...], wkv_ref[...], _DIMS["nt"], preferred_element_type=F32)

        zl = z_ref[...]

        def rms_bwd(c, dn, g_ref, gg_ref):
            r = lax.rsqrt(jnp.mean(c * c, axis=-1, keepdims=True) + EPS)
            ch = c * r
            gg_ref[...] += jnp.sum(dn * ch, axis=0, keepdims=True)
            dch = dn * g_ref[...]
            return r * (dch - ch * jnp.mean(dch * ch, axis=-1, keepdims=True))

        dz_ref[:, :Q_RANK] = rms_bwd(zl[:, :Q_RANK], dcqn, gq_ref, ggq_ref).astype(dz_ref.dtype)
        dz_ref[:, Q_RANK:Q_RANK + KV_RANK] = rms_bwd(zl[:, Q_RANK:Q_RANK + KV_RANK], dckvn, gkv_ref, ggkv_ref).astype(dz_ref.dtype)
        dz_ref[:, Q_RANK + KV_RANK:] = dkr.astype(dz_ref.dtype)

    def row(w):
        return pl.BlockSpec((tr, w), lambda i: (i, 0))

    def full(a):
        return pl.BlockSpec(a.shape, lambda i: (0, 0))

    lat = pl.BlockSpec((tr, LAT), lambda i: (i, lat_blk))
    return pl.pallas_call(
        body, name="lat_bwd",
        out_shape=(jax.ShapeDtypeStruct(dz.shape, dz.dtype), jax.ShapeDtypeStruct((T, HEADS * HEAD_PAD), MXU_DTYPE),
                   jax.ShapeDtypeStruct((T, 2 * HEADS * NOPE), MXU_DTYPE), jax.ShapeDtypeStruct(gq.shape, F32),
                   jax.ShapeDtypeStruct(gkv.shape, F32)),
        grid=(T // tr,),
        in_specs=[pl.BlockSpec(memory_space=pl.ANY), lat, row(HEADS * HEAD_PAD), row(HEADS * HEAD_PAD), row(HEADS * NOPE),
                  full(gq), full(gkv), full(wq), full(wkv), row(128), row(128)],
        out_specs=(lat, row(HEADS * HEAD_PAD), row(2 * HEADS * NOPE), full(gq), full(gkv)),
        input_output_aliases={0: 0},
        compiler_params=_params(("arbitrary",), 8 * _nbytes((tr, HEADS * HEAD_PAD), F32)),
    )(dz, z, dq, dk, dv, gq, gkv, wq, wkv, cos_a, sin_a)


GATE_ROWS = 64
HALO = 8


def _taps(ref, half, r, first):
    C = GATE_ROWS
    if first:
        xs = jnp.concatenate([jnp.zeros((HALO, ref.shape[-1]), F32), ref[half, 0:C, :]], axis=0)
    else:
        xs = ref[half, pl.ds(pl.multiple_of(r * C - HALO, HALO), C + HALO), :]
    return xs[HALO:, :], pltpu.roll(xs, 1, 0)[HALO:, :], pltpu.roll(xs, 2, 0)[HALO:, :]


def _conv_taps(taps, cw, cb):
    x0, x1, x2 = taps
    return cb + cw[0:1, :] * x2 + cw[1:2, :] * x1 + cw[2:3, :] * x0


def _fold8(x):
    acc = x[0:8, :]
    for i in range(1, x.shape[0] // 8):
        acc = acc + x[8 * i:8 * (i + 1), :]
    return acc


def _gate_fwd(up3, conv_w, conv_b, B, S):
    T = B * S
    W = FF_TILE
    C = GATE_ROWS

    def body(up_ref, cw_ref, cb_ref, act_ref, conv_ref):
        def chunk(r, first):
            gate = _conv_taps(_taps(up_ref, 0, r, first), cw_ref[0], cb_ref[0])
            val = _conv_taps(_taps(up_ref, 1, r, first), cw_ref[1], cb_ref[1])
            rows = pl.ds(0 if first else pl.multiple_of(r * C, C), C)
            conv_ref[0, rows, :] = gate.astype(conv_ref.dtype)
            conv_ref[1, rows, :] = val.astype(conv_ref.dtype)
            act_ref[rows, :] = (gate * _sigmoid(gate) * val).astype(act_ref.dtype)

        chunk(0, True)

        @pl.loop(1, S // C)
        def _(r):
            chunk(r, False)

    up_spec = pl.BlockSpec((2, S, W), lambda b, j: (0, b, j))
    return pl.pallas_call(
        body, name="gate_fwd",
        out_shape=(jax.ShapeDtypeStruct((T, D_FF), MXU_DTYPE), jax.ShapeDtypeStruct((2, T, D_FF), MXU_DTYPE)),
        grid=(B, N_FF_TILES),
        in_specs=[up_spec, pl.BlockSpec((2, 3, W), lambda b, j: (0, 0, j)), pl.BlockSpec((2, 1, W), lambda b, j: (0, 0, j))],
        out_specs=(pl.BlockSpec((S, W), lambda b, j: (b, j)), up_spec),
        compiler_params=_params(("parallel", "parallel"), 8 * _nbytes((S, W), F32)),
    )(up3, conv_w, conv_b)


def _gate_bwd(up3, conv3, dact, conv_w, B, S):
    T = B * S
    W = FF_TILE
    C = GATE_ROWS

    def body(up_ref, conv_ref, da_ref, cw_ref, dup_ref, gcw_ref, gcb_ref, d_s):
        @pl.when(pl.program_id(1) == 0)
        def _():
            gcw_ref[...] = jnp.zeros_like(gcw_ref)
            gcb_ref[...] = jnp.zeros_like(gcb_ref)

        def chunk(r, first, sums):
            rows = pl.ds(0 if first else pl.multiple_of(r * C, C), C)
            taps = [_taps(up_ref, half, r, first) for half in (0, 1)]
            gate, val = conv_ref[0, rows, :].astype(F32), conv_ref[1, rows, :].astype(F32)
            sg = _sigmoid(gate)
            da = da_ref[rows, :]
            d_halves = (da * val * (sg * (1.0 + gate * (1.0 - sg))), da * (gate * sg))
            out = []
            for half, dup in enumerate(d_halves):
                d_s[half, rows, :] = dup
                x0, x1, x2 = taps[half]
                sb, s0, s1, s2 = sums[half]
                out.append((sb + _fold8(dup), s0 + _fold8(dup * x2), s1 + _fold8(dup * x1), s2 + _fold8(dup * x0)))
            return tuple(out)

        zeros = tuple(tuple(jnp.zeros((8, W), F32) for _ in range(4)) for _ in range(2))
        sums = chunk(0, True, zeros)
        sums = lax.fori_loop(1, S // C, lambda r, s: chunk(r, False, s), sums)
        for half in (0, 1):
            sb, s0, s1, s2 = sums[half]
            gcb_ref[half] += jnp.sum(sb, axis=0, keepdims=True)
            gcw_ref[half, 0:1, :] += jnp.sum(s0, axis=0, keepdims=True)
            gcw_ref[half, 1:2, :] += jnp.sum(s1, axis=0, keepdims=True)
            gcw_ref[half, 2:3, :] += jnp.sum(s2, axis=0, keepdims=True)

        d_s[:, S:S + HALO, :] = jnp.zeros((2, HALO, W), F32)

        @pl.loop(0, S // C)
        def _(r):
            base = pl.multiple_of(r * C, C)
            for half in (0, 1):
                ds_ = d_s[half, pl.ds(base, C + HALO), :]
                cw = cw_ref[half]
                dx = (cw[2:3, :] * ds_[:C, :] + cw[1:2, :] * pltpu.roll(ds_, C + HALO - 1, 0)[:C, :]
                      + cw[0:1, :] * pltpu.roll(ds_, C + HALO - 2, 0)[:C, :])
                dup_ref[half, pl.ds(base, C), :] = dx.astype(dup_ref.dtype)

    up_spec = pl.BlockSpec((2, S, W), lambda j, b: (0, b, j))
    cw_spec = pl.BlockSpec((2, 3, W), lambda j, b: (0, 0, j))
    cb_spec = pl.BlockSpec((2, 1, W), lambda j, b: (0, 0, j))
    return pl.pallas_call(
        body, name="gate_bwd",
        out_shape=(jax.ShapeDtypeStruct((2, T, D_FF), MXU_DTYPE), jax.ShapeDtypeStruct((2, 3, D_FF), F32),
                   jax.ShapeDtypeStruct((2, 1, D_FF), F32)),
        grid=(N_FF_TILES, B),
        in_specs=[up_spec, up_spec, pl.BlockSpec((S, W), lambda j, b: (b, j)), cw_spec],
        out_specs=(up_spec, cw_spec, cb_spec),
        scratch_shapes=[pltpu.VMEM((2, S + HALO, W), F32)],
        compiler_params=_params(("parallel", "arbitrary"), 12 * _nbytes((S, W), F32)),
    )(up3, conv3, dact, conv_w)


def _final(x2, tgt, g, tr=512):
    T, D = x2.shape

    def body(x_ref, t_ref, g_ref, dx_ref, loss_ref, gg_ref):
        @pl.when(pl.program_id(0) == 0)
        def _():
            loss_ref[...] = jnp.zeros_like(loss_ref)
            gg_ref[...] = jnp.zeros_like(gg_ref)

        xv = x_ref[...]
        gv = g_ref[...]
        r = lax.rsqrt(jnp.mean(xv * xv, axis=-1, keepdims=True) + EPS)
        xn = xv * r
        err = xn * gv - t_ref[...]
        loss_ref[...] += 0.5 * jnp.sum(jnp.mean(err * err, axis=-1, keepdims=True), axis=0, keepdims=True)
        dy = err * (1.0 / D)
        gg_ref[...] += jnp.sum(dy * xn, axis=0, keepdims=True)
        dxn = dy * gv
        dx_ref[...] = r * (dxn - xn * jnp.mean(dxn * xn, axis=-1, keepdims=True))

    row = pl.BlockSpec((tr, D), lambda i: (i, 0))
    vec = pl.BlockSpec((1, D), lambda i: (0, 0))
    return pl.pallas_call(
        body, name="final_loss",
        out_shape=(jax.ShapeDtypeStruct((T, D), F32), jax.ShapeDtypeStruct((1, 128), F32), jax.ShapeDtypeStruct((1, D), F32)),
        grid=(T // tr,), in_specs=[row, row, vec],
        out_specs=(row, pl.BlockSpec((1, 128), lambda i: (0, 0)), vec),
        compiler_params=_params(("arbitrary",), 6 * _nbytes((tr, D), F32)),
    )(x2, tgt, g)


def _sum_slabs(parts, name, tr):
    rows, cols = parts[0].shape
    n = len(parts)

    def body(*refs):
        acc = refs[0][...]
        for r in refs[1:n]:
            acc = acc + r[...]
        refs[n][...] = acc

    blk = pl.BlockSpec((tr, cols), lambda i: (i, 0))
    return pl.pallas_call(
        body, name=name, out_shape=jax.ShapeDtypeStruct((rows, cols), F32), grid=(rows // tr,),
        in_specs=[blk] * n, out_specs=blk,
        compiler_params=_params(("parallel",), (n + 1) * _nbytes((tr, cols), F32)),
    )(*parts)


ADAMW_BLOCK_BYTES = 2400 * 1024


def _adamw(w, g, m, v, name):
    lead = w.ndim == 3
    rows, cols = w.shape[-2:]
    fits = [d for d in range(8, rows + 1, 8) if rows % d == 0 and d * cols * 4 <= ADAMW_BLOCK_BYTES]
    tr = max(fits) if fits else rows
    c1 = 1.0 - ADAM_B1 ** ADAM_STEP
    c2 = 1.0 - ADAM_B2 ** ADAM_STEP

    def body(w_ref, g_ref, m_ref, v_ref, d_ref, nm_ref, nv_ref):
        gv = g_ref[...]
        nm = ADAM_B1 * m_ref[...] + (1.0 - ADAM_B1) * gv
        nv = ADAM_B2 * v_ref[...] + (1.0 - ADAM_B2) * (gv * gv)
        nm_ref[...] = nm
        nv_ref[...] = nv
        d_ref[...] = -ADAM_LR * ((nm / c1) / (jnp.sqrt(nv / c2) + ADAM_EPS) + ADAM_WD * w_ref[...])

    blk = pl.BlockSpec((None, tr, cols), lambda i: (0, i, 0)) if lead else pl.BlockSpec((tr, cols), lambda i: (i, 0))
    sds = jax.ShapeDtypeStruct(w.shape, F32)
    return pl.pallas_call(
        body, name=name, out_shape=(sds, sds, sds), grid=(rows // tr,), in_specs=[blk] * 4, out_specs=(blk, blk, blk),
        compiler_params=_params(("parallel",), 7 * _nbytes((tr, cols), F32)),
    )(w, g, m, v)


_ANY = pl.BlockSpec(memory_space=pl.ANY)


def _place():
    x, y, c = lax.axis_index("x"), lax.axis_index("y"), lax.axis_index("c")
    chips = [(1 - x, y), (x, 1 - y), (1 - x, 1 - y)]
    return x, y, c, chips


def _forward_halves(lands):
    n = len(lands)

    def body(*refs):
        outs, send, recv = refs[n:2 * n], refs[2 * n], refs[2 * n + 1]
        x, y, c, chips = _place()
        cps = []
        for w in range(n):
            for j, (px, py) in enumerate(chips):
                landed = outs[w].at[2 * px + py, c]
                cps.append(pltpu.make_async_remote_copy(
                    src_ref=landed, dst_ref=landed, send_sem=send.at[3 * w + j], recv_sem=recv.at[3 * w + j],
                    device_id=(x, y, 1 - c), device_id_type=MESH))
        for cp in cps:
            cp.start()
        for w in range(n):
            for j, (px, py) in enumerate(chips):
                other = outs[w].at[2 * px + py, 1 - c]
                pltpu.make_async_remote_copy(src_ref=other, dst_ref=other, send_sem=send.at[3 * w + j],
                                             recv_sem=recv.at[3 * w + j], device_id=(x, y, 1 - c),
                                             device_id_type=MESH).wait_recv()
        for cp in cps:
            cp.wait_send()

    dma = lambda k: pltpu.SemaphoreType.DMA((k,))
    return pl.pallas_call(
        body, name="gather_forward_halves", out_shape=tuple(jax.ShapeDtypeStruct(a.shape, a.dtype) for a in lands),
        in_specs=[_ANY] * n, out_specs=tuple([_ANY] * n), input_output_aliases={w: w for w in range(n)},
        scratch_shapes=[dma(3 * n), dma(3 * n)],
    )(*lands)


_HBM = pl.BlockSpec(memory_space=pltpu.HBM)
_SEM = pl.BlockSpec(memory_space=pltpu.SEMAPHORE)
_EFFECT = pltpu.SideEffectType.DATAFLOW_SIDE_EFFECTING


SEMS_PER_ARRAY = 8


def _exchange_copies(srcs, lands, send, recv, mode):
    x, y, c, chips = _place()
    if mode == "halves":
        cps = []
        for w, (src, land) in enumerate(zip(srcs, lands)):
            pieces = [(src.at[c], land.at[2 * x + y, c], (px, py, c)) for px, py in chips]
            pieces.append((src, land.at[2 * x + y], (x, y, 1 - c)))
            for k, (piece, dst, peer) in enumerate(pieces):
                cps.append(pltpu.make_async_remote_copy(
                    src_ref=piece, dst_ref=dst, send_sem=send.at[SEMS_PER_ARRAY * w + k],
                    recv_sem=recv.at[SEMS_PER_ARRAY * w + k], device_id=peer, device_id_type=MESH))
        return cps
    if mode == "swap":
        return [pltpu.make_async_remote_copy(
            src_ref=src.at[:, 1 - c], dst_ref=land, send_sem=send.at[SEMS_PER_ARRAY * w],
            recv_sem=recv.at[SEMS_PER_ARRAY * w], device_id=(x, y, 1 - c), device_id_type=MESH)
            for w, (src, land) in enumerate(zip(srcs, lands))]
    if mode == "all":
        flips = [(fx, fy, fc) for fx in (0, 1) for fy in (0, 1) for fc in (0, 1)][1:]
        peers = [(x ^ fx, y ^ fy, c ^ fc) for fx, fy, fc in flips]
        slot = 4 * x + 2 * y + c
    else:
        peers = [(px, py, c) for px, py in chips] + ([(x, y, 1 - c)] if mode == "gather" else [])
        slot = 2 * x + y
    cps = []
    for w, (src, land) in enumerate(zip(srcs, lands)):
        for k, peer in enumerate(peers):
            piece = src.at[2 * peer[0] + peer[1]] if mode == "scatter" else src
            cps.append(pltpu.make_async_remote_copy(
                src_ref=piece, dst_ref=land.at[slot], send_sem=send.at[SEMS_PER_ARRAY * w + k],
                recv_sem=recv.at[SEMS_PER_ARRAY * w + k], device_id=peer, device_id_type=MESH))
    return cps


def _exchange_start(srcs, name, mode, after):
    n = len(srcs)
    if mode == "swap":
        land_shapes = [(s.shape[0],) + s.shape[2:] for s in srcs]
    else:
        lead = {"gather": (N_CHIPS,), "halves": (N_CHIPS,), "scatter": (), "all": (2 * N_CHIPS,)}[mode]
        land_shapes = [lead + s.shape for s in srcs]

    def body(*refs):
        src_refs, land_refs = refs[:n], refs[n:2 * n]
        send, recv = refs[2 * n + 1], refs[2 * n + 2]
        token = refs[-1]
        for cp in _exchange_copies(src_refs, land_refs, send, recv, mode):
            cp.start()
        token[...] = jnp.zeros_like(token)

    sems = pltpu.SemaphoreType.DMA((SEMS_PER_ARRAY * n,))
    out = pl.pallas_call(
        body, name=name,
        out_shape=(sems, sems, *[pltpu.HBM(s.shape, s.dtype) for s in srcs],
                   *[pltpu.HBM(shp, s.dtype) for shp, s in zip(land_shapes, srcs)], jax.ShapeDtypeStruct((8, 128), F32)),
        in_specs=[_HBM] * (2 * n) + [_ANY],
        out_specs=(_SEM, _SEM, *[_HBM] * (2 * n), pl.BlockSpec(memory_space=pltpu.VMEM)),
        input_output_aliases={i: 2 + i for i in range(2 * n)},
        compiler_params=pltpu.CompilerParams(has_side_effects=_EFFECT),
    )(*[pltpu.with_memory_space_constraint(s, pltpu.HBM) for s in srcs],
      *[pltpu.with_memory_space_constraint(lax.empty(shp, s.dtype), pltpu.HBM) for shp, s in zip(land_shapes, srcs)],
      after)
    return out[0], out[1], out[2:2 + n], out[2 + n:2 + 2 * n], out[-1]


def _exchange_wait(started, name, mode, after):
    send, recv, src_thru, land_thru, _ = started
    n = len(src_thru)
    after = list(after) if isinstance(after, (list, tuple)) else [after]

    def body(*refs):
        src_refs, land_refs, send_ref, recv_ref = refs[:n], refs[n:2 * n], refs[2 * n], refs[2 * n + 1]
        for cp in _exchange_copies(src_refs, land_refs, send_ref, recv_ref, mode):
            cp.wait_send()
            cp.wait_recv()

    out = pl.pallas_call(
        body, name=name,
        out_shape=tuple(pltpu.HBM(a.shape, a.dtype) for a in list(src_thru) + list(land_thru)),
        in_specs=[_HBM] * (2 * n) + [_SEM, _SEM] + [_ANY] * len(after), out_specs=tuple([_HBM] * (2 * n)),
        input_output_aliases={i: i for i in range(2 * n)},
        compiler_params=pltpu.CompilerParams(has_side_effects=_EFFECT),
    )(*src_thru, *land_thru, send, recv, *after)
    return out[:n], out[n:]


def _swap_halves(gs, name):
    n = len(gs)

    def body(*refs):
        ins, outs, send, recv = refs[:n], refs[n:2 * n], refs[2 * n], refs[2 * n + 1]
        x, y, c, _ = _place()
        cps = []
        for w in range(n):
            cps.append(pltpu.make_async_remote_copy(
                src_ref=ins[w].at[:, 1 - c], dst_ref=outs[w], send_sem=send.at[w], recv_sem=recv.at[w],
                device_id=(x, y, 1 - c), device_id_type=MESH))
        for cp in cps:
            cp.start()
        for cp in cps:
            cp.wait()

    return pl.pallas_call(
        body, name=name,
        out_shape=tuple(jax.ShapeDtypeStruct((g.shape[0],) + g.shape[2:], g.dtype) for g in gs),
        in_specs=[_ANY] * n, out_specs=tuple([_ANY] * n),
        scratch_shapes=[pltpu.SemaphoreType.DMA((n,)), pltpu.SemaphoreType.DMA((n,))],
    )(*gs)


GRAD_PAYLOAD = jnp.bfloat16


def _half_blocks(half_rows, cols):
    if (half_rows // 2) % 16 == 0:
        return (half_rows // 2, cols), (lambda r: (r, 0))
    assert cols % 256 == 0, (half_rows, cols)
    return (half_rows, cols // 2), (lambda r: (0, r))


def _pair_sum(gs, gots, name):
    n = len(gs)
    core = lax.axis_index("c").astype(jnp.int32).reshape(1)

    def body(core_ref, *refs):
        del core_ref
        for w in range(n):
            refs[2 * n + w][...] = (refs[w][...] + refs[n + w][...]).astype(GRAD_PAYLOAD)

    in_specs, out_specs, out_shape, nbytes = [], [], [], 0
    cuts = [_half_blocks(g.shape[1] // 2, g.shape[2]) for g in gs]
    for g, ((br, bc), at) in zip(gs, cuts):
        per_half = (g.shape[1] // 2) // br
        in_specs.append(pl.BlockSpec((1, br, bc), lambda s, r, core, at=at, per_half=per_half:
                                     (s, per_half * core[0] + at(r)[0], at(r)[1])))
        nbytes += 3 * _nbytes((br, bc), F32)
    for g, ((br, bc), at) in zip(gs, cuts):
        in_specs.append(pl.BlockSpec((1, br, bc), lambda s, r, core, at=at: (s,) + at(r)))
        out_specs.append(pl.BlockSpec((1, br, bc), lambda s, r, core, at=at: (s,) + at(r)))
        out_shape.append(jax.ShapeDtypeStruct((g.shape[0], g.shape[1] // 2, g.shape[2]), GRAD_PAYLOAD))
    return pl.pallas_call(
        body, name=name, out_shape=tuple(out_shape),
        grid_spec=pltpu.PrefetchScalarGridSpec(num_scalar_prefetch=1, grid=(N_CHIPS, 2), in_specs=in_specs,
                                               out_specs=tuple(out_specs)),
        compiler_params=_params(("parallel", "parallel"), nbytes),
    )(core, *gs, *gots)


def _chip_sum(ps, landed):
    n = len(ps)
    x, y, c = lax.axis_index("x"), lax.axis_index("y"), lax.axis_index("c")
    where = jnp.stack([2 * x + y, 2 * (1 - x) + y, 2 * x + (1 - y), 2 * (1 - x) + (1 - y), c]).astype(jnp.int32)

    def body(where_ref, *refs):
        del where_ref
        for w in range(n):
            terms = [refs[4 * w + t][...].astype(F32) for t in range(4)]
            refs[4 * n + w][...] = ((terms[0] + terms[1]) + terms[2]) + terms[3]

    in_specs, out_specs, out_shape, args, nbytes = [], [], [], [], 0
    for p, a in zip(ps, landed):
        (br, bc), at = _half_blocks(a.shape[1], a.shape[2])
        blk = (1, br, bc)
        in_specs.append(pl.BlockSpec(blk, lambda r, where, at=at: (where[0],) + at(r)))
        args.append(p)
        for t in (1, 2, 3):
            in_specs.append(pl.BlockSpec(blk, lambda r, where, t=t, at=at: (where[t],) + at(r)))
            args.append(a)
        out_specs.append(pl.BlockSpec(blk, lambda r, where, at=at: (where[4],) + at(r)))
        out_shape.append(jax.ShapeDtypeStruct((2,) + a.shape[1:], F32))
        nbytes += 4 * _nbytes(blk, F32)
    return pl.pallas_call(
        body, name="grad_chip_sum", out_shape=tuple(out_shape),
        grid_spec=pltpu.PrefetchScalarGridSpec(num_scalar_prefetch=1, grid=(2,), in_specs=in_specs,
                                               out_specs=tuple(out_specs)),
        compiler_params=_params(("parallel",), nbytes),
    )(where, *args)


def _join_halves(ss):
    n = len(ss)

    def body(*refs):
        outs, send, recv = refs[n:2 * n], refs[2 * n], refs[2 * n + 1]
        x, y, c, _ = _place()
        cps = []
        for w in range(n):
            cps.append(pltpu.make_async_remote_copy(
                src_ref=outs[w].at[c], dst_ref=outs[w].at[c], send_sem=send.at[w], recv_sem=recv.at[w],
                device_id=(x, y, 1 - c), device_id_type=MESH))
        for cp in cps:
            cp.start()
        for w in range(n):
            got = outs[w].at[1 - c]
            pltpu.make_async_remote_copy(src_ref=got, dst_ref=got, send_sem=send.at[w], recv_sem=recv.at[w],
                                         device_id=(x, y, 1 - c), device_id_type=MESH).wait_recv()
        for cp in cps:
            cp.wait_send()

    dma = lambda k: pltpu.SemaphoreType.DMA((k,))
    return pl.pallas_call(
        body, name="grad_join_halves",
        out_shape=tuple(jax.ShapeDtypeStruct(s.shape, s.dtype) for s in ss),
        in_specs=[_ANY] * n, out_specs=tuple([_ANY] * n), input_output_aliases={w: w for w in range(n)},
        scratch_shapes=[dma(n), dma(n)],
    )(*ss)


def _rot_cols(w, axis=-1):
    a, b = jnp.split(w, 2, axis=axis)
    return jnp.concatenate([-b, a], axis=axis)


def _rot_cols_t(g, axis=-1):
    a, b = jnp.split(g, 2, axis=axis)
    return jnp.concatenate([b, -a], axis=axis)


def _cols_from_chips(a):
    n, r, cs = a.shape
    return jnp.transpose(a, (1, 0, 2)).reshape(r, n * cs)


def _cols_to_chips(a):
    r, cc = a.shape
    return jnp.transpose(a.reshape(r, N_CHIPS, cc // N_CHIPS), (1, 0, 2))


def _conv_w_split(cw):
    return jnp.swapaxes(cw.reshape(3, 2, D_FF), 0, 1)


def _conv_w_join(g):
    return jnp.swapaxes(g, 0, 1).reshape(3, 2 * D_FF)


_SEG =(D_MODEL, 2 * D_MODEL, 2 * D_MODEL + Q_RANK, 2 * D_MODEL + Q_RANK + KV_RANK, 2 * D_MODEL + Q_RANK + KV_RANK + ROPE,
        3 * D_MODEL + Q_RANK + KV_RANK + ROPE)


def _w_in_t_to_pad(wt):
    u, v, cq, ckv, kr, ga, gb = jnp.split(wt, _SEG, axis=0)
    return jnp.concatenate([u, v, ga, gb, cq, ckv, kr, _rot_cols(kr, axis=0)], axis=0)


def _w_in_t_from_pad(gt):
    u, v, ga, gb, cq, ckv, kr, krr = jnp.split(
        gt, (D_MODEL, 2 * D_MODEL, 3 * D_MODEL, 4 * D_MODEL, 4 * D_MODEL + Q_RANK, 4 * D_MODEL + Q_RANK + KV_RANK,
             4 * D_MODEL + Q_RANK + KV_RANK + ROPE), axis=0)
    return jnp.concatenate([u, v, cq, ckv, kr + _rot_cols_t(krr, axis=0), ga, gb], axis=0)


def _w_uq_to_pad(w):
    t = w.reshape(Q_RANK, HEADS, QK_DIM)
    nope, rope = t[..., :NOPE], t[..., NOPE:]
    return jnp.concatenate([nope, rope, _rot_cols(rope)], axis=-1).reshape(Q_RANK, HEADS * HEAD_PAD)


def _w_uq_from_pad(g):
    t = g.reshape(Q_RANK, HEADS, HEAD_PAD)
    nope, rope, rot = t[..., :NOPE], t[..., NOPE:QK_DIM], t[..., QK_DIM:]
    return jnp.concatenate([nope, rope + _rot_cols_t(rot)], axis=-1).reshape(Q_RANK, HEADS * QK_DIM)


def _w_ukv_to_pad(w):
    t = w.reshape(KV_RANK, HEADS, 2, NOPE)
    return jnp.swapaxes(t, 1, 2).reshape(KV_RANK, 2 * HEADS * NOPE)


def _w_ukv_from_pad(g):
    t = g.reshape(KV_RANK, 2, HEADS, NOPE)
    return jnp.swapaxes(t, 1, 2).reshape(KV_RANK, 2 * HEADS * NOPE)


def _rope_tables(positions):
    inv_freq = 1.0 / (ROPE_THETA ** (jnp.arange(0, ROPE, 2, dtype=F32) / ROPE))
    ang = positions.astype(F32).reshape(-1, 1) * inv_freq
    cos, sin = jnp.cos(ang), jnp.sin(ang)
    zero = jnp.zeros((ang.shape[0], 64), F32)
    return jnp.concatenate([cos, cos, zero], axis=1), jnp.concatenate([sin, sin, zero], axis=1)


_BIG = ("w_in", "w_uq", "w_ukv", "w_out", "w_up", "w_down")
UP_SHARD = 2 * D_FF // N_CHIPS


def _local_step(x, positions, tgt, wts, in_weights, mixer_weights, ffn_weights, on_ffn_grads, on_mixer_grads):
    B, S, D = x.shape
    T = B * S
    xf = x.reshape(T, D)
    cos_a, sin_a = _rope_tables(positions)
    bs_t = jnp.pad(wts["a_spatial_b"].T, ((0, 0), (0, 128 - A_GROUPS)))

    h = _rms_fwd(xf, wts["mix_norm"], "norm1_fwd")
    wts = dict(wts)
    wts["w_in"], token = in_weights([h, cos_a, sin_a])
    z = _mm(h, wts["w_in"], "nt", "in_proj", tm=512, tn=1536, tk=D, n_outer=True, after=token)
    wts["w_q"], wts["w_kv"], wts["w_out"] = mixer_weights(z)
    q, k, v, cqn, ckvn = _lat_fwd(z, wts["q_a_norm"], wts["kv_a_norm"], wts["w_q"], wts["w_kv"], cos_a, sin_a)
    yb, *lses = _attn_fwd(q, k, v, B, S)
    merged = _mix_fwd(z, yb, wts["a_v_norm_g"], wts["a_v_norm_b"], wts["a_spatial_w"], bs_t)
    x1 = _mm(merged, wts["w_out"], "nn", "out_proj", tm=512, tn=D, tk=D, add=xf)
    h2 = _rms_fwd(x1, wts["ffn_norm"], "norm2_fwd")
    wts["w_up"], wts["w_down"], wts["conv_w"] = ffn_weights(h2)
    up_pre = _mm(h2, wts["w_up"], "nn", "up_proj", tm=512, tn=UP_SHARD, tk=D, dims=(T, 2 * D_FF, D),
                 b_spec=pl.BlockSpec((None, D, UP_SHARD), lambda i, j, k: (j, 0, 0)),
                 o_spec=pl.BlockSpec((None, 512, UP_SHARD), lambda i, j, k: (j // 2, i, j % 2)), out_shape=(2, T, D_FF),
                 n_outer=True)
    act, up_conv = _gate_fwd(up_pre, wts["conv_w"], wts["conv_b"], B, S)
    x2 = _mm(act, wts["w_down"], "nn", "down_proj", tm=512, tn=D, tk=1408, add=x1)
    dx2, loss_row, g_final = _final(x2, tgt.reshape(T, D), wts["final_norm"])

    g = {"final_norm": g_final}
    dact = _mm(dx2, wts["w_down"], "nt", "down_proj_dx", tm=512, tn=1408, tk=D, n_outer=True)
    tk2, tk1 = min(2048, T), min(1024, T)
    g["w_down"], g["w_down_lo"] = _mm(act, dx2, "tn", "down_proj_dw", tm=1408, tn=D, tk=tk1, copy_dtype=GRAD_PAYLOAD)
    dup, g["conv_w"], g["conv_b"] = _gate_bwd(up_pre, up_conv, dact, wts["conv_w"], B, S)
    g["w_up"], g["w_up_lo"] = _mm(
        h2, dup, "tn", "up_proj_dw", tm=D, tn=UP_SHARD, tk=tk2, dims=(D, 2 * D_FF, T), copy_dtype=GRAD_PAYLOAD,
        b_spec=pl.BlockSpec((None, tk2, UP_SHARD), lambda i, j, k: (j // 2, k, j % 2)),
        o_spec=pl.BlockSpec((None, D, UP_SHARD), lambda i, j, k: (j, 0, 0)), out_shape=(N_CHIPS, D, UP_SHARD))
    token, ffn_sent = on_ffn_grads(g)
    dh2 = _mm(dup, wts["w_up"], "nt", "up_proj_dx", tm=512, tn=D, tk=UP_SHARD, dims=(T, D, 2 * D_FF), after=token,
              a_spec=pl.BlockSpec((None, 512, UP_SHARD), lambda i, j, k: (k // 2, i, k % 2)),
              b_spec=pl.BlockSpec((None, D, UP_SHARD), lambda i, j, k: (k, 0, 0)))
    token = ffn_sent(dh2)
    dx1, g["ffn_norm"] = _rms_bwd(x1, wts["ffn_norm"], dh2, dx2, "norm2_bwd")
    dm = _mm(dx1, wts["w_out"], "nt", "out_proj_dx", tm=512, tn=D, tk=D, after=token)
    g["w_out"], g["w_out_lo"] = _mm(merged, dx1, "tn", "out_proj_dw", tm=D, tn=D, tk=tk1, copy_dtype=GRAD_PAYLOAD)
    dz, dyb, dl, g["a_spatial_w"], gbs, g["a_v_norm_g"], g["a_v_norm_b"] = _mix_bwd(
        z, yb, dm, wts["a_v_norm_g"], wts["a_v_norm_b"], wts["a_spatial_w"], bs_t)
    g["a_spatial_b"] = gbs[:, :A_GROUPS].T
    delta = dl.reshape(HEADS * T // ATT_BLOCK, 1, ATT_BLOCK)
    dq, dk, dv = _attn_bwd(q, k, v, dyb, lses, delta, B, S)
    dz, dq_raw, dkv, g["q_a_norm"], g["kv_a_norm"] = _lat_bwd(
        dz, z, dq, dk, dv, wts["q_a_norm"], wts["kv_a_norm"], wts["w_q"], wts["w_kv"], cos_a, sin_a)
    g["w_q"] = _mm(cqn, dq_raw, "tn", "q_proj_dw", tm=Q_RANK, tn=HEADS * HEAD_PAD, tk=tk2)
    g["w_kv"] = _mm(ckvn, dkv, "tn", "kv_proj_dw", tm=KV_RANK, tn=2 * HEADS * NOPE, tk=tk2)
    g["w_in"] = _mm(dz, h, "tn", "in_proj_dw", tm=1536, tn=D, tk=tk2)
    token = on_mixer_grads(g)
    dh = _mm(dz, wts["w_in"], "nn", "in_proj_dx", tm=512, tn=D, tk=1536, after=token)
    dx, g["mix_norm"] = _rms_bwd(xf, wts["mix_norm"], dh, dx1, "norm1_bwd")
    return loss_row[0, 0], dx.reshape(B, S, D), g


_SMALL = (("mix_norm", (1, D_MODEL)), ("a_v_norm_g", (1, D_MODEL)), ("a_v_norm_b", (1, D_MODEL)),
          ("a_spatial_w", (A_GROUPS * CHUNK, CHUNK)), ("a_spatial_b", (1, A_GROUPS * CHUNK)), ("q_a_norm", (1, Q_RANK)),
          ("kv_a_norm", (1, KV_RANK)), ("ffn_norm", (1, D_MODEL)), ("conv_b", (1, 2 * D_FF)), ("final_norm", (1, D_MODEL)),
          ("conv_w", (3, 2 * D_FF)))
_SMALL_SIZE = sum(math.prod(s) for _, s in _SMALL)
_SMALL_ROWS = -(-(_SMALL_SIZE + 1) // (128 * 8)) * 8


def kernel(x, positions, mix_norm, w_in, a_v_norm_g, a_v_norm_b, a_spatial_w, a_spatial_b, q_a_norm, w_uq, kv_a_norm, w_ukv, w_out, ffn_norm, w_up, conv_w, conv_b, w_down, final_norm, loss_target, m_mix_norm, m_w_in, m_a_v_norm_g, m_a_v_norm_b, m_a_spatial_w, m_a_spatial_b, m_q_a_norm, m_w_uq, m_kv_a_norm, m_w_ukv, m_w_out, m_ffn_norm, m_w_up, m_conv_w, m_conv_b, m_w_down, m_final_norm, v_mix_norm, v_w_in, v_a_v_norm_g, v_a_v_norm_b, v_a_spatial_w, v_a_spatial_b, v_q_a_norm, v_w_uq, v_kv_a_norm, v_w_ukv, v_w_out, v_ffn_norm, v_w_up, v_conv_w, v_conv_b, v_w_down, v_final_norm):
    weights = dict(mix_norm=mix_norm, w_in=w_in, a_v_norm_g=a_v_norm_g, a_v_norm_b=a_v_norm_b, a_spatial_w=a_spatial_w,
                   a_spatial_b=a_spatial_b, q_a_norm=q_a_norm, w_uq=w_uq, kv_a_norm=kv_a_norm, w_ukv=w_ukv, w_out=w_out,
                   ffn_norm=ffn_norm, w_up=w_up, conv_w=conv_w, conv_b=conv_b, w_down=w_down, final_norm=final_norm)
    m_in = dict(mix_norm=m_mix_norm, w_in=m_w_in, a_v_norm_g=m_a_v_norm_g, a_v_norm_b=m_a_v_norm_b,
                a_spatial_w=m_a_spatial_w, a_spatial_b=m_a_spatial_b, q_a_norm=m_q_a_norm, w_uq=m_w_uq,
                kv_a_norm=m_kv_a_norm, w_ukv=m_w_ukv, w_out=m_w_out, ffn_norm=m_ffn_norm, w_up=m_w_up, conv_w=m_conv_w,
                conv_b=m_conv_b, w_down=m_w_down, final_norm=m_final_norm)
    v_in = dict(mix_norm=v_mix_norm, w_in=v_w_in, a_v_norm_g=v_a_v_norm_g, a_v_norm_b=v_a_v_norm_b,
                a_spatial_w=v_a_spatial_w, a_spatial_b=v_a_spatial_b, q_a_norm=v_q_a_norm, w_uq=v_w_uq,
                kv_a_norm=v_kv_a_norm, w_ukv=v_w_ukv, w_out=v_w_out, ffn_norm=v_ffn_norm, w_up=v_w_up, conv_w=v_conv_w,
                conv_b=v_conv_b, w_down=v_w_down, final_norm=v_final_norm)
    names = list(weights)
    chip = 2 * lax.axis_index("x") + lax.axis_index("y")

    def halves(a):
        return a.reshape(a.shape[:-2] + (2, a.shape[-2] // 2, a.shape[-1]))

    w_in_t = jnp.swapaxes(w_in[0], 0, 1).astype(MXU_DTYPE)
    w_in_gather = _exchange_start([jnp.stack(jnp.split(w_in_t, 2, axis=1))], "w_in_gather_start", "halves",
                                  after=positions)
    gathers = {}
    wts = dict(
        mix_norm=mix_norm, a_v_norm_g=a_v_norm_g, a_v_norm_b=a_v_norm_b, a_spatial_w=a_spatial_w[0],
        a_spatial_b=a_spatial_b[0], q_a_norm=q_a_norm, kv_a_norm=kv_a_norm, ffn_norm=ffn_norm,
        final_norm=final_norm.reshape(1, D_MODEL), conv_b=conv_b.reshape(2, 1, D_FF))

    mixer_shards = [weights[n][0].astype(MXU_DTYPE) for n in _BIG[1:4]]
    ffn_shards = [w_up[0].astype(MXU_DTYPE), w_down[0].astype(MXU_DTYPE)]

    def in_weights(after):
        _, landed = _exchange_wait(w_in_gather, "w_in_gather_wait", "halves", list(after) + mixer_shards + ffn_shards)
        (w_in_sh,) = _forward_halves(list(landed))
        gathers["mixer"] = _exchange_start(mixer_shards, "mixer_gather_start", "gather", after=w_in_sh)
        gathers["ffn"] = _exchange_start(ffn_shards + [conv_w[0]], "ffn_gather_start", "gather",
                                         after=gathers["mixer"][4])
        w_in_pad = _w_in_t_to_pad(jnp.concatenate([w_in_sh[:, 0], w_in_sh[:, 1]], axis=-1).reshape(-1, D_MODEL))
        return w_in_pad, gathers["ffn"][4]

    def mixer_weights(after):
        _, (w_uq_sh, w_ukv_sh, w_out_sh) = _exchange_wait(gathers["mixer"], "mixer_gather_wait", "gather", after)
        return (_w_uq_to_pad(_cols_from_chips(w_uq_sh)), _w_ukv_to_pad(_cols_from_chips(w_ukv_sh)),
                w_out_sh.reshape(D_MODEL, D_MODEL))

    def ffn_weights(after):
        _, (w_up_sh, w_down_sh, cw_all) = _exchange_wait(gathers["ffn"], "ffn_gather_wait", "gather", after)
        return w_up_sh, w_down_sh.reshape(D_FF, D_MODEL), _conv_w_split(_cols_from_chips(cw_all))

    scatters = {}

    def start_scatter(slabs, slabs_lo, tag):
        got = _swap_halves([halves(s) for s in slabs_lo], tag + "_grad_swap_halves")
        sums = _pair_sum(slabs, got, tag + "_grad_pair_sum")
        scatters[tag] = _exchange_start(list(sums), tag + "_scatter_start", "scatter", after=slabs[-1])
        return scatters[tag][4]

    def on_ffn_grads(g):
        slabs, slabs_lo = [[g["w_up" + lo], g["w_down" + lo].reshape(N_CHIPS, D_FF // N_CHIPS, D_MODEL)]
                           for lo in ("", "_lo")]
        swap = _exchange_start([halves(s) for s in slabs_lo], "ffn_swap_start", "swap", after=slabs[1])

        def sent(after):
            _, got = _exchange_wait(swap, "ffn_swap_wait", "swap", after)
            sums = _pair_sum(slabs, got, "ffn_grad_pair_sum")
            scatters["ffn"] = _exchange_start(list(sums), "ffn_scatter_start", "scatter", after=got[0])
            return scatters["ffn"][4]

        return swap[4], sent

    def on_mixer_grads(g):
        slabs = [_w_in_t_from_pad(g["w_in"]).reshape(N_CHIPS, -1, D_MODEL), _cols_to_chips(_w_uq_from_pad(g["w_q"])),
                 _cols_to_chips(_w_ukv_from_pad(g["w_kv"]))]
        w_out_slabs = [g["w_out" + lo].reshape(N_CHIPS, D_MODEL // N_CHIPS, D_MODEL) for lo in ("", "_lo")]
        return start_scatter(slabs + w_out_slabs[:1], [s.astype(GRAD_PAYLOAD) for s in slabs] + w_out_slabs[1:], "mixer")

    loss_part, grad_x, g = _local_step(x, positions, loss_target, wts, in_weights, mixer_weights, ffn_weights,
                                       on_ffn_grads, on_mixer_grads)

    g_small_parts = dict(g)
    g_small_parts["conv_w"] = _conv_w_join(g["conv_w"])
    g_small_parts["conv_b"] = g["conv_b"].reshape(1, 2 * D_FF)
    flat = jnp.concatenate([g_small_parts[n].reshape(-1) for n, _ in _SMALL] + [loss_part.reshape(1)])
    flat = jnp.pad(flat, (0, _SMALL_ROWS * 128 - flat.shape[0])).reshape(_SMALL_ROWS, 128)
    small_gather = _exchange_start([flat], "small_gather_start", "all", after=grad_x)

    mixer_sums, mixer_landed = _exchange_wait(scatters["mixer"], "mixer_scatter_wait", "scatter", after=small_gather[4])
    ffn_sums, ffn_landed = _exchange_wait(scatters["ffn"], "ffn_scatter_wait", "scatter", after=mixer_landed[0])
    reduced = _chip_sum(list(mixer_sums) + list(ffn_sums), list(mixer_landed) + list(ffn_landed))
    g_big = dict(zip(_BIG, _join_halves(reduced)))

    grads, deltas, new_m, new_v = {}, {}, {}, {}

    def update(n, grad):
        w = weights[n]
        shape2 = grad.shape
        d, nm, nv = _adamw(w.reshape(shape2), grad, m_in[n].reshape(shape2), v_in[n].reshape(shape2), "adamw_" + n)
        grads[n], deltas[n], new_m[n], new_v[n] = (t.reshape(w.shape) for t in (grad, d, nm, nv))

    def update_transposed(n, grad_t):
        t = lambda a: jnp.swapaxes(a, 1, 2)
        d, nm, nv = _adamw(t(weights[n]), grad_t, t(m_in[n]), t(v_in[n]), "adamw_" + n)
        grads[n], deltas[n], new_m[n], new_v[n] = t(grad_t), t(d), t(nm), t(nv)

    for n in _BIG:
        g3 = g_big[n].reshape((1, -1, g_big[n].shape[-1]))
        if n == "w_in":
            update_transposed(n, g3)
        else:
            update(n, g3)

    (own,), (everyone,) = _exchange_wait(small_gather, "small_gather_wait", "all", after=[deltas[n] for n in _BIG])
    device = 2 * chip + lax.axis_index("c")
    everyone = lax.dynamic_update_slice(everyone, own[None], (device, 0, 0))
    total = _sum_slabs([everyone[j] for j in range(8)], "small_grads_sum", tr=_SMALL_ROWS).reshape(-1)
    o = 0
    for n, shp in _SMALL:
        piece = total[o:o + math.prod(shp)].reshape(shp)
        o += math.prod(shp)
        if n == "conv_w":
            piece = lax.dynamic_slice_in_dim(piece, chip * UP_SHARD, UP_SHARD, axis=1)
        update(n, piece)
    loss = total[_SMALL_SIZE]
    return (loss, grad_x, *[grads[n] for n in names], *[deltas[n] for n in names], *[new_m[n] for n in names],
            *[new_v[n] for n in names])
```

```python
import functools
import math

import jax
import jax.numpy as jnp
from jax import lax
from jax.experimental import pallas as pl
from jax.experimental.pallas import tpu as pltpu

F32 = jnp.float32
MXU_DTYPE = jnp.bfloat16
MESH = pl.DeviceIdType.MESH

D_MODEL = 1024
EPS = 1e-6
A_GROUPS = 8
CHUNK = 128
HEADS = 8
NOPE = 128
ROPE = 64
QK_DIM = NOPE + ROPE
HEAD_PAD = 256
Q_RANK = 256
KV_RANK = 128
ROPE_THETA = 10000.0
D_FF = 2816
FF_TILE = 256
N_FF_TILES = D_FF // FF_TILE
LAT = 512
IN_PAD = 4 * D_MODEL + LAT
N_CHIPS = 4
ADAM_LR, ADAM_B1, ADAM_B2, ADAM_EPS, ADAM_WD, ADAM_STEP = 0.001, 0.9, 0.999, 1e-08, 0.01, 10

VMEM_CAP_V7X = 64 * 1024 * 1024
NEG = -1e30


def _params(sem, nbytes):
    limit = int(min(VMEM_CAP_V7X - (8 << 20), max(32 << 20, 3 * nbytes)))
    return pltpu.CompilerParams(dimension_semantics=sem, vmem_limit_bytes=limit)


def _nbytes(shape, dtype):
    return math.prod(shape) * jnp.dtype(dtype).itemsize


_DIMS = {"nn": (((1,), (0,)), ((), ())), "nt": (((1,), (1,)), ((), ())), "tn": (((0,), (0,)), ((), ()))}


def _mm(a, b, mode, name, *, tm, tn, tk, out_dtype=F32, add=None, dims=None, a_spec=None, b_spec=None,
        o_spec=None, out_shape=None, n_outer=False, copy_dtype=None, after=None):
    if dims is None:
        if mode == "nn":
            (M, K), (_, N) = a.shape, b.shape
        elif mode == "nt":
            (M, K), (N, _) = a.shape, b.shape
        else:
            (K, M), (_, N) = a.shape, b.shape
    else:
        M, N, K = dims
    a_blk = (tk, tm) if mode == "tn" else (tm, tk)
    b_blk = (tn, tk) if mode == "nt" else (tk, tn)
    if a_spec is None:
        a_spec = pl.BlockSpec(a_blk, (lambda i, j, k: (k, i)) if mode == "tn" else (lambda i, j, k: (i, k)))
    if b_spec is None:
        b_spec = pl.BlockSpec(b_blk, (lambda i, j, k: (j, k)) if mode == "nt" else (lambda i, j, k: (k, j)))
    if o_spec is None:
        o_spec = pl.BlockSpec((tm, tn), lambda i, j, k: (i, j))
    if out_shape is None:
        out_shape = (M, N)
    assert M % tm == 0 and N % tn == 0 and K % tk == 0, (name, M, N, K, tm, tn, tk)
    nk = K // tk
    contract = _DIMS[mode]
    has_add = add is not None

    def body(*refs):
        a_ref, b_ref = refs[0], refs[1]
        add_ref = refs[2] if has_add else None
        n_in = 2 + has_add + (after is not None)
        o_ref = refs[n_in]
        copy_ref = refs[n_in + 1] if copy_dtype is not None else None

        def product():
            return lax.dot_general(a_ref[...].astype(MXU_DTYPE), b_ref[...].astype(MXU_DTYPE), contract,
                                   preferred_element_type=F32)

        def finish(r):
            if has_add:
                r = r + add_ref[...]
            o_ref[...] = r.astype(out_dtype)
            if copy_ref is not None:
                copy_ref[...] = r.astype(copy_dtype)

        if nk == 1:
            finish(product())
            return
        acc = refs[-1]
        k = pl.program_id(2)

        @pl.when(k == 0)
        def _():
            acc[...] = jnp.zeros_like(acc)

        acc[...] += product()

        @pl.when(k == nk - 1)
        def _():
            finish(acc[...])

    in_specs = [a_spec, b_spec]
    args = [a, b]
    nbytes = _nbytes(a_blk, a.dtype) + _nbytes(b_blk, b.dtype) + 3 * _nbytes((tm, tn), F32)
    if has_add:
        in_specs.append(pl.BlockSpec((tm, tn), lambda i, j, k: (i, j)))
        args.append(add)
        nbytes += _nbytes((tm, tn), F32)
    if after is not None:
        in_specs.append(pl.BlockSpec(after.shape, lambda i, j, k: (0, 0)))
        args.append(after)
    grid = (M // tm, N // tn, nk)
    if n_outer:
        def swapped(spec):
            return pl.BlockSpec(spec.block_shape, lambda j, i, k, at=spec.index_map: at(i, j, k))

        grid = (N // tn, M // tm, nk)
        in_specs = [swapped(s) for s in in_specs]
        o_spec = swapped(o_spec)
    out_sds, out_specs = jax.ShapeDtypeStruct(out_shape, out_dtype), o_spec
    if copy_dtype is not None:
        out_sds, out_specs = (out_sds, jax.ShapeDtypeStruct(out_shape, copy_dtype)), (o_spec, o_spec)
    return pl.pallas_call(
        body, name=name, out_shape=out_sds, grid=grid, in_specs=in_specs, out_specs=out_specs,
        scratch_shapes=[pltpu.VMEM((tm, tn), F32)] if nk > 1 else [],
        compiler_params=_params(("parallel", "parallel", "arbitrary"), nbytes),
    )(*args)


_GELU_C = math.sqrt(2.0 / math.pi)
_GELU_A = 0.044715


def _sigmoid(x):
    return 0.5 * jnp.tanh(0.5 * x) + 0.5


def _gelu(x):
    t = jnp.tanh(x * (_GELU_C + (_GELU_C * _GELU_A) * (x * x)))
    return x * (0.5 + 0.5 * t)


def _gelu_and_grad(x):
    x2 = x * x
    t = jnp.tanh(x * (_GELU_C + (_GELU_C * _GELU_A) * x2))
    cdf = 0.5 + 0.5 * t
    grad = cdf + (0.5 * x) * (1.0 - t * t) * (_GELU_C + (3.0 * _GELU_C * _GELU_A) * x2)
    return x * cdf, grad


def _rope_mix(g, cos_a, sin_a):
    return g * cos_a + pltpu.roll(g, 64, 1) * sin_a


def _rope_mix_bwd(d, cos_a, sin_a):
    return d * cos_a + pltpu.roll(d * sin_a, 64, 1)


def _rms_fwd(x, g, name, tr=512):
    T, D = x.shape

    def body(x_ref, g_ref, h_ref):
        xv = x_ref[...]
        r = lax.rsqrt(jnp.mean(xv * xv, axis=-1, keepdims=True) + EPS)
        h_ref[...] = ((xv * r) * g_ref[...]).astype(h_ref.dtype)

    return pl.pallas_call(
        body, name=name, out_shape=jax.ShapeDtypeStruct((T, D), MXU_DTYPE), grid=(T // tr,),
        in_specs=[pl.BlockSpec((tr, D), lambda i: (i, 0)), pl.BlockSpec((1, D), lambda i: (0, 0))],
        out_specs=pl.BlockSpec((tr, D), lambda i: (i, 0)),
        compiler_params=_params(("parallel",), 3 * _nbytes((tr, D), F32)),
    )(x, g)


def _rms_bwd(x, g, dh, dres, name, tr=512):
    T, D = x.shape

    def body(x_ref, g_ref, dh_ref, dres_ref, dx_ref, gg_ref):
        @pl.when(pl.program_id(0) == 0)
        def _():
            gg_ref[...] = jnp.zeros_like(gg_ref)

        xv = x_ref[...]
        r = lax.rsqrt(jnp.mean(xv * xv, axis=-1, keepdims=True) + EPS)
        xn = xv * r
        dhv = dh_ref[...]
        dxn = dhv * g_ref[...]
        dx_ref[...] = dres_ref[...] + r * (dxn - xn * jnp.mean(dxn * xn, axis=-1, keepdims=True))
        gg_ref[...] += jnp.sum(dhv * xn, axis=0, keepdims=True)

    row = pl.BlockSpec((tr, D), lambda i: (i, 0))
    vec = pl.BlockSpec((1, D), lambda i: (0, 0))
    return pl.pallas_call(
        body, name=name,
        out_shape=(jax.ShapeDtypeStruct((T, D), F32), jax.ShapeDtypeStruct((1, D), F32)),
        grid=(T // tr,), in_specs=[row, vec, row, row], out_specs=(row, vec),
        compiler_params=_params(("arbitrary",), 6 * _nbytes((tr, D), F32)),
    )(x, g, dh, dres)


def _lat_fwd(z, gq, gkv, wq, wkv, cos_a, sin_a, tr=256):
    T = z.shape[0]
    lat_blk = (4 * D_MODEL) // LAT

    def body(z_ref, gq_ref, gkv_ref, wq_ref, wkv_ref, cos_ref, sin_ref, q_ref, k_ref, v_ref, cqn_ref, ckvn_ref):
        zl = z_ref[...]
        cos_v, sin_v = cos_ref[...], sin_ref[...]
        cq = zl[:, :Q_RANK]
        ckv = zl[:, Q_RANK:Q_RANK + KV_RANK]
        krb = zl[:, Q_RANK + KV_RANK:]
        cqn = ((cq * lax.rsqrt(jnp.mean(cq * cq, axis=-1, keepdims=True) + EPS)) * gq_ref[...]).astype(MXU_DTYPE)
        ckvn = ((ckv * lax.rsqrt(jnp.mean(ckv * ckv, axis=-1, keepdims=True) + EPS)) * gkv_ref[...]).astype(MXU_DTYPE)
        cqn_ref[...] = cqn
        ckvn_ref[...] = ckvn
        krr = _rope_mix(krb, cos_v, sin_v).astype(MXU_DTYPE)
        q = jnp.dot(cqn, wq_ref[...], preferred_element_type=F32)
        kv = jnp.dot(ckvn, wkv_ref[...], preferred_element_type=F32)
        for h in range(HEADS):
            o = h * HEAD_PAD
            q_ref[:, o:o + NOPE] = q[:, o:o + NOPE].astype(MXU_DTYPE)
            q_ref[:, o + NOPE:o + HEAD_PAD] = _rope_mix(q[:, o + NOPE:o + HEAD_PAD], cos_v, sin_v).astype(MXU_DTYPE)
            k_ref[:, o:o + NOPE] = kv[:, h * NOPE:(h + 1) * NOPE].astype(MXU_DTYPE)
            k_ref[:, o + NOPE:o + HEAD_PAD] = krr
        v_ref[...] = kv[:, HEADS * NOPE:].astype(MXU_DTYPE)

    def row(w):
        return pl.BlockSpec((tr, w), lambda i: (i, 0))

    def full(a):
        return pl.BlockSpec(a.shape, lambda i: (0, 0))

    return pl.pallas_call(
        body, name="lat_fwd",
        out_shape=(jax.ShapeDtypeStruct((T, HEADS * HEAD_PAD), MXU_DTYPE), jax.ShapeDtypeStruct((T, HEADS * HEAD_PAD), MXU_DTYPE),
                   jax.ShapeDtypeStruct((T, HEADS * NOPE), MXU_DTYPE), jax.ShapeDtypeStruct((T, Q_RANK), MXU_DTYPE),
                   jax.ShapeDtypeStruct((T, KV_RANK), MXU_DTYPE)),
        grid=(T // tr,),
        in_specs=[pl.BlockSpec((tr, LAT), lambda i: (i, lat_blk)), full(gq), full(gkv), full(wq), full(wkv), row(128), row(128)],
        out_specs=(row(HEADS * HEAD_PAD), row(HEADS * HEAD_PAD), row(HEADS * NOPE), row(Q_RANK), row(KV_RANK)),
        compiler_params=_params(("parallel",), 8 * _nbytes((tr, HEADS * HEAD_PAD), F32)),
    )(z, gq, gkv, wq, wkv, cos_a, sin_a)


ATT_BLOCK = 256
_SCALE = QK_DIM ** -0.5


def _causal_mask(n):
    return lax.broadcasted_iota(jnp.int32, (n, n), 1) <= lax.broadcasted_iota(jnp.int32, (n, n), 0)


def _causal_mask_t(n):
    return lax.broadcasted_iota(jnp.int32, (n, n), 0) <= lax.broadcasted_iota(jnp.int32, (n, n), 1)


ATT_HEADS = 4


def _attn_fwd(q, k, v, B, S):
    tq = ATT_BLOCK
    nq = S // tq
    T = B * S
    hp, groups = ATT_HEADS, HEADS // ATT_HEADS

    def body(q_ref, k_ref, v_ref, o_ref, *lse_refs):
        qi = pl.program_id(2)
        qs = [q_ref[:, t * HEAD_PAD:(t + 1) * HEAD_PAD] for t in range(hp)]

        def scores(j, t):
            rows = pl.ds(pl.multiple_of(j * tq, tq), tq)
            return lax.dot_general(k_ref[rows, t * HEAD_PAD:(t + 1) * HEAD_PAD], qs[t], _DIMS["nt"],
                                   preferred_element_type=F32)

        def step(j, carry, last):
            rows = pl.ds(pl.multiple_of(j * tq, tq), tq)
            out = []
            for t in range(hp):
                m, l, acc, st = carry[t]
                st_next = st if last else scores(j + 1, t)
                st = st * _SCALE
                if last:
                    st = jnp.where(_causal_mask_t(tq), st, NEG)
                m_new = jnp.maximum(m, jnp.max(st, axis=0, keepdims=True))
                alpha = jnp.exp(m - m_new)
                p = jnp.exp(st - m_new)
                l = alpha * l + jnp.sum(p, axis=0, keepdims=True)
                acc = alpha * acc + lax.dot_general(v_ref[rows, t * NOPE:(t + 1) * NOPE], p.astype(MXU_DTYPE),
                                                    _DIMS["tn"], preferred_element_type=F32)
                out.append((m_new, l, acc, st_next))
            return tuple(out)

        init = tuple((jnp.full((1, tq), NEG, F32), jnp.zeros((1, tq), F32), jnp.zeros((NOPE, tq), F32), scores(0, t))
                     for t in range(hp))
        carry = lax.fori_loop(0, qi, lambda j, c: step(j, c, False), init)
        carry = step(qi, carry, True)
        for t in range(hp):
            m, l, acc, _ = carry[t]
            o_ref[:, t * NOPE:(t + 1) * NOPE] = (acc / l).T
            lse_refs[t][0] = m + jnp.log(l)

    lse_sds = jax.ShapeDtypeStruct((groups * B * nq, 1, tq), F32)
    lse_spec = pl.BlockSpec((1, 1, tq), lambda b, h, i: ((h * B + b) * nq + i, 0, 0))
    return pl.pallas_call(
        body, name="attn_fwd",
        out_shape=(jax.ShapeDtypeStruct((T, HEADS * NOPE), F32),) + (lse_sds,) * hp,
        grid=(B, groups, nq),
        in_specs=[pl.BlockSpec((tq, hp * HEAD_PAD), lambda b, h, i: (b * nq + i, h)),
                  pl.BlockSpec((S, hp * HEAD_PAD), lambda b, h, i: (b, h)),
                  pl.BlockSpec((S, hp * NOPE), lambda b, h, i: (b, h))],
        out_specs=(pl.BlockSpec((tq, hp * NOPE), lambda b, h, i: (b * nq + i, h)),) + (lse_spec,) * hp,
        compiler_params=_params(("parallel", "parallel", "arbitrary"), 4 * hp * _nbytes((S, HEAD_PAD), MXU_DTYPE)),
    )(q, k, v)


def _attn_bwd(q, k, v, do, lses, delta, B, S):
    tq = ATT_BLOCK
    nq = S // tq
    T = B * S
    hp, groups = ATT_HEADS, HEADS // ATT_HEADS

    def body(q_ref, k_ref, v_ref, do_ref, *refs):
        lse_refs, dl_refs = refs[:hp], refs[hp:2 * hp]
        dq_out, dk_ref, dv_ref, dq_ref = refs[2 * hp:]
        kj = pl.program_id(2)

        @pl.when(kj == 0)
        def _():
            dq_ref[...] = jnp.zeros_like(dq_ref)

        def products(i, t):
            rows = pl.ds(pl.multiple_of(i * tq, tq), tq)
            st = lax.dot_general(k_ref[:, t * HEAD_PAD:(t + 1) * HEAD_PAD], q_ref[rows, t * HEAD_PAD:(t + 1) * HEAD_PAD],
                                 _DIMS["nt"], preferred_element_type=F32)
            dpt = lax.dot_general(v_ref[:, t * NOPE:(t + 1) * NOPE], do_ref[rows, t * NOPE:(t + 1) * NOPE],
                                  _DIMS["nt"], preferred_element_type=F32)
            return st, dpt

        def step(i, carry, masked):
            rows = pl.ds(pl.multiple_of(i * tq, tq), tq)
            nxt = jnp.minimum(i + 1, nq - 1)
            out = []
            for t in range(hp):
                dk, dv, st, dpt = carry[t]
                st_next, dpt_next = products(nxt, t)
                qk_cols = slice(t * HEAD_PAD, (t + 1) * HEAD_PAD)
                v_cols = slice(t * NOPE, (t + 1) * NOPE)
                p = jnp.exp(st * _SCALE - lse_refs[t][i])
                if masked:
                    p = jnp.where(_causal_mask_t(tq), p, 0.0)
                dv = dv + jnp.dot(p.astype(MXU_DTYPE), do_ref[rows, v_cols], preferred_element_type=F32)
                ds = (p * (dpt - dl_refs[t][i]) * _SCALE).astype(MXU_DTYPE)
                dk = dk + jnp.dot(ds, q_ref[rows, qk_cols], preferred_element_type=F32)
                dq_ref[rows, qk_cols] += lax.dot_general(ds, k_ref[:, qk_cols], _DIMS["tn"], preferred_element_type=F32)
                out.append((dk, dv, st_next, dpt_next))
            return tuple(out)

        init = tuple((jnp.zeros((tq, HEAD_PAD), F32), jnp.zeros((tq, NOPE), F32)) + products(kj, t) for t in range(hp))
        carry = step(kj, init, True)
        carry = lax.fori_loop(kj + 1, nq, lambda i, c: step(i, c, False), carry)
        for t in range(hp):
            dk_ref[:, t * HEAD_PAD:(t + 1) * HEAD_PAD] = carry[t][0].astype(dk_ref.dtype)
            dv_ref[:, t * NOPE:(t + 1) * NOPE] = carry[t][1].astype(dv_ref.dtype)

        @pl.when(kj == nq - 1)
        def _():
            dq_out[...] = dq_ref[...].astype(dq_out.dtype)

    seq = lambda w: pl.BlockSpec((S, w), lambda b, h, j: (b, h))
    blk = lambda w: pl.BlockSpec((tq, w), lambda b, h, j: (b * nq + j, h))
    lse_spec = pl.BlockSpec((nq, 1, tq), lambda b, h, j: (h * B + b, 0, 0))
    dl_specs = [pl.BlockSpec((nq, 1, tq), lambda b, h, j, t=t: ((h * hp + t) * B + b, 0, 0)) for t in range(hp)]
    return pl.pallas_call(
        body, name="attn_bwd",
        out_shape=(jax.ShapeDtypeStruct((T, HEADS * HEAD_PAD), MXU_DTYPE), jax.ShapeDtypeStruct((T, HEADS * HEAD_PAD), MXU_DTYPE),
                   jax.ShapeDtypeStruct((T, HEADS * NOPE), MXU_DTYPE)),
        grid=(B, groups, nq),
        in_specs=[seq(hp * HEAD_PAD), blk(hp * HEAD_PAD), blk(hp * NOPE), seq(hp * NOPE)] + [lse_spec] * hp + dl_specs,
        out_specs=(seq(hp * HEAD_PAD), blk(hp * HEAD_PAD), blk(hp * NOPE)),
        scratch_shapes=[pltpu.VMEM((S, hp * HEAD_PAD), F32)],
        compiler_params=_params(("parallel", "parallel", "arbitrary"), 8 * hp * _nbytes((S, HEAD_PAD), F32)),
    )(q, k, v, do, *lses, *([delta] * hp))


MIX_ROWS = 256


def _tril_weights(ws_ref, g):
    return jnp.where(_causal_mask(CHUNK), ws_ref[g], 0.0).astype(MXU_DTYPE)


def _layer_norm_stats(va):
    mu = jnp.mean(va, axis=-1, keepdims=True)
    xc = va - mu
    rs = lax.rsqrt(jnp.mean(xc * xc, axis=-1, keepdims=True) + EPS)
    return xc * rs


def _mix_specs(tr):
    zcol = lambda c: pl.BlockSpec((tr, D_MODEL), lambda i, c=c: (i, c))
    row = pl.BlockSpec((tr, D_MODEL), lambda i: (i, 0))
    vec = pl.BlockSpec((1, D_MODEL), lambda i: (0, 0))
    ws = pl.BlockSpec((A_GROUPS, CHUNK, CHUNK), lambda i: (0, 0, 0))
    bs = pl.BlockSpec((CHUNK, 128), lambda i: (0, 0))
    return zcol, row, vec, ws, bs


def _mix_fwd(z, yb, ln_g, ln_b, ws, bs_t):
    T = z.shape[0]
    tr = MIX_ROWS
    zcol, row, vec, ws_spec, bs_spec = _mix_specs(tr)

    def body(zu_ref, zv_ref, zga_ref, zgb_ref, yb_ref, g_ref, b_ref, ws_ref, bs_ref, out_ref, vn_s):
        vhat = _layer_norm_stats(_gelu(zv_ref[...]))
        vn_s[...] = (vhat * g_ref[...] + b_ref[...]).astype(MXU_DTYPE)
        for g in range(A_GROUPS):
            w = _tril_weights(ws_ref, g)
            bias = bs_ref[:, g:g + 1]
            cols = slice(g * CHUNK, (g + 1) * CHUNK)
            for c in range(tr // CHUNK):
                rows = slice(c * CHUNK, (c + 1) * CHUNK)
                mixed = jnp.dot(w, vn_s[rows, cols], preferred_element_type=F32) + bias
                ya = _gelu(zu_ref[rows, cols]) * mixed
                merged = _sigmoid(zga_ref[rows, cols]) * ya + _sigmoid(zgb_ref[rows, cols]) * yb_ref[rows, cols]
                out_ref[rows, cols] = merged.astype(MXU_DTYPE)

    return pl.pallas_call(
        body, name="mix_fwd", out_shape=jax.ShapeDtypeStruct((T, D_MODEL), MXU_DTYPE), grid=(T // tr,),
        in_specs=[zcol(0), zcol(1), zcol(2), zcol(3), row, vec, vec, ws_spec, bs_spec], out_specs=row,
        scratch_shapes=[pltpu.VMEM((tr, D_MODEL), MXU_DTYPE)],
        compiler_params=_params(("parallel",), 8 * _nbytes((tr, D_MODEL), F32)),
    )(z, z, z, z, yb, ln_g, ln_b, ws, bs_t)


def _mix_bwd(z, yb, dm, ln_g, ln_b, ws, bs_t):
    T = z.shape[0]
    tr = MIX_ROWS
    zcol, row, vec, ws_spec, bs_spec = _mix_specs(tr)

    def body(zu_ref, zv_ref, zga_ref, zgb_ref, yb_ref, dm_ref, g_ref, b_ref, ws_ref, bs_ref,
             dz_ref, dyb_ref, dl_ref, gws_ref, gbs_ref, glg_ref, glb_ref, vn_s, dvn_s):
        @pl.when(pl.program_id(0) == 0)
        def _():
            gws_ref[...] = jnp.zeros_like(gws_ref)
            gbs_ref[...] = jnp.zeros_like(gbs_ref)
            glg_ref[...] = jnp.zeros_like(glg_ref)
            glb_ref[...] = jnp.zeros_like(glb_ref)

        lane = lax.broadcasted_iota(jnp.int32, (CHUNK, 128), 1)
        va, dgelu_v = _gelu_and_grad(zv_ref[...])
        mu = jnp.mean(va, axis=-1, keepdims=True)
        xc = va - mu
        rs = lax.rsqrt(jnp.mean(xc * xc, axis=-1, keepdims=True) + EPS)
        vhat = xc * rs
        vn_s[...] = (vhat * g_ref[...] + b_ref[...]).astype(MXU_DTYPE)
        gbs_acc = jnp.zeros((CHUNK, 128), F32)
        for g in range(A_GROUPS):
            w = _tril_weights(ws_ref, g)
            bias = bs_ref[:, g:g + 1]
            cols = slice(g * CHUNK, (g + 1) * CHUNK)
            gw_acc = jnp.zeros((CHUNK, CHUNK), F32)
            for c in range(tr // CHUNK):
                rows = slice(c * CHUNK, (c + 1) * CHUNK)
                vn = vn_s[rows, cols]
                mixed = jnp.dot(w, vn, preferred_element_type=F32) + bias
                ua, dgelu_u = _gelu_and_grad(zu_ref[rows, cols])
                dmv = dm_ref[rows, cols]
                sa = _sigmoid(zga_ref[rows, cols])
                dya = dmv * sa
                dz_ref[rows, 2 * D_MODEL + g * CHUNK:2 * D_MODEL + (g + 1) * CHUNK] = (
                    dmv * (ua * mixed) * (sa * (1.0 - sa))).astype(dz_ref.dtype)
                dz_ref[rows, cols] = (dya * mixed * dgelu_u).astype(dz_ref.dtype)
                dmix = dya * ua
                gbs_acc = gbs_acc + jnp.where(lane == g, jnp.sum(dmix, axis=-1, keepdims=True), 0.0)
                dmix_b = dmix.astype(MXU_DTYPE)
                gw_acc = gw_acc + lax.dot_general(dmix_b, vn, _DIMS["nt"], preferred_element_type=F32)
                dvn_s[rows, cols] = lax.dot_general(w, dmix_b, _DIMS["tn"], preferred_element_type=F32)
            gws_ref[g] += jnp.where(_causal_mask(CHUNK), gw_acc, 0.0)
        gbs_ref[...] += gbs_acc

        dvn = dvn_s[...]
        glg_ref[...] += jnp.sum(dvn * vhat, axis=0, keepdims=True)
        glb_ref[...] += jnp.sum(dvn, axis=0, keepdims=True)
        dvh = dvn * g_ref[...]
        dva = rs * (dvh - jnp.mean(dvh, axis=-1, keepdims=True) - vhat * jnp.mean(dvh * vhat, axis=-1, keepdims=True))
        dz_ref[:, D_MODEL:2 * D_MODEL] = (dva * dgelu_v).astype(dz_ref.dtype)

        dmv = dm_ref[...]
        ybv = yb_ref[...]
        sb = _sigmoid(zgb_ref[...])
        dyb = dmv * sb
        dyb_ref[...] = dyb.astype(dyb_ref.dtype)
        dz_ref[:, 3 * D_MODEL:4 * D_MODEL] = (dmv * ybv * (sb * (1.0 - sb))).astype(dz_ref.dtype)
        dz_ref[:, 4 * D_MODEL:] = jnp.zeros((tr, LAT), dz_ref.dtype)
        prod = dyb * ybv
        sel = (lax.broadcasted_iota(jnp.int32, (HEADS, D_MODEL), 1) // NOPE
               == lax.broadcasted_iota(jnp.int32, (HEADS, D_MODEL), 0)).astype(jnp.bfloat16)
        hi = prod.astype(jnp.bfloat16)
        rest = prod - hi.astype(F32)
        mid = rest.astype(jnp.bfloat16)
        lo = (rest - mid.astype(F32)).astype(jnp.bfloat16)
        dl_ref[...] = (lax.dot_general(sel, hi, _DIMS["nt"], preferred_element_type=F32)
                       + lax.dot_general(sel, mid, _DIMS["nt"], preferred_element_type=F32)
                       + lax.dot_general(sel, lo, _DIMS["nt"], preferred_element_type=F32))

    return pl.pallas_call(
        body, name="mix_bwd",
        out_shape=(jax.ShapeDtypeStruct((T, IN_PAD), MXU_DTYPE), jax.ShapeDtypeStruct((T, D_MODEL), MXU_DTYPE),
                   jax.ShapeDtypeStruct((HEADS, T), F32), jax.ShapeDtypeStruct((A_GROUPS, CHUNK, CHUNK), F32),
                   jax.ShapeDtypeStruct((CHUNK, 128), F32), jax.ShapeDtypeStruct((1, D_MODEL), F32),
                   jax.ShapeDtypeStruct((1, D_MODEL), F32)),
        grid=(T // tr,),
        in_specs=[zcol(0), zcol(1), zcol(2), zcol(3), row, row, vec, vec, ws_spec, bs_spec],
        out_specs=(pl.BlockSpec((tr, IN_PAD), lambda i: (i, 0)), row, pl.BlockSpec((HEADS, tr), lambda i: (0, i)),
                   ws_spec, bs_spec, vec, vec),
        scratch_shapes=[pltpu.VMEM((tr, D_MODEL), MXU_DTYPE), pltpu.VMEM((tr, D_MODEL), F32)],
        compiler_params=_params(("arbitrary",), 12 * _nbytes((tr, D_MODEL), F32)),
    )(z, z, z, z, yb, dm, ln_g, ln_b, ws, bs_t)


def _lat_bwd(dz, z, dq, dk, dv, gq, gkv, wq, wkv, cos_a, sin_a, tr=256):
    T = z.shape[0]
    lat_blk = (4 * D_MODEL) // LAT

    def body(dz_in, z_ref, dq_ref, dk_ref, dv_ref, gq_ref, gkv_ref, wq_ref, wkv_ref, cos_ref, sin_ref,
             dz_ref, dqr_ref, dkv_ref, ggq_ref, ggkv_ref):
        del dz_in

        @pl.when(pl.program_id(0) == 0)
        def _():
            ggq_ref[...] = jnp.zeros_like(ggq_ref)
            ggkv_ref[...] = jnp.zeros_like(ggkv_ref)

        cos_v, sin_v = cos_ref[...], sin_ref[...]
        dkr = jnp.zeros((tr, 128), F32)
        for h in range(HEADS):
            o = h * HEAD_PAD
            dqr_ref[:, o:o + NOPE] = dq_ref[:, o:o + NOPE].astype(MXU_DTYPE)
            dqr_ref[:, o + NOPE:o + HEAD_PAD] = _rope_mix_bwd(dq_ref[:, o + NOPE:o + HEAD_PAD], cos_v, sin_v).astype(MXU_DTYPE)
            dkv_ref[:, h * NOPE:(h + 1) * NOPE] = dk_ref[:, o:o + NOPE].astype(MXU_DTYPE)
            dkr = dkr + _rope_mix_bwd(dk_ref[:, o + NOPE:o + HEAD_PAD], cos_v, sin_v)
        dkv_ref[:, HEADS * NOPE:] = dv_ref[...]
        dcqn = lax.dot_general(dqr_ref[...], wq_ref[...], _DIMS["nt"], preferred_element_type=F32)
        dckvn = lax.dot_general(dkv_ref[...], wkv_ref[...], _DIMS["nt"], preferred_element_type=F32)

        zl = z_ref[...]

        def rms_bwd(c, dn, g_ref, gg_ref):
            r = lax.rsqrt(jnp.mean(c * c, axis=-1, keepdims=True) + EPS)
            ch = c * r
            gg_ref[...] += jnp.sum(dn * ch, axis=0, keepdims=True)
            dch = dn * g_ref[...]
            return r * (dch - ch * jnp.mean(dch * ch, axis=-1, keepdims=True))

        dz_ref[:, :Q_RANK] = rms_bwd(zl[:, :Q_RANK], dcqn, gq_ref, ggq_ref).astype(dz_ref.dtype)
        dz_ref[:, Q_RANK:Q_RANK + KV_RANK] = rms_bwd(zl[:, Q_RANK:Q_RANK + KV_RANK], dckvn, gkv_ref, ggkv_ref).astype(dz_ref.dtype)
        dz_ref[:, Q_RANK + KV_RANK:] = dkr.astype(dz_ref.dtype)

    def row(w):
        return pl.BlockSpec((tr, w), lambda i: (i, 0))

    def full(a):
        return pl.BlockSpec(a.shape, lambda i: (0, 0))

    lat = pl.BlockSpec((tr, LAT), lambda i: (i, lat_blk))
    return pl.pallas_call(
        body, name="lat_bwd",
        out_shape=(jax.ShapeDtypeStruct(dz.shape, dz.dtype), jax.ShapeDtypeStruct((T, HEADS * HEAD_PAD), MXU_DTYPE),
                   jax.ShapeDtypeStruct((T, 2 * HEADS * NOPE), MXU_DTYPE), jax.ShapeDtypeStruct(gq.shape, F32),
                   jax.ShapeDtypeStruct(gkv.shape, F32)),
        grid=(T // tr,),
        in_specs=[pl.BlockSpec(memory_space=pl.ANY), lat, row(HEADS * HEAD_PAD), row(HEADS * HEAD_PAD), row(HEADS * NOPE),
                  full(gq), full(gkv), full(wq), full(wkv), row(128), row(128)],
        out_specs=(lat, row(HEADS * HEAD_PAD), row(2 * HEADS * NOPE), full(gq), full(gkv)),
        input_output_aliases={0: 0},
        compiler_params=_params(("arbitrary",), 8 * _nbytes((tr, HEADS * HEAD_PAD), F32)),
    )(dz, z, dq, dk, dv, gq, gkv, wq, wkv, cos_a, sin_a)


GATE_ROWS = 64
HALO = 8


def _taps(ref, half, r, first):
    C = GATE_ROWS
    if first:
        xs = jnp.concatenate([jnp.zeros((HALO, ref.shape[-1]), F32), ref[half, 0:C, :]], axis=0)
    else:
        xs = ref[half, pl.ds(pl.multiple_of(r * C - HALO, HALO), C + HALO), :]
    return xs[HALO:, :], pltpu.roll(xs, 1, 0)[HALO:, :], pltpu.roll(xs, 2, 0)[HALO:, :]


def _conv_taps(taps, cw, cb):
    x0, x1, x2 = taps
    return cb + cw[0:1, :] * x2 + cw[1:2, :] * x1 + cw[2:3, :] * x0


def _fold8(x):
    acc = x[0:8, :]
    for i in range(1, x.shape[0] // 8):
        acc = acc + x[8 * i:8 * (i + 1), :]
    return acc


def _gate_fwd(up3, conv_w, conv_b, B, S):
    T = B * S
    W = FF_TILE
    C = GATE_ROWS

    def body(up_ref, cw_ref, cb_ref, act_ref, conv_ref):
        def chunk(r, first):
            gate = _conv_taps(_taps(up_ref, 0, r, first), cw_ref[0], cb_ref[0])
            val = _conv_taps(_taps(up_ref, 1, r, first), cw_ref[1], cb_ref[1])
            rows = pl.ds(0 if first else pl.multiple_of(r * C, C), C)
            conv_ref[0, rows, :] = gate.astype(conv_ref.dtype)
            conv_ref[1, rows, :] = val.astype(conv_ref.dtype)
            act_ref[rows, :] = (gate * _sigmoid(gate) * val).astype(act_ref.dtype)

        chunk(0, True)

        @pl.loop(1, S // C)
        def _(r):
            chunk(r, False)

    up_spec = pl.BlockSpec((2, S, W), lambda b, j: (0, b, j))
    return pl.pallas_call(
        body, name="gate_fwd",
        out_shape=(jax.ShapeDtypeStruct((T, D_FF), MXU_DTYPE), jax.ShapeDtypeStruct((2, T, D_FF), MXU_DTYPE)),
        grid=(B, N_FF_TILES),
        in_specs=[up_spec, pl.BlockSpec((2, 3, W), lambda b, j: (0, 0, j)), pl.BlockSpec((2, 1, W), lambda b, j: (0, 0, j))],
        out_specs=(pl.BlockSpec((S, W), lambda b, j: (b, j)), up_spec),
        compiler_params=_params(("parallel", "parallel"), 8 * _nbytes((S, W), F32)),
    )(up3, conv_w, conv_b)


def _gate_bwd(up3, conv3, dact, conv_w, B, S):
    T = B * S
    W = FF_TILE
    C = GATE_ROWS

    def body(up_ref, conv_ref, da_ref, cw_ref, dup_ref, gcw_ref, gcb_ref, d_s):
        @pl.when(pl.program_id(1) == 0)
        def _():
            gcw_ref[...] = jnp.zeros_like(gcw_ref)
            gcb_ref[...] = jnp.zeros_like(gcb_ref)

        @pl.loop(0, S // C)
        def _(r):
            rows = pl.ds(pl.multiple_of(r * C, C), C)
            gate, val = conv_ref[0, rows, :].astype(F32), conv_ref[1, rows, :].astype(F32)
            sg = _sigmoid(gate)
            da = da_ref[rows, :]
            d_s[0, rows, :] = da * val * (sg * (1.0 + gate * (1.0 - sg)))
            d_s[1, rows, :] = da * (gate * sg)

        d_s[:, S:S + HALO, :] = jnp.zeros((2, HALO, W), F32)

        def chunk(r, sums):
            base = pl.multiple_of(r * C, C)
            out = []
            for half in (0, 1):
                ds_ = d_s[half, pl.ds(base, C + HALO), :]
                d0, d1, d2 = ds_[:C, :], pltpu.roll(ds_, C + HALO - 1, 0)[:C, :], pltpu.roll(ds_, C + HALO - 2, 0)[:C, :]
                cw = cw_ref[half]
                dup_ref[half, pl.ds(base, C), :] = (cw[2:3, :] * d0 + cw[1:2, :] * d1 + cw[0:1, :] * d2).astype(dup_ref.dtype)
                x = up_ref[half, pl.ds(base, C), :]
                sb, s0, s1, s2 = sums[half]
                out.append((sb + _fold8(d0), s0 + _fold8(d2 * x), s1 + _fold8(d1 * x), s2 + _fold8(d0 * x)))
            return tuple(out)

        zeros = tuple(tuple(jnp.zeros((8, W), F32) for _ in range(4)) for _ in range(2))
        sums = lax.fori_loop(0, S // C, chunk, zeros)
        for half in (0, 1):
            sb, s0, s1, s2 = sums[half]
            gcb_ref[half] += jnp.sum(sb, axis=0, keepdims=True)
            gcw_ref[half, 0:1, :] += jnp.sum(s0, axis=0, keepdims=True)
            gcw_ref[half, 1:2, :] += jnp.sum(s1, axis=0, keepdims=True)
            gcw_ref[half, 2:3, :] += jnp.sum(s2, axis=0, keepdims=True)

    up_spec = pl.BlockSpec((2, S, W), lambda j, b: (0, b, j))
    cw_spec = pl.BlockSpec((2, 3, W), lambda j, b: (0, 0, j))
    cb_spec = pl.BlockSpec((2, 1, W), lambda j, b: (0, 0, j))
    return pl.pallas_call(
        body, name="gate_bwd",
        out_shape=(jax.ShapeDtypeStruct((2, T, D_FF), MXU_DTYPE), jax.ShapeDtypeStruct((2, 3, D_FF), F32),
                   jax.ShapeDtypeStruct((2, 1, D_FF), F32)),
        grid=(N_FF_TILES, B),
        in_specs=[up_spec, up_spec, pl.BlockSpec((S, W), lambda j, b: (b, j)), cw_spec],
        out_specs=(up_spec, cw_spec, cb_spec),
        scratch_shapes=[pltpu.VMEM((2, S + HALO, W), F32)],
        compiler_params=_params(("parallel", "arbitrary"), 12 * _nbytes((S, W), F32)),
    )(up3, conv3, dact, conv_w)


def _final(x2, tgt, g, tr=512):
    T, D = x2.shape

    def body(x_ref, t_ref, g_ref, dx_ref, loss_ref, gg_ref):
        @pl.when(pl.program_id(0) == 0)
        def _():
            loss_ref[...] = jnp.zeros_like(loss_ref)
            gg_ref[...] = jnp.zeros_like(gg_ref)

        xv = x_ref[...]
        gv = g_ref[...]
        r = lax.rsqrt(jnp.mean(xv * xv, axis=-1, keepdims=True) + EPS)
        xn = xv * r
        err = xn * gv - t_ref[...]
        loss_ref[...] += 0.5 * jnp.sum(jnp.mean(err * err, axis=-1, keepdims=True), axis=0, keepdims=True)
        dy = err * (1.0 / D)
        gg_ref[...] += jnp.sum(dy * xn, axis=0, keepdims=True)
        dxn = dy * gv
        dx_ref[...] = r * (dxn - xn * jnp.mean(dxn * xn, axis=-1, keepdims=True))

    row = pl.BlockSpec((tr, D), lambda i: (i, 0))
    vec = pl.BlockSpec((1, D), lambda i: (0, 0))
    return pl.pallas_call(
        body, name="final_loss",
        out_shape=(jax.ShapeDtypeStruct((T, D), F32), jax.ShapeDtypeStruct((1, 128), F32), jax.ShapeDtypeStruct((1, D), F32)),
        grid=(T // tr,), in_specs=[row, row, vec],
        out_specs=(row, pl.BlockSpec((1, 128), lambda i: (0, 0)), vec),
        compiler_params=_params(("arbitrary",), 6 * _nbytes((tr, D), F32)),
    )(x2, tgt, g)


def _sum_slabs(parts, name, tr):
    rows, cols = parts[0].shape
    n = len(parts)

    def body(*refs):
        acc = refs[0][...]
        for r in refs[1:n]:
            acc = acc + r[...]
        refs[n][...] = acc

    blk = pl.BlockSpec((tr, cols), lambda i: (i, 0))
    return pl.pallas_call(
        body, name=name, out_shape=jax.ShapeDtypeStruct((rows, cols), F32), grid=(rows // tr,),
        in_specs=[blk] * n, out_specs=blk,
        compiler_params=_params(("parallel",), (n + 1) * _nbytes((tr, cols), F32)),
    )(*parts)


ADAMW_BLOCK_BYTES = 2400 * 1024


def _adamw(w, g, m, v, name):
    lead = w.ndim == 3
    rows, cols = w.shape[-2:]
    fits = [d for d in range(8, rows + 1, 8) if rows % d == 0 and d * cols * 4 <= ADAMW_BLOCK_BYTES]
    tr = max(fits) if fits else rows
    c1 = 1.0 - ADAM_B1 ** ADAM_STEP
    c2 = 1.0 - ADAM_B2 ** ADAM_STEP

    def body(w_ref, g_ref, m_ref, v_ref, d_ref, nm_ref, nv_ref):
        gv = g_ref[...]
        nm = ADAM_B1 * m_ref[...] + (1.0 - ADAM_B1) * gv
        nv = ADAM_B2 * v_ref[...] + (1.0 - ADAM_B2) * (gv * gv)
        nm_ref[...] = nm
        nv_ref[...] = nv
        d_ref[...] = -ADAM_LR * ((nm / c1) / (jnp.sqrt(nv / c2) + ADAM_EPS) + ADAM_WD * w_ref[...])

    blk = pl.BlockSpec((None, tr, cols), lambda i: (0, i, 0)) if lead else pl.BlockSpec((tr, cols), lambda i: (i, 0))
    sds = jax.ShapeDtypeStruct(w.shape, F32)
    return pl.pallas_call(
        body, name=name, out_shape=(sds, sds, sds), grid=(rows // tr,), in_specs=[blk] * 4, out_specs=(blk, blk, blk),
        compiler_params=_params(("parallel",), 7 * _nbytes((tr, cols), F32)),
    )(w, g, m, v)


_ANY = pl.BlockSpec(memory_space=pl.ANY)


def _place():
    x, y, c = lax.axis_index("x"), lax.axis_index("y"), lax.axis_index("c")
    chips = [(1 - x, y), (x, 1 - y), (1 - x, 1 - y)]
    return x, y, c, chips


def _forward_halves(lands):
    n = len(lands)

    def body(*refs):
        outs, send, recv = refs[n:2 * n], refs[2 * n], refs[2 * n + 1]
        x, y, c, chips = _place()
        cps = []
        for w in range(n):
            for j, (px, py) in enumerate(chips):
                landed = outs[w].at[2 * px + py, c]
                cps.append(pltpu.make_async_remote_copy(
                    src_ref=landed, dst_ref=landed, send_sem=send.at[3 * w + j], recv_sem=recv.at[3 * w + j],
                    device_id=(x, y, 1 - c), device_id_type=MESH))
        for cp in cps:
            cp.start()
        for w in range(n):
            for j, (px, py) in enumerate(chips):
                other = outs[w].at[2 * px + py, 1 - c]
                pltpu.make_async_remote_copy(src_ref=other, dst_ref=other, send_sem=send.at[3 * w + j],
                                             recv_sem=recv.at[3 * w + j], device_id=(x, y, 1 - c),
                                             device_id_type=MESH).wait_recv()
        for cp in cps:
            cp.wait_send()

    dma = lambda k: pltpu.SemaphoreType.DMA((k,))
    return pl.pallas_call(
        body, name="gather_forward_halves", out_shape=tuple(jax.ShapeDtypeStruct(a.shape, a.dtype) for a in lands),
        in_specs=[_ANY] * n, out_specs=tuple([_ANY] * n), input_output_aliases={w: w for w in range(n)},
        scratch_shapes=[dma(3 * n), dma(3 * n)],
    )(*lands)


_HBM = pl.BlockSpec(memory_space=pltpu.HBM)
_SEM = pl.BlockSpec(memory_space=pltpu.SEMAPHORE)
_EFFECT = pltpu.SideEffectType.DATAFLOW_SIDE_EFFECTING


SEMS_PER_ARRAY = 8


def _exchange_copies(srcs, lands, send, recv, mode):
    x, y, c, chips = _place()
    if mode == "halves":
        cps = []
        for w, (src, land) in enumerate(zip(srcs, lands)):
            pieces = [(src.at[c], land.at[2 * x + y, c], (px, py, c)) for px, py in chips]
            pieces.append((src, land.at[2 * x + y], (x, y, 1 - c)))
            for k, (piece, dst, peer) in enumerate(pieces):
                cps.append(pltpu.make_async_remote_copy(
                    src_ref=piece, dst_ref=dst, send_sem=send.at[SEMS_PER_ARRAY * w + k],
                    recv_sem=recv.at[SEMS_PER_ARRAY * w + k], device_id=peer, device_id_type=MESH))
        return cps
    if mode == "swap":
        return [pltpu.make_async_remote_copy(
            src_ref=src.at[:, 1 - c], dst_ref=land, send_sem=send.at[SEMS_PER_ARRAY * w],
            recv_sem=recv.at[SEMS_PER_ARRAY * w], device_id=(x, y, 1 - c), device_id_type=MESH)
            for w, (src, land) in enumerate(zip(srcs, lands))]
    if mode == "all":
        flips = [(fx, fy, fc) for fx in (0, 1) for fy in (0, 1) for fc in (0, 1)][1:]
        peers = [(x ^ fx, y ^ fy, c ^ fc) for fx, fy, fc in flips]
        slot = 4 * x + 2 * y + c
    else:
        peers = [(px, py, c) for px, py in chips] + ([(x, y, 1 - c)] if mode == "gather" else [])
        slot = 2 * x + y
    cps = []
    for w, (src, land) in enumerate(zip(srcs, lands)):
        for k, peer in enumerate(peers):
            piece = src.at[2 * peer[0] + peer[1]] if mode == "scatter" else src
            cps.append(pltpu.make_async_remote_copy(
                src_ref=piece, dst_ref=land.at[slot], send_sem=send.at[SEMS_PER_ARRAY * w + k],
                recv_sem=recv.at[SEMS_PER_ARRAY * w + k], device_id=peer, device_id_type=MESH))
    return cps


def _exchange_start(srcs, name, mode, after):
    n = len(srcs)
    if mode == "swap":
        land_shapes = [(s.shape[0],) + s.shape[2:] for s in srcs]
    else:
        lead = {"gather": (N_CHIPS,), "halves": (N_CHIPS,), "scatter": (), "all": (2 * N_CHIPS,)}[mode]
        land_shapes = [lead + s.shape for s in srcs]

    def body(*refs):
        src_refs, land_refs = refs[:n], refs[n:2 * n]
        send, recv = refs[2 * n + 1], refs[2 * n + 2]
        token = refs[-1]
        for cp in _exchange_copies(src_refs, land_refs, send, recv, mode):
            cp.start()
        token[...] = jnp.zeros_like(token)

    sems = pltpu.SemaphoreType.DMA((SEMS_PER_ARRAY * n,))
    out = pl.pallas_call(
        body, name=name,
        out_shape=(sems, sems, *[pltpu.HBM(s.shape, s.dtype) for s in srcs],
                   *[pltpu.HBM(shp, s.dtype) for shp, s in zip(land_shapes, srcs)], jax.ShapeDtypeStruct((8, 128), F32)),
        in_specs=[_HBM] * (2 * n) + [_ANY],
        out_specs=(_SEM, _SEM, *[_HBM] * (2 * n), pl.BlockSpec(memory_space=pltpu.VMEM)),
        input_output_aliases={i: 2 + i for i in range(2 * n)},
        compiler_params=pltpu.CompilerParams(has_side_effects=_EFFECT),
    )(*[pltpu.with_memory_space_constraint(s, pltpu.HBM) for s in srcs],
      *[pltpu.with_memory_space_constraint(lax.empty(shp, s.dtype), pltpu.HBM) for shp, s in zip(land_shapes, srcs)],
      after)
    return out[0], out[1], out[2:2 + n], out[2 + n:2 + 2 * n], out[-1]


def _exchange_wait(started, name, mode, after):
    send, recv, src_thru, land_thru, _ = started
    n = len(src_thru)
    after = list(after) if isinstance(after, (list, tuple)) else [after]

    def body(*refs):
        src_refs, land_refs, send_ref, recv_ref = refs[:n], refs[n:2 * n], refs[2 * n], refs[2 * n + 1]
        for cp in _exchange_copies(src_refs, land_refs, send_ref, recv_ref, mode):
            cp.wait_send()
            cp.wait_recv()

    out = pl.pallas_call(
        body, name=name,
        out_shape=tuple(pltpu.HBM(a.shape, a.dtype) for a in list(src_thru) + list(land_thru)),
        in_specs=[_HBM] * (2 * n) + [_SEM, _SEM] + [_ANY] * len(after), out_specs=tuple([_HBM] * (2 * n)),
        input_output_aliases={i: i for i in range(2 * n)},
        compiler_params=pltpu.CompilerParams(has_side_effects=_EFFECT),
    )(*src_thru, *land_thru, send, recv, *after)
    return out[:n], out[n:]


def _swap_halves(gs, name):
    n = len(gs)

    def body(*refs):
        ins, outs, send, recv = refs[:n], refs[n:2 * n], refs[2 * n], refs[2 * n + 1]
        x, y, c, _ = _place()
        cps = []
        for w in range(n):
            cps.append(pltpu.make_async_remote_copy(
                src_ref=ins[w].at[:, 1 - c], dst_ref=outs[w], send_sem=send.at[w], recv_sem=recv.at[w],
                device_id=(x, y, 1 - c), device_id_type=MESH))
        for cp in cps:
            cp.start()
        for cp in cps:
            cp.wait()

    return pl.pallas_call(
        body, name=name,
        out_shape=tuple(jax.ShapeDtypeStruct((g.shape[0],) + g.shape[2:], g.dtype) for g in gs),
        in_specs=[_ANY] * n, out_specs=tuple([_ANY] * n),
        scratch_shapes=[pltpu.SemaphoreType.DMA((n,)), pltpu.SemaphoreType.DMA((n,))],
    )(*gs)


GRAD_PAYLOAD = jnp.bfloat16


def _half_blocks(half_rows, cols):
    if (half_rows // 2) % 16 == 0:
        return (half_rows // 2, cols), (lambda r: (r, 0))
    assert cols % 256 == 0, (half_rows, cols)
    return (half_rows, cols // 2), (lambda r: (0, r))


def _pair_sum(gs, gots, name):
    n = len(gs)
    core = lax.axis_index("c").astype(jnp.int32).reshape(1)

    def body(core_ref, *refs):
        del core_ref
        for w in range(n):
            refs[2 * n + w][...] = (refs[w][...] + refs[n + w][...]).astype(GRAD_PAYLOAD)

    in_specs, out_specs, out_shape, nbytes = [], [], [], 0
    cuts = [_half_blocks(g.shape[1] // 2, g.shape[2]) for g in gs]
    for g, ((br, bc), at) in zip(gs, cuts):
        per_half = (g.shape[1] // 2) // br
        in_specs.append(pl.BlockSpec((1, br, bc), lambda s, r, core, at=at, per_half=per_half:
                                     (s, per_half * core[0] + at(r)[0], at(r)[1])))
        nbytes += 3 * _nbytes((br, bc), F32)
    for g, ((br, bc), at) in zip(gs, cuts):
        in_specs.append(pl.BlockSpec((1, br, bc), lambda s, r, core, at=at: (s,) + at(r)))
        out_specs.append(pl.BlockSpec((1, br, bc), lambda s, r, core, at=at: (s,) + at(r)))
        out_shape.append(jax.ShapeDtypeStruct((g.shape[0], g.shape[1] // 2, g.shape[2]), GRAD_PAYLOAD))
    return pl.pallas_call(
        body, name=name, out_shape=tuple(out_shape),
        grid_spec=pltpu.PrefetchScalarGridSpec(num_scalar_prefetch=1, grid=(N_CHIPS, 2), in_specs=in_specs,
                                               out_specs=tuple(out_specs)),
        compiler_params=_params(("parallel", "parallel"), nbytes),
    )(core, *gs, *gots)


def _chip_sum(ps, landed):
    n = len(ps)
    x, y, c = lax.axis_index("x"), lax.axis_index("y"), lax.axis_index("c")
    where = jnp.stack([2 * x + y, 2 * (1 - x) + y, 2 * x + (1 - y), 2 * (1 - x) + (1 - y), c]).astype(jnp.int32)

    def body(where_ref, *refs):
        del where_ref
        for w in range(n):
            terms = [refs[4 * w + t][...].astype(F32) for t in range(4)]
            refs[4 * n + w][...] = ((terms[0] + terms[1]) + terms[2]) + terms[3]

    in_specs, out_specs, out_shape, args, nbytes = [], [], [], [], 0
    for p, a in zip(ps, landed):
        (br, bc), at = _half_blocks(a.shape[1], a.shape[2])
        blk = (1, br, bc)
        in_specs.append(pl.BlockSpec(blk, lambda r, where, at=at: (where[0],) + at(r)))
        args.append(p)
        for t in (1, 2, 3):
            in_specs.append(pl.BlockSpec(blk, lambda r, where, t=t, at=at: (where[t],) + at(r)))
            args.append(a)
        out_specs.append(pl.BlockSpec(blk, lambda r, where, at=at: (where[4],) + at(r)))
        out_shape.append(jax.ShapeDtypeStruct((2,) + a.shape[1:], F32))
        nbytes += 4 * _nbytes(blk, F32)
    return pl.pallas_call(
        body, name="grad_chip_sum", out_shape=tuple(out_shape),
        grid_spec=pltpu.PrefetchScalarGridSpec(num_scalar_prefetch=1, grid=(2,), in_specs=in_specs,
                                               out_specs=tuple(out_specs)),
        compiler_params=_params(("parallel",), nbytes),
    )(where, *args)


def _join_halves(ss):
    n = len(ss)

    def body(*refs):
        outs, send, recv = refs[n:2 * n], refs[2 * n], refs[2 * n + 1]
        x, y, c, _ = _place()
        cps = []
        for w in range(n):
            cps.append(pltpu.make_async_remote_copy(
                src_ref=outs[w].at[c], dst_ref=outs[w].at[c], send_sem=send.at[w], recv_sem=recv.at[w],
                device_id=(x, y, 1 - c), device_id_type=MESH))
        for cp in cps:
            cp.start()
        for w in range(n):
            got = outs[w].at[1 - c]
            pltpu.make_async_remote_copy(src_ref=got, dst_ref=got, send_sem=send.at[w], recv_sem=recv.at[w],
                                         device_id=(x, y, 1 - c), device_id_type=MESH).wait_recv()
        for cp in cps:
            cp.wait_send()

    dma = lambda k: pltpu.SemaphoreType.DMA((k,))
    return pl.pallas_call(
        body, name="grad_join_halves",
        out_shape=tuple(jax.ShapeDtypeStruct(s.shape, s.dtype) for s in ss),
        in_specs=[_ANY] * n, out_specs=tuple([_ANY] * n), input_output_aliases={w: w for w in range(n)},
        scratch_shapes=[dma(n), dma(n)],
    )(*ss)


def _rot_cols(w, axis=-1):
    a, b = jnp.split(w, 2, axis=axis)
    return jnp.concatenate([-b, a], axis=axis)


def _rot_cols_t(g, axis=-1):
    a, b = jnp.split(g, 2, axis=axis)
    return jnp.concatenate([b, -a], axis=axis)


def _cols_from_chips(a):
    n, r, cs = a.shape
    return jnp.transpose(a, (1, 0, 2)).reshape(r, n * cs)


def _cols_to_chips(a):
    r, cc = a.shape
    return jnp.transpose(a.reshape(r, N_CHIPS, cc // N_CHIPS), (1, 0, 2))


def _conv_w_split(cw):
    return jnp.swapaxes(cw.reshape(3, 2, D_FF), 0, 1)


def _conv_w_join(g):
    return jnp.swapaxes(g, 0, 1).reshape(3, 2 * D_FF)


_SEG =(D_MODEL, 2 * D_MODEL, 2 * D_MODEL + Q_RANK, 2 * D_MODEL + Q_RANK + KV_RANK, 2 * D_MODEL + Q_RANK + KV_RANK + ROPE,
        3 * D_MODEL + Q_RANK + KV_RANK + ROPE)


def _w_in_t_to_pad(wt):
    u, v, cq, ckv, kr, ga, gb = jnp.split(wt, _SEG, axis=0)
    return jnp.concatenate([u, v, ga, gb, cq, ckv, kr, _rot_cols(kr, axis=0)], axis=0)


def _w_in_t_from_pad(gt):
    u, v, ga, gb, cq, ckv, kr, krr = jnp.split(
        gt, (D_MODEL, 2 * D_MODEL, 3 * D_MODEL, 4 * D_MODEL, 4 * D_MODEL + Q_RANK, 4 * D_MODEL + Q_RANK + KV_RANK,
             4 * D_MODEL + Q_RANK + KV_RANK + ROPE), axis=0)
    return jnp.concatenate([u, v, cq, ckv, kr + _rot_cols_t(krr, axis=0), ga, gb], axis=0)


def _w_uq_to_pad(w):
    t = w.reshape(Q_RANK, HEADS, QK_DIM)
    nope, rope = t[..., :NOPE], t[..., NOPE:]
    return jnp.concatenate([nope, rope, _rot_cols(rope)], axis=-1).reshape(Q_RANK, HEADS * HEAD_PAD)


def _w_uq_from_pad(g):
    t = g.reshape(Q_RANK, HEADS, HEAD_PAD)
    nope, rope, rot = t[..., :NOPE], t[..., NOPE:QK_DIM], t[..., QK_DIM:]
    return jnp.concatenate([nope, rope + _rot_cols_t(rot)], axis=-1).reshape(Q_RANK, HEADS * QK_DIM)


def _w_ukv_to_pad(w):
    t = w.reshape(KV_RANK, HEADS, 2, NOPE)
    return jnp.swapaxes(t, 1, 2).reshape(KV_RANK, 2 * HEADS * NOPE)


def _w_ukv_from_pad(g):
    t = g.reshape(KV_RANK, 2, HEADS, NOPE)
    return jnp.swapaxes(t, 1, 2).reshape(KV_RANK, 2 * HEADS * NOPE)


def _rope_tables(positions):
    inv_freq = 1.0 / (ROPE_THETA ** (jnp.arange(0, ROPE, 2, dtype=F32) / ROPE))
    ang = positions.astype(F32).reshape(-1, 1) * inv_freq
    cos, sin = jnp.cos(ang), jnp.sin(ang)
    zero = jnp.zeros((ang.shape[0], 64), F32)
    return jnp.concatenate([cos, cos, zero], axis=1), jnp.concatenate([sin, sin, zero], axis=1)


_BIG = ("w_in", "w_uq", "w_ukv", "w_out", "w_up", "w_down")
UP_SHARD = 2 * D_FF // N_CHIPS


def _local_step(x, positions, tgt, wts, in_weights, mixer_weights, ffn_weights, on_ffn_grads, on_mixer_grads):
    B, S, D = x.shape
    T = B * S
    xf = x.reshape(T, D)
    cos_a, sin_a = _rope_tables(positions)
    bs_t = jnp.pad(wts["a_spatial_b"].T, ((0, 0), (0, 128 - A_GROUPS)))

    h = _rms_fwd(xf, wts["mix_norm"], "norm1_fwd")
    wts = dict(wts)
    wts["w_in"], token = in_weights([h, cos_a, sin_a])
    z = _mm(h, wts["w_in"], "nt", "in_proj", tm=512, tn=1536, tk=D, n_outer=True, after=token)
    wts["w_q"], wts["w_kv"], wts["w_out"] = mixer_weights(z)
    q, k, v, cqn, ckvn = _lat_fwd(z, wts["q_a_norm"], wts["kv_a_norm"], wts["w_q"], wts["w_kv"], cos_a, sin_a)
    yb, *lses = _attn_fwd(q, k, v, B, S)
    merged = _mix_fwd(z, yb, wts["a_v_norm_g"], wts["a_v_norm_b"], wts["a_spatial_w"], bs_t)
    x1 = _mm(merged, wts["w_out"], "nn", "out_proj", tm=512, tn=D, tk=D, add=xf)
    h2 = _rms_fwd(x1, wts["ffn_norm"], "norm2_fwd")
    wts["w_up"], wts["w_down"], wts["conv_w"] = ffn_weights(h2)
    up_pre = _mm(h2, wts["w_up"], "nn", "up_proj", tm=512, tn=UP_SHARD, tk=D, dims=(T, 2 * D_FF, D),
                 b_spec=pl.BlockSpec((None, D, UP_SHARD), lambda i, j, k: (j, 0, 0)),
                 o_spec=pl.BlockSpec((None, 512, UP_SHARD), lambda i, j, k: (j // 2, i, j % 2)), out_shape=(2, T, D_FF),
                 n_outer=True)
    act, up_conv = _gate_fwd(up_pre, wts["conv_w"], wts["conv_b"], B, S)
    x2 = _mm(act, wts["w_down"], "nn", "down_proj", tm=512, tn=D, tk=1408, add=x1)
    dx2, loss_row, g_final = _final(x2, tgt.reshape(T, D), wts["final_norm"])

    g = {"final_norm": g_final}
    dact = _mm(dx2, wts["w_down"], "nt", "down_proj_dx", tm=512, tn=1408, tk=D, n_outer=True)
    tk2, tk1 = min(2048, T), min(1024, T)
    g["w_down"], g["w_down_lo"] = _mm(act, dx2, "tn", "down_proj_dw", tm=1408, tn=D, tk=tk1, copy_dtype=GRAD_PAYLOAD)
    dup, g["conv_w"], g["conv_b"] = _gate_bwd(up_pre, up_conv, dact, wts["conv_w"], B, S)
    g["w_up"], g["w_up_lo"] = _mm(
        h2, dup, "tn", "up_proj_dw", tm=D, tn=UP_SHARD, tk=tk2, dims=(D, 2 * D_FF, T), copy_dtype=GRAD_PAYLOAD,
        b_spec=pl.BlockSpec((None, tk2, UP_SHARD), lambda i, j, k: (j // 2, k, j % 2)),
        o_spec=pl.BlockSpec((None, D, UP_SHARD), lambda i, j, k: (j, 0, 0)), out_shape=(N_CHIPS, D, UP_SHARD))
    token, ffn_sent = on_ffn_grads(g)
    dh2 = _mm(dup, wts["w_up"], "nt", "up_proj_dx", tm=512, tn=D, tk=UP_SHARD, dims=(T, D, 2 * D_FF), after=token,
              a_spec=pl.BlockSpec((None, 512, UP_SHARD), lambda i, j, k: (k // 2, i, k % 2)),
              b_spec=pl.BlockSpec((None, D, UP_SHARD), lambda i, j, k: (k, 0, 0)))
    token = ffn_sent(dh2)
    dx1, g["ffn_norm"] = _rms_bwd(x1, wts["ffn_norm"], dh2, dx2, "norm2_bwd")
    dm = _mm(dx1, wts["w_out"], "nt", "out_proj_dx", tm=512, tn=D, tk=D, after=token)
    g["w_out"], g["w_out_lo"] = _mm(merged, dx1, "tn", "out_proj_dw", tm=D, tn=D, tk=tk1, copy_dtype=GRAD_PAYLOAD)
    dz, dyb, dl, g["a_spatial_w"], gbs, g["a_v_norm_g"], g["a_v_norm_b"] = _mix_bwd(
        z, yb, dm, wts["a_v_norm_g"], wts["a_v_norm_b"], wts["a_spatial_w"], bs_t)
    g["a_spatial_b"] = gbs[:, :A_GROUPS].T
    delta = dl.reshape(HEADS * T // ATT_BLOCK, 1, ATT_BLOCK)
    dq, dk, dv = _attn_bwd(q, k, v, dyb, lses, delta, B, S)
    dz, dq_raw, dkv, g["q_a_norm"], g["kv_a_norm"] = _lat_bwd(
        dz, z, dq, dk, dv, wts["q_a_norm"], wts["kv_a_norm"], wts["w_q"], wts["w_kv"], cos_a, sin_a)
    g["w_q"] = _mm(cqn, dq_raw, "tn", "q_proj_dw", tm=Q_RANK, tn=HEADS * HEAD_PAD, tk=tk2)
    g["w_kv"] = _mm(ckvn, dkv, "tn", "kv_proj_dw", tm=KV_RANK, tn=2 * HEADS * NOPE, tk=tk2)
    g["w_in"] = _mm(dz, h, "tn", "in_proj_dw", tm=1536, tn=D, tk=tk2)
    token = on_mixer_grads(g)
    dh = _mm(dz, wts["w_in"], "nn", "in_proj_dx", tm=512, tn=D, tk=1536, after=token)
    dx, g["mix_norm"] = _rms_bwd(xf, wts["mix_norm"], dh, dx1, "norm1_bwd")
    return loss_row[0, 0], dx.reshape(B, S, D), g


_SMALL = (("mix_norm", (1, D_MODEL)), ("a_v_norm_g", (1, D_MODEL)), ("a_v_norm_b", (1, D_MODEL)),
          ("a_spatial_w", (A_GROUPS * CHUNK, CHUNK)), ("a_spatial_b", (1, A_GROUPS * CHUNK)), ("q_a_norm", (1, Q_RANK)),
          ("kv_a_norm", (1, KV_RANK)), ("ffn_norm", (1, D_MODEL)), ("conv_b", (1, 2 * D_FF)), ("final_norm", (1, D_MODEL)),
          ("conv_w", (3, 2 * D_FF)))
_SMALL_SIZE = sum(math.prod(s) for _, s in _SMALL)
_SMALL_ROWS = -(-(_SMALL_SIZE + 1) // (128 * 8)) * 8


def kernel(x, positions, mix_norm, w_in, a_v_norm_g, a_v_norm_b, a_spatial_w, a_spatial_b, q_a_norm, w_uq, kv_a_norm, w_ukv, w_out, ffn_norm, w_up, conv_w, conv_b, w_down, final_norm, loss_target, m_mix_norm, m_w_in, m_a_v_norm_g, m_a_v_norm_b, m_a_spatial_w, m_a_spatial_b, m_q_a_norm, m_w_uq, m_kv_a_norm, m_w_ukv, m_w_out, m_ffn_norm, m_w_up, m_conv_w, m_conv_b, m_w_down, m_final_norm, v_mix_norm, v_w_in, v_a_v_norm_g, v_a_v_norm_b, v_a_spatial_w, v_a_spatial_b, v_q_a_norm, v_w_uq, v_kv_a_norm, v_w_ukv, v_w_out, v_ffn_norm, v_w_up, v_conv_w, v_conv_b, v_w_down, v_final_norm):
    weights = dict(mix_norm=mix_norm, w_in=w_in, a_v_norm_g=a_v_norm_g, a_v_norm_b=a_v_norm_b, a_spatial_w=a_spatial_w,
                   a_spatial_b=a_spatial_b, q_a_norm=q_a_norm, w_uq=w_uq, kv_a_norm=kv_a_norm, w_ukv=w_ukv, w_out=w_out,
                   ffn_norm=ffn_norm, w_up=w_up, conv_w=conv_w, conv_b=conv_b, w_down=w_down, final_norm=final_norm)
    m_in = dict(mix_norm=m_mix_norm, w_in=m_w_in, a_v_norm_g=m_a_v_norm_g, a_v_norm_b=m_a_v_norm_b,
                a_spatial_w=m_a_spatial_w, a_spatial_b=m_a_spatial_b, q_a_norm=m_q_a_norm, w_uq=m_w_uq,
                kv_a_norm=m_kv_a_norm, w_ukv=m_w_ukv, w_out=m_w_out, ffn_norm=m_ffn_norm, w_up=m_w_up, conv_w=m_conv_w,
                conv_b=m_conv_b, w_down=m_w_down, final_norm=m_final_norm)
    v_in = dict(mix_norm=v_mix_norm, w_in=v_w_in, a_v_norm_g=v_a_v_norm_g, a_v_norm_b=v_a_v_norm_b,
                a_spatial_w=v_a_spatial_w, a_spatial_b=v_a_spatial_b, q_a_norm=v_q_a_norm, w_uq=v_w_uq,
                kv_a_norm=v_kv_a_norm, w_ukv=v_w_ukv, w_out=v_w_out, ffn_norm=v_ffn_norm, w_up=v_w_up, conv_w=v_conv_w,
                conv_b=v_conv_b, w_down=v_w_down, final_norm=v_final_norm)
    names = list(weights)
    chip = 2 * lax.axis_index("x") + lax.axis_index("y")

    def halves(a):
        return a.reshape(a.shape[:-2] + (2, a.shape[-2] // 2, a.shape[-1]))

    w_in_t = jnp.swapaxes(w_in[0], 0, 1).astype(MXU_DTYPE)
    w_in_gather = _exchange_start([jnp.stack(jnp.split(w_in_t, 2, axis=1))], "w_in_gather_start", "halves",
                                  after=positions)
    gathers = {}
    wts = dict(
        mix_norm=mix_norm, a_v_norm_g=a_v_norm_g, a_v_norm_b=a_v_norm_b, a_spatial_w=a_spatial_w[0],
        a_spatial_b=a_spatial_b[0], q_a_norm=q_a_norm, kv_a_norm=kv_a_norm, ffn_norm=ffn_norm,
        final_norm=final_norm.reshape(1, D_MODEL), conv_b=conv_b.reshape(2, 1, D_FF))

    mixer_shards = [weights[n][0].astype(MXU_DTYPE) for n in _BIG[1:4]]
    ffn_shards = [w_up[0].astype(MXU_DTYPE), w_down[0].astype(MXU_DTYPE)]

    def in_weights(after):
        _, landed = _exchange_wait(w_in_gather, "w_in_gather_wait", "halves", list(after) + mixer_shards + ffn_shards)
        (w_in_sh,) = _forward_halves(list(landed))
        gathers["mixer"] = _exchange_start(mixer_shards, "mixer_gather_start", "gather", after=w_in_sh)
        gathers["ffn"] = _exchange_start(ffn_shards + [conv_w[0]], "ffn_gather_start", "gather",
                                         after=gathers["mixer"][4])
        w_in_pad = _w_in_t_to_pad(jnp.concatenate([w_in_sh[:, 0], w_in_sh[:, 1]], axis=-1).reshape(-1, D_MODEL))
        return w_in_pad, gathers["ffn"][4]

    def mixer_weights(after):
        _, (w_uq_sh, w_ukv_sh, w_out_sh) = _exchange_wait(gathers["mixer"], "mixer_gather_wait", "gather", after)
        return (_w_uq_to_pad(_cols_from_chips(w_uq_sh)), _w_ukv_to_pad(_cols_from_chips(w_ukv_sh)),
                w_out_sh.reshape(D_MODEL, D_MODEL))

    def ffn_weights(after):
        _, (w_up_sh, w_down_sh, cw_all) = _exchange_wait(gathers["ffn"], "ffn_gather_wait", "gather", after)
        return w_up_sh, w_down_sh.reshape(D_FF, D_MODEL), _conv_w_split(_cols_from_chips(cw_all))

    scatters = {}

    def start_scatter(slabs, slabs_lo, tag):
        got = _swap_halves([halves(s) for s in slabs_lo], tag + "_grad_swap_halves")
        sums = _pair_sum(slabs, got, tag + "_grad_pair_sum")
        scatters[tag] = _exchange_start(list(sums), tag + "_scatter_start", "scatter", after=slabs[-1])
        return scatters[tag][4]

    def on_ffn_grads(g):
        slabs, slabs_lo = [[g["w_up" + lo], g["w_down" + lo].reshape(N_CHIPS, D_FF // N_CHIPS, D_MODEL)]
                           for lo in ("", "_lo")]
        swap = _exchange_start([halves(s) for s in slabs_lo], "ffn_swap_start", "swap", after=slabs[1])

        def sent(after):
            _, got = _exchange_wait(swap, "ffn_swap_wait", "swap", after)
            sums = _pair_sum(slabs, got, "ffn_grad_pair_sum")
            scatters["ffn"] = _exchange_start(list(sums), "ffn_scatter_start", "scatter", after=got[0])
            return scatters["ffn"][4]

        return swap[4], sent

    def on_mixer_grads(g):
        slabs = [_w_in_t_from_pad(g["w_in"]).reshape(N_CHIPS, -1, D_MODEL), _cols_to_chips(_w_uq_from_pad(g["w_q"])),
                 _cols_to_chips(_w_ukv_from_pad(g["w_kv"]))]
        w_out_slabs = [g["w_out" + lo].reshape(N_CHIPS, D_MODEL // N_CHIPS, D_MODEL) for lo in ("", "_lo")]
        return start_scatter(slabs + w_out_slabs[:1], [s.astype(GRAD_PAYLOAD) for s in slabs] + w_out_slabs[1:], "mixer")

    loss_part, grad_x, g = _local_step(x, positions, loss_target, wts, in_weights, mixer_weights, ffn_weights,
                                       on_ffn_grads, on_mixer_grads)

    g_small_parts = dict(g)
    g_small_parts["conv_w"] = _conv_w_join(g["conv_w"])
    g_small_parts["conv_b"] = g["conv_b"].reshape(1, 2 * D_FF)
    flat = jnp.concatenate([g_small_parts[n].reshape(-1) for n, _ in _SMALL] + [loss_part.reshape(1)])
    flat = jnp.pad(flat, (0, _SMALL_ROWS * 128 - flat.shape[0])).reshape(_SMALL_ROWS, 128)
    small_gather = _exchange_start([flat], "small_gather_start", "all", after=grad_x)

    mixer_sums, mixer_landed = _exchange_wait(scatters["mixer"], "mixer_scatter_wait", "scatter", after=small_gather[4])
    ffn_sums, ffn_landed = _exchange_wait(scatters["ffn"], "ffn_scatter_wait", "scatter", after=mixer_landed[0])
    reduced = _chip_sum(list(mixer_sums) + list(ffn_sums), list(mixer_landed) + list(ffn_landed))
    g_big = dict(zip(_BIG, _join_halves(reduced)))

    grads, deltas, new_m, new_v = {}, {}, {}, {}

    def update(n, grad):
        w = weights[n]
        shape2 = grad.shape
        d, nm, nv = _adamw(w.reshape(shape2), grad, m_in[n].reshape(shape2), v_in[n].reshape(shape2), "adamw_" + n)
        grads[n], deltas[n], new_m[n], new_v[n] = (t.reshape(w.shape) for t in (grad, d, nm, nv))

    def update_transposed(n, grad_t):
        t = lambda a: jnp.swapaxes(a, 1, 2)
        d, nm, nv = _adamw(t(weights[n]), grad_t, t(m_in[n]), t(v_in[n]), "adamw_" + n)
        grads[n], deltas[n], new_m[n], new_v[n] = t(grad_t), t(d), t(nm), t(nv)

    for n in _BIG:
        g3 = g_big[n].reshape((1, -1, g_big[n].shape[-1]))
        if n == "w_in":
            update_transposed(n, g3)
        else:
            update(n, g3)

    (own,), (everyone,) = _exchange_wait(small_gather, "small_gather_wait", "all", after=[deltas[n] for n in _BIG])
    device = 2 * chip + lax.axis_index("c")
    everyone = lax.dynamic_update_slice(everyone, own[None], (device, 0, 0))
    total = _sum_slabs([everyone[j] for j in range(8)], "small_grads_sum", tr=_SMALL_ROWS).reshape(-1)
    o = 0
    for n, shp in _SMALL:
        piece = total[o:o + math.prod(shp)].reshape(shp)
        o += math.prod(shp)
        if n == "conv_w":
            piece = lax.dynamic_slice_in_dim(piece, chip * UP_SHARD, UP_SHARD, axis=1)
        update(n, piece)
    loss = total[_SMALL_SIZE]
    return (loss, grad_x, *[grads[n] for n in names], *[deltas[n] for n in names], *[new_m[n] for n in names],
            *[new_v[n] for n in names])
```

```python
import functools
import math

import jax
import jax.numpy as jnp
from jax import lax
from jax.experimental import pallas as pl
from jax.experimental.pallas import tpu as pltpu

F32 = jnp.float32
MXU_DTYPE = jnp.bfloat16
MESH = pl.DeviceIdType.MESH

D_MODEL = 1024
EPS = 1e-6
A_GROUPS = 8
CHUNK = 128
HEADS = 8
NOPE = 128
ROPE = 64
QK_DIM = NOPE + ROPE
HEAD_PAD = 256
Q_RANK = 256
KV_RANK = 128
ROPE_THETA = 10000.0
D_FF = 2816
FF_TILE = 256
N_FF_TILES = D_FF // FF_TILE
LAT = 512
IN_PAD = 4 * D_MODEL + LAT
N_CHIPS = 4
ADAM_LR, ADAM_B1, ADAM_B2, ADAM_EPS, ADAM_WD, ADAM_STEP = 0.001, 0.9, 0.999, 1e-08, 0.01, 10

VMEM_CAP_V7X = 64 * 1024 * 1024
NEG = -1e30


def _params(sem, nbytes):
    limit = int(min(VMEM_CAP_V7X - (8 << 20), max(32 << 20, 3 * nbytes)))
    return pltpu.CompilerParams(dimension_semantics=sem, vmem_limit_bytes=limit)


def _nbytes(shape, dtype):
    return math.prod(shape) * jnp.dtype(dtype).itemsize


_DIMS = {"nn": (((1,), (0,)), ((), ())), "nt": (((1,), (1,)), ((), ())), "tn": (((0,), (0,)), ((), ()))}


def _mm(a, b, mode, name, *, tm, tn, tk, out_dtype=F32, add=None, dims=None, a_spec=None, b_spec=None,
        o_spec=None, out_shape=None, n_outer=False, copy_dtype=None, after=None):
    if dims is None:
        if mode == "nn":
            (M, K), (_, N) = a.shape, b.shape
        elif mode == "nt":
            (M, K), (N, _) = a.shape, b.shape
        else:
            (K, M), (_, N) = a.shape, b.shape
    else:
        M, N, K = dims
    a_blk = (tk, tm) if mode == "tn" else (tm, tk)
    b_blk = (tn, tk) if mode == "nt" else (tk, tn)
    if a_spec is None:
        a_spec = pl.BlockSpec(a_blk, (lambda i, j, k: (k, i)) if mode == "tn" else (lambda i, j, k: (i, k)))
    if b_spec is None:
        b_spec = pl.BlockSpec(b_blk, (lambda i, j, k: (j, k)) if mode == "nt" else (lambda i, j, k: (k, j)))
    if o_spec is None:
        o_spec = pl.BlockSpec((tm, tn), lambda i, j, k: (i, j))
    if out_shape is None:
        out_shape = (M, N)
    assert M % tm == 0 and N % tn == 0 and K % tk == 0, (name, M, N, K, tm, tn, tk)
    nk = K // tk
    contract = _DIMS[mode]
    has_add = add is not None

    def body(*refs):
        a_ref, b_ref = refs[0], refs[1]
        add_ref = refs[2] if has_add else None
        n_in = 2 + has_add + (after is not None)
        o_ref = refs[n_in]
        copy_ref = refs[n_in + 1] if copy_dtype is not None else None

        def product():
            return lax.dot_general(a_ref[...].astype(MXU_DTYPE), b_ref[...].astype(MXU_DTYPE), contract,
                                   preferred_element_type=F32)

        def finish(r):
            if has_add:
                r = r + add_ref[...]
            o_ref[...] = r.astype(out_dtype)
            if copy_ref is not None:
                copy_ref[...] = r.astype(copy_dtype)

        if nk == 1:
            finish(product())
            return
        acc = refs[-1]
        k = pl.program_id(2)

        @pl.when(k == 0)
        def _():
            acc[...] = jnp.zeros_like(acc)

        acc[...] += product()

        @pl.when(k == nk - 1)
        def _():
            finish(acc[...])

    in_specs = [a_spec, b_spec]
    args = [a, b]
    nbytes = _nbytes(a_blk, a.dtype) + _nbytes(b_blk, b.dtype) + 3 * _nbytes((tm, tn), F32)
    if has_add:
        in_specs.append(pl.BlockSpec((tm, tn), lambda i, j, k: (i, j)))
        args.append(add)
        nbytes += _nbytes((tm, tn), F32)
    if after is not None:
        in_specs.append(pl.BlockSpec(after.shape, lambda i, j, k: (0, 0)))
        args.append(after)
    grid = (M // tm, N // tn, nk)
    if n_outer:
        def swapped(spec):
            return pl.BlockSpec(spec.block_shape, lambda j, i, k, at=spec.index_map: at(i, j, k))

        grid = (N // tn, M // tm, nk)
        in_specs = [swapped(s) for s in in_specs]
        o_spec = swapped(o_spec)
    out_sds, out_specs = jax.ShapeDtypeStruct(out_shape, out_dtype), o_spec
    if copy_dtype is not None:
        out_sds, out_specs = (out_sds, jax.ShapeDtypeStruct(out_shape, copy_dtype)), (o_spec, o_spec)
    return pl.pallas_call(
        body, name=name, out_shape=out_sds, grid=grid, in_specs=in_specs, out_specs=out_specs,
        scratch_shapes=[pltpu.VMEM((tm, tn), F32)] if nk > 1 else [],
        compiler_params=_params(("parallel", "parallel", "arbitrary"), nbytes),
    )(*args)


_GELU_C = math.sqrt(2.0 / math.pi)
_GELU_A = 0.044715


def _sigmoid(x):
    return 0.5 * jnp.tanh(0.5 * x) + 0.5


def _gelu(x):
    t = jnp.tanh(x * (_GELU_C + (_GELU_C * _GELU_A) * (x * x)))
    return x * (0.5 + 0.5 * t)


def _gelu_and_grad(x):
    x2 = x * x
    t = jnp.tanh(x * (_GELU_C + (_GELU_C * _GELU_A) * x2))
    cdf = 0.5 + 0.5 * t
    grad = cdf + (0.5 * x) * (1.0 - t * t) * (_GELU_C + (3.0 * _GELU_C * _GELU_A) * x2)
    return x * cdf, grad


def _rope_mix(g, cos_a, sin_a):
    return g * cos_a + pltpu.roll(g, 64, 1) * sin_a


def _rope_mix_bwd(d, cos_a, sin_a):
    return d * cos_a + pltpu.roll(d * sin_a, 64, 1)


def _rms_fwd(x, g, name, tr=512):
    T, D = x.shape

    def body(x_ref, g_ref, h_ref):
        xv = x_ref[...]
        r = lax.rsqrt(jnp.mean(xv * xv, axis=-1, keepdims=True) + EPS)
        h_ref[...] = ((xv * r) * g_ref[...]).astype(h_ref.dtype)

    return pl.pallas_call(
        body, name=name, out_shape=jax.ShapeDtypeStruct((T, D), MXU_DTYPE), grid=(T // tr,),
        in_specs=[pl.BlockSpec((tr, D), lambda i: (i, 0)), pl.BlockSpec((1, D), lambda i: (0, 0))],
        out_specs=pl.BlockSpec((tr, D), lambda i: (i, 0)),
        compiler_params=_params(("parallel",), 3 * _nbytes((tr, D), F32)),
    )(x, g)


def _rms_bwd(x, g, dh, dres, name, tr=512):
    T, D = x.shape

    def body(x_ref, g_ref, dh_ref, dres_ref, dx_ref, gg_ref):
        @pl.when(pl.program_id(0) == 0)
        def _():
            gg_ref[...] = jnp.zeros_like(gg_ref)

        xv = x_ref[...]
        r = lax.rsqrt(jnp.mean(xv * xv, axis=-1, keepdims=True) + EPS)
        xn = xv * r
        dhv = dh_ref[...]
        dxn = dhv * g_ref[...]
        dx_ref[...] = dres_ref[...] + r * (dxn - xn * jnp.mean(dxn * xn, axis=-1, keepdims=True))
        gg_ref[...] += jnp.sum(dhv * xn, axis=0, keepdims=True)

    row = pl.BlockSpec((tr, D), lambda i: (i, 0))
    vec = pl.BlockSpec((1, D), lambda i: (0, 0))
    return pl.pallas_call(
        body, name=name,
        out_shape=(jax.ShapeDtypeStruct((T, D), F32), jax.ShapeDtypeStruct((1, D), F32)),
        grid=(T // tr,), in_specs=[row, vec, row, row], out_specs=(row, vec),
        compiler_params=_params(("arbitrary",), 6 * _nbytes((tr, D), F32)),
    )(x, g, dh, dres)


def _lat_fwd(z, gq, gkv, wq, wkv, cos_a, sin_a, tr=256):
    T = z.shape[0]
    lat_blk = (4 * D_MODEL) // LAT

    def body(z_ref, gq_ref, gkv_ref, wq_ref, wkv_ref, cos_ref, sin_ref, q_ref, k_ref, v_ref, cqn_ref, ckvn_ref):
        zl = z_ref[...]
        cos_v, sin_v = cos_ref[...], sin_ref[...]
        cq = zl[:, :Q_RANK]
        ckv = zl[:, Q_RANK:Q_RANK + KV_RANK]
        krb = zl[:, Q_RANK + KV_RANK:]
        cqn = ((cq * lax.rsqrt(jnp.mean(cq * cq, axis=-1, keepdims=True) + EPS)) * gq_ref[...]).astype(MXU_DTYPE)
        ckvn = ((ckv * lax.rsqrt(jnp.mean(ckv * ckv, axis=-1, keepdims=True) + EPS)) * gkv_ref[...]).astype(MXU_DTYPE)
        cqn_ref[...] = cqn
        ckvn_ref[...] = ckvn
        krr = _rope_mix(krb, cos_v, sin_v).astype(MXU_DTYPE)
        q = jnp.dot(cqn, wq_ref[...], preferred_element_type=F32)
        kv = jnp.dot(ckvn, wkv_ref[...], preferred_element_type=F32)
        for h in range(HEADS):
            o = h * HEAD_PAD
            q_ref[:, o:o + NOPE] = q[:, o:o + NOPE].astype(MXU_DTYPE)
            q_ref[:, o + NOPE:o + HEAD_PAD] = _rope_mix(q[:, o + NOPE:o + HEAD_PAD], cos_v, sin_v).astype(MXU_DTYPE)
            k_ref[:, o:o + NOPE] = kv[:, h * NOPE:(h + 1) * NOPE].astype(MXU_DTYPE)
            k_ref[:, o + NOPE:o + HEAD_PAD] = krr
        v_ref[...] = kv[:, HEADS * NOPE:].astype(MXU_DTYPE)

    def row(w):
        return pl.BlockSpec((tr, w), lambda i: (i, 0))

    def full(a):
        return pl.BlockSpec(a.shape, lambda i: (0, 0))

    return pl.pallas_call(
        body, name="lat_fwd",
        out_shape=(jax.ShapeDtypeStruct((T, HEADS * HEAD_PAD), MXU_DTYPE), jax.ShapeDtypeStruct((T, HEADS * HEAD_PAD), MXU_DTYPE),
                   jax.ShapeDtypeStruct((T, HEADS * NOPE), MXU_DTYPE), jax.ShapeDtypeStruct((T, Q_RANK), MXU_DTYPE),
                   jax.ShapeDtypeStruct((T, KV_RANK), MXU_DTYPE)),
        grid=(T // tr,),
        in_specs=[pl.BlockSpec((tr, LAT), lambda i: (i, lat_blk)), full(gq), full(gkv), full(wq), full(wkv), row(128), row(128)],
        out_specs=(row(HEADS * HEAD_PAD), row(HEADS * HEAD_PAD), row(HEADS * NOPE), row(Q_RANK), row(KV_RANK)),
        compiler_params=_params(("parallel",), 8 * _nbytes((tr, HEADS * HEAD_PAD), F32)),
    )(z, gq, gkv, wq, wkv, cos_a, sin_a)


ATT_BLOCK = 256
_SCALE = QK_DIM ** -0.5


def _causal_mask(n):
    return lax.broadcasted_iota(jnp.int32, (n, n), 1) <= lax.broadcasted_iota(jnp.int32, (n, n), 0)


def _causal_mask_t(n):
    return lax.broadcasted_iota(jnp.int32, (n, n), 0) <= lax.broadcasted_iota(jnp.int32, (n, n), 1)


ATT_HEADS = 4


def _attn_fwd(q, k, v, B, S):
    tq = ATT_BLOCK
    nq = S // tq
    T = B * S
    hp, groups = ATT_HEADS, HEADS // ATT_HEADS

    def body(q_ref, k_ref, v_ref, o_ref, *lse_refs):
        qi = pl.program_id(2)
        qs = [q_ref[:, t * HEAD_PAD:(t + 1) * HEAD_PAD] for t in range(hp)]

        def scores(j, t):
            rows = pl.ds(pl.multiple_of(j * tq, tq), tq)
            return lax.dot_general(k_ref[rows, t * HEAD_PAD:(t + 1) * HEAD_PAD], qs[t], _DIMS["nt"],
                                   preferred_element_type=F32)

        def step(j, carry, last):
            rows = pl.ds(pl.multiple_of(j * tq, tq), tq)
            out = []
            for t in range(hp):
                m, l, acc, st = carry[t]
                st_next = st if last else scores(j + 1, t)
                st = st * _SCALE
                if last:
                    st = jnp.where(_causal_mask_t(tq), st, NEG)
                m_new = jnp.maximum(m, jnp.max(st, axis=0, keepdims=True))
                alpha = jnp.exp(m - m_new)
                p = jnp.exp(st - m_new)
                l = alpha * l + jnp.sum(p, axis=0, keepdims=True)
                acc = alpha * acc + lax.dot_general(v_ref[rows, t * NOPE:(t + 1) * NOPE], p.astype(MXU_DTYPE),
                                                    _DIMS["tn"], preferred_element_type=F32)
                out.append((m_new, l, acc, st_next))
            return tuple(out)

        init = tuple((jnp.full((1, tq), NEG, F32), jnp.zeros((1, tq), F32), jnp.zeros((NOPE, tq), F32), scores(0, t))
                     for t in range(hp))
        carry = lax.fori_loop(0, qi, lambda j, c: step(j, c, False), init)
        carry = step(qi, carry, True)
        for t in range(hp):
            m, l, acc, _ = carry[t]
            o_ref[:, t * NOPE:(t + 1) * NOPE] = (acc / l).T
            lse_refs[t][0] = m + jnp.log(l)

    lse_sds = jax.ShapeDtypeStruct((groups * B * nq, 1, tq), F32)
    lse_spec = pl.BlockSpec((1, 1, tq), lambda b, h, i: ((h * B + b) * nq + i, 0, 0))
    return pl.pallas_call(
        body, name="attn_fwd",
        out_shape=(jax.ShapeDtypeStruct((T, HEADS * NOPE), F32),) + (lse_sds,) * hp,
        grid=(B, groups, nq),
        in_specs=[pl.BlockSpec((tq, hp * HEAD_PAD), lambda b, h, i: (b * nq + i, h)),
                  pl.BlockSpec((S, hp * HEAD_PAD), lambda b, h, i: (b, h)),
                  pl.BlockSpec((S, hp * NOPE), lambda b, h, i: (b, h))],
        out_specs=(pl.BlockSpec((tq, hp * NOPE), lambda b, h, i: (b * nq + i, h)),) + (lse_spec,) * hp,
        compiler_params=_params(("parallel", "parallel", "arbitrary"), 4 * hp * _nbytes((S, HEAD_PAD), MXU_DTYPE)),
    )(q, k, v)


def _attn_bwd(q, k, v, do, lses, delta, B, S):
    tq = ATT_BLOCK
    nq = S // tq
    T = B * S
    hp, groups = ATT_HEADS, HEADS // ATT_HEADS

    def body(q_ref, k_ref, v_ref, do_ref, *refs):
        lse_refs, dl_refs = refs[:hp], refs[hp:2 * hp]
        dq_out, dk_ref, dv_ref, dq_ref = refs[2 * hp:]
        kj = pl.program_id(2)

        @pl.when(kj == 0)
        def _():
            dq_ref[...] = jnp.zeros_like(dq_ref)

        def products(i, t):
            rows = pl.ds(pl.multiple_of(i * tq, tq), tq)
            st = lax.dot_general(k_ref[:, t * HEAD_PAD:(t + 1) * HEAD_PAD], q_ref[rows, t * HEAD_PAD:(t + 1) * HEAD_PAD],
                                 _DIMS["nt"], preferred_element_type=F32)
            dpt = lax.dot_general(v_ref[:, t * NOPE:(t + 1) * NOPE], do_ref[rows, t * NOPE:(t + 1) * NOPE],
                                  _DIMS["nt"], preferred_element_type=F32)
            return st, dpt

        def step(i, carry, masked):
            rows = pl.ds(pl.multiple_of(i * tq, tq), tq)
            nxt = jnp.minimum(i + 1, nq - 1)
            out = []
            for t in range(hp):
                dk, dv, st, dpt = carry[t]
                st_next, dpt_next = products(nxt, t)
                qk_cols = slice(t * HEAD_PAD, (t + 1) * HEAD_PAD)
                v_cols = slice(t * NOPE, (t + 1) * NOPE)
                p = jnp.exp(st * _SCALE - lse_refs[t][i])
                if masked:
                    p = jnp.where(_causal_mask_t(tq), p, 0.0)
                dv = dv + jnp.dot(p.astype(MXU_DTYPE), do_ref[rows, v_cols], preferred_element_type=F32)
                ds = (p * (dpt - dl_refs[t][i]) * _SCALE).astype(MXU_DTYPE)
                dk = dk + jnp.dot(ds, q_ref[rows, qk_cols], preferred_element_type=F32)
                dq_ref[rows, qk_cols] += lax.dot_general(ds, k_ref[:, qk_cols], _DIMS["tn"], preferred_element_type=F32)
                out.append((dk, dv, st_next, dpt_next))
            return tuple(out)

        init = tuple((jnp.zeros((tq, HEAD_PAD), F32), jnp.zeros((tq, NOPE), F32)) + products(kj, t) for t in range(hp))
        carry = step(kj, init, True)
        carry = lax.fori_loop(kj + 1, nq, lambda i, c: step(i, c, False), carry)
        for t in range(hp):
            dk_ref[:, t * HEAD_PAD:(t + 1) * HEAD_PAD] = carry[t][0].astype(dk_ref.dtype)
            dv_ref[:, t * NOPE:(t + 1) * NOPE] = carry[t][1].astype(dv_ref.dtype)

        @pl.when(kj == nq - 1)
        def _():
            dq_out[...] = dq_ref[...].astype(dq_out.dtype)

    seq = lambda w: pl.BlockSpec((S, w), lambda b, h, j: (b, h))
    blk = lambda w: pl.BlockSpec((tq, w), lambda b, h, j: (b * nq + j, h))
    lse_spec = pl.BlockSpec((nq, 1, tq), lambda b, h, j: (h * B + b, 0, 0))
    dl_specs = [pl.BlockSpec((nq, 1, tq), lambda b, h, j, t=t: ((h * hp + t) * B + b, 0, 0)) for t in range(hp)]
    return pl.pallas_call(
        body, name="attn_bwd",
        out_shape=(jax.ShapeDtypeStruct((T, HEADS * HEAD_PAD), MXU_DTYPE), jax.ShapeDtypeStruct((T, HEADS * HEAD_PAD), MXU_DTYPE),
                   jax.ShapeDtypeStruct((T, HEADS * NOPE), MXU_DTYPE)),
        grid=(B, groups, nq),
        in_specs=[seq(hp * HEAD_PAD), blk(hp * HEAD_PAD), blk(hp * NOPE), seq(hp * NOPE)] + [lse_spec] * hp + dl_specs,
        out_specs=(seq(hp * HEAD_PAD), blk(hp * HEAD_PAD), blk(hp * NOPE)),
        scratch_shapes=[pltpu.VMEM((S, hp * HEAD_PAD), F32)],
        compiler_params=_params(("parallel", "parallel", "arbitrary"), 8 * hp * _nbytes((S, HEAD_PAD), F32)),
    )(q, k, v, do, *lses, *([delta] * hp))


MIX_ROWS = 256


def _tril_weights(ws_ref, g):
    return jnp.where(_causal_mask(CHUNK), ws_ref[g], 0.0).astype(MXU_DTYPE)


def _layer_norm_stats(va):
    mu = jnp.mean(va, axis=-1, keepdims=True)
    xc = va - mu
    rs = lax.rsqrt(jnp.mean(xc * xc, axis=-1, keepdims=True) + EPS)
    return xc * rs


def _mix_specs(tr):
    zcol = lambda c: pl.BlockSpec((tr, D_MODEL), lambda i, c=c: (i, c))
    row = pl.BlockSpec((tr, D_MODEL), lambda i: (i, 0))
    vec = pl.BlockSpec((1, D_MODEL), lambda i: (0, 0))
    ws = pl.BlockSpec((A_GROUPS, CHUNK, CHUNK), lambda i: (0, 0, 0))
    bs = pl.BlockSpec((CHUNK, 128), lambda i: (0, 0))
    return zcol, row, vec, ws, bs


def _mix_fwd(z, yb, ln_g, ln_b, ws, bs_t):
    T = z.shape[0]
    tr = MIX_ROWS
    zcol, row, vec, ws_spec, bs_spec = _mix_specs(tr)

    def body(zu_ref, zv_ref, zga_ref, zgb_ref, yb_ref, g_ref, b_ref, ws_ref, bs_ref, out_ref, vn_s):
        vhat = _layer_norm_stats(_gelu(zv_ref[...]))
        vn_s[...] = (vhat * g_ref[...] + b_ref[...]).astype(MXU_DTYPE)
        for g in range(A_GROUPS):
            w = _tril_weights(ws_ref, g)
            bias = bs_ref[:, g:g + 1]
            cols = slice(g * CHUNK, (g + 1) * CHUNK)
            for c in range(tr // CHUNK):
                rows = slice(c * CHUNK, (c + 1) * CHUNK)
                mixed = jnp.dot(w, vn_s[rows, cols], preferred_element_type=F32) + bias
                ya = _gelu(zu_ref[rows, cols]) * mixed
                merged = _sigmoid(zga_ref[rows, cols]) * ya + _sigmoid(zgb_ref[rows, cols]) * yb_ref[rows, cols]
                out_ref[rows, cols] = merged.astype(MXU_DTYPE)

    return pl.pallas_call(
        body, name="mix_fwd", out_shape=jax.ShapeDtypeStruct((T, D_MODEL), MXU_DTYPE), grid=(T // tr,),
        in_specs=[zcol(0), zcol(1), zcol(2), zcol(3), row, vec, vec, ws_spec, bs_spec], out_specs=row,
        scratch_shapes=[pltpu.VMEM((tr, D_MODEL), MXU_DTYPE)],
        compiler_params=_params(("parallel",), 8 * _nbytes((tr, D_MODEL), F32)),
    )(z, z, z, z, yb, ln_g, ln_b, ws, bs_t)


def _mix_bwd(z, yb, dm, ln_g, ln_b, ws, bs_t):
    T = z.shape[0]
    tr = MIX_ROWS
    zcol, row, vec, ws_spec, bs_spec = _mix_specs(tr)

    def body(zu_ref, zv_ref, zga_ref, zgb_ref, yb_ref, dm_ref, g_ref, b_ref, ws_ref, bs_ref,
             dz_ref, dyb_ref, dl_ref, gws_ref, gbs_ref, glg_ref, glb_ref, vn_s, dvn_s):
        @pl.when(pl.program_id(0) == 0)
        def _():
            gws_ref[...] = jnp.zeros_like(gws_ref)
            gbs_ref[...] = jnp.zeros_like(gbs_ref)
            glg_ref[...] = jnp.zeros_like(glg_ref)
            glb_ref[...] = jnp.zeros_like(glb_ref)

        lane = lax.broadcasted_iota(jnp.int32, (CHUNK, 128), 1)
        va, dgelu_v = _gelu_and_grad(zv_ref[...])
        mu = jnp.mean(va, axis=-1, keepdims=True)
        xc = va - mu
        rs = lax.rsqrt(jnp.mean(xc * xc, axis=-1, keepdims=True) + EPS)
        vhat = xc * rs
        vn_s[...] = (vhat * g_ref[...] + b_ref[...]).astype(MXU_DTYPE)
        gbs_acc = jnp.zeros((CHUNK, 128), F32)
        for g in range(A_GROUPS):
            w = _tril_weights(ws_ref, g)
            bias = bs_ref[:, g:g + 1]
            cols = slice(g * CHUNK, (g + 1) * CHUNK)
            gw_acc = jnp.zeros((CHUNK, CHUNK), F32)
            for c in range(tr // CHUNK):
                rows = slice(c * CHUNK, (c + 1) * CHUNK)
                vn = vn_s[rows, cols]
                mixed = jnp.dot(w, vn, preferred_element_type=F32) + bias
                ua, dgelu_u = _gelu_and_grad(zu_ref[rows, cols])
                dmv = dm_ref[rows, cols]
                sa = _sigmoid(zga_ref[rows, cols])
                dya = dmv * sa
                dz_ref[rows, 2 * D_MODEL + g * CHUNK:2 * D_MODEL + (g + 1) * CHUNK] = (
                    dmv * (ua * mixed) * (sa * (1.0 - sa))).astype(dz_ref.dtype)
                dz_ref[rows, cols] = (dya * mixed * dgelu_u).astype(dz_ref.dtype)
                dmix = dya * ua
                gbs_acc = gbs_acc + jnp.where(lane == g, jnp.sum(dmix, axis=-1, keepdims=True), 0.0)
                dmix_b = dmix.astype(MXU_DTYPE)
                gw_acc = gw_acc + lax.dot_general(dmix_b, vn, _DIMS["nt"], preferred_element_type=F32)
                dvn_s[rows, cols] = lax.dot_general(w, dmix_b, _DIMS["tn"], preferred_element_type=F32)
            gws_ref[g] += jnp.where(_causal_mask(CHUNK), gw_acc, 0.0)
        gbs_ref[...] += gbs_acc

        dvn = dvn_s[...]
        glg_ref[...] += jnp.sum(dvn * vhat, axis=0, keepdims=True)
        glb_ref[...] += jnp.sum(dvn, axis=0, keepdims=True)
        dvh = dvn * g_ref[...]
        dva = rs * (dvh - jnp.mean(dvh, axis=-1, keepdims=True) - vhat * jnp.mean(dvh * vhat, axis=-1, keepdims=True))
        dz_ref[:, D_MODEL:2 * D_MODEL] = (dva * dgelu_v).astype(dz_ref.dtype)

        dmv = dm_ref[...]
        ybv = yb_ref[...]
        sb = _sigmoid(zgb_ref[...])
        dyb = dmv * sb
        dyb_ref[...] = dyb.astype(dyb_ref.dtype)
        dz_ref[:, 3 * D_MODEL:4 * D_MODEL] = (dmv * ybv * (sb * (1.0 - sb))).astype(dz_ref.dtype)
        dz_ref[:, 4 * D_MODEL:] = jnp.zeros((tr, LAT), dz_ref.dtype)
        prod = dyb * ybv
        sel = (lax.broadcasted_iota(jnp.int32, (HEADS, D_MODEL), 1) // NOPE
               == lax.broadcasted_iota(jnp.int32, (HEADS, D_MODEL), 0)).astype(jnp.bfloat16)
        hi = prod.astype(jnp.bfloat16)
        rest = prod - hi.astype(F32)
        mid = rest.astype(jnp.bfloat16)
        lo = (rest - mid.astype(F32)).astype(jnp.bfloat16)
        dl_ref[...] = (lax.dot_general(sel, hi, _DIMS["nt"], preferred_element_type=F32)
                       + lax.dot_general(sel, mid, _DIMS["nt"], preferred_element_type=F32)
                       + lax.dot_general(sel, lo, _DIMS["nt"], preferred_element_type=F32))

    return pl.pallas_call(
        body, name="mix_bwd",
        out_shape=(jax.ShapeDtypeStruct((T, IN_PAD), MXU_DTYPE), jax.ShapeDtypeStruct((T, D_MODEL), MXU_DTYPE),
                   jax.ShapeDtypeStruct((HEADS, T), F32), jax.ShapeDtypeStruct((A_GROUPS, CHUNK, CHUNK), F32),
                   jax.ShapeDtypeStruct((CHUNK, 128), F32), jax.ShapeDtypeStruct((1, D_MODEL), F32),
                   jax.ShapeDtypeStruct((1, D_MODEL), F32)),
        grid=(T // tr,),
        in_specs=[zcol(0), zcol(1), zcol(2), zcol(3), row, row, vec, vec, ws_spec, bs_spec],
        out_specs=(pl.BlockSpec((tr, IN_PAD), lambda i: (i, 0)), row, pl.BlockSpec((HEADS, tr), lambda i: (0, i)),
                   ws_spec, bs_spec, vec, vec),
        scratch_shapes=[pltpu.VMEM((tr, D_MODEL), MXU_DTYPE), pltpu.VMEM((tr, D_MODEL), F32)],
        compiler_params=_params(("arbitrary",), 12 * _nbytes((tr, D_MODEL), F32)),
    )(z, z, z, z, yb, dm, ln_g, ln_b, ws, bs_t)


def _lat_bwd(dz, z, dq, dk, dv, gq, gkv, wq, wkv, cos_a, sin_a, tr=256):
    T = z.shape[0]
    lat_blk = (4 * D_MODEL) // LAT

    def body(dz_in, z_ref, dq_ref, dk_ref, dv_ref, gq_ref, gkv_ref, wq_ref, wkv_ref, cos_ref, sin_ref,
             dz_ref, dqr_ref, dkv_ref, ggq_ref, ggkv_ref):
        del dz_in

        @pl.when(pl.program_id(0) == 0)
        def _():
            ggq_ref[...] = jnp.zeros_like(ggq_ref)
            ggkv_ref[...] = jnp.zeros_like(ggkv_ref)

        cos_v, sin_v = cos_ref[...], sin_ref[...]
        dkr = jnp.zeros((tr, 128), F32)
        for h in range(HEADS):
            o = h * HEAD_PAD
            dqr_ref[:, o:o + NOPE] = dq_ref[:, o:o + NOPE].astype(MXU_DTYPE)
            dqr_ref[:, o + NOPE:o + HEAD_PAD] = _rope_mix_bwd(dq_ref[:, o + NOPE:o + HEAD_PAD], cos_v, sin_v).astype(MXU_DTYPE)
            dkv_ref[:, h * NOPE:(h + 1) * NOPE] = dk_ref[:, o:o + NOPE].astype(MXU_DTYPE)
            dkr = dkr + _rope_mix_bwd(dk_ref[:, o + NOPE:o + HEAD_PAD], cos_v, sin_v)
        dkv_ref[:, HEADS * NOPE:] = dv_ref[...]
        dcqn = lax.dot_general(dqr_ref[...], wq_ref[...], _DIMS["nt"], preferred_element_type=F32)
        dckvn = lax.dot_general(dkv_ref[...], wkv_ref[...], _DIMS["nt"], preferred_element_type=F32)

        zl = z_ref[...]

        def rms_bwd(c, dn, g_ref, gg_ref):
            r = lax.rsqrt(jnp.mean(c * c, axis=-1, keepdims=True) + EPS)
            ch = c * r
            gg_ref[...] += jnp.sum(dn * ch, axis=0, keepdims=True)
            dch = dn * g_ref[...]
            return r * (dch - ch * jnp.mean(dch * ch, axis=-1, keepdims=True))

        dz_ref[:, :Q_RANK] = rms_bwd(zl[:, :Q_RANK], dcqn, gq_ref, ggq_ref).astype(dz_ref.dtype)
        dz_ref[:, Q_RANK:Q_RANK + KV_RANK] = rms_bwd(zl[:, Q_RANK:Q_RANK + KV_RANK], dckvn, gkv_ref, ggkv_ref).astype(dz_ref.dtype)
        dz_ref[:, Q_RANK + KV_RANK:] = dkr.astype(dz_ref.dtype)

    def row(w):
        return pl.BlockSpec((tr, w), lambda i: (i, 0))

    def full(a):
        return pl.BlockSpec(a.shape, lambda i: (0, 0))

    lat = pl.BlockSpec((tr, LAT), lambda i: (i, lat_blk))
    return pl.pallas_call(
        body, name="lat_bwd",
        out_shape=(jax.ShapeDtypeStruct(dz.shape, dz.dtype), jax.ShapeDtypeStruct((T, HEADS * HEAD_PAD), MXU_DTYPE),
                   jax.ShapeDtypeStruct((T, 2 * HEADS * NOPE), MXU_DTYPE), jax.ShapeDtypeStruct(gq.shape, F32),
                   jax.ShapeDtypeStruct(gkv.shape, F32)),
        grid=(T // tr,),
        in_specs=[pl.BlockSpec(memory_space=pl.ANY), lat, row(HEADS * HEAD_PAD), row(HEADS * HEAD_PAD), row(HEADS * NOPE),
                  full(gq), full(gkv), full(wq), full(wkv), row(128), row(128)],
        out_specs=(lat, row(HEADS * HEAD_PAD), row(2 * HEADS * NOPE), full(gq), full(gkv)),
        input_output_aliases={0: 0},
        compiler_params=_params(("arbitrary",), 8 * _nbytes((tr, HEADS * HEAD_PAD), F32)),
    )(dz, z, dq, dk, dv, gq, gkv, wq, wkv, cos_a, sin_a)


GATE_ROWS = 64
HALO = 8


def _taps(ref, half, r, first):
    C = GATE_ROWS
    if first:
        xs = jnp.concatenate([jnp.zeros((HALO, ref.shape[-1]), F32), ref[half, 0:C, :]], axis=0)
    else:
        xs = ref[half, pl.ds(pl.multiple_of(r * C - HALO, HALO), C + HALO), :]
    return xs[HALO:, :], pltpu.roll(xs, 1, 0)[HALO:, :], pltpu.roll(xs, 2, 0)[HALO:, :]


def _conv_taps(taps, cw, cb):
    x0, x1, x2 = taps
    return cb + cw[0:1, :] * x2 + cw[1:2, :] * x1 + cw[2:3, :] * x0


def _fold8(x):
    acc = x[0:8, :]
    for i in range(1, x.shape[0] // 8):
        acc = acc + x[8 * i:8 * (i + 1), :]
    return acc


def _gate_fwd(up3, conv_w, conv_b, B, S):
    T = B * S
    W = FF_TILE
    C = GATE_ROWS

    def body(up_ref, cw_ref, cb_ref, act_ref, conv_ref):
        def chunk(r, first):
            gate = _conv_taps(_taps(up_ref, 0, r, first), cw_ref[0], cb_ref[0])
            val = _conv_taps(_taps(up_ref, 1, r, first), cw_ref[1], cb_ref[1])
            rows = pl.ds(0 if first else pl.multiple_of(r * C, C), C)
            conv_ref[0, rows, :] = gate.astype(conv_ref.dtype)
            conv_ref[1, rows, :] = val.astype(conv_ref.dtype)
            act_ref[rows, :] = (gate * _sigmoid(gate) * val).astype(act_ref.dtype)

        chunk(0, True)

        @pl.loop(1, S // C)
        def _(r):
            chunk(r, False)

    up_spec = pl.BlockSpec((2, S, W), lambda b, j: (0, b, j))
    return pl.pallas_call(
        body, name="gate_fwd",
        out_shape=(jax.ShapeDtypeStruct((T, D_FF), MXU_DTYPE), jax.ShapeDtypeStruct((2, T, D_FF), MXU_DTYPE)),
        grid=(B, N_FF_TILES),
        in_specs=[up_spec, pl.BlockSpec((2, 3, W), lambda b, j: (0, 0, j)), pl.BlockSpec((2, 1, W), lambda b, j: (0, 0, j))],
        out_specs=(pl.BlockSpec((S, W), lambda b, j: (b, j)), up_spec),
        compiler_params=_params(("parallel", "parallel"), 8 * _nbytes((S, W), F32)),
    )(up3, conv_w, conv_b)


def _gate_bwd(up3, conv3, dact, conv_w, B, S):
    T = B * S
    W = FF_TILE
    C = GATE_ROWS

    def body(up_ref, conv_ref, da_ref, cw_ref, dup_ref, gcw_ref, gcb_ref, d_s):
        @pl.when(pl.program_id(1) == 0)
        def _():
            gcw_ref[...] = jnp.zeros_like(gcw_ref)
            gcb_ref[...] = jnp.zeros_like(gcb_ref)

        @pl.loop(0, S // C)
        def _(r):
            rows = pl.ds(pl.multiple_of(r * C, C), C)
            gate, val = conv_ref[0, rows, :].astype(F32), conv_ref[1, rows, :].astype(F32)
            sg = _sigmoid(gate)
            da = da_ref[rows, :]
            d_s[0, rows, :] = da * val * (sg * (1.0 + gate * (1.0 - sg)))
            d_s[1, rows, :] = da * (gate * sg)

        d_s[:, S:S + HALO, :] = jnp.zeros((2, HALO, W), F32)

        def chunk(r, sums):
            base = pl.multiple_of(r * C, C)
            out = []
            for half in (0, 1):
                ds_ = d_s[half, pl.ds(base, C + HALO), :]
                d0, d1, d2 = ds_[:C, :], pltpu.roll(ds_, C + HALO - 1, 0)[:C, :], pltpu.roll(ds_, C + HALO - 2, 0)[:C, :]
                cw = cw_ref[half]
                dup_ref[half, pl.ds(base, C), :] = (cw[2:3, :] * d0 + cw[1:2, :] * d1 + cw[0:1, :] * d2).astype(dup_ref.dtype)
                x = up_ref[half, pl.ds(base, C), :]
                sb, s0, s1, s2 = sums[half]
                out.append((sb + _fold8(d0), s0 + _fold8(d2 * x), s1 + _fold8(d1 * x), s2 + _fold8(d0 * x)))
            return tuple(out)

        zeros = tuple(tuple(jnp.zeros((8, W), F32) for _ in range(4)) for _ in range(2))
        sums = lax.fori_loop(0, S // C, chunk, zeros)
        for half in (0, 1):
            sb, s0, s1, s2 = sums[half]
            gcb_ref[half] += jnp.sum(sb, axis=0, keepdims=True)
            gcw_ref[half, 0:1, :] += jnp.sum(s0, axis=0, keepdims=True)
            gcw_ref[half, 1:2, :] += jnp.sum(s1, axis=0, keepdims=True)
            gcw_ref[half, 2:3, :] += jnp.sum(s2, axis=0, keepdims=True)

    up_spec = pl.BlockSpec((2, S, W), lambda j, b: (0, b, j))
    cw_spec = pl.BlockSpec((2, 3, W), lambda j, b: (0, 0, j))
    cb_spec = pl.BlockSpec((2, 1, W), lambda j, b: (0, 0, j))
    return pl.pallas_call(
        body, name="gate_bwd",
        out_shape=(jax.ShapeDtypeStruct((2, T, D_FF), MXU_DTYPE), jax.ShapeDtypeStruct((2, 3, D_FF), F32),
                   jax.ShapeDtypeStruct((2, 1, D_FF), F32)),
        grid=(N_FF_TILES, B),
        in_specs=[up_spec, up_spec, pl.BlockSpec((S, W), lambda j, b: (b, j)), cw_spec],
        out_specs=(up_spec, cw_spec, cb_spec),
        scratch_shapes=[pltpu.VMEM((2, S + HALO, W), F32)],
        compiler_params=_params(("parallel", "arbitrary"), 12 * _nbytes((S, W), F32)),
    )(up3, conv3, dact, conv_w)


def _final(x2, tgt, g, tr=512):
    T, D = x2.shape

    def body(x_ref, t_ref, g_ref, dx_ref, loss_ref, gg_ref):
        @pl.when(pl.program_id(0) == 0)
        def _():
            loss_ref[...] = jnp.zeros_like(loss_ref)
            gg_ref[...] = jnp.zeros_like(gg_ref)

        xv = x_ref[...]
        gv = g_ref[...]
        r = lax.rsqrt(jnp.mean(xv * xv, axis=-1, keepdims=True) + EPS)
        xn = xv * r
        err = xn * gv - t_ref[...]
        loss_ref[...] += 0.5 * jnp.sum(jnp.mean(err * err, axis=-1, keepdims=True), axis=0, keepdims=True)
        dy = err * (1.0 / D)
        gg_ref[...] += jnp.sum(dy * xn, axis=0, keepdims=True)
        dxn = dy * gv
        dx_ref[...] = r * (dxn - xn * jnp.mean(dxn * xn, axis=-1, keepdims=True))

    row = pl.BlockSpec((tr, D), lambda i: (i, 0))
    vec = pl.BlockSpec((1, D), lambda i: (0, 0))
    return pl.pallas_call(
        body, name="final_loss",
        out_shape=(jax.ShapeDtypeStruct((T, D), F32), jax.ShapeDtypeStruct((1, 128), F32), jax.ShapeDtypeStruct((1, D), F32)),
        grid=(T // tr,), in_specs=[row, row, vec],
        out_specs=(row, pl.BlockSpec((1, 128), lambda i: (0, 0)), vec),
        compiler_params=_params(("arbitrary",), 6 * _nbytes((tr, D), F32)),
    )(x2, tgt, g)


def _sum_slabs(parts, name, tr):
    rows, cols = parts[0].shape
    n = len(parts)

    def body(*refs):
        acc = refs[0][...]
        for r in refs[1:n]:
            acc = acc + r[...]
        refs[n][...] = acc

    blk = pl.BlockSpec((tr, cols), lambda i: (i, 0))
    return pl.pallas_call(
        body, name=name, out_shape=jax.ShapeDtypeStruct((rows, cols), F32), grid=(rows // tr,),
        in_specs=[blk] * n, out_specs=blk,
        compiler_params=_params(("parallel",), (n + 1) * _nbytes((tr, cols), F32)),
    )(*parts)


ADAMW_BLOCK_BYTES = 2400 * 1024


def _adamw(w, g, m, v, name, copy_grad=False):
    lead = w.ndim == 3
    rows, cols = w.shape[-2:]
    fits = [d for d in range(8, rows + 1, 8) if rows % d == 0 and d * cols * 4 <= ADAMW_BLOCK_BYTES]
    tr = max(fits) if fits else rows
    c1 = 1.0 - ADAM_B1 ** ADAM_STEP
    c2 = 1.0 - ADAM_B2 ** ADAM_STEP

    def body(w_ref, g_ref, m_ref, v_ref, d_ref, nm_ref, nv_ref, *g_out):
        gv = g_ref[...]
        nm = ADAM_B1 * m_ref[...] + (1.0 - ADAM_B1) * gv
        nv = ADAM_B2 * v_ref[...] + (1.0 - ADAM_B2) * (gv * gv)
        nm_ref[...] = nm
        nv_ref[...] = nv
        d_ref[...] = -ADAM_LR * ((nm / c1) / (jnp.sqrt(nv / c2) + ADAM_EPS) + ADAM_WD * w_ref[...])
        if copy_grad:
            g_out[0][...] = gv

    blk = pl.BlockSpec((None, tr, cols), lambda i: (0, i, 0)) if lead else pl.BlockSpec((tr, cols), lambda i: (i, 0))
    sds = jax.ShapeDtypeStruct(w.shape, F32)
    n_out = 4 if copy_grad else 3
    return pl.pallas_call(
        body, name=name, out_shape=(sds,) * n_out, grid=(rows // tr,), in_specs=[blk] * 4, out_specs=(blk,) * n_out,
        compiler_params=_params(("parallel",), (4 + n_out) * _nbytes((tr, cols), F32)),
    )(w, g, m, v)


_ANY = pl.BlockSpec(memory_space=pl.ANY)


def _place():
    x, y, c = lax.axis_index("x"), lax.axis_index("y"), lax.axis_index("c")
    chips = [(1 - x, y), (x, 1 - y), (1 - x, 1 - y)]
    return x, y, c, chips


def _forward_halves(lands):
    n = len(lands)

    def body(*refs):
        outs, send, recv = refs[n:2 * n], refs[2 * n], refs[2 * n + 1]
        x, y, c, chips = _place()
        cps = []
        for w in range(n):
            for j, (px, py) in enumerate(chips):
                landed = outs[w].at[2 * px + py, c]
                cps.append(pltpu.make_async_remote_copy(
                    src_ref=landed, dst_ref=landed, send_sem=send.at[3 * w + j], recv_sem=recv.at[3 * w + j],
                    device_id=(x, y, 1 - c), device_id_type=MESH))
        for cp in cps:
            cp.start()
        for w in range(n):
            for j, (px, py) in enumerate(chips):
                other = outs[w].at[2 * px + py, 1 - c]
                pltpu.make_async_remote_copy(src_ref=other, dst_ref=other, send_sem=send.at[3 * w + j],
                                             recv_sem=recv.at[3 * w + j], device_id=(x, y, 1 - c),
                                             device_id_type=MESH).wait_recv()
        for cp in cps:
            cp.wait_send()

    dma = lambda k: pltpu.SemaphoreType.DMA((k,))
    return pl.pallas_call(
        body, name="gather_forward_halves", out_shape=tuple(jax.ShapeDtypeStruct(a.shape, a.dtype) for a in lands),
        in_specs=[_ANY] * n, out_specs=tuple([_ANY] * n), input_output_aliases={w: w for w in range(n)},
        scratch_shapes=[dma(3 * n), dma(3 * n)],
    )(*lands)


_HBM = pl.BlockSpec(memory_space=pltpu.HBM)
_SEM = pl.BlockSpec(memory_space=pltpu.SEMAPHORE)
_EFFECT = pltpu.SideEffectType.DATAFLOW_SIDE_EFFECTING


SEMS_PER_ARRAY = 8


def _exchange_copies(srcs, lands, send, recv, mode):
    x, y, c, chips = _place()
    if mode == "halves":
        cps = []
        for w, (src, land) in enumerate(zip(srcs, lands)):
            pieces = [(src.at[c], land.at[2 * x + y, c], (px, py, c)) for px, py in chips]
            pieces.append((src, land.at[2 * x + y], (x, y, 1 - c)))
            for k, (piece, dst, peer) in enumerate(pieces):
                cps.append(pltpu.make_async_remote_copy(
                    src_ref=piece, dst_ref=dst, send_sem=send.at[SEMS_PER_ARRAY * w + k],
                    recv_sem=recv.at[SEMS_PER_ARRAY * w + k], device_id=peer, device_id_type=MESH))
        return cps
    if mode == "swap":
        return [pltpu.make_async_remote_copy(
            src_ref=src.at[:, 1 - c], dst_ref=land, send_sem=send.at[SEMS_PER_ARRAY * w],
            recv_sem=recv.at[SEMS_PER_ARRAY * w], device_id=(x, y, 1 - c), device_id_type=MESH)
            for w, (src, land) in enumerate(zip(srcs, lands))]
    if mode == "all":
        flips = [(fx, fy, fc) for fx in (0, 1) for fy in (0, 1) for fc in (0, 1)][1:]
        peers = [(x ^ fx, y ^ fy, c ^ fc) for fx, fy, fc in flips]
        slot = 4 * x + 2 * y + c
    else:
        peers = [(px, py, c) for px, py in chips] + ([(x, y, 1 - c)] if mode == "gather" else [])
        slot = 2 * x + y
    cps = []
    for w, (src, land) in enumerate(zip(srcs, lands)):
        for k, peer in enumerate(peers):
            piece = src.at[2 * peer[0] + peer[1]] if mode == "scatter" else src
            cps.append(pltpu.make_async_remote_copy(
                src_ref=piece, dst_ref=land.at[slot], send_sem=send.at[SEMS_PER_ARRAY * w + k],
                recv_sem=recv.at[SEMS_PER_ARRAY * w + k], device_id=peer, device_id_type=MESH))
    return cps


def _exchange_start(srcs, name, mode, after):
    n = len(srcs)
    if mode == "swap":
        land_shapes = [(s.shape[0],) + s.shape[2:] for s in srcs]
    else:
        lead = {"gather": (N_CHIPS,), "halves": (N_CHIPS,), "scatter": (), "all": (2 * N_CHIPS,)}[mode]
        land_shapes = [lead + s.shape for s in srcs]

    def body(*refs):
        src_refs, land_refs = refs[:n], refs[n:2 * n]
        send, recv = refs[2 * n + 1], refs[2 * n + 2]
        token = refs[-1]
        for cp in _exchange_copies(src_refs, land_refs, send, recv, mode):
            cp.start()
        token[...] = jnp.zeros_like(token)

    sems = pltpu.SemaphoreType.DMA((SEMS_PER_ARRAY * n,))
    out = pl.pallas_call(
        body, name=name,
        out_shape=(sems, sems, *[pltpu.HBM(s.shape, s.dtype) for s in srcs],
                   *[pltpu.HBM(shp, s.dtype) for shp, s in zip(land_shapes, srcs)], jax.ShapeDtypeStruct((8, 128), F32)),
        in_specs=[_HBM] * (2 * n) + [_ANY],
        out_specs=(_SEM, _SEM, *[_HBM] * (2 * n), pl.BlockSpec(memory_space=pltpu.VMEM)),
        input_output_aliases={i: 2 + i for i in range(2 * n)},
        compiler_params=pltpu.CompilerParams(has_side_effects=_EFFECT),
    )(*[pltpu.with_memory_space_constraint(s, pltpu.HBM) for s in srcs],
      *[pltpu.with_memory_space_constraint(lax.empty(shp, s.dtype), pltpu.HBM) for shp, s in zip(land_shapes, srcs)],
      after)
    return out[0], out[1], out[2:2 + n], out[2 + n:2 + 2 * n], out[-1]


def _exchange_wait(started, name, mode, after):
    send, recv, src_thru, land_thru, _ = started
    n = len(src_thru)
    after = list(after) if isinstance(after, (list, tuple)) else [after]

    def body(*refs):
        src_refs, land_refs, send_ref, recv_ref = refs[:n], refs[n:2 * n], refs[2 * n], refs[2 * n + 1]
        for cp in _exchange_copies(src_refs, land_refs, send_ref, recv_ref, mode):
            cp.wait_send()
            cp.wait_recv()

    out = pl.pallas_call(
        body, name=name,
        out_shape=tuple(pltpu.HBM(a.shape, a.dtype) for a in list(src_thru) + list(land_thru)),
        in_specs=[_HBM] * (2 * n) + [_SEM, _SEM] + [_ANY] * len(after), out_specs=tuple([_HBM] * (2 * n)),
        input_output_aliases={i: i for i in range(2 * n)},
        compiler_params=pltpu.CompilerParams(has_side_effects=_EFFECT),
    )(*src_thru, *land_thru, send, recv, *after)
    return out[:n], out[n:]


def _swap_halves(gs, name):
    n = len(gs)

    def body(*refs):
        ins, outs, send, recv = refs[:n], refs[n:2 * n], refs[2 * n], refs[2 * n + 1]
        x, y, c, _ = _place()
        cps = []
        for w in range(n):
            cps.append(pltpu.make_async_remote_copy(
                src_ref=ins[w].at[:, 1 - c], dst_ref=outs[w], send_sem=send.at[w], recv_sem=recv.at[w],
                device_id=(x, y, 1 - c), device_id_type=MESH))
        for cp in cps:
            cp.start()
        for cp in cps:
            cp.wait()

    return pl.pallas_call(
        body, name=name,
        out_shape=tuple(jax.ShapeDtypeStruct((g.shape[0],) + g.shape[2:], g.dtype) for g in gs),
        in_specs=[_ANY] * n, out_specs=tuple([_ANY] * n),
        scratch_shapes=[pltpu.SemaphoreType.DMA((n,)), pltpu.SemaphoreType.DMA((n,))],
    )(*gs)


GRAD_PAYLOAD = jnp.bfloat16


def _half_blocks(half_rows, cols):
    if (half_rows // 2) % 16 == 0:
        return (half_rows // 2, cols), (lambda r: (r, 0))
    assert cols % 256 == 0, (half_rows, cols)
    return (half_rows, cols // 2), (lambda r: (0, r))


def _pair_sum(gs, gots, name):
    n = len(gs)
    core = lax.axis_index("c").astype(jnp.int32).reshape(1)

    def body(core_ref, *refs):
        del core_ref
        for w in range(n):
            refs[2 * n + w][...] = (refs[w][...] + refs[n + w][...]).astype(GRAD_PAYLOAD)

    in_specs, out_specs, out_shape, nbytes = [], [], [], 0
    cuts = [_half_blocks(g.shape[1] // 2, g.shape[2]) for g in gs]
    for g, ((br, bc), at) in zip(gs, cuts):
        per_half = (g.shape[1] // 2) // br
        in_specs.append(pl.BlockSpec((1, br, bc), lambda s, r, core, at=at, per_half=per_half:
                                     (s, per_half * core[0] + at(r)[0], at(r)[1])))
        nbytes += 3 * _nbytes((br, bc), F32)
    for g, ((br, bc), at) in zip(gs, cuts):
        in_specs.append(pl.BlockSpec((1, br, bc), lambda s, r, core, at=at: (s,) + at(r)))
        out_specs.append(pl.BlockSpec((1, br, bc), lambda s, r, core, at=at: (s,) + at(r)))
        out_shape.append(jax.ShapeDtypeStruct((g.shape[0], g.shape[1] // 2, g.shape[2]), GRAD_PAYLOAD))
    return pl.pallas_call(
        body, name=name, out_shape=tuple(out_shape),
        grid_spec=pltpu.PrefetchScalarGridSpec(num_scalar_prefetch=1, grid=(N_CHIPS, 2), in_specs=in_specs,
                                               out_specs=tuple(out_specs)),
        compiler_params=_params(("parallel", "parallel"), nbytes),
    )(core, *gs, *gots)


def _chip_sum(ps, landed):
    n = len(ps)
    x, y, c = lax.axis_index("x"), lax.axis_index("y"), lax.axis_index("c")
    where = jnp.stack([2 * x + y, 2 * (1 - x) + y, 2 * x + (1 - y), 2 * (1 - x) + (1 - y), c]).astype(jnp.int32)

    def body(where_ref, *refs):
        del where_ref
        for w in range(n):
            terms = [refs[4 * w + t][...].astype(F32) for t in range(4)]
            refs[4 * n + w][...] = ((terms[0] + terms[1]) + terms[2]) + terms[3]

    in_specs, out_specs, out_shape, args, nbytes = [], [], [], [], 0
    for p, a in zip(ps, landed):
        (br, bc), at = _half_blocks(a.shape[1], a.shape[2])
        blk = (1, br, bc)
        in_specs.append(pl.BlockSpec(blk, lambda r, where, at=at: (where[0],) + at(r)))
        args.append(p)
        for t in (1, 2, 3):
            in_specs.append(pl.BlockSpec(blk, lambda r, where, t=t, at=at: (where[t],) + at(r)))
            args.append(a)
        out_specs.append(pl.BlockSpec(blk, lambda r, where, at=at: (where[4],) + at(r)))
        out_shape.append(jax.ShapeDtypeStruct((2,) + a.shape[1:], F32))
        nbytes += 4 * _nbytes(blk, F32)
    return pl.pallas_call(
        body, name="grad_chip_sum", out_shape=tuple(out_shape),
        grid_spec=pltpu.PrefetchScalarGridSpec(num_scalar_prefetch=1, grid=(2,), in_specs=in_specs,
                                               out_specs=tuple(out_specs)),
        compiler_params=_params(("parallel",), nbytes),
    )(where, *args)


def _join_halves(ss):
    n = len(ss)

    def body(*refs):
        outs, send, recv = refs[n:2 * n], refs[2 * n], refs[2 * n + 1]
        x, y, c, _ = _place()
        cps = []
        for w in range(n):
            cps.append(pltpu.make_async_remote_copy(
                src_ref=outs[w].at[c], dst_ref=outs[w].at[c], send_sem=send.at[w], recv_sem=recv.at[w],
                device_id=(x, y, 1 - c), device_id_type=MESH))
        for cp in cps:
            cp.start()
        for w in range(n):
            got = outs[w].at[1 - c]
            pltpu.make_async_remote_copy(src_ref=got, dst_ref=got, send_sem=send.at[w], recv_sem=recv.at[w],
                                         device_id=(x, y, 1 - c), device_id_type=MESH).wait_recv()
        for cp in cps:
            cp.wait_send()

    dma = lambda k: pltpu.SemaphoreType.DMA((k,))
    return pl.pallas_call(
        body, name="grad_join_halves",
        out_shape=tuple(jax.ShapeDtypeStruct(s.shape, s.dtype) for s in ss),
        in_specs=[_ANY] * n, out_specs=tuple([_ANY] * n), input_output_aliases={w: w for w in range(n)},
        scratch_shapes=[dma(n), dma(n)],
    )(*ss)


def _rot_cols(w, axis=-1):
    a, b = jnp.split(w, 2, axis=axis)
    return jnp.concatenate([-b, a], axis=axis)


def _rot_cols_t(g, axis=-1):
    a, b = jnp.split(g, 2, axis=axis)
    return jnp.concatenate([b, -a], axis=axis)


def _cols_from_chips(a):
    n, r, cs = a.shape
    return jnp.transpose(a, (1, 0, 2)).reshape(r, n * cs)


def _cols_to_chips(a):
    r, cc = a.shape
    return jnp.transpose(a.reshape(r, N_CHIPS, cc // N_CHIPS), (1, 0, 2))


def _conv_w_split(cw):
    return jnp.swapaxes(cw.reshape(3, 2, D_FF), 0, 1)


def _conv_w_join(g):
    return jnp.swapaxes(g, 0, 1).reshape(3, 2 * D_FF)


_SEG =(D_MODEL, 2 * D_MODEL, 2 * D_MODEL + Q_RANK, 2 * D_MODEL + Q_RANK + KV_RANK, 2 * D_MODEL + Q_RANK + KV_RANK + ROPE,
        3 * D_MODEL + Q_RANK + KV_RANK + ROPE)


def _w_in_t_to_pad(wt):
    u, v, cq, ckv, kr, ga, gb = jnp.split(wt, _SEG, axis=0)
    return jnp.concatenate([u, v, ga, gb, cq, ckv, kr, _rot_cols(kr, axis=0)], axis=0)


def _w_in_t_from_pad(gt):
    u, v, ga, gb, cq, ckv, kr, krr = jnp.split(
        gt, (D_MODEL, 2 * D_MODEL, 3 * D_MODEL, 4 * D_MODEL, 4 * D_MODEL + Q_RANK, 4 * D_MODEL + Q_RANK + KV_RANK,
             4 * D_MODEL + Q_RANK + KV_RANK + ROPE), axis=0)
    return jnp.concatenate([u, v, cq, ckv, kr + _rot_cols_t(krr, axis=0), ga, gb], axis=0)


def _w_uq_to_pad(w):
    t = w.reshape(Q_RANK, HEADS, QK_DIM)
    nope, rope = t[..., :NOPE], t[..., NOPE:]
    return jnp.concatenate([nope, rope, _rot_cols(rope)], axis=-1).reshape(Q_RANK, HEADS * HEAD_PAD)


def _w_uq_from_pad(g):
    t = g.reshape(Q_RANK, HEADS, HEAD_PAD)
    nope, rope, rot = t[..., :NOPE], t[..., NOPE:QK_DIM], t[..., QK_DIM:]
    return jnp.concatenate([nope, rope + _rot_cols_t(rot)], axis=-1).reshape(Q_RANK, HEADS * QK_DIM)


def _w_ukv_to_pad(w):
    t = w.reshape(KV_RANK, HEADS, 2, NOPE)
    return jnp.swapaxes(t, 1, 2).reshape(KV_RANK, 2 * HEADS * NOPE)


def _w_ukv_from_pad(g):
    t = g.reshape(KV_RANK, 2, HEADS, NOPE)
    return jnp.swapaxes(t, 1, 2).reshape(KV_RANK, 2 * HEADS * NOPE)


def _rope_tables(positions):
    inv_freq = 1.0 / (ROPE_THETA ** (jnp.arange(0, ROPE, 2, dtype=F32) / ROPE))
    ang = positions.astype(F32).reshape(-1, 1) * inv_freq
    cos, sin = jnp.cos(ang), jnp.sin(ang)
    zero = jnp.zeros((ang.shape[0], 64), F32)
    return jnp.concatenate([cos, cos, zero], axis=1), jnp.concatenate([sin, sin, zero], axis=1)


_BIG = ("w_in", "w_uq", "w_ukv", "w_out", "w_up", "w_down")
UP_SHARD = 2 * D_FF // N_CHIPS


def _local_step(x, positions, tgt, wts, in_weights, mixer_weights, ffn_weights, on_ffn_grads, on_mixer_grads):
    B, S, D = x.shape
    T = B * S
    xf = x.reshape(T, D)
    cos_a, sin_a = _rope_tables(positions)
    bs_t = jnp.pad(wts["a_spatial_b"].T, ((0, 0), (0, 128 - A_GROUPS)))

    h = _rms_fwd(xf, wts["mix_norm"], "norm1_fwd")
    wts = dict(wts)
    wts["w_in"], token = in_weights([h, cos_a, sin_a])
    z = _mm(h, wts["w_in"], "nt", "in_proj", tm=512, tn=1536, tk=D, n_outer=True, after=token)
    wts["w_q"], wts["w_kv"], wts["w_out"] = mixer_weights(z)
    q, k, v, cqn, ckvn = _lat_fwd(z, wts["q_a_norm"], wts["kv_a_norm"], wts["w_q"], wts["w_kv"], cos_a, sin_a)
    yb, *lses = _attn_fwd(q, k, v, B, S)
    merged = _mix_fwd(z, yb, wts["a_v_norm_g"], wts["a_v_norm_b"], wts["a_spatial_w"], bs_t)
    x1 = _mm(merged, wts["w_out"], "nn", "out_proj", tm=512, tn=D, tk=D, add=xf)
    h2 = _rms_fwd(x1, wts["ffn_norm"], "norm2_fwd")
    wts["w_up"], wts["w_down"], wts["conv_w"] = ffn_weights(h2)
    up_pre = _mm(h2, wts["w_up"], "nn", "up_proj", tm=512, tn=UP_SHARD, tk=D, dims=(T, 2 * D_FF, D),
                 b_spec=pl.BlockSpec((None, D, UP_SHARD), lambda i, j, k: (j, 0, 0)),
                 o_spec=pl.BlockSpec((None, 512, UP_SHARD), lambda i, j, k: (j // 2, i, j % 2)), out_shape=(2, T, D_FF),
                 n_outer=True)
    act, up_conv = _gate_fwd(up_pre, wts["conv_w"], wts["conv_b"], B, S)
    x2 = _mm(act, wts["w_down"], "nn", "down_proj", tm=512, tn=D, tk=1408, add=x1)
    dx2, loss_row, g_final = _final(x2, tgt.reshape(T, D), wts["final_norm"])

    g = {"final_norm": g_final}
    dact = _mm(dx2, wts["w_down"], "nt", "down_proj_dx", tm=512, tn=1408, tk=D, n_outer=True)
    tk2, tk1 = min(2048, T), min(1024, T)
    g["w_down"], g["w_down_lo"] = _mm(act, dx2, "tn", "down_proj_dw", tm=1408, tn=D, tk=tk1, copy_dtype=GRAD_PAYLOAD)
    dup, g["conv_w"], g["conv_b"] = _gate_bwd(up_pre, up_conv, dact, wts["conv_w"], B, S)
    g["w_up"], g["w_up_lo"] = _mm(
        h2, dup, "tn", "up_proj_dw", tm=D, tn=UP_SHARD, tk=tk2, dims=(D, 2 * D_FF, T), copy_dtype=GRAD_PAYLOAD,
        b_spec=pl.BlockSpec((None, tk2, UP_SHARD), lambda i, j, k: (j // 2, k, j % 2)),
        o_spec=pl.BlockSpec((None, D, UP_SHARD), lambda i, j, k: (j, 0, 0)), out_shape=(N_CHIPS, D, UP_SHARD))
    token, ffn_sent = on_ffn_grads(g)
    dh2 = _mm(dup, wts["w_up"], "nt", "up_proj_dx", tm=512, tn=D, tk=UP_SHARD, dims=(T, D, 2 * D_FF), after=token,
              a_spec=pl.BlockSpec((None, 512, UP_SHARD), lambda i, j, k: (k // 2, i, k % 2)),
              b_spec=pl.BlockSpec((None, D, UP_SHARD), lambda i, j, k: (k, 0, 0)))
    token = ffn_sent(dh2)
    dx1, g["ffn_norm"] = _rms_bwd(x1, wts["ffn_norm"], dh2, dx2, "norm2_bwd")
    dm = _mm(dx1, wts["w_out"], "nt", "out_proj_dx", tm=512, tn=D, tk=D, after=token)
    g["w_out"], g["w_out_lo"] = _mm(merged, dx1, "tn", "out_proj_dw", tm=D, tn=D, tk=tk1, copy_dtype=GRAD_PAYLOAD)
    dz, dyb, dl, g["a_spatial_w"], gbs, g["a_v_norm_g"], g["a_v_norm_b"] = _mix_bwd(
        z, yb, dm, wts["a_v_norm_g"], wts["a_v_norm_b"], wts["a_spatial_w"], bs_t)
    g["a_spatial_b"] = gbs[:, :A_GROUPS].T
    delta = dl.reshape(HEADS * T // ATT_BLOCK, 1, ATT_BLOCK)
    dq, dk, dv = _attn_bwd(q, k, v, dyb, lses, delta, B, S)
    dz, dq_raw, dkv, g["q_a_norm"], g["kv_a_norm"] = _lat_bwd(
        dz, z, dq, dk, dv, wts["q_a_norm"], wts["kv_a_norm"], wts["w_q"], wts["w_kv"], cos_a, sin_a)
    g["w_q"] = _mm(cqn, dq_raw, "tn", "q_proj_dw", tm=Q_RANK, tn=HEADS * HEAD_PAD, tk=tk2)
    g["w_kv"] = _mm(ckvn, dkv, "tn", "kv_proj_dw", tm=KV_RANK, tn=2 * HEADS * NOPE, tk=tk2)
    g["w_in"] = _mm(dz, h, "tn", "in_proj_dw", tm=1536, tn=D, tk=tk2)
    token = on_mixer_grads(g)
    dh = _mm(dz, wts["w_in"], "nn", "in_proj_dx", tm=512, tn=D, tk=1536, after=token)
    dx, g["mix_norm"] = _rms_bwd(xf, wts["mix_norm"], dh, dx1, "norm1_bwd")
    return loss_row[0, 0], dx.reshape(B, S, D), g


_SMALL = (("mix_norm", (1, D_MODEL)), ("a_v_norm_g", (1, D_MODEL)), ("a_v_norm_b", (1, D_MODEL)),
          ("a_spatial_w", (A_GROUPS * CHUNK, CHUNK)), ("a_spatial_b", (1, A_GROUPS * CHUNK)), ("q_a_norm", (1, Q_RANK)),
          ("kv_a_norm", (1, KV_RANK)), ("ffn_norm", (1, D_MODEL)), ("conv_b", (1, 2 * D_FF)), ("final_norm", (1, D_MODEL)),
          ("conv_w", (3, 2 * D_FF)))
_SMALL_SIZE = sum(math.prod(s) for _, s in _SMALL)
_SMALL_ROWS = -(-(_SMALL_SIZE + 1) // (128 * 8)) * 8


def kernel(x, positions, mix_norm, w_in, a_v_norm_g, a_v_norm_b, a_spatial_w, a_spatial_b, q_a_norm, w_uq, kv_a_norm, w_ukv, w_out, ffn_norm, w_up, conv_w, conv_b, w_down, final_norm, loss_target, m_mix_norm, m_w_in, m_a_v_norm_g, m_a_v_norm_b, m_a_spatial_w, m_a_spatial_b, m_q_a_norm, m_w_uq, m_kv_a_norm, m_w_ukv, m_w_out, m_ffn_norm, m_w_up, m_conv_w, m_conv_b, m_w_down, m_final_norm, v_mix_norm, v_w_in, v_a_v_norm_g, v_a_v_norm_b, v_a_spatial_w, v_a_spatial_b, v_q_a_norm, v_w_uq, v_kv_a_norm, v_w_ukv, v_w_out, v_ffn_norm, v_w_up, v_conv_w, v_conv_b, v_w_down, v_final_norm):
    weights = dict(mix_norm=mix_norm, w_in=w_in, a_v_norm_g=a_v_norm_g, a_v_norm_b=a_v_norm_b, a_spatial_w=a_spatial_w,
                   a_spatial_b=a_spatial_b, q_a_norm=q_a_norm, w_uq=w_uq, kv_a_norm=kv_a_norm, w_ukv=w_ukv, w_out=w_out,
                   ffn_norm=ffn_norm, w_up=w_up, conv_w=conv_w, conv_b=conv_b, w_down=w_down, final_norm=final_norm)
    m_in = dict(mix_norm=m_mix_norm, w_in=m_w_in, a_v_norm_g=m_a_v_norm_g, a_v_norm_b=m_a_v_norm_b,
                a_spatial_w=m_a_spatial_w, a_spatial_b=m_a_spatial_b, q_a_norm=m_q_a_norm, w_uq=m_w_uq,
                kv_a_norm=m_kv_a_norm, w_ukv=m_w_ukv, w_out=m_w_out, ffn_norm=m_ffn_norm, w_up=m_w_up, conv_w=m_conv_w,
                conv_b=m_conv_b, w_down=m_w_down, final_norm=m_final_norm)
    v_in = dict(mix_norm=v_mix_norm, w_in=v_w_in, a_v_norm_g=v_a_v_norm_g, a_v_norm_b=v_a_v_norm_b,
                a_spatial_w=v_a_spatial_w, a_spatial_b=v_a_spatial_b, q_a_norm=v_q_a_norm, w_uq=v_w_uq,
                kv_a_norm=v_kv_a_norm, w_ukv=v_w_ukv, w_out=v_w_out, ffn_norm=v_ffn_norm, w_up=v_w_up, conv_w=v_conv_w,
                conv_b=v_conv_b, w_down=v_w_down, final_norm=v_final_norm)
    names = list(weights)
    chip = 2 * lax.axis_index("x") + lax.axis_index("y")

    def halves(a):
        return a.reshape(a.shape[:-2] + (2, a.shape[-2] // 2, a.shape[-1]))

    w_in_t = jnp.swapaxes(w_in[0], 0, 1).astype(MXU_DTYPE)
    w_in_gather = _exchange_start([jnp.stack(jnp.split(w_in_t, 2, axis=1))], "w_in_gather_start", "halves",
                                  after=positions)
    gathers = {}
    wts = dict(
        mix_norm=mix_norm, a_v_norm_g=a_v_norm_g, a_v_norm_b=a_v_norm_b, a_spatial_w=a_spatial_w[0],
        a_spatial_b=a_spatial_b[0], q_a_norm=q_a_norm, kv_a_norm=kv_a_norm, ffn_norm=ffn_norm,
        final_norm=final_norm.reshape(1, D_MODEL), conv_b=conv_b.reshape(2, 1, D_FF))

    mixer_shards = [weights[n][0].astype(MXU_DTYPE) for n in _BIG[1:4]]
    ffn_shards = [w_up[0].astype(MXU_DTYPE), w_down[0].astype(MXU_DTYPE)]

    def in_weights(after):
        _, landed = _exchange_wait(w_in_gather, "w_in_gather_wait", "halves", list(after) + mixer_shards + ffn_shards)
        (w_in_sh,) = _forward_halves(list(landed))
        gathers["mixer"] = _exchange_start(mixer_shards, "mixer_gather_start", "gather", after=w_in_sh)
        gathers["ffn"] = _exchange_start(ffn_shards + [conv_w[0]], "ffn_gather_start", "gather",
                                         after=gathers["mixer"][4])
        w_in_pad = _w_in_t_to_pad(jnp.concatenate([w_in_sh[:, 0], w_in_sh[:, 1]], axis=-1).reshape(-1, D_MODEL))
        return w_in_pad, gathers["ffn"][4]

    def mixer_weights(after):
        _, (w_uq_sh, w_ukv_sh, w_out_sh) = _exchange_wait(gathers["mixer"], "mixer_gather_wait", "gather", after)
        return (_w_uq_to_pad(_cols_from_chips(w_uq_sh)), _w_ukv_to_pad(_cols_from_chips(w_ukv_sh)),
                w_out_sh.reshape(D_MODEL, D_MODEL))

    def ffn_weights(after):
        _, (w_up_sh, w_down_sh, cw_all) = _exchange_wait(gathers["ffn"], "ffn_gather_wait", "gather", after)
        return w_up_sh, w_down_sh.reshape(D_FF, D_MODEL), _conv_w_split(_cols_from_chips(cw_all))

    scatters = {}

    def start_scatter(slabs, slabs_lo, tag):
        got = _swap_halves([halves(s) for s in slabs_lo], tag + "_grad_swap_halves")
        sums = _pair_sum(slabs, got, tag + "_grad_pair_sum")
        scatters[tag] = _exchange_start(list(sums), tag + "_scatter_start", "scatter", after=slabs[-1])
        return scatters[tag][4]

    def on_ffn_grads(g):
        slabs, slabs_lo = [[g["w_up" + lo], g["w_down" + lo].reshape(N_CHIPS, D_FF // N_CHIPS, D_MODEL)]
                           for lo in ("", "_lo")]
        swap = _exchange_start([halves(s) for s in slabs_lo], "ffn_swap_start", "swap", after=slabs[1])

        def sent(after):
            _, got = _exchange_wait(swap, "ffn_swap_wait", "swap", after)
            sums = _pair_sum(slabs, got, "ffn_grad_pair_sum")
            scatters["ffn"] = _exchange_start(list(sums), "ffn_scatter_start", "scatter", after=got[0])
            return scatters["ffn"][4]

        return swap[4], sent

    def on_mixer_grads(g):
        slabs = [_w_in_t_from_pad(g["w_in"]).reshape(N_CHIPS, -1, D_MODEL), _cols_to_chips(_w_uq_from_pad(g["w_q"])),
                 _cols_to_chips(_w_ukv_from_pad(g["w_kv"]))]
        w_out_slabs = [g["w_out" + lo].reshape(N_CHIPS, D_MODEL // N_CHIPS, D_MODEL) for lo in ("", "_lo")]
        return start_scatter(slabs + w_out_slabs[:1], [s.astype(GRAD_PAYLOAD) for s in slabs] + w_out_slabs[1:], "mixer")

    loss_part, grad_x, g = _local_step(x, positions, loss_target, wts, in_weights, mixer_weights, ffn_weights,
                                       on_ffn_grads, on_mixer_grads)

    g_small_parts = dict(g)
    g_small_parts["conv_w"] = _conv_w_join(g["conv_w"])
    g_small_parts["conv_b"] = g["conv_b"].reshape(1, 2 * D_FF)
    flat = jnp.concatenate([g_small_parts[n].reshape(-1) for n, _ in _SMALL] + [loss_part.reshape(1)])
    flat = jnp.pad(flat, (0, _SMALL_ROWS * 128 - flat.shape[0])).reshape(_SMALL_ROWS, 128)
    small_gather = _exchange_start([flat], "small_gather_start", "all", after=grad_x)

    mixer_sums, mixer_landed = _exchange_wait(scatters["mixer"], "mixer_scatter_wait", "scatter", after=small_gather[4])
    ffn_sums, ffn_landed = _exchange_wait(scatters["ffn"], "ffn_scatter_wait", "scatter", after=mixer_landed[0])
    reduced = _chip_sum(list(mixer_sums) + list(ffn_sums), list(mixer_landed) + list(ffn_landed))
    g_big = dict(zip(_BIG, _join_halves(reduced)))

    grads, deltas, new_m, new_v = {}, {}, {}, {}

    def update(n, grad, copy_grad=False):
        w = weights[n]
        shape2 = grad.shape
        d, nm, nv, *again = _adamw(w.reshape(shape2), grad, m_in[n].reshape(shape2), v_in[n].reshape(shape2),
                                   "adamw_" + n, copy_grad)
        grads[n], deltas[n], new_m[n], new_v[n] = (t.reshape(w.shape) for t in (again[0] if copy_grad else grad, d, nm, nv))

    def update_transposed(n, grad_t):
        t = lambda a: jnp.swapaxes(a, 1, 2)
        d, nm, nv, again = _adamw(t(weights[n]), grad_t, t(m_in[n]), t(v_in[n]), "adamw_" + n, True)
        grads[n], deltas[n], new_m[n], new_v[n] = t(again), t(d), t(nm), t(nv)

    for n in _BIG:
        g3 = g_big[n].reshape((1, -1, g_big[n].shape[-1]))
        if n == "w_in":
            update_transposed(n, g3)
        else:
            update(n, g3, copy_grad=True)

    (own,), (everyone,) = _exchange_wait(small_gather, "small_gather_wait", "all", after=[deltas[n] for n in _BIG])
    device = 2 * chip + lax.axis_index("c")
    everyone = lax.dynamic_update_slice(everyone, own[None], (device, 0, 0))
    total = _sum_slabs([everyone[j] for j in range(8)], "small_grads_sum", tr=_SMALL_ROWS).reshape(-1)
    o = 0
    for n, shp in _SMALL:
        piece = total[o:o + math.prod(shp)].reshape(shp)
        o += math.prod(shp)
        if n == "conv_w":
            piece = lax.dynamic_slice_in_dim(piece, chip * UP_SHARD, UP_SHARD, axis=1)
        update(n, piece)
    loss = total[_SMALL_SIZE]
    return (loss, grad_x, *[grads[n] for n in names], *[deltas[n] for n in names], *[new_m[n] for n in names],
            *[new_v[n] for n in names])
```

```python
import functools
import math

import jax
import jax.numpy as jnp
from jax import lax
from jax.experimental import pallas as pl
from jax.experimental.pallas import tpu as pltpu

F32 = jnp.float32
MXU_DTYPE = jnp.bfloat16
MESH = pl.DeviceIdType.MESH

D_MODEL = 1024
EPS = 1e-6
A_GROUPS = 8
CHUNK = 128
HEADS = 8
NOPE = 128
ROPE = 64
QK_DIM = NOPE + ROPE
HEAD_PAD = 256
Q_RANK = 256
KV_RANK = 128
ROPE_THETA = 10000.0
D_FF = 2816
FF_TILE = 256
N_FF_TILES = D_FF // FF_TILE
LAT = 512
IN_PAD = 4 * D_MODEL + LAT
N_CHIPS = 4
ADAM_LR, ADAM_B1, ADAM_B2, ADAM_EPS, ADAM_WD, ADAM_STEP = 0.001, 0.9, 0.999, 1e-08, 0.01, 10

VMEM_CAP_V7X = 64 * 1024 * 1024
NEG = -1e30


def _params(sem, nbytes):
    limit = int(min(VMEM_CAP_V7X - (8 << 20), max(32 << 20, 3 * nbytes)))
    return pltpu.CompilerParams(dimension_semantics=sem, vmem_limit_bytes=limit)


def _nbytes(shape, dtype):
    return math.prod(shape) * jnp.dtype(dtype).itemsize


_DIMS = {"nn": (((1,), (0,)), ((), ())), "nt": (((1,), (1,)), ((), ())), "tn": (((0,), (0,)), ((), ()))}


def _mm(a, b, mode, name, *, tm, tn, tk, out_dtype=F32, add=None, dims=None, a_spec=None, b_spec=None,
        o_spec=None, out_shape=None, n_outer=False, copy_dtype=None, after=None):
    if dims is None:
        if mode == "nn":
            (M, K), (_, N) = a.shape, b.shape
        elif mode == "nt":
            (M, K), (N, _) = a.shape, b.shape
        else:
            (K, M), (_, N) = a.shape, b.shape
    else:
        M, N, K = dims
    a_blk = (tk, tm) if mode == "tn" else (tm, tk)
    b_blk = (tn, tk) if mode == "nt" else (tk, tn)
    if a_spec is None:
        a_spec = pl.BlockSpec(a_blk, (lambda i, j, k: (k, i)) if mode == "tn" else (lambda i, j, k: (i, k)))
    if b_spec is None:
        b_spec = pl.BlockSpec(b_blk, (lambda i, j, k: (j, k)) if mode == "nt" else (lambda i, j, k: (k, j)))
    if o_spec is None:
        o_spec = pl.BlockSpec((tm, tn), lambda i, j, k: (i, j))
    if out_shape is None:
        out_shape = (M, N)
    assert M % tm == 0 and N % tn == 0 and K % tk == 0, (name, M, N, K, tm, tn, tk)
    nk = K // tk
    contract = _DIMS[mode]
    has_add = add is not None

    def body(*refs):
        a_ref, b_ref = refs[0], refs[1]
        add_ref = refs[2] if has_add else None
        n_in = 2 + has_add + (after is not None)
        o_ref = refs[n_in]
        copy_ref = refs[n_in + 1] if copy_dtype is not None else None

        def product():
            return lax.dot_general(a_ref[...].astype(MXU_DTYPE), b_ref[...].astype(MXU_DTYPE), contract,
                                   preferred_element_type=F32)

        def finish(r):
            if has_add:
                r = r + add_ref[...]
            o_ref[...] = r.astype(out_dtype)
            if copy_ref is not None:
                copy_ref[...] = r.astype(copy_dtype)

        if nk == 1:
            finish(product())
            return
        acc = refs[-1]
        k = pl.program_id(2)

        @pl.when(k == 0)
        def _():
            acc[...] = jnp.zeros_like(acc)

        acc[...] += product()

        @pl.when(k == nk - 1)
        def _():
            finish(acc[...])

    in_specs = [a_spec, b_spec]
    args = [a, b]
    nbytes = _nbytes(a_blk, a.dtype) + _nbytes(b_blk, b.dtype) + 3 * _nbytes((tm, tn), F32)
    if has_add:
        in_specs.append(pl.BlockSpec((tm, tn), lambda i, j, k: (i, j)))
        args.append(add)
        nbytes += _nbytes((tm, tn), F32)
    if after is not None:
        in_specs.append(pl.BlockSpec(after.shape, lambda i, j, k: (0, 0)))
        args.append(after)
    grid = (M // tm, N // tn, nk)
    if n_outer:
        def swapped(spec):
            return pl.BlockSpec(spec.block_shape, lambda j, i, k, at=spec.index_map: at(i, j, k))

        grid = (N // tn, M // tm, nk)
        in_specs = [swapped(s) for s in in_specs]
        o_spec = swapped(o_spec)
    out_sds, out_specs = jax.ShapeDtypeStruct(out_shape, out_dtype), o_spec
    if copy_dtype is not None:
        out_sds, out_specs = (out_sds, jax.ShapeDtypeStruct(out_shape, copy_dtype)), (o_spec, o_spec)
    return pl.pallas_call(
        body, name=name, out_shape=out_sds, grid=grid, in_specs=in_specs, out_specs=out_specs,
        scratch_shapes=[pltpu.VMEM((tm, tn), F32)] if nk > 1 else [],
        compiler_params=_params(("parallel", "parallel", "arbitrary"), nbytes),
    )(*args)


_GELU_C = math.sqrt(2.0 / math.pi)
_GELU_A = 0.044715


def _sigmoid(x):
    return 0.5 * jnp.tanh(0.5 * x) + 0.5


def _gelu(x):
    t = jnp.tanh(x * (_GELU_C + (_GELU_C * _GELU_A) * (x * x)))
    return x * (0.5 + 0.5 * t)


def _gelu_and_grad(x):
    x2 = x * x
    t = jnp.tanh(x * (_GELU_C + (_GELU_C * _GELU_A) * x2))
    cdf = 0.5 + 0.5 * t
    grad = cdf + (0.5 * x) * (1.0 - t * t) * (_GELU_C + (3.0 * _GELU_C * _GELU_A) * x2)
    return x * cdf, grad


def _rope_mix(g, cos_a, sin_a):
    return g * cos_a + pltpu.roll(g, 64, 1) * sin_a


def _rope_mix_bwd(d, cos_a, sin_a):
    return d * cos_a + pltpu.roll(d * sin_a, 64, 1)


def _rms_fwd(x, g, name, tr=512):
    T, D = x.shape

    def body(x_ref, g_ref, h_ref):
        xv = x_ref[...]
        r = lax.rsqrt(jnp.mean(xv * xv, axis=-1, keepdims=True) + EPS)
        h_ref[...] = ((xv * r) * g_ref[...]).astype(h_ref.dtype)

    return pl.pallas_call(
        body, name=name, out_shape=jax.ShapeDtypeStruct((T, D), MXU_DTYPE), grid=(T // tr,),
        in_specs=[pl.BlockSpec((tr, D), lambda i: (i, 0)), pl.BlockSpec((1, D), lambda i: (0, 0))],
        out_specs=pl.BlockSpec((tr, D), lambda i: (i, 0)),
        compiler_params=_params(("parallel",), 3 * _nbytes((tr, D), F32)),
    )(x, g)


def _rms_bwd(x, g, dh, dres, name, tr=512):
    T, D = x.shape

    def body(x_ref, g_ref, dh_ref, dres_ref, dx_ref, gg_ref):
        @pl.when(pl.program_id(0) == 0)
        def _():
            gg_ref[...] = jnp.zeros_like(gg_ref)

        xv = x_ref[...]
        r = lax.rsqrt(jnp.mean(xv * xv, axis=-1, keepdims=True) + EPS)
        xn = xv * r
        dhv = dh_ref[...]
        dxn = dhv * g_ref[...]
        dx_ref[...] = dres_ref[...] + r * (dxn - xn * jnp.mean(dxn * xn, axis=-1, keepdims=True))
        gg_ref[...] += jnp.sum(dhv * xn, axis=0, keepdims=True)

    row = pl.BlockSpec((tr, D), lambda i: (i, 0))
    vec = pl.BlockSpec((1, D), lambda i: (0, 0))
    return pl.pallas_call(
        body, name=name,
        out_shape=(jax.ShapeDtypeStruct((T, D), F32), jax.ShapeDtypeStruct((1, D), F32)),
        grid=(T // tr,), in_specs=[row, vec, row, row], out_specs=(row, vec),
        compiler_params=_params(("arbitrary",), 6 * _nbytes((tr, D), F32)),
    )(x, g, dh, dres)


def _lat_fwd(z, gq, gkv, wq, wkv, cos_a, sin_a, tr=256):
    T = z.shape[0]
    lat_blk = (4 * D_MODEL) // LAT

    def body(z_ref, gq_ref, gkv_ref, wq_ref, wkv_ref, cos_ref, sin_ref, q_ref, k_ref, v_ref, cqn_ref, ckvn_ref):
        zl = z_ref[...]
        cos_v, sin_v = cos_ref[...], sin_ref[...]
        cq = zl[:, :Q_RANK]
        ckv = zl[:, Q_RANK:Q_RANK + KV_RANK]
        krb = zl[:, Q_RANK + KV_RANK:]
        cqn = ((cq * lax.rsqrt(jnp.mean(cq * cq, axis=-1, keepdims=True) + EPS)) * gq_ref[...]).astype(MXU_DTYPE)
        ckvn = ((ckv * lax.rsqrt(jnp.mean(ckv * ckv, axis=-1, keepdims=True) + EPS)) * gkv_ref[...]).astype(MXU_DTYPE)
        cqn_ref[...] = cqn
        ckvn_ref[...] = ckvn
        krr = _rope_mix(krb, cos_v, sin_v).astype(MXU_DTYPE)
        q = jnp.dot(cqn, wq_ref[...], preferred_element_type=F32)
        kv = jnp.dot(ckvn, wkv_ref[...], preferred_element_type=F32)
        for h in range(HEADS):
            o = h * HEAD_PAD
            q_ref[:, o:o + NOPE] = q[:, o:o + NOPE].astype(MXU_DTYPE)
            q_ref[:, o + NOPE:o + HEAD_PAD] = _rope_mix(q[:, o + NOPE:o + HEAD_PAD], cos_v, sin_v).astype(MXU_DTYPE)
            k_ref[:, o:o + NOPE] = kv[:, h * NOPE:(h + 1) * NOPE].astype(MXU_DTYPE)
            k_ref[:, o + NOPE:o + HEAD_PAD] = krr
        v_ref[...] = kv[:, HEADS * NOPE:].astype(MXU_DTYPE)

    def row(w):
        return pl.BlockSpec((tr, w), lambda i: (i, 0))

    def full(a):
        return pl.BlockSpec(a.shape, lambda i: (0, 0))

    return pl.pallas_call(
        body, name="lat_fwd",
        out_shape=(jax.ShapeDtypeStruct((T, HEADS * HEAD_PAD), MXU_DTYPE), jax.ShapeDtypeStruct((T, HEADS * HEAD_PAD), MXU_DTYPE),
                   jax.ShapeDtypeStruct((T, HEADS * NOPE), MXU_DTYPE), jax.ShapeDtypeStruct((T, Q_RANK), MXU_DTYPE),
                   jax.ShapeDtypeStruct((T, KV_RANK), MXU_DTYPE)),
        grid=(T // tr,),
        in_specs=[pl.BlockSpec((tr, LAT), lambda i: (i, lat_blk)), full(gq), full(gkv), full(wq), full(wkv), row(128), row(128)],
        out_specs=(row(HEADS * HEAD_PAD), row(HEADS * HEAD_PAD), row(HEADS * NOPE), row(Q_RANK), row(KV_RANK)),
        compiler_params=_params(("parallel",), 8 * _nbytes((tr, HEADS * HEAD_PAD), F32)),
    )(z, gq, gkv, wq, wkv, cos_a, sin_a)


ATT_BLOCK = 256
_SCALE = QK_DIM ** -0.5


def _causal_mask(n):
    return lax.broadcasted_iota(jnp.int32, (n, n), 1) <= lax.broadcasted_iota(jnp.int32, (n, n), 0)


def _causal_mask_t(n):
    return lax.broadcasted_iota(jnp.int32, (n, n), 0) <= lax.broadcasted_iota(jnp.int32, (n, n), 1)


ATT_HEADS = 4


def _attn_fwd(q, k, v, B, S):
    tq = ATT_BLOCK
    nq = S // tq
    T = B * S
    hp, groups = ATT_HEADS, HEADS // ATT_HEADS

    def body(q_ref, k_ref, v_ref, o_ref, *lse_refs):
        qi = pl.program_id(2)
        qs = [q_ref[:, t * HEAD_PAD:(t + 1) * HEAD_PAD] for t in range(hp)]

        def scores(j, t):
            rows = pl.ds(pl.multiple_of(j * tq, tq), tq)
            return lax.dot_general(k_ref[rows, t * HEAD_PAD:(t + 1) * HEAD_PAD], qs[t], _DIMS["nt"],
                                   preferred_element_type=F32)

        def step(j, carry, last):
            rows = pl.ds(pl.multiple_of(j * tq, tq), tq)
            out = []
            for t in range(hp):
                m, l, acc, st = carry[t]
                st_next = st if last else scores(j + 1, t)
                st = st * _SCALE
                if last:
                    st = jnp.where(_causal_mask_t(tq), st, NEG)
                m_new = jnp.maximum(m, jnp.max(st, axis=0, keepdims=True))
                alpha = jnp.exp(m - m_new)
                p = jnp.exp(st - m_new)
                l = alpha * l + jnp.sum(p, axis=0, keepdims=True)
                acc = alpha * acc + lax.dot_general(v_ref[rows, t * NOPE:(t + 1) * NOPE], p.astype(MXU_DTYPE),
                                                    _DIMS["tn"], preferred_element_type=F32)
                out.append((m_new, l, acc, st_next))
            return tuple(out)

        init = tuple((jnp.full((1, tq), NEG, F32), jnp.zeros((1, tq), F32), jnp.zeros((NOPE, tq), F32), scores(0, t))
                     for t in range(hp))
        carry = lax.fori_loop(0, qi, lambda j, c: step(j, c, False), init)
        carry = step(qi, carry, True)
        for t in range(hp):
            m, l, acc, _ = carry[t]
            o_ref[:, t * NOPE:(t + 1) * NOPE] = (acc / l).T
            lse_refs[t][0] = m + jnp.log(l)

    lse_sds = jax.ShapeDtypeStruct((groups * B * nq, 1, tq), F32)
    lse_spec = pl.BlockSpec((1, 1, tq), lambda b, h, i: ((h * B + b) * nq + i, 0, 0))
    return pl.pallas_call(
        body, name="attn_fwd",
        out_shape=(jax.ShapeDtypeStruct((T, HEADS * NOPE), F32),) + (lse_sds,) * hp,
        grid=(B, groups, nq),
        in_specs=[pl.BlockSpec((tq, hp * HEAD_PAD), lambda b, h, i: (b * nq + i, h)),
                  pl.BlockSpec((S, hp * HEAD_PAD), lambda b, h, i: (b, h)),
                  pl.BlockSpec((S, hp * NOPE), lambda b, h, i: (b, h))],
        out_specs=(pl.BlockSpec((tq, hp * NOPE), lambda b, h, i: (b * nq + i, h)),) + (lse_spec,) * hp,
        compiler_params=_params(("parallel", "parallel", "arbitrary"), 4 * hp * _nbytes((S, HEAD_PAD), MXU_DTYPE)),
    )(q, k, v)


def _attn_bwd(q, k, v, do, lses, delta, B, S):
    tq = ATT_BLOCK
    nq = S // tq
    T = B * S
    hp, groups = ATT_HEADS, HEADS // ATT_HEADS

    def body(q_ref, k_ref, v_ref, do_ref, *refs):
        lse_refs, dl_refs = refs[:hp], refs[hp:2 * hp]
        dq_out, dk_ref, dv_ref, dq_ref = refs[2 * hp:]
        kj = pl.program_id(2)

        @pl.when(kj == 0)
        def _():
            dq_ref[...] = jnp.zeros_like(dq_ref)

        def products(i, t):
            rows = pl.ds(pl.multiple_of(i * tq, tq), tq)
            st = lax.dot_general(k_ref[:, t * HEAD_PAD:(t + 1) * HEAD_PAD], q_ref[rows, t * HEAD_PAD:(t + 1) * HEAD_PAD],
                                 _DIMS["nt"], preferred_element_type=F32)
            dpt = lax.dot_general(v_ref[:, t * NOPE:(t + 1) * NOPE], do_ref[rows, t * NOPE:(t + 1) * NOPE],
                                  _DIMS["nt"], preferred_element_type=F32)
            return st, dpt

        def step(i, carry, masked):
            rows = pl.ds(pl.multiple_of(i * tq, tq), tq)
            nxt = jnp.minimum(i + 1, nq - 1)
            out = []
            for t in range(hp):
                dk, dv, st, dpt = carry[t]
                st_next, dpt_next = products(nxt, t)
                qk_cols = slice(t * HEAD_PAD, (t + 1) * HEAD_PAD)
                v_cols = slice(t * NOPE, (t + 1) * NOPE)
                p = jnp.exp(st * _SCALE - lse_refs[t][i])
                if masked:
                    p = jnp.where(_causal_mask_t(tq), p, 0.0)
                dv = dv + jnp.dot(p.astype(MXU_DTYPE), do_ref[rows, v_cols], preferred_element_type=F32)
                ds = (p * (dpt - dl_refs[t][i]) * _SCALE).astype(MXU_DTYPE)
                dk = dk + jnp.dot(ds, q_ref[rows, qk_cols], preferred_element_type=F32)
                dq_ref[rows, qk_cols] += lax.dot_general(ds, k_ref[:, qk_cols], _DIMS["tn"], preferred_element_type=F32)
                out.append((dk, dv, st_next, dpt_next))
            return tuple(out)

        init = tuple((jnp.zeros((tq, HEAD_PAD), F32), jnp.zeros((tq, NOPE), F32)) + products(kj, t) for t in range(hp))
        carry = step(kj, init, True)
        carry = lax.fori_loop(kj + 1, nq, lambda i, c: step(i, c, False), carry)
        for t in range(hp):
            dk_ref[:, t * HEAD_PAD:(t + 1) * HEAD_PAD] = carry[t][0].astype(dk_ref.dtype)
            dv_ref[:, t * NOPE:(t + 1) * NOPE] = carry[t][1].astype(dv_ref.dtype)

        @pl.when(kj == nq - 1)
        def _():
            dq_out[...] = dq_ref[...].astype(dq_out.dtype)

    seq = lambda w: pl.BlockSpec((S, w), lambda b, h, j: (b, h))
    blk = lambda w: pl.BlockSpec((tq, w), lambda b, h, j: (b * nq + j, h))
    lse_spec = pl.BlockSpec((nq, 1, tq), lambda b, h, j: (h * B + b, 0, 0))
    dl_specs = [pl.BlockSpec((nq, 1, tq), lambda b, h, j, t=t: ((h * hp + t) * B + b, 0, 0)) for t in range(hp)]
    return pl.pallas_call(
        body, name="attn_bwd",
        out_shape=(jax.ShapeDtypeStruct((T, HEADS * HEAD_PAD), MXU_DTYPE), jax.ShapeDtypeStruct((T, HEADS * HEAD_PAD), MXU_DTYPE),
                   jax.ShapeDtypeStruct((T, HEADS * NOPE), MXU_DTYPE)),
        grid=(B, groups, nq),
        in_specs=[seq(hp * HEAD_PAD), blk(hp * HEAD_PAD), blk(hp * NOPE), seq(hp * NOPE)] + [lse_spec] * hp + dl_specs,
        out_specs=(seq(hp * HEAD_PAD), blk(hp * HEAD_PAD), blk(hp * NOPE)),
        scratch_shapes=[pltpu.VMEM((S, hp * HEAD_PAD), F32)],
        compiler_params=_params(("parallel", "parallel", "arbitrary"), 8 * hp * _nbytes((S, HEAD_PAD), F32)),
    )(q, k, v, do, *lses, *([delta] * hp))


MIX_ROWS = 256


def _tril_weights(ws_ref, g):
    return jnp.where(_causal_mask(CHUNK), ws_ref[g], 0.0).astype(MXU_DTYPE)


def _layer_norm_stats(va):
    mu = jnp.mean(va, axis=-1, keepdims=True)
    xc = va - mu
    rs = lax.rsqrt(jnp.mean(xc * xc, axis=-1, keepdims=True) + EPS)
    return xc * rs


def _mix_specs(tr):
    zcol = lambda c: pl.BlockSpec((tr, D_MODEL), lambda i, c=c: (i, c))
    row = pl.BlockSpec((tr, D_MODEL), lambda i: (i, 0))
    vec = pl.BlockSpec((1, D_MODEL), lambda i: (0, 0))
    ws = pl.BlockSpec((A_GROUPS, CHUNK, CHUNK), lambda i: (0, 0, 0))
    bs = pl.BlockSpec((CHUNK, 128), lambda i: (0, 0))
    return zcol, row, vec, ws, bs


def _mix_fwd(z, yb, ln_g, ln_b, ws, bs_t):
    T = z.shape[0]
    tr = MIX_ROWS
    zcol, row, vec, ws_spec, bs_spec = _mix_specs(tr)

    def body(zu_ref, zv_ref, zga_ref, zgb_ref, yb_ref, g_ref, b_ref, ws_ref, bs_ref, out_ref, vn_s):
        vhat = _layer_norm_stats(_gelu(zv_ref[...]))
        vn_s[...] = (vhat * g_ref[...] + b_ref[...]).astype(MXU_DTYPE)
        for g in range(A_GROUPS):
            w = _tril_weights(ws_ref, g)
            bias = bs_ref[:, g:g + 1]
            cols = slice(g * CHUNK, (g + 1) * CHUNK)
            for c in range(tr // CHUNK):
                rows = slice(c * CHUNK, (c + 1) * CHUNK)
                mixed = jnp.dot(w, vn_s[rows, cols], preferred_element_type=F32) + bias
                ya = _gelu(zu_ref[rows, cols]) * mixed
                merged = _sigmoid(zga_ref[rows, cols]) * ya + _sigmoid(zgb_ref[rows, cols]) * yb_ref[rows, cols]
                out_ref[rows, cols] = merged.astype(MXU_DTYPE)

    return pl.pallas_call(
        body, name="mix_fwd", out_shape=jax.ShapeDtypeStruct((T, D_MODEL), MXU_DTYPE), grid=(T // tr,),
        in_specs=[zcol(0), zcol(1), zcol(2), zcol(3), row, vec, vec, ws_spec, bs_spec], out_specs=row,
        scratch_shapes=[pltpu.VMEM((tr, D_MODEL), MXU_DTYPE)],
        compiler_params=_params(("parallel",), 8 * _nbytes((tr, D_MODEL), F32)),
    )(z, z, z, z, yb, ln_g, ln_b, ws, bs_t)


def _mix_bwd(z, yb, dm, ln_g, ln_b, ws, bs_t):
    T = z.shape[0]
    tr = MIX_ROWS
    zcol, row, vec, ws_spec, bs_spec = _mix_specs(tr)

    def body(zu_ref, zv_ref, zga_ref, zgb_ref, yb_ref, dm_ref, g_ref, b_ref, ws_ref, bs_ref,
             dz_ref, dyb_ref, dl_ref, gws_ref, gbs_ref, glg_ref, glb_ref, vn_s, dvn_s):
        @pl.when(pl.program_id(0) == 0)
        def _():
            gws_ref[...] = jnp.zeros_like(gws_ref)
            gbs_ref[...] = jnp.zeros_like(gbs_ref)
            glg_ref[...] = jnp.zeros_like(glg_ref)
            glb_ref[...] = jnp.zeros_like(glb_ref)

        lane = lax.broadcasted_iota(jnp.int32, (CHUNK, 128), 1)
        va, dgelu_v = _gelu_and_grad(zv_ref[...])
        mu = jnp.mean(va, axis=-1, keepdims=True)
        xc = va - mu
        rs = lax.rsqrt(jnp.mean(xc * xc, axis=-1, keepdims=True) + EPS)
        vhat = xc * rs
        vn_s[...] = (vhat * g_ref[...] + b_ref[...]).astype(MXU_DTYPE)
        gbs_acc = jnp.zeros((CHUNK, 128), F32)
        for g in range(A_GROUPS):
            w = _tril_weights(ws_ref, g)
            bias = bs_ref[:, g:g + 1]
            cols = slice(g * CHUNK, (g + 1) * CHUNK)
            gw_acc = jnp.zeros((CHUNK, CHUNK), F32)
            for c in range(tr // CHUNK):
                rows = slice(c * CHUNK, (c + 1) * CHUNK)
                vn = vn_s[rows, cols]
                mixed = jnp.dot(w, vn, preferred_element_type=F32) + bias
                ua, dgelu_u = _gelu_and_grad(zu_ref[rows, cols])
                dmv = dm_ref[rows, cols]
                sa = _sigmoid(zga_ref[rows, cols])
                dya = dmv * sa
                dz_ref[rows, 2 * D_MODEL + g * CHUNK:2 * D_MODEL + (g + 1) * CHUNK] = (
                    dmv * (ua * mixed) * (sa * (1.0 - sa))).astype(dz_ref.dtype)
                dz_ref[rows, cols] = (dya * mixed * dgelu_u).astype(dz_ref.dtype)
                dmix = dya * ua
                gbs_acc = gbs_acc + jnp.where(lane == g, jnp.sum(dmix, axis=-1, keepdims=True), 0.0)
                dmix_b = dmix.astype(MXU_DTYPE)
                gw_acc = gw_acc + lax.dot_general(dmix_b, vn, _DIMS["nt"], preferred_element_type=F32)
                dvn_s[rows, cols] = lax.dot_general(w, dmix_b, _DIMS["tn"], preferred_element_type=F32)
            gws_ref[g] += jnp.where(_causal_mask(CHUNK), gw_acc, 0.0)
        gbs_ref[...] += gbs_acc

        dvn = dvn_s[...]
        glg_ref[...] += jnp.sum(dvn * vhat, axis=0, keepdims=True)
        glb_ref[...] += jnp.sum(dvn, axis=0, keepdims=True)
        dvh = dvn * g_ref[...]
        dva = rs * (dvh - jnp.mean(dvh, axis=-1, keepdims=True) - vhat * jnp.mean(dvh * vhat, axis=-1, keepdims=True))
        dz_ref[:, D_MODEL:2 * D_MODEL] = (dva * dgelu_v).astype(dz_ref.dtype)

        dmv = dm_ref[...]
        ybv = yb_ref[...]
        sb = _sigmoid(zgb_ref[...])
        dyb = dmv * sb
        dyb_ref[...] = dyb.astype(dyb_ref.dtype)
        dz_ref[:, 3 * D_MODEL:4 * D_MODEL] = (dmv * ybv * (sb * (1.0 - sb))).astype(dz_ref.dtype)
        dz_ref[:, 4 * D_MODEL:] = jnp.zeros((tr, LAT), dz_ref.dtype)
        prod = dyb * ybv
        sel = (lax.broadcasted_iota(jnp.int32, (HEADS, D_MODEL), 1) // NOPE
               == lax.broadcasted_iota(jnp.int32, (HEADS, D_MODEL), 0)).astype(jnp.bfloat16)
        hi = prod.astype(jnp.bfloat16)
        rest = prod - hi.astype(F32)
        mid = rest.astype(jnp.bfloat16)
        lo = (rest - mid.astype(F32)).astype(jnp.bfloat16)
        dl_ref[...] = (lax.dot_general(sel, hi, _DIMS["nt"], preferred_element_type=F32)
                       + lax.dot_general(sel, mid, _DIMS["nt"], preferred_element_type=F32)
                       + lax.dot_general(sel, lo, _DIMS["nt"], preferred_element_type=F32))

    return pl.pallas_call(
        body, name="mix_bwd",
        out_shape=(jax.ShapeDtypeStruct((T, IN_PAD), MXU_DTYPE), jax.ShapeDtypeStruct((T, D_MODEL), MXU_DTYPE),
                   jax.ShapeDtypeStruct((HEADS, T), F32), jax.ShapeDtypeStruct((A_GROUPS, CHUNK, CHUNK), F32),
                   jax.ShapeDtypeStruct((CHUNK, 128), F32), jax.ShapeDtypeStruct((1, D_MODEL), F32),
                   jax.ShapeDtypeStruct((1, D_MODEL), F32)),
        grid=(T // tr,),
        in_specs=[zcol(0), zcol(1), zcol(2), zcol(3), row, row, vec, vec, ws_spec, bs_spec],
        out_specs=(pl.BlockSpec((tr, IN_PAD), lambda i: (i, 0)), row, pl.BlockSpec((HEADS, tr), lambda i: (0, i)),
                   ws_spec, bs_spec, vec, vec),
        scratch_shapes=[pltpu.VMEM((tr, D_MODEL), MXU_DTYPE), pltpu.VMEM((tr, D_MODEL), F32)],
        compiler_params=_params(("arbitrary",), 12 * _nbytes((tr, D_MODEL), F32)),
    )(z, z, z, z, yb, dm, ln_g, ln_b, ws, bs_t)


def _lat_bwd(dz, z, dq, dk, dv, gq, gkv, wq, wkv, cos_a, sin_a, tr=256):
    T = z.shape[0]
    lat_blk = (4 * D_MODEL) // LAT

    def body(dz_in, z_ref, dq_ref, dk_ref, dv_ref, gq_ref, gkv_ref, wq_ref, wkv_ref, cos_ref, sin_ref,
             dz_ref, dqr_ref, dkv_ref, ggq_ref, ggkv_ref):
        del dz_in

        @pl.when(pl.program_id(0) == 0)
        def _():
            ggq_ref[...] = jnp.zeros_like(ggq_ref)
            ggkv_ref[...] = jnp.zeros_like(ggkv_ref)

        cos_v, sin_v = cos_ref[...], sin_ref[...]
        dkr = jnp.zeros((tr, 128), F32)
        for h in range(HEADS):
            o = h * HEAD_PAD
            dqr_ref[:, o:o + NOPE] = dq_ref[:, o:o + NOPE].astype(MXU_DTYPE)
            dqr_ref[:, o + NOPE:o + HEAD_PAD] = _rope_mix_bwd(dq_ref[:, o + NOPE:o + HEAD_PAD], cos_v, sin_v).astype(MXU_DTYPE)
            dkv_ref[:, h * NOPE:(h + 1) * NOPE] = dk_ref[:, o:o + NOPE].astype(MXU_DTYPE)
            dkr = dkr + _rope_mix_bwd(dk_ref[:, o + NOPE:o + HEAD_PAD], cos_v, sin_v)
        dkv_ref[:, HEADS * NOPE:] = dv_ref[...]
        dcqn = lax.dot_general(dqr_ref[...], wq_ref[...], _DIMS["nt"], preferred_element_type=F32)
        dckvn = lax.dot_general(dkv_ref[...], wkv_ref[...], _DIMS["nt"], preferred_element_type=F32)

        zl = z_ref[...]

        def rms_bwd(c, dn, g_ref, gg_ref):
            r = lax.rsqrt(jnp.mean(c * c, axis=-1, keepdims=True) + EPS)
            ch = c * r
            gg_ref[...] += jnp.sum(dn * ch, axis=0, keepdims=True)
            dch = dn * g_ref[...]
            return r * (dch - ch * jnp.mean(dch * ch, axis=-1, keepdims=True))

        dz_ref[:, :Q_RANK] = rms_bwd(zl[:, :Q_RANK], dcqn, gq_ref, ggq_ref).astype(dz_ref.dtype)
        dz_ref[:, Q_RANK:Q_RANK + KV_RANK] = rms_bwd(zl[:, Q_RANK:Q_RANK + KV_RANK], dckvn, gkv_ref, ggkv_ref).astype(dz_ref.dtype)
        dz_ref[:, Q_RANK + KV_RANK:] = dkr.astype(dz_ref.dtype)

    def row(w):
        return pl.BlockSpec((tr, w), lambda i: (i, 0))

    def full(a):
        return pl.BlockSpec(a.shape, lambda i: (0, 0))

    lat = pl.BlockSpec((tr, LAT), lambda i: (i, lat_blk))
    return pl.pallas_call(
        body, name="lat_bwd",
        out_shape=(jax.ShapeDtypeStruct(dz.shape, dz.dtype), jax.ShapeDtypeStruct((T, HEADS * HEAD_PAD), MXU_DTYPE),
                   jax.ShapeDtypeStruct((T, 2 * HEADS * NOPE), MXU_DTYPE), jax.ShapeDtypeStruct(gq.shape, F32),
                   jax.ShapeDtypeStruct(gkv.shape, F32)),
        grid=(T // tr,),
        in_specs=[pl.BlockSpec(memory_space=pl.ANY), lat, row(HEADS * HEAD_PAD), row(HEADS * HEAD_PAD), row(HEADS * NOPE),
                  full(gq), full(gkv), full(wq), full(wkv), row(128), row(128)],
        out_specs=(lat, row(HEADS * HEAD_PAD), row(2 * HEADS * NOPE), full(gq), full(gkv)),
        input_output_aliases={0: 0},
        compiler_params=_params(("arbitrary",), 8 * _nbytes((tr, HEADS * HEAD_PAD), F32)),
    )(dz, z, dq, dk, dv, gq, gkv, wq, wkv, cos_a, sin_a)


GATE_ROWS = 64
HALO = 8


def _taps(ref, half, r, first):
    C = GATE_ROWS
    if first:
        xs = jnp.concatenate([jnp.zeros((HALO, ref.shape[-1]), F32), ref[half, 0:C, :]], axis=0)
    else:
        xs = ref[half, pl.ds(pl.multiple_of(r * C - HALO, HALO), C + HALO), :]
    return xs[HALO:, :], pltpu.roll(xs, 1, 0)[HALO:, :], pltpu.roll(xs, 2, 0)[HALO:, :]


def _conv_taps(taps, cw, cb):
    x0, x1, x2 = taps
    return cb + cw[0:1, :] * x2 + cw[1:2, :] * x1 + cw[2:3, :] * x0


def _fold8(x):
    acc = x[0:8, :]
    for i in range(1, x.shape[0] // 8):
        acc = acc + x[8 * i:8 * (i + 1), :]
    return acc


def _gate_fwd(up3, conv_w, conv_b, B, S):
    T = B * S
    W = FF_TILE
    C = GATE_ROWS

    def body(up_ref, cw_ref, cb_ref, act_ref, conv_ref):
        def chunk(r, first):
            gate = _conv_taps(_taps(up_ref, 0, r, first), cw_ref[0], cb_ref[0])
            val = _conv_taps(_taps(up_ref, 1, r, first), cw_ref[1], cb_ref[1])
            rows = pl.ds(0 if first else pl.multiple_of(r * C, C), C)
            conv_ref[0, rows, :] = gate.astype(conv_ref.dtype)
            conv_ref[1, rows, :] = val.astype(conv_ref.dtype)
            act_ref[rows, :] = (gate * _sigmoid(gate) * val).astype(act_ref.dtype)

        chunk(0, True)

        @pl.loop(1, S // C)
        def _(r):
            chunk(r, False)

    up_spec = pl.BlockSpec((2, S, W), lambda b, j: (0, b, j))
    return pl.pallas_call(
        body, name="gate_fwd",
        out_shape=(jax.ShapeDtypeStruct((T, D_FF), MXU_DTYPE), jax.ShapeDtypeStruct((2, T, D_FF), MXU_DTYPE)),
        grid=(B, N_FF_TILES),
        in_specs=[up_spec, pl.BlockSpec((2, 3, W), lambda b, j: (0, 0, j)), pl.BlockSpec((2, 1, W), lambda b, j: (0, 0, j))],
        out_specs=(pl.BlockSpec((S, W), lambda b, j: (b, j)), up_spec),
        compiler_params=_params(("parallel", "parallel"), 8 * _nbytes((S, W), F32)),
    )(up3, conv_w, conv_b)


def _gate_bwd(up3, conv3, dact, conv_w, B, S):
    T = B * S
    W = FF_TILE
    C = GATE_ROWS

    def body(up_ref, conv_ref, da_ref, cw_ref, dup_ref, gcw_ref, gcb_ref, d_s):
        @pl.when(pl.program_id(1) == 0)
        def _():
            gcw_ref[...] = jnp.zeros_like(gcw_ref)
            gcb_ref[...] = jnp.zeros_like(gcb_ref)

        @pl.loop(0, S // C)
        def _(r):
            rows = pl.ds(pl.multiple_of(r * C, C), C)
            gate, val = conv_ref[0, rows, :].astype(F32), conv_ref[1, rows, :].astype(F32)
            sg = _sigmoid(gate)
            da = da_ref[rows, :]
            d_s[0, rows, :] = da * val * (sg * (1.0 + gate * (1.0 - sg)))
            d_s[1, rows, :] = da * (gate * sg)

        d_s[:, S:S + HALO, :] = jnp.zeros((2, HALO, W), F32)

        def chunk(r, sums):
            base = pl.multiple_of(r * C, C)
            out = []
            for half in (0, 1):
                ds_ = d_s[half, pl.ds(base, C + HALO), :]
                d0, d1, d2 = ds_[:C, :], pltpu.roll(ds_, C + HALO - 1, 0)[:C, :], pltpu.roll(ds_, C + HALO - 2, 0)[:C, :]
                cw = cw_ref[half]
                dup_ref[half, pl.ds(base, C), :] = (cw[2:3, :] * d0 + cw[1:2, :] * d1 + cw[0:1, :] * d2).astype(dup_ref.dtype)
                x = up_ref[half, pl.ds(base, C), :]
                sb, s0, s1, s2 = sums[half]
                out.append((sb + _fold8(d0), s0 + _fold8(d2 * x), s1 + _fold8(d1 * x), s2 + _fold8(d0 * x)))
            return tuple(out)

        zeros = tuple(tuple(jnp.zeros((8, W), F32) for _ in range(4)) for _ in range(2))
        sums = lax.fori_loop(0, S // C, chunk, zeros)
        for half in (0, 1):
            sb, s0, s1, s2 = sums[half]
            gcb_ref[half] += jnp.sum(sb, axis=0, keepdims=True)
            gcw_ref[half, 0:1, :] += jnp.sum(s0, axis=0, keepdims=True)
            gcw_ref[half, 1:2, :] += jnp.sum(s1, axis=0, keepdims=True)
            gcw_ref[half, 2:3, :] += jnp.sum(s2, axis=0, keepdims=True)

    up_spec = pl.BlockSpec((2, S, W), lambda j, b: (0, b, j))
    cw_spec = pl.BlockSpec((2, 3, W), lambda j, b: (0, 0, j))
    cb_spec = pl.BlockSpec((2, 1, W), lambda j, b: (0, 0, j))
    return pl.pallas_call(
        body, name="gate_bwd",
        out_shape=(jax.ShapeDtypeStruct((2, T, D_FF), MXU_DTYPE), jax.ShapeDtypeStruct((2, 3, D_FF), F32),
                   jax.ShapeDtypeStruct((2, 1, D_FF), F32)),
        grid=(N_FF_TILES, B),
        in_specs=[up_spec, up_spec, pl.BlockSpec((S, W), lambda j, b: (b, j)), cw_spec],
        out_specs=(up_spec, cw_spec, cb_spec),
        scratch_shapes=[pltpu.VMEM((2, S + HALO, W), F32)],
        compiler_params=_params(("parallel", "arbitrary"), 12 * _nbytes((S, W), F32)),
    )(up3, conv3, dact, conv_w)


def _final(x2, tgt, g, tr=512):
    T, D = x2.shape

    def body(x_ref, t_ref, g_ref, dx_ref, loss_ref, gg_ref):
        @pl.when(pl.program_id(0) == 0)
        def _():
            loss_ref[...] = jnp.zeros_like(loss_ref)
            gg_ref[...] = jnp.zeros_like(gg_ref)

        xv = x_ref[...]
        gv = g_ref[...]
        r = lax.rsqrt(jnp.mean(xv * xv, axis=-1, keepdims=True) + EPS)
        xn = xv * r
        err = xn * gv - t_ref[...]
        loss_ref[...] += 0.5 * jnp.sum(jnp.mean(err * err, axis=-1, keepdims=True), axis=0, keepdims=True)
        dy = err * (1.0 / D)
        gg_ref[...] += jnp.sum(dy * xn, axis=0, keepdims=True)
        dxn = dy * gv
        dx_ref[...] = r * (dxn - xn * jnp.mean(dxn * xn, axis=-1, keepdims=True))

    row = pl.BlockSpec((tr, D), lambda i: (i, 0))
    vec = pl.BlockSpec((1, D), lambda i: (0, 0))
    return pl.pallas_call(
        body, name="final_loss",
        out_shape=(jax.ShapeDtypeStruct((T, D), F32), jax.ShapeDtypeStruct((1, 128), F32), jax.ShapeDtypeStruct((1, D), F32)),
        grid=(T // tr,), in_specs=[row, row, vec],
        out_specs=(row, pl.BlockSpec((1, 128), lambda i: (0, 0)), vec),
        compiler_params=_params(("arbitrary",), 6 * _nbytes((tr, D), F32)),
    )(x2, tgt, g)


def _sum_slabs(parts, name, tr):
    rows, cols = parts[0].shape
    n = len(parts)

    def body(*refs):
        acc = refs[0][...]
        for r in refs[1:n]:
            acc = acc + r[...]
        refs[n][...] = acc

    blk = pl.BlockSpec((tr, cols), lambda i: (i, 0))
    return pl.pallas_call(
        body, name=name, out_shape=jax.ShapeDtypeStruct((rows, cols), F32), grid=(rows // tr,),
        in_specs=[blk] * n, out_specs=blk,
        compiler_params=_params(("parallel",), (n + 1) * _nbytes((tr, cols), F32)),
    )(*parts)


ADAMW_BLOCK_BYTES = 2400 * 1024


def _adamw(w, g, m, v, name, copy_grad=False):
    lead = w.ndim == 3
    rows, cols = w.shape[-2:]
    fits = [d for d in range(8, rows + 1, 8) if rows % d == 0 and d * cols * 4 <= ADAMW_BLOCK_BYTES]
    tr = max(fits) if fits else rows
    c1 = 1.0 - ADAM_B1 ** ADAM_STEP
    c2 = 1.0 - ADAM_B2 ** ADAM_STEP

    def body(w_ref, g_ref, m_ref, v_ref, d_ref, nm_ref, nv_ref, *g_out):
        gv = g_ref[...]
        nm = ADAM_B1 * m_ref[...] + (1.0 - ADAM_B1) * gv
        nv = ADAM_B2 * v_ref[...] + (1.0 - ADAM_B2) * (gv * gv)
        nm_ref[...] = nm
        nv_ref[...] = nv
        d_ref[...] = -ADAM_LR * ((nm / c1) / (jnp.sqrt(nv / c2) + ADAM_EPS) + ADAM_WD * w_ref[...])
        if copy_grad:
            g_out[0][...] = gv

    blk = pl.BlockSpec((None, tr, cols), lambda i: (0, i, 0)) if lead else pl.BlockSpec((tr, cols), lambda i: (i, 0))
    sds = jax.ShapeDtypeStruct(w.shape, F32)
    n_out = 4 if copy_grad else 3
    return pl.pallas_call(
        body, name=name, out_shape=(sds,) * n_out, grid=(rows // tr,), in_specs=[blk] * 4, out_specs=(blk,) * n_out,
        compiler_params=_params(("parallel",), (4 + n_out) * _nbytes((tr, cols), F32)),
    )(w, g, m, v)


_ANY = pl.BlockSpec(memory_space=pl.ANY)


def _place():
    x, y, c = lax.axis_index("x"), lax.axis_index("y"), lax.axis_index("c")
    chips = [(1 - x, y), (x, 1 - y), (1 - x, 1 - y)]
    return x, y, c, chips


def _forward_halves(lands):
    n = len(lands)

    def body(*refs):
        outs, send, recv = refs[n:2 * n], refs[2 * n], refs[2 * n + 1]
        x, y, c, chips = _place()
        cps = []
        for w in range(n):
            for j, (px, py) in enumerate(chips):
                landed = outs[w].at[2 * px + py, c]
                cps.append(pltpu.make_async_remote_copy(
                    src_ref=landed, dst_ref=landed, send_sem=send.at[3 * w + j], recv_sem=recv.at[3 * w + j],
                    device_id=(x, y, 1 - c), device_id_type=MESH))
        for cp in cps:
            cp.start()
        for w in range(n):
            for j, (px, py) in enumerate(chips):
                other = outs[w].at[2 * px + py, 1 - c]
                pltpu.make_async_remote_copy(src_ref=other, dst_ref=other, send_sem=send.at[3 * w + j],
                                             recv_sem=recv.at[3 * w + j], device_id=(x, y, 1 - c),
                                             device_id_type=MESH).wait_recv()
        for cp in cps:
            cp.wait_send()

    dma = lambda k: pltpu.SemaphoreType.DMA((k,))
    return pl.pallas_call(
        body, name="gather_forward_halves", out_shape=tuple(jax.ShapeDtypeStruct(a.shape, a.dtype) for a in lands),
        in_specs=[_ANY] * n, out_specs=tuple([_ANY] * n), input_output_aliases={w: w for w in range(n)},
        scratch_shapes=[dma(3 * n), dma(3 * n)],
    )(*lands)


_HBM = pl.BlockSpec(memory_space=pltpu.HBM)
_SEM = pl.BlockSpec(memory_space=pltpu.SEMAPHORE)
_EFFECT = pltpu.SideEffectType.DATAFLOW_SIDE_EFFECTING


SEMS_PER_ARRAY = 8


def _exchange_copies(srcs, lands, send, recv, mode):
    x, y, c, chips = _place()
    if mode == "halves":
        cps = []
        for w, (src, land) in enumerate(zip(srcs, lands)):
            pieces = [(src.at[c], land.at[2 * x + y, c], (px, py, c)) for px, py in chips]
            pieces.append((src, land.at[2 * x + y], (x, y, 1 - c)))
            for k, (piece, dst, peer) in enumerate(pieces):
                cps.append(pltpu.make_async_remote_copy(
                    src_ref=piece, dst_ref=dst, send_sem=send.at[SEMS_PER_ARRAY * w + k],
                    recv_sem=recv.at[SEMS_PER_ARRAY * w + k], device_id=peer, device_id_type=MESH))
        return cps
    if mode == "swap":
        return [pltpu.make_async_remote_copy(
            src_ref=src.at[:, 1 - c], dst_ref=land, send_sem=send.at[SEMS_PER_ARRAY * w],
            recv_sem=recv.at[SEMS_PER_ARRAY * w], device_id=(x, y, 1 - c), device_id_type=MESH)
            for w, (src, land) in enumerate(zip(srcs, lands))]
    if mode == "all":
        flips = [(fx, fy, fc) for fx in (0, 1) for fy in (0, 1) for fc in (0, 1)][1:]
        peers = [(x ^ fx, y ^ fy, c ^ fc) for fx, fy, fc in flips]
        slot = 4 * x + 2 * y + c
    else:
        peers = [(px, py, c) for px, py in chips] + ([(x, y, 1 - c)] if mode == "gather" else [])
        slot = 2 * x + y
    cps = []
    for w, (src, land) in enumerate(zip(srcs, lands)):
        for k, peer in enumerate(peers):
            piece = src.at[2 * peer[0] + peer[1]] if mode == "scatter" else src
            cps.append(pltpu.make_async_remote_copy(
                src_ref=piece, dst_ref=land.at[slot], send_sem=send.at[SEMS_PER_ARRAY * w + k],
                recv_sem=recv.at[SEMS_PER_ARRAY * w + k], device_id=peer, device_id_type=MESH))
    return cps


def _exchange_start(srcs, name, mode, after):
    n = len(srcs)
    if mode == "swap":
        land_shapes = [(s.shape[0],) + s.shape[2:] for s in srcs]
    else:
        lead = {"gather": (N_CHIPS,), "halves": (N_CHIPS,), "scatter": (), "all": (2 * N_CHIPS,)}[mode]
        land_shapes = [lead + s.shape for s in srcs]

    def body(*refs):
        src_refs, land_refs = refs[:n], refs[n:2 * n]
        send, recv = refs[2 * n + 1], refs[2 * n + 2]
        token = refs[-1]
        for cp in _exchange_copies(src_refs, land_refs, send, recv, mode):
            cp.start()
        token[...] = jnp.zeros_like(token)

    sems = pltpu.SemaphoreType.DMA((SEMS_PER_ARRAY * n,))
    out = pl.pallas_call(
        body, name=name,
        out_shape=(sems, sems, *[pltpu.HBM(s.shape, s.dtype) for s in srcs],
                   *[pltpu.HBM(shp, s.dtype) for shp, s in zip(land_shapes, srcs)], jax.ShapeDtypeStruct((8, 128), F32)),
        in_specs=[_HBM] * (2 * n) + [_ANY],
        out_specs=(_SEM, _SEM, *[_HBM] * (2 * n), pl.BlockSpec(memory_space=pltpu.VMEM)),
        input_output_aliases={i: 2 + i for i in range(2 * n)},
        compiler_params=pltpu.CompilerParams(has_side_effects=_EFFECT),
    )(*[pltpu.with_memory_space_constraint(s, pltpu.HBM) for s in srcs],
      *[pltpu.with_memory_space_constraint(lax.empty(shp, s.dtype), pltpu.HBM) for shp, s in zip(land_shapes, srcs)],
      after)
    return out[0], out[1], out[2:2 + n], out[2 + n:2 + 2 * n], out[-1]


def _exchange_wait(started, name, mode, after):
    send, recv, src_thru, land_thru, _ = started
    n = len(src_thru)
    after = list(after) if isinstance(after, (list, tuple)) else [after]

    def body(*refs):
        src_refs, land_refs, send_ref, recv_ref = refs[:n], refs[n:2 * n], refs[2 * n], refs[2 * n + 1]
        for cp in _exchange_copies(src_refs, land_refs, send_ref, recv_ref, mode):
            cp.wait_send()
            cp.wait_recv()

    out = pl.pallas_call(
        body, name=name,
        out_shape=tuple(pltpu.HBM(a.shape, a.dtype) for a in list(src_thru) + list(land_thru)),
        in_specs=[_HBM] * (2 * n) + [_SEM, _SEM] + [_ANY] * len(after), out_specs=tuple([_HBM] * (2 * n)),
        input_output_aliases={i: i for i in range(2 * n)},
        compiler_params=pltpu.CompilerParams(has_side_effects=_EFFECT),
    )(*src_thru, *land_thru, send, recv, *after)
    return out[:n], out[n:]


def _swap_halves(gs, name):
    n = len(gs)

    def body(*refs):
        ins, outs, send, recv = refs[:n], refs[n:2 * n], refs[2 * n], refs[2 * n + 1]
        x, y, c, _ = _place()
        cps = []
        for w in range(n):
            cps.append(pltpu.make_async_remote_copy(
                src_ref=ins[w].at[:, 1 - c], dst_ref=outs[w], send_sem=send.at[w], recv_sem=recv.at[w],
                device_id=(x, y, 1 - c), device_id_type=MESH))
        for cp in cps:
            cp.start()
        for cp in cps:
            cp.wait()

    return pl.pallas_call(
        body, name=name,
        out_shape=tuple(jax.ShapeDtypeStruct((g.shape[0],) + g.shape[2:], g.dtype) for g in gs),
        in_specs=[_ANY] * n, out_specs=tuple([_ANY] * n),
        scratch_shapes=[pltpu.SemaphoreType.DMA((n,)), pltpu.SemaphoreType.DMA((n,))],
    )(*gs)


GRAD_PAYLOAD = jnp.bfloat16


def _half_blocks(half_rows, cols):
    if (half_rows // 2) % 16 == 0:
        return (half_rows // 2, cols), (lambda r: (r, 0))
    assert cols % 256 == 0, (half_rows, cols)
    return (half_rows, cols // 2), (lambda r: (0, r))


def _pair_sum(gs, gots, name):
    n = len(gs)
    core = lax.axis_index("c").astype(jnp.int32).reshape(1)

    def body(core_ref, *refs):
        del core_ref
        for w in range(n):
            refs[2 * n + w][...] = (refs[w][...] + refs[n + w][...]).astype(GRAD_PAYLOAD)

    in_specs, out_specs, out_shape, nbytes = [], [], [], 0
    cuts = [_half_blocks(g.shape[1] // 2, g.shape[2]) for g in gs]
    for g, ((br, bc), at) in zip(gs, cuts):
        per_half = (g.shape[1] // 2) // br
        in_specs.append(pl.BlockSpec((1, br, bc), lambda s, r, core, at=at, per_half=per_half:
                                     (s, per_half * core[0] + at(r)[0], at(r)[1])))
        nbytes += 3 * _nbytes((br, bc), F32)
    for g, ((br, bc), at) in zip(gs, cuts):
        in_specs.append(pl.BlockSpec((1, br, bc), lambda s, r, core, at=at: (s,) + at(r)))
        out_specs.append(pl.BlockSpec((1, br, bc), lambda s, r, core, at=at: (s,) + at(r)))
        out_shape.append(jax.ShapeDtypeStruct((g.shape[0], g.shape[1] // 2, g.shape[2]), GRAD_PAYLOAD))
    return pl.pallas_call(
        body, name=name, out_shape=tuple(out_shape),
        grid_spec=pltpu.PrefetchScalarGridSpec(num_scalar_prefetch=1, grid=(N_CHIPS, 2), in_specs=in_specs,
                                               out_specs=tuple(out_specs)),
        compiler_params=_params(("parallel", "parallel"), nbytes),
    )(core, *gs, *gots)


def _chip_sum(ps, landed):
    n = len(ps)
    x, y, c = lax.axis_index("x"), lax.axis_index("y"), lax.axis_index("c")
    where = jnp.stack([2 * x + y, 2 * (1 - x) + y, 2 * x + (1 - y), 2 * (1 - x) + (1 - y), c]).astype(jnp.int32)

    def body(where_ref, *refs):
        del where_ref
        for w in range(n):
            terms = [refs[4 * w + t][...].astype(F32) for t in range(4)]
            refs[4 * n + w][...] = ((terms[0] + terms[1]) + terms[2]) + terms[3]

    in_specs, out_specs, out_shape, args, nbytes = [], [], [], [], 0
    for p, a in zip(ps, landed):
        (br, bc), at = _half_blocks(a.shape[1], a.shape[2])
        blk = (1, br, bc)
        in_specs.append(pl.BlockSpec(blk, lambda r, where, at=at: (where[0],) + at(r)))
        args.append(p)
        for t in (1, 2, 3):
            in_specs.append(pl.BlockSpec(blk, lambda r, where, t=t, at=at: (where[t],) + at(r)))
            args.append(a)
        out_specs.append(pl.BlockSpec(blk, lambda r, where, at=at: (where[4],) + at(r)))
        out_shape.append(jax.ShapeDtypeStruct((2,) + a.shape[1:], F32))
        nbytes += 4 * _nbytes(blk, F32)
    return pl.pallas_call(
        body, name="grad_chip_sum", out_shape=tuple(out_shape),
        grid_spec=pltpu.PrefetchScalarGridSpec(num_scalar_prefetch=1, grid=(2,), in_specs=in_specs,
                                               out_specs=tuple(out_specs)),
        compiler_params=_params(("parallel",), nbytes),
    )(where, *args)


def _join_halves(ss):
    n = len(ss)

    def body(*refs):
        outs, send, recv = refs[n:2 * n], refs[2 * n], refs[2 * n + 1]
        x, y, c, _ = _place()
        cps = []
        for w in range(n):
            cps.append(pltpu.make_async_remote_copy(
                src_ref=outs[w].at[c], dst_ref=outs[w].at[c], send_sem=send.at[w], recv_sem=recv.at[w],
                device_id=(x, y, 1 - c), device_id_type=MESH))
        for cp in cps:
            cp.start()
        for w in range(n):
            got = outs[w].at[1 - c]
            pltpu.make_async_remote_copy(src_ref=got, dst_ref=got, send_sem=send.at[w], recv_sem=recv.at[w],
                                         device_id=(x, y, 1 - c), device_id_type=MESH).wait_recv()
        for cp in cps:
            cp.wait_send()

    dma = lambda k: pltpu.SemaphoreType.DMA((k,))
    return pl.pallas_call(
        body, name="grad_join_halves",
        out_shape=tuple(jax.ShapeDtypeStruct(s.shape, s.dtype) for s in ss),
        in_specs=[_ANY] * n, out_specs=tuple([_ANY] * n), input_output_aliases={w: w for w in range(n)},
        scratch_shapes=[dma(n), dma(n)],
    )(*ss)


def _rot_cols(w, axis=-1):
    a, b = jnp.split(w, 2, axis=axis)
    return jnp.concatenate([-b, a], axis=axis)


def _rot_cols_t(g, axis=-1):
    a, b = jnp.split(g, 2, axis=axis)
    return jnp.concatenate([b, -a], axis=axis)


def _cols_from_chips(a):
    n, r, cs = a.shape
    return jnp.transpose(a, (1, 0, 2)).reshape(r, n * cs)


def _cols_to_chips(a):
    r, cc = a.shape
    return jnp.transpose(a.reshape(r, N_CHIPS, cc // N_CHIPS), (1, 0, 2))


def _conv_w_split(cw):
    return jnp.swapaxes(cw.reshape(3, 2, D_FF), 0, 1)


def _conv_w_join(g):
    return jnp.swapaxes(g, 0, 1).reshape(3, 2 * D_FF)


_SEG =(D_MODEL, 2 * D_MODEL, 2 * D_MODEL + Q_RANK, 2 * D_MODEL + Q_RANK + KV_RANK, 2 * D_MODEL + Q_RANK + KV_RANK + ROPE,
        3 * D_MODEL + Q_RANK + KV_RANK + ROPE)


def _w_in_t_to_pad(wt):
    u, v, cq, ckv, kr, ga, gb = jnp.split(wt, _SEG, axis=0)
    return jnp.concatenate([u, v, ga, gb, cq, ckv, kr, _rot_cols(kr, axis=0)], axis=0)


def _w_in_t_from_pad(gt):
    u, v, ga, gb, cq, ckv, kr, krr = jnp.split(
        gt, (D_MODEL, 2 * D_MODEL, 3 * D_MODEL, 4 * D_MODEL, 4 * D_MODEL + Q_RANK, 4 * D_MODEL + Q_RANK + KV_RANK,
             4 * D_MODEL + Q_RANK + KV_RANK + ROPE), axis=0)
    return jnp.concatenate([u, v, cq, ckv, kr + _rot_cols_t(krr, axis=0), ga, gb], axis=0)


def _w_uq_to_pad(w):
    t = w.reshape(Q_RANK, HEADS, QK_DIM)
    nope, rope = t[..., :NOPE], t[..., NOPE:]
    return jnp.concatenate([nope, rope, _rot_cols(rope)], axis=-1).reshape(Q_RANK, HEADS * HEAD_PAD)


def _w_uq_from_pad(g):
    t = g.reshape(Q_RANK, HEADS, HEAD_PAD)
    nope, rope, rot = t[..., :NOPE], t[..., NOPE:QK_DIM], t[..., QK_DIM:]
    return jnp.concatenate([nope, rope + _rot_cols_t(rot)], axis=-1).reshape(Q_RANK, HEADS * QK_DIM)


def _w_ukv_to_pad(w):
    t = w.reshape(KV_RANK, HEADS, 2, NOPE)
    return jnp.swapaxes(t, 1, 2).reshape(KV_RANK, 2 * HEADS * NOPE)


def _w_ukv_from_pad(g):
    t = g.reshape(KV_RANK, 2, HEADS, NOPE)
    return jnp.swapaxes(t, 1, 2).reshape(KV_RANK, 2 * HEADS * NOPE)


def _rope_tables(positions):
    inv_freq = 1.0 / (ROPE_THETA ** (jnp.arange(0, ROPE, 2, dtype=F32) / ROPE))
    ang = positions.astype(F32).reshape(-1, 1) * inv_freq
    cos, sin = jnp.cos(ang), jnp.sin(ang)
    zero = jnp.zeros((ang.shape[0], 64), F32)
    return jnp.concatenate([cos, cos, zero], axis=1), jnp.concatenate([sin, sin, zero], axis=1)


_BIG = ("w_in", "w_uq", "w_ukv", "w_out", "w_up", "w_down")
UP_SHARD = 2 * D_FF // N_CHIPS
TOKEN_TILE = 1024


def _local_step(x, positions, tgt, wts, in_weights, mixer_weights, ffn_weights, on_ffn_grads, on_mixer_grads):
    B, S, D = x.shape
    T = B * S
    xf = x.reshape(T, D)
    cos_a, sin_a = _rope_tables(positions)
    bs_t = jnp.pad(wts["a_spatial_b"].T, ((0, 0), (0, 128 - A_GROUPS)))

    h = _rms_fwd(xf, wts["mix_norm"], "norm1_fwd")
    wts = dict(wts)
    wts["w_in"], token = in_weights([h, cos_a, sin_a])
    tm = min(TOKEN_TILE, T)
    z = _mm(h, wts["w_in"], "nt", "in_proj", tm=tm, tn=1536, tk=D, n_outer=True, after=token)
    wts["w_q"], wts["w_kv"], wts["w_out"] = mixer_weights(z)
    q, k, v, cqn, ckvn = _lat_fwd(z, wts["q_a_norm"], wts["kv_a_norm"], wts["w_q"], wts["w_kv"], cos_a, sin_a)
    yb, *lses = _attn_fwd(q, k, v, B, S)
    merged = _mix_fwd(z, yb, wts["a_v_norm_g"], wts["a_v_norm_b"], wts["a_spatial_w"], bs_t)
    x1 = _mm(merged, wts["w_out"], "nn", "out_proj", tm=tm, tn=D, tk=D, add=xf)
    h2 = _rms_fwd(x1, wts["ffn_norm"], "norm2_fwd")
    wts["w_up"], wts["w_down"], wts["conv_w"] = ffn_weights(h2)
    up_pre = _mm(h2, wts["w_up"], "nn", "up_proj", tm=tm, tn=UP_SHARD, tk=D, dims=(T, 2 * D_FF, D),
                 b_spec=pl.BlockSpec((None, D, UP_SHARD), lambda i, j, k: (j, 0, 0)),
                 o_spec=pl.BlockSpec((None, tm, UP_SHARD), lambda i, j, k: (j // 2, i, j % 2)), out_shape=(2, T, D_FF),
                 n_outer=True)
    act, up_conv = _gate_fwd(up_pre, wts["conv_w"], wts["conv_b"], B, S)
    x2 = _mm(act, wts["w_down"], "nn", "down_proj", tm=tm, tn=D, tk=1408, add=x1)
    dx2, loss_row, g_final = _final(x2, tgt.reshape(T, D), wts["final_norm"])

    g = {"final_norm": g_final}
    dact = _mm(dx2, wts["w_down"], "nt", "down_proj_dx", tm=tm, tn=1408, tk=D, n_outer=True)
    tk2, tk1 = min(2048, T), min(1024, T)
    g["w_down"], g["w_down_lo"] = _mm(act, dx2, "tn", "down_proj_dw", tm=1408, tn=D, tk=tk1, copy_dtype=GRAD_PAYLOAD)
    dup, g["conv_w"], g["conv_b"] = _gate_bwd(up_pre, up_conv, dact, wts["conv_w"], B, S)
    g["w_up"], g["w_up_lo"] = _mm(
        h2, dup, "tn", "up_proj_dw", tm=D, tn=UP_SHARD, tk=tk2, dims=(D, 2 * D_FF, T), copy_dtype=GRAD_PAYLOAD,
        b_spec=pl.BlockSpec((None, tk2, UP_SHARD), lambda i, j, k: (j // 2, k, j % 2)),
        o_spec=pl.BlockSpec((None, D, UP_SHARD), lambda i, j, k: (j, 0, 0)), out_shape=(N_CHIPS, D, UP_SHARD))
    token, ffn_sent = on_ffn_grads(g)
    dh2 = _mm(dup, wts["w_up"], "nt", "up_proj_dx", tm=tm, tn=D, tk=UP_SHARD, dims=(T, D, 2 * D_FF), after=token,
              a_spec=pl.BlockSpec((None, tm, UP_SHARD), lambda i, j, k: (k // 2, i, k % 2)),
              b_spec=pl.BlockSpec((None, D, UP_SHARD), lambda i, j, k: (k, 0, 0)))
    token = ffn_sent(dh2)
    dx1, g["ffn_norm"] = _rms_bwd(x1, wts["ffn_norm"], dh2, dx2, "norm2_bwd")
    dm = _mm(dx1, wts["w_out"], "nt", "out_proj_dx", tm=tm, tn=D, tk=D, after=token)
    g["w_out"], g["w_out_lo"] = _mm(merged, dx1, "tn", "out_proj_dw", tm=D, tn=D, tk=tk1, copy_dtype=GRAD_PAYLOAD)
    dz, dyb, dl, g["a_spatial_w"], gbs, g["a_v_norm_g"], g["a_v_norm_b"] = _mix_bwd(
        z, yb, dm, wts["a_v_norm_g"], wts["a_v_norm_b"], wts["a_spatial_w"], bs_t)
    g["a_spatial_b"] = gbs[:, :A_GROUPS].T
    delta = dl.reshape(HEADS * T // ATT_BLOCK, 1, ATT_BLOCK)
    dq, dk, dv = _attn_bwd(q, k, v, dyb, lses, delta, B, S)
    dz, dq_raw, dkv, g["q_a_norm"], g["kv_a_norm"] = _lat_bwd(
        dz, z, dq, dk, dv, wts["q_a_norm"], wts["kv_a_norm"], wts["w_q"], wts["w_kv"], cos_a, sin_a)
    g["w_q"] = _mm(cqn, dq_raw, "tn", "q_proj_dw", tm=Q_RANK, tn=HEADS * HEAD_PAD, tk=tk2)
    g["w_kv"] = _mm(ckvn, dkv, "tn", "kv_proj_dw", tm=KV_RANK, tn=2 * HEADS * NOPE, tk=tk2)
    g["w_in"] = _mm(dz, h, "tn", "in_proj_dw", tm=1536, tn=D, tk=tk2)
    token = on_mixer_grads(g)
    dh = _mm(dz, wts["w_in"], "nn", "in_proj_dx", tm=tm, tn=D, tk=1536, after=token)
    dx, g["mix_norm"] = _rms_bwd(xf, wts["mix_norm"], dh, dx1, "norm1_bwd")
    return loss_row[0, 0], dx.reshape(B, S, D), g


_SMALL = (("mix_norm", (1, D_MODEL)), ("a_v_norm_g", (1, D_MODEL)), ("a_v_norm_b", (1, D_MODEL)),
          ("a_spatial_w", (A_GROUPS * CHUNK, CHUNK)), ("a_spatial_b", (1, A_GROUPS * CHUNK)), ("q_a_norm", (1, Q_RANK)),
          ("kv_a_norm", (1, KV_RANK)), ("ffn_norm", (1, D_MODEL)), ("conv_b", (1, 2 * D_FF)), ("final_norm", (1, D_MODEL)),
          ("conv_w", (3, 2 * D_FF)))
_SMALL_SIZE = sum(math.prod(s) for _, s in _SMALL)
_SMALL_ROWS = -(-(_SMALL_SIZE + 1) // (128 * 8)) * 8


def kernel(x, positions, mix_norm, w_in, a_v_norm_g, a_v_norm_b, a_spatial_w, a_spatial_b, q_a_norm, w_uq, kv_a_norm, w_ukv, w_out, ffn_norm, w_up, conv_w, conv_b, w_down, final_norm, loss_target, m_mix_norm, m_w_in, m_a_v_norm_g, m_a_v_norm_b, m_a_spatial_w, m_a_spatial_b, m_q_a_norm, m_w_uq, m_kv_a_norm, m_w_ukv, m_w_out, m_ffn_norm, m_w_up, m_conv_w, m_conv_b, m_w_down, m_final_norm, v_mix_norm, v_w_in, v_a_v_norm_g, v_a_v_norm_b, v_a_spatial_w, v_a_spatial_b, v_q_a_norm, v_w_uq, v_kv_a_norm, v_w_ukv, v_w_out, v_ffn_norm, v_w_up, v_conv_w, v_conv_b, v_w_down, v_final_norm):
    weights = dict(mix_norm=mix_norm, w_in=w_in, a_v_norm_g=a_v_norm_g, a_v_norm_b=a_v_norm_b, a_spatial_w=a_spatial_w,
                   a_spatial_b=a_spatial_b, q_a_norm=q_a_norm, w_uq=w_uq, kv_a_norm=kv_a_norm, w_ukv=w_ukv, w_out=w_out,
                   ffn_norm=ffn_norm, w_up=w_up, conv_w=conv_w, conv_b=conv_b, w_down=w_down, final_norm=final_norm)
    m_in = dict(mix_norm=m_mix_norm, w_in=m_w_in, a_v_norm_g=m_a_v_norm_g, a_v_norm_b=m_a_v_norm_b,
                a_spatial_w=m_a_spatial_w, a_spatial_b=m_a_spatial_b, q_a_norm=m_q_a_norm, w_uq=m_w_uq,
                kv_a_norm=m_kv_a_norm, w_ukv=m_w_ukv, w_out=m_w_out, ffn_norm=m_ffn_norm, w_up=m_w_up, conv_w=m_conv_w,
                conv_b=m_conv_b, w_down=m_w_down, final_norm=m_final_norm)
    v_in = dict(mix_norm=v_mix_norm, w_in=v_w_in, a_v_norm_g=v_a_v_norm_g, a_v_norm_b=v_a_v_norm_b,
                a_spatial_w=v_a_spatial_w, a_spatial_b=v_a_spatial_b, q_a_norm=v_q_a_norm, w_uq=v_w_uq,
                kv_a_norm=v_kv_a_norm, w_ukv=v_w_ukv, w_out=v_w_out, ffn_norm=v_ffn_norm, w_up=v_w_up, conv_w=v_conv_w,
                conv_b=v_conv_b, w_down=v_w_down, final_norm=v_final_norm)
    names = list(weights)
    chip = 2 * lax.axis_index("x") + lax.axis_index("y")

    def halves(a):
        return a.reshape(a.shape[:-2] + (2, a.shape[-2] // 2, a.shape[-1]))

    w_in_t = jnp.swapaxes(w_in[0], 0, 1).astype(MXU_DTYPE)
    w_in_gather = _exchange_start([jnp.stack(jnp.split(w_in_t, 2, axis=1))], "w_in_gather_start", "halves",
                                  after=positions)
    gathers = {}
    wts = dict(
        mix_norm=mix_norm, a_v_norm_g=a_v_norm_g, a_v_norm_b=a_v_norm_b, a_spatial_w=a_spatial_w[0],
        a_spatial_b=a_spatial_b[0], q_a_norm=q_a_norm, kv_a_norm=kv_a_norm, ffn_norm=ffn_norm,
        final_norm=final_norm.reshape(1, D_MODEL), conv_b=conv_b.reshape(2, 1, D_FF))

    mixer_shards = [weights[n][0].astype(MXU_DTYPE) for n in _BIG[1:4]]
    ffn_shards = [w_up[0].astype(MXU_DTYPE), w_down[0].astype(MXU_DTYPE)]

    def in_weights(after):
        _, landed = _exchange_wait(w_in_gather, "w_in_gather_wait", "halves", list(after) + mixer_shards + ffn_shards)
        (w_in_sh,) = _forward_halves(list(landed))
        gathers["mixer"] = _exchange_start(mixer_shards, "mixer_gather_start", "gather", after=w_in_sh)
        gathers["ffn"] = _exchange_start(ffn_shards + [conv_w[0]], "ffn_gather_start", "gather",
                                         after=gathers["mixer"][4])
        w_in_pad = _w_in_t_to_pad(jnp.concatenate([w_in_sh[:, 0], w_in_sh[:, 1]], axis=-1).reshape(-1, D_MODEL))
        return w_in_pad, gathers["ffn"][4]

    def mixer_weights(after):
        _, (w_uq_sh, w_ukv_sh, w_out_sh) = _exchange_wait(gathers["mixer"], "mixer_gather_wait", "gather", after)
        return (_w_uq_to_pad(_cols_from_chips(w_uq_sh)), _w_ukv_to_pad(_cols_from_chips(w_ukv_sh)),
                w_out_sh.reshape(D_MODEL, D_MODEL))

    def ffn_weights(after):
        _, (w_up_sh, w_down_sh, cw_all) = _exchange_wait(gathers["ffn"], "ffn_gather_wait", "gather", after)
        return w_up_sh, w_down_sh.reshape(D_FF, D_MODEL), _conv_w_split(_cols_from_chips(cw_all))

    scatters = {}

    def start_scatter(slabs, slabs_lo, tag):
        got = _swap_halves([halves(s) for s in slabs_lo], tag + "_grad_swap_halves")
        sums = _pair_sum(slabs, got, tag + "_grad_pair_sum")
        scatters[tag] = _exchange_start(list(sums), tag + "_scatter_start", "scatter", after=slabs[-1])
        return scatters[tag][4]

    def on_ffn_grads(g):
        slabs, slabs_lo = [[g["w_up" + lo], g["w_down" + lo].reshape(N_CHIPS, D_FF // N_CHIPS, D_MODEL)]
                           for lo in ("", "_lo")]
        swap = _exchange_start([halves(s) for s in slabs_lo], "ffn_swap_start", "swap", after=slabs[1])

        def sent(after):
            _, got = _exchange_wait(swap, "ffn_swap_wait", "swap", after)
            sums = _pair_sum(slabs, got, "ffn_grad_pair_sum")
            scatters["ffn"] = _exchange_start(list(sums), "ffn_scatter_start", "scatter", after=got[0])
            return scatters["ffn"][4]

        return swap[4], sent

    def on_mixer_grads(g):
        slabs = [_w_in_t_from_pad(g["w_in"]).reshape(N_CHIPS, -1, D_MODEL), _cols_to_chips(_w_uq_from_pad(g["w_q"])),
                 _cols_to_chips(_w_ukv_from_pad(g["w_kv"]))]
        w_out_slabs = [g["w_out" + lo].reshape(N_CHIPS, D_MODEL // N_CHIPS, D_MODEL) for lo in ("", "_lo")]
        return start_scatter(slabs + w_out_slabs[:1], [s.astype(GRAD_PAYLOAD) for s in slabs] + w_out_slabs[1:], "mixer")

    loss_part, grad_x, g = _local_step(x, positions, loss_target, wts, in_weights, mixer_weights, ffn_weights,
                                       on_ffn_grads, on_mixer_grads)

    g_small_parts = dict(g)
    g_small_parts["conv_w"] = _conv_w_join(g["conv_w"])
    g_small_parts["conv_b"] = g["conv_b"].reshape(1, 2 * D_FF)
    flat = jnp.concatenate([g_small_parts[n].reshape(-1) for n, _ in _SMALL] + [loss_part.reshape(1)])
    flat = jnp.pad(flat, (0, _SMALL_ROWS * 128 - flat.shape[0])).reshape(_SMALL_ROWS, 128)
    small_gather = _exchange_start([flat], "small_gather_start", "all", after=grad_x)

    mixer_sums, mixer_landed = _exchange_wait(scatters["mixer"], "mixer_scatter_wait", "scatter", after=small_gather[4])
    ffn_sums, ffn_landed = _exchange_wait(scatters["ffn"], "ffn_scatter_wait", "scatter", after=mixer_landed[0])
    reduced = _chip_sum(list(mixer_sums) + list(ffn_sums), list(mixer_landed) + list(ffn_landed))
    g_big = dict(zip(_BIG, _join_halves(reduced)))

    grads, deltas, new_m, new_v = {}, {}, {}, {}

    def update(n, grad, copy_grad=False):
        w = weights[n]
        shape2 = grad.shape
        d, nm, nv, *again = _adamw(w.reshape(shape2), grad, m_in[n].reshape(shape2), v_in[n].reshape(shape2),
                                   "adamw_" + n, copy_grad)
        grads[n], deltas[n], new_m[n], new_v[n] = (t.reshape(w.shape) for t in (again[0] if copy_grad else grad, d, nm, nv))

    def update_transposed(n, grad_t):
        t = lambda a: jnp.swapaxes(a, 1, 2)
        d, nm, nv, again = _adamw(t(weights[n]), grad_t, t(m_in[n]), t(v_in[n]), "adamw_" + n, True)
        grads[n], deltas[n], new_m[n], new_v[n] = t(again), t(d), t(nm), t(nv)

    for n in _BIG:
        g3 = g_big[n].reshape((1, -1, g_big[n].shape[-1]))
        if n == "w_in":
            update_transposed(n, g3)
        else:
            update(n, g3, copy_grad=True)

    (own,), (everyone,) = _exchange_wait(small_gather, "small_gather_wait", "all", after=[deltas[n] for n in _BIG])
    device = 2 * chip + lax.axis_index("c")
    everyone = lax.dynamic_update_slice(everyone, own[None], (device, 0, 0))
    total = _sum_slabs([everyone[j] for j in range(8)], "small_grads_sum", tr=_SMALL_ROWS).reshape(-1)
    o = 0
    for n, shp in _SMALL:
        piece = total[o:o + math.prod(shp)].reshape(shp)
        o += math.prod(shp)
        if n == "conv_w":
            piece = lax.dynamic_slice_in_dim(piece, chip * UP_SHARD, UP_SHARD, axis=1)
        update(n, piece)
    loss = total[_SMALL_SIZE]
    return (loss, grad_x, *[grads[n] for n in names], *[deltas[n] for n in names], *[new_m[n] for n in names],
            *[new_v[n] for n in names])
```

```python
import functools
import math

import jax
import jax.numpy as jnp
from jax import lax
from jax.experimental import pallas as pl
from jax.experimental.pallas import tpu as pltpu

F32 = jnp.float32
MXU_DTYPE = jnp.bfloat16
MESH = pl.DeviceIdType.MESH

D_MODEL = 1024
EPS = 1e-6
A_GROUPS = 8
CHUNK = 128
HEADS = 8
NOPE = 128
ROPE = 64
QK_DIM = NOPE + ROPE
HEAD_PAD = 256
Q_RANK = 256
KV_RANK = 128
ROPE_THETA = 10000.0
D_FF = 2816
FF_TILE = 256
N_FF_TILES = D_FF // FF_TILE
LAT = 512
IN_PAD = 4 * D_MODEL + LAT
N_CHIPS = 4
ADAM_LR, ADAM_B1, ADAM_B2, ADAM_EPS, ADAM_WD, ADAM_STEP = 0.001, 0.9, 0.999, 1e-08, 0.01, 10

VMEM_CAP_V7X = 64 * 1024 * 1024
NEG = -1e30


def _params(sem, nbytes):
    limit = int(min(VMEM_CAP_V7X - (8 << 20), max(32 << 20, 3 * nbytes)))
    return pltpu.CompilerParams(dimension_semantics=sem, vmem_limit_bytes=limit)


def _nbytes(shape, dtype):
    return math.prod(shape) * jnp.dtype(dtype).itemsize


_DIMS = {"nn": (((1,), (0,)), ((), ())), "nt": (((1,), (1,)), ((), ())), "tn": (((0,), (0,)), ((), ()))}


def _mm(a, b, mode, name, *, tm, tn, tk, out_dtype=F32, add=None, dims=None, a_spec=None, b_spec=None,
        o_spec=None, out_shape=None, n_outer=False, copy_dtype=None, after=None):
    if dims is None:
        if mode == "nn":
            (M, K), (_, N) = a.shape, b.shape
        elif mode == "nt":
            (M, K), (N, _) = a.shape, b.shape
        else:
            (K, M), (_, N) = a.shape, b.shape
    else:
        M, N, K = dims
    a_blk = (tk, tm) if mode == "tn" else (tm, tk)
    b_blk = (tn, tk) if mode == "nt" else (tk, tn)
    if a_spec is None:
        a_spec = pl.BlockSpec(a_blk, (lambda i, j, k: (k, i)) if mode == "tn" else (lambda i, j, k: (i, k)))
    if b_spec is None:
        b_spec = pl.BlockSpec(b_blk, (lambda i, j, k: (j, k)) if mode == "nt" else (lambda i, j, k: (k, j)))
    if o_spec is None:
        o_spec = pl.BlockSpec((tm, tn), lambda i, j, k: (i, j))
    if out_shape is None:
        out_shape = (M, N)
    assert M % tm == 0 and N % tn == 0 and K % tk == 0, (name, M, N, K, tm, tn, tk)
    nk = K // tk
    contract = _DIMS[mode]
    has_add = add is not None

    def body(*refs):
        a_ref, b_ref = refs[0], refs[1]
        add_ref = refs[2] if has_add else None
        n_in = 2 + has_add + (after is not None)
        o_ref = refs[n_in]
        copy_ref = refs[n_in + 1] if copy_dtype is not None else None

        def product():
            return lax.dot_general(a_ref[...].astype(MXU_DTYPE), b_ref[...].astype(MXU_DTYPE), contract,
                                   preferred_element_type=F32)

        def finish(r):
            if has_add:
                r = r + add_ref[...]
            o_ref[...] = r.astype(out_dtype)
            if copy_ref is not None:
                copy_ref[...] = r.astype(copy_dtype)

        if nk == 1:
            finish(product())
            return
        acc = refs[-1]
        k = pl.program_id(2)

        @pl.when(k == 0)
        def _():
            acc[...] = jnp.zeros_like(acc)

        acc[...] += product()

        @pl.when(k == nk - 1)
        def _():
            finish(acc[...])

    in_specs = [a_spec, b_spec]
    args = [a, b]
    nbytes = _nbytes(a_blk, a.dtype) + _nbytes(b_blk, b.dtype) + 3 * _nbytes((tm, tn), F32)
    if has_add:
        in_specs.append(pl.BlockSpec((tm, tn), lambda i, j, k: (i, j)))
        args.append(add)
        nbytes += _nbytes((tm, tn), F32)
    if after is not None:
        in_specs.append(pl.BlockSpec(after.shape, lambda i, j, k: (0, 0)))
        args.append(after)
    grid = (M // tm, N // tn, nk)
    if n_outer:
        def swapped(spec):
            return pl.BlockSpec(spec.block_shape, lambda j, i, k, at=spec.index_map: at(i, j, k))

        grid = (N // tn, M // tm, nk)
        in_specs = [swapped(s) for s in in_specs]
        o_spec = swapped(o_spec)
    out_sds, out_specs = jax.ShapeDtypeStruct(out_shape, out_dtype), o_spec
    if copy_dtype is not None:
        out_sds, out_specs = (out_sds, jax.ShapeDtypeStruct(out_shape, copy_dtype)), (o_spec, o_spec)
    return pl.pallas_call(
        body, name=name, out_shape=out_sds, grid=grid, in_specs=in_specs, out_specs=out_specs,
        scratch_shapes=[pltpu.VMEM((tm, tn), F32)] if nk > 1 else [],
        compiler_params=_params(("parallel", "parallel", "arbitrary"), nbytes),
    )(*args)


_GELU_C = math.sqrt(2.0 / math.pi)
_GELU_A = 0.044715


def _sigmoid(x):
    return 0.5 * jnp.tanh(0.5 * x) + 0.5


def _gelu(x):
    t = jnp.tanh(x * (_GELU_C + (_GELU_C * _GELU_A) * (x * x)))
    return x * (0.5 + 0.5 * t)


def _gelu_and_grad(x):
    x2 = x * x
    t = jnp.tanh(x * (_GELU_C + (_GELU_C * _GELU_A) * x2))
    cdf = 0.5 + 0.5 * t
    grad = cdf + (0.5 * x) * (1.0 - t * t) * (_GELU_C + (3.0 * _GELU_C * _GELU_A) * x2)
    return x * cdf, grad


def _rope_mix(g, cos_a, sin_a):
    return g * cos_a + pltpu.roll(g, 64, 1) * sin_a


def _rope_mix_bwd(d, cos_a, sin_a):
    return d * cos_a + pltpu.roll(d * sin_a, 64, 1)


def _rms_fwd(x, g, name, tr=512):
    T, D = x.shape

    def body(x_ref, g_ref, h_ref):
        xv = x_ref[...]
        r = lax.rsqrt(jnp.mean(xv * xv, axis=-1, keepdims=True) + EPS)
        h_ref[...] = ((xv * r) * g_ref[...]).astype(h_ref.dtype)

    return pl.pallas_call(
        body, name=name, out_shape=jax.ShapeDtypeStruct((T, D), MXU_DTYPE), grid=(T // tr,),
        in_specs=[pl.BlockSpec((tr, D), lambda i: (i, 0)), pl.BlockSpec((1, D), lambda i: (0, 0))],
        out_specs=pl.BlockSpec((tr, D), lambda i: (i, 0)),
        compiler_params=_params(("parallel",), 3 * _nbytes((tr, D), F32)),
    )(x, g)


def _rms_bwd(x, g, dh, dres, name, tr=512):
    T, D = x.shape

    def body(x_ref, g_ref, dh_ref, dres_ref, dx_ref, gg_ref):
        @pl.when(pl.program_id(0) == 0)
        def _():
            gg_ref[...] = jnp.zeros_like(gg_ref)

        xv = x_ref[...]
        r = lax.rsqrt(jnp.mean(xv * xv, axis=-1, keepdims=True) + EPS)
        xn = xv * r
        dhv = dh_ref[...]
        dxn = dhv * g_ref[...]
        dx_ref[...] = dres_ref[...] + r * (dxn - xn * jnp.mean(dxn * xn, axis=-1, keepdims=True))
        gg_ref[...] += jnp.sum(dhv * xn, axis=0, keepdims=True)

    row = pl.BlockSpec((tr, D), lambda i: (i, 0))
    vec = pl.BlockSpec((1, D), lambda i: (0, 0))
    return pl.pallas_call(
        body, name=name,
        out_shape=(jax.ShapeDtypeStruct((T, D), F32), jax.ShapeDtypeStruct((1, D), F32)),
        grid=(T // tr,), in_specs=[row, vec, row, row], out_specs=(row, vec),
        compiler_params=_params(("arbitrary",), 6 * _nbytes((tr, D), F32)),
    )(x, g, dh, dres)


def _lat_fwd(z, gq, gkv, wq, wkv, cos_a, sin_a, tr=256):
    T = z.shape[0]
    lat_blk = (4 * D_MODEL) // LAT

    def body(z_ref, gq_ref, gkv_ref, wq_ref, wkv_ref, cos_ref, sin_ref, q_ref, k_ref, v_ref, cqn_ref, ckvn_ref):
        zl = z_ref[...]
        cos_v, sin_v = cos_ref[...], sin_ref[...]
        cq = zl[:, :Q_RANK]
        ckv = zl[:, Q_RANK:Q_RANK + KV_RANK]
        krb = zl[:, Q_RANK + KV_RANK:]
        cqn = ((cq * lax.rsqrt(jnp.mean(cq * cq, axis=-1, keepdims=True) + EPS)) * gq_ref[...]).astype(MXU_DTYPE)
        ckvn = ((ckv * lax.rsqrt(jnp.mean(ckv * ckv, axis=-1, keepdims=True) + EPS)) * gkv_ref[...]).astype(MXU_DTYPE)
        cqn_ref[...] = cqn
        ckvn_ref[...] = ckvn
        krr = _rope_mix(krb, cos_v, sin_v).astype(MXU_DTYPE)
        q = jnp.dot(cqn, wq_ref[...], preferred_element_type=F32)
        kv = jnp.dot(ckvn, wkv_ref[...], preferred_element_type=F32)
        for h in range(HEADS):
            o = h * HEAD_PAD
            q_ref[:, o:o + NOPE] = q[:, o:o + NOPE].astype(MXU_DTYPE)
            q_ref[:, o + NOPE:o + HEAD_PAD] = _rope_mix(q[:, o + NOPE:o + HEAD_PAD], cos_v, sin_v).astype(MXU_DTYPE)
            k_ref[:, o:o + NOPE] = kv[:, h * NOPE:(h + 1) * NOPE].astype(MXU_DTYPE)
            k_ref[:, o + NOPE:o + HEAD_PAD] = krr
        v_ref[...] = kv[:, HEADS * NOPE:].astype(MXU_DTYPE)

    def row(w):
        return pl.BlockSpec((tr, w), lambda i: (i, 0))

    def full(a):
        return pl.BlockSpec(a.shape, lambda i: (0, 0))

    return pl.pallas_call(
        body, name="lat_fwd",
        out_shape=(jax.ShapeDtypeStruct((T, HEADS * HEAD_PAD), MXU_DTYPE), jax.ShapeDtypeStruct((T, HEADS * HEAD_PAD), MXU_DTYPE),
                   jax.ShapeDtypeStruct((T, HEADS * NOPE), MXU_DTYPE), jax.ShapeDtypeStruct((T, Q_RANK), MXU_DTYPE),
                   jax.ShapeDtypeStruct((T, KV_RANK), MXU_DTYPE)),
        grid=(T // tr,),
        in_specs=[pl.BlockSpec((tr, LAT), lambda i: (i, lat_blk)), full(gq), full(gkv), full(wq), full(wkv), row(128), row(128)],
        out_specs=(row(HEADS * HEAD_PAD), row(HEADS * HEAD_PAD), row(HEADS * NOPE), row(Q_RANK), row(KV_RANK)),
        compiler_params=_params(("parallel",), 8 * _nbytes((tr, HEADS * HEAD_PAD), F32)),
    )(z, gq, gkv, wq, wkv, cos_a, sin_a)


ATT_BLOCK = 256
_SCALE = QK_DIM ** -0.5


def _causal_mask(n):
    return lax.broadcasted_iota(jnp.int32, (n, n), 1) <= lax.broadcasted_iota(jnp.int32, (n, n), 0)


def _causal_mask_t(n):
    return lax.broadcasted_iota(jnp.int32, (n, n), 0) <= lax.broadcasted_iota(jnp.int32, (n, n), 1)


ATT_HEADS = 4


def _attn_fwd(q, k, v, B, S):
    tq = ATT_BLOCK
    nq = S // tq
    T = B * S
    hp, groups = ATT_HEADS, HEADS // ATT_HEADS

    def body(q_ref, k_ref, v_ref, o_ref, *lse_refs):
        qi = pl.program_id(2)
        qs = [q_ref[:, t * HEAD_PAD:(t + 1) * HEAD_PAD] for t in range(hp)]

        def scores(j, t):
            rows = pl.ds(pl.multiple_of(j * tq, tq), tq)
            return lax.dot_general(k_ref[rows, t * HEAD_PAD:(t + 1) * HEAD_PAD], qs[t], _DIMS["nt"],
                                   preferred_element_type=F32)

        def step(j, carry, last):
            rows = pl.ds(pl.multiple_of(j * tq, tq), tq)
            out = []
            for t in range(hp):
                m, l, acc, st = carry[t]
                st_next = st if last else scores(j + 1, t)
                st = st * _SCALE
                if last:
                    st = jnp.where(_causal_mask_t(tq), st, NEG)
                m_new = jnp.maximum(m, jnp.max(st, axis=0, keepdims=True))
                alpha = jnp.exp(m - m_new)
                p = jnp.exp(st - m_new)
                l = alpha * l + jnp.sum(p, axis=0, keepdims=True)
                acc = alpha * acc + lax.dot_general(v_ref[rows, t * NOPE:(t + 1) * NOPE], p.astype(MXU_DTYPE),
                                                    _DIMS["tn"], preferred_element_type=F32)
                out.append((m_new, l, acc, st_next))
            return tuple(out)

        init = tuple((jnp.full((1, tq), NEG, F32), jnp.zeros((1, tq), F32), jnp.zeros((NOPE, tq), F32), scores(0, t))
                     for t in range(hp))
        carry = lax.fori_loop(0, qi, lambda j, c: step(j, c, False), init)
        carry = step(qi, carry, True)
        for t in range(hp):
            m, l, acc, _ = carry[t]
            o_ref[:, t * NOPE:(t + 1) * NOPE] = (acc / l).T
            lse_refs[t][0] = m + jnp.log(l)

    lse_sds = jax.ShapeDtypeStruct((groups * B * nq, 1, tq), F32)
    lse_spec = pl.BlockSpec((1, 1, tq), lambda b, h, i: ((h * B + b) * nq + i, 0, 0))
    return pl.pallas_call(
        body, name="attn_fwd",
        out_shape=(jax.ShapeDtypeStruct((T, HEADS * NOPE), F32),) + (lse_sds,) * hp,
        grid=(B, groups, nq),
        in_specs=[pl.BlockSpec((tq, hp * HEAD_PAD), lambda b, h, i: (b * nq + i, h)),
                  pl.BlockSpec((S, hp * HEAD_PAD), lambda b, h, i: (b, h)),
                  pl.BlockSpec((S, hp * NOPE), lambda b, h, i: (b, h))],
        out_specs=(pl.BlockSpec((tq, hp * NOPE), lambda b, h, i: (b * nq + i, h)),) + (lse_spec,) * hp,
        compiler_params=_params(("parallel", "parallel", "arbitrary"), 4 * hp * _nbytes((S, HEAD_PAD), MXU_DTYPE)),
    )(q, k, v)


def _attn_bwd(q, k, v, do, lses, delta, B, S):
    tq = ATT_BLOCK
    nq = S // tq
    T = B * S
    hp, groups = ATT_HEADS, HEADS // ATT_HEADS

    def body(q_ref, k_ref, v_ref, do_ref, *refs):
        lse_refs, dl_refs = refs[:hp], refs[hp:2 * hp]
        dq_out, dk_ref, dv_ref, dq_ref = refs[2 * hp:]
        kj = pl.program_id(2)

        @pl.when(kj == 0)
        def _():
            dq_ref[...] = jnp.zeros_like(dq_ref)

        def products(i, t):
            rows = pl.ds(pl.multiple_of(i * tq, tq), tq)
            st = lax.dot_general(k_ref[:, t * HEAD_PAD:(t + 1) * HEAD_PAD], q_ref[rows, t * HEAD_PAD:(t + 1) * HEAD_PAD],
                                 _DIMS["nt"], preferred_element_type=F32)
            dpt = lax.dot_general(v_ref[:, t * NOPE:(t + 1) * NOPE], do_ref[rows, t * NOPE:(t + 1) * NOPE],
                                  _DIMS["nt"], preferred_element_type=F32)
            return st, dpt

        def step(i, carry, masked):
            rows = pl.ds(pl.multiple_of(i * tq, tq), tq)
            nxt = jnp.minimum(i + 1, nq - 1)
            out = []
            for t in range(hp):
                dk, dv, st, dpt = carry[t]
                st_next, dpt_next = products(nxt, t)
                qk_cols = slice(t * HEAD_PAD, (t + 1) * HEAD_PAD)
                v_cols = slice(t * NOPE, (t + 1) * NOPE)
                p = jnp.exp(st * _SCALE - lse_refs[t][i])
                if masked:
                    p = jnp.where(_causal_mask_t(tq), p, 0.0)
                dv = dv + jnp.dot(p.astype(MXU_DTYPE), do_ref[rows, v_cols], preferred_element_type=F32)
                ds = (p * (dpt - dl_refs[t][i]) * _SCALE).astype(MXU_DTYPE)
                dk = dk + jnp.dot(ds, q_ref[rows, qk_cols], preferred_element_type=F32)
                dq_ref[rows, qk_cols] += lax.dot_general(ds, k_ref[:, qk_cols], _DIMS["tn"], preferred_element_type=F32)
                out.append((dk, dv, st_next, dpt_next))
            return tuple(out)

        init = tuple((jnp.zeros((tq, HEAD_PAD), F32), jnp.zeros((tq, NOPE), F32)) + products(kj, t) for t in range(hp))
        carry = step(kj, init, True)
        carry = lax.fori_loop(kj + 1, nq, lambda i, c: step(i, c, False), carry)
        for t in range(hp):
            dk_ref[:, t * HEAD_PAD:(t + 1) * HEAD_PAD] = carry[t][0].astype(dk_ref.dtype)
            dv_ref[:, t * NOPE:(t + 1) * NOPE] = carry[t][1].astype(dv_ref.dtype)

        @pl.when(kj == nq - 1)
        def _():
            dq_out[...] = dq_ref[...].astype(dq_out.dtype)

    seq = lambda w: pl.BlockSpec((S, w), lambda b, h, j: (b, h))
    blk = lambda w: pl.BlockSpec((tq, w), lambda b, h, j: (b * nq + j, h))
    lse_spec = pl.BlockSpec((nq, 1, tq), lambda b, h, j: (h * B + b, 0, 0))
    dl_specs = [pl.BlockSpec((nq, 1, tq), lambda b, h, j, t=t: ((h * hp + t) * B + b, 0, 0)) for t in range(hp)]
    return pl.pallas_call(
        body, name="attn_bwd",
        out_shape=(jax.ShapeDtypeStruct((T, HEADS * HEAD_PAD), MXU_DTYPE), jax.ShapeDtypeStruct((T, HEADS * HEAD_PAD), MXU_DTYPE),
                   jax.ShapeDtypeStruct((T, HEADS * NOPE), MXU_DTYPE)),
        grid=(B, groups, nq),
        in_specs=[seq(hp * HEAD_PAD), blk(hp * HEAD_PAD), blk(hp * NOPE), seq(hp * NOPE)] + [lse_spec] * hp + dl_specs,
        out_specs=(seq(hp * HEAD_PAD), blk(hp * HEAD_PAD), blk(hp * NOPE)),
        scratch_shapes=[pltpu.VMEM((S, hp * HEAD_PAD), F32)],
        compiler_params=_params(("parallel", "parallel", "arbitrary"), 8 * hp * _nbytes((S, HEAD_PAD), F32)),
    )(q, k, v, do, *lses, *([delta] * hp))


MIX_ROWS = 256


def _tril_weights(ws_ref, g):
    return jnp.where(_causal_mask(CHUNK), ws_ref[g], 0.0).astype(MXU_DTYPE)


def _layer_norm_stats(va):
    mu = jnp.mean(va, axis=-1, keepdims=True)
    xc = va - mu
    rs = lax.rsqrt(jnp.mean(xc * xc, axis=-1, keepdims=True) + EPS)
    return xc * rs


def _mix_specs(tr):
    zcol = lambda c: pl.BlockSpec((tr, D_MODEL), lambda i, c=c: (i, c))
    row = pl.BlockSpec((tr, D_MODEL), lambda i: (i, 0))
    vec = pl.BlockSpec((1, D_MODEL), lambda i: (0, 0))
    ws = pl.BlockSpec((A_GROUPS, CHUNK, CHUNK), lambda i: (0, 0, 0))
    bs = pl.BlockSpec((CHUNK, 128), lambda i: (0, 0))
    return zcol, row, vec, ws, bs


def _mix_fwd(z, yb, ln_g, ln_b, ws, bs_t):
    T = z.shape[0]
    tr = MIX_ROWS
    zcol, row, vec, ws_spec, bs_spec = _mix_specs(tr)

    def body(zu_ref, zv_ref, zga_ref, zgb_ref, yb_ref, g_ref, b_ref, ws_ref, bs_ref, out_ref, vn_s):
        vhat = _layer_norm_stats(_gelu(zv_ref[...]))
        vn_s[...] = (vhat * g_ref[...] + b_ref[...]).astype(MXU_DTYPE)
        for g in range(A_GROUPS):
            w = _tril_weights(ws_ref, g)
            bias = bs_ref[:, g:g + 1]
            cols = slice(g * CHUNK, (g + 1) * CHUNK)
            for c in range(tr // CHUNK):
                rows = slice(c * CHUNK, (c + 1) * CHUNK)
                mixed = jnp.dot(w, vn_s[rows, cols], preferred_element_type=F32) + bias
                ya = _gelu(zu_ref[rows, cols]) * mixed
                merged = _sigmoid(zga_ref[rows, cols]) * ya + _sigmoid(zgb_ref[rows, cols]) * yb_ref[rows, cols]
                out_ref[rows, cols] = merged.astype(MXU_DTYPE)

    return pl.pallas_call(
        body, name="mix_fwd", out_shape=jax.ShapeDtypeStruct((T, D_MODEL), MXU_DTYPE), grid=(T // tr,),
        in_specs=[zcol(0), zcol(1), zcol(2), zcol(3), row, vec, vec, ws_spec, bs_spec], out_specs=row,
        scratch_shapes=[pltpu.VMEM((tr, D_MODEL), MXU_DTYPE)],
        compiler_params=_params(("parallel",), 8 * _nbytes((tr, D_MODEL), F32)),
    )(z, z, z, z, yb, ln_g, ln_b, ws, bs_t)


def _mix_bwd(z, yb, dm, ln_g, ln_b, ws, bs_t):
    T = z.shape[0]
    tr = MIX_ROWS
    zcol, row, vec, ws_spec, bs_spec = _mix_specs(tr)

    def body(zu_ref, zv_ref, zga_ref, zgb_ref, yb_ref, dm_ref, g_ref, b_ref, ws_ref, bs_ref,
             dz_ref, dyb_ref, dl_ref, gws_ref, gbs_ref, glg_ref, glb_ref, vn_s, dvn_s):
        @pl.when(pl.program_id(0) == 0)
        def _():
            gws_ref[...] = jnp.zeros_like(gws_ref)
            gbs_ref[...] = jnp.zeros_like(gbs_ref)
            glg_ref[...] = jnp.zeros_like(glg_ref)
            glb_ref[...] = jnp.zeros_like(glb_ref)

        lane = lax.broadcasted_iota(jnp.int32, (CHUNK, 128), 1)
        va, dgelu_v = _gelu_and_grad(zv_ref[...])
        mu = jnp.mean(va, axis=-1, keepdims=True)
        xc = va - mu
        rs = lax.rsqrt(jnp.mean(xc * xc, axis=-1, keepdims=True) + EPS)
        vhat = xc * rs
        vn_s[...] = (vhat * g_ref[...] + b_ref[...]).astype(MXU_DTYPE)
        gbs_acc = jnp.zeros((CHUNK, 128), F32)
        for g in range(A_GROUPS):
            w = _tril_weights(ws_ref, g)
            bias = bs_ref[:, g:g + 1]
            cols = slice(g * CHUNK, (g + 1) * CHUNK)
            gw_acc = jnp.zeros((CHUNK, CHUNK), F32)
            for c in range(tr // CHUNK):
                rows = slice(c * CHUNK, (c + 1) * CHUNK)
                vn = vn_s[rows, cols]
                mixed = jnp.dot(w, vn, preferred_element_type=F32) + bias
                ua, dgelu_u = _gelu_and_grad(zu_ref[rows, cols])
                dmv = dm_ref[rows, cols]
                sa = _sigmoid(zga_ref[rows, cols])
                dya = dmv * sa
                dz_ref[rows, 2 * D_MODEL + g * CHUNK:2 * D_MODEL + (g + 1) * CHUNK] = (
                    dmv * (ua * mixed) * (sa * (1.0 - sa))).astype(dz_ref.dtype)
                dz_ref[rows, cols] = (dya * mixed * dgelu_u).astype(dz_ref.dtype)
                dmix = dya * ua
                gbs_acc = gbs_acc + jnp.where(lane == g, jnp.sum(dmix, axis=-1, keepdims=True), 0.0)
                dmix_b = dmix.astype(MXU_DTYPE)
                gw_acc = gw_acc + lax.dot_general(dmix_b, vn, _DIMS["nt"], preferred_element_type=F32)
                dvn_s[rows, cols] = lax.dot_general(w, dmix_b, _DIMS["tn"], preferred_element_type=F32)
            gws_ref[g] += jnp.where(_causal_mask(CHUNK), gw_acc, 0.0)
        gbs_ref[...] += gbs_acc

        dvn = dvn_s[...]
        glg_ref[...] += jnp.sum(dvn * vhat, axis=0, keepdims=True)
        glb_ref[...] += jnp.sum(dvn, axis=0, keepdims=True)
        dvh = dvn * g_ref[...]
        dva = rs * (dvh - jnp.mean(dvh, axis=-1, keepdims=True) - vhat * jnp.mean(dvh * vhat, axis=-1, keepdims=True))
        dz_ref[:, D_MODEL:2 * D_MODEL] = (dva * dgelu_v).astype(dz_ref.dtype)

        dmv = dm_ref[...]
        ybv = yb_ref[...]
        sb = _sigmoid(zgb_ref[...])
        dyb = dmv * sb
        dyb_ref[...] = dyb.astype(dyb_ref.dtype)
        dz_ref[:, 3 * D_MODEL:4 * D_MODEL] = (dmv * ybv * (sb * (1.0 - sb))).astype(dz_ref.dtype)
        dz_ref[:, 4 * D_MODEL:] = jnp.zeros((tr, LAT), dz_ref.dtype)
        prod = dyb * ybv
        sel = (lax.broadcasted_iota(jnp.int32, (HEADS, D_MODEL), 1) // NOPE
               == lax.broadcasted_iota(jnp.int32, (HEADS, D_MODEL), 0)).astype(jnp.bfloat16)
        hi = prod.astype(jnp.bfloat16)
        rest = prod - hi.astype(F32)
        mid = rest.astype(jnp.bfloat16)
        lo = (rest - mid.astype(F32)).astype(jnp.bfloat16)
        dl_ref[...] = (lax.dot_general(sel, hi, _DIMS["nt"], preferred_element_type=F32)
                       + lax.dot_general(sel, mid, _DIMS["nt"], preferred_element_type=F32)
                       + lax.dot_general(sel, lo, _DIMS["nt"], preferred_element_type=F32))

    return pl.pallas_call(
        body, name="mix_bwd",
        out_shape=(jax.ShapeDtypeStruct((T, IN_PAD), MXU_DTYPE), jax.ShapeDtypeStruct((T, D_MODEL), MXU_DTYPE),
                   jax.ShapeDtypeStruct((HEADS, T), F32), jax.ShapeDtypeStruct((A_GROUPS, CHUNK, CHUNK), F32),
                   jax.ShapeDtypeStruct((CHUNK, 128), F32), jax.ShapeDtypeStruct((1, D_MODEL), F32),
                   jax.ShapeDtypeStruct((1, D_MODEL), F32)),
        grid=(T // tr,),
        in_specs=[zcol(0), zcol(1), zcol(2), zcol(3), row, row, vec, vec, ws_spec, bs_spec],
        out_specs=(pl.BlockSpec((tr, IN_PAD), lambda i: (i, 0)), row, pl.BlockSpec((HEADS, tr), lambda i: (0, i)),
                   ws_spec, bs_spec, vec, vec),
        scratch_shapes=[pltpu.VMEM((tr, D_MODEL), MXU_DTYPE), pltpu.VMEM((tr, D_MODEL), F32)],
        compiler_params=_params(("arbitrary",), 12 * _nbytes((tr, D_MODEL), F32)),
    )(z, z, z, z, yb, dm, ln_g, ln_b, ws, bs_t)


def _lat_bwd(dz, z, dq, dk, dv, gq, gkv, wq, wkv, cos_a, sin_a, tr=256):
    T = z.shape[0]
    lat_blk = (4 * D_MODEL) // LAT

    def body(dz_in, z_ref, dq_ref, dk_ref, dv_ref, gq_ref, gkv_ref, wq_ref, wkv_ref, cos_ref, sin_ref,
             dz_ref, dqr_ref, dkv_ref, ggq_ref, ggkv_ref):
        del dz_in

        @pl.when(pl.program_id(0) == 0)
        def _():
            ggq_ref[...] = jnp.zeros_like(ggq_ref)
            ggkv_ref[...] = jnp.zeros_like(ggkv_ref)

        cos_v, sin_v = cos_ref[...], sin_ref[...]
        dkr = jnp.zeros((tr, 128), F32)
        for h in range(HEADS):
            o = h * HEAD_PAD
            dqr_ref[:, o:o + NOPE] = dq_ref[:, o:o + NOPE].astype(MXU_DTYPE)
            dqr_ref[:, o + NOPE:o + HEAD_PAD] = _rope_mix_bwd(dq_ref[:, o + NOPE:o + HEAD_PAD], cos_v, sin_v).astype(MXU_DTYPE)
            dkv_ref[:, h * NOPE:(h + 1) * NOPE] = dk_ref[:, o:o + NOPE].astype(MXU_DTYPE)
            dkr = dkr + _rope_mix_bwd(dk_ref[:, o + NOPE:o + HEAD_PAD], cos_v, sin_v)
        dkv_ref[:, HEADS * NOPE:] = dv_ref[...]
        dcqn = lax.dot_general(dqr_ref[...], wq_ref[...], _DIMS["nt"], preferred_element_type=F32)
        dckvn = lax.dot_general(dkv_ref[...], wkv_ref[...], _DIMS["nt"], preferred_element_type=F32)

        zl = z_ref[...]

        def rms_bwd(c, dn, g_ref, gg_ref):
            r = lax.rsqrt(jnp.mean(c * c, axis=-1, keepdims=True) + EPS)
            ch = c * r
            gg_ref[...] += jnp.sum(dn * ch, axis=0, keepdims=True)
            dch = dn * g_ref[...]
            return r * (dch - ch * jnp.mean(dch * ch, axis=-1, keepdims=True))

        dz_ref[:, :Q_RANK] = rms_bwd(zl[:, :Q_RANK], dcqn, gq_ref, ggq_ref).astype(dz_ref.dtype)
        dz_ref[:, Q_RANK:Q_RANK + KV_RANK] = rms_bwd(zl[:, Q_RANK:Q_RANK + KV_RANK], dckvn, gkv_ref, ggkv_ref).astype(dz_ref.dtype)
        dz_ref[:, Q_RANK + KV_RANK:] = dkr.astype(dz_ref.dtype)

    def row(w):
        return pl.BlockSpec((tr, w), lambda i: (i, 0))

    def full(a):
        return pl.BlockSpec(a.shape, lambda i: (0, 0))

    lat = pl.BlockSpec((tr, LAT), lambda i: (i, lat_blk))
    return pl.pallas_call(
        body, name="lat_bwd",
        out_shape=(jax.ShapeDtypeStruct(dz.shape, dz.dtype), jax.ShapeDtypeStruct((T, HEADS * HEAD_PAD), MXU_DTYPE),
                   jax.ShapeDtypeStruct((T, 2 * HEADS * NOPE), MXU_DTYPE), jax.ShapeDtypeStruct(gq.shape, F32),
                   jax.ShapeDtypeStruct(gkv.shape, F32)),
        grid=(T // tr,),
        in_specs=[pl.BlockSpec(memory_space=pl.ANY), lat, row(HEADS * HEAD_PAD), row(HEADS * HEAD_PAD), row(HEADS * NOPE),
                  full(gq), full(gkv), full(wq), full(wkv), row(128), row(128)],
        out_specs=(lat, row(HEADS * HEAD_PAD), row(2 * HEADS * NOPE), full(gq), full(gkv)),
        input_output_aliases={0: 0},
        compiler_params=_params(("arbitrary",), 8 * _nbytes((tr, HEADS * HEAD_PAD), F32)),
    )(dz, z, dq, dk, dv, gq, gkv, wq, wkv, cos_a, sin_a)


GATE_ROWS = 64
HALO = 8


def _taps(ref, half, r, first):
    C = GATE_ROWS
    if first:
        xs = jnp.concatenate([jnp.zeros((HALO, ref.shape[-1]), F32), ref[half, 0:C, :]], axis=0)
    else:
        xs = ref[half, pl.ds(pl.multiple_of(r * C - HALO, HALO), C + HALO), :]
    return xs[HALO:, :], pltpu.roll(xs, 1, 0)[HALO:, :], pltpu.roll(xs, 2, 0)[HALO:, :]


def _conv_taps(taps, cw, cb):
    x0, x1, x2 = taps
    return cb + cw[0:1, :] * x2 + cw[1:2, :] * x1 + cw[2:3, :] * x0


def _fold8(x):
    acc = x[0:8, :]
    for i in range(1, x.shape[0] // 8):
        acc = acc + x[8 * i:8 * (i + 1), :]
    return acc


def _gate_fwd(up3, conv_w, conv_b, B, S):
    T = B * S
    W = FF_TILE
    C = GATE_ROWS

    def body(up_ref, cw_ref, cb_ref, act_ref, conv_ref):
        def chunk(r, first):
            gate = _conv_taps(_taps(up_ref, 0, r, first), cw_ref[0], cb_ref[0])
            val = _conv_taps(_taps(up_ref, 1, r, first), cw_ref[1], cb_ref[1])
            rows = pl.ds(0 if first else pl.multiple_of(r * C, C), C)
            conv_ref[0, rows, :] = gate.astype(conv_ref.dtype)
            conv_ref[1, rows, :] = val.astype(conv_ref.dtype)
            act_ref[rows, :] = (gate * _sigmoid(gate) * val).astype(act_ref.dtype)

        chunk(0, True)

        @pl.loop(1, S // C)
        def _(r):
            chunk(r, False)

    up_spec = pl.BlockSpec((2, S, W), lambda b, j: (0, b, j))
    return pl.pallas_call(
        body, name="gate_fwd",
        out_shape=(jax.ShapeDtypeStruct((T, D_FF), MXU_DTYPE), jax.ShapeDtypeStruct((2, T, D_FF), MXU_DTYPE)),
        grid=(B, N_FF_TILES),
        in_specs=[up_spec, pl.BlockSpec((2, 3, W), lambda b, j: (0, 0, j)), pl.BlockSpec((2, 1, W), lambda b, j: (0, 0, j))],
        out_specs=(pl.BlockSpec((S, W), lambda b, j: (b, j)), up_spec),
        compiler_params=_params(("parallel", "parallel"), 8 * _nbytes((S, W), F32)),
    )(up3, conv_w, conv_b)


def _gate_bwd(up3, conv3, dact, conv_w, B, S):
    T = B * S
    W = FF_TILE
    C = GATE_ROWS

    def body(up_ref, conv_ref, da_ref, cw_ref, dup_ref, gcw_ref, gcb_ref, d_s):
        @pl.when(pl.program_id(1) == 0)
        def _():
            gcw_ref[...] = jnp.zeros_like(gcw_ref)
            gcb_ref[...] = jnp.zeros_like(gcb_ref)

        @pl.loop(0, S // C)
        def _(r):
            rows = pl.ds(pl.multiple_of(r * C, C), C)
            gate, val = conv_ref[0, rows, :].astype(F32), conv_ref[1, rows, :].astype(F32)
            sg = _sigmoid(gate)
            da = da_ref[rows, :]
            d_s[0, rows, :] = da * val * (sg * (1.0 + gate * (1.0 - sg)))
            d_s[1, rows, :] = da * (gate * sg)

        d_s[:, S:S + HALO, :] = jnp.zeros((2, HALO, W), F32)

        def chunk(r, sums):
            base = pl.multiple_of(r * C, C)
            out = []
            for half in (0, 1):
                ds_ = d_s[half, pl.ds(base, C + HALO), :]
                d0, d1, d2 = ds_[:C, :], pltpu.roll(ds_, C + HALO - 1, 0)[:C, :], pltpu.roll(ds_, C + HALO - 2, 0)[:C, :]
                cw = cw_ref[half]
                dup_ref[half, pl.ds(base, C), :] = (cw[2:3, :] * d0 + cw[1:2, :] * d1 + cw[0:1, :] * d2).astype(dup_ref.dtype)
                x = up_ref[half, pl.ds(base, C), :]
                sb, s0, s1, s2 = sums[half]
                out.append((sb + _fold8(d0), s0 + _fold8(d2 * x), s1 + _fold8(d1 * x), s2 + _fold8(d0 * x)))
            return tuple(out)

        zeros = tuple(tuple(jnp.zeros((8, W), F32) for _ in range(4)) for _ in range(2))
        sums = lax.fori_loop(0, S // C, chunk, zeros)
        for half in (0, 1):
            sb, s0, s1, s2 = sums[half]
            gcb_ref[half] += jnp.sum(sb, axis=0, keepdims=True)
            gcw_ref[half, 0:1, :] += jnp.sum(s0, axis=0, keepdims=True)
            gcw_ref[half, 1:2, :] += jnp.sum(s1, axis=0, keepdims=True)
            gcw_ref[half, 2:3, :] += jnp.sum(s2, axis=0, keepdims=True)

    up_spec = pl.BlockSpec((2, S, W), lambda j, b: (0, b, j))
    cw_spec = pl.BlockSpec((2, 3, W), lambda j, b: (0, 0, j))
    cb_spec = pl.BlockSpec((2, 1, W), lambda j, b: (0, 0, j))
    return pl.pallas_call(
        body, name="gate_bwd",
        out_shape=(jax.ShapeDtypeStruct((2, T, D_FF), MXU_DTYPE), jax.ShapeDtypeStruct((2, 3, D_FF), F32),
                   jax.ShapeDtypeStruct((2, 1, D_FF), F32)),
        grid=(N_FF_TILES, B),
        in_specs=[up_spec, up_spec, pl.BlockSpec((S, W), lambda j, b: (b, j)), cw_spec],
        out_specs=(up_spec, cw_spec, cb_spec),
        scratch_shapes=[pltpu.VMEM((2, S + HALO, W), F32)],
        compiler_params=_params(("parallel", "arbitrary"), 12 * _nbytes((S, W), F32)),
    )(up3, conv3, dact, conv_w)


def _final(x2, tgt, g, tr=512):
    T, D = x2.shape

    def body(x_ref, t_ref, g_ref, dx_ref, loss_ref, gg_ref):
        @pl.when(pl.program_id(0) == 0)
        def _():
            loss_ref[...] = jnp.zeros_like(loss_ref)
            gg_ref[...] = jnp.zeros_like(gg_ref)

        xv = x_ref[...]
        gv = g_ref[...]
        r = lax.rsqrt(jnp.mean(xv * xv, axis=-1, keepdims=True) + EPS)
        xn = xv * r
        err = xn * gv - t_ref[...]
        loss_ref[...] += 0.5 * jnp.sum(jnp.mean(err * err, axis=-1, keepdims=True), axis=0, keepdims=True)
        dy = err * (1.0 / D)
        gg_ref[...] += jnp.sum(dy * xn, axis=0, keepdims=True)
        dxn = dy * gv
        dx_ref[...] = r * (dxn - xn * jnp.mean(dxn * xn, axis=-1, keepdims=True))

    row = pl.BlockSpec((tr, D), lambda i: (i, 0))
    vec = pl.BlockSpec((1, D), lambda i: (0, 0))
    return pl.pallas_call(
        body, name="final_loss",
        out_shape=(jax.ShapeDtypeStruct((T, D), F32), jax.ShapeDtypeStruct((1, 128), F32), jax.ShapeDtypeStruct((1, D), F32)),
        grid=(T // tr,), in_specs=[row, row, vec],
        out_specs=(row, pl.BlockSpec((1, 128), lambda i: (0, 0)), vec),
        compiler_params=_params(("arbitrary",), 6 * _nbytes((tr, D), F32)),
    )(x2, tgt, g)


def _sum_slabs(parts, name, tr):
    rows, cols = parts[0].shape
    n = len(parts)

    def body(*refs):
        acc = refs[0][...]
        for r in refs[1:n]:
            acc = acc + r[...]
        refs[n][...] = acc

    blk = pl.BlockSpec((tr, cols), lambda i: (i, 0))
    return pl.pallas_call(
        body, name=name, out_shape=jax.ShapeDtypeStruct((rows, cols), F32), grid=(rows // tr,),
        in_specs=[blk] * n, out_specs=blk,
        compiler_params=_params(("parallel",), (n + 1) * _nbytes((tr, cols), F32)),
    )(*parts)


ADAMW_BLOCK_BYTES = 2400 * 1024


def _adamw(w, g, m, v, name, copy_grad=False):
    lead = w.ndim == 3
    rows, cols = w.shape[-2:]
    fits = [d for d in range(8, rows + 1, 8) if rows % d == 0 and d * cols * 4 <= ADAMW_BLOCK_BYTES]
    tr = max(fits) if fits else rows
    c1 = 1.0 - ADAM_B1 ** ADAM_STEP
    c2 = 1.0 - ADAM_B2 ** ADAM_STEP

    def body(w_ref, g_ref, m_ref, v_ref, d_ref, nm_ref, nv_ref, *g_out):
        gv = g_ref[...]
        nm = ADAM_B1 * m_ref[...] + (1.0 - ADAM_B1) * gv
        nv = ADAM_B2 * v_ref[...] + (1.0 - ADAM_B2) * (gv * gv)
        nm_ref[...] = nm
        nv_ref[...] = nv
        d_ref[...] = -ADAM_LR * ((nm / c1) / (jnp.sqrt(nv / c2) + ADAM_EPS) + ADAM_WD * w_ref[...])
        if copy_grad:
            g_out[0][...] = gv

    blk = pl.BlockSpec((None, tr, cols), lambda i: (0, i, 0)) if lead else pl.BlockSpec((tr, cols), lambda i: (i, 0))
    sds = jax.ShapeDtypeStruct(w.shape, F32)
    n_out = 4 if copy_grad else 3
    return pl.pallas_call(
        body, name=name, out_shape=(sds,) * n_out, grid=(rows // tr,), in_specs=[blk] * 4, out_specs=(blk,) * n_out,
        compiler_params=_params(("parallel",), (4 + n_out) * _nbytes((tr, cols), F32)),
    )(w, g, m, v)


_ANY = pl.BlockSpec(memory_space=pl.ANY)


def _place():
    x, y, c = lax.axis_index("x"), lax.axis_index("y"), lax.axis_index("c")
    chips = [(1 - x, y), (x, 1 - y), (1 - x, 1 - y)]
    return x, y, c, chips


def _forward_halves(lands):
    n = len(lands)

    def body(*refs):
        outs, send, recv = refs[n:2 * n], refs[2 * n], refs[2 * n + 1]
        x, y, c, chips = _place()
        cps = []
        for w in range(n):
            for j, (px, py) in enumerate(chips):
                landed = outs[w].at[2 * px + py, c]
                cps.append(pltpu.make_async_remote_copy(
                    src_ref=landed, dst_ref=landed, send_sem=send.at[3 * w + j], recv_sem=recv.at[3 * w + j],
                    device_id=(x, y, 1 - c), device_id_type=MESH))
        for cp in cps:
            cp.start()
        for w in range(n):
            for j, (px, py) in enumerate(chips):
                other = outs[w].at[2 * px + py, 1 - c]
                pltpu.make_async_remote_copy(src_ref=other, dst_ref=other, send_sem=send.at[3 * w + j],
                                             recv_sem=recv.at[3 * w + j], device_id=(x, y, 1 - c),
                                             device_id_type=MESH).wait_recv()
        for cp in cps:
            cp.wait_send()

    dma = lambda k: pltpu.SemaphoreType.DMA((k,))
    return pl.pallas_call(
        body, name="gather_forward_halves", out_shape=tuple(jax.ShapeDtypeStruct(a.shape, a.dtype) for a in lands),
        in_specs=[_ANY] * n, out_specs=tuple([_ANY] * n), input_output_aliases={w: w for w in range(n)},
        scratch_shapes=[dma(3 * n), dma(3 * n)],
    )(*lands)


_HBM = pl.BlockSpec(memory_space=pltpu.HBM)
_SEM = pl.BlockSpec(memory_space=pltpu.SEMAPHORE)
_EFFECT = pltpu.SideEffectType.DATAFLOW_SIDE_EFFECTING


SEMS_PER_ARRAY = 8


def _exchange_copies(srcs, lands, send, recv, mode):
    x, y, c, chips = _place()
    if mode == "halves":
        cps = []
        for w, (src, land) in enumerate(zip(srcs, lands)):
            pieces = [(src.at[c], land.at[2 * x + y, c], (px, py, c)) for px, py in chips]
            pieces.append((src, land.at[2 * x + y], (x, y, 1 - c)))
            for k, (piece, dst, peer) in enumerate(pieces):
                cps.append(pltpu.make_async_remote_copy(
                    src_ref=piece, dst_ref=dst, send_sem=send.at[SEMS_PER_ARRAY * w + k],
                    recv_sem=recv.at[SEMS_PER_ARRAY * w + k], device_id=peer, device_id_type=MESH))
        return cps
    if mode == "swap":
        return [pltpu.make_async_remote_copy(
            src_ref=src.at[:, 1 - c], dst_ref=land, send_sem=send.at[SEMS_PER_ARRAY * w],
            recv_sem=recv.at[SEMS_PER_ARRAY * w], device_id=(x, y, 1 - c), device_id_type=MESH)
            for w, (src, land) in enumerate(zip(srcs, lands))]
    if mode == "all":
        flips = [(fx, fy, fc) for fx in (0, 1) for fy in (0, 1) for fc in (0, 1)][1:]
        peers = [(x ^ fx, y ^ fy, c ^ fc) for fx, fy, fc in flips]
        slot = 4 * x + 2 * y + c
    else:
        peers = [(px, py, c) for px, py in chips] + ([(x, y, 1 - c)] if mode == "gather" else [])
        slot = 2 * x + y
    cps = []
    for w, (src, land) in enumerate(zip(srcs, lands)):
        for k, peer in enumerate(peers):
            piece = src.at[2 * peer[0] + peer[1]] if mode == "scatter" else src
            cps.append(pltpu.make_async_remote_copy(
                src_ref=piece, dst_ref=land.at[slot], send_sem=send.at[SEMS_PER_ARRAY * w + k],
                recv_sem=recv.at[SEMS_PER_ARRAY * w + k], device_id=peer, device_id_type=MESH))
    return cps


def _exchange_start(srcs, name, mode, after):
    n = len(srcs)
    if mode == "swap":
        land_shapes = [(s.shape[0],) + s.shape[2:] for s in srcs]
    else:
        lead = {"gather": (N_CHIPS,), "halves": (N_CHIPS,), "scatter": (), "all": (2 * N_CHIPS,)}[mode]
        land_shapes = [lead + s.shape for s in srcs]

    def body(*refs):
        src_refs, land_refs = refs[:n], refs[n:2 * n]
        send, recv = refs[2 * n + 1], refs[2 * n + 2]
        token = refs[-1]
        for cp in _exchange_copies(src_refs, land_refs, send, recv, mode):
            cp.start()
        token[...] = jnp.zeros_like(token)

    sems = pltpu.SemaphoreType.DMA((SEMS_PER_ARRAY * n,))
    out = pl.pallas_call(
        body, name=name,
        out_shape=(sems, sems, *[pltpu.HBM(s.shape, s.dtype) for s in srcs],
                   *[pltpu.HBM(shp, s.dtype) for shp, s in zip(land_shapes, srcs)], jax.ShapeDtypeStruct((8, 128), F32)),
        in_specs=[_HBM] * (2 * n) + [_ANY],
        out_specs=(_SEM, _SEM, *[_HBM] * (2 * n), pl.BlockSpec(memory_space=pltpu.VMEM)),
        input_output_aliases={i: 2 + i for i in range(2 * n)},
        compiler_params=pltpu.CompilerParams(has_side_effects=_EFFECT),
    )(*[pltpu.with_memory_space_constraint(s, pltpu.HBM) for s in srcs],
      *[pltpu.with_memory_space_constraint(lax.empty(shp, s.dtype), pltpu.HBM) for shp, s in zip(land_shapes, srcs)],
      after)
    return out[0], out[1], out[2:2 + n], out[2 + n:2 + 2 * n], out[-1]


def _exchange_wait(started, name, mode, after):
    send, recv, src_thru, land_thru, _ = started
    n = len(src_thru)
    after = list(after) if isinstance(after, (list, tuple)) else [after]

    def body(*refs):
        src_refs, land_refs, send_ref, recv_ref = refs[:n], refs[n:2 * n], refs[2 * n], refs[2 * n + 1]
        for cp in _exchange_copies(src_refs, land_refs, send_ref, recv_ref, mode):
            cp.wait_send()
            cp.wait_recv()

    out = pl.pallas_call(
        body, name=name,
        out_shape=tuple(pltpu.HBM(a.shape, a.dtype) for a in list(src_thru) + list(land_thru)),
        in_specs=[_HBM] * (2 * n) + [_SEM, _SEM] + [_ANY] * len(after), out_specs=tuple([_HBM] * (2 * n)),
        input_output_aliases={i: i for i in range(2 * n)},
        compiler_params=pltpu.CompilerParams(has_side_effects=_EFFECT),
    )(*src_thru, *land_thru, send, recv, *after)
    return out[:n], out[n:]


def _swap_halves(gs, name):
    n = len(gs)

    def body(*refs):
        ins, outs, send, recv = refs[:n], refs[n:2 * n], refs[2 * n], refs[2 * n + 1]
        x, y, c, _ = _place()
        cps = []
        for w in range(n):
            cps.append(pltpu.make_async_remote_copy(
                src_ref=ins[w].at[:, 1 - c], dst_ref=outs[w], send_sem=send.at[w], recv_sem=recv.at[w],
                device_id=(x, y, 1 - c), device_id_type=MESH))
        for cp in cps:
            cp.start()
        for cp in cps:
            cp.wait()

    return pl.pallas_call(
        body, name=name,
        out_shape=tuple(jax.ShapeDtypeStruct((g.shape[0],) + g.shape[2:], g.dtype) for g in gs),
        in_specs=[_ANY] * n, out_specs=tuple([_ANY] * n),
        scratch_shapes=[pltpu.SemaphoreType.DMA((n,)), pltpu.SemaphoreType.DMA((n,))],
    )(*gs)


GRAD_PAYLOAD = jnp.bfloat16


def _half_blocks(half_rows, cols):
    if (half_rows // 2) % 16 == 0:
        return (half_rows // 2, cols), (lambda r: (r, 0))
    assert cols % 256 == 0, (half_rows, cols)
    return (half_rows, cols // 2), (lambda r: (0, r))


def _pair_sum(gs, gots, name):
    n = len(gs)
    core = lax.axis_index("c").astype(jnp.int32).reshape(1)

    def body(core_ref, *refs):
        del core_ref
        for w in range(n):
            refs[2 * n + w][...] = (refs[w][...] + refs[n + w][...]).astype(GRAD_PAYLOAD)

    in_specs, out_specs, out_shape, nbytes = [], [], [], 0
    cuts = [_half_blocks(g.shape[1] // 2, g.shape[2]) for g in gs]
    for g, ((br, bc), at) in zip(gs, cuts):
        per_half = (g.shape[1] // 2) // br
        in_specs.append(pl.BlockSpec((1, br, bc), lambda s, r, core, at=at, per_half=per_half:
                                     (s, per_half * core[0] + at(r)[0], at(r)[1])))
        nbytes += 3 * _nbytes((br, bc), F32)
    for g, ((br, bc), at) in zip(gs, cuts):
        in_specs.append(pl.BlockSpec((1, br, bc), lambda s, r, core, at=at: (s,) + at(r)))
        out_specs.append(pl.BlockSpec((1, br, bc), lambda s, r, core, at=at: (s,) + at(r)))
        out_shape.append(jax.ShapeDtypeStruct((g.shape[0], g.shape[1] // 2, g.shape[2]), GRAD_PAYLOAD))
    return pl.pallas_call(
        body, name=name, out_shape=tuple(out_shape),
        grid_spec=pltpu.PrefetchScalarGridSpec(num_scalar_prefetch=1, grid=(N_CHIPS, 2), in_specs=in_specs,
                                               out_specs=tuple(out_specs)),
        compiler_params=_params(("parallel", "parallel"), nbytes),
    )(core, *gs, *gots)


def _chip_sum(ps, landed):
    n = len(ps)
    x, y, c = lax.axis_index("x"), lax.axis_index("y"), lax.axis_index("c")
    where = jnp.stack([2 * x + y, 2 * (1 - x) + y, 2 * x + (1 - y), 2 * (1 - x) + (1 - y), c]).astype(jnp.int32)

    def body(where_ref, *refs):
        del where_ref
        for w in range(n):
            terms = [refs[4 * w + t][...].astype(F32) for t in range(4)]
            refs[4 * n + w][...] = ((terms[0] + terms[1]) + terms[2]) + terms[3]

    in_specs, out_specs, out_shape, args, nbytes = [], [], [], [], 0
    for p, a in zip(ps, landed):
        (br, bc), at = _half_blocks(a.shape[1], a.shape[2])
        blk = (1, br, bc)
        in_specs.append(pl.BlockSpec(blk, lambda r, where, at=at: (where[0],) + at(r)))
        args.append(p)
        for t in (1, 2, 3):
            in_specs.append(pl.BlockSpec(blk, lambda r, where, t=t, at=at: (where[t],) + at(r)))
            args.append(a)
        out_specs.append(pl.BlockSpec(blk, lambda r, where, at=at: (where[4],) + at(r)))
        out_shape.append(jax.ShapeDtypeStruct((2,) + a.shape[1:], F32))
        nbytes += 4 * _nbytes(blk, F32)
    return pl.pallas_call(
        body, name="grad_chip_sum", out_shape=tuple(out_shape),
        grid_spec=pltpu.PrefetchScalarGridSpec(num_scalar_prefetch=1, grid=(2,), in_specs=in_specs,
                                               out_specs=tuple(out_specs)),
        compiler_params=_params(("parallel",), nbytes),
    )(where, *args)


def _join_halves(ss):
    n = len(ss)

    def body(*refs):
        outs, send, recv = refs[n:2 * n], refs[2 * n], refs[2 * n + 1]
        x, y, c, _ = _place()
        cps = []
        for w in range(n):
            cps.append(pltpu.make_async_remote_copy(
                src_ref=outs[w].at[c], dst_ref=outs[w].at[c], send_sem=send.at[w], recv_sem=recv.at[w],
                device_id=(x, y, 1 - c), device_id_type=MESH))
        for cp in cps:
            cp.start()
        for w in range(n):
            got = outs[w].at[1 - c]
            pltpu.make_async_remote_copy(src_ref=got, dst_ref=got, send_sem=send.at[w], recv_sem=recv.at[w],
                                         device_id=(x, y, 1 - c), device_id_type=MESH).wait_recv()
        for cp in cps:
            cp.wait_send()

    dma = lambda k: pltpu.SemaphoreType.DMA((k,))
    return pl.pallas_call(
        body, name="grad_join_halves",
        out_shape=tuple(jax.ShapeDtypeStruct(s.shape, s.dtype) for s in ss),
        in_specs=[_ANY] * n, out_specs=tuple([_ANY] * n), input_output_aliases={w: w for w in range(n)},
        scratch_shapes=[dma(n), dma(n)],
    )(*ss)


def _rot_cols(w, axis=-1):
    a, b = jnp.split(w, 2, axis=axis)
    return jnp.concatenate([-b, a], axis=axis)


def _rot_cols_t(g, axis=-1):
    a, b = jnp.split(g, 2, axis=axis)
    return jnp.concatenate([b, -a], axis=axis)


def _cols_from_chips(a):
    n, r, cs = a.shape
    return jnp.transpose(a, (1, 0, 2)).reshape(r, n * cs)


def _cols_to_chips(a):
    r, cc = a.shape
    return jnp.transpose(a.reshape(r, N_CHIPS, cc // N_CHIPS), (1, 0, 2))


def _conv_w_split(cw):
    return jnp.swapaxes(cw.reshape(3, 2, D_FF), 0, 1)


def _conv_w_join(g):
    return jnp.swapaxes(g, 0, 1).reshape(3, 2 * D_FF)


_SEG =(D_MODEL, 2 * D_MODEL, 2 * D_MODEL + Q_RANK, 2 * D_MODEL + Q_RANK + KV_RANK, 2 * D_MODEL + Q_RANK + KV_RANK + ROPE,
        3 * D_MODEL + Q_RANK + KV_RANK + ROPE)


def _w_in_t_to_pad(wt):
    u, v, cq, ckv, kr, ga, gb = jnp.split(wt, _SEG, axis=0)
    return jnp.concatenate([u, v, ga, gb, cq, ckv, kr, _rot_cols(kr, axis=0)], axis=0)


def _w_in_t_from_pad(gt):
    u, v, ga, gb, cq, ckv, kr, krr = jnp.split(
        gt, (D_MODEL, 2 * D_MODEL, 3 * D_MODEL, 4 * D_MODEL, 4 * D_MODEL + Q_RANK, 4 * D_MODEL + Q_RANK + KV_RANK,
             4 * D_MODEL + Q_RANK + KV_RANK + ROPE), axis=0)
    return jnp.concatenate([u, v, cq, ckv, kr + _rot_cols_t(krr, axis=0), ga, gb], axis=0)


def _w_uq_to_pad(w):
    t = w.reshape(Q_RANK, HEADS, QK_DIM)
    nope, rope = t[..., :NOPE], t[..., NOPE:]
    return jnp.concatenate([nope, rope, _rot_cols(rope)], axis=-1).reshape(Q_RANK, HEADS * HEAD_PAD)


def _w_uq_from_pad(g):
    t = g.reshape(Q_RANK, HEADS, HEAD_PAD)
    nope, rope, rot = t[..., :NOPE], t[..., NOPE:QK_DIM], t[..., QK_DIM:]
    return jnp.concatenate([nope, rope + _rot_cols_t(rot)], axis=-1).reshape(Q_RANK, HEADS * QK_DIM)


def _w_ukv_to_pad(w):
    t = w.reshape(KV_RANK, HEADS, 2, NOPE)
    return jnp.swapaxes(t, 1, 2).reshape(KV_RANK, 2 * HEADS * NOPE)


def _w_ukv_from_pad(g):
    t = g.reshape(KV_RANK, 2, HEADS, NOPE)
    return jnp.swapaxes(t, 1, 2).reshape(KV_RANK, 2 * HEADS * NOPE)


def _rope_tables(positions):
    inv_freq = 1.0 / (ROPE_THETA ** (jnp.arange(0, ROPE, 2, dtype=F32) / ROPE))
    ang = positions.astype(F32).reshape(-1, 1) * inv_freq
    cos, sin = jnp.cos(ang), jnp.sin(ang)
    zero = jnp.zeros((ang.shape[0], 64), F32)
    return jnp.concatenate([cos, cos, zero], axis=1), jnp.concatenate([sin, sin, zero], axis=1)


_BIG = ("w_in", "w_uq", "w_ukv", "w_out", "w_up", "w_down")
UP_SHARD = 2 * D_FF // N_CHIPS
TOKEN_TILE = 1024


def _local_step(x, positions, tgt, wts, in_weights, mixer_weights, ffn_weights, on_ffn_grads, on_mixer_grads):
    B, S, D = x.shape
    T = B * S
    xf = x.reshape(T, D)
    cos_a, sin_a = _rope_tables(positions)
    bs_t = jnp.pad(wts["a_spatial_b"].T, ((0, 0), (0, 128 - A_GROUPS)))

    h = _rms_fwd(xf, wts["mix_norm"], "norm1_fwd")
    wts = dict(wts)
    wts["w_in"], token = in_weights([h, cos_a, sin_a])
    tm = min(TOKEN_TILE, T)
    z = _mm(h, wts["w_in"], "nt", "in_proj", tm=tm, tn=1536, tk=D, n_outer=True, after=token)
    wts["w_q"], wts["w_kv"], wts["w_out"] = mixer_weights(z)
    q, k, v, cqn, ckvn = _lat_fwd(z, wts["q_a_norm"], wts["kv_a_norm"], wts["w_q"], wts["w_kv"], cos_a, sin_a)
    yb, *lses = _attn_fwd(q, k, v, B, S)
    merged = _mix_fwd(z, yb, wts["a_v_norm_g"], wts["a_v_norm_b"], wts["a_spatial_w"], bs_t)
    x1 = _mm(merged, wts["w_out"], "nn", "out_proj", tm=min(512, T), tn=D, tk=D, add=xf)
    h2 = _rms_fwd(x1, wts["ffn_norm"], "norm2_fwd")
    wts["w_up"], wts["w_down"], wts["conv_w"] = ffn_weights(h2)
    up_pre = _mm(h2, wts["w_up"], "nn", "up_proj", tm=tm, tn=UP_SHARD, tk=D, dims=(T, 2 * D_FF, D),
                 b_spec=pl.BlockSpec((None, D, UP_SHARD), lambda i, j, k: (j, 0, 0)),
                 o_spec=pl.BlockSpec((None, tm, UP_SHARD), lambda i, j, k: (j // 2, i, j % 2)), out_shape=(2, T, D_FF),
                 n_outer=True)
    act, up_conv = _gate_fwd(up_pre, wts["conv_w"], wts["conv_b"], B, S)
    x2 = _mm(act, wts["w_down"], "nn", "down_proj", tm=tm, tn=D, tk=1408, add=x1)
    dx2, loss_row, g_final = _final(x2, tgt.reshape(T, D), wts["final_norm"])

    g = {"final_norm": g_final}
    dact = _mm(dx2, wts["w_down"], "nt", "down_proj_dx", tm=tm, tn=1408, tk=D, n_outer=True)
    tk2, tk1 = min(2048, T), min(1024, T)
    g["w_down"], g["w_down_lo"] = _mm(act, dx2, "tn", "down_proj_dw", tm=1408, tn=D, tk=tk1, copy_dtype=GRAD_PAYLOAD)
    dup, g["conv_w"], g["conv_b"] = _gate_bwd(up_pre, up_conv, dact, wts["conv_w"], B, S)
    g["w_up"], g["w_up_lo"] = _mm(
        h2, dup, "tn", "up_proj_dw", tm=D, tn=UP_SHARD, tk=tk2, dims=(D, 2 * D_FF, T), copy_dtype=GRAD_PAYLOAD,
        b_spec=pl.BlockSpec((None, tk2, UP_SHARD), lambda i, j, k: (j // 2, k, j % 2)),
        o_spec=pl.BlockSpec((None, D, UP_SHARD), lambda i, j, k: (j, 0, 0)), out_shape=(N_CHIPS, D, UP_SHARD))
    token, ffn_sent = on_ffn_grads(g)
    dh2 = _mm(dup, wts["w_up"], "nt", "up_proj_dx", tm=tm, tn=D, tk=UP_SHARD, dims=(T, D, 2 * D_FF), after=token,
              a_spec=pl.BlockSpec((None, tm, UP_SHARD), lambda i, j, k: (k // 2, i, k % 2)),
              b_spec=pl.BlockSpec((None, D, UP_SHARD), lambda i, j, k: (k, 0, 0)))
    token = ffn_sent(dh2)
    dx1, g["ffn_norm"] = _rms_bwd(x1, wts["ffn_norm"], dh2, dx2, "norm2_bwd")
    dm = _mm(dx1, wts["w_out"], "nt", "out_proj_dx", tm=min(512, T), tn=D, tk=D, after=token)
    g["w_out"], g["w_out_lo"] = _mm(merged, dx1, "tn", "out_proj_dw", tm=D, tn=D, tk=tk1, copy_dtype=GRAD_PAYLOAD)
    dz, dyb, dl, g["a_spatial_w"], gbs, g["a_v_norm_g"], g["a_v_norm_b"] = _mix_bwd(
        z, yb, dm, wts["a_v_norm_g"], wts["a_v_norm_b"], wts["a_spatial_w"], bs_t)
    g["a_spatial_b"] = gbs[:, :A_GROUPS].T
    delta = dl.reshape(HEADS * T // ATT_BLOCK, 1, ATT_BLOCK)
    dq, dk, dv = _attn_bwd(q, k, v, dyb, lses, delta, B, S)
    dz, dq_raw, dkv, g["q_a_norm"], g["kv_a_norm"] = _lat_bwd(
        dz, z, dq, dk, dv, wts["q_a_norm"], wts["kv_a_norm"], wts["w_q"], wts["w_kv"], cos_a, sin_a)
    g["w_q"] = _mm(cqn, dq_raw, "tn", "q_proj_dw", tm=Q_RANK, tn=HEADS * HEAD_PAD, tk=tk2)
    g["w_kv"] = _mm(ckvn, dkv, "tn", "kv_proj_dw", tm=KV_RANK, tn=2 * HEADS * NOPE, tk=tk2)
    g["w_in"] = _mm(dz, h, "tn", "in_proj_dw", tm=1536, tn=D, tk=tk2)
    token = on_mixer_grads(g)
    dh = _mm(dz, wts["w_in"], "nn", "in_proj_dx", tm=tm, tn=D, tk=1536, after=token)
    dx, g["mix_norm"] = _rms_bwd(xf, wts["mix_norm"], dh, dx1, "norm1_bwd")
    return loss_row[0, 0], dx.reshape(B, S, D), g


_SMALL = (("mix_norm", (1, D_MODEL)), ("a_v_norm_g", (1, D_MODEL)), ("a_v_norm_b", (1, D_MODEL)),
          ("a_spatial_w", (A_GROUPS * CHUNK, CHUNK)), ("a_spatial_b", (1, A_GROUPS * CHUNK)), ("q_a_norm", (1, Q_RANK)),
          ("kv_a_norm", (1, KV_RANK)), ("ffn_norm", (1, D_MODEL)), ("conv_b", (1, 2 * D_FF)), ("final_norm", (1, D_MODEL)),
          ("conv_w", (3, 2 * D_FF)))
_SMALL_SIZE = sum(math.prod(s) for _, s in _SMALL)
_SMALL_ROWS = -(-(_SMALL_SIZE + 1) // (128 * 8)) * 8


def kernel(x, positions, mix_norm, w_in, a_v_norm_g, a_v_norm_b, a_spatial_w, a_spatial_b, q_a_norm, w_uq, kv_a_norm, w_ukv, w_out, ffn_norm, w_up, conv_w, conv_b, w_down, final_norm, loss_target, m_mix_norm, m_w_in, m_a_v_norm_g, m_a_v_norm_b, m_a_spatial_w, m_a_spatial_b, m_q_a_norm, m_w_uq, m_kv_a_norm, m_w_ukv, m_w_out, m_ffn_norm, m_w_up, m_conv_w, m_conv_b, m_w_down, m_final_norm, v_mix_norm, v_w_in, v_a_v_norm_g, v_a_v_norm_b, v_a_spatial_w, v_a_spatial_b, v_q_a_norm, v_w_uq, v_kv_a_norm, v_w_ukv, v_w_out, v_ffn_norm, v_w_up, v_conv_w, v_conv_b, v_w_down, v_final_norm):
    weights = dict(mix_norm=mix_norm, w_in=w_in, a_v_norm_g=a_v_norm_g, a_v_norm_b=a_v_norm_b, a_spatial_w=a_spatial_w,
                   a_spatial_b=a_spatial_b, q_a_norm=q_a_norm, w_uq=w_uq, kv_a_norm=kv_a_norm, w_ukv=w_ukv, w_out=w_out,
                   ffn_norm=ffn_norm, w_up=w_up, conv_w=conv_w, conv_b=conv_b, w_down=w_down, final_norm=final_norm)
    m_in = dict(mix_norm=m_mix_norm, w_in=m_w_in, a_v_norm_g=m_a_v_norm_g, a_v_norm_b=m_a_v_norm_b,
                a_spatial_w=m_a_spatial_w, a_spatial_b=m_a_spatial_b, q_a_norm=m_q_a_norm, w_uq=m_w_uq,
                kv_a_norm=m_kv_a_norm, w_ukv=m_w_ukv, w_out=m_w_out, ffn_norm=m_ffn_norm, w_up=m_w_up, conv_w=m_conv_w,
                conv_b=m_conv_b, w_down=m_w_down, final_norm=m_final_norm)
    v_in = dict(mix_norm=v_mix_norm, w_in=v_w_in, a_v_norm_g=v_a_v_norm_g, a_v_norm_b=v_a_v_norm_b,
                a_spatial_w=v_a_spatial_w, a_spatial_b=v_a_spatial_b, q_a_norm=v_q_a_norm, w_uq=v_w_uq,
                kv_a_norm=v_kv_a_norm, w_ukv=v_w_ukv, w_out=v_w_out, ffn_norm=v_ffn_norm, w_up=v_w_up, conv_w=v_conv_w,
                conv_b=v_conv_b, w_down=v_w_down, final_norm=v_final_norm)
    names = list(weights)
    chip = 2 * lax.axis_index("x") + lax.axis_index("y")

    def halves(a):
        return a.reshape(a.shape[:-2] + (2, a.shape[-2] // 2, a.shape[-1]))

    w_in_t = jnp.swapaxes(w_in[0], 0, 1).astype(MXU_DTYPE)
    w_in_gather = _exchange_start([jnp.stack(jnp.split(w_in_t, 2, axis=1))], "w_in_gather_start", "halves",
                                  after=positions)
    gathers = {}
    wts = dict(
        mix_norm=mix_norm, a_v_norm_g=a_v_norm_g, a_v_norm_b=a_v_norm_b, a_spatial_w=a_spatial_w[0],
        a_spatial_b=a_spatial_b[0], q_a_norm=q_a_norm, kv_a_norm=kv_a_norm, ffn_norm=ffn_norm,
        final_norm=final_norm.reshape(1, D_MODEL), conv_b=conv_b.reshape(2, 1, D_FF))

    mixer_shards = [weights[n][0].astype(MXU_DTYPE) for n in _BIG[1:4]]
    ffn_shards = [w_up[0].astype(MXU_DTYPE), w_down[0].astype(MXU_DTYPE)]

    def in_weights(after):
        _, landed = _exchange_wait(w_in_gather, "w_in_gather_wait", "halves", list(after) + mixer_shards + ffn_shards)
        (w_in_sh,) = _forward_halves(list(landed))
        gathers["mixer"] = _exchange_start(mixer_shards, "mixer_gather_start", "gather", after=w_in_sh)
        gathers["ffn"] = _exchange_start(ffn_shards + [conv_w[0]], "ffn_gather_start", "gather",
                                         after=gathers["mixer"][4])
        w_in_pad = _w_in_t_to_pad(jnp.concatenate([w_in_sh[:, 0], w_in_sh[:, 1]], axis=-1).reshape(-1, D_MODEL))
        return w_in_pad, gathers["ffn"][4]

    def mixer_weights(after):
        _, (w_uq_sh, w_ukv_sh, w_out_sh) = _exchange_wait(gathers["mixer"], "mixer_gather_wait", "gather", after)
        return (_w_uq_to_pad(_cols_from_chips(w_uq_sh)), _w_ukv_to_pad(_cols_from_chips(w_ukv_sh)),
                w_out_sh.reshape(D_MODEL, D_MODEL))

    def ffn_weights(after):
        _, (w_up_sh, w_down_sh, cw_all) = _exchange_wait(gathers["ffn"], "ffn_gather_wait", "gather", after)
        return w_up_sh, w_down_sh.reshape(D_FF, D_MODEL), _conv_w_split(_cols_from_chips(cw_all))

    scatters = {}

    def start_scatter(slabs, slabs_lo, tag):
        got = _swap_halves([halves(s) for s in slabs_lo], tag + "_grad_swap_halves")
        sums = _pair_sum(slabs, got, tag + "_grad_pair_sum")
        scatters[tag] = _exchange_start(list(sums), tag + "_scatter_start", "scatter", after=slabs[-1])
        return scatters[tag][4]

    def on_ffn_grads(g):
        slabs, slabs_lo = [[g["w_up" + lo], g["w_down" + lo].reshape(N_CHIPS, D_FF // N_CHIPS, D_MODEL)]
                           for lo in ("", "_lo")]
        swap = _exchange_start([halves(s) for s in slabs_lo], "ffn_swap_start", "swap", after=slabs[1])

        def sent(after):
            _, got = _exchange_wait(swap, "ffn_swap_wait", "swap", after)
            sums = _pair_sum(slabs, got, "ffn_grad_pair_sum")
            scatters["ffn"] = _exchange_start(list(sums), "ffn_scatter_start", "scatter", after=got[0])
            return scatters["ffn"][4]

        return swap[4], sent

    def on_mixer_grads(g):
        slabs = [_w_in_t_from_pad(g["w_in"]).reshape(N_CHIPS, -1, D_MODEL), _cols_to_chips(_w_uq_from_pad(g["w_q"])),
                 _cols_to_chips(_w_ukv_from_pad(g["w_kv"]))]
        w_out_slabs = [g["w_out" + lo].reshape(N_CHIPS, D_MODEL // N_CHIPS, D_MODEL) for lo in ("", "_lo")]
        return start_scatter(slabs + w_out_slabs[:1], [s.astype(GRAD_PAYLOAD) for s in slabs] + w_out_slabs[1:], "mixer")

    loss_part, grad_x, g = _local_step(x, positions, loss_target, wts, in_weights, mixer_weights, ffn_weights,
                                       on_ffn_grads, on_mixer_grads)

    g_small_parts = dict(g)
    g_small_parts["conv_w"] = _conv_w_join(g["conv_w"])
    g_small_parts["conv_b"] = g["conv_b"].reshape(1, 2 * D_FF)
    flat = jnp.concatenate([g_small_parts[n].reshape(-1) for n, _ in _SMALL] + [loss_part.reshape(1)])
    flat = jnp.pad(flat, (0, _SMALL_ROWS * 128 - flat.shape[0])).reshape(_SMALL_ROWS, 128)
    small_gather = _exchange_start([flat], "small_gather_start", "all", after=grad_x)

    mixer_sums, mixer_landed = _exchange_wait(scatters["mixer"], "mixer_scatter_wait", "scatter", after=small_gather[4])
    ffn_sums, ffn_landed = _exchange_wait(scatters["ffn"], "ffn_scatter_wait", "scatter", after=mixer_landed[0])
    reduced = _chip_sum(list(mixer_sums) + list(ffn_sums), list(mixer_landed) + list(ffn_landed))
    g_big = dict(zip(_BIG, _join_halves(reduced)))

    grads, deltas, new_m, new_v = {}, {}, {}, {}

    def update(n, grad, copy_grad=False):
        w = weights[n]
        shape2 = grad.shape
        d, nm, nv, *again = _adamw(w.reshape(shape2), grad, m_in[n].reshape(shape2), v_in[n].reshape(shape2),
                                   "adamw_" + n, copy_grad)
        grads[n], deltas[n], new_m[n], new_v[n] = (t.reshape(w.shape) for t in (again[0] if copy_grad else grad, d, nm, nv))

    def update_transposed(n, grad_t):
        t = lambda a: jnp.swapaxes(a, 1, 2)
        d, nm, nv, again = _adamw(t(weights[n]), grad_t, t(m_in[n]), t(v_in[n]), "adamw_" + n, True)
        grads[n], deltas[n], new_m[n], new_v[n] = t(again), t(d), t(nm), t(nv)

    for n in _BIG:
        g3 = g_big[n].reshape((1, -1, g_big[n].shape[-1]))
        if n == "w_in":
            update_transposed(n, g3)
        else:
            update(n, g3, copy_grad=True)

    (own,), (everyone,) = _exchange_wait(small_gather, "small_gather_wait", "all", after=[deltas[n] for n in _BIG])
    device = 2 * chip + lax.axis_index("c")
    everyone = lax.dynamic_update_slice(everyone, own[None], (device, 0, 0))
    total = _sum_slabs([everyone[j] for j in range(8)], "small_grads_sum", tr=_SMALL_ROWS).reshape(-1)
    o = 0
    for n, shp in _SMALL:
        piece = total[o:o + math.prod(shp)].reshape(shp)
        o += math.prod(shp)
        if n == "conv_w":
            piece = lax.dynamic_slice_in_dim(piece, chip * UP_SHARD, UP_SHARD, axis=1)
        update(n, piece)
    loss = total[_SMALL_SIZE]
    return (loss, grad_x, *[grads[n] for n in names], *[deltas[n] for n in names], *[new_m[n] for n in names],
            *[new_v[n] for n in names])
```

```python
import functools
import math

import jax
import jax.numpy as jnp
from jax import lax
from jax.experimental import pallas as pl
from jax.experimental.pallas import tpu as pltpu

F32 = jnp.float32
MXU_DTYPE = jnp.bfloat16
MESH = pl.DeviceIdType.MESH

D_MODEL = 1024
EPS = 1e-6
A_GROUPS = 8
CHUNK = 128
HEADS = 8
NOPE = 128
ROPE = 64
QK_DIM = NOPE + ROPE
HEAD_PAD = 256
Q_RANK = 256
KV_RANK = 128
ROPE_THETA = 10000.0
D_FF = 2816
FF_TILE = 256
N_FF_TILES = D_FF // FF_TILE
LAT = 512
IN_PAD = 4 * D_MODEL + LAT
N_CHIPS = 4
ADAM_LR, ADAM_B1, ADAM_B2, ADAM_EPS, ADAM_WD, ADAM_STEP = 0.001, 0.9, 0.999, 1e-08, 0.01, 10

VMEM_CAP_V7X = 64 * 1024 * 1024
NEG = -1e30


def _params(sem, nbytes):
    limit = int(min(VMEM_CAP_V7X - (8 << 20), max(32 << 20, 3 * nbytes)))
    return pltpu.CompilerParams(dimension_semantics=sem, vmem_limit_bytes=limit)


def _nbytes(shape, dtype):
    return math.prod(shape) * jnp.dtype(dtype).itemsize


_DIMS = {"nn": (((1,), (0,)), ((), ())), "nt": (((1,), (1,)), ((), ())), "tn": (((0,), (0,)), ((), ()))}


def _mm(a, b, mode, name, *, tm, tn, tk, out_dtype=F32, add=None, dims=None, a_spec=None, b_spec=None,
        o_spec=None, out_shape=None, n_outer=False, copy_dtype=None, after=None):
    if dims is None:
        if mode == "nn":
            (M, K), (_, N) = a.shape, b.shape
        elif mode == "nt":
            (M, K), (N, _) = a.shape, b.shape
        else:
            (K, M), (_, N) = a.shape, b.shape
    else:
        M, N, K = dims
    a_blk = (tk, tm) if mode == "tn" else (tm, tk)
    b_blk = (tn, tk) if mode == "nt" else (tk, tn)
    if a_spec is None:
        a_spec = pl.BlockSpec(a_blk, (lambda i, j, k: (k, i)) if mode == "tn" else (lambda i, j, k: (i, k)))
    if b_spec is None:
        b_spec = pl.BlockSpec(b_blk, (lambda i, j, k: (j, k)) if mode == "nt" else (lambda i, j, k: (k, j)))
    if o_spec is None:
        o_spec = pl.BlockSpec((tm, tn), lambda i, j, k: (i, j))
    if out_shape is None:
        out_shape = (M, N)
    assert M % tm == 0 and N % tn == 0 and K % tk == 0, (name, M, N, K, tm, tn, tk)
    nk = K // tk
    contract = _DIMS[mode]
    has_add = add is not None

    def body(*refs):
        a_ref, b_ref = refs[0], refs[1]
        add_ref = refs[2] if has_add else None
        n_in = 2 + has_add + (after is not None)
        o_ref = refs[n_in]
        copy_ref = refs[n_in + 1] if copy_dtype is not None else None

        def product():
            return lax.dot_general(a_ref[...].astype(MXU_DTYPE), b_ref[...].astype(MXU_DTYPE), contract,
                                   preferred_element_type=F32)

        def finish(r):
            if has_add:
                r = r + add_ref[...]
            o_ref[...] = r.astype(out_dtype)
            if copy_ref is not None:
                copy_ref[...] = r.astype(copy_dtype)

        if nk == 1:
            finish(product())
            return
        acc = refs[-1]
        k = pl.program_id(2)

        @pl.when(k == 0)
        def _():
            acc[...] = jnp.zeros_like(acc)

        acc[...] += product()

        @pl.when(k == nk - 1)
        def _():
            finish(acc[...])

    in_specs = [a_spec, b_spec]
    args = [a, b]
    nbytes = _nbytes(a_blk, a.dtype) + _nbytes(b_blk, b.dtype) + 3 * _nbytes((tm, tn), F32)
    if has_add:
        in_specs.append(pl.BlockSpec((tm, tn), lambda i, j, k: (i, j)))
        args.append(add)
        nbytes += _nbytes((tm, tn), F32)
    if after is not None:
        in_specs.append(pl.BlockSpec(after.shape, lambda i, j, k: (0, 0)))
        args.append(after)
    grid = (M // tm, N // tn, nk)
    if n_outer:
        def swapped(spec):
            return pl.BlockSpec(spec.block_shape, lambda j, i, k, at=spec.index_map: at(i, j, k))

        grid = (N // tn, M // tm, nk)
        in_specs = [swapped(s) for s in in_specs]
        o_spec = swapped(o_spec)
    out_sds, out_specs = jax.ShapeDtypeStruct(out_shape, out_dtype), o_spec
    if copy_dtype is not None:
        out_sds, out_specs = (out_sds, jax.ShapeDtypeStruct(out_shape, copy_dtype)), (o_spec, o_spec)
    return pl.pallas_call(
        body, name=name, out_shape=out_sds, grid=grid, in_specs=in_specs, out_specs=out_specs,
        scratch_shapes=[pltpu.VMEM((tm, tn), F32)] if nk > 1 else [],
        compiler_params=_params(("parallel", "parallel", "arbitrary"), nbytes),
    )(*args)


_GELU_C = math.sqrt(2.0 / math.pi)
_GELU_A = 0.044715


def _sigmoid(x):
    return 0.5 * jnp.tanh(0.5 * x) + 0.5


def _gelu(x):
    t = jnp.tanh(x * (_GELU_C + (_GELU_C * _GELU_A) * (x * x)))
    return x * (0.5 + 0.5 * t)


def _gelu_and_grad(x):
    x2 = x * x
    t = jnp.tanh(x * (_GELU_C + (_GELU_C * _GELU_A) * x2))
    cdf = 0.5 + 0.5 * t
    grad = cdf + (0.5 * x) * (1.0 - t * t) * (_GELU_C + (3.0 * _GELU_C * _GELU_A) * x2)
    return x * cdf, grad


def _rope_mix(g, cos_a, sin_a):
    return g * cos_a + pltpu.roll(g, 64, 1) * sin_a


def _rope_mix_bwd(d, cos_a, sin_a):
    return d * cos_a + pltpu.roll(d * sin_a, 64, 1)


def _rms_fwd(x, g, name, tr=1024):
    T, D = x.shape

    def body(x_ref, g_ref, h_ref):
        xv = x_ref[...]
        r = lax.rsqrt(jnp.mean(xv * xv, axis=-1, keepdims=True) + EPS)
        h_ref[...] = ((xv * r) * g_ref[...]).astype(h_ref.dtype)

    return pl.pallas_call(
        body, name=name, out_shape=jax.ShapeDtypeStruct((T, D), MXU_DTYPE), grid=(T // tr,),
        in_specs=[pl.BlockSpec((tr, D), lambda i: (i, 0)), pl.BlockSpec((1, D), lambda i: (0, 0))],
        out_specs=pl.BlockSpec((tr, D), lambda i: (i, 0)),
        compiler_params=_params(("parallel",), 3 * _nbytes((tr, D), F32)),
    )(x, g)


def _rms_bwd(x, g, dh, dres, name, tr=512):
    T, D = x.shape

    def body(x_ref, g_ref, dh_ref, dres_ref, dx_ref, gg_ref):
        @pl.when(pl.program_id(0) == 0)
        def _():
            gg_ref[...] = jnp.zeros_like(gg_ref)

        xv = x_ref[...]
        r = lax.rsqrt(jnp.mean(xv * xv, axis=-1, keepdims=True) + EPS)
        xn = xv * r
        dhv = dh_ref[...]
        dxn = dhv * g_ref[...]
        dx_ref[...] = dres_ref[...] + r * (dxn - xn * jnp.mean(dxn * xn, axis=-1, keepdims=True))
        gg_ref[...] += jnp.sum(dhv * xn, axis=0, keepdims=True)

    row = pl.BlockSpec((tr, D), lambda i: (i, 0))
    vec = pl.BlockSpec((1, D), lambda i: (0, 0))
    return pl.pallas_call(
        body, name=name,
        out_shape=(jax.ShapeDtypeStruct((T, D), F32), jax.ShapeDtypeStruct((1, D), F32)),
        grid=(T // tr,), in_specs=[row, vec, row, row], out_specs=(row, vec),
        compiler_params=_params(("arbitrary",), 6 * _nbytes((tr, D), F32)),
    )(x, g, dh, dres)


def _lat_fwd(z, gq, gkv, wq, wkv, cos_a, sin_a, tr=512):
    T = z.shape[0]
    lat_blk = (4 * D_MODEL) // LAT

    def body(z_ref, gq_ref, gkv_ref, wq_ref, wkv_ref, cos_ref, sin_ref, q_ref, k_ref, v_ref, cqn_ref, ckvn_ref):
        zl = z_ref[...]
        cos_v, sin_v = cos_ref[...], sin_ref[...]
        cq = zl[:, :Q_RANK]
        ckv = zl[:, Q_RANK:Q_RANK + KV_RANK]
        krb = zl[:, Q_RANK + KV_RANK:]
        cqn = ((cq * lax.rsqrt(jnp.mean(cq * cq, axis=-1, keepdims=True) + EPS)) * gq_ref[...]).astype(MXU_DTYPE)
        ckvn = ((ckv * lax.rsqrt(jnp.mean(ckv * ckv, axis=-1, keepdims=True) + EPS)) * gkv_ref[...]).astype(MXU_DTYPE)
        cqn_ref[...] = cqn
        ckvn_ref[...] = ckvn
        krr = _rope_mix(krb, cos_v, sin_v).astype(MXU_DTYPE)
        q = jnp.dot(cqn, wq_ref[...], preferred_element_type=F32)
        kv = jnp.dot(ckvn, wkv_ref[...], preferred_element_type=F32)
        for h in range(HEADS):
            o = h * HEAD_PAD
            q_ref[:, o:o + NOPE] = q[:, o:o + NOPE].astype(MXU_DTYPE)
            q_ref[:, o + NOPE:o + HEAD_PAD] = _rope_mix(q[:, o + NOPE:o + HEAD_PAD], cos_v, sin_v).astype(MXU_DTYPE)
            k_ref[:, o:o + NOPE] = kv[:, h * NOPE:(h + 1) * NOPE].astype(MXU_DTYPE)
            k_ref[:, o + NOPE:o + HEAD_PAD] = krr
        v_ref[...] = kv[:, HEADS * NOPE:].astype(MXU_DTYPE)

    def row(w):
        return pl.BlockSpec((tr, w), lambda i: (i, 0))

    def full(a):
        return pl.BlockSpec(a.shape, lambda i: (0, 0))

    return pl.pallas_call(
        body, name="lat_fwd",
        out_shape=(jax.ShapeDtypeStruct((T, HEADS * HEAD_PAD), MXU_DTYPE), jax.ShapeDtypeStruct((T, HEADS * HEAD_PAD), MXU_DTYPE),
                   jax.ShapeDtypeStruct((T, HEADS * NOPE), MXU_DTYPE), jax.ShapeDtypeStruct((T, Q_RANK), MXU_DTYPE),
                   jax.ShapeDtypeStruct((T, KV_RANK), MXU_DTYPE)),
        grid=(T // tr,),
        in_specs=[pl.BlockSpec((tr, LAT), lambda i: (i, lat_blk)), full(gq), full(gkv), full(wq), full(wkv), row(128), row(128)],
        out_specs=(row(HEADS * HEAD_PAD), row(HEADS * HEAD_PAD), row(HEADS * NOPE), row(Q_RANK), row(KV_RANK)),
        compiler_params=_params(("parallel",), 8 * _nbytes((tr, HEADS * HEAD_PAD), F32)),
    )(z, gq, gkv, wq, wkv, cos_a, sin_a)


ATT_BLOCK = 256
_SCALE = QK_DIM ** -0.5


def _causal_mask(n):
    return lax.broadcasted_iota(jnp.int32, (n, n), 1) <= lax.broadcasted_iota(jnp.int32, (n, n), 0)


def _causal_mask_t(n):
    return lax.broadcasted_iota(jnp.int32, (n, n), 0) <= lax.broadcasted_iota(jnp.int32, (n, n), 1)


ATT_HEADS = 4


def _attn_fwd(q, k, v, B, S):
    tq = ATT_BLOCK
    nq = S // tq
    T = B * S
    hp, groups = ATT_HEADS, HEADS // ATT_HEADS

    def body(q_ref, k_ref, v_ref, o_ref, *lse_refs):
        qi = pl.program_id(2)
        qs = [q_ref[:, t * HEAD_PAD:(t + 1) * HEAD_PAD] for t in range(hp)]

        def scores(j, t):
            rows = pl.ds(pl.multiple_of(j * tq, tq), tq)
            return lax.dot_general(k_ref[rows, t * HEAD_PAD:(t + 1) * HEAD_PAD], qs[t], _DIMS["nt"],
                                   preferred_element_type=F32)

        def step(j, carry, last):
            rows = pl.ds(pl.multiple_of(j * tq, tq), tq)
            out = []
            for t in range(hp):
                m, l, acc, st = carry[t]
                st_next = st if last else scores(j + 1, t)
                st = st * _SCALE
                if last:
                    st = jnp.where(_causal_mask_t(tq), st, NEG)
                m_new = jnp.maximum(m, jnp.max(st, axis=0, keepdims=True))
                alpha = jnp.exp(m - m_new)
                p = jnp.exp(st - m_new)
                l = alpha * l + jnp.sum(p, axis=0, keepdims=True)
                acc = alpha * acc + lax.dot_general(v_ref[rows, t * NOPE:(t + 1) * NOPE], p.astype(MXU_DTYPE),
                                                    _DIMS["tn"], preferred_element_type=F32)
                out.append((m_new, l, acc, st_next))
            return tuple(out)

        init = tuple((jnp.full((1, tq), NEG, F32), jnp.zeros((1, tq), F32), jnp.zeros((NOPE, tq), F32), scores(0, t))
                     for t in range(hp))
        carry = lax.fori_loop(0, qi, lambda j, c: step(j, c, False), init)
        carry = step(qi, carry, True)
        for t in range(hp):
            m, l, acc, _ = carry[t]
            o_ref[:, t * NOPE:(t + 1) * NOPE] = (acc / l).T
            lse_refs[t][0] = m + jnp.log(l)

    lse_sds = jax.ShapeDtypeStruct((groups * B * nq, 1, tq), F32)
    lse_spec = pl.BlockSpec((1, 1, tq), lambda b, h, i: ((h * B + b) * nq + i, 0, 0))
    return pl.pallas_call(
        body, name="attn_fwd",
        out_shape=(jax.ShapeDtypeStruct((T, HEADS * NOPE), F32),) + (lse_sds,) * hp,
        grid=(B, groups, nq),
        in_specs=[pl.BlockSpec((tq, hp * HEAD_PAD), lambda b, h, i: (b * nq + i, h)),
                  pl.BlockSpec((S, hp * HEAD_PAD), lambda b, h, i: (b, h)),
                  pl.BlockSpec((S, hp * NOPE), lambda b, h, i: (b, h))],
        out_specs=(pl.BlockSpec((tq, hp * NOPE), lambda b, h, i: (b * nq + i, h)),) + (lse_spec,) * hp,
        compiler_params=_params(("parallel", "parallel", "arbitrary"), 4 * hp * _nbytes((S, HEAD_PAD), MXU_DTYPE)),
    )(q, k, v)


def _attn_bwd(q, k, v, do, lses, delta, B, S):
    tq = ATT_BLOCK
    nq = S // tq
    T = B * S
    hp, groups = ATT_HEADS, HEADS // ATT_HEADS

    def body(q_ref, k_ref, v_ref, do_ref, *refs):
        lse_refs, dl_refs = refs[:hp], refs[hp:2 * hp]
        dq_out, dk_ref, dv_ref, dq_ref = refs[2 * hp:]
        kj = pl.program_id(2)

        @pl.when(kj == 0)
        def _():
            dq_ref[...] = jnp.zeros_like(dq_ref)

        def products(i, t):
            rows = pl.ds(pl.multiple_of(i * tq, tq), tq)
            st = lax.dot_general(k_ref[:, t * HEAD_PAD:(t + 1) * HEAD_PAD], q_ref[rows, t * HEAD_PAD:(t + 1) * HEAD_PAD],
                                 _DIMS["nt"], preferred_element_type=F32)
            dpt = lax.dot_general(v_ref[:, t * NOPE:(t + 1) * NOPE], do_ref[rows, t * NOPE:(t + 1) * NOPE],
                                  _DIMS["nt"], preferred_element_type=F32)
            return st, dpt

        def step(i, carry, masked):
            rows = pl.ds(pl.multiple_of(i * tq, tq), tq)
            nxt = jnp.minimum(i + 1, nq - 1)
            out = []
            for t in range(hp):
                dk, dv, st, dpt = carry[t]
                st_next, dpt_next = products(nxt, t)
                qk_cols = slice(t * HEAD_PAD, (t + 1) * HEAD_PAD)
                v_cols = slice(t * NOPE, (t + 1) * NOPE)
                p = jnp.exp(st * _SCALE - lse_refs[t][i])
                if masked:
                    p = jnp.where(_causal_mask_t(tq), p, 0.0)
                dv = dv + jnp.dot(p.astype(MXU_DTYPE), do_ref[rows, v_cols], preferred_element_type=F32)
                ds = (p * (dpt - dl_refs[t][i]) * _SCALE).astype(MXU_DTYPE)
                dk = dk + jnp.dot(ds, q_ref[rows, qk_cols], preferred_element_type=F32)
                dq_ref[rows, qk_cols] += lax.dot_general(ds, k_ref[:, qk_cols], _DIMS["tn"], preferred_element_type=F32)
                out.append((dk, dv, st_next, dpt_next))
            return tuple(out)

        init = tuple((jnp.zeros((tq, HEAD_PAD), F32), jnp.zeros((tq, NOPE), F32)) + products(kj, t) for t in range(hp))
        carry = step(kj, init, True)
        carry = lax.fori_loop(kj + 1, nq, lambda i, c: step(i, c, False), carry)
        for t in range(hp):
            dk_ref[:, t * HEAD_PAD:(t + 1) * HEAD_PAD] = carry[t][0].astype(dk_ref.dtype)
            dv_ref[:, t * NOPE:(t + 1) * NOPE] = carry[t][1].astype(dv_ref.dtype)

        @pl.when(kj == nq - 1)
        def _():
            dq_out[...] = dq_ref[...].astype(dq_out.dtype)

    seq = lambda w: pl.BlockSpec((S, w), lambda b, h, j: (b, h))
    blk = lambda w: pl.BlockSpec((tq, w), lambda b, h, j: (b * nq + j, h))
    lse_spec = pl.BlockSpec((nq, 1, tq), lambda b, h, j: (h * B + b, 0, 0))
    dl_specs = [pl.BlockSpec((nq, 1, tq), lambda b, h, j, t=t: ((h * hp + t) * B + b, 0, 0)) for t in range(hp)]
    return pl.pallas_call(
        body, name="attn_bwd",
        out_shape=(jax.ShapeDtypeStruct((T, HEADS * HEAD_PAD), MXU_DTYPE), jax.ShapeDtypeStruct((T, HEADS * HEAD_PAD), MXU_DTYPE),
                   jax.ShapeDtypeStruct((T, HEADS * NOPE), MXU_DTYPE)),
        grid=(B, groups, nq),
        in_specs=[seq(hp * HEAD_PAD), blk(hp * HEAD_PAD), blk(hp * NOPE), seq(hp * NOPE)] + [lse_spec] * hp + dl_specs,
        out_specs=(seq(hp * HEAD_PAD), blk(hp * HEAD_PAD), blk(hp * NOPE)),
        scratch_shapes=[pltpu.VMEM((S, hp * HEAD_PAD), F32)],
        compiler_params=_params(("parallel", "parallel", "arbitrary"), 8 * hp * _nbytes((S, HEAD_PAD), F32)),
    )(q, k, v, do, *lses, *([delta] * hp))


MIX_ROWS = 256


def _tril_weights(ws_ref, g):
    return jnp.where(_causal_mask(CHUNK), ws_ref[g], 0.0).astype(MXU_DTYPE)


def _layer_norm_stats(va):
    mu = jnp.mean(va, axis=-1, keepdims=True)
    xc = va - mu
    rs = lax.rsqrt(jnp.mean(xc * xc, axis=-1, keepdims=True) + EPS)
    return xc * rs


def _mix_specs(tr):
    zcol = lambda c: pl.BlockSpec((tr, D_MODEL), lambda i, c=c: (i, c))
    row = pl.BlockSpec((tr, D_MODEL), lambda i: (i, 0))
    vec = pl.BlockSpec((1, D_MODEL), lambda i: (0, 0))
    ws = pl.BlockSpec((A_GROUPS, CHUNK, CHUNK), lambda i: (0, 0, 0))
    bs = pl.BlockSpec((CHUNK, 128), lambda i: (0, 0))
    return zcol, row, vec, ws, bs


def _mix_fwd(z, yb, ln_g, ln_b, ws, bs_t):
    T = z.shape[0]
    tr = MIX_ROWS
    zcol, row, vec, ws_spec, bs_spec = _mix_specs(tr)

    def body(zu_ref, zv_ref, zga_ref, zgb_ref, yb_ref, g_ref, b_ref, ws_ref, bs_ref, out_ref, vn_s):
        vhat = _layer_norm_stats(_gelu(zv_ref[...]))
        vn_s[...] = (vhat * g_ref[...] + b_ref[...]).astype(MXU_DTYPE)
        for g in range(A_GROUPS):
            w = _tril_weights(ws_ref, g)
            bias = bs_ref[:, g:g + 1]
            cols = slice(g * CHUNK, (g + 1) * CHUNK)
            for c in range(tr // CHUNK):
                rows = slice(c * CHUNK, (c + 1) * CHUNK)
                mixed = jnp.dot(w, vn_s[rows, cols], preferred_element_type=F32) + bias
                ya = _gelu(zu_ref[rows, cols]) * mixed
                merged = _sigmoid(zga_ref[rows, cols]) * ya + _sigmoid(zgb_ref[rows, cols]) * yb_ref[rows, cols]
                out_ref[rows, cols] = merged.astype(MXU_DTYPE)

    return pl.pallas_call(
        body, name="mix_fwd", out_shape=jax.ShapeDtypeStruct((T, D_MODEL), MXU_DTYPE), grid=(T // tr,),
        in_specs=[zcol(0), zcol(1), zcol(2), zcol(3), row, vec, vec, ws_spec, bs_spec], out_specs=row,
        scratch_shapes=[pltpu.VMEM((tr, D_MODEL), MXU_DTYPE)],
        compiler_params=_params(("parallel",), 8 * _nbytes((tr, D_MODEL), F32)),
    )(z, z, z, z, yb, ln_g, ln_b, ws, bs_t)


def _mix_bwd(z, yb, dm, ln_g, ln_b, ws, bs_t):
    T = z.shape[0]
    tr = MIX_ROWS
    zcol, row, vec, ws_spec, bs_spec = _mix_specs(tr)

    def body(zu_ref, zv_ref, zga_ref, zgb_ref, yb_ref, dm_ref, g_ref, b_ref, ws_ref, bs_ref,
             dz_ref, dyb_ref, dl_ref, gws_ref, gbs_ref, glg_ref, glb_ref, vn_s, dvn_s):
        @pl.when(pl.program_id(0) == 0)
        def _():
            gws_ref[...] = jnp.zeros_like(gws_ref)
            gbs_ref[...] = jnp.zeros_like(gbs_ref)
            glg_ref[...] = jnp.zeros_like(glg_ref)
            glb_ref[...] = jnp.zeros_like(glb_ref)

        lane = lax.broadcasted_iota(jnp.int32, (CHUNK, 128), 1)
        va, dgelu_v = _gelu_and_grad(zv_ref[...])
        mu = jnp.mean(va, axis=-1, keepdims=True)
        xc = va - mu
        rs = lax.rsqrt(jnp.mean(xc * xc, axis=-1, keepdims=True) + EPS)
        vhat = xc * rs
        vn_s[...] = (vhat * g_ref[...] + b_ref[...]).astype(MXU_DTYPE)
        gbs_acc = jnp.zeros((CHUNK, 128), F32)
        for g in range(A_GROUPS):
            w = _tril_weights(ws_ref, g)
            bias = bs_ref[:, g:g + 1]
            cols = slice(g * CHUNK, (g + 1) * CHUNK)
            gw_acc = jnp.zeros((CHUNK, CHUNK), F32)
            for c in range(tr // CHUNK):
                rows = slice(c * CHUNK, (c + 1) * CHUNK)
                vn = vn_s[rows, cols]
                mixed = jnp.dot(w, vn, preferred_element_type=F32) + bias
                ua, dgelu_u = _gelu_and_grad(zu_ref[rows, cols])
                dmv = dm_ref[rows, cols]
                sa = _sigmoid(zga_ref[rows, cols])
                dya = dmv * sa
                dz_ref[rows, 2 * D_MODEL + g * CHUNK:2 * D_MODEL + (g + 1) * CHUNK] = (
                    dmv * (ua * mixed) * (sa * (1.0 - sa))).astype(dz_ref.dtype)
                dz_ref[rows, cols] = (dya * mixed * dgelu_u).astype(dz_ref.dtype)
                dmix = dya * ua
                gbs_acc = gbs_acc + jnp.where(lane == g, jnp.sum(dmix, axis=-1, keepdims=True), 0.0)
                dmix_b = dmix.astype(MXU_DTYPE)
                gw_acc = gw_acc + lax.dot_general(dmix_b, vn, _DIMS["nt"], preferred_element_type=F32)
                dvn_s[rows, cols] = lax.dot_general(w, dmix_b, _DIMS["tn"], preferred_element_type=F32)
            gws_ref[g] += jnp.where(_causal_mask(CHUNK), gw_acc, 0.0)
        gbs_ref[...] += gbs_acc

        dvn = dvn_s[...]
        glg_ref[...] += jnp.sum(dvn * vhat, axis=0, keepdims=True)
        glb_ref[...] += jnp.sum(dvn, axis=0, keepdims=True)
        dvh = dvn * g_ref[...]
        dva = rs * (dvh - jnp.mean(dvh, axis=-1, keepdims=True) - vhat * jnp.mean(dvh * vhat, axis=-1, keepdims=True))
        dz_ref[:, D_MODEL:2 * D_MODEL] = (dva * dgelu_v).astype(dz_ref.dtype)

        dmv = dm_ref[...]
        ybv = yb_ref[...]
        sb = _sigmoid(zgb_ref[...])
        dyb = dmv * sb
        dyb_ref[...] = dyb.astype(dyb_ref.dtype)
        dz_ref[:, 3 * D_MODEL:4 * D_MODEL] = (dmv * ybv * (sb * (1.0 - sb))).astype(dz_ref.dtype)
        dz_ref[:, 4 * D_MODEL:] = jnp.zeros((tr, LAT), dz_ref.dtype)
        prod = dyb * ybv
        sel = (lax.broadcasted_iota(jnp.int32, (HEADS, D_MODEL), 1) // NOPE
               == lax.broadcasted_iota(jnp.int32, (HEADS, D_MODEL), 0)).astype(jnp.bfloat16)
        hi = prod.astype(jnp.bfloat16)
        rest = prod - hi.astype(F32)
        mid = rest.astype(jnp.bfloat16)
        lo = (rest - mid.astype(F32)).astype(jnp.bfloat16)
        dl_ref[...] = (lax.dot_general(sel, hi, _DIMS["nt"], preferred_element_type=F32)
                       + lax.dot_general(sel, mid, _DIMS["nt"], preferred_element_type=F32)
                       + lax.dot_general(sel, lo, _DIMS["nt"], preferred_element_type=F32))

    return pl.pallas_call(
        body, name="mix_bwd",
        out_shape=(jax.ShapeDtypeStruct((T, IN_PAD), MXU_DTYPE), jax.ShapeDtypeStruct((T, D_MODEL), MXU_DTYPE),
                   jax.ShapeDtypeStruct((HEADS, T), F32), jax.ShapeDtypeStruct((A_GROUPS, CHUNK, CHUNK), F32),
                   jax.ShapeDtypeStruct((CHUNK, 128), F32), jax.ShapeDtypeStruct((1, D_MODEL), F32),
                   jax.ShapeDtypeStruct((1, D_MODEL), F32)),
        grid=(T // tr,),
        in_specs=[zcol(0), zcol(1), zcol(2), zcol(3), row, row, vec, vec, ws_spec, bs_spec],
        out_specs=(pl.BlockSpec((tr, IN_PAD), lambda i: (i, 0)), row, pl.BlockSpec((HEADS, tr), lambda i: (0, i)),
                   ws_spec, bs_spec, vec, vec),
        scratch_shapes=[pltpu.VMEM((tr, D_MODEL), MXU_DTYPE), pltpu.VMEM((tr, D_MODEL), F32)],
        compiler_params=_params(("arbitrary",), 12 * _nbytes((tr, D_MODEL), F32)),
    )(z, z, z, z, yb, dm, ln_g, ln_b, ws, bs_t)


def _lat_bwd(dz, z, dq, dk, dv, gq, gkv, wq, wkv, cos_a, sin_a, tr=256):
    T = z.shape[0]
    lat_blk = (4 * D_MODEL) // LAT

    def body(dz_in, z_ref, dq_ref, dk_ref, dv_ref, gq_ref, gkv_ref, wq_ref, wkv_ref, cos_ref, sin_ref,
             dz_ref, dqr_ref, dkv_ref, ggq_ref, ggkv_ref):
        del dz_in

        @pl.when(pl.program_id(0) == 0)
        def _():
            ggq_ref[...] = jnp.zeros_like(ggq_ref)
            ggkv_ref[...] = jnp.zeros_like(ggkv_ref)

        cos_v, sin_v = cos_ref[...], sin_ref[...]
        dkr = jnp.zeros((tr, 128), F32)
        for h in range(HEADS):
            o = h * HEAD_PAD
            dqr_ref[:, o:o + NOPE] = dq_ref[:, o:o + NOPE].astype(MXU_DTYPE)
            dqr_ref[:, o + NOPE:o + HEAD_PAD] = _rope_mix_bwd(dq_ref[:, o + NOPE:o + HEAD_PAD], cos_v, sin_v).astype(MXU_DTYPE)
            dkv_ref[:, h * NOPE:(h + 1) * NOPE] = dk_ref[:, o:o + NOPE].astype(MXU_DTYPE)
            dkr = dkr + _rope_mix_bwd(dk_ref[:, o + NOPE:o + HEAD_PAD], cos_v, sin_v)
        dkv_ref[:, HEADS * NOPE:] = dv_ref[...]
        dcqn = lax.dot_general(dqr_ref[...], wq_ref[...], _DIMS["nt"], preferred_element_type=F32)
        dckvn = lax.dot_general(dkv_ref[...], wkv_ref[...], _DIMS["nt"], preferred_element_type=F32)

        zl = z_ref[...]

        def rms_bwd(c, dn, g_ref, gg_ref):
            r = lax.rsqrt(jnp.mean(c * c, axis=-1, keepdims=True) + EPS)
            ch = c * r
            gg_ref[...] += jnp.sum(dn * ch, axis=0, keepdims=True)
            dch = dn * g_ref[...]
            return r * (dch - ch * jnp.mean(dch * ch, axis=-1, keepdims=True))

        dz_ref[:, :Q_RANK] = rms_bwd(zl[:, :Q_RANK], dcqn, gq_ref, ggq_ref).astype(dz_ref.dtype)
        dz_ref[:, Q_RANK:Q_RANK + KV_RANK] = rms_bwd(zl[:, Q_RANK:Q_RANK + KV_RANK], dckvn, gkv_ref, ggkv_ref).astype(dz_ref.dtype)
        dz_ref[:, Q_RANK + KV_RANK:] = dkr.astype(dz_ref.dtype)

    def row(w):
        return pl.BlockSpec((tr, w), lambda i: (i, 0))

    def full(a):
        return pl.BlockSpec(a.shape, lambda i: (0, 0))

    lat = pl.BlockSpec((tr, LAT), lambda i: (i, lat_blk))
    return pl.pallas_call(
        body, name="lat_bwd",
        out_shape=(jax.ShapeDtypeStruct(dz.shape, dz.dtype), jax.ShapeDtypeStruct((T, HEADS * HEAD_PAD), MXU_DTYPE),
                   jax.ShapeDtypeStruct((T, 2 * HEADS * NOPE), MXU_DTYPE), jax.ShapeDtypeStruct(gq.shape, F32),
                   jax.ShapeDtypeStruct(gkv.shape, F32)),
        grid=(T // tr,),
        in_specs=[pl.BlockSpec(memory_space=pl.ANY), lat, row(HEADS * HEAD_PAD), row(HEADS * HEAD_PAD), row(HEADS * NOPE),
                  full(gq), full(gkv), full(wq), full(wkv), row(128), row(128)],
        out_specs=(lat, row(HEADS * HEAD_PAD), row(2 * HEADS * NOPE), full(gq), full(gkv)),
        input_output_aliases={0: 0},
        compiler_params=_params(("arbitrary",), 8 * _nbytes((tr, HEADS * HEAD_PAD), F32)),
    )(dz, z, dq, dk, dv, gq, gkv, wq, wkv, cos_a, sin_a)


GATE_ROWS = 64
HALO = 8


def _taps(ref, half, r, first):
    C = GATE_ROWS
    if first:
        xs = jnp.concatenate([jnp.zeros((HALO, ref.shape[-1]), F32), ref[half, 0:C, :]], axis=0)
    else:
        xs = ref[half, pl.ds(pl.multiple_of(r * C - HALO, HALO), C + HALO), :]
    return xs[HALO:, :], pltpu.roll(xs, 1, 0)[HALO:, :], pltpu.roll(xs, 2, 0)[HALO:, :]


def _conv_taps(taps, cw, cb):
    x0, x1, x2 = taps
    return cb + cw[0:1, :] * x2 + cw[1:2, :] * x1 + cw[2:3, :] * x0


def _fold8(x):
    acc = x[0:8, :]
    for i in range(1, x.shape[0] // 8):
        acc = acc + x[8 * i:8 * (i + 1), :]
    return acc


def _gate_fwd(up3, conv_w, conv_b, B, S):
    T = B * S
    W = FF_TILE
    C = GATE_ROWS

    def body(up_ref, cw_ref, cb_ref, act_ref, conv_ref):
        def chunk(r, first):
            gate = _conv_taps(_taps(up_ref, 0, r, first), cw_ref[0], cb_ref[0])
            val = _conv_taps(_taps(up_ref, 1, r, first), cw_ref[1], cb_ref[1])
            rows = pl.ds(0 if first else pl.multiple_of(r * C, C), C)
            conv_ref[0, rows, :] = gate.astype(conv_ref.dtype)
            conv_ref[1, rows, :] = val.astype(conv_ref.dtype)
            act_ref[rows, :] = (gate * _sigmoid(gate) * val).astype(act_ref.dtype)

        chunk(0, True)

        @pl.loop(1, S // C)
        def _(r):
            chunk(r, False)

    up_spec = pl.BlockSpec((2, S, W), lambda b, j: (0, b, j))
    return pl.pallas_call(
        body, name="gate_fwd",
        out_shape=(jax.ShapeDtypeStruct((T, D_FF), MXU_DTYPE), jax.ShapeDtypeStruct((2, T, D_FF), MXU_DTYPE)),
        grid=(B, N_FF_TILES),
        in_specs=[up_spec, pl.BlockSpec((2, 3, W), lambda b, j: (0, 0, j)), pl.BlockSpec((2, 1, W), lambda b, j: (0, 0, j))],
        out_specs=(pl.BlockSpec((S, W), lambda b, j: (b, j)), up_spec),
        compiler_params=_params(("parallel", "parallel"), 8 * _nbytes((S, W), F32)),
    )(up3, conv_w, conv_b)


def _gate_bwd(up3, conv3, dact, conv_w, B, S):
    T = B * S
    W = FF_TILE
    C = GATE_ROWS

    def body(up_ref, conv_ref, da_ref, cw_ref, dup_ref, gcw_ref, gcb_ref, d_s):
        @pl.when(pl.program_id(1) == 0)
        def _():
            gcw_ref[...] = jnp.zeros_like(gcw_ref)
            gcb_ref[...] = jnp.zeros_like(gcb_ref)

        @pl.loop(0, S // C)
        def _(r):
            rows = pl.ds(pl.multiple_of(r * C, C), C)
            gate, val = conv_ref[0, rows, :].astype(F32), conv_ref[1, rows, :].astype(F32)
            sg = _sigmoid(gate)
            da = da_ref[rows, :]
            d_s[0, rows, :] = da * val * (sg * (1.0 + gate * (1.0 - sg)))
            d_s[1, rows, :] = da * (gate * sg)

        d_s[:, S:S + HALO, :] = jnp.zeros((2, HALO, W), F32)

        def chunk(r, sums):
            base = pl.multiple_of(r * C, C)
            out = []
            for half in (0, 1):
                ds_ = d_s[half, pl.ds(base, C + HALO), :]
                d0, d1, d2 = ds_[:C, :], pltpu.roll(ds_, C + HALO - 1, 0)[:C, :], pltpu.roll(ds_, C + HALO - 2, 0)[:C, :]
                cw = cw_ref[half]
                dup_ref[half, pl.ds(base, C), :] = (cw[2:3, :] * d0 + cw[1:2, :] * d1 + cw[0:1, :] * d2).astype(dup_ref.dtype)
                x = up_ref[half, pl.ds(base, C), :]
                sb, s0, s1, s2 = sums[half]
                out.append((sb + _fold8(d0), s0 + _fold8(d2 * x), s1 + _fold8(d1 * x), s2 + _fold8(d0 * x)))
            return tuple(out)

        zeros = tuple(tuple(jnp.zeros((8, W), F32) for _ in range(4)) for _ in range(2))
        sums = lax.fori_loop(0, S // C, chunk, zeros)
        for half in (0, 1):
            sb, s0, s1, s2 = sums[half]
            gcb_ref[half] += jnp.sum(sb, axis=0, keepdims=True)
            gcw_ref[half, 0:1, :] += jnp.sum(s0, axis=0, keepdims=True)
            gcw_ref[half, 1:2, :] += jnp.sum(s1, axis=0, keepdims=True)
            gcw_ref[half, 2:3, :] += jnp.sum(s2, axis=0, keepdims=True)

    up_spec = pl.BlockSpec((2, S, W), lambda j, b: (0, b, j))
    cw_spec = pl.BlockSpec((2, 3, W), lambda j, b: (0, 0, j))
    cb_spec = pl.BlockSpec((2, 1, W), lambda j, b: (0, 0, j))
    return pl.pallas_call(
        body, name="gate_bwd",
        out_shape=(jax.ShapeDtypeStruct((2, T, D_FF), MXU_DTYPE), jax.ShapeDtypeStruct((2, 3, D_FF), F32),
                   jax.ShapeDtypeStruct((2, 1, D_FF), F32)),
        grid=(N_FF_TILES, B),
        in_specs=[up_spec, up_spec, pl.BlockSpec((S, W), lambda j, b: (b, j)), cw_spec],
        out_specs=(up_spec, cw_spec, cb_spec),
        scratch_shapes=[pltpu.VMEM((2, S + HALO, W), F32)],
        compiler_params=_params(("parallel", "arbitrary"), 12 * _nbytes((S, W), F32)),
    )(up3, conv3, dact, conv_w)


def _final(x2, tgt, g, tr=512):
    T, D = x2.shape

    def body(x_ref, t_ref, g_ref, dx_ref, loss_ref, gg_ref):
        @pl.when(pl.program_id(0) == 0)
        def _():
            loss_ref[...] = jnp.zeros_like(loss_ref)
            gg_ref[...] = jnp.zeros_like(gg_ref)

        xv = x_ref[...]
        gv = g_ref[...]
        r = lax.rsqrt(jnp.mean(xv * xv, axis=-1, keepdims=True) + EPS)
        xn = xv * r
        err = xn * gv - t_ref[...]
        loss_ref[...] += 0.5 * jnp.sum(jnp.mean(err * err, axis=-1, keepdims=True), axis=0, keepdims=True)
        dy = err * (1.0 / D)
        gg_ref[...] += jnp.sum(dy * xn, axis=0, keepdims=True)
        dxn = dy * gv
        dx_ref[...] = r * (dxn - xn * jnp.mean(dxn * xn, axis=-1, keepdims=True))

    row = pl.BlockSpec((tr, D), lambda i: (i, 0))
    vec = pl.BlockSpec((1, D), lambda i: (0, 0))
    return pl.pallas_call(
        body, name="final_loss",
        out_shape=(jax.ShapeDtypeStruct((T, D), F32), jax.ShapeDtypeStruct((1, 128), F32), jax.ShapeDtypeStruct((1, D), F32)),
        grid=(T // tr,), in_specs=[row, row, vec],
        out_specs=(row, pl.BlockSpec((1, 128), lambda i: (0, 0)), vec),
        compiler_params=_params(("arbitrary",), 6 * _nbytes((tr, D), F32)),
    )(x2, tgt, g)


def _sum_slabs(parts, name, tr):
    rows, cols = parts[0].shape
    n = len(parts)

    def body(*refs):
        acc = refs[0][...]
        for r in refs[1:n]:
            acc = acc + r[...]
        refs[n][...] = acc

    blk = pl.BlockSpec((tr, cols), lambda i: (i, 0))
    return pl.pallas_call(
        body, name=name, out_shape=jax.ShapeDtypeStruct((rows, cols), F32), grid=(rows // tr,),
        in_specs=[blk] * n, out_specs=blk,
        compiler_params=_params(("parallel",), (n + 1) * _nbytes((tr, cols), F32)),
    )(*parts)


ADAMW_BLOCK_BYTES = 2400 * 1024


def _adamw(w, g, m, v, name, copy_grad=False):
    lead = w.ndim == 3
    rows, cols = w.shape[-2:]
    fits = [d for d in range(8, rows + 1, 8) if rows % d == 0 and d * cols * 4 <= ADAMW_BLOCK_BYTES]
    tr = max(fits) if fits else rows
    c1 = 1.0 - ADAM_B1 ** ADAM_STEP
    c2 = 1.0 - ADAM_B2 ** ADAM_STEP

    def body(w_ref, g_ref, m_ref, v_ref, d_ref, nm_ref, nv_ref, *g_out):
        gv = g_ref[...]
        nm = ADAM_B1 * m_ref[...] + (1.0 - ADAM_B1) * gv
        nv = ADAM_B2 * v_ref[...] + (1.0 - ADAM_B2) * (gv * gv)
        nm_ref[...] = nm
        nv_ref[...] = nv
        d_ref[...] = -ADAM_LR * ((nm / c1) / (jnp.sqrt(nv / c2) + ADAM_EPS) + ADAM_WD * w_ref[...])
        if copy_grad:
            g_out[0][...] = gv

    blk = pl.BlockSpec((None, tr, cols), lambda i: (0, i, 0)) if lead else pl.BlockSpec((tr, cols), lambda i: (i, 0))
    sds = jax.ShapeDtypeStruct(w.shape, F32)
    n_out = 4 if copy_grad else 3
    return pl.pallas_call(
        body, name=name, out_shape=(sds,) * n_out, grid=(rows // tr,), in_specs=[blk] * 4, out_specs=(blk,) * n_out,
        compiler_params=_params(("parallel",), (4 + n_out) * _nbytes((tr, cols), F32)),
    )(w, g, m, v)


_ANY = pl.BlockSpec(memory_space=pl.ANY)


def _place():
    x, y, c = lax.axis_index("x"), lax.axis_index("y"), lax.axis_index("c")
    chips = [(1 - x, y), (x, 1 - y), (1 - x, 1 - y)]
    return x, y, c, chips


def _forward_halves(lands):
    n = len(lands)

    def body(*refs):
        outs, send, recv = refs[n:2 * n], refs[2 * n], refs[2 * n + 1]
        x, y, c, chips = _place()
        cps = []
        for w in range(n):
            for j, (px, py) in enumerate(chips):
                landed = outs[w].at[2 * px + py, c]
                cps.append(pltpu.make_async_remote_copy(
                    src_ref=landed, dst_ref=landed, send_sem=send.at[3 * w + j], recv_sem=recv.at[3 * w + j],
                    device_id=(x, y, 1 - c), device_id_type=MESH))
        for cp in cps:
            cp.start()
        for w in range(n):
            for j, (px, py) in enumerate(chips):
                other = outs[w].at[2 * px + py, 1 - c]
                pltpu.make_async_remote_copy(src_ref=other, dst_ref=other, send_sem=send.at[3 * w + j],
                                             recv_sem=recv.at[3 * w + j], device_id=(x, y, 1 - c),
                                             device_id_type=MESH).wait_recv()
        for cp in cps:
            cp.wait_send()

    dma = lambda k: pltpu.SemaphoreType.DMA((k,))
    return pl.pallas_call(
        body, name="gather_forward_halves", out_shape=tuple(jax.ShapeDtypeStruct(a.shape, a.dtype) for a in lands),
        in_specs=[_ANY] * n, out_specs=tuple([_ANY] * n), input_output_aliases={w: w for w in range(n)},
        scratch_shapes=[dma(3 * n), dma(3 * n)],
    )(*lands)


_HBM = pl.BlockSpec(memory_space=pltpu.HBM)
_SEM = pl.BlockSpec(memory_space=pltpu.SEMAPHORE)
_EFFECT = pltpu.SideEffectType.DATAFLOW_SIDE_EFFECTING


SEMS_PER_ARRAY = 8


def _exchange_copies(srcs, lands, send, recv, mode):
    x, y, c, chips = _place()
    if mode == "halves":
        cps = []
        for w, (src, land) in enumerate(zip(srcs, lands)):
            pieces = [(src.at[c], land.at[2 * x + y, c], (px, py, c)) for px, py in chips]
            pieces.append((src, land.at[2 * x + y], (x, y, 1 - c)))
            for k, (piece, dst, peer) in enumerate(pieces):
                cps.append(pltpu.make_async_remote_copy(
                    src_ref=piece, dst_ref=dst, send_sem=send.at[SEMS_PER_ARRAY * w + k],
                    recv_sem=recv.at[SEMS_PER_ARRAY * w + k], device_id=peer, device_id_type=MESH))
        return cps
    if mode == "swap":
        return [pltpu.make_async_remote_copy(
            src_ref=src.at[:, 1 - c], dst_ref=land, send_sem=send.at[SEMS_PER_ARRAY * w],
            recv_sem=recv.at[SEMS_PER_ARRAY * w], device_id=(x, y, 1 - c), device_id_type=MESH)
            for w, (src, land) in enumerate(zip(srcs, lands))]
    if mode == "all":
        flips = [(fx, fy, fc) for fx in (0, 1) for fy in (0, 1) for fc in (0, 1)][1:]
        peers = [(x ^ fx, y ^ fy, c ^ fc) for fx, fy, fc in flips]
        slot = 4 * x + 2 * y + c
    else:
        peers = [(px, py, c) for px, py in chips] + ([(x, y, 1 - c)] if mode == "gather" else [])
        slot = 2 * x + y
    cps = []
    for w, (src, land) in enumerate(zip(srcs, lands)):
        for k, peer in enumerate(peers):
            piece = src.at[2 * peer[0] + peer[1]] if mode == "scatter" else src
            cps.append(pltpu.make_async_remote_copy(
                src_ref=piece, dst_ref=land.at[slot], send_sem=send.at[SEMS_PER_ARRAY * w + k],
                recv_sem=recv.at[SEMS_PER_ARRAY * w + k], device_id=peer, device_id_type=MESH))
    return cps


def _exchange_start(srcs, name, mode, after):
    n = len(srcs)
    if mode == "swap":
        land_shapes = [(s.shape[0],) + s.shape[2:] for s in srcs]
    else:
        lead = {"gather": (N_CHIPS,), "halves": (N_CHIPS,), "scatter": (), "all": (2 * N_CHIPS,)}[mode]
        land_shapes = [lead + s.shape for s in srcs]

    def body(*refs):
        src_refs, land_refs = refs[:n], refs[n:2 * n]
        send, recv = refs[2 * n + 1], refs[2 * n + 2]
        token = refs[-1]
        for cp in _exchange_copies(src_refs, land_refs, send, recv, mode):
            cp.start()
        token[...] = jnp.zeros_like(token)

    sems = pltpu.SemaphoreType.DMA((SEMS_PER_ARRAY * n,))
    out = pl.pallas_call(
        body, name=name,
        out_shape=(sems, sems, *[pltpu.HBM(s.shape, s.dtype) for s in srcs],
                   *[pltpu.HBM(shp, s.dtype) for shp, s in zip(land_shapes, srcs)], jax.ShapeDtypeStruct((8, 128), F32)),
        in_specs=[_HBM] * (2 * n) + [_ANY],
        out_specs=(_SEM, _SEM, *[_HBM] * (2 * n), pl.BlockSpec(memory_space=pltpu.VMEM)),
        input_output_aliases={i: 2 + i for i in range(2 * n)},
        compiler_params=pltpu.CompilerParams(has_side_effects=_EFFECT),
    )(*[pltpu.with_memory_space_constraint(s, pltpu.HBM) for s in srcs],
      *[pltpu.with_memory_space_constraint(lax.empty(shp, s.dtype), pltpu.HBM) for shp, s in zip(land_shapes, srcs)],
      after)
    return out[0], out[1], out[2:2 + n], out[2 + n:2 + 2 * n], out[-1]


def _exchange_wait(started, name, mode, after):
    send, recv, src_thru, land_thru, _ = started
    n = len(src_thru)
    after = list(after) if isinstance(after, (list, tuple)) else [after]

    def body(*refs):
        src_refs, land_refs, send_ref, recv_ref = refs[:n], refs[n:2 * n], refs[2 * n], refs[2 * n + 1]
        for cp in _exchange_copies(src_refs, land_refs, send_ref, recv_ref, mode):
            cp.wait_send()
            cp.wait_recv()

    out = pl.pallas_call(
        body, name=name,
        out_shape=tuple(pltpu.HBM(a.shape, a.dtype) for a in list(src_thru) + list(land_thru)),
        in_specs=[_HBM] * (2 * n) + [_SEM, _SEM] + [_ANY] * len(after), out_specs=tuple([_HBM] * (2 * n)),
        input_output_aliases={i: i for i in range(2 * n)},
        compiler_params=pltpu.CompilerParams(has_side_effects=_EFFECT),
    )(*src_thru, *land_thru, send, recv, *after)
    return out[:n], out[n:]


def _swap_halves(gs, name):
    n = len(gs)

    def body(*refs):
        ins, outs, send, recv = refs[:n], refs[n:2 * n], refs[2 * n], refs[2 * n + 1]
        x, y, c, _ = _place()
        cps = []
        for w in range(n):
            cps.append(pltpu.make_async_remote_copy(
                src_ref=ins[w].at[:, 1 - c], dst_ref=outs[w], send_sem=send.at[w], recv_sem=recv.at[w],
                device_id=(x, y, 1 - c), device_id_type=MESH))
        for cp in cps:
            cp.start()
        for cp in cps:
            cp.wait()

    return pl.pallas_call(
        body, name=name,
        out_shape=tuple(jax.ShapeDtypeStruct((g.shape[0],) + g.shape[2:], g.dtype) for g in gs),
        in_specs=[_ANY] * n, out_specs=tuple([_ANY] * n),
        scratch_shapes=[pltpu.SemaphoreType.DMA((n,)), pltpu.SemaphoreType.DMA((n,))],
    )(*gs)


GRAD_PAYLOAD = jnp.bfloat16


def _half_blocks(half_rows, cols):
    if (half_rows // 2) % 16 == 0:
        return (half_rows // 2, cols), (lambda r: (r, 0))
    assert cols % 256 == 0, (half_rows, cols)
    return (half_rows, cols // 2), (lambda r: (0, r))


def _pair_sum(gs, gots, name):
    n = len(gs)
    core = lax.axis_index("c").astype(jnp.int32).reshape(1)

    def body(core_ref, *refs):
        del core_ref
        for w in range(n):
            refs[2 * n + w][...] = (refs[w][...] + refs[n + w][...]).astype(GRAD_PAYLOAD)

    in_specs, out_specs, out_shape, nbytes = [], [], [], 0
    cuts = [_half_blocks(g.shape[1] // 2, g.shape[2]) for g in gs]
    for g, ((br, bc), at) in zip(gs, cuts):
        per_half = (g.shape[1] // 2) // br
        in_specs.append(pl.BlockSpec((1, br, bc), lambda s, r, core, at=at, per_half=per_half:
                                     (s, per_half * core[0] + at(r)[0], at(r)[1])))
        nbytes += 3 * _nbytes((br, bc), F32)
    for g, ((br, bc), at) in zip(gs, cuts):
        in_specs.append(pl.BlockSpec((1, br, bc), lambda s, r, core, at=at: (s,) + at(r)))
        out_specs.append(pl.BlockSpec((1, br, bc), lambda s, r, core, at=at: (s,) + at(r)))
        out_shape.append(jax.ShapeDtypeStruct((g.shape[0], g.shape[1] // 2, g.shape[2]), GRAD_PAYLOAD))
    return pl.pallas_call(
        body, name=name, out_shape=tuple(out_shape),
        grid_spec=pltpu.PrefetchScalarGridSpec(num_scalar_prefetch=1, grid=(N_CHIPS, 2), in_specs=in_specs,
                                               out_specs=tuple(out_specs)),
        compiler_params=_params(("parallel", "parallel"), nbytes),
    )(core, *gs, *gots)


def _chip_sum(ps, landed):
    n = len(ps)
    x, y, c = lax.axis_index("x"), lax.axis_index("y"), lax.axis_index("c")
    where = jnp.stack([2 * x + y, 2 * (1 - x) + y, 2 * x + (1 - y), 2 * (1 - x) + (1 - y), c]).astype(jnp.int32)

    def body(where_ref, *refs):
        del where_ref
        for w in range(n):
            terms = [refs[4 * w + t][...].astype(F32) for t in range(4)]
            refs[4 * n + w][...] = ((terms[0] + terms[1]) + terms[2]) + terms[3]

    in_specs, out_specs, out_shape, args, nbytes = [], [], [], [], 0
    for p, a in zip(ps, landed):
        (br, bc), at = _half_blocks(a.shape[1], a.shape[2])
        blk = (1, br, bc)
        in_specs.append(pl.BlockSpec(blk, lambda r, where, at=at: (where[0],) + at(r)))
        args.append(p)
        for t in (1, 2, 3):
            in_specs.append(pl.BlockSpec(blk, lambda r, where, t=t, at=at: (where[t],) + at(r)))
            args.append(a)
        out_specs.append(pl.BlockSpec(blk, lambda r, where, at=at: (where[4],) + at(r)))
        out_shape.append(jax.ShapeDtypeStruct((2,) + a.shape[1:], F32))
        nbytes += 4 * _nbytes(blk, F32)
    return pl.pallas_call(
        body, name="grad_chip_sum", out_shape=tuple(out_shape),
        grid_spec=pltpu.PrefetchScalarGridSpec(num_scalar_prefetch=1, grid=(2,), in_specs=in_specs,
                                               out_specs=tuple(out_specs)),
        compiler_params=_params(("parallel",), nbytes),
    )(where, *args)


def _join_halves(ss):
    n = len(ss)

    def body(*refs):
        outs, send, recv = refs[n:2 * n], refs[2 * n], refs[2 * n + 1]
        x, y, c, _ = _place()
        cps = []
        for w in range(n):
            cps.append(pltpu.make_async_remote_copy(
                src_ref=outs[w].at[c], dst_ref=outs[w].at[c], send_sem=send.at[w], recv_sem=recv.at[w],
                device_id=(x, y, 1 - c), device_id_type=MESH))
        for cp in cps:
            cp.start()
        for w in range(n):
            got = outs[w].at[1 - c]
            pltpu.make_async_remote_copy(src_ref=got, dst_ref=got, send_sem=send.at[w], recv_sem=recv.at[w],
                                         device_id=(x, y, 1 - c), device_id_type=MESH).wait_recv()
        for cp in cps:
            cp.wait_send()

    dma = lambda k: pltpu.SemaphoreType.DMA((k,))
    return pl.pallas_call(
        body, name="grad_join_halves",
        out_shape=tuple(jax.ShapeDtypeStruct(s.shape, s.dtype) for s in ss),
        in_specs=[_ANY] * n, out_specs=tuple([_ANY] * n), input_output_aliases={w: w for w in range(n)},
        scratch_shapes=[dma(n), dma(n)],
    )(*ss)


def _rot_cols(w, axis=-1):
    a, b = jnp.split(w, 2, axis=axis)
    return jnp.concatenate([-b, a], axis=axis)


def _rot_cols_t(g, axis=-1):
    a, b = jnp.split(g, 2, axis=axis)
    return jnp.concatenate([b, -a], axis=axis)


def _cols_from_chips(a):
    n, r, cs = a.shape
    return jnp.transpose(a, (1, 0, 2)).reshape(r, n * cs)


def _cols_to_chips(a):
    r, cc = a.shape
    return jnp.transpose(a.reshape(r, N_CHIPS, cc // N_CHIPS), (1, 0, 2))


def _conv_w_split(cw):
    return jnp.swapaxes(cw.reshape(3, 2, D_FF), 0, 1)


def _conv_w_join(g):
    return jnp.swapaxes(g, 0, 1).reshape(3, 2 * D_FF)


_SEG =(D_MODEL, 2 * D_MODEL, 2 * D_MODEL + Q_RANK, 2 * D_MODEL + Q_RANK + KV_RANK, 2 * D_MODEL + Q_RANK + KV_RANK + ROPE,
        3 * D_MODEL + Q_RANK + KV_RANK + ROPE)


def _w_in_t_to_pad(wt):
    u, v, cq, ckv, kr, ga, gb = jnp.split(wt, _SEG, axis=0)
    return jnp.concatenate([u, v, ga, gb, cq, ckv, kr, _rot_cols(kr, axis=0)], axis=0)


def _w_in_t_from_pad(gt):
    u, v, ga, gb, cq, ckv, kr, krr = jnp.split(
        gt, (D_MODEL, 2 * D_MODEL, 3 * D_MODEL, 4 * D_MODEL, 4 * D_MODEL + Q_RANK, 4 * D_MODEL + Q_RANK + KV_RANK,
             4 * D_MODEL + Q_RANK + KV_RANK + ROPE), axis=0)
    return jnp.concatenate([u, v, cq, ckv, kr + _rot_cols_t(krr, axis=0), ga, gb], axis=0)


def _w_uq_to_pad(w):
    t = w.reshape(Q_RANK, HEADS, QK_DIM)
    nope, rope = t[..., :NOPE], t[..., NOPE:]
    return jnp.concatenate([nope, rope, _rot_cols(rope)], axis=-1).reshape(Q_RANK, HEADS * HEAD_PAD)


def _w_uq_from_pad(g):
    t = g.reshape(Q_RANK, HEADS, HEAD_PAD)
    nope, rope, rot = t[..., :NOPE], t[..., NOPE:QK_DIM], t[..., QK_DIM:]
    return jnp.concatenate([nope, rope + _rot_cols_t(rot)], axis=-1).reshape(Q_RANK, HEADS * QK_DIM)


def _w_ukv_to_pad(w):
    t = w.reshape(KV_RANK, HEADS, 2, NOPE)
    return jnp.swapaxes(t, 1, 2).reshape(KV_RANK, 2 * HEADS * NOPE)


def _w_ukv_from_pad(g):
    t = g.reshape(KV_RANK, 2, HEADS, NOPE)
    return jnp.swapaxes(t, 1, 2).reshape(KV_RANK, 2 * HEADS * NOPE)


def _rope_tables(positions):
    inv_freq = 1.0 / (ROPE_THETA ** (jnp.arange(0, ROPE, 2, dtype=F32) / ROPE))
    ang = positions.astype(F32).reshape(-1, 1) * inv_freq
    cos, sin = jnp.cos(ang), jnp.sin(ang)
    zero = jnp.zeros((ang.shape[0], 64), F32)
    return jnp.concatenate([cos, cos, zero], axis=1), jnp.concatenate([sin, sin, zero], axis=1)


_BIG = ("w_in", "w_uq", "w_ukv", "w_out", "w_up", "w_down")
UP_SHARD = 2 * D_FF // N_CHIPS
TOKEN_TILE = 1024


def _local_step(x, positions, tgt, wts, in_weights, mixer_weights, ffn_weights, on_ffn_grads, on_mixer_grads):
    B, S, D = x.shape
    T = B * S
    xf = x.reshape(T, D)
    cos_a, sin_a = _rope_tables(positions)
    bs_t = jnp.pad(wts["a_spatial_b"].T, ((0, 0), (0, 128 - A_GROUPS)))

    h = _rms_fwd(xf, wts["mix_norm"], "norm1_fwd")
    wts = dict(wts)
    wts["w_in"], token = in_weights([h, cos_a, sin_a])
    tm = min(TOKEN_TILE, T)
    z = _mm(h, wts["w_in"], "nt", "in_proj", tm=tm, tn=1536, tk=D, n_outer=True, after=token)
    wts["w_q"], wts["w_kv"], wts["w_out"] = mixer_weights(z)
    q, k, v, cqn, ckvn = _lat_fwd(z, wts["q_a_norm"], wts["kv_a_norm"], wts["w_q"], wts["w_kv"], cos_a, sin_a)
    yb, *lses = _attn_fwd(q, k, v, B, S)
    merged = _mix_fwd(z, yb, wts["a_v_norm_g"], wts["a_v_norm_b"], wts["a_spatial_w"], bs_t)
    x1 = _mm(merged, wts["w_out"], "nn", "out_proj", tm=min(512, T), tn=D, tk=D, add=xf)
    h2 = _rms_fwd(x1, wts["ffn_norm"], "norm2_fwd")
    wts["w_up"], wts["w_down"], wts["conv_w"] = ffn_weights(h2)
    up_pre = _mm(h2, wts["w_up"], "nn", "up_proj", tm=tm, tn=UP_SHARD, tk=D, dims=(T, 2 * D_FF, D),
                 b_spec=pl.BlockSpec((None, D, UP_SHARD), lambda i, j, k: (j, 0, 0)),
                 o_spec=pl.BlockSpec((None, tm, UP_SHARD), lambda i, j, k: (j // 2, i, j % 2)), out_shape=(2, T, D_FF),
                 n_outer=True)
    act, up_conv = _gate_fwd(up_pre, wts["conv_w"], wts["conv_b"], B, S)
    x2 = _mm(act, wts["w_down"], "nn", "down_proj", tm=tm, tn=D, tk=1408, add=x1)
    dx2, loss_row, g_final = _final(x2, tgt.reshape(T, D), wts["final_norm"])

    g = {"final_norm": g_final}
    dact = _mm(dx2, wts["w_down"], "nt", "down_proj_dx", tm=tm, tn=1408, tk=D, n_outer=True)
    tk2, tk1 = min(2048, T), min(1024, T)
    g["w_down"], g["w_down_lo"] = _mm(act, dx2, "tn", "down_proj_dw", tm=1408, tn=D, tk=tk1, copy_dtype=GRAD_PAYLOAD)
    dup, g["conv_w"], g["conv_b"] = _gate_bwd(up_pre, up_conv, dact, wts["conv_w"], B, S)
    g["w_up"], g["w_up_lo"] = _mm(
        h2, dup, "tn", "up_proj_dw", tm=D, tn=UP_SHARD, tk=tk2, dims=(D, 2 * D_FF, T), copy_dtype=GRAD_PAYLOAD,
        b_spec=pl.BlockSpec((None, tk2, UP_SHARD), lambda i, j, k: (j // 2, k, j % 2)),
        o_spec=pl.BlockSpec((None, D, UP_SHARD), lambda i, j, k: (j, 0, 0)), out_shape=(N_CHIPS, D, UP_SHARD))
    token, ffn_sent = on_ffn_grads(g)
    dh2 = _mm(dup, wts["w_up"], "nt", "up_proj_dx", tm=tm, tn=D, tk=UP_SHARD, dims=(T, D, 2 * D_FF), after=token,
              a_spec=pl.BlockSpec((None, tm, UP_SHARD), lambda i, j, k: (k // 2, i, k % 2)),
              b_spec=pl.BlockSpec((None, D, UP_SHARD), lambda i, j, k: (k, 0, 0)))
    token = ffn_sent(dh2)
    dx1, g["ffn_norm"] = _rms_bwd(x1, wts["ffn_norm"], dh2, dx2, "norm2_bwd")
    dm = _mm(dx1, wts["w_out"], "nt", "out_proj_dx", tm=min(512, T), tn=D, tk=D, after=token)
    g["w_out"], g["w_out_lo"] = _mm(merged, dx1, "tn", "out_proj_dw", tm=D, tn=D, tk=tk1, copy_dtype=GRAD_PAYLOAD)
    dz, dyb, dl, g["a_spatial_w"], gbs, g["a_v_norm_g"], g["a_v_norm_b"] = _mix_bwd(
        z, yb, dm, wts["a_v_norm_g"], wts["a_v_norm_b"], wts["a_spatial_w"], bs_t)
    g["a_spatial_b"] = gbs[:, :A_GROUPS].T
    delta = dl.reshape(HEADS * T // ATT_BLOCK, 1, ATT_BLOCK)
    dq, dk, dv = _attn_bwd(q, k, v, dyb, lses, delta, B, S)
    dz, dq_raw, dkv, g["q_a_norm"], g["kv_a_norm"] = _lat_bwd(
        dz, z, dq, dk, dv, wts["q_a_norm"], wts["kv_a_norm"], wts["w_q"], wts["w_kv"], cos_a, sin_a)
    g["w_q"] = _mm(cqn, dq_raw, "tn", "q_proj_dw", tm=Q_RANK, tn=HEADS * HEAD_PAD, tk=tk2)
    g["w_kv"] = _mm(ckvn, dkv, "tn", "kv_proj_dw", tm=KV_RANK, tn=2 * HEADS * NOPE, tk=tk2)
    g["w_in"] = _mm(dz, h, "tn", "in_proj_dw", tm=1536, tn=D, tk=tk2)
    token = on_mixer_grads(g)
    dh = _mm(dz, wts["w_in"], "nn", "in_proj_dx", tm=tm, tn=D, tk=1536, after=token)
    dx, g["mix_norm"] = _rms_bwd(xf, wts["mix_norm"], dh, dx1, "norm1_bwd")
    return loss_row[0, 0], dx.reshape(B, S, D), g


_SMALL = (("mix_norm", (1, D_MODEL)), ("a_v_norm_g", (1, D_MODEL)), ("a_v_norm_b", (1, D_MODEL)),
          ("a_spatial_w", (A_GROUPS * CHUNK, CHUNK)), ("a_spatial_b", (1, A_GROUPS * CHUNK)), ("q_a_norm", (1, Q_RANK)),
          ("kv_a_norm", (1, KV_RANK)), ("ffn_norm", (1, D_MODEL)), ("conv_b", (1, 2 * D_FF)), ("final_norm", (1, D_MODEL)),
          ("conv_w", (3, 2 * D_FF)))
_SMALL_SIZE = sum(math.prod(s) for _, s in _SMALL)
_SMALL_ROWS = -(-(_SMALL_SIZE + 1) // (128 * 8)) * 8


def kernel(x, positions, mix_norm, w_in, a_v_norm_g, a_v_norm_b, a_spatial_w, a_spatial_b, q_a_norm, w_uq, kv_a_norm, w_ukv, w_out, ffn_norm, w_up, conv_w, conv_b, w_down, final_norm, loss_target, m_mix_norm, m_w_in, m_a_v_norm_g, m_a_v_norm_b, m_a_spatial_w, m_a_spatial_b, m_q_a_norm, m_w_uq, m_kv_a_norm, m_w_ukv, m_w_out, m_ffn_norm, m_w_up, m_conv_w, m_conv_b, m_w_down, m_final_norm, v_mix_norm, v_w_in, v_a_v_norm_g, v_a_v_norm_b, v_a_spatial_w, v_a_spatial_b, v_q_a_norm, v_w_uq, v_kv_a_norm, v_w_ukv, v_w_out, v_ffn_norm, v_w_up, v_conv_w, v_conv_b, v_w_down, v_final_norm):
    weights = dict(mix_norm=mix_norm, w_in=w_in, a_v_norm_g=a_v_norm_g, a_v_norm_b=a_v_norm_b, a_spatial_w=a_spatial_w,
                   a_spatial_b=a_spatial_b, q_a_norm=q_a_norm, w_uq=w_uq, kv_a_norm=kv_a_norm, w_ukv=w_ukv, w_out=w_out,
                   ffn_norm=ffn_norm, w_up=w_up, conv_w=conv_w, conv_b=conv_b, w_down=w_down, final_norm=final_norm)
    m_in = dict(mix_norm=m_mix_norm, w_in=m_w_in, a_v_norm_g=m_a_v_norm_g, a_v_norm_b=m_a_v_norm_b,
                a_spatial_w=m_a_spatial_w, a_spatial_b=m_a_spatial_b, q_a_norm=m_q_a_norm, w_uq=m_w_uq,
                kv_a_norm=m_kv_a_norm, w_ukv=m_w_ukv, w_out=m_w_out, ffn_norm=m_ffn_norm, w_up=m_w_up, conv_w=m_conv_w,
                conv_b=m_conv_b, w_down=m_w_down, final_norm=m_final_norm)
    v_in = dict(mix_norm=v_mix_norm, w_in=v_w_in, a_v_norm_g=v_a_v_norm_g, a_v_norm_b=v_a_v_norm_b,
                a_spatial_w=v_a_spatial_w, a_spatial_b=v_a_spatial_b, q_a_norm=v_q_a_norm, w_uq=v_w_uq,
                kv_a_norm=v_kv_a_norm, w_ukv=v_w_ukv, w_out=v_w_out, ffn_norm=v_ffn_norm, w_up=v_w_up, conv_w=v_conv_w,
                conv_b=v_conv_b, w_down=v_w_down, final_norm=v_final_norm)
    names = list(weights)
    chip = 2 * lax.axis_index("x") + lax.axis_index("y")

    def halves(a):
        return a.reshape(a.shape[:-2] + (2, a.shape[-2] // 2, a.shape[-1]))

    w_in_t = jnp.swapaxes(w_in[0], 0, 1).astype(MXU_DTYPE)
    w_in_gather = _exchange_start([jnp.stack(jnp.split(w_in_t, 2, axis=1))], "w_in_gather_start", "halves",
                                  after=positions)
    gathers = {}
    wts = dict(
        mix_norm=mix_norm, a_v_norm_g=a_v_norm_g, a_v_norm_b=a_v_norm_b, a_spatial_w=a_spatial_w[0],
        a_spatial_b=a_spatial_b[0], q_a_norm=q_a_norm, kv_a_norm=kv_a_norm, ffn_norm=ffn_norm,
        final_norm=final_norm.reshape(1, D_MODEL), conv_b=conv_b.reshape(2, 1, D_FF))

    mixer_shards = [weights[n][0].astype(MXU_DTYPE) for n in _BIG[1:4]]
    ffn_shards = [w_up[0].astype(MXU_DTYPE), w_down[0].astype(MXU_DTYPE)]

    def in_weights(after):
        _, landed = _exchange_wait(w_in_gather, "w_in_gather_wait", "halves", list(after) + mixer_shards + ffn_shards)
        (w_in_sh,) = _forward_halves(list(landed))
        gathers["mixer"] = _exchange_start(mixer_shards, "mixer_gather_start", "gather", after=w_in_sh)
        gathers["ffn"] = _exchange_start(ffn_shards + [conv_w[0]], "ffn_gather_start", "gather",
                                         after=gathers["mixer"][4])
        w_in_pad = _w_in_t_to_pad(jnp.concatenate([w_in_sh[:, 0], w_in_sh[:, 1]], axis=-1).reshape(-1, D_MODEL))
        return w_in_pad, gathers["ffn"][4]

    def mixer_weights(after):
        _, (w_uq_sh, w_ukv_sh, w_out_sh) = _exchange_wait(gathers["mixer"], "mixer_gather_wait", "gather", after)
        return (_w_uq_to_pad(_cols_from_chips(w_uq_sh)), _w_ukv_to_pad(_cols_from_chips(w_ukv_sh)),
                w_out_sh.reshape(D_MODEL, D_MODEL))

    def ffn_weights(after):
        _, (w_up_sh, w_down_sh, cw_all) = _exchange_wait(gathers["ffn"], "ffn_gather_wait", "gather", after)
        return w_up_sh, w_down_sh.reshape(D_FF, D_MODEL), _conv_w_split(_cols_from_chips(cw_all))

    scatters = {}

    def start_scatter(slabs, slabs_lo, tag):
        got = _swap_halves([halves(s) for s in slabs_lo], tag + "_grad_swap_halves")
        sums = _pair_sum(slabs, got, tag + "_grad_pair_sum")
        scatters[tag] = _exchange_start(list(sums), tag + "_scatter_start", "scatter", after=slabs[-1])
        return scatters[tag][4]

    def on_ffn_grads(g):
        slabs, slabs_lo = [[g["w_up" + lo], g["w_down" + lo].reshape(N_CHIPS, D_FF // N_CHIPS, D_MODEL)]
                           for lo in ("", "_lo")]
        swap = _exchange_start([halves(s) for s in slabs_lo], "ffn_swap_start", "swap", after=slabs[1])

        def sent(after):
            _, got = _exchange_wait(swap, "ffn_swap_wait", "swap", after)
            sums = _pair_sum(slabs, got, "ffn_grad_pair_sum")
            scatters["ffn"] = _exchange_start(list(sums), "ffn_scatter_start", "scatter", after=got[0])
            return scatters["ffn"][4]

        return swap[4], sent

    def on_mixer_grads(g):
        slabs = [_w_in_t_from_pad(g["w_in"]).reshape(N_CHIPS, -1, D_MODEL), _cols_to_chips(_w_uq_from_pad(g["w_q"])),
                 _cols_to_chips(_w_ukv_from_pad(g["w_kv"]))]
        w_out_slabs = [g["w_out" + lo].reshape(N_CHIPS, D_MODEL // N_CHIPS, D_MODEL) for lo in ("", "_lo")]
        return start_scatter(slabs + w_out_slabs[:1], [s.astype(GRAD_PAYLOAD) for s in slabs] + w_out_slabs[1:], "mixer")

    loss_part, grad_x, g = _local_step(x, positions, loss_target, wts, in_weights, mixer_weights, ffn_weights,
                                       on_ffn_grads, on_mixer_grads)

    g_small_parts = dict(g)
    g_small_parts["conv_w"] = _conv_w_join(g["conv_w"])
    g_small_parts["conv_b"] = g["conv_b"].reshape(1, 2 * D_FF)
    flat = jnp.concatenate([g_small_parts[n].reshape(-1) for n, _ in _SMALL] + [loss_part.reshape(1)])
    flat = jnp.pad(flat, (0, _SMALL_ROWS * 128 - flat.shape[0])).reshape(_SMALL_ROWS, 128)
    small_gather = _exchange_start([flat], "small_gather_start", "all", after=grad_x)

    mixer_sums, mixer_landed = _exchange_wait(scatters["mixer"], "mixer_scatter_wait", "scatter", after=small_gather[4])
    ffn_sums, ffn_landed = _exchange_wait(scatters["ffn"], "ffn_scatter_wait", "scatter", after=mixer_landed[0])
    reduced = _chip_sum(list(mixer_sums) + list(ffn_sums), list(mixer_landed) + list(ffn_landed))
    g_big = dict(zip(_BIG, _join_halves(reduced)))

    grads, deltas, new_m, new_v = {}, {}, {}, {}

    def update(n, grad, copy_grad=False):
        w = weights[n]
        shape2 = grad.shape
        d, nm, nv, *again = _adamw(w.reshape(shape2), grad, m_in[n].reshape(shape2), v_in[n].reshape(shape2),
                                   "adamw_" + n, copy_grad)
        grads[n], deltas[n], new_m[n], new_v[n] = (t.reshape(w.shape) for t in (again[0] if copy_grad else grad, d, nm, nv))

    def update_transposed(n, grad_t):
        t = lambda a: jnp.swapaxes(a, 1, 2)
        d, nm, nv, again = _adamw(t(weights[n]), grad_t, t(m_in[n]), t(v_in[n]), "adamw_" + n, True)
        grads[n], deltas[n], new_m[n], new_v[n] = t(again), t(d), t(nm), t(nv)

    for n in _BIG:
        g3 = g_big[n].reshape((1, -1, g_big[n].shape[-1]))
        if n == "w_in":
            update_transposed(n, g3)
        else:
            update(n, g3, copy_grad=True)

    (own,), (everyone,) = _exchange_wait(small_gather, "small_gather_wait", "all", after=[deltas[n] for n in _BIG])
    device = 2 * chip + lax.axis_index("c")
    everyone = lax.dynamic_update_slice(everyone, own[None], (device, 0, 0))
    total = _sum_slabs([everyone[j] for j in range(8)], "small_grads_sum", tr=_SMALL_ROWS).reshape(-1)
    o = 0
    for n, shp in _SMALL:
        piece = total[o:o + math.prod(shp)].reshape(shp)
        o += math.prod(shp)
        if n == "conv_w":
            piece = lax.dynamic_slice_in_dim(piece, chip * UP_SHARD, UP_SHARD, axis=1)
        update(n, piece)
    loss = total[_SMALL_SIZE]
    return (loss, grad_x, *[grads[n] for n in names], *[deltas[n] for n in names], *[new_m[n] for n in names],
            *[new_v[n] for n in names])
```

```python
import functools
import math

import jax
import jax.numpy as jnp
from jax import lax
from jax.experimental import pallas as pl
from jax.experimental.pallas import tpu as pltpu

F32 = jnp.float32
MXU_DTYPE = jnp.bfloat16
MESH = pl.DeviceIdType.MESH

D_MODEL = 1024
EPS = 1e-6
A_GROUPS = 8
CHUNK = 128
HEADS = 8
NOPE = 128
ROPE = 64
QK_DIM = NOPE + ROPE
HEAD_PAD = 256
Q_RANK = 256
KV_RANK = 128
ROPE_THETA = 10000.0
D_FF = 2816
FF_TILE = 256
N_FF_TILES = D_FF // FF_TILE
LAT = 512
IN_PAD = 4 * D_MODEL + LAT
N_CHIPS = 4
ADAM_LR, ADAM_B1, ADAM_B2, ADAM_EPS, ADAM_WD, ADAM_STEP = 0.001, 0.9, 0.999, 1e-08, 0.01, 10

VMEM_CAP_V7X = 64 * 1024 * 1024
NEG = -1e30


def _params(sem, nbytes):
    limit = int(min(VMEM_CAP_V7X - (8 << 20), max(32 << 20, 3 * nbytes)))
    return pltpu.CompilerParams(dimension_semantics=sem, vmem_limit_bytes=limit)


def _nbytes(shape, dtype):
    return math.prod(shape) * jnp.dtype(dtype).itemsize


_DIMS = {"nn": (((1,), (0,)), ((), ())), "nt": (((1,), (1,)), ((), ())), "tn": (((0,), (0,)), ((), ()))}


def _mm(a, b, mode, name, *, tm, tn, tk, out_dtype=F32, add=None, dims=None, a_spec=None, b_spec=None,
        o_spec=None, out_shape=None, n_outer=False, copy_dtype=None, after=None):
    if dims is None:
        if mode == "nn":
            (M, K), (_, N) = a.shape, b.shape
        elif mode == "nt":
            (M, K), (N, _) = a.shape, b.shape
        else:
            (K, M), (_, N) = a.shape, b.shape
    else:
        M, N, K = dims
    a_blk = (tk, tm) if mode == "tn" else (tm, tk)
    b_blk = (tn, tk) if mode == "nt" else (tk, tn)
    if a_spec is None:
        a_spec = pl.BlockSpec(a_blk, (lambda i, j, k: (k, i)) if mode == "tn" else (lambda i, j, k: (i, k)))
    if b_spec is None:
        b_spec = pl.BlockSpec(b_blk, (lambda i, j, k: (j, k)) if mode == "nt" else (lambda i, j, k: (k, j)))
    if o_spec is None:
        o_spec = pl.BlockSpec((tm, tn), lambda i, j, k: (i, j))
    if out_shape is None:
        out_shape = (M, N)
    assert M % tm == 0 and N % tn == 0 and K % tk == 0, (name, M, N, K, tm, tn, tk)
    nk = K // tk
    contract = _DIMS[mode]
    has_add = add is not None

    def body(*refs):
        a_ref, b_ref = refs[0], refs[1]
        add_ref = refs[2] if has_add else None
        n_in = 2 + has_add + (after is not None)
        o_ref = refs[n_in]
        copy_ref = refs[n_in + 1] if copy_dtype is not None else None

        def product():
            return lax.dot_general(a_ref[...].astype(MXU_DTYPE), b_ref[...].astype(MXU_DTYPE), contract,
                                   preferred_element_type=F32)

        def finish(r):
            if has_add:
                r = r + add_ref[...]
            o_ref[...] = r.astype(out_dtype)
            if copy_ref is not None:
                copy_ref[...] = r.astype(copy_dtype)

        if nk == 1:
            finish(product())
            return
        acc = refs[-1]
        k = pl.program_id(2)

        @pl.when(k == 0)
        def _():
            acc[...] = jnp.zeros_like(acc)

        acc[...] += product()

        @pl.when(k == nk - 1)
        def _():
            finish(acc[...])

    in_specs = [a_spec, b_spec]
    args = [a, b]
    nbytes = _nbytes(a_blk, a.dtype) + _nbytes(b_blk, b.dtype) + 3 * _nbytes((tm, tn), F32)
    if has_add:
        in_specs.append(pl.BlockSpec((tm, tn), lambda i, j, k: (i, j)))
        args.append(add)
        nbytes += _nbytes((tm, tn), F32)
    if after is not None:
        in_specs.append(pl.BlockSpec(after.shape, lambda i, j, k: (0, 0)))
        args.append(after)
    grid = (M // tm, N // tn, nk)
    if n_outer:
        def swapped(spec):
            return pl.BlockSpec(spec.block_shape, lambda j, i, k, at=spec.index_map: at(i, j, k))

        grid = (N // tn, M // tm, nk)
        in_specs = [swapped(s) for s in in_specs]
        o_spec = swapped(o_spec)
    out_sds, out_specs = jax.ShapeDtypeStruct(out_shape, out_dtype), o_spec
    if copy_dtype is not None:
        out_sds, out_specs = (out_sds, jax.ShapeDtypeStruct(out_shape, copy_dtype)), (o_spec, o_spec)
    return pl.pallas_call(
        body, name=name, out_shape=out_sds, grid=grid, in_specs=in_specs, out_specs=out_specs,
        scratch_shapes=[pltpu.VMEM((tm, tn), F32)] if nk > 1 else [],
        compiler_params=_params(("parallel", "parallel", "arbitrary"), nbytes),
    )(*args)


_GELU_C = math.sqrt(2.0 / math.pi)
_GELU_A = 0.044715


def _sigmoid(x):
    return 0.5 * jnp.tanh(0.5 * x) + 0.5


def _gelu(x):
    t = jnp.tanh(x * (_GELU_C + (_GELU_C * _GELU_A) * (x * x)))
    return x * (0.5 + 0.5 * t)


def _gelu_and_grad(x):
    x2 = x * x
    t = jnp.tanh(x * (_GELU_C + (_GELU_C * _GELU_A) * x2))
    cdf = 0.5 + 0.5 * t
    grad = cdf + (0.5 * x) * (1.0 - t * t) * (_GELU_C + (3.0 * _GELU_C * _GELU_A) * x2)
    return x * cdf, grad


def _rope_mix(g, cos_a, sin_a):
    return g * cos_a + pltpu.roll(g, 64, 1) * sin_a


def _rope_mix_bwd(d, cos_a, sin_a):
    return d * cos_a + pltpu.roll(d * sin_a, 64, 1)


def _rms_fwd(x, g, name, tr=1024):
    T, D = x.shape

    def body(x_ref, g_ref, h_ref):
        xv = x_ref[...]
        r = lax.rsqrt(jnp.mean(xv * xv, axis=-1, keepdims=True) + EPS)
        h_ref[...] = ((xv * r) * g_ref[...]).astype(h_ref.dtype)

    return pl.pallas_call(
        body, name=name, out_shape=jax.ShapeDtypeStruct((T, D), MXU_DTYPE), grid=(T // tr,),
        in_specs=[pl.BlockSpec((tr, D), lambda i: (i, 0)), pl.BlockSpec((1, D), lambda i: (0, 0))],
        out_specs=pl.BlockSpec((tr, D), lambda i: (i, 0)),
        compiler_params=_params(("parallel",), 3 * _nbytes((tr, D), F32)),
    )(x, g)


def _rms_bwd(x, g, dh, dres, name, tr=512):
    T, D = x.shape

    def body(x_ref, g_ref, dh_ref, dres_ref, dx_ref, gg_ref):
        @pl.when(pl.program_id(0) == 0)
        def _():
            gg_ref[...] = jnp.zeros_like(gg_ref)

        xv = x_ref[...]
        r = lax.rsqrt(jnp.mean(xv * xv, axis=-1, keepdims=True) + EPS)
        xn = xv * r
        dhv = dh_ref[...]
        dxn = dhv * g_ref[...]
        dx_ref[...] = dres_ref[...] + r * (dxn - xn * jnp.mean(dxn * xn, axis=-1, keepdims=True))
        gg_ref[...] += jnp.sum(dhv * xn, axis=0, keepdims=True)

    row = pl.BlockSpec((tr, D), lambda i: (i, 0))
    vec = pl.BlockSpec((1, D), lambda i: (0, 0))
    return pl.pallas_call(
        body, name=name,
        out_shape=(jax.ShapeDtypeStruct((T, D), F32), jax.ShapeDtypeStruct((1, D), F32)),
        grid=(T // tr,), in_specs=[row, vec, row, row], out_specs=(row, vec),
        compiler_params=_params(("arbitrary",), 6 * _nbytes((tr, D), F32)),
    )(x, g, dh, dres)


def _lat_fwd(z, gq, gkv, wq, wkv, cos_a, sin_a, tr=512):
    T = z.shape[0]
    lat_blk = (4 * D_MODEL) // LAT

    def body(z_ref, gq_ref, gkv_ref, wq_ref, wkv_ref, cos_ref, sin_ref, q_ref, k_ref, v_ref, cqn_ref, ckvn_ref):
        zl = z_ref[...]
        cos_v, sin_v = cos_ref[...], sin_ref[...]
        cq = zl[:, :Q_RANK]
        ckv = zl[:, Q_RANK:Q_RANK + KV_RANK]
        krb = zl[:, Q_RANK + KV_RANK:]
        cqn = ((cq * lax.rsqrt(jnp.mean(cq * cq, axis=-1, keepdims=True) + EPS)) * gq_ref[...]).astype(MXU_DTYPE)
        ckvn = ((ckv * lax.rsqrt(jnp.mean(ckv * ckv, axis=-1, keepdims=True) + EPS)) * gkv_ref[...]).astype(MXU_DTYPE)
        cqn_ref[...] = cqn
        ckvn_ref[...] = ckvn
        krr = _rope_mix(krb, cos_v, sin_v).astype(MXU_DTYPE)
        q = jnp.dot(cqn, wq_ref[...], preferred_element_type=F32)
        kv = jnp.dot(ckvn, wkv_ref[...], preferred_element_type=F32)
        for h in range(HEADS):
            o = h * HEAD_PAD
            q_ref[:, o:o + NOPE] = q[:, o:o + NOPE].astype(MXU_DTYPE)
            q_ref[:, o + NOPE:o + HEAD_PAD] = _rope_mix(q[:, o + NOPE:o + HEAD_PAD], cos_v, sin_v).astype(MXU_DTYPE)
            k_ref[:, o:o + NOPE] = kv[:, h * NOPE:(h + 1) * NOPE].astype(MXU_DTYPE)
            k_ref[:, o + NOPE:o + HEAD_PAD] = krr
        v_ref[...] = kv[:, HEADS * NOPE:].astype(MXU_DTYPE)

    def row(w):
        return pl.BlockSpec((tr, w), lambda i: (i, 0))

    def full(a):
        return pl.BlockSpec(a.shape, lambda i: (0, 0))

    return pl.pallas_call(
        body, name="lat_fwd",
        out_shape=(jax.ShapeDtypeStruct((T, HEADS * HEAD_PAD), MXU_DTYPE), jax.ShapeDtypeStruct((T, HEADS * HEAD_PAD), MXU_DTYPE),
                   jax.ShapeDtypeStruct((T, HEADS * NOPE), MXU_DTYPE), jax.ShapeDtypeStruct((T, Q_RANK), MXU_DTYPE),
                   jax.ShapeDtypeStruct((T, KV_RANK), MXU_DTYPE)),
        grid=(T // tr,),
        in_specs=[pl.BlockSpec((tr, LAT), lambda i: (i, lat_blk)), full(gq), full(gkv), full(wq), full(wkv), row(128), row(128)],
        out_specs=(row(HEADS * HEAD_PAD), row(HEADS * HEAD_PAD), row(HEADS * NOPE), row(Q_RANK), row(KV_RANK)),
        compiler_params=_params(("parallel",), 8 * _nbytes((tr, HEADS * HEAD_PAD), F32)),
    )(z, gq, gkv, wq, wkv, cos_a, sin_a)


ATT_BLOCK = 256
_SCALE = QK_DIM ** -0.5


def _causal_mask(n):
    return lax.broadcasted_iota(jnp.int32, (n, n), 1) <= lax.broadcasted_iota(jnp.int32, (n, n), 0)


def _causal_mask_t(n):
    return lax.broadcasted_iota(jnp.int32, (n, n), 0) <= lax.broadcasted_iota(jnp.int32, (n, n), 1)


ATT_HEADS = 4


def _attn_fwd(q, k, v, B, S):
    tq = ATT_BLOCK
    nq = S // tq
    T = B * S
    hp, groups = ATT_HEADS, HEADS // ATT_HEADS

    def body(q_ref, k_ref, v_ref, o_ref, *lse_refs):
        qi = pl.program_id(2)
        qs = [q_ref[:, t * HEAD_PAD:(t + 1) * HEAD_PAD] for t in range(hp)]

        def scores(j, t):
            rows = pl.ds(pl.multiple_of(j * tq, tq), tq)
            return lax.dot_general(k_ref[rows, t * HEAD_PAD:(t + 1) * HEAD_PAD], qs[t], _DIMS["nt"],
                                   preferred_element_type=F32)

        def step(j, carry, last):
            rows = pl.ds(pl.multiple_of(j * tq, tq), tq)
            out = []
            for t in range(hp):
                m, l, acc, st = carry[t]
                st_next = st if last else scores(j + 1, t)
                st = st * _SCALE
                if last:
                    st = jnp.where(_causal_mask_t(tq), st, NEG)
                m_new = jnp.maximum(m, jnp.max(st, axis=0, keepdims=True))
                alpha = jnp.exp(m - m_new)
                p = jnp.exp(st - m_new)
                l = alpha * l + jnp.sum(p, axis=0, keepdims=True)
                acc = alpha * acc + lax.dot_general(v_ref[rows, t * NOPE:(t + 1) * NOPE], p.astype(MXU_DTYPE),
                                                    _DIMS["tn"], preferred_element_type=F32)
                out.append((m_new, l, acc, st_next))
            return tuple(out)

        init = tuple((jnp.full((1, tq), NEG, F32), jnp.zeros((1, tq), F32), jnp.zeros((NOPE, tq), F32), scores(0, t))
                     for t in range(hp))
        carry = lax.fori_loop(0, qi, lambda j, c: step(j, c, False), init)
        carry = step(qi, carry, True)
        for t in range(hp):
            m, l, acc, _ = carry[t]
            o_ref[:, t * NOPE:(t + 1) * NOPE] = (acc / l).T
            lse_refs[t][0] = m + jnp.log(l)

    lse_sds = jax.ShapeDtypeStruct((groups * B * nq, 1, tq), F32)
    lse_spec = pl.BlockSpec((1, 1, tq), lambda b, h, i: ((h * B + b) * nq + i, 0, 0))
    return pl.pallas_call(
        body, name="attn_fwd",
        out_shape=(jax.ShapeDtypeStruct((T, HEADS * NOPE), F32),) + (lse_sds,) * hp,
        grid=(B, groups, nq),
        in_specs=[pl.BlockSpec((tq, hp * HEAD_PAD), lambda b, h, i: (b * nq + i, h)),
                  pl.BlockSpec((S, hp * HEAD_PAD), lambda b, h, i: (b, h)),
                  pl.BlockSpec((S, hp * NOPE), lambda b, h, i: (b, h))],
        out_specs=(pl.BlockSpec((tq, hp * NOPE), lambda b, h, i: (b * nq + i, h)),) + (lse_spec,) * hp,
        compiler_params=_params(("parallel", "parallel", "arbitrary"), 4 * hp * _nbytes((S, HEAD_PAD), MXU_DTYPE)),
    )(q, k, v)


def _attn_bwd(q, k, v, do, lses, delta, B, S):
    tq = ATT_BLOCK
    nq = S // tq
    T = B * S
    hp, groups = ATT_HEADS, HEADS // ATT_HEADS

    def body(q_ref, k_ref, v_ref, do_ref, *refs):
        lse_refs, dl_refs = refs[:hp], refs[hp:2 * hp]
        dq_out, dk_ref, dv_ref, dq_ref = refs[2 * hp:]
        kj = pl.program_id(2)

        @pl.when(kj == 0)
        def _():
            dq_ref[...] = jnp.zeros_like(dq_ref)

        def products(i, t):
            rows = pl.ds(pl.multiple_of(i * tq, tq), tq)
            st = lax.dot_general(k_ref[:, t * HEAD_PAD:(t + 1) * HEAD_PAD], q_ref[rows, t * HEAD_PAD:(t + 1) * HEAD_PAD],
                                 _DIMS["nt"], preferred_element_type=F32)
            dpt = lax.dot_general(v_ref[:, t * NOPE:(t + 1) * NOPE], do_ref[rows, t * NOPE:(t + 1) * NOPE],
                                  _DIMS["nt"], preferred_element_type=F32)
            return st, dpt

        def step(i, carry, masked):
            rows = pl.ds(pl.multiple_of(i * tq, tq), tq)
            nxt = jnp.minimum(i + 1, nq - 1)
            out = []
            for t in range(hp):
                dk, dv, st, dpt = carry[t]
                st_next, dpt_next = products(nxt, t)
                qk_cols = slice(t * HEAD_PAD, (t + 1) * HEAD_PAD)
                v_cols = slice(t * NOPE, (t + 1) * NOPE)
                p = jnp.exp(st * _SCALE - lse_refs[t][i])
                if masked:
                    p = jnp.where(_causal_mask_t(tq), p, 0.0)
                dv = dv + jnp.dot(p.astype(MXU_DTYPE), do_ref[rows, v_cols], preferred_element_type=F32)
                ds = (p * (dpt - dl_refs[t][i]) * _SCALE).astype(MXU_DTYPE)
                dk = dk + jnp.dot(ds, q_ref[rows, qk_cols], preferred_element_type=F32)
                dq_ref[rows, qk_cols] += lax.dot_general(ds, k_ref[:, qk_cols], _DIMS["tn"], preferred_element_type=F32)
                out.append((dk, dv, st_next, dpt_next))
            return tuple(out)

        init = tuple((jnp.zeros((tq, HEAD_PAD), F32), jnp.zeros((tq, NOPE), F32)) + products(kj, t) for t in range(hp))
        carry = step(kj, init, True)
        carry = lax.fori_loop(kj + 1, nq, lambda i, c: step(i, c, False), carry)
        for t in range(hp):
            dk_ref[:, t * HEAD_PAD:(t + 1) * HEAD_PAD] = carry[t][0].astype(dk_ref.dtype)
            dv_ref[:, t * NOPE:(t + 1) * NOPE] = carry[t][1].astype(dv_ref.dtype)

        @pl.when(kj == nq - 1)
        def _():
            dq_out[...] = dq_ref[...].astype(dq_out.dtype)

    seq = lambda w: pl.BlockSpec((S, w), lambda b, h, j: (b, h))
    blk = lambda w: pl.BlockSpec((tq, w), lambda b, h, j: (b * nq + j, h))
    lse_spec = pl.BlockSpec((nq, 1, tq), lambda b, h, j: (h * B + b, 0, 0))
    dl_specs = [pl.BlockSpec((nq, 1, tq), lambda b, h, j, t=t: ((h * hp + t) * B + b, 0, 0)) for t in range(hp)]
    return pl.pallas_call(
        body, name="attn_bwd",
        out_shape=(jax.ShapeDtypeStruct((T, HEADS * HEAD_PAD), MXU_DTYPE), jax.ShapeDtypeStruct((T, HEADS * HEAD_PAD), MXU_DTYPE),
                   jax.ShapeDtypeStruct((T, HEADS * NOPE), MXU_DTYPE)),
        grid=(B, groups, nq),
        in_specs=[seq(hp * HEAD_PAD), blk(hp * HEAD_PAD), blk(hp * NOPE), seq(hp * NOPE)] + [lse_spec] * hp + dl_specs,
        out_specs=(seq(hp * HEAD_PAD), blk(hp * HEAD_PAD), blk(hp * NOPE)),
        scratch_shapes=[pltpu.VMEM((S, hp * HEAD_PAD), F32)],
        compiler_params=_params(("parallel", "parallel", "arbitrary"), 8 * hp * _nbytes((S, HEAD_PAD), F32)),
    )(q, k, v, do, *lses, *([delta] * hp))


MIX_ROWS = 256


def _tril_weights(ws_ref, g):
    return jnp.where(_causal_mask(CHUNK), ws_ref[g], 0.0).astype(MXU_DTYPE)


def _layer_norm_stats(va):
    mu = jnp.mean(va, axis=-1, keepdims=True)
    xc = va - mu
    rs = lax.rsqrt(jnp.mean(xc * xc, axis=-1, keepdims=True) + EPS)
    return xc * rs


def _mix_specs(tr):
    zcol = lambda c: pl.BlockSpec((tr, D_MODEL), lambda i, c=c: (i, c))
    row = pl.BlockSpec((tr, D_MODEL), lambda i: (i, 0))
    vec = pl.BlockSpec((1, D_MODEL), lambda i: (0, 0))
    ws = pl.BlockSpec((A_GROUPS, CHUNK, CHUNK), lambda i: (0, 0, 0))
    bs = pl.BlockSpec((CHUNK, 128), lambda i: (0, 0))
    return zcol, row, vec, ws, bs


def _mix_fwd(z, yb, ln_g, ln_b, ws, bs_t):
    T = z.shape[0]
    tr = MIX_ROWS
    zcol, row, vec, ws_spec, bs_spec = _mix_specs(tr)

    def body(zu_ref, zv_ref, zga_ref, zgb_ref, yb_ref, g_ref, b_ref, ws_ref, bs_ref, out_ref, vn_s):
        vhat = _layer_norm_stats(_gelu(zv_ref[...]))
        vn_s[...] = (vhat * g_ref[...] + b_ref[...]).astype(MXU_DTYPE)
        for g in range(A_GROUPS):
            w = _tril_weights(ws_ref, g)
            bias = bs_ref[:, g:g + 1]
            cols = slice(g * CHUNK, (g + 1) * CHUNK)
            for c in range(tr // CHUNK):
                rows = slice(c * CHUNK, (c + 1) * CHUNK)
                mixed = jnp.dot(w, vn_s[rows, cols], preferred_element_type=F32) + bias
                ya = _gelu(zu_ref[rows, cols]) * mixed
                merged = _sigmoid(zga_ref[rows, cols]) * ya + _sigmoid(zgb_ref[rows, cols]) * yb_ref[rows, cols]
                out_ref[rows, cols] = merged.astype(MXU_DTYPE)

    return pl.pallas_call(
        body, name="mix_fwd", out_shape=jax.ShapeDtypeStruct((T, D_MODEL), MXU_DTYPE), grid=(T // tr,),
        in_specs=[zcol(0), zcol(1), zcol(2), zcol(3), row, vec, vec, ws_spec, bs_spec], out_specs=row,
        scratch_shapes=[pltpu.VMEM((tr, D_MODEL), MXU_DTYPE)],
        compiler_params=_params(("parallel",), 8 * _nbytes((tr, D_MODEL), F32)),
    )(z, z, z, z, yb, ln_g, ln_b, ws, bs_t)


def _mix_bwd(z, yb, dm, ln_g, ln_b, ws, bs_t):
    T = z.shape[0]
    tr = MIX_ROWS
    zcol, row, vec, ws_spec, bs_spec = _mix_specs(tr)

    def body(zu_ref, zv_ref, zga_ref, zgb_ref, yb_ref, dm_ref, g_ref, b_ref, ws_ref, bs_ref,
             dz_ref, dyb_ref, dl_ref, gws_ref, gbs_ref, glg_ref, glb_ref, vn_s, dvn_s):
        @pl.when(pl.program_id(0) == 0)
        def _():
            gws_ref[...] = jnp.zeros_like(gws_ref)
            gbs_ref[...] = jnp.zeros_like(gbs_ref)
            glg_ref[...] = jnp.zeros_like(glg_ref)
            glb_ref[...] = jnp.zeros_like(glb_ref)

        lane = lax.broadcasted_iota(jnp.int32, (CHUNK, 128), 1)
        va, dgelu_v = _gelu_and_grad(zv_ref[...])
        mu = jnp.mean(va, axis=-1, keepdims=True)
        xc = va - mu
        rs = lax.rsqrt(jnp.mean(xc * xc, axis=-1, keepdims=True) + EPS)
        vhat = xc * rs
        vn_s[...] = (vhat * g_ref[...] + b_ref[...]).astype(MXU_DTYPE)
        gbs_acc = jnp.zeros((CHUNK, 128), F32)
        for g in range(A_GROUPS):
            w = _tril_weights(ws_ref, g)
            bias = bs_ref[:, g:g + 1]
            cols = slice(g * CHUNK, (g + 1) * CHUNK)
            gw_acc = jnp.zeros((CHUNK, CHUNK), F32)
            for c in range(tr // CHUNK):
                rows = slice(c * CHUNK, (c + 1) * CHUNK)
                vn = vn_s[rows, cols]
                mixed = jnp.dot(w, vn, preferred_element_type=F32) + bias
                ua, dgelu_u = _gelu_and_grad(zu_ref[rows, cols])
                dmv = dm_ref[rows, cols]
                sa = _sigmoid(zga_ref[rows, cols])
                dya = dmv * sa
                dz_ref[rows, 2 * D_MODEL + g * CHUNK:2 * D_MODEL + (g + 1) * CHUNK] = (
                    dmv * (ua * mixed) * (sa * (1.0 - sa))).astype(dz_ref.dtype)
                dz_ref[rows, cols] = (dya * mixed * dgelu_u).astype(dz_ref.dtype)
                dmix = dya * ua
                gbs_acc = gbs_acc + jnp.where(lane == g, jnp.sum(dmix, axis=-1, keepdims=True), 0.0)
                dmix_b = dmix.astype(MXU_DTYPE)
                gw_acc = gw_acc + lax.dot_general(dmix_b, vn, _DIMS["nt"], preferred_element_type=F32)
                dvn_s[rows, cols] = lax.dot_general(w, dmix_b, _DIMS["tn"], preferred_element_type=F32)
            gws_ref[g] += jnp.where(_causal_mask(CHUNK), gw_acc, 0.0)
        gbs_ref[...] += gbs_acc

        dvn = dvn_s[...]
        glg_ref[...] += jnp.sum(dvn * vhat, axis=0, keepdims=True)
        glb_ref[...] += jnp.sum(dvn, axis=0, keepdims=True)
        dvh = dvn * g_ref[...]
        dva = rs * (dvh - jnp.mean(dvh, axis=-1, keepdims=True) - vhat * jnp.mean(dvh * vhat, axis=-1, keepdims=True))
        dz_ref[:, D_MODEL:2 * D_MODEL] = (dva * dgelu_v).astype(dz_ref.dtype)

        dmv = dm_ref[...]
        ybv = yb_ref[...]
        sb = _sigmoid(zgb_ref[...])
        dyb = dmv * sb
        dyb_ref[...] = dyb.astype(dyb_ref.dtype)
        dz_ref[:, 3 * D_MODEL:4 * D_MODEL] = (dmv * ybv * (sb * (1.0 - sb))).astype(dz_ref.dtype)
        dz_ref[:, 4 * D_MODEL:] = jnp.zeros((tr, LAT), dz_ref.dtype)
        prod = dyb * ybv
        sel = (lax.broadcasted_iota(jnp.int32, (HEADS, D_MODEL), 1) // NOPE
               == lax.broadcasted_iota(jnp.int32, (HEADS, D_MODEL), 0)).astype(jnp.bfloat16)
        hi = prod.astype(jnp.bfloat16)
        rest = prod - hi.astype(F32)
        mid = rest.astype(jnp.bfloat16)
        lo = (rest - mid.astype(F32)).astype(jnp.bfloat16)
        dl_ref[...] = (lax.dot_general(sel, hi, _DIMS["nt"], preferred_element_type=F32)
                       + lax.dot_general(sel, mid, _DIMS["nt"], preferred_element_type=F32)
                       + lax.dot_general(sel, lo, _DIMS["nt"], preferred_element_type=F32))

    return pl.pallas_call(
        body, name="mix_bwd",
        out_shape=(jax.ShapeDtypeStruct((T, IN_PAD), MXU_DTYPE), jax.ShapeDtypeStruct((T, D_MODEL), MXU_DTYPE),
                   jax.ShapeDtypeStruct((HEADS, T), F32), jax.ShapeDtypeStruct((A_GROUPS, CHUNK, CHUNK), F32),
                   jax.ShapeDtypeStruct((CHUNK, 128), F32), jax.ShapeDtypeStruct((1, D_MODEL), F32),
                   jax.ShapeDtypeStruct((1, D_MODEL), F32)),
        grid=(T // tr,),
        in_specs=[zcol(0), zcol(1), zcol(2), zcol(3), row, row, vec, vec, ws_spec, bs_spec],
        out_specs=(pl.BlockSpec((tr, IN_PAD), lambda i: (i, 0)), row, pl.BlockSpec((HEADS, tr), lambda i: (0, i)),
                   ws_spec, bs_spec, vec, vec),
        scratch_shapes=[pltpu.VMEM((tr, D_MODEL), MXU_DTYPE), pltpu.VMEM((tr, D_MODEL), F32)],
        compiler_params=_params(("arbitrary",), 12 * _nbytes((tr, D_MODEL), F32)),
    )(z, z, z, z, yb, dm, ln_g, ln_b, ws, bs_t)


def _lat_bwd(dz, z, dq, dk, dv, gq, gkv, wq, wkv, cos_a, sin_a, tr=256):
    T = z.shape[0]
    lat_blk = (4 * D_MODEL) // LAT

    def body(dz_in, z_ref, dq_ref, dk_ref, dv_ref, gq_ref, gkv_ref, wq_ref, wkv_ref, cos_ref, sin_ref,
             dz_ref, dqr_ref, dkv_ref, ggq_ref, ggkv_ref):
        del dz_in

        @pl.when(pl.program_id(0) == 0)
        def _():
            ggq_ref[...] = jnp.zeros_like(ggq_ref)
            ggkv_ref[...] = jnp.zeros_like(ggkv_ref)

        cos_v, sin_v = cos_ref[...], sin_ref[...]
        dkr = jnp.zeros((tr, 128), F32)
        for h in range(HEADS):
            o = h * HEAD_PAD
            dqr_ref[:, o:o + NOPE] = dq_ref[:, o:o + NOPE].astype(MXU_DTYPE)
            dqr_ref[:, o + NOPE:o + HEAD_PAD] = _rope_mix_bwd(dq_ref[:, o + NOPE:o + HEAD_PAD], cos_v, sin_v).astype(MXU_DTYPE)
            dkv_ref[:, h * NOPE:(h + 1) * NOPE] = dk_ref[:, o:o + NOPE].astype(MXU_DTYPE)
            dkr = dkr + _rope_mix_bwd(dk_ref[:, o + NOPE:o + HEAD_PAD], cos_v, sin_v)
        dkv_ref[:, HEADS * NOPE:] = dv_ref[...]
        dcqn = lax.dot_general(dqr_ref[...], wq_ref[...], _DIMS["nt"], preferred_element_type=F32)
        dckvn = lax.dot_general(dkv_ref[...], wkv_ref[...], _DIMS["nt"], preferred_element_type=F32)

        zl = z_ref[...]

        def rms_bwd(c, dn, g_ref, gg_ref):
            r = lax.rsqrt(jnp.mean(c * c, axis=-1, keepdims=True) + EPS)
            ch = c * r
            gg_ref[...] += jnp.sum(dn * ch, axis=0, keepdims=True)
            dch = dn * g_ref[...]
            return r * (dch - ch * jnp.mean(dch * ch, axis=-1, keepdims=True))

        dz_ref[:, :Q_RANK] = rms_bwd(zl[:, :Q_RANK], dcqn, gq_ref, ggq_ref).astype(dz_ref.dtype)
        dz_ref[:, Q_RANK:Q_RANK + KV_RANK] = rms_bwd(zl[:, Q_RANK:Q_RANK + KV_RANK], dckvn, gkv_ref, ggkv_ref).astype(dz_ref.dtype)
        dz_ref[:, Q_RANK + KV_RANK:] = dkr.astype(dz_ref.dtype)

    def row(w):
        return pl.BlockSpec((tr, w), lambda i: (i, 0))

    def full(a):
        return pl.BlockSpec(a.shape, lambda i: (0, 0))

    lat = pl.BlockSpec((tr, LAT), lambda i: (i, lat_blk))
    return pl.pallas_call(
        body, name="lat_bwd",
        out_shape=(jax.ShapeDtypeStruct(dz.shape, dz.dtype), jax.ShapeDtypeStruct((T, HEADS * HEAD_PAD), MXU_DTYPE),
                   jax.ShapeDtypeStruct((T, 2 * HEADS * NOPE), MXU_DTYPE), jax.ShapeDtypeStruct(gq.shape, F32),
                   jax.ShapeDtypeStruct(gkv.shape, F32)),
        grid=(T // tr,),
        in_specs=[pl.BlockSpec(memory_space=pl.ANY), lat, row(HEADS * HEAD_PAD), row(HEADS * HEAD_PAD), row(HEADS * NOPE),
                  full(gq), full(gkv), full(wq), full(wkv), row(128), row(128)],
        out_specs=(lat, row(HEADS * HEAD_PAD), row(2 * HEADS * NOPE), full(gq), full(gkv)),
        input_output_aliases={0: 0},
        compiler_params=_params(("arbitrary",), 8 * _nbytes((tr, HEADS * HEAD_PAD), F32)),
    )(dz, z, dq, dk, dv, gq, gkv, wq, wkv, cos_a, sin_a)


GATE_ROWS = 256
HALO = 8


def _taps(ref, half, r, first):
    C = GATE_ROWS
    if first:
        xs = jnp.concatenate([jnp.zeros((HALO, ref.shape[-1]), F32), ref[half, 0:C, :]], axis=0)
    else:
        xs = ref[half, pl.ds(pl.multiple_of(r * C - HALO, HALO), C + HALO), :]
    return xs[HALO:, :], pltpu.roll(xs, 1, 0)[HALO:, :], pltpu.roll(xs, 2, 0)[HALO:, :]


def _conv_taps(taps, cw, cb):
    x0, x1, x2 = taps
    return cb + cw[0:1, :] * x2 + cw[1:2, :] * x1 + cw[2:3, :] * x0


def _fold8(x):
    acc = x[0:8, :]
    for i in range(1, x.shape[0] // 8):
        acc = acc + x[8 * i:8 * (i + 1), :]
    return acc


def _gate_fwd(up3, conv_w, conv_b, B, S):
    T = B * S
    W = FF_TILE
    C = GATE_ROWS

    def body(up_ref, cw_ref, cb_ref, act_ref, conv_ref):
        def chunk(r, first):
            gate = _conv_taps(_taps(up_ref, 0, r, first), cw_ref[0], cb_ref[0])
            val = _conv_taps(_taps(up_ref, 1, r, first), cw_ref[1], cb_ref[1])
            rows = pl.ds(0 if first else pl.multiple_of(r * C, C), C)
            conv_ref[0, rows, :] = gate.astype(conv_ref.dtype)
            conv_ref[1, rows, :] = val.astype(conv_ref.dtype)
            act_ref[rows, :] = (gate * _sigmoid(gate) * val).astype(act_ref.dtype)

        chunk(0, True)

        @pl.loop(1, S // C)
        def _(r):
            chunk(r, False)

    up_spec = pl.BlockSpec((2, S, W), lambda b, j: (0, b, j))
    return pl.pallas_call(
        body, name="gate_fwd",
        out_shape=(jax.ShapeDtypeStruct((T, D_FF), MXU_DTYPE), jax.ShapeDtypeStruct((2, T, D_FF), MXU_DTYPE)),
        grid=(B, N_FF_TILES),
        in_specs=[up_spec, pl.BlockSpec((2, 3, W), lambda b, j: (0, 0, j)), pl.BlockSpec((2, 1, W), lambda b, j: (0, 0, j))],
        out_specs=(pl.BlockSpec((S, W), lambda b, j: (b, j)), up_spec),
        compiler_params=_params(("parallel", "parallel"), 8 * _nbytes((S, W), F32)),
    )(up3, conv_w, conv_b)


def _gate_bwd(up3, conv3, dact, conv_w, B, S):
    T = B * S
    W = FF_TILE
    C = GATE_ROWS

    def body(up_ref, conv_ref, da_ref, cw_ref, dup_ref, gcw_ref, gcb_ref, d_s):
        @pl.when(pl.program_id(1) == 0)
        def _():
            gcw_ref[...] = jnp.zeros_like(gcw_ref)
            gcb_ref[...] = jnp.zeros_like(gcb_ref)

        @pl.loop(0, S // C)
        def _(r):
            rows = pl.ds(pl.multiple_of(r * C, C), C)
            gate, val = conv_ref[0, rows, :].astype(F32), conv_ref[1, rows, :].astype(F32)
            sg = _sigmoid(gate)
            da = da_ref[rows, :]
            d_s[0, rows, :] = da * val * (sg * (1.0 + gate * (1.0 - sg)))
            d_s[1, rows, :] = da * (gate * sg)

        d_s[:, S:S + HALO, :] = jnp.zeros((2, HALO, W), F32)

        def chunk(r, sums):
            base = pl.multiple_of(r * C, C)
            out = []
            for half in (0, 1):
                ds_ = d_s[half, pl.ds(base, C + HALO), :]
                d0, d1, d2 = ds_[:C, :], pltpu.roll(ds_, C + HALO - 1, 0)[:C, :], pltpu.roll(ds_, C + HALO - 2, 0)[:C, :]
                cw = cw_ref[half]
                dup_ref[half, pl.ds(base, C), :] = (cw[2:3, :] * d0 + cw[1:2, :] * d1 + cw[0:1, :] * d2).astype(dup_ref.dtype)
                x = up_ref[half, pl.ds(base, C), :]
                sb, s0, s1, s2 = sums[half]
                out.append((sb + _fold8(d0), s0 + _fold8(d2 * x), s1 + _fold8(d1 * x), s2 + _fold8(d0 * x)))
            return tuple(out)

        zeros = tuple(tuple(jnp.zeros((8, W), F32) for _ in range(4)) for _ in range(2))
        sums = lax.fori_loop(0, S // C, chunk, zeros)
        for half in (0, 1):
            sb, s0, s1, s2 = sums[half]
            gcb_ref[half] += jnp.sum(sb, axis=0, keepdims=True)
            gcw_ref[half, 0:1, :] += jnp.sum(s0, axis=0, keepdims=True)
            gcw_ref[half, 1:2, :] += jnp.sum(s1, axis=0, keepdims=True)
            gcw_ref[half, 2:3, :] += jnp.sum(s2, axis=0, keepdims=True)

    up_spec = pl.BlockSpec((2, S, W), lambda j, b: (0, b, j))
    cw_spec = pl.BlockSpec((2, 3, W), lambda j, b: (0, 0, j))
    cb_spec = pl.BlockSpec((2, 1, W), lambda j, b: (0, 0, j))
    return pl.pallas_call(
        body, name="gate_bwd",
        out_shape=(jax.ShapeDtypeStruct((2, T, D_FF), MXU_DTYPE), jax.ShapeDtypeStruct((2, 3, D_FF), F32),
                   jax.ShapeDtypeStruct((2, 1, D_FF), F32)),
        grid=(N_FF_TILES, B),
        in_specs=[up_spec, up_spec, pl.BlockSpec((S, W), lambda j, b: (b, j)), cw_spec],
        out_specs=(up_spec, cw_spec, cb_spec),
        scratch_shapes=[pltpu.VMEM((2, S + HALO, W), F32)],
        compiler_params=_params(("parallel", "arbitrary"), 12 * _nbytes((S, W), F32)),
    )(up3, conv3, dact, conv_w)


def _final(x2, tgt, g, tr=512):
    T, D = x2.shape

    def body(x_ref, t_ref, g_ref, dx_ref, loss_ref, gg_ref):
        @pl.when(pl.program_id(0) == 0)
        def _():
            loss_ref[...] = jnp.zeros_like(loss_ref)
            gg_ref[...] = jnp.zeros_like(gg_ref)

        xv = x_ref[...]
        gv = g_ref[...]
        r = lax.rsqrt(jnp.mean(xv * xv, axis=-1, keepdims=True) + EPS)
        xn = xv * r
        err = xn * gv - t_ref[...]
        loss_ref[...] += 0.5 * jnp.sum(jnp.mean(err * err, axis=-1, keepdims=True), axis=0, keepdims=True)
        dy = err * (1.0 / D)
        gg_ref[...] += jnp.sum(dy * xn, axis=0, keepdims=True)
        dxn = dy * gv
        dx_ref[...] = r * (dxn - xn * jnp.mean(dxn * xn, axis=-1, keepdims=True))

    row = pl.BlockSpec((tr, D), lambda i: (i, 0))
    vec = pl.BlockSpec((1, D), lambda i: (0, 0))
    return pl.pallas_call(
        body, name="final_loss",
        out_shape=(jax.ShapeDtypeStruct((T, D), F32), jax.ShapeDtypeStruct((1, 128), F32), jax.ShapeDtypeStruct((1, D), F32)),
        grid=(T // tr,), in_specs=[row, row, vec],
        out_specs=(row, pl.BlockSpec((1, 128), lambda i: (0, 0)), vec),
        compiler_params=_params(("arbitrary",), 6 * _nbytes((tr, D), F32)),
    )(x2, tgt, g)


def _sum_slabs(parts, name, tr):
    rows, cols = parts[0].shape
    n = len(parts)

    def body(*refs):
        acc = refs[0][...]
        for r in refs[1:n]:
            acc = acc + r[...]
        refs[n][...] = acc

    blk = pl.BlockSpec((tr, cols), lambda i: (i, 0))
    return pl.pallas_call(
        body, name=name, out_shape=jax.ShapeDtypeStruct((rows, cols), F32), grid=(rows // tr,),
        in_specs=[blk] * n, out_specs=blk,
        compiler_params=_params(("parallel",), (n + 1) * _nbytes((tr, cols), F32)),
    )(*parts)


ADAMW_BLOCK_BYTES = 2400 * 1024


def _adamw(w, g, m, v, name, copy_grad=False):
    lead = w.ndim == 3
    rows, cols = w.shape[-2:]
    fits = [d for d in range(8, rows + 1, 8) if rows % d == 0 and d * cols * 4 <= ADAMW_BLOCK_BYTES]
    tr = max(fits) if fits else rows
    c1 = 1.0 - ADAM_B1 ** ADAM_STEP
    c2 = 1.0 - ADAM_B2 ** ADAM_STEP

    def body(w_ref, g_ref, m_ref, v_ref, d_ref, nm_ref, nv_ref, *g_out):
        gv = g_ref[...]
        nm = ADAM_B1 * m_ref[...] + (1.0 - ADAM_B1) * gv
        nv = ADAM_B2 * v_ref[...] + (1.0 - ADAM_B2) * (gv * gv)
        nm_ref[...] = nm
        nv_ref[...] = nv
        d_ref[...] = -ADAM_LR * ((nm / c1) / (jnp.sqrt(nv / c2) + ADAM_EPS) + ADAM_WD * w_ref[...])
        if copy_grad:
            g_out[0][...] = gv

    blk = pl.BlockSpec((None, tr, cols), lambda i: (0, i, 0)) if lead else pl.BlockSpec((tr, cols), lambda i: (i, 0))
    sds = jax.ShapeDtypeStruct(w.shape, F32)
    n_out = 4 if copy_grad else 3
    return pl.pallas_call(
        body, name=name, out_shape=(sds,) * n_out, grid=(rows // tr,), in_specs=[blk] * 4, out_specs=(blk,) * n_out,
        compiler_params=_params(("parallel",), (4 + n_out) * _nbytes((tr, cols), F32)),
    )(w, g, m, v)


_ANY = pl.BlockSpec(memory_space=pl.ANY)


def _place():
    x, y, c = lax.axis_index("x"), lax.axis_index("y"), lax.axis_index("c")
    chips = [(1 - x, y), (x, 1 - y), (1 - x, 1 - y)]
    return x, y, c, chips


def _forward_halves(lands):
    n = len(lands)

    def body(*refs):
        outs, send, recv = refs[n:2 * n], refs[2 * n], refs[2 * n + 1]
        x, y, c, chips = _place()
        cps = []
        for w in range(n):
            for j, (px, py) in enumerate(chips):
                landed = outs[w].at[2 * px + py, c]
                cps.append(pltpu.make_async_remote_copy(
                    src_ref=landed, dst_ref=landed, send_sem=send.at[3 * w + j], recv_sem=recv.at[3 * w + j],
                    device_id=(x, y, 1 - c), device_id_type=MESH))
        for cp in cps:
            cp.start()
        for w in range(n):
            for j, (px, py) in enumerate(chips):
                other = outs[w].at[2 * px + py, 1 - c]
                pltpu.make_async_remote_copy(src_ref=other, dst_ref=other, send_sem=send.at[3 * w + j],
                                             recv_sem=recv.at[3 * w + j], device_id=(x, y, 1 - c),
                                             device_id_type=MESH).wait_recv()
        for cp in cps:
            cp.wait_send()

    dma = lambda k: pltpu.SemaphoreType.DMA((k,))
    return pl.pallas_call(
        body, name="gather_forward_halves", out_shape=tuple(jax.ShapeDtypeStruct(a.shape, a.dtype) for a in lands),
        in_specs=[_ANY] * n, out_specs=tuple([_ANY] * n), input_output_aliases={w: w for w in range(n)},
        scratch_shapes=[dma(3 * n), dma(3 * n)],
    )(*lands)


_HBM = pl.BlockSpec(memory_space=pltpu.HBM)
_SEM = pl.BlockSpec(memory_space=pltpu.SEMAPHORE)
_EFFECT = pltpu.SideEffectType.DATAFLOW_SIDE_EFFECTING


SEMS_PER_ARRAY = 8


def _exchange_copies(srcs, lands, send, recv, mode):
    x, y, c, chips = _place()
    if mode == "halves":
        cps = []
        for w, (src, land) in enumerate(zip(srcs, lands)):
            pieces = [(src.at[c], land.at[2 * x + y, c], (px, py, c)) for px, py in chips]
            pieces.append((src, land.at[2 * x + y], (x, y, 1 - c)))
            for k, (piece, dst, peer) in enumerate(pieces):
                cps.append(pltpu.make_async_remote_copy(
                    src_ref=piece, dst_ref=dst, send_sem=send.at[SEMS_PER_ARRAY * w + k],
                    recv_sem=recv.at[SEMS_PER_ARRAY * w + k], device_id=peer, device_id_type=MESH))
        return cps
    if mode == "swap":
        return [pltpu.make_async_remote_copy(
            src_ref=src.at[:, 1 - c], dst_ref=land, send_sem=send.at[SEMS_PER_ARRAY * w],
            recv_sem=recv.at[SEMS_PER_ARRAY * w], device_id=(x, y, 1 - c), device_id_type=MESH)
            for w, (src, land) in enumerate(zip(srcs, lands))]
    if mode == "all":
        flips = [(fx, fy, fc) for fx in (0, 1) for fy in (0, 1) for fc in (0, 1)][1:]
        peers = [(x ^ fx, y ^ fy, c ^ fc) for fx, fy, fc in flips]
        slot = 4 * x + 2 * y + c
    else:
        peers = [(px, py, c) for px, py in chips] + ([(x, y, 1 - c)] if mode == "gather" else [])
        slot = 2 * x + y
    cps = []
    for w, (src, land) in enumerate(zip(srcs, lands)):
        for k, peer in enumerate(peers):
            piece = src.at[2 * peer[0] + peer[1]] if mode == "scatter" else src
            cps.append(pltpu.make_async_remote_copy(
                src_ref=piece, dst_ref=land.at[slot], send_sem=send.at[SEMS_PER_ARRAY * w + k],
                recv_sem=recv.at[SEMS_PER_ARRAY * w + k], device_id=peer, device_id_type=MESH))
    return cps


def _exchange_start(srcs, name, mode, after):
    n = len(srcs)
    if mode == "swap":
        land_shapes = [(s.shape[0],) + s.shape[2:] for s in srcs]
    else:
        lead = {"gather": (N_CHIPS,), "halves": (N_CHIPS,), "scatter": (), "all": (2 * N_CHIPS,)}[mode]
        land_shapes = [lead + s.shape for s in srcs]

    def body(*refs):
        src_refs, land_refs = refs[:n], refs[n:2 * n]
        send, recv = refs[2 * n + 1], refs[2 * n + 2]
        token = refs[-1]
        for cp in _exchange_copies(src_refs, land_refs, send, recv, mode):
            cp.start()
        token[...] = jnp.zeros_like(token)

    sems = pltpu.SemaphoreType.DMA((SEMS_PER_ARRAY * n,))
    out = pl.pallas_call(
        body, name=name,
        out_shape=(sems, sems, *[pltpu.HBM(s.shape, s.dtype) for s in srcs],
                   *[pltpu.HBM(shp, s.dtype) for shp, s in zip(land_shapes, srcs)], jax.ShapeDtypeStruct((8, 128), F32)),
        in_specs=[_HBM] * (2 * n) + [_ANY],
        out_specs=(_SEM, _SEM, *[_HBM] * (2 * n), pl.BlockSpec(memory_space=pltpu.VMEM)),
        input_output_aliases={i: 2 + i for i in range(2 * n)},
        compiler_params=pltpu.CompilerParams(has_side_effects=_EFFECT),
    )(*[pltpu.with_memory_space_constraint(s, pltpu.HBM) for s in srcs],
      *[pltpu.with_memory_space_constraint(lax.empty(shp, s.dtype), pltpu.HBM) for shp, s in zip(land_shapes, srcs)],
      after)
    return out[0], out[1], out[2:2 + n], out[2 + n:2 + 2 * n], out[-1]


def _exchange_wait(started, name, mode, after):
    send, recv, src_thru, land_thru, _ = started
    n = len(src_thru)
    after = list(after) if isinstance(after, (list, tuple)) else [after]

    def body(*refs):
        src_refs, land_refs, send_ref, recv_ref = refs[:n], refs[n:2 * n], refs[2 * n], refs[2 * n + 1]
        for cp in _exchange_copies(src_refs, land_refs, send_ref, recv_ref, mode):
            cp.wait_send()
            cp.wait_recv()

    out = pl.pallas_call(
        body, name=name,
        out_shape=tuple(pltpu.HBM(a.shape, a.dtype) for a in list(src_thru) + list(land_thru)),
        in_specs=[_HBM] * (2 * n) + [_SEM, _SEM] + [_ANY] * len(after), out_specs=tuple([_HBM] * (2 * n)),
        input_output_aliases={i: i for i in range(2 * n)},
        compiler_params=pltpu.CompilerParams(has_side_effects=_EFFECT),
    )(*src_thru, *land_thru, send, recv, *after)
    return out[:n], out[n:]


def _swap_halves(gs, name):
    n = len(gs)

    def body(*refs):
        ins, outs, send, recv = refs[:n], refs[n:2 * n], refs[2 * n], refs[2 * n + 1]
        x, y, c, _ = _place()
        cps = []
        for w in range(n):
            cps.append(pltpu.make_async_remote_copy(
                src_ref=ins[w].at[:, 1 - c], dst_ref=outs[w], send_sem=send.at[w], recv_sem=recv.at[w],
                device_id=(x, y, 1 - c), device_id_type=MESH))
        for cp in cps:
            cp.start()
        for cp in cps:
            cp.wait()

    return pl.pallas_call(
        body, name=name,
        out_shape=tuple(jax.ShapeDtypeStruct((g.shape[0],) + g.shape[2:], g.dtype) for g in gs),
        in_specs=[_ANY] * n, out_specs=tuple([_ANY] * n),
        scratch_shapes=[pltpu.SemaphoreType.DMA((n,)), pltpu.SemaphoreType.DMA((n,))],
    )(*gs)


GRAD_PAYLOAD = jnp.bfloat16


def _half_blocks(half_rows, cols):
    if (half_rows // 2) % 16 == 0:
        return (half_rows // 2, cols), (lambda r: (r, 0))
    assert cols % 256 == 0, (half_rows, cols)
    return (half_rows, cols // 2), (lambda r: (0, r))


def _pair_sum(gs, gots, name):
    n = len(gs)
    core = lax.axis_index("c").astype(jnp.int32).reshape(1)

    def body(core_ref, *refs):
        del core_ref
        for w in range(n):
            refs[2 * n + w][...] = (refs[w][...] + refs[n + w][...]).astype(GRAD_PAYLOAD)

    in_specs, out_specs, out_shape, nbytes = [], [], [], 0
    cuts = [_half_blocks(g.shape[1] // 2, g.shape[2]) for g in gs]
    for g, ((br, bc), at) in zip(gs, cuts):
        per_half = (g.shape[1] // 2) // br
        in_specs.append(pl.BlockSpec((1, br, bc), lambda s, r, core, at=at, per_half=per_half:
                                     (s, per_half * core[0] + at(r)[0], at(r)[1])))
        nbytes += 3 * _nbytes((br, bc), F32)
    for g, ((br, bc), at) in zip(gs, cuts):
        in_specs.append(pl.BlockSpec((1, br, bc), lambda s, r, core, at=at: (s,) + at(r)))
        out_specs.append(pl.BlockSpec((1, br, bc), lambda s, r, core, at=at: (s,) + at(r)))
        out_shape.append(jax.ShapeDtypeStruct((g.shape[0], g.shape[1] // 2, g.shape[2]), GRAD_PAYLOAD))
    return pl.pallas_call(
        body, name=name, out_shape=tuple(out_shape),
        grid_spec=pltpu.PrefetchScalarGridSpec(num_scalar_prefetch=1, grid=(N_CHIPS, 2), in_specs=in_specs,
                                               out_specs=tuple(out_specs)),
        compiler_params=_params(("parallel", "parallel"), nbytes),
    )(core, *gs, *gots)


def _chip_sum(ps, landed):
    n = len(ps)
    x, y, c = lax.axis_index("x"), lax.axis_index("y"), lax.axis_index("c")
    where = jnp.stack([2 * x + y, 2 * (1 - x) + y, 2 * x + (1 - y), 2 * (1 - x) + (1 - y), c]).astype(jnp.int32)

    def body(where_ref, *refs):
        del where_ref
        for w in range(n):
            terms = [refs[4 * w + t][...].astype(F32) for t in range(4)]
            refs[4 * n + w][...] = ((terms[0] + terms[1]) + terms[2]) + terms[3]

    in_specs, out_specs, out_shape, args, nbytes = [], [], [], [], 0
    for p, a in zip(ps, landed):
        (br, bc), at = _half_blocks(a.shape[1], a.shape[2])
        blk = (1, br, bc)
        in_specs.append(pl.BlockSpec(blk, lambda r, where, at=at: (where[0],) + at(r)))
        args.append(p)
        for t in (1, 2, 3):
            in_specs.append(pl.BlockSpec(blk, lambda r, where, t=t, at=at: (where[t],) + at(r)))
            args.append(a)
        out_specs.append(pl.BlockSpec(blk, lambda r, where, at=at: (where[4],) + at(r)))
        out_shape.append(jax.ShapeDtypeStruct((2,) + a.shape[1:], F32))
        nbytes += 4 * _nbytes(blk, F32)
    return pl.pallas_call(
        body, name="grad_chip_sum", out_shape=tuple(out_shape),
        grid_spec=pltpu.PrefetchScalarGridSpec(num_scalar_prefetch=1, grid=(2,), in_specs=in_specs,
                                               out_specs=tuple(out_specs)),
        compiler_params=_params(("parallel",), nbytes),
    )(where, *args)


def _join_halves(ss):
    n = len(ss)

    def body(*refs):
        outs, send, recv = refs[n:2 * n], refs[2 * n], refs[2 * n + 1]
        x, y, c, _ = _place()
        cps = []
        for w in range(n):
            cps.append(pltpu.make_async_remote_copy(
                src_ref=outs[w].at[c], dst_ref=outs[w].at[c], send_sem=send.at[w], recv_sem=recv.at[w],
                device_id=(x, y, 1 - c), device_id_type=MESH))
        for cp in cps:
            cp.start()
        for w in range(n):
            got = outs[w].at[1 - c]
            pltpu.make_async_remote_copy(src_ref=got, dst_ref=got, send_sem=send.at[w], recv_sem=recv.at[w],
                                         device_id=(x, y, 1 - c), device_id_type=MESH).wait_recv()
        for cp in cps:
            cp.wait_send()

    dma = lambda k: pltpu.SemaphoreType.DMA((k,))
    return pl.pallas_call(
        body, name="grad_join_halves",
        out_shape=tuple(jax.ShapeDtypeStruct(s.shape, s.dtype) for s in ss),
        in_specs=[_ANY] * n, out_specs=tuple([_ANY] * n), input_output_aliases={w: w for w in range(n)},
        scratch_shapes=[dma(n), dma(n)],
    )(*ss)


def _rot_cols(w, axis=-1):
    a, b = jnp.split(w, 2, axis=axis)
    return jnp.concatenate([-b, a], axis=axis)


def _rot_cols_t(g, axis=-1):
    a, b = jnp.split(g, 2, axis=axis)
    return jnp.concatenate([b, -a], axis=axis)


def _cols_from_chips(a):
    n, r, cs = a.shape
    return jnp.transpose(a, (1, 0, 2)).reshape(r, n * cs)


def _cols_to_chips(a):
    r, cc = a.shape
    return jnp.transpose(a.reshape(r, N_CHIPS, cc // N_CHIPS), (1, 0, 2))


def _conv_w_split(cw):
    return jnp.swapaxes(cw.reshape(3, 2, D_FF), 0, 1)


def _conv_w_join(g):
    return jnp.swapaxes(g, 0, 1).reshape(3, 2 * D_FF)


_SEG =(D_MODEL, 2 * D_MODEL, 2 * D_MODEL + Q_RANK, 2 * D_MODEL + Q_RANK + KV_RANK, 2 * D_MODEL + Q_RANK + KV_RANK + ROPE,
        3 * D_MODEL + Q_RANK + KV_RANK + ROPE)


def _w_in_t_to_pad(wt):
    u, v, cq, ckv, kr, ga, gb = jnp.split(wt, _SEG, axis=0)
    return jnp.concatenate([u, v, ga, gb, cq, ckv, kr, _rot_cols(kr, axis=0)], axis=0)


def _w_in_t_from_pad(gt):
    u, v, ga, gb, cq, ckv, kr, krr = jnp.split(
        gt, (D_MODEL, 2 * D_MODEL, 3 * D_MODEL, 4 * D_MODEL, 4 * D_MODEL + Q_RANK, 4 * D_MODEL + Q_RANK + KV_RANK,
             4 * D_MODEL + Q_RANK + KV_RANK + ROPE), axis=0)
    return jnp.concatenate([u, v, cq, ckv, kr + _rot_cols_t(krr, axis=0), ga, gb], axis=0)


def _w_uq_to_pad(w):
    t = w.reshape(Q_RANK, HEADS, QK_DIM)
    nope, rope = t[..., :NOPE], t[..., NOPE:]
    return jnp.concatenate([nope, rope, _rot_cols(rope)], axis=-1).reshape(Q_RANK, HEADS * HEAD_PAD)


def _w_uq_from_pad(g):
    t = g.reshape(Q_RANK, HEADS, HEAD_PAD)
    nope, rope, rot = t[..., :NOPE], t[..., NOPE:QK_DIM], t[..., QK_DIM:]
    return jnp.concatenate([nope, rope + _rot_cols_t(rot)], axis=-1).reshape(Q_RANK, HEADS * QK_DIM)


def _w_ukv_to_pad(w):
    t = w.reshape(KV_RANK, HEADS, 2, NOPE)
    return jnp.swapaxes(t, 1, 2).reshape(KV_RANK, 2 * HEADS * NOPE)


def _w_ukv_from_pad(g):
    t = g.reshape(KV_RANK, 2, HEADS, NOPE)
    return jnp.swapaxes(t, 1, 2).reshape(KV_RANK, 2 * HEADS * NOPE)


def _rope_tables(positions):
    inv_freq = 1.0 / (ROPE_THETA ** (jnp.arange(0, ROPE, 2, dtype=F32) / ROPE))
    ang = positions.astype(F32).reshape(-1, 1) * inv_freq
    cos, sin = jnp.cos(ang), jnp.sin(ang)
    zero = jnp.zeros((ang.shape[0], 64), F32)
    return jnp.concatenate([cos, cos, zero], axis=1), jnp.concatenate([sin, sin, zero], axis=1)


_BIG = ("w_in", "w_uq", "w_ukv", "w_out", "w_up", "w_down")
UP_SHARD = 2 * D_FF // N_CHIPS
TOKEN_TILE = 1024


def _local_step(x, positions, tgt, wts, in_weights, mixer_weights, ffn_weights, on_ffn_grads, on_mixer_grads):
    B, S, D = x.shape
    T = B * S
    xf = x.reshape(T, D)
    cos_a, sin_a = _rope_tables(positions)
    bs_t = jnp.pad(wts["a_spatial_b"].T, ((0, 0), (0, 128 - A_GROUPS)))

    h = _rms_fwd(xf, wts["mix_norm"], "norm1_fwd")
    wts = dict(wts)
    wts["w_in"], token = in_weights([h, cos_a, sin_a])
    tm = min(TOKEN_TILE, T)
    z = _mm(h, wts["w_in"], "nt", "in_proj", tm=tm, tn=1536, tk=D, n_outer=True, after=token)
    wts["w_q"], wts["w_kv"], wts["w_out"] = mixer_weights(z)
    q, k, v, cqn, ckvn = _lat_fwd(z, wts["q_a_norm"], wts["kv_a_norm"], wts["w_q"], wts["w_kv"], cos_a, sin_a)
    yb, *lses = _attn_fwd(q, k, v, B, S)
    merged = _mix_fwd(z, yb, wts["a_v_norm_g"], wts["a_v_norm_b"], wts["a_spatial_w"], bs_t)
    x1 = _mm(merged, wts["w_out"], "nn", "out_proj", tm=min(512, T), tn=D, tk=D, add=xf)
    h2 = _rms_fwd(x1, wts["ffn_norm"], "norm2_fwd")
    wts["w_up"], wts["w_down"], wts["conv_w"] = ffn_weights(h2)
    up_pre = _mm(h2, wts["w_up"], "nn", "up_proj", tm=tm, tn=UP_SHARD, tk=D, dims=(T, 2 * D_FF, D),
                 b_spec=pl.BlockSpec((None, D, UP_SHARD), lambda i, j, k: (j, 0, 0)),
                 o_spec=pl.BlockSpec((None, tm, UP_SHARD), lambda i, j, k: (j // 2, i, j % 2)), out_shape=(2, T, D_FF),
                 n_outer=True)
    act, up_conv = _gate_fwd(up_pre, wts["conv_w"], wts["conv_b"], B, S)
    x2 = _mm(act, wts["w_down"], "nn", "down_proj", tm=tm, tn=D, tk=1408, add=x1)
    dx2, loss_row, g_final = _final(x2, tgt.reshape(T, D), wts["final_norm"])

    g = {"final_norm": g_final}
    dact = _mm(dx2, wts["w_down"], "nt", "down_proj_dx", tm=tm, tn=1408, tk=D, n_outer=True)
    tk2, tk1 = min(2048, T), min(1024, T)
    g["w_down"], g["w_down_lo"] = _mm(act, dx2, "tn", "down_proj_dw", tm=1408, tn=D, tk=tk1, copy_dtype=GRAD_PAYLOAD)
    dup, g["conv_w"], g["conv_b"] = _gate_bwd(up_pre, up_conv, dact, wts["conv_w"], B, S)
    g["w_up"], g["w_up_lo"] = _mm(
        h2, dup, "tn", "up_proj_dw", tm=D, tn=UP_SHARD, tk=tk2, dims=(D, 2 * D_FF, T), copy_dtype=GRAD_PAYLOAD,
        b_spec=pl.BlockSpec((None, tk2, UP_SHARD), lambda i, j, k: (j // 2, k, j % 2)),
        o_spec=pl.BlockSpec((None, D, UP_SHARD), lambda i, j, k: (j, 0, 0)), out_shape=(N_CHIPS, D, UP_SHARD))
    token, ffn_sent = on_ffn_grads(g)
    dh2 = _mm(dup, wts["w_up"], "nt", "up_proj_dx", tm=tm, tn=D, tk=UP_SHARD, dims=(T, D, 2 * D_FF), after=token,
              a_spec=pl.BlockSpec((None, tm, UP_SHARD), lambda i, j, k: (k // 2, i, k % 2)),
              b_spec=pl.BlockSpec((None, D, UP_SHARD), lambda i, j, k: (k, 0, 0)))
    token = ffn_sent(dh2)
    dx1, g["ffn_norm"] = _rms_bwd(x1, wts["ffn_norm"], dh2, dx2, "norm2_bwd")
    dm = _mm(dx1, wts["w_out"], "nt", "out_proj_dx", tm=min(512, T), tn=D, tk=D, after=token)
    g["w_out"], g["w_out_lo"] = _mm(merged, dx1, "tn", "out_proj_dw", tm=D, tn=D, tk=tk1, copy_dtype=GRAD_PAYLOAD)
    dz, dyb, dl, g["a_spatial_w"], gbs, g["a_v_norm_g"], g["a_v_norm_b"] = _mix_bwd(
        z, yb, dm, wts["a_v_norm_g"], wts["a_v_norm_b"], wts["a_spatial_w"], bs_t)
    g["a_spatial_b"] = gbs[:, :A_GROUPS].T
    delta = dl.reshape(HEADS * T // ATT_BLOCK, 1, ATT_BLOCK)
    dq, dk, dv = _attn_bwd(q, k, v, dyb, lses, delta, B, S)
    dz, dq_raw, dkv, g["q_a_norm"], g["kv_a_norm"] = _lat_bwd(
        dz, z, dq, dk, dv, wts["q_a_norm"], wts["kv_a_norm"], wts["w_q"], wts["w_kv"], cos_a, sin_a)
    g["w_q"] = _mm(cqn, dq_raw, "tn", "q_proj_dw", tm=Q_RANK, tn=HEADS * HEAD_PAD, tk=tk2)
    g["w_kv"] = _mm(ckvn, dkv, "tn", "kv_proj_dw", tm=KV_RANK, tn=2 * HEADS * NOPE, tk=tk2)
    g["w_in"] = _mm(dz, h, "tn", "in_proj_dw", tm=1536, tn=D, tk=tk2)
    token = on_mixer_grads(g)
    dh = _mm(dz, wts["w_in"], "nn", "in_proj_dx", tm=tm, tn=D, tk=1536, after=token)
    dx, g["mix_norm"] = _rms_bwd(xf, wts["mix_norm"], dh, dx1, "norm1_bwd")
    return loss_row[0, 0], dx.reshape(B, S, D), g


_SMALL = (("mix_norm", (1, D_MODEL)), ("a_v_norm_g", (1, D_MODEL)), ("a_v_norm_b", (1, D_MODEL)),
          ("a_spatial_w", (A_GROUPS * CHUNK, CHUNK)), ("a_spatial_b", (1, A_GROUPS * CHUNK)), ("q_a_norm", (1, Q_RANK)),
          ("kv_a_norm", (1, KV_RANK)), ("ffn_norm", (1, D_MODEL)), ("conv_b", (1, 2 * D_FF)), ("final_norm", (1, D_MODEL)),
          ("conv_w", (3, 2 * D_FF)))
_SMALL_SIZE = sum(math.prod(s) for _, s in _SMALL)
_SMALL_ROWS = -(-(_SMALL_SIZE + 1) // (128 * 8)) * 8


def kernel(x, positions, mix_norm, w_in, a_v_norm_g, a_v_norm_b, a_spatial_w, a_spatial_b, q_a_norm, w_uq, kv_a_norm, w_ukv, w_out, ffn_norm, w_up, conv_w, conv_b, w_down, final_norm, loss_target, m_mix_norm, m_w_in, m_a_v_norm_g, m_a_v_norm_b, m_a_spatial_w, m_a_spatial_b, m_q_a_norm, m_w_uq, m_kv_a_norm, m_w_ukv, m_w_out, m_ffn_norm, m_w_up, m_conv_w, m_conv_b, m_w_down, m_final_norm, v_mix_norm, v_w_in, v_a_v_norm_g, v_a_v_norm_b, v_a_spatial_w, v_a_spatial_b, v_q_a_norm, v_w_uq, v_kv_a_norm, v_w_ukv, v_w_out, v_ffn_norm, v_w_up, v_conv_w, v_conv_b, v_w_down, v_final_norm):
    weights = dict(mix_norm=mix_norm, w_in=w_in, a_v_norm_g=a_v_norm_g, a_v_norm_b=a_v_norm_b, a_spatial_w=a_spatial_w,
                   a_spatial_b=a_spatial_b, q_a_norm=q_a_norm, w_uq=w_uq, kv_a_norm=kv_a_norm, w_ukv=w_ukv, w_out=w_out,
                   ffn_norm=ffn_norm, w_up=w_up, conv_w=conv_w, conv_b=conv_b, w_down=w_down, final_norm=final_norm)
    m_in = dict(mix_norm=m_mix_norm, w_in=m_w_in, a_v_norm_g=m_a_v_norm_g, a_v_norm_b=m_a_v_norm_b,
                a_spatial_w=m_a_spatial_w, a_spatial_b=m_a_spatial_b, q_a_norm=m_q_a_norm, w_uq=m_w_uq,
                kv_a_norm=m_kv_a_norm, w_ukv=m_w_ukv, w_out=m_w_out, ffn_norm=m_ffn_norm, w_up=m_w_up, conv_w=m_conv_w,
                conv_b=m_conv_b, w_down=m_w_down, final_norm=m_final_norm)
    v_in = dict(mix_norm=v_mix_norm, w_in=v_w_in, a_v_norm_g=v_a_v_norm_g, a_v_norm_b=v_a_v_norm_b,
                a_spatial_w=v_a_spatial_w, a_spatial_b=v_a_spatial_b, q_a_norm=v_q_a_norm, w_uq=v_w_uq,
                kv_a_norm=v_kv_a_norm, w_ukv=v_w_ukv, w_out=v_w_out, ffn_norm=v_ffn_norm, w_up=v_w_up, conv_w=v_conv_w,
                conv_b=v_conv_b, w_down=v_w_down, final_norm=v_final_norm)
    names = list(weights)
    chip = 2 * lax.axis_index("x") + lax.axis_index("y")

    def halves(a):
        return a.reshape(a.shape[:-2] + (2, a.shape[-2] // 2, a.shape[-1]))

    w_in_t = jnp.swapaxes(w_in[0], 0, 1).astype(MXU_DTYPE)
    w_in_gather = _exchange_start([jnp.stack(jnp.split(w_in_t, 2, axis=1))], "w_in_gather_start", "halves",
                                  after=positions)
    gathers = {}
    wts = dict(
        mix_norm=mix_norm, a_v_norm_g=a_v_norm_g, a_v_norm_b=a_v_norm_b, a_spatial_w=a_spatial_w[0],
        a_spatial_b=a_spatial_b[0], q_a_norm=q_a_norm, kv_a_norm=kv_a_norm, ffn_norm=ffn_norm,
        final_norm=final_norm.reshape(1, D_MODEL), conv_b=conv_b.reshape(2, 1, D_FF))

    mixer_shards = [weights[n][0].astype(MXU_DTYPE) for n in _BIG[1:4]]
    ffn_shards = [w_up[0].astype(MXU_DTYPE), w_down[0].astype(MXU_DTYPE)]

    def in_weights(after):
        _, landed = _exchange_wait(w_in_gather, "w_in_gather_wait", "halves", list(after) + mixer_shards + ffn_shards)
        (w_in_sh,) = _forward_halves(list(landed))
        gathers["mixer"] = _exchange_start(mixer_shards, "mixer_gather_start", "gather", after=w_in_sh)
        gathers["ffn"] = _exchange_start(ffn_shards + [conv_w[0]], "ffn_gather_start", "gather",
                                         after=gathers["mixer"][4])
        w_in_pad = _w_in_t_to_pad(jnp.concatenate([w_in_sh[:, 0], w_in_sh[:, 1]], axis=-1).reshape(-1, D_MODEL))
        return w_in_pad, gathers["ffn"][4]

    def mixer_weights(after):
        _, (w_uq_sh, w_ukv_sh, w_out_sh) = _exchange_wait(gathers["mixer"], "mixer_gather_wait", "gather", after)
        return (_w_uq_to_pad(_cols_from_chips(w_uq_sh)), _w_ukv_to_pad(_cols_from_chips(w_ukv_sh)),
                w_out_sh.reshape(D_MODEL, D_MODEL))

    def ffn_weights(after):
        _, (w_up_sh, w_down_sh, cw_all) = _exchange_wait(gathers["ffn"], "ffn_gather_wait", "gather", after)
        return w_up_sh, w_down_sh.reshape(D_FF, D_MODEL), _conv_w_split(_cols_from_chips(cw_all))

    scatters = {}

    def start_scatter(slabs, slabs_lo, tag):
        got = _swap_halves([halves(s) for s in slabs_lo], tag + "_grad_swap_halves")
        sums = _pair_sum(slabs, got, tag + "_grad_pair_sum")
        scatters[tag] = _exchange_start(list(sums), tag + "_scatter_start", "scatter", after=slabs[-1])
        return scatters[tag][4]

    def on_ffn_grads(g):
        slabs, slabs_lo = [[g["w_up" + lo], g["w_down" + lo].reshape(N_CHIPS, D_FF // N_CHIPS, D_MODEL)]
                           for lo in ("", "_lo")]
        swap = _exchange_start([halves(s) for s in slabs_lo], "ffn_swap_start", "swap", after=slabs[1])

        def sent(after):
            _, got = _exchange_wait(swap, "ffn_swap_wait", "swap", after)
            sums = _pair_sum(slabs, got, "ffn_grad_pair_sum")
            scatters["ffn"] = _exchange_start(list(sums), "ffn_scatter_start", "scatter", after=got[0])
            return scatters["ffn"][4]

        return swap[4], sent

    def on_mixer_grads(g):
        slabs = [_w_in_t_from_pad(g["w_in"]).reshape(N_CHIPS, -1, D_MODEL), _cols_to_chips(_w_uq_from_pad(g["w_q"])),
                 _cols_to_chips(_w_ukv_from_pad(g["w_kv"]))]
        w_out_slabs = [g["w_out" + lo].reshape(N_CHIPS, D_MODEL // N_CHIPS, D_MODEL) for lo in ("", "_lo")]
        return start_scatter(slabs + w_out_slabs[:1], [s.astype(GRAD_PAYLOAD) for s in slabs] + w_out_slabs[1:], "mixer")

    loss_part, grad_x, g = _local_step(x, positions, loss_target, wts, in_weights, mixer_weights, ffn_weights,
                                       on_ffn_grads, on_mixer_grads)

    g_small_parts = dict(g)
    g_small_parts["conv_w"] = _conv_w_join(g["conv_w"])
    g_small_parts["conv_b"] = g["conv_b"].reshape(1, 2 * D_FF)
    flat = jnp.concatenate([g_small_parts[n].reshape(-1) for n, _ in _SMALL] + [loss_part.reshape(1)])
    flat = jnp.pad(flat, (0, _SMALL_ROWS * 128 - flat.shape[0])).reshape(_SMALL_ROWS, 128)
    small_gather = _exchange_start([flat], "small_gather_start", "all", after=grad_x)

    mixer_sums, mixer_landed = _exchange_wait(scatters["mixer"], "mixer_scatter_wait", "scatter", after=small_gather[4])
    ffn_sums, ffn_landed = _exchange_wait(scatters["ffn"], "ffn_scatter_wait", "scatter", after=mixer_landed[0])
    reduced = _chip_sum(list(mixer_sums) + list(ffn_sums), list(mixer_landed) + list(ffn_landed))
    g_big = dict(zip(_BIG, _join_halves(reduced)))

    grads, deltas, new_m, new_v = {}, {}, {}, {}

    def update(n, grad, copy_grad=False):
        w = weights[n]
        shape2 = grad.shape
        d, nm, nv, *again = _adamw(w.reshape(shape2), grad, m_in[n].reshape(shape2), v_in[n].reshape(shape2),
                                   "adamw_" + n, copy_grad)
        grads[n], deltas[n], new_m[n], new_v[n] = (t.reshape(w.shape) for t in (again[0] if copy_grad else grad, d, nm, nv))

    def update_transposed(n, grad_t):
        t = lambda a: jnp.swapaxes(a, 1, 2)
        d, nm, nv, again = _adamw(t(weights[n]), grad_t, t(m_in[n]), t(v_in[n]), "adamw_" + n, True)
        grads[n], deltas[n], new_m[n], new_v[n] = t(again), t(d), t(nm), t(nv)

    for n in _BIG:
        g3 = g_big[n].reshape((1, -1, g_big[n].shape[-1]))
        if n == "w_in":
            update_transposed(n, g3)
        else:
            update(n, g3, copy_grad=True)

    (own,), (everyone,) = _exchange_wait(small_gather, "small_gather_wait", "all", after=[deltas[n] for n in _BIG])
    device = 2 * chip + lax.axis_index("c")
    everyone = lax.dynamic_update_slice(everyone, own[None], (device, 0, 0))
    total = _sum_slabs([everyone[j] for j in range(8)], "small_grads_sum", tr=_SMALL_ROWS).reshape(-1)
    o = 0
    for n, shp in _SMALL:
        piece = total[o:o + math.prod(shp)].reshape(shp)
        o += math.prod(shp)
        if n == "conv_w":
            piece = lax.dynamic_slice_in_dim(piece, chip * UP_SHARD, UP_SHARD, axis=1)
        update(n, piece)
    loss = total[_SMALL_SIZE]
    return (loss, grad_x, *[grads[n] for n in names], *[deltas[n] for n in names], *[new_m[n] for n in names],
            *[new_v[n] for n in names])
```

```python
import functools
import math

import jax
import jax.numpy as jnp
from jax import lax
from jax.experimental import pallas as pl
from jax.experimental.pallas import tpu as pltpu

F32 = jnp.float32
MXU_DTYPE = jnp.bfloat16
MESH = pl.DeviceIdType.MESH

D_MODEL = 1024
EPS = 1e-6
A_GROUPS = 8
CHUNK = 128
HEADS = 8
NOPE = 128
ROPE = 64
QK_DIM = NOPE + ROPE
HEAD_PAD = 256
Q_RANK = 256
KV_RANK = 128
ROPE_THETA = 10000.0
D_FF = 2816
FF_TILE = 256
N_FF_TILES = D_FF // FF_TILE
LAT = 512
IN_PAD = 4 * D_MODEL + LAT
N_CHIPS = 4
ADAM_LR, ADAM_B1, ADAM_B2, ADAM_EPS, ADAM_WD, ADAM_STEP = 0.001, 0.9, 0.999, 1e-08, 0.01, 10

VMEM_CAP_V7X = 64 * 1024 * 1024
NEG = -1e30


def _params(sem, nbytes):
    limit = int(min(VMEM_CAP_V7X - (8 << 20), max(32 << 20, 3 * nbytes)))
    return pltpu.CompilerParams(dimension_semantics=sem, vmem_limit_bytes=limit)


def _nbytes(shape, dtype):
    return math.prod(shape) * jnp.dtype(dtype).itemsize


_DIMS = {"nn": (((1,), (0,)), ((), ())), "nt": (((1,), (1,)), ((), ())), "tn": (((0,), (0,)), ((), ()))}


def _mm(a, b, mode, name, *, tm, tn, tk, out_dtype=F32, add=None, dims=None, a_spec=None, b_spec=None,
        o_spec=None, out_shape=None, n_outer=False, copy_dtype=None, after=None):
    if dims is None:
        if mode == "nn":
            (M, K), (_, N) = a.shape, b.shape
        elif mode == "nt":
            (M, K), (N, _) = a.shape, b.shape
        else:
            (K, M), (_, N) = a.shape, b.shape
    else:
        M, N, K = dims
    a_blk = (tk, tm) if mode == "tn" else (tm, tk)
    b_blk = (tn, tk) if mode == "nt" else (tk, tn)
    if a_spec is None:
        a_spec = pl.BlockSpec(a_blk, (lambda i, j, k: (k, i)) if mode == "tn" else (lambda i, j, k: (i, k)))
    if b_spec is None:
        b_spec = pl.BlockSpec(b_blk, (lambda i, j, k: (j, k)) if mode == "nt" else (lambda i, j, k: (k, j)))
    if o_spec is None:
        o_spec = pl.BlockSpec((tm, tn), lambda i, j, k: (i, j))
    if out_shape is None:
        out_shape = (M, N)
    assert M % tm == 0 and N % tn == 0 and K % tk == 0, (name, M, N, K, tm, tn, tk)
    nk = K // tk
    contract = _DIMS[mode]
    has_add = add is not None

    def body(*refs):
        a_ref, b_ref = refs[0], refs[1]
        add_ref = refs[2] if has_add else None
        n_in = 2 + has_add + (after is not None)
        o_ref = refs[n_in]
        copy_ref = refs[n_in + 1] if copy_dtype is not None else None

        def product():
            return lax.dot_general(a_ref[...].astype(MXU_DTYPE), b_ref[...].astype(MXU_DTYPE), contract,
                                   preferred_element_type=F32)

        def finish(r):
            if has_add:
                r = r + add_ref[...]
            o_ref[...] = r.astype(out_dtype)
            if copy_ref is not None:
                copy_ref[...] = r.astype(copy_dtype)

        if nk == 1:
            finish(product())
            return
        acc = refs[-1]
        k = pl.program_id(2)

        @pl.when(k == 0)
        def _():
            acc[...] = jnp.zeros_like(acc)

        acc[...] += product()

        @pl.when(k == nk - 1)
        def _():
            finish(acc[...])

    in_specs = [a_spec, b_spec]
    args = [a, b]
    nbytes = _nbytes(a_blk, a.dtype) + _nbytes(b_blk, b.dtype) + 3 * _nbytes((tm, tn), F32)
    if has_add:
        in_specs.append(pl.BlockSpec((tm, tn), lambda i, j, k: (i, j)))
        args.append(add)
        nbytes += _nbytes((tm, tn), F32)
    if after is not None:
        in_specs.append(pl.BlockSpec(after.shape, lambda i, j, k: (0, 0)))
        args.append(after)
    grid = (M // tm, N // tn, nk)
    if n_outer:
        def swapped(spec):
            return pl.BlockSpec(spec.block_shape, lambda j, i, k, at=spec.index_map: at(i, j, k))

        grid = (N // tn, M // tm, nk)
        in_specs = [swapped(s) for s in in_specs]
        o_spec = swapped(o_spec)
    out_sds, out_specs = jax.ShapeDtypeStruct(out_shape, out_dtype), o_spec
    if copy_dtype is not None:
        out_sds, out_specs = (out_sds, jax.ShapeDtypeStruct(out_shape, copy_dtype)), (o_spec, o_spec)
    return pl.pallas_call(
        body, name=name, out_shape=out_sds, grid=grid, in_specs=in_specs, out_specs=out_specs,
        scratch_shapes=[pltpu.VMEM((tm, tn), F32)] if nk > 1 else [],
        compiler_params=_params(("parallel", "parallel", "arbitrary"), nbytes),
    )(*args)


_GELU_C = math.sqrt(2.0 / math.pi)
_GELU_A = 0.044715


def _sigmoid(x):
    return 0.5 * jnp.tanh(0.5 * x) + 0.5


def _gelu(x):
    t = jnp.tanh(x * (_GELU_C + (_GELU_C * _GELU_A) * (x * x)))
    return x * (0.5 + 0.5 * t)


def _gelu_and_grad(x):
    x2 = x * x
    t = jnp.tanh(x * (_GELU_C + (_GELU_C * _GELU_A) * x2))
    cdf = 0.5 + 0.5 * t
    grad = cdf + (0.5 * x) * (1.0 - t * t) * (_GELU_C + (3.0 * _GELU_C * _GELU_A) * x2)
    return x * cdf, grad


def _rope_mix(g, cos_a, sin_a):
    return g * cos_a + pltpu.roll(g, 64, 1) * sin_a


def _rope_mix_bwd(d, cos_a, sin_a):
    return d * cos_a + pltpu.roll(d * sin_a, 64, 1)


def _rms_fwd(x, g, name, tr=1024):
    T, D = x.shape

    def body(x_ref, g_ref, h_ref):
        xv = x_ref[...]
        r = lax.rsqrt(jnp.mean(xv * xv, axis=-1, keepdims=True) + EPS)
        h_ref[...] = ((xv * r) * g_ref[...]).astype(h_ref.dtype)

    return pl.pallas_call(
        body, name=name, out_shape=jax.ShapeDtypeStruct((T, D), MXU_DTYPE), grid=(T // tr,),
        in_specs=[pl.BlockSpec((tr, D), lambda i: (i, 0)), pl.BlockSpec((1, D), lambda i: (0, 0))],
        out_specs=pl.BlockSpec((tr, D), lambda i: (i, 0)),
        compiler_params=_params(("parallel",), 3 * _nbytes((tr, D), F32)),
    )(x, g)


def _rms_bwd(x, g, dh, dres, name, tr=512):
    T, D = x.shape

    def body(x_ref, g_ref, dh_ref, dres_ref, dx_ref, gg_ref):
        @pl.when(pl.program_id(0) == 0)
        def _():
            gg_ref[...] = jnp.zeros_like(gg_ref)

        xv = x_ref[...]
        r = lax.rsqrt(jnp.mean(xv * xv, axis=-1, keepdims=True) + EPS)
        xn = xv * r
        dhv = dh_ref[...]
        dxn = dhv * g_ref[...]
        dx_ref[...] = dres_ref[...] + r * (dxn - xn * jnp.mean(dxn * xn, axis=-1, keepdims=True))
        gg_ref[...] += jnp.sum(dhv * xn, axis=0, keepdims=True)

    row = pl.BlockSpec((tr, D), lambda i: (i, 0))
    vec = pl.BlockSpec((1, D), lambda i: (0, 0))
    return pl.pallas_call(
        body, name=name,
        out_shape=(jax.ShapeDtypeStruct((T, D), F32), jax.ShapeDtypeStruct((1, D), F32)),
        grid=(T // tr,), in_specs=[row, vec, row, row], out_specs=(row, vec),
        compiler_params=_params(("arbitrary",), 6 * _nbytes((tr, D), F32)),
    )(x, g, dh, dres)


def _lat_fwd(z, gq, gkv, wq, wkv, cos_a, sin_a, tr=512):
    T = z.shape[0]
    lat_blk = (4 * D_MODEL) // LAT

    def body(z_ref, gq_ref, gkv_ref, wq_ref, wkv_ref, cos_ref, sin_ref, q_ref, k_ref, v_ref, cqn_ref, ckvn_ref):
        zl = z_ref[...]
        cos_v, sin_v = cos_ref[...], sin_ref[...]
        cq = zl[:, :Q_RANK]
        ckv = zl[:, Q_RANK:Q_RANK + KV_RANK]
        krb = zl[:, Q_RANK + KV_RANK:]
        cqn = ((cq * lax.rsqrt(jnp.mean(cq * cq, axis=-1, keepdims=True) + EPS)) * gq_ref[...]).astype(MXU_DTYPE)
        ckvn = ((ckv * lax.rsqrt(jnp.mean(ckv * ckv, axis=-1, keepdims=True) + EPS)) * gkv_ref[...]).astype(MXU_DTYPE)
        cqn_ref[...] = cqn
        ckvn_ref[...] = ckvn
        krr = _rope_mix(krb, cos_v, sin_v).astype(MXU_DTYPE)
        q = jnp.dot(cqn, wq_ref[...], preferred_element_type=F32)
        kv = jnp.dot(ckvn, wkv_ref[...], preferred_element_type=F32)
        for h in range(HEADS):
            o = h * HEAD_PAD
            q_ref[:, o:o + NOPE] = q[:, o:o + NOPE].astype(MXU_DTYPE)
            q_ref[:, o + NOPE:o + HEAD_PAD] = _rope_mix(q[:, o + NOPE:o + HEAD_PAD], cos_v, sin_v).astype(MXU_DTYPE)
            k_ref[:, o:o + NOPE] = kv[:, h * NOPE:(h + 1) * NOPE].astype(MXU_DTYPE)
            k_ref[:, o + NOPE:o + HEAD_PAD] = krr
        v_ref[...] = kv[:, HEADS * NOPE:].astype(MXU_DTYPE)

    def row(w):
        return pl.BlockSpec((tr, w), lambda i: (i, 0))

    def full(a):
        return pl.BlockSpec(a.shape, lambda i: (0, 0))

    return pl.pallas_call(
        body, name="lat_fwd",
        out_shape=(jax.ShapeDtypeStruct((T, HEADS * HEAD_PAD), MXU_DTYPE), jax.ShapeDtypeStruct((T, HEADS * HEAD_PAD), MXU_DTYPE),
                   jax.ShapeDtypeStruct((T, HEADS * NOPE), MXU_DTYPE), jax.ShapeDtypeStruct((T, Q_RANK), MXU_DTYPE),
                   jax.ShapeDtypeStruct((T, KV_RANK), MXU_DTYPE)),
        grid=(T // tr,),
        in_specs=[pl.BlockSpec((tr, LAT), lambda i: (i, lat_blk)), full(gq), full(gkv), full(wq), full(wkv), row(128), row(128)],
        out_specs=(row(HEADS * HEAD_PAD), row(HEADS * HEAD_PAD), row(HEADS * NOPE), row(Q_RANK), row(KV_RANK)),
        compiler_params=_params(("parallel",), 8 * _nbytes((tr, HEADS * HEAD_PAD), F32)),
    )(z, gq, gkv, wq, wkv, cos_a, sin_a)


ATT_BLOCK = 256
_SCALE = QK_DIM ** -0.5


def _causal_mask(n):
    return lax.broadcasted_iota(jnp.int32, (n, n), 1) <= lax.broadcasted_iota(jnp.int32, (n, n), 0)


def _causal_mask_t(n):
    return lax.broadcasted_iota(jnp.int32, (n, n), 0) <= lax.broadcasted_iota(jnp.int32, (n, n), 1)


ATT_HEADS = 4


def _attn_fwd(q, k, v, B, S):
    tq = ATT_BLOCK
    nq = S // tq
    T = B * S
    hp, groups = ATT_HEADS, HEADS // ATT_HEADS

    def body(q_ref, k_ref, v_ref, o_ref, *lse_refs):
        qi = pl.program_id(2)
        qs = [q_ref[:, t * HEAD_PAD:(t + 1) * HEAD_PAD] for t in range(hp)]

        def scores(j, t):
            rows = pl.ds(pl.multiple_of(j * tq, tq), tq)
            return lax.dot_general(k_ref[rows, t * HEAD_PAD:(t + 1) * HEAD_PAD], qs[t], _DIMS["nt"],
                                   preferred_element_type=F32)

        def step(j, carry, last):
            rows = pl.ds(pl.multiple_of(j * tq, tq), tq)
            out = []
            for t in range(hp):
                m, l, acc, st = carry[t]
                st_next = st if last else scores(j + 1, t)
                st = st * _SCALE
                if last:
                    st = jnp.where(_causal_mask_t(tq), st, NEG)
                m_new = jnp.maximum(m, jnp.max(st, axis=0, keepdims=True))
                alpha = jnp.exp(m - m_new)
                p = jnp.exp(st - m_new)
                l = alpha * l + jnp.sum(p, axis=0, keepdims=True)
                acc = alpha * acc + lax.dot_general(v_ref[rows, t * NOPE:(t + 1) * NOPE], p.astype(MXU_DTYPE),
                                                    _DIMS["tn"], preferred_element_type=F32)
                out.append((m_new, l, acc, st_next))
            return tuple(out)

        init = tuple((jnp.full((1, tq), NEG, F32), jnp.zeros((1, tq), F32), jnp.zeros((NOPE, tq), F32), scores(0, t))
                     for t in range(hp))
        carry = lax.fori_loop(0, qi, lambda j, c: step(j, c, False), init)
        carry = step(qi, carry, True)
        for t in range(hp):
            m, l, acc, _ = carry[t]
            o_ref[:, t * NOPE:(t + 1) * NOPE] = (acc / l).T
            lse_refs[t][0] = m + jnp.log(l)

    lse_sds = jax.ShapeDtypeStruct((groups * B * nq, 1, tq), F32)
    lse_spec = pl.BlockSpec((1, 1, tq), lambda b, h, i: ((h * B + b) * nq + i, 0, 0))
    return pl.pallas_call(
        body, name="attn_fwd",
        out_shape=(jax.ShapeDtypeStruct((T, HEADS * NOPE), F32),) + (lse_sds,) * hp,
        grid=(B, groups, nq),
        in_specs=[pl.BlockSpec((tq, hp * HEAD_PAD), lambda b, h, i: (b * nq + i, h)),
                  pl.BlockSpec((S, hp * HEAD_PAD), lambda b, h, i: (b, h)),
                  pl.BlockSpec((S, hp * NOPE), lambda b, h, i: (b, h))],
        out_specs=(pl.BlockSpec((tq, hp * NOPE), lambda b, h, i: (b * nq + i, h)),) + (lse_spec,) * hp,
        compiler_params=_params(("parallel", "parallel", "arbitrary"), 4 * hp * _nbytes((S, HEAD_PAD), MXU_DTYPE)),
    )(q, k, v)


def _attn_bwd(q, k, v, do, lses, delta, B, S):
    tq = ATT_BLOCK
    nq = S // tq
    T = B * S
    hp, groups = ATT_HEADS, HEADS // ATT_HEADS

    def body(q_ref, k_ref, v_ref, do_ref, *refs):
        lse_refs, dl_refs = refs[:hp], refs[hp:2 * hp]
        dq_out, dk_ref, dv_ref, dq_ref = refs[2 * hp:]
        kj = pl.program_id(2)

        @pl.when(kj == 0)
        def _():
            dq_ref[...] = jnp.zeros_like(dq_ref)

        def products(i, t):
            rows = pl.ds(pl.multiple_of(i * tq, tq), tq)
            st = lax.dot_general(k_ref[:, t * HEAD_PAD:(t + 1) * HEAD_PAD], q_ref[rows, t * HEAD_PAD:(t + 1) * HEAD_PAD],
                                 _DIMS["nt"], preferred_element_type=F32)
            dpt = lax.dot_general(v_ref[:, t * NOPE:(t + 1) * NOPE], do_ref[rows, t * NOPE:(t + 1) * NOPE],
                                  _DIMS["nt"], preferred_element_type=F32)
            return st, dpt

        def step(i, carry, masked, prefetch):
            rows = pl.ds(i * tq if isinstance(i, int) else pl.multiple_of(i * tq, tq), tq)
            out = []
            for t in range(hp):
                dk, dv, st, dpt = carry[t]
                st_next, dpt_next = products(i + 1, t) if prefetch else (st, dpt)
                qk_cols = slice(t * HEAD_PAD, (t + 1) * HEAD_PAD)
                v_cols = slice(t * NOPE, (t + 1) * NOPE)
                p = jnp.exp(st * _SCALE - lse_refs[t][i])
                if masked:
                    p = jnp.where(_causal_mask_t(tq), p, 0.0)
                dv = dv + jnp.dot(p.astype(MXU_DTYPE), do_ref[rows, v_cols], preferred_element_type=F32)
                ds = (p * (dpt - dl_refs[t][i]) * _SCALE).astype(MXU_DTYPE)
                dk = dk + jnp.dot(ds, q_ref[rows, qk_cols], preferred_element_type=F32)
                dq_ref[rows, qk_cols] += lax.dot_general(ds, k_ref[:, qk_cols], _DIMS["tn"], preferred_element_type=F32)
                out.append((dk, dv, st_next, dpt_next))
            return tuple(out)

        def first_products():
            return tuple((jnp.zeros((tq, HEAD_PAD), F32), jnp.zeros((tq, NOPE), F32)) + products(kj, t) for t in range(hp))

        def finish(carry):
            for t in range(hp):
                dk_ref[:, t * HEAD_PAD:(t + 1) * HEAD_PAD] = carry[t][0].astype(dk_ref.dtype)
                dv_ref[:, t * NOPE:(t + 1) * NOPE] = carry[t][1].astype(dv_ref.dtype)

        @pl.when(kj < nq - 1)
        def _():
            carry = step(kj, first_products(), True, True)
            carry = lax.fori_loop(kj + 1, nq - 1, lambda i, c: step(i, c, False, True), carry)
            finish(step(nq - 1, carry, False, False))

        @pl.when(kj == nq - 1)
        def _():
            finish(step(kj, first_products(), True, False))
            dq_out[...] = dq_ref[...].astype(dq_out.dtype)

    seq = lambda w: pl.BlockSpec((S, w), lambda b, h, j: (b, h))
    blk = lambda w: pl.BlockSpec((tq, w), lambda b, h, j: (b * nq + j, h))
    lse_spec = pl.BlockSpec((nq, 1, tq), lambda b, h, j: (h * B + b, 0, 0))
    dl_specs = [pl.BlockSpec((nq, 1, tq), lambda b, h, j, t=t: ((h * hp + t) * B + b, 0, 0)) for t in range(hp)]
    return pl.pallas_call(
        body, name="attn_bwd",
        out_shape=(jax.ShapeDtypeStruct((T, HEADS * HEAD_PAD), MXU_DTYPE), jax.ShapeDtypeStruct((T, HEADS * HEAD_PAD), MXU_DTYPE),
                   jax.ShapeDtypeStruct((T, HEADS * NOPE), MXU_DTYPE)),
        grid=(B, groups, nq),
        in_specs=[seq(hp * HEAD_PAD), blk(hp * HEAD_PAD), blk(hp * NOPE), seq(hp * NOPE)] + [lse_spec] * hp + dl_specs,
        out_specs=(seq(hp * HEAD_PAD), blk(hp * HEAD_PAD), blk(hp * NOPE)),
        scratch_shapes=[pltpu.VMEM((S, hp * HEAD_PAD), F32)],
        compiler_params=_params(("parallel", "parallel", "arbitrary"), 8 * hp * _nbytes((S, HEAD_PAD), F32)),
    )(q, k, v, do, *lses, *([delta] * hp))


MIX_ROWS = 256


def _tril_weights(ws_ref, g):
    return jnp.where(_causal_mask(CHUNK), ws_ref[g], 0.0).astype(MXU_DTYPE)


def _layer_norm_stats(va):
    mu = jnp.mean(va, axis=-1, keepdims=True)
    xc = va - mu
    rs = lax.rsqrt(jnp.mean(xc * xc, axis=-1, keepdims=True) + EPS)
    return xc * rs


def _mix_specs(tr):
    zcol = lambda c: pl.BlockSpec((tr, D_MODEL), lambda i, c=c: (i, c))
    row = pl.BlockSpec((tr, D_MODEL), lambda i: (i, 0))
    vec = pl.BlockSpec((1, D_MODEL), lambda i: (0, 0))
    ws = pl.BlockSpec((A_GROUPS, CHUNK, CHUNK), lambda i: (0, 0, 0))
    bs = pl.BlockSpec((CHUNK, 128), lambda i: (0, 0))
    return zcol, row, vec, ws, bs


def _mix_fwd(z, yb, ln_g, ln_b, ws, bs_t):
    T = z.shape[0]
    tr = MIX_ROWS
    zcol, row, vec, ws_spec, bs_spec = _mix_specs(tr)

    def body(zu_ref, zv_ref, zga_ref, zgb_ref, yb_ref, g_ref, b_ref, ws_ref, bs_ref, out_ref, vn_s):
        vhat = _layer_norm_stats(_gelu(zv_ref[...]))
        vn_s[...] = (vhat * g_ref[...] + b_ref[...]).astype(MXU_DTYPE)
        for g in range(A_GROUPS):
            w = _tril_weights(ws_ref, g)
            bias = bs_ref[:, g:g + 1]
            cols = slice(g * CHUNK, (g + 1) * CHUNK)
            for c in range(tr // CHUNK):
                rows = slice(c * CHUNK, (c + 1) * CHUNK)
                mixed = jnp.dot(w, vn_s[rows, cols], preferred_element_type=F32) + bias
                ya = _gelu(zu_ref[rows, cols]) * mixed
                merged = _sigmoid(zga_ref[rows, cols]) * ya + _sigmoid(zgb_ref[rows, cols]) * yb_ref[rows, cols]
                out_ref[rows, cols] = merged.astype(MXU_DTYPE)

    return pl.pallas_call(
        body, name="mix_fwd", out_shape=jax.ShapeDtypeStruct((T, D_MODEL), MXU_DTYPE), grid=(T // tr,),
        in_specs=[zcol(0), zcol(1), zcol(2), zcol(3), row, vec, vec, ws_spec, bs_spec], out_specs=row,
        scratch_shapes=[pltpu.VMEM((tr, D_MODEL), MXU_DTYPE)],
        compiler_params=_params(("parallel",), 8 * _nbytes((tr, D_MODEL), F32)),
    )(z, z, z, z, yb, ln_g, ln_b, ws, bs_t)


def _mix_bwd(z, yb, dm, ln_g, ln_b, ws, bs_t):
    T = z.shape[0]
    tr = MIX_ROWS
    zcol, row, vec, ws_spec, bs_spec = _mix_specs(tr)

    def body(zu_ref, zv_ref, zga_ref, zgb_ref, yb_ref, dm_ref, g_ref, b_ref, ws_ref, bs_ref,
             dz_ref, dyb_ref, dl_ref, gws_ref, gbs_ref, glg_ref, glb_ref, vn_s, dvn_s):
        @pl.when(pl.program_id(0) == 0)
        def _():
            gws_ref[...] = jnp.zeros_like(gws_ref)
            gbs_ref[...] = jnp.zeros_like(gbs_ref)
            glg_ref[...] = jnp.zeros_like(glg_ref)
            glb_ref[...] = jnp.zeros_like(glb_ref)

        lane = lax.broadcasted_iota(jnp.int32, (CHUNK, 128), 1)
        va, dgelu_v = _gelu_and_grad(zv_ref[...])
        mu = jnp.mean(va, axis=-1, keepdims=True)
        xc = va - mu
        rs = lax.rsqrt(jnp.mean(xc * xc, axis=-1, keepdims=True) + EPS)
        vhat = xc * rs
        vn_s[...] = (vhat * g_ref[...] + b_ref[...]).astype(MXU_DTYPE)
        gbs_acc = jnp.zeros((CHUNK, 128), F32)
        for g in range(A_GROUPS):
            w = _tril_weights(ws_ref, g)
            bias = bs_ref[:, g:g + 1]
            cols = slice(g * CHUNK, (g + 1) * CHUNK)
            gw_acc = jnp.zeros((CHUNK, CHUNK), F32)
            for c in range(tr // CHUNK):
                rows = slice(c * CHUNK, (c + 1) * CHUNK)
                vn = vn_s[rows, cols]
                mixed = jnp.dot(w, vn, preferred_element_type=F32) + bias
                ua, dgelu_u = _gelu_and_grad(zu_ref[rows, cols])
                dmv = dm_ref[rows, cols]
                sa = _sigmoid(zga_ref[rows, cols])
                dya = dmv * sa
                dz_ref[rows, 2 * D_MODEL + g * CHUNK:2 * D_MODEL + (g + 1) * CHUNK] = (
                    dmv * (ua * mixed) * (sa * (1.0 - sa))).astype(dz_ref.dtype)
                dz_ref[rows, cols] = (dya * mixed * dgelu_u).astype(dz_ref.dtype)
                dmix = dya * ua
                gbs_acc = gbs_acc + jnp.where(lane == g, jnp.sum(dmix, axis=-1, keepdims=True), 0.0)
                dmix_b = dmix.astype(MXU_DTYPE)
                gw_acc = gw_acc + lax.dot_general(dmix_b, vn, _DIMS["nt"], preferred_element_type=F32)
                dvn_s[rows, cols] = lax.dot_general(w, dmix_b, _DIMS["tn"], preferred_element_type=F32)
            gws_ref[g] += jnp.where(_causal_mask(CHUNK), gw_acc, 0.0)
        gbs_ref[...] += gbs_acc

        dvn = dvn_s[...]
        glg_ref[...] += jnp.sum(dvn * vhat, axis=0, keepdims=True)
        glb_ref[...] += jnp.sum(dvn, axis=0, keepdims=True)
        dvh = dvn * g_ref[...]
        dva = rs * (dvh - jnp.mean(dvh, axis=-1, keepdims=True) - vhat * jnp.mean(dvh * vhat, axis=-1, keepdims=True))
        dz_ref[:, D_MODEL:2 * D_MODEL] = (dva * dgelu_v).astype(dz_ref.dtype)

        dmv = dm_ref[...]
        ybv = yb_ref[...]
        sb = _sigmoid(zgb_ref[...])
        dyb = dmv * sb
        dyb_ref[...] = dyb.astype(dyb_ref.dtype)
        dz_ref[:, 3 * D_MODEL:4 * D_MODEL] = (dmv * ybv * (sb * (1.0 - sb))).astype(dz_ref.dtype)
        dz_ref[:, 4 * D_MODEL:] = jnp.zeros((tr, LAT), dz_ref.dtype)
        prod = dyb * ybv
        sel = (lax.broadcasted_iota(jnp.int32, (HEADS, D_MODEL), 1) // NOPE
               == lax.broadcasted_iota(jnp.int32, (HEADS, D_MODEL), 0)).astype(jnp.bfloat16)
        hi = prod.astype(jnp.bfloat16)
        rest = prod - hi.astype(F32)
        mid = rest.astype(jnp.bfloat16)
        lo = (rest - mid.astype(F32)).astype(jnp.bfloat16)
        dl_ref[...] = (lax.dot_general(sel, hi, _DIMS["nt"], preferred_element_type=F32)
                       + lax.dot_general(sel, mid, _DIMS["nt"], preferred_element_type=F32)
                       + lax.dot_general(sel, lo, _DIMS["nt"], preferred_element_type=F32))

    return pl.pallas_call(
        body, name="mix_bwd",
        out_shape=(jax.ShapeDtypeStruct((T, IN_PAD), MXU_DTYPE), jax.ShapeDtypeStruct((T, D_MODEL), MXU_DTYPE),
                   jax.ShapeDtypeStruct((HEADS, T), F32), jax.ShapeDtypeStruct((A_GROUPS, CHUNK, CHUNK), F32),
                   jax.ShapeDtypeStruct((CHUNK, 128), F32), jax.ShapeDtypeStruct((1, D_MODEL), F32),
                   jax.ShapeDtypeStruct((1, D_MODEL), F32)),
        grid=(T // tr,),
        in_specs=[zcol(0), zcol(1), zcol(2), zcol(3), row, row, vec, vec, ws_spec, bs_spec],
        out_specs=(pl.BlockSpec((tr, IN_PAD), lambda i: (i, 0)), row, pl.BlockSpec((HEADS, tr), lambda i: (0, i)),
                   ws_spec, bs_spec, vec, vec),
        scratch_shapes=[pltpu.VMEM((tr, D_MODEL), MXU_DTYPE), pltpu.VMEM((tr, D_MODEL), F32)],
        compiler_params=_params(("arbitrary",), 12 * _nbytes((tr, D_MODEL), F32)),
    )(z, z, z, z, yb, dm, ln_g, ln_b, ws, bs_t)


def _lat_bwd(dz, z, dq, dk, dv, gq, gkv, wq, wkv, cos_a, sin_a, tr=256):
    T = z.shape[0]
    lat_blk = (4 * D_MODEL) // LAT

    def body(dz_in, z_ref, dq_ref, dk_ref, dv_ref, gq_ref, gkv_ref, wq_ref, wkv_ref, cos_ref, sin_ref,
             dz_ref, dqr_ref, dkv_ref, ggq_ref, ggkv_ref):
        del dz_in

        @pl.when(pl.program_id(0) == 0)
        def _():
            ggq_ref[...] = jnp.zeros_like(ggq_ref)
            ggkv_ref[...] = jnp.zeros_like(ggkv_ref)

        cos_v, sin_v = cos_ref[...], sin_ref[...]
        dkr = jnp.zeros((tr, 128), F32)
        for h in range(HEADS):
            o = h * HEAD_PAD
            dqr_ref[:, o:o + NOPE] = dq_ref[:, o:o + NOPE].astype(MXU_DTYPE)
            dqr_ref[:, o + NOPE:o + HEAD_PAD] = _rope_mix_bwd(dq_ref[:, o + NOPE:o + HEAD_PAD], cos_v, sin_v).astype(MXU_DTYPE)
            dkv_ref[:, h * NOPE:(h + 1) * NOPE] = dk_ref[:, o:o + NOPE].astype(MXU_DTYPE)
            dkr = dkr + _rope_mix_bwd(dk_ref[:, o + NOPE:o + HEAD_PAD], cos_v, sin_v)
        dkv_ref[:, HEADS * NOPE:] = dv_ref[...]
        dcqn = lax.dot_general(dqr_ref[...], wq_ref[...], _DIMS["nt"], preferred_element_type=F32)
        dckvn = lax.dot_general(dkv_ref[...], wkv_ref[...], _DIMS["nt"], preferred_element_type=F32)

        zl = z_ref[...]

        def rms_bwd(c, dn, g_ref, gg_ref):
            r = lax.rsqrt(jnp.mean(c * c, axis=-1, keepdims=True) + EPS)
            ch = c * r
            gg_ref[...] += jnp.sum(dn * ch, axis=0, keepdims=True)
            dch = dn * g_ref[...]
            return r * (dch - ch * jnp.mean(dch * ch, axis=-1, keepdims=True))

        dz_ref[:, :Q_RANK] = rms_bwd(zl[:, :Q_RANK], dcqn, gq_ref, ggq_ref).astype(dz_ref.dtype)
        dz_ref[:, Q_RANK:Q_RANK + KV_RANK] = rms_bwd(zl[:, Q_RANK:Q_RANK + KV_RANK], dckvn, gkv_ref, ggkv_ref).astype(dz_ref.dtype)
        dz_ref[:, Q_RANK + KV_RANK:] = dkr.astype(dz_ref.dtype)

    def row(w):
        return pl.BlockSpec((tr, w), lambda i: (i, 0))

    def full(a):
        return pl.BlockSpec(a.shape, lambda i: (0, 0))

    lat = pl.BlockSpec((tr, LAT), lambda i: (i, lat_blk))
    return pl.pallas_call(
        body, name="lat_bwd",
        out_shape=(jax.ShapeDtypeStruct(dz.shape, dz.dtype), jax.ShapeDtypeStruct((T, HEADS * HEAD_PAD), MXU_DTYPE),
                   jax.ShapeDtypeStruct((T, 2 * HEADS * NOPE), MXU_DTYPE), jax.ShapeDtypeStruct(gq.shape, F32),
                   jax.ShapeDtypeStruct(gkv.shape, F32)),
        grid=(T // tr,),
        in_specs=[pl.BlockSpec(memory_space=pl.ANY), lat, row(HEADS * HEAD_PAD), row(HEADS * HEAD_PAD), row(HEADS * NOPE),
                  full(gq), full(gkv), full(wq), full(wkv), row(128), row(128)],
        out_specs=(lat, row(HEADS * HEAD_PAD), row(2 * HEADS * NOPE), full(gq), full(gkv)),
        input_output_aliases={0: 0},
        compiler_params=_params(("arbitrary",), 8 * _nbytes((tr, HEADS * HEAD_PAD), F32)),
    )(dz, z, dq, dk, dv, gq, gkv, wq, wkv, cos_a, sin_a)


GATE_ROWS = 256
HALO = 8


def _taps(ref, half, r, first):
    C = GATE_ROWS
    if first:
        xs = jnp.concatenate([jnp.zeros((HALO, ref.shape[-1]), F32), ref[half, 0:C, :]], axis=0)
    else:
        xs = ref[half, pl.ds(pl.multiple_of(r * C - HALO, HALO), C + HALO), :]
    return xs[HALO:, :], pltpu.roll(xs, 1, 0)[HALO:, :], pltpu.roll(xs, 2, 0)[HALO:, :]


def _conv_taps(taps, cw, cb):
    x0, x1, x2 = taps
    return cb + cw[0:1, :] * x2 + cw[1:2, :] * x1 + cw[2:3, :] * x0


def _fold8(x):
    acc = x[0:8, :]
    for i in range(1, x.shape[0] // 8):
        acc = acc + x[8 * i:8 * (i + 1), :]
    return acc


def _gate_fwd(up3, conv_w, conv_b, B, S):
    T = B * S
    W = FF_TILE
    C = GATE_ROWS

    def body(up_ref, cw_ref, cb_ref, act_ref, conv_ref):
        def chunk(r, first):
            gate = _conv_taps(_taps(up_ref, 0, r, first), cw_ref[0], cb_ref[0])
            val = _conv_taps(_taps(up_ref, 1, r, first), cw_ref[1], cb_ref[1])
            rows = pl.ds(0 if first else pl.multiple_of(r * C, C), C)
            conv_ref[0, rows, :] = gate.astype(conv_ref.dtype)
            conv_ref[1, rows, :] = val.astype(conv_ref.dtype)
            act_ref[rows, :] = (gate * _sigmoid(gate) * val).astype(act_ref.dtype)

        chunk(0, True)

        @pl.loop(1, S // C)
        def _(r):
            chunk(r, False)

    up_spec = pl.BlockSpec((2, S, W), lambda b, j: (0, b, j))
    return pl.pallas_call(
        body, name="gate_fwd",
        out_shape=(jax.ShapeDtypeStruct((T, D_FF), MXU_DTYPE), jax.ShapeDtypeStruct((2, T, D_FF), MXU_DTYPE)),
        grid=(B, N_FF_TILES),
        in_specs=[up_spec, pl.BlockSpec((2, 3, W), lambda b, j: (0, 0, j)), pl.BlockSpec((2, 1, W), lambda b, j: (0, 0, j))],
        out_specs=(pl.BlockSpec((S, W), lambda b, j: (b, j)), up_spec),
        compiler_params=_params(("parallel", "parallel"), 8 * _nbytes((S, W), F32)),
    )(up3, conv_w, conv_b)


def _gate_bwd(up3, conv3, dact, conv_w, B, S):
    T = B * S
    W = FF_TILE
    C = GATE_ROWS

    def body(up_ref, conv_ref, da_ref, cw_ref, dup_ref, gcw_ref, gcb_ref, d_s):
        @pl.when(pl.program_id(1) == 0)
        def _():
            gcw_ref[...] = jnp.zeros_like(gcw_ref)
            gcb_ref[...] = jnp.zeros_like(gcb_ref)

        @pl.loop(0, S // C)
        def _(r):
            rows = pl.ds(pl.multiple_of(r * C, C), C)
            gate, val = conv_ref[0, rows, :].astype(F32), conv_ref[1, rows, :].astype(F32)
            sg = _sigmoid(gate)
            da = da_ref[rows, :]
            d_s[0, rows, :] = da * val * (sg * (1.0 + gate * (1.0 - sg)))
            d_s[1, rows, :] = da * (gate * sg)

        d_s[:, S:S + HALO, :] = jnp.zeros((2, HALO, W), F32)

        def chunk(r, sums):
            base = pl.multiple_of(r * C, C)
            out = []
            for half in (0, 1):
                ds_ = d_s[half, pl.ds(base, C + HALO), :]
                d0, d1, d2 = ds_[:C, :], pltpu.roll(ds_, C + HALO - 1, 0)[:C, :], pltpu.roll(ds_, C + HALO - 2, 0)[:C, :]
                cw = cw_ref[half]
                dup_ref[half, pl.ds(base, C), :] = (cw[2:3, :] * d0 + cw[1:2, :] * d1 + cw[0:1, :] * d2).astype(dup_ref.dtype)
                x = up_ref[half, pl.ds(base, C), :]
                sb, s0, s1, s2 = sums[half]
                out.append((sb + _fold8(d0), s0 + _fold8(d2 * x), s1 + _fold8(d1 * x), s2 + _fold8(d0 * x)))
            return tuple(out)

        zeros = tuple(tuple(jnp.zeros((8, W), F32) for _ in range(4)) for _ in range(2))
        sums = lax.fori_loop(0, S // C, chunk, zeros)
        for half in (0, 1):
            sb, s0, s1, s2 = sums[half]
            gcb_ref[half] += jnp.sum(sb, axis=0, keepdims=True)
            gcw_ref[half, 0:1, :] += jnp.sum(s0, axis=0, keepdims=True)
            gcw_ref[half, 1:2, :] += jnp.sum(s1, axis=0, keepdims=True)
            gcw_ref[half, 2:3, :] += jnp.sum(s2, axis=0, keepdims=True)

    up_spec = pl.BlockSpec((2, S, W), lambda j, b: (0, b, j))
    cw_spec = pl.BlockSpec((2, 3, W), lambda j, b: (0, 0, j))
    cb_spec = pl.BlockSpec((2, 1, W), lambda j, b: (0, 0, j))
    return pl.pallas_call(
        body, name="gate_bwd",
        out_shape=(jax.ShapeDtypeStruct((2, T, D_FF), MXU_DTYPE), jax.ShapeDtypeStruct((2, 3, D_FF), F32),
                   jax.ShapeDtypeStruct((2, 1, D_FF), F32)),
        grid=(N_FF_TILES, B),
        in_specs=[up_spec, up_spec, pl.BlockSpec((S, W), lambda j, b: (b, j)), cw_spec],
        out_specs=(up_spec, cw_spec, cb_spec),
        scratch_shapes=[pltpu.VMEM((2, S + HALO, W), F32)],
        compiler_params=_params(("parallel", "arbitrary"), 12 * _nbytes((S, W), F32)),
    )(up3, conv3, dact, conv_w)


def _final(x2, tgt, g, tr=512):
    T, D = x2.shape

    def body(x_ref, t_ref, g_ref, dx_ref, loss_ref, gg_ref):
        @pl.when(pl.program_id(0) == 0)
        def _():
            loss_ref[...] = jnp.zeros_like(loss_ref)
            gg_ref[...] = jnp.zeros_like(gg_ref)

        xv = x_ref[...]
        gv = g_ref[...]
        r = lax.rsqrt(jnp.mean(xv * xv, axis=-1, keepdims=True) + EPS)
        xn = xv * r
        err = xn * gv - t_ref[...]
        loss_ref[...] += 0.5 * jnp.sum(jnp.mean(err * err, axis=-1, keepdims=True), axis=0, keepdims=True)
        dy = err * (1.0 / D)
        gg_ref[...] += jnp.sum(dy * xn, axis=0, keepdims=True)
        dxn = dy * gv
        dx_ref[...] = r * (dxn - xn * jnp.mean(dxn * xn, axis=-1, keepdims=True))

    row = pl.BlockSpec((tr, D), lambda i: (i, 0))
    vec = pl.BlockSpec((1, D), lambda i: (0, 0))
    return pl.pallas_call(
        body, name="final_loss",
        out_shape=(jax.ShapeDtypeStruct((T, D), F32), jax.ShapeDtypeStruct((1, 128), F32), jax.ShapeDtypeStruct((1, D), F32)),
        grid=(T // tr,), in_specs=[row, row, vec],
        out_specs=(row, pl.BlockSpec((1, 128), lambda i: (0, 0)), vec),
        compiler_params=_params(("arbitrary",), 6 * _nbytes((tr, D), F32)),
    )(x2, tgt, g)


def _sum_slabs(parts, name, tr):
    rows, cols = parts[0].shape
    n = len(parts)

    def body(*refs):
        acc = refs[0][...]
        for r in refs[1:n]:
            acc = acc + r[...]
        refs[n][...] = acc

    blk = pl.BlockSpec((tr, cols), lambda i: (i, 0))
    return pl.pallas_call(
        body, name=name, out_shape=jax.ShapeDtypeStruct((rows, cols), F32), grid=(rows // tr,),
        in_specs=[blk] * n, out_specs=blk,
        compiler_params=_params(("parallel",), (n + 1) * _nbytes((tr, cols), F32)),
    )(*parts)


ADAMW_BLOCK_BYTES = 2400 * 1024


def _adamw(w, g, m, v, name, copy_grad=False):
    lead = w.ndim == 3
    rows, cols = w.shape[-2:]
    fits = [d for d in range(8, rows + 1, 8) if rows % d == 0 and d * cols * 4 <= ADAMW_BLOCK_BYTES]
    tr = max(fits) if fits else rows
    c1 = 1.0 - ADAM_B1 ** ADAM_STEP
    c2 = 1.0 - ADAM_B2 ** ADAM_STEP

    def body(w_ref, g_ref, m_ref, v_ref, d_ref, nm_ref, nv_ref, *g_out):
        gv = g_ref[...]
        nm = ADAM_B1 * m_ref[...] + (1.0 - ADAM_B1) * gv
        nv = ADAM_B2 * v_ref[...] + (1.0 - ADAM_B2) * (gv * gv)
        nm_ref[...] = nm
        nv_ref[...] = nv
        d_ref[...] = -ADAM_LR * ((nm / c1) / (jnp.sqrt(nv / c2) + ADAM_EPS) + ADAM_WD * w_ref[...])
        if copy_grad:
            g_out[0][...] = gv

    blk = pl.BlockSpec((None, tr, cols), lambda i: (0, i, 0)) if lead else pl.BlockSpec((tr, cols), lambda i: (i, 0))
    sds = jax.ShapeDtypeStruct(w.shape, F32)
    n_out = 4 if copy_grad else 3
    return pl.pallas_call(
        body, name=name, out_shape=(sds,) * n_out, grid=(rows // tr,), in_specs=[blk] * 4, out_specs=(blk,) * n_out,
        compiler_params=_params(("parallel",), (4 + n_out) * _nbytes((tr, cols), F32)),
    )(w, g, m, v)


_ANY = pl.BlockSpec(memory_space=pl.ANY)


def _place():
    x, y, c = lax.axis_index("x"), lax.axis_index("y"), lax.axis_index("c")
    chips = [(1 - x, y), (x, 1 - y), (1 - x, 1 - y)]
    return x, y, c, chips


def _forward_halves(lands):
    n = len(lands)

    def body(*refs):
        outs, send, recv = refs[n:2 * n], refs[2 * n], refs[2 * n + 1]
        x, y, c, chips = _place()
        cps = []
        for w in range(n):
            for j, (px, py) in enumerate(chips):
                landed = outs[w].at[2 * px + py, c]
                cps.append(pltpu.make_async_remote_copy(
                    src_ref=landed, dst_ref=landed, send_sem=send.at[3 * w + j], recv_sem=recv.at[3 * w + j],
                    device_id=(x, y, 1 - c), device_id_type=MESH))
        for cp in cps:
            cp.start()
        for w in range(n):
            for j, (px, py) in enumerate(chips):
                other = outs[w].at[2 * px + py, 1 - c]
                pltpu.make_async_remote_copy(src_ref=other, dst_ref=other, send_sem=send.at[3 * w + j],
                                             recv_sem=recv.at[3 * w + j], device_id=(x, y, 1 - c),
                                             device_id_type=MESH).wait_recv()
        for cp in cps:
            cp.wait_send()

    dma = lambda k: pltpu.SemaphoreType.DMA((k,))
    return pl.pallas_call(
        body, name="gather_forward_halves", out_shape=tuple(jax.ShapeDtypeStruct(a.shape, a.dtype) for a in lands),
        in_specs=[_ANY] * n, out_specs=tuple([_ANY] * n), input_output_aliases={w: w for w in range(n)},
        scratch_shapes=[dma(3 * n), dma(3 * n)],
    )(*lands)


_HBM = pl.BlockSpec(memory_space=pltpu.HBM)
_SEM = pl.BlockSpec(memory_space=pltpu.SEMAPHORE)
_EFFECT = pltpu.SideEffectType.DATAFLOW_SIDE_EFFECTING


SEMS_PER_ARRAY = 8


def _exchange_copies(srcs, lands, send, recv, mode):
    x, y, c, chips = _place()
    if mode == "halves":
        cps = []
        for w, (src, land) in enumerate(zip(srcs, lands)):
            pieces = [(src.at[c], land.at[2 * x + y, c], (px, py, c)) for px, py in chips]
            pieces.append((src, land.at[2 * x + y], (x, y, 1 - c)))
            for k, (piece, dst, peer) in enumerate(pieces):
                cps.append(pltpu.make_async_remote_copy(
                    src_ref=piece, dst_ref=dst, send_sem=send.at[SEMS_PER_ARRAY * w + k],
                    recv_sem=recv.at[SEMS_PER_ARRAY * w + k], device_id=peer, device_id_type=MESH))
        return cps
    if mode == "swap":
        return [pltpu.make_async_remote_copy(
            src_ref=src.at[:, 1 - c], dst_ref=land, send_sem=send.at[SEMS_PER_ARRAY * w],
            recv_sem=recv.at[SEMS_PER_ARRAY * w], device_id=(x, y, 1 - c), device_id_type=MESH)
            for w, (src, land) in enumerate(zip(srcs, lands))]
    if mode == "all":
        flips = [(fx, fy, fc) for fx in (0, 1) for fy in (0, 1) for fc in (0, 1)][1:]
        peers = [(x ^ fx, y ^ fy, c ^ fc) for fx, fy, fc in flips]
        slot = 4 * x + 2 * y + c
    else:
        peers = [(px, py, c) for px, py in chips] + ([(x, y, 1 - c)] if mode == "gather" else [])
        slot = 2 * x + y
    cps = []
    for w, (src, land) in enumerate(zip(srcs, lands)):
        for k, peer in enumerate(peers):
            piece = src.at[2 * peer[0] + peer[1]] if mode == "scatter" else src
            cps.append(pltpu.make_async_remote_copy(
                src_ref=piece, dst_ref=land.at[slot], send_sem=send.at[SEMS_PER_ARRAY * w + k],
                recv_sem=recv.at[SEMS_PER_ARRAY * w + k], device_id=peer, device_id_type=MESH))
    return cps


def _exchange_start(srcs, name, mode, after):
    n = len(srcs)
    if mode == "swap":
        land_shapes = [(s.shape[0],) + s.shape[2:] for s in srcs]
    else:
        lead = {"gather": (N_CHIPS,), "halves": (N_CHIPS,), "scatter": (), "all": (2 * N_CHIPS,)}[mode]
        land_shapes = [lead + s.shape for s in srcs]

    def body(*refs):
        src_refs, land_refs = refs[:n], refs[n:2 * n]
        send, recv = refs[2 * n + 1], refs[2 * n + 2]
        token = refs[-1]
        for cp in _exchange_copies(src_refs, land_refs, send, recv, mode):
            cp.start()
        token[...] = jnp.zeros_like(token)

    sems = pltpu.SemaphoreType.DMA((SEMS_PER_ARRAY * n,))
    out = pl.pallas_call(
        body, name=name,
        out_shape=(sems, sems, *[pltpu.HBM(s.shape, s.dtype) for s in srcs],
                   *[pltpu.HBM(shp, s.dtype) for shp, s in zip(land_shapes, srcs)], jax.ShapeDtypeStruct((8, 128), F32)),
        in_specs=[_HBM] * (2 * n) + [_ANY],
        out_specs=(_SEM, _SEM, *[_HBM] * (2 * n), pl.BlockSpec(memory_space=pltpu.VMEM)),
        input_output_aliases={i: 2 + i for i in range(2 * n)},
        compiler_params=pltpu.CompilerParams(has_side_effects=_EFFECT),
    )(*[pltpu.with_memory_space_constraint(s, pltpu.HBM) for s in srcs],
      *[pltpu.with_memory_space_constraint(lax.empty(shp, s.dtype), pltpu.HBM) for shp, s in zip(land_shapes, srcs)],
      after)
    return out[0], out[1], out[2:2 + n], out[2 + n:2 + 2 * n], out[-1]


def _exchange_wait(started, name, mode, after):
    send, recv, src_thru, land_thru, _ = started
    n = len(src_thru)
    after = list(after) if isinstance(after, (list, tuple)) else [after]

    def body(*refs):
        src_refs, land_refs, send_ref, recv_ref = refs[:n], refs[n:2 * n], refs[2 * n], refs[2 * n + 1]
        for cp in _exchange_copies(src_refs, land_refs, send_ref, recv_ref, mode):
            cp.wait_send()
            cp.wait_recv()

    out = pl.pallas_call(
        body, name=name,
        out_shape=tuple(pltpu.HBM(a.shape, a.dtype) for a in list(src_thru) + list(land_thru)),
        in_specs=[_HBM] * (2 * n) + [_SEM, _SEM] + [_ANY] * len(after), out_specs=tuple([_HBM] * (2 * n)),
        input_output_aliases={i: i for i in range(2 * n)},
        compiler_params=pltpu.CompilerParams(has_side_effects=_EFFECT),
    )(*src_thru, *land_thru, send, recv, *after)
    return out[:n], out[n:]


def _swap_halves(gs, name):
    n = len(gs)

    def body(*refs):
        ins, outs, send, recv = refs[:n], refs[n:2 * n], refs[2 * n], refs[2 * n + 1]
        x, y, c, _ = _place()
        cps = []
        for w in range(n):
            cps.append(pltpu.make_async_remote_copy(
                src_ref=ins[w].at[:, 1 - c], dst_ref=outs[w], send_sem=send.at[w], recv_sem=recv.at[w],
                device_id=(x, y, 1 - c), device_id_type=MESH))
        for cp in cps:
            cp.start()
        for cp in cps:
            cp.wait()

    return pl.pallas_call(
        body, name=name,
        out_shape=tuple(jax.ShapeDtypeStruct((g.shape[0],) + g.shape[2:], g.dtype) for g in gs),
        in_specs=[_ANY] * n, out_specs=tuple([_ANY] * n),
        scratch_shapes=[pltpu.SemaphoreType.DMA((n,)), pltpu.SemaphoreType.DMA((n,))],
    )(*gs)


GRAD_PAYLOAD = jnp.bfloat16


def _half_blocks(half_rows, cols):
    if (half_rows // 2) % 16 == 0:
        return (half_rows // 2, cols), (lambda r: (r, 0))
    assert cols % 256 == 0, (half_rows, cols)
    return (half_rows, cols // 2), (lambda r: (0, r))


def _pair_sum(gs, gots, name):
    n = len(gs)
    core = lax.axis_index("c").astype(jnp.int32).reshape(1)

    def body(core_ref, *refs):
        del core_ref
        for w in range(n):
            refs[2 * n + w][...] = (refs[w][...] + refs[n + w][...]).astype(GRAD_PAYLOAD)

    in_specs, out_specs, out_shape, nbytes = [], [], [], 0
    cuts = [_half_blocks(g.shape[1] // 2, g.shape[2]) for g in gs]
    for g, ((br, bc), at) in zip(gs, cuts):
        per_half = (g.shape[1] // 2) // br
        in_specs.append(pl.BlockSpec((1, br, bc), lambda s, r, core, at=at, per_half=per_half:
                                     (s, per_half * core[0] + at(r)[0], at(r)[1])))
        nbytes += 3 * _nbytes((br, bc), F32)
    for g, ((br, bc), at) in zip(gs, cuts):
        in_specs.append(pl.BlockSpec((1, br, bc), lambda s, r, core, at=at: (s,) + at(r)))
        out_specs.append(pl.BlockSpec((1, br, bc), lambda s, r, core, at=at: (s,) + at(r)))
        out_shape.append(jax.ShapeDtypeStruct((g.shape[0], g.shape[1] // 2, g.shape[2]), GRAD_PAYLOAD))
    return pl.pallas_call(
        body, name=name, out_shape=tuple(out_shape),
        grid_spec=pltpu.PrefetchScalarGridSpec(num_scalar_prefetch=1, grid=(N_CHIPS, 2), in_specs=in_specs,
                                               out_specs=tuple(out_specs)),
        compiler_params=_params(("parallel", "parallel"), nbytes),
    )(core, *gs, *gots)


def _chip_sum(ps, landed):
    n = len(ps)
    x, y, c = lax.axis_index("x"), lax.axis_index("y"), lax.axis_index("c")
    where = jnp.stack([2 * x + y, 2 * (1 - x) + y, 2 * x + (1 - y), 2 * (1 - x) + (1 - y), c]).astype(jnp.int32)

    def body(where_ref, *refs):
        del where_ref
        for w in range(n):
            terms = [refs[4 * w + t][...].astype(F32) for t in range(4)]
            refs[4 * n + w][...] = ((terms[0] + terms[1]) + terms[2]) + terms[3]

    in_specs, out_specs, out_shape, args, nbytes = [], [], [], [], 0
    for p, a in zip(ps, landed):
        (br, bc), at = _half_blocks(a.shape[1], a.shape[2])
        blk = (1, br, bc)
        in_specs.append(pl.BlockSpec(blk, lambda r, where, at=at: (where[0],) + at(r)))
        args.append(p)
        for t in (1, 2, 3):
            in_specs.append(pl.BlockSpec(blk, lambda r, where, t=t, at=at: (where[t],) + at(r)))
            args.append(a)
        out_specs.append(pl.BlockSpec(blk, lambda r, where, at=at: (where[4],) + at(r)))
        out_shape.append(jax.ShapeDtypeStruct((2,) + a.shape[1:], F32))
        nbytes += 4 * _nbytes(blk, F32)
    return pl.pallas_call(
        body, name="grad_chip_sum", out_shape=tuple(out_shape),
        grid_spec=pltpu.PrefetchScalarGridSpec(num_scalar_prefetch=1, grid=(2,), in_specs=in_specs,
                                               out_specs=tuple(out_specs)),
        compiler_params=_params(("parallel",), nbytes),
    )(where, *args)


def _join_halves(ss):
    n = len(ss)

    def body(*refs):
        outs, send, recv = refs[n:2 * n], refs[2 * n], refs[2 * n + 1]
        x, y, c, _ = _place()
        cps = []
        for w in range(n):
            cps.append(pltpu.make_async_remote_copy(
                src_ref=outs[w].at[c], dst_ref=outs[w].at[c], send_sem=send.at[w], recv_sem=recv.at[w],
                device_id=(x, y, 1 - c), device_id_type=MESH))
        for cp in cps:
            cp.start()
        for w in range(n):
            got = outs[w].at[1 - c]
            pltpu.make_async_remote_copy(src_ref=got, dst_ref=got, send_sem=send.at[w], recv_sem=recv.at[w],
                                         device_id=(x, y, 1 - c), device_id_type=MESH).wait_recv()
        for cp in cps:
            cp.wait_send()

    dma = lambda k: pltpu.SemaphoreType.DMA((k,))
    return pl.pallas_call(
        body, name="grad_join_halves",
        out_shape=tuple(jax.ShapeDtypeStruct(s.shape, s.dtype) for s in ss),
        in_specs=[_ANY] * n, out_specs=tuple([_ANY] * n), input_output_aliases={w: w for w in range(n)},
        scratch_shapes=[dma(n), dma(n)],
    )(*ss)


def _rot_cols(w, axis=-1):
    a, b = jnp.split(w, 2, axis=axis)
    return jnp.concatenate([-b, a], axis=axis)


def _rot_cols_t(g, axis=-1):
    a, b = jnp.split(g, 2, axis=axis)
    return jnp.concatenate([b, -a], axis=axis)


def _cols_from_chips(a):
    n, r, cs = a.shape
    return jnp.transpose(a, (1, 0, 2)).reshape(r, n * cs)


def _cols_to_chips(a):
    r, cc = a.shape
    return jnp.transpose(a.reshape(r, N_CHIPS, cc // N_CHIPS), (1, 0, 2))


def _conv_w_split(cw):
    return jnp.swapaxes(cw.reshape(3, 2, D_FF), 0, 1)


def _conv_w_join(g):
    return jnp.swapaxes(g, 0, 1).reshape(3, 2 * D_FF)


_SEG =(D_MODEL, 2 * D_MODEL, 2 * D_MODEL + Q_RANK, 2 * D_MODEL + Q_RANK + KV_RANK, 2 * D_MODEL + Q_RANK + KV_RANK + ROPE,
        3 * D_MODEL + Q_RANK + KV_RANK + ROPE)


def _w_in_t_to_pad(wt):
    u, v, cq, ckv, kr, ga, gb = jnp.split(wt, _SEG, axis=0)
    return jnp.concatenate([u, v, ga, gb, cq, ckv, kr, _rot_cols(kr, axis=0)], axis=0)


def _w_in_t_from_pad(gt):
    u, v, ga, gb, cq, ckv, kr, krr = jnp.split(
        gt, (D_MODEL, 2 * D_MODEL, 3 * D_MODEL, 4 * D_MODEL, 4 * D_MODEL + Q_RANK, 4 * D_MODEL + Q_RANK + KV_RANK,
             4 * D_MODEL + Q_RANK + KV_RANK + ROPE), axis=0)
    return jnp.concatenate([u, v, cq, ckv, kr + _rot_cols_t(krr, axis=0), ga, gb], axis=0)


def _w_uq_to_pad(w):
    t = w.reshape(Q_RANK, HEADS, QK_DIM)
    nope, rope = t[..., :NOPE], t[..., NOPE:]
    return jnp.concatenate([nope, rope, _rot_cols(rope)], axis=-1).reshape(Q_RANK, HEADS * HEAD_PAD)


def _w_uq_from_pad(g):
    t = g.reshape(Q_RANK, HEADS, HEAD_PAD)
    nope, rope, rot = t[..., :NOPE], t[..., NOPE:QK_DIM], t[..., QK_DIM:]
    return jnp.concatenate([nope, rope + _rot_cols_t(rot)], axis=-1).reshape(Q_RANK, HEADS * QK_DIM)


def _w_ukv_to_pad(w):
    t = w.reshape(KV_RANK, HEADS, 2, NOPE)
    return jnp.swapaxes(t, 1, 2).reshape(KV_RANK, 2 * HEADS * NOPE)


def _w_ukv_from_pad(g):
    t = g.reshape(KV_RANK, 2, HEADS, NOPE)
    return jnp.swapaxes(t, 1, 2).reshape(KV_RANK, 2 * HEADS * NOPE)


def _rope_tables(positions):
    inv_freq = 1.0 / (ROPE_THETA ** (jnp.arange(0, ROPE, 2, dtype=F32) / ROPE))
    ang = positions.astype(F32).reshape(-1, 1) * inv_freq
    cos, sin = jnp.cos(ang), jnp.sin(ang)
    zero = jnp.zeros((ang.shape[0], 64), F32)
    return jnp.concatenate([cos, cos, zero], axis=1), jnp.concatenate([sin, sin, zero], axis=1)


_BIG = ("w_in", "w_uq", "w_ukv", "w_out", "w_up", "w_down")
UP_SHARD = 2 * D_FF // N_CHIPS
TOKEN_TILE = 1024


def _local_step(x, positions, tgt, wts, in_weights, mixer_weights, ffn_weights, on_ffn_grads, on_mixer_grads):
    B, S, D = x.shape
    T = B * S
    xf = x.reshape(T, D)
    cos_a, sin_a = _rope_tables(positions)
    bs_t = jnp.pad(wts["a_spatial_b"].T, ((0, 0), (0, 128 - A_GROUPS)))

    h = _rms_fwd(xf, wts["mix_norm"], "norm1_fwd")
    wts = dict(wts)
    wts["w_in"], token = in_weights([h, cos_a, sin_a])
    tm = min(TOKEN_TILE, T)
    z = _mm(h, wts["w_in"], "nt", "in_proj", tm=tm, tn=1536, tk=D, n_outer=True, after=token)
    wts["w_q"], wts["w_kv"], wts["w_out"] = mixer_weights(z)
    q, k, v, cqn, ckvn = _lat_fwd(z, wts["q_a_norm"], wts["kv_a_norm"], wts["w_q"], wts["w_kv"], cos_a, sin_a)
    yb, *lses = _attn_fwd(q, k, v, B, S)
    merged = _mix_fwd(z, yb, wts["a_v_norm_g"], wts["a_v_norm_b"], wts["a_spatial_w"], bs_t)
    x1 = _mm(merged, wts["w_out"], "nn", "out_proj", tm=min(512, T), tn=D, tk=D, add=xf)
    h2 = _rms_fwd(x1, wts["ffn_norm"], "norm2_fwd")
    wts["w_up"], wts["w_down"], wts["conv_w"] = ffn_weights(h2)
    up_pre = _mm(h2, wts["w_up"], "nn", "up_proj", tm=tm, tn=UP_SHARD, tk=D, dims=(T, 2 * D_FF, D),
                 b_spec=pl.BlockSpec((None, D, UP_SHARD), lambda i, j, k: (j, 0, 0)),
                 o_spec=pl.BlockSpec((None, tm, UP_SHARD), lambda i, j, k: (j // 2, i, j % 2)), out_shape=(2, T, D_FF),
                 n_outer=True)
    act, up_conv = _gate_fwd(up_pre, wts["conv_w"], wts["conv_b"], B, S)
    x2 = _mm(act, wts["w_down"], "nn", "down_proj", tm=tm, tn=D, tk=1408, add=x1)
    dx2, loss_row, g_final = _final(x2, tgt.reshape(T, D), wts["final_norm"])

    g = {"final_norm": g_final}
    dact = _mm(dx2, wts["w_down"], "nt", "down_proj_dx", tm=tm, tn=1408, tk=D, n_outer=True)
    tk2, tk1 = min(2048, T), min(1024, T)
    g["w_down"], g["w_down_lo"] = _mm(act, dx2, "tn", "down_proj_dw", tm=1408, tn=D, tk=tk1, copy_dtype=GRAD_PAYLOAD)
    dup, g["conv_w"], g["conv_b"] = _gate_bwd(up_pre, up_conv, dact, wts["conv_w"], B, S)
    g["w_up"], g["w_up_lo"] = _mm(
        h2, dup, "tn", "up_proj_dw", tm=D, tn=UP_SHARD, tk=tk2, dims=(D, 2 * D_FF, T), copy_dtype=GRAD_PAYLOAD,
        b_spec=pl.BlockSpec((None, tk2, UP_SHARD), lambda i, j, k: (j // 2, k, j % 2)),
        o_spec=pl.BlockSpec((None, D, UP_SHARD), lambda i, j, k: (j, 0, 0)), out_shape=(N_CHIPS, D, UP_SHARD))
    token, ffn_sent = on_ffn_grads(g)
    dh2 = _mm(dup, wts["w_up"], "nt", "up_proj_dx", tm=tm, tn=D, tk=UP_SHARD, dims=(T, D, 2 * D_FF), after=token,
              a_spec=pl.BlockSpec((None, tm, UP_SHARD), lambda i, j, k: (k // 2, i, k % 2)),
              b_spec=pl.BlockSpec((None, D, UP_SHARD), lambda i, j, k: (k, 0, 0)))
    token = ffn_sent(dh2)
    dx1, g["ffn_norm"] = _rms_bwd(x1, wts["ffn_norm"], dh2, dx2, "norm2_bwd")
    dm = _mm(dx1, wts["w_out"], "nt", "out_proj_dx", tm=min(512, T), tn=D, tk=D, after=token)
    g["w_out"], g["w_out_lo"] = _mm(merged, dx1, "tn", "out_proj_dw", tm=D, tn=D, tk=tk1, copy_dtype=GRAD_PAYLOAD)
    dz, dyb, dl, g["a_spatial_w"], gbs, g["a_v_norm_g"], g["a_v_norm_b"] = _mix_bwd(
        z, yb, dm, wts["a_v_norm_g"], wts["a_v_norm_b"], wts["a_spatial_w"], bs_t)
    g["a_spatial_b"] = gbs[:, :A_GROUPS].T
    delta = dl.reshape(HEADS * T // ATT_BLOCK, 1, ATT_BLOCK)
    dq, dk, dv = _attn_bwd(q, k, v, dyb, lses, delta, B, S)
    dz, dq_raw, dkv, g["q_a_norm"], g["kv_a_norm"] = _lat_bwd(
        dz, z, dq, dk, dv, wts["q_a_norm"], wts["kv_a_norm"], wts["w_q"], wts["w_kv"], cos_a, sin_a)
    g["w_q"] = _mm(cqn, dq_raw, "tn", "q_proj_dw", tm=Q_RANK, tn=HEADS * HEAD_PAD, tk=tk2)
    g["w_kv"] = _mm(ckvn, dkv, "tn", "kv_proj_dw", tm=KV_RANK, tn=2 * HEADS * NOPE, tk=tk2)
    g["w_in"] = _mm(dz, h, "tn", "in_proj_dw", tm=1536, tn=D, tk=tk2)
    token = on_mixer_grads(g)
    dh = _mm(dz, wts["w_in"], "nn", "in_proj_dx", tm=tm, tn=D, tk=1536, after=token)
    dx, g["mix_norm"] = _rms_bwd(xf, wts["mix_norm"], dh, dx1, "norm1_bwd")
    return loss_row[0, 0], dx.reshape(B, S, D), g


_SMALL = (("mix_norm", (1, D_MODEL)), ("a_v_norm_g", (1, D_MODEL)), ("a_v_norm_b", (1, D_MODEL)),
          ("a_spatial_w", (A_GROUPS * CHUNK, CHUNK)), ("a_spatial_b", (1, A_GROUPS * CHUNK)), ("q_a_norm", (1, Q_RANK)),
          ("kv_a_norm", (1, KV_RANK)), ("ffn_norm", (1, D_MODEL)), ("conv_b", (1, 2 * D_FF)), ("final_norm", (1, D_MODEL)),
          ("conv_w", (3, 2 * D_FF)))
_SMALL_SIZE = sum(math.prod(s) for _, s in _SMALL)
_SMALL_ROWS = -(-(_SMALL_SIZE + 1) // (128 * 8)) * 8


def kernel(x, positions, mix_norm, w_in, a_v_norm_g, a_v_norm_b, a_spatial_w, a_spatial_b, q_a_norm, w_uq, kv_a_norm, w_ukv, w_out, ffn_norm, w_up, conv_w, conv_b, w_down, final_norm, loss_target, m_mix_norm, m_w_in, m_a_v_norm_g, m_a_v_norm_b, m_a_spatial_w, m_a_spatial_b, m_q_a_norm, m_w_uq, m_kv_a_norm, m_w_ukv, m_w_out, m_ffn_norm, m_w_up, m_conv_w, m_conv_b, m_w_down, m_final_norm, v_mix_norm, v_w_in, v_a_v_norm_g, v_a_v_norm_b, v_a_spatial_w, v_a_spatial_b, v_q_a_norm, v_w_uq, v_kv_a_norm, v_w_ukv, v_w_out, v_ffn_norm, v_w_up, v_conv_w, v_conv_b, v_w_down, v_final_norm):
    weights = dict(mix_norm=mix_norm, w_in=w_in, a_v_norm_g=a_v_norm_g, a_v_norm_b=a_v_norm_b, a_spatial_w=a_spatial_w,
                   a_spatial_b=a_spatial_b, q_a_norm=q_a_norm, w_uq=w_uq, kv_a_norm=kv_a_norm, w_ukv=w_ukv, w_out=w_out,
                   ffn_norm=ffn_norm, w_up=w_up, conv_w=conv_w, conv_b=conv_b, w_down=w_down, final_norm=final_norm)
    m_in = dict(mix_norm=m_mix_norm, w_in=m_w_in, a_v_norm_g=m_a_v_norm_g, a_v_norm_b=m_a_v_norm_b,
                a_spatial_w=m_a_spatial_w, a_spatial_b=m_a_spatial_b, q_a_norm=m_q_a_norm, w_uq=m_w_uq,
                kv_a_norm=m_kv_a_norm, w_ukv=m_w_ukv, w_out=m_w_out, ffn_norm=m_ffn_norm, w_up=m_w_up, conv_w=m_conv_w,
                conv_b=m_conv_b, w_down=m_w_down, final_norm=m_final_norm)
    v_in = dict(mix_norm=v_mix_norm, w_in=v_w_in, a_v_norm_g=v_a_v_norm_g, a_v_norm_b=v_a_v_norm_b,
                a_spatial_w=v_a_spatial_w, a_spatial_b=v_a_spatial_b, q_a_norm=v_q_a_norm, w_uq=v_w_uq,
                kv_a_norm=v_kv_a_norm, w_ukv=v_w_ukv, w_out=v_w_out, ffn_norm=v_ffn_norm, w_up=v_w_up, conv_w=v_conv_w,
                conv_b=v_conv_b, w_down=v_w_down, final_norm=v_final_norm)
    names = list(weights)
    chip = 2 * lax.axis_index("x") + lax.axis_index("y")

    def halves(a):
        return a.reshape(a.shape[:-2] + (2, a.shape[-2] // 2, a.shape[-1]))

    w_in_t = jnp.swapaxes(w_in[0], 0, 1).astype(MXU_DTYPE)
    w_in_gather = _exchange_start([jnp.stack(jnp.split(w_in_t, 2, axis=1))], "w_in_gather_start", "halves",
                                  after=positions)
    gathers = {}
    wts = dict(
        mix_norm=mix_norm, a_v_norm_g=a_v_norm_g, a_v_norm_b=a_v_norm_b, a_spatial_w=a_spatial_w[0],
        a_spatial_b=a_spatial_b[0], q_a_norm=q_a_norm, kv_a_norm=kv_a_norm, ffn_norm=ffn_norm,
        final_norm=final_norm.reshape(1, D_MODEL), conv_b=conv_b.reshape(2, 1, D_FF))

    mixer_shards = [weights[n][0].astype(MXU_DTYPE) for n in _BIG[1:4]]
    ffn_shards = [w_up[0].astype(MXU_DTYPE), w_down[0].astype(MXU_DTYPE)]

    def in_weights(after):
        _, landed = _exchange_wait(w_in_gather, "w_in_gather_wait", "halves", list(after) + mixer_shards + ffn_shards)
        (w_in_sh,) = _forward_halves(list(landed))
        gathers["mixer"] = _exchange_start(mixer_shards, "mixer_gather_start", "gather", after=w_in_sh)
        gathers["ffn"] = _exchange_start(ffn_shards + [conv_w[0]], "ffn_gather_start", "gather",
                                         after=gathers["mixer"][4])
        w_in_pad = _w_in_t_to_pad(jnp.concatenate([w_in_sh[:, 0], w_in_sh[:, 1]], axis=-1).reshape(-1, D_MODEL))
        return w_in_pad, gathers["ffn"][4]

    def mixer_weights(after):
        _, (w_uq_sh, w_ukv_sh, w_out_sh) = _exchange_wait(gathers["mixer"], "mixer_gather_wait", "gather", after)
        return (_w_uq_to_pad(_cols_from_chips(w_uq_sh)), _w_ukv_to_pad(_cols_from_chips(w_ukv_sh)),
                w_out_sh.reshape(D_MODEL, D_MODEL))

    def ffn_weights(after):
        _, (w_up_sh, w_down_sh, cw_all) = _exchange_wait(gathers["ffn"], "ffn_gather_wait", "gather", after)
        return w_up_sh, w_down_sh.reshape(D_FF, D_MODEL), _conv_w_split(_cols_from_chips(cw_all))

    scatters = {}

    def start_scatter(slabs, slabs_lo, tag):
        got = _swap_halves([halves(s) for s in slabs_lo], tag + "_grad_swap_halves")
        sums = _pair_sum(slabs, got, tag + "_grad_pair_sum")
        scatters[tag] = _exchange_start(list(sums), tag + "_scatter_start", "scatter", after=slabs[-1])
        return scatters[tag][4]

    def on_ffn_grads(g):
        slabs, slabs_lo = [[g["w_up" + lo], g["w_down" + lo].reshape(N_CHIPS, D_FF // N_CHIPS, D_MODEL)]
                           for lo in ("", "_lo")]
        swap = _exchange_start([halves(s) for s in slabs_lo], "ffn_swap_start", "swap", after=slabs[1])

        def sent(after):
            _, got = _exchange_wait(swap, "ffn_swap_wait", "swap", after)
            sums = _pair_sum(slabs, got, "ffn_grad_pair_sum")
            scatters["ffn"] = _exchange_start(list(sums), "ffn_scatter_start", "scatter", after=got[0])
            return scatters["ffn"][4]

        return swap[4], sent

    def on_mixer_grads(g):
        slabs = [_w_in_t_from_pad(g["w_in"]).reshape(N_CHIPS, -1, D_MODEL), _cols_to_chips(_w_uq_from_pad(g["w_q"])),
                 _cols_to_chips(_w_ukv_from_pad(g["w_kv"]))]
        w_out_slabs = [g["w_out" + lo].reshape(N_CHIPS, D_MODEL // N_CHIPS, D_MODEL) for lo in ("", "_lo")]
        return start_scatter(slabs + w_out_slabs[:1], [s.astype(GRAD_PAYLOAD) for s in slabs] + w_out_slabs[1:], "mixer")

    loss_part, grad_x, g = _local_step(x, positions, loss_target, wts, in_weights, mixer_weights, ffn_weights,
                                       on_ffn_grads, on_mixer_grads)

    g_small_parts = dict(g)
    g_small_parts["conv_w"] = _conv_w_join(g["conv_w"])
    g_small_parts["conv_b"] = g["conv_b"].reshape(1, 2 * D_FF)
    flat = jnp.concatenate([g_small_parts[n].reshape(-1) for n, _ in _SMALL] + [loss_part.reshape(1)])
    flat = jnp.pad(flat, (0, _SMALL_ROWS * 128 - flat.shape[0])).reshape(_SMALL_ROWS, 128)
    small_gather = _exchange_start([flat], "small_gather_start", "all", after=grad_x)

    mixer_sums, mixer_landed = _exchange_wait(scatters["mixer"], "mixer_scatter_wait", "scatter", after=small_gather[4])
    ffn_sums, ffn_landed = _exchange_wait(scatters["ffn"], "ffn_scatter_wait", "scatter", after=mixer_landed[0])
    reduced = _chip_sum(list(mixer_sums) + list(ffn_sums), list(mixer_landed) + list(ffn_landed))
    g_big = dict(zip(_BIG, _join_halves(reduced)))

    grads, deltas, new_m, new_v = {}, {}, {}, {}

    def update(n, grad, copy_grad=False):
        w = weights[n]
        shape2 = grad.shape
        d, nm, nv, *again = _adamw(w.reshape(shape2), grad, m_in[n].reshape(shape2), v_in[n].reshape(shape2),
                                   "adamw_" + n, copy_grad)
        grads[n], deltas[n], new_m[n], new_v[n] = (t.reshape(w.shape) for t in (again[0] if copy_grad else grad, d, nm, nv))

    def update_transposed(n, grad_t):
        t = lambda a: jnp.swapaxes(a, 1, 2)
        d, nm, nv, again = _adamw(t(weights[n]), grad_t, t(m_in[n]), t(v_in[n]), "adamw_" + n, True)
        grads[n], deltas[n], new_m[n], new_v[n] = t(again), t(d), t(nm), t(nv)

    for n in _BIG:
        g3 = g_big[n].reshape((1, -1, g_big[n].shape[-1]))
        if n == "w_in":
            update_transposed(n, g3)
        else:
            update(n, g3, copy_grad=True)

    (own,), (everyone,) = _exchange_wait(small_gather, "small_gather_wait", "all", after=[deltas[n] for n in _BIG])
    device = 2 * chip + lax.axis_index("c")
    everyone = lax.dynamic_update_slice(everyone, own[None], (device, 0, 0))
    total = _sum_slabs([everyone[j] for j in range(8)], "small_grads_sum", tr=_SMALL_ROWS).reshape(-1)
    o = 0
    for n, shp in _SMALL:
        piece = total[o:o + math.prod(shp)].reshape(shp)
        o += math.prod(shp)
        if n == "conv_w":
            piece = lax.dynamic_slice_in_dim(piece, chip * UP_SHARD, UP_SHARD, axis=1)
        update(n, piece)
    loss = total[_SMALL_SIZE]
    return (loss, grad_x, *[grads[n] for n in names], *[deltas[n] for n in names], *[new_m[n] for n in names],
            *[new_v[n] for n in names])
```

```python
import functools
import math

import jax
import jax.numpy as jnp
from jax import lax
from jax.experimental import pallas as pl
from jax.experimental.pallas import tpu as pltpu

F32 = jnp.float32
MXU_DTYPE = jnp.bfloat16
MESH = pl.DeviceIdType.MESH

D_MODEL = 1024
EPS = 1e-6
A_GROUPS = 8
CHUNK = 128
HEADS = 8
NOPE = 128
ROPE = 64
QK_DIM = NOPE + ROPE
HEAD_PAD = 256
Q_RANK = 256
KV_RANK = 128
ROPE_THETA = 10000.0
D_FF = 2816
FF_TILE = 256
N_FF_TILES = D_FF // FF_TILE
LAT = 512
IN_PAD = 4 * D_MODEL + LAT
N_CHIPS = 4
ADAM_LR, ADAM_B1, ADAM_B2, ADAM_EPS, ADAM_WD, ADAM_STEP = 0.001, 0.9, 0.999, 1e-08, 0.01, 10

VMEM_CAP_V7X = 64 * 1024 * 1024
NEG = -1e30


def _params(sem, nbytes):
    limit = int(min(VMEM_CAP_V7X - (8 << 20), max(32 << 20, 3 * nbytes)))
    return pltpu.CompilerParams(dimension_semantics=sem, vmem_limit_bytes=limit)


def _nbytes(shape, dtype):
    return math.prod(shape) * jnp.dtype(dtype).itemsize


_DIMS = {"nn": (((1,), (0,)), ((), ())), "nt": (((1,), (1,)), ((), ())), "tn": (((0,), (0,)), ((), ()))}


def _mm(a, b, mode, name, *, tm, tn, tk, out_dtype=F32, add=None, dims=None, a_spec=None, b_spec=None,
        o_spec=None, out_shape=None, n_outer=False, copy_dtype=None, after=None):
    if dims is None:
        if mode == "nn":
            (M, K), (_, N) = a.shape, b.shape
        elif mode == "nt":
            (M, K), (N, _) = a.shape, b.shape
        else:
            (K, M), (_, N) = a.shape, b.shape
    else:
        M, N, K = dims
    a_blk = (tk, tm) if mode == "tn" else (tm, tk)
    b_blk = (tn, tk) if mode == "nt" else (tk, tn)
    if a_spec is None:
        a_spec = pl.BlockSpec(a_blk, (lambda i, j, k: (k, i)) if mode == "tn" else (lambda i, j, k: (i, k)))
    if b_spec is None:
        b_spec = pl.BlockSpec(b_blk, (lambda i, j, k: (j, k)) if mode == "nt" else (lambda i, j, k: (k, j)))
    if o_spec is None:
        o_spec = pl.BlockSpec((tm, tn), lambda i, j, k: (i, j))
    if out_shape is None:
        out_shape = (M, N)
    assert M % tm == 0 and N % tn == 0 and K % tk == 0, (name, M, N, K, tm, tn, tk)
    nk = K // tk
    contract = _DIMS[mode]
    has_add = add is not None

    def body(*refs):
        a_ref, b_ref = refs[0], refs[1]
        add_ref = refs[2] if has_add else None
        n_in = 2 + has_add + (after is not None)
        o_ref = refs[n_in]
        copy_ref = refs[n_in + 1] if copy_dtype is not None else None

        def product():
            return lax.dot_general(a_ref[...].astype(MXU_DTYPE), b_ref[...].astype(MXU_DTYPE), contract,
                                   preferred_element_type=F32)

        def finish(r):
            if has_add:
                r = r + add_ref[...]
            o_ref[...] = r.astype(out_dtype)
            if copy_ref is not None:
                copy_ref[...] = r.astype(copy_dtype)

        if nk == 1:
            finish(product())
            return
        acc = refs[-1]
        k = pl.program_id(2)

        @pl.when(k == 0)
        def _():
            acc[...] = jnp.zeros_like(acc)

        acc[...] += product()

        @pl.when(k == nk - 1)
        def _():
            finish(acc[...])

    in_specs = [a_spec, b_spec]
    args = [a, b]
    nbytes = _nbytes(a_blk, a.dtype) + _nbytes(b_blk, b.dtype) + 3 * _nbytes((tm, tn), F32)
    if has_add:
        in_specs.append(pl.BlockSpec((tm, tn), lambda i, j, k: (i, j)))
        args.append(add)
        nbytes += _nbytes((tm, tn), F32)
    if after is not None:
        in_specs.append(pl.BlockSpec(after.shape, lambda i, j, k: (0, 0)))
        args.append(after)
    grid = (M // tm, N // tn, nk)
    if n_outer:
        def swapped(spec):
            return pl.BlockSpec(spec.block_shape, lambda j, i, k, at=spec.index_map: at(i, j, k))

        grid = (N // tn, M // tm, nk)
        in_specs = [swapped(s) for s in in_specs]
        o_spec = swapped(o_spec)
    out_sds, out_specs = jax.ShapeDtypeStruct(out_shape, out_dtype), o_spec
    if copy_dtype is not None:
        out_sds, out_specs = (out_sds, jax.ShapeDtypeStruct(out_shape, copy_dtype)), (o_spec, o_spec)
    return pl.pallas_call(
        body, name=name, out_shape=out_sds, grid=grid, in_specs=in_specs, out_specs=out_specs,
        scratch_shapes=[pltpu.VMEM((tm, tn), F32)] if nk > 1 else [],
        compiler_params=_params(("parallel", "parallel", "arbitrary"), nbytes),
    )(*args)


_GELU_C = math.sqrt(2.0 / math.pi)
_GELU_A = 0.044715


def _sigmoid(x):
    return 0.5 * jnp.tanh(0.5 * x) + 0.5


def _gelu(x):
    t = jnp.tanh(x * (_GELU_C + (_GELU_C * _GELU_A) * (x * x)))
    return x * (0.5 + 0.5 * t)


def _gelu_and_grad(x):
    x2 = x * x
    t = jnp.tanh(x * (_GELU_C + (_GELU_C * _GELU_A) * x2))
    cdf = 0.5 + 0.5 * t
    grad = cdf + (0.5 * x) * (1.0 - t * t) * (_GELU_C + (3.0 * _GELU_C * _GELU_A) * x2)
    return x * cdf, grad


def _rope_mix(g, cos_a, sin_a):
    return g * cos_a + pltpu.roll(g, 64, 1) * sin_a


def _rope_mix_bwd(d, cos_a, sin_a):
    return d * cos_a + pltpu.roll(d * sin_a, 64, 1)


def _rms_fwd(x, g, name, tr=1024):
    T, D = x.shape

    def body(x_ref, g_ref, h_ref):
        xv = x_ref[...]
        r = lax.rsqrt(jnp.mean(xv * xv, axis=-1, keepdims=True) + EPS)
        h_ref[...] = ((xv * r) * g_ref[...]).astype(h_ref.dtype)

    return pl.pallas_call(
        body, name=name, out_shape=jax.ShapeDtypeStruct((T, D), MXU_DTYPE), grid=(T // tr,),
        in_specs=[pl.BlockSpec((tr, D), lambda i: (i, 0)), pl.BlockSpec((1, D), lambda i: (0, 0))],
        out_specs=pl.BlockSpec((tr, D), lambda i: (i, 0)),
        compiler_params=_params(("parallel",), 3 * _nbytes((tr, D), F32)),
    )(x, g)


def _rms_bwd(x, g, dh, dres, name, tr=512):
    T, D = x.shape

    def body(x_ref, g_ref, dh_ref, dres_ref, dx_ref, gg_ref):
        @pl.when(pl.program_id(0) == 0)
        def _():
            gg_ref[...] = jnp.zeros_like(gg_ref)

        xv = x_ref[...]
        r = lax.rsqrt(jnp.mean(xv * xv, axis=-1, keepdims=True) + EPS)
        xn = xv * r
        dhv = dh_ref[...]
        dxn = dhv * g_ref[...]
        dx_ref[...] = dres_ref[...] + r * (dxn - xn * jnp.mean(dxn * xn, axis=-1, keepdims=True))
        gg_ref[...] += jnp.sum(dhv * xn, axis=0, keepdims=True)

    row = pl.BlockSpec((tr, D), lambda i: (i, 0))
    vec = pl.BlockSpec((1, D), lambda i: (0, 0))
    return pl.pallas_call(
        body, name=name,
        out_shape=(jax.ShapeDtypeStruct((T, D), F32), jax.ShapeDtypeStruct((1, D), F32)),
        grid=(T // tr,), in_specs=[row, vec, row, row], out_specs=(row, vec),
        compiler_params=_params(("arbitrary",), 6 * _nbytes((tr, D), F32)),
    )(x, g, dh, dres)


def _lat_fwd(z, gq, gkv, wq, wkv, cos_a, sin_a, tr=512):
    T = z.shape[0]
    lat_blk = (4 * D_MODEL) // LAT

    def body(z_ref, gq_ref, gkv_ref, wq_ref, wkv_ref, cos_ref, sin_ref, q_ref, k_ref, v_ref, cqn_ref, ckvn_ref):
        zl = z_ref[...]
        cos_v, sin_v = cos_ref[...], sin_ref[...]
        cq = zl[:, :Q_RANK]
        ckv = zl[:, Q_RANK:Q_RANK + KV_RANK]
        krb = zl[:, Q_RANK + KV_RANK:]
        cqn = ((cq * lax.rsqrt(jnp.mean(cq * cq, axis=-1, keepdims=True) + EPS)) * gq_ref[...]).astype(MXU_DTYPE)
        ckvn = ((ckv * lax.rsqrt(jnp.mean(ckv * ckv, axis=-1, keepdims=True) + EPS)) * gkv_ref[...]).astype(MXU_DTYPE)
        cqn_ref[...] = cqn
        ckvn_ref[...] = ckvn
        krr = _rope_mix(krb, cos_v, sin_v).astype(MXU_DTYPE)
        q = jnp.dot(cqn, wq_ref[...], preferred_element_type=F32)
        kv = jnp.dot(ckvn, wkv_ref[...], preferred_element_type=F32)
        for h in range(HEADS):
            o = h * HEAD_PAD
            q_ref[:, o:o + NOPE] = q[:, o:o + NOPE].astype(MXU_DTYPE)
            q_ref[:, o + NOPE:o + HEAD_PAD] = _rope_mix(q[:, o + NOPE:o + HEAD_PAD], cos_v, sin_v).astype(MXU_DTYPE)
            k_ref[:, o:o + NOPE] = kv[:, h * NOPE:(h + 1) * NOPE].astype(MXU_DTYPE)
            k_ref[:, o + NOPE:o + HEAD_PAD] = krr
        v_ref[...] = kv[:, HEADS * NOPE:].astype(MXU_DTYPE)

    def row(w):
        return pl.BlockSpec((tr, w), lambda i: (i, 0))

    def full(a):
        return pl.BlockSpec(a.shape, lambda i: (0, 0))

    return pl.pallas_call(
        body, name="lat_fwd",
        out_shape=(jax.ShapeDtypeStruct((T, HEADS * HEAD_PAD), MXU_DTYPE), jax.ShapeDtypeStruct((T, HEADS * HEAD_PAD), MXU_DTYPE),
                   jax.ShapeDtypeStruct((T, HEADS * NOPE), MXU_DTYPE), jax.ShapeDtypeStruct((T, Q_RANK), MXU_DTYPE),
                   jax.ShapeDtypeStruct((T, KV_RANK), MXU_DTYPE)),
        grid=(T // tr,),
        in_specs=[pl.BlockSpec((tr, LAT), lambda i: (i, lat_blk)), full(gq), full(gkv), full(wq), full(wkv), row(128), row(128)],
        out_specs=(row(HEADS * HEAD_PAD), row(HEADS * HEAD_PAD), row(HEADS * NOPE), row(Q_RANK), row(KV_RANK)),
        compiler_params=_params(("parallel",), 8 * _nbytes((tr, HEADS * HEAD_PAD), F32)),
    )(z, gq, gkv, wq, wkv, cos_a, sin_a)


ATT_BLOCK = 256
_SCALE = QK_DIM ** -0.5


def _causal_mask(n):
    return lax.broadcasted_iota(jnp.int32, (n, n), 1) <= lax.broadcasted_iota(jnp.int32, (n, n), 0)


def _causal_mask_t(n):
    return lax.broadcasted_iota(jnp.int32, (n, n), 0) <= lax.broadcasted_iota(jnp.int32, (n, n), 1)


ATT_HEADS = 4
ATT_HEADS_FWD = HEADS


def _attn_fwd(q, k, v, B, S):
    tq = ATT_BLOCK
    nq = S // tq
    T = B * S
    hp, groups = ATT_HEADS_FWD, HEADS // ATT_HEADS_FWD

    def body(q_ref, k_ref, v_ref, o_ref, *lse_refs):
        qi = pl.program_id(2)
        qs = [q_ref[:, t * HEAD_PAD:(t + 1) * HEAD_PAD] for t in range(hp)]

        def scores(j, t):
            rows = pl.ds(pl.multiple_of(j * tq, tq), tq)
            return lax.dot_general(k_ref[rows, t * HEAD_PAD:(t + 1) * HEAD_PAD], qs[t], _DIMS["nt"],
                                   preferred_element_type=F32)

        def step(j, carry, last):
            rows = pl.ds(pl.multiple_of(j * tq, tq), tq)
            out = []
            for t in range(hp):
                m, l, acc, st = carry[t]
                st_next = st if last else scores(j + 1, t)
                st = st * _SCALE
                if last:
                    st = jnp.where(_causal_mask_t(tq), st, NEG)
                m_new = jnp.maximum(m, jnp.max(st, axis=0, keepdims=True))
                alpha = jnp.exp(m - m_new)
                p = jnp.exp(st - m_new)
                l = alpha * l + jnp.sum(p, axis=0, keepdims=True)
                acc = alpha * acc + lax.dot_general(v_ref[rows, t * NOPE:(t + 1) * NOPE], p.astype(MXU_DTYPE),
                                                    _DIMS["tn"], preferred_element_type=F32)
                out.append((m_new, l, acc, st_next))
            return tuple(out)

        init = tuple((jnp.full((1, tq), NEG, F32), jnp.zeros((1, tq), F32), jnp.zeros((NOPE, tq), F32), scores(0, t))
                     for t in range(hp))
        carry = lax.fori_loop(0, qi, lambda j, c: step(j, c, False), init)
        carry = step(qi, carry, True)
        for t in range(hp):
            m, l, acc, _ = carry[t]
            o_ref[:, t * NOPE:(t + 1) * NOPE] = (acc / l).T
            lse_refs[t][0] = m + jnp.log(l)

    lse_sds = jax.ShapeDtypeStruct((groups * B * nq, 1, tq), F32)
    lse_spec = pl.BlockSpec((1, 1, tq), lambda b, h, i: ((h * B + b) * nq + i, 0, 0))
    return pl.pallas_call(
        body, name="attn_fwd",
        out_shape=(jax.ShapeDtypeStruct((T, HEADS * NOPE), F32),) + (lse_sds,) * hp,
        grid=(B, groups, nq),
        in_specs=[pl.BlockSpec((tq, hp * HEAD_PAD), lambda b, h, i: (b * nq + i, h)),
                  pl.BlockSpec((S, hp * HEAD_PAD), lambda b, h, i: (b, h)),
                  pl.BlockSpec((S, hp * NOPE), lambda b, h, i: (b, h))],
        out_specs=(pl.BlockSpec((tq, hp * NOPE), lambda b, h, i: (b * nq + i, h)),) + (lse_spec,) * hp,
        compiler_params=_params(("parallel", "parallel", "arbitrary"), 4 * hp * _nbytes((S, HEAD_PAD), MXU_DTYPE)),
    )(q, k, v)


def _attn_bwd(q, k, v, do, lses, delta, B, S):
    tq = ATT_BLOCK
    nq = S // tq
    T = B * S
    hp, groups = ATT_HEADS, HEADS // ATT_HEADS
    assert groups == 2 and len(lses) == HEADS

    def body(q_ref, k_ref, v_ref, do_ref, *refs):
        lse_refs, dl_refs = refs[:HEADS], refs[HEADS:HEADS + hp]
        dq_out, dk_ref, dv_ref, dq_ref = refs[HEADS + hp:]
        kj = pl.program_id(2)
        upper = pl.program_id(1) == 1

        def lse(t, i):
            return jnp.where(upper, lse_refs[hp + t][i], lse_refs[t][i])

        @pl.when(kj == 0)
        def _():
            dq_ref[...] = jnp.zeros_like(dq_ref)

        def products(i, t):
            rows = pl.ds(pl.multiple_of(i * tq, tq), tq)
            st = lax.dot_general(k_ref[:, t * HEAD_PAD:(t + 1) * HEAD_PAD], q_ref[rows, t * HEAD_PAD:(t + 1) * HEAD_PAD],
                                 _DIMS["nt"], preferred_element_type=F32)
            dpt = lax.dot_general(v_ref[:, t * NOPE:(t + 1) * NOPE], do_ref[rows, t * NOPE:(t + 1) * NOPE],
                                  _DIMS["nt"], preferred_element_type=F32)
            return st, dpt

        def step(i, carry, masked, prefetch):
            rows = pl.ds(i * tq if isinstance(i, int) else pl.multiple_of(i * tq, tq), tq)
            out = []
            for t in range(hp):
                dk, dv, st, dpt = carry[t]
                st_next, dpt_next = products(i + 1, t) if prefetch else (st, dpt)
                qk_cols = slice(t * HEAD_PAD, (t + 1) * HEAD_PAD)
                v_cols = slice(t * NOPE, (t + 1) * NOPE)
                p = jnp.exp(st * _SCALE - lse(t, i))
                if masked:
                    p = jnp.where(_causal_mask_t(tq), p, 0.0)
                dv = dv + jnp.dot(p.astype(MXU_DTYPE), do_ref[rows, v_cols], preferred_element_type=F32)
                ds = (p * (dpt - dl_refs[t][i]) * _SCALE).astype(MXU_DTYPE)
                dk = dk + jnp.dot(ds, q_ref[rows, qk_cols], preferred_element_type=F32)
                dq_ref[rows, qk_cols] += lax.dot_general(ds, k_ref[:, qk_cols], _DIMS["tn"], preferred_element_type=F32)
                out.append((dk, dv, st_next, dpt_next))
            return tuple(out)

        def first_products():
            return tuple((jnp.zeros((tq, HEAD_PAD), F32), jnp.zeros((tq, NOPE), F32)) + products(kj, t) for t in range(hp))

        def finish(carry):
            for t in range(hp):
                dk_ref[:, t * HEAD_PAD:(t + 1) * HEAD_PAD] = carry[t][0].astype(dk_ref.dtype)
                dv_ref[:, t * NOPE:(t + 1) * NOPE] = carry[t][1].astype(dv_ref.dtype)

        @pl.when(kj < nq - 1)
        def _():
            carry = step(kj, first_products(), True, True)
            carry = lax.fori_loop(kj + 1, nq - 1, lambda i, c: step(i, c, False, True), carry)
            finish(step(nq - 1, carry, False, False))

        @pl.when(kj == nq - 1)
        def _():
            finish(step(kj, first_products(), True, False))
            dq_out[...] = dq_ref[...].astype(dq_out.dtype)

    seq = lambda w: pl.BlockSpec((S, w), lambda b, h, j: (b, h))
    blk = lambda w: pl.BlockSpec((tq, w), lambda b, h, j: (b * nq + j, h))
    lse_spec = pl.BlockSpec((nq, 1, tq), lambda b, h, j: (b, 0, 0))
    dl_specs = [pl.BlockSpec((nq, 1, tq), lambda b, h, j, t=t: ((h * hp + t) * B + b, 0, 0)) for t in range(hp)]
    return pl.pallas_call(
        body, name="attn_bwd",
        out_shape=(jax.ShapeDtypeStruct((T, HEADS * HEAD_PAD), MXU_DTYPE), jax.ShapeDtypeStruct((T, HEADS * HEAD_PAD), MXU_DTYPE),
                   jax.ShapeDtypeStruct((T, HEADS * NOPE), MXU_DTYPE)),
        grid=(B, groups, nq),
        in_specs=[seq(hp * HEAD_PAD), blk(hp * HEAD_PAD), blk(hp * NOPE), seq(hp * NOPE)] + [lse_spec] * HEADS + dl_specs,
        out_specs=(seq(hp * HEAD_PAD), blk(hp * HEAD_PAD), blk(hp * NOPE)),
        scratch_shapes=[pltpu.VMEM((S, hp * HEAD_PAD), F32)],
        compiler_params=_params(("parallel", "parallel", "arbitrary"), 8 * hp * _nbytes((S, HEAD_PAD), F32)),
    )(q, k, v, do, *lses, *([delta] * hp))


MIX_ROWS = 256


def _tril_weights(ws_ref, g):
    return jnp.where(_causal_mask(CHUNK), ws_ref[g], 0.0).astype(MXU_DTYPE)


def _layer_norm_stats(va):
    mu = jnp.mean(va, axis=-1, keepdims=True)
    xc = va - mu
    rs = lax.rsqrt(jnp.mean(xc * xc, axis=-1, keepdims=True) + EPS)
    return xc * rs


def _mix_specs(tr):
    zcol = lambda c: pl.BlockSpec((tr, D_MODEL), lambda i, c=c: (i, c))
    row = pl.BlockSpec((tr, D_MODEL), lambda i: (i, 0))
    vec = pl.BlockSpec((1, D_MODEL), lambda i: (0, 0))
    ws = pl.BlockSpec((A_GROUPS, CHUNK, CHUNK), lambda i: (0, 0, 0))
    bs = pl.BlockSpec((CHUNK, 128), lambda i: (0, 0))
    return zcol, row, vec, ws, bs


def _mix_fwd(z, yb, ln_g, ln_b, ws, bs_t):
    T = z.shape[0]
    tr = MIX_ROWS
    zcol, row, vec, ws_spec, bs_spec = _mix_specs(tr)

    def body(zu_ref, zv_ref, zga_ref, zgb_ref, yb_ref, g_ref, b_ref, ws_ref, bs_ref, out_ref, vn_s):
        vhat = _layer_norm_stats(_gelu(zv_ref[...]))
        vn_s[...] = (vhat * g_ref[...] + b_ref[...]).astype(MXU_DTYPE)
        for g in range(A_GROUPS):
            w = _tril_weights(ws_ref, g)
            bias = bs_ref[:, g:g + 1]
            cols = slice(g * CHUNK, (g + 1) * CHUNK)
            for c in range(tr // CHUNK):
                rows = slice(c * CHUNK, (c + 1) * CHUNK)
                mixed = jnp.dot(w, vn_s[rows, cols], preferred_element_type=F32) + bias
                ya = _gelu(zu_ref[rows, cols]) * mixed
                merged = _sigmoid(zga_ref[rows, cols]) * ya + _sigmoid(zgb_ref[rows, cols]) * yb_ref[rows, cols]
                out_ref[rows, cols] = merged.astype(MXU_DTYPE)

    return pl.pallas_call(
        body, name="mix_fwd", out_shape=jax.ShapeDtypeStruct((T, D_MODEL), MXU_DTYPE), grid=(T // tr,),
        in_specs=[zcol(0), zcol(1), zcol(2), zcol(3), row, vec, vec, ws_spec, bs_spec], out_specs=row,
        scratch_shapes=[pltpu.VMEM((tr, D_MODEL), MXU_DTYPE)],
        compiler_params=_params(("parallel",), 8 * _nbytes((tr, D_MODEL), F32)),
    )(z, z, z, z, yb, ln_g, ln_b, ws, bs_t)


def _mix_bwd(z, yb, dm, ln_g, ln_b, ws, bs_t):
    T = z.shape[0]
    tr = MIX_ROWS
    zcol, row, vec, ws_spec, bs_spec = _mix_specs(tr)

    def body(zu_ref, zv_ref, zga_ref, zgb_ref, yb_ref, dm_ref, g_ref, b_ref, ws_ref, bs_ref,
             dz_ref, dyb_ref, dl_ref, gws_ref, gbs_ref, glg_ref, glb_ref, vn_s, dvn_s):
        @pl.when(pl.program_id(0) == 0)
        def _():
            gws_ref[...] = jnp.zeros_like(gws_ref)
            gbs_ref[...] = jnp.zeros_like(gbs_ref)
            glg_ref[...] = jnp.zeros_like(glg_ref)
            glb_ref[...] = jnp.zeros_like(glb_ref)

        lane = lax.broadcasted_iota(jnp.int32, (CHUNK, 128), 1)
        va, dgelu_v = _gelu_and_grad(zv_ref[...])
        mu = jnp.mean(va, axis=-1, keepdims=True)
        xc = va - mu
        rs = lax.rsqrt(jnp.mean(xc * xc, axis=-1, keepdims=True) + EPS)
        vhat = xc * rs
        vn_s[...] = (vhat * g_ref[...] + b_ref[...]).astype(MXU_DTYPE)
        gbs_acc = jnp.zeros((CHUNK, 128), F32)
        for g in range(A_GROUPS):
            w = _tril_weights(ws_ref, g)
            bias = bs_ref[:, g:g + 1]
            cols = slice(g * CHUNK, (g + 1) * CHUNK)
            gw_acc = jnp.zeros((CHUNK, CHUNK), F32)
            for c in range(tr // CHUNK):
                rows = slice(c * CHUNK, (c + 1) * CHUNK)
                vn = vn_s[rows, cols]
                mixed = jnp.dot(w, vn, preferred_element_type=F32) + bias
                ua, dgelu_u = _gelu_and_grad(zu_ref[rows, cols])
                dmv = dm_ref[rows, cols]
                sa = _sigmoid(zga_ref[rows, cols])
                dya = dmv * sa
                dz_ref[rows, 2 * D_MODEL + g * CHUNK:2 * D_MODEL + (g + 1) * CHUNK] = (
                    dmv * (ua * mixed) * (sa * (1.0 - sa))).astype(dz_ref.dtype)
                dz_ref[rows, cols] = (dya * mixed * dgelu_u).astype(dz_ref.dtype)
                dmix = dya * ua
                gbs_acc = gbs_acc + jnp.where(lane == g, jnp.sum(dmix, axis=-1, keepdims=True), 0.0)
                dmix_b = dmix.astype(MXU_DTYPE)
                gw_acc = gw_acc + lax.dot_general(dmix_b, vn, _DIMS["nt"], preferred_element_type=F32)
                dvn_s[rows, cols] = lax.dot_general(w, dmix_b, _DIMS["tn"], preferred_element_type=F32)
            gws_ref[g] += jnp.where(_causal_mask(CHUNK), gw_acc, 0.0)
        gbs_ref[...] += gbs_acc

        dvn = dvn_s[...]
        glg_ref[...] += jnp.sum(dvn * vhat, axis=0, keepdims=True)
        glb_ref[...] += jnp.sum(dvn, axis=0, keepdims=True)
        dvh = dvn * g_ref[...]
        dva = rs * (dvh - jnp.mean(dvh, axis=-1, keepdims=True) - vhat * jnp.mean(dvh * vhat, axis=-1, keepdims=True))
        dz_ref[:, D_MODEL:2 * D_MODEL] = (dva * dgelu_v).astype(dz_ref.dtype)

        dmv = dm_ref[...]
        ybv = yb_ref[...]
        sb = _sigmoid(zgb_ref[...])
        dyb = dmv * sb
        dyb_ref[...] = dyb.astype(dyb_ref.dtype)
        dz_ref[:, 3 * D_MODEL:4 * D_MODEL] = (dmv * ybv * (sb * (1.0 - sb))).astype(dz_ref.dtype)
        dz_ref[:, 4 * D_MODEL:] = jnp.zeros((tr, LAT), dz_ref.dtype)
        prod = dyb * ybv
        sel = (lax.broadcasted_iota(jnp.int32, (HEADS, D_MODEL), 1) // NOPE
               == lax.broadcasted_iota(jnp.int32, (HEADS, D_MODEL), 0)).astype(jnp.bfloat16)
        hi = prod.astype(jnp.bfloat16)
        rest = prod - hi.astype(F32)
        mid = rest.astype(jnp.bfloat16)
        lo = (rest - mid.astype(F32)).astype(jnp.bfloat16)
        dl_ref[...] = (lax.dot_general(sel, hi, _DIMS["nt"], preferred_element_type=F32)
                       + lax.dot_general(sel, mid, _DIMS["nt"], preferred_element_type=F32)
                       + lax.dot_general(sel, lo, _DIMS["nt"], preferred_element_type=F32))

    return pl.pallas_call(
        body, name="mix_bwd",
        out_shape=(jax.ShapeDtypeStruct((T, IN_PAD), MXU_DTYPE), jax.ShapeDtypeStruct((T, D_MODEL), MXU_DTYPE),
                   jax.ShapeDtypeStruct((HEADS, T), F32), jax.ShapeDtypeStruct((A_GROUPS, CHUNK, CHUNK), F32),
                   jax.ShapeDtypeStruct((CHUNK, 128), F32), jax.ShapeDtypeStruct((1, D_MODEL), F32),
                   jax.ShapeDtypeStruct((1, D_MODEL), F32)),
        grid=(T // tr,),
        in_specs=[zcol(0), zcol(1), zcol(2), zcol(3), row, row, vec, vec, ws_spec, bs_spec],
        out_specs=(pl.BlockSpec((tr, IN_PAD), lambda i: (i, 0)), row, pl.BlockSpec((HEADS, tr), lambda i: (0, i)),
                   ws_spec, bs_spec, vec, vec),
        scratch_shapes=[pltpu.VMEM((tr, D_MODEL), MXU_DTYPE), pltpu.VMEM((tr, D_MODEL), F32)],
        compiler_params=_params(("arbitrary",), 12 * _nbytes((tr, D_MODEL), F32)),
    )(z, z, z, z, yb, dm, ln_g, ln_b, ws, bs_t)


def _lat_bwd(dz, z, dq, dk, dv, gq, gkv, wq, wkv, cos_a, sin_a, tr=256):
    T = z.shape[0]
    lat_blk = (4 * D_MODEL) // LAT

    def body(dz_in, z_ref, dq_ref, dk_ref, dv_ref, gq_ref, gkv_ref, wq_ref, wkv_ref, cos_ref, sin_ref,
             dz_ref, dqr_ref, dkv_ref, ggq_ref, ggkv_ref):
        del dz_in

        @pl.when(pl.program_id(0) == 0)
        def _():
            ggq_ref[...] = jnp.zeros_like(ggq_ref)
            ggkv_ref[...] = jnp.zeros_like(ggkv_ref)

        cos_v, sin_v = cos_ref[...], sin_ref[...]
        dkr = jnp.zeros((tr, 128), F32)
        for h in range(HEADS):
            o = h * HEAD_PAD
            dqr_ref[:, o:o + NOPE] = dq_ref[:, o:o + NOPE].astype(MXU_DTYPE)
            dqr_ref[:, o + NOPE:o + HEAD_PAD] = _rope_mix_bwd(dq_ref[:, o + NOPE:o + HEAD_PAD], cos_v, sin_v).astype(MXU_DTYPE)
            dkv_ref[:, h * NOPE:(h + 1) * NOPE] = dk_ref[:, o:o + NOPE].astype(MXU_DTYPE)
            dkr = dkr + _rope_mix_bwd(dk_ref[:, o + NOPE:o + HEAD_PAD], cos_v, sin_v)
        dkv_ref[:, HEADS * NOPE:] = dv_ref[...]
        dcqn = lax.dot_general(dqr_ref[...], wq_ref[...], _DIMS["nt"], preferred_element_type=F32)
        dckvn = lax.dot_general(dkv_ref[...], wkv_ref[...], _DIMS["nt"], preferred_element_type=F32)

        zl = z_ref[...]

        def rms_bwd(c, dn, g_ref, gg_ref):
            r = lax.rsqrt(jnp.mean(c * c, axis=-1, keepdims=True) + EPS)
            ch = c * r
            gg_ref[...] += jnp.sum(dn * ch, axis=0, keepdims=True)
            dch = dn * g_ref[...]
            return r * (dch - ch * jnp.mean(dch * ch, axis=-1, keepdims=True))

        dz_ref[:, :Q_RANK] = rms_bwd(zl[:, :Q_RANK], dcqn, gq_ref, ggq_ref).astype(dz_ref.dtype)
        dz_ref[:, Q_RANK:Q_RANK + KV_RANK] = rms_bwd(zl[:, Q_RANK:Q_RANK + KV_RANK], dckvn, gkv_ref, ggkv_ref).astype(dz_ref.dtype)
        dz_ref[:, Q_RANK + KV_RANK:] = dkr.astype(dz_ref.dtype)

    def row(w):
        return pl.BlockSpec((tr, w), lambda i: (i, 0))

    def full(a):
        return pl.BlockSpec(a.shape, lambda i: (0, 0))

    lat = pl.BlockSpec((tr, LAT), lambda i: (i, lat_blk))
    return pl.pallas_call(
        body, name="lat_bwd",
        out_shape=(jax.ShapeDtypeStruct(dz.shape, dz.dtype), jax.ShapeDtypeStruct((T, HEADS * HEAD_PAD), MXU_DTYPE),
                   jax.ShapeDtypeStruct((T, 2 * HEADS * NOPE), MXU_DTYPE), jax.ShapeDtypeStruct(gq.shape, F32),
                   jax.ShapeDtypeStruct(gkv.shape, F32)),
        grid=(T // tr,),
        in_specs=[pl.BlockSpec(memory_space=pl.ANY), lat, row(HEADS * HEAD_PAD), row(HEADS * HEAD_PAD), row(HEADS * NOPE),
                  full(gq), full(gkv), full(wq), full(wkv), row(128), row(128)],
        out_specs=(lat, row(HEADS * HEAD_PAD), row(2 * HEADS * NOPE), full(gq), full(gkv)),
        input_output_aliases={0: 0},
        compiler_params=_params(("arbitrary",), 8 * _nbytes((tr, HEADS * HEAD_PAD), F32)),
    )(dz, z, dq, dk, dv, gq, gkv, wq, wkv, cos_a, sin_a)


GATE_ROWS = 256
HALO = 8


def _taps(ref, half, r, first):
    C = GATE_ROWS
    if first:
        xs = jnp.concatenate([jnp.zeros((HALO, ref.shape[-1]), F32), ref[half, 0:C, :]], axis=0)
    else:
        xs = ref[half, pl.ds(pl.multiple_of(r * C - HALO, HALO), C + HALO), :]
    return xs[HALO:, :], pltpu.roll(xs, 1, 0)[HALO:, :], pltpu.roll(xs, 2, 0)[HALO:, :]


def _conv_taps(taps, cw, cb):
    x0, x1, x2 = taps
    return cb + cw[0:1, :] * x2 + cw[1:2, :] * x1 + cw[2:3, :] * x0


def _fold8(x):
    acc = x[0:8, :]
    for i in range(1, x.shape[0] // 8):
        acc = acc + x[8 * i:8 * (i + 1), :]
    return acc


def _gate_fwd(up3, conv_w, conv_b, B, S):
    T = B * S
    W = FF_TILE
    C = GATE_ROWS

    def body(up_ref, cw_ref, cb_ref, act_ref, conv_ref):
        def chunk(r, first):
            gate = _conv_taps(_taps(up_ref, 0, r, first), cw_ref[0], cb_ref[0])
            val = _conv_taps(_taps(up_ref, 1, r, first), cw_ref[1], cb_ref[1])
            rows = pl.ds(0 if first else pl.multiple_of(r * C, C), C)
            conv_ref[0, rows, :] = gate.astype(conv_ref.dtype)
            conv_ref[1, rows, :] = val.astype(conv_ref.dtype)
            act_ref[rows, :] = (gate * _sigmoid(gate) * val).astype(act_ref.dtype)

        chunk(0, True)

        @pl.loop(1, S // C)
        def _(r):
            chunk(r, False)

    up_spec = pl.BlockSpec((2, S, W), lambda b, j: (0, b, j))
    return pl.pallas_call(
        body, name="gate_fwd",
        out_shape=(jax.ShapeDtypeStruct((T, D_FF), MXU_DTYPE), jax.ShapeDtypeStruct((2, T, D_FF), MXU_DTYPE)),
        grid=(B, N_FF_TILES),
        in_specs=[up_spec, pl.BlockSpec((2, 3, W), lambda b, j: (0, 0, j)), pl.BlockSpec((2, 1, W), lambda b, j: (0, 0, j))],
        out_specs=(pl.BlockSpec((S, W), lambda b, j: (b, j)), up_spec),
        compiler_params=_params(("parallel", "parallel"), 8 * _nbytes((S, W), F32)),
    )(up3, conv_w, conv_b)


def _gate_bwd(up3, conv3, dact, conv_w, B, S):
    T = B * S
    W = FF_TILE
    C = GATE_ROWS

    def body(up_ref, conv_ref, da_ref, cw_ref, dup_ref, gcw_ref, gcb_ref, d_s):
        @pl.when(pl.program_id(1) == 0)
        def _():
            gcw_ref[...] = jnp.zeros_like(gcw_ref)
            gcb_ref[...] = jnp.zeros_like(gcb_ref)

        @pl.loop(0, S // C)
        def _(r):
            rows = pl.ds(pl.multiple_of(r * C, C), C)
            gate, val = conv_ref[0, rows, :].astype(F32), conv_ref[1, rows, :].astype(F32)
            sg = _sigmoid(gate)
            da = da_ref[rows, :]
            d_s[0, rows, :] = da * val * (sg * (1.0 + gate * (1.0 - sg)))
            d_s[1, rows, :] = da * (gate * sg)

        d_s[:, S:S + HALO, :] = jnp.zeros((2, HALO, W), F32)

        def chunk(r, sums):
            base = pl.multiple_of(r * C, C)
            out = []
            for half in (0, 1):
                ds_ = d_s[half, pl.ds(base, C + HALO), :]
                d0, d1, d2 = ds_[:C, :], pltpu.roll(ds_, C + HALO - 1, 0)[:C, :], pltpu.roll(ds_, C + HALO - 2, 0)[:C, :]
                cw = cw_ref[half]
                dup_ref[half, pl.ds(base, C), :] = (cw[2:3, :] * d0 + cw[1:2, :] * d1 + cw[0:1, :] * d2).astype(dup_ref.dtype)
                x = up_ref[half, pl.ds(base, C), :]
                sb, s0, s1, s2 = sums[half]
                out.append((sb + _fold8(d0), s0 + _fold8(d2 * x), s1 + _fold8(d1 * x), s2 + _fold8(d0 * x)))
            return tuple(out)

        zeros = tuple(tuple(jnp.zeros((8, W), F32) for _ in range(4)) for _ in range(2))
        sums = lax.fori_loop(0, S // C, chunk, zeros)
        for half in (0, 1):
            sb, s0, s1, s2 = sums[half]
            gcb_ref[half] += jnp.sum(sb, axis=0, keepdims=True)
            gcw_ref[half, 0:1, :] += jnp.sum(s0, axis=0, keepdims=True)
            gcw_ref[half, 1:2, :] += jnp.sum(s1, axis=0, keepdims=True)
            gcw_ref[half, 2:3, :] += jnp.sum(s2, axis=0, keepdims=True)

    up_spec = pl.BlockSpec((2, S, W), lambda j, b: (0, b, j))
    cw_spec = pl.BlockSpec((2, 3, W), lambda j, b: (0, 0, j))
    cb_spec = pl.BlockSpec((2, 1, W), lambda j, b: (0, 0, j))
    return pl.pallas_call(
        body, name="gate_bwd",
        out_shape=(jax.ShapeDtypeStruct((2, T, D_FF), MXU_DTYPE), jax.ShapeDtypeStruct((2, 3, D_FF), F32),
                   jax.ShapeDtypeStruct((2, 1, D_FF), F32)),
        grid=(N_FF_TILES, B),
        in_specs=[up_spec, up_spec, pl.BlockSpec((S, W), lambda j, b: (b, j)), cw_spec],
        out_specs=(up_spec, cw_spec, cb_spec),
        scratch_shapes=[pltpu.VMEM((2, S + HALO, W), F32)],
        compiler_params=_params(("parallel", "arbitrary"), 12 * _nbytes((S, W), F32)),
    )(up3, conv3, dact, conv_w)


def _final(x2, tgt, g, tr=512):
    T, D = x2.shape

    def body(x_ref, t_ref, g_ref, dx_ref, loss_ref, gg_ref):
        @pl.when(pl.program_id(0) == 0)
        def _():
            loss_ref[...] = jnp.zeros_like(loss_ref)
            gg_ref[...] = jnp.zeros_like(gg_ref)

        xv = x_ref[...]
        gv = g_ref[...]
        r = lax.rsqrt(jnp.mean(xv * xv, axis=-1, keepdims=True) + EPS)
        xn = xv * r
        err = xn * gv - t_ref[...]
        loss_ref[...] += 0.5 * jnp.sum(jnp.mean(err * err, axis=-1, keepdims=True), axis=0, keepdims=True)
        dy = err * (1.0 / D)
        gg_ref[...] += jnp.sum(dy * xn, axis=0, keepdims=True)
        dxn = dy * gv
        dx_ref[...] = r * (dxn - xn * jnp.mean(dxn * xn, axis=-1, keepdims=True))

    row = pl.BlockSpec((tr, D), lambda i: (i, 0))
    vec = pl.BlockSpec((1, D), lambda i: (0, 0))
    return pl.pallas_call(
        body, name="final_loss",
        out_shape=(jax.ShapeDtypeStruct((T, D), F32), jax.ShapeDtypeStruct((1, 128), F32), jax.ShapeDtypeStruct((1, D), F32)),
        grid=(T // tr,), in_specs=[row, row, vec],
        out_specs=(row, pl.BlockSpec((1, 128), lambda i: (0, 0)), vec),
        compiler_params=_params(("arbitrary",), 6 * _nbytes((tr, D), F32)),
    )(x2, tgt, g)


def _sum_slabs(parts, name, tr):
    rows, cols = parts[0].shape
    n = len(parts)

    def body(*refs):
        acc = refs[0][...]
        for r in refs[1:n]:
            acc = acc + r[...]
        refs[n][...] = acc

    blk = pl.BlockSpec((tr, cols), lambda i: (i, 0))
    return pl.pallas_call(
        body, name=name, out_shape=jax.ShapeDtypeStruct((rows, cols), F32), grid=(rows // tr,),
        in_specs=[blk] * n, out_specs=blk,
        compiler_params=_params(("parallel",), (n + 1) * _nbytes((tr, cols), F32)),
    )(*parts)


ADAMW_BLOCK_BYTES = 2400 * 1024


def _adamw(w, g, m, v, name, copy_grad=False):
    lead = w.ndim == 3
    rows, cols = w.shape[-2:]
    fits = [d for d in range(8, rows + 1, 8) if rows % d == 0 and d * cols * 4 <= ADAMW_BLOCK_BYTES]
    tr = max(fits) if fits else rows
    c1 = 1.0 - ADAM_B1 ** ADAM_STEP
    c2 = 1.0 - ADAM_B2 ** ADAM_STEP

    def body(w_ref, g_ref, m_ref, v_ref, d_ref, nm_ref, nv_ref, *g_out):
        gv = g_ref[...]
        nm = ADAM_B1 * m_ref[...] + (1.0 - ADAM_B1) * gv
        nv = ADAM_B2 * v_ref[...] + (1.0 - ADAM_B2) * (gv * gv)
        nm_ref[...] = nm
        nv_ref[...] = nv
        d_ref[...] = -ADAM_LR * ((nm / c1) / (jnp.sqrt(nv / c2) + ADAM_EPS) + ADAM_WD * w_ref[...])
        if copy_grad:
            g_out[0][...] = gv

    blk = pl.BlockSpec((None, tr, cols), lambda i: (0, i, 0)) if lead else pl.BlockSpec((tr, cols), lambda i: (i, 0))
    sds = jax.ShapeDtypeStruct(w.shape, F32)
    n_out = 4 if copy_grad else 3
    return pl.pallas_call(
        body, name=name, out_shape=(sds,) * n_out, grid=(rows // tr,), in_specs=[blk] * 4, out_specs=(blk,) * n_out,
        compiler_params=_params(("parallel",), (4 + n_out) * _nbytes((tr, cols), F32)),
    )(w, g, m, v)


_ANY = pl.BlockSpec(memory_space=pl.ANY)


def _place():
    x, y, c = lax.axis_index("x"), lax.axis_index("y"), lax.axis_index("c")
    chips = [(1 - x, y), (x, 1 - y), (1 - x, 1 - y)]
    return x, y, c, chips


def _forward_halves(lands):
    n = len(lands)

    def body(*refs):
        outs, send, recv = refs[n:2 * n], refs[2 * n], refs[2 * n + 1]
        x, y, c, chips = _place()
        cps = []
        for w in range(n):
            for j, (px, py) in enumerate(chips):
                landed = outs[w].at[2 * px + py, c]
                cps.append(pltpu.make_async_remote_copy(
                    src_ref=landed, dst_ref=landed, send_sem=send.at[3 * w + j], recv_sem=recv.at[3 * w + j],
                    device_id=(x, y, 1 - c), device_id_type=MESH))
        for cp in cps:
            cp.start()
        for w in range(n):
            for j, (px, py) in enumerate(chips):
                other = outs[w].at[2 * px + py, 1 - c]
                pltpu.make_async_remote_copy(src_ref=other, dst_ref=other, send_sem=send.at[3 * w + j],
                                             recv_sem=recv.at[3 * w + j], device_id=(x, y, 1 - c),
                                             device_id_type=MESH).wait_recv()
        for cp in cps:
            cp.wait_send()

    dma = lambda k: pltpu.SemaphoreType.DMA((k,))
    return pl.pallas_call(
        body, name="gather_forward_halves", out_shape=tuple(jax.ShapeDtypeStruct(a.shape, a.dtype) for a in lands),
        in_specs=[_ANY] * n, out_specs=tuple([_ANY] * n), input_output_aliases={w: w for w in range(n)},
        scratch_shapes=[dma(3 * n), dma(3 * n)],
    )(*lands)


_HBM = pl.BlockSpec(memory_space=pltpu.HBM)
_SEM = pl.BlockSpec(memory_space=pltpu.SEMAPHORE)
_EFFECT = pltpu.SideEffectType.DATAFLOW_SIDE_EFFECTING


SEMS_PER_ARRAY = 8


def _exchange_copies(srcs, lands, send, recv, mode):
    x, y, c, chips = _place()
    if mode == "halves":
        cps = []
        for w, (src, land) in enumerate(zip(srcs, lands)):
            pieces = [(src.at[c], land.at[2 * x + y, c], (px, py, c)) for px, py in chips]
            pieces.append((src, land.at[2 * x + y], (x, y, 1 - c)))
            for k, (piece, dst, peer) in enumerate(pieces):
                cps.append(pltpu.make_async_remote_copy(
                    src_ref=piece, dst_ref=dst, send_sem=send.at[SEMS_PER_ARRAY * w + k],
                    recv_sem=recv.at[SEMS_PER_ARRAY * w + k], device_id=peer, device_id_type=MESH))
        return cps
    if mode == "swap":
        return [pltpu.make_async_remote_copy(
            src_ref=src.at[:, 1 - c], dst_ref=land, send_sem=send.at[SEMS_PER_ARRAY * w],
            recv_sem=recv.at[SEMS_PER_ARRAY * w], device_id=(x, y, 1 - c), device_id_type=MESH)
            for w, (src, land) in enumerate(zip(srcs, lands))]
    if mode == "all":
        flips = [(fx, fy, fc) for fx in (0, 1) for fy in (0, 1) for fc in (0, 1)][1:]
        peers = [(x ^ fx, y ^ fy, c ^ fc) for fx, fy, fc in flips]
        slot = 4 * x + 2 * y + c
    else:
        peers = [(px, py, c) for px, py in chips] + ([(x, y, 1 - c)] if mode == "gather" else [])
        slot = 2 * x + y
    cps = []
    for w, (src, land) in enumerate(zip(srcs, lands)):
        for k, peer in enumerate(peers):
            piece = src.at[2 * peer[0] + peer[1]] if mode == "scatter" else src
            cps.append(pltpu.make_async_remote_copy(
                src_ref=piece, dst_ref=land.at[slot], send_sem=send.at[SEMS_PER_ARRAY * w + k],
                recv_sem=recv.at[SEMS_PER_ARRAY * w + k], device_id=peer, device_id_type=MESH))
    return cps


def _exchange_start(srcs, name, mode, after):
    n = len(srcs)
    if mode == "swap":
        land_shapes = [(s.shape[0],) + s.shape[2:] for s in srcs]
    else:
        lead = {"gather": (N_CHIPS,), "halves": (N_CHIPS,), "scatter": (), "all": (2 * N_CHIPS,)}[mode]
        land_shapes = [lead + s.shape for s in srcs]

    def body(*refs):
        src_refs, land_refs = refs[:n], refs[n:2 * n]
        send, recv = refs[2 * n + 1], refs[2 * n + 2]
        token = refs[-1]
        for cp in _exchange_copies(src_refs, land_refs, send, recv, mode):
            cp.start()
        token[...] = jnp.zeros_like(token)

    sems = pltpu.SemaphoreType.DMA((SEMS_PER_ARRAY * n,))
    out = pl.pallas_call(
        body, name=name,
        out_shape=(sems, sems, *[pltpu.HBM(s.shape, s.dtype) for s in srcs],
                   *[pltpu.HBM(shp, s.dtype) for shp, s in zip(land_shapes, srcs)], jax.ShapeDtypeStruct((8, 128), F32)),
        in_specs=[_HBM] * (2 * n) + [_ANY],
        out_specs=(_SEM, _SEM, *[_HBM] * (2 * n), pl.BlockSpec(memory_space=pltpu.VMEM)),
        input_output_aliases={i: 2 + i for i in range(2 * n)},
        compiler_params=pltpu.CompilerParams(has_side_effects=_EFFECT),
    )(*[pltpu.with_memory_space_constraint(s, pltpu.HBM) for s in srcs],
      *[pltpu.with_memory_space_constraint(lax.empty(shp, s.dtype), pltpu.HBM) for shp, s in zip(land_shapes, srcs)],
      after)
    return out[0], out[1], out[2:2 + n], out[2 + n:2 + 2 * n], out[-1]


def _exchange_wait(started, name, mode, after):
    send, recv, src_thru, land_thru, _ = started
    n = len(src_thru)
    after = list(after) if isinstance(after, (list, tuple)) else [after]

    def body(*refs):
        src_refs, land_refs, send_ref, recv_ref = refs[:n], refs[n:2 * n], refs[2 * n], refs[2 * n + 1]
        for cp in _exchange_copies(src_refs, land_refs, send_ref, recv_ref, mode):
            cp.wait_send()
            cp.wait_recv()

    out = pl.pallas_call(
        body, name=name,
        out_shape=tuple(pltpu.HBM(a.shape, a.dtype) for a in list(src_thru) + list(land_thru)),
        in_specs=[_HBM] * (2 * n) + [_SEM, _SEM] + [_ANY] * len(after), out_specs=tuple([_HBM] * (2 * n)),
        input_output_aliases={i: i for i in range(2 * n)},
        compiler_params=pltpu.CompilerParams(has_side_effects=_EFFECT),
    )(*src_thru, *land_thru, send, recv, *after)
    return out[:n], out[n:]


def _swap_halves(gs, name):
    n = len(gs)

    def body(*refs):
        ins, outs, send, recv = refs[:n], refs[n:2 * n], refs[2 * n], refs[2 * n + 1]
        x, y, c, _ = _place()
        cps = []
        for w in range(n):
            cps.append(pltpu.make_async_remote_copy(
                src_ref=ins[w].at[:, 1 - c], dst_ref=outs[w], send_sem=send.at[w], recv_sem=recv.at[w],
                device_id=(x, y, 1 - c), device_id_type=MESH))
        for cp in cps:
            cp.start()
        for cp in cps:
            cp.wait()

    return pl.pallas_call(
        body, name=name,
        out_shape=tuple(jax.ShapeDtypeStruct((g.shape[0],) + g.shape[2:], g.dtype) for g in gs),
        in_specs=[_ANY] * n, out_specs=tuple([_ANY] * n),
        scratch_shapes=[pltpu.SemaphoreType.DMA((n,)), pltpu.SemaphoreType.DMA((n,))],
    )(*gs)


GRAD_PAYLOAD = jnp.bfloat16


def _half_blocks(half_rows, cols):
    if (half_rows // 2) % 16 == 0:
        return (half_rows // 2, cols), (lambda r: (r, 0))
    assert cols % 256 == 0, (half_rows, cols)
    return (half_rows, cols // 2), (lambda r: (0, r))


def _pair_sum(gs, gots, name):
    n = len(gs)
    core = lax.axis_index("c").astype(jnp.int32).reshape(1)

    def body(core_ref, *refs):
        del core_ref
        for w in range(n):
            refs[2 * n + w][...] = (refs[w][...] + refs[n + w][...]).astype(GRAD_PAYLOAD)

    in_specs, out_specs, out_shape, nbytes = [], [], [], 0
    cuts = [_half_blocks(g.shape[1] // 2, g.shape[2]) for g in gs]
    for g, ((br, bc), at) in zip(gs, cuts):
        per_half = (g.shape[1] // 2) // br
        in_specs.append(pl.BlockSpec((1, br, bc), lambda s, r, core, at=at, per_half=per_half:
                                     (s, per_half * core[0] + at(r)[0], at(r)[1])))
        nbytes += 3 * _nbytes((br, bc), F32)
    for g, ((br, bc), at) in zip(gs, cuts):
        in_specs.append(pl.BlockSpec((1, br, bc), lambda s, r, core, at=at: (s,) + at(r)))
        out_specs.append(pl.BlockSpec((1, br, bc), lambda s, r, core, at=at: (s,) + at(r)))
        out_shape.append(jax.ShapeDtypeStruct((g.shape[0], g.shape[1] // 2, g.shape[2]), GRAD_PAYLOAD))
    return pl.pallas_call(
        body, name=name, out_shape=tuple(out_shape),
        grid_spec=pltpu.PrefetchScalarGridSpec(num_scalar_prefetch=1, grid=(N_CHIPS, 2), in_specs=in_specs,
                                               out_specs=tuple(out_specs)),
        compiler_params=_params(("parallel", "parallel"), nbytes),
    )(core, *gs, *gots)


def _chip_sum(ps, landed):
    n = len(ps)
    x, y, c = lax.axis_index("x"), lax.axis_index("y"), lax.axis_index("c")
    where = jnp.stack([2 * x + y, 2 * (1 - x) + y, 2 * x + (1 - y), 2 * (1 - x) + (1 - y), c]).astype(jnp.int32)

    def body(where_ref, *refs):
        del where_ref
        for w in range(n):
            terms = [refs[4 * w + t][...].astype(F32) for t in range(4)]
            refs[4 * n + w][...] = ((terms[0] + terms[1]) + terms[2]) + terms[3]

    in_specs, out_specs, out_shape, args, nbytes = [], [], [], [], 0
    for p, a in zip(ps, landed):
        (br, bc), at = _half_blocks(a.shape[1], a.shape[2])
        blk = (1, br, bc)
        in_specs.append(pl.BlockSpec(blk, lambda r, where, at=at: (where[0],) + at(r)))
        args.append(p)
        for t in (1, 2, 3):
            in_specs.append(pl.BlockSpec(blk, lambda r, where, t=t, at=at: (where[t],) + at(r)))
            args.append(a)
        out_specs.append(pl.BlockSpec(blk, lambda r, where, at=at: (where[4],) + at(r)))
        out_shape.append(jax.ShapeDtypeStruct((2,) + a.shape[1:], F32))
        nbytes += 4 * _nbytes(blk, F32)
    return pl.pallas_call(
        body, name="grad_chip_sum", out_shape=tuple(out_shape),
        grid_spec=pltpu.PrefetchScalarGridSpec(num_scalar_prefetch=1, grid=(2,), in_specs=in_specs,
                                               out_specs=tuple(out_specs)),
        compiler_params=_params(("parallel",), nbytes),
    )(where, *args)


def _join_halves(ss):
    n = len(ss)

    def body(*refs):
        outs, send, recv = refs[n:2 * n], refs[2 * n], refs[2 * n + 1]
        x, y, c, _ = _place()
        cps = []
        for w in range(n):
            cps.append(pltpu.make_async_remote_copy(
                src_ref=outs[w].at[c], dst_ref=outs[w].at[c], send_sem=send.at[w], recv_sem=recv.at[w],
                device_id=(x, y, 1 - c), device_id_type=MESH))
        for cp in cps:
            cp.start()
        for w in range(n):
            got = outs[w].at[1 - c]
            pltpu.make_async_remote_copy(src_ref=got, dst_ref=got, send_sem=send.at[w], recv_sem=recv.at[w],
                                         device_id=(x, y, 1 - c), device_id_type=MESH).wait_recv()
        for cp in cps:
            cp.wait_send()

    dma = lambda k: pltpu.SemaphoreType.DMA((k,))
    return pl.pallas_call(
        body, name="grad_join_halves",
        out_shape=tuple(jax.ShapeDtypeStruct(s.shape, s.dtype) for s in ss),
        in_specs=[_ANY] * n, out_specs=tuple([_ANY] * n), input_output_aliases={w: w for w in range(n)},
        scratch_shapes=[dma(n), dma(n)],
    )(*ss)


def _rot_cols(w, axis=-1):
    a, b = jnp.split(w, 2, axis=axis)
    return jnp.concatenate([-b, a], axis=axis)


def _rot_cols_t(g, axis=-1):
    a, b = jnp.split(g, 2, axis=axis)
    return jnp.concatenate([b, -a], axis=axis)


def _cols_from_chips(a):
    n, r, cs = a.shape
    return jnp.transpose(a, (1, 0, 2)).reshape(r, n * cs)


def _cols_to_chips(a):
    r, cc = a.shape
    return jnp.transpose(a.reshape(r, N_CHIPS, cc // N_CHIPS), (1, 0, 2))


def _conv_w_split(cw):
    return jnp.swapaxes(cw.reshape(3, 2, D_FF), 0, 1)


def _conv_w_join(g):
    return jnp.swapaxes(g, 0, 1).reshape(3, 2 * D_FF)


_SEG =(D_MODEL, 2 * D_MODEL, 2 * D_MODEL + Q_RANK, 2 * D_MODEL + Q_RANK + KV_RANK, 2 * D_MODEL + Q_RANK + KV_RANK + ROPE,
        3 * D_MODEL + Q_RANK + KV_RANK + ROPE)


def _w_in_t_to_pad(wt):
    u, v, cq, ckv, kr, ga, gb = jnp.split(wt, _SEG, axis=0)
    return jnp.concatenate([u, v, ga, gb, cq, ckv, kr, _rot_cols(kr, axis=0)], axis=0)


def _w_in_t_from_pad(gt):
    u, v, ga, gb, cq, ckv, kr, krr = jnp.split(
        gt, (D_MODEL, 2 * D_MODEL, 3 * D_MODEL, 4 * D_MODEL, 4 * D_MODEL + Q_RANK, 4 * D_MODEL + Q_RANK + KV_RANK,
             4 * D_MODEL + Q_RANK + KV_RANK + ROPE), axis=0)
    return jnp.concatenate([u, v, cq, ckv, kr + _rot_cols_t(krr, axis=0), ga, gb], axis=0)


def _w_uq_to_pad(w):
    t = w.reshape(Q_RANK, HEADS, QK_DIM)
    nope, rope = t[..., :NOPE], t[..., NOPE:]
    return jnp.concatenate([nope, rope, _rot_cols(rope)], axis=-1).reshape(Q_RANK, HEADS * HEAD_PAD)


def _w_uq_from_pad(g):
    t = g.reshape(Q_RANK, HEADS, HEAD_PAD)
    nope, rope, rot = t[..., :NOPE], t[..., NOPE:QK_DIM], t[..., QK_DIM:]
    return jnp.concatenate([nope, rope + _rot_cols_t(rot)], axis=-1).reshape(Q_RANK, HEADS * QK_DIM)


def _w_ukv_to_pad(w):
    t = w.reshape(KV_RANK, HEADS, 2, NOPE)
    return jnp.swapaxes(t, 1, 2).reshape(KV_RANK, 2 * HEADS * NOPE)


def _w_ukv_from_pad(g):
    t = g.reshape(KV_RANK, 2, HEADS, NOPE)
    return jnp.swapaxes(t, 1, 2).reshape(KV_RANK, 2 * HEADS * NOPE)


def _rope_tables(positions):
    inv_freq = 1.0 / (ROPE_THETA ** (jnp.arange(0, ROPE, 2, dtype=F32) / ROPE))
    ang = positions.astype(F32).reshape(-1, 1) * inv_freq
    cos, sin = jnp.cos(ang), jnp.sin(ang)
    zero = jnp.zeros((ang.shape[0], 64), F32)
    return jnp.concatenate([cos, cos, zero], axis=1), jnp.concatenate([sin, sin, zero], axis=1)


_BIG = ("w_in", "w_uq", "w_ukv", "w_out", "w_up", "w_down")
UP_SHARD = 2 * D_FF // N_CHIPS
TOKEN_TILE = 1024


def _local_step(x, positions, tgt, wts, in_weights, mixer_weights, ffn_weights, on_ffn_grads, on_mixer_grads):
    B, S, D = x.shape
    T = B * S
    xf = x.reshape(T, D)
    cos_a, sin_a = _rope_tables(positions)
    bs_t = jnp.pad(wts["a_spatial_b"].T, ((0, 0), (0, 128 - A_GROUPS)))

    h = _rms_fwd(xf, wts["mix_norm"], "norm1_fwd")
    wts = dict(wts)
    wts["w_in"], token = in_weights([h, cos_a, sin_a])
    tm = min(TOKEN_TILE, T)
    z = _mm(h, wts["w_in"], "nt", "in_proj", tm=tm, tn=1536, tk=D, n_outer=True, after=token)
    wts["w_q"], wts["w_kv"], wts["w_out"] = mixer_weights(z)
    q, k, v, cqn, ckvn = _lat_fwd(z, wts["q_a_norm"], wts["kv_a_norm"], wts["w_q"], wts["w_kv"], cos_a, sin_a)
    yb, *lses = _attn_fwd(q, k, v, B, S)
    merged = _mix_fwd(z, yb, wts["a_v_norm_g"], wts["a_v_norm_b"], wts["a_spatial_w"], bs_t)
    x1 = _mm(merged, wts["w_out"], "nn", "out_proj", tm=min(512, T), tn=D, tk=D, add=xf)
    h2 = _rms_fwd(x1, wts["ffn_norm"], "norm2_fwd")
    wts["w_up"], wts["w_down"], wts["conv_w"] = ffn_weights(h2)
    up_pre = _mm(h2, wts["w_up"], "nn", "up_proj", tm=tm, tn=UP_SHARD, tk=D, dims=(T, 2 * D_FF, D),
                 b_spec=pl.BlockSpec((None, D, UP_SHARD), lambda i, j, k: (j, 0, 0)),
                 o_spec=pl.BlockSpec((None, tm, UP_SHARD), lambda i, j, k: (j // 2, i, j % 2)), out_shape=(2, T, D_FF),
                 n_outer=True)
    act, up_conv = _gate_fwd(up_pre, wts["conv_w"], wts["conv_b"], B, S)
    x2 = _mm(act, wts["w_down"], "nn", "down_proj", tm=tm, tn=D, tk=1408, add=x1)
    dx2, loss_row, g_final = _final(x2, tgt.reshape(T, D), wts["final_norm"])

    g = {"final_norm": g_final}
    dact = _mm(dx2, wts["w_down"], "nt", "down_proj_dx", tm=tm, tn=1408, tk=D, n_outer=True)
    tk2, tk1 = min(2048, T), min(1024, T)
    g["w_down"], g["w_down_lo"] = _mm(act, dx2, "tn", "down_proj_dw", tm=1408, tn=D, tk=tk1, copy_dtype=GRAD_PAYLOAD)
    dup, g["conv_w"], g["conv_b"] = _gate_bwd(up_pre, up_conv, dact, wts["conv_w"], B, S)
    g["w_up"], g["w_up_lo"] = _mm(
        h2, dup, "tn", "up_proj_dw", tm=D, tn=UP_SHARD, tk=tk2, dims=(D, 2 * D_FF, T), copy_dtype=GRAD_PAYLOAD,
        b_spec=pl.BlockSpec((None, tk2, UP_SHARD), lambda i, j, k: (j // 2, k, j % 2)),
        o_spec=pl.BlockSpec((None, D, UP_SHARD), lambda i, j, k: (j, 0, 0)), out_shape=(N_CHIPS, D, UP_SHARD))
    token, ffn_sent = on_ffn_grads(g)
    dh2 = _mm(dup, wts["w_up"], "nt", "up_proj_dx", tm=tm, tn=D, tk=UP_SHARD, dims=(T, D, 2 * D_FF), after=token,
              a_spec=pl.BlockSpec((None, tm, UP_SHARD), lambda i, j, k: (k // 2, i, k % 2)),
              b_spec=pl.BlockSpec((None, D, UP_SHARD), lambda i, j, k: (k, 0, 0)))
    token = ffn_sent(dh2)
    dx1, g["ffn_norm"] = _rms_bwd(x1, wts["ffn_norm"], dh2, dx2, "norm2_bwd")
    dm = _mm(dx1, wts["w_out"], "nt", "out_proj_dx", tm=min(512, T), tn=D, tk=D, after=token)
    g["w_out"], g["w_out_lo"] = _mm(merged, dx1, "tn", "out_proj_dw", tm=D, tn=D, tk=tk1, copy_dtype=GRAD_PAYLOAD)
    dz, dyb, dl, g["a_spatial_w"], gbs, g["a_v_norm_g"], g["a_v_norm_b"] = _mix_bwd(
        z, yb, dm, wts["a_v_norm_g"], wts["a_v_norm_b"], wts["a_spatial_w"], bs_t)
    g["a_spatial_b"] = gbs[:, :A_GROUPS].T
    delta = dl.reshape(HEADS * T // ATT_BLOCK, 1, ATT_BLOCK)
    dq, dk, dv = _attn_bwd(q, k, v, dyb, lses, delta, B, S)
    dz, dq_raw, dkv, g["q_a_norm"], g["kv_a_norm"] = _lat_bwd(
        dz, z, dq, dk, dv, wts["q_a_norm"], wts["kv_a_norm"], wts["w_q"], wts["w_kv"], cos_a, sin_a)
    g["w_q"] = _mm(cqn, dq_raw, "tn", "q_proj_dw", tm=Q_RANK, tn=HEADS * HEAD_PAD, tk=tk2)
    g["w_kv"] = _mm(ckvn, dkv, "tn", "kv_proj_dw", tm=KV_RANK, tn=2 * HEADS * NOPE, tk=tk2)
    g["w_in"] = _mm(dz, h, "tn", "in_proj_dw", tm=1536, tn=D, tk=tk2)
    token = on_mixer_grads(g)
    dh = _mm(dz, wts["w_in"], "nn", "in_proj_dx", tm=tm, tn=D, tk=1536, after=token)
    dx, g["mix_norm"] = _rms_bwd(xf, wts["mix_norm"], dh, dx1, "norm1_bwd")
    return loss_row[0, 0], dx.reshape(B, S, D), g


_SMALL = (("mix_norm", (1, D_MODEL)), ("a_v_norm_g", (1, D_MODEL)), ("a_v_norm_b", (1, D_MODEL)),
          ("a_spatial_w", (A_GROUPS * CHUNK, CHUNK)), ("a_spatial_b", (1, A_GROUPS * CHUNK)), ("q_a_norm", (1, Q_RANK)),
          ("kv_a_norm", (1, KV_RANK)), ("ffn_norm", (1, D_MODEL)), ("conv_b", (1, 2 * D_FF)), ("final_norm", (1, D_MODEL)),
          ("conv_w", (3, 2 * D_FF)))
_SMALL_SIZE = sum(math.prod(s) for _, s in _SMALL)
_SMALL_ROWS = -(-(_SMALL_SIZE + 1) // (128 * 8)) * 8


def kernel(x, positions, mix_norm, w_in, a_v_norm_g, a_v_norm_b, a_spatial_w, a_spatial_b, q_a_norm, w_uq, kv_a_norm, w_ukv, w_out, ffn_norm, w_up, conv_w, conv_b, w_down, final_norm, loss_target, m_mix_norm, m_w_in, m_a_v_norm_g, m_a_v_norm_b, m_a_spatial_w, m_a_spatial_b, m_q_a_norm, m_w_uq, m_kv_a_norm, m_w_ukv, m_w_out, m_ffn_norm, m_w_up, m_conv_w, m_conv_b, m_w_down, m_final_norm, v_mix_norm, v_w_in, v_a_v_norm_g, v_a_v_norm_b, v_a_spatial_w, v_a_spatial_b, v_q_a_norm, v_w_uq, v_kv_a_norm, v_w_ukv, v_w_out, v_ffn_norm, v_w_up, v_conv_w, v_conv_b, v_w_down, v_final_norm):
    weights = dict(mix_norm=mix_norm, w_in=w_in, a_v_norm_g=a_v_norm_g, a_v_norm_b=a_v_norm_b, a_spatial_w=a_spatial_w,
                   a_spatial_b=a_spatial_b, q_a_norm=q_a_norm, w_uq=w_uq, kv_a_norm=kv_a_norm, w_ukv=w_ukv, w_out=w_out,
                   ffn_norm=ffn_norm, w_up=w_up, conv_w=conv_w, conv_b=conv_b, w_down=w_down, final_norm=final_norm)
    m_in = dict(mix_norm=m_mix_norm, w_in=m_w_in, a_v_norm_g=m_a_v_norm_g, a_v_norm_b=m_a_v_norm_b,
                a_spatial_w=m_a_spatial_w, a_spatial_b=m_a_spatial_b, q_a_norm=m_q_a_norm, w_uq=m_w_uq,
                kv_a_norm=m_kv_a_norm, w_ukv=m_w_ukv, w_out=m_w_out, ffn_norm=m_ffn_norm, w_up=m_w_up, conv_w=m_conv_w,
                conv_b=m_conv_b, w_down=m_w_down, final_norm=m_final_norm)
    v_in = dict(mix_norm=v_mix_norm, w_in=v_w_in, a_v_norm_g=v_a_v_norm_g, a_v_norm_b=v_a_v_norm_b,
                a_spatial_w=v_a_spatial_w, a_spatial_b=v_a_spatial_b, q_a_norm=v_q_a_norm, w_uq=v_w_uq,
                kv_a_norm=v_kv_a_norm, w_ukv=v_w_ukv, w_out=v_w_out, ffn_norm=v_ffn_norm, w_up=v_w_up, conv_w=v_conv_w,
                conv_b=v_conv_b, w_down=v_w_down, final_norm=v_final_norm)
    names = list(weights)
    chip = 2 * lax.axis_index("x") + lax.axis_index("y")

    def halves(a):
        return a.reshape(a.shape[:-2] + (2, a.shape[-2] // 2, a.shape[-1]))

    w_in_t = jnp.swapaxes(w_in[0], 0, 1).astype(MXU_DTYPE)
    w_in_gather = _exchange_start([jnp.stack(jnp.split(w_in_t, 2, axis=1))], "w_in_gather_start", "halves",
                                  after=positions)
    gathers = {}
    wts = dict(
        mix_norm=mix_norm, a_v_norm_g=a_v_norm_g, a_v_norm_b=a_v_norm_b, a_spatial_w=a_spatial_w[0],
        a_spatial_b=a_spatial_b[0], q_a_norm=q_a_norm, kv_a_norm=kv_a_norm, ffn_norm=ffn_norm,
        final_norm=final_norm.reshape(1, D_MODEL), conv_b=conv_b.reshape(2, 1, D_FF))

    mixer_shards = [weights[n][0].astype(MXU_DTYPE) for n in _BIG[1:4]]
    ffn_shards = [w_up[0].astype(MXU_DTYPE), w_down[0].astype(MXU_DTYPE)]

    def in_weights(after):
        _, landed = _exchange_wait(w_in_gather, "w_in_gather_wait", "halves", list(after) + mixer_shards + ffn_shards)
        (w_in_sh,) = _forward_halves(list(landed))
        gathers["mixer"] = _exchange_start(mixer_shards, "mixer_gather_start", "gather", after=w_in_sh)
        gathers["ffn"] = _exchange_start(ffn_shards + [conv_w[0]], "ffn_gather_start", "gather",
                                         after=gathers["mixer"][4])
        w_in_pad = _w_in_t_to_pad(jnp.concatenate([w_in_sh[:, 0], w_in_sh[:, 1]], axis=-1).reshape(-1, D_MODEL))
        return w_in_pad, gathers["ffn"][4]

    def mixer_weights(after):
        _, (w_uq_sh, w_ukv_sh, w_out_sh) = _exchange_wait(gathers["mixer"], "mixer_gather_wait", "gather", after)
        return (_w_uq_to_pad(_cols_from_chips(w_uq_sh)), _w_ukv_to_pad(_cols_from_chips(w_ukv_sh)),
                w_out_sh.reshape(D_MODEL, D_MODEL))

    def ffn_weights(after):
        _, (w_up_sh, w_down_sh, cw_all) = _exchange_wait(gathers["ffn"], "ffn_gather_wait", "gather", after)
        return w_up_sh, w_down_sh.reshape(D_FF, D_MODEL), _conv_w_split(_cols_from_chips(cw_all))

    scatters = {}

    def start_scatter(slabs, slabs_lo, tag):
        got = _swap_halves([halves(s) for s in slabs_lo], tag + "_grad_swap_halves")
        sums = _pair_sum(slabs, got, tag + "_grad_pair_sum")
        scatters[tag] = _exchange_start(list(sums), tag + "_scatter_start", "scatter", after=slabs[-1])
        return scatters[tag][4]

    def on_ffn_grads(g):
        slabs, slabs_lo = [[g["w_up" + lo], g["w_down" + lo].reshape(N_CHIPS, D_FF // N_CHIPS, D_MODEL)]
                           for lo in ("", "_lo")]
        swap = _exchange_start([halves(s) for s in slabs_lo], "ffn_swap_start", "swap", after=slabs[1])

        def sent(after):
            _, got = _exchange_wait(swap, "ffn_swap_wait", "swap", after)
            sums = _pair_sum(slabs, got, "ffn_grad_pair_sum")
            scatters["ffn"] = _exchange_start(list(sums), "ffn_scatter_start", "scatter", after=got[0])
            return scatters["ffn"][4]

        return swap[4], sent

    def on_mixer_grads(g):
        slabs = [_w_in_t_from_pad(g["w_in"]).reshape(N_CHIPS, -1, D_MODEL), _cols_to_chips(_w_uq_from_pad(g["w_q"])),
                 _cols_to_chips(_w_ukv_from_pad(g["w_kv"]))]
        w_out_slabs = [g["w_out" + lo].reshape(N_CHIPS, D_MODEL // N_CHIPS, D_MODEL) for lo in ("", "_lo")]
        return start_scatter(slabs + w_out_slabs[:1], [s.astype(GRAD_PAYLOAD) for s in slabs] + w_out_slabs[1:], "mixer")

    loss_part, grad_x, g = _local_step(x, positions, loss_target, wts, in_weights, mixer_weights, ffn_weights,
                                       on_ffn_grads, on_mixer_grads)

    g_small_parts = dict(g)
    g_small_parts["conv_w"] = _conv_w_join(g["conv_w"])
    g_small_parts["conv_b"] = g["conv_b"].reshape(1, 2 * D_FF)
    flat = jnp.concatenate([g_small_parts[n].reshape(-1) for n, _ in _SMALL] + [loss_part.reshape(1)])
    flat = jnp.pad(flat, (0, _SMALL_ROWS * 128 - flat.shape[0])).reshape(_SMALL_ROWS, 128)
    small_gather = _exchange_start([flat], "small_gather_start", "all", after=grad_x)

    mixer_sums, mixer_landed = _exchange_wait(scatters["mixer"], "mixer_scatter_wait", "scatter", after=small_gather[4])
    ffn_sums, ffn_landed = _exchange_wait(scatters["ffn"], "ffn_scatter_wait", "scatter", after=mixer_landed[0])
    reduced = _chip_sum(list(mixer_sums) + list(ffn_sums), list(mixer_landed) + list(ffn_landed))
    g_big = dict(zip(_BIG, _join_halves(reduced)))

    grads, deltas, new_m, new_v = {}, {}, {}, {}

    def update(n, grad, copy_grad=False):
        w = weights[n]
        shape2 = grad.shape
        d, nm, nv, *again = _adamw(w.reshape(shape2), grad, m_in[n].reshape(shape2), v_in[n].reshape(shape2),
                                   "adamw_" + n, copy_grad)
        grads[n], deltas[n], new_m[n], new_v[n] = (t.reshape(w.shape) for t in (again[0] if copy_grad else grad, d, nm, nv))

    def update_transposed(n, grad_t):
        t = lambda a: jnp.swapaxes(a, 1, 2)
        d, nm, nv, again = _adamw(t(weights[n]), grad_t, t(m_in[n]), t(v_in[n]), "adamw_" + n, True)
        grads[n], deltas[n], new_m[n], new_v[n] = t(again), t(d), t(nm), t(nv)

    for n in _BIG:
        g3 = g_big[n].reshape((1, -1, g_big[n].shape[-1]))
        if n == "w_in":
            update_transposed(n, g3)
        else:
            update(n, g3, copy_grad=True)

    (own,), (everyone,) = _exchange_wait(small_gather, "small_gather_wait", "all", after=[deltas[n] for n in _BIG])
    device = 2 * chip + lax.axis_index("c")
    everyone = lax.dynamic_update_slice(everyone, own[None], (device, 0, 0))
    total = _sum_slabs([everyone[j] for j in range(8)], "small_grads_sum", tr=_SMALL_ROWS).reshape(-1)
    o = 0
    for n, shp in _SMALL:
        piece = total[o:o + math.prod(shp)].reshape(shp)
        o += math.prod(shp)
        if n == "conv_w":
            piece = lax.dynamic_slice_in_dim(piece, chip * UP_SHARD, UP_SHARD, axis=1)
        update(n, piece)
    loss = total[_SMALL_SIZE]
    return (loss, grad_x, *[grads[n] for n in names], *[deltas[n] for n in names], *[new_m[n] for n in names],
            *[new_v[n] for n in names])
```

```python
import functools
import math

import jax
import jax.numpy as jnp
from jax import lax
from jax.experimental import pallas as pl
from jax.experimental.pallas import tpu as pltpu

F32 = jnp.float32
MXU_DTYPE = jnp.bfloat16
MESH = pl.DeviceIdType.MESH

D_MODEL = 1024
EPS = 1e-6
A_GROUPS = 8
CHUNK = 128
HEADS = 8
NOPE = 128
ROPE = 64
QK_DIM = NOPE + ROPE
HEAD_PAD = 256
Q_RANK = 256
KV_RANK = 128
ROPE_THETA = 10000.0
D_FF = 2816
FF_TILE = 256
N_FF_TILES = D_FF // FF_TILE
LAT = 512
IN_PAD = 4 * D_MODEL + LAT
N_CHIPS = 4
ADAM_LR, ADAM_B1, ADAM_B2, ADAM_EPS, ADAM_WD, ADAM_STEP = 0.001, 0.9, 0.999, 1e-08, 0.01, 10

VMEM_CAP_V7X = 64 * 1024 * 1024
NEG = -1e30


def _params(sem, nbytes):
    limit = int(min(VMEM_CAP_V7X - (8 << 20), max(32 << 20, 3 * nbytes)))
    return pltpu.CompilerParams(dimension_semantics=sem, vmem_limit_bytes=limit)


def _nbytes(shape, dtype):
    return math.prod(shape) * jnp.dtype(dtype).itemsize


_DIMS = {"nn": (((1,), (0,)), ((), ())), "nt": (((1,), (1,)), ((), ())), "tn": (((0,), (0,)), ((), ()))}


def _mm(a, b, mode, name, *, tm, tn, tk, out_dtype=F32, add=None, dims=None, a_spec=None, b_spec=None,
        o_spec=None, out_shape=None, n_outer=False, copy_dtype=None, after=None, norm_gain=None):
    if dims is None:
        if mode == "nn":
            (M, K), (_, N) = a.shape, b.shape
        elif mode == "nt":
            (M, K), (N, _) = a.shape, b.shape
        else:
            (K, M), (_, N) = a.shape, b.shape
    else:
        M, N, K = dims
    a_blk = (tk, tm) if mode == "tn" else (tm, tk)
    b_blk = (tn, tk) if mode == "nt" else (tk, tn)
    if a_spec is None:
        a_spec = pl.BlockSpec(a_blk, (lambda i, j, k: (k, i)) if mode == "tn" else (lambda i, j, k: (i, k)))
    if b_spec is None:
        b_spec = pl.BlockSpec(b_blk, (lambda i, j, k: (j, k)) if mode == "nt" else (lambda i, j, k: (k, j)))
    if o_spec is None:
        o_spec = pl.BlockSpec((tm, tn), lambda i, j, k: (i, j))
    if out_shape is None:
        out_shape = (M, N)
    assert M % tm == 0 and N % tn == 0 and K % tk == 0, (name, M, N, K, tm, tn, tk)
    nk = K // tk
    contract = _DIMS[mode]
    has_add = add is not None
    has_gain = norm_gain is not None
    assert not has_gain or (tn == N and copy_dtype is not None), name

    def body(*refs):
        a_ref, b_ref = refs[0], refs[1]
        add_ref = refs[2] if has_add else None
        n_in = 2 + has_add + (after is not None) + has_gain
        gain_ref = refs[n_in - 1] if has_gain else None
        o_ref = refs[n_in]
        copy_ref = refs[n_in + 1] if copy_dtype is not None else None

        def product():
            return lax.dot_general(a_ref[...].astype(MXU_DTYPE), b_ref[...].astype(MXU_DTYPE), contract,
                                   preferred_element_type=F32)

        def finish(r):
            if has_add:
                r = r + add_ref[...]
            o_ref[...] = r.astype(out_dtype)
            if has_gain:
                rs = lax.rsqrt(jnp.mean(r * r, axis=-1, keepdims=True) + EPS)
                copy_ref[...] = ((r * rs) * gain_ref[...]).astype(copy_dtype)
            elif copy_ref is not None:
                copy_ref[...] = r.astype(copy_dtype)

        if nk == 1:
            finish(product())
            return
        acc = refs[-1]
        k = pl.program_id(2)

        @pl.when(k == 0)
        def _():
            acc[...] = jnp.zeros_like(acc)

        acc[...] += product()

        @pl.when(k == nk - 1)
        def _():
            finish(acc[...])

    in_specs = [a_spec, b_spec]
    args = [a, b]
    nbytes = _nbytes(a_blk, a.dtype) + _nbytes(b_blk, b.dtype) + 3 * _nbytes((tm, tn), F32)
    if has_add:
        in_specs.append(pl.BlockSpec((tm, tn), lambda i, j, k: (i, j)))
        args.append(add)
        nbytes += _nbytes((tm, tn), F32)
    if after is not None:
        in_specs.append(pl.BlockSpec(after.shape, lambda i, j, k: (0, 0)))
        args.append(after)
    if has_gain:
        in_specs.append(pl.BlockSpec((1, tn), lambda i, j, k: (0, 0)))
        args.append(norm_gain)
        nbytes += _nbytes((tm, tn), F32)
    grid = (M // tm, N // tn, nk)
    if n_outer:
        def swapped(spec):
            return pl.BlockSpec(spec.block_shape, lambda j, i, k, at=spec.index_map: at(i, j, k))

        grid = (N // tn, M // tm, nk)
        in_specs = [swapped(s) for s in in_specs]
        o_spec = swapped(o_spec)
    out_sds, out_specs = jax.ShapeDtypeStruct(out_shape, out_dtype), o_spec
    if copy_dtype is not None:
        out_sds, out_specs = (out_sds, jax.ShapeDtypeStruct(out_shape, copy_dtype)), (o_spec, o_spec)
    return pl.pallas_call(
        body, name=name, out_shape=out_sds, grid=grid, in_specs=in_specs, out_specs=out_specs,
        scratch_shapes=[pltpu.VMEM((tm, tn), F32)] if nk > 1 else [],
        compiler_params=_params(("parallel", "parallel", "arbitrary"), nbytes),
    )(*args)


_GELU_C = math.sqrt(2.0 / math.pi)
_GELU_A = 0.044715


def _sigmoid(x):
    return 0.5 * jnp.tanh(0.5 * x) + 0.5


def _gelu(x):
    t = jnp.tanh(x * (_GELU_C + (_GELU_C * _GELU_A) * (x * x)))
    return x * (0.5 + 0.5 * t)


def _gelu_and_grad(x):
    x2 = x * x
    t = jnp.tanh(x * (_GELU_C + (_GELU_C * _GELU_A) * x2))
    cdf = 0.5 + 0.5 * t
    grad = cdf + (0.5 * x) * (1.0 - t * t) * (_GELU_C + (3.0 * _GELU_C * _GELU_A) * x2)
    return x * cdf, grad


def _rope_mix(g, cos_a, sin_a):
    return g * cos_a + pltpu.roll(g, 64, 1) * sin_a


def _rope_mix_bwd(d, cos_a, sin_a):
    return d * cos_a + pltpu.roll(d * sin_a, 64, 1)


def _rms_fwd(x, g, name, tr=1024):
    T, D = x.shape

    def body(x_ref, g_ref, h_ref):
        xv = x_ref[...]
        r = lax.rsqrt(jnp.mean(xv * xv, axis=-1, keepdims=True) + EPS)
        h_ref[...] = ((xv * r) * g_ref[...]).astype(h_ref.dtype)

    return pl.pallas_call(
        body, name=name, out_shape=jax.ShapeDtypeStruct((T, D), MXU_DTYPE), grid=(T // tr,),
        in_specs=[pl.BlockSpec((tr, D), lambda i: (i, 0)), pl.BlockSpec((1, D), lambda i: (0, 0))],
        out_specs=pl.BlockSpec((tr, D), lambda i: (i, 0)),
        compiler_params=_params(("parallel",), 3 * _nbytes((tr, D), F32)),
    )(x, g)


def _rms_bwd(x, g, dh, dres, name, tr=512):
    T, D = x.shape

    def body(x_ref, g_ref, dh_ref, dres_ref, dx_ref, gg_ref):
        @pl.when(pl.program_id(0) == 0)
        def _():
            gg_ref[...] = jnp.zeros_like(gg_ref)

        xv = x_ref[...]
        r = lax.rsqrt(jnp.mean(xv * xv, axis=-1, keepdims=True) + EPS)
        xn = xv * r
        dhv = dh_ref[...]
        dxn = dhv * g_ref[...]
        dx_ref[...] = dres_ref[...] + r * (dxn - xn * jnp.mean(dxn * xn, axis=-1, keepdims=True))
        gg_ref[...] += jnp.sum(dhv * xn, axis=0, keepdims=True)

    row = pl.BlockSpec((tr, D), lambda i: (i, 0))
    vec = pl.BlockSpec((1, D), lambda i: (0, 0))
    return pl.pallas_call(
        body, name=name,
        out_shape=(jax.ShapeDtypeStruct((T, D), F32), jax.ShapeDtypeStruct((1, D), F32)),
        grid=(T // tr,), in_specs=[row, vec, row, row], out_specs=(row, vec),
        compiler_params=_params(("arbitrary",), 6 * _nbytes((tr, D), F32)),
    )(x, g, dh, dres)


def _lat_fwd(z, gq, gkv, wq, wkv, cos_a, sin_a, tr=512):
    T = z.shape[0]
    lat_blk = (4 * D_MODEL) // LAT

    def body(z_ref, gq_ref, gkv_ref, wq_ref, wkv_ref, cos_ref, sin_ref, q_ref, k_ref, v_ref, cqn_ref, ckvn_ref):
        zl = z_ref[...]
        cos_v, sin_v = cos_ref[...], sin_ref[...]
        cq = zl[:, :Q_RANK]
        ckv = zl[:, Q_RANK:Q_RANK + KV_RANK]
        krb = zl[:, Q_RANK + KV_RANK:]
        cqn = ((cq * lax.rsqrt(jnp.mean(cq * cq, axis=-1, keepdims=True) + EPS)) * gq_ref[...]).astype(MXU_DTYPE)
        ckvn = ((ckv * lax.rsqrt(jnp.mean(ckv * ckv, axis=-1, keepdims=True) + EPS)) * gkv_ref[...]).astype(MXU_DTYPE)
        cqn_ref[...] = cqn
        ckvn_ref[...] = ckvn
        krr = _rope_mix(krb, cos_v, sin_v).astype(MXU_DTYPE)
        q = jnp.dot(cqn, wq_ref[...], preferred_element_type=F32)
        kv = jnp.dot(ckvn, wkv_ref[...], preferred_element_type=F32)
        for h in range(HEADS):
            o = h * HEAD_PAD
            q_ref[:, o:o + NOPE] = q[:, o:o + NOPE].astype(MXU_DTYPE)
            q_ref[:, o + NOPE:o + HEAD_PAD] = _rope_mix(q[:, o + NOPE:o + HEAD_PAD], cos_v, sin_v).astype(MXU_DTYPE)
            k_ref[:, o:o + NOPE] = kv[:, h * NOPE:(h + 1) * NOPE].astype(MXU_DTYPE)
            k_ref[:, o + NOPE:o + HEAD_PAD] = krr
        v_ref[...] = kv[:, HEADS * NOPE:].astype(MXU_DTYPE)

    def row(w):
        return pl.BlockSpec((tr, w), lambda i: (i, 0))

    def full(a):
        return pl.BlockSpec(a.shape, lambda i: (0, 0))

    return pl.pallas_call(
        body, name="lat_fwd",
        out_shape=(jax.ShapeDtypeStruct((T, HEADS * HEAD_PAD), MXU_DTYPE), jax.ShapeDtypeStruct((T, HEADS * HEAD_PAD), MXU_DTYPE),
                   jax.ShapeDtypeStruct((T, HEADS * NOPE), MXU_DTYPE), jax.ShapeDtypeStruct((T, Q_RANK), MXU_DTYPE),
                   jax.ShapeDtypeStruct((T, KV_RANK), MXU_DTYPE)),
        grid=(T // tr,),
        in_specs=[pl.BlockSpec((tr, LAT), lambda i: (i, lat_blk)), full(gq), full(gkv), full(wq), full(wkv), row(128), row(128)],
        out_specs=(row(HEADS * HEAD_PAD), row(HEADS * HEAD_PAD), row(HEADS * NOPE), row(Q_RANK), row(KV_RANK)),
        compiler_params=_params(("parallel",), 8 * _nbytes((tr, HEADS * HEAD_PAD), F32)),
    )(z, gq, gkv, wq, wkv, cos_a, sin_a)


ATT_BLOCK = 256
_SCALE = QK_DIM ** -0.5


def _causal_mask(n):
    return lax.broadcasted_iota(jnp.int32, (n, n), 1) <= lax.broadcasted_iota(jnp.int32, (n, n), 0)


def _causal_mask_t(n):
    return lax.broadcasted_iota(jnp.int32, (n, n), 0) <= lax.broadcasted_iota(jnp.int32, (n, n), 1)


ATT_HEADS = 4
ATT_HEADS_FWD = HEADS


def _attn_fwd(q, k, v, B, S):
    tq = ATT_BLOCK
    nq = S // tq
    T = B * S
    hp, groups = ATT_HEADS_FWD, HEADS // ATT_HEADS_FWD

    def body(q_ref, k_ref, v_ref, o_ref, *lse_refs):
        qi = pl.program_id(2)
        qs = [q_ref[:, t * HEAD_PAD:(t + 1) * HEAD_PAD] for t in range(hp)]

        def scores(j, t):
            rows = pl.ds(pl.multiple_of(j * tq, tq), tq)
            return lax.dot_general(k_ref[rows, t * HEAD_PAD:(t + 1) * HEAD_PAD], qs[t], _DIMS["nt"],
                                   preferred_element_type=F32)

        def step(j, carry, last):
            rows = pl.ds(pl.multiple_of(j * tq, tq), tq)
            out = []
            for t in range(hp):
                m, l, acc, st = carry[t]
                st_next = st if last else scores(j + 1, t)
                st = st * _SCALE
                if last:
                    st = jnp.where(_causal_mask_t(tq), st, NEG)
                m_new = jnp.maximum(m, jnp.max(st, axis=0, keepdims=True))
                alpha = jnp.exp(m - m_new)
                p = jnp.exp(st - m_new)
                l = alpha * l + jnp.sum(p, axis=0, keepdims=True)
                acc = alpha * acc + lax.dot_general(v_ref[rows, t * NOPE:(t + 1) * NOPE], p.astype(MXU_DTYPE),
                                                    _DIMS["tn"], preferred_element_type=F32)
                out.append((m_new, l, acc, st_next))
            return tuple(out)

        init = tuple((jnp.full((1, tq), NEG, F32), jnp.zeros((1, tq), F32), jnp.zeros((NOPE, tq), F32), scores(0, t))
                     for t in range(hp))
        carry = lax.fori_loop(0, qi, lambda j, c: step(j, c, False), init)
        carry = step(qi, carry, True)
        for t in range(hp):
            m, l, acc, _ = carry[t]
            o_ref[:, t * NOPE:(t + 1) * NOPE] = (acc / l).T
            lse_refs[t][0] = m + jnp.log(l)

    lse_sds = jax.ShapeDtypeStruct((groups * B * nq, 1, tq), F32)
    lse_spec = pl.BlockSpec((1, 1, tq), lambda b, h, i: ((h * B + b) * nq + i, 0, 0))
    return pl.pallas_call(
        body, name="attn_fwd",
        out_shape=(jax.ShapeDtypeStruct((T, HEADS * NOPE), F32),) + (lse_sds,) * hp,
        grid=(B, groups, nq),
        in_specs=[pl.BlockSpec((tq, hp * HEAD_PAD), lambda b, h, i: (b * nq + i, h)),
                  pl.BlockSpec((S, hp * HEAD_PAD), lambda b, h, i: (b, h)),
                  pl.BlockSpec((S, hp * NOPE), lambda b, h, i: (b, h))],
        out_specs=(pl.BlockSpec((tq, hp * NOPE), lambda b, h, i: (b * nq + i, h)),) + (lse_spec,) * hp,
        compiler_params=_params(("parallel", "parallel", "arbitrary"), 4 * hp * _nbytes((S, HEAD_PAD), MXU_DTYPE)),
    )(q, k, v)


def _attn_bwd(q, k, v, do, lses, delta, B, S):
    tq = ATT_BLOCK
    nq = S // tq
    T = B * S
    hp, groups = ATT_HEADS, HEADS // ATT_HEADS
    assert groups == 2 and len(lses) == HEADS

    def body(q_ref, k_ref, v_ref, do_ref, *refs):
        lse_refs, dl_refs = refs[:HEADS], refs[HEADS:HEADS + hp]
        dq_out, dk_ref, dv_ref, dq_ref = refs[HEADS + hp:]
        kj = pl.program_id(2)
        upper = pl.program_id(1) == 1

        def lse(t, i):
            return jnp.where(upper, lse_refs[hp + t][i], lse_refs[t][i])

        @pl.when(kj == 0)
        def _():
            dq_ref[...] = jnp.zeros_like(dq_ref)

        def products(i, t):
            rows = pl.ds(pl.multiple_of(i * tq, tq), tq)
            st = lax.dot_general(k_ref[:, t * HEAD_PAD:(t + 1) * HEAD_PAD], q_ref[rows, t * HEAD_PAD:(t + 1) * HEAD_PAD],
                                 _DIMS["nt"], preferred_element_type=F32)
            dpt = lax.dot_general(v_ref[:, t * NOPE:(t + 1) * NOPE], do_ref[rows, t * NOPE:(t + 1) * NOPE],
                                  _DIMS["nt"], preferred_element_type=F32)
            return st, dpt

        def step(i, carry, masked, prefetch):
            rows = pl.ds(i * tq if isinstance(i, int) else pl.multiple_of(i * tq, tq), tq)
            out = []
            for t in range(hp):
                dk, dv, st, dpt = carry[t]
                st_next, dpt_next = products(i + 1, t) if prefetch else (st, dpt)
                qk_cols = slice(t * HEAD_PAD, (t + 1) * HEAD_PAD)
                v_cols = slice(t * NOPE, (t + 1) * NOPE)
                p = jnp.exp(st * _SCALE - lse(t, i))
                if masked:
                    p = jnp.where(_causal_mask_t(tq), p, 0.0)
                dv = dv + jnp.dot(p.astype(MXU_DTYPE), do_ref[rows, v_cols], preferred_element_type=F32)
                ds = (p * (dpt - dl_refs[t][i]) * _SCALE).astype(MXU_DTYPE)
                dk = dk + jnp.dot(ds, q_ref[rows, qk_cols], preferred_element_type=F32)
                dq_ref[rows, qk_cols] += lax.dot_general(ds, k_ref[:, qk_cols], _DIMS["tn"], preferred_element_type=F32)
                out.append((dk, dv, st_next, dpt_next))
            return tuple(out)

        def first_products():
            return tuple((jnp.zeros((tq, HEAD_PAD), F32), jnp.zeros((tq, NOPE), F32)) + products(kj, t) for t in range(hp))

        def finish(carry):
            for t in range(hp):
                dk_ref[:, t * HEAD_PAD:(t + 1) * HEAD_PAD] = carry[t][0].astype(dk_ref.dtype)
                dv_ref[:, t * NOPE:(t + 1) * NOPE] = carry[t][1].astype(dv_ref.dtype)

        @pl.when(kj < nq - 1)
        def _():
            carry = step(kj, first_products(), True, True)
            carry = lax.fori_loop(kj + 1, nq - 1, lambda i, c: step(i, c, False, True), carry)
            finish(step(nq - 1, carry, False, False))

        @pl.when(kj == nq - 1)
        def _():
            finish(step(kj, first_products(), True, False))
            dq_out[...] = dq_ref[...].astype(dq_out.dtype)

    seq = lambda w: pl.BlockSpec((S, w), lambda b, h, j: (b, h))
    blk = lambda w: pl.BlockSpec((tq, w), lambda b, h, j: (b * nq + j, h))
    lse_spec = pl.BlockSpec((nq, 1, tq), lambda b, h, j: (b, 0, 0))
    dl_specs = [pl.BlockSpec((nq, 1, tq), lambda b, h, j, t=t: ((h * hp + t) * B + b, 0, 0)) for t in range(hp)]
    return pl.pallas_call(
        body, name="attn_bwd",
        out_shape=(jax.ShapeDtypeStruct((T, HEADS * HEAD_PAD), MXU_DTYPE), jax.ShapeDtypeStruct((T, HEADS * HEAD_PAD), MXU_DTYPE),
                   jax.ShapeDtypeStruct((T, HEADS * NOPE), MXU_DTYPE)),
        grid=(B, groups, nq),
        in_specs=[seq(hp * HEAD_PAD), blk(hp * HEAD_PAD), blk(hp * NOPE), seq(hp * NOPE)] + [lse_spec] * HEADS + dl_specs,
        out_specs=(seq(hp * HEAD_PAD), blk(hp * HEAD_PAD), blk(hp * NOPE)),
        scratch_shapes=[pltpu.VMEM((S, hp * HEAD_PAD), F32)],
        compiler_params=_params(("parallel", "parallel", "arbitrary"), 8 * hp * _nbytes((S, HEAD_PAD), F32)),
    )(q, k, v, do, *lses, *([delta] * hp))


MIX_ROWS = 256


def _tril_weights(ws_ref, g):
    return jnp.where(_causal_mask(CHUNK), ws_ref[g], 0.0).astype(MXU_DTYPE)


def _layer_norm_stats(va):
    mu = jnp.mean(va, axis=-1, keepdims=True)
    xc = va - mu
    rs = lax.rsqrt(jnp.mean(xc * xc, axis=-1, keepdims=True) + EPS)
    return xc * rs


def _mix_specs(tr):
    zcol = lambda c: pl.BlockSpec((tr, D_MODEL), lambda i, c=c: (i, c))
    row = pl.BlockSpec((tr, D_MODEL), lambda i: (i, 0))
    vec = pl.BlockSpec((1, D_MODEL), lambda i: (0, 0))
    ws = pl.BlockSpec((A_GROUPS, CHUNK, CHUNK), lambda i: (0, 0, 0))
    bs = pl.BlockSpec((CHUNK, 128), lambda i: (0, 0))
    return zcol, row, vec, ws, bs


def _mix_fwd(z, yb, ln_g, ln_b, ws, bs_t):
    T = z.shape[0]
    tr = MIX_ROWS
    zcol, row, vec, ws_spec, bs_spec = _mix_specs(tr)

    def body(zu_ref, zv_ref, zga_ref, zgb_ref, yb_ref, g_ref, b_ref, ws_ref, bs_ref, out_ref, vn_s):
        vhat = _layer_norm_stats(_gelu(zv_ref[...]))
        vn_s[...] = (vhat * g_ref[...] + b_ref[...]).astype(MXU_DTYPE)
        for g in range(A_GROUPS):
            w = _tril_weights(ws_ref, g)
            bias = bs_ref[:, g:g + 1]
            cols = slice(g * CHUNK, (g + 1) * CHUNK)
            for c in range(tr // CHUNK):
                rows = slice(c * CHUNK, (c + 1) * CHUNK)
                mixed = jnp.dot(w, vn_s[rows, cols], preferred_element_type=F32) + bias
                ya = _gelu(zu_ref[rows, cols]) * mixed
                merged = _sigmoid(zga_ref[rows, cols]) * ya + _sigmoid(zgb_ref[rows, cols]) * yb_ref[rows, cols]
                out_ref[rows, cols] = merged.astype(MXU_DTYPE)

    return pl.pallas_call(
        body, name="mix_fwd", out_shape=jax.ShapeDtypeStruct((T, D_MODEL), MXU_DTYPE), grid=(T // tr,),
        in_specs=[zcol(0), zcol(1), zcol(2), zcol(3), row, vec, vec, ws_spec, bs_spec], out_specs=row,
        scratch_shapes=[pltpu.VMEM((tr, D_MODEL), MXU_DTYPE)],
        compiler_params=_params(("parallel",), 8 * _nbytes((tr, D_MODEL), F32)),
    )(z, z, z, z, yb, ln_g, ln_b, ws, bs_t)


def _mix_bwd(z, yb, dm, ln_g, ln_b, ws, bs_t):
    T = z.shape[0]
    tr = MIX_ROWS
    zcol, row, vec, ws_spec, bs_spec = _mix_specs(tr)

    def body(zu_ref, zv_ref, zga_ref, zgb_ref, yb_ref, dm_ref, g_ref, b_ref, ws_ref, bs_ref,
             dz_ref, dyb_ref, dl_ref, gws_ref, gbs_ref, glg_ref, glb_ref, vn_s, dvn_s):
        @pl.when(pl.program_id(0) == 0)
        def _():
            gws_ref[...] = jnp.zeros_like(gws_ref)
            gbs_ref[...] = jnp.zeros_like(gbs_ref)
            glg_ref[...] = jnp.zeros_like(glg_ref)
            glb_ref[...] = jnp.zeros_like(glb_ref)

        lane = lax.broadcasted_iota(jnp.int32, (CHUNK, 128), 1)
        va, dgelu_v = _gelu_and_grad(zv_ref[...])
        mu = jnp.mean(va, axis=-1, keepdims=True)
        xc = va - mu
        rs = lax.rsqrt(jnp.mean(xc * xc, axis=-1, keepdims=True) + EPS)
        vhat = xc * rs
        vn_s[...] = (vhat * g_ref[...] + b_ref[...]).astype(MXU_DTYPE)
        gbs_acc = jnp.zeros((CHUNK, 128), F32)
        for g in range(A_GROUPS):
            w = _tril_weights(ws_ref, g)
            bias = bs_ref[:, g:g + 1]
            cols = slice(g * CHUNK, (g + 1) * CHUNK)
            gw_acc = jnp.zeros((CHUNK, CHUNK), F32)
            for c in range(tr // CHUNK):
                rows = slice(c * CHUNK, (c + 1) * CHUNK)
                vn = vn_s[rows, cols]
                mixed = jnp.dot(w, vn, preferred_element_type=F32) + bias
                ua, dgelu_u = _gelu_and_grad(zu_ref[rows, cols])
                dmv = dm_ref[rows, cols]
                sa = _sigmoid(zga_ref[rows, cols])
                dya = dmv * sa
                dz_ref[rows, 2 * D_MODEL + g * CHUNK:2 * D_MODEL + (g + 1) * CHUNK] = (
                    dmv * (ua * mixed) * (sa * (1.0 - sa))).astype(dz_ref.dtype)
                dz_ref[rows, cols] = (dya * mixed * dgelu_u).astype(dz_ref.dtype)
                dmix = dya * ua
                gbs_acc = gbs_acc + jnp.where(lane == g, jnp.sum(dmix, axis=-1, keepdims=True), 0.0)
                dmix_b = dmix.astype(MXU_DTYPE)
                gw_acc = gw_acc + lax.dot_general(dmix_b, vn, _DIMS["nt"], preferred_element_type=F32)
                dvn_s[rows, cols] = lax.dot_general(w, dmix_b, _DIMS["tn"], preferred_element_type=F32)
            gws_ref[g] += jnp.where(_causal_mask(CHUNK), gw_acc, 0.0)
        gbs_ref[...] += gbs_acc

        dvn = dvn_s[...]
        glg_ref[...] += jnp.sum(dvn * vhat, axis=0, keepdims=True)
        glb_ref[...] += jnp.sum(dvn, axis=0, keepdims=True)
        dvh = dvn * g_ref[...]
        dva = rs * (dvh - jnp.mean(dvh, axis=-1, keepdims=True) - vhat * jnp.mean(dvh * vhat, axis=-1, keepdims=True))
        dz_ref[:, D_MODEL:2 * D_MODEL] = (dva * dgelu_v).astype(dz_ref.dtype)

        dmv = dm_ref[...]
        ybv = yb_ref[...]
        sb = _sigmoid(zgb_ref[...])
        dyb = dmv * sb
        dyb_ref[...] = dyb.astype(dyb_ref.dtype)
        dz_ref[:, 3 * D_MODEL:4 * D_MODEL] = (dmv * ybv * (sb * (1.0 - sb))).astype(dz_ref.dtype)
        dz_ref[:, 4 * D_MODEL:] = jnp.zeros((tr, LAT), dz_ref.dtype)
        prod = dyb * ybv
        sel = (lax.broadcasted_iota(jnp.int32, (HEADS, D_MODEL), 1) // NOPE
               == lax.broadcasted_iota(jnp.int32, (HEADS, D_MODEL), 0)).astype(jnp.bfloat16)
        hi = prod.astype(jnp.bfloat16)
        rest = prod - hi.astype(F32)
        mid = rest.astype(jnp.bfloat16)
        lo = (rest - mid.astype(F32)).astype(jnp.bfloat16)
        dl_ref[...] = (lax.dot_general(sel, hi, _DIMS["nt"], preferred_element_type=F32)
                       + lax.dot_general(sel, mid, _DIMS["nt"], preferred_element_type=F32)
                       + lax.dot_general(sel, lo, _DIMS["nt"], preferred_element_type=F32))

    return pl.pallas_call(
        body, name="mix_bwd",
        out_shape=(jax.ShapeDtypeStruct((T, IN_PAD), MXU_DTYPE), jax.ShapeDtypeStruct((T, D_MODEL), MXU_DTYPE),
                   jax.ShapeDtypeStruct((HEADS, T), F32), jax.ShapeDtypeStruct((A_GROUPS, CHUNK, CHUNK), F32),
                   jax.ShapeDtypeStruct((CHUNK, 128), F32), jax.ShapeDtypeStruct((1, D_MODEL), F32),
                   jax.ShapeDtypeStruct((1, D_MODEL), F32)),
        grid=(T // tr,),
        in_specs=[zcol(0), zcol(1), zcol(2), zcol(3), row, row, vec, vec, ws_spec, bs_spec],
        out_specs=(pl.BlockSpec((tr, IN_PAD), lambda i: (i, 0)), row, pl.BlockSpec((HEADS, tr), lambda i: (0, i)),
                   ws_spec, bs_spec, vec, vec),
        scratch_shapes=[pltpu.VMEM((tr, D_MODEL), MXU_DTYPE), pltpu.VMEM((tr, D_MODEL), F32)],
        compiler_params=_params(("arbitrary",), 12 * _nbytes((tr, D_MODEL), F32)),
    )(z, z, z, z, yb, dm, ln_g, ln_b, ws, bs_t)


def _lat_bwd(dz, z, dq, dk, dv, gq, gkv, wq, wkv, cos_a, sin_a, tr=256):
    T = z.shape[0]
    lat_blk = (4 * D_MODEL) // LAT

    def body(dz_in, z_ref, dq_ref, dk_ref, dv_ref, gq_ref, gkv_ref, wq_ref, wkv_ref, cos_ref, sin_ref,
             dz_ref, dqr_ref, dkv_ref, ggq_ref, ggkv_ref):
        del dz_in

        @pl.when(pl.program_id(0) == 0)
        def _():
            ggq_ref[...] = jnp.zeros_like(ggq_ref)
            ggkv_ref[...] = jnp.zeros_like(ggkv_ref)

        cos_v, sin_v = cos_ref[...], sin_ref[...]
        dkr = jnp.zeros((tr, 128), F32)
        for h in range(HEADS):
            o = h * HEAD_PAD
            dqr_ref[:, o:o + NOPE] = dq_ref[:, o:o + NOPE].astype(MXU_DTYPE)
            dqr_ref[:, o + NOPE:o + HEAD_PAD] = _rope_mix_bwd(dq_ref[:, o + NOPE:o + HEAD_PAD], cos_v, sin_v).astype(MXU_DTYPE)
            dkv_ref[:, h * NOPE:(h + 1) * NOPE] = dk_ref[:, o:o + NOPE].astype(MXU_DTYPE)
            dkr = dkr + _rope_mix_bwd(dk_ref[:, o + NOPE:o + HEAD_PAD], cos_v, sin_v)
        dkv_ref[:, HEADS * NOPE:] = dv_ref[...]
        dcqn = lax.dot_general(dqr_ref[...], wq_ref[...], _DIMS["nt"], preferred_element_type=F32)
        dckvn = lax.dot_general(dkv_ref[...], wkv_ref[...], _DIMS["nt"], preferred_element_type=F32)

        zl = z_ref[...]

        def rms_bwd(c, dn, g_ref, gg_ref):
            r = lax.rsqrt(jnp.mean(c * c, axis=-1, keepdims=True) + EPS)
            ch = c * r
            gg_ref[...] += jnp.sum(dn * ch, axis=0, keepdims=True)
            dch = dn * g_ref[...]
            return r * (dch - ch * jnp.mean(dch * ch, axis=-1, keepdims=True))

        dz_ref[:, :Q_RANK] = rms_bwd(zl[:, :Q_RANK], dcqn, gq_ref, ggq_ref).astype(dz_ref.dtype)
        dz_ref[:, Q_RANK:Q_RANK + KV_RANK] = rms_bwd(zl[:, Q_RANK:Q_RANK + KV_RANK], dckvn, gkv_ref, ggkv_ref).astype(dz_ref.dtype)
        dz_ref[:, Q_RANK + KV_RANK:] = dkr.astype(dz_ref.dtype)

    def row(w):
        return pl.BlockSpec((tr, w), lambda i: (i, 0))

    def full(a):
        return pl.BlockSpec(a.shape, lambda i: (0, 0))

    lat = pl.BlockSpec((tr, LAT), lambda i: (i, lat_blk))
    return pl.pallas_call(
        body, name="lat_bwd",
        out_shape=(jax.ShapeDtypeStruct(dz.shape, dz.dtype), jax.ShapeDtypeStruct((T, HEADS * HEAD_PAD), MXU_DTYPE),
                   jax.ShapeDtypeStruct((T, 2 * HEADS * NOPE), MXU_DTYPE), jax.ShapeDtypeStruct(gq.shape, F32),
                   jax.ShapeDtypeStruct(gkv.shape, F32)),
        grid=(T // tr,),
        in_specs=[pl.BlockSpec(memory_space=pl.ANY), lat, row(HEADS * HEAD_PAD), row(HEADS * HEAD_PAD), row(HEADS * NOPE),
                  full(gq), full(gkv), full(wq), full(wkv), row(128), row(128)],
        out_specs=(lat, row(HEADS * HEAD_PAD), row(2 * HEADS * NOPE), full(gq), full(gkv)),
        input_output_aliases={0: 0},
        compiler_params=_params(("arbitrary",), 8 * _nbytes((tr, HEADS * HEAD_PAD), F32)),
    )(dz, z, dq, dk, dv, gq, gkv, wq, wkv, cos_a, sin_a)


GATE_ROWS = 256
HALO = 8


def _taps(ref, half, r, first):
    C = GATE_ROWS
    if first:
        xs = jnp.concatenate([jnp.zeros((HALO, ref.shape[-1]), F32), ref[half, 0:C, :]], axis=0)
    else:
        xs = ref[half, pl.ds(pl.multiple_of(r * C - HALO, HALO), C + HALO), :]
    return xs[HALO:, :], pltpu.roll(xs, 1, 0)[HALO:, :], pltpu.roll(xs, 2, 0)[HALO:, :]


def _conv_taps(taps, cw, cb):
    x0, x1, x2 = taps
    return cb + cw[0:1, :] * x2 + cw[1:2, :] * x1 + cw[2:3, :] * x0


def _fold8(x):
    acc = x[0:8, :]
    for i in range(1, x.shape[0] // 8):
        acc = acc + x[8 * i:8 * (i + 1), :]
    return acc


def _gate_fwd(up3, conv_w, conv_b, B, S):
    T = B * S
    W = FF_TILE
    C = GATE_ROWS

    def body(up_ref, cw_ref, cb_ref, act_ref, conv_ref):
        def chunk(r, first):
            gate = _conv_taps(_taps(up_ref, 0, r, first), cw_ref[0], cb_ref[0])
            val = _conv_taps(_taps(up_ref, 1, r, first), cw_ref[1], cb_ref[1])
            rows = pl.ds(0 if first else pl.multiple_of(r * C, C), C)
            conv_ref[0, rows, :] = gate.astype(conv_ref.dtype)
            conv_ref[1, rows, :] = val.astype(conv_ref.dtype)
            act_ref[rows, :] = (gate * _sigmoid(gate) * val).astype(act_ref.dtype)

        chunk(0, True)

        @pl.loop(1, S // C)
        def _(r):
            chunk(r, False)

    up_spec = pl.BlockSpec((2, S, W), lambda b, j: (0, b, j))
    return pl.pallas_call(
        body, name="gate_fwd",
        out_shape=(jax.ShapeDtypeStruct((T, D_FF), MXU_DTYPE), jax.ShapeDtypeStruct((2, T, D_FF), MXU_DTYPE)),
        grid=(B, N_FF_TILES),
        in_specs=[up_spec, pl.BlockSpec((2, 3, W), lambda b, j: (0, 0, j)), pl.BlockSpec((2, 1, W), lambda b, j: (0, 0, j))],
        out_specs=(pl.BlockSpec((S, W), lambda b, j: (b, j)), up_spec),
        compiler_params=_params(("parallel", "parallel"), 8 * _nbytes((S, W), F32)),
    )(up3, conv_w, conv_b)


def _gate_bwd(up3, conv3, dact, conv_w, B, S):
    T = B * S
    W = FF_TILE
    C = GATE_ROWS

    def body(up_ref, conv_ref, da_ref, cw_ref, dup_ref, gcw_ref, gcb_ref, d_s):
        @pl.when(pl.program_id(1) == 0)
        def _():
            gcw_ref[...] = jnp.zeros_like(gcw_ref)
            gcb_ref[...] = jnp.zeros_like(gcb_ref)

        @pl.loop(0, S // C)
        def _(r):
            rows = pl.ds(pl.multiple_of(r * C, C), C)
            gate, val = conv_ref[0, rows, :].astype(F32), conv_ref[1, rows, :].astype(F32)
            sg = _sigmoid(gate)
            da = da_ref[rows, :]
            d_s[0, rows, :] = da * val * (sg * (1.0 + gate * (1.0 - sg)))
            d_s[1, rows, :] = da * (gate * sg)

        d_s[:, S:S + HALO, :] = jnp.zeros((2, HALO, W), F32)

        def chunk(r, sums):
            base = pl.multiple_of(r * C, C)
            out = []
            for half in (0, 1):
                ds_ = d_s[half, pl.ds(base, C + HALO), :]
                d0, d1, d2 = ds_[:C, :], pltpu.roll(ds_, C + HALO - 1, 0)[:C, :], pltpu.roll(ds_, C + HALO - 2, 0)[:C, :]
                cw = cw_ref[half]
                dup_ref[half, pl.ds(base, C), :] = (cw[2:3, :] * d0 + cw[1:2, :] * d1 + cw[0:1, :] * d2).astype(dup_ref.dtype)
                x = up_ref[half, pl.ds(base, C), :]
                sb, s0, s1, s2 = sums[half]
                out.append((sb + _fold8(d0), s0 + _fold8(d2 * x), s1 + _fold8(d1 * x), s2 + _fold8(d0 * x)))
            return tuple(out)

        zeros = tuple(tuple(jnp.zeros((8, W), F32) for _ in range(4)) for _ in range(2))
        sums = lax.fori_loop(0, S // C, chunk, zeros)
        for half in (0, 1):
            sb, s0, s1, s2 = sums[half]
            gcb_ref[half] += jnp.sum(sb, axis=0, keepdims=True)
            gcw_ref[half, 0:1, :] += jnp.sum(s0, axis=0, keepdims=True)
            gcw_ref[half, 1:2, :] += jnp.sum(s1, axis=0, keepdims=True)
            gcw_ref[half, 2:3, :] += jnp.sum(s2, axis=0, keepdims=True)

    up_spec = pl.BlockSpec((2, S, W), lambda j, b: (0, b, j))
    cw_spec = pl.BlockSpec((2, 3, W), lambda j, b: (0, 0, j))
    cb_spec = pl.BlockSpec((2, 1, W), lambda j, b: (0, 0, j))
    return pl.pallas_call(
        body, name="gate_bwd",
        out_shape=(jax.ShapeDtypeStruct((2, T, D_FF), MXU_DTYPE), jax.ShapeDtypeStruct((2, 3, D_FF), F32),
                   jax.ShapeDtypeStruct((2, 1, D_FF), F32)),
        grid=(N_FF_TILES, B),
        in_specs=[up_spec, up_spec, pl.BlockSpec((S, W), lambda j, b: (b, j)), cw_spec],
        out_specs=(up_spec, cw_spec, cb_spec),
        scratch_shapes=[pltpu.VMEM((2, S + HALO, W), F32)],
        compiler_params=_params(("parallel", "arbitrary"), 12 * _nbytes((S, W), F32)),
    )(up3, conv3, dact, conv_w)


def _final(x2, tgt, g, tr=512):
    T, D = x2.shape

    def body(x_ref, t_ref, g_ref, dx_ref, loss_ref, gg_ref):
        @pl.when(pl.program_id(0) == 0)
        def _():
            loss_ref[...] = jnp.zeros_like(loss_ref)
            gg_ref[...] = jnp.zeros_like(gg_ref)

        xv = x_ref[...]
        gv = g_ref[...]
        r = lax.rsqrt(jnp.mean(xv * xv, axis=-1, keepdims=True) + EPS)
        xn = xv * r
        err = xn * gv - t_ref[...]
        loss_ref[...] += 0.5 * jnp.sum(jnp.mean(err * err, axis=-1, keepdims=True), axis=0, keepdims=True)
        dy = err * (1.0 / D)
        gg_ref[...] += jnp.sum(dy * xn, axis=0, keepdims=True)
        dxn = dy * gv
        dx_ref[...] = r * (dxn - xn * jnp.mean(dxn * xn, axis=-1, keepdims=True))

    row = pl.BlockSpec((tr, D), lambda i: (i, 0))
    vec = pl.BlockSpec((1, D), lambda i: (0, 0))
    return pl.pallas_call(
        body, name="final_loss",
        out_shape=(jax.ShapeDtypeStruct((T, D), F32), jax.ShapeDtypeStruct((1, 128), F32), jax.ShapeDtypeStruct((1, D), F32)),
        grid=(T // tr,), in_specs=[row, row, vec],
        out_specs=(row, pl.BlockSpec((1, 128), lambda i: (0, 0)), vec),
        compiler_params=_params(("arbitrary",), 6 * _nbytes((tr, D), F32)),
    )(x2, tgt, g)


def _sum_slabs(parts, name, tr):
    rows, cols = parts[0].shape
    n = len(parts)

    def body(*refs):
        acc = refs[0][...]
        for r in refs[1:n]:
            acc = acc + r[...]
        refs[n][...] = acc

    blk = pl.BlockSpec((tr, cols), lambda i: (i, 0))
    return pl.pallas_call(
        body, name=name, out_shape=jax.ShapeDtypeStruct((rows, cols), F32), grid=(rows // tr,),
        in_specs=[blk] * n, out_specs=blk,
        compiler_params=_params(("parallel",), (n + 1) * _nbytes((tr, cols), F32)),
    )(*parts)


ADAMW_BLOCK_BYTES = 2400 * 1024


def _adamw(w, g, m, v, name, copy_grad=False):
    lead = w.ndim == 3
    rows, cols = w.shape[-2:]
    fits = [d for d in range(8, rows + 1, 8) if rows % d == 0 and d * cols * 4 <= ADAMW_BLOCK_BYTES]
    tr = max(fits) if fits else rows
    c1 = 1.0 - ADAM_B1 ** ADAM_STEP
    c2 = 1.0 - ADAM_B2 ** ADAM_STEP

    def body(w_ref, g_ref, m_ref, v_ref, d_ref, nm_ref, nv_ref, *g_out):
        gv = g_ref[...]
        nm = ADAM_B1 * m_ref[...] + (1.0 - ADAM_B1) * gv
        nv = ADAM_B2 * v_ref[...] + (1.0 - ADAM_B2) * (gv * gv)
        nm_ref[...] = nm
        nv_ref[...] = nv
        d_ref[...] = -ADAM_LR * ((nm / c1) / (jnp.sqrt(nv / c2) + ADAM_EPS) + ADAM_WD * w_ref[...])
        if copy_grad:
            g_out[0][...] = gv

    blk = pl.BlockSpec((None, tr, cols), lambda i: (0, i, 0)) if lead else pl.BlockSpec((tr, cols), lambda i: (i, 0))
    sds = jax.ShapeDtypeStruct(w.shape, F32)
    n_out = 4 if copy_grad else 3
    return pl.pallas_call(
        body, name=name, out_shape=(sds,) * n_out, grid=(rows // tr,), in_specs=[blk] * 4, out_specs=(blk,) * n_out,
        compiler_params=_params(("parallel",), (4 + n_out) * _nbytes((tr, cols), F32)),
    )(w, g, m, v)


_ANY = pl.BlockSpec(memory_space=pl.ANY)


def _place():
    x, y, c = lax.axis_index("x"), lax.axis_index("y"), lax.axis_index("c")
    chips = [(1 - x, y), (x, 1 - y), (1 - x, 1 - y)]
    return x, y, c, chips


def _forward_halves(lands):
    n = len(lands)

    def body(*refs):
        outs, send, recv = refs[n:2 * n], refs[2 * n], refs[2 * n + 1]
        x, y, c, chips = _place()
        cps = []
        for w in range(n):
            for j, (px, py) in enumerate(chips):
                landed = outs[w].at[2 * px + py, c]
                cps.append(pltpu.make_async_remote_copy(
                    src_ref=landed, dst_ref=landed, send_sem=send.at[3 * w + j], recv_sem=recv.at[3 * w + j],
                    device_id=(x, y, 1 - c), device_id_type=MESH))
        for cp in cps:
            cp.start()
        for w in range(n):
            for j, (px, py) in enumerate(chips):
                other = outs[w].at[2 * px + py, 1 - c]
                pltpu.make_async_remote_copy(src_ref=other, dst_ref=other, send_sem=send.at[3 * w + j],
                                             recv_sem=recv.at[3 * w + j], device_id=(x, y, 1 - c),
                                             device_id_type=MESH).wait_recv()
        for cp in cps:
            cp.wait_send()

    dma = lambda k: pltpu.SemaphoreType.DMA((k,))
    return pl.pallas_call(
        body, name="gather_forward_halves", out_shape=tuple(jax.ShapeDtypeStruct(a.shape, a.dtype) for a in lands),
        in_specs=[_ANY] * n, out_specs=tuple([_ANY] * n), input_output_aliases={w: w for w in range(n)},
        scratch_shapes=[dma(3 * n), dma(3 * n)],
    )(*lands)


_HBM = pl.BlockSpec(memory_space=pltpu.HBM)
_SEM = pl.BlockSpec(memory_space=pltpu.SEMAPHORE)
_EFFECT = pltpu.SideEffectType.DATAFLOW_SIDE_EFFECTING


SEMS_PER_ARRAY = 8


def _exchange_copies(srcs, lands, send, recv, mode):
    x, y, c, chips = _place()
    if mode == "halves":
        cps = []
        for w, (src, land) in enumerate(zip(srcs, lands)):
            pieces = [(src.at[c], land.at[2 * x + y, c], (px, py, c)) for px, py in chips]
            pieces.append((src, land.at[2 * x + y], (x, y, 1 - c)))
            for k, (piece, dst, peer) in enumerate(pieces):
                cps.append(pltpu.make_async_remote_copy(
                    src_ref=piece, dst_ref=dst, send_sem=send.at[SEMS_PER_ARRAY * w + k],
                    recv_sem=recv.at[SEMS_PER_ARRAY * w + k], device_id=peer, device_id_type=MESH))
        return cps
    if mode == "swap":
        return [pltpu.make_async_remote_copy(
            src_ref=src.at[:, 1 - c], dst_ref=land, send_sem=send.at[SEMS_PER_ARRAY * w],
            recv_sem=recv.at[SEMS_PER_ARRAY * w], device_id=(x, y, 1 - c), device_id_type=MESH)
            for w, (src, land) in enumerate(zip(srcs, lands))]
    if mode == "all":
        flips = [(fx, fy, fc) for fx in (0, 1) for fy in (0, 1) for fc in (0, 1)][1:]
        peers = [(x ^ fx, y ^ fy, c ^ fc) for fx, fy, fc in flips]
        slot = 4 * x + 2 * y + c
    else:
        peers = [(px, py, c) for px, py in chips] + ([(x, y, 1 - c)] if mode == "gather" else [])
        slot = 2 * x + y
    cps = []
    for w, (src, land) in enumerate(zip(srcs, lands)):
        for k, peer in enumerate(peers):
            piece = src.at[2 * peer[0] + peer[1]] if mode == "scatter" else src
            cps.append(pltpu.make_async_remote_copy(
                src_ref=piece, dst_ref=land.at[slot], send_sem=send.at[SEMS_PER_ARRAY * w + k],
                recv_sem=recv.at[SEMS_PER_ARRAY * w + k], device_id=peer, device_id_type=MESH))
    return cps


def _exchange_start(srcs, name, mode, after):
    n = len(srcs)
    if mode == "swap":
        land_shapes = [(s.shape[0],) + s.shape[2:] for s in srcs]
    else:
        lead = {"gather": (N_CHIPS,), "halves": (N_CHIPS,), "scatter": (), "all": (2 * N_CHIPS,)}[mode]
        land_shapes = [lead + s.shape for s in srcs]

    def body(*refs):
        src_refs, land_refs = refs[:n], refs[n:2 * n]
        send, recv = refs[2 * n + 1], refs[2 * n + 2]
        token = refs[-1]
        for cp in _exchange_copies(src_refs, land_refs, send, recv, mode):
            cp.start()
        token[...] = jnp.zeros_like(token)

    sems = pltpu.SemaphoreType.DMA((SEMS_PER_ARRAY * n,))
    out = pl.pallas_call(
        body, name=name,
        out_shape=(sems, sems, *[pltpu.HBM(s.shape, s.dtype) for s in srcs],
                   *[pltpu.HBM(shp, s.dtype) for shp, s in zip(land_shapes, srcs)], jax.ShapeDtypeStruct((8, 128), F32)),
        in_specs=[_HBM] * (2 * n) + [_ANY],
        out_specs=(_SEM, _SEM, *[_HBM] * (2 * n), pl.BlockSpec(memory_space=pltpu.VMEM)),
        input_output_aliases={i: 2 + i for i in range(2 * n)},
        compiler_params=pltpu.CompilerParams(has_side_effects=_EFFECT),
    )(*[pltpu.with_memory_space_constraint(s, pltpu.HBM) for s in srcs],
      *[pltpu.with_memory_space_constraint(lax.empty(shp, s.dtype), pltpu.HBM) for shp, s in zip(land_shapes, srcs)],
      after)
    return out[0], out[1], out[2:2 + n], out[2 + n:2 + 2 * n], out[-1]


def _exchange_wait(started, name, mode, after):
    send, recv, src_thru, land_thru, _ = started
    n = len(src_thru)
    after = list(after) if isinstance(after, (list, tuple)) else [after]

    def body(*refs):
        src_refs, land_refs, send_ref, recv_ref = refs[:n], refs[n:2 * n], refs[2 * n], refs[2 * n + 1]
        for cp in _exchange_copies(src_refs, land_refs, send_ref, recv_ref, mode):
            cp.wait_send()
            cp.wait_recv()

    out = pl.pallas_call(
        body, name=name,
        out_shape=tuple(pltpu.HBM(a.shape, a.dtype) for a in list(src_thru) + list(land_thru)),
        in_specs=[_HBM] * (2 * n) + [_SEM, _SEM] + [_ANY] * len(after), out_specs=tuple([_HBM] * (2 * n)),
        input_output_aliases={i: i for i in range(2 * n)},
        compiler_params=pltpu.CompilerParams(has_side_effects=_EFFECT),
    )(*src_thru, *land_thru, send, recv, *after)
    return out[:n], out[n:]


def _swap_halves(gs, name):
    n = len(gs)

    def body(*refs):
        ins, outs, send, recv = refs[:n], refs[n:2 * n], refs[2 * n], refs[2 * n + 1]
        x, y, c, _ = _place()
        cps = []
        for w in range(n):
            cps.append(pltpu.make_async_remote_copy(
                src_ref=ins[w].at[:, 1 - c], dst_ref=outs[w], send_sem=send.at[w], recv_sem=recv.at[w],
                device_id=(x, y, 1 - c), device_id_type=MESH))
        for cp in cps:
            cp.start()
        for cp in cps:
            cp.wait()

    return pl.pallas_call(
        body, name=name,
        out_shape=tuple(jax.ShapeDtypeStruct((g.shape[0],) + g.shape[2:], g.dtype) for g in gs),
        in_specs=[_ANY] * n, out_specs=tuple([_ANY] * n),
        scratch_shapes=[pltpu.SemaphoreType.DMA((n,)), pltpu.SemaphoreType.DMA((n,))],
    )(*gs)


GRAD_PAYLOAD = jnp.bfloat16


def _half_blocks(half_rows, cols):
    if (half_rows // 2) % 16 == 0:
        return (half_rows // 2, cols), (lambda r: (r, 0))
    assert cols % 256 == 0, (half_rows, cols)
    return (half_rows, cols // 2), (lambda r: (0, r))


def _pair_sum(gs, gots, name):
    n = len(gs)
    core = lax.axis_index("c").astype(jnp.int32).reshape(1)

    def body(core_ref, *refs):
        del core_ref
        for w in range(n):
            refs[2 * n + w][...] = (refs[w][...] + refs[n + w][...]).astype(GRAD_PAYLOAD)

    in_specs, out_specs, out_shape, nbytes = [], [], [], 0
    cuts = [_half_blocks(g.shape[1] // 2, g.shape[2]) for g in gs]
    for g, ((br, bc), at) in zip(gs, cuts):
        per_half = (g.shape[1] // 2) // br
        in_specs.append(pl.BlockSpec((1, br, bc), lambda s, r, core, at=at, per_half=per_half:
                                     (s, per_half * core[0] + at(r)[0], at(r)[1])))
        nbytes += 3 * _nbytes((br, bc), F32)
    for g, ((br, bc), at) in zip(gs, cuts):
        in_specs.append(pl.BlockSpec((1, br, bc), lambda s, r, core, at=at: (s,) + at(r)))
        out_specs.append(pl.BlockSpec((1, br, bc), lambda s, r, core, at=at: (s,) + at(r)))
        out_shape.append(jax.ShapeDtypeStruct((g.shape[0], g.shape[1] // 2, g.shape[2]), GRAD_PAYLOAD))
    return pl.pallas_call(
        body, name=name, out_shape=tuple(out_shape),
        grid_spec=pltpu.PrefetchScalarGridSpec(num_scalar_prefetch=1, grid=(N_CHIPS, 2), in_specs=in_specs,
                                               out_specs=tuple(out_specs)),
        compiler_params=_params(("parallel", "parallel"), nbytes),
    )(core, *gs, *gots)


def _chip_sum(ps, landed):
    n = len(ps)
    x, y, c = lax.axis_index("x"), lax.axis_index("y"), lax.axis_index("c")
    where = jnp.stack([2 * x + y, 2 * (1 - x) + y, 2 * x + (1 - y), 2 * (1 - x) + (1 - y), c]).astype(jnp.int32)

    def body(where_ref, *refs):
        del where_ref
        for w in range(n):
            terms = [refs[4 * w + t][...].astype(F32) for t in range(4)]
            refs[4 * n + w][...] = ((terms[0] + terms[1]) + terms[2]) + terms[3]

    in_specs, out_specs, out_shape, args, nbytes = [], [], [], [], 0
    for p, a in zip(ps, landed):
        (br, bc), at = _half_blocks(a.shape[1], a.shape[2])
        blk = (1, br, bc)
        in_specs.append(pl.BlockSpec(blk, lambda r, where, at=at: (where[0],) + at(r)))
        args.append(p)
        for t in (1, 2, 3):
            in_specs.append(pl.BlockSpec(blk, lambda r, where, t=t, at=at: (where[t],) + at(r)))
            args.append(a)
        out_specs.append(pl.BlockSpec(blk, lambda r, where, at=at: (where[4],) + at(r)))
        out_shape.append(jax.ShapeDtypeStruct((2,) + a.shape[1:], F32))
        nbytes += 4 * _nbytes(blk, F32)
    return pl.pallas_call(
        body, name="grad_chip_sum", out_shape=tuple(out_shape),
        grid_spec=pltpu.PrefetchScalarGridSpec(num_scalar_prefetch=1, grid=(2,), in_specs=in_specs,
                                               out_specs=tuple(out_specs)),
        compiler_params=_params(("parallel",), nbytes),
    )(where, *args)


def _join_halves(ss):
    n = len(ss)

    def body(*refs):
        outs, send, recv = refs[n:2 * n], refs[2 * n], refs[2 * n + 1]
        x, y, c, _ = _place()
        cps = []
        for w in range(n):
            cps.append(pltpu.make_async_remote_copy(
                src_ref=outs[w].at[c], dst_ref=outs[w].at[c], send_sem=send.at[w], recv_sem=recv.at[w],
                device_id=(x, y, 1 - c), device_id_type=MESH))
        for cp in cps:
            cp.start()
        for w in range(n):
            got = outs[w].at[1 - c]
            pltpu.make_async_remote_copy(src_ref=got, dst_ref=got, send_sem=send.at[w], recv_sem=recv.at[w],
                                         device_id=(x, y, 1 - c), device_id_type=MESH).wait_recv()
        for cp in cps:
            cp.wait_send()

    dma = lambda k: pltpu.SemaphoreType.DMA((k,))
    return pl.pallas_call(
        body, name="grad_join_halves",
        out_shape=tuple(jax.ShapeDtypeStruct(s.shape, s.dtype) for s in ss),
        in_specs=[_ANY] * n, out_specs=tuple([_ANY] * n), input_output_aliases={w: w for w in range(n)},
        scratch_shapes=[dma(n), dma(n)],
    )(*ss)


def _rot_cols(w, axis=-1):
    a, b = jnp.split(w, 2, axis=axis)
    return jnp.concatenate([-b, a], axis=axis)


def _rot_cols_t(g, axis=-1):
    a, b = jnp.split(g, 2, axis=axis)
    return jnp.concatenate([b, -a], axis=axis)


def _cols_from_chips(a):
    n, r, cs = a.shape
    return jnp.transpose(a, (1, 0, 2)).reshape(r, n * cs)


def _cols_to_chips(a):
    r, cc = a.shape
    return jnp.transpose(a.reshape(r, N_CHIPS, cc // N_CHIPS), (1, 0, 2))


def _conv_w_split(cw):
    return jnp.swapaxes(cw.reshape(3, 2, D_FF), 0, 1)


def _conv_w_join(g):
    return jnp.swapaxes(g, 0, 1).reshape(3, 2 * D_FF)


_SEG =(D_MODEL, 2 * D_MODEL, 2 * D_MODEL + Q_RANK, 2 * D_MODEL + Q_RANK + KV_RANK, 2 * D_MODEL + Q_RANK + KV_RANK + ROPE,
        3 * D_MODEL + Q_RANK + KV_RANK + ROPE)


def _w_in_t_to_pad(wt):
    u, v, cq, ckv, kr, ga, gb = jnp.split(wt, _SEG, axis=0)
    return jnp.concatenate([u, v, ga, gb, cq, ckv, kr, _rot_cols(kr, axis=0)], axis=0)


def _w_in_t_from_pad(gt):
    u, v, ga, gb, cq, ckv, kr, krr = jnp.split(
        gt, (D_MODEL, 2 * D_MODEL, 3 * D_MODEL, 4 * D_MODEL, 4 * D_MODEL + Q_RANK, 4 * D_MODEL + Q_RANK + KV_RANK,
             4 * D_MODEL + Q_RANK + KV_RANK + ROPE), axis=0)
    return jnp.concatenate([u, v, cq, ckv, kr + _rot_cols_t(krr, axis=0), ga, gb], axis=0)


def _w_uq_to_pad(w):
    t = w.reshape(Q_RANK, HEADS, QK_DIM)
    nope, rope = t[..., :NOPE], t[..., NOPE:]
    return jnp.concatenate([nope, rope, _rot_cols(rope)], axis=-1).reshape(Q_RANK, HEADS * HEAD_PAD)


def _w_uq_from_pad(g):
    t = g.reshape(Q_RANK, HEADS, HEAD_PAD)
    nope, rope, rot = t[..., :NOPE], t[..., NOPE:QK_DIM], t[..., QK_DIM:]
    return jnp.concatenate([nope, rope + _rot_cols_t(rot)], axis=-1).reshape(Q_RANK, HEADS * QK_DIM)


def _w_ukv_to_pad(w):
    t = w.reshape(KV_RANK, HEADS, 2, NOPE)
    return jnp.swapaxes(t, 1, 2).reshape(KV_RANK, 2 * HEADS * NOPE)


def _w_ukv_from_pad(g):
    t = g.reshape(KV_RANK, 2, HEADS, NOPE)
    return jnp.swapaxes(t, 1, 2).reshape(KV_RANK, 2 * HEADS * NOPE)


def _rope_tables(positions):
    inv_freq = 1.0 / (ROPE_THETA ** (jnp.arange(0, ROPE, 2, dtype=F32) / ROPE))
    ang = positions.astype(F32).reshape(-1, 1) * inv_freq
    cos, sin = jnp.cos(ang), jnp.sin(ang)
    zero = jnp.zeros((ang.shape[0], 64), F32)
    return jnp.concatenate([cos, cos, zero], axis=1), jnp.concatenate([sin, sin, zero], axis=1)


_BIG = ("w_in", "w_uq", "w_ukv", "w_out", "w_up", "w_down")
UP_SHARD = 2 * D_FF // N_CHIPS
TOKEN_TILE = 1024


def _local_step(x, positions, tgt, wts, in_weights, mixer_weights, ffn_weights, on_ffn_grads, on_mixer_grads):
    B, S, D = x.shape
    T = B * S
    xf = x.reshape(T, D)
    cos_a, sin_a = _rope_tables(positions)
    bs_t = jnp.pad(wts["a_spatial_b"].T, ((0, 0), (0, 128 - A_GROUPS)))

    h = _rms_fwd(xf, wts["mix_norm"], "norm1_fwd")
    wts = dict(wts)
    wts["w_in"], token = in_weights([h, cos_a, sin_a])
    tm = min(TOKEN_TILE, T)
    z = _mm(h, wts["w_in"], "nt", "in_proj", tm=tm, tn=1536, tk=D, n_outer=True, after=token)
    wts["w_q"], wts["w_kv"], wts["w_out"] = mixer_weights(z)
    q, k, v, cqn, ckvn = _lat_fwd(z, wts["q_a_norm"], wts["kv_a_norm"], wts["w_q"], wts["w_kv"], cos_a, sin_a)
    yb, *lses = _attn_fwd(q, k, v, B, S)
    merged = _mix_fwd(z, yb, wts["a_v_norm_g"], wts["a_v_norm_b"], wts["a_spatial_w"], bs_t)
    x1, h2 = _mm(merged, wts["w_out"], "nn", "out_proj", tm=min(512, T), tn=D, tk=D, add=xf, copy_dtype=MXU_DTYPE,
                 norm_gain=wts["ffn_norm"])
    wts["w_up"], wts["w_down"], wts["conv_w"] = ffn_weights(h2)
    up_pre = _mm(h2, wts["w_up"], "nn", "up_proj", tm=tm, tn=UP_SHARD, tk=D, dims=(T, 2 * D_FF, D),
                 b_spec=pl.BlockSpec((None, D, UP_SHARD), lambda i, j, k: (j, 0, 0)),
                 o_spec=pl.BlockSpec((None, tm, UP_SHARD), lambda i, j, k: (j // 2, i, j % 2)), out_shape=(2, T, D_FF),
                 n_outer=True)
    act, up_conv = _gate_fwd(up_pre, wts["conv_w"], wts["conv_b"], B, S)
    x2 = _mm(act, wts["w_down"], "nn", "down_proj", tm=tm, tn=D, tk=1408, add=x1)
    dx2, loss_row, g_final = _final(x2, tgt.reshape(T, D), wts["final_norm"])

    g = {"final_norm": g_final}
    dact = _mm(dx2, wts["w_down"], "nt", "down_proj_dx", tm=tm, tn=1408, tk=D, n_outer=True)
    tk2, tk1 = min(2048, T), min(1024, T)
    g["w_down"], g["w_down_lo"] = _mm(act, dx2, "tn", "down_proj_dw", tm=1408, tn=D, tk=tk1, copy_dtype=GRAD_PAYLOAD)
    dup, g["conv_w"], g["conv_b"] = _gate_bwd(up_pre, up_conv, dact, wts["conv_w"], B, S)
    g["w_up"], g["w_up_lo"] = _mm(
        h2, dup, "tn", "up_proj_dw", tm=D, tn=UP_SHARD, tk=tk2, dims=(D, 2 * D_FF, T), copy_dtype=GRAD_PAYLOAD,
        b_spec=pl.BlockSpec((None, tk2, UP_SHARD), lambda i, j, k: (j // 2, k, j % 2)),
        o_spec=pl.BlockSpec((None, D, UP_SHARD), lambda i, j, k: (j, 0, 0)), out_shape=(N_CHIPS, D, UP_SHARD))
    token, ffn_sent = on_ffn_grads(g)
    dh2 = _mm(dup, wts["w_up"], "nt", "up_proj_dx", tm=tm, tn=D, tk=UP_SHARD, dims=(T, D, 2 * D_FF), after=token,
              a_spec=pl.BlockSpec((None, tm, UP_SHARD), lambda i, j, k: (k // 2, i, k % 2)),
              b_spec=pl.BlockSpec((None, D, UP_SHARD), lambda i, j, k: (k, 0, 0)))
    token = ffn_sent(dh2)
    dx1, g["ffn_norm"] = _rms_bwd(x1, wts["ffn_norm"], dh2, dx2, "norm2_bwd")
    dm = _mm(dx1, wts["w_out"], "nt", "out_proj_dx", tm=min(512, T), tn=D, tk=D, after=token)
    g["w_out"], g["w_out_lo"] = _mm(merged, dx1, "tn", "out_proj_dw", tm=D, tn=D, tk=tk1, copy_dtype=GRAD_PAYLOAD)
    dz, dyb, dl, g["a_spatial_w"], gbs, g["a_v_norm_g"], g["a_v_norm_b"] = _mix_bwd(
        z, yb, dm, wts["a_v_norm_g"], wts["a_v_norm_b"], wts["a_spatial_w"], bs_t)
    g["a_spatial_b"] = gbs[:, :A_GROUPS].T
    delta = dl.reshape(HEADS * T // ATT_BLOCK, 1, ATT_BLOCK)
    dq, dk, dv = _attn_bwd(q, k, v, dyb, lses, delta, B, S)
    dz, dq_raw, dkv, g["q_a_norm"], g["kv_a_norm"] = _lat_bwd(
        dz, z, dq, dk, dv, wts["q_a_norm"], wts["kv_a_norm"], wts["w_q"], wts["w_kv"], cos_a, sin_a)
    g["w_q"] = _mm(cqn, dq_raw, "tn", "q_proj_dw", tm=Q_RANK, tn=HEADS * HEAD_PAD, tk=tk2)
    g["w_kv"] = _mm(ckvn, dkv, "tn", "kv_proj_dw", tm=KV_RANK, tn=2 * HEADS * NOPE, tk=tk2)
    g["w_in"] = _mm(dz, h, "tn", "in_proj_dw", tm=1536, tn=D, tk=tk2)
    token = on_mixer_grads(g)
    dh = _mm(dz, wts["w_in"], "nn", "in_proj_dx", tm=tm, tn=D, tk=1536, after=token)
    dx, g["mix_norm"] = _rms_bwd(xf, wts["mix_norm"], dh, dx1, "norm1_bwd")
    return loss_row[0, 0], dx.reshape(B, S, D), g


_SMALL = (("mix_norm", (1, D_MODEL)), ("a_v_norm_g", (1, D_MODEL)), ("a_v_norm_b", (1, D_MODEL)),
          ("a_spatial_w", (A_GROUPS * CHUNK, CHUNK)), ("a_spatial_b", (1, A_GROUPS * CHUNK)), ("q_a_norm", (1, Q_RANK)),
          ("kv_a_norm", (1, KV_RANK)), ("ffn_norm", (1, D_MODEL)), ("conv_b", (1, 2 * D_FF)), ("final_norm", (1, D_MODEL)),
          ("conv_w", (3, 2 * D_FF)))
_SMALL_SIZE = sum(math.prod(s) for _, s in _SMALL)
_SMALL_ROWS = -(-(_SMALL_SIZE + 1) // (128 * 8)) * 8


def kernel(x, positions, mix_norm, w_in, a_v_norm_g, a_v_norm_b, a_spatial_w, a_spatial_b, q_a_norm, w_uq, kv_a_norm, w_ukv, w_out, ffn_norm, w_up, conv_w, conv_b, w_down, final_norm, loss_target, m_mix_norm, m_w_in, m_a_v_norm_g, m_a_v_norm_b, m_a_spatial_w, m_a_spatial_b, m_q_a_norm, m_w_uq, m_kv_a_norm, m_w_ukv, m_w_out, m_ffn_norm, m_w_up, m_conv_w, m_conv_b, m_w_down, m_final_norm, v_mix_norm, v_w_in, v_a_v_norm_g, v_a_v_norm_b, v_a_spatial_w, v_a_spatial_b, v_q_a_norm, v_w_uq, v_kv_a_norm, v_w_ukv, v_w_out, v_ffn_norm, v_w_up, v_conv_w, v_conv_b, v_w_down, v_final_norm):
    weights = dict(mix_norm=mix_norm, w_in=w_in, a_v_norm_g=a_v_norm_g, a_v_norm_b=a_v_norm_b, a_spatial_w=a_spatial_w,
                   a_spatial_b=a_spatial_b, q_a_norm=q_a_norm, w_uq=w_uq, kv_a_norm=kv_a_norm, w_ukv=w_ukv, w_out=w_out,
                   ffn_norm=ffn_norm, w_up=w_up, conv_w=conv_w, conv_b=conv_b, w_down=w_down, final_norm=final_norm)
    m_in = dict(mix_norm=m_mix_norm, w_in=m_w_in, a_v_norm_g=m_a_v_norm_g, a_v_norm_b=m_a_v_norm_b,
                a_spatial_w=m_a_spatial_w, a_spatial_b=m_a_spatial_b, q_a_norm=m_q_a_norm, w_uq=m_w_uq,
                kv_a_norm=m_kv_a_norm, w_ukv=m_w_ukv, w_out=m_w_out, ffn_norm=m_ffn_norm, w_up=m_w_up, conv_w=m_conv_w,
                conv_b=m_conv_b, w_down=m_w_down, final_norm=m_final_norm)
    v_in = dict(mix_norm=v_mix_norm, w_in=v_w_in, a_v_norm_g=v_a_v_norm_g, a_v_norm_b=v_a_v_norm_b,
                a_spatial_w=v_a_spatial_w, a_spatial_b=v_a_spatial_b, q_a_norm=v_q_a_norm, w_uq=v_w_uq,
                kv_a_norm=v_kv_a_norm, w_ukv=v_w_ukv, w_out=v_w_out, ffn_norm=v_ffn_norm, w_up=v_w_up, conv_w=v_conv_w,
                conv_b=v_conv_b, w_down=v_w_down, final_norm=v_final_norm)
    names = list(weights)
    chip = 2 * lax.axis_index("x") + lax.axis_index("y")

    def halves(a):
        return a.reshape(a.shape[:-2] + (2, a.shape[-2] // 2, a.shape[-1]))

    w_in_t = jnp.swapaxes(w_in[0], 0, 1).astype(MXU_DTYPE)
    w_in_gather = _exchange_start([jnp.stack(jnp.split(w_in_t, 2, axis=1))], "w_in_gather_start", "halves",
                                  after=positions)
    gathers = {}
    wts = dict(
        mix_norm=mix_norm, a_v_norm_g=a_v_norm_g, a_v_norm_b=a_v_norm_b, a_spatial_w=a_spatial_w[0],
        a_spatial_b=a_spatial_b[0], q_a_norm=q_a_norm, kv_a_norm=kv_a_norm, ffn_norm=ffn_norm,
        final_norm=final_norm.reshape(1, D_MODEL), conv_b=conv_b.reshape(2, 1, D_FF))

    mixer_shards = [weights[n][0].astype(MXU_DTYPE) for n in _BIG[1:4]]
    ffn_shards = [w_up[0].astype(MXU_DTYPE), w_down[0].astype(MXU_DTYPE)]

    def in_weights(after):
        _, landed = _exchange_wait(w_in_gather, "w_in_gather_wait", "halves", list(after) + mixer_shards + ffn_shards)
        (w_in_sh,) = _forward_halves(list(landed))
        gathers["mixer"] = _exchange_start(mixer_shards, "mixer_gather_start", "gather", after=w_in_sh)
        gathers["ffn"] = _exchange_start(ffn_shards + [conv_w[0]], "ffn_gather_start", "gather",
                                         after=gathers["mixer"][4])
        w_in_pad = _w_in_t_to_pad(jnp.concatenate([w_in_sh[:, 0], w_in_sh[:, 1]], axis=-1).reshape(-1, D_MODEL))
        return w_in_pad, gathers["ffn"][4]

    def mixer_weights(after):
        _, (w_uq_sh, w_ukv_sh, w_out_sh) = _exchange_wait(gathers["mixer"], "mixer_gather_wait", "gather", after)
        return (_w_uq_to_pad(_cols_from_chips(w_uq_sh)), _w_ukv_to_pad(_cols_from_chips(w_ukv_sh)),
                w_out_sh.reshape(D_MODEL, D_MODEL))

    def ffn_weights(after):
        _, (w_up_sh, w_down_sh, cw_all) = _exchange_wait(gathers["ffn"], "ffn_gather_wait", "gather", after)
        return w_up_sh, w_down_sh.reshape(D_FF, D_MODEL), _conv_w_split(_cols_from_chips(cw_all))

    scatters = {}

    def start_scatter(slabs, slabs_lo, tag):
        got = _swap_halves([halves(s) for s in slabs_lo], tag + "_grad_swap_halves")
        sums = _pair_sum(slabs, got, tag + "_grad_pair_sum")
        scatters[tag] = _exchange_start(list(sums), tag + "_scatter_start", "scatter", after=slabs[-1])
        return scatters[tag][4]

    def on_ffn_grads(g):
        slabs, slabs_lo = [[g["w_up" + lo], g["w_down" + lo].reshape(N_CHIPS, D_FF // N_CHIPS, D_MODEL)]
                           for lo in ("", "_lo")]
        swap = _exchange_start([halves(s) for s in slabs_lo], "ffn_swap_start", "swap", after=slabs[1])

        def sent(after):
            _, got = _exchange_wait(swap, "ffn_swap_wait", "swap", after)
            sums = _pair_sum(slabs, got, "ffn_grad_pair_sum")
            scatters["ffn"] = _exchange_start(list(sums), "ffn_scatter_start", "scatter", after=got[0])
            return scatters["ffn"][4]

        return swap[4], sent

    def on_mixer_grads(g):
        slabs = [_w_in_t_from_pad(g["w_in"]).reshape(N_CHIPS, -1, D_MODEL), _cols_to_chips(_w_uq_from_pad(g["w_q"])),
                 _cols_to_chips(_w_ukv_from_pad(g["w_kv"]))]
        w_out_slabs = [g["w_out" + lo].reshape(N_CHIPS, D_MODEL // N_CHIPS, D_MODEL) for lo in ("", "_lo")]
        return start_scatter(slabs + w_out_slabs[:1], [s.astype(GRAD_PAYLOAD) for s in slabs] + w_out_slabs[1:], "mixer")

    loss_part, grad_x, g = _local_step(x, positions, loss_target, wts, in_weights, mixer_weights, ffn_weights,
                                       on_ffn_grads, on_mixer_grads)

    g_small_parts = dict(g)
    g_small_parts["conv_w"] = _conv_w_join(g["conv_w"])
    g_small_parts["conv_b"] = g["conv_b"].reshape(1, 2 * D_FF)
    flat = jnp.concatenate([g_small_parts[n].reshape(-1) for n, _ in _SMALL] + [loss_part.reshape(1)])
    flat = jnp.pad(flat, (0, _SMALL_ROWS * 128 - flat.shape[0])).reshape(_SMALL_ROWS, 128)
    small_gather = _exchange_start([flat], "small_gather_start", "all", after=grad_x)

    mixer_sums, mixer_landed = _exchange_wait(scatters["mixer"], "mixer_scatter_wait", "scatter", after=small_gather[4])
    ffn_sums, ffn_landed = _exchange_wait(scatters["ffn"], "ffn_scatter_wait", "scatter", after=mixer_landed[0])
    reduced = _chip_sum(list(mixer_sums) + list(ffn_sums), list(mixer_landed) + list(ffn_landed))
    g_big = dict(zip(_BIG, _join_halves(reduced)))

    grads, deltas, new_m, new_v = {}, {}, {}, {}

    def update(n, grad, copy_grad=False):
        w = weights[n]
        shape2 = grad.shape
        d, nm, nv, *again = _adamw(w.reshape(shape2), grad, m_in[n].reshape(shape2), v_in[n].reshape(shape2),
                                   "adamw_" + n, copy_grad)
        grads[n], deltas[n], new_m[n], new_v[n] = (t.reshape(w.shape) for t in (again[0] if copy_grad else grad, d, nm, nv))

    def update_transposed(n, grad_t):
        t = lambda a: jnp.swapaxes(a, 1, 2)
        d, nm, nv, again = _adamw(t(weights[n]), grad_t, t(m_in[n]), t(v_in[n]), "adamw_" + n, True)
        grads[n], deltas[n], new_m[n], new_v[n] = t(again), t(d), t(nm), t(nv)

    for n in _BIG:
        g3 = g_big[n].reshape((1, -1, g_big[n].shape[-1]))
        if n == "w_in":
            update_transposed(n, g3)
        else:
            update(n, g3, copy_grad=True)

    (own,), (everyone,) = _exchange_wait(small_gather, "small_gather_wait", "all", after=[deltas[n] for n in _BIG])
    device = 2 * chip + lax.axis_index("c")
    everyone = lax.dynamic_update_slice(everyone, own[None], (device, 0, 0))
    total = _sum_slabs([everyone[j] for j in range(8)], "small_grads_sum", tr=_SMALL_ROWS).reshape(-1)
    o = 0
    for n, shp in _SMALL:
        piece = total[o:o + math.prod(shp)].reshape(shp)
        o += math.prod(shp)
        if n == "conv_w":
            piece = lax.dynamic_slice_in_dim(piece, chip * UP_SHARD, UP_SHARD, axis=1)
        update(n, piece)
    loss = total[_SMALL_SIZE]
    return (loss, grad_x, *[grads[n] for n in names], *[deltas[n] for n in names], *[new_m[n] for n in names],
            *[new_v[n] for n in names])
```

```python
import functools
import math

import jax
import jax.numpy as jnp
from jax import lax
from jax.experimental import pallas as pl
from jax.experimental.pallas import tpu as pltpu

F32 = jnp.float32
MXU_DTYPE = jnp.bfloat16
MESH = pl.DeviceIdType.MESH

D_MODEL = 1024
EPS = 1e-6
A_GROUPS = 8
CHUNK = 128
HEADS = 8
NOPE = 128
ROPE = 64
QK_DIM = NOPE + ROPE
HEAD_PAD = 256
Q_RANK = 256
KV_RANK = 128
ROPE_THETA = 10000.0
D_FF = 2816
FF_TILE = 256
N_FF_TILES = D_FF // FF_TILE
LAT = 512
IN_PAD = 4 * D_MODEL + LAT
N_CHIPS = 4
ADAM_LR, ADAM_B1, ADAM_B2, ADAM_EPS, ADAM_WD, ADAM_STEP = 0.001, 0.9, 0.999, 1e-08, 0.01, 10

VMEM_CAP_V7X = 64 * 1024 * 1024
NEG = -1e30


def _params(sem, nbytes):
    limit = int(min(VMEM_CAP_V7X - (8 << 20), max(32 << 20, 3 * nbytes)))
    return pltpu.CompilerParams(dimension_semantics=sem, vmem_limit_bytes=limit)


def _nbytes(shape, dtype):
    return math.prod(shape) * jnp.dtype(dtype).itemsize


_DIMS = {"nn": (((1,), (0,)), ((), ())), "nt": (((1,), (1,)), ((), ())), "tn": (((0,), (0,)), ((), ()))}


def _mm(a, b, mode, name, *, tm, tn, tk, out_dtype=F32, add=None, dims=None, a_spec=None, b_spec=None,
        o_spec=None, out_shape=None, n_outer=False, copy_dtype=None, after=None, norm_gain=None):
    if dims is None:
        if mode == "nn":
            (M, K), (_, N) = a.shape, b.shape
        elif mode == "nt":
            (M, K), (N, _) = a.shape, b.shape
        else:
            (K, M), (_, N) = a.shape, b.shape
    else:
        M, N, K = dims
    a_blk = (tk, tm) if mode == "tn" else (tm, tk)
    b_blk = (tn, tk) if mode == "nt" else (tk, tn)
    if a_spec is None:
        a_spec = pl.BlockSpec(a_blk, (lambda i, j, k: (k, i)) if mode == "tn" else (lambda i, j, k: (i, k)))
    if b_spec is None:
        b_spec = pl.BlockSpec(b_blk, (lambda i, j, k: (j, k)) if mode == "nt" else (lambda i, j, k: (k, j)))
    if o_spec is None:
        o_spec = pl.BlockSpec((tm, tn), lambda i, j, k: (i, j))
    if out_shape is None:
        out_shape = (M, N)
    assert M % tm == 0 and N % tn == 0 and K % tk == 0, (name, M, N, K, tm, tn, tk)
    nk = K // tk
    contract = _DIMS[mode]
    has_add = add is not None
    has_gain = norm_gain is not None
    assert not has_gain or (tn == N and copy_dtype is not None), name

    def body(*refs):
        a_ref, b_ref = refs[0], refs[1]
        add_ref = refs[2] if has_add else None
        n_in = 2 + has_add + (after is not None) + has_gain
        gain_ref = refs[n_in - 1] if has_gain else None
        o_ref = refs[n_in]
        copy_ref = refs[n_in + 1] if copy_dtype is not None else None

        def product():
            return lax.dot_general(a_ref[...].astype(MXU_DTYPE), b_ref[...].astype(MXU_DTYPE), contract,
                                   preferred_element_type=F32)

        def finish(r):
            if has_add:
                r = r + add_ref[...]
            o_ref[...] = r.astype(out_dtype)
            if has_gain:
                rs = lax.rsqrt(jnp.mean(r * r, axis=-1, keepdims=True) + EPS)
                copy_ref[...] = ((r * rs) * gain_ref[...]).astype(copy_dtype)
            elif copy_ref is not None:
                copy_ref[...] = r.astype(copy_dtype)

        if nk == 1:
            finish(product())
            return
        acc = refs[-1]
        k = pl.program_id(2)

        @pl.when(k == 0)
        def _():
            acc[...] = jnp.zeros_like(acc)

        acc[...] += product()

        @pl.when(k == nk - 1)
        def _():
            finish(acc[...])

    in_specs = [a_spec, b_spec]
    args = [a, b]
    nbytes = _nbytes(a_blk, a.dtype) + _nbytes(b_blk, b.dtype) + 3 * _nbytes((tm, tn), F32)
    if has_add:
        in_specs.append(pl.BlockSpec((tm, tn), lambda i, j, k: (i, j)))
        args.append(add)
        nbytes += _nbytes((tm, tn), F32)
    if after is not None:
        in_specs.append(pl.BlockSpec(after.shape, lambda i, j, k: (0, 0)))
        args.append(after)
    if has_gain:
        in_specs.append(pl.BlockSpec((1, tn), lambda i, j, k: (0, 0)))
        args.append(norm_gain)
        nbytes += _nbytes((tm, tn), F32)
    grid = (M // tm, N // tn, nk)
    if n_outer:
        def swapped(spec):
            return pl.BlockSpec(spec.block_shape, lambda j, i, k, at=spec.index_map: at(i, j, k))

        grid = (N // tn, M // tm, nk)
        in_specs = [swapped(s) for s in in_specs]
        o_spec = swapped(o_spec)
    out_sds, out_specs = jax.ShapeDtypeStruct(out_shape, out_dtype), o_spec
    if copy_dtype is not None:
        out_sds, out_specs = (out_sds, jax.ShapeDtypeStruct(out_shape, copy_dtype)), (o_spec, o_spec)
    return pl.pallas_call(
        body, name=name, out_shape=out_sds, grid=grid, in_specs=in_specs, out_specs=out_specs,
        scratch_shapes=[pltpu.VMEM((tm, tn), F32)] if nk > 1 else [],
        compiler_params=_params(("parallel", "parallel", "arbitrary"), nbytes),
    )(*args)


_GELU_C = math.sqrt(2.0 / math.pi)
_GELU_A = 0.044715


def _sigmoid(x):
    return 0.5 * jnp.tanh(0.5 * x) + 0.5


def _gelu(x):
    t = jnp.tanh(x * (_GELU_C + (_GELU_C * _GELU_A) * (x * x)))
    return x * (0.5 + 0.5 * t)


def _gelu_and_grad(x):
    x2 = x * x
    t = jnp.tanh(x * (_GELU_C + (_GELU_C * _GELU_A) * x2))
    cdf = 0.5 + 0.5 * t
    grad = cdf + (0.5 * x) * (1.0 - t * t) * (_GELU_C + (3.0 * _GELU_C * _GELU_A) * x2)
    return x * cdf, grad


def _rope_mix(g, cos_a, sin_a):
    return g * cos_a + pltpu.roll(g, 64, 1) * sin_a


def _rope_mix_bwd(d, cos_a, sin_a):
    return d * cos_a + pltpu.roll(d * sin_a, 64, 1)


def _rms_fwd(x, g, name, tr=1024):
    T, D = x.shape

    def body(x_ref, g_ref, h_ref):
        xv = x_ref[...]
        r = lax.rsqrt(jnp.mean(xv * xv, axis=-1, keepdims=True) + EPS)
        h_ref[...] = ((xv * r) * g_ref[...]).astype(h_ref.dtype)

    return pl.pallas_call(
        body, name=name, out_shape=jax.ShapeDtypeStruct((T, D), MXU_DTYPE), grid=(T // tr,),
        in_specs=[pl.BlockSpec((tr, D), lambda i: (i, 0)), pl.BlockSpec((1, D), lambda i: (0, 0))],
        out_specs=pl.BlockSpec((tr, D), lambda i: (i, 0)),
        compiler_params=_params(("parallel",), 3 * _nbytes((tr, D), F32)),
    )(x, g)


def _rms_bwd(x, g, dh, dres, name, tr=512):
    T, D = x.shape

    def body(x_ref, g_ref, dh_ref, dres_ref, dx_ref, gg_ref):
        @pl.when(pl.program_id(0) == 0)
        def _():
            gg_ref[...] = jnp.zeros_like(gg_ref)

        xv = x_ref[...]
        r = lax.rsqrt(jnp.mean(xv * xv, axis=-1, keepdims=True) + EPS)
        xn = xv * r
        dhv = dh_ref[...]
        dxn = dhv * g_ref[...]
        dx_ref[...] = dres_ref[...] + r * (dxn - xn * jnp.mean(dxn * xn, axis=-1, keepdims=True))
        gg_ref[...] += jnp.sum(dhv * xn, axis=0, keepdims=True)

    row = pl.BlockSpec((tr, D), lambda i: (i, 0))
    vec = pl.BlockSpec((1, D), lambda i: (0, 0))
    return pl.pallas_call(
        body, name=name,
        out_shape=(jax.ShapeDtypeStruct((T, D), F32), jax.ShapeDtypeStruct((1, D), F32)),
        grid=(T // tr,), in_specs=[row, vec, row, row], out_specs=(row, vec),
        compiler_params=_params(("arbitrary",), 6 * _nbytes((tr, D), F32)),
    )(x, g, dh, dres)


def _lat_fwd(z, gq, gkv, wq, wkv, cos_a, sin_a, tr=512):
    T = z.shape[0]
    lat_blk = (4 * D_MODEL) // LAT

    def body(z_ref, gq_ref, gkv_ref, wq_ref, wkv_ref, cos_ref, sin_ref, q_ref, k_ref, v_ref, cqn_ref, ckvn_ref):
        zl = z_ref[...]
        cos_v, sin_v = cos_ref[...], sin_ref[...]
        cq = zl[:, :Q_RANK]
        ckv = zl[:, Q_RANK:Q_RANK + KV_RANK]
        krb = zl[:, Q_RANK + KV_RANK:]
        cqn = ((cq * lax.rsqrt(jnp.mean(cq * cq, axis=-1, keepdims=True) + EPS)) * gq_ref[...]).astype(MXU_DTYPE)
        ckvn = ((ckv * lax.rsqrt(jnp.mean(ckv * ckv, axis=-1, keepdims=True) + EPS)) * gkv_ref[...]).astype(MXU_DTYPE)
        cqn_ref[...] = cqn
        ckvn_ref[...] = ckvn
        krr = _rope_mix(krb, cos_v, sin_v).astype(MXU_DTYPE)
        q = jnp.dot(cqn, wq_ref[...], preferred_element_type=F32)
        kv = jnp.dot(ckvn, wkv_ref[...], preferred_element_type=F32)
        for h in range(HEADS):
            o = h * HEAD_PAD
            q_ref[:, o:o + NOPE] = q[:, o:o + NOPE].astype(MXU_DTYPE)
            q_ref[:, o + NOPE:o + HEAD_PAD] = _rope_mix(q[:, o + NOPE:o + HEAD_PAD], cos_v, sin_v).astype(MXU_DTYPE)
            k_ref[:, o:o + NOPE] = kv[:, h * NOPE:(h + 1) * NOPE].astype(MXU_DTYPE)
            k_ref[:, o + NOPE:o + HEAD_PAD] = krr
        v_ref[...] = kv[:, HEADS * NOPE:].astype(MXU_DTYPE)

    def row(w):
        return pl.BlockSpec((tr, w), lambda i: (i, 0))

    def full(a):
        return pl.BlockSpec(a.shape, lambda i: (0, 0))

    return pl.pallas_call(
        body, name="lat_fwd",
        out_shape=(jax.ShapeDtypeStruct((T, HEADS * HEAD_PAD), MXU_DTYPE), jax.ShapeDtypeStruct((T, HEADS * HEAD_PAD), MXU_DTYPE),
                   jax.ShapeDtypeStruct((T, HEADS * NOPE), MXU_DTYPE), jax.ShapeDtypeStruct((T, Q_RANK), MXU_DTYPE),
                   jax.ShapeDtypeStruct((T, KV_RANK), MXU_DTYPE)),
        grid=(T // tr,),
        in_specs=[pl.BlockSpec((tr, LAT), lambda i: (i, lat_blk)), full(gq), full(gkv), full(wq), full(wkv), row(128), row(128)],
        out_specs=(row(HEADS * HEAD_PAD), row(HEADS * HEAD_PAD), row(HEADS * NOPE), row(Q_RANK), row(KV_RANK)),
        compiler_params=_params(("parallel",), 8 * _nbytes((tr, HEADS * HEAD_PAD), F32)),
    )(z, gq, gkv, wq, wkv, cos_a, sin_a)


ATT_BLOCK = 256
_SCALE = QK_DIM ** -0.5


def _causal_mask(n):
    return lax.broadcasted_iota(jnp.int32, (n, n), 1) <= lax.broadcasted_iota(jnp.int32, (n, n), 0)


def _causal_mask_t(n):
    return lax.broadcasted_iota(jnp.int32, (n, n), 0) <= lax.broadcasted_iota(jnp.int32, (n, n), 1)


ATT_HEADS = 4
ATT_HEADS_FWD = HEADS


def _attn_fwd(q, k, v, B, S):
    tq = ATT_BLOCK
    nq = S // tq
    T = B * S
    hp, groups = ATT_HEADS_FWD, HEADS // ATT_HEADS_FWD

    def body(q_ref, k_ref, v_ref, o_ref, *lse_refs):
        qi = pl.program_id(2)
        qs = [q_ref[:, t * HEAD_PAD:(t + 1) * HEAD_PAD] for t in range(hp)]

        def scores(j, t):
            rows = pl.ds(pl.multiple_of(j * tq, tq), tq)
            return lax.dot_general(k_ref[rows, t * HEAD_PAD:(t + 1) * HEAD_PAD], qs[t], _DIMS["nt"],
                                   preferred_element_type=F32)

        def step(j, carry, last):
            rows = pl.ds(pl.multiple_of(j * tq, tq), tq)
            out = []
            for t in range(hp):
                m, l, acc, st = carry[t]
                st_next = st if last else scores(j + 1, t)
                st = st * _SCALE
                if last:
                    st = jnp.where(_causal_mask_t(tq), st, NEG)
                m_new = jnp.maximum(m, jnp.max(st, axis=0, keepdims=True))
                alpha = jnp.exp(m - m_new)
                p = jnp.exp(st - m_new)
                l = alpha * l + jnp.sum(p, axis=0, keepdims=True)
                acc = alpha * acc + lax.dot_general(v_ref[rows, t * NOPE:(t + 1) * NOPE], p.astype(MXU_DTYPE),
                                                    _DIMS["tn"], preferred_element_type=F32)
                out.append((m_new, l, acc, st_next))
            return tuple(out)

        init = tuple((jnp.full((1, tq), NEG, F32), jnp.zeros((1, tq), F32), jnp.zeros((NOPE, tq), F32), scores(0, t))
                     for t in range(hp))
        carry = lax.fori_loop(0, qi, lambda j, c: step(j, c, False), init)
        carry = step(qi, carry, True)
        for t in range(hp):
            m, l, acc, _ = carry[t]
            o_ref[:, t * NOPE:(t + 1) * NOPE] = (acc / l).T
            lse_refs[t][0] = m + jnp.log(l)

    lse_sds = jax.ShapeDtypeStruct((groups * B * nq, 1, tq), F32)
    lse_spec = pl.BlockSpec((1, 1, tq), lambda b, h, i: ((h * B + b) * nq + i, 0, 0))
    return pl.pallas_call(
        body, name="attn_fwd",
        out_shape=(jax.ShapeDtypeStruct((T, HEADS * NOPE), F32),) + (lse_sds,) * hp,
        grid=(B, groups, nq),
        in_specs=[pl.BlockSpec((tq, hp * HEAD_PAD), lambda b, h, i: (b * nq + i, h)),
                  pl.BlockSpec((S, hp * HEAD_PAD), lambda b, h, i: (b, h)),
                  pl.BlockSpec((S, hp * NOPE), lambda b, h, i: (b, h))],
        out_specs=(pl.BlockSpec((tq, hp * NOPE), lambda b, h, i: (b * nq + i, h)),) + (lse_spec,) * hp,
        compiler_params=_params(("parallel", "parallel", "arbitrary"), 4 * hp * _nbytes((S, HEAD_PAD), MXU_DTYPE)),
    )(q, k, v)


def _attn_bwd(q, k, v, do, lses, delta, B, S):
    tq = ATT_BLOCK
    nq = S // tq
    T = B * S
    hp, groups = ATT_HEADS, HEADS // ATT_HEADS
    assert groups == 2 and len(lses) == HEADS

    def body(q_ref, k_ref, v_ref, do_ref, *refs):
        lse_refs, dl_refs = refs[:HEADS], refs[HEADS:HEADS + hp]
        dq_out, dk_ref, dv_ref, dq_ref = refs[HEADS + hp:]
        kj = pl.program_id(2)
        upper = pl.program_id(1) == 1

        def lse(t, i):
            return jnp.where(upper, lse_refs[hp + t][i], lse_refs[t][i])

        @pl.when(kj == 0)
        def _():
            dq_ref[...] = jnp.zeros_like(dq_ref)

        def products(i, t):
            rows = pl.ds(pl.multiple_of(i * tq, tq), tq)
            st = lax.dot_general(k_ref[:, t * HEAD_PAD:(t + 1) * HEAD_PAD], q_ref[rows, t * HEAD_PAD:(t + 1) * HEAD_PAD],
                                 _DIMS["nt"], preferred_element_type=F32)
            dpt = lax.dot_general(v_ref[:, t * NOPE:(t + 1) * NOPE], do_ref[rows, t * NOPE:(t + 1) * NOPE],
                                  _DIMS["nt"], preferred_element_type=F32)
            return st, dpt

        def step(i, carry, masked, prefetch):
            rows = pl.ds(i * tq if isinstance(i, int) else pl.multiple_of(i * tq, tq), tq)
            out = []
            for t in range(hp):
                dk, dv, st, dpt = carry[t]
                st_next, dpt_next = products(i + 1, t) if prefetch else (st, dpt)
                qk_cols = slice(t * HEAD_PAD, (t + 1) * HEAD_PAD)
                v_cols = slice(t * NOPE, (t + 1) * NOPE)
                p = jnp.exp(st * _SCALE - lse(t, i))
                if masked:
                    p = jnp.where(_causal_mask_t(tq), p, 0.0)
                dv = dv + jnp.dot(p.astype(MXU_DTYPE), do_ref[rows, v_cols], preferred_element_type=F32)
                ds = (p * (dpt - dl_refs[t][i]) * _SCALE).astype(MXU_DTYPE)
                dk = dk + jnp.dot(ds, q_ref[rows, qk_cols], preferred_element_type=F32)
                dq_ref[rows, qk_cols] += lax.dot_general(ds, k_ref[:, qk_cols], _DIMS["tn"], preferred_element_type=F32)
                out.append((dk, dv, st_next, dpt_next))
            return tuple(out)

        def first_products():
            return tuple((jnp.zeros((tq, HEAD_PAD), F32), jnp.zeros((tq, NOPE), F32)) + products(kj, t) for t in range(hp))

        def finish(carry):
            for t in range(hp):
                dk_ref[:, t * HEAD_PAD:(t + 1) * HEAD_PAD] = carry[t][0].astype(dk_ref.dtype)
                dv_ref[:, t * NOPE:(t + 1) * NOPE] = carry[t][1].astype(dv_ref.dtype)

        @pl.when(kj < nq - 1)
        def _():
            carry = step(kj, first_products(), True, True)
            carry = lax.fori_loop(kj + 1, nq - 1, lambda i, c: step(i, c, False, True), carry)
            finish(step(nq - 1, carry, False, False))

        @pl.when(kj == nq - 1)
        def _():
            finish(step(kj, first_products(), True, False))
            dq_out[...] = dq_ref[...].astype(dq_out.dtype)

    seq = lambda w: pl.BlockSpec((S, w), lambda b, h, j: (b, h))
    blk = lambda w: pl.BlockSpec((tq, w), lambda b, h, j: (b * nq + j, h))
    lse_spec = pl.BlockSpec((nq, 1, tq), lambda b, h, j: (b, 0, 0))
    dl_specs = [pl.BlockSpec((nq, 1, tq), lambda b, h, j, t=t: ((h * hp + t) * B + b, 0, 0)) for t in range(hp)]
    return pl.pallas_call(
        body, name="attn_bwd",
        out_shape=(jax.ShapeDtypeStruct((T, HEADS * HEAD_PAD), MXU_DTYPE), jax.ShapeDtypeStruct((T, HEADS * HEAD_PAD), MXU_DTYPE),
                   jax.ShapeDtypeStruct((T, HEADS * NOPE), MXU_DTYPE)),
        grid=(B, groups, nq),
        in_specs=[seq(hp * HEAD_PAD), blk(hp * HEAD_PAD), blk(hp * NOPE), seq(hp * NOPE)] + [lse_spec] * HEADS + dl_specs,
        out_specs=(seq(hp * HEAD_PAD), blk(hp * HEAD_PAD), blk(hp * NOPE)),
        scratch_shapes=[pltpu.VMEM((S, hp * HEAD_PAD), F32)],
        compiler_params=_params(("parallel", "parallel", "arbitrary"), 8 * hp * _nbytes((S, HEAD_PAD), F32)),
    )(q, k, v, do, *lses, *([delta] * hp))


MIX_ROWS = 256


def _tril_weights(ws_ref, g):
    return jnp.where(_causal_mask(CHUNK), ws_ref[g], 0.0).astype(MXU_DTYPE)


def _layer_norm_stats(va):
    mu = jnp.mean(va, axis=-1, keepdims=True)
    xc = va - mu
    rs = lax.rsqrt(jnp.mean(xc * xc, axis=-1, keepdims=True) + EPS)
    return xc * rs


def _mix_specs(tr):
    zcol = lambda c: pl.BlockSpec((tr, D_MODEL), lambda i, c=c: (i, c))
    row = pl.BlockSpec((tr, D_MODEL), lambda i: (i, 0))
    vec = pl.BlockSpec((1, D_MODEL), lambda i: (0, 0))
    ws = pl.BlockSpec((A_GROUPS, CHUNK, CHUNK), lambda i: (0, 0, 0))
    bs = pl.BlockSpec((CHUNK, 128), lambda i: (0, 0))
    return zcol, row, vec, ws, bs


def _mix_fwd(z, yb, ln_g, ln_b, ws, bs_t):
    T = z.shape[0]
    tr = MIX_ROWS
    zcol, row, vec, ws_spec, bs_spec = _mix_specs(tr)

    def body(zu_ref, zv_ref, zga_ref, zgb_ref, yb_ref, g_ref, b_ref, ws_ref, bs_ref, out_ref, vn_s):
        vhat = _layer_norm_stats(_gelu(zv_ref[...]))
        vn_s[...] = (vhat * g_ref[...] + b_ref[...]).astype(MXU_DTYPE)
        for g in range(A_GROUPS):
            w = _tril_weights(ws_ref, g)
            bias = bs_ref[:, g:g + 1]
            cols = slice(g * CHUNK, (g + 1) * CHUNK)
            for c in range(tr // CHUNK):
                rows = slice(c * CHUNK, (c + 1) * CHUNK)
                mixed = jnp.dot(w, vn_s[rows, cols], preferred_element_type=F32) + bias
                ya = _gelu(zu_ref[rows, cols]) * mixed
                merged = _sigmoid(zga_ref[rows, cols]) * ya + _sigmoid(zgb_ref[rows, cols]) * yb_ref[rows, cols]
                out_ref[rows, cols] = merged.astype(MXU_DTYPE)

    return pl.pallas_call(
        body, name="mix_fwd", out_shape=jax.ShapeDtypeStruct((T, D_MODEL), MXU_DTYPE), grid=(T // tr,),
        in_specs=[zcol(0), zcol(1), zcol(2), zcol(3), row, vec, vec, ws_spec, bs_spec], out_specs=row,
        scratch_shapes=[pltpu.VMEM((tr, D_MODEL), MXU_DTYPE)],
        compiler_params=_params(("parallel",), 8 * _nbytes((tr, D_MODEL), F32)),
    )(z, z, z, z, yb, ln_g, ln_b, ws, bs_t)


def _mix_bwd(z, yb, dm, ln_g, ln_b, ws, bs_t):
    T = z.shape[0]
    tr = MIX_ROWS
    zcol, row, vec, ws_spec, bs_spec = _mix_specs(tr)

    def body(zu_ref, zv_ref, zga_ref, zgb_ref, yb_ref, dm_ref, g_ref, b_ref, ws_ref, bs_ref,
             dz_ref, dyb_ref, dl_ref, gws_ref, gbs_ref, glg_ref, glb_ref, vn_s, dvn_s):
        @pl.when(pl.program_id(0) == 0)
        def _():
            gws_ref[...] = jnp.zeros_like(gws_ref)
            gbs_ref[...] = jnp.zeros_like(gbs_ref)
            glg_ref[...] = jnp.zeros_like(glg_ref)
            glb_ref[...] = jnp.zeros_like(glb_ref)

        lane = lax.broadcasted_iota(jnp.int32, (CHUNK, 128), 1)
        va, dgelu_v = _gelu_and_grad(zv_ref[...])
        mu = jnp.mean(va, axis=-1, keepdims=True)
        xc = va - mu
        rs = lax.rsqrt(jnp.mean(xc * xc, axis=-1, keepdims=True) + EPS)
        vhat = xc * rs
        vn_s[...] = (vhat * g_ref[...] + b_ref[...]).astype(MXU_DTYPE)
        gbs_acc = jnp.zeros((CHUNK, 128), F32)
        for g in range(A_GROUPS):
            w = _tril_weights(ws_ref, g)
            bias = bs_ref[:, g:g + 1]
            cols = slice(g * CHUNK, (g + 1) * CHUNK)
            gw_acc = jnp.zeros((CHUNK, CHUNK), F32)
            for c in range(tr // CHUNK):
                rows = slice(c * CHUNK, (c + 1) * CHUNK)
                vn = vn_s[rows, cols]
                mixed = jnp.dot(w, vn, preferred_element_type=F32) + bias
                ua, dgelu_u = _gelu_and_grad(zu_ref[rows, cols])
                dmv = dm_ref[rows, cols]
                sa = _sigmoid(zga_ref[rows, cols])
                dya = dmv * sa
                dz_ref[rows, 2 * D_MODEL + g * CHUNK:2 * D_MODEL + (g + 1) * CHUNK] = (
                    dmv * (ua * mixed) * (sa * (1.0 - sa))).astype(dz_ref.dtype)
                dz_ref[rows, cols] = (dya * mixed * dgelu_u).astype(dz_ref.dtype)
                dmix = dya * ua
                gbs_acc = gbs_acc + jnp.where(lane == g, jnp.sum(dmix, axis=-1, keepdims=True), 0.0)
                dmix_b = dmix.astype(MXU_DTYPE)
                gw_acc = gw_acc + lax.dot_general(dmix_b, vn, _DIMS["nt"], preferred_element_type=F32)
                dvn_s[rows, cols] = lax.dot_general(w, dmix_b, _DIMS["tn"], preferred_element_type=F32)
            gws_ref[g] += jnp.where(_causal_mask(CHUNK), gw_acc, 0.0)
        gbs_ref[...] += gbs_acc

        dvn = dvn_s[...]
        glg_ref[...] += jnp.sum(dvn * vhat, axis=0, keepdims=True)
        glb_ref[...] += jnp.sum(dvn, axis=0, keepdims=True)
        dvh = dvn * g_ref[...]
        dva = rs * (dvh - jnp.mean(dvh, axis=-1, keepdims=True) - vhat * jnp.mean(dvh * vhat, axis=-1, keepdims=True))
        dz_ref[:, D_MODEL:2 * D_MODEL] = (dva * dgelu_v).astype(dz_ref.dtype)

        dmv = dm_ref[...]
        ybv = yb_ref[...]
        sb = _sigmoid(zgb_ref[...])
        dyb = dmv * sb
        dyb_ref[...] = dyb.astype(dyb_ref.dtype)
        dz_ref[:, 3 * D_MODEL:4 * D_MODEL] = (dmv * ybv * (sb * (1.0 - sb))).astype(dz_ref.dtype)
        dz_ref[:, 4 * D_MODEL:] = jnp.zeros((tr, LAT), dz_ref.dtype)
        prod = dyb * ybv
        sel = (lax.broadcasted_iota(jnp.int32, (HEADS, D_MODEL), 1) // NOPE
               == lax.broadcasted_iota(jnp.int32, (HEADS, D_MODEL), 0)).astype(jnp.bfloat16)
        hi = prod.astype(jnp.bfloat16)
        rest = prod - hi.astype(F32)
        mid = rest.astype(jnp.bfloat16)
        lo = (rest - mid.astype(F32)).astype(jnp.bfloat16)
        dl_ref[...] = (lax.dot_general(sel, hi, _DIMS["nt"], preferred_element_type=F32)
                       + lax.dot_general(sel, mid, _DIMS["nt"], preferred_element_type=F32)
                       + lax.dot_general(sel, lo, _DIMS["nt"], preferred_element_type=F32))

    return pl.pallas_call(
        body, name="mix_bwd",
        out_shape=(jax.ShapeDtypeStruct((T, IN_PAD), MXU_DTYPE), jax.ShapeDtypeStruct((T, D_MODEL), MXU_DTYPE),
                   jax.ShapeDtypeStruct((HEADS, T), F32), jax.ShapeDtypeStruct((A_GROUPS, CHUNK, CHUNK), F32),
                   jax.ShapeDtypeStruct((CHUNK, 128), F32), jax.ShapeDtypeStruct((1, D_MODEL), F32),
                   jax.ShapeDtypeStruct((1, D_MODEL), F32)),
        grid=(T // tr,),
        in_specs=[zcol(0), zcol(1), zcol(2), zcol(3), row, row, vec, vec, ws_spec, bs_spec],
        out_specs=(pl.BlockSpec((tr, IN_PAD), lambda i: (i, 0)), row, pl.BlockSpec((HEADS, tr), lambda i: (0, i)),
                   ws_spec, bs_spec, vec, vec),
        scratch_shapes=[pltpu.VMEM((tr, D_MODEL), MXU_DTYPE), pltpu.VMEM((tr, D_MODEL), F32)],
        compiler_params=_params(("arbitrary",), 12 * _nbytes((tr, D_MODEL), F32)),
    )(z, z, z, z, yb, dm, ln_g, ln_b, ws, bs_t)


def _lat_bwd(dz, z, dq, dk, dv, gq, gkv, wq, wkv, cos_a, sin_a, tr=256):
    T = z.shape[0]
    lat_blk = (4 * D_MODEL) // LAT

    def body(dz_in, z_ref, dq_ref, dk_ref, dv_ref, gq_ref, gkv_ref, wq_ref, wkv_ref, cos_ref, sin_ref,
             dz_ref, dqr_ref, dkv_ref, ggq_ref, ggkv_ref):
        del dz_in

        @pl.when(pl.program_id(0) == 0)
        def _():
            ggq_ref[...] = jnp.zeros_like(ggq_ref)
            ggkv_ref[...] = jnp.zeros_like(ggkv_ref)

        cos_v, sin_v = cos_ref[...], sin_ref[...]
        dkr = jnp.zeros((tr, 128), F32)
        for h in range(HEADS):
            o = h * HEAD_PAD
            dqr_ref[:, o:o + NOPE] = dq_ref[:, o:o + NOPE].astype(MXU_DTYPE)
            dqr_ref[:, o + NOPE:o + HEAD_PAD] = _rope_mix_bwd(dq_ref[:, o + NOPE:o + HEAD_PAD], cos_v, sin_v).astype(MXU_DTYPE)
            dkv_ref[:, h * NOPE:(h + 1) * NOPE] = dk_ref[:, o:o + NOPE].astype(MXU_DTYPE)
            dkr = dkr + _rope_mix_bwd(dk_ref[:, o + NOPE:o + HEAD_PAD], cos_v, sin_v)
        dkv_ref[:, HEADS * NOPE:] = dv_ref[...]
        dcqn = lax.dot_general(dqr_ref[...], wq_ref[...], _DIMS["nt"], preferred_element_type=F32)
        dckvn = lax.dot_general(dkv_ref[...], wkv_ref[...], _DIMS["nt"], preferred_element_type=F32)

        zl = z_ref[...]

        def rms_bwd(c, dn, g_ref, gg_ref):
            r = lax.rsqrt(jnp.mean(c * c, axis=-1, keepdims=True) + EPS)
            ch = c * r
            gg_ref[...] += jnp.sum(dn * ch, axis=0, keepdims=True)
            dch = dn * g_ref[...]
            return r * (dch - ch * jnp.mean(dch * ch, axis=-1, keepdims=True))

        dz_ref[:, :Q_RANK] = rms_bwd(zl[:, :Q_RANK], dcqn, gq_ref, ggq_ref).astype(dz_ref.dtype)
        dz_ref[:, Q_RANK:Q_RANK + KV_RANK] = rms_bwd(zl[:, Q_RANK:Q_RANK + KV_RANK], dckvn, gkv_ref, ggkv_ref).astype(dz_ref.dtype)
        dz_ref[:, Q_RANK + KV_RANK:] = dkr.astype(dz_ref.dtype)

    def row(w):
        return pl.BlockSpec((tr, w), lambda i: (i, 0))

    def full(a):
        return pl.BlockSpec(a.shape, lambda i: (0, 0))

    lat = pl.BlockSpec((tr, LAT), lambda i: (i, lat_blk))
    return pl.pallas_call(
        body, name="lat_bwd",
        out_shape=(jax.ShapeDtypeStruct(dz.shape, dz.dtype), jax.ShapeDtypeStruct((T, HEADS * HEAD_PAD), MXU_DTYPE),
                   jax.ShapeDtypeStruct((T, 2 * HEADS * NOPE), MXU_DTYPE), jax.ShapeDtypeStruct(gq.shape, F32),
                   jax.ShapeDtypeStruct(gkv.shape, F32)),
        grid=(T // tr,),
        in_specs=[pl.BlockSpec(memory_space=pl.ANY), lat, row(HEADS * HEAD_PAD), row(HEADS * HEAD_PAD), row(HEADS * NOPE),
                  full(gq), full(gkv), full(wq), full(wkv), row(128), row(128)],
        out_specs=(lat, row(HEADS * HEAD_PAD), row(2 * HEADS * NOPE), full(gq), full(gkv)),
        input_output_aliases={0: 0},
        compiler_params=_params(("arbitrary",), 8 * _nbytes((tr, HEADS * HEAD_PAD), F32)),
    )(dz, z, dq, dk, dv, gq, gkv, wq, wkv, cos_a, sin_a)


GATE_ROWS = 256
HALO = 8


def _taps(ref, half, r, first):
    C = GATE_ROWS
    if first:
        xs = jnp.concatenate([jnp.zeros((HALO, ref.shape[-1]), F32), ref[half, 0:C, :]], axis=0)
    else:
        xs = ref[half, pl.ds(pl.multiple_of(r * C - HALO, HALO), C + HALO), :]
    return xs[HALO:, :], pltpu.roll(xs, 1, 0)[HALO:, :], pltpu.roll(xs, 2, 0)[HALO:, :]


def _conv_taps(taps, cw, cb):
    x0, x1, x2 = taps
    return cb + cw[0:1, :] * x2 + cw[1:2, :] * x1 + cw[2:3, :] * x0


def _fold8(x):
    acc = x[0:8, :]
    for i in range(1, x.shape[0] // 8):
        acc = acc + x[8 * i:8 * (i + 1), :]
    return acc


def _gate_fwd(up3, conv_w, conv_b, B, S):
    T = B * S
    W = FF_TILE
    C = GATE_ROWS

    def body(up_ref, cw_ref, cb_ref, act_ref, conv_ref):
        def chunk(r, first):
            gate = _conv_taps(_taps(up_ref, 0, r, first), cw_ref[0], cb_ref[0])
            val = _conv_taps(_taps(up_ref, 1, r, first), cw_ref[1], cb_ref[1])
            rows = pl.ds(0 if first else pl.multiple_of(r * C, C), C)
            conv_ref[0, rows, :] = gate.astype(conv_ref.dtype)
            conv_ref[1, rows, :] = val.astype(conv_ref.dtype)
            act_ref[rows, :] = (gate * _sigmoid(gate) * val).astype(act_ref.dtype)

        chunk(0, True)

        @pl.loop(1, S // C)
        def _(r):
            chunk(r, False)

    up_spec = pl.BlockSpec((2, S, W), lambda b, j: (0, b, j))
    return pl.pallas_call(
        body, name="gate_fwd",
        out_shape=(jax.ShapeDtypeStruct((T, D_FF), MXU_DTYPE), jax.ShapeDtypeStruct((2, T, D_FF), MXU_DTYPE)),
        grid=(B, N_FF_TILES),
        in_specs=[up_spec, pl.BlockSpec((2, 3, W), lambda b, j: (0, 0, j)), pl.BlockSpec((2, 1, W), lambda b, j: (0, 0, j))],
        out_specs=(pl.BlockSpec((S, W), lambda b, j: (b, j)), up_spec),
        compiler_params=_params(("parallel", "parallel"), 8 * _nbytes((S, W), F32)),
    )(up3, conv_w, conv_b)


def _gate_bwd(up3, conv3, dact, conv_w, B, S):
    T = B * S
    W = FF_TILE
    C = GATE_ROWS

    def body(up_ref, conv_ref, da_ref, cw_ref, dup_ref, gcw_ref, gcb_ref, d_s):
        @pl.when(pl.program_id(1) == 0)
        def _():
            gcw_ref[...] = jnp.zeros_like(gcw_ref)
            gcb_ref[...] = jnp.zeros_like(gcb_ref)

        @pl.loop(0, S // C)
        def _(r):
            rows = pl.ds(pl.multiple_of(r * C, C), C)
            gate, val = conv_ref[0, rows, :].astype(F32), conv_ref[1, rows, :].astype(F32)
            sg = _sigmoid(gate)
            da = da_ref[rows, :]
            d_s[0, rows, :] = da * val * (sg * (1.0 + gate * (1.0 - sg)))
            d_s[1, rows, :] = da * (gate * sg)

        d_s[:, S:S + HALO, :] = jnp.zeros((2, HALO, W), F32)

        def chunk(r, sums):
            base = pl.multiple_of(r * C, C)
            out = []
            for half in (0, 1):
                ds_ = d_s[half, pl.ds(base, C + HALO), :]
                d0, d1, d2 = ds_[:C, :], pltpu.roll(ds_, C + HALO - 1, 0)[:C, :], pltpu.roll(ds_, C + HALO - 2, 0)[:C, :]
                cw = cw_ref[half]
                dup_ref[half, pl.ds(base, C), :] = (cw[2:3, :] * d0 + cw[1:2, :] * d1 + cw[0:1, :] * d2).astype(dup_ref.dtype)
                x = up_ref[half, pl.ds(base, C), :]
                sb, s0, s1, s2 = sums[half]
                out.append((sb + _fold8(d0), s0 + _fold8(d2 * x), s1 + _fold8(d1 * x), s2 + _fold8(d0 * x)))
            return tuple(out)

        zeros = tuple(tuple(jnp.zeros((8, W), F32) for _ in range(4)) for _ in range(2))
        sums = lax.fori_loop(0, S // C, chunk, zeros)
        for half in (0, 1):
            sb, s0, s1, s2 = sums[half]
            gcb_ref[half] += jnp.sum(sb, axis=0, keepdims=True)
            gcw_ref[half, 0:1, :] += jnp.sum(s0, axis=0, keepdims=True)
            gcw_ref[half, 1:2, :] += jnp.sum(s1, axis=0, keepdims=True)
            gcw_ref[half, 2:3, :] += jnp.sum(s2, axis=0, keepdims=True)

    up_spec = pl.BlockSpec((2, S, W), lambda j, b: (0, b, j))
    cw_spec = pl.BlockSpec((2, 3, W), lambda j, b: (0, 0, j))
    cb_spec = pl.BlockSpec((2, 1, W), lambda j, b: (0, 0, j))
    return pl.pallas_call(
        body, name="gate_bwd",
        out_shape=(jax.ShapeDtypeStruct((2, T, D_FF), MXU_DTYPE), jax.ShapeDtypeStruct((2, 3, D_FF), F32),
                   jax.ShapeDtypeStruct((2, 1, D_FF), F32)),
        grid=(N_FF_TILES, B),
        in_specs=[up_spec, up_spec, pl.BlockSpec((S, W), lambda j, b: (b, j)), cw_spec],
        out_specs=(up_spec, cw_spec, cb_spec),
        scratch_shapes=[pltpu.VMEM((2, S + HALO, W), F32)],
        compiler_params=_params(("parallel", "arbitrary"), 12 * _nbytes((S, W), F32)),
    )(up3, conv3, dact, conv_w)


def _final(x2, tgt, g, tr=512):
    T, D = x2.shape

    def body(x_ref, t_ref, g_ref, dx_ref, loss_ref, gg_ref):
        @pl.when(pl.program_id(0) == 0)
        def _():
            loss_ref[...] = jnp.zeros_like(loss_ref)
            gg_ref[...] = jnp.zeros_like(gg_ref)

        xv = x_ref[...]
        gv = g_ref[...]
        r = lax.rsqrt(jnp.mean(xv * xv, axis=-1, keepdims=True) + EPS)
        xn = xv * r
        err = xn * gv - t_ref[...]
        loss_ref[...] += 0.5 * jnp.sum(jnp.mean(err * err, axis=-1, keepdims=True), axis=0, keepdims=True)
        dy = err * (1.0 / D)
        gg_ref[...] += jnp.sum(dy * xn, axis=0, keepdims=True)
        dxn = dy * gv
        dx_ref[...] = r * (dxn - xn * jnp.mean(dxn * xn, axis=-1, keepdims=True))

    row = pl.BlockSpec((tr, D), lambda i: (i, 0))
    vec = pl.BlockSpec((1, D), lambda i: (0, 0))
    return pl.pallas_call(
        body, name="final_loss",
        out_shape=(jax.ShapeDtypeStruct((T, D), F32), jax.ShapeDtypeStruct((1, 128), F32), jax.ShapeDtypeStruct((1, D), F32)),
        grid=(T // tr,), in_specs=[row, row, vec],
        out_specs=(row, pl.BlockSpec((1, 128), lambda i: (0, 0)), vec),
        compiler_params=_params(("arbitrary",), 6 * _nbytes((tr, D), F32)),
    )(x2, tgt, g)


def _down_proj_final(act, w, x1, tgt, g, tm, tk):
    (T, K), D = act.shape, w.shape[1]
    nk = K // tk
    assert T % tm == 0 and K % tk == 0

    def body(a_ref, b_ref, add_ref, t_ref, g_ref, dx_ref, loss_ref, gg_ref, acc):
        i, k = pl.program_id(0), pl.program_id(1)

        @pl.when((i == 0) & (k == 0))
        def _():
            loss_ref[...] = jnp.zeros_like(loss_ref)
            gg_ref[...] = jnp.zeros_like(gg_ref)

        @pl.when(k == 0)
        def _():
            acc[...] = jnp.zeros_like(acc)

        acc[...] += lax.dot_general(a_ref[...].astype(MXU_DTYPE), b_ref[...].astype(MXU_DTYPE), _DIMS["nn"],
                                    preferred_element_type=F32)

        @pl.when(k == nk - 1)
        def _():
            xv = acc[...] + add_ref[...]
            gv = g_ref[...]
            r = lax.rsqrt(jnp.mean(xv * xv, axis=-1, keepdims=True) + EPS)
            xn = xv * r
            err = xn * gv - t_ref[...]
            loss_ref[...] += 0.5 * jnp.sum(jnp.mean(err * err, axis=-1, keepdims=True), axis=0, keepdims=True)
            dy = err * (1.0 / D)
            gg_ref[...] += jnp.sum(dy * xn, axis=0, keepdims=True)
            dxn = dy * gv
            dx_ref[...] = r * (dxn - xn * jnp.mean(dxn * xn, axis=-1, keepdims=True))

    row = pl.BlockSpec((tm, D), lambda i, k: (i, 0))
    vec = pl.BlockSpec((1, D), lambda i, k: (0, 0))
    return pl.pallas_call(
        body, name="down_proj_final_loss",
        out_shape=(jax.ShapeDtypeStruct((T, D), F32), jax.ShapeDtypeStruct((1, 128), F32), jax.ShapeDtypeStruct((1, D), F32)),
        grid=(T // tm, nk),
        in_specs=[pl.BlockSpec((tm, tk), lambda i, k: (i, k)), pl.BlockSpec((tk, D), lambda i, k: (k, 0)), row, row, vec],
        out_specs=(row, pl.BlockSpec((1, 128), lambda i, k: (0, 0)), vec),
        scratch_shapes=[pltpu.VMEM((tm, D), F32)],
        compiler_params=_params(("arbitrary", "arbitrary"), 10 * _nbytes((tm, D), F32)),
    )(act, w, x1, tgt, g)


def _sum_slabs(parts, name, tr):
    rows, cols = parts[0].shape
    n = len(parts)

    def body(*refs):
        acc = refs[0][...]
        for r in refs[1:n]:
            acc = acc + r[...]
        refs[n][...] = acc

    blk = pl.BlockSpec((tr, cols), lambda i: (i, 0))
    return pl.pallas_call(
        body, name=name, out_shape=jax.ShapeDtypeStruct((rows, cols), F32), grid=(rows // tr,),
        in_specs=[blk] * n, out_specs=blk,
        compiler_params=_params(("parallel",), (n + 1) * _nbytes((tr, cols), F32)),
    )(*parts)


ADAMW_BLOCK_BYTES = 2400 * 1024


def _adamw(w, g, m, v, name, copy_grad=False):
    lead = w.ndim == 3
    rows, cols = w.shape[-2:]
    fits = [d for d in range(8, rows + 1, 8) if rows % d == 0 and d * cols * 4 <= ADAMW_BLOCK_BYTES]
    tr = max(fits) if fits else rows
    c1 = 1.0 - ADAM_B1 ** ADAM_STEP
    c2 = 1.0 - ADAM_B2 ** ADAM_STEP

    def body(w_ref, g_ref, m_ref, v_ref, d_ref, nm_ref, nv_ref, *g_out):
        gv = g_ref[...]
        nm = ADAM_B1 * m_ref[...] + (1.0 - ADAM_B1) * gv
        nv = ADAM_B2 * v_ref[...] + (1.0 - ADAM_B2) * (gv * gv)
        nm_ref[...] = nm
        nv_ref[...] = nv
        d_ref[...] = -ADAM_LR * ((nm / c1) / (jnp.sqrt(nv / c2) + ADAM_EPS) + ADAM_WD * w_ref[...])
        if copy_grad:
            g_out[0][...] = gv

    blk = pl.BlockSpec((None, tr, cols), lambda i: (0, i, 0)) if lead else pl.BlockSpec((tr, cols), lambda i: (i, 0))
    sds = jax.ShapeDtypeStruct(w.shape, F32)
    n_out = 4 if copy_grad else 3
    return pl.pallas_call(
        body, name=name, out_shape=(sds,) * n_out, grid=(rows // tr,), in_specs=[blk] * 4, out_specs=(blk,) * n_out,
        compiler_params=_params(("parallel",), (4 + n_out) * _nbytes((tr, cols), F32)),
    )(w, g, m, v)


_ANY = pl.BlockSpec(memory_space=pl.ANY)


def _place():
    x, y, c = lax.axis_index("x"), lax.axis_index("y"), lax.axis_index("c")
    chips = [(1 - x, y), (x, 1 - y), (1 - x, 1 - y)]
    return x, y, c, chips


def _forward_halves(lands):
    n = len(lands)

    def body(*refs):
        outs, send, recv = refs[n:2 * n], refs[2 * n], refs[2 * n + 1]
        x, y, c, chips = _place()
        cps = []
        for w in range(n):
            for j, (px, py) in enumerate(chips):
                landed = outs[w].at[2 * px + py, c]
                cps.append(pltpu.make_async_remote_copy(
                    src_ref=landed, dst_ref=landed, send_sem=send.at[3 * w + j], recv_sem=recv.at[3 * w + j],
                    device_id=(x, y, 1 - c), device_id_type=MESH))
        for cp in cps:
            cp.start()
        for w in range(n):
            for j, (px, py) in enumerate(chips):
                other = outs[w].at[2 * px + py, 1 - c]
                pltpu.make_async_remote_copy(src_ref=other, dst_ref=other, send_sem=send.at[3 * w + j],
                                             recv_sem=recv.at[3 * w + j], device_id=(x, y, 1 - c),
                                             device_id_type=MESH).wait_recv()
        for cp in cps:
            cp.wait_send()

    dma = lambda k: pltpu.SemaphoreType.DMA((k,))
    return pl.pallas_call(
        body, name="gather_forward_halves", out_shape=tuple(jax.ShapeDtypeStruct(a.shape, a.dtype) for a in lands),
        in_specs=[_ANY] * n, out_specs=tuple([_ANY] * n), input_output_aliases={w: w for w in range(n)},
        scratch_shapes=[dma(3 * n), dma(3 * n)],
    )(*lands)


_HBM = pl.BlockSpec(memory_space=pltpu.HBM)
_SEM = pl.BlockSpec(memory_space=pltpu.SEMAPHORE)
_EFFECT = pltpu.SideEffectType.DATAFLOW_SIDE_EFFECTING


SEMS_PER_ARRAY = 8


def _exchange_copies(srcs, lands, send, recv, mode):
    x, y, c, chips = _place()
    if mode == "halves":
        cps = []
        for w, (src, land) in enumerate(zip(srcs, lands)):
            pieces = [(src.at[c], land.at[2 * x + y, c], (px, py, c)) for px, py in chips]
            pieces.append((src, land.at[2 * x + y], (x, y, 1 - c)))
            for k, (piece, dst, peer) in enumerate(pieces):
                cps.append(pltpu.make_async_remote_copy(
                    src_ref=piece, dst_ref=dst, send_sem=send.at[SEMS_PER_ARRAY * w + k],
                    recv_sem=recv.at[SEMS_PER_ARRAY * w + k], device_id=peer, device_id_type=MESH))
        return cps
    if mode == "swap":
        return [pltpu.make_async_remote_copy(
            src_ref=src.at[:, 1 - c], dst_ref=land, send_sem=send.at[SEMS_PER_ARRAY * w],
            recv_sem=recv.at[SEMS_PER_ARRAY * w], device_id=(x, y, 1 - c), device_id_type=MESH)
            for w, (src, land) in enumerate(zip(srcs, lands))]
    if mode == "all":
        flips = [(fx, fy, fc) for fx in (0, 1) for fy in (0, 1) for fc in (0, 1)][1:]
        peers = [(x ^ fx, y ^ fy, c ^ fc) for fx, fy, fc in flips]
        slot = 4 * x + 2 * y + c
    else:
        peers = [(px, py, c) for px, py in chips] + ([(x, y, 1 - c)] if mode == "gather" else [])
        slot = 2 * x + y
    cps = []
    for w, (src, land) in enumerate(zip(srcs, lands)):
        for k, peer in enumerate(peers):
            piece = src.at[2 * peer[0] + peer[1]] if mode == "scatter" else src
            cps.append(pltpu.make_async_remote_copy(
                src_ref=piece, dst_ref=land.at[slot], send_sem=send.at[SEMS_PER_ARRAY * w + k],
                recv_sem=recv.at[SEMS_PER_ARRAY * w + k], device_id=peer, device_id_type=MESH))
    return cps


def _exchange_start(srcs, name, mode, after):
    n = len(srcs)
    if mode == "swap":
        land_shapes = [(s.shape[0],) + s.shape[2:] for s in srcs]
    else:
        lead = {"gather": (N_CHIPS,), "halves": (N_CHIPS,), "scatter": (), "all": (2 * N_CHIPS,)}[mode]
        land_shapes = [lead + s.shape for s in srcs]

    def body(*refs):
        src_refs, land_refs = refs[:n], refs[n:2 * n]
        send, recv = refs[2 * n + 1], refs[2 * n + 2]
        token = refs[-1]
        for cp in _exchange_copies(src_refs, land_refs, send, recv, mode):
            cp.start()
        token[...] = jnp.zeros_like(token)

    sems = pltpu.SemaphoreType.DMA((SEMS_PER_ARRAY * n,))
    out = pl.pallas_call(
        body, name=name,
        out_shape=(sems, sems, *[pltpu.HBM(s.shape, s.dtype) for s in srcs],
                   *[pltpu.HBM(shp, s.dtype) for shp, s in zip(land_shapes, srcs)], jax.ShapeDtypeStruct((8, 128), F32)),
        in_specs=[_HBM] * (2 * n) + [_ANY],
        out_specs=(_SEM, _SEM, *[_HBM] * (2 * n), pl.BlockSpec(memory_space=pltpu.VMEM)),
        input_output_aliases={i: 2 + i for i in range(2 * n)},
        compiler_params=pltpu.CompilerParams(has_side_effects=_EFFECT),
    )(*[pltpu.with_memory_space_constraint(s, pltpu.HBM) for s in srcs],
      *[pltpu.with_memory_space_constraint(lax.empty(shp, s.dtype), pltpu.HBM) for shp, s in zip(land_shapes, srcs)],
      after)
    return out[0], out[1], out[2:2 + n], out[2 + n:2 + 2 * n], out[-1]


def _exchange_wait(started, name, mode, after):
    send, recv, src_thru, land_thru, _ = started
    n = len(src_thru)
    after = list(after) if isinstance(after, (list, tuple)) else [after]

    def body(*refs):
        src_refs, land_refs, send_ref, recv_ref = refs[:n], refs[n:2 * n], refs[2 * n], refs[2 * n + 1]
        for cp in _exchange_copies(src_refs, land_refs, send_ref, recv_ref, mode):
            cp.wait_send()
            cp.wait_recv()

    out = pl.pallas_call(
        body, name=name,
        out_shape=tuple(pltpu.HBM(a.shape, a.dtype) for a in list(src_thru) + list(land_thru)),
        in_specs=[_HBM] * (2 * n) + [_SEM, _SEM] + [_ANY] * len(after), out_specs=tuple([_HBM] * (2 * n)),
        input_output_aliases={i: i for i in range(2 * n)},
        compiler_params=pltpu.CompilerParams(has_side_effects=_EFFECT),
    )(*src_thru, *land_thru, send, recv, *after)
    return out[:n], out[n:]


def _swap_halves(gs, name):
    n = len(gs)

    def body(*refs):
        ins, outs, send, recv = refs[:n], refs[n:2 * n], refs[2 * n], refs[2 * n + 1]
        x, y, c, _ = _place()
        cps = []
        for w in range(n):
            cps.append(pltpu.make_async_remote_copy(
                src_ref=ins[w].at[:, 1 - c], dst_ref=outs[w], send_sem=send.at[w], recv_sem=recv.at[w],
                device_id=(x, y, 1 - c), device_id_type=MESH))
        for cp in cps:
            cp.start()
        for cp in cps:
            cp.wait()

    return pl.pallas_call(
        body, name=name,
        out_shape=tuple(jax.ShapeDtypeStruct((g.shape[0],) + g.shape[2:], g.dtype) for g in gs),
        in_specs=[_ANY] * n, out_specs=tuple([_ANY] * n),
        scratch_shapes=[pltpu.SemaphoreType.DMA((n,)), pltpu.SemaphoreType.DMA((n,))],
    )(*gs)


GRAD_PAYLOAD = jnp.bfloat16


def _half_blocks(half_rows, cols):
    if (half_rows // 2) % 16 == 0:
        return (half_rows // 2, cols), (lambda r: (r, 0))
    assert cols % 256 == 0, (half_rows, cols)
    return (half_rows, cols // 2), (lambda r: (0, r))


def _pair_sum(gs, gots, name):
    n = len(gs)
    core = lax.axis_index("c").astype(jnp.int32).reshape(1)

    def body(core_ref, *refs):
        del core_ref
        for w in range(n):
            refs[2 * n + w][...] = (refs[w][...] + refs[n + w][...]).astype(GRAD_PAYLOAD)

    in_specs, out_specs, out_shape, nbytes = [], [], [], 0
    cuts = [_half_blocks(g.shape[1] // 2, g.shape[2]) for g in gs]
    for g, ((br, bc), at) in zip(gs, cuts):
        per_half = (g.shape[1] // 2) // br
        in_specs.append(pl.BlockSpec((1, br, bc), lambda s, r, core, at=at, per_half=per_half:
                                     (s, per_half * core[0] + at(r)[0], at(r)[1])))
        nbytes += 3 * _nbytes((br, bc), F32)
    for g, ((br, bc), at) in zip(gs, cuts):
        in_specs.append(pl.BlockSpec((1, br, bc), lambda s, r, core, at=at: (s,) + at(r)))
        out_specs.append(pl.BlockSpec((1, br, bc), lambda s, r, core, at=at: (s,) + at(r)))
        out_shape.append(jax.ShapeDtypeStruct((g.shape[0], g.shape[1] // 2, g.shape[2]), GRAD_PAYLOAD))
    return pl.pallas_call(
        body, name=name, out_shape=tuple(out_shape),
        grid_spec=pltpu.PrefetchScalarGridSpec(num_scalar_prefetch=1, grid=(N_CHIPS, 2), in_specs=in_specs,
                                               out_specs=tuple(out_specs)),
        compiler_params=_params(("parallel", "parallel"), nbytes),
    )(core, *gs, *gots)


def _chip_sum(ps, landed):
    n = len(ps)
    x, y, c = lax.axis_index("x"), lax.axis_index("y"), lax.axis_index("c")
    where = jnp.stack([2 * x + y, 2 * (1 - x) + y, 2 * x + (1 - y), 2 * (1 - x) + (1 - y), c]).astype(jnp.int32)

    def body(where_ref, *refs):
        del where_ref
        for w in range(n):
            terms = [refs[4 * w + t][...].astype(F32) for t in range(4)]
            refs[4 * n + w][...] = ((terms[0] + terms[1]) + terms[2]) + terms[3]

    in_specs, out_specs, out_shape, args, nbytes = [], [], [], [], 0
    for p, a in zip(ps, landed):
        (br, bc), at = _half_blocks(a.shape[1], a.shape[2])
        blk = (1, br, bc)
        in_specs.append(pl.BlockSpec(blk, lambda r, where, at=at: (where[0],) + at(r)))
        args.append(p)
        for t in (1, 2, 3):
            in_specs.append(pl.BlockSpec(blk, lambda r, where, t=t, at=at: (where[t],) + at(r)))
            args.append(a)
        out_specs.append(pl.BlockSpec(blk, lambda r, where, at=at: (where[4],) + at(r)))
        out_shape.append(jax.ShapeDtypeStruct((2,) + a.shape[1:], F32))
        nbytes += 4 * _nbytes(blk, F32)
    return pl.pallas_call(
        body, name="grad_chip_sum", out_shape=tuple(out_shape),
        grid_spec=pltpu.PrefetchScalarGridSpec(num_scalar_prefetch=1, grid=(2,), in_specs=in_specs,
                                               out_specs=tuple(out_specs)),
        compiler_params=_params(("parallel",), nbytes),
    )(where, *args)


def _join_halves(ss):
    n = len(ss)

    def body(*refs):
        outs, send, recv = refs[n:2 * n], refs[2 * n], refs[2 * n + 1]
        x, y, c, _ = _place()
        cps = []
        for w in range(n):
            cps.append(pltpu.make_async_remote_copy(
                src_ref=outs[w].at[c], dst_ref=outs[w].at[c], send_sem=send.at[w], recv_sem=recv.at[w],
                device_id=(x, y, 1 - c), device_id_type=MESH))
        for cp in cps:
            cp.start()
        for w in range(n):
            got = outs[w].at[1 - c]
            pltpu.make_async_remote_copy(src_ref=got, dst_ref=got, send_sem=send.at[w], recv_sem=recv.at[w],
                                         device_id=(x, y, 1 - c), device_id_type=MESH).wait_recv()
        for cp in cps:
            cp.wait_send()

    dma = lambda k: pltpu.SemaphoreType.DMA((k,))
    return pl.pallas_call(
        body, name="grad_join_halves",
        out_shape=tuple(jax.ShapeDtypeStruct(s.shape, s.dtype) for s in ss),
        in_specs=[_ANY] * n, out_specs=tuple([_ANY] * n), input_output_aliases={w: w for w in range(n)},
        scratch_shapes=[dma(n), dma(n)],
    )(*ss)


def _rot_cols(w, axis=-1):
    a, b = jnp.split(w, 2, axis=axis)
    return jnp.concatenate([-b, a], axis=axis)


def _rot_cols_t(g, axis=-1):
    a, b = jnp.split(g, 2, axis=axis)
    return jnp.concatenate([b, -a], axis=axis)


def _cols_from_chips(a):
    n, r, cs = a.shape
    return jnp.transpose(a, (1, 0, 2)).reshape(r, n * cs)


def _cols_to_chips(a):
    r, cc = a.shape
    return jnp.transpose(a.reshape(r, N_CHIPS, cc // N_CHIPS), (1, 0, 2))


def _conv_w_split(cw):
    return jnp.swapaxes(cw.reshape(3, 2, D_FF), 0, 1)


def _conv_w_join(g):
    return jnp.swapaxes(g, 0, 1).reshape(3, 2 * D_FF)


_SEG =(D_MODEL, 2 * D_MODEL, 2 * D_MODEL + Q_RANK, 2 * D_MODEL + Q_RANK + KV_RANK, 2 * D_MODEL + Q_RANK + KV_RANK + ROPE,
        3 * D_MODEL + Q_RANK + KV_RANK + ROPE)


def _w_in_t_to_pad(wt):
    u, v, cq, ckv, kr, ga, gb = jnp.split(wt, _SEG, axis=0)
    return jnp.concatenate([u, v, ga, gb, cq, ckv, kr, _rot_cols(kr, axis=0)], axis=0)


def _w_in_t_from_pad(gt):
    u, v, ga, gb, cq, ckv, kr, krr = jnp.split(
        gt, (D_MODEL, 2 * D_MODEL, 3 * D_MODEL, 4 * D_MODEL, 4 * D_MODEL + Q_RANK, 4 * D_MODEL + Q_RANK + KV_RANK,
             4 * D_MODEL + Q_RANK + KV_RANK + ROPE), axis=0)
    return jnp.concatenate([u, v, cq, ckv, kr + _rot_cols_t(krr, axis=0), ga, gb], axis=0)


def _w_uq_to_pad(w):
    t = w.reshape(Q_RANK, HEADS, QK_DIM)
    nope, rope = t[..., :NOPE], t[..., NOPE:]
    return jnp.concatenate([nope, rope, _rot_cols(rope)], axis=-1).reshape(Q_RANK, HEADS * HEAD_PAD)


def _w_uq_from_pad(g):
    t = g.reshape(Q_RANK, HEADS, HEAD_PAD)
    nope, rope, rot = t[..., :NOPE], t[..., NOPE:QK_DIM], t[..., QK_DIM:]
    return jnp.concatenate([nope, rope + _rot_cols_t(rot)], axis=-1).reshape(Q_RANK, HEADS * QK_DIM)


def _w_ukv_to_pad(w):
    t = w.reshape(KV_RANK, HEADS, 2, NOPE)
    return jnp.swapaxes(t, 1, 2).reshape(KV_RANK, 2 * HEADS * NOPE)


def _w_ukv_from_pad(g):
    t = g.reshape(KV_RANK, 2, HEADS, NOPE)
    return jnp.swapaxes(t, 1, 2).reshape(KV_RANK, 2 * HEADS * NOPE)


def _rope_tables(positions):
    inv_freq = 1.0 / (ROPE_THETA ** (jnp.arange(0, ROPE, 2, dtype=F32) / ROPE))
    ang = positions.astype(F32).reshape(-1, 1) * inv_freq
    cos, sin = jnp.cos(ang), jnp.sin(ang)
    zero = jnp.zeros((ang.shape[0], 64), F32)
    return jnp.concatenate([cos, cos, zero], axis=1), jnp.concatenate([sin, sin, zero], axis=1)


_BIG = ("w_in", "w_uq", "w_ukv", "w_out", "w_up", "w_down")
UP_SHARD = 2 * D_FF // N_CHIPS
TOKEN_TILE = 1024


def _local_step(x, positions, tgt, wts, in_weights, mixer_weights, ffn_weights, on_ffn_grads, on_mixer_grads):
    B, S, D = x.shape
    T = B * S
    xf = x.reshape(T, D)
    cos_a, sin_a = _rope_tables(positions)
    bs_t = jnp.pad(wts["a_spatial_b"].T, ((0, 0), (0, 128 - A_GROUPS)))

    h = _rms_fwd(xf, wts["mix_norm"], "norm1_fwd")
    wts = dict(wts)
    wts["w_in"], token = in_weights([h, cos_a, sin_a])
    tm = min(TOKEN_TILE, T)
    z = _mm(h, wts["w_in"], "nt", "in_proj", tm=tm, tn=1536, tk=D, n_outer=True, after=token)
    wts["w_q"], wts["w_kv"], wts["w_out"] = mixer_weights(z)
    q, k, v, cqn, ckvn = _lat_fwd(z, wts["q_a_norm"], wts["kv_a_norm"], wts["w_q"], wts["w_kv"], cos_a, sin_a)
    yb, *lses = _attn_fwd(q, k, v, B, S)
    merged = _mix_fwd(z, yb, wts["a_v_norm_g"], wts["a_v_norm_b"], wts["a_spatial_w"], bs_t)
    x1, h2 = _mm(merged, wts["w_out"], "nn", "out_proj", tm=min(512, T), tn=D, tk=D, add=xf, copy_dtype=MXU_DTYPE,
                 norm_gain=wts["ffn_norm"])
    wts["w_up"], wts["w_down"], wts["conv_w"] = ffn_weights(h2)
    up_pre = _mm(h2, wts["w_up"], "nn", "up_proj", tm=tm, tn=UP_SHARD, tk=D, dims=(T, 2 * D_FF, D),
                 b_spec=pl.BlockSpec((None, D, UP_SHARD), lambda i, j, k: (j, 0, 0)),
                 o_spec=pl.BlockSpec((None, tm, UP_SHARD), lambda i, j, k: (j // 2, i, j % 2)), out_shape=(2, T, D_FF),
                 n_outer=True)
    act, up_conv = _gate_fwd(up_pre, wts["conv_w"], wts["conv_b"], B, S)
    dx2, loss_row, g_final = _down_proj_final(act, wts["w_down"], x1, tgt.reshape(T, D), wts["final_norm"],
                                              tm=min(512, T), tk=1408)

    g = {"final_norm": g_final}
    dact = _mm(dx2, wts["w_down"], "nt", "down_proj_dx", tm=tm, tn=1408, tk=D, n_outer=True)
    tk2, tk1 = min(2048, T), min(1024, T)
    g["w_down"], g["w_down_lo"] = _mm(act, dx2, "tn", "down_proj_dw", tm=1408, tn=D, tk=tk1, copy_dtype=GRAD_PAYLOAD)
    dup, g["conv_w"], g["conv_b"] = _gate_bwd(up_pre, up_conv, dact, wts["conv_w"], B, S)
    g["w_up"], g["w_up_lo"] = _mm(
        h2, dup, "tn", "up_proj_dw", tm=D, tn=UP_SHARD, tk=tk2, dims=(D, 2 * D_FF, T), copy_dtype=GRAD_PAYLOAD,
        b_spec=pl.BlockSpec((None, tk2, UP_SHARD), lambda i, j, k: (j // 2, k, j % 2)),
        o_spec=pl.BlockSpec((None, D, UP_SHARD), lambda i, j, k: (j, 0, 0)), out_shape=(N_CHIPS, D, UP_SHARD))
    token, ffn_sent = on_ffn_grads(g)
    dh2 = _mm(dup, wts["w_up"], "nt", "up_proj_dx", tm=tm, tn=D, tk=UP_SHARD, dims=(T, D, 2 * D_FF), after=token,
              a_spec=pl.BlockSpec((None, tm, UP_SHARD), lambda i, j, k: (k // 2, i, k % 2)),
              b_spec=pl.BlockSpec((None, D, UP_SHARD), lambda i, j, k: (k, 0, 0)))
    token = ffn_sent(dh2)
    dx1, g["ffn_norm"] = _rms_bwd(x1, wts["ffn_norm"], dh2, dx2, "norm2_bwd")
    dm = _mm(dx1, wts["w_out"], "nt", "out_proj_dx", tm=min(512, T), tn=D, tk=D, after=token)
    g["w_out"], g["w_out_lo"] = _mm(merged, dx1, "tn", "out_proj_dw", tm=D, tn=D, tk=tk1, copy_dtype=GRAD_PAYLOAD)
    dz, dyb, dl, g["a_spatial_w"], gbs, g["a_v_norm_g"], g["a_v_norm_b"] = _mix_bwd(
        z, yb, dm, wts["a_v_norm_g"], wts["a_v_norm_b"], wts["a_spatial_w"], bs_t)
    g["a_spatial_b"] = gbs[:, :A_GROUPS].T
    delta = dl.reshape(HEADS * T // ATT_BLOCK, 1, ATT_BLOCK)
    dq, dk, dv = _attn_bwd(q, k, v, dyb, lses, delta, B, S)
    dz, dq_raw, dkv, g["q_a_norm"], g["kv_a_norm"] = _lat_bwd(
        dz, z, dq, dk, dv, wts["q_a_norm"], wts["kv_a_norm"], wts["w_q"], wts["w_kv"], cos_a, sin_a)
    g["w_q"] = _mm(cqn, dq_raw, "tn", "q_proj_dw", tm=Q_RANK, tn=HEADS * HEAD_PAD, tk=tk2)
    g["w_kv"] = _mm(ckvn, dkv, "tn", "kv_proj_dw", tm=KV_RANK, tn=2 * HEADS * NOPE, tk=tk2)
    g["w_in"] = _mm(dz, h, "tn", "in_proj_dw", tm=1536, tn=D, tk=tk2)
    token = on_mixer_grads(g)
    dh = _mm(dz, wts["w_in"], "nn", "in_proj_dx", tm=tm, tn=D, tk=1536, after=token)
    dx, g["mix_norm"] = _rms_bwd(xf, wts["mix_norm"], dh, dx1, "norm1_bwd")
    return loss_row[0, 0], dx.reshape(B, S, D), g


_SMALL = (("mix_norm", (1, D_MODEL)), ("a_v_norm_g", (1, D_MODEL)), ("a_v_norm_b", (1, D_MODEL)),
          ("a_spatial_w", (A_GROUPS * CHUNK, CHUNK)), ("a_spatial_b", (1, A_GROUPS * CHUNK)), ("q_a_norm", (1, Q_RANK)),
          ("kv_a_norm", (1, KV_RANK)), ("ffn_norm", (1, D_MODEL)), ("conv_b", (1, 2 * D_FF)), ("final_norm", (1, D_MODEL)),
          ("conv_w", (3, 2 * D_FF)))
_SMALL_SIZE = sum(math.prod(s) for _, s in _SMALL)
_SMALL_ROWS = -(-(_SMALL_SIZE + 1) // (128 * 8)) * 8


def kernel(x, positions, mix_norm, w_in, a_v_norm_g, a_v_norm_b, a_spatial_w, a_spatial_b, q_a_norm, w_uq, kv_a_norm, w_ukv, w_out, ffn_norm, w_up, conv_w, conv_b, w_down, final_norm, loss_target, m_mix_norm, m_w_in, m_a_v_norm_g, m_a_v_norm_b, m_a_spatial_w, m_a_spatial_b, m_q_a_norm, m_w_uq, m_kv_a_norm, m_w_ukv, m_w_out, m_ffn_norm, m_w_up, m_conv_w, m_conv_b, m_w_down, m_final_norm, v_mix_norm, v_w_in, v_a_v_norm_g, v_a_v_norm_b, v_a_spatial_w, v_a_spatial_b, v_q_a_norm, v_w_uq, v_kv_a_norm, v_w_ukv, v_w_out, v_ffn_norm, v_w_up, v_conv_w, v_conv_b, v_w_down, v_final_norm):
    weights = dict(mix_norm=mix_norm, w_in=w_in, a_v_norm_g=a_v_norm_g, a_v_norm_b=a_v_norm_b, a_spatial_w=a_spatial_w,
                   a_spatial_b=a_spatial_b, q_a_norm=q_a_norm, w_uq=w_uq, kv_a_norm=kv_a_norm, w_ukv=w_ukv, w_out=w_out,
                   ffn_norm=ffn_norm, w_up=w_up, conv_w=conv_w, conv_b=conv_b, w_down=w_down, final_norm=final_norm)
    m_in = dict(mix_norm=m_mix_norm, w_in=m_w_in, a_v_norm_g=m_a_v_norm_g, a_v_norm_b=m_a_v_norm_b,
                a_spatial_w=m_a_spatial_w, a_spatial_b=m_a_spatial_b, q_a_norm=m_q_a_norm, w_uq=m_w_uq,
                kv_a_norm=m_kv_a_norm, w_ukv=m_w_ukv, w_out=m_w_out, ffn_norm=m_ffn_norm, w_up=m_w_up, conv_w=m_conv_w,
                conv_b=m_conv_b, w_down=m_w_down, final_norm=m_final_norm)
    v_in = dict(mix_norm=v_mix_norm, w_in=v_w_in, a_v_norm_g=v_a_v_norm_g, a_v_norm_b=v_a_v_norm_b,
                a_spatial_w=v_a_spatial_w, a_spatial_b=v_a_spatial_b, q_a_norm=v_q_a_norm, w_uq=v_w_uq,
                kv_a_norm=v_kv_a_norm, w_ukv=v_w_ukv, w_out=v_w_out, ffn_norm=v_ffn_norm, w_up=v_w_up, conv_w=v_conv_w,
                conv_b=v_conv_b, w_down=v_w_down, final_norm=v_final_norm)
    names = list(weights)
    chip = 2 * lax.axis_index("x") + lax.axis_index("y")

    def halves(a):
        return a.reshape(a.shape[:-2] + (2, a.shape[-2] // 2, a.shape[-1]))

    w_in_t = jnp.swapaxes(w_in[0], 0, 1).astype(MXU_DTYPE)
    w_in_gather = _exchange_start([jnp.stack(jnp.split(w_in_t, 2, axis=1))], "w_in_gather_start", "halves",
                                  after=positions)
    gathers = {}
    wts = dict(
        mix_norm=mix_norm, a_v_norm_g=a_v_norm_g, a_v_norm_b=a_v_norm_b, a_spatial_w=a_spatial_w[0],
        a_spatial_b=a_spatial_b[0], q_a_norm=q_a_norm, kv_a_norm=kv_a_norm, ffn_norm=ffn_norm,
        final_norm=final_norm.reshape(1, D_MODEL), conv_b=conv_b.reshape(2, 1, D_FF))

    mixer_shards = [weights[n][0].astype(MXU_DTYPE) for n in _BIG[1:4]]
    ffn_shards = [w_up[0].astype(MXU_DTYPE), w_down[0].astype(MXU_DTYPE)]

    def in_weights(after):
        _, landed = _exchange_wait(w_in_gather, "w_in_gather_wait", "halves", list(after) + mixer_shards + ffn_shards)
        (w_in_sh,) = _forward_halves(list(landed))
        gathers["mixer"] = _exchange_start(mixer_shards, "mixer_gather_start", "gather", after=w_in_sh)
        gathers["ffn"] = _exchange_start(ffn_shards + [conv_w[0]], "ffn_gather_start", "gather",
                                         after=gathers["mixer"][4])
        w_in_pad = _w_in_t_to_pad(jnp.concatenate([w_in_sh[:, 0], w_in_sh[:, 1]], axis=-1).reshape(-1, D_MODEL))
        return w_in_pad, gathers["ffn"][4]

    def mixer_weights(after):
        _, (w_uq_sh, w_ukv_sh, w_out_sh) = _exchange_wait(gathers["mixer"], "mixer_gather_wait", "gather", after)
        return (_w_uq_to_pad(_cols_from_chips(w_uq_sh)), _w_ukv_to_pad(_cols_from_chips(w_ukv_sh)),
                w_out_sh.reshape(D_MODEL, D_MODEL))

    def ffn_weights(after):
        _, (w_up_sh, w_down_sh, cw_all) = _exchange_wait(gathers["ffn"], "ffn_gather_wait", "gather", after)
        return w_up_sh, w_down_sh.reshape(D_FF, D_MODEL), _conv_w_split(_cols_from_chips(cw_all))

    scatters = {}

    def start_scatter(slabs, slabs_lo, tag):
        got = _swap_halves([halves(s) for s in slabs_lo], tag + "_grad_swap_halves")
        sums = _pair_sum(slabs, got, tag + "_grad_pair_sum")
        scatters[tag] = _exchange_start(list(sums), tag + "_scatter_start", "scatter", after=slabs[-1])
        return scatters[tag][4]

    def on_ffn_grads(g):
        slabs, slabs_lo = [[g["w_up" + lo], g["w_down" + lo].reshape(N_CHIPS, D_FF // N_CHIPS, D_MODEL)]
                           for lo in ("", "_lo")]
        swap = _exchange_start([halves(s) for s in slabs_lo], "ffn_swap_start", "swap", after=slabs[1])

        def sent(after):
            _, got = _exchange_wait(swap, "ffn_swap_wait", "swap", after)
            sums = _pair_sum(slabs, got, "ffn_grad_pair_sum")
            scatters["ffn"] = _exchange_start(list(sums), "ffn_scatter_start", "scatter", after=got[0])
            return scatters["ffn"][4]

        return swap[4], sent

    def on_mixer_grads(g):
        slabs = [_w_in_t_from_pad(g["w_in"]).reshape(N_CHIPS, -1, D_MODEL), _cols_to_chips(_w_uq_from_pad(g["w_q"])),
                 _cols_to_chips(_w_ukv_from_pad(g["w_kv"]))]
        w_out_slabs = [g["w_out" + lo].reshape(N_CHIPS, D_MODEL // N_CHIPS, D_MODEL) for lo in ("", "_lo")]
        return start_scatter(slabs + w_out_slabs[:1], [s.astype(GRAD_PAYLOAD) for s in slabs] + w_out_slabs[1:], "mixer")

    loss_part, grad_x, g = _local_step(x, positions, loss_target, wts, in_weights, mixer_weights, ffn_weights,
                                       on_ffn_grads, on_mixer_grads)

    g_small_parts = dict(g)
    g_small_parts["conv_w"] = _conv_w_join(g["conv_w"])
    g_small_parts["conv_b"] = g["conv_b"].reshape(1, 2 * D_FF)
    flat = jnp.concatenate([g_small_parts[n].reshape(-1) for n, _ in _SMALL] + [loss_part.reshape(1)])
    flat = jnp.pad(flat, (0, _SMALL_ROWS * 128 - flat.shape[0])).reshape(_SMALL_ROWS, 128)
    small_gather = _exchange_start([flat], "small_gather_start", "all", after=grad_x)

    mixer_sums, mixer_landed = _exchange_wait(scatters["mixer"], "mixer_scatter_wait", "scatter", after=small_gather[4])
    ffn_sums, ffn_landed = _exchange_wait(scatters["ffn"], "ffn_scatter_wait", "scatter", after=mixer_landed[0])
    reduced = _chip_sum(list(mixer_sums) + list(ffn_sums), list(mixer_landed) + list(ffn_landed))
    g_big = dict(zip(_BIG, _join_halves(reduced)))

    grads, deltas, new_m, new_v = {}, {}, {}, {}

    def update(n, grad, copy_grad=False):
        w = weights[n]
        shape2 = grad.shape
        d, nm, nv, *again = _adamw(w.reshape(shape2), grad, m_in[n].reshape(shape2), v_in[n].reshape(shape2),
                                   "adamw_" + n, copy_grad)
        grads[n], deltas[n], new_m[n], new_v[n] = (t.reshape(w.shape) for t in (again[0] if copy_grad else grad, d, nm, nv))

    def update_transposed(n, grad_t):
        t = lambda a: jnp.swapaxes(a, 1, 2)
        d, nm, nv, again = _adamw(t(weights[n]), grad_t, t(m_in[n]), t(v_in[n]), "adamw_" + n, True)
        grads[n], deltas[n], new_m[n], new_v[n] = t(again), t(d), t(nm), t(nv)

    for n in _BIG:
        g3 = g_big[n].reshape((1, -1, g_big[n].shape[-1]))
        if n == "w_in":
            update_transposed(n, g3)
        else:
            update(n, g3, copy_grad=True)

    (own,), (everyone,) = _exchange_wait(small_gather, "small_gather_wait", "all", after=[deltas[n] for n in _BIG])
    device = 2 * chip + lax.axis_index("c")
    everyone = lax.dynamic_update_slice(everyone, own[None], (device, 0, 0))
    total = _sum_slabs([everyone[j] for j in range(8)], "small_grads_sum", tr=_SMALL_ROWS).reshape(-1)
    o = 0
    for n, shp in _SMALL:
        piece = total[o:o + math.prod(shp)].reshape(shp)
        o += math.prod(shp)
        if n == "conv_w":
            piece = lax.dynamic_slice_in_dim(piece, chip * UP_SHARD, UP_SHARD, axis=1)
        update(n, piece)
    loss = total[_SMALL_SIZE]
    return (loss, grad_x, *[grads[n] for n in names], *[deltas[n] for n in names], *[new_m[n] for n in names],
            *[new_v[n] for n in names])
```
